```python
import math
import jax, jax.numpy as jnp
from jax import lax
import numpy as np

D_MODEL = 1024
BATCH = 32
SEQ = 2048
DEPTH = 2

CHUNK = 64
N_META = 16
PAD = 2 * CHUNK - N_META
LEAD = PAD + N_META
Q_BLOCK = 128
N_A_LAYERS = DEPTH // 2
N_B_LAYERS = DEPTH - N_A_LAYERS
EPS = 1e-6
NEG = -1e30

DN_HEADS = D_MODEL // 128
DN_DK = 128
DN_DV = 128
DN_KW = DN_HEADS * DN_DK
DN_VW = DN_HEADS * DN_DV
CONV_K = 4

MLA_HEADS = D_MODEL // 128
QK_NOPE = 128
QK_ROPE = 64
QK_DIM = QK_NOPE + QK_ROPE
V_HEAD = 128
MLA_VW = MLA_HEADS * V_HEAD
KV_RANK = D_MODEL // 4
Q_RANK = 3 * D_MODEL // 8
ROPE_THETA = 10000.0

kernel_name = "yoco_gdn_mla_hybrid"


def rms_norm(x, g):
    xf = x.astype(jnp.float32)
    y = xf * lax.rsqrt(jnp.mean(xf * xf, -1, keepdims=True) + EPS)
    return (y * g.astype(jnp.float32)).astype(x.dtype)


def l2_norm(x):
    xf = x.astype(jnp.float32)
    return (xf * lax.rsqrt(jnp.sum(xf * xf, -1, keepdims=True) + EPS)).astype(x.dtype)


def causal_dwconv(x, w):
    c = x.shape[-1]
    return lax.conv_general_dilated(
        x, w[:, None, :].astype(x.dtype), window_strides=(1,),
        padding=[(w.shape[0] - 1, 0)], dimension_numbers=('NWC', 'WIO', 'NWC'),
        feature_group_count=c)


def rope(x, pos):
    half = x.shape[-1] // 2
    inv = ROPE_THETA ** (-jnp.arange(half, dtype=jnp.float32) / half)
    ang = pos.astype(jnp.float32)[:, None] * inv[None, :]
    cos = jnp.cos(ang)[None, :, None, :]
    sin = jnp.sin(ang)[None, :, None, :]
    x1 = x[..., :half].astype(jnp.float32)
    x2 = x[..., half:].astype(jnp.float32)
    return jnp.concatenate([x1 * cos - x2 * sin, x2 * cos + x1 * sin], -1).astype(x.dtype)


def gated_delta_rule(q, k, v, g, beta):
    out_dtype = v.dtype
    q, k, v = q.astype(jnp.float32), k.astype(jnp.float32), v.astype(jnp.float32)
    bsz, nh, _, c, dk = q.shape
    dv = v.shape[-1]
    gc = jnp.cumsum(g, -1)
    tril = jnp.tril(jnp.ones((c, c), bool))
    tril_strict = jnp.tril(jnp.ones((c, c), bool), -1)
    decay = jnp.exp(jnp.where(tril, gc[..., :, None] - gc[..., None, :], -jnp.inf))
    kb = k * beta[..., None]
    lower = jnp.where(tril_strict, jnp.einsum('bhnid,bhnjd->bhnij', kb, k) * decay, 0.0)
    system = lower + jnp.eye(c, dtype=jnp.float32)
    rhs = jnp.concatenate([v * beta[..., None], kb * jnp.exp(gc)[..., None]], -1)
    sol = lax.linalg.triangular_solve(system, rhs, left_side=True, lower=True)
    u_base, w_dec = sol[..., :dv], sol[..., dv:]
    attn_intra = jnp.where(tril, jnp.einsum('bhnid,bhnjd->bhnij', q, k) * decay, 0.0)
    q_dec = q * jnp.exp(gc)[..., None]
    k_dec = k * jnp.exp(gc[..., -1:] - gc)[..., None]
    g_last = jnp.exp(gc[..., -1])

    def step(state, xs):
        u0, wd, qd, att, kd, gl = xs
        u = u0 - jnp.einsum('bhck,bhkv->bhcv', wd, state)
        o = jnp.einsum('bhck,bhkv->bhcv', qd, state) + jnp.einsum('bhij,bhjv->bhiv', att, u)
        state = state * gl[..., None, None] + jnp.einsum('bhck,bhcv->bhkv', kd, u)
        return state, o

    xs = tuple(jnp.moveaxis(a, 2, 0) for a in (u_base, w_dec, q_dec, attn_intra, k_dec, g_last))
    s0 = jnp.zeros((bsz, nh, dk, dv), jnp.float32)
    _, o = lax.scan(step, s0, xs)
    return jnp.moveaxis(o, 0, 2).astype(out_dtype)


def deltanet_layer(x, valid, g_norm, w_in, conv_w, a_log, dt_bias, o_gain, w_out):
    bsz, lp, _ = x.shape
    n = lp // CHUNK
    h = jnp.where(valid[None, :, None], rms_norm(x, g_norm), 0)
    z = h @ w_in
    qkv = jax.nn.silu(causal_dwconv(z[..., :2 * DN_KW + DN_VW], conv_w))
    gate = z[..., 2 * DN_KW + DN_VW:2 * DN_KW + 2 * DN_VW]
    b_raw = z[..., 2 * DN_KW + 2 * DN_VW:2 * DN_KW + 2 * DN_VW + DN_HEADS].astype(jnp.float32)
    a_raw = z[..., 2 * DN_KW + 2 * DN_VW + DN_HEADS:].astype(jnp.float32)
    q = qkv[..., :DN_KW].reshape(bsz, lp, DN_HEADS, DN_DK)
    k = qkv[..., DN_KW:2 * DN_KW].reshape(bsz, lp, DN_HEADS, DN_DK)
    v = qkv[..., 2 * DN_KW:].reshape(bsz, lp, DN_HEADS, DN_DV)
    q = l2_norm(q) * (DN_DK ** -0.5)
    k = l2_norm(k)
    beta = jax.nn.sigmoid(b_raw)
    g = -jnp.exp(a_log.astype(jnp.float32)) * jax.nn.softplus(a_raw + dt_bias.astype(jnp.float32))

    def to_chunks(a):
        return a.reshape(bsz, n, CHUNK, DN_HEADS, -1).transpose(0, 3, 1, 2, 4)

    def gate_chunks(a):
        return a.reshape(bsz, n, CHUNK, DN_HEADS).transpose(0, 3, 1, 2)

    o = gated_delta_rule(to_chunks(q), to_chunks(k), to_chunks(v), gate_chunks(g), gate_chunks(beta))
    o = o.transpose(0, 2, 3, 1, 4).reshape(bsz, lp, DN_HEADS, DN_DV)
    o = rms_norm(o, o_gain) * jax.nn.silu(gate).reshape(bsz, lp, DN_HEADS, DN_DV)
    return x + o.reshape(bsz, lp, DN_VW) @ w_out


def shared_latent_kv(x, pos, g_norm, w_down, g_latent, w_uk, w_uv, k_gain):
    bsz, lp, _ = x.shape
    h = rms_norm(x, g_norm)
    c = h @ w_down
    c_kv = rms_norm(c[..., :KV_RANK], g_latent)
    k_pe = jnp.broadcast_to(c[..., None, KV_RANK:], (bsz, lp, MLA_HEADS, QK_ROPE))
    k_nope = (c_kv @ w_uk).reshape(bsz, lp, MLA_HEADS, QK_NOPE)
    v = (c_kv @ w_uv).reshape(bsz, lp, MLA_HEADS, V_HEAD)
    k = rms_norm(jnp.concatenate([k_nope, k_pe], -1), k_gain)
    k = jnp.concatenate([k[..., :QK_NOPE], rope(k[..., QK_NOPE:], pos)], -1)
    return k, v


def chunk_causal_attention(q, k, v):
    bsz, lp, nh, d = q.shape
    nb = lp // Q_BLOCK
    scale = d ** -0.5
    key_pos = jnp.arange(lp)
    key_chunk = key_pos // CHUNK
    key_ok = key_pos >= PAD
    kf = k.astype(jnp.float32)
    vf = v.astype(jnp.float32)
    qb = q.reshape(bsz, nb, Q_BLOCK, nh, d).transpose(1, 0, 2, 3, 4)

    def one_block(args):
        q_blk, i = args
        q_chunk = (i * Q_BLOCK + jnp.arange(Q_BLOCK)) // CHUNK
        mask = (key_chunk[None, :] <= q_chunk[:, None]) & key_ok[None, :]
        s = jnp.einsum('bqhd,bkhd->bhqk', q_blk.astype(jnp.float32), kf) * scale
        p = jax.nn.softmax(jnp.where(mask[None, None], s, NEG), -1)
        return jnp.einsum('bhqk,bkhd->bqhd', p, vf).astype(v.dtype)

    o = lax.map(one_block, (qb, jnp.arange(nb)))
    return o.transpose(1, 0, 2, 3, 4).reshape(bsz, lp, nh, v.shape[-1])


def mla_layer(x, pos, k, v, g_norm, w_in, g_q_latent, w_uq, q_gain, w_out):
    bsz, lp, _ = x.shape
    h = rms_norm(x, g_norm)
    z = h @ w_in
    c_q = rms_norm(z[..., :Q_RANK], g_q_latent)
    gate = z[..., Q_RANK:]
    q = rms_norm((c_q @ w_uq).reshape(bsz, lp, MLA_HEADS, QK_DIM), q_gain)
    q = jnp.concatenate([q[..., :QK_NOPE], rope(q[..., QK_NOPE:], pos)], -1)
    o = chunk_causal_attention(q, k, v).reshape(bsz, lp, MLA_VW) * jax.nn.silu(gate)
    return x + o @ w_out


def _fwd_setup_inputs(seed: int = 0) -> dict:
    key = jax.random.key(seed)
    ks = jax.random.split(key, 24)
    f32 = jnp.float32

    def nrm(k, shape, fan_in):
        return jax.random.normal(k, shape, f32) * (fan_in ** -0.5)

    def gain(k, shape):
        return 1.0 + 0.01 * jax.random.normal(k, shape, f32)

    a_in_w = 2 * DN_KW + 2 * DN_VW + 2 * DN_HEADS
    dt = jnp.exp(jax.random.uniform(ks[5], (N_A_LAYERS, DN_HEADS), f32) * (math.log(0.1) - math.log(0.001)) + math.log(0.001))
    return {
        "x": jax.random.normal(ks[0], (BATCH, SEQ, D_MODEL), f32),
        "meta_tokens": jax.random.normal(ks[1], (N_META, D_MODEL), f32),
        "a_norm": gain(ks[2], (N_A_LAYERS, D_MODEL)),
        "a_w_in": nrm(ks[3], (N_A_LAYERS, D_MODEL, a_in_w), D_MODEL),
        "a_conv": nrm(ks[4], (N_A_LAYERS, CONV_K, 2 * DN_KW + DN_VW), CONV_K),
        "a_log": jnp.log(jax.random.uniform(ks[6], (N_A_LAYERS, DN_HEADS), f32, 1.0, 16.0)),
        "a_dt_bias": dt + jnp.log(-jnp.expm1(-dt)),
        "a_o_gain": gain(ks[7], (N_A_LAYERS, DN_DV)),
        "a_w_out": nrm(ks[8], (N_A_LAYERS, DN_VW, D_MODEL), DN_VW),
        "kv_norm": gain(ks[9], (D_MODEL,)),
        "kv_w_down": nrm(ks[10], (D_MODEL, KV_RANK + QK_ROPE), D_MODEL),
        "kv_latent_norm": gain(ks[11], (KV_RANK,)),
        "kv_w_uk": nrm(ks[12], (KV_RANK, MLA_HEADS * QK_NOPE), KV_RANK),
        "kv_w_uv": nrm(ks[13], (KV_RANK, MLA_VW), KV_RANK),
        "k_gain": gain(ks[14], (QK_DIM,)),
        "b_norm": gain(ks[15], (N_B_LAYERS, D_MODEL)),
        "b_w_in": nrm(ks[16], (N_B_LAYERS, D_MODEL, Q_RANK + MLA_VW), D_MODEL),
        "b_q_latent_norm": gain(ks[17], (N_B_LAYERS, Q_RANK)),
        "b_w_uq": nrm(ks[18], (N_B_LAYERS, Q_RANK, MLA_HEADS * QK_DIM), Q_RANK),
        "b_q_gain": gain(ks[19], (N_B_LAYERS, QK_DIM)),
        "b_w_out": nrm(ks[20], (N_B_LAYERS, MLA_VW, D_MODEL), MLA_VW),
    }


def _fwd_reference(x, meta_tokens, a_norm, a_w_in, a_conv, a_log, a_dt_bias, a_o_gain, a_w_out,
              kv_norm, kv_w_down, kv_latent_norm, kv_w_uk, kv_w_uv, k_gain,
              b_norm, b_w_in, b_q_latent_norm, b_w_uq, b_q_gain, b_w_out):
    bsz = x.shape[0]
    pad = jnp.zeros((bsz, PAD, D_MODEL), x.dtype)
    meta = jnp.broadcast_to(meta_tokens.astype(x.dtype)[None], (bsz, N_META, D_MODEL))
    h = jnp.concatenate([pad, meta, x], 1)
    lp = h.shape[1]
    p = jnp.arange(lp)
    valid = p >= PAD
    pos = jnp.maximum(p - PAD, 0)
    k_sh = None
    v_sh = None
    for i in range(DEPTH):
        if i < N_A_LAYERS:
            h = deltanet_layer(h, valid, a_norm[i], a_w_in[i], a_conv[i], a_log[i],
                               a_dt_bias[i], a_o_gain[i], a_w_out[i])
            if i == N_A_LAYERS - 1:
                k_sh, v_sh = shared_latent_kv(h, pos, kv_norm, kv_w_down, kv_latent_norm,
                                              kv_w_uk, kv_w_uv, k_gain)
        else:
            j = i - N_A_LAYERS
            h = mla_layer(h, pos, k_sh, v_sh, b_norm[j], b_w_in[j], b_q_latent_norm[j],
                          b_w_uq[j], b_q_gain[j], b_w_out[j])
    return h[:, LEAD:]


import jax as _jax
import jax.numpy as _jnp

TWIN_FORMAT = 'train_step'
FWD_PARAMS = ['x', 'meta_tokens', 'a_norm', 'a_w_in', 'a_conv', 'a_log', 'a_dt_bias', 'a_o_gain', 'a_w_out', 'kv_norm', 'kv_w_down', 'kv_latent_norm', 'kv_w_uk', 'kv_w_uv', 'k_gain', 'b_norm', 'b_w_in', 'b_q_latent_norm', 'b_w_uq', 'b_q_gain', 'b_w_out']
TWIN_WEIGHTS = ['meta_tokens', 'a_norm', 'a_w_in', 'a_conv', 'a_log', 'a_dt_bias', 'a_o_gain', 'a_w_out', 'kv_norm', 'kv_w_down', 'kv_latent_norm', 'kv_w_uk', 'kv_w_uv', 'k_gain', 'b_norm', 'b_w_in', 'b_q_latent_norm', 'b_w_uq', 'b_q_gain', 'b_w_out']
TWIN_DIFF_INPUT = 'x'
TWIN_INPUTS = ['x', 'meta_tokens', 'a_norm', 'a_w_in', 'a_conv', 'a_log', 'a_dt_bias', 'a_o_gain', 'a_w_out', 'kv_norm', 'kv_w_down', 'kv_latent_norm', 'kv_w_uk', 'kv_w_uv', 'k_gain', 'b_norm', 'b_w_in', 'b_q_latent_norm', 'b_w_uq', 'b_q_gain', 'b_w_out', 'loss_target', 'm_meta_tokens', 'm_a_norm', 'm_a_w_in', 'm_a_conv', 'm_a_log', 'm_a_dt_bias', 'm_a_o_gain', 'm_a_w_out', 'm_kv_norm', 'm_kv_w_down', 'm_kv_latent_norm', 'm_kv_w_uk', 'm_kv_w_uv', 'm_k_gain', 'm_b_norm', 'm_b_w_in', 'm_b_q_latent_norm', 'm_b_w_uq', 'm_b_q_gain', 'm_b_w_out', 'v_meta_tokens', 'v_a_norm', 'v_a_w_in', 'v_a_conv', 'v_a_log', 'v_a_dt_bias', 'v_a_o_gain', 'v_a_w_out', 'v_kv_norm', 'v_kv_w_down', 'v_kv_latent_norm', 'v_kv_w_uk', 'v_kv_w_uv', 'v_k_gain', 'v_b_norm', 'v_b_w_in', 'v_b_q_latent_norm', 'v_b_w_uq', 'v_b_q_gain', 'v_b_w_out']
TWIN_OUTPUTS = ['loss', 'grad_x', 'grad_meta_tokens', 'grad_a_norm', 'grad_a_w_in', 'grad_a_conv', 'grad_a_log', 'grad_a_dt_bias', 'grad_a_o_gain', 'grad_a_w_out', 'grad_kv_norm', 'grad_kv_w_down', 'grad_kv_latent_norm', 'grad_kv_w_uk', 'grad_kv_w_uv', 'grad_k_gain', 'grad_b_norm', 'grad_b_w_in', 'grad_b_q_latent_norm', 'grad_b_w_uq', 'grad_b_q_gain', 'grad_b_w_out', 'delta_meta_tokens', 'delta_a_norm', 'delta_a_w_in', 'delta_a_conv', 'delta_a_log', 'delta_a_dt_bias', 'delta_a_o_gain', 'delta_a_w_out', 'delta_kv_norm', 'delta_kv_w_down', 'delta_kv_latent_norm', 'delta_kv_w_uk', 'delta_kv_w_uv', 'delta_k_gain', 'delta_b_norm', 'delta_b_w_in', 'delta_b_q_latent_norm', 'delta_b_w_uq', 'delta_b_q_gain', 'delta_b_w_out', 'new_m_meta_tokens', 'new_m_a_norm', 'new_m_a_w_in', 'new_m_a_conv', 'new_m_a_log', 'new_m_a_dt_bias', 'new_m_a_o_gain', 'new_m_a_w_out', 'new_m_kv_norm', 'new_m_kv_w_down', 'new_m_kv_latent_norm', 'new_m_kv_w_uk', 'new_m_kv_w_uv', 'new_m_k_gain', 'new_m_b_norm', 'new_m_b_w_in', 'new_m_b_q_latent_norm', 'new_m_b_w_uq', 'new_m_b_q_gain', 'new_m_b_w_out', 'new_v_meta_tokens', 'new_v_a_norm', 'new_v_a_w_in', 'new_v_a_conv', 'new_v_a_log', 'new_v_a_dt_bias', 'new_v_a_o_gain', 'new_v_a_w_out', 'new_v_kv_norm', 'new_v_kv_w_down', 'new_v_kv_latent_norm', 'new_v_kv_w_uk', 'new_v_kv_w_uv', 'new_v_k_gain', 'new_v_b_norm', 'new_v_b_w_in', 'new_v_b_q_latent_norm', 'new_v_b_w_uq', 'new_v_b_q_gain', 'new_v_b_w_out']
TWIN_LEAF_KINDS = {'loss': 'loss', 'grad_x': 'grad_x', 'grad_meta_tokens': 'grad_w', 'grad_a_norm': 'grad_w', 'grad_a_w_in': 'grad_w', 'grad_a_conv': 'grad_w', 'grad_a_log': 'grad_w', 'grad_a_dt_bias': 'grad_w', 'grad_a_o_gain': 'grad_w', 'grad_a_w_out': 'grad_w', 'grad_kv_norm': 'grad_w', 'grad_kv_w_down': 'grad_w', 'grad_kv_latent_norm': 'grad_w', 'grad_kv_w_uk': 'grad_w', 'grad_kv_w_uv': 'grad_w', 'grad_k_gain': 'grad_w', 'grad_b_norm': 'grad_w', 'grad_b_w_in': 'grad_w', 'grad_b_q_latent_norm': 'grad_w', 'grad_b_w_uq': 'grad_w', 'grad_b_q_gain': 'grad_w', 'grad_b_w_out': 'grad_w', 'delta_meta_tokens': 'delta_w', 'delta_a_norm': 'delta_w', 'delta_a_w_in': 'delta_w', 'delta_a_conv': 'delta_w', 'delta_a_log': 'delta_w', 'delta_a_dt_bias': 'delta_w', 'delta_a_o_gain': 'delta_w', 'delta_a_w_out': 'delta_w', 'delta_kv_norm': 'delta_w', 'delta_kv_w_down': 'delta_w', 'delta_kv_latent_norm': 'delta_w', 'delta_kv_w_uk': 'delta_w', 'delta_kv_w_uv': 'delta_w', 'delta_k_gain': 'delta_w', 'delta_b_norm': 'delta_w', 'delta_b_w_in': 'delta_w', 'delta_b_q_latent_norm': 'delta_w', 'delta_b_w_uq': 'delta_w', 'delta_b_q_gain': 'delta_w', 'delta_b_w_out': 'delta_w', 'new_m_meta_tokens': 'new_m', 'new_m_a_norm': 'new_m', 'new_m_a_w_in': 'new_m', 'new_m_a_conv': 'new_m', 'new_m_a_log': 'new_m', 'new_m_a_dt_bias': 'new_m', 'new_m_a_o_gain': 'new_m', 'new_m_a_w_out': 'new_m', 'new_m_kv_norm': 'new_m', 'new_m_kv_w_down': 'new_m', 'new_m_kv_latent_norm': 'new_m', 'new_m_kv_w_uk': 'new_m', 'new_m_kv_w_uv': 'new_m', 'new_m_k_gain': 'new_m', 'new_m_b_norm': 'new_m', 'new_m_b_w_in': 'new_m', 'new_m_b_q_latent_norm': 'new_m', 'new_m_b_w_uq': 'new_m', 'new_m_b_q_gain': 'new_m', 'new_m_b_w_out': 'new_m', 'new_v_meta_tokens': 'new_v', 'new_v_a_norm': 'new_v', 'new_v_a_w_in': 'new_v', 'new_v_a_conv': 'new_v', 'new_v_a_log': 'new_v', 'new_v_a_dt_bias': 'new_v', 'new_v_a_o_gain': 'new_v', 'new_v_a_w_out': 'new_v', 'new_v_kv_norm': 'new_v', 'new_v_kv_w_down': 'new_v', 'new_v_kv_latent_norm': 'new_v', 'new_v_kv_w_uk': 'new_v', 'new_v_kv_w_uv': 'new_v', 'new_v_k_gain': 'new_v', 'new_v_b_norm': 'new_v', 'new_v_b_w_in': 'new_v', 'new_v_b_q_latent_norm': 'new_v', 'new_v_b_w_uq': 'new_v', 'new_v_b_q_gain': 'new_v', 'new_v_b_w_out': 'new_v'}


def _forward(args):
    return _fwd_reference(*[args[k] for k in FWD_PARAMS])


def _output_shape():
    out = _jax.eval_shape(lambda: _forward(_fwd_setup_inputs(0)))
    return out.shape, out.dtype

N_MICROBATCH = 1
ADAM_LR = 0.001
ADAM_B1 = 0.9
ADAM_B2 = 0.999
ADAM_EPS = 1e-08
ADAM_WD = 0.01
ADAM_STEP = 10
PER_EXAMPLE_BATCH_AXIS = {'x': 0, 'loss_target': 0}
SHARED_INPUTS = []
_WEIGHT_DTYPES = {'meta_tokens': _jnp.float32, 'a_norm': _jnp.float32, 'a_w_in': _jnp.float32, 'a_conv': _jnp.float32, 'a_log': _jnp.float32, 'a_dt_bias': _jnp.float32, 'a_o_gain': _jnp.float32, 'a_w_out': _jnp.float32, 'kv_norm': _jnp.float32, 'kv_w_down': _jnp.float32, 'kv_latent_norm': _jnp.float32, 'kv_w_uk': _jnp.float32, 'kv_w_uv': _jnp.float32, 'k_gain': _jnp.float32, 'b_norm': _jnp.float32, 'b_w_in': _jnp.float32, 'b_q_latent_norm': _jnp.float32, 'b_w_uq': _jnp.float32, 'b_q_gain': _jnp.float32, 'b_w_out': _jnp.float32}
MOMENT_SCALE = {'meta_tokens': 2.138741e-02, 'a_norm': 2.632404e+01, 'a_w_in': 4.443736e-01, 'a_conv': 6.815934e-01, 'a_log': 1.073736e+02, 'a_dt_bias': 1.031090e+02, 'a_o_gain': 1.734719e+02, 'a_w_out': 9.989981e-01, 'kv_norm': 3.948484e-01, 'kv_w_down': 6.097509e-01, 'kv_latent_norm': 1.356849e+00, 'kv_w_uk': 2.846425e-02, 'kv_w_uv': 1.701387e-01, 'k_gain': 3.630580e-01, 'b_norm': 2.429361e-01, 'b_w_in': 4.848035e-02, 'b_q_latent_norm': 5.080416e-02, 'b_w_uq': 2.748354e-02, 'b_q_gain': 3.638157e-01, 'b_w_out': 9.394470e-02}


def _to_microbatches(a, axis):
    t = _jnp.moveaxis(a, axis, 0)
    t = t.reshape((N_MICROBATCH, t.shape[0] // N_MICROBATCH) + t.shape[1:])
    return _jnp.moveaxis(t, 1, axis + 1)


def setup_inputs(seed: int = 0) -> dict:
    inp = _fwd_setup_inputs(seed)
    key = _jax.random.fold_in(_jax.random.key(seed), 7919)
    shape, _ = _output_shape()
    out = dict(inp)
    out["loss_target"] = _jax.random.normal(_jax.random.fold_in(key, 0), shape, _jnp.float32)
    for i, name in enumerate(TWIN_WEIGHTS):
        w = inp[name].astype(_jnp.float32)
        if MOMENT_SCALE is None:
            s = _jnp.sqrt(_jnp.mean(_jnp.square(w)) + 1e-30)
        else:
            s = MOMENT_SCALE[name]
        km, kv = _jax.random.split(_jax.random.fold_in(key, i + 1))
        out[name] = w
        out["m_" + name] = s * _jax.random.normal(km, w.shape, _jnp.float32)
        out["v_" + name] = (s * s) * _jax.random.uniform(kv, w.shape, _jnp.float32, 0.5, 1.5)
    if N_MICROBATCH > 1:
        for name, axis in PER_EXAMPLE_BATCH_AXIS.items():
            out[name] = _to_microbatches(out[name], axis)
    return {'x': out['x'], 'meta_tokens': out['meta_tokens'], 'a_norm': out['a_norm'], 'a_w_in': out['a_w_in'], 'a_conv': out['a_conv'], 'a_log': out['a_log'], 'a_dt_bias': out['a_dt_bias'], 'a_o_gain': out['a_o_gain'], 'a_w_out': out['a_w_out'], 'kv_norm': out['kv_norm'], 'kv_w_down': out['kv_w_down'], 'kv_latent_norm': out['kv_latent_norm'], 'kv_w_uk': out['kv_w_uk'], 'kv_w_uv': out['kv_w_uv'], 'k_gain': out['k_gain'], 'b_norm': out['b_norm'], 'b_w_in': out['b_w_in'], 'b_q_latent_norm': out['b_q_latent_norm'], 'b_w_uq': out['b_w_uq'], 'b_q_gain': out['b_q_gain'], 'b_w_out': out['b_w_out'], 'loss_target': out['loss_target'], 'm_meta_tokens': out['m_meta_tokens'], 'm_a_norm': out['m_a_norm'], 'm_a_w_in': out['m_a_w_in'], 'm_a_conv': out['m_a_conv'], 'm_a_log': out['m_a_log'], 'm_a_dt_bias': out['m_a_dt_bias'], 'm_a_o_gain': out['m_a_o_gain'], 'm_a_w_out': out['m_a_w_out'], 'm_kv_norm': out['m_kv_norm'], 'm_kv_w_down': out['m_kv_w_down'], 'm_kv_latent_norm': out['m_kv_latent_norm'], 'm_kv_w_uk': out['m_kv_w_uk'], 'm_kv_w_uv': out['m_kv_w_uv'], 'm_k_gain': out['m_k_gain'], 'm_b_norm': out['m_b_norm'], 'm_b_w_in': out['m_b_w_in'], 'm_b_q_latent_norm': out['m_b_q_latent_norm'], 'm_b_w_uq': out['m_b_w_uq'], 'm_b_q_gain': out['m_b_q_gain'], 'm_b_w_out': out['m_b_w_out'], 'v_meta_tokens': out['v_meta_tokens'], 'v_a_norm': out['v_a_norm'], 'v_a_w_in': out['v_a_w_in'], 'v_a_conv': out['v_a_conv'], 'v_a_log': out['v_a_log'], 'v_a_dt_bias': out['v_a_dt_bias'], 'v_a_o_gain': out['v_a_o_gain'], 'v_a_w_out': out['v_a_w_out'], 'v_kv_norm': out['v_kv_norm'], 'v_kv_w_down': out['v_kv_w_down'], 'v_kv_latent_norm': out['v_kv_latent_norm'], 'v_kv_w_uk': out['v_kv_w_uk'], 'v_kv_w_uv': out['v_kv_w_uv'], 'v_k_gain': out['v_k_gain'], 'v_b_norm': out['v_b_norm'], 'v_b_w_in': out['v_b_w_in'], 'v_b_q_latent_norm': out['v_b_q_latent_norm'], 'v_b_w_uq': out['v_b_w_uq'], 'v_b_q_gain': out['v_b_q_gain'], 'v_b_w_out': out['v_b_w_out']}


def _loss(weights, diff, rest, loss_target):
    with _jax.named_scope("forward"):
        args = {**rest, TWIN_DIFF_INPUT: diff, **{k: w.astype(_WEIGHT_DTYPES[k]) for k, w in weights.items()}}
        y = _forward(args)
    with _jax.named_scope("loss_head"):
        err = _jnp.square(y.astype(_jnp.float32) - loss_target)
        return 0.5 * _jnp.sum(_jnp.mean(err, axis=-1)) if err.ndim else 0.5 * err


def _adamw(w, g, m, v):
    m = ADAM_B1 * m + (1.0 - ADAM_B1) * g
    v = ADAM_B2 * v + (1.0 - ADAM_B2) * _jnp.square(g)
    m_hat = m / (1.0 - ADAM_B1 ** ADAM_STEP)
    v_hat = v / (1.0 - ADAM_B2 ** ADAM_STEP)
    delta = -ADAM_LR * (m_hat / (_jnp.sqrt(v_hat) + ADAM_EPS) + ADAM_WD * w)
    return delta, m, v


def reference(x, meta_tokens, a_norm, a_w_in, a_conv, a_log, a_dt_bias, a_o_gain, a_w_out, kv_norm, kv_w_down, kv_latent_norm, kv_w_uk, kv_w_uv, k_gain, b_norm, b_w_in, b_q_latent_norm, b_w_uq, b_q_gain, b_w_out, loss_target, m_meta_tokens, m_a_norm, m_a_w_in, m_a_conv, m_a_log, m_a_dt_bias, m_a_o_gain, m_a_w_out, m_kv_norm, m_kv_w_down, m_kv_latent_norm, m_kv_w_uk, m_kv_w_uv, m_k_gain, m_b_norm, m_b_w_in, m_b_q_latent_norm, m_b_w_uq, m_b_q_gain, m_b_w_out, v_meta_tokens, v_a_norm, v_a_w_in, v_a_conv, v_a_log, v_a_dt_bias, v_a_o_gain, v_a_w_out, v_kv_norm, v_kv_w_down, v_kv_latent_norm, v_kv_w_uk, v_kv_w_uv, v_k_gain, v_b_norm, v_b_w_in, v_b_q_latent_norm, v_b_w_uq, v_b_q_gain, v_b_w_out):
    given = dict(x=x, meta_tokens=meta_tokens, a_norm=a_norm, a_w_in=a_w_in, a_conv=a_conv, a_log=a_log, a_dt_bias=a_dt_bias, a_o_gain=a_o_gain, a_w_out=a_w_out, kv_norm=kv_norm, kv_w_down=kv_w_down, kv_latent_norm=kv_latent_norm, kv_w_uk=kv_w_uk, kv_w_uv=kv_w_uv, k_gain=k_gain, b_norm=b_norm, b_w_in=b_w_in, b_q_latent_norm=b_q_latent_norm, b_w_uq=b_w_uq, b_q_gain=b_q_gain, b_w_out=b_w_out, loss_target=loss_target, m_meta_tokens=m_meta_tokens, m_a_norm=m_a_norm, m_a_w_in=m_a_w_in, m_a_conv=m_a_conv, m_a_log=m_a_log, m_a_dt_bias=m_a_dt_bias, m_a_o_gain=m_a_o_gain, m_a_w_out=m_a_w_out, m_kv_norm=m_kv_norm, m_kv_w_down=m_kv_w_down, m_kv_latent_norm=m_kv_latent_norm, m_kv_w_uk=m_kv_w_uk, m_kv_w_uv=m_kv_w_uv, m_k_gain=m_k_gain, m_b_norm=m_b_norm, m_b_w_in=m_b_w_in, m_b_q_latent_norm=m_b_q_latent_norm, m_b_w_uq=m_b_w_uq, m_b_q_gain=m_b_q_gain, m_b_w_out=m_b_w_out, v_meta_tokens=v_meta_tokens, v_a_norm=v_a_norm, v_a_w_in=v_a_w_in, v_a_conv=v_a_conv, v_a_log=v_a_log, v_a_dt_bias=v_a_dt_bias, v_a_o_gain=v_a_o_gain, v_a_w_out=v_a_w_out, v_kv_norm=v_kv_norm, v_kv_w_down=v_kv_w_down, v_kv_latent_norm=v_kv_latent_norm, v_kv_w_uk=v_kv_w_uk, v_kv_w_uv=v_kv_w_uv, v_k_gain=v_k_gain, v_b_norm=v_b_norm, v_b_w_in=v_b_w_in, v_b_q_latent_norm=v_b_q_latent_norm, v_b_w_uq=v_b_w_uq, v_b_q_gain=v_b_q_gain, v_b_w_out=v_b_w_out)
    weights = {n: given[n] for n in TWIN_WEIGHTS}
    shared = {n: given[n] for n in SHARED_INPUTS}
    per_example = {n: given[n] for n in ['x']}
    grad_fn = _jax.value_and_grad(_loss, argnums=(0, 1))

    def one_microbatch(ex, loss_target):
        ex = dict(ex)
        diff = ex.pop(TWIN_DIFF_INPUT)
        return grad_fn(weights, diff, {**shared, **ex}, loss_target)

    if N_MICROBATCH == 1:
        loss, (grad_w, grad_x) = one_microbatch(per_example, given["loss_target"])
    else:
        def body(carry, xs):
            loss_sum, grad_sum = carry
            l_k, (gw_k, gx_k) = one_microbatch(xs[0], xs[1])
            with _jax.named_scope("update"):
                return (loss_sum + l_k, _jax.tree.map(_jnp.add, grad_sum, gw_k)), gx_k

        init = (_jnp.zeros((), _jnp.float32), _jax.tree.map(_jnp.zeros_like, weights))
        (loss, grad_w), grad_x = _jax.lax.scan(body, init, (per_example, given["loss_target"]))
    with _jax.named_scope("update"):
        delta_w, new_m, new_v = {}, {}, {}
        for n in TWIN_WEIGHTS:
            delta_w[n], new_m[n], new_v[n] = _adamw(weights[n], grad_w[n], given["m_" + n], given["v_" + n])
    return (loss, grad_x, *[grad_w[n] for n in TWIN_WEIGHTS], *[delta_w[n] for n in TWIN_WEIGHTS],
            *[new_m[n] for n in TWIN_WEIGHTS], *[new_v[n] for n in TWIN_WEIGHTS])
```

```python
import dataclasses
import functools
import math

import jax
import jax.numpy as jnp
from jax import lax
from jax.experimental import pallas as pl
from jax.experimental.pallas import tpu as pltpu

F32 = jnp.float32
BF16 = jnp.bfloat16
_MXU_DTYPE = jnp.bfloat16
_HI = lax.Precision.HIGHEST

N_DEV = 8
D_MODEL = 1024
N_HEADS = 8
HEAD = 128
CHUNK = 64
N_META = 16
PAD_ROWS = 2 * CHUNK - N_META
LEAD = PAD_ROWS + N_META
ROPE = 64
QK_DIM = HEAD + ROPE
QK_PAD = 2 * HEAD
KV_RANK = 256
Q_RANK = 384
CONV_K = 4
EPS = 1e-6
NEG = -1e30
ROPE_THETA = 10000.0
ADAM_LR, ADAM_B1, ADAM_B2, ADAM_EPS, ADAM_WD, ADAM_STEP = 0.001, 0.9, 0.999, 1e-08, 0.01, 10
PACK_COLS = 512
VMEM_LIMIT = 56 * 1024 * 1024


def _pick(n, options):
    for o in options:
        if n % o == 0:
            return o
    raise ValueError(f"no tile for {n} among {options}")


def _cparams(sem):
    return pltpu.CompilerParams(dimension_semantics=sem, vmem_limit_bytes=VMEM_LIMIT)


def _dot(a, b, dims):
    return lax.dot_general(a.astype(_MXU_DTYPE), b.astype(_MXU_DTYPE), (dims, ((), ())),
                           preferred_element_type=F32)


@jax.custom_vjp
def mm_nn(a, b):
    return _dot(a, b, ((1,), (0,)))


@jax.custom_vjp
def mm_nt(a, b):
    return _dot(a, b, ((1,), (1,)))


@jax.custom_vjp
def mm_tn(a, b):
    return _dot(a, b, ((0,), (0,)))


mm_nn.defvjp(lambda a, b: (mm_nn(a, b), (a, b)), lambda r, g: (mm_nt(g, r[1]), mm_tn(r[0], g)))
mm_nt.defvjp(lambda a, b: (mm_nt(a, b), (a, b)), lambda r, g: (mm_nn(g, r[1]), mm_tn(g, r[0])))
mm_tn.defvjp(lambda a, b: (mm_tn(a, b), (a, b)), lambda r, g: (mm_nt(r[1], g), mm_nn(r[0], g)))


def _dot_f32(a, b):
    return lax.dot_general(a, b, (((1,), (0,)), ((), ())), precision=_HI, preferred_element_type=F32)


def _split_hi_lo(x):
    hi = x.astype(_MXU_DTYPE)
    lo = (x - hi.astype(F32)).astype(_MXU_DTYPE)
    return hi, lo


def _mm_3pass(a, b):
    ah, al = _split_hi_lo(a)
    bh, bl = _split_hi_lo(b)
    d = lambda u, w: lax.dot_general(u, w, (((1,), (0,)), ((), ())), preferred_element_type=F32)
    return d(ah, bh) + (d(ah, bl) + d(al, bh))


def _inv_unit_lower(a):
    n = a.shape[0]
    eye = (lax.broadcasted_iota(jnp.int32, (n, n), 0) == lax.broadcasted_iota(jnp.int32, (n, n), 1)).astype(F32)
    t = eye - a
    p = _mm_3pass(a, a)
    squarings = int(math.log2(n)) - 1
    for s in range(squarings):
        t = t + _mm_3pass(t, p)
        if s + 1 < squarings:
            p = _mm_3pass(p, p)
    return t


@jax.custom_vjp
def _inv_lookup(a, t):
    return t


def _inv_lookup_bwd(t, g):
    return -mm_tn(t, mm_nt(g, t)), jnp.zeros_like(t)


_inv_lookup.defvjp(lambda a, t: (t, t), _inv_lookup_bwd)


def _sigmoid(x):
    return 1.0 / (1.0 + jnp.exp(-x))


def _silu(x):
    return x * _sigmoid(x)


def _softplus(x):
    return jnp.where(x > 20.0, x, jnp.log(1.0 + jnp.exp(jnp.minimum(x, 20.0))))


def _rms(x, g, width=None):
    ms = jnp.sum(x * x, -1, keepdims=True) / (x.shape[-1] if width is None else width)
    return x * lax.rsqrt(ms + EPS) * g


def matmul(name, a, b, mode, out_dtype=F32, res=None):
    if mode == "nn":
        (m, k), (k2, n) = a.shape, b.shape
    elif mode == "nt":
        (m, k), (n, k2) = a.shape, b.shape
    else:
        (k, m), (k2, n) = a.shape, b.shape
    assert k == k2, (name, a.shape, b.shape, mode)
    tm = _pick(m, (512, 384, 256, 128))
    tn = _pick(n, (512, 384, 256, 128))
    tk = _pick(k, (512, 256, 128)) if mode == "tn" else _pick(k, (1024, 512, 384, 256, 128))
    nk = k // tk
    dims = {"nn": ((1,), (0,)), "nt": ((1,), (1,)), "tn": ((0,), (0,))}[mode]

    def body(*refs):
        if res is None:
            a_ref, b_ref, o_ref, acc_ref = refs
        else:
            a_ref, b_ref, r_ref, o_ref, acc_ref = refs
        kk = pl.program_id(2)

        @pl.when(kk == 0)
        def _():
            acc_ref[...] = jnp.zeros_like(acc_ref)

        acc_ref[...] += _dot(a_ref[...], b_ref[...], dims)

        @pl.when(kk == nk - 1)
        def _():
            out = acc_ref[...]
            if res is not None:
                out = out + r_ref[...].astype(F32)
            o_ref[...] = out.astype(o_ref.dtype)

    a_spec = pl.BlockSpec((tk, tm), lambda i, j, kk: (kk, i)) if mode == "tn" else pl.BlockSpec((tm, tk), lambda i, j, kk: (i, kk))
    b_spec = pl.BlockSpec((tn, tk), lambda i, j, kk: (j, kk)) if mode == "nt" else pl.BlockSpec((tk, tn), lambda i, j, kk: (kk, j))
    o_spec = pl.BlockSpec((tm, tn), lambda i, j, kk: (i, j))
    in_specs = [a_spec, b_spec] + ([o_spec] if res is not None else [])
    args = (a, b) + ((res,) if res is not None else ())
    return pl.pallas_call(
        body, grid=(m // tm, n // tn, nk), in_specs=in_specs, out_specs=o_spec,
        out_shape=jax.ShapeDtypeStruct((m, n), out_dtype), scratch_shapes=[pltpu.VMEM((tm, tn), F32)],
        compiler_params=_cparams(("parallel", "parallel", "arbitrary")), name=name)(*args)


@dataclasses.dataclass
class Arg:
    arr: jax.Array
    kind: str = "row"
    bc: int = 0
    base: int = 0
    ph: bool = False
    diff: bool = False


def _arg_spec(a, tr, head_inner, ntab, base=None):
    bc = a.bc or a.arr.shape[1]
    base = a.base if base is None else base

    def imap(g0, g1):
        i, h = (g0, g1) if head_inner else (g1, g0)
        col = base + (h if a.ph else 0)
        if a.kind == "row":
            return (i, col)
        if a.kind == "tab":
            return (i % ntab, col)
        return (0, col)

    rows = tr if a.kind in ("row", "tab") else a.arr.shape[0]
    return pl.BlockSpec((rows, bc), imap)


def _load(ref):
    v = ref[...]
    return v.astype(F32) if jnp.issubdtype(v.dtype, jnp.floating) else v


def row_call(name, fn, args, outs, tr, nh=1, head_inner=True, ntab=1):
    t = args[0].arr.shape[0]
    ni = t // tr
    n_in = len(args)

    def body(*refs):
        res = fn(*[_load(r) for r in refs[:n_in]])
        for r, v in zip(refs[n_in:], res, strict=True):
            r[...] = v.astype(r.dtype)

    grid = (ni, nh) if head_inner else (nh, ni)
    out_specs = [_arg_spec(Arg(None, "row", bc, 0, ph), tr, head_inner, ntab) for (_, _, bc, ph) in outs]
    out_shape = [jax.ShapeDtypeStruct((t, cols), dt) for (cols, dt, _, _) in outs]
    return pl.pallas_call(
        body, grid=grid, in_specs=[_arg_spec(a, tr, head_inner, ntab) for a in args], out_specs=out_specs,
        out_shape=out_shape, compiler_params=_cparams(("arbitrary", "arbitrary")), name=name)(*[a.arr for a in args])


def row_vjp_call(name, fn, args, cts, tr, nh=1, head_inner=True, ntab=1):
    t = args[0].arr.shape[0]
    ni = t // tr
    n_in, n_ct = len(args), len(cts)
    diff_idx = [k for k, a in enumerate(args) if a.diff]
    modes = []
    for k in diff_idx:
        a = args[k]
        if a.kind == "row":
            modes.append("write" if (a.ph or nh == 1) else "acc_heads")
        else:
            modes.append("acc_rows" if a.ph else "acc_all")
    assert not ("acc_heads" in modes and not head_inner) and not ("acc_rows" in modes and head_inner and nh > 1)

    def body(*refs):
        vals = [_load(r) for r in refs[:n_in]]
        ct_vals = tuple(_load(r) for r in refs[n_in:n_in + n_ct])
        out_refs = refs[n_in + n_ct:]
        g0, g1 = pl.program_id(0), pl.program_id(1)
        i, h = (g0, g1) if head_inner else (g1, g0)

        def f(*dv):
            full = list(vals)
            for k, v in zip(diff_idx, dv, strict=True):
                full[k] = v
            return tuple(fn(*full))

        _, vjp = jax.vjp(f, *[vals[k] for k in diff_idx])
        grads = vjp(ct_vals)
        for r, g, mode in zip(out_refs, grads, modes, strict=True):
            if mode == "write":
                r[...] = g.astype(r.dtype)
            else:
                first = {"acc_heads": h == 0, "acc_rows": i == 0, "acc_all": (i == 0) & (h == 0)}[mode]

                @pl.when(first)
                def _(r=r):
                    r[...] = jnp.zeros_like(r)

                r[...] += g

    grid = (ni, nh) if head_inner else (nh, ni)
    out_specs, out_shape = [], []
    for k in diff_idx:
        a = args[k]
        bc = a.bc or a.arr.shape[1]
        out_specs.append(_arg_spec(a, tr, head_inner, ntab, base=0))
        out_shape.append(jax.ShapeDtypeStruct((t if a.kind == "row" else a.arr.shape[0], bc * (nh if a.ph else 1)), F32))
    in_specs = [_arg_spec(a, tr, head_inner, ntab) for a in list(args) + list(cts)]
    return pl.pallas_call(
        body, grid=grid, in_specs=in_specs, out_specs=out_specs, out_shape=out_shape,
        compiler_params=_cparams(("arbitrary", "arbitrary")), name=name)(*[a.arr for a in list(args) + list(cts)])


def _conv_taps(x, w):
    rows = lax.broadcasted_iota(jnp.int32, x.shape, 0)
    y = x * w[CONV_K - 1:CONV_K, :]
    shifted = []
    for s in range(1, CONV_K):
        xs = jnp.where(rows >= s, pltpu.roll(x, s, 0), 0.0)
        shifted.append(xs)
        y = y + xs * w[CONV_K - 1 - s:CONV_K - s, :]
    return y, shifted


def _conv_post(y, mode):
    a = _silu(y)
    if mode == "v":
        return a
    out = a * lax.rsqrt(jnp.sum(a * a, -1, keepdims=True) + EPS)
    return out * (HEAD ** -0.5) if mode == "q" else out


def conv_fwd(name, z, w, mode, lp):
    t = z.shape[0]
    base = {"q": 0, "k": N_HEADS, "v": 2 * N_HEADS}[mode]

    def body(z_ref, w_ref, o_ref):
        y, _ = _conv_taps(z_ref[...], w_ref[...])
        o_ref[...] = _conv_post(y, mode)

    return pl.pallas_call(
        body, grid=(t // lp, N_HEADS),
        in_specs=[pl.BlockSpec((lp, HEAD), lambda b, h: (b, base + h)), pl.BlockSpec((CONV_K, HEAD), lambda b, h: (0, base + h))],
        out_specs=pl.BlockSpec((lp, HEAD), lambda b, h: (b, h)), out_shape=jax.ShapeDtypeStruct((t, N_HEADS * HEAD), F32),
        compiler_params=_cparams(("arbitrary", "arbitrary")), name=name)(z, w)


def conv_bwd(name, z, w, dout, mode, lp):
    t = z.shape[0]
    base = {"q": 0, "k": N_HEADS, "v": 2 * N_HEADS}[mode]

    def body(z_ref, w_ref, g_ref, dz_ref, dw_ref):
        x, wv = z_ref[...], w_ref[...]
        y, shifted = _conv_taps(x, wv)
        _, vjp = jax.vjp(lambda y_: _conv_post(y_, mode), y)
        (dy,) = vjp(g_ref[...])
        rows = lax.broadcasted_iota(jnp.int32, x.shape, 0)
        dx = dy * wv[CONV_K - 1:CONV_K, :]
        for s in range(1, CONV_K):
            dx = dx + jnp.where(rows < lp - s, pltpu.roll(dy, lp - s, 0), 0.0) * wv[CONV_K - 1 - s:CONV_K - s, :]
        dz_ref[...] = dx

        @pl.when(pl.program_id(1) == 0)
        def _():
            dw_ref[...] = jnp.zeros_like(dw_ref)

        dw_ref[CONV_K - 1:CONV_K, :] += jnp.sum(dy * x, axis=0, keepdims=True)
        for s in range(1, CONV_K):
            dw_ref[CONV_K - 1 - s:CONV_K - s, :] += jnp.sum(dy * shifted[s - 1], axis=0, keepdims=True)

    return pl.pallas_call(
        body, grid=(N_HEADS, t // lp),
        in_specs=[pl.BlockSpec((lp, HEAD), lambda h, b: (b, base + h)), pl.BlockSpec((CONV_K, HEAD), lambda h, b: (0, base + h)),
                  pl.BlockSpec((lp, HEAD), lambda h, b: (b, h))],
        out_specs=[pl.BlockSpec((lp, HEAD), lambda h, b: (b, h)), pl.BlockSpec((CONV_K, HEAD), lambda h, b: (0, h))],
        out_shape=[jax.ShapeDtypeStruct((t, N_HEADS * HEAD), F32), jax.ShapeDtypeStruct((CONV_K, N_HEADS * HEAD), F32)],
        compiler_params=_cparams(("arbitrary", "arbitrary")), name=name)(z, w, dout)


def _delta_chunk(q, k, v, ba, alog, dtb, state, t_stored, h):
    c = q.shape[0]
    lane = lax.broadcasted_iota(jnp.int32, (1, HEAD), 1)
    sel_b = (lane == h).astype(F32)
    sel_a = (lane == N_HEADS + h).astype(F32)
    b_raw = jnp.sum(ba * sel_b, axis=1, keepdims=True)
    a_raw = jnp.sum(ba * sel_a, axis=1, keepdims=True)
    a_log = jnp.sum(alog * sel_b, axis=1, keepdims=True)
    dt_bias = jnp.sum(dtb * sel_b, axis=1, keepdims=True)
    beta = _sigmoid(b_raw)
    g = -jnp.exp(a_log) * _softplus(a_raw + dt_bias)
    ri = lax.broadcasted_iota(jnp.int32, (c, c), 0)
    ci = lax.broadcasted_iota(jnp.int32, (c, c), 1)
    tril = ci <= ri
    gc_col = _dot_f32(tril.astype(F32), g * jnp.ones((1, HEAD), F32))[:, :1]
    gc_row = _dot_f32(jnp.ones((8, c), F32), g * (ri <= ci).astype(F32))[0:1, :]
    gc_last = jnp.sum(g, axis=0, keepdims=True)
    decay = jnp.exp(jnp.where(tril, gc_col - gc_row, NEG))
    e_gc = jnp.exp(gc_col)
    kb = k * beta
    a_mat = jnp.where(ci < ri, mm_nt(kb, k) * decay, 0.0)
    t_inv = _inv_unit_lower(a_mat) if t_stored is None else _inv_lookup(a_mat, t_stored)
    u_base = mm_nn(t_inv, v * beta)
    w_dec = mm_nn(t_inv, kb * e_gc)
    attn = jnp.where(tril, mm_nt(q, k) * decay, 0.0)
    u = u_base - mm_nn(w_dec, state)
    o = mm_nn(q * e_gc, state) + mm_nn(attn, u)
    new_state = state * jnp.exp(gc_last) + mm_tn(k * jnp.exp(gc_last - gc_col), u)
    return o, new_state, t_inv


def delta_fwd(q, k, v, ba, alog, dtb, lp):
    t = q.shape[0]
    nb, nc = t // lp, lp // CHUNK

    def body(q_ref, k_ref, v_ref, ba_ref, al_ref, dt_ref, o_ref, s_ref, t_ref):
        h = pl.program_id(1)
        al, dtv = al_ref[...], dt_ref[...]

        def step(n, state):
            sl = pl.ds(pl.multiple_of(n * CHUNK, CHUNK), CHUNK)
            o, new_state, t_inv = _delta_chunk(q_ref[sl, :], k_ref[sl, :], v_ref[sl, :], ba_ref[sl, :], al, dtv, state, None, h)
            o_ref[sl, :] = o
            s_ref[n] = state
            t_ref[n] = t_inv
            return new_state

        lax.fori_loop(0, nc, step, jnp.zeros((HEAD, HEAD), F32))

    head_spec = pl.BlockSpec((lp, HEAD), lambda b, h: (b, h))
    par_spec = pl.BlockSpec((1, HEAD), lambda b, h: (0, 0))
    return pl.pallas_call(
        body, grid=(nb, N_HEADS),
        in_specs=[head_spec, head_spec, head_spec, pl.BlockSpec((lp, HEAD), lambda b, h: (b, 0)), par_spec, par_spec],
        out_specs=[head_spec, pl.BlockSpec((None, None, nc, HEAD, HEAD), lambda b, h: (b, h, 0, 0, 0)),
                   pl.BlockSpec((None, None, nc, CHUNK, CHUNK), lambda b, h: (b, h, 0, 0, 0))],
        out_shape=[jax.ShapeDtypeStruct((t, N_HEADS * HEAD), F32), jax.ShapeDtypeStruct((nb, N_HEADS, nc, HEAD, HEAD), F32),
                   jax.ShapeDtypeStruct((nb, N_HEADS, nc, CHUNK, CHUNK), F32)],
        compiler_params=_cparams(("arbitrary", "arbitrary")), name="delta_fwd")(q, k, v, ba, alog, dtb)


def delta_bwd(q, k, v, ba, alog, dtb, states, t_invs, do, lp):
    t = q.shape[0]
    nb, nc = t // lp, lp // CHUNK

    def body(q_ref, k_ref, v_ref, ba_ref, al_ref, dt_ref, s_ref, t_ref, do_ref, dq_ref, dk_ref, dv_ref, dba_ref, dal_ref, ddt_ref):
        b, h = pl.program_id(0), pl.program_id(1)
        al, dtv = al_ref[...], dt_ref[...]

        @pl.when(h == 0)
        def _():
            dba_ref[...] = jnp.zeros_like(dba_ref)

        @pl.when((b == 0) & (h == 0))
        def _():
            dal_ref[...] = jnp.zeros_like(dal_ref)
            ddt_ref[...] = jnp.zeros_like(ddt_ref)

        def step(it, carry):
            d_state, d_al, d_dt = carry
            n = nc - 1 - it
            sl = pl.ds(pl.multiple_of(n * CHUNK, CHUNK), CHUNK)
            t_n = t_ref[n]

            def f(q_, k_, v_, ba_, al_, dt_, s_):
                return _delta_chunk(q_, k_, v_, ba_, al_, dt_, s_, t_n, h)[:2]

            _, vjp = jax.vjp(f, q_ref[sl, :], k_ref[sl, :], v_ref[sl, :], ba_ref[sl, :], al, dtv, s_ref[n])
            gq, gk, gv, gba, gal, gdt, gs = vjp((do_ref[sl, :], d_state))
            dq_ref[sl, :] = gq
            dk_ref[sl, :] = gk
            dv_ref[sl, :] = gv
            dba_ref[sl, :] += gba
            return gs, d_al + gal, d_dt + gdt

        zero_par = jnp.zeros((1, HEAD), F32)
        _, d_al, d_dt = lax.fori_loop(0, nc, step, (jnp.zeros((HEAD, HEAD), F32), zero_par, zero_par))
        dal_ref[...] += d_al
        ddt_ref[...] += d_dt

    head_spec = pl.BlockSpec((lp, HEAD), lambda b, h: (b, h))
    ba_spec = pl.BlockSpec((lp, HEAD), lambda b, h: (b, 0))
    par_spec = pl.BlockSpec((1, HEAD), lambda b, h: (0, 0))
    big = jax.ShapeDtypeStruct((t, N_HEADS * HEAD), F32)
    return pl.pallas_call(
        body, grid=(nb, N_HEADS),
        in_specs=[head_spec, head_spec, head_spec, ba_spec, par_spec, par_spec,
                  pl.BlockSpec((None, None, nc, HEAD, HEAD), lambda b, h: (b, h, 0, 0, 0)),
                  pl.BlockSpec((None, None, nc, CHUNK, CHUNK), lambda b, h: (b, h, 0, 0, 0)), head_spec],
        out_specs=[head_spec, head_spec, head_spec, ba_spec, par_spec, par_spec],
        out_shape=[big, big, big, jax.ShapeDtypeStruct((t, HEAD), F32), jax.ShapeDtypeStruct((1, HEAD), F32), jax.ShapeDtypeStruct((1, HEAD), F32)],
        compiler_params=_cparams(("arbitrary", "arbitrary")), name="delta_bwd")(q, k, v, ba, alog, dtb, states, t_invs, do)


ATT_BLOCK = 128
ATT_SCALE = QK_DIM ** -0.5


def _att_mask(i, j):
    qpos = i * ATT_BLOCK + lax.broadcasted_iota(jnp.int32, (ATT_BLOCK, ATT_BLOCK), 0)
    kpos = j * ATT_BLOCK + lax.broadcasted_iota(jnp.int32, (ATT_BLOCK, ATT_BLOCK), 1)
    shift = CHUNK.bit_length() - 1
    return (jnp.right_shift(kpos, shift) <= jnp.right_shift(qpos, shift)) & (kpos >= PAD_ROWS)


def _att_specs(lp):
    nq = lp // ATT_BLOCK
    q_blk = pl.BlockSpec((ATT_BLOCK, QK_PAD), lambda b, h, i: (b * nq + i, h))
    o_blk = pl.BlockSpec((ATT_BLOCK, HEAD), lambda b, h, i: (b * nq + i, h))
    q_seq = pl.BlockSpec((lp, QK_PAD), lambda b, h, i: (b, h))
    o_seq = pl.BlockSpec((lp, HEAD), lambda b, h, i: (b, h))
    return nq, q_blk, o_blk, q_seq, o_seq


def flash_fwd(q, k, v, lp):
    t = q.shape[0]
    nq, q_blk, o_blk, q_seq, o_seq = _att_specs(lp)

    def body(q_ref, k_ref, v_ref, o_ref, lse_ref):
        i = pl.program_id(2)
        qb = q_ref[...]

        def step(j, carry):
            m, l, acc = carry
            sl = pl.ds(pl.multiple_of(j * ATT_BLOCK, ATT_BLOCK), ATT_BLOCK)
            s = jnp.where(_att_mask(i, j), mm_nt(qb, k_ref[sl, :]) * ATT_SCALE, NEG)
            m2 = jnp.maximum(m, jnp.max(s, -1, keepdims=True))
            p = jnp.exp(s - m2)
            alpha = jnp.exp(m - m2)
            return m2, alpha * l + jnp.sum(p, -1, keepdims=True), alpha * acc + mm_nn(p, v_ref[sl, :])

        init = (jnp.full((ATT_BLOCK, 1), NEG, F32), jnp.zeros((ATT_BLOCK, 1), F32), jnp.zeros((ATT_BLOCK, HEAD), F32))
        m, l, acc = lax.fori_loop(0, i + 1, step, init)
        o_ref[...] = acc / l
        lse_ref[...] = jnp.broadcast_to(m + jnp.log(l), (ATT_BLOCK, HEAD))

    big = jax.ShapeDtypeStruct((t, N_HEADS * HEAD), F32)
    return pl.pallas_call(
        body, grid=(t // lp, N_HEADS, nq), in_specs=[q_blk, q_seq, o_seq], out_specs=[o_blk, o_blk], out_shape=[big, big],
        compiler_params=_cparams(("arbitrary", "arbitrary", "arbitrary")), name="flash_fwd")(q, k, v)


def flash_dq(q, k, v, o, lse, do, lp):
    t = q.shape[0]
    nq, q_blk, o_blk, q_seq, o_seq = _att_specs(lp)

    def body(q_ref, k_ref, v_ref, o_ref, lse_ref, do_ref, dq_ref, dsum_ref):
        i = pl.program_id(2)
        qb, dob = q_ref[...], do_ref[...]
        lse = lse_ref[...][:, :1]
        dsum = jnp.sum(dob * o_ref[...], -1, keepdims=True)

        def step(j, dq):
            sl = pl.ds(pl.multiple_of(j * ATT_BLOCK, ATT_BLOCK), ATT_BLOCK)
            kb = k_ref[sl, :]
            s = jnp.where(_att_mask(i, j), mm_nt(qb, kb) * ATT_SCALE, NEG)
            p = jnp.exp(s - lse)
            ds = p * (mm_nt(dob, v_ref[sl, :]) - dsum) * ATT_SCALE
            return dq + mm_nn(ds, kb)

        dq_ref[...] = lax.fori_loop(0, i + 1, step, jnp.zeros((ATT_BLOCK, QK_PAD), F32))
        dsum_ref[...] = jnp.broadcast_to(dsum, (ATT_BLOCK, HEAD))

    return pl.pallas_call(
        body, grid=(t // lp, N_HEADS, nq), in_specs=[q_blk, q_seq, o_seq, o_blk, o_blk, o_blk], out_specs=[q_blk, o_blk],
        out_shape=[jax.ShapeDtypeStruct((t, N_HEADS * QK_PAD), F32), jax.ShapeDtypeStruct((t, N_HEADS * HEAD), F32)],
        compiler_params=_cparams(("arbitrary", "arbitrary", "arbitrary")), name="flash_dq")(q, k, v, o, lse, do)


def flash_dkv(q, k, v, lse, dsum, do, lp):
    t = q.shape[0]
    nq, q_blk, o_blk, q_seq, o_seq = _att_specs(lp)

    def body(q_ref, k_ref, v_ref, lse_ref, dsum_ref, do_ref, dk_ref, dv_ref):
        j = pl.program_id(2)
        kb, vb = k_ref[...], v_ref[...]

        def step(i, carry):
            dk, dv = carry
            sl = pl.ds(pl.multiple_of(i * ATT_BLOCK, ATT_BLOCK), ATT_BLOCK)
            qb, dob = q_ref[sl, :], do_ref[sl, :]
            s = jnp.where(_att_mask(i, j), mm_nt(qb, kb) * ATT_SCALE, NEG)
            p = jnp.exp(s - lse_ref[sl, :][:, :1])
            ds = p * (mm_nt(dob, vb) - dsum_ref[sl, :][:, :1]) * ATT_SCALE
            return dk + mm_tn(ds, qb), dv + mm_tn(p, dob)

        dk, dv = lax.fori_loop(j, nq, step, (jnp.zeros((ATT_BLOCK, QK_PAD), F32), jnp.zeros((ATT_BLOCK, HEAD), F32)))
        dk_ref[...] = dk
        dv_ref[...] = dv

    return pl.pallas_call(
        body, grid=(t // lp, N_HEADS, nq), in_specs=[q_seq, q_blk, o_blk, o_seq, o_seq, o_seq], out_specs=[q_blk, o_blk],
        out_shape=[jax.ShapeDtypeStruct((t, N_HEADS * QK_PAD), F32), jax.ShapeDtypeStruct((t, N_HEADS * HEAD), F32)],
        compiler_params=_cparams(("arbitrary", "arbitrary", "arbitrary")), name="flash_dkv")(q, k, v, lse, dsum, do)


def loss_head(h2, target, lp):
    nb, seq, d = target.shape
    tr = 128
    nblk = lp // tr
    lead_blocks = LEAD // tr

    def body(h_ref, t_ref, loss_ref, dh_ref, acc_ref):
        b, i = pl.program_id(0), pl.program_id(1)

        @pl.when((b == 0) & (i == 0))
        def _():
            acc_ref[...] = jnp.zeros_like(acc_ref)

        @pl.when(i < lead_blocks)
        def _():
            dh_ref[...] = jnp.zeros_like(dh_ref)

        @pl.when(i >= lead_blocks)
        def _():
            err = h_ref[...] - t_ref[...]
            dh_ref[...] = err * (1.0 / d)
            acc_ref[...] += jnp.sum(err * err, axis=0, keepdims=True)

        @pl.when((b == nb - 1) & (i == nblk - 1))
        def _():
            loss_ref[...] = jnp.sum(acc_ref[...], axis=1, keepdims=True) * (0.5 / d)

    return pl.pallas_call(
        body, grid=(nb, nblk),
        in_specs=[pl.BlockSpec((None, tr, d), lambda b, i: (b, i, 0)),
                  pl.BlockSpec((None, tr, d), lambda b, i: (b, jnp.maximum(i - lead_blocks, 0), 0))],
        out_specs=[pl.BlockSpec((1, 1), lambda b, i: (0, 0)), pl.BlockSpec((None, tr, d), lambda b, i: (b, i, 0))],
        out_shape=[jax.ShapeDtypeStruct((1, 1), F32), jax.ShapeDtypeStruct((nb, lp, d), F32)],
        scratch_shapes=[pltpu.VMEM((1, d), F32)], compiler_params=_cparams(("arbitrary", "arbitrary")), name="loss_head")(h2, target)


def meta_grad(dh0):
    nb, _, d = dh0.shape

    def body(g_ref, o_ref):
        @pl.when(pl.program_id(0) == 0)
        def _():
            o_ref[...] = jnp.zeros_like(o_ref)

        o_ref[...] += g_ref[PAD_ROWS:LEAD, :]

    return pl.pallas_call(
        body, grid=(nb,), in_specs=[pl.BlockSpec((None, LEAD, d), lambda b: (b, 0, 0))],
        out_specs=pl.BlockSpec((N_META, d), lambda b: (0, 0)), out_shape=jax.ShapeDtypeStruct((N_META, d), F32),
        compiler_params=_cparams(("arbitrary",)), name="meta_grad")(dh0)


def adamw_reduce(recv, w, m, v):
    _, r, c = recv.shape
    tr = _pick(r, (256, 128, 64, 32, 16, 8))

    def body(g_ref, w_ref, m_ref, v_ref, go_ref, d_ref, mo_ref, vo_ref):
        g = g_ref[0]
        for dev in range(1, N_DEV):
            g = g + g_ref[dev]
        m_new = ADAM_B1 * m_ref[...] + (1.0 - ADAM_B1) * g
        v_new = ADAM_B2 * v_ref[...] + (1.0 - ADAM_B2) * (g * g)
        m_hat = m_new / (1.0 - ADAM_B1 ** ADAM_STEP)
        v_hat = v_new / (1.0 - ADAM_B2 ** ADAM_STEP)
        go_ref[...] = g
        d_ref[...] = -ADAM_LR * (m_hat / (jnp.sqrt(v_hat) + ADAM_EPS) + ADAM_WD * w_ref[...])
        mo_ref[...] = m_new
        vo_ref[...] = v_new

    blk = pl.BlockSpec((tr, c), lambda i: (i, 0))
    out = jax.ShapeDtypeStruct((r, c), F32)
    return pl.pallas_call(
        body, grid=(r // tr,), in_specs=[pl.BlockSpec((N_DEV, tr, c), lambda i: (0, i, 0)), blk, blk, blk],
        out_specs=[blk, blk, blk, blk], out_shape=[out, out, out, out], compiler_params=_cparams(("arbitrary",)), name="adamw_reduce")(recv, w, m, v)


_HBM = pl.BlockSpec(memory_space=pltpu.HBM)


def _mesh_pos():
    x, y, c = lax.axis_index("x"), lax.axis_index("y"), lax.axis_index("c")
    return x, y, c


def _peer(x, y, c, k):
    px = 1 - x if k & 4 else x
    py = 1 - y if k & 2 else y
    pc = 1 - c if k & 1 else c
    return (px, py, pc), 4 * px + 2 * py + pc


def all_gather(buf):
    def body(x_ref, out_ref, send_sems, recv_sems, local_sem):
        x, y, c = _mesh_pos()
        me = 4 * x + 2 * y + c
        mine = pltpu.make_async_copy(x_ref, out_ref.at[me], local_sem)
        mine.start()
        sends = []
        for k in range(1, N_DEV):
            peer, _ = _peer(x, y, c, k)
            cp = pltpu.make_async_remote_copy(src_ref=x_ref, dst_ref=out_ref.at[me], send_sem=send_sems.at[k - 1],
                                              recv_sem=recv_sems.at[k - 1], device_id=peer, device_id_type=pl.DeviceIdType.MESH)
            cp.start()
            sends.append(cp)
        for k in range(1, N_DEV):
            peer, peer_id = _peer(x, y, c, k)
            pltpu.make_async_remote_copy(src_ref=x_ref, dst_ref=out_ref.at[peer_id], send_sem=send_sems.at[k - 1],
                                         recv_sem=recv_sems.at[k - 1], device_id=peer, device_id_type=pl.DeviceIdType.MESH).wait_recv()
        for cp in sends:
            cp.wait_send()
        mine.wait()

    return pl.pallas_call(
        body, in_specs=[_HBM], out_specs=_HBM, out_shape=jax.ShapeDtypeStruct((N_DEV,) + buf.shape, buf.dtype),
        scratch_shapes=[pltpu.SemaphoreType.DMA((N_DEV - 1,)), pltpu.SemaphoreType.DMA((N_DEV - 1,)), pltpu.SemaphoreType.DMA],
        name="all_gather")(buf)


def all_to_all(buf):
    def body(x_ref, out_ref, send_sems, recv_sems, local_sem):
        x, y, c = _mesh_pos()
        me = 4 * x + 2 * y + c
        mine = pltpu.make_async_copy(x_ref.at[me], out_ref.at[me], local_sem)
        mine.start()
        sends = []
        for k in range(1, N_DEV):
            peer, peer_id = _peer(x, y, c, k)
            cp = pltpu.make_async_remote_copy(src_ref=x_ref.at[peer_id], dst_ref=out_ref.at[me], send_sem=send_sems.at[k - 1],
                                              recv_sem=recv_sems.at[k - 1], device_id=peer, device_id_type=pl.DeviceIdType.MESH)
            cp.start()
            sends.append(cp)
        for k in range(1, N_DEV):
            peer, peer_id = _peer(x, y, c, k)
            pltpu.make_async_remote_copy(src_ref=x_ref.at[peer_id], dst_ref=out_ref.at[peer_id], send_sem=send_sems.at[k - 1],
                                         recv_sem=recv_sems.at[k - 1], device_id=peer, device_id_type=pl.DeviceIdType.MESH).wait_recv()
        for cp in sends:
            cp.wait_send()
        mine.wait()

    return pl.pallas_call(
        body, in_specs=[_HBM], out_specs=_HBM, out_shape=jax.ShapeDtypeStruct(buf.shape, buf.dtype),
        scratch_shapes=[pltpu.SemaphoreType.DMA((N_DEV - 1,)), pltpu.SemaphoreType.DMA((N_DEV - 1,)), pltpu.SemaphoreType.DMA],
        name="all_to_all")(buf)


def _f_rms(x, g):
    return (_rms(x, g),)


def _f_rms2(x, g1, g2):
    r = x * lax.rsqrt(jnp.sum(x * x, -1, keepdims=True) / x.shape[-1] + EPS)
    return r * g1, r * g2


def _f_out_gate(o, gate, gain):
    return (_rms(o, gain) * _silu(gate),)


def _f_gate(o, gate):
    return (o * _silu(gate),)


def _f_qk_final(nope, rope_in, g_nope, g_rope, cos, sin, perm):
    ms = (jnp.sum(nope * nope, -1, keepdims=True) + jnp.sum(rope_in * rope_in, -1, keepdims=True)) / QK_DIM
    r = lax.rsqrt(ms + EPS)
    a = nope * r * g_nope
    b = rope_in * r * g_rope
    return (jnp.concatenate([a, b * cos + _dot_f32(b, perm) * sin], axis=1),)


def _rope_tables(lp):
    half = ROPE // 2
    pos = jnp.maximum(jnp.arange(lp) - PAD_ROWS, 0)
    inv = ROPE_THETA ** (-jnp.arange(half, dtype=F32) / half)
    ang = pos.astype(F32)[:, None] * inv[None, :]
    zeros = jnp.zeros((lp, HEAD - ROPE), F32)
    cos = jnp.concatenate([jnp.cos(ang), jnp.cos(ang), zeros], 1)
    sin = jnp.concatenate([-jnp.sin(ang), jnp.sin(ang), zeros], 1)
    src = jnp.arange(HEAD)[:, None]
    dst = jnp.arange(HEAD)[None, :]
    perm = (((dst < half) & (src == dst + half)) | ((dst >= half) & (dst < ROPE) & (src == dst - half))).astype(F32)
    return cos, sin, perm


def _pad_lanes(w, width=HEAD):
    return jnp.pad(w, ((0, 0), (0, width - w.shape[1])))


def _split_heads_qk(w):
    k = w.shape[0]
    w3 = w.reshape(k, N_HEADS, QK_DIM)
    nope = w3[:, :, :HEAD].reshape(k, N_HEADS * HEAD)
    rope = jnp.pad(w3[:, :, HEAD:], ((0, 0), (0, 0), (0, HEAD - ROPE))).reshape(k, N_HEADS * HEAD)
    return nope, rope


def _merge_heads_qk(nope, rope):
    k = nope.shape[0]
    return jnp.concatenate([nope.reshape(k, N_HEADS, HEAD), rope.reshape(k, N_HEADS, HEAD)[:, :, :ROPE]], 2).reshape(k, N_HEADS * QK_DIM)


def local_step(x, target, w):
    nb, seq, d = x.shape
    lp = seq + LEAD
    t = nb * lp
    tr = _pick(lp, (544, 128))
    ntab = lp // tr
    mxu = _MXU_DTYPE
    kw = N_HEADS * HEAD

    a_w_in = w["a_w_in"][0]
    w_qkv, w_ga, w_ba = a_w_in[:, :3 * kw].astype(mxu), a_w_in[:, 3 * kw:4 * kw].astype(mxu), _pad_lanes(a_w_in[:, 4 * kw:]).astype(mxu)
    a_conv = w["a_conv"][0]
    a_w_out = w["a_w_out"][0].astype(mxu)
    alog, dtb, o_gain = _pad_lanes(w["a_log"]), _pad_lanes(w["a_dt_bias"]), w["a_o_gain"]
    w_dkv, w_dpe = w["kv_w_down"][:, :KV_RANK].astype(mxu), _pad_lanes(w["kv_w_down"][:, KV_RANK:]).astype(mxu)
    w_ukv = jnp.concatenate([w["kv_w_uk"], w["kv_w_uv"]], 1).astype(mxu)
    b_w_in = w["b_w_in"][0]
    w_cq, w_gb = b_w_in[:, :Q_RANK].astype(mxu), b_w_in[:, Q_RANK:].astype(mxu)
    w_q = jnp.concatenate(_split_heads_qk(w["b_w_uq"][0]), 1).astype(mxu)
    b_w_out = w["b_w_out"][0].astype(mxu)
    a_norm, kv_norm, b_norm = w["a_norm"], w["kv_norm"][None, :], w["b_norm"]
    lat_norm, qlat_norm = w["kv_latent_norm"][None, :], w["b_q_latent_norm"]
    kg_nope, kg_rope = w["k_gain"][None, :HEAD], _pad_lanes(w["k_gain"][None, HEAD:])
    qg_nope, qg_rope = w["b_q_gain"][:, :HEAD], _pad_lanes(w["b_q_gain"][:, HEAD:])
    cos, sin, perm = _rope_tables(lp)

    meta = jnp.broadcast_to(w["meta_tokens"][None], (nb, N_META, d))
    h0 = jnp.concatenate([jnp.zeros((nb, PAD_ROWS, d), F32), meta, x], 1).reshape(t, d)
    (hn,) = row_call("a_norm_fwd", _f_rms, [Arg(h0), Arg(a_norm, "par")], [(d, mxu, d, False)], tr)
    z_qkv = matmul("a_in_qkv", hn, w_qkv, "nn")
    gate_a = matmul("a_in_gate", hn, w_ga, "nn")
    z_ba = matmul("a_in_ba", hn, w_ba, "nn")
    qa = conv_fwd("a_conv_q", z_qkv, a_conv, "q", lp)
    ka = conv_fwd("a_conv_k", z_qkv, a_conv, "k", lp)
    va = conv_fwd("a_conv_v", z_qkv, a_conv, "v", lp)
    o_a, states, t_invs = delta_fwd(qa, ka, va, z_ba, alog, dtb, lp)
    og_args = [Arg(o_a, bc=HEAD, ph=True, diff=True), Arg(gate_a, bc=HEAD, ph=True, diff=True), Arg(o_gain, "par", diff=True)]
    (og_a,) = row_call("a_out_gate_fwd", _f_out_gate, og_args, [(kw, mxu, HEAD, True)], tr, nh=N_HEADS)
    h1 = matmul("a_out", og_a, a_w_out, "nn", res=h0)

    hk, hb = row_call("b_norms_fwd", _f_rms2, [Arg(h1), Arg(kv_norm, "par"), Arg(b_norm, "par")], [(d, mxu, d, False), (d, mxu, d, False)], tr)
    c_kv_raw = matmul("kv_down", hk, w_dkv, "nn")
    k_pe = matmul("kv_down_pe", hk, w_dpe, "nn")
    c_q_raw = matmul("b_in_q", hb, w_cq, "nn")
    gate_b = matmul("b_in_gate", hb, w_gb, "nn")
    (c_kv,) = row_call("kv_latent_fwd", _f_rms, [Arg(c_kv_raw), Arg(lat_norm, "par")], [(KV_RANK, mxu, KV_RANK, False)], tr)
    (c_q,) = row_call("q_latent_fwd", _f_rms, [Arg(c_q_raw), Arg(qlat_norm, "par")], [(Q_RANK, mxu, Q_RANK, False)], tr)
    kv_up = matmul("kv_up", c_kv, w_ukv, "nn")
    q_up = matmul("q_up", c_q, w_q, "nn")
    tabs = [Arg(cos, "tab"), Arg(sin, "tab"), Arg(perm, "par")]
    k_args = [Arg(kv_up, bc=HEAD, ph=True, diff=True), Arg(k_pe, diff=True), Arg(kg_nope, "par", diff=True), Arg(kg_rope, "par", diff=True)] + tabs
    q_args = [Arg(q_up, bc=HEAD, ph=True, diff=True), Arg(q_up, bc=HEAD, base=N_HEADS, ph=True, diff=True),
              Arg(qg_nope, "par", diff=True), Arg(qg_rope, "par", diff=True)] + tabs
    (k_fin,) = row_call("k_final_fwd", _f_qk_final, k_args, [(N_HEADS * QK_PAD, mxu, QK_PAD, True)], tr, nh=N_HEADS, ntab=ntab)
    (q_fin,) = row_call("q_final_fwd", _f_qk_final, q_args, [(N_HEADS * QK_PAD, mxu, QK_PAD, True)], tr, nh=N_HEADS, ntab=ntab)
    v_b = kv_up[:, kw:].astype(mxu)
    o_b, lse = flash_fwd(q_fin, k_fin, v_b, lp)
    gb_args = [Arg(o_b, diff=True), Arg(gate_b, diff=True)]
    (og_b,) = row_call("b_gate_fwd", _f_gate, gb_args, [(kw, mxu, kw, False)], tr)
    h2 = matmul("b_out", og_b, b_w_out, "nn", res=h1)

    loss, dh2 = loss_head(h2.reshape(nb, lp, d), target, lp)
    dh2 = dh2.reshape(t, d)
    grads = {}

    d_og_b = matmul("b_out_dx", dh2, b_w_out, "nt")
    grads["b_w_out"] = matmul("b_out_dw", og_b, dh2, "tn")[None]
    d_o_b, d_gate_b = row_vjp_call("b_gate_bwd", _f_gate, gb_args, [Arg(d_og_b)], tr)
    dq_fin, dsum = flash_dq(q_fin, k_fin, v_b, o_b, lse, d_o_b, lp)
    dk_fin, dv_b = flash_dkv(q_fin, k_fin, v_b, lse, dsum, d_o_b, lp)
    dq_nope, dq_rope, d_qg_nope, d_qg_rope = row_vjp_call(
        "q_final_bwd", _f_qk_final, q_args, [Arg(dq_fin, bc=QK_PAD, ph=True)], tr, nh=N_HEADS, ntab=ntab)
    dk_nope, dk_pe, d_kg_nope, d_kg_rope = row_vjp_call(
        "k_final_bwd", _f_qk_final, k_args, [Arg(dk_fin, bc=QK_PAD, ph=True)], tr, nh=N_HEADS, ntab=ntab)
    grads["b_q_gain"] = jnp.concatenate([d_qg_nope, d_qg_rope[:, :ROPE]], 1)
    grads["k_gain"] = jnp.concatenate([d_kg_nope, d_kg_rope[:, :ROPE]], 1)[0]
    dq_up = jnp.concatenate([dq_nope, dq_rope], 1)
    dkv_up = jnp.concatenate([dk_nope, dv_b], 1)
    d_c_q = matmul("q_up_dx", dq_up, w_q, "nt")
    d_w_q = matmul("q_up_dw", c_q, dq_up, "tn")
    grads["b_w_uq"] = _merge_heads_qk(d_w_q[:, :kw], d_w_q[:, kw:])[None]
    d_c_kv = matmul("kv_up_dx", dkv_up, w_ukv, "nt")
    d_w_ukv = matmul("kv_up_dw", c_kv, dkv_up, "tn")
    grads["kv_w_uk"], grads["kv_w_uv"] = d_w_ukv[:, :kw], d_w_ukv[:, kw:]
    d_c_q_raw, grads["b_q_latent_norm"] = row_vjp_call(
        "q_latent_bwd", _f_rms, [Arg(c_q_raw, diff=True), Arg(qlat_norm, "par", diff=True)], [Arg(d_c_q)], tr)
    d_c_kv_raw, d_lat = row_vjp_call(
        "kv_latent_bwd", _f_rms, [Arg(c_kv_raw, diff=True), Arg(lat_norm, "par", diff=True)], [Arg(d_c_kv)], tr)
    grads["kv_latent_norm"] = d_lat[0]
    d_hb = matmul("b_in_q_dx", d_c_q_raw, w_cq, "nt")
    d_hb = matmul("b_in_gate_dx", d_gate_b, w_gb, "nt", res=d_hb)
    grads["b_w_in"] = jnp.concatenate([matmul("b_in_q_dw", hb, d_c_q_raw, "tn"), matmul("b_in_gate_dw", hb, d_gate_b, "tn")], 1)[None]
    d_hk = matmul("kv_down_dx", d_c_kv_raw, w_dkv, "nt")
    d_hk = matmul("kv_down_pe_dx", dk_pe, w_dpe, "nt", res=d_hk)
    grads["kv_w_down"] = jnp.concatenate([matmul("kv_down_dw", hk, d_c_kv_raw, "tn"), matmul("kv_down_pe_dw", hk, dk_pe, "tn")[:, :ROPE]], 1)
    d_h1_norms, d_kv_norm, grads["b_norm"] = row_vjp_call(
        "b_norms_bwd", _f_rms2, [Arg(h1, diff=True), Arg(kv_norm, "par", diff=True), Arg(b_norm, "par", diff=True)], [Arg(d_hk), Arg(d_hb)], tr)
    grads["kv_norm"] = d_kv_norm[0]

    (dh1,) = row_call("dh1_sum", lambda a, b: (a + b,), [Arg(dh2), Arg(d_h1_norms)], [(d, F32, d, False)], tr)
    d_og_a = matmul("a_out_dx", dh1, a_w_out, "nt")
    grads["a_w_out"] = matmul("a_out_dw", og_a, dh1, "tn")[None]
    d_o_a, d_gate_a, grads["a_o_gain"] = row_vjp_call(
        "a_out_gate_bwd", _f_out_gate, og_args, [Arg(d_og_a, bc=HEAD, ph=True)], tr, nh=N_HEADS, head_inner=True)
    dqa, dka, dva, d_ba, d_alog, d_dtb = delta_bwd(qa, ka, va, z_ba, alog, dtb, states, t_invs, d_o_a, lp)
    grads["a_log"], grads["a_dt_bias"] = d_alog[:, :N_HEADS], d_dtb[:, :N_HEADS]
    dz_q, dw_q = conv_bwd("a_conv_q_bwd", z_qkv, a_conv, dqa, "q", lp)
    dz_k, dw_k = conv_bwd("a_conv_k_bwd", z_qkv, a_conv, dka, "k", lp)
    dz_v, dw_v = conv_bwd("a_conv_v_bwd", z_qkv, a_conv, dva, "v", lp)
    grads["a_conv"] = jnp.concatenate([dw_q, dw_k, dw_v], 1)[None]
    dz_qkv = jnp.concatenate([dz_q, dz_k, dz_v], 1)
    d_hn = matmul("a_in_qkv_dx", dz_qkv, w_qkv, "nt")
    d_hn = matmul("a_in_gate_dx", d_gate_a, w_ga, "nt", res=d_hn)
    d_hn = matmul("a_in_ba_dx", d_ba, w_ba, "nt", res=d_hn)
    grads["a_w_in"] = jnp.concatenate([matmul("a_in_qkv_dw", hn, dz_qkv, "tn"), matmul("a_in_gate_dw", hn, d_gate_a, "tn"),
                                       matmul("a_in_ba_dw", hn, d_ba, "tn")[:, :2 * N_HEADS]], 1)[None]
    d_h0_norm, grads["a_norm"] = row_vjp_call("a_norm_bwd", _f_rms, [Arg(h0, diff=True), Arg(a_norm, "par", diff=True)], [Arg(d_hn)], tr)
    (dh0,) = row_call("dh0_sum", lambda a, b: (a + b,), [Arg(dh1), Arg(d_h0_norm)], [(d, F32, d, False)], tr)
    dh0 = dh0.reshape(nb, lp, d)
    grads["meta_tokens"] = meta_grad(dh0)
    return loss, dh0[:, LEAD:], grads


_SHARDED = (
    ("meta_tokens", 1, False), ("a_norm", 1, False), ("a_w_in", 2, True), ("a_conv", 2, False), ("a_w_out", 1, True),
    ("kv_w_down", 0, True), ("kv_w_uk", 1, True), ("kv_w_uv", 1, True), ("b_w_in", 2, True), ("b_w_uq", 2, True), ("b_w_out", 1, True))
_REPLICATED = ("a_log", "a_dt_bias", "a_o_gain", "kv_norm", "kv_latent_norm", "k_gain", "b_norm", "b_q_latent_norm", "b_q_gain")
_ALL_WEIGHTS = ("meta_tokens", "a_norm", "a_w_in", "a_conv", "a_log", "a_dt_bias", "a_o_gain", "a_w_out", "kv_norm", "kv_w_down",
                "kv_latent_norm", "kv_w_uk", "kv_w_uv", "k_gain", "b_norm", "b_w_in", "b_q_latent_norm", "b_w_uq", "b_q_gain", "b_w_out")


def _round_up(n, m):
    return (n + m - 1) // m * m


def _pack_rows(pieces, row_multiple):
    padded = []
    for p in pieces:
        n = p.shape[-1]
        padded.append(jnp.pad(p, [(0, 0)] * (p.ndim - 1) + [(0, _round_up(n, PACK_COLS) - n)]))
    flat = jnp.concatenate(padded, -1)
    rows = _round_up(flat.shape[-1] // PACK_COLS, row_multiple)
    flat = jnp.pad(flat, [(0, 0)] * (flat.ndim - 1) + [(0, rows * PACK_COLS - flat.shape[-1])])
    return flat.reshape(flat.shape[:-1] + (rows, PACK_COLS))


def _unpack_rows(buf, sizes):
    flat = buf.reshape(buf.shape[:-2] + (-1,))
    out, off = [], 0
    for n in sizes:
        out.append(flat[..., off:off + n])
        off += _round_up(n, PACK_COLS)
    return out


def _as_bf16_pairs(a):
    return lax.bitcast_convert_type(a, BF16).reshape(-1)


def _from_bf16_pairs(a):
    return lax.bitcast_convert_type(a.reshape(a.shape[:-1] + (-1, 2)), F32)


def gather_weights(local):
    pieces = []
    for name, _, narrow in _SHARDED:
        flat = local[name].reshape(-1)
        pieces.append(flat.astype(BF16) if narrow else _as_bf16_pairs(flat))
    gathered = all_gather(_pack_rows(pieces, 16))
    parts = _unpack_rows(gathered, [p.shape[0] for p in pieces])
    full = {}
    for (name, axis, narrow), part in zip(_SHARDED, parts, strict=True):
        shard_shape = local[name].shape
        vals = part if narrow else _from_bf16_pairs(part)
        vals = jnp.moveaxis(vals.reshape((N_DEV,) + shard_shape), 0, axis)
        full[name] = vals.reshape(shard_shape[:axis] + (N_DEV * shard_shape[axis],) + shard_shape[axis + 1:])
    return full


def _to_slices(g, axis):
    shape = g.shape
    g = g.reshape(shape[:axis] + (N_DEV, shape[axis] // N_DEV) + shape[axis + 1:])
    return jnp.moveaxis(g, axis, 0).reshape(N_DEV, -1)


def kernel(x, meta_tokens, a_norm, a_w_in, a_conv, a_log, a_dt_bias, a_o_gain, a_w_out, kv_norm, kv_w_down, kv_latent_norm, kv_w_uk, kv_w_uv, k_gain, b_norm, b_w_in, b_q_latent_norm, b_w_uq, b_q_gain, b_w_out, loss_target, m_meta_tokens, m_a_norm, m_a_w_in, m_a_conv, m_a_log, m_a_dt_bias, m_a_o_gain, m_a_w_out, m_kv_norm, m_kv_w_down, m_kv_latent_norm, m_kv_w_uk, m_kv_w_uv, m_k_gain, m_b_norm, m_b_w_in, m_b_q_latent_norm, m_b_w_uq, m_b_q_gain, m_b_w_out, v_meta_tokens, v_a_norm, v_a_w_in, v_a_conv, v_a_log, v_a_dt_bias, v_a_o_gain, v_a_w_out, v_kv_norm, v_kv_w_down, v_kv_latent_norm, v_kv_w_uk, v_kv_w_uv, v_k_gain, v_b_norm, v_b_w_in, v_b_q_latent_norm, v_b_w_uq, v_b_q_gain, v_b_w_out):
    given = dict(locals())
    local_w = {n: given[n] for n in _ALL_WEIGHTS}
    full = gather_weights(local_w)
    for n in _REPLICATED:
        full[n] = local_w[n]

    loss_part, grad_x, grads = local_step(x, loss_target, full)
    loss = lax.psum(loss_part[0, 0], ("x", "y", "c"))

    order = [n for n, _, _ in _SHARDED] + list(_REPLICATED)
    pieces = [_to_slices(grads[n], axis) for n, axis, _ in _SHARDED]
    pieces += [jnp.broadcast_to(grads[n].reshape(1, -1), (N_DEV, grads[n].size)) for n in _REPLICATED]
    sizes = [p.shape[1] for p in pieces]
    recv = all_to_all(_pack_rows(pieces, 256))

    def pack_local(prefix):
        return _pack_rows([given[prefix + n].reshape(-1) for n in order], 256)

    packed = adamw_reduce(recv, pack_local(""), pack_local("m_"), pack_local("v_"))
    results = []
    for buf in packed:
        parts = _unpack_rows(buf, sizes)
        by_name = {n: p.reshape(local_w[n].shape) for n, p in zip(order, parts, strict=True)}
        results.extend(by_name[n] for n in _ALL_WEIGHTS)
    return (loss, grad_x, *results)
```

```python
import dataclasses
import functools
import math

import jax
import jax.numpy as jnp
from jax import lax
from jax.experimental import pallas as pl
from jax.experimental.pallas import tpu as pltpu

F32 = jnp.float32
BF16 = jnp.bfloat16
_MXU_DTYPE = jnp.bfloat16
_HI = lax.Precision.HIGHEST

N_DEV = 8
D_MODEL = 1024
N_HEADS = 8
HEAD = 128
CHUNK = 64
N_META = 16
PAD_ROWS = 2 * CHUNK - N_META
LEAD = PAD_ROWS + N_META
ROPE = 64
QK_DIM = HEAD + ROPE
QK_PAD = 2 * HEAD
KV_RANK = 256
Q_RANK = 384
CONV_K = 4
EPS = 1e-6
NEG = -1e30
ROPE_THETA = 10000.0
ADAM_LR, ADAM_B1, ADAM_B2, ADAM_EPS, ADAM_WD, ADAM_STEP = 0.001, 0.9, 0.999, 1e-08, 0.01, 10
PACK_COLS = 512
VMEM_LIMIT = 56 * 1024 * 1024


def _pick(n, options):
    for o in options:
        if n % o == 0:
            return o
    raise ValueError(f"no tile for {n} among {options}")


def _cparams(sem):
    return pltpu.CompilerParams(dimension_semantics=sem, vmem_limit_bytes=VMEM_LIMIT)


def _dot(a, b, dims):
    return lax.dot_general(a.astype(_MXU_DTYPE), b.astype(_MXU_DTYPE), (dims, ((), ())),
                           preferred_element_type=F32)


@jax.custom_vjp
def mm_nn(a, b):
    return _dot(a, b, ((1,), (0,)))


@jax.custom_vjp
def mm_nt(a, b):
    return _dot(a, b, ((1,), (1,)))


@jax.custom_vjp
def mm_tn(a, b):
    return _dot(a, b, ((0,), (0,)))


mm_nn.defvjp(lambda a, b: (mm_nn(a, b), (a, b)), lambda r, g: (mm_nt(g, r[1]), mm_tn(r[0], g)))
mm_nt.defvjp(lambda a, b: (mm_nt(a, b), (a, b)), lambda r, g: (mm_nn(g, r[1]), mm_tn(g, r[0])))
mm_tn.defvjp(lambda a, b: (mm_tn(a, b), (a, b)), lambda r, g: (mm_nt(r[1], g), mm_nn(r[0], g)))


def _dot_f32(a, b):
    return lax.dot_general(a, b, (((1,), (0,)), ((), ())), precision=_HI, preferred_element_type=F32)


def _split_hi_lo(x):
    hi = x.astype(_MXU_DTYPE)
    lo = (x - hi.astype(F32)).astype(_MXU_DTYPE)
    return hi, lo


def _mm_3pass(a, b):
    ah, al = _split_hi_lo(a)
    bh, bl = _split_hi_lo(b)
    d = lambda u, w: lax.dot_general(u, w, (((1,), (0,)), ((), ())), preferred_element_type=F32)
    return d(ah, bh) + (d(ah, bl) + d(al, bh))


def _inv_unit_lower(a):
    n = a.shape[0]
    eye = (lax.broadcasted_iota(jnp.int32, (n, n), 0) == lax.broadcasted_iota(jnp.int32, (n, n), 1)).astype(F32)
    t = eye - a
    p = _mm_3pass(a, a)
    squarings = int(math.log2(n)) - 1
    for s in range(squarings):
        t = t + _mm_3pass(t, p)
        if s + 1 < squarings:
            p = _mm_3pass(p, p)
    return t


@jax.custom_vjp
def _inv_lookup(a, t):
    return t


def _inv_lookup_bwd(t, g):
    return -mm_tn(t, mm_nt(g, t)), jnp.zeros_like(t)


_inv_lookup.defvjp(lambda a, t: (t, t), _inv_lookup_bwd)


def _sigmoid(x):
    return 1.0 / (1.0 + jnp.exp(-x))


def _silu(x):
    return x * _sigmoid(x)


def _softplus(x):
    return jnp.where(x > 20.0, x, jnp.log(1.0 + jnp.exp(jnp.minimum(x, 20.0))))


def _rms(x, g, width=None):
    ms = jnp.sum(x * x, -1, keepdims=True) / (x.shape[-1] if width is None else width)
    return x * lax.rsqrt(ms + EPS) * g


def matmul(name, a, b, mode, out_dtype=F32, res=None):
    if mode == "nn":
        (m, k), (k2, n) = a.shape, b.shape
    elif mode == "nt":
        (m, k), (n, k2) = a.shape, b.shape
    else:
        (k, m), (k2, n) = a.shape, b.shape
    assert k == k2, (name, a.shape, b.shape, mode)
    tm = _pick(m, (512, 384, 256, 128))
    tn = _pick(n, (512, 384, 256, 128))
    tk = _pick(k, (512, 256, 128)) if mode == "tn" else _pick(k, (1024, 512, 384, 256, 128))
    nk = k // tk
    dims = {"nn": ((1,), (0,)), "nt": ((1,), (1,)), "tn": ((0,), (0,))}[mode]

    def body(*refs):
        if res is None:
            a_ref, b_ref, o_ref, acc_ref = refs
        else:
            a_ref, b_ref, r_ref, o_ref, acc_ref = refs
        kk = pl.program_id(2)

        @pl.when(kk == 0)
        def _():
            acc_ref[...] = jnp.zeros_like(acc_ref)

        acc_ref[...] += _dot(a_ref[...], b_ref[...], dims)

        @pl.when(kk == nk - 1)
        def _():
            out = acc_ref[...]
            if res is not None:
                out = out + r_ref[...].astype(F32)
            o_ref[...] = out.astype(o_ref.dtype)

    a_spec = pl.BlockSpec((tk, tm), lambda i, j, kk: (kk, i)) if mode == "tn" else pl.BlockSpec((tm, tk), lambda i, j, kk: (i, kk))
    b_spec = pl.BlockSpec((tn, tk), lambda i, j, kk: (j, kk)) if mode == "nt" else pl.BlockSpec((tk, tn), lambda i, j, kk: (kk, j))
    o_spec = pl.BlockSpec((tm, tn), lambda i, j, kk: (i, j))
    in_specs = [a_spec, b_spec] + ([o_spec] if res is not None else [])
    args = (a, b) + ((res,) if res is not None else ())
    return pl.pallas_call(
        body, grid=(m // tm, n // tn, nk), in_specs=in_specs, out_specs=o_spec,
        out_shape=jax.ShapeDtypeStruct((m, n), out_dtype), scratch_shapes=[pltpu.VMEM((tm, tn), F32)],
        compiler_params=_cparams(("parallel", "parallel", "arbitrary")), name=name)(*args)


@dataclasses.dataclass
class Arg:
    arr: jax.Array
    kind: str = "row"
    bc: int = 0
    base: int = 0
    ph: bool = False
    diff: bool = False


def _arg_spec(a, tr, head_inner, ntab, base=None):
    bc = a.bc or a.arr.shape[1]
    base = a.base if base is None else base

    def imap(g0, g1):
        i, h = (g0, g1) if head_inner else (g1, g0)
        col = base + (h if a.ph else 0)
        if a.kind == "row":
            return (i, col)
        if a.kind == "tab":
            return (i % ntab, col)
        return (0, col)

    rows = tr if a.kind in ("row", "tab") else a.arr.shape[0]
    return pl.BlockSpec((rows, bc), imap)


def _load(ref):
    v = ref[...]
    return v.astype(F32) if jnp.issubdtype(v.dtype, jnp.floating) else v


def row_call(name, fn, args, outs, tr, nh=1, head_inner=True, ntab=1):
    t = args[0].arr.shape[0]
    ni = t // tr
    n_in = len(args)

    def body(*refs):
        res = fn(*[_load(r) for r in refs[:n_in]])
        for r, v in zip(refs[n_in:], res, strict=True):
            r[...] = v.astype(r.dtype)

    grid = (ni, nh) if head_inner else (nh, ni)
    out_specs = [_arg_spec(Arg(None, "row", bc, 0, ph), tr, head_inner, ntab) for (_, _, bc, ph) in outs]
    out_shape = [jax.ShapeDtypeStruct((t, cols), dt) for (cols, dt, _, _) in outs]
    return pl.pallas_call(
        body, grid=grid, in_specs=[_arg_spec(a, tr, head_inner, ntab) for a in args], out_specs=out_specs,
        out_shape=out_shape, compiler_params=_cparams(("arbitrary", "arbitrary")), name=name)(*[a.arr for a in args])


def row_vjp_call(name, fn, args, cts, tr, nh=1, head_inner=True, ntab=1):
    t = args[0].arr.shape[0]
    ni = t // tr
    n_in, n_ct = len(args), len(cts)
    diff_idx = [k for k, a in enumerate(args) if a.diff]
    modes = []
    for k in diff_idx:
        a = args[k]
        if a.kind == "row":
            modes.append("write" if (a.ph or nh == 1) else "acc_heads")
        else:
            modes.append("acc_rows" if a.ph else "acc_all")
    assert not ("acc_heads" in modes and not head_inner) and not ("acc_rows" in modes and head_inner and nh > 1)

    def body(*refs):
        vals = [_load(r) for r in refs[:n_in]]
        ct_vals = tuple(_load(r) for r in refs[n_in:n_in + n_ct])
        out_refs = refs[n_in + n_ct:]
        g0, g1 = pl.program_id(0), pl.program_id(1)
        i, h = (g0, g1) if head_inner else (g1, g0)

        def f(*dv):
            full = list(vals)
            for k, v in zip(diff_idx, dv, strict=True):
                full[k] = v
            return tuple(fn(*full))

        _, vjp = jax.vjp(f, *[vals[k] for k in diff_idx])
        grads = vjp(ct_vals)
        for r, g, mode in zip(out_refs, grads, modes, strict=True):
            if mode == "write":
                r[...] = g.astype(r.dtype)
            else:
                first = {"acc_heads": h == 0, "acc_rows": i == 0, "acc_all": (i == 0) & (h == 0)}[mode]

                @pl.when(first)
                def _(r=r):
                    r[...] = jnp.zeros_like(r)

                r[...] += g

    grid = (ni, nh) if head_inner else (nh, ni)
    out_specs, out_shape = [], []
    for k in diff_idx:
        a = args[k]
        bc = a.bc or a.arr.shape[1]
        out_specs.append(_arg_spec(a, tr, head_inner, ntab, base=0))
        out_shape.append(jax.ShapeDtypeStruct((t if a.kind == "row" else a.arr.shape[0], bc * (nh if a.ph else 1)), F32))
    in_specs = [_arg_spec(a, tr, head_inner, ntab) for a in list(args) + list(cts)]
    return pl.pallas_call(
        body, grid=grid, in_specs=in_specs, out_specs=out_specs, out_shape=out_shape,
        compiler_params=_cparams(("arbitrary", "arbitrary")), name=name)(*[a.arr for a in list(args) + list(cts)])


def _conv_taps(x, w):
    rows = lax.broadcasted_iota(jnp.int32, x.shape, 0)
    y = x * w[CONV_K - 1:CONV_K, :]
    shifted = []
    for s in range(1, CONV_K):
        xs = jnp.where(rows >= s, pltpu.roll(x, s, 0), 0.0)
        shifted.append(xs)
        y = y + xs * w[CONV_K - 1 - s:CONV_K - s, :]
    return y, shifted


def _conv_post(y, mode):
    a = _silu(y)
    if mode == "v":
        return a
    out = a * lax.rsqrt(jnp.sum(a * a, -1, keepdims=True) + EPS)
    return out * (HEAD ** -0.5) if mode == "q" else out


def conv_fwd(name, z, w, mode, lp):
    t = z.shape[0]
    base = {"q": 0, "k": N_HEADS, "v": 2 * N_HEADS}[mode]

    def body(z_ref, w_ref, o_ref):
        y, _ = _conv_taps(z_ref[...], w_ref[...])
        o_ref[...] = _conv_post(y, mode)

    return pl.pallas_call(
        body, grid=(t // lp, N_HEADS),
        in_specs=[pl.BlockSpec((lp, HEAD), lambda b, h: (b, base + h)), pl.BlockSpec((CONV_K, HEAD), lambda b, h: (0, base + h))],
        out_specs=pl.BlockSpec((lp, HEAD), lambda b, h: (b, h)), out_shape=jax.ShapeDtypeStruct((t, N_HEADS * HEAD), F32),
        compiler_params=_cparams(("arbitrary", "arbitrary")), name=name)(z, w)


def conv_bwd(name, z, w, dout, mode, lp):
    t = z.shape[0]
    base = {"q": 0, "k": N_HEADS, "v": 2 * N_HEADS}[mode]

    def body(z_ref, w_ref, g_ref, dz_ref, dw_ref):
        x, wv = z_ref[...], w_ref[...]
        y, shifted = _conv_taps(x, wv)
        _, vjp = jax.vjp(lambda y_: _conv_post(y_, mode), y)
        (dy,) = vjp(g_ref[...])
        rows = lax.broadcasted_iota(jnp.int32, x.shape, 0)
        dx = dy * wv[CONV_K - 1:CONV_K, :]
        for s in range(1, CONV_K):
            dx = dx + jnp.where(rows < lp - s, pltpu.roll(dy, lp - s, 0), 0.0) * wv[CONV_K - 1 - s:CONV_K - s, :]
        dz_ref[...] = dx

        @pl.when(pl.program_id(1) == 0)
        def _():
            dw_ref[...] = jnp.zeros_like(dw_ref)

        dw_ref[CONV_K - 1:CONV_K, :] += jnp.sum(dy * x, axis=0, keepdims=True)
        for s in range(1, CONV_K):
            dw_ref[CONV_K - 1 - s:CONV_K - s, :] += jnp.sum(dy * shifted[s - 1], axis=0, keepdims=True)

    return pl.pallas_call(
        body, grid=(N_HEADS, t // lp),
        in_specs=[pl.BlockSpec((lp, HEAD), lambda h, b: (b, base + h)), pl.BlockSpec((CONV_K, HEAD), lambda h, b: (0, base + h)),
                  pl.BlockSpec((lp, HEAD), lambda h, b: (b, h))],
        out_specs=[pl.BlockSpec((lp, HEAD), lambda h, b: (b, h)), pl.BlockSpec((CONV_K, HEAD), lambda h, b: (0, h))],
        out_shape=[jax.ShapeDtypeStruct((t, N_HEADS * HEAD), F32), jax.ShapeDtypeStruct((CONV_K, N_HEADS * HEAD), F32)],
        compiler_params=_cparams(("arbitrary", "arbitrary")), name=name)(z, w, dout)


def _delta_chunk(q, k, v, ba, alog, dtb, state, t_stored, h):
    c = q.shape[0]
    lane = lax.broadcasted_iota(jnp.int32, (1, HEAD), 1)
    sel_b = (lane == h).astype(F32)
    sel_a = (lane == N_HEADS + h).astype(F32)
    b_raw = jnp.sum(ba * sel_b, axis=1, keepdims=True)
    a_raw = jnp.sum(ba * sel_a, axis=1, keepdims=True)
    a_log = jnp.sum(alog * sel_b, axis=1, keepdims=True)
    dt_bias = jnp.sum(dtb * sel_b, axis=1, keepdims=True)
    beta = _sigmoid(b_raw)
    g = -jnp.exp(a_log) * _softplus(a_raw + dt_bias)
    ri = lax.broadcasted_iota(jnp.int32, (c, c), 0)
    ci = lax.broadcasted_iota(jnp.int32, (c, c), 1)
    tril = ci <= ri
    gc_col = _dot_f32(tril.astype(F32), g * jnp.ones((1, HEAD), F32))[:, :1]
    gc_row = _dot_f32(jnp.ones((8, c), F32), g * (ri <= ci).astype(F32))[0:1, :]
    gc_last = jnp.sum(g, axis=0, keepdims=True)
    decay = jnp.exp(jnp.where(tril, gc_col - gc_row, NEG))
    e_gc = jnp.exp(gc_col)
    kb = k * beta
    a_mat = jnp.where(ci < ri, mm_nt(kb, k) * decay, 0.0)
    t_inv = _inv_unit_lower(a_mat) if t_stored is None else _inv_lookup(a_mat, t_stored)
    u_base = mm_nn(t_inv, v * beta)
    w_dec = mm_nn(t_inv, kb * e_gc)
    attn = jnp.where(tril, mm_nt(q, k) * decay, 0.0)
    u = u_base - mm_nn(w_dec, state)
    o = mm_nn(q * e_gc, state) + mm_nn(attn, u)
    new_state = state * jnp.exp(gc_last) + mm_tn(k * jnp.exp(gc_last - gc_col), u)
    return o, new_state, t_inv


DELTA_HEADS = 4
DELTA_CHUNKS_FWD = 2
DELTA_CHUNKS_BWD = 1


def delta_fwd(q, k, v, ba, alog, dtb, lp):
    t = q.shape[0]
    nb, nc = t // lp, lp // CHUNK
    hg, cps = DELTA_HEADS, DELTA_CHUNKS_FWD
    ng, rows = nc // cps, cps * CHUNK
    assert nc % cps == 0 and N_HEADS % hg == 0

    def body(q_ref, k_ref, v_ref, ba_ref, al_ref, dt_ref, o_ref, s_ref, t_ref, state_ref):
        group = pl.program_id(1)

        @pl.when(pl.program_id(2) == 0)
        def _():
            state_ref[...] = jnp.zeros_like(state_ref)

        al, dtv = al_ref[...], dt_ref[...]
        for c in range(cps):
            rs = slice(c * CHUNK, (c + 1) * CHUNK)
            ba_c = ba_ref[rs, :]
            for g in range(hg):
                cs = slice(g * HEAD, (g + 1) * HEAD)
                state = state_ref[g]
                o, new_state, t_inv = _delta_chunk(q_ref[rs, cs], k_ref[rs, cs], v_ref[rs, cs], ba_c, al, dtv, state, None, group * hg + g)
                o_ref[rs, cs] = o
                s_ref[g, c] = state
                t_ref[g, c] = t_inv
                state_ref[g] = new_state

    head_spec = pl.BlockSpec((rows, hg * HEAD), lambda b, h, n: (b * ng + n, h))
    par_spec = pl.BlockSpec((1, HEAD), lambda b, h, n: (0, 0))
    return pl.pallas_call(
        body, grid=(nb, N_HEADS // hg, ng),
        in_specs=[head_spec, head_spec, head_spec, pl.BlockSpec((rows, HEAD), lambda b, h, n: (b * ng + n, 0)), par_spec, par_spec],
        out_specs=[head_spec, pl.BlockSpec((None, hg, cps, HEAD, HEAD), lambda b, h, n: (b, h, n, 0, 0)),
                   pl.BlockSpec((None, hg, cps, CHUNK, CHUNK), lambda b, h, n: (b, h, n, 0, 0))],
        out_shape=[jax.ShapeDtypeStruct((t, N_HEADS * HEAD), F32), jax.ShapeDtypeStruct((nb, N_HEADS, nc, HEAD, HEAD), F32),
                   jax.ShapeDtypeStruct((nb, N_HEADS, nc, CHUNK, CHUNK), F32)],
        scratch_shapes=[pltpu.VMEM((hg, HEAD, HEAD), F32)],
        compiler_params=_cparams(("arbitrary", "arbitrary", "arbitrary")), name="delta_fwd")(q, k, v, ba, alog, dtb)


def delta_bwd(q, k, v, ba, alog, dtb, states, t_invs, do, lp):
    t = q.shape[0]
    nb, nc = t // lp, lp // CHUNK
    hg, cps = DELTA_HEADS, DELTA_CHUNKS_BWD
    ng, rows = nc // cps, cps * CHUNK
    n_groups = N_HEADS // hg

    def body(q_ref, k_ref, v_ref, ba_ref, al_ref, dt_ref, s_ref, t_ref, do_ref, dq_ref, dk_ref, dv_ref, dba_ref, dal_ref, ddt_ref, dstate_ref):
        b, group, step = pl.program_id(0), pl.program_id(1), pl.program_id(2)

        @pl.when(step == 0)
        def _():
            dstate_ref[...] = jnp.zeros_like(dstate_ref)

        @pl.when((b == 0) & (group == 0) & (step == 0))
        def _():
            dal_ref[...] = jnp.zeros_like(dal_ref)
            ddt_ref[...] = jnp.zeros_like(ddt_ref)

        al, dtv = al_ref[...], dt_ref[...]
        d_al = jnp.zeros((1, HEAD), F32)
        d_dt = jnp.zeros((1, HEAD), F32)
        for c in reversed(range(cps)):
            rs = slice(c * CHUNK, (c + 1) * CHUNK)
            ba_c = ba_ref[rs, :]
            d_ba = jnp.zeros((CHUNK, HEAD), F32)
            for g in range(hg):
                cs = slice(g * HEAD, (g + 1) * HEAD)
                t_n = t_ref[g, c]
                h = group * hg + g

                def f(q_, k_, v_, ba_, al_, dt_, s_, t_n=t_n, h=h):
                    return _delta_chunk(q_, k_, v_, ba_, al_, dt_, s_, t_n, h)[:2]

                _, vjp = jax.vjp(f, q_ref[rs, cs], k_ref[rs, cs], v_ref[rs, cs], ba_c, al, dtv, s_ref[g, c])
                gq, gk, gv, gba, gal, gdt, gs = vjp((do_ref[rs, cs], dstate_ref[g]))
                dq_ref[rs, cs] = gq
                dk_ref[rs, cs] = gk
                dv_ref[rs, cs] = gv
                dstate_ref[g] = gs
                d_ba, d_al, d_dt = d_ba + gba, d_al + gal, d_dt + gdt
            dba_ref[rs, :] = d_ba
        dal_ref[...] += d_al
        ddt_ref[...] += d_dt

    head_spec = pl.BlockSpec((rows, hg * HEAD), lambda b, h, n: (b * ng + ng - 1 - n, h))
    par_spec = pl.BlockSpec((1, HEAD), lambda b, h, n: (0, 0))
    big = jax.ShapeDtypeStruct((t, N_HEADS * HEAD), F32)
    return pl.pallas_call(
        body, grid=(nb, n_groups, ng),
        in_specs=[head_spec, head_spec, head_spec, pl.BlockSpec((rows, HEAD), lambda b, h, n: (b * ng + ng - 1 - n, 0)), par_spec, par_spec,
                  pl.BlockSpec((None, hg, cps, HEAD, HEAD), lambda b, h, n: (b, h, ng - 1 - n, 0, 0)),
                  pl.BlockSpec((None, hg, cps, CHUNK, CHUNK), lambda b, h, n: (b, h, ng - 1 - n, 0, 0)), head_spec],
        out_specs=[head_spec, head_spec, head_spec, pl.BlockSpec((rows, HEAD), lambda b, h, n: (b * ng + ng - 1 - n, h)), par_spec, par_spec],
        out_shape=[big, big, big, jax.ShapeDtypeStruct((t, n_groups * HEAD), F32), jax.ShapeDtypeStruct((1, HEAD), F32), jax.ShapeDtypeStruct((1, HEAD), F32)],
        scratch_shapes=[pltpu.VMEM((hg, HEAD, HEAD), F32)],
        compiler_params=_cparams(("arbitrary", "arbitrary", "arbitrary")), name="delta_bwd")(q, k, v, ba, alog, dtb, states, t_invs, do)


ATT_Q_TILE = 256
ATT_K_TILE = 512
ATT_SCALE = QK_DIM ** -0.5


def _tiles(end, size):
    return [(s, min(s + size, end)) for s in range(0, end, size)]


def _att_visible(q0, q1, k0, k1, keys_first):
    if k1 <= q0 + CHUNK and k0 >= PAD_ROWS:
        return None
    shape = (k1 - k0, q1 - q0) if keys_first else (q1 - q0, k1 - k0)
    qpos = q0 + lax.broadcasted_iota(jnp.int32, shape, 1 if keys_first else 0)
    kpos = k0 + lax.broadcasted_iota(jnp.int32, shape, 0 if keys_first else 1)
    shift = CHUNK.bit_length() - 1
    return (jnp.right_shift(kpos, shift) <= jnp.right_shift(qpos, shift)) & (kpos >= PAD_ROWS)


def _att_seq_specs(lp):
    return pl.BlockSpec((lp, QK_PAD), lambda b, h: (b, h)), pl.BlockSpec((lp, HEAD), lambda b, h: (b, h))


def flash_fwd(q, k, v, lp):
    t = q.shape[0]
    qk_seq, o_seq = _att_seq_specs(lp)

    def body(q_ref, k_ref, v_ref, o_ref, lse_ref):
        for q0, q1 in _tiles(lp, ATT_Q_TILE):
            qb = q_ref[q0:q1, :]
            k_tiles = _tiles(q1, ATT_K_TILE)
            scores, m = [], None
            for k0, k1 in k_tiles:
                s = mm_nt(qb, k_ref[k0:k1, :]) * ATT_SCALE
                vis = _att_visible(q0, q1, k0, k1, False)
                s = s if vis is None else jnp.where(vis, s, NEG)
                scores.append(s)
                row_max = jnp.max(s, -1, keepdims=True)
                m = row_max if m is None else jnp.maximum(m, row_max)
            l = jnp.zeros((q1 - q0, 1), F32)
            acc = jnp.zeros((q1 - q0, HEAD), F32)
            for s, (k0, k1) in zip(scores, k_tiles, strict=True):
                p = jnp.exp(s - m)
                l = l + jnp.sum(p, -1, keepdims=True)
                acc = acc + mm_nn(p, v_ref[k0:k1, :])
            o_ref[q0:q1, :] = acc / l
            lse_ref[q0:q1, :] = jnp.broadcast_to(m + jnp.log(l), (q1 - q0, HEAD))

    big = jax.ShapeDtypeStruct((t, N_HEADS * HEAD), F32)
    return pl.pallas_call(
        body, grid=(t // lp, N_HEADS), in_specs=[qk_seq, qk_seq, o_seq], out_specs=[o_seq, o_seq], out_shape=[big, big],
        compiler_params=_cparams(("arbitrary", "arbitrary")), name="flash_fwd")(q, k, v)


def flash_bwd(q, k, v, o, lse, do, lp):
    t = q.shape[0]
    qk_seq, o_seq = _att_seq_specs(lp)

    def body(q_ref, k_ref, v_ref, o_ref, lse_ref, do_ref, dq_ref, dk_ref, dv_ref):
        dk_ref[...] = jnp.zeros_like(dk_ref)
        dv_ref[...] = jnp.zeros_like(dv_ref)
        for q0, q1 in _tiles(lp, ATT_Q_TILE):
            qb, dob = q_ref[q0:q1, :], do_ref[q0:q1, :]
            lse_row = jnp.transpose(lse_ref[q0:q1, :])[0:1, :]
            dsum_row = jnp.sum(jnp.transpose(dob * o_ref[q0:q1, :]), axis=0, keepdims=True)
            dq = jnp.zeros((q1 - q0, QK_PAD), F32)
            for k0, k1 in _tiles(q1, ATT_K_TILE):
                kb, vb = k_ref[k0:k1, :], v_ref[k0:k1, :]
                s = mm_nt(kb, qb) * ATT_SCALE
                vis = _att_visible(q0, q1, k0, k1, True)
                s = s if vis is None else jnp.where(vis, s, NEG)
                p = jnp.exp(s - lse_row)
                ds = p * (mm_nt(vb, dob) - dsum_row) * ATT_SCALE
                dv_ref[k0:k1, :] += mm_nn(p, dob)
                dk_ref[k0:k1, :] += mm_nn(ds, qb)
                dq = dq + mm_tn(ds, kb)
            dq_ref[q0:q1, :] = dq

    return pl.pallas_call(
        body, grid=(t // lp, N_HEADS), in_specs=[qk_seq, qk_seq, o_seq, o_seq, o_seq, o_seq], out_specs=[qk_seq, qk_seq, o_seq],
        out_shape=[jax.ShapeDtypeStruct((t, N_HEADS * QK_PAD), F32), jax.ShapeDtypeStruct((t, N_HEADS * QK_PAD), F32),
                   jax.ShapeDtypeStruct((t, N_HEADS * HEAD), F32)],
        compiler_params=_cparams(("arbitrary", "arbitrary")), name="flash_bwd")(q, k, v, o, lse, do)


def loss_head(h2, target, lp):
    nb, seq, d = target.shape
    tr = 128
    nblk = lp // tr
    lead_blocks = LEAD // tr

    def body(h_ref, t_ref, loss_ref, dh_ref, acc_ref):
        b, i = pl.program_id(0), pl.program_id(1)

        @pl.when((b == 0) & (i == 0))
        def _():
            acc_ref[...] = jnp.zeros_like(acc_ref)

        @pl.when(i < lead_blocks)
        def _():
            dh_ref[...] = jnp.zeros_like(dh_ref)

        @pl.when(i >= lead_blocks)
        def _():
            err = h_ref[...] - t_ref[...]
            dh_ref[...] = err * (1.0 / d)
            acc_ref[...] += jnp.sum(err * err, axis=0, keepdims=True)

        @pl.when((b == nb - 1) & (i == nblk - 1))
        def _():
            loss_ref[...] = jnp.sum(acc_ref[...], axis=1, keepdims=True) * (0.5 / d)

    return pl.pallas_call(
        body, grid=(nb, nblk),
        in_specs=[pl.BlockSpec((None, tr, d), lambda b, i: (b, i, 0)),
                  pl.BlockSpec((None, tr, d), lambda b, i: (b, jnp.maximum(i - lead_blocks, 0), 0))],
        out_specs=[pl.BlockSpec((1, 1), lambda b, i: (0, 0)), pl.BlockSpec((None, tr, d), lambda b, i: (b, i, 0))],
        out_shape=[jax.ShapeDtypeStruct((1, 1), F32), jax.ShapeDtypeStruct((nb, lp, d), F32)],
        scratch_shapes=[pltpu.VMEM((1, d), F32)], compiler_params=_cparams(("arbitrary", "arbitrary")), name="loss_head")(h2, target)


def meta_grad(dh0):
    nb, _, d = dh0.shape

    def body(g_ref, o_ref):
        @pl.when(pl.program_id(0) == 0)
        def _():
            o_ref[...] = jnp.zeros_like(o_ref)

        o_ref[...] += g_ref[PAD_ROWS:LEAD, :]

    return pl.pallas_call(
        body, grid=(nb,), in_specs=[pl.BlockSpec((None, LEAD, d), lambda b: (b, 0, 0))],
        out_specs=pl.BlockSpec((N_META, d), lambda b: (0, 0)), out_shape=jax.ShapeDtypeStruct((N_META, d), F32),
        compiler_params=_cparams(("arbitrary",)), name="meta_grad")(dh0)


_HBM = pl.BlockSpec(memory_space=pltpu.HBM)


def _mesh_pos():
    x, y, c = lax.axis_index("x"), lax.axis_index("y"), lax.axis_index("c")
    return x, y, c


def _peer(x, y, c, k):
    px = 1 - x if k & 4 else x
    py = 1 - y if k & 2 else y
    pc = 1 - c if k & 1 else c
    return (px, py, pc), 4 * px + 2 * py + pc


def all_gather(buf):
    def body(x_ref, out_ref, send_sems, recv_sems, local_sem):
        x, y, c = _mesh_pos()
        me = 4 * x + 2 * y + c
        mine = pltpu.make_async_copy(x_ref, out_ref.at[me], local_sem)
        mine.start()
        sends = []
        for k in range(1, N_DEV):
            peer, _ = _peer(x, y, c, k)
            cp = pltpu.make_async_remote_copy(src_ref=x_ref, dst_ref=out_ref.at[me], send_sem=send_sems.at[k - 1],
                                              recv_sem=recv_sems.at[k - 1], device_id=peer, device_id_type=pl.DeviceIdType.MESH)
            cp.start()
            sends.append(cp)
        for k in range(1, N_DEV):
            peer, peer_id = _peer(x, y, c, k)
            pltpu.make_async_remote_copy(src_ref=x_ref, dst_ref=out_ref.at[peer_id], send_sem=send_sems.at[k - 1],
                                         recv_sem=recv_sems.at[k - 1], device_id=peer, device_id_type=pl.DeviceIdType.MESH).wait_recv()
        for cp in sends:
            cp.wait_send()
        mine.wait()

    return pl.pallas_call(
        body, in_specs=[_HBM], out_specs=_HBM, out_shape=jax.ShapeDtypeStruct((N_DEV,) + buf.shape, buf.dtype),
        scratch_shapes=[pltpu.SemaphoreType.DMA((N_DEV - 1,)), pltpu.SemaphoreType.DMA((N_DEV - 1,)), pltpu.SemaphoreType.DMA],
        name="all_gather")(buf)


def all_to_all(buf):
    def body(x_ref, out_ref, send_sems, recv_sems, local_sem):
        x, y, c = _mesh_pos()
        me = 4 * x + 2 * y + c
        mine = pltpu.make_async_copy(x_ref.at[me], out_ref.at[me], local_sem)
        mine.start()
        sends = []
        for k in range(1, N_DEV):
            peer, peer_id = _peer(x, y, c, k)
            cp = pltpu.make_async_remote_copy(src_ref=x_ref.at[peer_id], dst_ref=out_ref.at[me], send_sem=send_sems.at[k - 1],
                                              recv_sem=recv_sems.at[k - 1], device_id=peer, device_id_type=pl.DeviceIdType.MESH)
            cp.start()
            sends.append(cp)
        for k in range(1, N_DEV):
            peer, peer_id = _peer(x, y, c, k)
            pltpu.make_async_remote_copy(src_ref=x_ref.at[peer_id], dst_ref=out_ref.at[peer_id], send_sem=send_sems.at[k - 1],
                                         recv_sem=recv_sems.at[k - 1], device_id=peer, device_id_type=pl.DeviceIdType.MESH).wait_recv()
        for cp in sends:
            cp.wait_send()
        mine.wait()

    return pl.pallas_call(
        body, in_specs=[_HBM], out_specs=_HBM, out_shape=jax.ShapeDtypeStruct(buf.shape, buf.dtype),
        scratch_shapes=[pltpu.SemaphoreType.DMA((N_DEV - 1,)), pltpu.SemaphoreType.DMA((N_DEV - 1,)), pltpu.SemaphoreType.DMA],
        name="all_to_all")(buf)


def _f_rms(x, g):
    return (_rms(x, g),)


def _f_rms2(x, g1, g2):
    r = x * lax.rsqrt(jnp.sum(x * x, -1, keepdims=True) / x.shape[-1] + EPS)
    return r * g1, r * g2


def _f_out_gate(o, gate, gain):
    return (_rms(o, gain) * _silu(gate),)


def _f_gate(o, gate):
    return (o * _silu(gate),)


def _f_qk_final(nope, rope_in, g_nope, g_rope, cos, sin, perm):
    ms = (jnp.sum(nope * nope, -1, keepdims=True) + jnp.sum(rope_in * rope_in, -1, keepdims=True)) / QK_DIM
    r = lax.rsqrt(ms + EPS)
    a = nope * r * g_nope
    b = rope_in * r * g_rope
    return (jnp.concatenate([a, b * cos + _dot_f32(b, perm) * sin], axis=1),)


def _rope_tables(lp):
    half = ROPE // 2
    pos = jnp.maximum(jnp.arange(lp) - PAD_ROWS, 0)
    inv = ROPE_THETA ** (-jnp.arange(half, dtype=F32) / half)
    ang = pos.astype(F32)[:, None] * inv[None, :]
    zeros = jnp.zeros((lp, HEAD - ROPE), F32)
    cos = jnp.concatenate([jnp.cos(ang), jnp.cos(ang), zeros], 1)
    sin = jnp.concatenate([-jnp.sin(ang), jnp.sin(ang), zeros], 1)
    src = jnp.arange(HEAD)[:, None]
    dst = jnp.arange(HEAD)[None, :]
    perm = (((dst < half) & (src == dst + half)) | ((dst >= half) & (dst < ROPE) & (src == dst - half))).astype(F32)
    return cos, sin, perm


def _pad_lanes(w, width=HEAD):
    return jnp.pad(w, ((0, 0), (0, width - w.shape[1])))


def _pad_rows(w, rows=HEAD):
    return jnp.pad(w, ((0, rows - w.shape[0]), (0, 0)))


def _split_heads_qk_t(w_t):
    k = w_t.shape[1]
    w3 = w_t.reshape(N_HEADS, QK_DIM, k)
    nope = w3[:, :HEAD].reshape(N_HEADS * HEAD, k)
    rope = jnp.pad(w3[:, HEAD:], ((0, 0), (0, HEAD - ROPE), (0, 0))).reshape(N_HEADS * HEAD, k)
    return jnp.concatenate([nope, rope], 0)


def _merge_heads_qk_t(g_t):
    k = g_t.shape[1]
    kw = N_HEADS * HEAD
    nope, rope = g_t[:kw].reshape(N_HEADS, HEAD, k), g_t[kw:].reshape(N_HEADS, HEAD, k)[:, :ROPE]
    return jnp.concatenate([nope, rope], 1).reshape(N_HEADS * QK_DIM, k)


def local_step(x, target, w):
    nb, seq, d = x.shape
    lp = seq + LEAD
    t = nb * lp
    tr = _pick(lp, (544, 128))
    ntab = lp // tr
    mxu = _MXU_DTYPE
    kw = N_HEADS * HEAD

    a_w_in_t = w["a_w_in"].astype(mxu)
    w_qkv_t, w_ga_t, w_ba_t = a_w_in_t[:3 * kw], a_w_in_t[3 * kw:4 * kw], _pad_rows(a_w_in_t[4 * kw:])
    a_conv = w["a_conv"].T
    a_w_out = w["a_w_out"].astype(mxu)
    alog, dtb, o_gain = _pad_lanes(w["a_log"]), _pad_lanes(w["a_dt_bias"]), w["a_o_gain"]
    w_dkv, w_dpe = w["kv_w_down"][:, :KV_RANK].astype(mxu), _pad_lanes(w["kv_w_down"][:, KV_RANK:]).astype(mxu)
    w_ukv_t = jnp.concatenate([w["kv_w_uk"], w["kv_w_uv"]], 0).astype(mxu)
    b_w_in_t = w["b_w_in"].astype(mxu)
    w_cq_t, w_gb_t = b_w_in_t[:Q_RANK], b_w_in_t[Q_RANK:]
    w_q_t = _split_heads_qk_t(w["b_w_uq"]).astype(mxu)
    b_w_out = w["b_w_out"].astype(mxu)
    a_norm, kv_norm, b_norm = w["a_norm"], w["kv_norm"][None, :], w["b_norm"]
    lat_norm, qlat_norm = w["kv_latent_norm"][None, :], w["b_q_latent_norm"]
    kg_nope, kg_rope = w["k_gain"][None, :HEAD], _pad_lanes(w["k_gain"][None, HEAD:])
    qg_nope, qg_rope = w["b_q_gain"][:, :HEAD], _pad_lanes(w["b_q_gain"][:, HEAD:])
    cos, sin, perm = _rope_tables(lp)

    meta = jnp.broadcast_to(w["meta_tokens"].T[None], (nb, N_META, d))
    h0 = jnp.concatenate([jnp.zeros((nb, PAD_ROWS, d), F32), meta, x], 1).reshape(t, d)
    (hn,) = row_call("a_norm_fwd", _f_rms, [Arg(h0), Arg(a_norm, "par")], [(d, mxu, d, False)], tr)
    z_qkv = matmul("a_in_qkv", hn, w_qkv_t, "nt")
    gate_a = matmul("a_in_gate", hn, w_ga_t, "nt")
    z_ba = matmul("a_in_ba", hn, w_ba_t, "nt")
    qa = conv_fwd("a_conv_q", z_qkv, a_conv, "q", lp)
    ka = conv_fwd("a_conv_k", z_qkv, a_conv, "k", lp)
    va = conv_fwd("a_conv_v", z_qkv, a_conv, "v", lp)
    o_a, states, t_invs = delta_fwd(qa, ka, va, z_ba, alog, dtb, lp)
    og_args = [Arg(o_a, bc=HEAD, ph=True, diff=True), Arg(gate_a, bc=HEAD, ph=True, diff=True), Arg(o_gain, "par", diff=True)]
    (og_a,) = row_call("a_out_gate_fwd", _f_out_gate, og_args, [(kw, mxu, HEAD, True)], tr, nh=N_HEADS)
    h1 = matmul("a_out", og_a, a_w_out, "nn", res=h0)

    hk, hb = row_call("b_norms_fwd", _f_rms2, [Arg(h1), Arg(kv_norm, "par"), Arg(b_norm, "par")], [(d, mxu, d, False), (d, mxu, d, False)], tr)
    c_kv_raw = matmul("kv_down", hk, w_dkv, "nn")
    k_pe = matmul("kv_down_pe", hk, w_dpe, "nn")
    c_q_raw = matmul("b_in_q", hb, w_cq_t, "nt")
    gate_b = matmul("b_in_gate", hb, w_gb_t, "nt")
    (c_kv,) = row_call("kv_latent_fwd", _f_rms, [Arg(c_kv_raw), Arg(lat_norm, "par")], [(KV_RANK, mxu, KV_RANK, False)], tr)
    (c_q,) = row_call("q_latent_fwd", _f_rms, [Arg(c_q_raw), Arg(qlat_norm, "par")], [(Q_RANK, mxu, Q_RANK, False)], tr)
    kv_up = matmul("kv_up", c_kv, w_ukv_t, "nt")
    q_up = matmul("q_up", c_q, w_q_t, "nt")
    tabs = [Arg(cos, "tab"), Arg(sin, "tab"), Arg(perm, "par")]
    k_args = [Arg(kv_up, bc=HEAD, ph=True, diff=True), Arg(k_pe, diff=True), Arg(kg_nope, "par", diff=True), Arg(kg_rope, "par", diff=True)] + tabs
    q_args = [Arg(q_up, bc=HEAD, ph=True, diff=True), Arg(q_up, bc=HEAD, base=N_HEADS, ph=True, diff=True),
              Arg(qg_nope, "par", diff=True), Arg(qg_rope, "par", diff=True)] + tabs
    (k_fin,) = row_call("k_final_fwd", _f_qk_final, k_args, [(N_HEADS * QK_PAD, mxu, QK_PAD, True)], tr, nh=N_HEADS, ntab=ntab)
    (q_fin,) = row_call("q_final_fwd", _f_qk_final, q_args, [(N_HEADS * QK_PAD, mxu, QK_PAD, True)], tr, nh=N_HEADS, ntab=ntab)
    v_b = kv_up[:, kw:].astype(mxu)
    o_b, lse = flash_fwd(q_fin, k_fin, v_b, lp)
    gb_args = [Arg(o_b, diff=True), Arg(gate_b, diff=True)]
    (og_b,) = row_call("b_gate_fwd", _f_gate, gb_args, [(kw, mxu, kw, False)], tr)
    h2 = matmul("b_out", og_b, b_w_out, "nn", res=h1)

    loss, dh2 = loss_head(h2.reshape(nb, lp, d), target, lp)
    dh2 = dh2.reshape(t, d)
    grads = {}

    d_og_b = matmul("b_out_dx", dh2, b_w_out, "nt")
    grads["b_w_out"] = matmul("b_out_dw", og_b, dh2, "tn")
    d_o_b, d_gate_b = row_vjp_call("b_gate_bwd", _f_gate, gb_args, [Arg(d_og_b)], tr)
    dq_fin, dk_fin, dv_b = flash_bwd(q_fin, k_fin, v_b, o_b, lse, d_o_b, lp)
    dq_nope, dq_rope, d_qg_nope, d_qg_rope = row_vjp_call(
        "q_final_bwd", _f_qk_final, q_args, [Arg(dq_fin, bc=QK_PAD, ph=True)], tr, nh=N_HEADS, ntab=ntab)
    dk_nope, dk_pe, d_kg_nope, d_kg_rope = row_vjp_call(
        "k_final_bwd", _f_qk_final, k_args, [Arg(dk_fin, bc=QK_PAD, ph=True)], tr, nh=N_HEADS, ntab=ntab)
    grads["b_q_gain"] = jnp.concatenate([d_qg_nope, d_qg_rope[:, :ROPE]], 1)
    grads["k_gain"] = jnp.concatenate([d_kg_nope, d_kg_rope[:, :ROPE]], 1)[0]
    dq_up = jnp.concatenate([dq_nope, dq_rope], 1)
    dkv_up = jnp.concatenate([dk_nope, dv_b], 1)
    d_c_q = matmul("q_up_dx", dq_up, w_q_t, "nn")
    grads["b_w_uq"] = _merge_heads_qk_t(matmul("q_up_dw", dq_up, c_q, "tn"))
    d_c_kv = matmul("kv_up_dx", dkv_up, w_ukv_t, "nn")
    d_w_ukv_t = matmul("kv_up_dw", dkv_up, c_kv, "tn")
    grads["kv_w_uk"], grads["kv_w_uv"] = d_w_ukv_t[:kw], d_w_ukv_t[kw:]
    d_c_q_raw, grads["b_q_latent_norm"] = row_vjp_call(
        "q_latent_bwd", _f_rms, [Arg(c_q_raw, diff=True), Arg(qlat_norm, "par", diff=True)], [Arg(d_c_q)], tr)
    d_c_kv_raw, d_lat = row_vjp_call(
        "kv_latent_bwd", _f_rms, [Arg(c_kv_raw, diff=True), Arg(lat_norm, "par", diff=True)], [Arg(d_c_kv)], tr)
    grads["kv_latent_norm"] = d_lat[0]
    d_hb = matmul("b_in_q_dx", d_c_q_raw, w_cq_t, "nn")
    d_hb = matmul("b_in_gate_dx", d_gate_b, w_gb_t, "nn", res=d_hb)
    grads["b_w_in"] = jnp.concatenate([matmul("b_in_q_dw", d_c_q_raw, hb, "tn"), matmul("b_in_gate_dw", d_gate_b, hb, "tn")], 0)
    d_hk = matmul("kv_down_dx", d_c_kv_raw, w_dkv, "nt")
    d_hk = matmul("kv_down_pe_dx", dk_pe, w_dpe, "nt", res=d_hk)
    grads["kv_w_down"] = jnp.concatenate([matmul("kv_down_dw", hk, d_c_kv_raw, "tn"), matmul("kv_down_pe_dw", hk, dk_pe, "tn")[:, :ROPE]], 1)
    d_h1_norms, d_kv_norm, grads["b_norm"] = row_vjp_call(
        "b_norms_bwd", _f_rms2, [Arg(h1, diff=True), Arg(kv_norm, "par", diff=True), Arg(b_norm, "par", diff=True)], [Arg(d_hk), Arg(d_hb)], tr)
    grads["kv_norm"] = d_kv_norm[0]

    (dh1,) = row_call("dh1_sum", lambda a, b: (a + b,), [Arg(dh2), Arg(d_h1_norms)], [(d, F32, d, False)], tr)
    d_og_a = matmul("a_out_dx", dh1, a_w_out, "nt")
    grads["a_w_out"] = matmul("a_out_dw", og_a, dh1, "tn")
    d_o_a, d_gate_a, grads["a_o_gain"] = row_vjp_call(
        "a_out_gate_bwd", _f_out_gate, og_args, [Arg(d_og_a, bc=HEAD, ph=True)], tr, nh=N_HEADS, head_inner=True)
    dqa, dka, dva, d_ba_groups, d_alog, d_dtb = delta_bwd(qa, ka, va, z_ba, alog, dtb, states, t_invs, d_o_a, lp)
    (d_ba,) = row_call("a_dba_sum", lambda *parts: (sum(parts[1:], parts[0]),),
                       [Arg(d_ba_groups, bc=HEAD, base=g) for g in range(N_HEADS // DELTA_HEADS)], [(HEAD, F32, HEAD, False)], tr)
    grads["a_log"], grads["a_dt_bias"] = d_alog[:, :N_HEADS], d_dtb[:, :N_HEADS]
    dz_q, dw_q = conv_bwd("a_conv_q_bwd", z_qkv, a_conv, dqa, "q", lp)
    dz_k, dw_k = conv_bwd("a_conv_k_bwd", z_qkv, a_conv, dka, "k", lp)
    dz_v, dw_v = conv_bwd("a_conv_v_bwd", z_qkv, a_conv, dva, "v", lp)
    grads["a_conv"] = jnp.concatenate([dw_q, dw_k, dw_v], 1).T
    dz_qkv = jnp.concatenate([dz_q, dz_k, dz_v], 1)
    d_hn = matmul("a_in_qkv_dx", dz_qkv, w_qkv_t, "nn")
    d_hn = matmul("a_in_gate_dx", d_gate_a, w_ga_t, "nn", res=d_hn)
    d_hn = matmul("a_in_ba_dx", d_ba, w_ba_t, "nn", res=d_hn)
    grads["a_w_in"] = jnp.concatenate([matmul("a_in_qkv_dw", dz_qkv, hn, "tn"), matmul("a_in_gate_dw", d_gate_a, hn, "tn"),
                                       matmul("a_in_ba_dw", d_ba, hn, "tn")[:2 * N_HEADS]], 0)
    d_h0_norm, grads["a_norm"] = row_vjp_call("a_norm_bwd", _f_rms, [Arg(h0, diff=True), Arg(a_norm, "par", diff=True)], [Arg(d_hn)], tr)
    (dh0,) = row_call("dh0_sum", lambda a, b: (a + b,), [Arg(dh1), Arg(d_h0_norm)], [(d, F32, d, False)], tr)
    dh0 = dh0.reshape(nb, lp, d)
    grads["meta_tokens"] = meta_grad(dh0).T
    return loss, dh0[:, LEAD:], grads


_SHARDED = (
    ("meta_tokens", True, False), ("a_norm", True, False), ("a_w_in", True, True), ("a_conv", True, False), ("a_w_out", False, True),
    ("kv_w_down", False, True), ("kv_w_uk", True, True), ("kv_w_uv", True, True), ("b_w_in", True, True), ("b_w_uq", True, True),
    ("b_w_out", False, True))
_REPLICATED = ("a_log", "a_dt_bias", "a_o_gain", "kv_norm", "kv_latent_norm", "k_gain", "b_norm", "b_q_latent_norm", "b_q_gain")
_ALL_WEIGHTS = ("meta_tokens", "a_norm", "a_w_in", "a_conv", "a_log", "a_dt_bias", "a_o_gain", "a_w_out", "kv_norm", "kv_w_down",
                "kv_latent_norm", "kv_w_uk", "kv_w_uv", "k_gain", "b_norm", "b_w_in", "b_q_latent_norm", "b_w_uq", "b_q_gain", "b_w_out")


def _round_up(n, m):
    return (n + m - 1) // m * m


def _pack_rows(pieces, row_multiple):
    padded = []
    for p in pieces:
        n = p.shape[-1]
        padded.append(jnp.pad(p, [(0, 0)] * (p.ndim - 1) + [(0, _round_up(n, PACK_COLS) - n)]))
    flat = jnp.concatenate(padded, -1)
    rows = _round_up(flat.shape[-1] // PACK_COLS, row_multiple)
    flat = jnp.pad(flat, [(0, 0)] * (flat.ndim - 1) + [(0, rows * PACK_COLS - flat.shape[-1])])
    return flat.reshape(flat.shape[:-1] + (rows, PACK_COLS))


def _unpack_rows(buf, sizes):
    flat = buf.reshape(buf.shape[:-2] + (-1,))
    out, off = [], 0
    for n in sizes:
        out.append(flat[..., off:off + n])
        off += _round_up(n, PACK_COLS)
    return out


def _as_bf16_pairs(a):
    return lax.bitcast_convert_type(a, BF16).reshape(-1)


def _from_bf16_pairs(a):
    return lax.bitcast_convert_type(a.reshape(a.shape[:-1] + (-1, 2)), F32)


def _shard_2d(a):
    return a.reshape(a.shape[-2:]) if a.ndim > 2 else a


def gather_weights(local):
    pieces, shapes = [], []
    for name, by_cols, narrow in _SHARDED:
        shard = _shard_2d(local[name])
        shard = shard.T if by_cols else shard
        shapes.append(shard.shape)
        flat = shard.reshape(-1)
        pieces.append(flat.astype(BF16) if narrow else _as_bf16_pairs(flat))
    gathered = all_gather(_pack_rows(pieces, 16))
    parts = _unpack_rows(gathered, [p.shape[0] for p in pieces])
    full = {}
    for (name, _, narrow), part, (rows, cols) in zip(_SHARDED, parts, shapes, strict=True):
        vals = part if narrow else _from_bf16_pairs(part)
        full[name] = vals.reshape(N_DEV * rows, cols)
    full["a_norm"] = full["a_norm"].reshape(1, -1)
    return full


def reduce_contributions(recv):
    _, r, c = recv.shape
    tr = _pick(r, (256, 128, 64, 32, 16, 8))

    def body(g_ref, o_ref):
        g = g_ref[0]
        for dev in range(1, N_DEV):
            g = g + g_ref[dev]
        o_ref[...] = g

    return pl.pallas_call(
        body, grid=(r // tr,), in_specs=[pl.BlockSpec((N_DEV, tr, c), lambda i: (0, i, 0))], out_specs=pl.BlockSpec((tr, c), lambda i: (i, 0)),
        out_shape=jax.ShapeDtypeStruct((r, c), F32), compiler_params=_cparams(("arbitrary",)), name="reduce_contributions")(recv)


def adamw(g, w, m, v):
    r, c = g.shape
    tr = _pick(r, (256, 128, 64, 32, 16, 8))

    def body(g_ref, w_ref, m_ref, v_ref, d_ref, mo_ref, vo_ref):
        g_ = g_ref[...]
        m_new = ADAM_B1 * m_ref[...] + (1.0 - ADAM_B1) * g_
        v_new = ADAM_B2 * v_ref[...] + (1.0 - ADAM_B2) * (g_ * g_)
        m_hat = m_new / (1.0 - ADAM_B1 ** ADAM_STEP)
        v_hat = v_new / (1.0 - ADAM_B2 ** ADAM_STEP)
        d_ref[...] = -ADAM_LR * (m_hat / (jnp.sqrt(v_hat) + ADAM_EPS) + ADAM_WD * w_ref[...])
        mo_ref[...] = m_new
        vo_ref[...] = v_new

    blk = pl.BlockSpec((tr, c), lambda i: (i, 0))
    out = jax.ShapeDtypeStruct((r, c), F32)
    return pl.pallas_call(body, grid=(r // tr,), in_specs=[blk, blk, blk, blk], out_specs=[blk, blk, blk], out_shape=[out, out, out],
                          compiler_params=_cparams(("arbitrary",)), name="adamw")(g, w, m, v)


def kernel(x, meta_tokens, a_norm, a_w_in, a_conv, a_log, a_dt_bias, a_o_gain, a_w_out, kv_norm, kv_w_down, kv_latent_norm, kv_w_uk, kv_w_uv, k_gain, b_norm, b_w_in, b_q_latent_norm, b_w_uq, b_q_gain, b_w_out, loss_target, m_meta_tokens, m_a_norm, m_a_w_in, m_a_conv, m_a_log, m_a_dt_bias, m_a_o_gain, m_a_w_out, m_kv_norm, m_kv_w_down, m_kv_latent_norm, m_kv_w_uk, m_kv_w_uv, m_k_gain, m_b_norm, m_b_w_in, m_b_q_latent_norm, m_b_w_uq, m_b_q_gain, m_b_w_out, v_meta_tokens, v_a_norm, v_a_w_in, v_a_conv, v_a_log, v_a_dt_bias, v_a_o_gain, v_a_w_out, v_kv_norm, v_kv_w_down, v_kv_latent_norm, v_kv_w_uk, v_kv_w_uv, v_k_gain, v_b_norm, v_b_w_in, v_b_q_latent_norm, v_b_w_uq, v_b_q_gain, v_b_w_out):
    given = dict(locals())
    local_w = {n: given[n] for n in _ALL_WEIGHTS}
    full = gather_weights(local_w)
    for n in _REPLICATED:
        full[n] = local_w[n]

    loss_part, grad_x, grads = local_step(x, loss_target, full)

    pieces = [grads[n].reshape(N_DEV, -1) for n, _, _ in _SHARDED]
    pieces += [jnp.broadcast_to(grads[n].reshape(1, -1), (N_DEV, grads[n].size)) for n in _REPLICATED]
    pieces.append(jnp.broadcast_to(loss_part, (N_DEV, 1)))
    summed = reduce_contributions(all_to_all(_pack_rows(pieces, 256)))
    parts = _unpack_rows(summed, [p.shape[1] for p in pieces])
    loss = parts[-1][0]

    order = [n for n, _, _ in _SHARDED] + list(_REPLICATED)
    grad_local = {}
    for (n, by_cols, _), part in zip(_SHARDED, parts, strict=False):
        rows, cols = _shard_2d(local_w[n]).shape
        g2 = part.reshape(cols, rows).T if by_cols else part.reshape(rows, cols)
        grad_local[n] = g2.reshape(local_w[n].shape)
    for n, part in zip(_REPLICATED, parts[len(_SHARDED):], strict=False):
        grad_local[n] = part.reshape(local_w[n].shape)

    def pack_local(tree):
        return _pack_rows([tree[n].reshape(-1) for n in order], 256)

    packed = adamw(pack_local(grad_local), pack_local(local_w), pack_local({n: given["m_" + n] for n in order}),
                   pack_local({n: given["v_" + n] for n in order}))
    results = [grad_local[n] for n in _ALL_WEIGHTS]
    for buf in packed:
        by_name = {n: p.reshape(local_w[n].shape) for n, p in zip(order, _unpack_rows(buf, [local_w[n].size for n in order]), strict=True)}
        results.extend(by_name[n] for n in _ALL_WEIGHTS)
    return (loss, grad_x, *results)
```

```python
import dataclasses
import functools
import math

import jax
import jax.numpy as jnp
from jax import lax
from jax.experimental import pallas as pl
from jax.experimental.pallas import tpu as pltpu

F32 = jnp.float32
BF16 = jnp.bfloat16
_MXU_DTYPE = jnp.bfloat16
_HI = lax.Precision.HIGHEST

N_DEV = 8
D_MODEL = 1024
N_HEADS = 8
HEAD = 128
CHUNK = 64
N_META = 16
PAD_ROWS = 2 * CHUNK - N_META
LEAD = PAD_ROWS + N_META
ROPE = 64
QK_DIM = HEAD + ROPE
QK_PAD = 2 * HEAD
KV_RANK = 256
Q_RANK = 384
CONV_K = 4
EPS = 1e-6
NEG = -1e30
ROPE_THETA = 10000.0
ADAM_LR, ADAM_B1, ADAM_B2, ADAM_EPS, ADAM_WD, ADAM_STEP = 0.001, 0.9, 0.999, 1e-08, 0.01, 10
PACK_COLS = 512
VMEM_LIMIT = 56 * 1024 * 1024


def _pick(n, options):
    for o in options:
        if n % o == 0:
            return o
    raise ValueError(f"no tile for {n} among {options}")


def _cparams(sem):
    return pltpu.CompilerParams(dimension_semantics=sem, vmem_limit_bytes=VMEM_LIMIT)


def _dims(a, dims):
    if a.ndim == 2:
        return (dims, ((), ()))
    (ca,), (cb,) = dims
    return (((ca + 1,), (cb + 1,)), ((0,), (0,)))


def _dot(a, b, dims):
    return lax.dot_general(a.astype(_MXU_DTYPE), b.astype(_MXU_DTYPE), _dims(a, dims), preferred_element_type=F32)


@jax.custom_vjp
def mm_nn(a, b):
    return _dot(a, b, ((1,), (0,)))


@jax.custom_vjp
def mm_nt(a, b):
    return _dot(a, b, ((1,), (1,)))


@jax.custom_vjp
def mm_tn(a, b):
    return _dot(a, b, ((0,), (0,)))


mm_nn.defvjp(lambda a, b: (mm_nn(a, b), (a, b)), lambda r, g: (mm_nt(g, r[1]), mm_tn(r[0], g)))
mm_nt.defvjp(lambda a, b: (mm_nt(a, b), (a, b)), lambda r, g: (mm_nn(g, r[1]), mm_tn(g, r[0])))
mm_tn.defvjp(lambda a, b: (mm_tn(a, b), (a, b)), lambda r, g: (mm_nt(r[1], g), mm_nn(r[0], g)))


def _dot_f32(a, b):
    return lax.dot_general(a, b, _dims(a, ((1,), (0,))), precision=_HI, preferred_element_type=F32)


def _split_hi_lo(x):
    hi = x.astype(_MXU_DTYPE)
    lo = (x - hi.astype(F32)).astype(_MXU_DTYPE)
    return hi, lo


def _mm_3pass(a, b):
    ah, al = _split_hi_lo(a)
    bh, bl = _split_hi_lo(b)
    d = lambda u, w: lax.dot_general(u, w, _dims(u, ((1,), (0,))), preferred_element_type=F32)
    return d(ah, bh) + (d(ah, bl) + d(al, bh))


def _inv_unit_lower(a):
    n = a.shape[-1]
    eye = (lax.broadcasted_iota(jnp.int32, (n, n), 0) == lax.broadcasted_iota(jnp.int32, (n, n), 1)).astype(F32)
    t = eye - a
    p = _mm_3pass(a, a)
    squarings = int(math.log2(n)) - 1
    for s in range(squarings):
        t = t + _mm_3pass(t, p)
        if s + 1 < squarings:
            p = _mm_3pass(p, p)
    return t


@jax.custom_vjp
def _inv_lookup(a, t):
    return t


def _inv_lookup_bwd(t, g):
    return -mm_tn(t, mm_nt(g, t)), jnp.zeros_like(t)


_inv_lookup.defvjp(lambda a, t: (t, t), _inv_lookup_bwd)


def _sigmoid(x):
    return 1.0 / (1.0 + jnp.exp(-x))


def _silu(x):
    return x * _sigmoid(x)


def _softplus(x):
    return jnp.where(x > 20.0, x, jnp.log(1.0 + jnp.exp(jnp.minimum(x, 20.0))))


def _rms(x, g, width=None):
    ms = jnp.sum(x * x, -1, keepdims=True) / (x.shape[-1] if width is None else width)
    return x * lax.rsqrt(ms + EPS) * g


def matmul(name, a, b, mode, out_dtype=F32, res=None):
    if mode == "nn":
        (m, k), (k2, n) = a.shape, b.shape
    elif mode == "nt":
        (m, k), (n, k2) = a.shape, b.shape
    else:
        (k, m), (k2, n) = a.shape, b.shape
    assert k == k2, (name, a.shape, b.shape, mode)
    tm = _pick(m, (512, 384, 256, 128))
    tn = _pick(n, (512, 384, 256, 128))
    tk = _pick(k, (512, 256, 128)) if mode == "tn" else _pick(k, (1024, 512, 384, 256, 128))
    nk = k // tk
    dims = {"nn": ((1,), (0,)), "nt": ((1,), (1,)), "tn": ((0,), (0,))}[mode]

    def body(*refs):
        if res is None:
            a_ref, b_ref, o_ref, acc_ref = refs
        else:
            a_ref, b_ref, r_ref, o_ref, acc_ref = refs
        kk = pl.program_id(2)

        @pl.when(kk == 0)
        def _():
            acc_ref[...] = jnp.zeros_like(acc_ref)

        acc_ref[...] += _dot(a_ref[...], b_ref[...], dims)

        @pl.when(kk == nk - 1)
        def _():
            out = acc_ref[...]
            if res is not None:
                out = out + r_ref[...].astype(F32)
            o_ref[...] = out.astype(o_ref.dtype)

    a_spec = pl.BlockSpec((tk, tm), lambda i, j, kk: (kk, i)) if mode == "tn" else pl.BlockSpec((tm, tk), lambda i, j, kk: (i, kk))
    b_spec = pl.BlockSpec((tn, tk), lambda i, j, kk: (j, kk)) if mode == "nt" else pl.BlockSpec((tk, tn), lambda i, j, kk: (kk, j))
    o_spec = pl.BlockSpec((tm, tn), lambda i, j, kk: (i, j))
    in_specs = [a_spec, b_spec] + ([o_spec] if res is not None else [])
    args = (a, b) + ((res,) if res is not None else ())
    return pl.pallas_call(
        body, grid=(m // tm, n // tn, nk), in_specs=in_specs, out_specs=o_spec,
        out_shape=jax.ShapeDtypeStruct((m, n), out_dtype), scratch_shapes=[pltpu.VMEM((tm, tn), F32)],
        compiler_params=_cparams(("parallel", "parallel", "arbitrary")), name=name)(*args)


@dataclasses.dataclass
class Arg:
    arr: jax.Array
    kind: str = "row"
    bc: int = 0
    base: int = 0
    ph: bool = False
    diff: bool = False


def _arg_spec(a, tr, head_inner, ntab, base=None):
    bc = a.bc or a.arr.shape[1]
    base = a.base if base is None else base

    def imap(g0, g1):
        i, h = (g0, g1) if head_inner else (g1, g0)
        col = base + (h if a.ph else 0)
        if a.kind == "row":
            return (i, col)
        if a.kind == "tab":
            return (i % ntab, col)
        return (0, col)

    rows = tr if a.kind in ("row", "tab") else a.arr.shape[0]
    return pl.BlockSpec((rows, bc), imap)


def _load(ref):
    v = ref[...]
    return v.astype(F32) if jnp.issubdtype(v.dtype, jnp.floating) else v


def row_call(name, fn, args, outs, tr, nh=1, head_inner=True, ntab=1):
    t = args[0].arr.shape[0]
    ni = t // tr
    n_in = len(args)

    def body(*refs):
        res = fn(*[_load(r) for r in refs[:n_in]])
        for r, v in zip(refs[n_in:], res, strict=True):
            r[...] = v.astype(r.dtype)

    grid = (ni, nh) if head_inner else (nh, ni)
    out_specs = [_arg_spec(Arg(None, "row", bc, 0, ph), tr, head_inner, ntab) for (_, _, bc, ph) in outs]
    out_shape = [jax.ShapeDtypeStruct((t, cols), dt) for (cols, dt, _, _) in outs]
    return pl.pallas_call(
        body, grid=grid, in_specs=[_arg_spec(a, tr, head_inner, ntab) for a in args], out_specs=out_specs,
        out_shape=out_shape, compiler_params=_cparams(("arbitrary", "arbitrary")), name=name)(*[a.arr for a in args])


def row_vjp_call(name, fn, args, cts, tr, nh=1, head_inner=True, ntab=1):
    t = args[0].arr.shape[0]
    ni = t // tr
    n_in, n_ct = len(args), len(cts)
    diff_idx = [k for k, a in enumerate(args) if a.diff]
    modes = []
    for k in diff_idx:
        a = args[k]
        if a.kind == "row":
            modes.append("write" if (a.ph or nh == 1) else "acc_heads")
        else:
            modes.append("acc_rows" if a.ph else "acc_all")
    assert not ("acc_heads" in modes and not head_inner) and not ("acc_rows" in modes and head_inner and nh > 1)

    def body(*refs):
        vals = [_load(r) for r in refs[:n_in]]
        ct_vals = tuple(_load(r) for r in refs[n_in:n_in + n_ct])
        out_refs = refs[n_in + n_ct:]
        g0, g1 = pl.program_id(0), pl.program_id(1)
        i, h = (g0, g1) if head_inner else (g1, g0)

        def f(*dv):
            full = list(vals)
            for k, v in zip(diff_idx, dv, strict=True):
                full[k] = v
            return tuple(fn(*full))

        _, vjp = jax.vjp(f, *[vals[k] for k in diff_idx])
        grads = vjp(ct_vals)
        for r, g, mode in zip(out_refs, grads, modes, strict=True):
            if mode == "write":
                r[...] = g.astype(r.dtype)
            else:
                first = {"acc_heads": h == 0, "acc_rows": i == 0, "acc_all": (i == 0) & (h == 0)}[mode]

                @pl.when(first)
                def _(r=r):
                    r[...] = jnp.zeros_like(r)

                r[...] += g

    grid = (ni, nh) if head_inner else (nh, ni)
    out_specs, out_shape = [], []
    for k in diff_idx:
        a = args[k]
        bc = a.bc or a.arr.shape[1]
        out_specs.append(_arg_spec(a, tr, head_inner, ntab, base=0))
        out_shape.append(jax.ShapeDtypeStruct((t if a.kind == "row" else a.arr.shape[0], bc * (nh if a.ph else 1)), F32))
    in_specs = [_arg_spec(a, tr, head_inner, ntab) for a in list(args) + list(cts)]
    return pl.pallas_call(
        body, grid=grid, in_specs=in_specs, out_specs=out_specs, out_shape=out_shape,
        compiler_params=_cparams(("arbitrary", "arbitrary")), name=name)(*[a.arr for a in list(args) + list(cts)])


def _conv_taps(x, w):
    rows = lax.broadcasted_iota(jnp.int32, x.shape, 0)
    y = x * w[CONV_K - 1:CONV_K, :]
    shifted = []
    for s in range(1, CONV_K):
        xs = jnp.where(rows >= s, pltpu.roll(x, s, 0), 0.0)
        shifted.append(xs)
        y = y + xs * w[CONV_K - 1 - s:CONV_K - s, :]
    return y, shifted


def _conv_post(y, mode):
    a = _silu(y)
    if mode == "v":
        return a
    out = a * lax.rsqrt(jnp.sum(a * a, -1, keepdims=True) + EPS)
    return out * (HEAD ** -0.5) if mode == "q" else out


def conv_fwd(name, z, w, mode, lp):
    t = z.shape[0]
    base = {"q": 0, "k": N_HEADS, "v": 2 * N_HEADS}[mode]

    def body(z_ref, w_ref, o_ref):
        y, _ = _conv_taps(z_ref[...], w_ref[...])
        o_ref[...] = _conv_post(y, mode)

    return pl.pallas_call(
        body, grid=(t // lp, N_HEADS),
        in_specs=[pl.BlockSpec((lp, HEAD), lambda b, h: (b, base + h)), pl.BlockSpec((CONV_K, HEAD), lambda b, h: (0, base + h))],
        out_specs=pl.BlockSpec((lp, HEAD), lambda b, h: (b, h)), out_shape=jax.ShapeDtypeStruct((t, N_HEADS * HEAD), F32),
        compiler_params=_cparams(("arbitrary", "arbitrary")), name=name)(z, w)


def conv_bwd(name, z, w, dout, mode, lp):
    t = z.shape[0]
    base = {"q": 0, "k": N_HEADS, "v": 2 * N_HEADS}[mode]

    def body(z_ref, w_ref, g_ref, dz_ref, dw_ref):
        x, wv = z_ref[...], w_ref[...]
        y, shifted = _conv_taps(x, wv)
        _, vjp = jax.vjp(lambda y_: _conv_post(y_, mode), y)
        (dy,) = vjp(g_ref[...])
        rows = lax.broadcasted_iota(jnp.int32, x.shape, 0)
        dx = dy * wv[CONV_K - 1:CONV_K, :]
        for s in range(1, CONV_K):
            dx = dx + jnp.where(rows < lp - s, pltpu.roll(dy, lp - s, 0), 0.0) * wv[CONV_K - 1 - s:CONV_K - s, :]
        dz_ref[...] = dx

        @pl.when(pl.program_id(1) == 0)
        def _():
            dw_ref[...] = jnp.zeros_like(dw_ref)

        dw_ref[CONV_K - 1:CONV_K, :] += jnp.sum(dy * x, axis=0, keepdims=True)
        for s in range(1, CONV_K):
            dw_ref[CONV_K - 1 - s:CONV_K - s, :] += jnp.sum(dy * shifted[s - 1], axis=0, keepdims=True)

    return pl.pallas_call(
        body, grid=(N_HEADS, t // lp),
        in_specs=[pl.BlockSpec((lp, HEAD), lambda h, b: (b, base + h)), pl.BlockSpec((CONV_K, HEAD), lambda h, b: (0, base + h)),
                  pl.BlockSpec((lp, HEAD), lambda h, b: (b, h))],
        out_specs=[pl.BlockSpec((lp, HEAD), lambda h, b: (b, h)), pl.BlockSpec((CONV_K, HEAD), lambda h, b: (0, h))],
        out_shape=[jax.ShapeDtypeStruct((t, N_HEADS * HEAD), F32), jax.ShapeDtypeStruct((CONV_K, N_HEADS * HEAD), F32)],
        compiler_params=_cparams(("arbitrary", "arbitrary")), name=name)(z, w, dout)


def _delta_chunk(q, k, v, ba, alog, dtb, state, t_stored, h0):
    n_g, c = q.shape[0], q.shape[1]
    lane = lax.broadcasted_iota(jnp.int32, (1, HEAD), 1)

    def pick(x, offset):
        return jnp.stack([jnp.sum(x * (lane == offset + h0 + g).astype(F32), axis=1, keepdims=True) for g in range(n_g)])

    b_raw, a_raw = pick(ba, 0), pick(ba, N_HEADS)
    a_log, dt_bias = pick(alog, 0), pick(dtb, 0)
    beta = _sigmoid(b_raw)
    g = -jnp.exp(a_log) * _softplus(a_raw + dt_bias)
    ri = lax.broadcasted_iota(jnp.int32, (c, c), 0)
    ci = lax.broadcasted_iota(jnp.int32, (c, c), 1)
    tril = ci <= ri
    lower = jnp.broadcast_to(tril.astype(F32), (n_g, c, c))
    gc_col = _dot_f32(lower, g * jnp.ones((1, 1, HEAD), F32))[:, :, :1]
    gc_row = _dot_f32(jnp.ones((n_g, 8, c), F32), g * (ri <= ci).astype(F32)[None])[:, 0:1, :]
    gc_last = jnp.sum(g, axis=1, keepdims=True)
    decay = jnp.exp(jnp.where(tril, gc_col - gc_row, NEG))
    e_gc = jnp.exp(gc_col)
    kb = k * beta
    a_mat = jnp.where(ci < ri, mm_nt(kb, k) * decay, 0.0)
    t_inv = _inv_unit_lower(a_mat) if t_stored is None else _inv_lookup(a_mat, t_stored)
    u_base = mm_nn(t_inv, v * beta)
    w_dec = mm_nn(t_inv, kb * e_gc)
    attn = jnp.where(tril, mm_nt(q, k) * decay, 0.0)
    u = u_base - mm_nn(w_dec, state)
    o = mm_nn(q * e_gc, state) + mm_nn(attn, u)
    new_state = state * jnp.exp(gc_last) + mm_tn(k * jnp.exp(gc_last - gc_col), u)
    return o, new_state, t_inv


DELTA_HEADS = 8
DELTA_CHUNKS_FWD = 2
DELTA_CHUNKS_BWD = 1


def delta_fwd(q, k, v, ba, alog, dtb, lp):
    t = q.shape[0]
    nb, nc = t // lp, lp // CHUNK
    hg, cps = DELTA_HEADS, DELTA_CHUNKS_FWD
    ng, rows = nc // cps, cps * CHUNK
    assert nc % cps == 0 and N_HEADS % hg == 0

    def body(q_ref, k_ref, v_ref, ba_ref, al_ref, dt_ref, o_ref, s_ref, t_ref, state_ref):
        group = pl.program_id(1)

        @pl.when(pl.program_id(2) == 0)
        def _():
            state_ref[...] = jnp.zeros_like(state_ref)

        al, dtv = al_ref[...], dt_ref[...]
        for c in range(cps):
            rs = slice(c * CHUNK, (c + 1) * CHUNK)
            heads = lambda ref: jnp.stack([ref[rs, g * HEAD:(g + 1) * HEAD] for g in range(hg)])
            state = state_ref[...]
            o, new_state, t_inv = _delta_chunk(heads(q_ref), heads(k_ref), heads(v_ref), ba_ref[rs, :], al, dtv, state, None, group * hg)
            for g in range(hg):
                o_ref[rs, g * HEAD:(g + 1) * HEAD] = o[g]
                s_ref[g, c] = state[g]
                t_ref[g, c] = t_inv[g]
            state_ref[...] = new_state

    head_spec = pl.BlockSpec((rows, hg * HEAD), lambda b, h, n: (b * ng + n, h))
    par_spec = pl.BlockSpec((1, HEAD), lambda b, h, n: (0, 0))
    return pl.pallas_call(
        body, grid=(nb, N_HEADS // hg, ng),
        in_specs=[head_spec, head_spec, head_spec, pl.BlockSpec((rows, HEAD), lambda b, h, n: (b * ng + n, 0)), par_spec, par_spec],
        out_specs=[head_spec, pl.BlockSpec((None, hg, cps, HEAD, HEAD), lambda b, h, n: (b, h, n, 0, 0)),
                   pl.BlockSpec((None, hg, cps, CHUNK, CHUNK), lambda b, h, n: (b, h, n, 0, 0))],
        out_shape=[jax.ShapeDtypeStruct((t, N_HEADS * HEAD), F32), jax.ShapeDtypeStruct((nb, N_HEADS, nc, HEAD, HEAD), F32),
                   jax.ShapeDtypeStruct((nb, N_HEADS, nc, CHUNK, CHUNK), F32)],
        scratch_shapes=[pltpu.VMEM((hg, HEAD, HEAD), F32)],
        compiler_params=_cparams(("arbitrary", "arbitrary", "arbitrary")), name="delta_fwd")(q, k, v, ba, alog, dtb)


def delta_bwd(q, k, v, ba, alog, dtb, states, t_invs, do, lp):
    t = q.shape[0]
    nb, nc = t // lp, lp // CHUNK
    hg, cps = DELTA_HEADS, DELTA_CHUNKS_BWD
    ng, rows = nc // cps, cps * CHUNK
    n_groups = N_HEADS // hg

    def body(q_ref, k_ref, v_ref, ba_ref, al_ref, dt_ref, s_ref, t_ref, do_ref, dq_ref, dk_ref, dv_ref, dba_ref, dal_ref, ddt_ref, dstate_ref):
        b, group, step = pl.program_id(0), pl.program_id(1), pl.program_id(2)

        @pl.when(step == 0)
        def _():
            dstate_ref[...] = jnp.zeros_like(dstate_ref)

        @pl.when((b == 0) & (group == 0) & (step == 0))
        def _():
            dal_ref[...] = jnp.zeros_like(dal_ref)
            ddt_ref[...] = jnp.zeros_like(ddt_ref)

        al, dtv = al_ref[...], dt_ref[...]
        d_al = jnp.zeros((1, HEAD), F32)
        d_dt = jnp.zeros((1, HEAD), F32)
        for c in reversed(range(cps)):
            rs = slice(c * CHUNK, (c + 1) * CHUNK)
            heads = lambda ref: jnp.stack([ref[rs, g * HEAD:(g + 1) * HEAD] for g in range(hg)])
            t_n = jnp.stack([t_ref[g, c] for g in range(hg)])
            s_n = jnp.stack([s_ref[g, c] for g in range(hg)])

            def f(q_, k_, v_, ba_, al_, dt_, s_, t_n=t_n):
                return _delta_chunk(q_, k_, v_, ba_, al_, dt_, s_, t_n, group * hg)[:2]

            _, vjp = jax.vjp(f, heads(q_ref), heads(k_ref), heads(v_ref), ba_ref[rs, :], al, dtv, s_n)
            gq, gk, gv, gba, gal, gdt, gs = vjp((heads(do_ref), dstate_ref[...]))
            for g in range(hg):
                dq_ref[rs, g * HEAD:(g + 1) * HEAD] = gq[g]
                dk_ref[rs, g * HEAD:(g + 1) * HEAD] = gk[g]
                dv_ref[rs, g * HEAD:(g + 1) * HEAD] = gv[g]
            dstate_ref[...] = gs
            dba_ref[rs, :] = gba
            d_al, d_dt = d_al + gal, d_dt + gdt
        dal_ref[...] += d_al
        ddt_ref[...] += d_dt

    head_spec = pl.BlockSpec((rows, hg * HEAD), lambda b, h, n: (b * ng + ng - 1 - n, h))
    par_spec = pl.BlockSpec((1, HEAD), lambda b, h, n: (0, 0))
    big = jax.ShapeDtypeStruct((t, N_HEADS * HEAD), F32)
    return pl.pallas_call(
        body, grid=(nb, n_groups, ng),
        in_specs=[head_spec, head_spec, head_spec, pl.BlockSpec((rows, HEAD), lambda b, h, n: (b * ng + ng - 1 - n, 0)), par_spec, par_spec,
                  pl.BlockSpec((None, hg, cps, HEAD, HEAD), lambda b, h, n: (b, h, ng - 1 - n, 0, 0)),
                  pl.BlockSpec((None, hg, cps, CHUNK, CHUNK), lambda b, h, n: (b, h, ng - 1 - n, 0, 0)), head_spec],
        out_specs=[head_spec, head_spec, head_spec, pl.BlockSpec((rows, HEAD), lambda b, h, n: (b * ng + ng - 1 - n, h)), par_spec, par_spec],
        out_shape=[big, big, big, jax.ShapeDtypeStruct((t, n_groups * HEAD), F32), jax.ShapeDtypeStruct((1, HEAD), F32), jax.ShapeDtypeStruct((1, HEAD), F32)],
        scratch_shapes=[pltpu.VMEM((hg, HEAD, HEAD), F32)],
        compiler_params=_cparams(("arbitrary", "arbitrary", "arbitrary")), name="delta_bwd")(q, k, v, ba, alog, dtb, states, t_invs, do)


ATT_Q_TILE = 256
ATT_K_TILE = 512
ATT_SCALE = QK_DIM ** -0.5


def _tiles(end, size):
    return [(s, min(s + size, end)) for s in range(0, end, size)]


def _att_visible(q0, q1, k0, k1, keys_first):
    if k1 <= q0 + CHUNK and k0 >= PAD_ROWS:
        return None
    shape = (k1 - k0, q1 - q0) if keys_first else (q1 - q0, k1 - k0)
    qpos = q0 + lax.broadcasted_iota(jnp.int32, shape, 1 if keys_first else 0)
    kpos = k0 + lax.broadcasted_iota(jnp.int32, shape, 0 if keys_first else 1)
    shift = CHUNK.bit_length() - 1
    return (jnp.right_shift(kpos, shift) <= jnp.right_shift(qpos, shift)) & (kpos >= PAD_ROWS)


def _att_seq_specs(lp):
    return pl.BlockSpec((lp, QK_PAD), lambda b, h: (b, h)), pl.BlockSpec((lp, HEAD), lambda b, h: (b, h))


def flash_fwd(q, k, v, lp):
    t = q.shape[0]
    qk_seq, o_seq = _att_seq_specs(lp)

    def body(q_ref, k_ref, v_ref, o_ref, lse_ref):
        for q0, q1 in _tiles(lp, ATT_Q_TILE):
            qb = q_ref[q0:q1, :]
            k_tiles = _tiles(q1, ATT_K_TILE)
            scores, m = [], None
            for k0, k1 in k_tiles:
                s = mm_nt(qb, k_ref[k0:k1, :]) * ATT_SCALE
                vis = _att_visible(q0, q1, k0, k1, False)
                s = s if vis is None else jnp.where(vis, s, NEG)
                scores.append(s)
                row_max = jnp.max(s, -1, keepdims=True)
                m = row_max if m is None else jnp.maximum(m, row_max)
            l = jnp.zeros((q1 - q0, 1), F32)
            acc = jnp.zeros((q1 - q0, HEAD), F32)
            for s, (k0, k1) in zip(scores, k_tiles, strict=True):
                p = jnp.exp(s - m)
                l = l + jnp.sum(p, -1, keepdims=True)
                acc = acc + mm_nn(p, v_ref[k0:k1, :])
            o_ref[q0:q1, :] = acc / l
            lse_ref[q0:q1, :] = jnp.broadcast_to(m + jnp.log(l), (q1 - q0, HEAD))

    big = jax.ShapeDtypeStruct((t, N_HEADS * HEAD), F32)
    return pl.pallas_call(
        body, grid=(t // lp, N_HEADS), in_specs=[qk_seq, qk_seq, o_seq], out_specs=[o_seq, o_seq], out_shape=[big, big],
        compiler_params=_cparams(("arbitrary", "arbitrary")), name="flash_fwd")(q, k, v)


def flash_bwd(q, k, v, o, lse, do, lp):
    t = q.shape[0]
    qk_seq, o_seq = _att_seq_specs(lp)

    def body(q_ref, k_ref, v_ref, o_ref, lse_ref, do_ref, dq_ref, dk_ref, dv_ref):
        dk_ref[...] = jnp.zeros_like(dk_ref)
        dv_ref[...] = jnp.zeros_like(dv_ref)
        for q0, q1 in _tiles(lp, ATT_Q_TILE):
            qb, dob = q_ref[q0:q1, :], do_ref[q0:q1, :]
            lse_row = jnp.transpose(lse_ref[q0:q1, :])[0:1, :]
            dsum_row = jnp.sum(jnp.transpose(dob * o_ref[q0:q1, :]), axis=0, keepdims=True)
            dq = jnp.zeros((q1 - q0, QK_PAD), F32)
            for k0, k1 in _tiles(q1, ATT_K_TILE):
                kb, vb = k_ref[k0:k1, :], v_ref[k0:k1, :]
                s = mm_nt(kb, qb) * ATT_SCALE
                vis = _att_visible(q0, q1, k0, k1, True)
                s = s if vis is None else jnp.where(vis, s, NEG)
                p = jnp.exp(s - lse_row)
                ds = p * (mm_nt(vb, dob) - dsum_row) * ATT_SCALE
                dv_ref[k0:k1, :] += mm_nn(p, dob)
                dk_ref[k0:k1, :] += mm_nn(ds, qb)
                dq = dq + mm_tn(ds, kb)
            dq_ref[q0:q1, :] = dq

    return pl.pallas_call(
        body, grid=(t // lp, N_HEADS), in_specs=[qk_seq, qk_seq, o_seq, o_seq, o_seq, o_seq], out_specs=[qk_seq, qk_seq, o_seq],
        out_shape=[jax.ShapeDtypeStruct((t, N_HEADS * QK_PAD), F32), jax.ShapeDtypeStruct((t, N_HEADS * QK_PAD), F32),
                   jax.ShapeDtypeStruct((t, N_HEADS * HEAD), F32)],
        compiler_params=_cparams(("arbitrary", "arbitrary")), name="flash_bwd")(q, k, v, o, lse, do)


def loss_head(h2, target, lp):
    nb, seq, d = target.shape
    tr = 128
    nblk = lp // tr
    lead_blocks = LEAD // tr

    def body(h_ref, t_ref, loss_ref, dh_ref, acc_ref):
        b, i = pl.program_id(0), pl.program_id(1)

        @pl.when((b == 0) & (i == 0))
        def _():
            acc_ref[...] = jnp.zeros_like(acc_ref)

        @pl.when(i < lead_blocks)
        def _():
            dh_ref[...] = jnp.zeros_like(dh_ref)

        @pl.when(i >= lead_blocks)
        def _():
            err = h_ref[...] - t_ref[...]
            dh_ref[...] = err * (1.0 / d)
            acc_ref[...] += jnp.sum(err * err, axis=0, keepdims=True)

        @pl.when((b == nb - 1) & (i == nblk - 1))
        def _():
            loss_ref[...] = jnp.sum(acc_ref[...], axis=1, keepdims=True) * (0.5 / d)

    return pl.pallas_call(
        body, grid=(nb, nblk),
        in_specs=[pl.BlockSpec((None, tr, d), lambda b, i: (b, i, 0)),
                  pl.BlockSpec((None, tr, d), lambda b, i: (b, jnp.maximum(i - lead_blocks, 0), 0))],
        out_specs=[pl.BlockSpec((1, 1), lambda b, i: (0, 0)), pl.BlockSpec((None, tr, d), lambda b, i: (b, i, 0))],
        out_shape=[jax.ShapeDtypeStruct((1, 1), F32), jax.ShapeDtypeStruct((nb, lp, d), F32)],
        scratch_shapes=[pltpu.VMEM((1, d), F32)], compiler_params=_cparams(("arbitrary", "arbitrary")), name="loss_head")(h2, target)


def meta_grad(dh0):
    nb, _, d = dh0.shape

    def body(g_ref, o_ref):
        @pl.when(pl.program_id(0) == 0)
        def _():
            o_ref[...] = jnp.zeros_like(o_ref)

        o_ref[...] += g_ref[PAD_ROWS:LEAD, :]

    return pl.pallas_call(
        body, grid=(nb,), in_specs=[pl.BlockSpec((None, LEAD, d), lambda b: (b, 0, 0))],
        out_specs=pl.BlockSpec((N_META, d), lambda b: (0, 0)), out_shape=jax.ShapeDtypeStruct((N_META, d), F32),
        compiler_params=_cparams(("arbitrary",)), name="meta_grad")(dh0)


_HBM = pl.BlockSpec(memory_space=pltpu.HBM)


def _mesh_pos():
    x, y, c = lax.axis_index("x"), lax.axis_index("y"), lax.axis_index("c")
    return x, y, c


def _peer(x, y, c, k):
    px = 1 - x if k & 4 else x
    py = 1 - y if k & 2 else y
    pc = 1 - c if k & 1 else c
    return (px, py, pc), 4 * px + 2 * py + pc


def all_gather(buf):
    def body(x_ref, out_ref, send_sems, recv_sems, local_sem):
        x, y, c = _mesh_pos()
        me = 4 * x + 2 * y + c
        mine = pltpu.make_async_copy(x_ref, out_ref.at[me], local_sem)
        mine.start()
        sends = []
        for k in range(1, N_DEV):
            peer, _ = _peer(x, y, c, k)
            cp = pltpu.make_async_remote_copy(src_ref=x_ref, dst_ref=out_ref.at[me], send_sem=send_sems.at[k - 1],
                                              recv_sem=recv_sems.at[k - 1], device_id=peer, device_id_type=pl.DeviceIdType.MESH)
            cp.start()
            sends.append(cp)
        for k in range(1, N_DEV):
            peer, peer_id = _peer(x, y, c, k)
            pltpu.make_async_remote_copy(src_ref=x_ref, dst_ref=out_ref.at[peer_id], send_sem=send_sems.at[k - 1],
                                         recv_sem=recv_sems.at[k - 1], device_id=peer, device_id_type=pl.DeviceIdType.MESH).wait_recv()
        for cp in sends:
            cp.wait_send()
        mine.wait()

    return pl.pallas_call(
        body, in_specs=[_HBM], out_specs=_HBM, out_shape=jax.ShapeDtypeStruct((N_DEV,) + buf.shape, buf.dtype),
        scratch_shapes=[pltpu.SemaphoreType.DMA((N_DEV - 1,)), pltpu.SemaphoreType.DMA((N_DEV - 1,)), pltpu.SemaphoreType.DMA],
        name="all_gather")(buf)


def all_to_all(buf):
    def body(x_ref, out_ref, send_sems, recv_sems, local_sem):
        x, y, c = _mesh_pos()
        me = 4 * x + 2 * y + c
        mine = pltpu.make_async_copy(x_ref.at[me], out_ref.at[me], local_sem)
        mine.start()
        sends = []
        for k in range(1, N_DEV):
            peer, peer_id = _peer(x, y, c, k)
            cp = pltpu.make_async_remote_copy(src_ref=x_ref.at[peer_id], dst_ref=out_ref.at[me], send_sem=send_sems.at[k - 1],
                                              recv_sem=recv_sems.at[k - 1], device_id=peer, device_id_type=pl.DeviceIdType.MESH)
            cp.start()
            sends.append(cp)
        for k in range(1, N_DEV):
            peer, peer_id = _peer(x, y, c, k)
            pltpu.make_async_remote_copy(src_ref=x_ref.at[peer_id], dst_ref=out_ref.at[peer_id], send_sem=send_sems.at[k - 1],
                                         recv_sem=recv_sems.at[k - 1], device_id=peer, device_id_type=pl.DeviceIdType.MESH).wait_recv()
        for cp in sends:
            cp.wait_send()
        mine.wait()

    return pl.pallas_call(
        body, in_specs=[_HBM], out_specs=_HBM, out_shape=jax.ShapeDtypeStruct(buf.shape, buf.dtype),
        scratch_shapes=[pltpu.SemaphoreType.DMA((N_DEV - 1,)), pltpu.SemaphoreType.DMA((N_DEV - 1,)), pltpu.SemaphoreType.DMA],
        name="all_to_all")(buf)


def _f_rms(x, g):
    return (_rms(x, g),)


def _f_rms2(x, g1, g2):
    r = x * lax.rsqrt(jnp.sum(x * x, -1, keepdims=True) / x.shape[-1] + EPS)
    return r * g1, r * g2


def _f_out_gate(o, gate, gain):
    return (_rms(o, gain) * _silu(gate),)


def _f_gate(o, gate):
    return (o * _silu(gate),)


def _f_qk_final(nope, rope_in, g_nope, g_rope, cos, sin, perm):
    ms = (jnp.sum(nope * nope, -1, keepdims=True) + jnp.sum(rope_in * rope_in, -1, keepdims=True)) / QK_DIM
    r = lax.rsqrt(ms + EPS)
    a = nope * r * g_nope
    b = rope_in * r * g_rope
    return (jnp.concatenate([a, b * cos + _dot_f32(b, perm) * sin], axis=1),)


def _rope_tables(lp):
    half = ROPE // 2
    pos = jnp.maximum(jnp.arange(lp) - PAD_ROWS, 0)
    inv = ROPE_THETA ** (-jnp.arange(half, dtype=F32) / half)
    ang = pos.astype(F32)[:, None] * inv[None, :]
    zeros = jnp.zeros((lp, HEAD - ROPE), F32)
    cos = jnp.concatenate([jnp.cos(ang), jnp.cos(ang), zeros], 1)
    sin = jnp.concatenate([-jnp.sin(ang), jnp.sin(ang), zeros], 1)
    src = jnp.arange(HEAD)[:, None]
    dst = jnp.arange(HEAD)[None, :]
    perm = (((dst < half) & (src == dst + half)) | ((dst >= half) & (dst < ROPE) & (src == dst - half))).astype(F32)
    return cos, sin, perm


def _pad_lanes(w, width=HEAD):
    return jnp.pad(w, ((0, 0), (0, width - w.shape[1])))


def _pad_rows(w, rows=HEAD):
    return jnp.pad(w, ((0, rows - w.shape[0]), (0, 0)))


def _split_heads_qk_t(w_t):
    k = w_t.shape[1]
    w3 = w_t.reshape(N_HEADS, QK_DIM, k)
    nope = w3[:, :HEAD].reshape(N_HEADS * HEAD, k)
    rope = jnp.pad(w3[:, HEAD:], ((0, 0), (0, HEAD - ROPE), (0, 0))).reshape(N_HEADS * HEAD, k)
    return jnp.concatenate([nope, rope], 0)


def _merge_heads_qk_t(g_t):
    k = g_t.shape[1]
    kw = N_HEADS * HEAD
    nope, rope = g_t[:kw].reshape(N_HEADS, HEAD, k), g_t[kw:].reshape(N_HEADS, HEAD, k)[:, :ROPE]
    return jnp.concatenate([nope, rope], 1).reshape(N_HEADS * QK_DIM, k)


def local_step(x, target, w):
    nb, seq, d = x.shape
    lp = seq + LEAD
    t = nb * lp
    tr = _pick(lp, (544, 128))
    ntab = lp // tr
    mxu = _MXU_DTYPE
    kw = N_HEADS * HEAD

    a_w_in_t = w["a_w_in"].astype(mxu)
    w_qkv_t, w_ga_t, w_ba_t = a_w_in_t[:3 * kw], a_w_in_t[3 * kw:4 * kw], _pad_rows(a_w_in_t[4 * kw:])
    a_conv = w["a_conv"].T
    a_w_out = w["a_w_out"].astype(mxu)
    alog, dtb, o_gain = _pad_lanes(w["a_log"]), _pad_lanes(w["a_dt_bias"]), w["a_o_gain"]
    w_dkv, w_dpe = w["kv_w_down"][:, :KV_RANK].astype(mxu), _pad_lanes(w["kv_w_down"][:, KV_RANK:]).astype(mxu)
    w_ukv_t = jnp.concatenate([w["kv_w_uk"], w["kv_w_uv"]], 0).astype(mxu)
    b_w_in_t = w["b_w_in"].astype(mxu)
    w_cq_t, w_gb_t = b_w_in_t[:Q_RANK], b_w_in_t[Q_RANK:]
    w_q_t = _split_heads_qk_t(w["b_w_uq"]).astype(mxu)
    b_w_out = w["b_w_out"].astype(mxu)
    a_norm, kv_norm, b_norm = w["a_norm"], w["kv_norm"][None, :], w["b_norm"]
    lat_norm, qlat_norm = w["kv_latent_norm"][None, :], w["b_q_latent_norm"]
    kg_nope, kg_rope = w["k_gain"][None, :HEAD], _pad_lanes(w["k_gain"][None, HEAD:])
    qg_nope, qg_rope = w["b_q_gain"][:, :HEAD], _pad_lanes(w["b_q_gain"][:, HEAD:])
    cos, sin, perm = _rope_tables(lp)

    meta = jnp.broadcast_to(w["meta_tokens"].T[None], (nb, N_META, d))
    h0 = jnp.concatenate([jnp.zeros((nb, PAD_ROWS, d), F32), meta, x], 1).reshape(t, d)
    (hn,) = row_call("a_norm_fwd", _f_rms, [Arg(h0), Arg(a_norm, "par")], [(d, mxu, d, False)], tr)
    z_qkv = matmul("a_in_qkv", hn, w_qkv_t, "nt")
    gate_a = matmul("a_in_gate", hn, w_ga_t, "nt")
    z_ba = matmul("a_in_ba", hn, w_ba_t, "nt")
    qa = conv_fwd("a_conv_q", z_qkv, a_conv, "q", lp)
    ka = conv_fwd("a_conv_k", z_qkv, a_conv, "k", lp)
    va = conv_fwd("a_conv_v", z_qkv, a_conv, "v", lp)
    o_a, states, t_invs = delta_fwd(qa, ka, va, z_ba, alog, dtb, lp)
    og_args = [Arg(o_a, bc=HEAD, ph=True, diff=True), Arg(gate_a, bc=HEAD, ph=True, diff=True), Arg(o_gain, "par", diff=True)]
    (og_a,) = row_call("a_out_gate_fwd", _f_out_gate, og_args, [(kw, mxu, HEAD, True)], tr, nh=N_HEADS)
    h1 = matmul("a_out", og_a, a_w_out, "nn", res=h0)

    hk, hb = row_call("b_norms_fwd", _f_rms2, [Arg(h1), Arg(kv_norm, "par"), Arg(b_norm, "par")], [(d, mxu, d, False), (d, mxu, d, False)], tr)
    c_kv_raw = matmul("kv_down", hk, w_dkv, "nn")
    k_pe = matmul("kv_down_pe", hk, w_dpe, "nn")
    c_q_raw = matmul("b_in_q", hb, w_cq_t, "nt")
    gate_b = matmul("b_in_gate", hb, w_gb_t, "nt")
    (c_kv,) = row_call("kv_latent_fwd", _f_rms, [Arg(c_kv_raw), Arg(lat_norm, "par")], [(KV_RANK, mxu, KV_RANK, False)], tr)
    (c_q,) = row_call("q_latent_fwd", _f_rms, [Arg(c_q_raw), Arg(qlat_norm, "par")], [(Q_RANK, mxu, Q_RANK, False)], tr)
    kv_up = matmul("kv_up", c_kv, w_ukv_t, "nt")
    q_up = matmul("q_up", c_q, w_q_t, "nt")
    tabs = [Arg(cos, "tab"), Arg(sin, "tab"), Arg(perm, "par")]
    k_args = [Arg(kv_up, bc=HEAD, ph=True, diff=True), Arg(k_pe, diff=True), Arg(kg_nope, "par", diff=True), Arg(kg_rope, "par", diff=True)] + tabs
    q_args = [Arg(q_up, bc=HEAD, ph=True, diff=True), Arg(q_up, bc=HEAD, base=N_HEADS, ph=True, diff=True),
              Arg(qg_nope, "par", diff=True), Arg(qg_rope, "par", diff=True)] + tabs
    (k_fin,) = row_call("k_final_fwd", _f_qk_final, k_args, [(N_HEADS * QK_PAD, mxu, QK_PAD, True)], tr, nh=N_HEADS, ntab=ntab)
    (q_fin,) = row_call("q_final_fwd", _f_qk_final, q_args, [(N_HEADS * QK_PAD, mxu, QK_PAD, True)], tr, nh=N_HEADS, ntab=ntab)
    v_b = kv_up[:, kw:].astype(mxu)
    o_b, lse = flash_fwd(q_fin, k_fin, v_b, lp)
    gb_args = [Arg(o_b, diff=True), Arg(gate_b, diff=True)]
    (og_b,) = row_call("b_gate_fwd", _f_gate, gb_args, [(kw, mxu, kw, False)], tr)
    h2 = matmul("b_out", og_b, b_w_out, "nn", res=h1)

    loss, dh2 = loss_head(h2.reshape(nb, lp, d), target, lp)
    dh2 = dh2.reshape(t, d)
    grads = {}

    d_og_b = matmul("b_out_dx", dh2, b_w_out, "nt")
    grads["b_w_out"] = matmul("b_out_dw", og_b, dh2, "tn")
    d_o_b, d_gate_b = row_vjp_call("b_gate_bwd", _f_gate, gb_args, [Arg(d_og_b)], tr)
    dq_fin, dk_fin, dv_b = flash_bwd(q_fin, k_fin, v_b, o_b, lse, d_o_b, lp)
    dq_nope, dq_rope, d_qg_nope, d_qg_rope = row_vjp_call(
        "q_final_bwd", _f_qk_final, q_args, [Arg(dq_fin, bc=QK_PAD, ph=True)], tr, nh=N_HEADS, ntab=ntab)
    dk_nope, dk_pe, d_kg_nope, d_kg_rope = row_vjp_call(
        "k_final_bwd", _f_qk_final, k_args, [Arg(dk_fin, bc=QK_PAD, ph=True)], tr, nh=N_HEADS, ntab=ntab)
    grads["b_q_gain"] = jnp.concatenate([d_qg_nope, d_qg_rope[:, :ROPE]], 1)
    grads["k_gain"] = jnp.concatenate([d_kg_nope, d_kg_rope[:, :ROPE]], 1)[0]
    dq_up = jnp.concatenate([dq_nope, dq_rope], 1)
    dkv_up = jnp.concatenate([dk_nope, dv_b], 1)
    d_c_q = matmul("q_up_dx", dq_up, w_q_t, "nn")
    grads["b_w_uq"] = _merge_heads_qk_t(matmul("q_up_dw", dq_up, c_q, "tn"))
    d_c_kv = matmul("kv_up_dx", dkv_up, w_ukv_t, "nn")
    d_w_ukv_t = matmul("kv_up_dw", dkv_up, c_kv, "tn")
    grads["kv_w_uk"], grads["kv_w_uv"] = d_w_ukv_t[:kw], d_w_ukv_t[kw:]
    d_c_q_raw, grads["b_q_latent_norm"] = row_vjp_call(
        "q_latent_bwd", _f_rms, [Arg(c_q_raw, diff=True), Arg(qlat_norm, "par", diff=True)], [Arg(d_c_q)], tr)
    d_c_kv_raw, d_lat = row_vjp_call(
        "kv_latent_bwd", _f_rms, [Arg(c_kv_raw, diff=True), Arg(lat_norm, "par", diff=True)], [Arg(d_c_kv)], tr)
    grads["kv_latent_norm"] = d_lat[0]
    d_hb = matmul("b_in_q_dx", d_c_q_raw, w_cq_t, "nn")
    d_hb = matmul("b_in_gate_dx", d_gate_b, w_gb_t, "nn", res=d_hb)
    grads["b_w_in"] = jnp.concatenate([matmul("b_in_q_dw", d_c_q_raw, hb, "tn"), matmul("b_in_gate_dw", d_gate_b, hb, "tn")], 0)
    d_hk = matmul("kv_down_dx", d_c_kv_raw, w_dkv, "nt")
    d_hk = matmul("kv_down_pe_dx", dk_pe, w_dpe, "nt", res=d_hk)
    grads["kv_w_down"] = jnp.concatenate([matmul("kv_down_dw", hk, d_c_kv_raw, "tn"), matmul("kv_down_pe_dw", hk, dk_pe, "tn")[:, :ROPE]], 1)
    d_h1_norms, d_kv_norm, grads["b_norm"] = row_vjp_call(
        "b_norms_bwd", _f_rms2, [Arg(h1, diff=True), Arg(kv_norm, "par", diff=True), Arg(b_norm, "par", diff=True)], [Arg(d_hk), Arg(d_hb)], tr)
    grads["kv_norm"] = d_kv_norm[0]

    (dh1,) = row_call("dh1_sum", lambda a, b: (a + b,), [Arg(dh2), Arg(d_h1_norms)], [(d, F32, d, False)], tr)
    d_og_a = matmul("a_out_dx", dh1, a_w_out, "nt")
    grads["a_w_out"] = matmul("a_out_dw", og_a, dh1, "tn")
    d_o_a, d_gate_a, grads["a_o_gain"] = row_vjp_call(
        "a_out_gate_bwd", _f_out_gate, og_args, [Arg(d_og_a, bc=HEAD, ph=True)], tr, nh=N_HEADS, head_inner=True)
    dqa, dka, dva, d_ba_groups, d_alog, d_dtb = delta_bwd(qa, ka, va, z_ba, alog, dtb, states, t_invs, d_o_a, lp)
    (d_ba,) = row_call("a_dba_sum", lambda *parts: (sum(parts[1:], parts[0]),),
                       [Arg(d_ba_groups, bc=HEAD, base=g) for g in range(N_HEADS // DELTA_HEADS)], [(HEAD, F32, HEAD, False)], tr)
    grads["a_log"], grads["a_dt_bias"] = d_alog[:, :N_HEADS], d_dtb[:, :N_HEADS]
    dz_q, dw_q = conv_bwd("a_conv_q_bwd", z_qkv, a_conv, dqa, "q", lp)
    dz_k, dw_k = conv_bwd("a_conv_k_bwd", z_qkv, a_conv, dka, "k", lp)
    dz_v, dw_v = conv_bwd("a_conv_v_bwd", z_qkv, a_conv, dva, "v", lp)
    grads["a_conv"] = jnp.concatenate([dw_q, dw_k, dw_v], 1).T
    dz_qkv = jnp.concatenate([dz_q, dz_k, dz_v], 1)
    d_hn = matmul("a_in_qkv_dx", dz_qkv, w_qkv_t, "nn")
    d_hn = matmul("a_in_gate_dx", d_gate_a, w_ga_t, "nn", res=d_hn)
    d_hn = matmul("a_in_ba_dx", d_ba, w_ba_t, "nn", res=d_hn)
    grads["a_w_in"] = jnp.concatenate([matmul("a_in_qkv_dw", dz_qkv, hn, "tn"), matmul("a_in_gate_dw", d_gate_a, hn, "tn"),
                                       matmul("a_in_ba_dw", d_ba, hn, "tn")[:2 * N_HEADS]], 0)
    d_h0_norm, grads["a_norm"] = row_vjp_call("a_norm_bwd", _f_rms, [Arg(h0, diff=True), Arg(a_norm, "par", diff=True)], [Arg(d_hn)], tr)
    (dh0,) = row_call("dh0_sum", lambda a, b: (a + b,), [Arg(dh1), Arg(d_h0_norm)], [(d, F32, d, False)], tr)
    dh0 = dh0.reshape(nb, lp, d)
    grads["meta_tokens"] = meta_grad(dh0).T
    return loss, dh0[:, LEAD:], grads


_SHARDED = (
    ("meta_tokens", True, False), ("a_norm", True, False), ("a_w_in", True, True), ("a_conv", True, False), ("a_w_out", False, True),
    ("kv_w_down", False, True), ("kv_w_uk", True, True), ("kv_w_uv", True, True), ("b_w_in", True, True), ("b_w_uq", True, True),
    ("b_w_out", False, True))
_REPLICATED = ("a_log", "a_dt_bias", "a_o_gain", "kv_norm", "kv_latent_norm", "k_gain", "b_norm", "b_q_latent_norm", "b_q_gain")
_ALL_WEIGHTS = ("meta_tokens", "a_norm", "a_w_in", "a_conv", "a_log", "a_dt_bias", "a_o_gain", "a_w_out", "kv_norm", "kv_w_down",
                "kv_latent_norm", "kv_w_uk", "kv_w_uv", "k_gain", "b_norm", "b_w_in", "b_q_latent_norm", "b_w_uq", "b_q_gain", "b_w_out")


def _round_up(n, m):
    return (n + m - 1) // m * m


def _pack_rows(pieces, row_multiple):
    padded = []
    for p in pieces:
        n = p.shape[-1]
        padded.append(jnp.pad(p, [(0, 0)] * (p.ndim - 1) + [(0, _round_up(n, PACK_COLS) - n)]))
    flat = jnp.concatenate(padded, -1)
    rows = _round_up(flat.shape[-1] // PACK_COLS, row_multiple)
    flat = jnp.pad(flat, [(0, 0)] * (flat.ndim - 1) + [(0, rows * PACK_COLS - flat.shape[-1])])
    return flat.reshape(flat.shape[:-1] + (rows, PACK_COLS))


def _unpack_rows(buf, sizes):
    flat = buf.reshape(buf.shape[:-2] + (-1,))
    out, off = [], 0
    for n in sizes:
        out.append(flat[..., off:off + n])
        off += _round_up(n, PACK_COLS)
    return out


def _as_bf16_pairs(a):
    return lax.bitcast_convert_type(a, BF16).reshape(-1)


def _from_bf16_pairs(a):
    return lax.bitcast_convert_type(a.reshape(a.shape[:-1] + (-1, 2)), F32)


def _shard_2d(a):
    return a.reshape(a.shape[-2:]) if a.ndim > 2 else a


def gather_weights(local):
    pieces, shapes = [], []
    for name, by_cols, narrow in _SHARDED:
        shard = _shard_2d(local[name])
        shard = shard.T if by_cols else shard
        shapes.append(shard.shape)
        flat = shard.reshape(-1)
        pieces.append(flat.astype(BF16) if narrow else _as_bf16_pairs(flat))
    gathered = all_gather(_pack_rows(pieces, 16))
    parts = _unpack_rows(gathered, [p.shape[0] for p in pieces])
    full = {}
    for (name, _, narrow), part, (rows, cols) in zip(_SHARDED, parts, shapes, strict=True):
        vals = part if narrow else _from_bf16_pairs(part)
        full[name] = vals.reshape(N_DEV * rows, cols)
    full["a_norm"] = full["a_norm"].reshape(1, -1)
    return full


def reduce_contributions(recv):
    _, r, c = recv.shape
    tr = _pick(r, (256, 128, 64, 32, 16, 8))

    def body(g_ref, o_ref):
        g = g_ref[0]
        for dev in range(1, N_DEV):
            g = g + g_ref[dev]
        o_ref[...] = g

    return pl.pallas_call(
        body, grid=(r // tr,), in_specs=[pl.BlockSpec((N_DEV, tr, c), lambda i: (0, i, 0))], out_specs=pl.BlockSpec((tr, c), lambda i: (i, 0)),
        out_shape=jax.ShapeDtypeStruct((r, c), F32), compiler_params=_cparams(("arbitrary",)), name="reduce_contributions")(recv)


def adamw(g, w, m, v):
    r, c = g.shape
    tr = _pick(r, (256, 128, 64, 32, 16, 8))

    def body(g_ref, w_ref, m_ref, v_ref, d_ref, mo_ref, vo_ref):
        g_ = g_ref[...]
        m_new = ADAM_B1 * m_ref[...] + (1.0 - ADAM_B1) * g_
        v_new = ADAM_B2 * v_ref[...] + (1.0 - ADAM_B2) * (g_ * g_)
        m_hat = m_new / (1.0 - ADAM_B1 ** ADAM_STEP)
        v_hat = v_new / (1.0 - ADAM_B2 ** ADAM_STEP)
        d_ref[...] = -ADAM_LR * (m_hat / (jnp.sqrt(v_hat) + ADAM_EPS) + ADAM_WD * w_ref[...])
        mo_ref[...] = m_new
        vo_ref[...] = v_new

    blk = pl.BlockSpec((tr, c), lambda i: (i, 0))
    out = jax.ShapeDtypeStruct((r, c), F32)
    return pl.pallas_call(body, grid=(r // tr,), in_specs=[blk, blk, blk, blk], out_specs=[blk, blk, blk], out_shape=[out, out, out],
                          compiler_params=_cparams(("arbitrary",)), name="adamw")(g, w, m, v)


def kernel(x, meta_tokens, a_norm, a_w_in, a_conv, a_log, a_dt_bias, a_o_gain, a_w_out, kv_norm, kv_w_down, kv_latent_norm, kv_w_uk, kv_w_uv, k_gain, b_norm, b_w_in, b_q_latent_norm, b_w_uq, b_q_gain, b_w_out, loss_target, m_meta_tokens, m_a_norm, m_a_w_in, m_a_conv, m_a_log, m_a_dt_bias, m_a_o_gain, m_a_w_out, m_kv_norm, m_kv_w_down, m_kv_latent_norm, m_kv_w_uk, m_kv_w_uv, m_k_gain, m_b_norm, m_b_w_in, m_b_q_latent_norm, m_b_w_uq, m_b_q_gain, m_b_w_out, v_meta_tokens, v_a_norm, v_a_w_in, v_a_conv, v_a_log, v_a_dt_bias, v_a_o_gain, v_a_w_out, v_kv_norm, v_kv_w_down, v_kv_latent_norm, v_kv_w_uk, v_kv_w_uv, v_k_gain, v_b_norm, v_b_w_in, v_b_q_latent_norm, v_b_w_uq, v_b_q_gain, v_b_w_out):
    given = dict(locals())
    local_w = {n: given[n] for n in _ALL_WEIGHTS}
    full = gather_weights(local_w)
    for n in _REPLICATED:
        full[n] = local_w[n]

    loss_part, grad_x, grads = local_step(x, loss_target, full)

    pieces = [grads[n].reshape(N_DEV, -1) for n, _, _ in _SHARDED]
    pieces += [jnp.broadcast_to(grads[n].reshape(1, -1), (N_DEV, grads[n].size)) for n in _REPLICATED]
    pieces.append(jnp.broadcast_to(loss_part, (N_DEV, 1)))
    summed = reduce_contributions(all_to_all(_pack_rows(pieces, 256)))
    parts = _unpack_rows(summed, [p.shape[1] for p in pieces])
    loss = parts[-1][0]

    order = [n for n, _, _ in _SHARDED] + list(_REPLICATED)
    grad_local = {}
    for (n, by_cols, _), part in zip(_SHARDED, parts, strict=False):
        rows, cols = _shard_2d(local_w[n]).shape
        g2 = part.reshape(cols, rows).T if by_cols else part.reshape(rows, cols)
        grad_local[n] = g2.reshape(local_w[n].shape)
    for n, part in zip(_REPLICATED, parts[len(_SHARDED):], strict=False):
        grad_local[n] = part.reshape(local_w[n].shape)

    def pack_local(tree):
        return _pack_rows([tree[n].reshape(-1) for n in order], 256)

    packed = adamw(pack_local(grad_local), pack_local(local_w), pack_local({n: given["m_" + n] for n in order}),
                   pack_local({n: given["v_" + n] for n in order}))
    results = [grad_local[n] for n in _ALL_WEIGHTS]
    for buf in packed:
        by_name = {n: p.reshape(local_w[n].shape) for n, p in zip(order, _unpack_rows(buf, [local_w[n].size for n in order]), strict=True)}
        results.extend(by_name[n] for n in _ALL_WEIGHTS)
    return (loss, grad_x, *results)
```

```python
import dataclasses
import functools
import math

import jax
import jax.numpy as jnp
from jax import lax
from jax.experimental import pallas as pl
from jax.experimental.pallas import tpu as pltpu

F32 = jnp.float32
BF16 = jnp.bfloat16
_MXU_DTYPE = jnp.bfloat16
_HI = lax.Precision.HIGHEST

N_DEV = 8
D_MODEL = 1024
N_HEADS = 8
HEAD = 128
CHUNK = 64
N_META = 16
PAD_ROWS = 2 * CHUNK - N_META
LEAD = PAD_ROWS + N_META
ROPE = 64
QK_DIM = HEAD + ROPE
QK_PAD = 2 * HEAD
KV_RANK = 256
Q_RANK = 384
CONV_K = 4
EPS = 1e-6
NEG = -1e30
ROPE_THETA = 10000.0
ADAM_LR, ADAM_B1, ADAM_B2, ADAM_EPS, ADAM_WD, ADAM_STEP = 0.001, 0.9, 0.999, 1e-08, 0.01, 10
PACK_COLS = 512
VMEM_LIMIT = 56 * 1024 * 1024


def _pick(n, options):
    for o in options:
        if n % o == 0:
            return o
    raise ValueError(f"no tile for {n} among {options}")


def _cparams(sem):
    return pltpu.CompilerParams(dimension_semantics=sem, vmem_limit_bytes=VMEM_LIMIT)


def _dims(a, dims):
    if a.ndim == 2:
        return (dims, ((), ()))
    (ca,), (cb,) = dims
    return (((ca + 1,), (cb + 1,)), ((0,), (0,)))


def _dot(a, b, dims):
    return lax.dot_general(a.astype(_MXU_DTYPE), b.astype(_MXU_DTYPE), _dims(a, dims), preferred_element_type=F32)


@jax.custom_vjp
def mm_nn(a, b):
    return _dot(a, b, ((1,), (0,)))


@jax.custom_vjp
def mm_nt(a, b):
    return _dot(a, b, ((1,), (1,)))


@jax.custom_vjp
def mm_tn(a, b):
    return _dot(a, b, ((0,), (0,)))


mm_nn.defvjp(lambda a, b: (mm_nn(a, b), (a, b)), lambda r, g: (mm_nt(g, r[1]), mm_tn(r[0], g)))
mm_nt.defvjp(lambda a, b: (mm_nt(a, b), (a, b)), lambda r, g: (mm_nn(g, r[1]), mm_tn(g, r[0])))
mm_tn.defvjp(lambda a, b: (mm_tn(a, b), (a, b)), lambda r, g: (mm_nt(r[1], g), mm_nn(r[0], g)))


def _dot_f32(a, b):
    return lax.dot_general(a, b, _dims(a, ((1,), (0,))), precision=_HI, preferred_element_type=F32)


def _split_hi_lo(x):
    hi = x.astype(_MXU_DTYPE)
    lo = (x - hi.astype(F32)).astype(_MXU_DTYPE)
    return hi, lo


def _mm_3pass(a, b):
    ah, al = _split_hi_lo(a)
    bh, bl = _split_hi_lo(b)
    d = lambda u, w: lax.dot_general(u, w, _dims(u, ((1,), (0,))), preferred_element_type=F32)
    return d(ah, bh) + (d(ah, bl) + d(al, bh))


def _inv_unit_lower(a):
    n = a.shape[-1]
    eye = (lax.broadcasted_iota(jnp.int32, (n, n), 0) == lax.broadcasted_iota(jnp.int32, (n, n), 1)).astype(F32)
    t = eye - a
    p = _mm_3pass(a, a)
    squarings = int(math.log2(n)) - 1
    for s in range(squarings):
        t = t + _mm_3pass(t, p)
        if s + 1 < squarings:
            p = _mm_3pass(p, p)
    return t


@jax.custom_vjp
def _inv_lookup(a, t):
    return t


def _inv_lookup_bwd(t, g):
    return -mm_tn(t, mm_nt(g, t)), jnp.zeros_like(t)


_inv_lookup.defvjp(lambda a, t: (t, t), _inv_lookup_bwd)


def _sigmoid(x):
    return 1.0 / (1.0 + jnp.exp(-x))


def _silu(x):
    return x * _sigmoid(x)


def _softplus(x):
    return jnp.where(x > 20.0, x, jnp.log(1.0 + jnp.exp(jnp.minimum(x, 20.0))))


def _rms(x, g, width=None):
    ms = jnp.sum(x * x, -1, keepdims=True) / (x.shape[-1] if width is None else width)
    return x * lax.rsqrt(ms + EPS) * g


MM_VMEM_BUDGET = 40 * 1024 * 1024


def _matmul_rows(name, a, b, mode, out_dtype, res):
    m, k = a.shape
    n = b.shape[1] if mode == "nn" else b.shape[0]
    dims = {"nn": ((1,), (0,)), "nt": ((1,), (1,))}[mode]
    out_bytes = jnp.dtype(out_dtype).itemsize

    def vmem(tm):
        blocks = 2 * tm * k * a.dtype.itemsize + 2 * k * n * b.dtype.itemsize + 2 * tm * n * out_bytes + tm * n * 4
        return blocks + (2 * tm * n * res.dtype.itemsize if res is not None else 0)

    tm = next(c for c in (2176, 1088, 512, 256, 128, 64) if m % c == 0 and vmem(c) <= MM_VMEM_BUDGET)

    def body(*refs):
        a_ref, b_ref = refs[:2]
        out = _dot(a_ref[...], b_ref[...], dims)
        if res is not None:
            out = out + refs[2][...].astype(F32)
        refs[-1][...] = out.astype(refs[-1].dtype)

    o_spec = pl.BlockSpec((tm, n), lambda i: (i, 0))
    in_specs = [pl.BlockSpec((tm, k), lambda i: (i, 0)), pl.BlockSpec(b.shape, lambda i: (0, 0))] + ([o_spec] if res is not None else [])
    args = (a, b) + ((res,) if res is not None else ())
    return pl.pallas_call(body, grid=(m // tm,), in_specs=in_specs, out_specs=o_spec, out_shape=jax.ShapeDtypeStruct((m, n), out_dtype),
                          compiler_params=_cparams(("parallel",)), name=name)(*args)


def matmul(name, a, b, mode, out_dtype=F32, res=None):
    if mode != "tn":
        return _matmul_rows(name, a, b, mode, out_dtype, res)
    (k, m), (k2, n) = a.shape, b.shape
    assert k == k2 and res is None, (name, a.shape, b.shape, mode)
    tm = _pick(m, (1024, 512, 384, 256, 128))
    tn = _pick(n, (1024, 512, 384, 256, 128))
    tk = _pick(k, (512, 256, 128))
    nk = k // tk
    dims = ((0,), (0,))

    def body(*refs):
        if res is None:
            a_ref, b_ref, o_ref, acc_ref = refs
        else:
            a_ref, b_ref, r_ref, o_ref, acc_ref = refs
        kk = pl.program_id(2)

        @pl.when(kk == 0)
        def _():
            acc_ref[...] = jnp.zeros_like(acc_ref)

        acc_ref[...] += _dot(a_ref[...], b_ref[...], dims)

        @pl.when(kk == nk - 1)
        def _():
            out = acc_ref[...]
            if res is not None:
                out = out + r_ref[...].astype(F32)
            o_ref[...] = out.astype(o_ref.dtype)

    a_spec = pl.BlockSpec((tk, tm), lambda i, j, kk: (kk, i)) if mode == "tn" else pl.BlockSpec((tm, tk), lambda i, j, kk: (i, kk))
    b_spec = pl.BlockSpec((tn, tk), lambda i, j, kk: (j, kk)) if mode == "nt" else pl.BlockSpec((tk, tn), lambda i, j, kk: (kk, j))
    o_spec = pl.BlockSpec((tm, tn), lambda i, j, kk: (i, j))
    in_specs = [a_spec, b_spec] + ([o_spec] if res is not None else [])
    args = (a, b) + ((res,) if res is not None else ())
    return pl.pallas_call(
        body, grid=(m // tm, n // tn, nk), in_specs=in_specs, out_specs=o_spec,
        out_shape=jax.ShapeDtypeStruct((m, n), out_dtype), scratch_shapes=[pltpu.VMEM((tm, tn), F32)],
        compiler_params=_cparams(("parallel", "parallel", "arbitrary")), name=name)(*args)


@dataclasses.dataclass
class Arg:
    arr: jax.Array
    kind: str = "row"
    bc: int = 0
    base: int = 0
    ph: bool = False
    diff: bool = False
    gdt: object = F32


def _arg_spec(a, tr, nh, ntab, base=None):
    bc = a.bc or a.arr.shape[1]
    base = a.base if base is None else base
    width = bc * nh if a.ph else bc
    col = base // nh if a.ph else base
    assert not a.ph or base % nh == 0
    if a.kind == "row":
        return pl.BlockSpec((tr, width), lambda i: (i, col))
    if a.kind == "tab":
        return pl.BlockSpec((tr, width), lambda i: (i % ntab, col))
    return pl.BlockSpec((a.arr.shape[0], width), lambda i: (0, col))


def _head_view(ref, a, h):
    bc = a.bc or a.arr.shape[1]
    v = ref[:, h * bc:(h + 1) * bc] if a.ph else ref[...]
    return v.astype(F32) if jnp.issubdtype(v.dtype, jnp.floating) else v


def row_call(name, fn, args, outs, tr, nh=1, ntab=1):
    t = args[0].arr.shape[0]
    n_in = len(args)
    out_args = [Arg(None, "row", bc, 0, ph) for (_, _, bc, ph) in outs]

    def body(*refs):
        for h in range(nh):
            res = fn(*[_head_view(r, a, h) for r, a in zip(refs[:n_in], args, strict=True)])
            for r, a, v in zip(refs[n_in:], out_args, res, strict=True):
                if a.ph:
                    r[:, h * a.bc:(h + 1) * a.bc] = v.astype(r.dtype)
                elif h == nh - 1:
                    r[...] = v.astype(r.dtype)

    return pl.pallas_call(
        body, grid=(t // tr,), in_specs=[_arg_spec(a, tr, nh, ntab) for a in args], out_specs=[_arg_spec(a, tr, nh, ntab) for a in out_args],
        out_shape=[jax.ShapeDtypeStruct((t, cols), dt) for (cols, dt, _, _) in outs],
        compiler_params=_cparams(("arbitrary",)), name=name)(*[a.arr for a in args])


def row_vjp_call(name, fn, args, cts, tr, nh=1, ntab=1):
    t = args[0].arr.shape[0]
    n_in, n_ct = len(args), len(cts)
    diff_idx = [k for k, a in enumerate(args) if a.diff]

    def body(*refs):
        out_refs = refs[n_in + n_ct:]
        shared = [None] * len(diff_idx)
        for k, r in zip(diff_idx, out_refs, strict=True):
            if args[k].kind == "par":
                @pl.when(pl.program_id(0) == 0)
                def _(r=r):
                    r[...] = jnp.zeros_like(r)

        for h in range(nh):
            vals = [_head_view(r, a, h) for r, a in zip(refs[:n_in], args, strict=True)]
            ct_vals = tuple(_head_view(r, a, h) for r, a in zip(refs[n_in:n_in + n_ct], cts, strict=True))

            def f(*dv, vals=vals):
                full = list(vals)
                for k, v in zip(diff_idx, dv, strict=True):
                    full[k] = v
                return tuple(fn(*full))

            _, vjp = jax.vjp(f, *[vals[k] for k in diff_idx])
            for j, (k, r, g) in enumerate(zip(diff_idx, out_refs, vjp(ct_vals), strict=True)):
                a = args[k]
                bc = a.bc or a.arr.shape[1]
                if not a.ph:
                    shared[j] = g if shared[j] is None else shared[j] + g
                elif a.kind == "row":
                    r[:, h * bc:(h + 1) * bc] = g.astype(r.dtype)
                else:
                    r[:, h * bc:(h + 1) * bc] += g
        for j, (k, r) in enumerate(zip(diff_idx, out_refs, strict=True)):
            if not args[k].ph:
                if args[k].kind == "row":
                    r[...] = shared[j].astype(r.dtype)
                else:
                    r[...] += shared[j]

    out_specs, out_shape = [], []
    for k in diff_idx:
        a = args[k]
        bc = a.bc or a.arr.shape[1]
        out_specs.append(_arg_spec(a, tr, nh, ntab, base=0))
        out_shape.append(jax.ShapeDtypeStruct((t if a.kind == "row" else a.arr.shape[0], bc * (nh if a.ph else 1)), a.gdt if a.kind == "row" else F32))
    in_specs = [_arg_spec(a, tr, nh, ntab) for a in list(args) + list(cts)]
    return pl.pallas_call(
        body, grid=(t // tr,), in_specs=in_specs, out_specs=out_specs, out_shape=out_shape,
        compiler_params=_cparams(("arbitrary",)), name=name)(*[a.arr for a in list(args) + list(cts)])


def _conv_taps(x, w):
    rows = lax.broadcasted_iota(jnp.int32, x.shape, 0)
    y = x * w[CONV_K - 1:CONV_K, :]
    shifted = []
    for s in range(1, CONV_K):
        xs = jnp.where(rows >= s, pltpu.roll(x, s, 0), 0.0)
        shifted.append(xs)
        y = y + xs * w[CONV_K - 1 - s:CONV_K - s, :]
    return y, shifted


CONV_HEADS = 4
CONV_BLOCKS_PER_THIRD = N_HEADS // CONV_HEADS


def _conv_post(y, block):
    a = _silu(y)
    normed = block < 2 * CONV_BLOCKS_PER_THIRD
    scale = jnp.where(block < CONV_BLOCKS_PER_THIRD, HEAD ** -0.5, 1.0)
    return a * jnp.where(normed, lax.rsqrt(jnp.sum(a * a, -1, keepdims=True) + EPS) * scale, 1.0)


def conv_fwd(z, w, lp):
    t, width = z.shape
    cols = CONV_HEADS * HEAD

    def body(z_ref, w_ref, o_ref):
        block = pl.program_id(1)
        for h in range(CONV_HEADS):
            cs = slice(h * HEAD, (h + 1) * HEAD)
            y, _ = _conv_taps(z_ref[:, cs], w_ref[:, cs])
            o_ref[:, cs] = _conv_post(y, block)

    return pl.pallas_call(
        body, grid=(t // lp, width // cols),
        in_specs=[pl.BlockSpec((lp, cols), lambda b, j: (b, j)), pl.BlockSpec((CONV_K, cols), lambda b, j: (0, j))],
        out_specs=pl.BlockSpec((lp, cols), lambda b, j: (b, j)), out_shape=jax.ShapeDtypeStruct((t, width), F32),
        compiler_params=_cparams(("arbitrary", "arbitrary")), name="a_conv_fwd")(z, w)


def conv_bwd(z, w, dout, lp):
    t, width = z.shape
    cols = CONV_HEADS * HEAD

    def body(z_ref, w_ref, g_ref, dz_ref, dw_ref):
        block = pl.program_id(0)

        @pl.when(pl.program_id(1) == 0)
        def _():
            dw_ref[...] = jnp.zeros_like(dw_ref)

        for h in range(CONV_HEADS):
            cs = slice(h * HEAD, (h + 1) * HEAD)
            x, wv = z_ref[:, cs], w_ref[:, cs]
            y, shifted = _conv_taps(x, wv)
            _, vjp = jax.vjp(lambda y_: _conv_post(y_, block), y)
            (dy,) = vjp(g_ref[:, cs])
            rows = lax.broadcasted_iota(jnp.int32, x.shape, 0)
            dx = dy * wv[CONV_K - 1:CONV_K, :]
            for s in range(1, CONV_K):
                dx = dx + jnp.where(rows < lp - s, pltpu.roll(dy, lp - s, 0), 0.0) * wv[CONV_K - 1 - s:CONV_K - s, :]
            dz_ref[:, cs] = dx.astype(dz_ref.dtype)
            dw_ref[CONV_K - 1:CONV_K, cs] += jnp.sum(dy * x, axis=0, keepdims=True)
            for s in range(1, CONV_K):
                dw_ref[CONV_K - 1 - s:CONV_K - s, cs] += jnp.sum(dy * shifted[s - 1], axis=0, keepdims=True)

    blk = pl.BlockSpec((lp, cols), lambda j, b: (b, j))
    w_blk = pl.BlockSpec((CONV_K, cols), lambda j, b: (0, j))
    return pl.pallas_call(
        body, grid=(width // cols, t // lp), in_specs=[blk, w_blk, blk], out_specs=[blk, w_blk],
        out_shape=[jax.ShapeDtypeStruct((t, width), _MXU_DTYPE), jax.ShapeDtypeStruct((CONV_K, width), F32)],
        compiler_params=_cparams(("arbitrary", "arbitrary")), name="a_conv_bwd")(z, w, dout)


def _delta_chunk(q, k, v, ba, alog, dtb, state, t_stored, h0):
    n_g, c = q.shape[0], q.shape[1]
    lane = lax.broadcasted_iota(jnp.int32, (1, HEAD), 1)

    def pick(x, offset):
        return jnp.stack([jnp.sum(x * (lane == offset + h0 + g).astype(F32), axis=1, keepdims=True) for g in range(n_g)])

    b_raw, a_raw = pick(ba, 0), pick(ba, N_HEADS)
    a_log, dt_bias = pick(alog, 0), pick(dtb, 0)
    beta = _sigmoid(b_raw)
    g = -jnp.exp(a_log) * _softplus(a_raw + dt_bias)
    ri = lax.broadcasted_iota(jnp.int32, (c, c), 0)
    ci = lax.broadcasted_iota(jnp.int32, (c, c), 1)
    tril = ci <= ri
    lower = jnp.broadcast_to(tril.astype(F32), (n_g, c, c))
    gc_col = _dot_f32(lower, g * jnp.ones((1, 1, HEAD), F32))[:, :, :1]
    gc_row = _dot_f32(jnp.ones((n_g, 8, c), F32), g * (ri <= ci).astype(F32)[None])[:, 0:1, :]
    gc_last = jnp.sum(g, axis=1, keepdims=True)
    decay = jnp.exp(jnp.where(tril, gc_col - gc_row, NEG))
    e_gc = jnp.exp(gc_col)
    kb = k * beta
    a_mat = jnp.where(ci < ri, mm_nt(kb, k) * decay, 0.0)
    t_inv = _inv_unit_lower(a_mat) if t_stored is None else _inv_lookup(a_mat, t_stored)
    u_base = mm_nn(t_inv, v * beta)
    w_dec = mm_nn(t_inv, kb * e_gc)
    attn = jnp.where(tril, mm_nt(q, k) * decay, 0.0)
    u = u_base - mm_nn(w_dec, state)
    o = mm_nn(q * e_gc, state) + mm_nn(attn, u)
    new_state = state * jnp.exp(gc_last) + mm_tn(k * jnp.exp(gc_last - gc_col), u)
    return o, new_state, t_inv


DELTA_CHUNKS_FWD = 2
DELTA_CHUNKS_BWD = 1


def _qkv_heads(ref, rs, part):
    return jnp.stack([ref[rs, (part * N_HEADS + g) * HEAD:(part * N_HEADS + g + 1) * HEAD] for g in range(N_HEADS)])


def delta_fwd(qkv, ba, alog, dtb, lp):
    t = qkv.shape[0]
    nb, nc = t // lp, lp // CHUNK
    cps = DELTA_CHUNKS_FWD
    ng, rows = nc // cps, cps * CHUNK
    assert nc % cps == 0

    def body(qkv_ref, ba_ref, al_ref, dt_ref, o_ref, s_ref, t_ref, state_ref):
        @pl.when(pl.program_id(1) == 0)
        def _():
            state_ref[...] = jnp.zeros_like(state_ref)

        al, dtv = al_ref[...], dt_ref[...]
        for c in range(cps):
            rs = slice(c * CHUNK, (c + 1) * CHUNK)
            state = state_ref[...]
            o, new_state, t_inv = _delta_chunk(_qkv_heads(qkv_ref, rs, 0), _qkv_heads(qkv_ref, rs, 1), _qkv_heads(qkv_ref, rs, 2),
                                               ba_ref[rs, :], al, dtv, state, None, 0)
            for g in range(N_HEADS):
                o_ref[rs, g * HEAD:(g + 1) * HEAD] = o[g]
                s_ref[g, c] = state[g]
                t_ref[g, c] = t_inv[g]
            state_ref[...] = new_state

    rows_of = lambda width: pl.BlockSpec((rows, width), lambda b, n: (b * ng + n, 0))
    par_spec = pl.BlockSpec((1, HEAD), lambda b, n: (0, 0))
    return pl.pallas_call(
        body, grid=(nb, ng), in_specs=[rows_of(3 * N_HEADS * HEAD), rows_of(HEAD), par_spec, par_spec],
        out_specs=[rows_of(N_HEADS * HEAD), pl.BlockSpec((None, N_HEADS, cps, HEAD, HEAD), lambda b, n: (b, 0, n, 0, 0)),
                   pl.BlockSpec((None, N_HEADS, cps, CHUNK, CHUNK), lambda b, n: (b, 0, n, 0, 0))],
        out_shape=[jax.ShapeDtypeStruct((t, N_HEADS * HEAD), F32), jax.ShapeDtypeStruct((nb, N_HEADS, nc, HEAD, HEAD), F32),
                   jax.ShapeDtypeStruct((nb, N_HEADS, nc, CHUNK, CHUNK), F32)],
        scratch_shapes=[pltpu.VMEM((N_HEADS, HEAD, HEAD), F32)],
        compiler_params=_cparams(("arbitrary", "arbitrary")), name="delta_fwd")(qkv, ba, alog, dtb)


def delta_bwd(qkv, ba, alog, dtb, states, t_invs, do, lp):
    t = qkv.shape[0]
    nb, nc = t // lp, lp // CHUNK
    cps = DELTA_CHUNKS_BWD
    ng, rows = nc // cps, cps * CHUNK

    def body(qkv_ref, ba_ref, al_ref, dt_ref, s_ref, t_ref, do_ref, dqkv_ref, dba_ref, dal_ref, ddt_ref, dstate_ref):
        b, step = pl.program_id(0), pl.program_id(1)

        @pl.when(step == 0)
        def _():
            dstate_ref[...] = jnp.zeros_like(dstate_ref)

        @pl.when((b == 0) & (step == 0))
        def _():
            dal_ref[...] = jnp.zeros_like(dal_ref)
            ddt_ref[...] = jnp.zeros_like(ddt_ref)

        al, dtv = al_ref[...], dt_ref[...]
        d_al = jnp.zeros((1, HEAD), F32)
        d_dt = jnp.zeros((1, HEAD), F32)
        for c in reversed(range(cps)):
            rs = slice(c * CHUNK, (c + 1) * CHUNK)
            t_n = jnp.stack([t_ref[g, c] for g in range(N_HEADS)])
            s_n = jnp.stack([s_ref[g, c] for g in range(N_HEADS)])
            d_o = jnp.stack([do_ref[rs, g * HEAD:(g + 1) * HEAD] for g in range(N_HEADS)])

            def f(q_, k_, v_, ba_, al_, dt_, s_, t_n=t_n):
                return _delta_chunk(q_, k_, v_, ba_, al_, dt_, s_, t_n, 0)[:2]

            _, vjp = jax.vjp(f, _qkv_heads(qkv_ref, rs, 0), _qkv_heads(qkv_ref, rs, 1), _qkv_heads(qkv_ref, rs, 2), ba_ref[rs, :], al, dtv, s_n)
            grads = vjp((d_o, dstate_ref[...]))
            for part in range(3):
                for g in range(N_HEADS):
                    dqkv_ref[rs, (part * N_HEADS + g) * HEAD:(part * N_HEADS + g + 1) * HEAD] = grads[part][g]
            dba_ref[rs, :] = grads[3]
            d_al, d_dt = d_al + grads[4], d_dt + grads[5]
            dstate_ref[...] = grads[6]
        dal_ref[...] += d_al
        ddt_ref[...] += d_dt

    rows_of = lambda width: pl.BlockSpec((rows, width), lambda b, n: (b * ng + ng - 1 - n, 0))
    par_spec = pl.BlockSpec((1, HEAD), lambda b, n: (0, 0))
    return pl.pallas_call(
        body, grid=(nb, ng),
        in_specs=[rows_of(3 * N_HEADS * HEAD), rows_of(HEAD), par_spec, par_spec,
                  pl.BlockSpec((None, N_HEADS, cps, HEAD, HEAD), lambda b, n: (b, 0, ng - 1 - n, 0, 0)),
                  pl.BlockSpec((None, N_HEADS, cps, CHUNK, CHUNK), lambda b, n: (b, 0, ng - 1 - n, 0, 0)), rows_of(N_HEADS * HEAD)],
        out_specs=[rows_of(3 * N_HEADS * HEAD), rows_of(HEAD), par_spec, par_spec],
        out_shape=[jax.ShapeDtypeStruct((t, 3 * N_HEADS * HEAD), F32), jax.ShapeDtypeStruct((t, HEAD), F32),
                   jax.ShapeDtypeStruct((1, HEAD), F32), jax.ShapeDtypeStruct((1, HEAD), F32)],
        scratch_shapes=[pltpu.VMEM((N_HEADS, HEAD, HEAD), F32)],
        compiler_params=_cparams(("arbitrary", "arbitrary")), name="delta_bwd")(qkv, ba, alog, dtb, states, t_invs, do)


ATT_Q_TILE = 256
ATT_K_TILE = 512
ATT_SCALE = QK_DIM ** -0.5


def _tiles(end, size):
    return [(s, min(s + size, end)) for s in range(0, end, size)]


def _att_visible(q0, q1, k0, k1, keys_first):
    if k1 <= q0 + CHUNK and k0 >= PAD_ROWS:
        return None
    shape = (k1 - k0, q1 - q0) if keys_first else (q1 - q0, k1 - k0)
    qpos = q0 + lax.broadcasted_iota(jnp.int32, shape, 1 if keys_first else 0)
    kpos = k0 + lax.broadcasted_iota(jnp.int32, shape, 0 if keys_first else 1)
    shift = CHUNK.bit_length() - 1
    return (jnp.right_shift(kpos, shift) <= jnp.right_shift(qpos, shift)) & (kpos >= PAD_ROWS)


def _att_seq_specs(lp):
    return pl.BlockSpec((lp, QK_PAD), lambda b, h: (b, h)), pl.BlockSpec((lp, HEAD), lambda b, h: (b, h))


def flash_fwd(q, k, v, lp):
    t = q.shape[0]
    qk_seq, o_seq = _att_seq_specs(lp)

    def body(q_ref, k_ref, v_ref, o_ref, lse_ref):
        for q0, q1 in _tiles(lp, ATT_Q_TILE):
            qb = q_ref[q0:q1, :]
            k_tiles = _tiles(q1, ATT_K_TILE)
            scores, m = [], None
            for k0, k1 in k_tiles:
                s = mm_nt(qb, k_ref[k0:k1, :]) * ATT_SCALE
                vis = _att_visible(q0, q1, k0, k1, False)
                s = s if vis is None else jnp.where(vis, s, NEG)
                scores.append(s)
                row_max = jnp.max(s, -1, keepdims=True)
                m = row_max if m is None else jnp.maximum(m, row_max)
            l = jnp.zeros((q1 - q0, 1), F32)
            acc = jnp.zeros((q1 - q0, HEAD), F32)
            for s, (k0, k1) in zip(scores, k_tiles, strict=True):
                p = jnp.exp(s - m)
                l = l + jnp.sum(p, -1, keepdims=True)
                acc = acc + mm_nn(p, v_ref[k0:k1, :])
            o_ref[q0:q1, :] = acc / l
            lse_ref[q0:q1, :] = jnp.broadcast_to(m + jnp.log(l), (q1 - q0, HEAD))

    big = jax.ShapeDtypeStruct((t, N_HEADS * HEAD), F32)
    return pl.pallas_call(
        body, grid=(t // lp, N_HEADS), in_specs=[qk_seq, qk_seq, o_seq], out_specs=[o_seq, o_seq], out_shape=[big, big],
        compiler_params=_cparams(("arbitrary", "arbitrary")), name="flash_fwd")(q, k, v)


def flash_bwd(q, k, v, o, lse, do, lp):
    t = q.shape[0]
    qk_seq, o_seq = _att_seq_specs(lp)

    def body(q_ref, k_ref, v_ref, o_ref, lse_ref, do_ref, dq_ref, dk_ref, dv_ref):
        dk_ref[...] = jnp.zeros_like(dk_ref)
        dv_ref[...] = jnp.zeros_like(dv_ref)
        for q0, q1 in _tiles(lp, ATT_Q_TILE):
            qb, dob = q_ref[q0:q1, :], do_ref[q0:q1, :]
            lse_row = jnp.transpose(lse_ref[q0:q1, :])[0:1, :]
            dsum_row = jnp.sum(jnp.transpose(dob * o_ref[q0:q1, :]), axis=0, keepdims=True)
            dq = jnp.zeros((q1 - q0, QK_PAD), F32)
            for k0, k1 in _tiles(q1, ATT_K_TILE):
                kb, vb = k_ref[k0:k1, :], v_ref[k0:k1, :]
                s = mm_nt(kb, qb) * ATT_SCALE
                vis = _att_visible(q0, q1, k0, k1, True)
                s = s if vis is None else jnp.where(vis, s, NEG)
                p = jnp.exp(s - lse_row)
                ds = p * (mm_nt(vb, dob) - dsum_row) * ATT_SCALE
                dv_ref[k0:k1, :] += mm_nn(p, dob)
                dk_ref[k0:k1, :] += mm_nn(ds, qb)
                dq = dq + mm_tn(ds, kb)
            dq_ref[q0:q1, :] = dq

    return pl.pallas_call(
        body, grid=(t // lp, N_HEADS), in_specs=[qk_seq, qk_seq, o_seq, o_seq, o_seq, o_seq], out_specs=[qk_seq, qk_seq, o_seq],
        out_shape=[jax.ShapeDtypeStruct((t, N_HEADS * QK_PAD), F32), jax.ShapeDtypeStruct((t, N_HEADS * QK_PAD), F32),
                   jax.ShapeDtypeStruct((t, N_HEADS * HEAD), F32)],
        compiler_params=_cparams(("arbitrary", "arbitrary")), name="flash_bwd")(q, k, v, o, lse, do)


def loss_head(h2, target, lp):
    nb, seq, d = target.shape
    tr = 128
    nblk = lp // tr
    lead_blocks = LEAD // tr

    def body(h_ref, t_ref, loss_ref, dh_ref, acc_ref):
        b, i = pl.program_id(0), pl.program_id(1)

        @pl.when((b == 0) & (i == 0))
        def _():
            acc_ref[...] = jnp.zeros_like(acc_ref)

        @pl.when(i < lead_blocks)
        def _():
            dh_ref[...] = jnp.zeros_like(dh_ref)

        @pl.when(i >= lead_blocks)
        def _():
            err = h_ref[...] - t_ref[...]
            dh_ref[...] = err * (1.0 / d)
            acc_ref[...] += jnp.sum(err * err, axis=0, keepdims=True)

        @pl.when((b == nb - 1) & (i == nblk - 1))
        def _():
            loss_ref[...] = jnp.sum(acc_ref[...], axis=1, keepdims=True) * (0.5 / d)

    return pl.pallas_call(
        body, grid=(nb, nblk),
        in_specs=[pl.BlockSpec((None, tr, d), lambda b, i: (b, i, 0)),
                  pl.BlockSpec((None, tr, d), lambda b, i: (b, jnp.maximum(i - lead_blocks, 0), 0))],
        out_specs=[pl.BlockSpec((1, 1), lambda b, i: (0, 0)), pl.BlockSpec((None, tr, d), lambda b, i: (b, i, 0))],
        out_shape=[jax.ShapeDtypeStruct((1, 1), F32), jax.ShapeDtypeStruct((nb, lp, d), F32)],
        scratch_shapes=[pltpu.VMEM((1, d), F32)], compiler_params=_cparams(("arbitrary", "arbitrary")), name="loss_head")(h2, target)


def meta_grad(dh0):
    nb, _, d = dh0.shape

    def body(g_ref, o_ref):
        @pl.when(pl.program_id(0) == 0)
        def _():
            o_ref[...] = jnp.zeros_like(o_ref)

        o_ref[...] += g_ref[PAD_ROWS:LEAD, :]

    return pl.pallas_call(
        body, grid=(nb,), in_specs=[pl.BlockSpec((None, LEAD, d), lambda b: (b, 0, 0))],
        out_specs=pl.BlockSpec((N_META, d), lambda b: (0, 0)), out_shape=jax.ShapeDtypeStruct((N_META, d), F32),
        compiler_params=_cparams(("arbitrary",)), name="meta_grad")(dh0)


_HBM = pl.BlockSpec(memory_space=pltpu.HBM)


def _mesh_pos():
    x, y, c = lax.axis_index("x"), lax.axis_index("y"), lax.axis_index("c")
    return x, y, c


def _peer(x, y, c, k):
    px = 1 - x if k & 4 else x
    py = 1 - y if k & 2 else y
    pc = 1 - c if k & 1 else c
    return (px, py, pc), 4 * px + 2 * py + pc


def all_gather(buf):
    def body(x_ref, out_ref, send_sems, recv_sems, local_sem):
        x, y, c = _mesh_pos()
        me = 4 * x + 2 * y + c
        mine = pltpu.make_async_copy(x_ref, out_ref.at[me], local_sem)
        mine.start()
        sends = []
        for k in range(1, N_DEV):
            peer, _ = _peer(x, y, c, k)
            cp = pltpu.make_async_remote_copy(src_ref=x_ref, dst_ref=out_ref.at[me], send_sem=send_sems.at[k - 1],
                                              recv_sem=recv_sems.at[k - 1], device_id=peer, device_id_type=pl.DeviceIdType.MESH)
            cp.start()
            sends.append(cp)
        for k in range(1, N_DEV):
            peer, peer_id = _peer(x, y, c, k)
            pltpu.make_async_remote_copy(src_ref=x_ref, dst_ref=out_ref.at[peer_id], send_sem=send_sems.at[k - 1],
                                         recv_sem=recv_sems.at[k - 1], device_id=peer, device_id_type=pl.DeviceIdType.MESH).wait_recv()
        for cp in sends:
            cp.wait_send()
        mine.wait()

    return pl.pallas_call(
        body, in_specs=[_HBM], out_specs=_HBM, out_shape=jax.ShapeDtypeStruct((N_DEV,) + buf.shape, buf.dtype),
        scratch_shapes=[pltpu.SemaphoreType.DMA((N_DEV - 1,)), pltpu.SemaphoreType.DMA((N_DEV - 1,)), pltpu.SemaphoreType.DMA],
        name="all_gather")(buf)


def all_to_all(buf):
    def body(x_ref, out_ref, send_sems, recv_sems, local_sem):
        x, y, c = _mesh_pos()
        me = 4 * x + 2 * y + c
        mine = pltpu.make_async_copy(x_ref.at[me], out_ref.at[me], local_sem)
        mine.start()
        sends = []
        for k in range(1, N_DEV):
            peer, peer_id = _peer(x, y, c, k)
            cp = pltpu.make_async_remote_copy(src_ref=x_ref.at[peer_id], dst_ref=out_ref.at[me], send_sem=send_sems.at[k - 1],
                                              recv_sem=recv_sems.at[k - 1], device_id=peer, device_id_type=pl.DeviceIdType.MESH)
            cp.start()
            sends.append(cp)
        for k in range(1, N_DEV):
            peer, peer_id = _peer(x, y, c, k)
            pltpu.make_async_remote_copy(src_ref=x_ref.at[peer_id], dst_ref=out_ref.at[peer_id], send_sem=send_sems.at[k - 1],
                                         recv_sem=recv_sems.at[k - 1], device_id=peer, device_id_type=pl.DeviceIdType.MESH).wait_recv()
        for cp in sends:
            cp.wait_send()
        mine.wait()

    return pl.pallas_call(
        body, in_specs=[_HBM], out_specs=_HBM, out_shape=jax.ShapeDtypeStruct(buf.shape, buf.dtype),
        scratch_shapes=[pltpu.SemaphoreType.DMA((N_DEV - 1,)), pltpu.SemaphoreType.DMA((N_DEV - 1,)), pltpu.SemaphoreType.DMA],
        name="all_to_all")(buf)


def _f_rms(x, g):
    return (_rms(x, g),)


def _f_rms2(x, g1, g2):
    r = x * lax.rsqrt(jnp.sum(x * x, -1, keepdims=True) / x.shape[-1] + EPS)
    return r * g1, r * g2


def _f_out_gate(o, gate, gain):
    return (_rms(o, gain) * _silu(gate),)


def _f_gate(o, gate):
    return (o * _silu(gate),)


@jax.custom_vjp
def _swap_rope_halves(x):
    half = ROPE // 2
    lane = lax.broadcasted_iota(jnp.int32, x.shape, 1)
    return jnp.where(lane < half, pltpu.roll(x, HEAD - half, 1), jnp.where(lane < ROPE, pltpu.roll(x, half, 1), 0.0))


_swap_rope_halves.defvjp(lambda x: (_swap_rope_halves(x), None), lambda _, g: (_swap_rope_halves(g),))


def _f_qk_final(nope, rope_in, g_nope, g_rope, cos, sin):
    ms = (jnp.sum(nope * nope, -1, keepdims=True) + jnp.sum(rope_in * rope_in, -1, keepdims=True)) / QK_DIM
    r = lax.rsqrt(ms + EPS)
    a = nope * r * g_nope
    b = rope_in * r * g_rope
    return (jnp.concatenate([a, b * cos + _swap_rope_halves(b) * sin], axis=1),)


def _rope_tables(lp):
    half = ROPE // 2
    pos = jnp.maximum(jnp.arange(lp) - PAD_ROWS, 0)
    inv = ROPE_THETA ** (-jnp.arange(half, dtype=F32) / half)
    ang = pos.astype(F32)[:, None] * inv[None, :]
    zeros = jnp.zeros((lp, HEAD - ROPE), F32)
    cos = jnp.concatenate([jnp.cos(ang), jnp.cos(ang), zeros], 1)
    sin = jnp.concatenate([-jnp.sin(ang), jnp.sin(ang), zeros], 1)
    return cos, sin


def _pad_lanes(w, width=HEAD):
    return jnp.pad(w, ((0, 0), (0, width - w.shape[1])))


def _pad_rows(w, rows=HEAD):
    return jnp.pad(w, ((0, rows - w.shape[0]), (0, 0)))


def _split_heads_qk_t(w_t):
    k = w_t.shape[1]
    w3 = w_t.reshape(N_HEADS, QK_DIM, k)
    nope = w3[:, :HEAD].reshape(N_HEADS * HEAD, k)
    rope = jnp.pad(w3[:, HEAD:], ((0, 0), (0, HEAD - ROPE), (0, 0))).reshape(N_HEADS * HEAD, k)
    return jnp.concatenate([nope, rope], 0)


def _merge_heads_qk_t(g_t):
    k = g_t.shape[1]
    kw = N_HEADS * HEAD
    nope, rope = g_t[:kw].reshape(N_HEADS, HEAD, k), g_t[kw:].reshape(N_HEADS, HEAD, k)[:, :ROPE]
    return jnp.concatenate([nope, rope], 1).reshape(N_HEADS * QK_DIM, k)


def local_step(x, target, w):
    nb, seq, d = x.shape
    lp = seq + LEAD
    t = nb * lp
    tr = _pick(lp, (544, 128))
    ntab = lp // tr
    mxu = _MXU_DTYPE
    kw = N_HEADS * HEAD

    a_w_in_t = w["a_w_in"].astype(mxu)
    w_qkv_t, w_ga_t, w_ba_t = a_w_in_t[:3 * kw], a_w_in_t[3 * kw:4 * kw], _pad_rows(a_w_in_t[4 * kw:])
    a_conv = w["a_conv"].T
    a_w_out = w["a_w_out"].astype(mxu)
    alog, dtb, o_gain = _pad_lanes(w["a_log"]), _pad_lanes(w["a_dt_bias"]), w["a_o_gain"]
    w_dkv, w_dpe = w["kv_w_down"][:, :KV_RANK].astype(mxu), _pad_lanes(w["kv_w_down"][:, KV_RANK:]).astype(mxu)
    w_ukv_t = jnp.concatenate([w["kv_w_uk"], w["kv_w_uv"]], 0).astype(mxu)
    b_w_in_t = w["b_w_in"].astype(mxu)
    w_cq_t, w_gb_t = b_w_in_t[:Q_RANK], b_w_in_t[Q_RANK:]
    w_q_t = _split_heads_qk_t(w["b_w_uq"]).astype(mxu)
    b_w_out = w["b_w_out"].astype(mxu)
    a_norm, kv_norm, b_norm = w["a_norm"], w["kv_norm"][None, :], w["b_norm"]
    lat_norm, qlat_norm = w["kv_latent_norm"][None, :], w["b_q_latent_norm"]
    kg_nope, kg_rope = w["k_gain"][None, :HEAD], _pad_lanes(w["k_gain"][None, HEAD:])
    qg_nope, qg_rope = w["b_q_gain"][:, :HEAD], _pad_lanes(w["b_q_gain"][:, HEAD:])
    cos, sin = _rope_tables(lp)

    meta = jnp.broadcast_to(w["meta_tokens"].T[None], (nb, N_META, d))
    h0 = jnp.concatenate([jnp.zeros((nb, PAD_ROWS, d), F32), meta, x], 1).reshape(t, d)
    (hn,) = row_call("a_norm_fwd", _f_rms, [Arg(h0), Arg(a_norm, "par")], [(d, mxu, d, False)], tr)
    z_qkv = matmul("a_in_qkv", hn, w_qkv_t, "nt")
    gate_a = matmul("a_in_gate", hn, w_ga_t, "nt")
    z_ba = matmul("a_in_ba", hn, w_ba_t, "nt")
    qkv_a = conv_fwd(z_qkv, a_conv, lp)
    o_a, states, t_invs = delta_fwd(qkv_a, z_ba, alog, dtb, lp)
    og_args = [Arg(o_a, bc=HEAD, ph=True, diff=True), Arg(gate_a, bc=HEAD, ph=True, diff=True, gdt=mxu), Arg(o_gain, "par", diff=True)]
    (og_a,) = row_call("a_out_gate_fwd", _f_out_gate, og_args, [(kw, mxu, HEAD, True)], tr, nh=N_HEADS)
    h1 = matmul("a_out", og_a, a_w_out, "nn", res=h0)

    hk, hb = row_call("b_norms_fwd", _f_rms2, [Arg(h1), Arg(kv_norm, "par"), Arg(b_norm, "par")], [(d, mxu, d, False), (d, mxu, d, False)], tr)
    c_kv_raw = matmul("kv_down", hk, w_dkv, "nn")
    k_pe = matmul("kv_down_pe", hk, w_dpe, "nn")
    c_q_raw = matmul("b_in_q", hb, w_cq_t, "nt")
    gate_b = matmul("b_in_gate", hb, w_gb_t, "nt")
    (c_kv,) = row_call("kv_latent_fwd", _f_rms, [Arg(c_kv_raw), Arg(lat_norm, "par")], [(KV_RANK, mxu, KV_RANK, False)], tr)
    (c_q,) = row_call("q_latent_fwd", _f_rms, [Arg(c_q_raw), Arg(qlat_norm, "par")], [(Q_RANK, mxu, Q_RANK, False)], tr)
    k_nope = matmul("k_up", c_kv, w_ukv_t[:kw], "nt")
    v_b = matmul("v_up", c_kv, w_ukv_t[kw:], "nt", out_dtype=mxu)
    q_up = matmul("q_up", c_q, w_q_t, "nt")
    tabs = [Arg(cos, "tab"), Arg(sin, "tab")]
    k_args = [Arg(k_nope, bc=HEAD, ph=True, diff=True, gdt=mxu), Arg(k_pe, diff=True), Arg(kg_nope, "par", diff=True), Arg(kg_rope, "par", diff=True)] + tabs
    q_args = [Arg(q_up, bc=HEAD, ph=True, diff=True, gdt=mxu), Arg(q_up, bc=HEAD, base=N_HEADS, ph=True, diff=True, gdt=mxu),
              Arg(qg_nope, "par", diff=True), Arg(qg_rope, "par", diff=True)] + tabs
    (k_fin,) = row_call("k_final_fwd", _f_qk_final, k_args, [(N_HEADS * QK_PAD, mxu, QK_PAD, True)], tr, nh=N_HEADS, ntab=ntab)
    (q_fin,) = row_call("q_final_fwd", _f_qk_final, q_args, [(N_HEADS * QK_PAD, mxu, QK_PAD, True)], tr, nh=N_HEADS, ntab=ntab)
    o_b, lse = flash_fwd(q_fin, k_fin, v_b, lp)
    gb_args = [Arg(o_b, diff=True), Arg(gate_b, diff=True, gdt=mxu)]
    (og_b,) = row_call("b_gate_fwd", _f_gate, gb_args, [(kw, mxu, kw, False)], tr)
    h2 = matmul("b_out", og_b, b_w_out, "nn", res=h1)

    loss, dh2 = loss_head(h2.reshape(nb, lp, d), target, lp)
    dh2 = dh2.reshape(t, d)
    grads = {}

    d_og_b = matmul("b_out_dx", dh2, b_w_out, "nt")
    grads["b_w_out"] = matmul("b_out_dw", og_b, dh2, "tn")
    d_o_b, d_gate_b = row_vjp_call("b_gate_bwd", _f_gate, gb_args, [Arg(d_og_b)], tr)
    dq_fin, dk_fin, dv_b = flash_bwd(q_fin, k_fin, v_b, o_b, lse, d_o_b, lp)
    dq_nope, dq_rope, d_qg_nope, d_qg_rope = row_vjp_call(
        "q_final_bwd", _f_qk_final, q_args, [Arg(dq_fin, bc=QK_PAD, ph=True)], tr, nh=N_HEADS, ntab=ntab)
    dk_nope, dk_pe, d_kg_nope, d_kg_rope = row_vjp_call(
        "k_final_bwd", _f_qk_final, k_args, [Arg(dk_fin, bc=QK_PAD, ph=True)], tr, nh=N_HEADS, ntab=ntab)
    grads["b_q_gain"] = jnp.concatenate([d_qg_nope, d_qg_rope[:, :ROPE]], 1)
    grads["k_gain"] = jnp.concatenate([d_kg_nope, d_kg_rope[:, :ROPE]], 1)[0]
    d_c_q = matmul("q_nope_dx", dq_nope, w_q_t[:kw], "nn")
    d_c_q = matmul("q_rope_dx", dq_rope, w_q_t[kw:], "nn", res=d_c_q)
    grads["b_w_uq"] = _merge_heads_qk_t(jnp.concatenate([matmul("q_nope_dw", dq_nope, c_q, "tn"), matmul("q_rope_dw", dq_rope, c_q, "tn")], 0))
    d_c_kv = matmul("k_up_dx", dk_nope, w_ukv_t[:kw], "nn")
    d_c_kv = matmul("v_up_dx", dv_b, w_ukv_t[kw:], "nn", res=d_c_kv)
    grads["kv_w_uk"], grads["kv_w_uv"] = matmul("k_up_dw", dk_nope, c_kv, "tn"), matmul("v_up_dw", dv_b, c_kv, "tn")
    d_c_q_raw, grads["b_q_latent_norm"] = row_vjp_call(
        "q_latent_bwd", _f_rms, [Arg(c_q_raw, diff=True, gdt=mxu), Arg(qlat_norm, "par", diff=True)], [Arg(d_c_q)], tr)
    d_c_kv_raw, d_lat = row_vjp_call(
        "kv_latent_bwd", _f_rms, [Arg(c_kv_raw, diff=True, gdt=mxu), Arg(lat_norm, "par", diff=True)], [Arg(d_c_kv)], tr)
    grads["kv_latent_norm"] = d_lat[0]
    d_hb = matmul("b_in_q_dx", d_c_q_raw, w_cq_t, "nn")
    d_hb = matmul("b_in_gate_dx", d_gate_b, w_gb_t, "nn", res=d_hb)
    grads["b_w_in"] = jnp.concatenate([matmul("b_in_q_dw", d_c_q_raw, hb, "tn"), matmul("b_in_gate_dw", d_gate_b, hb, "tn")], 0)
    d_hk = matmul("kv_down_dx", d_c_kv_raw, w_dkv, "nt")
    d_hk = matmul("kv_down_pe_dx", dk_pe, w_dpe, "nt", res=d_hk)
    grads["kv_w_down"] = jnp.concatenate([matmul("kv_down_dw", hk, d_c_kv_raw, "tn"), matmul("kv_down_pe_dw", hk, dk_pe, "tn")[:, :ROPE]], 1)
    d_h1_norms, d_kv_norm, grads["b_norm"] = row_vjp_call(
        "b_norms_bwd", _f_rms2, [Arg(h1, diff=True), Arg(kv_norm, "par", diff=True), Arg(b_norm, "par", diff=True)], [Arg(d_hk), Arg(d_hb)], tr)
    grads["kv_norm"] = d_kv_norm[0]

    (dh1,) = row_call("dh1_sum", lambda a, b: (a + b,), [Arg(dh2), Arg(d_h1_norms)], [(d, F32, d, False)], tr)
    d_og_a = matmul("a_out_dx", dh1, a_w_out, "nt")
    grads["a_w_out"] = matmul("a_out_dw", og_a, dh1, "tn")
    d_o_a, d_gate_a, grads["a_o_gain"] = row_vjp_call(
        "a_out_gate_bwd", _f_out_gate, og_args, [Arg(d_og_a, bc=HEAD, ph=True)], tr, nh=N_HEADS)
    dqkv_a, d_ba, d_alog, d_dtb = delta_bwd(qkv_a, z_ba, alog, dtb, states, t_invs, d_o_a, lp)
    grads["a_log"], grads["a_dt_bias"] = d_alog[:, :N_HEADS], d_dtb[:, :N_HEADS]
    dz_qkv, d_conv = conv_bwd(z_qkv, a_conv, dqkv_a, lp)
    grads["a_conv"] = d_conv.T
    d_hn = matmul("a_in_qkv_dx", dz_qkv, w_qkv_t, "nn")
    d_hn = matmul("a_in_gate_dx", d_gate_a, w_ga_t, "nn", res=d_hn)
    d_hn = matmul("a_in_ba_dx", d_ba, w_ba_t, "nn", res=d_hn)
    grads["a_w_in"] = jnp.concatenate([matmul("a_in_qkv_dw", dz_qkv, hn, "tn"), matmul("a_in_gate_dw", d_gate_a, hn, "tn"),
                                       matmul("a_in_ba_dw", d_ba, hn, "tn")[:2 * N_HEADS]], 0)
    d_h0_norm, grads["a_norm"] = row_vjp_call("a_norm_bwd", _f_rms, [Arg(h0, diff=True), Arg(a_norm, "par", diff=True)], [Arg(d_hn)], tr)
    (dh0,) = row_call("dh0_sum", lambda a, b: (a + b,), [Arg(dh1), Arg(d_h0_norm)], [(d, F32, d, False)], tr)
    dh0 = dh0.reshape(nb, lp, d)
    grads["meta_tokens"] = meta_grad(dh0).T
    return loss, dh0[:, LEAD:], grads


_SHARDED = (
    ("meta_tokens", True, False), ("a_norm", True, False), ("a_w_in", True, True), ("a_conv", True, False), ("a_w_out", False, True),
    ("kv_w_down", False, True), ("kv_w_uk", True, True), ("kv_w_uv", True, True), ("b_w_in", True, True), ("b_w_uq", True, True),
    ("b_w_out", False, True))
_REPLICATED = ("a_log", "a_dt_bias", "a_o_gain", "kv_norm", "kv_latent_norm", "k_gain", "b_norm", "b_q_latent_norm", "b_q_gain")
_ALL_WEIGHTS = ("meta_tokens", "a_norm", "a_w_in", "a_conv", "a_log", "a_dt_bias", "a_o_gain", "a_w_out", "kv_norm", "kv_w_down",
                "kv_latent_norm", "kv_w_uk", "kv_w_uv", "k_gain", "b_norm", "b_w_in", "b_q_latent_norm", "b_w_uq", "b_q_gain", "b_w_out")


def _round_up(n, m):
    return (n + m - 1) // m * m


def _pack_rows(pieces, row_multiple):
    padded = []
    for p in pieces:
        n = p.shape[-1]
        padded.append(jnp.pad(p, [(0, 0)] * (p.ndim - 1) + [(0, _round_up(n, PACK_COLS) - n)]))
    flat = jnp.concatenate(padded, -1)
    rows = _round_up(flat.shape[-1] // PACK_COLS, row_multiple)
    flat = jnp.pad(flat, [(0, 0)] * (flat.ndim - 1) + [(0, rows * PACK_COLS - flat.shape[-1])])
    return flat.reshape(flat.shape[:-1] + (rows, PACK_COLS))


def _unpack_rows(buf, sizes):
    flat = buf.reshape(buf.shape[:-2] + (-1,))
    out, off = [], 0
    for n in sizes:
        out.append(flat[..., off:off + n])
        off += _round_up(n, PACK_COLS)
    return out


def _as_bf16_pairs(a):
    return lax.bitcast_convert_type(a, BF16).reshape(-1)


def _from_bf16_pairs(a):
    return lax.bitcast_convert_type(a.reshape(a.shape[:-1] + (-1, 2)), F32)


def _shard_2d(a):
    return a.reshape(a.shape[-2:]) if a.ndim > 2 else a


def gather_weights(local):
    pieces, shapes = [], []
    for name, by_cols, narrow in _SHARDED:
        shard = _shard_2d(local[name])
        shard = shard.T if by_cols else shard
        shapes.append(shard.shape)
        flat = shard.reshape(-1)
        pieces.append(flat.astype(BF16) if narrow else _as_bf16_pairs(flat))
    gathered = all_gather(_pack_rows(pieces, 16))
    parts = _unpack_rows(gathered, [p.shape[0] for p in pieces])
    full = {}
    for (name, _, narrow), part, (rows, cols) in zip(_SHARDED, parts, shapes, strict=True):
        vals = part if narrow else _from_bf16_pairs(part)
        full[name] = vals.reshape(N_DEV * rows, cols)
    full["a_norm"] = full["a_norm"].reshape(1, -1)
    return full


def reduce_contributions(recv):
    _, r, c = recv.shape
    tr = _pick(r, (256, 128, 64, 32, 16, 8))

    def body(g_ref, o_ref):
        g = g_ref[0]
        for dev in range(1, N_DEV):
            g = g + g_ref[dev]
        o_ref[...] = g

    return pl.pallas_call(
        body, grid=(r // tr,), in_specs=[pl.BlockSpec((N_DEV, tr, c), lambda i: (0, i, 0))], out_specs=pl.BlockSpec((tr, c), lambda i: (i, 0)),
        out_shape=jax.ShapeDtypeStruct((r, c), F32), compiler_params=_cparams(("arbitrary",)), name="reduce_contributions")(recv)


def adamw(g, w, m, v):
    r, c = g.shape
    tr = _pick(r, (256, 128, 64, 32, 16, 8))

    def body(g_ref, w_ref, m_ref, v_ref, d_ref, mo_ref, vo_ref):
        g_ = g_ref[...]
        m_new = ADAM_B1 * m_ref[...] + (1.0 - ADAM_B1) * g_
        v_new = ADAM_B2 * v_ref[...] + (1.0 - ADAM_B2) * (g_ * g_)
        m_hat = m_new / (1.0 - ADAM_B1 ** ADAM_STEP)
        v_hat = v_new / (1.0 - ADAM_B2 ** ADAM_STEP)
        d_ref[...] = -ADAM_LR * (m_hat / (jnp.sqrt(v_hat) + ADAM_EPS) + ADAM_WD * w_ref[...])
        mo_ref[...] = m_new
        vo_ref[...] = v_new

    blk = pl.BlockSpec((tr, c), lambda i: (i, 0))
    out = jax.ShapeDtypeStruct((r, c), F32)
    return pl.pallas_call(body, grid=(r // tr,), in_specs=[blk, blk, blk, blk], out_specs=[blk, blk, blk], out_shape=[out, out, out],
                          compiler_params=_cparams(("arbitrary",)), name="adamw")(g, w, m, v)


def kernel(x, meta_tokens, a_norm, a_w_in, a_conv, a_log, a_dt_bias, a_o_gain, a_w_out, kv_norm, kv_w_down, kv_latent_norm, kv_w_uk, kv_w_uv, k_gain, b_norm, b_w_in, b_q_latent_norm, b_w_uq, b_q_gain, b_w_out, loss_target, m_meta_tokens, m_a_norm, m_a_w_in, m_a_conv, m_a_log, m_a_dt_bias, m_a_o_gain, m_a_w_out, m_kv_norm, m_kv_w_down, m_kv_latent_norm, m_kv_w_uk, m_kv_w_uv, m_k_gain, m_b_norm, m_b_w_in, m_b_q_latent_norm, m_b_w_uq, m_b_q_gain, m_b_w_out, v_meta_tokens, v_a_norm, v_a_w_in, v_a_conv, v_a_log, v_a_dt_bias, v_a_o_gain, v_a_w_out, v_kv_norm, v_kv_w_down, v_kv_latent_norm, v_kv_w_uk, v_kv_w_uv, v_k_gain, v_b_norm, v_b_w_in, v_b_q_latent_norm, v_b_w_uq, v_b_q_gain, v_b_w_out):
    given = dict(locals())
    local_w = {n: given[n] for n in _ALL_WEIGHTS}
    full = gather_weights(local_w)
    for n in _REPLICATED:
        full[n] = local_w[n]

    loss_part, grad_x, grads = local_step(x, loss_target, full)

    pieces = [grads[n].reshape(N_DEV, -1) for n, _, _ in _SHARDED]
    pieces += [jnp.broadcast_to(grads[n].reshape(1, -1), (N_DEV, grads[n].size)) for n in _REPLICATED]
    pieces.append(jnp.broadcast_to(loss_part, (N_DEV, 1)))
    summed = reduce_contributions(all_to_all(_pack_rows(pieces, 256)))
    parts = _unpack_rows(summed, [p.shape[1] for p in pieces])
    loss = parts[-1][0]

    order = [n for n, _, _ in _SHARDED] + list(_REPLICATED)
    grad_local = {}
    for (n, by_cols, _), part in zip(_SHARDED, parts, strict=False):
        rows, cols = _shard_2d(local_w[n]).shape
        g2 = part.reshape(cols, rows).T if by_cols else part.reshape(rows, cols)
        grad_local[n] = g2.reshape(local_w[n].shape)
    for n, part in zip(_REPLICATED, parts[len(_SHARDED):], strict=False):
        grad_local[n] = part.reshape(local_w[n].shape)

    def pack_local(tree):
        return _pack_rows([tree[n].reshape(-1) for n in order], 256)

    packed = adamw(pack_local(grad_local), pack_local(local_w), pack_local({n: given["m_" + n] for n in order}),
                   pack_local({n: given["v_" + n] for n in order}))
    results = [grad_local[n] for n in _ALL_WEIGHTS]
    for buf in packed:
        by_name = {n: p.reshape(local_w[n].shape) for n, p in zip(order, _unpack_rows(buf, [local_w[n].size for n in order]), strict=True)}
        results.extend(by_name[n] for n in _ALL_WEIGHTS)
    return (loss, grad_x, *results)
```

```python
import dataclasses
import functools
import math

import jax
import jax.numpy as jnp
from jax import lax
from jax.experimental import pallas as pl
from jax.experimental.pallas import tpu as pltpu

F32 = jnp.float32
BF16 = jnp.bfloat16
_MXU_DTYPE = jnp.bfloat16
_HI = lax.Precision.HIGHEST

N_DEV = 8
D_MODEL = 1024
N_HEADS = 8
HEAD = 128
CHUNK = 64
N_META = 16
PAD_ROWS = 2 * CHUNK - N_META
LEAD = PAD_ROWS + N_META
ROPE = 64
QK_DIM = HEAD + ROPE
QK_PAD = 2 * HEAD
KV_RANK = 256
Q_RANK = 384
CONV_K = 4
EPS = 1e-6
NEG = -1e30
ROPE_THETA = 10000.0
ADAM_LR, ADAM_B1, ADAM_B2, ADAM_EPS, ADAM_WD, ADAM_STEP = 0.001, 0.9, 0.999, 1e-08, 0.01, 10
PACK_COLS = 512
VMEM_LIMIT = 56 * 1024 * 1024


def _pick(n, options):
    for o in options:
        if n % o == 0:
            return o
    raise ValueError(f"no tile for {n} among {options}")


def _cparams(sem):
    return pltpu.CompilerParams(dimension_semantics=sem, vmem_limit_bytes=VMEM_LIMIT)


def _dims(a, dims):
    if a.ndim == 2:
        return (dims, ((), ()))
    (ca,), (cb,) = dims
    return (((ca + 1,), (cb + 1,)), ((0,), (0,)))


def _dot(a, b, dims):
    return lax.dot_general(a.astype(_MXU_DTYPE), b.astype(_MXU_DTYPE), _dims(a, dims), preferred_element_type=F32)


@jax.custom_vjp
def mm_nn(a, b):
    return _dot(a, b, ((1,), (0,)))


@jax.custom_vjp
def mm_nt(a, b):
    return _dot(a, b, ((1,), (1,)))


@jax.custom_vjp
def mm_tn(a, b):
    return _dot(a, b, ((0,), (0,)))


mm_nn.defvjp(lambda a, b: (mm_nn(a, b), (a, b)), lambda r, g: (mm_nt(g, r[1]), mm_tn(r[0], g)))
mm_nt.defvjp(lambda a, b: (mm_nt(a, b), (a, b)), lambda r, g: (mm_nn(g, r[1]), mm_tn(g, r[0])))
mm_tn.defvjp(lambda a, b: (mm_tn(a, b), (a, b)), lambda r, g: (mm_nt(r[1], g), mm_nn(r[0], g)))


def _dot_f32(a, b):
    return lax.dot_general(a, b, _dims(a, ((1,), (0,))), precision=_HI, preferred_element_type=F32)


def _split_hi_lo(x):
    hi = x.astype(_MXU_DTYPE)
    lo = (x - hi.astype(F32)).astype(_MXU_DTYPE)
    return hi, lo


def _mm_3pass(a, b):
    ah, al = _split_hi_lo(a)
    bh, bl = _split_hi_lo(b)
    d = lambda u, w: lax.dot_general(u, w, _dims(u, ((1,), (0,))), preferred_element_type=F32)
    return d(ah, bh) + (d(ah, bl) + d(al, bh))


def _inv_unit_lower(a):
    n = a.shape[-1]
    eye = (lax.broadcasted_iota(jnp.int32, (n, n), 0) == lax.broadcasted_iota(jnp.int32, (n, n), 1)).astype(F32)
    t = eye - a
    p = _mm_3pass(a, a)
    squarings = int(math.log2(n)) - 1
    for s in range(squarings):
        t = t + _mm_3pass(t, p)
        if s + 1 < squarings:
            p = _mm_3pass(p, p)
    return t


@jax.custom_vjp
def _inv_lookup(a, t):
    return t


def _inv_lookup_bwd(t, g):
    return -mm_tn(t, mm_nt(g, t)), jnp.zeros_like(t)


_inv_lookup.defvjp(lambda a, t: (t, t), _inv_lookup_bwd)


def _sigmoid(x):
    return 1.0 / (1.0 + jnp.exp(-x))


def _silu(x):
    return x * _sigmoid(x)


def _softplus(x):
    return jnp.where(x > 20.0, x, jnp.log(1.0 + jnp.exp(jnp.minimum(x, 20.0))))


def _rms(x, g, width=None):
    ms = jnp.sum(x * x, -1, keepdims=True) / (x.shape[-1] if width is None else width)
    return x * lax.rsqrt(ms + EPS) * g


MM_VMEM_BUDGET = 40 * 1024 * 1024


def _matmul_rows(name, a, b, mode, out_dtype, res):
    m, k = a.shape
    n = b.shape[1] if mode == "nn" else b.shape[0]
    dims = {"nn": ((1,), (0,)), "nt": ((1,), (1,))}[mode]
    out_bytes = jnp.dtype(out_dtype).itemsize

    def vmem(tm):
        blocks = 2 * tm * k * a.dtype.itemsize + 2 * k * n * b.dtype.itemsize + 2 * tm * n * out_bytes + tm * n * 4
        return blocks + (2 * tm * n * res.dtype.itemsize if res is not None else 0)

    tm = next(c for c in (2176, 1088, 512, 256, 128, 64) if m % c == 0 and vmem(c) <= MM_VMEM_BUDGET)

    def body(*refs):
        a_ref, b_ref = refs[:2]
        out = _dot(a_ref[...], b_ref[...], dims)
        if res is not None:
            out = out + refs[2][...].astype(F32)
        refs[-1][...] = out.astype(refs[-1].dtype)

    o_spec = pl.BlockSpec((tm, n), lambda i: (i, 0))
    in_specs = [pl.BlockSpec((tm, k), lambda i: (i, 0)), pl.BlockSpec(b.shape, lambda i: (0, 0))] + ([o_spec] if res is not None else [])
    args = (a, b) + ((res,) if res is not None else ())
    return pl.pallas_call(body, grid=(m // tm,), in_specs=in_specs, out_specs=o_spec, out_shape=jax.ShapeDtypeStruct((m, n), out_dtype),
                          compiler_params=_cparams(("parallel",)), name=name)(*args)


def matmul(name, a, b, mode, out_dtype=F32, res=None):
    if mode != "tn":
        return _matmul_rows(name, a, b, mode, out_dtype, res)
    (k, m), (k2, n) = a.shape, b.shape
    assert k == k2 and res is None, (name, a.shape, b.shape, mode)
    tm = _pick(m, (1024, 512, 384, 256, 128))
    tn = _pick(n, (1024, 512, 384, 256, 128))
    tk = _pick(k, (512, 256, 128))
    nk = k // tk
    dims = ((0,), (0,))

    def body(*refs):
        if res is None:
            a_ref, b_ref, o_ref, acc_ref = refs
        else:
            a_ref, b_ref, r_ref, o_ref, acc_ref = refs
        kk = pl.program_id(2)

        @pl.when(kk == 0)
        def _():
            acc_ref[...] = jnp.zeros_like(acc_ref)

        acc_ref[...] += _dot(a_ref[...], b_ref[...], dims)

        @pl.when(kk == nk - 1)
        def _():
            out = acc_ref[...]
            if res is not None:
                out = out + r_ref[...].astype(F32)
            o_ref[...] = out.astype(o_ref.dtype)

    a_spec = pl.BlockSpec((tk, tm), lambda i, j, kk: (kk, i)) if mode == "tn" else pl.BlockSpec((tm, tk), lambda i, j, kk: (i, kk))
    b_spec = pl.BlockSpec((tn, tk), lambda i, j, kk: (j, kk)) if mode == "nt" else pl.BlockSpec((tk, tn), lambda i, j, kk: (kk, j))
    o_spec = pl.BlockSpec((tm, tn), lambda i, j, kk: (i, j))
    in_specs = [a_spec, b_spec] + ([o_spec] if res is not None else [])
    args = (a, b) + ((res,) if res is not None else ())
    return pl.pallas_call(
        body, grid=(m // tm, n // tn, nk), in_specs=in_specs, out_specs=o_spec,
        out_shape=jax.ShapeDtypeStruct((m, n), out_dtype), scratch_shapes=[pltpu.VMEM((tm, tn), F32)],
        compiler_params=_cparams(("parallel", "parallel", "arbitrary")), name=name)(*args)


@dataclasses.dataclass
class Arg:
    arr: jax.Array
    kind: str = "row"
    bc: int = 0
    base: int = 0
    ph: bool = False
    diff: bool = False
    gdt: object = F32


def _arg_spec(a, tr, nh, ntab, base=None):
    bc = a.bc or a.arr.shape[1]
    base = a.base if base is None else base
    width = bc * nh if a.ph else bc
    col = base // nh if a.ph else base
    assert not a.ph or base % nh == 0
    if a.kind == "row":
        return pl.BlockSpec((tr, width), lambda i: (i, col))
    if a.kind == "tab":
        return pl.BlockSpec((tr, width), lambda i: (i % ntab, col))
    return pl.BlockSpec((a.arr.shape[0], width), lambda i: (0, col))


def _head_view(ref, a, h):
    bc = a.bc or a.arr.shape[1]
    v = ref[:, h * bc:(h + 1) * bc] if a.ph else ref[...]
    return v.astype(F32) if jnp.issubdtype(v.dtype, jnp.floating) else v


def row_call(name, fn, args, outs, tr, nh=1, ntab=1):
    t = args[0].arr.shape[0]
    n_in = len(args)
    out_args = [Arg(None, "row", bc, 0, ph) for (_, _, bc, ph) in outs]

    def body(*refs):
        for h in range(nh):
            res = fn(*[_head_view(r, a, h) for r, a in zip(refs[:n_in], args, strict=True)])
            for r, a, v in zip(refs[n_in:], out_args, res, strict=True):
                if a.ph:
                    r[:, h * a.bc:(h + 1) * a.bc] = v.astype(r.dtype)
                elif h == nh - 1:
                    r[...] = v.astype(r.dtype)

    return pl.pallas_call(
        body, grid=(t // tr,), in_specs=[_arg_spec(a, tr, nh, ntab) for a in args], out_specs=[_arg_spec(a, tr, nh, ntab) for a in out_args],
        out_shape=[jax.ShapeDtypeStruct((t, cols), dt) for (cols, dt, _, _) in outs],
        compiler_params=_cparams(("arbitrary",)), name=name)(*[a.arr for a in args])


def row_vjp_call(name, fn, args, cts, tr, nh=1, ntab=1):
    t = args[0].arr.shape[0]
    n_in, n_ct = len(args), len(cts)
    diff_idx = [k for k, a in enumerate(args) if a.diff]

    def body(*refs):
        out_refs = refs[n_in + n_ct:]
        shared = [None] * len(diff_idx)
        for k, r in zip(diff_idx, out_refs, strict=True):
            if args[k].kind == "par":
                @pl.when(pl.program_id(0) == 0)
                def _(r=r):
                    r[...] = jnp.zeros_like(r)

        for h in range(nh):
            vals = [_head_view(r, a, h) for r, a in zip(refs[:n_in], args, strict=True)]
            ct_vals = tuple(_head_view(r, a, h) for r, a in zip(refs[n_in:n_in + n_ct], cts, strict=True))

            def f(*dv, vals=vals):
                full = list(vals)
                for k, v in zip(diff_idx, dv, strict=True):
                    full[k] = v
                return tuple(fn(*full))

            _, vjp = jax.vjp(f, *[vals[k] for k in diff_idx])
            for j, (k, r, g) in enumerate(zip(diff_idx, out_refs, vjp(ct_vals), strict=True)):
                a = args[k]
                bc = a.bc or a.arr.shape[1]
                if not a.ph:
                    shared[j] = g if shared[j] is None else shared[j] + g
                elif a.kind == "row":
                    r[:, h * bc:(h + 1) * bc] = g.astype(r.dtype)
                else:
                    r[:, h * bc:(h + 1) * bc] += g
        for j, (k, r) in enumerate(zip(diff_idx, out_refs, strict=True)):
            if not args[k].ph:
                if args[k].kind == "row":
                    r[...] = shared[j].astype(r.dtype)
                else:
                    r[...] += shared[j]

    out_specs, out_shape = [], []
    for k in diff_idx:
        a = args[k]
        bc = a.bc or a.arr.shape[1]
        out_specs.append(_arg_spec(a, tr, nh, ntab, base=0))
        out_shape.append(jax.ShapeDtypeStruct((t if a.kind == "row" else a.arr.shape[0], bc * (nh if a.ph else 1)), a.gdt if a.kind == "row" else F32))
    in_specs = [_arg_spec(a, tr, nh, ntab) for a in list(args) + list(cts)]
    return pl.pallas_call(
        body, grid=(t // tr,), in_specs=in_specs, out_specs=out_specs, out_shape=out_shape,
        compiler_params=_cparams(("arbitrary",)), name=name)(*[a.arr for a in list(args) + list(cts)])


def _conv_taps(x, w):
    rows = lax.broadcasted_iota(jnp.int32, x.shape, 0)
    y = x * w[CONV_K - 1:CONV_K, :]
    shifted = []
    for s in range(1, CONV_K):
        xs = jnp.where(rows >= s, pltpu.roll(x, s, 0), 0.0)
        shifted.append(xs)
        y = y + xs * w[CONV_K - 1 - s:CONV_K - s, :]
    return y, shifted


CONV_HEADS = 4
CONV_BLOCKS_PER_THIRD = N_HEADS // CONV_HEADS


def _conv_post(y, block):
    a = _silu(y)
    normed = block < 2 * CONV_BLOCKS_PER_THIRD
    scale = jnp.where(block < CONV_BLOCKS_PER_THIRD, HEAD ** -0.5, 1.0)
    return a * jnp.where(normed, lax.rsqrt(jnp.sum(a * a, -1, keepdims=True) + EPS) * scale, 1.0)


def conv_fwd(z, w, lp):
    t, width = z.shape
    cols = CONV_HEADS * HEAD

    def body(z_ref, w_ref, o_ref):
        block = pl.program_id(1)
        for h in range(CONV_HEADS):
            cs = slice(h * HEAD, (h + 1) * HEAD)
            y, _ = _conv_taps(z_ref[:, cs], w_ref[:, cs])
            o_ref[:, cs] = _conv_post(y, block)

    return pl.pallas_call(
        body, grid=(t // lp, width // cols),
        in_specs=[pl.BlockSpec((lp, cols), lambda b, j: (b, j)), pl.BlockSpec((CONV_K, cols), lambda b, j: (0, j))],
        out_specs=pl.BlockSpec((lp, cols), lambda b, j: (b, j)), out_shape=jax.ShapeDtypeStruct((t, width), F32),
        compiler_params=_cparams(("arbitrary", "arbitrary")), name="a_conv_fwd")(z, w)


def conv_bwd(z, w, dout, lp):
    t, width = z.shape
    cols = CONV_HEADS * HEAD

    def body(z_ref, w_ref, g_ref, dz_ref, dw_ref):
        block = pl.program_id(0)

        @pl.when(pl.program_id(1) == 0)
        def _():
            dw_ref[...] = jnp.zeros_like(dw_ref)

        for h in range(CONV_HEADS):
            cs = slice(h * HEAD, (h + 1) * HEAD)
            x, wv = z_ref[:, cs], w_ref[:, cs]
            y, shifted = _conv_taps(x, wv)
            _, vjp = jax.vjp(lambda y_: _conv_post(y_, block), y)
            (dy,) = vjp(g_ref[:, cs])
            rows = lax.broadcasted_iota(jnp.int32, x.shape, 0)
            dx = dy * wv[CONV_K - 1:CONV_K, :]
            for s in range(1, CONV_K):
                dx = dx + jnp.where(rows < lp - s, pltpu.roll(dy, lp - s, 0), 0.0) * wv[CONV_K - 1 - s:CONV_K - s, :]
            dz_ref[:, cs] = dx.astype(dz_ref.dtype)
            dw_ref[CONV_K - 1:CONV_K, cs] += jnp.sum(dy * x, axis=0, keepdims=True)
            for s in range(1, CONV_K):
                dw_ref[CONV_K - 1 - s:CONV_K - s, cs] += jnp.sum(dy * shifted[s - 1], axis=0, keepdims=True)

    blk = pl.BlockSpec((lp, cols), lambda j, b: (b, j))
    w_blk = pl.BlockSpec((CONV_K, cols), lambda j, b: (0, j))
    return pl.pallas_call(
        body, grid=(width // cols, t // lp), in_specs=[blk, w_blk, blk], out_specs=[blk, w_blk],
        out_shape=[jax.ShapeDtypeStruct((t, width), _MXU_DTYPE), jax.ShapeDtypeStruct((CONV_K, width), F32)],
        compiler_params=_cparams(("arbitrary", "arbitrary")), name="a_conv_bwd")(z, w, dout)


def _delta_chunk(q, k, v, ba, alog, dtb, state, t_stored, h0):
    n_g, c = q.shape[0], q.shape[1]
    lane = lax.broadcasted_iota(jnp.int32, (1, HEAD), 1)

    def pick(x, offset):
        return jnp.stack([jnp.sum(x * (lane == offset + h0 + g).astype(F32), axis=1, keepdims=True) for g in range(n_g)])

    b_raw, a_raw = pick(ba, 0), pick(ba, N_HEADS)
    a_log, dt_bias = pick(alog, 0), pick(dtb, 0)
    beta = _sigmoid(b_raw)
    g = -jnp.exp(a_log) * _softplus(a_raw + dt_bias)
    ri = lax.broadcasted_iota(jnp.int32, (c, c), 0)
    ci = lax.broadcasted_iota(jnp.int32, (c, c), 1)
    tril = ci <= ri
    lower = jnp.broadcast_to(tril.astype(F32), (n_g, c, c))
    gc_col = _dot_f32(lower, g * jnp.ones((1, 1, HEAD), F32))[:, :, :1]
    gc_row = _dot_f32(jnp.ones((n_g, 8, c), F32), g * (ri <= ci).astype(F32)[None])[:, 0:1, :]
    gc_last = jnp.sum(g, axis=1, keepdims=True)
    decay = jnp.exp(jnp.where(tril, gc_col - gc_row, NEG))
    e_gc = jnp.exp(gc_col)
    kb = k * beta
    a_mat = jnp.where(ci < ri, mm_nt(kb, k) * decay, 0.0)
    t_inv = _inv_unit_lower(a_mat) if t_stored is None else _inv_lookup(a_mat, t_stored)
    u_base = mm_nn(t_inv, v * beta)
    w_dec = mm_nn(t_inv, kb * e_gc)
    attn = jnp.where(tril, mm_nt(q, k) * decay, 0.0)
    u = u_base - mm_nn(w_dec, state)
    o = mm_nn(q * e_gc, state) + mm_nn(attn, u)
    new_state = state * jnp.exp(gc_last) + mm_tn(k * jnp.exp(gc_last - gc_col), u)
    return o, new_state, t_inv


DELTA_CHUNKS_FWD = 2
DELTA_CHUNKS_BWD = 1


def _qkv_heads(ref, rs, part):
    return jnp.stack([ref[rs, (part * N_HEADS + g) * HEAD:(part * N_HEADS + g + 1) * HEAD] for g in range(N_HEADS)])


def delta_fwd(qkv, ba, alog, dtb, lp):
    t = qkv.shape[0]
    nb, nc = t // lp, lp // CHUNK
    cps = DELTA_CHUNKS_FWD
    ng, rows = nc // cps, cps * CHUNK
    assert nc % cps == 0

    def body(qkv_ref, ba_ref, al_ref, dt_ref, o_ref, s_ref, t_ref, state_ref):
        @pl.when(pl.program_id(1) == 0)
        def _():
            state_ref[...] = jnp.zeros_like(state_ref)

        al, dtv = al_ref[...], dt_ref[...]
        for c in range(cps):
            rs = slice(c * CHUNK, (c + 1) * CHUNK)
            state = state_ref[...]
            o, new_state, t_inv = _delta_chunk(_qkv_heads(qkv_ref, rs, 0), _qkv_heads(qkv_ref, rs, 1), _qkv_heads(qkv_ref, rs, 2),
                                               ba_ref[rs, :], al, dtv, state, None, 0)
            for g in range(N_HEADS):
                o_ref[rs, g * HEAD:(g + 1) * HEAD] = o[g]
                s_ref[g, c] = state[g]
                t_ref[g, c] = t_inv[g]
            state_ref[...] = new_state

    rows_of = lambda width: pl.BlockSpec((rows, width), lambda b, n: (b * ng + n, 0))
    par_spec = pl.BlockSpec((1, HEAD), lambda b, n: (0, 0))
    return pl.pallas_call(
        body, grid=(nb, ng), in_specs=[rows_of(3 * N_HEADS * HEAD), rows_of(HEAD), par_spec, par_spec],
        out_specs=[rows_of(N_HEADS * HEAD), pl.BlockSpec((None, N_HEADS, cps, HEAD, HEAD), lambda b, n: (b, 0, n, 0, 0)),
                   pl.BlockSpec((None, N_HEADS, cps, CHUNK, CHUNK), lambda b, n: (b, 0, n, 0, 0))],
        out_shape=[jax.ShapeDtypeStruct((t, N_HEADS * HEAD), F32), jax.ShapeDtypeStruct((nb, N_HEADS, nc, HEAD, HEAD), F32),
                   jax.ShapeDtypeStruct((nb, N_HEADS, nc, CHUNK, CHUNK), F32)],
        scratch_shapes=[pltpu.VMEM((N_HEADS, HEAD, HEAD), F32)],
        compiler_params=_cparams(("arbitrary", "arbitrary")), name="delta_fwd")(qkv, ba, alog, dtb)


def delta_bwd(qkv, ba, alog, dtb, states, t_invs, do, lp):
    t = qkv.shape[0]
    nb, nc = t // lp, lp // CHUNK
    cps = DELTA_CHUNKS_BWD
    ng, rows = nc // cps, cps * CHUNK

    def body(qkv_ref, ba_ref, al_ref, dt_ref, s_ref, t_ref, do_ref, dqkv_ref, dba_ref, dal_ref, ddt_ref, dstate_ref):
        b, step = pl.program_id(0), pl.program_id(1)

        @pl.when(step == 0)
        def _():
            dstate_ref[...] = jnp.zeros_like(dstate_ref)

        @pl.when((b == 0) & (step == 0))
        def _():
            dal_ref[...] = jnp.zeros_like(dal_ref)
            ddt_ref[...] = jnp.zeros_like(ddt_ref)

        al, dtv = al_ref[...], dt_ref[...]
        d_al = jnp.zeros((1, HEAD), F32)
        d_dt = jnp.zeros((1, HEAD), F32)
        for c in reversed(range(cps)):
            rs = slice(c * CHUNK, (c + 1) * CHUNK)
            t_n = jnp.stack([t_ref[g, c] for g in range(N_HEADS)])
            s_n = jnp.stack([s_ref[g, c] for g in range(N_HEADS)])
            d_o = jnp.stack([do_ref[rs, g * HEAD:(g + 1) * HEAD] for g in range(N_HEADS)])

            def f(q_, k_, v_, ba_, al_, dt_, s_, t_n=t_n):
                return _delta_chunk(q_, k_, v_, ba_, al_, dt_, s_, t_n, 0)[:2]

            _, vjp = jax.vjp(f, _qkv_heads(qkv_ref, rs, 0), _qkv_heads(qkv_ref, rs, 1), _qkv_heads(qkv_ref, rs, 2), ba_ref[rs, :], al, dtv, s_n)
            grads = vjp((d_o, dstate_ref[...]))
            for part in range(3):
                for g in range(N_HEADS):
                    dqkv_ref[rs, (part * N_HEADS + g) * HEAD:(part * N_HEADS + g + 1) * HEAD] = grads[part][g]
            dba_ref[rs, :] = grads[3]
            d_al, d_dt = d_al + grads[4], d_dt + grads[5]
            dstate_ref[...] = grads[6]
        dal_ref[...] += d_al
        ddt_ref[...] += d_dt

    rows_of = lambda width: pl.BlockSpec((rows, width), lambda b, n: (b * ng + ng - 1 - n, 0))
    par_spec = pl.BlockSpec((1, HEAD), lambda b, n: (0, 0))
    return pl.pallas_call(
        body, grid=(nb, ng),
        in_specs=[rows_of(3 * N_HEADS * HEAD), rows_of(HEAD), par_spec, par_spec,
                  pl.BlockSpec((None, N_HEADS, cps, HEAD, HEAD), lambda b, n: (b, 0, ng - 1 - n, 0, 0)),
                  pl.BlockSpec((None, N_HEADS, cps, CHUNK, CHUNK), lambda b, n: (b, 0, ng - 1 - n, 0, 0)), rows_of(N_HEADS * HEAD)],
        out_specs=[rows_of(3 * N_HEADS * HEAD), rows_of(HEAD), par_spec, par_spec],
        out_shape=[jax.ShapeDtypeStruct((t, 3 * N_HEADS * HEAD), F32), jax.ShapeDtypeStruct((t, HEAD), F32),
                   jax.ShapeDtypeStruct((1, HEAD), F32), jax.ShapeDtypeStruct((1, HEAD), F32)],
        scratch_shapes=[pltpu.VMEM((N_HEADS, HEAD, HEAD), F32)],
        compiler_params=_cparams(("arbitrary", "arbitrary")), name="delta_bwd")(qkv, ba, alog, dtb, states, t_invs, do)


ATT_Q_TILE = 256
ATT_K_TILE = 512
ATT_SCALE = QK_DIM ** -0.5


def _tiles(end, size):
    return [(s, min(s + size, end)) for s in range(0, end, size)]


def _att_visible(q0, q1, k0, k1, keys_first):
    if k1 <= q0 + CHUNK and k0 >= PAD_ROWS:
        return None
    shape = (k1 - k0, q1 - q0) if keys_first else (q1 - q0, k1 - k0)
    qpos = q0 + lax.broadcasted_iota(jnp.int32, shape, 1 if keys_first else 0)
    kpos = k0 + lax.broadcasted_iota(jnp.int32, shape, 0 if keys_first else 1)
    shift = CHUNK.bit_length() - 1
    return (jnp.right_shift(kpos, shift) <= jnp.right_shift(qpos, shift)) & (kpos >= PAD_ROWS)


def _att_seq_specs(lp):
    return pl.BlockSpec((lp, QK_PAD), lambda b, h: (b, h)), pl.BlockSpec((lp, HEAD), lambda b, h: (b, h))


def flash_fwd(q, k, v, lp):
    t = q.shape[0]
    qk_seq, o_seq = _att_seq_specs(lp)

    def body(q_ref, k_ref, v_ref, o_ref, lse_ref):
        for q0, q1 in _tiles(lp, ATT_Q_TILE):
            qb = q_ref[q0:q1, :]
            k_tiles = _tiles(q1, ATT_K_TILE)
            scores, m = [], None
            for k0, k1 in k_tiles:
                s = mm_nt(qb, k_ref[k0:k1, :]) * ATT_SCALE
                vis = _att_visible(q0, q1, k0, k1, False)
                s = s if vis is None else jnp.where(vis, s, NEG)
                scores.append(s)
                row_max = jnp.max(s, -1, keepdims=True)
                m = row_max if m is None else jnp.maximum(m, row_max)
            l = jnp.zeros((q1 - q0, 1), F32)
            acc = jnp.zeros((q1 - q0, HEAD), F32)
            for s, (k0, k1) in zip(scores, k_tiles, strict=True):
                p = jnp.exp(s - m)
                l = l + jnp.sum(p, -1, keepdims=True)
                acc = acc + mm_nn(p, v_ref[k0:k1, :])
            o_ref[q0:q1, :] = acc / l
            lse_ref[q0:q1, :] = jnp.broadcast_to(m + jnp.log(l), (q1 - q0, HEAD))

    big = jax.ShapeDtypeStruct((t, N_HEADS * HEAD), F32)
    return pl.pallas_call(
        body, grid=(t // lp, N_HEADS), in_specs=[qk_seq, qk_seq, o_seq], out_specs=[o_seq, o_seq], out_shape=[big, big],
        compiler_params=_cparams(("arbitrary", "arbitrary")), name="flash_fwd")(q, k, v)


def flash_bwd(q, k, v, o, lse, do, lp):
    t = q.shape[0]
    qk_seq, o_seq = _att_seq_specs(lp)

    def body(q_ref, k_ref, v_ref, o_ref, lse_ref, do_ref, dq_ref, dk_ref, dv_ref):
        dk_ref[...] = jnp.zeros_like(dk_ref)
        dv_ref[...] = jnp.zeros_like(dv_ref)
        for q0, q1 in _tiles(lp, ATT_Q_TILE):
            qb, dob = q_ref[q0:q1, :], do_ref[q0:q1, :]
            lse_row = jnp.transpose(lse_ref[q0:q1, :])[0:1, :]
            dsum_row = jnp.sum(jnp.transpose(dob * o_ref[q0:q1, :]), axis=0, keepdims=True)
            dq = jnp.zeros((q1 - q0, QK_PAD), F32)
            for k0, k1 in _tiles(q1, ATT_K_TILE):
                kb, vb = k_ref[k0:k1, :], v_ref[k0:k1, :]
                s = mm_nt(kb, qb) * ATT_SCALE
                vis = _att_visible(q0, q1, k0, k1, True)
                s = s if vis is None else jnp.where(vis, s, NEG)
                p = jnp.exp(s - lse_row)
                ds = p * (mm_nt(vb, dob) - dsum_row) * ATT_SCALE
                dv_ref[k0:k1, :] += mm_nn(p, dob)
                dk_ref[k0:k1, :] += mm_nn(ds, qb)
                dq = dq + mm_tn(ds, kb)
            dq_ref[q0:q1, :] = dq

    return pl.pallas_call(
        body, grid=(t // lp, N_HEADS), in_specs=[qk_seq, qk_seq, o_seq, o_seq, o_seq, o_seq], out_specs=[qk_seq, qk_seq, o_seq],
        out_shape=[jax.ShapeDtypeStruct((t, N_HEADS * QK_PAD), F32), jax.ShapeDtypeStruct((t, N_HEADS * QK_PAD), F32),
                   jax.ShapeDtypeStruct((t, N_HEADS * HEAD), F32)],
        compiler_params=_cparams(("arbitrary", "arbitrary")), name="flash_bwd")(q, k, v, o, lse, do)


def loss_head(h2, target, lp):
    nb, seq, d = target.shape
    tr = 128
    nblk = lp // tr
    lead_blocks = LEAD // tr

    def body(h_ref, t_ref, loss_ref, dh_ref, acc_ref):
        b, i = pl.program_id(0), pl.program_id(1)

        @pl.when((b == 0) & (i == 0))
        def _():
            acc_ref[...] = jnp.zeros_like(acc_ref)

        @pl.when(i < lead_blocks)
        def _():
            dh_ref[...] = jnp.zeros_like(dh_ref)

        @pl.when(i >= lead_blocks)
        def _():
            err = h_ref[...] - t_ref[...]
            dh_ref[...] = err * (1.0 / d)
            acc_ref[...] += jnp.sum(err * err, axis=0, keepdims=True)

        @pl.when((b == nb - 1) & (i == nblk - 1))
        def _():
            loss_ref[...] = jnp.sum(acc_ref[...], axis=1, keepdims=True) * (0.5 / d)

    return pl.pallas_call(
        body, grid=(nb, nblk),
        in_specs=[pl.BlockSpec((None, tr, d), lambda b, i: (b, i, 0)),
                  pl.BlockSpec((None, tr, d), lambda b, i: (b, jnp.maximum(i - lead_blocks, 0), 0))],
        out_specs=[pl.BlockSpec((1, 1), lambda b, i: (0, 0)), pl.BlockSpec((None, tr, d), lambda b, i: (b, i, 0))],
        out_shape=[jax.ShapeDtypeStruct((1, 1), F32), jax.ShapeDtypeStruct((nb, lp, d), F32)],
        scratch_shapes=[pltpu.VMEM((1, d), F32)], compiler_params=_cparams(("arbitrary", "arbitrary")), name="loss_head")(h2, target)


def meta_grad(dh0):
    nb, _, d = dh0.shape

    def body(g_ref, o_ref):
        @pl.when(pl.program_id(0) == 0)
        def _():
            o_ref[...] = jnp.zeros_like(o_ref)

        o_ref[...] += g_ref[PAD_ROWS:LEAD, :]

    return pl.pallas_call(
        body, grid=(nb,), in_specs=[pl.BlockSpec((None, LEAD, d), lambda b: (b, 0, 0))],
        out_specs=pl.BlockSpec((N_META, d), lambda b: (0, 0)), out_shape=jax.ShapeDtypeStruct((N_META, d), F32),
        compiler_params=_cparams(("arbitrary",)), name="meta_grad")(dh0)


_HBM = pl.BlockSpec(memory_space=pltpu.HBM)


def _mesh_pos():
    x, y, c = lax.axis_index("x"), lax.axis_index("y"), lax.axis_index("c")
    return x, y, c


def _peer(x, y, c, k):
    px = 1 - x if k & 4 else x
    py = 1 - y if k & 2 else y
    pc = 1 - c if k & 1 else c
    return (px, py, pc), 4 * px + 2 * py + pc


def _exchange(name, bufs, scatter):
    n = len(bufs)

    def body(*refs):
        x_refs, out_refs = refs[:n], refs[n:2 * n]
        send_sems, recv_sems, local_sems = refs[2 * n:]
        x, y, c = _mesh_pos()
        me = 4 * x + 2 * y + c
        local, sends = [], []
        for i in range(n):
            cp = pltpu.make_async_copy(x_refs[i].at[me] if scatter else x_refs[i], out_refs[i].at[me], local_sems.at[i])
            cp.start()
            local.append(cp)

        def copy(i, k, landing):
            peer, peer_id = _peer(x, y, c, k)
            src = x_refs[i].at[peer_id] if scatter else x_refs[i]
            return pltpu.make_async_remote_copy(src_ref=src, dst_ref=out_refs[i].at[peer_id if landing else me], send_sem=send_sems.at[i, k - 1],
                                                recv_sem=recv_sems.at[i, k - 1], device_id=peer, device_id_type=pl.DeviceIdType.MESH)

        for k in range(1, N_DEV):
            for i in range(n):
                cp = copy(i, k, False)
                cp.start()
                sends.append(cp)
        for k in range(1, N_DEV):
            for i in range(n):
                copy(i, k, True).wait_recv()
        for cp in sends:
            cp.wait_send()
        for cp in local:
            cp.wait()

    out_shape = [jax.ShapeDtypeStruct(b.shape if scatter else (N_DEV,) + b.shape, b.dtype) for b in bufs]
    return pl.pallas_call(
        body, in_specs=[_HBM] * n, out_specs=[_HBM] * n, out_shape=out_shape,
        scratch_shapes=[pltpu.SemaphoreType.DMA((n, N_DEV - 1)), pltpu.SemaphoreType.DMA((n, N_DEV - 1)), pltpu.SemaphoreType.DMA((n,))],
        name=name)(*bufs)


def _f_rms(x, g):
    return (_rms(x, g),)


def _f_rms2(x, g1, g2):
    r = x * lax.rsqrt(jnp.sum(x * x, -1, keepdims=True) / x.shape[-1] + EPS)
    return r * g1, r * g2


def _f_out_gate(o, gate, gain):
    return (_rms(o, gain) * _silu(gate),)


def _f_gate(o, gate):
    return (o * _silu(gate),)


@jax.custom_vjp
def _swap_rope_halves(x):
    half = ROPE // 2
    lane = lax.broadcasted_iota(jnp.int32, x.shape, 1)
    return jnp.where(lane < half, pltpu.roll(x, HEAD - half, 1), jnp.where(lane < ROPE, pltpu.roll(x, half, 1), 0.0))


_swap_rope_halves.defvjp(lambda x: (_swap_rope_halves(x), None), lambda _, g: (_swap_rope_halves(g),))


def _f_qk_final(nope, rope_in, g_nope, g_rope, cos, sin):
    ms = (jnp.sum(nope * nope, -1, keepdims=True) + jnp.sum(rope_in * rope_in, -1, keepdims=True)) / QK_DIM
    r = lax.rsqrt(ms + EPS)
    a = nope * r * g_nope
    b = rope_in * r * g_rope
    return (jnp.concatenate([a, b * cos + _swap_rope_halves(b) * sin], axis=1),)


def _rope_tables(lp):
    half = ROPE // 2
    pos = jnp.maximum(jnp.arange(lp) - PAD_ROWS, 0)
    inv = ROPE_THETA ** (-jnp.arange(half, dtype=F32) / half)
    ang = pos.astype(F32)[:, None] * inv[None, :]
    zeros = jnp.zeros((lp, HEAD - ROPE), F32)
    cos = jnp.concatenate([jnp.cos(ang), jnp.cos(ang), zeros], 1)
    sin = jnp.concatenate([-jnp.sin(ang), jnp.sin(ang), zeros], 1)
    return cos, sin


def _pad_lanes(w, width=HEAD):
    return jnp.pad(w, ((0, 0), (0, width - w.shape[1])))


def _pad_rows(w, rows=HEAD):
    return jnp.pad(w, ((0, rows - w.shape[0]), (0, 0)))


def _split_heads_qk_t(w_t):
    k = w_t.shape[1]
    w3 = w_t.reshape(N_HEADS, QK_DIM, k)
    nope = w3[:, :HEAD].reshape(N_HEADS * HEAD, k)
    rope = jnp.pad(w3[:, HEAD:], ((0, 0), (0, HEAD - ROPE), (0, 0))).reshape(N_HEADS * HEAD, k)
    return jnp.concatenate([nope, rope], 0)


def _merge_heads_qk_t(g_t):
    k = g_t.shape[1]
    kw = N_HEADS * HEAD
    nope, rope = g_t[:kw].reshape(N_HEADS, HEAD, k), g_t[kw:].reshape(N_HEADS, HEAD, k)[:, :ROPE]
    return jnp.concatenate([nope, rope], 1).reshape(N_HEADS * QK_DIM, k)


def local_step(x, target, w):
    nb, seq, d = x.shape
    lp = seq + LEAD
    t = nb * lp
    tr = _pick(lp, (544, 128))
    ntab = lp // tr
    mxu = _MXU_DTYPE
    kw = N_HEADS * HEAD

    a_w_in_t = w["a_w_in"].astype(mxu)
    w_qkv_t, w_ga_t, w_ba_t = a_w_in_t[:3 * kw], a_w_in_t[3 * kw:4 * kw], _pad_rows(a_w_in_t[4 * kw:])
    a_conv = w["a_conv"].T
    a_w_out = w["a_w_out"].astype(mxu)
    alog, dtb, o_gain = _pad_lanes(w["a_log"]), _pad_lanes(w["a_dt_bias"]), w["a_o_gain"]
    w_dkv, w_dpe = w["kv_w_down"][:, :KV_RANK].astype(mxu), _pad_lanes(w["kv_w_down"][:, KV_RANK:]).astype(mxu)
    w_ukv_t = jnp.concatenate([w["kv_w_uk"], w["kv_w_uv"]], 0).astype(mxu)
    b_w_in_t = w["b_w_in"].astype(mxu)
    w_cq_t, w_gb_t = b_w_in_t[:Q_RANK], b_w_in_t[Q_RANK:]
    w_q_t = _split_heads_qk_t(w["b_w_uq"]).astype(mxu)
    b_w_out = w["b_w_out"].astype(mxu)
    a_norm, kv_norm, b_norm = w["a_norm"], w["kv_norm"][None, :], w["b_norm"]
    lat_norm, qlat_norm = w["kv_latent_norm"][None, :], w["b_q_latent_norm"]
    kg_nope, kg_rope = w["k_gain"][None, :HEAD], _pad_lanes(w["k_gain"][None, HEAD:])
    qg_nope, qg_rope = w["b_q_gain"][:, :HEAD], _pad_lanes(w["b_q_gain"][:, HEAD:])
    cos, sin = _rope_tables(lp)

    meta = jnp.broadcast_to(w["meta_tokens"].T[None], (nb, N_META, d))
    h0 = jnp.concatenate([jnp.zeros((nb, PAD_ROWS, d), F32), meta, x], 1).reshape(t, d)
    (hn,) = row_call("a_norm_fwd", _f_rms, [Arg(h0), Arg(a_norm, "par")], [(d, mxu, d, False)], tr)
    z_qkv = matmul("a_in_qkv", hn, w_qkv_t, "nt")
    gate_a = matmul("a_in_gate", hn, w_ga_t, "nt")
    z_ba = matmul("a_in_ba", hn, w_ba_t, "nt")
    qkv_a = conv_fwd(z_qkv, a_conv, lp)
    o_a, states, t_invs = delta_fwd(qkv_a, z_ba, alog, dtb, lp)
    og_args = [Arg(o_a, bc=HEAD, ph=True, diff=True), Arg(gate_a, bc=HEAD, ph=True, diff=True, gdt=mxu), Arg(o_gain, "par", diff=True)]
    (og_a,) = row_call("a_out_gate_fwd", _f_out_gate, og_args, [(kw, mxu, HEAD, True)], tr, nh=N_HEADS)
    h1 = matmul("a_out", og_a, a_w_out, "nn", res=h0)

    hk, hb = row_call("b_norms_fwd", _f_rms2, [Arg(h1), Arg(kv_norm, "par"), Arg(b_norm, "par")], [(d, mxu, d, False), (d, mxu, d, False)], tr)
    c_kv_raw = matmul("kv_down", hk, w_dkv, "nn")
    k_pe = matmul("kv_down_pe", hk, w_dpe, "nn")
    c_q_raw = matmul("b_in_q", hb, w_cq_t, "nt")
    gate_b = matmul("b_in_gate", hb, w_gb_t, "nt")
    (c_kv,) = row_call("kv_latent_fwd", _f_rms, [Arg(c_kv_raw), Arg(lat_norm, "par")], [(KV_RANK, mxu, KV_RANK, False)], tr)
    (c_q,) = row_call("q_latent_fwd", _f_rms, [Arg(c_q_raw), Arg(qlat_norm, "par")], [(Q_RANK, mxu, Q_RANK, False)], tr)
    k_nope = matmul("k_up", c_kv, w_ukv_t[:kw], "nt")
    v_b = matmul("v_up", c_kv, w_ukv_t[kw:], "nt", out_dtype=mxu)
    q_up = matmul("q_up", c_q, w_q_t, "nt")
    tabs = [Arg(cos, "tab"), Arg(sin, "tab")]
    k_args = [Arg(k_nope, bc=HEAD, ph=True, diff=True, gdt=mxu), Arg(k_pe, diff=True), Arg(kg_nope, "par", diff=True), Arg(kg_rope, "par", diff=True)] + tabs
    q_args = [Arg(q_up, bc=HEAD, ph=True, diff=True, gdt=mxu), Arg(q_up, bc=HEAD, base=N_HEADS, ph=True, diff=True, gdt=mxu),
              Arg(qg_nope, "par", diff=True), Arg(qg_rope, "par", diff=True)] + tabs
    (k_fin,) = row_call("k_final_fwd", _f_qk_final, k_args, [(N_HEADS * QK_PAD, mxu, QK_PAD, True)], tr, nh=N_HEADS, ntab=ntab)
    (q_fin,) = row_call("q_final_fwd", _f_qk_final, q_args, [(N_HEADS * QK_PAD, mxu, QK_PAD, True)], tr, nh=N_HEADS, ntab=ntab)
    o_b, lse = flash_fwd(q_fin, k_fin, v_b, lp)
    gb_args = [Arg(o_b, diff=True), Arg(gate_b, diff=True, gdt=mxu)]
    (og_b,) = row_call("b_gate_fwd", _f_gate, gb_args, [(kw, mxu, kw, False)], tr)
    h2 = matmul("b_out", og_b, b_w_out, "nn", res=h1)

    loss, dh2 = loss_head(h2.reshape(nb, lp, d), target, lp)
    dh2 = dh2.reshape(t, d)
    grads = {}

    d_og_b = matmul("b_out_dx", dh2, b_w_out, "nt")
    grads["b_w_out"] = matmul("b_out_dw", og_b, dh2, "tn")
    d_o_b, d_gate_b = row_vjp_call("b_gate_bwd", _f_gate, gb_args, [Arg(d_og_b)], tr)
    dq_fin, dk_fin, dv_b = flash_bwd(q_fin, k_fin, v_b, o_b, lse, d_o_b, lp)
    dq_nope, dq_rope, d_qg_nope, d_qg_rope = row_vjp_call(
        "q_final_bwd", _f_qk_final, q_args, [Arg(dq_fin, bc=QK_PAD, ph=True)], tr, nh=N_HEADS, ntab=ntab)
    dk_nope, dk_pe, d_kg_nope, d_kg_rope = row_vjp_call(
        "k_final_bwd", _f_qk_final, k_args, [Arg(dk_fin, bc=QK_PAD, ph=True)], tr, nh=N_HEADS, ntab=ntab)
    grads["b_q_gain"] = jnp.concatenate([d_qg_nope, d_qg_rope[:, :ROPE]], 1)
    grads["k_gain"] = jnp.concatenate([d_kg_nope, d_kg_rope[:, :ROPE]], 1)[0]
    d_c_q = matmul("q_nope_dx", dq_nope, w_q_t[:kw], "nn")
    d_c_q = matmul("q_rope_dx", dq_rope, w_q_t[kw:], "nn", res=d_c_q)
    grads["b_w_uq"] = _merge_heads_qk_t(jnp.concatenate([matmul("q_nope_dw", dq_nope, c_q, "tn"), matmul("q_rope_dw", dq_rope, c_q, "tn")], 0))
    d_c_kv = matmul("k_up_dx", dk_nope, w_ukv_t[:kw], "nn")
    d_c_kv = matmul("v_up_dx", dv_b, w_ukv_t[kw:], "nn", res=d_c_kv)
    grads["kv_w_uk"], grads["kv_w_uv"] = matmul("k_up_dw", dk_nope, c_kv, "tn"), matmul("v_up_dw", dv_b, c_kv, "tn")
    d_c_q_raw, grads["b_q_latent_norm"] = row_vjp_call(
        "q_latent_bwd", _f_rms, [Arg(c_q_raw, diff=True, gdt=mxu), Arg(qlat_norm, "par", diff=True)], [Arg(d_c_q)], tr)
    d_c_kv_raw, d_lat = row_vjp_call(
        "kv_latent_bwd", _f_rms, [Arg(c_kv_raw, diff=True, gdt=mxu), Arg(lat_norm, "par", diff=True)], [Arg(d_c_kv)], tr)
    grads["kv_latent_norm"] = d_lat[0]
    d_hb = matmul("b_in_q_dx", d_c_q_raw, w_cq_t, "nn")
    d_hb = matmul("b_in_gate_dx", d_gate_b, w_gb_t, "nn", res=d_hb)
    grads["b_w_in"] = jnp.concatenate([matmul("b_in_q_dw", d_c_q_raw, hb, "tn"), matmul("b_in_gate_dw", d_gate_b, hb, "tn")], 0)
    d_hk = matmul("kv_down_dx", d_c_kv_raw, w_dkv, "nt")
    d_hk = matmul("kv_down_pe_dx", dk_pe, w_dpe, "nt", res=d_hk)
    grads["kv_w_down"] = jnp.concatenate([matmul("kv_down_dw", hk, d_c_kv_raw, "tn"), matmul("kv_down_pe_dw", hk, dk_pe, "tn")[:, :ROPE]], 1)
    d_h1_norms, d_kv_norm, grads["b_norm"] = row_vjp_call(
        "b_norms_bwd", _f_rms2, [Arg(h1, diff=True), Arg(kv_norm, "par", diff=True), Arg(b_norm, "par", diff=True)], [Arg(d_hk), Arg(d_hb)], tr)
    grads["kv_norm"] = d_kv_norm[0]

    (dh1,) = row_call("dh1_sum", lambda a, b: (a + b,), [Arg(dh2), Arg(d_h1_norms)], [(d, F32, d, False)], tr)
    d_og_a = matmul("a_out_dx", dh1, a_w_out, "nt")
    grads["a_w_out"] = matmul("a_out_dw", og_a, dh1, "tn")
    d_o_a, d_gate_a, grads["a_o_gain"] = row_vjp_call(
        "a_out_gate_bwd", _f_out_gate, og_args, [Arg(d_og_a, bc=HEAD, ph=True)], tr, nh=N_HEADS)
    dqkv_a, d_ba, d_alog, d_dtb = delta_bwd(qkv_a, z_ba, alog, dtb, states, t_invs, d_o_a, lp)
    grads["a_log"], grads["a_dt_bias"] = d_alog[:, :N_HEADS], d_dtb[:, :N_HEADS]
    dz_qkv, d_conv = conv_bwd(z_qkv, a_conv, dqkv_a, lp)
    grads["a_conv"] = d_conv.T
    d_hn = matmul("a_in_qkv_dx", dz_qkv, w_qkv_t, "nn")
    d_hn = matmul("a_in_gate_dx", d_gate_a, w_ga_t, "nn", res=d_hn)
    d_hn = matmul("a_in_ba_dx", d_ba, w_ba_t, "nn", res=d_hn)
    grads["a_w_in"] = jnp.concatenate([matmul("a_in_qkv_dw", dz_qkv, hn, "tn"), matmul("a_in_gate_dw", d_gate_a, hn, "tn"),
                                       matmul("a_in_ba_dw", d_ba, hn, "tn")[:2 * N_HEADS]], 0)
    d_h0_norm, grads["a_norm"] = row_vjp_call("a_norm_bwd", _f_rms, [Arg(h0, diff=True), Arg(a_norm, "par", diff=True)], [Arg(d_hn)], tr)
    (dh0,) = row_call("dh0_sum", lambda a, b: (a + b,), [Arg(dh1), Arg(d_h0_norm)], [(d, F32, d, False)], tr)
    dh0 = dh0.reshape(nb, lp, d)
    grads["meta_tokens"] = meta_grad(dh0).T
    return loss, dh0[:, LEAD:], grads


_SHARDED = (
    ("meta_tokens", True, False), ("a_norm", True, False), ("a_w_in", True, True), ("a_conv", True, False), ("a_w_out", False, True),
    ("kv_w_down", False, True), ("kv_w_uk", True, True), ("kv_w_uv", True, True), ("b_w_in", True, True), ("b_w_uq", True, True),
    ("b_w_out", False, True))
_REPLICATED = ("a_log", "a_dt_bias", "a_o_gain", "kv_norm", "kv_latent_norm", "k_gain", "b_norm", "b_q_latent_norm", "b_q_gain")
_ALL_WEIGHTS = ("meta_tokens", "a_norm", "a_w_in", "a_conv", "a_log", "a_dt_bias", "a_o_gain", "a_w_out", "kv_norm", "kv_w_down",
                "kv_latent_norm", "kv_w_uk", "kv_w_uv", "k_gain", "b_norm", "b_w_in", "b_q_latent_norm", "b_w_uq", "b_q_gain", "b_w_out")


def _round_up(n, m):
    return (n + m - 1) // m * m


def _pack_rows(pieces, row_multiple):
    padded = []
    for p in pieces:
        n = p.shape[-1]
        padded.append(jnp.pad(p, [(0, 0)] * (p.ndim - 1) + [(0, _round_up(n, PACK_COLS) - n)]))
    flat = jnp.concatenate(padded, -1)
    rows = _round_up(flat.shape[-1] // PACK_COLS, row_multiple)
    flat = jnp.pad(flat, [(0, 0)] * (flat.ndim - 1) + [(0, rows * PACK_COLS - flat.shape[-1])])
    return flat.reshape(flat.shape[:-1] + (rows, PACK_COLS))


def _unpack_rows(buf, sizes):
    flat = buf.reshape(buf.shape[:-2] + (-1,))
    out, off = [], 0
    for n in sizes:
        out.append(flat[..., off:off + n])
        off += _round_up(n, PACK_COLS)
    return out


def _shard_2d(a):
    return a.reshape(a.shape[-2:]) if a.ndim > 2 else a


def _kl_shard(a, by_cols):
    return _shard_2d(a).T if by_cols else _shard_2d(a)


_WIDE_GROUPS = (("a_w_in", "b_w_in", "a_w_out", "b_w_out"), ("b_w_uq",), ("kv_w_down",), ("kv_w_uk", "kv_w_uv"))
_SMALL_SHARDED = ("meta_tokens", "a_norm", "a_conv")
_BY_COLS = {name: by_cols for name, by_cols, _ in _SHARDED}
ROW_ALIGN = 16


def _stack_rows(pieces):
    padded, starts, row = [], [], 0
    for p in pieces:
        r = p.shape[-2]
        padded.append(jnp.pad(p, [(0, 0)] * (p.ndim - 2) + [(0, _round_up(r, ROW_ALIGN) - r), (0, 0)]))
        starts.append(row)
        row += _round_up(r, ROW_ALIGN)
    return jnp.concatenate(padded, -2), starts


def gather_weights(local):
    bufs, layout = [], []
    for names in _WIDE_GROUPS:
        shards = [_kl_shard(local[n], _BY_COLS[n]).astype(BF16) for n in names]
        buf, starts = _stack_rows(shards)
        bufs.append(buf)
        layout.append([(n, s, sh.shape[0]) for n, s, sh in zip(names, starts, shards, strict=True)])
    small = [_kl_shard(local[n], _BY_COLS[n]) for n in _SMALL_SHARDED]
    bufs.append(_pack_rows([s.reshape(-1) for s in small], 8))
    gathered = _exchange("all_gather", bufs, scatter=False)
    full = {}
    for got, entries in zip(gathered, layout):
        for name, start, rows in entries:
            full[name] = got[:, start:start + rows].reshape(N_DEV * rows, got.shape[-1])
    for name, part, sh in zip(_SMALL_SHARDED, _unpack_rows(gathered[-1], [s.size for s in small]), small, strict=True):
        full[name] = part.reshape(N_DEV * sh.shape[0], sh.shape[1])
    full["a_norm"] = full["a_norm"].reshape(1, -1)
    return full


def reduce_contributions(name, recv):
    _, r, c = recv.shape
    tr = _pick(r, (256, 128, 64, 32, 16, 8))

    def body(g_ref, o_ref):
        g = g_ref[0].astype(F32)
        for dev in range(1, N_DEV):
            g = g + g_ref[dev].astype(F32)
        o_ref[...] = g

    return pl.pallas_call(
        body, grid=(r // tr,), in_specs=[pl.BlockSpec((N_DEV, tr, c), lambda i: (0, i, 0))], out_specs=pl.BlockSpec((tr, c), lambda i: (i, 0)),
        out_shape=jax.ShapeDtypeStruct((r, c), F32), compiler_params=_cparams(("arbitrary",)), name=name)(recv)


def adamw_all(gs, ws, ms, vs):
    n = len(gs)

    def body(*refs):
        for i in range(n):
            g_ref, w_ref, m_ref, v_ref = (refs[j * n + i] for j in range(4))
            d_ref, mo_ref, vo_ref = (refs[(4 + j) * n + i] for j in range(3))
            g = g_ref[...]
            m_new = ADAM_B1 * m_ref[...] + (1.0 - ADAM_B1) * g
            v_new = ADAM_B2 * v_ref[...] + (1.0 - ADAM_B2) * (g * g)
            m_hat = m_new / (1.0 - ADAM_B1 ** ADAM_STEP)
            v_hat = v_new / (1.0 - ADAM_B2 ** ADAM_STEP)
            d_ref[...] = -ADAM_LR * (m_hat / (jnp.sqrt(v_hat) + ADAM_EPS) + ADAM_WD * w_ref[...])
            mo_ref[...] = m_new
            vo_ref[...] = v_new

    out = [jax.ShapeDtypeStruct(g.shape, F32) for g in gs] * 3
    res = pl.pallas_call(body, out_shape=out, compiler_params=pltpu.CompilerParams(vmem_limit_bytes=VMEM_LIMIT), name="adamw_all")(*gs, *ws, *ms, *vs)
    return res[:n], res[n:2 * n], res[2 * n:]


def kernel(x, meta_tokens, a_norm, a_w_in, a_conv, a_log, a_dt_bias, a_o_gain, a_w_out, kv_norm, kv_w_down, kv_latent_norm, kv_w_uk, kv_w_uv, k_gain, b_norm, b_w_in, b_q_latent_norm, b_w_uq, b_q_gain, b_w_out, loss_target, m_meta_tokens, m_a_norm, m_a_w_in, m_a_conv, m_a_log, m_a_dt_bias, m_a_o_gain, m_a_w_out, m_kv_norm, m_kv_w_down, m_kv_latent_norm, m_kv_w_uk, m_kv_w_uv, m_k_gain, m_b_norm, m_b_w_in, m_b_q_latent_norm, m_b_w_uq, m_b_q_gain, m_b_w_out, v_meta_tokens, v_a_norm, v_a_w_in, v_a_conv, v_a_log, v_a_dt_bias, v_a_o_gain, v_a_w_out, v_kv_norm, v_kv_w_down, v_kv_latent_norm, v_kv_w_uk, v_kv_w_uv, v_k_gain, v_b_norm, v_b_w_in, v_b_q_latent_norm, v_b_w_uq, v_b_q_gain, v_b_w_out):
    given = dict(locals())
    local_w = {n: given[n] for n in _ALL_WEIGHTS}
    full = gather_weights(local_w)
    for n in _REPLICATED:
        full[n] = local_w[n]

    loss_part, grad_x, grads = local_step(x, loss_target, full)

    bufs, layout = [], []
    for names in _WIDE_GROUPS:
        slices = [grads[n].reshape(N_DEV, -1, grads[n].shape[-1]).astype(BF16) for n in names]
        buf, starts = _stack_rows(slices)
        bufs.append(buf)
        layout.append([(n, s, sl.shape[1]) for n, s, sl in zip(names, starts, slices, strict=True)])
    exact = [grads[n].reshape(N_DEV, -1) for n in _SMALL_SHARDED]
    exact += [jnp.broadcast_to(grads[n].reshape(1, -1), (N_DEV, grads[n].size)) for n in _REPLICATED]
    exact.append(jnp.broadcast_to(loss_part, (N_DEV, 1)))
    bufs.append(_pack_rows(exact, 8))
    received = _exchange("all_to_all", bufs, scatter=True)
    summed = [reduce_contributions(f"reduce_{i}", r) for i, r in enumerate(received)]

    grad_kl = {}
    for got, entries in zip(summed, layout):
        for n, start, rows in entries:
            grad_kl[n] = got[start:start + rows]
    parts = _unpack_rows(summed[-1], [p.shape[1] for p in exact])
    for n, part in zip(_SMALL_SHARDED + _REPLICATED, parts, strict=False):
        grad_kl[n] = part
    loss = parts[-1][0]

    def natural_2d(n, a):
        shape = _shard_2d(local_w[n]).shape if local_w[n].ndim > 1 else (1, local_w[n].size)
        return a.reshape(shape[::-1]).T if _BY_COLS.get(n, False) else a.reshape(shape)

    as_2d = lambda n, a: a.reshape(natural_2d(n, grad_kl[n]).shape)
    gs = [natural_2d(n, grad_kl[n]) for n in _ALL_WEIGHTS]
    deltas, new_m, new_v = adamw_all(gs, [as_2d(n, local_w[n]) for n in _ALL_WEIGHTS], [as_2d(n, given["m_" + n]) for n in _ALL_WEIGHTS],
                                     [as_2d(n, given["v_" + n]) for n in _ALL_WEIGHTS])
    results = [a.reshape(local_w[n].shape) for group in (gs, deltas, new_m, new_v) for n, a in zip(_ALL_WEIGHTS, group, strict=True)]
    return (loss, grad_x, *results)
```

```python
import dataclasses
import functools
import math

import jax
import jax.numpy as jnp
from jax import lax
from jax.experimental import pallas as pl
from jax.experimental.pallas import tpu as pltpu

F32 = jnp.float32
BF16 = jnp.bfloat16
_MXU_DTYPE = jnp.bfloat16

N_DEV = 8
D_MODEL = 1024
N_HEADS = 8
HEAD = 128
CHUNK = 64
N_META = 16
PAD_ROWS = 2 * CHUNK - N_META
LEAD = PAD_ROWS + N_META
ROPE = 64
QK_DIM = HEAD + ROPE
QK_PAD = 2 * HEAD
KV_RANK = 256
Q_RANK = 384
CONV_K = 4
EPS = 1e-6
NEG = -1e30
ROPE_THETA = 10000.0
ADAM_LR, ADAM_B1, ADAM_B2, ADAM_EPS, ADAM_WD, ADAM_STEP = 0.001, 0.9, 0.999, 1e-08, 0.01, 10
PACK_COLS = 512
VMEM_LIMIT = 56 * 1024 * 1024


def _pick(n, options):
    for o in options:
        if n % o == 0:
            return o
    raise ValueError(f"no tile for {n} among {options}")


def _cparams(sem):
    return pltpu.CompilerParams(dimension_semantics=sem, vmem_limit_bytes=VMEM_LIMIT)


def _dims(a, dims):
    if a.ndim == 2:
        return (dims, ((), ()))
    (ca,), (cb,) = dims
    return (((ca + 1,), (cb + 1,)), ((0,), (0,)))


def _dot(a, b, dims):
    return lax.dot_general(a.astype(_MXU_DTYPE), b.astype(_MXU_DTYPE), _dims(a, dims), preferred_element_type=F32)


@jax.custom_vjp
def mm_nn(a, b):
    return _dot(a, b, ((1,), (0,)))


@jax.custom_vjp
def mm_nt(a, b):
    return _dot(a, b, ((1,), (1,)))


@jax.custom_vjp
def mm_tn(a, b):
    return _dot(a, b, ((0,), (0,)))


mm_nn.defvjp(lambda a, b: (mm_nn(a, b), (a, b)), lambda r, g: (mm_nt(g, r[1]), mm_tn(r[0], g)))
mm_nt.defvjp(lambda a, b: (mm_nt(a, b), (a, b)), lambda r, g: (mm_nn(g, r[1]), mm_tn(g, r[0])))
mm_tn.defvjp(lambda a, b: (mm_tn(a, b), (a, b)), lambda r, g: (mm_nt(r[1], g), mm_nn(r[0], g)))


def _split_terms(x, n):
    terms, rest = [], x
    for _ in range(n):
        t = rest.astype(_MXU_DTYPE)
        terms.append(t)
        rest = rest - t.astype(F32)
    return terms


def _dot_01_raw(m, x, dims):
    m = m.astype(_MXU_DTYPE)
    return sum(lax.dot_general(m, t, _dims(m, dims), preferred_element_type=F32) for t in _split_terms(x, 3))


@jax.custom_vjp
def _dot_01(m, x):
    return _dot_01_raw(m, x, ((1,), (0,)))


_dot_01.defvjp(lambda m, x: (_dot_01(m, x), m), lambda m, g: (jnp.zeros_like(m), _dot_01_raw(m, g, ((0,), (0,)))))


def _inv_unit_lower(a):
    n = a.shape[-1]
    eye = (lax.broadcasted_iota(jnp.int32, (n, n), 0) == lax.broadcasted_iota(jnp.int32, (n, n), 1)).astype(F32)
    d = lambda u, w: lax.dot_general(u, w, _dims(u, ((1,), (0,))), preferred_element_type=F32)
    t = eye - a
    p = a.astype(_MXU_DTYPE)
    p = d(p, p)
    squarings = int(math.log2(n)) - 1
    for s in range(squarings):
        ph = p.astype(_MXU_DTYPE)
        t_hi, t_lo = _split_terms(t, 2)
        t = t + (d(t_hi, ph) + d(t_lo, ph))
        if s + 1 < squarings:
            p = d(ph, ph)
    return t


@jax.custom_vjp
def _inv_lookup(a, t):
    return t


def _inv_lookup_bwd(t, g):
    return -mm_tn(t, mm_nt(g, t)), jnp.zeros_like(t)


_inv_lookup.defvjp(lambda a, t: (t, t), _inv_lookup_bwd)


def _sigmoid(x):
    return 1.0 / (1.0 + jnp.exp(-x))


def _silu(x):
    return x * _sigmoid(x)


def _softplus(x):
    return jnp.where(x > 20.0, x, jnp.log(1.0 + jnp.exp(jnp.minimum(x, 20.0))))


def _rms(x, g, width=None):
    ms = jnp.sum(x * x, -1, keepdims=True) / (x.shape[-1] if width is None else width)
    return x * lax.rsqrt(ms + EPS) * g


MM_VMEM_BUDGET = 40 * 1024 * 1024


def _matmul_rows(name, a, b, mode, out_dtype, res):
    m, k = a.shape
    n = b.shape[1] if mode == "nn" else b.shape[0]
    dims = {"nn": ((1,), (0,)), "nt": ((1,), (1,))}[mode]
    out_bytes = jnp.dtype(out_dtype).itemsize

    def vmem(tm):
        blocks = 2 * tm * k * a.dtype.itemsize + 2 * k * n * b.dtype.itemsize + 2 * tm * n * out_bytes + tm * n * 4
        return blocks + (2 * tm * n * res.dtype.itemsize if res is not None else 0)

    tm = next(c for c in (2176, 1088, 512, 256, 128, 64) if m % c == 0 and vmem(c) <= MM_VMEM_BUDGET)

    def body(*refs):
        a_ref, b_ref = refs[:2]
        out = _dot(a_ref[...], b_ref[...], dims)
        if res is not None:
            out = out + refs[2][...].astype(F32)
        refs[-1][...] = out.astype(refs[-1].dtype)

    o_spec = pl.BlockSpec((tm, n), lambda i: (i, 0))
    in_specs = [pl.BlockSpec((tm, k), lambda i: (i, 0)), pl.BlockSpec(b.shape, lambda i: (0, 0))] + ([o_spec] if res is not None else [])
    args = (a, b) + ((res,) if res is not None else ())
    return pl.pallas_call(body, grid=(m // tm,), in_specs=in_specs, out_specs=o_spec, out_shape=jax.ShapeDtypeStruct((m, n), out_dtype),
                          compiler_params=_cparams(("parallel",)), name=name)(*args)


def matmul(name, a, b, mode, out_dtype=F32, res=None):
    if mode != "tn":
        return _matmul_rows(name, a, b, mode, out_dtype, res)
    (k, m), (k2, n) = a.shape, b.shape
    assert k == k2 and res is None, (name, a.shape, b.shape, mode)
    tm = _pick(m, (1024, 512, 384, 256, 128))
    tn = _pick(n, (1024, 512, 384, 256, 128))
    tk = _pick(k, (512, 256, 128))
    nk = k // tk
    dims = ((0,), (0,))

    def body(*refs):
        if res is None:
            a_ref, b_ref, o_ref, acc_ref = refs
        else:
            a_ref, b_ref, r_ref, o_ref, acc_ref = refs
        kk = pl.program_id(2)

        @pl.when(kk == 0)
        def _():
            acc_ref[...] = jnp.zeros_like(acc_ref)

        acc_ref[...] += _dot(a_ref[...], b_ref[...], dims)

        @pl.when(kk == nk - 1)
        def _():
            out = acc_ref[...]
            if res is not None:
                out = out + r_ref[...].astype(F32)
            o_ref[...] = out.astype(o_ref.dtype)

    a_spec = pl.BlockSpec((tk, tm), lambda i, j, kk: (kk, i)) if mode == "tn" else pl.BlockSpec((tm, tk), lambda i, j, kk: (i, kk))
    b_spec = pl.BlockSpec((tn, tk), lambda i, j, kk: (j, kk)) if mode == "nt" else pl.BlockSpec((tk, tn), lambda i, j, kk: (kk, j))
    o_spec = pl.BlockSpec((tm, tn), lambda i, j, kk: (i, j))
    in_specs = [a_spec, b_spec] + ([o_spec] if res is not None else [])
    args = (a, b) + ((res,) if res is not None else ())
    return pl.pallas_call(
        body, grid=(m // tm, n // tn, nk), in_specs=in_specs, out_specs=o_spec,
        out_shape=jax.ShapeDtypeStruct((m, n), out_dtype), scratch_shapes=[pltpu.VMEM((tm, tn), F32)],
        compiler_params=_cparams(("parallel", "parallel", "arbitrary")), name=name)(*args)


@dataclasses.dataclass
class Arg:
    arr: jax.Array
    kind: str = "row"
    bc: int = 0
    base: int = 0
    ph: bool = False
    diff: bool = False
    gdt: object = F32


def _arg_spec(a, tr, nh, ntab, base=None):
    bc = a.bc or a.arr.shape[1]
    base = a.base if base is None else base
    width = bc * nh if a.ph else bc
    col = base // nh if a.ph else base
    assert not a.ph or base % nh == 0
    if a.kind == "row":
        return pl.BlockSpec((tr, width), lambda i: (i, col))
    if a.kind == "tab":
        return pl.BlockSpec((tr, width), lambda i: (i % ntab, col))
    return pl.BlockSpec((a.arr.shape[0], width), lambda i: (0, col))


def _head_view(ref, a, h):
    bc = a.bc or a.arr.shape[1]
    v = ref[:, h * bc:(h + 1) * bc] if a.ph else ref[...]
    return v.astype(F32) if jnp.issubdtype(v.dtype, jnp.floating) else v


def row_call(name, fn, args, outs, tr, nh=1, ntab=1):
    t = args[0].arr.shape[0]
    n_in = len(args)
    out_args = [Arg(None, "row", bc, 0, ph) for (_, _, bc, ph) in outs]

    def body(*refs):
        for h in range(nh):
            res = fn(*[_head_view(r, a, h) for r, a in zip(refs[:n_in], args, strict=True)])
            for r, a, v in zip(refs[n_in:], out_args, res, strict=True):
                if a.ph:
                    r[:, h * a.bc:(h + 1) * a.bc] = v.astype(r.dtype)
                elif h == nh - 1:
                    r[...] = v.astype(r.dtype)

    return pl.pallas_call(
        body, grid=(t // tr,), in_specs=[_arg_spec(a, tr, nh, ntab) for a in args], out_specs=[_arg_spec(a, tr, nh, ntab) for a in out_args],
        out_shape=[jax.ShapeDtypeStruct((t, cols), dt) for (cols, dt, _, _) in outs],
        compiler_params=_cparams(("arbitrary",)), name=name)(*[a.arr for a in args])


def row_vjp_call(name, fn, args, cts, tr, nh=1, ntab=1):
    t = args[0].arr.shape[0]
    n_in, n_ct = len(args), len(cts)
    diff_idx = [k for k, a in enumerate(args) if a.diff]

    def body(*refs):
        out_refs = refs[n_in + n_ct:]
        shared = [None] * len(diff_idx)
        for k, r in zip(diff_idx, out_refs, strict=True):
            if args[k].kind == "par":
                @pl.when(pl.program_id(0) == 0)
                def _(r=r):
                    r[...] = jnp.zeros_like(r)

        for h in range(nh):
            vals = [_head_view(r, a, h) for r, a in zip(refs[:n_in], args, strict=True)]
            ct_vals = tuple(_head_view(r, a, h) for r, a in zip(refs[n_in:n_in + n_ct], cts, strict=True))

            def f(*dv, vals=vals):
                full = list(vals)
                for k, v in zip(diff_idx, dv, strict=True):
                    full[k] = v
                return tuple(fn(*full))

            _, vjp = jax.vjp(f, *[vals[k] for k in diff_idx])
            for j, (k, r, g) in enumerate(zip(diff_idx, out_refs, vjp(ct_vals), strict=True)):
                a = args[k]
                bc = a.bc or a.arr.shape[1]
                if not a.ph:
                    shared[j] = g if shared[j] is None else shared[j] + g
                elif a.kind == "row":
                    r[:, h * bc:(h + 1) * bc] = g.astype(r.dtype)
                else:
                    r[:, h * bc:(h + 1) * bc] += g
        for j, (k, r) in enumerate(zip(diff_idx, out_refs, strict=True)):
            if not args[k].ph:
                if args[k].kind == "row":
                    r[...] = shared[j].astype(r.dtype)
                else:
                    r[...] += shared[j]

    out_specs, out_shape = [], []
    for k in diff_idx:
        a = args[k]
        bc = a.bc or a.arr.shape[1]
        out_specs.append(_arg_spec(a, tr, nh, ntab, base=0))
        out_shape.append(jax.ShapeDtypeStruct((t if a.kind == "row" else a.arr.shape[0], bc * (nh if a.ph else 1)), a.gdt if a.kind == "row" else F32))
    in_specs = [_arg_spec(a, tr, nh, ntab) for a in list(args) + list(cts)]
    return pl.pallas_call(
        body, grid=(t // tr,), in_specs=in_specs, out_specs=out_specs, out_shape=out_shape,
        compiler_params=_cparams(("arbitrary",)), name=name)(*[a.arr for a in list(args) + list(cts)])


def _conv_taps(x, w):
    rows = lax.broadcasted_iota(jnp.int32, x.shape, 0)
    y = x * w[CONV_K - 1:CONV_K, :]
    shifted = []
    for s in range(1, CONV_K):
        xs = jnp.where(rows >= s, pltpu.roll(x, s, 0), 0.0)
        shifted.append(xs)
        y = y + xs * w[CONV_K - 1 - s:CONV_K - s, :]
    return y, shifted


CONV_HEADS = 4
CONV_BLOCKS_PER_THIRD = N_HEADS // CONV_HEADS


def _conv_post(y, block):
    a = _silu(y)
    normed = block < 2 * CONV_BLOCKS_PER_THIRD
    scale = jnp.where(block < CONV_BLOCKS_PER_THIRD, HEAD ** -0.5, 1.0)
    return a * jnp.where(normed, lax.rsqrt(jnp.sum(a * a, -1, keepdims=True) + EPS) * scale, 1.0)


def conv_fwd(z, w, lp):
    t, width = z.shape
    cols = CONV_HEADS * HEAD

    def body(z_ref, w_ref, o_ref):
        block = pl.program_id(1)
        for h in range(CONV_HEADS):
            cs = slice(h * HEAD, (h + 1) * HEAD)
            y, _ = _conv_taps(z_ref[:, cs], w_ref[:, cs])
            o_ref[:, cs] = _conv_post(y, block)

    return pl.pallas_call(
        body, grid=(t // lp, width // cols),
        in_specs=[pl.BlockSpec((lp, cols), lambda b, j: (b, j)), pl.BlockSpec((CONV_K, cols), lambda b, j: (0, j))],
        out_specs=pl.BlockSpec((lp, cols), lambda b, j: (b, j)), out_shape=jax.ShapeDtypeStruct((t, width), F32),
        compiler_params=_cparams(("arbitrary", "arbitrary")), name="a_conv_fwd")(z, w)


def conv_bwd(z, w, dout, lp):
    t, width = z.shape
    cols = CONV_HEADS * HEAD

    def body(z_ref, w_ref, g_ref, dz_ref, dw_ref):
        block = pl.program_id(0)

        @pl.when(pl.program_id(1) == 0)
        def _():
            dw_ref[...] = jnp.zeros_like(dw_ref)

        for h in range(CONV_HEADS):
            cs = slice(h * HEAD, (h + 1) * HEAD)
            x, wv = z_ref[:, cs], w_ref[:, cs]
            y, shifted = _conv_taps(x, wv)
            _, vjp = jax.vjp(lambda y_: _conv_post(y_, block), y)
            (dy,) = vjp(g_ref[:, cs])
            rows = lax.broadcasted_iota(jnp.int32, x.shape, 0)
            dx = dy * wv[CONV_K - 1:CONV_K, :]
            for s in range(1, CONV_K):
                dx = dx + jnp.where(rows < lp - s, pltpu.roll(dy, lp - s, 0), 0.0) * wv[CONV_K - 1 - s:CONV_K - s, :]
            dz_ref[:, cs] = dx.astype(dz_ref.dtype)
            dw_ref[CONV_K - 1:CONV_K, cs] += jnp.sum(dy * x, axis=0, keepdims=True)
            for s in range(1, CONV_K):
                dw_ref[CONV_K - 1 - s:CONV_K - s, cs] += jnp.sum(dy * shifted[s - 1], axis=0, keepdims=True)

    blk = pl.BlockSpec((lp, cols), lambda j, b: (b, j))
    w_blk = pl.BlockSpec((CONV_K, cols), lambda j, b: (0, j))
    return pl.pallas_call(
        body, grid=(width // cols, t // lp), in_specs=[blk, w_blk, blk], out_specs=[blk, w_blk],
        out_shape=[jax.ShapeDtypeStruct((t, width), _MXU_DTYPE), jax.ShapeDtypeStruct((CONV_K, width), F32)],
        compiler_params=_cparams(("arbitrary", "arbitrary")), name="a_conv_bwd")(z, w, dout)


def _delta_chunk(q, k, v, ba, alog, dtb, state, t_stored, h0):
    n_g, c = q.shape[0], q.shape[1]
    lane = lax.broadcasted_iota(jnp.int32, (1, HEAD), 1)

    def pick(x, offset):
        return jnp.stack([jnp.sum(x * (lane == offset + h0 + g).astype(F32), axis=1, keepdims=True) for g in range(n_g)])

    b_raw, a_raw = pick(ba, 0), pick(ba, N_HEADS)
    a_log, dt_bias = pick(alog, 0), pick(dtb, 0)
    beta = _sigmoid(b_raw)
    g = -jnp.exp(a_log) * _softplus(a_raw + dt_bias)
    ri = lax.broadcasted_iota(jnp.int32, (c, c), 0)
    ci = lax.broadcasted_iota(jnp.int32, (c, c), 1)
    tril = ci <= ri
    lower = jnp.broadcast_to(tril.astype(F32), (n_g, c, c))
    gc_col = _dot_01(lower, g * jnp.ones((1, 1, HEAD), F32))[:, :, :1]
    gc_row = _dot_01(jnp.ones((n_g, 8, c), F32), g * (ri <= ci).astype(F32)[None])[:, 0:1, :]
    gc_last = jnp.sum(g, axis=1, keepdims=True)
    decay = jnp.exp(jnp.where(tril, gc_col - gc_row, NEG))
    e_gc = jnp.exp(gc_col)
    kb = k * beta
    a_mat = jnp.where(ci < ri, mm_nt(kb, k) * decay, 0.0)
    t_inv = _inv_unit_lower(a_mat) if t_stored is None else _inv_lookup(a_mat, t_stored)
    u_base = mm_nn(t_inv, v * beta)
    w_dec = mm_nn(t_inv, kb * e_gc)
    attn = jnp.where(tril, mm_nt(q, k) * decay, 0.0)
    u = u_base - mm_nn(w_dec, state)
    o = mm_nn(q * e_gc, state) + mm_nn(attn, u)
    new_state = state * jnp.exp(gc_last) + mm_tn(k * jnp.exp(gc_last - gc_col), u)
    return o, new_state, t_inv


DELTA_CHUNKS_FWD = 2
DELTA_CHUNKS_BWD = 1


def _qkv_heads(ref, rs, part):
    return jnp.stack([ref[rs, (part * N_HEADS + g) * HEAD:(part * N_HEADS + g + 1) * HEAD] for g in range(N_HEADS)])


def delta_fwd(qkv, ba, alog, dtb, lp):
    t = qkv.shape[0]
    nb, nc = t // lp, lp // CHUNK
    cps = DELTA_CHUNKS_FWD
    ng, rows = nc // cps, cps * CHUNK
    assert nc % cps == 0

    def body(qkv_ref, ba_ref, al_ref, dt_ref, o_ref, s_ref, t_ref, state_ref):
        @pl.when(pl.program_id(1) == 0)
        def _():
            state_ref[...] = jnp.zeros_like(state_ref)

        al, dtv = al_ref[...], dt_ref[...]
        for c in range(cps):
            rs = slice(c * CHUNK, (c + 1) * CHUNK)
            state = state_ref[...]
            o, new_state, t_inv = _delta_chunk(_qkv_heads(qkv_ref, rs, 0), _qkv_heads(qkv_ref, rs, 1), _qkv_heads(qkv_ref, rs, 2),
                                               ba_ref[rs, :], al, dtv, state, None, 0)
            for g in range(N_HEADS):
                o_ref[rs, g * HEAD:(g + 1) * HEAD] = o[g]
                s_ref[g, c] = state[g]
                t_ref[g, c] = t_inv[g]
            state_ref[...] = new_state

    rows_of = lambda width: pl.BlockSpec((rows, width), lambda b, n: (b * ng + n, 0))
    par_spec = pl.BlockSpec((1, HEAD), lambda b, n: (0, 0))
    return pl.pallas_call(
        body, grid=(nb, ng), in_specs=[rows_of(3 * N_HEADS * HEAD), rows_of(HEAD), par_spec, par_spec],
        out_specs=[rows_of(N_HEADS * HEAD), pl.BlockSpec((None, N_HEADS, cps, HEAD, HEAD), lambda b, n: (b, 0, n, 0, 0)),
                   pl.BlockSpec((None, N_HEADS, cps, CHUNK, CHUNK), lambda b, n: (b, 0, n, 0, 0))],
        out_shape=[jax.ShapeDtypeStruct((t, N_HEADS * HEAD), F32), jax.ShapeDtypeStruct((nb, N_HEADS, nc, HEAD, HEAD), F32),
                   jax.ShapeDtypeStruct((nb, N_HEADS, nc, CHUNK, CHUNK), F32)],
        scratch_shapes=[pltpu.VMEM((N_HEADS, HEAD, HEAD), F32)],
        compiler_params=_cparams(("arbitrary", "arbitrary")), name="delta_fwd")(qkv, ba, alog, dtb)


def delta_bwd(qkv, ba, alog, dtb, states, t_invs, do, lp):
    t = qkv.shape[0]
    nb, nc = t // lp, lp // CHUNK
    cps = DELTA_CHUNKS_BWD
    ng, rows = nc // cps, cps * CHUNK

    def body(qkv_ref, ba_ref, al_ref, dt_ref, s_ref, t_ref, do_ref, dqkv_ref, dba_ref, dal_ref, ddt_ref, dstate_ref):
        b, step = pl.program_id(0), pl.program_id(1)

        @pl.when(step == 0)
        def _():
            dstate_ref[...] = jnp.zeros_like(dstate_ref)

        @pl.when((b == 0) & (step == 0))
        def _():
            dal_ref[...] = jnp.zeros_like(dal_ref)
            ddt_ref[...] = jnp.zeros_like(ddt_ref)

        al, dtv = al_ref[...], dt_ref[...]
        d_al = jnp.zeros((1, HEAD), F32)
        d_dt = jnp.zeros((1, HEAD), F32)
        for c in reversed(range(cps)):
            rs = slice(c * CHUNK, (c + 1) * CHUNK)
            t_n = jnp.stack([t_ref[g, c] for g in range(N_HEADS)])
            s_n = jnp.stack([s_ref[g, c] for g in range(N_HEADS)])
            d_o = jnp.stack([do_ref[rs, g * HEAD:(g + 1) * HEAD] for g in range(N_HEADS)])

            def f(q_, k_, v_, ba_, al_, dt_, s_, t_n=t_n):
                return _delta_chunk(q_, k_, v_, ba_, al_, dt_, s_, t_n, 0)[:2]

            _, vjp = jax.vjp(f, _qkv_heads(qkv_ref, rs, 0), _qkv_heads(qkv_ref, rs, 1), _qkv_heads(qkv_ref, rs, 2), ba_ref[rs, :], al, dtv, s_n)
            grads = vjp((d_o, dstate_ref[...]))
            for part in range(3):
                for g in range(N_HEADS):
                    dqkv_ref[rs, (part * N_HEADS + g) * HEAD:(part * N_HEADS + g + 1) * HEAD] = grads[part][g]
            dba_ref[rs, :] = grads[3]
            d_al, d_dt = d_al + grads[4], d_dt + grads[5]
            dstate_ref[...] = grads[6]
        dal_ref[...] += d_al
        ddt_ref[...] += d_dt

    rows_of = lambda width: pl.BlockSpec((rows, width), lambda b, n: (b * ng + ng - 1 - n, 0))
    par_spec = pl.BlockSpec((1, HEAD), lambda b, n: (0, 0))
    return pl.pallas_call(
        body, grid=(nb, ng),
        in_specs=[rows_of(3 * N_HEADS * HEAD), rows_of(HEAD), par_spec, par_spec,
                  pl.BlockSpec((None, N_HEADS, cps, HEAD, HEAD), lambda b, n: (b, 0, ng - 1 - n, 0, 0)),
                  pl.BlockSpec((None, N_HEADS, cps, CHUNK, CHUNK), lambda b, n: (b, 0, ng - 1 - n, 0, 0)), rows_of(N_HEADS * HEAD)],
        out_specs=[rows_of(3 * N_HEADS * HEAD), rows_of(HEAD), par_spec, par_spec],
        out_shape=[jax.ShapeDtypeStruct((t, 3 * N_HEADS * HEAD), F32), jax.ShapeDtypeStruct((t, HEAD), F32),
                   jax.ShapeDtypeStruct((1, HEAD), F32), jax.ShapeDtypeStruct((1, HEAD), F32)],
        scratch_shapes=[pltpu.VMEM((N_HEADS, HEAD, HEAD), F32)],
        compiler_params=_cparams(("arbitrary", "arbitrary")), name="delta_bwd")(qkv, ba, alog, dtb, states, t_invs, do)


ATT_Q_TILE = 256
ATT_K_TILE = 512
ATT_SCALE = QK_DIM ** -0.5


def _tiles(end, size):
    return [(s, min(s + size, end)) for s in range(0, end, size)]


def _att_visible(q0, q1, k0, k1, keys_first):
    if k1 <= q0 + CHUNK and k0 >= PAD_ROWS:
        return None
    shape = (k1 - k0, q1 - q0) if keys_first else (q1 - q0, k1 - k0)
    qpos = q0 + lax.broadcasted_iota(jnp.int32, shape, 1 if keys_first else 0)
    kpos = k0 + lax.broadcasted_iota(jnp.int32, shape, 0 if keys_first else 1)
    shift = CHUNK.bit_length() - 1
    return (jnp.right_shift(kpos, shift) <= jnp.right_shift(qpos, shift)) & (kpos >= PAD_ROWS)


def _att_seq_specs(lp):
    return pl.BlockSpec((lp, QK_PAD), lambda b, h: (b, h)), pl.BlockSpec((lp, HEAD), lambda b, h: (b, h))


def flash_fwd(q, k, v, lp):
    t = q.shape[0]
    qk_seq, o_seq = _att_seq_specs(lp)

    def body(q_ref, k_ref, v_ref, o_ref, lse_ref):
        for q0, q1 in _tiles(lp, ATT_Q_TILE):
            qb = q_ref[q0:q1, :]
            k_tiles = _tiles(q1, ATT_K_TILE)
            scores, m = [], None
            for k0, k1 in k_tiles:
                s = mm_nt(qb, k_ref[k0:k1, :]) * ATT_SCALE
                vis = _att_visible(q0, q1, k0, k1, False)
                s = s if vis is None else jnp.where(vis, s, NEG)
                scores.append(s)
                row_max = jnp.max(s, -1, keepdims=True)
                m = row_max if m is None else jnp.maximum(m, row_max)
            l = jnp.zeros((q1 - q0, 1), F32)
            acc = jnp.zeros((q1 - q0, HEAD), F32)
            for s, (k0, k1) in zip(scores, k_tiles, strict=True):
                p = jnp.exp(s - m)
                l = l + jnp.sum(p, -1, keepdims=True)
                acc = acc + mm_nn(p, v_ref[k0:k1, :])
            o_ref[q0:q1, :] = acc / l
            lse_ref[q0:q1, :] = jnp.broadcast_to(m + jnp.log(l), (q1 - q0, HEAD))

    big = jax.ShapeDtypeStruct((t, N_HEADS * HEAD), F32)
    return pl.pallas_call(
        body, grid=(t // lp, N_HEADS), in_specs=[qk_seq, qk_seq, o_seq], out_specs=[o_seq, o_seq], out_shape=[big, big],
        compiler_params=_cparams(("arbitrary", "arbitrary")), name="flash_fwd")(q, k, v)


def flash_bwd(q, k, v, o, lse, do, lp):
    t = q.shape[0]
    qk_seq, o_seq = _att_seq_specs(lp)

    def body(q_ref, k_ref, v_ref, o_ref, lse_ref, do_ref, dq_ref, dk_ref, dv_ref):
        dk_ref[...] = jnp.zeros_like(dk_ref)
        dv_ref[...] = jnp.zeros_like(dv_ref)
        for q0, q1 in _tiles(lp, ATT_Q_TILE):
            qb, dob = q_ref[q0:q1, :], do_ref[q0:q1, :]
            lse_row = jnp.transpose(lse_ref[q0:q1, :])[0:1, :]
            dsum_row = jnp.sum(jnp.transpose(dob * o_ref[q0:q1, :]), axis=0, keepdims=True)
            dq = jnp.zeros((q1 - q0, QK_PAD), F32)
            for k0, k1 in _tiles(q1, ATT_K_TILE):
                kb, vb = k_ref[k0:k1, :], v_ref[k0:k1, :]
                s = mm_nt(kb, qb) * ATT_SCALE
                vis = _att_visible(q0, q1, k0, k1, True)
                s = s if vis is None else jnp.where(vis, s, NEG)
                p = jnp.exp(s - lse_row)
                ds = p * (mm_nt(vb, dob) - dsum_row) * ATT_SCALE
                dv_ref[k0:k1, :] += mm_nn(p, dob)
                dk_ref[k0:k1, :] += mm_nn(ds, qb)
                dq = dq + mm_tn(ds, kb)
            dq_ref[q0:q1, :] = dq

    return pl.pallas_call(
        body, grid=(t // lp, N_HEADS), in_specs=[qk_seq, qk_seq, o_seq, o_seq, o_seq, o_seq], out_specs=[qk_seq, qk_seq, o_seq],
        out_shape=[jax.ShapeDtypeStruct((t, N_HEADS * QK_PAD), F32), jax.ShapeDtypeStruct((t, N_HEADS * QK_PAD), F32),
                   jax.ShapeDtypeStruct((t, N_HEADS * HEAD), F32)],
        compiler_params=_cparams(("arbitrary", "arbitrary")), name="flash_bwd")(q, k, v, o, lse, do)


def loss_head(h2, target, lp):
    nb, seq, d = target.shape
    tr = 128
    nblk = lp // tr
    lead_blocks = LEAD // tr

    def body(h_ref, t_ref, loss_ref, dh_ref, acc_ref):
        b, i = pl.program_id(0), pl.program_id(1)

        @pl.when((b == 0) & (i == 0))
        def _():
            acc_ref[...] = jnp.zeros_like(acc_ref)

        @pl.when(i < lead_blocks)
        def _():
            dh_ref[...] = jnp.zeros_like(dh_ref)

        @pl.when(i >= lead_blocks)
        def _():
            err = h_ref[...] - t_ref[...]
            dh_ref[...] = err * (1.0 / d)
            acc_ref[...] += jnp.sum(err * err, axis=0, keepdims=True)

        @pl.when((b == nb - 1) & (i == nblk - 1))
        def _():
            loss_ref[...] = jnp.sum(acc_ref[...], axis=1, keepdims=True) * (0.5 / d)

    return pl.pallas_call(
        body, grid=(nb, nblk),
        in_specs=[pl.BlockSpec((None, tr, d), lambda b, i: (b, i, 0)),
                  pl.BlockSpec((None, tr, d), lambda b, i: (b, jnp.maximum(i - lead_blocks, 0), 0))],
        out_specs=[pl.BlockSpec((1, 1), lambda b, i: (0, 0)), pl.BlockSpec((None, tr, d), lambda b, i: (b, i, 0))],
        out_shape=[jax.ShapeDtypeStruct((1, 1), F32), jax.ShapeDtypeStruct((nb, lp, d), F32)],
        scratch_shapes=[pltpu.VMEM((1, d), F32)], compiler_params=_cparams(("arbitrary", "arbitrary")), name="loss_head")(h2, target)


def meta_grad(dh0):
    nb, _, d = dh0.shape

    def body(g_ref, o_ref):
        @pl.when(pl.program_id(0) == 0)
        def _():
            o_ref[...] = jnp.zeros_like(o_ref)

        o_ref[...] += g_ref[PAD_ROWS:LEAD, :]

    return pl.pallas_call(
        body, grid=(nb,), in_specs=[pl.BlockSpec((None, LEAD, d), lambda b: (b, 0, 0))],
        out_specs=pl.BlockSpec((N_META, d), lambda b: (0, 0)), out_shape=jax.ShapeDtypeStruct((N_META, d), F32),
        compiler_params=_cparams(("arbitrary",)), name="meta_grad")(dh0)


_HBM = pl.BlockSpec(memory_space=pltpu.HBM)


def _mesh_pos():
    x, y, c = lax.axis_index("x"), lax.axis_index("y"), lax.axis_index("c")
    return x, y, c


def _peer(x, y, c, k):
    px = 1 - x if k & 4 else x
    py = 1 - y if k & 2 else y
    pc = 1 - c if k & 1 else c
    return (px, py, pc), 4 * px + 2 * py + pc


def _exchange(name, bufs, scatter):
    n = len(bufs)

    def body(*refs):
        x_refs, out_refs = refs[:n], refs[n:2 * n]
        send_sems, recv_sems, local_sems = refs[2 * n:]
        x, y, c = _mesh_pos()
        me = 4 * x + 2 * y + c
        local, sends = [], []
        for i in range(n):
            cp = pltpu.make_async_copy(x_refs[i].at[me] if scatter else x_refs[i], out_refs[i].at[me], local_sems.at[i])
            cp.start()
            local.append(cp)

        def copy(i, k, landing):
            peer, peer_id = _peer(x, y, c, k)
            src = x_refs[i].at[peer_id] if scatter else x_refs[i]
            return pltpu.make_async_remote_copy(src_ref=src, dst_ref=out_refs[i].at[peer_id if landing else me], send_sem=send_sems.at[i, k - 1],
                                                recv_sem=recv_sems.at[i, k - 1], device_id=peer, device_id_type=pl.DeviceIdType.MESH)

        for k in range(1, N_DEV):
            for i in range(n):
                cp = copy(i, k, False)
                cp.start()
                sends.append(cp)
        for k in range(1, N_DEV):
            for i in range(n):
                copy(i, k, True).wait_recv()
        for cp in sends:
            cp.wait_send()
        for cp in local:
            cp.wait()

    out_shape = [jax.ShapeDtypeStruct(b.shape if scatter else (N_DEV,) + b.shape, b.dtype) for b in bufs]
    return pl.pallas_call(
        body, in_specs=[_HBM] * n, out_specs=[_HBM] * n, out_shape=out_shape,
        scratch_shapes=[pltpu.SemaphoreType.DMA((n, N_DEV - 1)), pltpu.SemaphoreType.DMA((n, N_DEV - 1)), pltpu.SemaphoreType.DMA((n,))],
        name=name)(*bufs)


def _f_rms(x, g):
    return (_rms(x, g),)


def _f_rms2(x, g1, g2):
    r = x * lax.rsqrt(jnp.sum(x * x, -1, keepdims=True) / x.shape[-1] + EPS)
    return r * g1, r * g2


def _f_out_gate(o, gate, gain):
    return (_rms(o, gain) * _silu(gate),)


def _f_gate(o, gate):
    return (o * _silu(gate),)


@jax.custom_vjp
def _swap_rope_halves(x):
    half = ROPE // 2
    lane = lax.broadcasted_iota(jnp.int32, x.shape, 1)
    return jnp.where(lane < half, pltpu.roll(x, HEAD - half, 1), jnp.where(lane < ROPE, pltpu.roll(x, half, 1), 0.0))


_swap_rope_halves.defvjp(lambda x: (_swap_rope_halves(x), None), lambda _, g: (_swap_rope_halves(g),))


def _f_qk_final(nope, rope_in, g_nope, g_rope, cos, sin):
    ms = (jnp.sum(nope * nope, -1, keepdims=True) + jnp.sum(rope_in * rope_in, -1, keepdims=True)) / QK_DIM
    r = lax.rsqrt(ms + EPS)
    a = nope * r * g_nope
    b = rope_in * r * g_rope
    return (jnp.concatenate([a, b * cos + _swap_rope_halves(b) * sin], axis=1),)


def _rope_tables(lp):
    half = ROPE // 2
    pos = jnp.maximum(jnp.arange(lp) - PAD_ROWS, 0)
    inv = ROPE_THETA ** (-jnp.arange(half, dtype=F32) / half)
    ang = pos.astype(F32)[:, None] * inv[None, :]
    zeros = jnp.zeros((lp, HEAD - ROPE), F32)
    cos = jnp.concatenate([jnp.cos(ang), jnp.cos(ang), zeros], 1)
    sin = jnp.concatenate([-jnp.sin(ang), jnp.sin(ang), zeros], 1)
    return cos, sin


def _pad_lanes(w, width=HEAD):
    return jnp.pad(w, ((0, 0), (0, width - w.shape[1])))


def _pad_rows(w, rows=HEAD):
    return jnp.pad(w, ((0, rows - w.shape[0]), (0, 0)))


def _split_heads_qk_t(w_t):
    k = w_t.shape[1]
    w3 = w_t.reshape(N_HEADS, QK_DIM, k)
    nope = w3[:, :HEAD].reshape(N_HEADS * HEAD, k)
    rope = jnp.pad(w3[:, HEAD:], ((0, 0), (0, HEAD - ROPE), (0, 0))).reshape(N_HEADS * HEAD, k)
    return jnp.concatenate([nope, rope], 0)


def _merge_heads_qk_t(g_t):
    k = g_t.shape[1]
    kw = N_HEADS * HEAD
    nope, rope = g_t[:kw].reshape(N_HEADS, HEAD, k), g_t[kw:].reshape(N_HEADS, HEAD, k)[:, :ROPE]
    return jnp.concatenate([nope, rope], 1).reshape(N_HEADS * QK_DIM, k)


def local_step(x, target, w):
    nb, seq, d = x.shape
    lp = seq + LEAD
    t = nb * lp
    tr = _pick(lp, (544, 128))
    ntab = lp // tr
    mxu = _MXU_DTYPE
    kw = N_HEADS * HEAD

    a_w_in_t = w["a_w_in"].astype(mxu)
    w_qkv_t, w_ga_t, w_ba_t = a_w_in_t[:3 * kw], a_w_in_t[3 * kw:4 * kw], _pad_rows(a_w_in_t[4 * kw:])
    a_conv = w["a_conv"].T
    a_w_out = w["a_w_out"].astype(mxu)
    alog, dtb, o_gain = _pad_lanes(w["a_log"]), _pad_lanes(w["a_dt_bias"]), w["a_o_gain"]
    w_dkv, w_dpe = w["kv_w_down"][:, :KV_RANK].astype(mxu), _pad_lanes(w["kv_w_down"][:, KV_RANK:]).astype(mxu)
    w_ukv_t = jnp.concatenate([w["kv_w_uk"], w["kv_w_uv"]], 0).astype(mxu)
    b_w_in_t = w["b_w_in"].astype(mxu)
    w_cq_t, w_gb_t = b_w_in_t[:Q_RANK], b_w_in_t[Q_RANK:]
    w_q_t = _split_heads_qk_t(w["b_w_uq"]).astype(mxu)
    b_w_out = w["b_w_out"].astype(mxu)
    a_norm, kv_norm, b_norm = w["a_norm"], w["kv_norm"][None, :], w["b_norm"]
    lat_norm, qlat_norm = w["kv_latent_norm"][None, :], w["b_q_latent_norm"]
    kg_nope, kg_rope = w["k_gain"][None, :HEAD], _pad_lanes(w["k_gain"][None, HEAD:])
    qg_nope, qg_rope = w["b_q_gain"][:, :HEAD], _pad_lanes(w["b_q_gain"][:, HEAD:])
    cos, sin = _rope_tables(lp)

    meta = jnp.broadcast_to(w["meta_tokens"].T[None], (nb, N_META, d))
    h0 = jnp.concatenate([jnp.zeros((nb, PAD_ROWS, d), F32), meta, x], 1).reshape(t, d)
    (hn,) = row_call("a_norm_fwd", _f_rms, [Arg(h0), Arg(a_norm, "par")], [(d, mxu, d, False)], tr)
    z_qkv = matmul("a_in_qkv", hn, w_qkv_t, "nt")
    gate_a = matmul("a_in_gate", hn, w_ga_t, "nt")
    z_ba = matmul("a_in_ba", hn, w_ba_t, "nt")
    qkv_a = conv_fwd(z_qkv, a_conv, lp)
    o_a, states, t_invs = delta_fwd(qkv_a, z_ba, alog, dtb, lp)
    og_args = [Arg(o_a, bc=HEAD, ph=True, diff=True), Arg(gate_a, bc=HEAD, ph=True, diff=True, gdt=mxu), Arg(o_gain, "par", diff=True)]
    (og_a,) = row_call("a_out_gate_fwd", _f_out_gate, og_args, [(kw, mxu, HEAD, True)], tr, nh=N_HEADS)
    h1 = matmul("a_out", og_a, a_w_out, "nn", res=h0)

    hk, hb = row_call("b_norms_fwd", _f_rms2, [Arg(h1), Arg(kv_norm, "par"), Arg(b_norm, "par")], [(d, mxu, d, False), (d, mxu, d, False)], tr)
    c_kv_raw = matmul("kv_down", hk, w_dkv, "nn")
    k_pe = matmul("kv_down_pe", hk, w_dpe, "nn")
    c_q_raw = matmul("b_in_q", hb, w_cq_t, "nt")
    gate_b = matmul("b_in_gate", hb, w_gb_t, "nt")
    (c_kv,) = row_call("kv_latent_fwd", _f_rms, [Arg(c_kv_raw), Arg(lat_norm, "par")], [(KV_RANK, mxu, KV_RANK, False)], tr)
    (c_q,) = row_call("q_latent_fwd", _f_rms, [Arg(c_q_raw), Arg(qlat_norm, "par")], [(Q_RANK, mxu, Q_RANK, False)], tr)
    k_nope = matmul("k_up", c_kv, w_ukv_t[:kw], "nt")
    v_b = matmul("v_up", c_kv, w_ukv_t[kw:], "nt", out_dtype=mxu)
    q_up = matmul("q_up", c_q, w_q_t, "nt")
    tabs = [Arg(cos, "tab"), Arg(sin, "tab")]
    k_args = [Arg(k_nope, bc=HEAD, ph=True, diff=True, gdt=mxu), Arg(k_pe, diff=True), Arg(kg_nope, "par", diff=True), Arg(kg_rope, "par", diff=True)] + tabs
    q_args = [Arg(q_up, bc=HEAD, ph=True, diff=True, gdt=mxu), Arg(q_up, bc=HEAD, base=N_HEADS, ph=True, diff=True, gdt=mxu),
              Arg(qg_nope, "par", diff=True), Arg(qg_rope, "par", diff=True)] + tabs
    (k_fin,) = row_call("k_final_fwd", _f_qk_final, k_args, [(N_HEADS * QK_PAD, mxu, QK_PAD, True)], tr, nh=N_HEADS, ntab=ntab)
    (q_fin,) = row_call("q_final_fwd", _f_qk_final, q_args, [(N_HEADS * QK_PAD, mxu, QK_PAD, True)], tr, nh=N_HEADS, ntab=ntab)
    o_b, lse = flash_fwd(q_fin, k_fin, v_b, lp)
    gb_args = [Arg(o_b, diff=True), Arg(gate_b, diff=True, gdt=mxu)]
    (og_b,) = row_call("b_gate_fwd", _f_gate, gb_args, [(kw, mxu, kw, False)], tr)
    h2 = matmul("b_out", og_b, b_w_out, "nn", res=h1)

    loss, dh2 = loss_head(h2.reshape(nb, lp, d), target, lp)
    dh2 = dh2.reshape(t, d)
    grads = {}

    d_og_b = matmul("b_out_dx", dh2, b_w_out, "nt")
    grads["b_w_out"] = matmul("b_out_dw", og_b, dh2, "tn")
    d_o_b, d_gate_b = row_vjp_call("b_gate_bwd", _f_gate, gb_args, [Arg(d_og_b)], tr)
    dq_fin, dk_fin, dv_b = flash_bwd(q_fin, k_fin, v_b, o_b, lse, d_o_b, lp)
    dq_nope, dq_rope, d_qg_nope, d_qg_rope = row_vjp_call(
        "q_final_bwd", _f_qk_final, q_args, [Arg(dq_fin, bc=QK_PAD, ph=True)], tr, nh=N_HEADS, ntab=ntab)
    dk_nope, dk_pe, d_kg_nope, d_kg_rope = row_vjp_call(
        "k_final_bwd", _f_qk_final, k_args, [Arg(dk_fin, bc=QK_PAD, ph=True)], tr, nh=N_HEADS, ntab=ntab)
    grads["b_q_gain"] = jnp.concatenate([d_qg_nope, d_qg_rope[:, :ROPE]], 1)
    grads["k_gain"] = jnp.concatenate([d_kg_nope, d_kg_rope[:, :ROPE]], 1)[0]
    d_c_q = matmul("q_nope_dx", dq_nope, w_q_t[:kw], "nn")
    d_c_q = matmul("q_rope_dx", dq_rope, w_q_t[kw:], "nn", res=d_c_q)
    grads["b_w_uq"] = _merge_heads_qk_t(jnp.concatenate([matmul("q_nope_dw", dq_nope, c_q, "tn"), matmul("q_rope_dw", dq_rope, c_q, "tn")], 0))
    d_c_kv = matmul("k_up_dx", dk_nope, w_ukv_t[:kw], "nn")
    d_c_kv = matmul("v_up_dx", dv_b, w_ukv_t[kw:], "nn", res=d_c_kv)
    grads["kv_w_uk"], grads["kv_w_uv"] = matmul("k_up_dw", dk_nope, c_kv, "tn"), matmul("v_up_dw", dv_b, c_kv, "tn")
    d_c_q_raw, grads["b_q_latent_norm"] = row_vjp_call(
        "q_latent_bwd", _f_rms, [Arg(c_q_raw, diff=True, gdt=mxu), Arg(qlat_norm, "par", diff=True)], [Arg(d_c_q)], tr)
    d_c_kv_raw, d_lat = row_vjp_call(
        "kv_latent_bwd", _f_rms, [Arg(c_kv_raw, diff=True, gdt=mxu), Arg(lat_norm, "par", diff=True)], [Arg(d_c_kv)], tr)
    grads["kv_latent_norm"] = d_lat[0]
    d_hb = matmul("b_in_q_dx", d_c_q_raw, w_cq_t, "nn")
    d_hb = matmul("b_in_gate_dx", d_gate_b, w_gb_t, "nn", res=d_hb)
    grads["b_w_in"] = jnp.concatenate([matmul("b_in_q_dw", d_c_q_raw, hb, "tn"), matmul("b_in_gate_dw", d_gate_b, hb, "tn")], 0)
    d_hk = matmul("kv_down_dx", d_c_kv_raw, w_dkv, "nt")
    d_hk = matmul("kv_down_pe_dx", dk_pe, w_dpe, "nt", res=d_hk)
    grads["kv_w_down"] = jnp.concatenate([matmul("kv_down_dw", hk, d_c_kv_raw, "tn"), matmul("kv_down_pe_dw", hk, dk_pe, "tn")[:, :ROPE]], 1)
    dh1, d_kv_norm, grads["b_norm"] = row_vjp_call(
        "b_norms_bwd", lambda x_, g1, g2: _f_rms2(x_, g1, g2) + (x_,),
        [Arg(h1, diff=True), Arg(kv_norm, "par", diff=True), Arg(b_norm, "par", diff=True)], [Arg(d_hk), Arg(d_hb), Arg(dh2)], tr)
    grads["kv_norm"] = d_kv_norm[0]

    d_og_a = matmul("a_out_dx", dh1, a_w_out, "nt")
    grads["a_w_out"] = matmul("a_out_dw", og_a, dh1, "tn")
    d_o_a, d_gate_a, grads["a_o_gain"] = row_vjp_call(
        "a_out_gate_bwd", _f_out_gate, og_args, [Arg(d_og_a, bc=HEAD, ph=True)], tr, nh=N_HEADS)
    dqkv_a, d_ba, d_alog, d_dtb = delta_bwd(qkv_a, z_ba, alog, dtb, states, t_invs, d_o_a, lp)
    grads["a_log"], grads["a_dt_bias"] = d_alog[:, :N_HEADS], d_dtb[:, :N_HEADS]
    dz_qkv, d_conv = conv_bwd(z_qkv, a_conv, dqkv_a, lp)
    grads["a_conv"] = d_conv.T
    d_hn = matmul("a_in_qkv_dx", dz_qkv, w_qkv_t, "nn")
    d_hn = matmul("a_in_gate_dx", d_gate_a, w_ga_t, "nn", res=d_hn)
    d_hn = matmul("a_in_ba_dx", d_ba, w_ba_t, "nn", res=d_hn)
    grads["a_w_in"] = jnp.concatenate([matmul("a_in_qkv_dw", dz_qkv, hn, "tn"), matmul("a_in_gate_dw", d_gate_a, hn, "tn"),
                                       matmul("a_in_ba_dw", d_ba, hn, "tn")[:2 * N_HEADS]], 0)
    dh0, grads["a_norm"] = row_vjp_call("a_norm_bwd", lambda x_, g_: _f_rms(x_, g_) + (x_,),
                                        [Arg(h0, diff=True), Arg(a_norm, "par", diff=True)], [Arg(d_hn), Arg(dh1)], tr)
    dh0 = dh0.reshape(nb, lp, d)
    grads["meta_tokens"] = meta_grad(dh0).T
    return loss, dh0[:, LEAD:], grads


_SHARDED = (
    ("meta_tokens", True, False), ("a_norm", True, False), ("a_w_in", True, True), ("a_conv", True, False), ("a_w_out", False, True),
    ("kv_w_down", False, True), ("kv_w_uk", True, True), ("kv_w_uv", True, True), ("b_w_in", True, True), ("b_w_uq", True, True),
    ("b_w_out", False, True))
_REPLICATED = ("a_log", "a_dt_bias", "a_o_gain", "kv_norm", "kv_latent_norm", "k_gain", "b_norm", "b_q_latent_norm", "b_q_gain")
_ALL_WEIGHTS = ("meta_tokens", "a_norm", "a_w_in", "a_conv", "a_log", "a_dt_bias", "a_o_gain", "a_w_out", "kv_norm", "kv_w_down",
                "kv_latent_norm", "kv_w_uk", "kv_w_uv", "k_gain", "b_norm", "b_w_in", "b_q_latent_norm", "b_w_uq", "b_q_gain", "b_w_out")


def _round_up(n, m):
    return (n + m - 1) // m * m


def _pack_rows(pieces, row_multiple):
    padded = []
    for p in pieces:
        n = p.shape[-1]
        padded.append(jnp.pad(p, [(0, 0)] * (p.ndim - 1) + [(0, _round_up(n, PACK_COLS) - n)]))
    flat = jnp.concatenate(padded, -1)
    rows = _round_up(flat.shape[-1] // PACK_COLS, row_multiple)
    flat = jnp.pad(flat, [(0, 0)] * (flat.ndim - 1) + [(0, rows * PACK_COLS - flat.shape[-1])])
    return flat.reshape(flat.shape[:-1] + (rows, PACK_COLS))


def _unpack_rows(buf, sizes):
    flat = buf.reshape(buf.shape[:-2] + (-1,))
    out, off = [], 0
    for n in sizes:
        out.append(flat[..., off:off + n])
        off += _round_up(n, PACK_COLS)
    return out


def _shard_2d(a):
    return a.reshape(a.shape[-2:]) if a.ndim > 2 else a


def _kl_shard(a, by_cols):
    return _shard_2d(a).T if by_cols else _shard_2d(a)


_WIDE_GROUPS = (("a_w_in", "b_w_in", "a_w_out", "b_w_out"), ("b_w_uq",), ("kv_w_down",), ("kv_w_uk", "kv_w_uv"))
_SMALL_SHARDED = ("meta_tokens", "a_norm", "a_conv")
_BY_COLS = {name: by_cols for name, by_cols, _ in _SHARDED}
ROW_ALIGN = 16


def _stack_rows(pieces):
    padded, starts, row = [], [], 0
    for p in pieces:
        r = p.shape[-2]
        padded.append(jnp.pad(p, [(0, 0)] * (p.ndim - 2) + [(0, _round_up(r, ROW_ALIGN) - r), (0, 0)]))
        starts.append(row)
        row += _round_up(r, ROW_ALIGN)
    return jnp.concatenate(padded, -2), starts


def gather_weights(local):
    bufs, layout = [], []
    for names in _WIDE_GROUPS:
        shards = [_kl_shard(local[n], _BY_COLS[n]).astype(BF16) for n in names]
        buf, starts = _stack_rows(shards)
        bufs.append(buf)
        layout.append([(n, s, sh.shape[0]) for n, s, sh in zip(names, starts, shards, strict=True)])
    small = [_kl_shard(local[n], _BY_COLS[n]) for n in _SMALL_SHARDED]
    bufs.append(_pack_rows([s.reshape(-1) for s in small], 8))
    gathered = _exchange("all_gather", bufs, scatter=False)
    full = {}
    for got, entries in zip(gathered, layout):
        for name, start, rows in entries:
            full[name] = got[:, start:start + rows].reshape(N_DEV * rows, got.shape[-1])
    for name, part, sh in zip(_SMALL_SHARDED, _unpack_rows(gathered[-1], [s.size for s in small]), small, strict=True):
        full[name] = part.reshape(N_DEV * sh.shape[0], sh.shape[1])
    full["a_norm"] = full["a_norm"].reshape(1, -1)
    return full


def reduce_contributions(name, recv):
    _, r, c = recv.shape
    tr = _pick(r, (256, 128, 64, 32, 16, 8))

    def body(g_ref, o_ref):
        g = g_ref[0].astype(F32)
        for dev in range(1, N_DEV):
            g = g + g_ref[dev].astype(F32)
        o_ref[...] = g

    return pl.pallas_call(
        body, grid=(r // tr,), in_specs=[pl.BlockSpec((N_DEV, tr, c), lambda i: (0, i, 0))], out_specs=pl.BlockSpec((tr, c), lambda i: (i, 0)),
        out_shape=jax.ShapeDtypeStruct((r, c), F32), compiler_params=_cparams(("arbitrary",)), name=name)(recv)


def adamw_all(gs, ws, ms, vs):
    n = len(gs)

    def body(*refs):
        for i in range(n):
            g_ref, w_ref, m_ref, v_ref = (refs[j * n + i] for j in range(4))
            d_ref, mo_ref, vo_ref = (refs[(4 + j) * n + i] for j in range(3))
            g = g_ref[...]
            m_new = ADAM_B1 * m_ref[...] + (1.0 - ADAM_B1) * g
            v_new = ADAM_B2 * v_ref[...] + (1.0 - ADAM_B2) * (g * g)
            m_hat = m_new / (1.0 - ADAM_B1 ** ADAM_STEP)
            v_hat = v_new / (1.0 - ADAM_B2 ** ADAM_STEP)
            d_ref[...] = -ADAM_LR * (m_hat / (jnp.sqrt(v_hat) + ADAM_EPS) + ADAM_WD * w_ref[...])
            mo_ref[...] = m_new
            vo_ref[...] = v_new

    out = [jax.ShapeDtypeStruct(g.shape, F32) for g in gs] * 3
    res = pl.pallas_call(body, out_shape=out, compiler_params=pltpu.CompilerParams(vmem_limit_bytes=VMEM_LIMIT), name="adamw_all")(*gs, *ws, *ms, *vs)
    return res[:n], res[n:2 * n], res[2 * n:]


def kernel(x, meta_tokens, a_norm, a_w_in, a_conv, a_log, a_dt_bias, a_o_gain, a_w_out, kv_norm, kv_w_down, kv_latent_norm, kv_w_uk, kv_w_uv, k_gain, b_norm, b_w_in, b_q_latent_norm, b_w_uq, b_q_gain, b_w_out, loss_target, m_meta_tokens, m_a_norm, m_a_w_in, m_a_conv, m_a_log, m_a_dt_bias, m_a_o_gain, m_a_w_out, m_kv_norm, m_kv_w_down, m_kv_latent_norm, m_kv_w_uk, m_kv_w_uv, m_k_gain, m_b_norm, m_b_w_in, m_b_q_latent_norm, m_b_w_uq, m_b_q_gain, m_b_w_out, v_meta_tokens, v_a_norm, v_a_w_in, v_a_conv, v_a_log, v_a_dt_bias, v_a_o_gain, v_a_w_out, v_kv_norm, v_kv_w_down, v_kv_latent_norm, v_kv_w_uk, v_kv_w_uv, v_k_gain, v_b_norm, v_b_w_in, v_b_q_latent_norm, v_b_w_uq, v_b_q_gain, v_b_w_out):
    given = dict(locals())
    local_w = {n: given[n] for n in _ALL_WEIGHTS}
    full = gather_weights(local_w)
    for n in _REPLICATED:
        full[n] = local_w[n]

    loss_part, grad_x, grads = local_step(x, loss_target, full)

    bufs, layout = [], []
    for names in _WIDE_GROUPS:
        slices = [grads[n].reshape(N_DEV, -1, grads[n].shape[-1]).astype(BF16) for n in names]
        buf, starts = _stack_rows(slices)
        bufs.append(buf)
        layout.append([(n, s, sl.shape[1]) for n, s, sl in zip(names, starts, slices, strict=True)])
    exact = [grads[n].reshape(N_DEV, -1) for n in _SMALL_SHARDED]
    exact += [jnp.broadcast_to(grads[n].reshape(1, -1), (N_DEV, grads[n].size)) for n in _REPLICATED]
    exact.append(jnp.broadcast_to(loss_part, (N_DEV, 1)))
    bufs.append(_pack_rows(exact, 8))
    received = _exchange("all_to_all", bufs, scatter=True)
    summed = [reduce_contributions(f"reduce_{i}", r) for i, r in enumerate(received)]

    grad_kl = {}
    for got, entries in zip(summed, layout):
        for n, start, rows in entries:
            grad_kl[n] = got[start:start + rows]
    parts = _unpack_rows(summed[-1], [p.shape[1] for p in exact])
    for n, part in zip(_SMALL_SHARDED + _REPLICATED, parts, strict=False):
        grad_kl[n] = part
    loss = parts[-1][0]

    def natural_2d(n, a):
        shape = _shard_2d(local_w[n]).shape if local_w[n].ndim > 1 else (1, local_w[n].size)
        return a.reshape(shape[::-1]).T if _BY_COLS.get(n, False) else a.reshape(shape)

    as_2d = lambda n, a: a.reshape(natural_2d(n, grad_kl[n]).shape)
    gs = [natural_2d(n, grad_kl[n]) for n in _ALL_WEIGHTS]
    deltas, new_m, new_v = adamw_all(gs, [as_2d(n, local_w[n]) for n in _ALL_WEIGHTS], [as_2d(n, given["m_" + n]) for n in _ALL_WEIGHTS],
                                     [as_2d(n, given["v_" + n]) for n in _ALL_WEIGHTS])
    results = [a.reshape(local_w[n].shape) for group in (gs, deltas, new_m, new_v) for n, a in zip(_ALL_WEIGHTS, group, strict=True)]
    return (loss, grad_x, *results)
```

```python
import dataclasses
import functools
import math

import jax
import jax.numpy as jnp
from jax import lax
from jax.experimental import pallas as pl
from jax.experimental.pallas import tpu as pltpu

F32 = jnp.float32
BF16 = jnp.bfloat16
_MXU_DTYPE = jnp.bfloat16

N_DEV = 8
D_MODEL = 1024
N_HEADS = 8
HEAD = 128
CHUNK = 64
N_META = 16
PAD_ROWS = 2 * CHUNK - N_META
LEAD = PAD_ROWS + N_META
ROPE = 64
QK_DIM = HEAD + ROPE
QK_PAD = 2 * HEAD
KV_RANK = 256
Q_RANK = 384
CONV_K = 4
EPS = 1e-6
NEG = -1e30
ROPE_THETA = 10000.0
ADAM_LR, ADAM_B1, ADAM_B2, ADAM_EPS, ADAM_WD, ADAM_STEP = 0.001, 0.9, 0.999, 1e-08, 0.01, 10
PACK_COLS = 512
VMEM_LIMIT = 56 * 1024 * 1024


def _pick(n, options):
    for o in options:
        if n % o == 0:
            return o
    raise ValueError(f"no tile for {n} among {options}")


def _cparams(sem):
    return pltpu.CompilerParams(dimension_semantics=sem, vmem_limit_bytes=VMEM_LIMIT)


def _dims(a, dims):
    if a.ndim == 2:
        return (dims, ((), ()))
    (ca,), (cb,) = dims
    return (((ca + 1,), (cb + 1,)), ((0,), (0,)))


def _dot(a, b, dims):
    return lax.dot_general(a.astype(_MXU_DTYPE), b.astype(_MXU_DTYPE), _dims(a, dims), preferred_element_type=F32)


@jax.custom_vjp
def mm_nn(a, b):
    return _dot(a, b, ((1,), (0,)))


@jax.custom_vjp
def mm_nt(a, b):
    return _dot(a, b, ((1,), (1,)))


@jax.custom_vjp
def mm_tn(a, b):
    return _dot(a, b, ((0,), (0,)))


mm_nn.defvjp(lambda a, b: (mm_nn(a, b), (a, b)), lambda r, g: (mm_nt(g, r[1]), mm_tn(r[0], g)))
mm_nt.defvjp(lambda a, b: (mm_nt(a, b), (a, b)), lambda r, g: (mm_nn(g, r[1]), mm_tn(g, r[0])))
mm_tn.defvjp(lambda a, b: (mm_tn(a, b), (a, b)), lambda r, g: (mm_nt(r[1], g), mm_nn(r[0], g)))


def _split_terms(x, n):
    terms, rest = [], x
    for _ in range(n):
        t = rest.astype(_MXU_DTYPE)
        terms.append(t)
        rest = rest - t.astype(F32)
    return terms


def _dot_01_raw(m, x, dims):
    m = m.astype(_MXU_DTYPE)
    return sum(lax.dot_general(m, t, _dims(m, dims), preferred_element_type=F32) for t in _split_terms(x, 3))


@jax.custom_vjp
def _dot_01(m, x):
    return _dot_01_raw(m, x, ((1,), (0,)))


_dot_01.defvjp(lambda m, x: (_dot_01(m, x), m), lambda m, g: (jnp.zeros_like(m), _dot_01_raw(m, g, ((0,), (0,)))))


def _inv_unit_lower(a):
    n = a.shape[-1]
    eye = (lax.broadcasted_iota(jnp.int32, (n, n), 0) == lax.broadcasted_iota(jnp.int32, (n, n), 1)).astype(F32)
    d = lambda u, w: lax.dot_general(u, w, _dims(u, ((1,), (0,))), preferred_element_type=F32)
    t = eye - a
    p = a.astype(_MXU_DTYPE)
    p = d(p, p)
    squarings = int(math.log2(n)) - 1
    for s in range(squarings):
        ph = p.astype(_MXU_DTYPE)
        t_hi, t_lo = _split_terms(t, 2)
        t = t + (d(t_hi, ph) + d(t_lo, ph))
        if s + 1 < squarings:
            p = d(ph, ph)
    return t


@jax.custom_vjp
def _inv_lookup(a, t):
    return t


def _inv_lookup_bwd(t, g):
    return -mm_tn(t, mm_nt(g, t)), jnp.zeros_like(t)


_inv_lookup.defvjp(lambda a, t: (t, t), _inv_lookup_bwd)


def _sigmoid(x):
    return 1.0 / (1.0 + jnp.exp(-x))


def _silu(x):
    return x * _sigmoid(x)


def _softplus(x):
    return jnp.where(x > 20.0, x, jnp.log(1.0 + jnp.exp(jnp.minimum(x, 20.0))))


def _rms(x, g, width=None):
    ms = jnp.sum(x * x, -1, keepdims=True) / (x.shape[-1] if width is None else width)
    return x * lax.rsqrt(ms + EPS) * g


MM_VMEM_BUDGET = 40 * 1024 * 1024


def _matmul_rows(name, a, b, mode, out_dtype, res):
    m, k = a.shape
    n = b.shape[1] if mode == "nn" else b.shape[0]
    dims = {"nn": ((1,), (0,)), "nt": ((1,), (1,))}[mode]
    out_bytes = jnp.dtype(out_dtype).itemsize

    def vmem(tm):
        blocks = 2 * tm * k * a.dtype.itemsize + 2 * k * n * b.dtype.itemsize + 2 * tm * n * out_bytes + tm * n * 4
        return blocks + (2 * tm * n * res.dtype.itemsize if res is not None else 0)

    tm = next(c for c in (2176, 1088, 512, 256, 128, 64) if m % c == 0 and vmem(c) <= MM_VMEM_BUDGET)

    def body(*refs):
        a_ref, b_ref = refs[:2]
        out = _dot(a_ref[...], b_ref[...], dims)
        if res is not None:
            out = out + refs[2][...].astype(F32)
        refs[-1][...] = out.astype(refs[-1].dtype)

    o_spec = pl.BlockSpec((tm, n), lambda i: (i, 0))
    in_specs = [pl.BlockSpec((tm, k), lambda i: (i, 0)), pl.BlockSpec(b.shape, lambda i: (0, 0))] + ([o_spec] if res is not None else [])
    args = (a, b) + ((res,) if res is not None else ())
    return pl.pallas_call(body, grid=(m // tm,), in_specs=in_specs, out_specs=o_spec, out_shape=jax.ShapeDtypeStruct((m, n), out_dtype),
                          compiler_params=_cparams(("parallel",)), name=name)(*args)


def matmul(name, a, b, mode, out_dtype=F32, res=None):
    if mode != "tn":
        return _matmul_rows(name, a, b, mode, out_dtype, res)
    (k, m), (k2, n) = a.shape, b.shape
    assert k == k2 and res is None, (name, a.shape, b.shape, mode)
    tm = _pick(m, (1024, 512, 384, 256, 128))
    tn = _pick(n, (1024, 512, 384, 256, 128))
    tk = _pick(k, (512, 256, 128))
    nk = k // tk
    dims = ((0,), (0,))

    def body(*refs):
        if res is None:
            a_ref, b_ref, o_ref, acc_ref = refs
        else:
            a_ref, b_ref, r_ref, o_ref, acc_ref = refs
        kk = pl.program_id(2)

        @pl.when(kk == 0)
        def _():
            acc_ref[...] = jnp.zeros_like(acc_ref)

        acc_ref[...] += _dot(a_ref[...], b_ref[...], dims)

        @pl.when(kk == nk - 1)
        def _():
            out = acc_ref[...]
            if res is not None:
                out = out + r_ref[...].astype(F32)
            o_ref[...] = out.astype(o_ref.dtype)

    a_spec = pl.BlockSpec((tk, tm), lambda i, j, kk: (kk, i)) if mode == "tn" else pl.BlockSpec((tm, tk), lambda i, j, kk: (i, kk))
    b_spec = pl.BlockSpec((tn, tk), lambda i, j, kk: (j, kk)) if mode == "nt" else pl.BlockSpec((tk, tn), lambda i, j, kk: (kk, j))
    o_spec = pl.BlockSpec((tm, tn), lambda i, j, kk: (i, j))
    in_specs = [a_spec, b_spec] + ([o_spec] if res is not None else [])
    args = (a, b) + ((res,) if res is not None else ())
    return pl.pallas_call(
        body, grid=(m // tm, n // tn, nk), in_specs=in_specs, out_specs=o_spec,
        out_shape=jax.ShapeDtypeStruct((m, n), out_dtype), scratch_shapes=[pltpu.VMEM((tm, tn), F32)],
        compiler_params=_cparams(("parallel", "parallel", "arbitrary")), name=name)(*args)


@dataclasses.dataclass
class Arg:
    arr: jax.Array
    kind: str = "row"
    bc: int = 0
    base: int = 0
    ph: bool = False
    diff: bool = False
    gdt: object = F32


def _arg_spec(a, tr, nh, ntab, base=None):
    bc = a.bc or a.arr.shape[1]
    base = a.base if base is None else base
    width = bc * nh if a.ph else bc
    col = base // nh if a.ph else base
    assert not a.ph or base % nh == 0
    if a.kind == "row":
        return pl.BlockSpec((tr, width), lambda i: (i, col))
    if a.kind == "tab":
        return pl.BlockSpec((tr, width), lambda i: (i % ntab, col))
    return pl.BlockSpec((a.arr.shape[0], width), lambda i: (0, col))


def _head_view(ref, a, h):
    bc = a.bc or a.arr.shape[1]
    v = ref[:, h * bc:(h + 1) * bc] if a.ph else ref[...]
    return v.astype(F32) if jnp.issubdtype(v.dtype, jnp.floating) else v


def row_call(name, fn, args, outs, tr, nh=1, ntab=1):
    t = args[0].arr.shape[0]
    n_in = len(args)
    out_args = [Arg(None, "row", bc, 0, ph) for (_, _, bc, ph) in outs]

    def body(*refs):
        for h in range(nh):
            res = fn(*[_head_view(r, a, h) for r, a in zip(refs[:n_in], args, strict=True)])
            for r, a, v in zip(refs[n_in:], out_args, res, strict=True):
                if a.ph:
                    r[:, h * a.bc:(h + 1) * a.bc] = v.astype(r.dtype)
                elif h == nh - 1:
                    r[...] = v.astype(r.dtype)

    return pl.pallas_call(
        body, grid=(t // tr,), in_specs=[_arg_spec(a, tr, nh, ntab) for a in args], out_specs=[_arg_spec(a, tr, nh, ntab) for a in out_args],
        out_shape=[jax.ShapeDtypeStruct((t, cols), dt) for (cols, dt, _, _) in outs],
        compiler_params=_cparams(("arbitrary",)), name=name)(*[a.arr for a in args])


def row_vjp_call(name, fn, args, cts, tr, nh=1, ntab=1):
    t = args[0].arr.shape[0]
    n_in, n_ct = len(args), len(cts)
    diff_idx = [k for k, a in enumerate(args) if a.diff]

    def body(*refs):
        out_refs = refs[n_in + n_ct:]
        shared = [None] * len(diff_idx)
        for k, r in zip(diff_idx, out_refs, strict=True):
            if args[k].kind == "par":
                @pl.when(pl.program_id(0) == 0)
                def _(r=r):
                    r[...] = jnp.zeros_like(r)

        for h in range(nh):
            vals = [_head_view(r, a, h) for r, a in zip(refs[:n_in], args, strict=True)]
            ct_vals = tuple(_head_view(r, a, h) for r, a in zip(refs[n_in:n_in + n_ct], cts, strict=True))

            def f(*dv, vals=vals):
                full = list(vals)
                for k, v in zip(diff_idx, dv, strict=True):
                    full[k] = v
                return tuple(fn(*full))

            _, vjp = jax.vjp(f, *[vals[k] for k in diff_idx])
            for j, (k, r, g) in enumerate(zip(diff_idx, out_refs, vjp(ct_vals), strict=True)):
                a = args[k]
                bc = a.bc or a.arr.shape[1]
                if not a.ph:
                    shared[j] = g if shared[j] is None else shared[j] + g
                elif a.kind == "row":
                    r[:, h * bc:(h + 1) * bc] = g.astype(r.dtype)
                else:
                    r[:, h * bc:(h + 1) * bc] += g
        for j, (k, r) in enumerate(zip(diff_idx, out_refs, strict=True)):
            if not args[k].ph:
                if args[k].kind == "row":
                    r[...] = shared[j].astype(r.dtype)
                else:
                    r[...] += shared[j]

    out_specs, out_shape = [], []
    for k in diff_idx:
        a = args[k]
        bc = a.bc or a.arr.shape[1]
        out_specs.append(_arg_spec(a, tr, nh, ntab, base=0))
        out_shape.append(jax.ShapeDtypeStruct((t if a.kind == "row" else a.arr.shape[0], bc * (nh if a.ph else 1)), a.gdt if a.kind == "row" else F32))
    in_specs = [_arg_spec(a, tr, nh, ntab) for a in list(args) + list(cts)]
    return pl.pallas_call(
        body, grid=(t // tr,), in_specs=in_specs, out_specs=out_specs, out_shape=out_shape,
        compiler_params=_cparams(("arbitrary",)), name=name)(*[a.arr for a in list(args) + list(cts)])


def _conv_taps(x, w):
    rows = lax.broadcasted_iota(jnp.int32, x.shape, 0)
    y = x * w[CONV_K - 1:CONV_K, :]
    shifted = []
    for s in range(1, CONV_K):
        xs = jnp.where(rows >= s, pltpu.roll(x, s, 0), 0.0)
        shifted.append(xs)
        y = y + xs * w[CONV_K - 1 - s:CONV_K - s, :]
    return y, shifted


CONV_HEADS = 4
CONV_BLOCKS_PER_THIRD = N_HEADS // CONV_HEADS


def _conv_post(y, block):
    a = _silu(y)
    normed = block < 2 * CONV_BLOCKS_PER_THIRD
    scale = jnp.where(block < CONV_BLOCKS_PER_THIRD, HEAD ** -0.5, 1.0)
    return a * jnp.where(normed, lax.rsqrt(jnp.sum(a * a, -1, keepdims=True) + EPS) * scale, 1.0)


def conv_fwd(z, w, lp):
    t, width = z.shape
    cols = CONV_HEADS * HEAD

    def body(z_ref, w_ref, o_ref):
        block = pl.program_id(1)
        for h in range(CONV_HEADS):
            cs = slice(h * HEAD, (h + 1) * HEAD)
            y, _ = _conv_taps(z_ref[:, cs], w_ref[:, cs])
            o_ref[:, cs] = _conv_post(y, block)

    return pl.pallas_call(
        body, grid=(t // lp, width // cols),
        in_specs=[pl.BlockSpec((lp, cols), lambda b, j: (b, j)), pl.BlockSpec((CONV_K, cols), lambda b, j: (0, j))],
        out_specs=pl.BlockSpec((lp, cols), lambda b, j: (b, j)), out_shape=jax.ShapeDtypeStruct((t, width), F32),
        compiler_params=_cparams(("arbitrary", "arbitrary")), name="a_conv_fwd")(z, w)


def conv_bwd(z, w, dout, lp):
    t, width = z.shape
    cols = CONV_HEADS * HEAD

    def body(z_ref, w_ref, g_ref, dz_ref, dw_ref):
        block = pl.program_id(0)

        @pl.when(pl.program_id(1) == 0)
        def _():
            dw_ref[...] = jnp.zeros_like(dw_ref)

        for h in range(CONV_HEADS):
            cs = slice(h * HEAD, (h + 1) * HEAD)
            x, wv = z_ref[:, cs], w_ref[:, cs]
            y, shifted = _conv_taps(x, wv)
            _, vjp = jax.vjp(lambda y_: _conv_post(y_, block), y)
            (dy,) = vjp(g_ref[:, cs])
            rows = lax.broadcasted_iota(jnp.int32, x.shape, 0)
            dx = dy * wv[CONV_K - 1:CONV_K, :]
            for s in range(1, CONV_K):
                dx = dx + jnp.where(rows < lp - s, pltpu.roll(dy, lp - s, 0), 0.0) * wv[CONV_K - 1 - s:CONV_K - s, :]
            dz_ref[:, cs] = dx.astype(dz_ref.dtype)
            dw_ref[CONV_K - 1:CONV_K, cs] += jnp.sum(dy * x, axis=0, keepdims=True)
            for s in range(1, CONV_K):
                dw_ref[CONV_K - 1 - s:CONV_K - s, cs] += jnp.sum(dy * shifted[s - 1], axis=0, keepdims=True)

    blk = pl.BlockSpec((lp, cols), lambda j, b: (b, j))
    w_blk = pl.BlockSpec((CONV_K, cols), lambda j, b: (0, j))
    return pl.pallas_call(
        body, grid=(width // cols, t // lp), in_specs=[blk, w_blk, blk], out_specs=[blk, w_blk],
        out_shape=[jax.ShapeDtypeStruct((t, width), _MXU_DTYPE), jax.ShapeDtypeStruct((CONV_K, width), F32)],
        compiler_params=_cparams(("arbitrary", "arbitrary")), name="a_conv_bwd")(z, w, dout)


def _delta_chunk(q, k, v, ba, alog, dtb, state, t_stored, h0):
    n_g, c = q.shape[0], q.shape[1]
    lane = lax.broadcasted_iota(jnp.int32, (1, HEAD), 1)

    def pick(x, offset):
        return jnp.concatenate([jnp.sum(x * (lane == offset + h0 + g).astype(F32), axis=1, keepdims=True)[None] for g in range(n_g)], 0)

    b_raw, a_raw = pick(ba, 0), pick(ba, N_HEADS)
    a_log, dt_bias = pick(alog, 0), pick(dtb, 0)
    beta = _sigmoid(b_raw)
    g = -jnp.exp(a_log) * _softplus(a_raw + dt_bias)
    ri = lax.broadcasted_iota(jnp.int32, (c, c), 0)
    ci = lax.broadcasted_iota(jnp.int32, (c, c), 1)
    tril = ci <= ri
    lower = jnp.broadcast_to(tril.astype(F32), (n_g, c, c))
    gc_col = _dot_01(lower, g * jnp.ones((1, 1, HEAD), F32))[:, :, :1]
    gc_row = _dot_01(jnp.ones((n_g, 8, c), F32), g * (ri <= ci).astype(F32)[None])[:, 0:1, :]
    gc_last = jnp.sum(g, axis=1, keepdims=True)
    decay = jnp.exp(jnp.where(tril, gc_col - gc_row, NEG))
    e_gc = jnp.exp(gc_col)
    kb = k * beta
    a_mat = jnp.where(ci < ri, mm_nt(kb, k) * decay, 0.0)
    t_inv = _inv_unit_lower(a_mat) if t_stored is None else _inv_lookup(a_mat, t_stored)
    u_base = mm_nn(t_inv, v * beta)
    w_dec = mm_nn(t_inv, kb * e_gc)
    attn = jnp.where(tril, mm_nt(q, k) * decay, 0.0)
    u = u_base - mm_nn(w_dec, state)
    o = mm_nn(q * e_gc, state) + mm_nn(attn, u)
    new_state = state * jnp.exp(gc_last) + mm_tn(k * jnp.exp(gc_last - gc_col), u)
    return o, new_state, t_inv


DELTA_CHUNKS_FWD = 2
DELTA_CHUNKS_BWD = 1


def _qkv_heads(ref, rs, part):
    return jnp.stack([ref[rs, (part * N_HEADS + g) * HEAD:(part * N_HEADS + g + 1) * HEAD] for g in range(N_HEADS)])


def _ride(bufs, scatter, refs_in, refs_out, sems, first, last):
    if not bufs:
        return lambda: None

    @pl.when(first)
    def _():
        Exchange(refs_in, refs_out, *sems, scatter).start()

    def finish():
        @pl.when(last)
        def _():
            Exchange(refs_in, refs_out, *sems, scatter).wait()

    return finish


def delta_fwd(qkv, ba, alog, dtb, lp, gather=()):
    t = qkv.shape[0]
    nb, nc = t // lp, lp // CHUNK
    cps = DELTA_CHUNKS_FWD
    ng, rows = nc // cps, cps * CHUNK
    nx = len(gather)
    assert nc % cps == 0

    def body(*refs):
        qkv_ref, ba_ref, al_ref, dt_ref = refs[:4]
        o_ref, s_ref, t_ref = refs[4 + nx:7 + nx]
        state_ref = refs[7 + 2 * nx]
        b, n = pl.program_id(0), pl.program_id(1)
        finish = _ride(gather, False, refs[4:4 + nx], refs[7 + nx:7 + 2 * nx], refs[8 + 2 * nx:], (b == 0) & (n == 0), (b == nb - 1) & (n == ng - 1))

        @pl.when(n == 0)
        def _():
            state_ref[...] = jnp.zeros_like(state_ref)

        al, dtv = al_ref[...], dt_ref[...]
        for c in range(cps):
            rs = slice(c * CHUNK, (c + 1) * CHUNK)
            state = state_ref[...]
            o, new_state, t_inv = _delta_chunk(_qkv_heads(qkv_ref, rs, 0), _qkv_heads(qkv_ref, rs, 1), _qkv_heads(qkv_ref, rs, 2),
                                               ba_ref[rs, :], al, dtv, state, None, 0)
            for g in range(N_HEADS):
                o_ref[rs, g * HEAD:(g + 1) * HEAD] = o[g]
                s_ref[g, c] = state[g]
                t_ref[g, c] = t_inv[g]
            state_ref[...] = new_state
        finish()

    rows_of = lambda width: pl.BlockSpec((rows, width), lambda b, n: (b * ng + n, 0))
    par_spec = pl.BlockSpec((1, HEAD), lambda b, n: (0, 0))
    return pl.pallas_call(
        body, grid=(nb, ng), in_specs=[rows_of(3 * N_HEADS * HEAD), rows_of(HEAD), par_spec, par_spec] + [_HBM] * nx,
        out_specs=[rows_of(N_HEADS * HEAD), pl.BlockSpec((None, N_HEADS, cps, HEAD, HEAD), lambda b, n: (b, 0, n, 0, 0)),
                   pl.BlockSpec((None, N_HEADS, cps, CHUNK, CHUNK), lambda b, n: (b, 0, n, 0, 0))] + [_HBM] * nx,
        out_shape=[jax.ShapeDtypeStruct((t, N_HEADS * HEAD), F32), jax.ShapeDtypeStruct((nb, N_HEADS, nc, HEAD, HEAD), F32),
                   jax.ShapeDtypeStruct((nb, N_HEADS, nc, CHUNK, CHUNK), F32)] + Exchange.out_shape(gather, False),
        scratch_shapes=[pltpu.VMEM((N_HEADS, HEAD, HEAD), F32)] + (Exchange.scratch(nx) if nx else []),
        compiler_params=_cparams(("arbitrary", "arbitrary")), name="delta_fwd")(qkv, ba, alog, dtb, *gather)


def delta_bwd(qkv, ba, alog, dtb, states, t_invs, do, lp, scatter=()):
    t = qkv.shape[0]
    nb, nc = t // lp, lp // CHUNK
    cps = DELTA_CHUNKS_BWD
    ng, rows = nc // cps, cps * CHUNK
    nx = len(scatter)

    def body(*refs):
        qkv_ref, ba_ref, al_ref, dt_ref, s_ref, t_ref, do_ref = refs[:7]
        dqkv_ref, dba_ref, dal_ref, ddt_ref = refs[7 + nx:11 + nx]
        dstate_ref = refs[11 + 2 * nx]
        b, step = pl.program_id(0), pl.program_id(1)
        finish = _ride(scatter, True, refs[7:7 + nx], refs[11 + nx:11 + 2 * nx], refs[12 + 2 * nx:], (b == 0) & (step == 0),
                       (b == nb - 1) & (step == ng - 1))

        @pl.when(step == 0)
        def _():
            dstate_ref[...] = jnp.zeros_like(dstate_ref)

        @pl.when((b == 0) & (step == 0))
        def _():
            dal_ref[...] = jnp.zeros_like(dal_ref)
            ddt_ref[...] = jnp.zeros_like(ddt_ref)

        al, dtv = al_ref[...], dt_ref[...]
        d_al = jnp.zeros((1, HEAD), F32)
        d_dt = jnp.zeros((1, HEAD), F32)
        for c in reversed(range(cps)):
            rs = slice(c * CHUNK, (c + 1) * CHUNK)
            t_n = jnp.stack([t_ref[g, c] for g in range(N_HEADS)])
            s_n = jnp.stack([s_ref[g, c] for g in range(N_HEADS)])
            d_o = jnp.stack([do_ref[rs, g * HEAD:(g + 1) * HEAD] for g in range(N_HEADS)])

            def f(q_, k_, v_, ba_, al_, dt_, s_, t_n=t_n):
                return _delta_chunk(q_, k_, v_, ba_, al_, dt_, s_, t_n, 0)[:2]

            _, vjp = jax.vjp(f, _qkv_heads(qkv_ref, rs, 0), _qkv_heads(qkv_ref, rs, 1), _qkv_heads(qkv_ref, rs, 2), ba_ref[rs, :], al, dtv, s_n)
            grads = vjp((d_o, dstate_ref[...]))
            for part in range(3):
                for g in range(N_HEADS):
                    dqkv_ref[rs, (part * N_HEADS + g) * HEAD:(part * N_HEADS + g + 1) * HEAD] = grads[part][g]
            dba_ref[rs, :] = grads[3]
            d_al, d_dt = d_al + grads[4], d_dt + grads[5]
            dstate_ref[...] = grads[6]
        dal_ref[...] += d_al
        ddt_ref[...] += d_dt
        finish()

    rows_of = lambda width: pl.BlockSpec((rows, width), lambda b, n: (b * ng + ng - 1 - n, 0))
    par_spec = pl.BlockSpec((1, HEAD), lambda b, n: (0, 0))
    return pl.pallas_call(
        body, grid=(nb, ng),
        in_specs=[rows_of(3 * N_HEADS * HEAD), rows_of(HEAD), par_spec, par_spec,
                  pl.BlockSpec((None, N_HEADS, cps, HEAD, HEAD), lambda b, n: (b, 0, ng - 1 - n, 0, 0)),
                  pl.BlockSpec((None, N_HEADS, cps, CHUNK, CHUNK), lambda b, n: (b, 0, ng - 1 - n, 0, 0)), rows_of(N_HEADS * HEAD)] + [_HBM] * nx,
        out_specs=[rows_of(3 * N_HEADS * HEAD), rows_of(HEAD), par_spec, par_spec] + [_HBM] * nx,
        out_shape=[jax.ShapeDtypeStruct((t, 3 * N_HEADS * HEAD), F32), jax.ShapeDtypeStruct((t, HEAD), F32),
                   jax.ShapeDtypeStruct((1, HEAD), F32), jax.ShapeDtypeStruct((1, HEAD), F32)] + Exchange.out_shape(scatter, True),
        scratch_shapes=[pltpu.VMEM((N_HEADS, HEAD, HEAD), F32)] + (Exchange.scratch(nx) if nx else []),
        compiler_params=_cparams(("arbitrary", "arbitrary")), name="delta_bwd")(qkv, ba, alog, dtb, states, t_invs, do, *scatter)


ATT_Q_TILE = 256
ATT_K_TILE = 512
ATT_SCALE = QK_DIM ** -0.5


def _tiles(end, size):
    return [(s, min(s + size, end)) for s in range(0, end, size)]


def _att_visible(q0, q1, k0, k1, keys_first):
    if k1 <= q0 + CHUNK and k0 >= PAD_ROWS:
        return None
    shape = (k1 - k0, q1 - q0) if keys_first else (q1 - q0, k1 - k0)
    qpos = q0 + lax.broadcasted_iota(jnp.int32, shape, 1 if keys_first else 0)
    kpos = k0 + lax.broadcasted_iota(jnp.int32, shape, 0 if keys_first else 1)
    shift = CHUNK.bit_length() - 1
    return (jnp.right_shift(kpos, shift) <= jnp.right_shift(qpos, shift)) & (kpos >= PAD_ROWS)


def _att_seq_specs(lp):
    return pl.BlockSpec((lp, QK_PAD), lambda b, h: (b, h)), pl.BlockSpec((lp, HEAD), lambda b, h: (b, h))


def flash_fwd(q, k, v, lp):
    t = q.shape[0]
    qk_seq, o_seq = _att_seq_specs(lp)

    def body(q_ref, k_ref, v_ref, o_ref, lse_ref):
        for q0, q1 in _tiles(lp, ATT_Q_TILE):
            qb = q_ref[q0:q1, :]
            k_tiles = _tiles(q1, ATT_K_TILE)
            scores, m = [], None
            for k0, k1 in k_tiles:
                s = mm_nt(qb, k_ref[k0:k1, :]) * ATT_SCALE
                vis = _att_visible(q0, q1, k0, k1, False)
                s = s if vis is None else jnp.where(vis, s, NEG)
                scores.append(s)
                row_max = jnp.max(s, -1, keepdims=True)
                m = row_max if m is None else jnp.maximum(m, row_max)
            l = jnp.zeros((q1 - q0, 1), F32)
            acc = jnp.zeros((q1 - q0, HEAD), F32)
            for s, (k0, k1) in zip(scores, k_tiles, strict=True):
                p = jnp.exp(s - m)
                l = l + jnp.sum(p, -1, keepdims=True)
                acc = acc + mm_nn(p, v_ref[k0:k1, :])
            o_ref[q0:q1, :] = acc / l
            lse_ref[q0:q1, :] = jnp.broadcast_to(m + jnp.log(l), (q1 - q0, HEAD))

    big = jax.ShapeDtypeStruct((t, N_HEADS * HEAD), F32)
    return pl.pallas_call(
        body, grid=(t // lp, N_HEADS), in_specs=[qk_seq, qk_seq, o_seq], out_specs=[o_seq, o_seq], out_shape=[big, big],
        compiler_params=_cparams(("arbitrary", "arbitrary")), name="flash_fwd")(q, k, v)


def flash_bwd(q, k, v, o, lse, do, lp):
    t = q.shape[0]
    qk_seq, o_seq = _att_seq_specs(lp)

    def body(q_ref, k_ref, v_ref, o_ref, lse_ref, do_ref, dq_ref, dk_ref, dv_ref):
        dk_ref[...] = jnp.zeros_like(dk_ref)
        dv_ref[...] = jnp.zeros_like(dv_ref)
        for q0, q1 in _tiles(lp, ATT_Q_TILE):
            qb, dob = q_ref[q0:q1, :], do_ref[q0:q1, :]
            lse_row = jnp.transpose(lse_ref[q0:q1, :])[0:1, :]
            dsum_row = jnp.sum(jnp.transpose(dob * o_ref[q0:q1, :]), axis=0, keepdims=True)
            dq = jnp.zeros((q1 - q0, QK_PAD), F32)
            for k0, k1 in _tiles(q1, ATT_K_TILE):
                kb, vb = k_ref[k0:k1, :], v_ref[k0:k1, :]
                s = mm_nt(kb, qb) * ATT_SCALE
                vis = _att_visible(q0, q1, k0, k1, True)
                s = s if vis is None else jnp.where(vis, s, NEG)
                p = jnp.exp(s - lse_row)
                ds = p * (mm_nt(vb, dob) - dsum_row) * ATT_SCALE
                dv_ref[k0:k1, :] += mm_nn(p, dob)
                dk_ref[k0:k1, :] += mm_nn(ds, qb)
                dq = dq + mm_tn(ds, kb)
            dq_ref[q0:q1, :] = dq

    return pl.pallas_call(
        body, grid=(t // lp, N_HEADS), in_specs=[qk_seq, qk_seq, o_seq, o_seq, o_seq, o_seq], out_specs=[qk_seq, qk_seq, o_seq],
        out_shape=[jax.ShapeDtypeStruct((t, N_HEADS * QK_PAD), F32), jax.ShapeDtypeStruct((t, N_HEADS * QK_PAD), F32),
                   jax.ShapeDtypeStruct((t, N_HEADS * HEAD), F32)],
        compiler_params=_cparams(("arbitrary", "arbitrary")), name="flash_bwd")(q, k, v, o, lse, do)


def loss_head(h2, target, lp):
    nb, seq, d = target.shape
    tr = 128
    nblk = lp // tr
    lead_blocks = LEAD // tr

    def body(h_ref, t_ref, loss_ref, dh_ref, acc_ref):
        b, i = pl.program_id(0), pl.program_id(1)

        @pl.when((b == 0) & (i == 0))
        def _():
            acc_ref[...] = jnp.zeros_like(acc_ref)

        @pl.when(i < lead_blocks)
        def _():
            dh_ref[...] = jnp.zeros_like(dh_ref)

        @pl.when(i >= lead_blocks)
        def _():
            err = h_ref[...] - t_ref[...]
            dh_ref[...] = err * (1.0 / d)
            acc_ref[...] += jnp.sum(err * err, axis=0, keepdims=True)

        @pl.when((b == nb - 1) & (i == nblk - 1))
        def _():
            loss_ref[...] = jnp.sum(acc_ref[...], axis=1, keepdims=True) * (0.5 / d)

    return pl.pallas_call(
        body, grid=(nb, nblk),
        in_specs=[pl.BlockSpec((None, tr, d), lambda b, i: (b, i, 0)),
                  pl.BlockSpec((None, tr, d), lambda b, i: (b, jnp.maximum(i - lead_blocks, 0), 0))],
        out_specs=[pl.BlockSpec((1, 1), lambda b, i: (0, 0)), pl.BlockSpec((None, tr, d), lambda b, i: (b, i, 0))],
        out_shape=[jax.ShapeDtypeStruct((1, 1), F32), jax.ShapeDtypeStruct((nb, lp, d), F32)],
        scratch_shapes=[pltpu.VMEM((1, d), F32)], compiler_params=_cparams(("arbitrary", "arbitrary")), name="loss_head")(h2, target)


def meta_grad(dh0):
    nb, _, d = dh0.shape

    def body(g_ref, o_ref):
        @pl.when(pl.program_id(0) == 0)
        def _():
            o_ref[...] = jnp.zeros_like(o_ref)

        o_ref[...] += g_ref[PAD_ROWS:LEAD, :]

    return pl.pallas_call(
        body, grid=(nb,), in_specs=[pl.BlockSpec((None, LEAD, d), lambda b: (b, 0, 0))],
        out_specs=pl.BlockSpec((N_META, d), lambda b: (0, 0)), out_shape=jax.ShapeDtypeStruct((N_META, d), F32),
        compiler_params=_cparams(("arbitrary",)), name="meta_grad")(dh0)


_HBM = pl.BlockSpec(memory_space=pltpu.HBM)


def _mesh_pos():
    x, y, c = lax.axis_index("x"), lax.axis_index("y"), lax.axis_index("c")
    return x, y, c


def _peer(x, y, c, k):
    px = 1 - x if k & 4 else x
    py = 1 - y if k & 2 else y
    pc = 1 - c if k & 1 else c
    return (px, py, pc), 4 * px + 2 * py + pc


class Exchange:
    def __init__(self, x_refs, out_refs, send_sems, recv_sems, local_sems, scatter):
        self.x_refs, self.out_refs, self.scatter = x_refs, out_refs, scatter
        self.send_sems, self.recv_sems, self.local_sems = send_sems, recv_sems, local_sems
        self.pos = _mesh_pos()
        x, y, c = self.pos
        self.me = 4 * x + 2 * y + c

    @staticmethod
    def scratch(n):
        return [pltpu.SemaphoreType.DMA((n, N_DEV - 1)), pltpu.SemaphoreType.DMA((n, N_DEV - 1)), pltpu.SemaphoreType.DMA((n,))]

    @staticmethod
    def out_shape(bufs, scatter):
        return [jax.ShapeDtypeStruct(b.shape if scatter else (N_DEV,) + b.shape, b.dtype) for b in bufs]

    def _local(self, i):
        return pltpu.make_async_copy(self.x_refs[i].at[self.me] if self.scatter else self.x_refs[i], self.out_refs[i].at[self.me], self.local_sems.at[i])

    def _copy(self, i, k, landing):
        peer, peer_id = _peer(*self.pos, k)
        src = self.x_refs[i].at[peer_id] if self.scatter else self.x_refs[i]
        return pltpu.make_async_remote_copy(src_ref=src, dst_ref=self.out_refs[i].at[peer_id if landing else self.me],
                                            send_sem=self.send_sems.at[i, k - 1], recv_sem=self.recv_sems.at[i, k - 1],
                                            device_id=peer, device_id_type=pl.DeviceIdType.MESH)

    def start(self):
        for i in range(len(self.x_refs)):
            self._local(i).start()
        for k in range(1, N_DEV):
            for i in range(len(self.x_refs)):
                self._copy(i, k, False).start()

    def wait(self):
        for k in range(1, N_DEV):
            for i in range(len(self.x_refs)):
                self._copy(i, k, True).wait_recv()
        for k in range(1, N_DEV):
            for i in range(len(self.x_refs)):
                self._copy(i, k, False).wait_send()
        for i in range(len(self.x_refs)):
            self._local(i).wait()


def _exchange(name, bufs, scatter):
    n = len(bufs)

    def body(*refs):
        ex = Exchange(refs[:n], refs[n:2 * n], *refs[2 * n:], scatter)
        ex.start()
        ex.wait()

    return pl.pallas_call(body, in_specs=[_HBM] * n, out_specs=[_HBM] * n, out_shape=Exchange.out_shape(bufs, scatter),
                          scratch_shapes=Exchange.scratch(n), name=name)(*bufs)


def _f_rms(x, g):
    return (_rms(x, g),)


def _f_rms2(x, g1, g2):
    r = x * lax.rsqrt(jnp.sum(x * x, -1, keepdims=True) / x.shape[-1] + EPS)
    return r * g1, r * g2


def _f_out_gate(o, gate, gain):
    return (_rms(o, gain) * _silu(gate),)


def _f_gate(o, gate):
    return (o * _silu(gate),)


@jax.custom_vjp
def _swap_rope_halves(x):
    half = ROPE // 2
    lane = lax.broadcasted_iota(jnp.int32, x.shape, 1)
    return jnp.where(lane < half, pltpu.roll(x, HEAD - half, 1), jnp.where(lane < ROPE, pltpu.roll(x, half, 1), 0.0))


_swap_rope_halves.defvjp(lambda x: (_swap_rope_halves(x), None), lambda _, g: (_swap_rope_halves(g),))


def _f_qk_final(nope, rope_in, g_nope, g_rope, cos, sin):
    ms = (jnp.sum(nope * nope, -1, keepdims=True) + jnp.sum(rope_in * rope_in, -1, keepdims=True)) / QK_DIM
    r = lax.rsqrt(ms + EPS)
    a = nope * r * g_nope
    b = rope_in * r * g_rope
    return (jnp.concatenate([a, b * cos + _swap_rope_halves(b) * sin], axis=1),)


def _rope_tables(lp):
    half = ROPE // 2
    pos = jnp.maximum(jnp.arange(lp) - PAD_ROWS, 0)
    inv = ROPE_THETA ** (-jnp.arange(half, dtype=F32) / half)
    ang = pos.astype(F32)[:, None] * inv[None, :]
    zeros = jnp.zeros((lp, HEAD - ROPE), F32)
    cos = jnp.concatenate([jnp.cos(ang), jnp.cos(ang), zeros], 1)
    sin = jnp.concatenate([-jnp.sin(ang), jnp.sin(ang), zeros], 1)
    return cos, sin


def _pad_lanes(w, width=HEAD):
    return jnp.pad(w, ((0, 0), (0, width - w.shape[1])))


def _pad_rows(w, rows=HEAD):
    return jnp.pad(w, ((0, rows - w.shape[0]), (0, 0)))


def _split_heads_qk_t(w_t):
    k = w_t.shape[1]
    w3 = w_t.reshape(N_HEADS, QK_DIM, k)
    nope = w3[:, :HEAD].reshape(N_HEADS * HEAD, k)
    rope = jnp.pad(w3[:, HEAD:], ((0, 0), (0, HEAD - ROPE), (0, 0))).reshape(N_HEADS * HEAD, k)
    return jnp.concatenate([nope, rope], 0)


def _merge_heads_qk_t(g_t):
    k = g_t.shape[1]
    kw = N_HEADS * HEAD
    nope, rope = g_t[:kw].reshape(N_HEADS, HEAD, k), g_t[kw:].reshape(N_HEADS, HEAD, k)[:, :ROPE]
    return jnp.concatenate([nope, rope], 1).reshape(N_HEADS * QK_DIM, k)


def local_step(x, target, w, deferred=None):
    nb, seq, d = x.shape
    lp = seq + LEAD
    t = nb * lp
    tr = _pick(lp, (544, 128))
    ntab = lp // tr
    mxu = _MXU_DTYPE
    kw = N_HEADS * HEAD

    a_w_in_t = w["a_w_in"].astype(mxu)
    w_qkv_t, w_ga_t, w_ba_t = a_w_in_t[:3 * kw], a_w_in_t[3 * kw:4 * kw], _pad_rows(a_w_in_t[4 * kw:])
    a_conv = w["a_conv"].T
    a_w_out = w["a_w_out"].astype(mxu)
    alog, dtb, o_gain = _pad_lanes(w["a_log"]), _pad_lanes(w["a_dt_bias"]), w["a_o_gain"]
    a_norm, kv_norm, b_norm = w["a_norm"], w["kv_norm"][None, :], w["b_norm"]
    lat_norm, qlat_norm = w["kv_latent_norm"][None, :], w["b_q_latent_norm"]
    kg_nope, kg_rope = w["k_gain"][None, :HEAD], _pad_lanes(w["k_gain"][None, HEAD:])
    qg_nope, qg_rope = w["b_q_gain"][:, :HEAD], _pad_lanes(w["b_q_gain"][:, HEAD:])
    cos, sin = _rope_tables(lp)

    meta = jnp.broadcast_to(w["meta_tokens"].T[None], (nb, N_META, d))
    h0 = jnp.concatenate([jnp.zeros((nb, PAD_ROWS, d), F32), meta, x], 1).reshape(t, d)
    (hn,) = row_call("a_norm_fwd", _f_rms, [Arg(h0), Arg(a_norm, "par")], [(d, mxu, d, False)], tr)
    z_qkv = matmul("a_in_qkv", hn, w_qkv_t, "nt")
    gate_a = matmul("a_in_gate", hn, w_ga_t, "nt")
    z_ba = matmul("a_in_ba", hn, w_ba_t, "nt")
    qkv_a = conv_fwd(z_qkv, a_conv, lp)
    o_a, states, t_invs, *gathered = delta_fwd(qkv_a, z_ba, alog, dtb, lp, gather=deferred.gather_bufs if deferred else ())
    if deferred:
        w = {**w, **deferred.finish(gathered)}
    w_dkv, w_dpe = w["kv_w_down"][:, :KV_RANK].astype(mxu), _pad_lanes(w["kv_w_down"][:, KV_RANK:]).astype(mxu)
    w_ukv_t = jnp.concatenate([w["kv_w_uk"], w["kv_w_uv"]], 0).astype(mxu)
    b_w_in_t = w["b_w_in"].astype(mxu)
    w_cq_t, w_gb_t = b_w_in_t[:Q_RANK], b_w_in_t[Q_RANK:]
    w_q_t = _split_heads_qk_t(w["b_w_uq"]).astype(mxu)
    b_w_out = w["b_w_out"].astype(mxu)
    og_args = [Arg(o_a, bc=HEAD, ph=True, diff=True), Arg(gate_a, bc=HEAD, ph=True, diff=True, gdt=mxu), Arg(o_gain, "par", diff=True)]
    (og_a,) = row_call("a_out_gate_fwd", _f_out_gate, og_args, [(kw, mxu, HEAD, True)], tr, nh=N_HEADS)
    h1 = matmul("a_out", og_a, a_w_out, "nn", res=h0)

    hk, hb = row_call("b_norms_fwd", _f_rms2, [Arg(h1), Arg(kv_norm, "par"), Arg(b_norm, "par")], [(d, mxu, d, False), (d, mxu, d, False)], tr)
    c_kv_raw = matmul("kv_down", hk, w_dkv, "nn")
    k_pe = matmul("kv_down_pe", hk, w_dpe, "nn")
    c_q_raw = matmul("b_in_q", hb, w_cq_t, "nt")
    gate_b = matmul("b_in_gate", hb, w_gb_t, "nt")
    (c_kv,) = row_call("kv_latent_fwd", _f_rms, [Arg(c_kv_raw), Arg(lat_norm, "par")], [(KV_RANK, mxu, KV_RANK, False)], tr)
    (c_q,) = row_call("q_latent_fwd", _f_rms, [Arg(c_q_raw), Arg(qlat_norm, "par")], [(Q_RANK, mxu, Q_RANK, False)], tr)
    k_nope = matmul("k_up", c_kv, w_ukv_t[:kw], "nt")
    v_b = matmul("v_up", c_kv, w_ukv_t[kw:], "nt", out_dtype=mxu)
    q_up = matmul("q_up", c_q, w_q_t, "nt")
    tabs = [Arg(cos, "tab"), Arg(sin, "tab")]
    k_args = [Arg(k_nope, bc=HEAD, ph=True, diff=True, gdt=mxu), Arg(k_pe, diff=True), Arg(kg_nope, "par", diff=True), Arg(kg_rope, "par", diff=True)] + tabs
    q_args = [Arg(q_up, bc=HEAD, ph=True, diff=True, gdt=mxu), Arg(q_up, bc=HEAD, base=N_HEADS, ph=True, diff=True, gdt=mxu),
              Arg(qg_nope, "par", diff=True), Arg(qg_rope, "par", diff=True)] + tabs
    (k_fin,) = row_call("k_final_fwd", _f_qk_final, k_args, [(N_HEADS * QK_PAD, mxu, QK_PAD, True)], tr, nh=N_HEADS, ntab=ntab)
    (q_fin,) = row_call("q_final_fwd", _f_qk_final, q_args, [(N_HEADS * QK_PAD, mxu, QK_PAD, True)], tr, nh=N_HEADS, ntab=ntab)
    o_b, lse = flash_fwd(q_fin, k_fin, v_b, lp)
    gb_args = [Arg(o_b, diff=True), Arg(gate_b, diff=True, gdt=mxu)]
    (og_b,) = row_call("b_gate_fwd", _f_gate, gb_args, [(kw, mxu, kw, False)], tr)
    h2 = matmul("b_out", og_b, b_w_out, "nn", res=h1)

    loss, dh2 = loss_head(h2.reshape(nb, lp, d), target, lp)
    dh2 = dh2.reshape(t, d)
    grads = {}

    d_og_b = matmul("b_out_dx", dh2, b_w_out, "nt")
    grads["b_w_out"] = matmul("b_out_dw", og_b, dh2, "tn")
    d_o_b, d_gate_b = row_vjp_call("b_gate_bwd", _f_gate, gb_args, [Arg(d_og_b)], tr)
    dq_fin, dk_fin, dv_b = flash_bwd(q_fin, k_fin, v_b, o_b, lse, d_o_b, lp)
    dq_nope, dq_rope, d_qg_nope, d_qg_rope = row_vjp_call(
        "q_final_bwd", _f_qk_final, q_args, [Arg(dq_fin, bc=QK_PAD, ph=True)], tr, nh=N_HEADS, ntab=ntab)
    dk_nope, dk_pe, d_kg_nope, d_kg_rope = row_vjp_call(
        "k_final_bwd", _f_qk_final, k_args, [Arg(dk_fin, bc=QK_PAD, ph=True)], tr, nh=N_HEADS, ntab=ntab)
    grads["b_q_gain"] = jnp.concatenate([d_qg_nope, d_qg_rope[:, :ROPE]], 1)
    grads["k_gain"] = jnp.concatenate([d_kg_nope, d_kg_rope[:, :ROPE]], 1)[0]
    d_c_q = matmul("q_nope_dx", dq_nope, w_q_t[:kw], "nn")
    d_c_q = matmul("q_rope_dx", dq_rope, w_q_t[kw:], "nn", res=d_c_q)
    grads["b_w_uq"] = _merge_heads_qk_t(jnp.concatenate([matmul("q_nope_dw", dq_nope, c_q, "tn"), matmul("q_rope_dw", dq_rope, c_q, "tn")], 0))
    d_c_kv = matmul("k_up_dx", dk_nope, w_ukv_t[:kw], "nn")
    d_c_kv = matmul("v_up_dx", dv_b, w_ukv_t[kw:], "nn", res=d_c_kv)
    grads["kv_w_uk"], grads["kv_w_uv"] = matmul("k_up_dw", dk_nope, c_kv, "tn"), matmul("v_up_dw", dv_b, c_kv, "tn")
    d_c_q_raw, grads["b_q_latent_norm"] = row_vjp_call(
        "q_latent_bwd", _f_rms, [Arg(c_q_raw, diff=True, gdt=mxu), Arg(qlat_norm, "par", diff=True)], [Arg(d_c_q)], tr)
    d_c_kv_raw, d_lat = row_vjp_call(
        "kv_latent_bwd", _f_rms, [Arg(c_kv_raw, diff=True, gdt=mxu), Arg(lat_norm, "par", diff=True)], [Arg(d_c_kv)], tr)
    grads["kv_latent_norm"] = d_lat[0]
    d_hb = matmul("b_in_q_dx", d_c_q_raw, w_cq_t, "nn")
    d_hb = matmul("b_in_gate_dx", d_gate_b, w_gb_t, "nn", res=d_hb)
    grads["b_w_in"] = jnp.concatenate([matmul("b_in_q_dw", d_c_q_raw, hb, "tn"), matmul("b_in_gate_dw", d_gate_b, hb, "tn")], 0)
    d_hk = matmul("kv_down_dx", d_c_kv_raw, w_dkv, "nt")
    d_hk = matmul("kv_down_pe_dx", dk_pe, w_dpe, "nt", res=d_hk)
    grads["kv_w_down"] = jnp.concatenate([matmul("kv_down_dw", hk, d_c_kv_raw, "tn"), matmul("kv_down_pe_dw", hk, dk_pe, "tn")[:, :ROPE]], 1)
    dh1, d_kv_norm, grads["b_norm"] = row_vjp_call(
        "b_norms_bwd", lambda x_, g1, g2: _f_rms2(x_, g1, g2) + (x_,),
        [Arg(h1, diff=True), Arg(kv_norm, "par", diff=True), Arg(b_norm, "par", diff=True)], [Arg(d_hk), Arg(d_hb), Arg(dh2)], tr)
    grads["kv_norm"] = d_kv_norm[0]

    d_og_a = matmul("a_out_dx", dh1, a_w_out, "nt")
    grads["a_w_out"] = matmul("a_out_dw", og_a, dh1, "tn")
    d_o_a, d_gate_a, grads["a_o_gain"] = row_vjp_call(
        "a_out_gate_bwd", _f_out_gate, og_args, [Arg(d_og_a, bc=HEAD, ph=True)], tr, nh=N_HEADS)
    dqkv_a, d_ba, d_alog, d_dtb, *received = delta_bwd(qkv_a, z_ba, alog, dtb, states, t_invs, d_o_a, lp,
                                                        scatter=deferred.scatter_bufs(grads) if deferred else ())
    grads["a_log"], grads["a_dt_bias"] = d_alog[:, :N_HEADS], d_dtb[:, :N_HEADS]
    dz_qkv, d_conv = conv_bwd(z_qkv, a_conv, dqkv_a, lp)
    grads["a_conv"] = d_conv.T
    d_hn = matmul("a_in_qkv_dx", dz_qkv, w_qkv_t, "nn")
    d_hn = matmul("a_in_gate_dx", d_gate_a, w_ga_t, "nn", res=d_hn)
    d_hn = matmul("a_in_ba_dx", d_ba, w_ba_t, "nn", res=d_hn)
    grads["a_w_in"] = jnp.concatenate([matmul("a_in_qkv_dw", dz_qkv, hn, "tn"), matmul("a_in_gate_dw", d_gate_a, hn, "tn"),
                                       matmul("a_in_ba_dw", d_ba, hn, "tn")[:2 * N_HEADS]], 0)
    dh0, grads["a_norm"] = row_vjp_call("a_norm_bwd", lambda x_, g_: _f_rms(x_, g_) + (x_,),
                                        [Arg(h0, diff=True), Arg(a_norm, "par", diff=True)], [Arg(d_hn), Arg(dh1)], tr)
    dh0 = dh0.reshape(nb, lp, d)
    grads["meta_tokens"] = meta_grad(dh0).T
    return loss, dh0[:, LEAD:], grads, received


_SHARDED = (
    ("meta_tokens", True, False), ("a_norm", True, False), ("a_w_in", True, True), ("a_conv", True, False), ("a_w_out", False, True),
    ("kv_w_down", False, True), ("kv_w_uk", True, True), ("kv_w_uv", True, True), ("b_w_in", True, True), ("b_w_uq", True, True),
    ("b_w_out", False, True))
_REPLICATED = ("a_log", "a_dt_bias", "a_o_gain", "kv_norm", "kv_latent_norm", "k_gain", "b_norm", "b_q_latent_norm", "b_q_gain")
_ALL_WEIGHTS = ("meta_tokens", "a_norm", "a_w_in", "a_conv", "a_log", "a_dt_bias", "a_o_gain", "a_w_out", "kv_norm", "kv_w_down",
                "kv_latent_norm", "kv_w_uk", "kv_w_uv", "k_gain", "b_norm", "b_w_in", "b_q_latent_norm", "b_w_uq", "b_q_gain", "b_w_out")


def _round_up(n, m):
    return (n + m - 1) // m * m


def _pack_rows(pieces, row_multiple):
    padded = []
    for p in pieces:
        n = p.shape[-1]
        padded.append(jnp.pad(p, [(0, 0)] * (p.ndim - 1) + [(0, _round_up(n, PACK_COLS) - n)]))
    flat = jnp.concatenate(padded, -1)
    rows = _round_up(flat.shape[-1] // PACK_COLS, row_multiple)
    flat = jnp.pad(flat, [(0, 0)] * (flat.ndim - 1) + [(0, rows * PACK_COLS - flat.shape[-1])])
    return flat.reshape(flat.shape[:-1] + (rows, PACK_COLS))


def _unpack_rows(buf, sizes):
    flat = buf.reshape(buf.shape[:-2] + (-1,))
    out, off = [], 0
    for n in sizes:
        out.append(flat[..., off:off + n])
        off += _round_up(n, PACK_COLS)
    return out


def _shard_2d(a):
    return a.reshape(a.shape[-2:]) if a.ndim > 2 else a


def _kl_shard(a, by_cols):
    return _shard_2d(a).T if by_cols else _shard_2d(a)


_GROUPS_FIRST = (("a_w_in", "a_w_out"),)
_GROUPS_LATER = (("b_w_in", "b_w_out"), ("b_w_uq",), ("kv_w_down",), ("kv_w_uk", "kv_w_uv"))
_SMALL_SHARDED = ("meta_tokens", "a_norm", "a_conv")
_BY_COLS = {name: by_cols for name, by_cols, _ in _SHARDED}
ROW_ALIGN = 16


def _stack_rows(pieces):
    padded, starts, row = [], [], 0
    for p in pieces:
        r = p.shape[-2]
        padded.append(jnp.pad(p, [(0, 0)] * (p.ndim - 2) + [(0, _round_up(r, ROW_ALIGN) - r), (0, 0)]))
        starts.append(row)
        row += _round_up(r, ROW_ALIGN)
    return jnp.concatenate(padded, -2), starts


def _stack_group(arrays_by_name, names):
    arrays = [arrays_by_name[n].astype(BF16) for n in names]
    buf, starts = _stack_rows(arrays)
    return buf, [(n, s, a.shape[-2]) for n, s, a in zip(names, starts, arrays, strict=True)]


def _stack_groups(arrays_by_name, groups):
    stacked = [_stack_group(arrays_by_name, names) for names in groups]
    return [b for b, _ in stacked], [entries for _, entries in stacked]


def _full_from_gathered(gathered, layout):
    full = {}
    for got, entries in zip(gathered, layout, strict=True):
        for name, start, rows in entries:
            full[name] = got[:, start:start + rows].reshape(N_DEV * rows, got.shape[-1])
    return full


def gather_first_weights(local):
    shards = {n: _kl_shard(local[n], _BY_COLS[n]) for names in _GROUPS_FIRST for n in names}
    bufs, layout = _stack_groups(shards, _GROUPS_FIRST)
    small = [_kl_shard(local[n], _BY_COLS[n]) for n in _SMALL_SHARDED]
    bufs.append(_pack_rows([s.reshape(-1) for s in small], 8))
    gathered = _exchange("all_gather", bufs, scatter=False)
    full = _full_from_gathered(gathered[:-1], layout)
    for name, part, sh in zip(_SMALL_SHARDED, _unpack_rows(gathered[-1], [s.size for s in small]), small, strict=True):
        full[name] = part.reshape(N_DEV * sh.shape[0], sh.shape[1])
    full["a_norm"] = full["a_norm"].reshape(1, -1)
    return full


class LaterExchanges:
    def __init__(self, local):
        shards = {n: _kl_shard(local[n], _BY_COLS[n]) for names in _GROUPS_LATER for n in names}
        self.gather_bufs, self.layout = _stack_groups(shards, _GROUPS_LATER)

    def finish(self, gathered):
        return _full_from_gathered(gathered, self.layout)

    def scatter_bufs(self, grads):
        return _stack_groups(_owner_slices(grads, _GROUPS_LATER), _GROUPS_LATER)[0]


def _owner_slices(grads, groups):
    return {n: grads[n].reshape(N_DEV, -1, grads[n].shape[-1]) for names in groups for n in names}


def reduce_contributions(name, recv):
    _, r, c = recv.shape
    tr = _pick(r, (256, 128, 64, 32, 16, 8))

    def body(g_ref, o_ref):
        g = g_ref[0].astype(F32)
        for dev in range(1, N_DEV):
            g = g + g_ref[dev].astype(F32)
        o_ref[...] = g

    return pl.pallas_call(
        body, grid=(r // tr,), in_specs=[pl.BlockSpec((N_DEV, tr, c), lambda i: (0, i, 0))], out_specs=pl.BlockSpec((tr, c), lambda i: (i, 0)),
        out_shape=jax.ShapeDtypeStruct((r, c), F32), compiler_params=_cparams(("arbitrary",)), name=name)(recv)


def adamw_all(gs, ws, ms, vs):
    n = len(gs)

    def body(*refs):
        for i in range(n):
            g_ref, w_ref, m_ref, v_ref = (refs[j * n + i] for j in range(4))
            d_ref, mo_ref, vo_ref = (refs[(4 + j) * n + i] for j in range(3))
            g = g_ref[...]
            m_new = ADAM_B1 * m_ref[...] + (1.0 - ADAM_B1) * g
            v_new = ADAM_B2 * v_ref[...] + (1.0 - ADAM_B2) * (g * g)
            m_hat = m_new / (1.0 - ADAM_B1 ** ADAM_STEP)
            v_hat = v_new / (1.0 - ADAM_B2 ** ADAM_STEP)
            d_ref[...] = -ADAM_LR * (m_hat / (jnp.sqrt(v_hat) + ADAM_EPS) + ADAM_WD * w_ref[...])
            mo_ref[...] = m_new
            vo_ref[...] = v_new

    out = [jax.ShapeDtypeStruct(g.shape, F32) for g in gs] * 3
    res = pl.pallas_call(body, out_shape=out, compiler_params=pltpu.CompilerParams(vmem_limit_bytes=VMEM_LIMIT), name="adamw_all")(*gs, *ws, *ms, *vs)
    return res[:n], res[n:2 * n], res[2 * n:]


def kernel(x, meta_tokens, a_norm, a_w_in, a_conv, a_log, a_dt_bias, a_o_gain, a_w_out, kv_norm, kv_w_down, kv_latent_norm, kv_w_uk, kv_w_uv, k_gain, b_norm, b_w_in, b_q_latent_norm, b_w_uq, b_q_gain, b_w_out, loss_target, m_meta_tokens, m_a_norm, m_a_w_in, m_a_conv, m_a_log, m_a_dt_bias, m_a_o_gain, m_a_w_out, m_kv_norm, m_kv_w_down, m_kv_latent_norm, m_kv_w_uk, m_kv_w_uv, m_k_gain, m_b_norm, m_b_w_in, m_b_q_latent_norm, m_b_w_uq, m_b_q_gain, m_b_w_out, v_meta_tokens, v_a_norm, v_a_w_in, v_a_conv, v_a_log, v_a_dt_bias, v_a_o_gain, v_a_w_out, v_kv_norm, v_kv_w_down, v_kv_latent_norm, v_kv_w_uk, v_kv_w_uv, v_k_gain, v_b_norm, v_b_w_in, v_b_q_latent_norm, v_b_w_uq, v_b_q_gain, v_b_w_out):
    given = dict(locals())
    local_w = {n: given[n] for n in _ALL_WEIGHTS}
    full = gather_first_weights(local_w)
    for n in _REPLICATED:
        full[n] = local_w[n]
    later = LaterExchanges(local_w)

    loss_part, grad_x, grads, received_later = local_step(x, loss_target, full, later)

    bufs, layout = _stack_groups(_owner_slices(grads, _GROUPS_FIRST), _GROUPS_FIRST)
    exact = [grads[n].reshape(N_DEV, -1) for n in _SMALL_SHARDED]
    exact += [jnp.broadcast_to(grads[n].reshape(1, -1), (N_DEV, grads[n].size)) for n in _REPLICATED]
    exact.append(jnp.broadcast_to(loss_part, (N_DEV, 1)))
    bufs.append(_pack_rows(exact, 8))
    received = list(received_later) + list(_exchange("all_to_all", bufs, scatter=True))
    layout = later.layout + layout
    summed = [reduce_contributions(f"reduce_{i}", r) for i, r in enumerate(received)]

    grad_kl = {}
    for got, entries in zip(summed, layout):
        for n, start, rows in entries:
            grad_kl[n] = got[start:start + rows]
    parts = _unpack_rows(summed[-1], [p.shape[1] for p in exact])
    for n, part in zip(_SMALL_SHARDED + _REPLICATED, parts, strict=False):
        grad_kl[n] = part
    loss = parts[-1][0]

    def natural_2d(n, a):
        shape = _shard_2d(local_w[n]).shape if local_w[n].ndim > 1 else (1, local_w[n].size)
        return a.reshape(shape[::-1]).T if _BY_COLS.get(n, False) else a.reshape(shape)

    as_2d = lambda n, a: a.reshape(natural_2d(n, grad_kl[n]).shape)
    gs = [natural_2d(n, grad_kl[n]) for n in _ALL_WEIGHTS]
    deltas, new_m, new_v = adamw_all(gs, [as_2d(n, local_w[n]) for n in _ALL_WEIGHTS], [as_2d(n, given["m_" + n]) for n in _ALL_WEIGHTS],
                                     [as_2d(n, given["v_" + n]) for n in _ALL_WEIGHTS])
    results = [a.reshape(local_w[n].shape) for group in (gs, deltas, new_m, new_v) for n, a in zip(_ALL_WEIGHTS, group, strict=True)]
    return (loss, grad_x, *results)
```

```python
import dataclasses
import functools
import math

import jax
import jax.numpy as jnp
from jax import lax
from jax.experimental import pallas as pl
from jax.experimental.pallas import tpu as pltpu

F32 = jnp.float32
BF16 = jnp.bfloat16
_MXU_DTYPE = jnp.bfloat16

N_DEV = 8
D_MODEL = 1024
N_HEADS = 8
HEAD = 128
CHUNK = 64
N_META = 16
PAD_ROWS = 2 * CHUNK - N_META
LEAD = PAD_ROWS + N_META
ROPE = 64
QK_DIM = HEAD + ROPE
QK_PAD = 2 * HEAD
KV_RANK = 256
Q_RANK = 384
CONV_K = 4
EPS = 1e-6
NEG = -1e30
ROPE_THETA = 10000.0
ADAM_LR, ADAM_B1, ADAM_B2, ADAM_EPS, ADAM_WD, ADAM_STEP = 0.001, 0.9, 0.999, 1e-08, 0.01, 10
PACK_COLS = 512
VMEM_LIMIT = 56 * 1024 * 1024


def _pick(n, options):
    for o in options:
        if n % o == 0:
            return o
    raise ValueError(f"no tile for {n} among {options}")


def _cparams(sem):
    return pltpu.CompilerParams(dimension_semantics=sem, vmem_limit_bytes=VMEM_LIMIT)


def _dims(a, dims):
    if a.ndim == 2:
        return (dims, ((), ()))
    (ca,), (cb,) = dims
    return (((ca + 1,), (cb + 1,)), ((0,), (0,)))


def _dot(a, b, dims):
    return lax.dot_general(a.astype(_MXU_DTYPE), b.astype(_MXU_DTYPE), _dims(a, dims), preferred_element_type=F32)


@jax.custom_vjp
def mm_nn(a, b):
    return _dot(a, b, ((1,), (0,)))


@jax.custom_vjp
def mm_nt(a, b):
    return _dot(a, b, ((1,), (1,)))


@jax.custom_vjp
def mm_tn(a, b):
    return _dot(a, b, ((0,), (0,)))


mm_nn.defvjp(lambda a, b: (mm_nn(a, b), (a, b)), lambda r, g: (mm_nt(g, r[1]), mm_tn(r[0], g)))
mm_nt.defvjp(lambda a, b: (mm_nt(a, b), (a, b)), lambda r, g: (mm_nn(g, r[1]), mm_tn(g, r[0])))
mm_tn.defvjp(lambda a, b: (mm_tn(a, b), (a, b)), lambda r, g: (mm_nt(r[1], g), mm_nn(r[0], g)))


def _split_terms(x, n):
    terms, rest = [], x
    for _ in range(n):
        t = rest.astype(_MXU_DTYPE)
        terms.append(t)
        rest = rest - t.astype(F32)
    return terms


def _dot_01_raw(m, x, dims):
    m = m.astype(_MXU_DTYPE)
    return sum(lax.dot_general(m, t, _dims(m, dims), preferred_element_type=F32) for t in _split_terms(x, 3))


@jax.custom_vjp
def _dot_01(m, x):
    return _dot_01_raw(m, x, ((1,), (0,)))


_dot_01.defvjp(lambda m, x: (_dot_01(m, x), m), lambda m, g: (jnp.zeros_like(m), _dot_01_raw(m, g, ((0,), (0,)))))


def _inv_unit_lower(a):
    n = a.shape[-1]
    eye = (lax.broadcasted_iota(jnp.int32, (n, n), 0) == lax.broadcasted_iota(jnp.int32, (n, n), 1)).astype(F32)
    d = lambda u, w: lax.dot_general(u, w, _dims(u, ((1,), (0,))), preferred_element_type=F32)
    t = eye - a
    p = a.astype(_MXU_DTYPE)
    p = d(p, p)
    squarings = int(math.log2(n)) - 1
    for s in range(squarings):
        ph = p.astype(_MXU_DTYPE)
        t_hi, t_lo = _split_terms(t, 2)
        t = t + (d(t_hi, ph) + d(t_lo, ph))
        if s + 1 < squarings:
            p = d(ph, ph)
    return t


@jax.custom_vjp
def _inv_lookup(a, t):
    return t


def _inv_lookup_bwd(t, g):
    return -mm_tn(t, mm_nt(g, t)), jnp.zeros_like(t)


_inv_lookup.defvjp(lambda a, t: (t, t), _inv_lookup_bwd)


def _sigmoid(x):
    return 1.0 / (1.0 + jnp.exp(-x))


def _silu(x):
    return x * _sigmoid(x)


def _softplus(x):
    return jnp.where(x > 20.0, x, jnp.log(1.0 + jnp.exp(jnp.minimum(x, 20.0))))


def _rms(x, g, width=None):
    ms = jnp.sum(x * x, -1, keepdims=True) / (x.shape[-1] if width is None else width)
    return x * lax.rsqrt(ms + EPS) * g


MM_VMEM_BUDGET = 40 * 1024 * 1024


def _matmul_rows(name, a, b, mode, out_dtype, res, scatter):
    m, k = a.shape
    n = b.shape[1] if mode == "nn" else b.shape[0]
    dims = {"nn": ((1,), (0,)), "nt": ((1,), (1,))}[mode]
    out_bytes = jnp.dtype(out_dtype).itemsize
    n_in, nx = 2 + (res is not None), len(scatter)

    def vmem(tm):
        blocks = 2 * tm * k * a.dtype.itemsize + 2 * k * n * b.dtype.itemsize + 2 * tm * n * out_bytes + tm * n * 4
        return blocks + (2 * tm * n * res.dtype.itemsize if res is not None else 0)

    tm = next(c for c in (2176, 1088, 512, 256, 128, 64) if m % c == 0 and vmem(c) <= MM_VMEM_BUDGET)
    steps = m // tm

    def body(*refs):
        a_ref, b_ref, o_ref = refs[0], refs[1], refs[n_in + nx]
        i = pl.program_id(0)
        finish = _ride(scatter, True, refs[n_in:n_in + nx], refs[n_in + nx + 1:n_in + 2 * nx + 1], refs[n_in + 2 * nx + 1:], i == 0, i == steps - 1)
        out = _dot(a_ref[...], b_ref[...], dims)
        if res is not None:
            out = out + refs[2][...].astype(F32)
        o_ref[...] = out.astype(o_ref.dtype)
        finish()

    o_spec = pl.BlockSpec((tm, n), lambda i: (i, 0))
    in_specs = [pl.BlockSpec((tm, k), lambda i: (i, 0)), pl.BlockSpec(b.shape, lambda i: (0, 0))] + ([o_spec] if res is not None else [])
    args = (a, b) + ((res,) if res is not None else ())
    out = pl.pallas_call(
        body, grid=(steps,), in_specs=in_specs + [_HBM] * nx, out_specs=[o_spec] + [_HBM] * nx,
        out_shape=[jax.ShapeDtypeStruct((m, n), out_dtype)] + Exchange.out_shape(scatter, True), scratch_shapes=Exchange.scratch(nx) if nx else [],
        compiler_params=_cparams(("arbitrary",) if nx else ("parallel",)), name=name)(*args, *scatter)
    return out if nx else out[0]


def matmul(name, a, b, mode, out_dtype=F32, res=None, scatter=()):
    if mode != "tn":
        return _matmul_rows(name, a, b, mode, out_dtype, res, scatter)
    (k, m), (k2, n) = a.shape, b.shape
    assert k == k2 and res is None, (name, a.shape, b.shape, mode)
    tm = _pick(m, (1024, 512, 384, 256, 128))
    tn = _pick(n, (1024, 512, 384, 256, 128))
    tk = _pick(k, (512, 256, 128))
    nk = k // tk
    dims = ((0,), (0,))

    def body(*refs):
        if res is None:
            a_ref, b_ref, o_ref, acc_ref = refs
        else:
            a_ref, b_ref, r_ref, o_ref, acc_ref = refs
        kk = pl.program_id(2)

        @pl.when(kk == 0)
        def _():
            acc_ref[...] = jnp.zeros_like(acc_ref)

        acc_ref[...] += _dot(a_ref[...], b_ref[...], dims)

        @pl.when(kk == nk - 1)
        def _():
            out = acc_ref[...]
            if res is not None:
                out = out + r_ref[...].astype(F32)
            o_ref[...] = out.astype(o_ref.dtype)

    a_spec = pl.BlockSpec((tk, tm), lambda i, j, kk: (kk, i)) if mode == "tn" else pl.BlockSpec((tm, tk), lambda i, j, kk: (i, kk))
    b_spec = pl.BlockSpec((tn, tk), lambda i, j, kk: (j, kk)) if mode == "nt" else pl.BlockSpec((tk, tn), lambda i, j, kk: (kk, j))
    o_spec = pl.BlockSpec((tm, tn), lambda i, j, kk: (i, j))
    in_specs = [a_spec, b_spec] + ([o_spec] if res is not None else [])
    args = (a, b) + ((res,) if res is not None else ())
    return pl.pallas_call(
        body, grid=(m // tm, n // tn, nk), in_specs=in_specs, out_specs=o_spec,
        out_shape=jax.ShapeDtypeStruct((m, n), out_dtype), scratch_shapes=[pltpu.VMEM((tm, tn), F32)],
        compiler_params=_cparams(("parallel", "parallel", "arbitrary")), name=name)(*args)


@dataclasses.dataclass
class Arg:
    arr: jax.Array
    kind: str = "row"
    bc: int = 0
    base: int = 0
    ph: bool = False
    diff: bool = False
    gdt: object = F32


def _arg_spec(a, tr, nh, ntab, base=None):
    bc = a.bc or a.arr.shape[1]
    base = a.base if base is None else base
    width = bc * nh if a.ph else bc
    col = base // nh if a.ph else base
    assert not a.ph or base % nh == 0
    if a.kind == "row":
        return pl.BlockSpec((tr, width), lambda i: (i, col))
    if a.kind == "tab":
        return pl.BlockSpec((tr, width), lambda i: (i % ntab, col))
    return pl.BlockSpec((a.arr.shape[0], width), lambda i: (0, col))


def _head_view(ref, a, h):
    bc = a.bc or a.arr.shape[1]
    v = ref[:, h * bc:(h + 1) * bc] if a.ph else ref[...]
    return v.astype(F32) if jnp.issubdtype(v.dtype, jnp.floating) else v


def row_call(name, fn, args, outs, tr, nh=1, ntab=1):
    t = args[0].arr.shape[0]
    n_in = len(args)
    out_args = [Arg(None, "row", bc, 0, ph) for (_, _, bc, ph) in outs]

    def body(*refs):
        for h in range(nh):
            res = fn(*[_head_view(r, a, h) for r, a in zip(refs[:n_in], args, strict=True)])
            for r, a, v in zip(refs[n_in:], out_args, res, strict=True):
                if a.ph:
                    r[:, h * a.bc:(h + 1) * a.bc] = v.astype(r.dtype)
                elif h == nh - 1:
                    r[...] = v.astype(r.dtype)

    return pl.pallas_call(
        body, grid=(t // tr,), in_specs=[_arg_spec(a, tr, nh, ntab) for a in args], out_specs=[_arg_spec(a, tr, nh, ntab) for a in out_args],
        out_shape=[jax.ShapeDtypeStruct((t, cols), dt) for (cols, dt, _, _) in outs],
        compiler_params=_cparams(("arbitrary",)), name=name)(*[a.arr for a in args])


def row_vjp_call(name, fn, args, cts, tr, nh=1, ntab=1):
    t = args[0].arr.shape[0]
    n_in, n_ct = len(args), len(cts)
    diff_idx = [k for k, a in enumerate(args) if a.diff]

    def body(*refs):
        out_refs = refs[n_in + n_ct:]
        shared = [None] * len(diff_idx)
        for k, r in zip(diff_idx, out_refs, strict=True):
            if args[k].kind == "par":
                @pl.when(pl.program_id(0) == 0)
                def _(r=r):
                    r[...] = jnp.zeros_like(r)

        for h in range(nh):
            vals = [_head_view(r, a, h) for r, a in zip(refs[:n_in], args, strict=True)]
            ct_vals = tuple(_head_view(r, a, h) for r, a in zip(refs[n_in:n_in + n_ct], cts, strict=True))

            def f(*dv, vals=vals):
                full = list(vals)
                for k, v in zip(diff_idx, dv, strict=True):
                    full[k] = v
                return tuple(fn(*full))

            _, vjp = jax.vjp(f, *[vals[k] for k in diff_idx])
            for j, (k, r, g) in enumerate(zip(diff_idx, out_refs, vjp(ct_vals), strict=True)):
                a = args[k]
                bc = a.bc or a.arr.shape[1]
                if not a.ph:
                    shared[j] = g if shared[j] is None else shared[j] + g
                elif a.kind == "row":
                    r[:, h * bc:(h + 1) * bc] = g.astype(r.dtype)
                else:
                    r[:, h * bc:(h + 1) * bc] += g
        for j, (k, r) in enumerate(zip(diff_idx, out_refs, strict=True)):
            if not args[k].ph:
                if args[k].kind == "row":
                    r[...] = shared[j].astype(r.dtype)
                else:
                    r[...] += shared[j]

    out_specs, out_shape = [], []
    for k in diff_idx:
        a = args[k]
        bc = a.bc or a.arr.shape[1]
        out_specs.append(_arg_spec(a, tr, nh, ntab, base=0))
        out_shape.append(jax.ShapeDtypeStruct((t if a.kind == "row" else a.arr.shape[0], bc * (nh if a.ph else 1)), a.gdt if a.kind == "row" else F32))
    in_specs = [_arg_spec(a, tr, nh, ntab) for a in list(args) + list(cts)]
    return pl.pallas_call(
        body, grid=(t // tr,), in_specs=in_specs, out_specs=out_specs, out_shape=out_shape,
        compiler_params=_cparams(("arbitrary",)), name=name)(*[a.arr for a in list(args) + list(cts)])


def _conv_taps(x, w):
    rows = lax.broadcasted_iota(jnp.int32, x.shape, 0)
    y = x * w[CONV_K - 1:CONV_K, :]
    shifted = []
    for s in range(1, CONV_K):
        xs = jnp.where(rows >= s, pltpu.roll(x, s, 0), 0.0)
        shifted.append(xs)
        y = y + xs * w[CONV_K - 1 - s:CONV_K - s, :]
    return y, shifted


CONV_HEADS = 4
CONV_BLOCKS_PER_THIRD = N_HEADS // CONV_HEADS


def _conv_post(y, block):
    a = _silu(y)
    normed = block < 2 * CONV_BLOCKS_PER_THIRD
    scale = jnp.where(block < CONV_BLOCKS_PER_THIRD, HEAD ** -0.5, 1.0)
    return a * jnp.where(normed, lax.rsqrt(jnp.sum(a * a, -1, keepdims=True) + EPS) * scale, 1.0)


def conv_fwd(z, w, lp):
    t, width = z.shape
    cols = CONV_HEADS * HEAD

    def body(z_ref, w_ref, o_ref):
        block = pl.program_id(1)
        for h in range(CONV_HEADS):
            cs = slice(h * HEAD, (h + 1) * HEAD)
            y, _ = _conv_taps(z_ref[:, cs], w_ref[:, cs])
            o_ref[:, cs] = _conv_post(y, block)

    return pl.pallas_call(
        body, grid=(t // lp, width // cols),
        in_specs=[pl.BlockSpec((lp, cols), lambda b, j: (b, j)), pl.BlockSpec((CONV_K, cols), lambda b, j: (0, j))],
        out_specs=pl.BlockSpec((lp, cols), lambda b, j: (b, j)), out_shape=jax.ShapeDtypeStruct((t, width), F32),
        compiler_params=_cparams(("arbitrary", "arbitrary")), name="a_conv_fwd")(z, w)


def conv_bwd(z, w, dout, lp):
    t, width = z.shape
    cols = CONV_HEADS * HEAD

    def body(z_ref, w_ref, g_ref, dz_ref, dw_ref):
        block = pl.program_id(0)

        @pl.when(pl.program_id(1) == 0)
        def _():
            dw_ref[...] = jnp.zeros_like(dw_ref)

        for h in range(CONV_HEADS):
            cs = slice(h * HEAD, (h + 1) * HEAD)
            x, wv = z_ref[:, cs], w_ref[:, cs]
            y, shifted = _conv_taps(x, wv)
            _, vjp = jax.vjp(lambda y_: _conv_post(y_, block), y)
            (dy,) = vjp(g_ref[:, cs])
            rows = lax.broadcasted_iota(jnp.int32, x.shape, 0)
            dx = dy * wv[CONV_K - 1:CONV_K, :]
            for s in range(1, CONV_K):
                dx = dx + jnp.where(rows < lp - s, pltpu.roll(dy, lp - s, 0), 0.0) * wv[CONV_K - 1 - s:CONV_K - s, :]
            dz_ref[:, cs] = dx.astype(dz_ref.dtype)
            dw_ref[CONV_K - 1:CONV_K, cs] += jnp.sum(dy * x, axis=0, keepdims=True)
            for s in range(1, CONV_K):
                dw_ref[CONV_K - 1 - s:CONV_K - s, cs] += jnp.sum(dy * shifted[s - 1], axis=0, keepdims=True)

    blk = pl.BlockSpec((lp, cols), lambda j, b: (b, j))
    w_blk = pl.BlockSpec((CONV_K, cols), lambda j, b: (0, j))
    return pl.pallas_call(
        body, grid=(width // cols, t // lp), in_specs=[blk, w_blk, blk], out_specs=[blk, w_blk],
        out_shape=[jax.ShapeDtypeStruct((t, width), _MXU_DTYPE), jax.ShapeDtypeStruct((CONV_K, width), F32)],
        compiler_params=_cparams(("arbitrary", "arbitrary")), name="a_conv_bwd")(z, w, dout)


def _delta_chunk(q, k, v, ba, alog, dtb, state, t_stored, h0):
    n_g, c = q.shape[0], q.shape[1]
    lane = lax.broadcasted_iota(jnp.int32, (1, HEAD), 1)

    def pick(x, offset):
        return jnp.concatenate([jnp.sum(x * (lane == offset + h0 + g).astype(F32), axis=1, keepdims=True)[None] for g in range(n_g)], 0)

    b_raw, a_raw = pick(ba, 0), pick(ba, N_HEADS)
    a_log, dt_bias = pick(alog, 0), pick(dtb, 0)
    beta = _sigmoid(b_raw)
    g = -jnp.exp(a_log) * _softplus(a_raw + dt_bias)
    ri = lax.broadcasted_iota(jnp.int32, (c, c), 0)
    ci = lax.broadcasted_iota(jnp.int32, (c, c), 1)
    tril = ci <= ri
    lower = jnp.broadcast_to(tril.astype(F32), (n_g, c, c))
    gc_col = _dot_01(lower, g * jnp.ones((1, 1, HEAD), F32))[:, :, :1]
    gc_row = _dot_01(jnp.ones((n_g, 8, c), F32), g * (ri <= ci).astype(F32)[None])[:, 0:1, :]
    gc_last = jnp.sum(g, axis=1, keepdims=True)
    decay = jnp.exp(jnp.where(tril, gc_col - gc_row, NEG))
    e_gc = jnp.exp(gc_col)
    kb = k * beta
    a_mat = jnp.where(ci < ri, mm_nt(kb, k) * decay, 0.0)
    t_inv = _inv_unit_lower(a_mat) if t_stored is None else _inv_lookup(a_mat, t_stored)
    u_base = mm_nn(t_inv, v * beta)
    w_dec = mm_nn(t_inv, kb * e_gc)
    attn = jnp.where(tril, mm_nt(q, k) * decay, 0.0)
    u = u_base - mm_nn(w_dec, state)
    o = mm_nn(q * e_gc, state) + mm_nn(attn, u)
    new_state = state * jnp.exp(gc_last) + mm_tn(k * jnp.exp(gc_last - gc_col), u)
    return o, new_state, t_inv


DELTA_CHUNKS_FWD = 2
DELTA_CHUNKS_BWD = 2


def _qkv_heads(ref, rs, part):
    return jnp.stack([ref[rs, (part * N_HEADS + g) * HEAD:(part * N_HEADS + g + 1) * HEAD] for g in range(N_HEADS)])


def _ride(bufs, scatter, refs_in, refs_out, sems, first, last):
    if not bufs:
        return lambda: None

    @pl.when(first)
    def _():
        Exchange(refs_in, refs_out, *sems, scatter).start()

    def finish():
        @pl.when(last)
        def _():
            Exchange(refs_in, refs_out, *sems, scatter).wait()

    return finish


def delta_fwd(qkv, ba, alog, dtb, lp, gather=()):
    t = qkv.shape[0]
    nb, nc = t // lp, lp // CHUNK
    cps = DELTA_CHUNKS_FWD
    ng, rows = nc // cps, cps * CHUNK
    nx = len(gather)
    assert nc % cps == 0

    def body(*refs):
        qkv_ref, ba_ref, al_ref, dt_ref = refs[:4]
        o_ref, s_ref, t_ref = refs[4 + nx:7 + nx]
        state_ref = refs[7 + 2 * nx]
        b, n = pl.program_id(0), pl.program_id(1)
        finish = _ride(gather, False, refs[4:4 + nx], refs[7 + nx:7 + 2 * nx], refs[8 + 2 * nx:], (b == 0) & (n == 0), (b == nb - 1) & (n == ng - 1))

        @pl.when(n == 0)
        def _():
            state_ref[...] = jnp.zeros_like(state_ref)

        al, dtv = al_ref[...], dt_ref[...]
        for c in range(cps):
            rs = slice(c * CHUNK, (c + 1) * CHUNK)
            state = state_ref[...]
            o, new_state, t_inv = _delta_chunk(_qkv_heads(qkv_ref, rs, 0), _qkv_heads(qkv_ref, rs, 1), _qkv_heads(qkv_ref, rs, 2),
                                               ba_ref[rs, :], al, dtv, state, None, 0)
            for g in range(N_HEADS):
                o_ref[rs, g * HEAD:(g + 1) * HEAD] = o[g]
                s_ref[g, c] = state[g]
                t_ref[g, c] = t_inv[g]
            state_ref[...] = new_state
        finish()

    rows_of = lambda width: pl.BlockSpec((rows, width), lambda b, n: (b * ng + n, 0))
    par_spec = pl.BlockSpec((1, HEAD), lambda b, n: (0, 0))
    return pl.pallas_call(
        body, grid=(nb, ng), in_specs=[rows_of(3 * N_HEADS * HEAD), rows_of(HEAD), par_spec, par_spec] + [_HBM] * nx,
        out_specs=[rows_of(N_HEADS * HEAD), pl.BlockSpec((None, N_HEADS, cps, HEAD, HEAD), lambda b, n: (b, 0, n, 0, 0)),
                   pl.BlockSpec((None, N_HEADS, cps, CHUNK, CHUNK), lambda b, n: (b, 0, n, 0, 0))] + [_HBM] * nx,
        out_shape=[jax.ShapeDtypeStruct((t, N_HEADS * HEAD), F32), jax.ShapeDtypeStruct((nb, N_HEADS, nc, HEAD, HEAD), F32),
                   jax.ShapeDtypeStruct((nb, N_HEADS, nc, CHUNK, CHUNK), F32)] + Exchange.out_shape(gather, False),
        scratch_shapes=[pltpu.VMEM((N_HEADS, HEAD, HEAD), F32)] + (Exchange.scratch(nx) if nx else []),
        compiler_params=_cparams(("arbitrary", "arbitrary")), name="delta_fwd")(qkv, ba, alog, dtb, *gather)


def delta_bwd(qkv, ba, alog, dtb, states, t_invs, do, lp, scatter=()):
    t = qkv.shape[0]
    nb, nc = t // lp, lp // CHUNK
    cps = DELTA_CHUNKS_BWD
    ng, rows = nc // cps, cps * CHUNK
    nx = len(scatter)

    def body(*refs):
        qkv_ref, ba_ref, al_ref, dt_ref, s_ref, t_ref, do_ref = refs[:7]
        dqkv_ref, dba_ref, dal_ref, ddt_ref = refs[7 + nx:11 + nx]
        dstate_ref = refs[11 + 2 * nx]
        b, step = pl.program_id(0), pl.program_id(1)
        finish = _ride(scatter, True, refs[7:7 + nx], refs[11 + nx:11 + 2 * nx], refs[12 + 2 * nx:], (b == 0) & (step == 0),
                       (b == nb - 1) & (step == ng - 1))

        @pl.when(step == 0)
        def _():
            dstate_ref[...] = jnp.zeros_like(dstate_ref)

        @pl.when((b == 0) & (step == 0))
        def _():
            dal_ref[...] = jnp.zeros_like(dal_ref)
            ddt_ref[...] = jnp.zeros_like(ddt_ref)

        al, dtv = al_ref[...], dt_ref[...]
        d_al = jnp.zeros((1, HEAD), F32)
        d_dt = jnp.zeros((1, HEAD), F32)
        for c in reversed(range(cps)):
            rs = slice(c * CHUNK, (c + 1) * CHUNK)
            t_n = jnp.stack([t_ref[g, c] for g in range(N_HEADS)])
            s_n = jnp.stack([s_ref[g, c] for g in range(N_HEADS)])
            d_o = jnp.stack([do_ref[rs, g * HEAD:(g + 1) * HEAD] for g in range(N_HEADS)])

            def f(q_, k_, v_, ba_, al_, dt_, s_, t_n=t_n):
                return _delta_chunk(q_, k_, v_, ba_, al_, dt_, s_, t_n, 0)[:2]

            _, vjp = jax.vjp(f, _qkv_heads(qkv_ref, rs, 0), _qkv_heads(qkv_ref, rs, 1), _qkv_heads(qkv_ref, rs, 2), ba_ref[rs, :], al, dtv, s_n)
            grads = vjp((d_o, dstate_ref[...]))
            for part in range(3):
                for g in range(N_HEADS):
                    dqkv_ref[rs, (part * N_HEADS + g) * HEAD:(part * N_HEADS + g + 1) * HEAD] = grads[part][g]
            dba_ref[rs, :] = grads[3]
            d_al, d_dt = d_al + grads[4], d_dt + grads[5]
            dstate_ref[...] = grads[6]
        dal_ref[...] += d_al
        ddt_ref[...] += d_dt
        finish()

    rows_of = lambda width: pl.BlockSpec((rows, width), lambda b, n: (b * ng + ng - 1 - n, 0))
    par_spec = pl.BlockSpec((1, HEAD), lambda b, n: (0, 0))
    return pl.pallas_call(
        body, grid=(nb, ng),
        in_specs=[rows_of(3 * N_HEADS * HEAD), rows_of(HEAD), par_spec, par_spec,
                  pl.BlockSpec((None, N_HEADS, cps, HEAD, HEAD), lambda b, n: (b, 0, ng - 1 - n, 0, 0)),
                  pl.BlockSpec((None, N_HEADS, cps, CHUNK, CHUNK), lambda b, n: (b, 0, ng - 1 - n, 0, 0)), rows_of(N_HEADS * HEAD)] + [_HBM] * nx,
        out_specs=[rows_of(3 * N_HEADS * HEAD), rows_of(HEAD), par_spec, par_spec] + [_HBM] * nx,
        out_shape=[jax.ShapeDtypeStruct((t, 3 * N_HEADS * HEAD), F32), jax.ShapeDtypeStruct((t, HEAD), F32),
                   jax.ShapeDtypeStruct((1, HEAD), F32), jax.ShapeDtypeStruct((1, HEAD), F32)] + Exchange.out_shape(scatter, True),
        scratch_shapes=[pltpu.VMEM((N_HEADS, HEAD, HEAD), F32)] + (Exchange.scratch(nx) if nx else []),
        compiler_params=_cparams(("arbitrary", "arbitrary")), name="delta_bwd")(qkv, ba, alog, dtb, states, t_invs, do, *scatter)


ATT_Q_TILE = 256
ATT_K_TILE = 512
ATT_SCALE = QK_DIM ** -0.5


def _tiles(end, size):
    return [(s, min(s + size, end)) for s in range(0, end, size)]


def _att_visible(q0, q1, k0, k1, keys_first):
    if k1 <= q0 + CHUNK and k0 >= PAD_ROWS:
        return None
    shape = (k1 - k0, q1 - q0) if keys_first else (q1 - q0, k1 - k0)
    qpos = q0 + lax.broadcasted_iota(jnp.int32, shape, 1 if keys_first else 0)
    kpos = k0 + lax.broadcasted_iota(jnp.int32, shape, 0 if keys_first else 1)
    shift = CHUNK.bit_length() - 1
    return (jnp.right_shift(kpos, shift) <= jnp.right_shift(qpos, shift)) & (kpos >= PAD_ROWS)


def _att_seq_specs(lp):
    return pl.BlockSpec((lp, QK_PAD), lambda b, h: (b, h)), pl.BlockSpec((lp, HEAD), lambda b, h: (b, h))


def flash_fwd(q, k, v, lp):
    t = q.shape[0]
    qk_seq, o_seq = _att_seq_specs(lp)

    def body(q_ref, k_ref, v_ref, o_ref, lse_ref):
        for q0, q1 in _tiles(lp, ATT_Q_TILE):
            qb = q_ref[q0:q1, :]
            k_tiles = _tiles(q1, ATT_K_TILE)
            scores, m = [], None
            for k0, k1 in k_tiles:
                s = mm_nt(qb, k_ref[k0:k1, :]) * ATT_SCALE
                vis = _att_visible(q0, q1, k0, k1, False)
                s = s if vis is None else jnp.where(vis, s, NEG)
                scores.append(s)
                row_max = jnp.max(s, -1, keepdims=True)
                m = row_max if m is None else jnp.maximum(m, row_max)
            l = jnp.zeros((q1 - q0, 1), F32)
            acc = jnp.zeros((q1 - q0, HEAD), F32)
            for s, (k0, k1) in zip(scores, k_tiles, strict=True):
                p = jnp.exp(s - m)
                l = l + jnp.sum(p, -1, keepdims=True)
                acc = acc + mm_nn(p, v_ref[k0:k1, :])
            o_ref[q0:q1, :] = acc / l
            lse_ref[q0:q1, :] = jnp.broadcast_to(m + jnp.log(l), (q1 - q0, HEAD))

    big = jax.ShapeDtypeStruct((t, N_HEADS * HEAD), F32)
    return pl.pallas_call(
        body, grid=(t // lp, N_HEADS), in_specs=[qk_seq, qk_seq, o_seq], out_specs=[o_seq, o_seq], out_shape=[big, big],
        compiler_params=_cparams(("arbitrary", "arbitrary")), name="flash_fwd")(q, k, v)


def flash_bwd(q, k, v, o, lse, do, lp):
    t = q.shape[0]
    qk_seq, o_seq = _att_seq_specs(lp)

    def body(q_ref, k_ref, v_ref, o_ref, lse_ref, do_ref, dq_ref, dk_ref, dv_ref):
        dk_ref[...] = jnp.zeros_like(dk_ref)
        dv_ref[...] = jnp.zeros_like(dv_ref)
        for q0, q1 in _tiles(lp, ATT_Q_TILE):
            qb, dob = q_ref[q0:q1, :], do_ref[q0:q1, :]
            lse_row = jnp.transpose(lse_ref[q0:q1, :])[0:1, :]
            dsum_row = jnp.sum(jnp.transpose(dob * o_ref[q0:q1, :]), axis=0, keepdims=True)
            dq = jnp.zeros((q1 - q0, QK_PAD), F32)
            for k0, k1 in _tiles(q1, ATT_K_TILE):
                kb, vb = k_ref[k0:k1, :], v_ref[k0:k1, :]
                s = mm_nt(kb, qb) * ATT_SCALE
                vis = _att_visible(q0, q1, k0, k1, True)
                s = s if vis is None else jnp.where(vis, s, NEG)
                p = jnp.exp(s - lse_row)
                ds = p * (mm_nt(vb, dob) - dsum_row) * ATT_SCALE
                dv_ref[k0:k1, :] += mm_nn(p, dob)
                dk_ref[k0:k1, :] += mm_nn(ds, qb)
                dq = dq + mm_tn(ds, kb)
            dq_ref[q0:q1, :] = dq

    return pl.pallas_call(
        body, grid=(t // lp, N_HEADS), in_specs=[qk_seq, qk_seq, o_seq, o_seq, o_seq, o_seq], out_specs=[qk_seq, qk_seq, o_seq],
        out_shape=[jax.ShapeDtypeStruct((t, N_HEADS * QK_PAD), F32), jax.ShapeDtypeStruct((t, N_HEADS * QK_PAD), F32),
                   jax.ShapeDtypeStruct((t, N_HEADS * HEAD), F32)],
        compiler_params=_cparams(("arbitrary", "arbitrary")), name="flash_bwd")(q, k, v, o, lse, do)


def loss_head(h2, target, lp):
    nb, seq, d = target.shape
    tr = 128
    nblk = lp // tr
    lead_blocks = LEAD // tr

    def body(h_ref, t_ref, loss_ref, dh_ref, acc_ref):
        b, i = pl.program_id(0), pl.program_id(1)

        @pl.when((b == 0) & (i == 0))
        def _():
            acc_ref[...] = jnp.zeros_like(acc_ref)

        @pl.when(i < lead_blocks)
        def _():
            dh_ref[...] = jnp.zeros_like(dh_ref)

        @pl.when(i >= lead_blocks)
        def _():
            err = h_ref[...] - t_ref[...]
            dh_ref[...] = err * (1.0 / d)
            acc_ref[...] += jnp.sum(err * err, axis=0, keepdims=True)

        @pl.when((b == nb - 1) & (i == nblk - 1))
        def _():
            loss_ref[...] = jnp.sum(acc_ref[...], axis=1, keepdims=True) * (0.5 / d)

    return pl.pallas_call(
        body, grid=(nb, nblk),
        in_specs=[pl.BlockSpec((None, tr, d), lambda b, i: (b, i, 0)),
                  pl.BlockSpec((None, tr, d), lambda b, i: (b, jnp.maximum(i - lead_blocks, 0), 0))],
        out_specs=[pl.BlockSpec((1, 1), lambda b, i: (0, 0)), pl.BlockSpec((None, tr, d), lambda b, i: (b, i, 0))],
        out_shape=[jax.ShapeDtypeStruct((1, 1), F32), jax.ShapeDtypeStruct((nb, lp, d), F32)],
        scratch_shapes=[pltpu.VMEM((1, d), F32)], compiler_params=_cparams(("arbitrary", "arbitrary")), name="loss_head")(h2, target)


def meta_grad(dh0):
    nb, _, d = dh0.shape

    def body(g_ref, o_ref):
        @pl.when(pl.program_id(0) == 0)
        def _():
            o_ref[...] = jnp.zeros_like(o_ref)

        o_ref[...] += g_ref[PAD_ROWS:LEAD, :]

    return pl.pallas_call(
        body, grid=(nb,), in_specs=[pl.BlockSpec((None, LEAD, d), lambda b: (b, 0, 0))],
        out_specs=pl.BlockSpec((N_META, d), lambda b: (0, 0)), out_shape=jax.ShapeDtypeStruct((N_META, d), F32),
        compiler_params=_cparams(("arbitrary",)), name="meta_grad")(dh0)


_HBM = pl.BlockSpec(memory_space=pltpu.HBM)


def _mesh_pos():
    x, y, c = lax.axis_index("x"), lax.axis_index("y"), lax.axis_index("c")
    return x, y, c


def _peer(x, y, c, k):
    px = 1 - x if k & 4 else x
    py = 1 - y if k & 2 else y
    pc = 1 - c if k & 1 else c
    return (px, py, pc), 4 * px + 2 * py + pc


class Exchange:
    def __init__(self, x_refs, out_refs, send_sems, recv_sems, local_sems, scatter):
        self.x_refs, self.out_refs, self.scatter = x_refs, out_refs, scatter
        self.send_sems, self.recv_sems, self.local_sems = send_sems, recv_sems, local_sems
        self.pos = _mesh_pos()
        x, y, c = self.pos
        self.me = 4 * x + 2 * y + c

    @staticmethod
    def scratch(n):
        return [pltpu.SemaphoreType.DMA((n, N_DEV - 1)), pltpu.SemaphoreType.DMA((n, N_DEV - 1)), pltpu.SemaphoreType.DMA((n,))]

    @staticmethod
    def out_shape(bufs, scatter):
        return [jax.ShapeDtypeStruct(b.shape if scatter else (N_DEV,) + b.shape, b.dtype) for b in bufs]

    def _local(self, i):
        return pltpu.make_async_copy(self.x_refs[i].at[self.me] if self.scatter else self.x_refs[i], self.out_refs[i].at[self.me], self.local_sems.at[i])

    def _copy(self, i, k, landing):
        peer, peer_id = _peer(*self.pos, k)
        src = self.x_refs[i].at[peer_id] if self.scatter else self.x_refs[i]
        return pltpu.make_async_remote_copy(src_ref=src, dst_ref=self.out_refs[i].at[peer_id if landing else self.me],
                                            send_sem=self.send_sems.at[i, k - 1], recv_sem=self.recv_sems.at[i, k - 1],
                                            device_id=peer, device_id_type=pl.DeviceIdType.MESH)

    def start(self):
        for i in range(len(self.x_refs)):
            self._local(i).start()
        for k in range(1, N_DEV):
            for i in range(len(self.x_refs)):
                self._copy(i, k, False).start()

    def wait(self):
        for k in range(1, N_DEV):
            for i in range(len(self.x_refs)):
                self._copy(i, k, True).wait_recv()
        for k in range(1, N_DEV):
            for i in range(len(self.x_refs)):
                self._copy(i, k, False).wait_send()
        for i in range(len(self.x_refs)):
            self._local(i).wait()


def _exchange(name, bufs, scatter):
    n = len(bufs)

    def body(*refs):
        ex = Exchange(refs[:n], refs[n:2 * n], *refs[2 * n:], scatter)
        ex.start()
        ex.wait()

    return pl.pallas_call(body, in_specs=[_HBM] * n, out_specs=[_HBM] * n, out_shape=Exchange.out_shape(bufs, scatter),
                          scratch_shapes=Exchange.scratch(n), name=name)(*bufs)


def _f_rms(x, g):
    return (_rms(x, g),)


def _f_rms2(x, g1, g2):
    r = x * lax.rsqrt(jnp.sum(x * x, -1, keepdims=True) / x.shape[-1] + EPS)
    return r * g1, r * g2


def _f_out_gate(o, gate, gain):
    return (_rms(o, gain) * _silu(gate),)


def _f_gate(o, gate):
    return (o * _silu(gate),)


@jax.custom_vjp
def _swap_rope_halves(x):
    half = ROPE // 2
    lane = lax.broadcasted_iota(jnp.int32, x.shape, 1)
    return jnp.where(lane < half, pltpu.roll(x, HEAD - half, 1), jnp.where(lane < ROPE, pltpu.roll(x, half, 1), 0.0))


_swap_rope_halves.defvjp(lambda x: (_swap_rope_halves(x), None), lambda _, g: (_swap_rope_halves(g),))


def _f_qk_final(nope, rope_in, g_nope, g_rope, cos, sin):
    ms = (jnp.sum(nope * nope, -1, keepdims=True) + jnp.sum(rope_in * rope_in, -1, keepdims=True)) / QK_DIM
    r = lax.rsqrt(ms + EPS)
    a = nope * r * g_nope
    b = rope_in * r * g_rope
    return (jnp.concatenate([a, b * cos + _swap_rope_halves(b) * sin], axis=1),)


def _rope_tables(lp):
    half = ROPE // 2
    pos = jnp.maximum(jnp.arange(lp) - PAD_ROWS, 0)
    inv = ROPE_THETA ** (-jnp.arange(half, dtype=F32) / half)
    ang = pos.astype(F32)[:, None] * inv[None, :]
    zeros = jnp.zeros((lp, HEAD - ROPE), F32)
    cos = jnp.concatenate([jnp.cos(ang), jnp.cos(ang), zeros], 1)
    sin = jnp.concatenate([-jnp.sin(ang), jnp.sin(ang), zeros], 1)
    return cos, sin


def _pad_lanes(w, width=HEAD):
    return jnp.pad(w, ((0, 0), (0, width - w.shape[1])))


def _pad_rows(w, rows=HEAD):
    return jnp.pad(w, ((0, rows - w.shape[0]), (0, 0)))


def _split_heads_qk_t(w_t):
    k = w_t.shape[1]
    w3 = w_t.reshape(N_HEADS, QK_DIM, k)
    nope = w3[:, :HEAD].reshape(N_HEADS * HEAD, k)
    rope = jnp.pad(w3[:, HEAD:], ((0, 0), (0, HEAD - ROPE), (0, 0))).reshape(N_HEADS * HEAD, k)
    return jnp.concatenate([nope, rope], 0)


def _merge_heads_qk_t(g_t):
    k = g_t.shape[1]
    kw = N_HEADS * HEAD
    nope, rope = g_t[:kw].reshape(N_HEADS, HEAD, k), g_t[kw:].reshape(N_HEADS, HEAD, k)[:, :ROPE]
    return jnp.concatenate([nope, rope], 1).reshape(N_HEADS * QK_DIM, k)


def local_step(x, target, w, deferred=None):
    nb, seq, d = x.shape
    lp = seq + LEAD
    t = nb * lp
    tr = _pick(lp, (544, 128))
    ntab = lp // tr
    mxu = _MXU_DTYPE
    kw = N_HEADS * HEAD

    a_w_in_t = w["a_w_in"].astype(mxu)
    w_qkv_t, w_ga_t, w_ba_t = a_w_in_t[:3 * kw], a_w_in_t[3 * kw:4 * kw], _pad_rows(a_w_in_t[4 * kw:])
    a_conv = w["a_conv"].T
    alog, dtb, o_gain = _pad_lanes(w["a_log"]), _pad_lanes(w["a_dt_bias"]), w["a_o_gain"]
    a_norm, kv_norm, b_norm = w["a_norm"], w["kv_norm"][None, :], w["b_norm"]
    lat_norm, qlat_norm = w["kv_latent_norm"][None, :], w["b_q_latent_norm"]
    kg_nope, kg_rope = w["k_gain"][None, :HEAD], _pad_lanes(w["k_gain"][None, HEAD:])
    qg_nope, qg_rope = w["b_q_gain"][:, :HEAD], _pad_lanes(w["b_q_gain"][:, HEAD:])
    cos, sin = _rope_tables(lp)

    meta = jnp.broadcast_to(w["meta_tokens"].T[None], (nb, N_META, d))
    h0 = jnp.concatenate([jnp.zeros((nb, PAD_ROWS, d), F32), meta, x], 1).reshape(t, d)
    (hn,) = row_call("a_norm_fwd", _f_rms, [Arg(h0), Arg(a_norm, "par")], [(d, mxu, d, False)], tr)
    z_qkv = matmul("a_in_qkv", hn, w_qkv_t, "nt")
    gate_a = matmul("a_in_gate", hn, w_ga_t, "nt")
    z_ba = matmul("a_in_ba", hn, w_ba_t, "nt")
    qkv_a = conv_fwd(z_qkv, a_conv, lp)
    o_a, states, t_invs, *gathered = delta_fwd(qkv_a, z_ba, alog, dtb, lp, gather=deferred.gather_bufs if deferred else ())
    if deferred:
        w = {**w, **deferred.finish(gathered)}
    a_w_out = w["a_w_out"].astype(mxu)
    w_dkv, w_dpe = w["kv_w_down"][:, :KV_RANK].astype(mxu), _pad_lanes(w["kv_w_down"][:, KV_RANK:]).astype(mxu)
    w_ukv_t = jnp.concatenate([w["kv_w_uk"], w["kv_w_uv"]], 0).astype(mxu)
    b_w_in_t = w["b_w_in"].astype(mxu)
    w_cq_t, w_gb_t = b_w_in_t[:Q_RANK], b_w_in_t[Q_RANK:]
    w_q_t = _split_heads_qk_t(w["b_w_uq"]).astype(mxu)
    b_w_out = w["b_w_out"].astype(mxu)
    og_args = [Arg(o_a, bc=HEAD, ph=True, diff=True), Arg(gate_a, bc=HEAD, ph=True, diff=True, gdt=mxu), Arg(o_gain, "par", diff=True)]
    (og_a,) = row_call("a_out_gate_fwd", _f_out_gate, og_args, [(kw, mxu, HEAD, True)], tr, nh=N_HEADS)
    h1 = matmul("a_out", og_a, a_w_out, "nn", res=h0)

    hk, hb = row_call("b_norms_fwd", _f_rms2, [Arg(h1), Arg(kv_norm, "par"), Arg(b_norm, "par")], [(d, mxu, d, False), (d, mxu, d, False)], tr)
    c_kv_raw = matmul("kv_down", hk, w_dkv, "nn")
    k_pe = matmul("kv_down_pe", hk, w_dpe, "nn")
    c_q_raw = matmul("b_in_q", hb, w_cq_t, "nt")
    gate_b = matmul("b_in_gate", hb, w_gb_t, "nt")
    (c_kv,) = row_call("kv_latent_fwd", _f_rms, [Arg(c_kv_raw), Arg(lat_norm, "par")], [(KV_RANK, mxu, KV_RANK, False)], tr)
    (c_q,) = row_call("q_latent_fwd", _f_rms, [Arg(c_q_raw), Arg(qlat_norm, "par")], [(Q_RANK, mxu, Q_RANK, False)], tr)
    k_nope = matmul("k_up", c_kv, w_ukv_t[:kw], "nt")
    v_b = matmul("v_up", c_kv, w_ukv_t[kw:], "nt", out_dtype=mxu)
    q_up = matmul("q_up", c_q, w_q_t, "nt")
    tabs = [Arg(cos, "tab"), Arg(sin, "tab")]
    k_args = [Arg(k_nope, bc=HEAD, ph=True, diff=True, gdt=mxu), Arg(k_pe, diff=True), Arg(kg_nope, "par", diff=True), Arg(kg_rope, "par", diff=True)] + tabs
    q_args = [Arg(q_up, bc=HEAD, ph=True, diff=True, gdt=mxu), Arg(q_up, bc=HEAD, base=N_HEADS, ph=True, diff=True, gdt=mxu),
              Arg(qg_nope, "par", diff=True), Arg(qg_rope, "par", diff=True)] + tabs
    (k_fin,) = row_call("k_final_fwd", _f_qk_final, k_args, [(N_HEADS * QK_PAD, mxu, QK_PAD, True)], tr, nh=N_HEADS, ntab=ntab)
    (q_fin,) = row_call("q_final_fwd", _f_qk_final, q_args, [(N_HEADS * QK_PAD, mxu, QK_PAD, True)], tr, nh=N_HEADS, ntab=ntab)
    o_b, lse = flash_fwd(q_fin, k_fin, v_b, lp)
    gb_args = [Arg(o_b, diff=True), Arg(gate_b, diff=True, gdt=mxu)]
    (og_b,) = row_call("b_gate_fwd", _f_gate, gb_args, [(kw, mxu, kw, False)], tr)
    h2 = matmul("b_out", og_b, b_w_out, "nn", res=h1)

    loss, dh2 = loss_head(h2.reshape(nb, lp, d), target, lp)
    dh2 = dh2.reshape(t, d)
    grads = {}

    d_og_b = matmul("b_out_dx", dh2, b_w_out, "nt")
    grads["b_w_out"] = matmul("b_out_dw", og_b, dh2, "tn")
    d_o_b, d_gate_b = row_vjp_call("b_gate_bwd", _f_gate, gb_args, [Arg(d_og_b)], tr)
    dq_fin, dk_fin, dv_b = flash_bwd(q_fin, k_fin, v_b, o_b, lse, d_o_b, lp)
    dq_nope, dq_rope, d_qg_nope, d_qg_rope = row_vjp_call(
        "q_final_bwd", _f_qk_final, q_args, [Arg(dq_fin, bc=QK_PAD, ph=True)], tr, nh=N_HEADS, ntab=ntab)
    dk_nope, dk_pe, d_kg_nope, d_kg_rope = row_vjp_call(
        "k_final_bwd", _f_qk_final, k_args, [Arg(dk_fin, bc=QK_PAD, ph=True)], tr, nh=N_HEADS, ntab=ntab)
    grads["b_q_gain"] = jnp.concatenate([d_qg_nope, d_qg_rope[:, :ROPE]], 1)
    grads["k_gain"] = jnp.concatenate([d_kg_nope, d_kg_rope[:, :ROPE]], 1)[0]
    d_c_q = matmul("q_nope_dx", dq_nope, w_q_t[:kw], "nn")
    d_c_q = matmul("q_rope_dx", dq_rope, w_q_t[kw:], "nn", res=d_c_q)
    grads["b_w_uq"] = _merge_heads_qk_t(jnp.concatenate([matmul("q_nope_dw", dq_nope, c_q, "tn"), matmul("q_rope_dw", dq_rope, c_q, "tn")], 0))
    d_c_kv = matmul("k_up_dx", dk_nope, w_ukv_t[:kw], "nn")
    d_c_kv = matmul("v_up_dx", dv_b, w_ukv_t[kw:], "nn", res=d_c_kv)
    grads["kv_w_uk"], grads["kv_w_uv"] = matmul("k_up_dw", dk_nope, c_kv, "tn"), matmul("v_up_dw", dv_b, c_kv, "tn")
    d_c_q_raw, grads["b_q_latent_norm"] = row_vjp_call(
        "q_latent_bwd", _f_rms, [Arg(c_q_raw, diff=True, gdt=mxu), Arg(qlat_norm, "par", diff=True)], [Arg(d_c_q)], tr)
    d_c_kv_raw, d_lat = row_vjp_call(
        "kv_latent_bwd", _f_rms, [Arg(c_kv_raw, diff=True, gdt=mxu), Arg(lat_norm, "par", diff=True)], [Arg(d_c_kv)], tr)
    grads["kv_latent_norm"] = d_lat[0]
    d_hb = matmul("b_in_q_dx", d_c_q_raw, w_cq_t, "nn")
    d_hb = matmul("b_in_gate_dx", d_gate_b, w_gb_t, "nn", res=d_hb)
    grads["b_w_in"] = jnp.concatenate([matmul("b_in_q_dw", d_c_q_raw, hb, "tn"), matmul("b_in_gate_dw", d_gate_b, hb, "tn")], 0)
    d_hk = matmul("kv_down_dx", d_c_kv_raw, w_dkv, "nt")
    d_hk = matmul("kv_down_pe_dx", dk_pe, w_dpe, "nt", res=d_hk)
    grads["kv_w_down"] = jnp.concatenate([matmul("kv_down_dw", hk, d_c_kv_raw, "tn"), matmul("kv_down_pe_dw", hk, dk_pe, "tn")[:, :ROPE]], 1)
    dh1, d_kv_norm, grads["b_norm"] = row_vjp_call(
        "b_norms_bwd", lambda x_, g1, g2: _f_rms2(x_, g1, g2) + (x_,),
        [Arg(h1, diff=True), Arg(kv_norm, "par", diff=True), Arg(b_norm, "par", diff=True)], [Arg(d_hk), Arg(d_hb), Arg(dh2)], tr)
    grads["kv_norm"] = d_kv_norm[0]

    d_og_a = matmul("a_out_dx", dh1, a_w_out, "nt")
    grads["a_w_out"] = matmul("a_out_dw", og_a, dh1, "tn")
    d_o_a, d_gate_a, grads["a_o_gain"] = row_vjp_call(
        "a_out_gate_bwd", _f_out_gate, og_args, [Arg(d_og_a, bc=HEAD, ph=True)], tr, nh=N_HEADS)
    dqkv_a, d_ba, d_alog, d_dtb, *received = delta_bwd(qkv_a, z_ba, alog, dtb, states, t_invs, d_o_a, lp,
                                                        scatter=deferred.scatter_bufs(grads) if deferred else ())
    grads["a_log"], grads["a_dt_bias"] = d_alog[:, :N_HEADS], d_dtb[:, :N_HEADS]
    dz_qkv, d_conv = conv_bwd(z_qkv, a_conv, dqkv_a, lp)
    grads["a_conv"] = d_conv.T
    grads["a_w_in"] = jnp.concatenate([matmul("a_in_qkv_dw", dz_qkv, hn, "tn"), matmul("a_in_gate_dw", d_gate_a, hn, "tn"),
                                       matmul("a_in_ba_dw", d_ba, hn, "tn")[:2 * N_HEADS]], 0)
    ride = deferred.last_scatter_bufs(grads) if deferred else ()
    d_hn = matmul("a_in_qkv_dx", dz_qkv, w_qkv_t, "nn", scatter=ride)
    if ride:
        d_hn, *received_last = d_hn
        received = list(received) + received_last
    d_hn = matmul("a_in_gate_dx", d_gate_a, w_ga_t, "nn", res=d_hn)
    d_hn = matmul("a_in_ba_dx", d_ba, w_ba_t, "nn", res=d_hn)
    dh0, grads["a_norm"] = row_vjp_call("a_norm_bwd", lambda x_, g_: _f_rms(x_, g_) + (x_,),
                                        [Arg(h0, diff=True), Arg(a_norm, "par", diff=True)], [Arg(d_hn), Arg(dh1)], tr)
    dh0 = dh0.reshape(nb, lp, d)
    grads["meta_tokens"] = meta_grad(dh0).T
    return loss, dh0[:, LEAD:], grads, received


_SHARDED = (
    ("meta_tokens", True, False), ("a_norm", True, False), ("a_w_in", True, True), ("a_conv", True, False), ("a_w_out", False, True),
    ("kv_w_down", False, True), ("kv_w_uk", True, True), ("kv_w_uv", True, True), ("b_w_in", True, True), ("b_w_uq", True, True),
    ("b_w_out", False, True))
_REPLICATED = ("a_log", "a_dt_bias", "a_o_gain", "kv_norm", "kv_latent_norm", "k_gain", "b_norm", "b_q_latent_norm", "b_q_gain")
_ALL_WEIGHTS = ("meta_tokens", "a_norm", "a_w_in", "a_conv", "a_log", "a_dt_bias", "a_o_gain", "a_w_out", "kv_norm", "kv_w_down",
                "kv_latent_norm", "kv_w_uk", "kv_w_uv", "k_gain", "b_norm", "b_w_in", "b_q_latent_norm", "b_w_uq", "b_q_gain", "b_w_out")


def _round_up(n, m):
    return (n + m - 1) // m * m


def _pack_rows(pieces, row_multiple):
    padded = []
    for p in pieces:
        n = p.shape[-1]
        padded.append(jnp.pad(p, [(0, 0)] * (p.ndim - 1) + [(0, _round_up(n, PACK_COLS) - n)]))
    flat = jnp.concatenate(padded, -1)
    rows = _round_up(flat.shape[-1] // PACK_COLS, row_multiple)
    flat = jnp.pad(flat, [(0, 0)] * (flat.ndim - 1) + [(0, rows * PACK_COLS - flat.shape[-1])])
    return flat.reshape(flat.shape[:-1] + (rows, PACK_COLS))


def _unpack_rows(buf, sizes):
    flat = buf.reshape(buf.shape[:-2] + (-1,))
    out, off = [], 0
    for n in sizes:
        out.append(flat[..., off:off + n])
        off += _round_up(n, PACK_COLS)
    return out


def _shard_2d(a):
    return a.reshape(a.shape[-2:]) if a.ndim > 2 else a


def _kl_shard(a, by_cols):
    return _shard_2d(a).T if by_cols else _shard_2d(a)


_GROUPS_FIRST = (("a_w_in",),)
_GROUPS_LATER = (("a_w_out", "b_w_in", "b_w_out"), ("b_w_uq",), ("kv_w_down",), ("kv_w_uk", "kv_w_uv"))
_SMALL_SHARDED = ("meta_tokens", "a_norm", "a_conv")
_BY_COLS = {name: by_cols for name, by_cols, _ in _SHARDED}
ROW_ALIGN = 16


def _stack_rows(pieces):
    padded, starts, row = [], [], 0
    for p in pieces:
        r = p.shape[-2]
        padded.append(jnp.pad(p, [(0, 0)] * (p.ndim - 2) + [(0, _round_up(r, ROW_ALIGN) - r), (0, 0)]))
        starts.append(row)
        row += _round_up(r, ROW_ALIGN)
    return jnp.concatenate(padded, -2), starts


def _stack_group(arrays_by_name, names):
    arrays = [arrays_by_name[n].astype(BF16) for n in names]
    buf, starts = _stack_rows(arrays)
    return buf, [(n, s, a.shape[-2]) for n, s, a in zip(names, starts, arrays, strict=True)]


def _stack_groups(arrays_by_name, groups):
    stacked = [_stack_group(arrays_by_name, names) for names in groups]
    return [b for b, _ in stacked], [entries for _, entries in stacked]


def _full_from_gathered(gathered, layout):
    full = {}
    for got, entries in zip(gathered, layout, strict=True):
        for name, start, rows in entries:
            full[name] = got[:, start:start + rows].reshape(N_DEV * rows, got.shape[-1])
    return full


def gather_first_weights(local):
    shards = {n: _kl_shard(local[n], _BY_COLS[n]) for names in _GROUPS_FIRST for n in names}
    bufs, layout = _stack_groups(shards, _GROUPS_FIRST)
    small = [_kl_shard(local[n], _BY_COLS[n]) for n in _SMALL_SHARDED]
    bufs.append(_pack_rows([s.reshape(-1) for s in small], 8))
    gathered = _exchange("all_gather", bufs, scatter=False)
    full = _full_from_gathered(gathered[:-1], layout)
    for name, part, sh in zip(_SMALL_SHARDED, _unpack_rows(gathered[-1], [s.size for s in small]), small, strict=True):
        full[name] = part.reshape(N_DEV * sh.shape[0], sh.shape[1])
    full["a_norm"] = full["a_norm"].reshape(1, -1)
    return full


class LaterExchanges:
    def __init__(self, local):
        shards = {n: _kl_shard(local[n], _BY_COLS[n]) for names in _GROUPS_LATER for n in names}
        self.gather_bufs, self.layout = _stack_groups(shards, _GROUPS_LATER)

    def finish(self, gathered):
        return _full_from_gathered(gathered, self.layout)

    def scatter_bufs(self, grads):
        return _stack_groups(_owner_slices(grads, _GROUPS_LATER), _GROUPS_LATER)[0]

    def last_scatter_bufs(self, grads):
        bufs, self.last_layout = _stack_groups(_owner_slices(grads, _GROUPS_FIRST), _GROUPS_FIRST)
        return bufs


def _owner_slices(grads, groups):
    return {n: grads[n].reshape(N_DEV, -1, grads[n].shape[-1]) for names in groups for n in names}


def reduce_contributions(name, recv):
    _, r, c = recv.shape
    tr = _pick(r, (256, 128, 64, 32, 16, 8))

    def body(g_ref, o_ref):
        g = g_ref[0].astype(F32)
        for dev in range(1, N_DEV):
            g = g + g_ref[dev].astype(F32)
        o_ref[...] = g

    return pl.pallas_call(
        body, grid=(r // tr,), in_specs=[pl.BlockSpec((N_DEV, tr, c), lambda i: (0, i, 0))], out_specs=pl.BlockSpec((tr, c), lambda i: (i, 0)),
        out_shape=jax.ShapeDtypeStruct((r, c), F32), compiler_params=_cparams(("arbitrary",)), name=name)(recv)


def adamw_all(gs, ws, ms, vs):
    n = len(gs)

    def body(*refs):
        for i in range(n):
            g_ref, w_ref, m_ref, v_ref = (refs[j * n + i] for j in range(4))
            d_ref, mo_ref, vo_ref = (refs[(4 + j) * n + i] for j in range(3))
            g = g_ref[...]
            m_new = ADAM_B1 * m_ref[...] + (1.0 - ADAM_B1) * g
            v_new = ADAM_B2 * v_ref[...] + (1.0 - ADAM_B2) * (g * g)
            m_hat = m_new / (1.0 - ADAM_B1 ** ADAM_STEP)
            v_hat = v_new / (1.0 - ADAM_B2 ** ADAM_STEP)
            d_ref[...] = -ADAM_LR * (m_hat / (jnp.sqrt(v_hat) + ADAM_EPS) + ADAM_WD * w_ref[...])
            mo_ref[...] = m_new
            vo_ref[...] = v_new

    out = [jax.ShapeDtypeStruct(g.shape, F32) for g in gs] * 3
    res = pl.pallas_call(body, out_shape=out, compiler_params=pltpu.CompilerParams(vmem_limit_bytes=VMEM_LIMIT), name="adamw_all")(*gs, *ws, *ms, *vs)
    return res[:n], res[n:2 * n], res[2 * n:]


def kernel(x, meta_tokens, a_norm, a_w_in, a_conv, a_log, a_dt_bias, a_o_gain, a_w_out, kv_norm, kv_w_down, kv_latent_norm, kv_w_uk, kv_w_uv, k_gain, b_norm, b_w_in, b_q_latent_norm, b_w_uq, b_q_gain, b_w_out, loss_target, m_meta_tokens, m_a_norm, m_a_w_in, m_a_conv, m_a_log, m_a_dt_bias, m_a_o_gain, m_a_w_out, m_kv_norm, m_kv_w_down, m_kv_latent_norm, m_kv_w_uk, m_kv_w_uv, m_k_gain, m_b_norm, m_b_w_in, m_b_q_latent_norm, m_b_w_uq, m_b_q_gain, m_b_w_out, v_meta_tokens, v_a_norm, v_a_w_in, v_a_conv, v_a_log, v_a_dt_bias, v_a_o_gain, v_a_w_out, v_kv_norm, v_kv_w_down, v_kv_latent_norm, v_kv_w_uk, v_kv_w_uv, v_k_gain, v_b_norm, v_b_w_in, v_b_q_latent_norm, v_b_w_uq, v_b_q_gain, v_b_w_out):
    given = dict(locals())
    local_w = {n: given[n] for n in _ALL_WEIGHTS}
    full = gather_first_weights(local_w)
    for n in _REPLICATED:
        full[n] = local_w[n]
    later = LaterExchanges(local_w)

    loss_part, grad_x, grads, received_riding = local_step(x, loss_target, full, later)

    exact = [grads[n].reshape(N_DEV, -1) for n in _SMALL_SHARDED]
    exact += [jnp.broadcast_to(grads[n].reshape(1, -1), (N_DEV, grads[n].size)) for n in _REPLICATED]
    exact.append(jnp.broadcast_to(loss_part, (N_DEV, 1)))
    received = list(received_riding) + list(_exchange("all_to_all", [_pack_rows(exact, 8)], scatter=True))
    layout = later.layout + later.last_layout
    summed = [reduce_contributions(f"reduce_{i}", r) for i, r in enumerate(received)]

    grad_kl = {}
    for got, entries in zip(summed, layout):
        for n, start, rows in entries:
            grad_kl[n] = got[start:start + rows]
    parts = _unpack_rows(summed[-1], [p.shape[1] for p in exact])
    for n, part in zip(_SMALL_SHARDED + _REPLICATED, parts, strict=False):
        grad_kl[n] = part
    loss = parts[-1][0]

    def natural_2d(n, a):
        shape = _shard_2d(local_w[n]).shape if local_w[n].ndim > 1 else (1, local_w[n].size)
        return a.reshape(shape[::-1]).T if _BY_COLS.get(n, False) else a.reshape(shape)

    as_2d = lambda n, a: a.reshape(natural_2d(n, grad_kl[n]).shape)
    gs = [natural_2d(n, grad_kl[n]) for n in _ALL_WEIGHTS]
    deltas, new_m, new_v = adamw_all(gs, [as_2d(n, local_w[n]) for n in _ALL_WEIGHTS], [as_2d(n, given["m_" + n]) for n in _ALL_WEIGHTS],
                                     [as_2d(n, given["v_" + n]) for n in _ALL_WEIGHTS])
    results = [a.reshape(local_w[n].shape) for group in (gs, deltas, new_m, new_v) for n, a in zip(_ALL_WEIGHTS, group, strict=True)]
    return (loss, grad_x, *results)
```

```python
import dataclasses
import functools
import math

import jax
import jax.numpy as jnp
from jax import lax
from jax.experimental import pallas as pl
from jax.experimental.pallas import tpu as pltpu

F32 = jnp.float32
BF16 = jnp.bfloat16
_MXU_DTYPE = jnp.bfloat16

N_DEV = 8
D_MODEL = 1024
N_HEADS = 8
HEAD = 128
CHUNK = 64
N_META = 16
PAD_ROWS = 2 * CHUNK - N_META
LEAD = PAD_ROWS + N_META
ROPE = 64
QK_DIM = HEAD + ROPE
QK_PAD = 2 * HEAD
KV_RANK = 256
Q_RANK = 384
CONV_K = 4
EPS = 1e-6
NEG = -1e30
ROPE_THETA = 10000.0
ADAM_LR, ADAM_B1, ADAM_B2, ADAM_EPS, ADAM_WD, ADAM_STEP = 0.001, 0.9, 0.999, 1e-08, 0.01, 10
PACK_COLS = 512
VMEM_LIMIT = 56 * 1024 * 1024


def _pick(n, options):
    for o in options:
        if n % o == 0:
            return o
    raise ValueError(f"no tile for {n} among {options}")


def _cparams(sem):
    return pltpu.CompilerParams(dimension_semantics=sem, vmem_limit_bytes=VMEM_LIMIT)


def _dims(a, dims):
    if a.ndim == 2:
        return (dims, ((), ()))
    (ca,), (cb,) = dims
    return (((ca + 1,), (cb + 1,)), ((0,), (0,)))


def _dot(a, b, dims):
    return lax.dot_general(a.astype(_MXU_DTYPE), b.astype(_MXU_DTYPE), _dims(a, dims), preferred_element_type=F32)


@jax.custom_vjp
def mm_nn(a, b):
    return _dot(a, b, ((1,), (0,)))


@jax.custom_vjp
def mm_nt(a, b):
    return _dot(a, b, ((1,), (1,)))


@jax.custom_vjp
def mm_tn(a, b):
    return _dot(a, b, ((0,), (0,)))


mm_nn.defvjp(lambda a, b: (mm_nn(a, b), (a, b)), lambda r, g: (mm_nt(g, r[1]), mm_tn(r[0], g)))
mm_nt.defvjp(lambda a, b: (mm_nt(a, b), (a, b)), lambda r, g: (mm_nn(g, r[1]), mm_tn(g, r[0])))
mm_tn.defvjp(lambda a, b: (mm_tn(a, b), (a, b)), lambda r, g: (mm_nt(r[1], g), mm_nn(r[0], g)))


def _split_terms(x, n):
    terms, rest = [], x
    for _ in range(n):
        t = rest.astype(_MXU_DTYPE)
        terms.append(t)
        rest = rest - t.astype(F32)
    return terms


def _dot_01_raw(m, x, dims):
    m = m.astype(_MXU_DTYPE)
    return sum(lax.dot_general(m, t, _dims(m, dims), preferred_element_type=F32) for t in _split_terms(x, 3))


@jax.custom_vjp
def _dot_01(m, x):
    return _dot_01_raw(m, x, ((1,), (0,)))


_dot_01.defvjp(lambda m, x: (_dot_01(m, x), m), lambda m, g: (jnp.zeros_like(m), _dot_01_raw(m, g, ((0,), (0,)))))


def _inv_unit_lower(a):
    n = a.shape[-1]
    eye = (lax.broadcasted_iota(jnp.int32, (n, n), 0) == lax.broadcasted_iota(jnp.int32, (n, n), 1)).astype(F32)
    d = lambda u, w: lax.dot_general(u, w, _dims(u, ((1,), (0,))), preferred_element_type=F32)
    t = eye - a
    p = a.astype(_MXU_DTYPE)
    p = d(p, p)
    squarings = int(math.log2(n)) - 1
    for s in range(squarings):
        ph = p.astype(_MXU_DTYPE)
        t_hi, t_lo = _split_terms(t, 2)
        t = t + (d(t_hi, ph) + d(t_lo, ph))
        if s + 1 < squarings:
            p = d(ph, ph)
    return t


@jax.custom_vjp
def _inv_lookup(a, t):
    return t


def _inv_lookup_bwd(t, g):
    return -mm_tn(t, mm_nt(g, t)), jnp.zeros_like(t)


_inv_lookup.defvjp(lambda a, t: (t, t), _inv_lookup_bwd)


def _sigmoid(x):
    return 1.0 / (1.0 + jnp.exp(-x))


@jax.custom_vjp
def _silu(x):
    return x * _sigmoid(x)


def _silu_fwd(x):
    s = _sigmoid(x)
    return x * s, (x, s)


_silu.defvjp(_silu_fwd, lambda r, g: (g * (r[1] * (1.0 + r[0] * (1.0 - r[1]))),))


def _softplus(x):
    return jnp.where(x > 20.0, x, jnp.log(1.0 + jnp.exp(jnp.minimum(x, 20.0))))


def _rms(x, g, width=None):
    ms = jnp.sum(x * x, -1, keepdims=True) / (x.shape[-1] if width is None else width)
    return x * lax.rsqrt(ms + EPS) * g


MM_VMEM_BUDGET = 40 * 1024 * 1024


def _matmul_rows(name, a, b, mode, out_dtype, res, scatter):
    m, k = a.shape
    n = b.shape[1] if mode == "nn" else b.shape[0]
    dims = {"nn": ((1,), (0,)), "nt": ((1,), (1,))}[mode]
    out_bytes = jnp.dtype(out_dtype).itemsize
    n_in, nx = 2 + (res is not None), len(scatter)

    def vmem(tm):
        blocks = 2 * tm * k * a.dtype.itemsize + 2 * k * n * b.dtype.itemsize + 2 * tm * n * out_bytes + tm * n * 4
        return blocks + (2 * tm * n * res.dtype.itemsize if res is not None else 0)

    tm = next(c for c in (2176, 1088, 512, 256, 128, 64) if m % c == 0 and vmem(c) <= MM_VMEM_BUDGET)
    steps = m // tm

    def body(*refs):
        a_ref, b_ref, o_ref = refs[0], refs[1], refs[n_in + nx]
        i = pl.program_id(0)
        finish = _ride(scatter, True, refs[n_in:n_in + nx], refs[n_in + nx + 1:n_in + 2 * nx + 1], refs[n_in + 2 * nx + 1:], i == 0, i == steps - 1)
        out = _dot(a_ref[...], b_ref[...], dims)
        if res is not None:
            out = out + refs[2][...].astype(F32)
        o_ref[...] = out.astype(o_ref.dtype)
        finish()

    o_spec = pl.BlockSpec((tm, n), lambda i: (i, 0))
    in_specs = [pl.BlockSpec((tm, k), lambda i: (i, 0)), pl.BlockSpec(b.shape, lambda i: (0, 0))] + ([o_spec] if res is not None else [])
    args = (a, b) + ((res,) if res is not None else ())
    out = pl.pallas_call(
        body, grid=(steps,), in_specs=in_specs + [_HBM] * nx, out_specs=[o_spec] + [_HBM] * nx,
        out_shape=[jax.ShapeDtypeStruct((m, n), out_dtype)] + Exchange.out_shape(scatter, True), scratch_shapes=Exchange.scratch(nx) if nx else [],
        compiler_params=_cparams(("arbitrary",) if nx else ("parallel",)), name=name)(*args, *scatter)
    return out if nx else out[0]


def matmul(name, a, b, mode, out_dtype=F32, res=None, scatter=()):
    if mode != "tn":
        return _matmul_rows(name, a, b, mode, out_dtype, res, scatter)
    (k, m), (k2, n) = a.shape, b.shape
    assert k == k2 and res is None, (name, a.shape, b.shape, mode)
    tm = _pick(m, (1024, 512, 384, 256, 128))
    tn = _pick(n, (1024, 512, 384, 256, 128))
    tk = _pick(k, (512, 256, 128))
    nk = k // tk
    dims = ((0,), (0,))

    def body(*refs):
        if res is None:
            a_ref, b_ref, o_ref, acc_ref = refs
        else:
            a_ref, b_ref, r_ref, o_ref, acc_ref = refs
        kk = pl.program_id(2)

        @pl.when(kk == 0)
        def _():
            acc_ref[...] = jnp.zeros_like(acc_ref)

        acc_ref[...] += _dot(a_ref[...], b_ref[...], dims)

        @pl.when(kk == nk - 1)
        def _():
            out = acc_ref[...]
            if res is not None:
                out = out + r_ref[...].astype(F32)
            o_ref[...] = out.astype(o_ref.dtype)

    a_spec = pl.BlockSpec((tk, tm), lambda i, j, kk: (kk, i)) if mode == "tn" else pl.BlockSpec((tm, tk), lambda i, j, kk: (i, kk))
    b_spec = pl.BlockSpec((tn, tk), lambda i, j, kk: (j, kk)) if mode == "nt" else pl.BlockSpec((tk, tn), lambda i, j, kk: (kk, j))
    o_spec = pl.BlockSpec((tm, tn), lambda i, j, kk: (i, j))
    in_specs = [a_spec, b_spec] + ([o_spec] if res is not None else [])
    args = (a, b) + ((res,) if res is not None else ())
    return pl.pallas_call(
        body, grid=(m // tm, n // tn, nk), in_specs=in_specs, out_specs=o_spec,
        out_shape=jax.ShapeDtypeStruct((m, n), out_dtype), scratch_shapes=[pltpu.VMEM((tm, tn), F32)],
        compiler_params=_cparams(("parallel", "parallel", "arbitrary")), name=name)(*args)


@dataclasses.dataclass
class Arg:
    arr: jax.Array
    kind: str = "row"
    bc: int = 0
    base: int = 0
    ph: bool = False
    diff: bool = False
    gdt: object = F32


def _arg_spec(a, tr, nh, ntab, base=None):
    bc = a.bc or a.arr.shape[1]
    base = a.base if base is None else base
    width = bc * nh if a.ph else bc
    col = base // nh if a.ph else base
    assert not a.ph or base % nh == 0
    if a.kind == "row":
        return pl.BlockSpec((tr, width), lambda i: (i, col))
    if a.kind == "tab":
        return pl.BlockSpec((tr, width), lambda i: (i % ntab, col))
    return pl.BlockSpec((a.arr.shape[0], width), lambda i: (0, col))


def _head_view(ref, a, h):
    bc = a.bc or a.arr.shape[1]
    v = ref[:, h * bc:(h + 1) * bc] if a.ph else ref[...]
    return v.astype(F32) if jnp.issubdtype(v.dtype, jnp.floating) else v


def row_call(name, fn, args, outs, tr, nh=1, ntab=1):
    t = args[0].arr.shape[0]
    n_in = len(args)
    out_args = [Arg(None, "row", bc, 0, ph) for (_, _, bc, ph) in outs]

    def body(*refs):
        for h in range(nh):
            res = fn(*[_head_view(r, a, h) for r, a in zip(refs[:n_in], args, strict=True)])
            for r, a, v in zip(refs[n_in:], out_args, res, strict=True):
                if a.ph:
                    r[:, h * a.bc:(h + 1) * a.bc] = v.astype(r.dtype)
                elif h == nh - 1:
                    r[...] = v.astype(r.dtype)

    return pl.pallas_call(
        body, grid=(t // tr,), in_specs=[_arg_spec(a, tr, nh, ntab) for a in args], out_specs=[_arg_spec(a, tr, nh, ntab) for a in out_args],
        out_shape=[jax.ShapeDtypeStruct((t, cols), dt) for (cols, dt, _, _) in outs],
        compiler_params=_cparams(("arbitrary",)), name=name)(*[a.arr for a in args])


def row_vjp_call(name, fn, args, cts, tr, nh=1, ntab=1):
    t = args[0].arr.shape[0]
    n_in, n_ct = len(args), len(cts)
    diff_idx = [k for k, a in enumerate(args) if a.diff]

    def body(*refs):
        out_refs = refs[n_in + n_ct:]
        shared = [None] * len(diff_idx)
        for k, r in zip(diff_idx, out_refs, strict=True):
            if args[k].kind == "par":
                @pl.when(pl.program_id(0) == 0)
                def _(r=r):
                    r[...] = jnp.zeros_like(r)

        for h in range(nh):
            vals = [_head_view(r, a, h) for r, a in zip(refs[:n_in], args, strict=True)]
            ct_vals = tuple(_head_view(r, a, h) for r, a in zip(refs[n_in:n_in + n_ct], cts, strict=True))

            def f(*dv, vals=vals):
                full = list(vals)
                for k, v in zip(diff_idx, dv, strict=True):
                    full[k] = v
                return tuple(fn(*full))

            _, vjp = jax.vjp(f, *[vals[k] for k in diff_idx])
            for j, (k, r, g) in enumerate(zip(diff_idx, out_refs, vjp(ct_vals), strict=True)):
                a = args[k]
                bc = a.bc or a.arr.shape[1]
                if not a.ph:
                    shared[j] = g if shared[j] is None else shared[j] + g
                elif a.kind == "row":
                    r[:, h * bc:(h + 1) * bc] = g.astype(r.dtype)
                else:
                    r[:, h * bc:(h + 1) * bc] += g
        for j, (k, r) in enumerate(zip(diff_idx, out_refs, strict=True)):
            if not args[k].ph:
                if args[k].kind == "row":
                    r[...] = shared[j].astype(r.dtype)
                else:
                    r[...] += shared[j]

    out_specs, out_shape = [], []
    for k in diff_idx:
        a = args[k]
        bc = a.bc or a.arr.shape[1]
        out_specs.append(_arg_spec(a, tr, nh, ntab, base=0))
        out_shape.append(jax.ShapeDtypeStruct((t if a.kind == "row" else a.arr.shape[0], bc * (nh if a.ph else 1)), a.gdt if a.kind == "row" else F32))
    in_specs = [_arg_spec(a, tr, nh, ntab) for a in list(args) + list(cts)]
    return pl.pallas_call(
        body, grid=(t // tr,), in_specs=in_specs, out_specs=out_specs, out_shape=out_shape,
        compiler_params=_cparams(("arbitrary",)), name=name)(*[a.arr for a in list(args) + list(cts)])


def _conv_taps(x, w):
    rows = lax.broadcasted_iota(jnp.int32, x.shape, 0)
    y = x * w[CONV_K - 1:CONV_K, :]
    for s in range(1, CONV_K):
        y = y + jnp.where(rows >= s, pltpu.roll(x, s, 0), 0.0) * w[CONV_K - 1 - s:CONV_K - s, :]
    return y


CONV_HEADS = 4
CONV_BLOCKS_PER_THIRD = N_HEADS // CONV_HEADS


def _conv_post(y, block):
    a = _silu(y)
    normed = block < 2 * CONV_BLOCKS_PER_THIRD
    scale = jnp.where(block < CONV_BLOCKS_PER_THIRD, HEAD ** -0.5, 1.0)
    return a * jnp.where(normed, lax.rsqrt(jnp.sum(a * a, -1, keepdims=True) + EPS) * scale, 1.0)


def conv_fwd(z, w, lp):
    t, width = z.shape
    cols = CONV_HEADS * HEAD

    def body(z_ref, w_ref, o_ref, y_ref):
        block = pl.program_id(1)
        for h in range(CONV_HEADS):
            cs = slice(h * HEAD, (h + 1) * HEAD)
            y = _conv_taps(z_ref[:, cs], w_ref[:, cs])
            y_ref[:, cs] = y
            o_ref[:, cs] = _conv_post(y, block)

    blk = pl.BlockSpec((lp, cols), lambda b, j: (b, j))
    out = jax.ShapeDtypeStruct((t, width), F32)
    return pl.pallas_call(
        body, grid=(t // lp, width // cols), in_specs=[blk, pl.BlockSpec((CONV_K, cols), lambda b, j: (0, j))],
        out_specs=[blk, blk], out_shape=[out, out], compiler_params=_cparams(("arbitrary", "arbitrary")), name="a_conv_fwd")(z, w)


def conv_bwd(z, y, w, dout, lp):
    t, width = z.shape
    cols = CONV_HEADS * HEAD

    def body(z_ref, y_ref, w_ref, g_ref, dz_ref, dw_ref):
        block = pl.program_id(0)

        @pl.when(pl.program_id(1) == 0)
        def _():
            dw_ref[...] = jnp.zeros_like(dw_ref)

        for h in range(CONV_HEADS):
            cs = slice(h * HEAD, (h + 1) * HEAD)
            x, wv = z_ref[:, cs], w_ref[:, cs]
            _, vjp = jax.vjp(lambda y_: _conv_post(y_, block), y_ref[:, cs])
            (dy,) = vjp(g_ref[:, cs])
            rows = lax.broadcasted_iota(jnp.int32, x.shape, 0)
            dx = dy * wv[CONV_K - 1:CONV_K, :]
            dw_ref[CONV_K - 1:CONV_K, cs] += jnp.sum(dy * x, axis=0, keepdims=True)
            for s in range(1, CONV_K):
                dy_up = jnp.where(rows < lp - s, pltpu.roll(dy, lp - s, 0), 0.0)
                dx = dx + dy_up * wv[CONV_K - 1 - s:CONV_K - s, :]
                dw_ref[CONV_K - 1 - s:CONV_K - s, cs] += jnp.sum(dy_up * x, axis=0, keepdims=True)
            dz_ref[:, cs] = dx.astype(dz_ref.dtype)

    blk = pl.BlockSpec((lp, cols), lambda j, b: (b, j))
    w_blk = pl.BlockSpec((CONV_K, cols), lambda j, b: (0, j))
    return pl.pallas_call(
        body, grid=(width // cols, t // lp), in_specs=[blk, blk, w_blk, blk], out_specs=[blk, w_blk],
        out_shape=[jax.ShapeDtypeStruct((t, width), _MXU_DTYPE), jax.ShapeDtypeStruct((CONV_K, width), F32)],
        compiler_params=_cparams(("arbitrary", "arbitrary")), name="a_conv_bwd")(z, y, w, dout)


def _delta_chunk(q, k, v, ba, alog, dtb, state, t_stored, h0):
    n_g, c = q.shape[0], q.shape[1]
    lane = lax.broadcasted_iota(jnp.int32, (1, HEAD), 1)

    def pick(x, offset):
        return jnp.concatenate([jnp.sum(x * (lane == offset + h0 + g).astype(F32), axis=1, keepdims=True)[None] for g in range(n_g)], 0)

    b_raw, a_raw = pick(ba, 0), pick(ba, N_HEADS)
    a_log, dt_bias = pick(alog, 0), pick(dtb, 0)
    beta = _sigmoid(b_raw)
    g = -jnp.exp(a_log) * _softplus(a_raw + dt_bias)
    ri = lax.broadcasted_iota(jnp.int32, (c, c), 0)
    ci = lax.broadcasted_iota(jnp.int32, (c, c), 1)
    tril = ci <= ri
    lower = jnp.broadcast_to(tril.astype(F32), (n_g, c, c))
    gc_col = _dot_01(lower, g * jnp.ones((1, 1, HEAD), F32))[:, :, :1]
    gc_row = _dot_01(jnp.ones((n_g, 8, c), F32), g * (ri <= ci).astype(F32)[None])[:, 0:1, :]
    gc_last = jnp.sum(g, axis=1, keepdims=True)
    decay = jnp.exp(jnp.where(tril, gc_col - gc_row, NEG))
    e_gc = jnp.exp(gc_col)
    kb = k * beta
    a_mat = jnp.where(ci < ri, mm_nt(kb, k) * decay, 0.0)
    t_inv = _inv_unit_lower(a_mat) if t_stored is None else _inv_lookup(a_mat, t_stored)
    u_base = mm_nn(t_inv, v * beta)
    w_dec = mm_nn(t_inv, kb * e_gc)
    attn = jnp.where(tril, mm_nt(q, k) * decay, 0.0)
    u = u_base - mm_nn(w_dec, state)
    o = mm_nn(q * e_gc, state) + mm_nn(attn, u)
    new_state = state * jnp.exp(gc_last) + mm_tn(k * jnp.exp(gc_last - gc_col), u)
    return o, new_state, t_inv


DELTA_CHUNKS_FWD = 2
DELTA_CHUNKS_BWD = 2


def _qkv_heads(ref, rs, part):
    return jnp.stack([ref[rs, (part * N_HEADS + g) * HEAD:(part * N_HEADS + g + 1) * HEAD] for g in range(N_HEADS)])


def _ride(bufs, scatter, refs_in, refs_out, sems, first, last):
    if not bufs:
        return lambda: None

    @pl.when(first)
    def _():
        Exchange(refs_in, refs_out, *sems, scatter).start()

    def finish():
        @pl.when(last)
        def _():
            Exchange(refs_in, refs_out, *sems, scatter).wait()

    return finish


def delta_fwd(qkv, ba, alog, dtb, lp, gather=()):
    t = qkv.shape[0]
    nb, nc = t // lp, lp // CHUNK
    cps = DELTA_CHUNKS_FWD
    ng, rows = nc // cps, cps * CHUNK
    nx = len(gather)
    assert nc % cps == 0

    def body(*refs):
        qkv_ref, ba_ref, al_ref, dt_ref = refs[:4]
        o_ref, s_ref, t_ref = refs[4 + nx:7 + nx]
        state_ref = refs[7 + 2 * nx]
        b, n = pl.program_id(0), pl.program_id(1)
        finish = _ride(gather, False, refs[4:4 + nx], refs[7 + nx:7 + 2 * nx], refs[8 + 2 * nx:], (b == 0) & (n == 0), (b == nb - 1) & (n == ng - 1))

        @pl.when(n == 0)
        def _():
            state_ref[...] = jnp.zeros_like(state_ref)

        al, dtv = al_ref[...], dt_ref[...]
        for c in range(cps):
            rs = slice(c * CHUNK, (c + 1) * CHUNK)
            state = state_ref[...]
            o, new_state, t_inv = _delta_chunk(_qkv_heads(qkv_ref, rs, 0), _qkv_heads(qkv_ref, rs, 1), _qkv_heads(qkv_ref, rs, 2),
                                               ba_ref[rs, :], al, dtv, state, None, 0)
            for g in range(N_HEADS):
                o_ref[rs, g * HEAD:(g + 1) * HEAD] = o[g]
                s_ref[g, c] = state[g]
                t_ref[g, c] = t_inv[g]
            state_ref[...] = new_state
        finish()

    rows_of = lambda width: pl.BlockSpec((rows, width), lambda b, n: (b * ng + n, 0))
    par_spec = pl.BlockSpec((1, HEAD), lambda b, n: (0, 0))
    return pl.pallas_call(
        body, grid=(nb, ng), in_specs=[rows_of(3 * N_HEADS * HEAD), rows_of(HEAD), par_spec, par_spec] + [_HBM] * nx,
        out_specs=[rows_of(N_HEADS * HEAD), pl.BlockSpec((None, N_HEADS, cps, HEAD, HEAD), lambda b, n: (b, 0, n, 0, 0)),
                   pl.BlockSpec((None, N_HEADS, cps, CHUNK, CHUNK), lambda b, n: (b, 0, n, 0, 0))] + [_HBM] * nx,
        out_shape=[jax.ShapeDtypeStruct((t, N_HEADS * HEAD), F32), jax.ShapeDtypeStruct((nb, N_HEADS, nc, HEAD, HEAD), F32),
                   jax.ShapeDtypeStruct((nb, N_HEADS, nc, CHUNK, CHUNK), F32)] + Exchange.out_shape(gather, False),
        scratch_shapes=[pltpu.VMEM((N_HEADS, HEAD, HEAD), F32)] + (Exchange.scratch(nx) if nx else []),
        compiler_params=_cparams(("arbitrary", "arbitrary")), name="delta_fwd")(qkv, ba, alog, dtb, *gather)


def delta_bwd(qkv, ba, alog, dtb, states, t_invs, do, lp, scatter=()):
    t = qkv.shape[0]
    nb, nc = t // lp, lp // CHUNK
    cps = DELTA_CHUNKS_BWD
    ng, rows = nc // cps, cps * CHUNK
    nx = len(scatter)

    def body(*refs):
        qkv_ref, ba_ref, al_ref, dt_ref, s_ref, t_ref, do_ref = refs[:7]
        dqkv_ref, dba_ref, dal_ref, ddt_ref = refs[7 + nx:11 + nx]
        dstate_ref = refs[11 + 2 * nx]
        b, step = pl.program_id(0), pl.program_id(1)
        finish = _ride(scatter, True, refs[7:7 + nx], refs[11 + nx:11 + 2 * nx], refs[12 + 2 * nx:], (b == 0) & (step == 0),
                       (b == nb - 1) & (step == ng - 1))

        @pl.when(step == 0)
        def _():
            dstate_ref[...] = jnp.zeros_like(dstate_ref)

        @pl.when((b == 0) & (step == 0))
        def _():
            dal_ref[...] = jnp.zeros_like(dal_ref)
            ddt_ref[...] = jnp.zeros_like(ddt_ref)

        al, dtv = al_ref[...], dt_ref[...]
        d_al = jnp.zeros((1, HEAD), F32)
        d_dt = jnp.zeros((1, HEAD), F32)
        for c in reversed(range(cps)):
            rs = slice(c * CHUNK, (c + 1) * CHUNK)
            t_n = jnp.stack([t_ref[g, c] for g in range(N_HEADS)])
            s_n = jnp.stack([s_ref[g, c] for g in range(N_HEADS)])
            d_o = jnp.stack([do_ref[rs, g * HEAD:(g + 1) * HEAD] for g in range(N_HEADS)])

            def f(q_, k_, v_, ba_, al_, dt_, s_, t_n=t_n):
                return _delta_chunk(q_, k_, v_, ba_, al_, dt_, s_, t_n, 0)[:2]

            _, vjp = jax.vjp(f, _qkv_heads(qkv_ref, rs, 0), _qkv_heads(qkv_ref, rs, 1), _qkv_heads(qkv_ref, rs, 2), ba_ref[rs, :], al, dtv, s_n)
            grads = vjp((d_o, dstate_ref[...]))
            for part in range(3):
                for g in range(N_HEADS):
                    dqkv_ref[rs, (part * N_HEADS + g) * HEAD:(part * N_HEADS + g + 1) * HEAD] = grads[part][g]
            dba_ref[rs, :] = grads[3]
            d_al, d_dt = d_al + grads[4], d_dt + grads[5]
            dstate_ref[...] = grads[6]
        dal_ref[...] += d_al
        ddt_ref[...] += d_dt
        finish()

    rows_of = lambda width: pl.BlockSpec((rows, width), lambda b, n: (b * ng + ng - 1 - n, 0))
    par_spec = pl.BlockSpec((1, HEAD), lambda b, n: (0, 0))
    return pl.pallas_call(
        body, grid=(nb, ng),
        in_specs=[rows_of(3 * N_HEADS * HEAD), rows_of(HEAD), par_spec, par_spec,
                  pl.BlockSpec((None, N_HEADS, cps, HEAD, HEAD), lambda b, n: (b, 0, ng - 1 - n, 0, 0)),
                  pl.BlockSpec((None, N_HEADS, cps, CHUNK, CHUNK), lambda b, n: (b, 0, ng - 1 - n, 0, 0)), rows_of(N_HEADS * HEAD)] + [_HBM] * nx,
        out_specs=[rows_of(3 * N_HEADS * HEAD), rows_of(HEAD), par_spec, par_spec] + [_HBM] * nx,
        out_shape=[jax.ShapeDtypeStruct((t, 3 * N_HEADS * HEAD), F32), jax.ShapeDtypeStruct((t, HEAD), F32),
                   jax.ShapeDtypeStruct((1, HEAD), F32), jax.ShapeDtypeStruct((1, HEAD), F32)] + Exchange.out_shape(scatter, True),
        scratch_shapes=[pltpu.VMEM((N_HEADS, HEAD, HEAD), F32)] + (Exchange.scratch(nx) if nx else []),
        compiler_params=_cparams(("arbitrary", "arbitrary")), name="delta_bwd")(qkv, ba, alog, dtb, states, t_invs, do, *scatter)


ATT_Q_TILE = 256
ATT_K_TILE = 512
ATT_SCALE = QK_DIM ** -0.5


def _tiles(end, size):
    return [(s, min(s + size, end)) for s in range(0, end, size)]


def _att_visible(q0, q1, k0, k1, keys_first):
    if k1 <= q0 + CHUNK and k0 >= PAD_ROWS:
        return None
    shape = (k1 - k0, q1 - q0) if keys_first else (q1 - q0, k1 - k0)
    qpos = q0 + lax.broadcasted_iota(jnp.int32, shape, 1 if keys_first else 0)
    kpos = k0 + lax.broadcasted_iota(jnp.int32, shape, 0 if keys_first else 1)
    shift = CHUNK.bit_length() - 1
    return (jnp.right_shift(kpos, shift) <= jnp.right_shift(qpos, shift)) & (kpos >= PAD_ROWS)


def _att_seq_specs(lp):
    return pl.BlockSpec((lp, QK_PAD), lambda b, h: (b, h)), pl.BlockSpec((lp, HEAD), lambda b, h: (b, h))


def flash_fwd(q, k, v, lp):
    t = q.shape[0]
    qk_seq, o_seq = _att_seq_specs(lp)

    def body(q_ref, k_ref, v_ref, o_ref, lse_ref):
        q_tiles = _tiles(lp, ATT_Q_TILE)

        def score_steps(q0, q1, out):
            def step(k0, k1):
                s = mm_nt(q_ref[q0:q1, :], k_ref[k0:k1, :])
                vis = _att_visible(q0, q1, k0, k1, False)
                s = s if vis is None else jnp.where(vis, s, NEG)
                out["scores"].append(s)
                row_max = jnp.max(s, -1, keepdims=True)
                out["m"] = row_max if out["m"] is None else jnp.maximum(out["m"], row_max)
            return [functools.partial(step, k0, k1) for k0, k1 in _tiles(q1, ATT_K_TILE)]

        cur = {"scores": [], "m": None}
        for step in score_steps(*q_tiles[0], cur):
            step()
        for i, (q0, q1) in enumerate(q_tiles):
            nxt = {"scores": [], "m": None}
            ahead = score_steps(*q_tiles[i + 1], nxt) if i + 1 < len(q_tiles) else []
            l = jnp.zeros((q1 - q0, 1), F32)
            acc = jnp.zeros((q1 - q0, HEAD), F32)
            for s, (k0, k1) in zip(cur["scores"], _tiles(q1, ATT_K_TILE), strict=True):
                if ahead:
                    ahead.pop(0)()
                p = jnp.exp(s - cur["m"])
                l = l + jnp.sum(p, -1, keepdims=True)
                acc = acc + mm_nn(p, v_ref[k0:k1, :])
            for step in ahead:
                step()
            o_ref[q0:q1, :] = acc / l
            lse_ref[q0:q1, :] = jnp.broadcast_to(cur["m"] + jnp.log(l), (q1 - q0, HEAD))
            cur = nxt

    big = jax.ShapeDtypeStruct((t, N_HEADS * HEAD), F32)
    return pl.pallas_call(
        body, grid=(t // lp, N_HEADS), in_specs=[qk_seq, qk_seq, o_seq], out_specs=[o_seq, o_seq], out_shape=[big, big],
        compiler_params=_cparams(("arbitrary", "arbitrary")), name="flash_fwd")(q, k, v)


def flash_bwd(q, k, v, o, lse, do, lp):
    t = q.shape[0]
    qk_seq, o_seq = _att_seq_specs(lp)

    def body(q_ref, k_ref, v_ref, o_ref, lse_ref, do_ref, dq_ref, dk_ref, dv_ref):
        dk_ref[...] = jnp.zeros_like(dk_ref)
        dv_ref[...] = jnp.zeros_like(dv_ref)
        for q0, q1 in _tiles(lp, ATT_Q_TILE):
            qb, dob = q_ref[q0:q1, :], do_ref[q0:q1, :]
            lse_row = jnp.transpose(lse_ref[q0:q1, :])[0:1, :]
            dsum_row = jnp.sum(jnp.transpose(dob * o_ref[q0:q1, :]), axis=0, keepdims=True)
            dq = jnp.zeros((q1 - q0, QK_PAD), F32)
            for k0, k1 in _tiles(q1, ATT_K_TILE):
                kb, vb = k_ref[k0:k1, :], v_ref[k0:k1, :]
                s = mm_nt(kb, qb)
                vis = _att_visible(q0, q1, k0, k1, True)
                s = s if vis is None else jnp.where(vis, s, NEG)
                p = jnp.exp(s - lse_row)
                ds = p * (mm_nt(vb, dob) - dsum_row)
                dv_ref[k0:k1, :] += mm_nn(p, dob)
                dk_ref[k0:k1, :] += mm_nn(ds, qb)
                dq = dq + mm_tn(ds, kb)
            dq_ref[q0:q1, :] = dq

    return pl.pallas_call(
        body, grid=(t // lp, N_HEADS), in_specs=[qk_seq, qk_seq, o_seq, o_seq, o_seq, o_seq], out_specs=[qk_seq, qk_seq, o_seq],
        out_shape=[jax.ShapeDtypeStruct((t, N_HEADS * QK_PAD), F32), jax.ShapeDtypeStruct((t, N_HEADS * QK_PAD), F32),
                   jax.ShapeDtypeStruct((t, N_HEADS * HEAD), F32)],
        compiler_params=_cparams(("arbitrary", "arbitrary")), name="flash_bwd")(q, k, v, o, lse, do)


def loss_head(h2, target, lp):
    nb, seq, d = target.shape
    tr = 128
    nblk = lp // tr
    lead_blocks = LEAD // tr

    def body(h_ref, t_ref, loss_ref, dh_ref, acc_ref):
        b, i = pl.program_id(0), pl.program_id(1)

        @pl.when((b == 0) & (i == 0))
        def _():
            acc_ref[...] = jnp.zeros_like(acc_ref)

        @pl.when(i < lead_blocks)
        def _():
            dh_ref[...] = jnp.zeros_like(dh_ref)

        @pl.when(i >= lead_blocks)
        def _():
            err = h_ref[...] - t_ref[...]
            dh_ref[...] = err * (1.0 / d)
            acc_ref[...] += jnp.sum(err * err, axis=0, keepdims=True)

        @pl.when((b == nb - 1) & (i == nblk - 1))
        def _():
            loss_ref[...] = jnp.sum(acc_ref[...], axis=1, keepdims=True) * (0.5 / d)

    return pl.pallas_call(
        body, grid=(nb, nblk),
        in_specs=[pl.BlockSpec((None, tr, d), lambda b, i: (b, i, 0)),
                  pl.BlockSpec((None, tr, d), lambda b, i: (b, jnp.maximum(i - lead_blocks, 0), 0))],
        out_specs=[pl.BlockSpec((1, 1), lambda b, i: (0, 0)), pl.BlockSpec((None, tr, d), lambda b, i: (b, i, 0))],
        out_shape=[jax.ShapeDtypeStruct((1, 1), F32), jax.ShapeDtypeStruct((nb, lp, d), F32)],
        scratch_shapes=[pltpu.VMEM((1, d), F32)], compiler_params=_cparams(("arbitrary", "arbitrary")), name="loss_head")(h2, target)


def meta_grad(dh0):
    nb, _, d = dh0.shape

    def body(g_ref, o_ref):
        @pl.when(pl.program_id(0) == 0)
        def _():
            o_ref[...] = jnp.zeros_like(o_ref)

        o_ref[...] += g_ref[PAD_ROWS:LEAD, :]

    return pl.pallas_call(
        body, grid=(nb,), in_specs=[pl.BlockSpec((None, LEAD, d), lambda b: (b, 0, 0))],
        out_specs=pl.BlockSpec((N_META, d), lambda b: (0, 0)), out_shape=jax.ShapeDtypeStruct((N_META, d), F32),
        compiler_params=_cparams(("arbitrary",)), name="meta_grad")(dh0)


_HBM = pl.BlockSpec(memory_space=pltpu.HBM)


def _mesh_pos():
    x, y, c = lax.axis_index("x"), lax.axis_index("y"), lax.axis_index("c")
    return x, y, c


def _peer(x, y, c, k):
    px = 1 - x if k & 4 else x
    py = 1 - y if k & 2 else y
    pc = 1 - c if k & 1 else c
    return (px, py, pc), 4 * px + 2 * py + pc


class Exchange:
    def __init__(self, x_refs, out_refs, send_sems, recv_sems, local_sems, scatter):
        self.x_refs, self.out_refs, self.scatter = x_refs, out_refs, scatter
        self.send_sems, self.recv_sems, self.local_sems = send_sems, recv_sems, local_sems
        self.pos = _mesh_pos()
        x, y, c = self.pos
        self.me = 4 * x + 2 * y + c

    @staticmethod
    def scratch(n):
        return [pltpu.SemaphoreType.DMA((n, N_DEV - 1)), pltpu.SemaphoreType.DMA((n, N_DEV - 1)), pltpu.SemaphoreType.DMA((n,))]

    @staticmethod
    def out_shape(bufs, scatter):
        return [jax.ShapeDtypeStruct(b.shape if scatter else (N_DEV,) + b.shape, b.dtype) for b in bufs]

    def _local(self, i):
        return pltpu.make_async_copy(self.x_refs[i].at[self.me] if self.scatter else self.x_refs[i], self.out_refs[i].at[self.me], self.local_sems.at[i])

    def _copy(self, i, k, landing):
        peer, peer_id = _peer(*self.pos, k)
        src = self.x_refs[i].at[peer_id] if self.scatter else self.x_refs[i]
        return pltpu.make_async_remote_copy(src_ref=src, dst_ref=self.out_refs[i].at[peer_id if landing else self.me],
                                            send_sem=self.send_sems.at[i, k - 1], recv_sem=self.recv_sems.at[i, k - 1],
                                            device_id=peer, device_id_type=pl.DeviceIdType.MESH)

    def start(self):
        for i in range(len(self.x_refs)):
            self._local(i).start()
        for k in range(1, N_DEV):
            for i in range(len(self.x_refs)):
                self._copy(i, k, False).start()

    def wait(self):
        for k in range(1, N_DEV):
            for i in range(len(self.x_refs)):
                self._copy(i, k, True).wait_recv()
        for k in range(1, N_DEV):
            for i in range(len(self.x_refs)):
                self._copy(i, k, False).wait_send()
        for i in range(len(self.x_refs)):
            self._local(i).wait()


def _exchange(name, bufs, scatter):
    n = len(bufs)

    def body(*refs):
        ex = Exchange(refs[:n], refs[n:2 * n], *refs[2 * n:], scatter)
        ex.start()
        ex.wait()

    return pl.pallas_call(body, in_specs=[_HBM] * n, out_specs=[_HBM] * n, out_shape=Exchange.out_shape(bufs, scatter),
                          scratch_shapes=Exchange.scratch(n), name=name)(*bufs)


def _f_rms(x, g):
    return (_rms(x, g),)


def _f_rms2(x, g1, g2):
    r = x * lax.rsqrt(jnp.sum(x * x, -1, keepdims=True) / x.shape[-1] + EPS)
    return r * g1, r * g2


def _f_out_gate(o, gate, gain):
    return (_rms(o, gain) * _silu(gate),)


def _f_gate(o, gate):
    return (o * _silu(gate),)


@jax.custom_vjp
def _swap_rope_halves(x):
    half = ROPE // 2
    lane = lax.broadcasted_iota(jnp.int32, x.shape, 1)
    return jnp.where(lane < half, pltpu.roll(x, HEAD - half, 1), jnp.where(lane < ROPE, pltpu.roll(x, half, 1), 0.0))


_swap_rope_halves.defvjp(lambda x: (_swap_rope_halves(x), None), lambda _, g: (_swap_rope_halves(g),))


def _f_qk_final(scale, nope, rope_in, g_nope, g_rope, cos, sin):
    ms = (jnp.sum(nope * nope, -1, keepdims=True) + jnp.sum(rope_in * rope_in, -1, keepdims=True)) / QK_DIM
    r = lax.rsqrt(ms + EPS)
    a = nope * r * g_nope
    b = rope_in * r * g_rope
    out = jnp.concatenate([a, b * cos + _swap_rope_halves(b) * sin], axis=1)
    return (out if scale == 1.0 else out * scale,)


def _rope_tables(lp):
    half = ROPE // 2
    pos = jnp.maximum(jnp.arange(lp) - PAD_ROWS, 0)
    inv = ROPE_THETA ** (-jnp.arange(half, dtype=F32) / half)
    ang = pos.astype(F32)[:, None] * inv[None, :]
    zeros = jnp.zeros((lp, HEAD - ROPE), F32)
    cos = jnp.concatenate([jnp.cos(ang), jnp.cos(ang), zeros], 1)
    sin = jnp.concatenate([-jnp.sin(ang), jnp.sin(ang), zeros], 1)
    return cos, sin


def _pad_lanes(w, width=HEAD):
    return jnp.pad(w, ((0, 0), (0, width - w.shape[1])))


def _pad_rows(w, rows=HEAD):
    return jnp.pad(w, ((0, rows - w.shape[0]), (0, 0)))


def _split_heads_qk_t(w_t):
    k = w_t.shape[1]
    w3 = w_t.reshape(N_HEADS, QK_DIM, k)
    nope = w3[:, :HEAD].reshape(N_HEADS * HEAD, k)
    rope = jnp.pad(w3[:, HEAD:], ((0, 0), (0, HEAD - ROPE), (0, 0))).reshape(N_HEADS * HEAD, k)
    return jnp.concatenate([nope, rope], 0)


def _merge_heads_qk_t(g_t):
    k = g_t.shape[1]
    kw = N_HEADS * HEAD
    nope, rope = g_t[:kw].reshape(N_HEADS, HEAD, k), g_t[kw:].reshape(N_HEADS, HEAD, k)[:, :ROPE]
    return jnp.concatenate([nope, rope], 1).reshape(N_HEADS * QK_DIM, k)


def local_step(x, target, w, deferred=None):
    nb, seq, d = x.shape
    lp = seq + LEAD
    t = nb * lp
    tr = _pick(lp, (544, 128))
    ntab = lp // tr
    mxu = _MXU_DTYPE
    kw = N_HEADS * HEAD

    a_w_in_t = w["a_w_in"].astype(mxu)
    w_qkv_t, w_ga_t, w_ba_t = a_w_in_t[:3 * kw], a_w_in_t[3 * kw:4 * kw], _pad_rows(a_w_in_t[4 * kw:])
    a_conv = w["a_conv"].T
    alog, dtb, o_gain = _pad_lanes(w["a_log"]), _pad_lanes(w["a_dt_bias"]), w["a_o_gain"]
    a_norm, kv_norm, b_norm = w["a_norm"], w["kv_norm"][None, :], w["b_norm"]
    lat_norm, qlat_norm = w["kv_latent_norm"][None, :], w["b_q_latent_norm"]
    kg_nope, kg_rope = w["k_gain"][None, :HEAD], _pad_lanes(w["k_gain"][None, HEAD:])
    qg_nope, qg_rope = w["b_q_gain"][:, :HEAD], _pad_lanes(w["b_q_gain"][:, HEAD:])
    cos, sin = _rope_tables(lp)

    meta = jnp.broadcast_to(w["meta_tokens"].T[None], (nb, N_META, d))
    h0 = jnp.concatenate([jnp.zeros((nb, PAD_ROWS, d), F32), meta, x], 1).reshape(t, d)
    (hn,) = row_call("a_norm_fwd", _f_rms, [Arg(h0), Arg(a_norm, "par")], [(d, mxu, d, False)], tr)
    z_qkv = matmul("a_in_qkv", hn, w_qkv_t, "nt")
    gate_a = matmul("a_in_gate", hn, w_ga_t, "nt")
    z_ba = matmul("a_in_ba", hn, w_ba_t, "nt")
    qkv_a, y_conv = conv_fwd(z_qkv, a_conv, lp)
    o_a, states, t_invs, *gathered = delta_fwd(qkv_a, z_ba, alog, dtb, lp, gather=deferred.gather_bufs if deferred else ())
    if deferred:
        w = {**w, **deferred.finish(gathered)}
    a_w_out = w["a_w_out"].astype(mxu)
    w_dkv, w_dpe = w["kv_w_down"][:, :KV_RANK].astype(mxu), _pad_lanes(w["kv_w_down"][:, KV_RANK:]).astype(mxu)
    w_ukv_t = jnp.concatenate([w["kv_w_uk"], w["kv_w_uv"]], 0).astype(mxu)
    b_w_in_t = w["b_w_in"].astype(mxu)
    w_cq_t, w_gb_t = b_w_in_t[:Q_RANK], b_w_in_t[Q_RANK:]
    w_q_t = _split_heads_qk_t(w["b_w_uq"]).astype(mxu)
    b_w_out = w["b_w_out"].astype(mxu)
    og_args = [Arg(o_a, bc=HEAD, ph=True, diff=True), Arg(gate_a, bc=HEAD, ph=True, diff=True, gdt=mxu), Arg(o_gain, "par", diff=True)]
    (og_a,) = row_call("a_out_gate_fwd", _f_out_gate, og_args, [(kw, mxu, HEAD, True)], tr, nh=N_HEADS)
    h1 = matmul("a_out", og_a, a_w_out, "nn", res=h0)

    hk, hb = row_call("b_norms_fwd", _f_rms2, [Arg(h1), Arg(kv_norm, "par"), Arg(b_norm, "par")], [(d, mxu, d, False), (d, mxu, d, False)], tr)
    c_kv_raw = matmul("kv_down", hk, w_dkv, "nn")
    k_pe = matmul("kv_down_pe", hk, w_dpe, "nn")
    c_q_raw = matmul("b_in_q", hb, w_cq_t, "nt")
    gate_b = matmul("b_in_gate", hb, w_gb_t, "nt")
    (c_kv,) = row_call("kv_latent_fwd", _f_rms, [Arg(c_kv_raw), Arg(lat_norm, "par")], [(KV_RANK, mxu, KV_RANK, False)], tr)
    (c_q,) = row_call("q_latent_fwd", _f_rms, [Arg(c_q_raw), Arg(qlat_norm, "par")], [(Q_RANK, mxu, Q_RANK, False)], tr)
    k_nope = matmul("k_up", c_kv, w_ukv_t[:kw], "nt")
    v_b = matmul("v_up", c_kv, w_ukv_t[kw:], "nt", out_dtype=mxu)
    q_up = matmul("q_up", c_q, w_q_t, "nt")
    tabs = [Arg(cos, "tab"), Arg(sin, "tab")]
    k_args = [Arg(k_nope, bc=HEAD, ph=True, diff=True, gdt=mxu), Arg(k_pe, diff=True), Arg(kg_nope, "par", diff=True), Arg(kg_rope, "par", diff=True)] + tabs
    q_args = [Arg(q_up, bc=HEAD, ph=True, diff=True, gdt=mxu), Arg(q_up, bc=HEAD, base=N_HEADS, ph=True, diff=True, gdt=mxu),
              Arg(qg_nope, "par", diff=True), Arg(qg_rope, "par", diff=True)] + tabs
    f_k_final, f_q_final = functools.partial(_f_qk_final, 1.0), functools.partial(_f_qk_final, ATT_SCALE)
    (k_fin,) = row_call("k_final_fwd", f_k_final, k_args, [(N_HEADS * QK_PAD, mxu, QK_PAD, True)], tr, nh=N_HEADS, ntab=ntab)
    (q_fin,) = row_call("q_final_fwd", f_q_final, q_args, [(N_HEADS * QK_PAD, mxu, QK_PAD, True)], tr, nh=N_HEADS, ntab=ntab)
    o_b, lse = flash_fwd(q_fin, k_fin, v_b, lp)
    gb_args = [Arg(o_b, diff=True), Arg(gate_b, diff=True, gdt=mxu)]
    (og_b,) = row_call("b_gate_fwd", _f_gate, gb_args, [(kw, mxu, kw, False)], tr)
    h2 = matmul("b_out", og_b, b_w_out, "nn", res=h1)

    loss, dh2 = loss_head(h2.reshape(nb, lp, d), target, lp)
    dh2 = dh2.reshape(t, d)
    grads = {}

    d_og_b = matmul("b_out_dx", dh2, b_w_out, "nt")
    grads["b_w_out"] = matmul("b_out_dw", og_b, dh2, "tn")
    d_o_b, d_gate_b = row_vjp_call("b_gate_bwd", _f_gate, gb_args, [Arg(d_og_b)], tr)
    dq_fin, dk_fin, dv_b = flash_bwd(q_fin, k_fin, v_b, o_b, lse, d_o_b, lp)
    dq_nope, dq_rope, d_qg_nope, d_qg_rope = row_vjp_call(
        "q_final_bwd", f_q_final, q_args, [Arg(dq_fin, bc=QK_PAD, ph=True)], tr, nh=N_HEADS, ntab=ntab)
    dk_nope, dk_pe, d_kg_nope, d_kg_rope = row_vjp_call(
        "k_final_bwd", f_k_final, k_args, [Arg(dk_fin, bc=QK_PAD, ph=True)], tr, nh=N_HEADS, ntab=ntab)
    grads["b_q_gain"] = jnp.concatenate([d_qg_nope, d_qg_rope[:, :ROPE]], 1)
    grads["k_gain"] = jnp.concatenate([d_kg_nope, d_kg_rope[:, :ROPE]], 1)[0]
    d_c_q = matmul("q_nope_dx", dq_nope, w_q_t[:kw], "nn")
    d_c_q = matmul("q_rope_dx", dq_rope, w_q_t[kw:], "nn", res=d_c_q)
    grads["b_w_uq"] = _merge_heads_qk_t(jnp.concatenate([matmul("q_nope_dw", dq_nope, c_q, "tn"), matmul("q_rope_dw", dq_rope, c_q, "tn")], 0))
    d_c_kv = matmul("k_up_dx", dk_nope, w_ukv_t[:kw], "nn")
    d_c_kv = matmul("v_up_dx", dv_b, w_ukv_t[kw:], "nn", res=d_c_kv)
    grads["kv_w_uk"], grads["kv_w_uv"] = matmul("k_up_dw", dk_nope, c_kv, "tn"), matmul("v_up_dw", dv_b, c_kv, "tn")
    d_c_q_raw, grads["b_q_latent_norm"] = row_vjp_call(
        "q_latent_bwd", _f_rms, [Arg(c_q_raw, diff=True, gdt=mxu), Arg(qlat_norm, "par", diff=True)], [Arg(d_c_q)], tr)
    d_c_kv_raw, d_lat = row_vjp_call(
        "kv_latent_bwd", _f_rms, [Arg(c_kv_raw, diff=True, gdt=mxu), Arg(lat_norm, "par", diff=True)], [Arg(d_c_kv)], tr)
    grads["kv_latent_norm"] = d_lat[0]
    d_hb = matmul("b_in_q_dx", d_c_q_raw, w_cq_t, "nn")
    d_hb = matmul("b_in_gate_dx", d_gate_b, w_gb_t, "nn", res=d_hb)
    grads["b_w_in"] = jnp.concatenate([matmul("b_in_q_dw", d_c_q_raw, hb, "tn"), matmul("b_in_gate_dw", d_gate_b, hb, "tn")], 0)
    d_hk = matmul("kv_down_dx", d_c_kv_raw, w_dkv, "nt")
    d_hk = matmul("kv_down_pe_dx", dk_pe, w_dpe, "nt", res=d_hk)
    grads["kv_w_down"] = jnp.concatenate([matmul("kv_down_dw", hk, d_c_kv_raw, "tn"), matmul("kv_down_pe_dw", hk, dk_pe, "tn")[:, :ROPE]], 1)
    dh1, d_kv_norm, grads["b_norm"] = row_vjp_call(
        "b_norms_bwd", lambda x_, g1, g2: _f_rms2(x_, g1, g2) + (x_,),
        [Arg(h1, diff=True), Arg(kv_norm, "par", diff=True), Arg(b_norm, "par", diff=True)], [Arg(d_hk), Arg(d_hb), Arg(dh2)], tr)
    grads["kv_norm"] = d_kv_norm[0]

    d_og_a = matmul("a_out_dx", dh1, a_w_out, "nt")
    grads["a_w_out"] = matmul("a_out_dw", og_a, dh1, "tn")
    d_o_a, d_gate_a, grads["a_o_gain"] = row_vjp_call(
        "a_out_gate_bwd", _f_out_gate, og_args, [Arg(d_og_a, bc=HEAD, ph=True)], tr, nh=N_HEADS)
    dqkv_a, d_ba, d_alog, d_dtb, *received = delta_bwd(qkv_a, z_ba, alog, dtb, states, t_invs, d_o_a, lp,
                                                        scatter=deferred.scatter_bufs(grads) if deferred else ())
    grads["a_log"], grads["a_dt_bias"] = d_alog[:, :N_HEADS], d_dtb[:, :N_HEADS]
    dz_qkv, d_conv = conv_bwd(z_qkv, y_conv, a_conv, dqkv_a, lp)
    grads["a_conv"] = d_conv.T
    grads["a_w_in"] = jnp.concatenate([matmul("a_in_qkv_dw", dz_qkv, hn, "tn"), matmul("a_in_gate_dw", d_gate_a, hn, "tn"),
                                       matmul("a_in_ba_dw", d_ba, hn, "tn")[:2 * N_HEADS]], 0)
    ride = deferred.last_scatter_bufs(grads) if deferred else ()
    d_hn = matmul("a_in_qkv_dx", dz_qkv, w_qkv_t, "nn", scatter=ride)
    if ride:
        d_hn, *received_last = d_hn
        received = list(received) + received_last
    d_hn = matmul("a_in_gate_dx", d_gate_a, w_ga_t, "nn", res=d_hn)
    d_hn = matmul("a_in_ba_dx", d_ba, w_ba_t, "nn", res=d_hn)
    dh0, grads["a_norm"] = row_vjp_call("a_norm_bwd", lambda x_, g_: _f_rms(x_, g_) + (x_,),
                                        [Arg(h0, diff=True), Arg(a_norm, "par", diff=True)], [Arg(d_hn), Arg(dh1)], tr)
    dh0 = dh0.reshape(nb, lp, d)
    grads["meta_tokens"] = meta_grad(dh0).T
    return loss, dh0[:, LEAD:], grads, received


_SHARDED = (
    ("meta_tokens", True, False), ("a_norm", True, False), ("a_w_in", True, True), ("a_conv", True, False), ("a_w_out", False, True),
    ("kv_w_down", False, True), ("kv_w_uk", True, True), ("kv_w_uv", True, True), ("b_w_in", True, True), ("b_w_uq", True, True),
    ("b_w_out", False, True))
_REPLICATED = ("a_log", "a_dt_bias", "a_o_gain", "kv_norm", "kv_latent_norm", "k_gain", "b_norm", "b_q_latent_norm", "b_q_gain")
_ALL_WEIGHTS = ("meta_tokens", "a_norm", "a_w_in", "a_conv", "a_log", "a_dt_bias", "a_o_gain", "a_w_out", "kv_norm", "kv_w_down",
                "kv_latent_norm", "kv_w_uk", "kv_w_uv", "k_gain", "b_norm", "b_w_in", "b_q_latent_norm", "b_w_uq", "b_q_gain", "b_w_out")


def _round_up(n, m):
    return (n + m - 1) // m * m


def _pack_rows(pieces, row_multiple):
    padded = []
    for p in pieces:
        n = p.shape[-1]
        padded.append(jnp.pad(p, [(0, 0)] * (p.ndim - 1) + [(0, _round_up(n, PACK_COLS) - n)]))
    flat = jnp.concatenate(padded, -1)
    rows = _round_up(flat.shape[-1] // PACK_COLS, row_multiple)
    flat = jnp.pad(flat, [(0, 0)] * (flat.ndim - 1) + [(0, rows * PACK_COLS - flat.shape[-1])])
    return flat.reshape(flat.shape[:-1] + (rows, PACK_COLS))


def _unpack_rows(buf, sizes):
    flat = buf.reshape(buf.shape[:-2] + (-1,))
    out, off = [], 0
    for n in sizes:
        out.append(flat[..., off:off + n])
        off += _round_up(n, PACK_COLS)
    return out


def _shard_2d(a):
    return a.reshape(a.shape[-2:]) if a.ndim > 2 else a


def _kl_shard(a, by_cols):
    return _shard_2d(a).T if by_cols else _shard_2d(a)


_GROUPS_FIRST = (("a_w_in",),)
_GROUPS_LATER = (("a_w_out", "b_w_in", "b_w_out"), ("b_w_uq",), ("kv_w_down",), ("kv_w_uk", "kv_w_uv"))
_SMALL_SHARDED = ("meta_tokens", "a_norm", "a_conv")
_BY_COLS = {name: by_cols for name, by_cols, _ in _SHARDED}
ROW_ALIGN = 16


def _stack_rows(pieces):
    padded, starts, row = [], [], 0
    for p in pieces:
        r = p.shape[-2]
        padded.append(jnp.pad(p, [(0, 0)] * (p.ndim - 2) + [(0, _round_up(r, ROW_ALIGN) - r), (0, 0)]))
        starts.append(row)
        row += _round_up(r, ROW_ALIGN)
    return jnp.concatenate(padded, -2), starts


def _stack_group(arrays_by_name, names):
    arrays = [arrays_by_name[n].astype(BF16) for n in names]
    buf, starts = _stack_rows(arrays)
    return buf, [(n, s, a.shape[-2]) for n, s, a in zip(names, starts, arrays, strict=True)]


def _stack_groups(arrays_by_name, groups):
    stacked = [_stack_group(arrays_by_name, names) for names in groups]
    return [b for b, _ in stacked], [entries for _, entries in stacked]


def _full_from_gathered(gathered, layout):
    full = {}
    for got, entries in zip(gathered, layout, strict=True):
        for name, start, rows in entries:
            full[name] = got[:, start:start + rows].reshape(N_DEV * rows, got.shape[-1])
    return full


def gather_first_weights(local):
    shards = {n: _kl_shard(local[n], _BY_COLS[n]) for names in _GROUPS_FIRST for n in names}
    bufs, layout = _stack_groups(shards, _GROUPS_FIRST)
    small = [_kl_shard(local[n], _BY_COLS[n]) for n in _SMALL_SHARDED]
    bufs.append(_pack_rows([s.reshape(-1) for s in small], 8))
    gathered = _exchange("all_gather", bufs, scatter=False)
    full = _full_from_gathered(gathered[:-1], layout)
    for name, part, sh in zip(_SMALL_SHARDED, _unpack_rows(gathered[-1], [s.size for s in small]), small, strict=True):
        full[name] = part.reshape(N_DEV * sh.shape[0], sh.shape[1])
    full["a_norm"] = full["a_norm"].reshape(1, -1)
    return full


class LaterExchanges:
    def __init__(self, local):
        shards = {n: _kl_shard(local[n], _BY_COLS[n]) for names in _GROUPS_LATER for n in names}
        self.gather_bufs, self.layout = _stack_groups(shards, _GROUPS_LATER)

    def finish(self, gathered):
        return _full_from_gathered(gathered, self.layout)

    def scatter_bufs(self, grads):
        return _stack_groups(_owner_slices(grads, _GROUPS_LATER), _GROUPS_LATER)[0]

    def last_scatter_bufs(self, grads):
        bufs, self.last_layout = _stack_groups(_owner_slices(grads, _GROUPS_FIRST), _GROUPS_FIRST)
        return bufs


def _owner_slices(grads, groups):
    return {n: grads[n].reshape(N_DEV, -1, grads[n].shape[-1]) for names in groups for n in names}


def reduce_contributions(name, recv):
    _, r, c = recv.shape
    tr = _pick(r, (256, 128, 64, 32, 16, 8))

    def body(g_ref, o_ref):
        g = g_ref[0].astype(F32)
        for dev in range(1, N_DEV):
            g = g + g_ref[dev].astype(F32)
        o_ref[...] = g

    return pl.pallas_call(
        body, grid=(r // tr,), in_specs=[pl.BlockSpec((N_DEV, tr, c), lambda i: (0, i, 0))], out_specs=pl.BlockSpec((tr, c), lambda i: (i, 0)),
        out_shape=jax.ShapeDtypeStruct((r, c), F32), compiler_params=_cparams(("arbitrary",)), name=name)(recv)


def adamw_all(gs, ws, ms, vs):
    n = len(gs)

    def body(*refs):
        for i in range(n):
            g_ref, w_ref, m_ref, v_ref = (refs[j * n + i] for j in range(4))
            d_ref, mo_ref, vo_ref = (refs[(4 + j) * n + i] for j in range(3))
            g = g_ref[...]
            m_new = ADAM_B1 * m_ref[...] + (1.0 - ADAM_B1) * g
            v_new = ADAM_B2 * v_ref[...] + (1.0 - ADAM_B2) * (g * g)
            m_hat = m_new / (1.0 - ADAM_B1 ** ADAM_STEP)
            v_hat = v_new / (1.0 - ADAM_B2 ** ADAM_STEP)
            d_ref[...] = -ADAM_LR * (m_hat / (jnp.sqrt(v_hat) + ADAM_EPS) + ADAM_WD * w_ref[...])
            mo_ref[...] = m_new
            vo_ref[...] = v_new

    out = [jax.ShapeDtypeStruct(g.shape, F32) for g in gs] * 3
    res = pl.pallas_call(body, out_shape=out, compiler_params=pltpu.CompilerParams(vmem_limit_bytes=VMEM_LIMIT), name="adamw_all")(*gs, *ws, *ms, *vs)
    return res[:n], res[n:2 * n], res[2 * n:]


def kernel(x, meta_tokens, a_norm, a_w_in, a_conv, a_log, a_dt_bias, a_o_gain, a_w_out, kv_norm, kv_w_down, kv_latent_norm, kv_w_uk, kv_w_uv, k_gain, b_norm, b_w_in, b_q_latent_norm, b_w_uq, b_q_gain, b_w_out, loss_target, m_meta_tokens, m_a_norm, m_a_w_in, m_a_conv, m_a_log, m_a_dt_bias, m_a_o_gain, m_a_w_out, m_kv_norm, m_kv_w_down, m_kv_latent_norm, m_kv_w_uk, m_kv_w_uv, m_k_gain, m_b_norm, m_b_w_in, m_b_q_latent_norm, m_b_w_uq, m_b_q_gain, m_b_w_out, v_meta_tokens, v_a_norm, v_a_w_in, v_a_conv, v_a_log, v_a_dt_bias, v_a_o_gain, v_a_w_out, v_kv_norm, v_kv_w_down, v_kv_latent_norm, v_kv_w_uk, v_kv_w_uv, v_k_gain, v_b_norm, v_b_w_in, v_b_q_latent_norm, v_b_w_uq, v_b_q_gain, v_b_w_out):
    given = dict(locals())
    local_w = {n: given[n] for n in _ALL_WEIGHTS}
    full = gather_first_weights(local_w)
    for n in _REPLICATED:
        full[n] = local_w[n]
    later = LaterExchanges(local_w)

    loss_part, grad_x, grads, received_riding = local_step(x, loss_target, full, later)

    exact = [grads[n].reshape(N_DEV, -1) for n in _SMALL_SHARDED]
    exact += [jnp.broadcast_to(grads[n].reshape(1, -1), (N_DEV, grads[n].size)) for n in _REPLICATED]
    exact.append(jnp.broadcast_to(loss_part, (N_DEV, 1)))
    received = list(received_riding) + list(_exchange("all_to_all", [_pack_rows(exact, 8)], scatter=True))
    layout = later.layout + later.last_layout
    summed = [reduce_contributions(f"reduce_{i}", r) for i, r in enumerate(received)]

    grad_kl = {}
    for got, entries in zip(summed, layout):
        for n, start, rows in entries:
            grad_kl[n] = got[start:start + rows]
    parts = _unpack_rows(summed[-1], [p.shape[1] for p in exact])
    for n, part in zip(_SMALL_SHARDED + _REPLICATED, parts, strict=False):
        grad_kl[n] = part
    loss = parts[-1][0]

    def natural_2d(n, a):
        shape = _shard_2d(local_w[n]).shape if local_w[n].ndim > 1 else (1, local_w[n].size)
        return a.reshape(shape[::-1]).T if _BY_COLS.get(n, False) else a.reshape(shape)

    as_2d = lambda n, a: a.reshape(natural_2d(n, grad_kl[n]).shape)
    gs = [natural_2d(n, grad_kl[n]) for n in _ALL_WEIGHTS]
    deltas, new_m, new_v = adamw_all(gs, [as_2d(n, local_w[n]) for n in _ALL_WEIGHTS], [as_2d(n, given["m_" + n]) for n in _ALL_WEIGHTS],
                                     [as_2d(n, given["v_" + n]) for n in _ALL_WEIGHTS])
    results = [a.reshape(local_w[n].shape) for group in (gs, deltas, new_m, new_v) for n, a in zip(_ALL_WEIGHTS, group, strict=True)]
    return (loss, grad_x, *results)
```

```python
import dataclasses
import functools
import math

import jax
import jax.numpy as jnp
from jax import lax
from jax.experimental import pallas as pl
from jax.experimental.pallas import tpu as pltpu

F32 = jnp.float32
BF16 = jnp.bfloat16
_MXU_DTYPE = jnp.bfloat16

N_DEV = 8
D_MODEL = 1024
N_HEADS = 8
HEAD = 128
CHUNK = 64
N_META = 16
PAD_ROWS = 2 * CHUNK - N_META
LEAD = PAD_ROWS + N_META
ROPE = 64
QK_DIM = HEAD + ROPE
QK_PAD = 2 * HEAD
KV_RANK = 256
Q_RANK = 384
CONV_K = 4
EPS = 1e-6
NEG = -1e30
ROPE_THETA = 10000.0
ADAM_LR, ADAM_B1, ADAM_B2, ADAM_EPS, ADAM_WD, ADAM_STEP = 0.001, 0.9, 0.999, 1e-08, 0.01, 10
PACK_COLS = 512
VMEM_LIMIT = 56 * 1024 * 1024


def _pick(n, options):
    for o in options:
        if n % o == 0:
            return o
    raise ValueError(f"no tile for {n} among {options}")


def _cparams(sem):
    return pltpu.CompilerParams(dimension_semantics=sem, vmem_limit_bytes=VMEM_LIMIT)


def _dims(a, dims):
    if a.ndim == 2:
        return (dims, ((), ()))
    (ca,), (cb,) = dims
    return (((ca + 1,), (cb + 1,)), ((0,), (0,)))


def _dot(a, b, dims):
    return lax.dot_general(a.astype(_MXU_DTYPE), b.astype(_MXU_DTYPE), _dims(a, dims), preferred_element_type=F32)


@jax.custom_vjp
def mm_nn(a, b):
    return _dot(a, b, ((1,), (0,)))


@jax.custom_vjp
def mm_nt(a, b):
    return _dot(a, b, ((1,), (1,)))


@jax.custom_vjp
def mm_tn(a, b):
    return _dot(a, b, ((0,), (0,)))


mm_nn.defvjp(lambda a, b: (mm_nn(a, b), (a, b)), lambda r, g: (mm_nt(g, r[1]), mm_tn(r[0], g)))
mm_nt.defvjp(lambda a, b: (mm_nt(a, b), (a, b)), lambda r, g: (mm_nn(g, r[1]), mm_tn(g, r[0])))
mm_tn.defvjp(lambda a, b: (mm_tn(a, b), (a, b)), lambda r, g: (mm_nt(r[1], g), mm_nn(r[0], g)))


def _split_terms(x, n):
    terms, rest = [], x
    for _ in range(n):
        t = rest.astype(_MXU_DTYPE)
        terms.append(t)
        rest = rest - t.astype(F32)
    return terms


def _dot_01_raw(m, x, dims):
    m = m.astype(_MXU_DTYPE)
    return sum(lax.dot_general(m, t, _dims(m, dims), preferred_element_type=F32) for t in _split_terms(x, 3))


@jax.custom_vjp
def _dot_01(m, x):
    return _dot_01_raw(m, x, ((1,), (0,)))


_dot_01.defvjp(lambda m, x: (_dot_01(m, x), m), lambda m, g: (jnp.zeros_like(m), _dot_01_raw(m, g, ((0,), (0,)))))


def _inv_unit_lower(a):
    n = a.shape[-1]
    eye = (lax.broadcasted_iota(jnp.int32, (n, n), 0) == lax.broadcasted_iota(jnp.int32, (n, n), 1)).astype(F32)
    d = lambda u, w: lax.dot_general(u, w, _dims(u, ((1,), (0,))), preferred_element_type=F32)
    t = eye - a
    p = a.astype(_MXU_DTYPE)
    p = d(p, p)
    squarings = int(math.log2(n)) - 1
    for s in range(squarings):
        ph = p.astype(_MXU_DTYPE)
        t_hi, t_lo = _split_terms(t, 2)
        t = t + (d(t_hi, ph) + d(t_lo, ph))
        if s + 1 < squarings:
            p = d(ph, ph)
    return t


@jax.custom_vjp
def _inv_lookup(a, t):
    return t


def _inv_lookup_bwd(t, g):
    return -mm_tn(t, mm_nt(g, t)), jnp.zeros_like(t)


_inv_lookup.defvjp(lambda a, t: (t, t), _inv_lookup_bwd)


def _sigmoid(x):
    return 1.0 / (1.0 + jnp.exp(-x))


@jax.custom_vjp
def _silu(x):
    return x * _sigmoid(x)


def _silu_fwd(x):
    s = _sigmoid(x)
    return x * s, (x, s)


_silu.defvjp(_silu_fwd, lambda r, g: (g * (r[1] * (1.0 + r[0] * (1.0 - r[1]))),))


def _softplus(x):
    return jnp.where(x > 20.0, x, jnp.log(1.0 + jnp.exp(jnp.minimum(x, 20.0))))


def _rms(x, g, width=None):
    ms = jnp.sum(x * x, -1, keepdims=True) / (x.shape[-1] if width is None else width)
    return x * lax.rsqrt(ms + EPS) * g


MM_VMEM_BUDGET = 40 * 1024 * 1024


def _matmul_rows(name, a, b, mode, out_dtype, res, scatter):
    m, k = a.shape
    n = b.shape[1] if mode == "nn" else b.shape[0]
    dims = {"nn": ((1,), (0,)), "nt": ((1,), (1,))}[mode]
    out_bytes = jnp.dtype(out_dtype).itemsize
    n_in, nx = 2 + (res is not None), len(scatter)

    def vmem(tm):
        blocks = 2 * tm * k * a.dtype.itemsize + 2 * k * n * b.dtype.itemsize + 2 * tm * n * out_bytes + tm * n * 4
        return blocks + (2 * tm * n * res.dtype.itemsize if res is not None else 0)

    tm = next(c for c in (2176, 1088, 512, 256, 128, 64) if m % c == 0 and vmem(c) <= MM_VMEM_BUDGET)
    steps = m // tm

    def body(*refs):
        a_ref, b_ref, o_ref = refs[0], refs[1], refs[n_in + nx]
        i = pl.program_id(0)
        finish = _ride(scatter, True, refs[n_in:n_in + nx], refs[n_in + nx + 1:n_in + 2 * nx + 1], refs[n_in + 2 * nx + 1:], i == 0, i == steps - 1)
        out = _dot(a_ref[...], b_ref[...], dims)
        if res is not None:
            out = out + refs[2][...].astype(F32)
        o_ref[...] = out.astype(o_ref.dtype)
        finish()

    o_spec = pl.BlockSpec((tm, n), lambda i: (i, 0))
    in_specs = [pl.BlockSpec((tm, k), lambda i: (i, 0)), pl.BlockSpec(b.shape, lambda i: (0, 0))] + ([o_spec] if res is not None else [])
    args = (a, b) + ((res,) if res is not None else ())
    out = pl.pallas_call(
        body, grid=(steps,), in_specs=in_specs + [_HBM] * nx, out_specs=[o_spec] + [_HBM] * nx,
        out_shape=[jax.ShapeDtypeStruct((m, n), out_dtype)] + Exchange.out_shape(scatter, True), scratch_shapes=Exchange.scratch(nx) if nx else [],
        compiler_params=_cparams(("arbitrary",) if nx else ("parallel",)), name=name)(*args, *scatter)
    return out if nx else out[0]


def matmul(name, a, b, mode, out_dtype=F32, res=None, scatter=()):
    if mode != "tn":
        return _matmul_rows(name, a, b, mode, out_dtype, res, scatter)
    (k, m), (k2, n) = a.shape, b.shape
    assert k == k2 and res is None, (name, a.shape, b.shape, mode)
    tm = _pick(m, (1024, 512, 384, 256, 128))
    tn = _pick(n, (1024, 512, 384, 256, 128))
    tk = _pick(k, (512, 256, 128))
    nk = k // tk
    dims = ((0,), (0,))

    def body(*refs):
        if res is None:
            a_ref, b_ref, o_ref, acc_ref = refs
        else:
            a_ref, b_ref, r_ref, o_ref, acc_ref = refs
        kk = pl.program_id(2)

        @pl.when(kk == 0)
        def _():
            acc_ref[...] = jnp.zeros_like(acc_ref)

        acc_ref[...] += _dot(a_ref[...], b_ref[...], dims)

        @pl.when(kk == nk - 1)
        def _():
            out = acc_ref[...]
            if res is not None:
                out = out + r_ref[...].astype(F32)
            o_ref[...] = out.astype(o_ref.dtype)

    a_spec = pl.BlockSpec((tk, tm), lambda i, j, kk: (kk, i)) if mode == "tn" else pl.BlockSpec((tm, tk), lambda i, j, kk: (i, kk))
    b_spec = pl.BlockSpec((tn, tk), lambda i, j, kk: (j, kk)) if mode == "nt" else pl.BlockSpec((tk, tn), lambda i, j, kk: (kk, j))
    o_spec = pl.BlockSpec((tm, tn), lambda i, j, kk: (i, j))
    in_specs = [a_spec, b_spec] + ([o_spec] if res is not None else [])
    args = (a, b) + ((res,) if res is not None else ())
    return pl.pallas_call(
        body, grid=(m // tm, n // tn, nk), in_specs=in_specs, out_specs=o_spec,
        out_shape=jax.ShapeDtypeStruct((m, n), out_dtype), scratch_shapes=[pltpu.VMEM((tm, tn), F32)],
        compiler_params=_cparams(("parallel", "parallel", "arbitrary")), name=name)(*args)


@dataclasses.dataclass
class Arg:
    arr: jax.Array
    kind: str = "row"
    bc: int = 0
    base: int = 0
    ph: bool = False
    diff: bool = False
    gdt: object = F32


def _arg_spec(a, tr, nh, ntab, base=None):
    bc = a.bc or a.arr.shape[1]
    base = a.base if base is None else base
    width = bc * nh if a.ph else bc
    col = base // nh if a.ph else base
    assert not a.ph or base % nh == 0
    if a.kind == "row":
        return pl.BlockSpec((tr, width), lambda i: (i, col))
    if a.kind == "tab":
        return pl.BlockSpec((tr, width), lambda i: (i % ntab, col))
    return pl.BlockSpec((a.arr.shape[0], width), lambda i: (0, col))


def _head_view(ref, a, h):
    bc = a.bc or a.arr.shape[1]
    v = ref[:, h * bc:(h + 1) * bc] if a.ph else ref[...]
    return v.astype(F32) if jnp.issubdtype(v.dtype, jnp.floating) else v


def row_call(name, fn, args, outs, tr, nh=1, ntab=1):
    t = args[0].arr.shape[0]
    n_in = len(args)
    out_args = [Arg(None, "row", bc, 0, ph) for (_, _, bc, ph) in outs]

    def body(*refs):
        for h in range(nh):
            res = fn(*[_head_view(r, a, h) for r, a in zip(refs[:n_in], args, strict=True)])
            for r, a, v in zip(refs[n_in:], out_args, res, strict=True):
                if a.ph:
                    r[:, h * a.bc:(h + 1) * a.bc] = v.astype(r.dtype)
                elif h == nh - 1:
                    r[...] = v.astype(r.dtype)

    return pl.pallas_call(
        body, grid=(t // tr,), in_specs=[_arg_spec(a, tr, nh, ntab) for a in args], out_specs=[_arg_spec(a, tr, nh, ntab) for a in out_args],
        out_shape=[jax.ShapeDtypeStruct((t, cols), dt) for (cols, dt, _, _) in outs],
        compiler_params=_cparams(("arbitrary",)), name=name)(*[a.arr for a in args])


def row_vjp_call(name, fn, args, cts, tr, nh=1, ntab=1):
    t = args[0].arr.shape[0]
    n_in, n_ct = len(args), len(cts)
    diff_idx = [k for k, a in enumerate(args) if a.diff]

    def body(*refs):
        out_refs = refs[n_in + n_ct:]
        shared = [None] * len(diff_idx)
        for k, r in zip(diff_idx, out_refs, strict=True):
            if args[k].kind == "par":
                @pl.when(pl.program_id(0) == 0)
                def _(r=r):
                    r[...] = jnp.zeros_like(r)

        for h in range(nh):
            vals = [_head_view(r, a, h) for r, a in zip(refs[:n_in], args, strict=True)]
            ct_vals = tuple(_head_view(r, a, h) for r, a in zip(refs[n_in:n_in + n_ct], cts, strict=True))

            def f(*dv, vals=vals):
                full = list(vals)
                for k, v in zip(diff_idx, dv, strict=True):
                    full[k] = v
                return tuple(fn(*full))

            _, vjp = jax.vjp(f, *[vals[k] for k in diff_idx])
            for j, (k, r, g) in enumerate(zip(diff_idx, out_refs, vjp(ct_vals), strict=True)):
                a = args[k]
                bc = a.bc or a.arr.shape[1]
                if not a.ph:
                    shared[j] = g if shared[j] is None else shared[j] + g
                elif a.kind == "row":
                    r[:, h * bc:(h + 1) * bc] = g.astype(r.dtype)
                else:
                    r[:, h * bc:(h + 1) * bc] += g
        for j, (k, r) in enumerate(zip(diff_idx, out_refs, strict=True)):
            if not args[k].ph:
                if args[k].kind == "row":
                    r[...] = shared[j].astype(r.dtype)
                else:
                    r[...] += shared[j]

    out_specs, out_shape = [], []
    for k in diff_idx:
        a = args[k]
        bc = a.bc or a.arr.shape[1]
        out_specs.append(_arg_spec(a, tr, nh, ntab, base=0))
        out_shape.append(jax.ShapeDtypeStruct((t if a.kind == "row" else a.arr.shape[0], bc * (nh if a.ph else 1)), a.gdt if a.kind == "row" else F32))
    in_specs = [_arg_spec(a, tr, nh, ntab) for a in list(args) + list(cts)]
    return pl.pallas_call(
        body, grid=(t // tr,), in_specs=in_specs, out_specs=out_specs, out_shape=out_shape,
        compiler_params=_cparams(("arbitrary",)), name=name)(*[a.arr for a in list(args) + list(cts)])


def _conv_taps(x, w):
    rows = lax.broadcasted_iota(jnp.int32, x.shape, 0)
    y = x * w[CONV_K - 1:CONV_K, :]
    for s in range(1, CONV_K):
        y = y + jnp.where(rows >= s, pltpu.roll(x, s, 0), 0.0) * w[CONV_K - 1 - s:CONV_K - s, :]
    return y


CONV_HEADS = 4
CONV_BLOCKS_PER_THIRD = N_HEADS // CONV_HEADS


def _conv_post(y, block):
    a = _silu(y)
    normed = block < 2 * CONV_BLOCKS_PER_THIRD
    scale = jnp.where(block < CONV_BLOCKS_PER_THIRD, HEAD ** -0.5, 1.0)
    return a * jnp.where(normed, lax.rsqrt(jnp.sum(a * a, -1, keepdims=True) + EPS) * scale, 1.0)


def conv_fwd(z, w, lp):
    t, width = z.shape
    cols = CONV_HEADS * HEAD

    def body(z_ref, w_ref, o_ref, y_ref):
        block = pl.program_id(1)
        for h in range(CONV_HEADS):
            cs = slice(h * HEAD, (h + 1) * HEAD)
            y = _conv_taps(z_ref[:, cs], w_ref[:, cs])
            y_ref[:, cs] = y
            o_ref[:, cs] = _conv_post(y, block)

    blk = pl.BlockSpec((lp, cols), lambda b, j: (b, j))
    out = jax.ShapeDtypeStruct((t, width), F32)
    return pl.pallas_call(
        body, grid=(t // lp, width // cols), in_specs=[blk, pl.BlockSpec((CONV_K, cols), lambda b, j: (0, j))],
        out_specs=[blk, blk], out_shape=[out, out], compiler_params=_cparams(("arbitrary", "arbitrary")), name="a_conv_fwd")(z, w)


def conv_bwd(z, y, w, dout, lp):
    t, width = z.shape
    cols = CONV_HEADS * HEAD

    def body(z_ref, y_ref, w_ref, g_ref, dz_ref, dw_ref):
        block = pl.program_id(0)

        @pl.when(pl.program_id(1) == 0)
        def _():
            dw_ref[...] = jnp.zeros_like(dw_ref)

        for h in range(CONV_HEADS):
            cs = slice(h * HEAD, (h + 1) * HEAD)
            x, wv = z_ref[:, cs], w_ref[:, cs]
            _, vjp = jax.vjp(lambda y_: _conv_post(y_, block), y_ref[:, cs])
            (dy,) = vjp(g_ref[:, cs])
            rows = lax.broadcasted_iota(jnp.int32, x.shape, 0)
            dx = dy * wv[CONV_K - 1:CONV_K, :]
            dw_ref[CONV_K - 1:CONV_K, cs] += jnp.sum(dy * x, axis=0, keepdims=True)
            for s in range(1, CONV_K):
                dy_up = jnp.where(rows < lp - s, pltpu.roll(dy, lp - s, 0), 0.0)
                dx = dx + dy_up * wv[CONV_K - 1 - s:CONV_K - s, :]
                dw_ref[CONV_K - 1 - s:CONV_K - s, cs] += jnp.sum(dy_up * x, axis=0, keepdims=True)
            dz_ref[:, cs] = dx.astype(dz_ref.dtype)

    blk = pl.BlockSpec((lp, cols), lambda j, b: (b, j))
    w_blk = pl.BlockSpec((CONV_K, cols), lambda j, b: (0, j))
    return pl.pallas_call(
        body, grid=(width // cols, t // lp), in_specs=[blk, blk, w_blk, blk], out_specs=[blk, w_blk],
        out_shape=[jax.ShapeDtypeStruct((t, width), _MXU_DTYPE), jax.ShapeDtypeStruct((CONV_K, width), F32)],
        compiler_params=_cparams(("arbitrary", "arbitrary")), name="a_conv_bwd")(z, y, w, dout)


def _delta_chunk(q, k, v, ba, alog, dtb, state, t_stored):
    n_g, c = q.shape[0], q.shape[1]
    lane = lax.broadcasted_iota(jnp.int32, (1, HEAD), 1)

    def pick(xs, offset):
        cols = [jnp.sum(xs[i // N_HEADS if len(xs) > 1 else 0] * (lane == offset + i % N_HEADS).astype(F32), axis=1, keepdims=True)[None]
                for i in range(n_g)]
        return jnp.concatenate(cols, 0)

    b_raw, a_raw = pick(ba, 0), pick(ba, N_HEADS)
    a_log, dt_bias = pick((alog,), 0), pick((dtb,), 0)
    beta = _sigmoid(b_raw)
    g = -jnp.exp(a_log) * _softplus(a_raw + dt_bias)
    ri = lax.broadcasted_iota(jnp.int32, (c, c), 0)
    ci = lax.broadcasted_iota(jnp.int32, (c, c), 1)
    tril = ci <= ri
    lower = jnp.broadcast_to(tril.astype(F32), (n_g, c, c))
    gc_col = _dot_01(lower, g * jnp.ones((1, 1, HEAD), F32))[:, :, :1]
    gc_row = _dot_01(jnp.ones((n_g, 8, c), F32), g * (ri <= ci).astype(F32)[None])[:, 0:1, :]
    gc_last = jnp.sum(g, axis=1, keepdims=True)
    decay = jnp.exp(jnp.where(tril, gc_col - gc_row, NEG))
    e_gc = jnp.exp(gc_col)
    kb = k * beta
    a_mat = jnp.where(ci < ri, mm_nt(kb, k) * decay, 0.0)
    t_inv = _inv_unit_lower(a_mat) if t_stored is None else _inv_lookup(a_mat, t_stored)
    u_base = mm_nn(t_inv, v * beta)
    w_dec = mm_nn(t_inv, kb * e_gc)
    attn = jnp.where(tril, mm_nt(q, k) * decay, 0.0)
    u = u_base - mm_nn(w_dec, state)
    o = mm_nn(q * e_gc, state) + mm_nn(attn, u)
    new_state = state * jnp.exp(gc_last) + mm_tn(k * jnp.exp(gc_last - gc_col), u)
    return o, new_state, t_inv


DELTA_CHUNKS_FWD = 2
DELTA_CHUNKS_BWD = 2
DELTA_SEQS = 2
DELTA_BATCH = DELTA_SEQS * N_HEADS


def _heads_of(ref, rs, first_col):
    return jnp.stack([ref[i // N_HEADS, rs, first_col + (i % N_HEADS) * HEAD:first_col + (i % N_HEADS + 1) * HEAD] for i in range(DELTA_BATCH)])


def _qkv_heads(ref, rs, part):
    return _heads_of(ref, rs, part * N_HEADS * HEAD)


def _by_sequence(a, lp):
    return a.reshape(a.shape[0] // lp, lp, a.shape[1])


def _ride(bufs, scatter, refs_in, refs_out, sems, first, last):
    if not bufs:
        return lambda: None

    @pl.when(first)
    def _():
        Exchange(refs_in, refs_out, *sems, scatter).start()

    def finish():
        @pl.when(last)
        def _():
            Exchange(refs_in, refs_out, *sems, scatter).wait()

    return finish


def delta_fwd(qkv, ba, alog, dtb, lp, gather=()):
    t = qkv.shape[0]
    nb, nc = t // lp, lp // CHUNK
    cps = DELTA_CHUNKS_FWD
    ng, rows = nc // cps, cps * CHUNK
    nx = len(gather)
    nbg = nb // DELTA_SEQS
    assert nc % cps == 0 and nb % DELTA_SEQS == 0

    def body(*refs):
        qkv_ref, ba_ref, al_ref, dt_ref = refs[:4]
        o_ref, s_ref, t_ref = refs[4 + nx:7 + nx]
        state_ref = refs[7 + 2 * nx]
        b, n = pl.program_id(0), pl.program_id(1)
        finish = _ride(gather, False, refs[4:4 + nx], refs[7 + nx:7 + 2 * nx], refs[8 + 2 * nx:], (b == 0) & (n == 0), (b == nbg - 1) & (n == ng - 1))

        @pl.when(n == 0)
        def _():
            state_ref[...] = jnp.zeros_like(state_ref)

        al, dtv = al_ref[...], dt_ref[...]
        for c in range(cps):
            rs = slice(c * CHUNK, (c + 1) * CHUNK)
            state = state_ref[...]
            o, new_state, t_inv = _delta_chunk(_qkv_heads(qkv_ref, rs, 0), _qkv_heads(qkv_ref, rs, 1), _qkv_heads(qkv_ref, rs, 2),
                                               tuple(ba_ref[i, rs, :] for i in range(DELTA_SEQS)), al, dtv, state, None)
            for i in range(DELTA_BATCH):
                seq, g = divmod(i, N_HEADS)
                o_ref[seq, rs, g * HEAD:(g + 1) * HEAD] = o[i]
                s_ref[seq, g, c] = state[i]
                t_ref[seq, g, c] = t_inv[i]
            state_ref[...] = new_state
        finish()

    rows_of = lambda width: pl.BlockSpec((DELTA_SEQS, rows, width), lambda b, n: (b, n, 0))
    par_spec = pl.BlockSpec((1, HEAD), lambda b, n: (0, 0))
    out = pl.pallas_call(
        body, grid=(nbg, ng), in_specs=[rows_of(3 * N_HEADS * HEAD), rows_of(HEAD), par_spec, par_spec] + [_HBM] * nx,
        out_specs=[rows_of(N_HEADS * HEAD), pl.BlockSpec((DELTA_SEQS, N_HEADS, cps, HEAD, HEAD), lambda b, n: (b, 0, n, 0, 0)),
                   pl.BlockSpec((DELTA_SEQS, N_HEADS, cps, CHUNK, CHUNK), lambda b, n: (b, 0, n, 0, 0))] + [_HBM] * nx,
        out_shape=[jax.ShapeDtypeStruct((nb, lp, N_HEADS * HEAD), F32), jax.ShapeDtypeStruct((nb, N_HEADS, nc, HEAD, HEAD), F32),
                   jax.ShapeDtypeStruct((nb, N_HEADS, nc, CHUNK, CHUNK), F32)] + Exchange.out_shape(gather, False),
        scratch_shapes=[pltpu.VMEM((DELTA_BATCH, HEAD, HEAD), F32)] + (Exchange.scratch(nx) if nx else []),
        compiler_params=_cparams(("arbitrary", "arbitrary")), name="delta_fwd")(_by_sequence(qkv, lp), _by_sequence(ba, lp), alog, dtb, *gather)
    return [out[0].reshape(t, N_HEADS * HEAD)] + list(out[1:])


def delta_bwd(qkv, ba, alog, dtb, states, t_invs, do, lp, scatter=()):
    t = qkv.shape[0]
    nb, nc = t // lp, lp // CHUNK
    cps = DELTA_CHUNKS_BWD
    ng, rows = nc // cps, cps * CHUNK
    nx = len(scatter)
    nbg = nb // DELTA_SEQS

    def body(*refs):
        qkv_ref, ba_ref, al_ref, dt_ref, s_ref, t_ref, do_ref = refs[:7]
        dqkv_ref, dba_ref, dal_ref, ddt_ref = refs[7 + nx:11 + nx]
        dstate_ref = refs[11 + 2 * nx]
        b, step = pl.program_id(0), pl.program_id(1)
        finish = _ride(scatter, True, refs[7:7 + nx], refs[11 + nx:11 + 2 * nx], refs[12 + 2 * nx:], (b == 0) & (step == 0),
                       (b == nbg - 1) & (step == ng - 1))

        @pl.when(step == 0)
        def _():
            dstate_ref[...] = jnp.zeros_like(dstate_ref)

        @pl.when((b == 0) & (step == 0))
        def _():
            dal_ref[...] = jnp.zeros_like(dal_ref)
            ddt_ref[...] = jnp.zeros_like(ddt_ref)

        al, dtv = al_ref[...], dt_ref[...]
        d_al = jnp.zeros((1, HEAD), F32)
        d_dt = jnp.zeros((1, HEAD), F32)
        for c in reversed(range(cps)):
            rs = slice(c * CHUNK, (c + 1) * CHUNK)
            t_n = jnp.stack([t_ref[i // N_HEADS, i % N_HEADS, c] for i in range(DELTA_BATCH)])
            s_n = jnp.stack([s_ref[i // N_HEADS, i % N_HEADS, c] for i in range(DELTA_BATCH)])

            def f(q_, k_, v_, ba_, al_, dt_, s_, t_n=t_n):
                return _delta_chunk(q_, k_, v_, ba_, al_, dt_, s_, t_n)[:2]

            _, vjp = jax.vjp(f, _qkv_heads(qkv_ref, rs, 0), _qkv_heads(qkv_ref, rs, 1), _qkv_heads(qkv_ref, rs, 2), tuple(ba_ref[i, rs, :] for i in range(DELTA_SEQS)), al, dtv, s_n)
            grads = vjp((_heads_of(do_ref, rs, 0), dstate_ref[...]))
            for part in range(3):
                for i in range(DELTA_BATCH):
                    col = (part * N_HEADS + i % N_HEADS) * HEAD
                    dqkv_ref[i // N_HEADS, rs, col:col + HEAD] = grads[part][i]
            for i in range(DELTA_SEQS):
                dba_ref[i, rs, :] = grads[3][i]
            d_al, d_dt = d_al + grads[4], d_dt + grads[5]
            dstate_ref[...] = grads[6]
        dal_ref[...] += d_al
        ddt_ref[...] += d_dt
        finish()

    rows_of = lambda width: pl.BlockSpec((DELTA_SEQS, rows, width), lambda b, n: (b, ng - 1 - n, 0))
    par_spec = pl.BlockSpec((1, HEAD), lambda b, n: (0, 0))
    out = pl.pallas_call(
        body, grid=(nbg, ng),
        in_specs=[rows_of(3 * N_HEADS * HEAD), rows_of(HEAD), par_spec, par_spec,
                  pl.BlockSpec((DELTA_SEQS, N_HEADS, cps, HEAD, HEAD), lambda b, n: (b, 0, ng - 1 - n, 0, 0)),
                  pl.BlockSpec((DELTA_SEQS, N_HEADS, cps, CHUNK, CHUNK), lambda b, n: (b, 0, ng - 1 - n, 0, 0)), rows_of(N_HEADS * HEAD)] + [_HBM] * nx,
        out_specs=[rows_of(3 * N_HEADS * HEAD), rows_of(HEAD), par_spec, par_spec] + [_HBM] * nx,
        out_shape=[jax.ShapeDtypeStruct((nb, lp, 3 * N_HEADS * HEAD), F32), jax.ShapeDtypeStruct((nb, lp, HEAD), F32),
                   jax.ShapeDtypeStruct((1, HEAD), F32), jax.ShapeDtypeStruct((1, HEAD), F32)] + Exchange.out_shape(scatter, True),
        scratch_shapes=[pltpu.VMEM((DELTA_BATCH, HEAD, HEAD), F32)] + (Exchange.scratch(nx) if nx else []),
        compiler_params=_cparams(("arbitrary", "arbitrary")), name="delta_bwd")(
            _by_sequence(qkv, lp), _by_sequence(ba, lp), alog, dtb, states, t_invs, _by_sequence(do, lp), *scatter)
    return [out[0].reshape(t, 3 * N_HEADS * HEAD), out[1].reshape(t, HEAD)] + list(out[2:])


ATT_Q_TILE = 256
ATT_K_TILE = 512
ATT_SCALE = QK_DIM ** -0.5


def _tiles(end, size):
    return [(s, min(s + size, end)) for s in range(0, end, size)]


def _att_visible(q0, q1, k0, k1, keys_first):
    if k1 <= q0 + CHUNK and k0 >= PAD_ROWS:
        return None
    shape = (k1 - k0, q1 - q0) if keys_first else (q1 - q0, k1 - k0)
    qpos = q0 + lax.broadcasted_iota(jnp.int32, shape, 1 if keys_first else 0)
    kpos = k0 + lax.broadcasted_iota(jnp.int32, shape, 0 if keys_first else 1)
    shift = CHUNK.bit_length() - 1
    return (jnp.right_shift(kpos, shift) <= jnp.right_shift(qpos, shift)) & (kpos >= PAD_ROWS)


def _att_seq_specs(lp):
    return pl.BlockSpec((lp, QK_PAD), lambda b, h: (b, h)), pl.BlockSpec((lp, HEAD), lambda b, h: (b, h))


def flash_fwd(q, k, v, lp):
    t = q.shape[0]
    qk_seq, o_seq = _att_seq_specs(lp)

    def body(q_ref, k_ref, v_ref, o_ref, lse_ref):
        q_tiles = _tiles(lp, ATT_Q_TILE)

        def score_steps(q0, q1, out):
            def step(k0, k1):
                s = mm_nt(q_ref[q0:q1, :], k_ref[k0:k1, :])
                vis = _att_visible(q0, q1, k0, k1, False)
                s = s if vis is None else jnp.where(vis, s, NEG)
                out["scores"].append(s)
                row_max = jnp.max(s, -1, keepdims=True)
                out["m"] = row_max if out["m"] is None else jnp.maximum(out["m"], row_max)
            return [functools.partial(step, k0, k1) for k0, k1 in _tiles(q1, ATT_K_TILE)]

        cur = {"scores": [], "m": None}
        for step in score_steps(*q_tiles[0], cur):
            step()
        for i, (q0, q1) in enumerate(q_tiles):
            nxt = {"scores": [], "m": None}
            ahead = score_steps(*q_tiles[i + 1], nxt) if i + 1 < len(q_tiles) else []
            l = jnp.zeros((q1 - q0, 1), F32)
            acc = jnp.zeros((q1 - q0, HEAD), F32)
            for s, (k0, k1) in zip(cur["scores"], _tiles(q1, ATT_K_TILE), strict=True):
                if ahead:
                    ahead.pop(0)()
                p = jnp.exp(s - cur["m"])
                l = l + jnp.sum(p, -1, keepdims=True)
                acc = acc + mm_nn(p, v_ref[k0:k1, :])
            for step in ahead:
                step()
            o_ref[q0:q1, :] = acc / l
            lse_ref[q0:q1, :] = jnp.broadcast_to(cur["m"] + jnp.log(l), (q1 - q0, HEAD))
            cur = nxt

    big = jax.ShapeDtypeStruct((t, N_HEADS * HEAD), F32)
    return pl.pallas_call(
        body, grid=(t // lp, N_HEADS), in_specs=[qk_seq, qk_seq, o_seq], out_specs=[o_seq, o_seq], out_shape=[big, big],
        compiler_params=_cparams(("arbitrary", "arbitrary")), name="flash_fwd")(q, k, v)


def flash_bwd(q, k, v, o, lse, do, lp):
    t = q.shape[0]
    qk_seq, o_seq = _att_seq_specs(lp)

    def body(q_ref, k_ref, v_ref, o_ref, lse_ref, do_ref, dq_ref, dk_ref, dv_ref):
        dk_ref[...] = jnp.zeros_like(dk_ref)
        dv_ref[...] = jnp.zeros_like(dv_ref)
        for q0, q1 in _tiles(lp, ATT_Q_TILE):
            qb, dob = q_ref[q0:q1, :], do_ref[q0:q1, :]
            lse_row = jnp.transpose(lse_ref[q0:q1, :])[0:1, :]
            dsum_row = jnp.sum(jnp.transpose(dob * o_ref[q0:q1, :]), axis=0, keepdims=True)
            dq = jnp.zeros((q1 - q0, QK_PAD), F32)
            for k0, k1 in _tiles(q1, ATT_K_TILE):
                kb, vb = k_ref[k0:k1, :], v_ref[k0:k1, :]
                s = mm_nt(kb, qb)
                vis = _att_visible(q0, q1, k0, k1, True)
                s = s if vis is None else jnp.where(vis, s, NEG)
                p = jnp.exp(s - lse_row)
                ds = p * (mm_nt(vb, dob) - dsum_row)
                dv_ref[k0:k1, :] += mm_nn(p, dob)
                dk_ref[k0:k1, :] += mm_nn(ds, qb)
                dq = dq + mm_tn(ds, kb)
            dq_ref[q0:q1, :] = dq

    return pl.pallas_call(
        body, grid=(t // lp, N_HEADS), in_specs=[qk_seq, qk_seq, o_seq, o_seq, o_seq, o_seq], out_specs=[qk_seq, qk_seq, o_seq],
        out_shape=[jax.ShapeDtypeStruct((t, N_HEADS * QK_PAD), F32), jax.ShapeDtypeStruct((t, N_HEADS * QK_PAD), F32),
                   jax.ShapeDtypeStruct((t, N_HEADS * HEAD), F32)],
        compiler_params=_cparams(("arbitrary", "arbitrary")), name="flash_bwd")(q, k, v, o, lse, do)


def loss_head(h2, target, lp):
    nb, seq, d = target.shape
    tr = 128
    nblk = lp // tr
    lead_blocks = LEAD // tr

    def body(h_ref, t_ref, loss_ref, dh_ref, acc_ref):
        b, i = pl.program_id(0), pl.program_id(1)

        @pl.when((b == 0) & (i == 0))
        def _():
            acc_ref[...] = jnp.zeros_like(acc_ref)

        @pl.when(i < lead_blocks)
        def _():
            dh_ref[...] = jnp.zeros_like(dh_ref)

        @pl.when(i >= lead_blocks)
        def _():
            err = h_ref[...] - t_ref[...]
            dh_ref[...] = err * (1.0 / d)
            acc_ref[...] += jnp.sum(err * err, axis=0, keepdims=True)

        @pl.when((b == nb - 1) & (i == nblk - 1))
        def _():
            loss_ref[...] = jnp.sum(acc_ref[...], axis=1, keepdims=True) * (0.5 / d)

    return pl.pallas_call(
        body, grid=(nb, nblk),
        in_specs=[pl.BlockSpec((None, tr, d), lambda b, i: (b, i, 0)),
                  pl.BlockSpec((None, tr, d), lambda b, i: (b, jnp.maximum(i - lead_blocks, 0), 0))],
        out_specs=[pl.BlockSpec((1, 1), lambda b, i: (0, 0)), pl.BlockSpec((None, tr, d), lambda b, i: (b, i, 0))],
        out_shape=[jax.ShapeDtypeStruct((1, 1), F32), jax.ShapeDtypeStruct((nb, lp, d), F32)],
        scratch_shapes=[pltpu.VMEM((1, d), F32)], compiler_params=_cparams(("arbitrary", "arbitrary")), name="loss_head")(h2, target)


def meta_grad(dh0):
    nb, _, d = dh0.shape

    def body(g_ref, o_ref):
        @pl.when(pl.program_id(0) == 0)
        def _():
            o_ref[...] = jnp.zeros_like(o_ref)

        o_ref[...] += g_ref[PAD_ROWS:LEAD, :]

    return pl.pallas_call(
        body, grid=(nb,), in_specs=[pl.BlockSpec((None, LEAD, d), lambda b: (b, 0, 0))],
        out_specs=pl.BlockSpec((N_META, d), lambda b: (0, 0)), out_shape=jax.ShapeDtypeStruct((N_META, d), F32),
        compiler_params=_cparams(("arbitrary",)), name="meta_grad")(dh0)


_HBM = pl.BlockSpec(memory_space=pltpu.HBM)


def _mesh_pos():
    x, y, c = lax.axis_index("x"), lax.axis_index("y"), lax.axis_index("c")
    return x, y, c


def _peer(x, y, c, k):
    px = 1 - x if k & 4 else x
    py = 1 - y if k & 2 else y
    pc = 1 - c if k & 1 else c
    return (px, py, pc), 4 * px + 2 * py + pc


class Exchange:
    def __init__(self, x_refs, out_refs, send_sems, recv_sems, local_sems, scatter):
        self.x_refs, self.out_refs, self.scatter = x_refs, out_refs, scatter
        self.send_sems, self.recv_sems, self.local_sems = send_sems, recv_sems, local_sems
        self.pos = _mesh_pos()
        x, y, c = self.pos
        self.me = 4 * x + 2 * y + c

    @staticmethod
    def scratch(n):
        return [pltpu.SemaphoreType.DMA((n, N_DEV - 1)), pltpu.SemaphoreType.DMA((n, N_DEV - 1)), pltpu.SemaphoreType.DMA((n,))]

    @staticmethod
    def out_shape(bufs, scatter):
        return [jax.ShapeDtypeStruct(b.shape if scatter else (N_DEV,) + b.shape, b.dtype) for b in bufs]

    def _local(self, i):
        return pltpu.make_async_copy(self.x_refs[i].at[self.me] if self.scatter else self.x_refs[i], self.out_refs[i].at[self.me], self.local_sems.at[i])

    def _copy(self, i, k, landing):
        peer, peer_id = _peer(*self.pos, k)
        src = self.x_refs[i].at[peer_id] if self.scatter else self.x_refs[i]
        return pltpu.make_async_remote_copy(src_ref=src, dst_ref=self.out_refs[i].at[peer_id if landing else self.me],
                                            send_sem=self.send_sems.at[i, k - 1], recv_sem=self.recv_sems.at[i, k - 1],
                                            device_id=peer, device_id_type=pl.DeviceIdType.MESH)

    def start(self):
        for i in range(len(self.x_refs)):
            self._local(i).start()
        for k in range(1, N_DEV):
            for i in range(len(self.x_refs)):
                self._copy(i, k, False).start()

    def wait(self):
        for k in range(1, N_DEV):
            for i in range(len(self.x_refs)):
                self._copy(i, k, True).wait_recv()
        for k in range(1, N_DEV):
            for i in range(len(self.x_refs)):
                self._copy(i, k, False).wait_send()
        for i in range(len(self.x_refs)):
            self._local(i).wait()


def _exchange(name, bufs, scatter):
    n = len(bufs)

    def body(*refs):
        ex = Exchange(refs[:n], refs[n:2 * n], *refs[2 * n:], scatter)
        ex.start()
        ex.wait()

    return pl.pallas_call(body, in_specs=[_HBM] * n, out_specs=[_HBM] * n, out_shape=Exchange.out_shape(bufs, scatter),
                          scratch_shapes=Exchange.scratch(n), name=name)(*bufs)


def _f_rms(x, g):
    return (_rms(x, g),)


def _f_rms2(x, g1, g2):
    r = x * lax.rsqrt(jnp.sum(x * x, -1, keepdims=True) / x.shape[-1] + EPS)
    return r * g1, r * g2


def _f_out_gate(o, gate, gain):
    return (_rms(o, gain) * _silu(gate),)


def _f_gate(o, gate):
    return (o * _silu(gate),)


@jax.custom_vjp
def _swap_rope_halves(x):
    half = ROPE // 2
    lane = lax.broadcasted_iota(jnp.int32, x.shape, 1)
    return jnp.where(lane < half, pltpu.roll(x, HEAD - half, 1), jnp.where(lane < ROPE, pltpu.roll(x, half, 1), 0.0))


_swap_rope_halves.defvjp(lambda x: (_swap_rope_halves(x), None), lambda _, g: (_swap_rope_halves(g),))


def _f_qk_final(scale, nope, rope_in, g_nope, g_rope, cos, sin):
    ms = (jnp.sum(nope * nope, -1, keepdims=True) + jnp.sum(rope_in * rope_in, -1, keepdims=True)) / QK_DIM
    r = lax.rsqrt(ms + EPS)
    a = nope * r * g_nope
    b = rope_in * r * g_rope
    out = jnp.concatenate([a, b * cos + _swap_rope_halves(b) * sin], axis=1)
    return (out if scale == 1.0 else out * scale,)


def _rope_tables(lp):
    half = ROPE // 2
    pos = jnp.maximum(jnp.arange(lp) - PAD_ROWS, 0)
    inv = ROPE_THETA ** (-jnp.arange(half, dtype=F32) / half)
    ang = pos.astype(F32)[:, None] * inv[None, :]
    zeros = jnp.zeros((lp, HEAD - ROPE), F32)
    cos = jnp.concatenate([jnp.cos(ang), jnp.cos(ang), zeros], 1)
    sin = jnp.concatenate([-jnp.sin(ang), jnp.sin(ang), zeros], 1)
    return cos, sin


def _pad_lanes(w, width=HEAD):
    return jnp.pad(w, ((0, 0), (0, width - w.shape[1])))


def _pad_rows(w, rows=HEAD):
    return jnp.pad(w, ((0, rows - w.shape[0]), (0, 0)))


def _split_heads_qk_t(w_t):
    k = w_t.shape[1]
    w3 = w_t.reshape(N_HEADS, QK_DIM, k)
    nope = w3[:, :HEAD].reshape(N_HEADS * HEAD, k)
    rope = jnp.pad(w3[:, HEAD:], ((0, 0), (0, HEAD - ROPE), (0, 0))).reshape(N_HEADS * HEAD, k)
    return jnp.concatenate([nope, rope], 0)


def _merge_heads_qk_t(g_t):
    k = g_t.shape[1]
    kw = N_HEADS * HEAD
    nope, rope = g_t[:kw].reshape(N_HEADS, HEAD, k), g_t[kw:].reshape(N_HEADS, HEAD, k)[:, :ROPE]
    return jnp.concatenate([nope, rope], 1).reshape(N_HEADS * QK_DIM, k)


def local_step(x, target, w, deferred=None):
    nb, seq, d = x.shape
    lp = seq + LEAD
    t = nb * lp
    tr = _pick(lp, (544, 128))
    ntab = lp // tr
    mxu = _MXU_DTYPE
    kw = N_HEADS * HEAD

    a_w_in_t = w["a_w_in"].astype(mxu)
    w_qkv_t, w_ga_t, w_ba_t = a_w_in_t[:3 * kw], a_w_in_t[3 * kw:4 * kw], _pad_rows(a_w_in_t[4 * kw:])
    a_conv = w["a_conv"].T
    alog, dtb, o_gain = _pad_lanes(w["a_log"]), _pad_lanes(w["a_dt_bias"]), w["a_o_gain"]
    a_norm, kv_norm, b_norm = w["a_norm"], w["kv_norm"][None, :], w["b_norm"]
    lat_norm, qlat_norm = w["kv_latent_norm"][None, :], w["b_q_latent_norm"]
    kg_nope, kg_rope = w["k_gain"][None, :HEAD], _pad_lanes(w["k_gain"][None, HEAD:])
    qg_nope, qg_rope = w["b_q_gain"][:, :HEAD], _pad_lanes(w["b_q_gain"][:, HEAD:])
    cos, sin = _rope_tables(lp)

    meta = jnp.broadcast_to(w["meta_tokens"].T[None], (nb, N_META, d))
    h0 = jnp.concatenate([jnp.zeros((nb, PAD_ROWS, d), F32), meta, x], 1).reshape(t, d)
    (hn,) = row_call("a_norm_fwd", _f_rms, [Arg(h0), Arg(a_norm, "par")], [(d, mxu, d, False)], tr)
    z_qkv = matmul("a_in_qkv", hn, w_qkv_t, "nt")
    gate_a = matmul("a_in_gate", hn, w_ga_t, "nt")
    z_ba = matmul("a_in_ba", hn, w_ba_t, "nt")
    qkv_a, y_conv = conv_fwd(z_qkv, a_conv, lp)
    o_a, states, t_invs, *gathered = delta_fwd(qkv_a, z_ba, alog, dtb, lp, gather=deferred.gather_bufs if deferred else ())
    if deferred:
        w = {**w, **deferred.finish(gathered)}
    a_w_out = w["a_w_out"].astype(mxu)
    w_dkv, w_dpe = w["kv_w_down"][:, :KV_RANK].astype(mxu), _pad_lanes(w["kv_w_down"][:, KV_RANK:]).astype(mxu)
    w_ukv_t = jnp.concatenate([w["kv_w_uk"], w["kv_w_uv"]], 0).astype(mxu)
    b_w_in_t = w["b_w_in"].astype(mxu)
    w_cq_t, w_gb_t = b_w_in_t[:Q_RANK], b_w_in_t[Q_RANK:]
    w_q_t = _split_heads_qk_t(w["b_w_uq"]).astype(mxu)
    b_w_out = w["b_w_out"].astype(mxu)
    og_args = [Arg(o_a, bc=HEAD, ph=True, diff=True), Arg(gate_a, bc=HEAD, ph=True, diff=True, gdt=mxu), Arg(o_gain, "par", diff=True)]
    (og_a,) = row_call("a_out_gate_fwd", _f_out_gate, og_args, [(kw, mxu, HEAD, True)], tr, nh=N_HEADS)
    h1 = matmul("a_out", og_a, a_w_out, "nn", res=h0)

    hk, hb = row_call("b_norms_fwd", _f_rms2, [Arg(h1), Arg(kv_norm, "par"), Arg(b_norm, "par")], [(d, mxu, d, False), (d, mxu, d, False)], tr)
    c_kv_raw = matmul("kv_down", hk, w_dkv, "nn")
    k_pe = matmul("kv_down_pe", hk, w_dpe, "nn")
    c_q_raw = matmul("b_in_q", hb, w_cq_t, "nt")
    gate_b = matmul("b_in_gate", hb, w_gb_t, "nt")
    (c_kv,) = row_call("kv_latent_fwd", _f_rms, [Arg(c_kv_raw), Arg(lat_norm, "par")], [(KV_RANK, mxu, KV_RANK, False)], tr)
    (c_q,) = row_call("q_latent_fwd", _f_rms, [Arg(c_q_raw), Arg(qlat_norm, "par")], [(Q_RANK, mxu, Q_RANK, False)], tr)
    k_nope = matmul("k_up", c_kv, w_ukv_t[:kw], "nt")
    v_b = matmul("v_up", c_kv, w_ukv_t[kw:], "nt", out_dtype=mxu)
    q_up = matmul("q_up", c_q, w_q_t, "nt")
    tabs = [Arg(cos, "tab"), Arg(sin, "tab")]
    k_args = [Arg(k_nope, bc=HEAD, ph=True, diff=True, gdt=mxu), Arg(k_pe, diff=True), Arg(kg_nope, "par", diff=True), Arg(kg_rope, "par", diff=True)] + tabs
    q_args = [Arg(q_up, bc=HEAD, ph=True, diff=True, gdt=mxu), Arg(q_up, bc=HEAD, base=N_HEADS, ph=True, diff=True, gdt=mxu),
              Arg(qg_nope, "par", diff=True), Arg(qg_rope, "par", diff=True)] + tabs
    f_k_final, f_q_final = functools.partial(_f_qk_final, 1.0), functools.partial(_f_qk_final, ATT_SCALE)
    (k_fin,) = row_call("k_final_fwd", f_k_final, k_args, [(N_HEADS * QK_PAD, mxu, QK_PAD, True)], tr, nh=N_HEADS, ntab=ntab)
    (q_fin,) = row_call("q_final_fwd", f_q_final, q_args, [(N_HEADS * QK_PAD, mxu, QK_PAD, True)], tr, nh=N_HEADS, ntab=ntab)
    o_b, lse = flash_fwd(q_fin, k_fin, v_b, lp)
    gb_args = [Arg(o_b, diff=True), Arg(gate_b, diff=True, gdt=mxu)]
    (og_b,) = row_call("b_gate_fwd", _f_gate, gb_args, [(kw, mxu, kw, False)], tr)
    h2 = matmul("b_out", og_b, b_w_out, "nn", res=h1)

    loss, dh2 = loss_head(h2.reshape(nb, lp, d), target, lp)
    dh2 = dh2.reshape(t, d)
    grads = {}

    d_og_b = matmul("b_out_dx", dh2, b_w_out, "nt")
    grads["b_w_out"] = matmul("b_out_dw", og_b, dh2, "tn")
    d_o_b, d_gate_b = row_vjp_call("b_gate_bwd", _f_gate, gb_args, [Arg(d_og_b)], tr)
    dq_fin, dk_fin, dv_b = flash_bwd(q_fin, k_fin, v_b, o_b, lse, d_o_b, lp)
    dq_nope, dq_rope, d_qg_nope, d_qg_rope = row_vjp_call(
        "q_final_bwd", f_q_final, q_args, [Arg(dq_fin, bc=QK_PAD, ph=True)], tr, nh=N_HEADS, ntab=ntab)
    dk_nope, dk_pe, d_kg_nope, d_kg_rope = row_vjp_call(
        "k_final_bwd", f_k_final, k_args, [Arg(dk_fin, bc=QK_PAD, ph=True)], tr, nh=N_HEADS, ntab=ntab)
    grads["b_q_gain"] = jnp.concatenate([d_qg_nope, d_qg_rope[:, :ROPE]], 1)
    grads["k_gain"] = jnp.concatenate([d_kg_nope, d_kg_rope[:, :ROPE]], 1)[0]
    d_c_q = matmul("q_nope_dx", dq_nope, w_q_t[:kw], "nn")
    d_c_q = matmul("q_rope_dx", dq_rope, w_q_t[kw:], "nn", res=d_c_q)
    grads["b_w_uq"] = _merge_heads_qk_t(jnp.concatenate([matmul("q_nope_dw", dq_nope, c_q, "tn"), matmul("q_rope_dw", dq_rope, c_q, "tn")], 0))
    d_c_kv = matmul("k_up_dx", dk_nope, w_ukv_t[:kw], "nn")
    d_c_kv = matmul("v_up_dx", dv_b, w_ukv_t[kw:], "nn", res=d_c_kv)
    grads["kv_w_uk"], grads["kv_w_uv"] = matmul("k_up_dw", dk_nope, c_kv, "tn"), matmul("v_up_dw", dv_b, c_kv, "tn")
    d_c_q_raw, grads["b_q_latent_norm"] = row_vjp_call(
        "q_latent_bwd", _f_rms, [Arg(c_q_raw, diff=True, gdt=mxu), Arg(qlat_norm, "par", diff=True)], [Arg(d_c_q)], tr)
    d_c_kv_raw, d_lat = row_vjp_call(
        "kv_latent_bwd", _f_rms, [Arg(c_kv_raw, diff=True, gdt=mxu), Arg(lat_norm, "par", diff=True)], [Arg(d_c_kv)], tr)
    grads["kv_latent_norm"] = d_lat[0]
    d_hb = matmul("b_in_q_dx", d_c_q_raw, w_cq_t, "nn")
    d_hb = matmul("b_in_gate_dx", d_gate_b, w_gb_t, "nn", res=d_hb)
    grads["b_w_in"] = jnp.concatenate([matmul("b_in_q_dw", d_c_q_raw, hb, "tn"), matmul("b_in_gate_dw", d_gate_b, hb, "tn")], 0)
    d_hk = matmul("kv_down_dx", d_c_kv_raw, w_dkv, "nt")
    d_hk = matmul("kv_down_pe_dx", dk_pe, w_dpe, "nt", res=d_hk)
    grads["kv_w_down"] = jnp.concatenate([matmul("kv_down_dw", hk, d_c_kv_raw, "tn"), matmul("kv_down_pe_dw", hk, dk_pe, "tn")[:, :ROPE]], 1)
    dh1, d_kv_norm, grads["b_norm"] = row_vjp_call(
        "b_norms_bwd", lambda x_, g1, g2: _f_rms2(x_, g1, g2) + (x_,),
        [Arg(h1, diff=True), Arg(kv_norm, "par", diff=True), Arg(b_norm, "par", diff=True)], [Arg(d_hk), Arg(d_hb), Arg(dh2)], tr)
    grads["kv_norm"] = d_kv_norm[0]

    d_og_a = matmul("a_out_dx", dh1, a_w_out, "nt")
    grads["a_w_out"] = matmul("a_out_dw", og_a, dh1, "tn")
    d_o_a, d_gate_a, grads["a_o_gain"] = row_vjp_call(
        "a_out_gate_bwd", _f_out_gate, og_args, [Arg(d_og_a, bc=HEAD, ph=True)], tr, nh=N_HEADS)
    dqkv_a, d_ba, d_alog, d_dtb, *received = delta_bwd(qkv_a, z_ba, alog, dtb, states, t_invs, d_o_a, lp,
                                                        scatter=deferred.scatter_bufs(grads) if deferred else ())
    grads["a_log"], grads["a_dt_bias"] = d_alog[:, :N_HEADS], d_dtb[:, :N_HEADS]
    dz_qkv, d_conv = conv_bwd(z_qkv, y_conv, a_conv, dqkv_a, lp)
    grads["a_conv"] = d_conv.T
    grads["a_w_in"] = jnp.concatenate([matmul("a_in_qkv_dw", dz_qkv, hn, "tn"), matmul("a_in_gate_dw", d_gate_a, hn, "tn"),
                                       matmul("a_in_ba_dw", d_ba, hn, "tn")[:2 * N_HEADS]], 0)
    ride = deferred.last_scatter_bufs(grads) if deferred else ()
    d_hn = matmul("a_in_qkv_dx", dz_qkv, w_qkv_t, "nn", scatter=ride)
    if ride:
        d_hn, *received_last = d_hn
        received = list(received) + received_last
    d_hn = matmul("a_in_gate_dx", d_gate_a, w_ga_t, "nn", res=d_hn)
    d_hn = matmul("a_in_ba_dx", d_ba, w_ba_t, "nn", res=d_hn)
    dh0, grads["a_norm"] = row_vjp_call("a_norm_bwd", lambda x_, g_: _f_rms(x_, g_) + (x_,),
                                        [Arg(h0, diff=True), Arg(a_norm, "par", diff=True)], [Arg(d_hn), Arg(dh1)], tr)
    dh0 = dh0.reshape(nb, lp, d)
    grads["meta_tokens"] = meta_grad(dh0).T
    return loss, dh0[:, LEAD:], grads, received


_SHARDED = (
    ("meta_tokens", True, False), ("a_norm", True, False), ("a_w_in", True, True), ("a_conv", True, False), ("a_w_out", False, True),
    ("kv_w_down", False, True), ("kv_w_uk", True, True), ("kv_w_uv", True, True), ("b_w_in", True, True), ("b_w_uq", True, True),
    ("b_w_out", False, True))
_REPLICATED = ("a_log", "a_dt_bias", "a_o_gain", "kv_norm", "kv_latent_norm", "k_gain", "b_norm", "b_q_latent_norm", "b_q_gain")
_ALL_WEIGHTS = ("meta_tokens", "a_norm", "a_w_in", "a_conv", "a_log", "a_dt_bias", "a_o_gain", "a_w_out", "kv_norm", "kv_w_down",
                "kv_latent_norm", "kv_w_uk", "kv_w_uv", "k_gain", "b_norm", "b_w_in", "b_q_latent_norm", "b_w_uq", "b_q_gain", "b_w_out")


def _round_up(n, m):
    return (n + m - 1) // m * m


def _pack_rows(pieces, row_multiple):
    padded = []
    for p in pieces:
        n = p.shape[-1]
        padded.append(jnp.pad(p, [(0, 0)] * (p.ndim - 1) + [(0, _round_up(n, PACK_COLS) - n)]))
    flat = jnp.concatenate(padded, -1)
    rows = _round_up(flat.shape[-1] // PACK_COLS, row_multiple)
    flat = jnp.pad(flat, [(0, 0)] * (flat.ndim - 1) + [(0, rows * PACK_COLS - flat.shape[-1])])
    return flat.reshape(flat.shape[:-1] + (rows, PACK_COLS))


def _unpack_rows(buf, sizes):
    flat = buf.reshape(buf.shape[:-2] + (-1,))
    out, off = [], 0
    for n in sizes:
        out.append(flat[..., off:off + n])
        off += _round_up(n, PACK_COLS)
    return out


def _shard_2d(a):
    return a.reshape(a.shape[-2:]) if a.ndim > 2 else a


def _kl_shard(a, by_cols):
    return _shard_2d(a).T if by_cols else _shard_2d(a)


_GROUPS_FIRST = (("a_w_in",),)
_GROUPS_LATER = (("a_w_out", "b_w_in", "b_w_out"), ("b_w_uq",), ("kv_w_down",), ("kv_w_uk", "kv_w_uv"))
_SMALL_SHARDED = ("meta_tokens", "a_norm", "a_conv")
_BY_COLS = {name: by_cols for name, by_cols, _ in _SHARDED}
ROW_ALIGN = 16


def _stack_rows(pieces):
    padded, starts, row = [], [], 0
    for p in pieces:
        r = p.shape[-2]
        padded.append(jnp.pad(p, [(0, 0)] * (p.ndim - 2) + [(0, _round_up(r, ROW_ALIGN) - r), (0, 0)]))
        starts.append(row)
        row += _round_up(r, ROW_ALIGN)
    return jnp.concatenate(padded, -2), starts


def _stack_group(arrays_by_name, names):
    arrays = [arrays_by_name[n].astype(BF16) for n in names]
    buf, starts = _stack_rows(arrays)
    return buf, [(n, s, a.shape[-2]) for n, s, a in zip(names, starts, arrays, strict=True)]


def _stack_groups(arrays_by_name, groups):
    stacked = [_stack_group(arrays_by_name, names) for names in groups]
    return [b for b, _ in stacked], [entries for _, entries in stacked]


def _full_from_gathered(gathered, layout):
    full = {}
    for got, entries in zip(gathered, layout, strict=True):
        for name, start, rows in entries:
            full[name] = got[:, start:start + rows].reshape(N_DEV * rows, got.shape[-1])
    return full


def gather_first_weights(local):
    shards = {n: _kl_shard(local[n], _BY_COLS[n]) for names in _GROUPS_FIRST for n in names}
    bufs, layout = _stack_groups(shards, _GROUPS_FIRST)
    small = [_kl_shard(local[n], _BY_COLS[n]) for n in _SMALL_SHARDED]
    bufs.append(_pack_rows([s.reshape(-1) for s in small], 8))
    gathered = _exchange("all_gather", bufs, scatter=False)
    full = _full_from_gathered(gathered[:-1], layout)
    for name, part, sh in zip(_SMALL_SHARDED, _unpack_rows(gathered[-1], [s.size for s in small]), small, strict=True):
        full[name] = part.reshape(N_DEV * sh.shape[0], sh.shape[1])
    full["a_norm"] = full["a_norm"].reshape(1, -1)
    return full


class LaterExchanges:
    def __init__(self, local):
        shards = {n: _kl_shard(local[n], _BY_COLS[n]) for names in _GROUPS_LATER for n in names}
        self.gather_bufs, self.layout = _stack_groups(shards, _GROUPS_LATER)

    def finish(self, gathered):
        return _full_from_gathered(gathered, self.layout)

    def scatter_bufs(self, grads):
        return _stack_groups(_owner_slices(grads, _GROUPS_LATER), _GROUPS_LATER)[0]

    def last_scatter_bufs(self, grads):
        bufs, self.last_layout = _stack_groups(_owner_slices(grads, _GROUPS_FIRST), _GROUPS_FIRST)
        return bufs


def _owner_slices(grads, groups):
    return {n: grads[n].reshape(N_DEV, -1, grads[n].shape[-1]) for names in groups for n in names}


def reduce_contributions(name, recv):
    _, r, c = recv.shape
    tr = _pick(r, (256, 128, 64, 32, 16, 8))

    def body(g_ref, o_ref):
        g = g_ref[0].astype(F32)
        for dev in range(1, N_DEV):
            g = g + g_ref[dev].astype(F32)
        o_ref[...] = g

    return pl.pallas_call(
        body, grid=(r // tr,), in_specs=[pl.BlockSpec((N_DEV, tr, c), lambda i: (0, i, 0))], out_specs=pl.BlockSpec((tr, c), lambda i: (i, 0)),
        out_shape=jax.ShapeDtypeStruct((r, c), F32), compiler_params=_cparams(("arbitrary",)), name=name)(recv)


def adamw_all(gs, ws, ms, vs):
    n = len(gs)

    def body(*refs):
        for i in range(n):
            g_ref, w_ref, m_ref, v_ref = (refs[j * n + i] for j in range(4))
            d_ref, mo_ref, vo_ref = (refs[(4 + j) * n + i] for j in range(3))
            g = g_ref[...]
            m_new = ADAM_B1 * m_ref[...] + (1.0 - ADAM_B1) * g
            v_new = ADAM_B2 * v_ref[...] + (1.0 - ADAM_B2) * (g * g)
            m_hat = m_new / (1.0 - ADAM_B1 ** ADAM_STEP)
            v_hat = v_new / (1.0 - ADAM_B2 ** ADAM_STEP)
            d_ref[...] = -ADAM_LR * (m_hat / (jnp.sqrt(v_hat) + ADAM_EPS) + ADAM_WD * w_ref[...])
            mo_ref[...] = m_new
            vo_ref[...] = v_new

    out = [jax.ShapeDtypeStruct(g.shape, F32) for g in gs] * 3
    res = pl.pallas_call(body, out_shape=out, compiler_params=pltpu.CompilerParams(vmem_limit_bytes=VMEM_LIMIT), name="adamw_all")(*gs, *ws, *ms, *vs)
    return res[:n], res[n:2 * n], res[2 * n:]


def kernel(x, meta_tokens, a_norm, a_w_in, a_conv, a_log, a_dt_bias, a_o_gain, a_w_out, kv_norm, kv_w_down, kv_latent_norm, kv_w_uk, kv_w_uv, k_gain, b_norm, b_w_in, b_q_latent_norm, b_w_uq, b_q_gain, b_w_out, loss_target, m_meta_tokens, m_a_norm, m_a_w_in, m_a_conv, m_a_log, m_a_dt_bias, m_a_o_gain, m_a_w_out, m_kv_norm, m_kv_w_down, m_kv_latent_norm, m_kv_w_uk, m_kv_w_uv, m_k_gain, m_b_norm, m_b_w_in, m_b_q_latent_norm, m_b_w_uq, m_b_q_gain, m_b_w_out, v_meta_tokens, v_a_norm, v_a_w_in, v_a_conv, v_a_log, v_a_dt_bias, v_a_o_gain, v_a_w_out, v_kv_norm, v_kv_w_down, v_kv_latent_norm, v_kv_w_uk, v_kv_w_uv, v_k_gain, v_b_norm, v_b_w_in, v_b_q_latent_norm, v_b_w_uq, v_b_q_gain, v_b_w_out):
    given = dict(locals())
    local_w = {n: given[n] for n in _ALL_WEIGHTS}
    full = gather_first_weights(local_w)
    for n in _REPLICATED:
        full[n] = local_w[n]
    later = LaterExchanges(local_w)

    loss_part, grad_x, grads, received_riding = local_step(x, loss_target, full, later)

    exact = [grads[n].reshape(N_DEV, -1) for n in _SMALL_SHARDED]
    exact += [jnp.broadcast_to(grads[n].reshape(1, -1), (N_DEV, grads[n].size)) for n in _REPLICATED]
    exact.append(jnp.broadcast_to(loss_part, (N_DEV, 1)))
    received = list(received_riding) + list(_exchange("all_to_all", [_pack_rows(exact, 8)], scatter=True))
    layout = later.layout + later.last_layout
    summed = [reduce_contributions(f"reduce_{i}", r) for i, r in enumerate(received)]

    grad_kl = {}
    for got, entries in zip(summed, layout):
        for n, start, rows in entries:
            grad_kl[n] = got[start:start + rows]
    parts = _unpack_rows(summed[-1], [p.shape[1] for p in exact])
    for n, part in zip(_SMALL_SHARDED + _REPLICATED, parts, strict=False):
        grad_kl[n] = part
    loss = parts[-1][0]

    def natural_2d(n, a):
        shape = _shard_2d(local_w[n]).shape if local_w[n].ndim > 1 else (1, local_w[n].size)
        return a.reshape(shape[::-1]).T if _BY_COLS.get(n, False) else a.reshape(shape)

    as_2d = lambda n, a: a.reshape(natural_2d(n, grad_kl[n]).shape)
    gs = [natural_2d(n, grad_kl[n]) for n in _ALL_WEIGHTS]
    deltas, new_m, new_v = adamw_all(gs, [as_2d(n, local_w[n]) for n in _ALL_WEIGHTS], [as_2d(n, given["m_" + n]) for n in _ALL_WEIGHTS],
                                     [as_2d(n, given["v_" + n]) for n in _ALL_WEIGHTS])
    results = [a.reshape(local_w[n].shape) for group in (gs, deltas, new_m, new_v) for n, a in zip(_ALL_WEIGHTS, group, strict=True)]
    return (loss, grad_x, *results)
```

```python
import dataclasses
import functools
import math

import jax
import jax.numpy as jnp
from jax import lax
from jax.experimental import pallas as pl
from jax.experimental.pallas import tpu as pltpu

F32 = jnp.float32
BF16 = jnp.bfloat16
_MXU_DTYPE = jnp.bfloat16

N_DEV = 8
D_MODEL = 1024
N_HEADS = 8
HEAD = 128
CHUNK = 64
N_META = 16
PAD_ROWS = 2 * CHUNK - N_META
LEAD = PAD_ROWS + N_META
ROPE = 64
QK_DIM = HEAD + ROPE
QK_PAD = 2 * HEAD
KV_RANK = 256
Q_RANK = 384
CONV_K = 4
EPS = 1e-6
NEG = -1e30
ROPE_THETA = 10000.0
ADAM_LR, ADAM_B1, ADAM_B2, ADAM_EPS, ADAM_WD, ADAM_STEP = 0.001, 0.9, 0.999, 1e-08, 0.01, 10
PACK_COLS = 512
VMEM_LIMIT = 56 * 1024 * 1024


def _pick(n, options):
    for o in options:
        if n % o == 0:
            return o
    raise ValueError(f"no tile for {n} among {options}")


def _cparams(sem):
    return pltpu.CompilerParams(dimension_semantics=sem, vmem_limit_bytes=VMEM_LIMIT)


def _dims(a, dims):
    if a.ndim == 2:
        return (dims, ((), ()))
    (ca,), (cb,) = dims
    return (((ca + 1,), (cb + 1,)), ((0,), (0,)))


def _dot(a, b, dims):
    return lax.dot_general(a.astype(_MXU_DTYPE), b.astype(_MXU_DTYPE), _dims(a, dims), preferred_element_type=F32)


@jax.custom_vjp
def mm_nn(a, b):
    return _dot(a, b, ((1,), (0,)))


@jax.custom_vjp
def mm_nt(a, b):
    return _dot(a, b, ((1,), (1,)))


@jax.custom_vjp
def mm_tn(a, b):
    return _dot(a, b, ((0,), (0,)))


mm_nn.defvjp(lambda a, b: (mm_nn(a, b), (a, b)), lambda r, g: (mm_nt(g, r[1]), mm_tn(r[0], g)))
mm_nt.defvjp(lambda a, b: (mm_nt(a, b), (a, b)), lambda r, g: (mm_nn(g, r[1]), mm_tn(g, r[0])))
mm_tn.defvjp(lambda a, b: (mm_tn(a, b), (a, b)), lambda r, g: (mm_nt(r[1], g), mm_nn(r[0], g)))


def _split_terms(x, n):
    terms, rest = [], x
    for _ in range(n):
        t = rest.astype(_MXU_DTYPE)
        terms.append(t)
        rest = rest - t.astype(F32)
    return terms


def _dot_01_raw(m, x, dims):
    m = m.astype(_MXU_DTYPE)
    return sum(lax.dot_general(m, t, _dims(m, dims), preferred_element_type=F32) for t in _split_terms(x, 3))


@jax.custom_vjp
def _dot_01(m, x):
    return _dot_01_raw(m, x, ((1,), (0,)))


_dot_01.defvjp(lambda m, x: (_dot_01(m, x), m), lambda m, g: (jnp.zeros_like(m), _dot_01_raw(m, g, ((0,), (0,)))))


def _inv_unit_lower(a):
    n = a.shape[-1]
    eye = (lax.broadcasted_iota(jnp.int32, (n, n), 0) == lax.broadcasted_iota(jnp.int32, (n, n), 1)).astype(F32)
    d = lambda u, w: lax.dot_general(u, w, _dims(u, ((1,), (0,))), preferred_element_type=F32)
    t = eye - a
    p = a.astype(_MXU_DTYPE)
    p = d(p, p)
    squarings = int(math.log2(n)) - 1
    for s in range(squarings):
        ph = p.astype(_MXU_DTYPE)
        t_hi, t_lo = _split_terms(t, 2)
        t = t + (d(t_hi, ph) + d(t_lo, ph))
        if s + 1 < squarings:
            p = d(ph, ph)
    return t


@jax.custom_vjp
def _inv_lookup(a, t):
    return t


def _inv_lookup_bwd(t, g):
    return -mm_tn(t, mm_nt(g, t)), jnp.zeros_like(t)


_inv_lookup.defvjp(lambda a, t: (t, t), _inv_lookup_bwd)


def _sigmoid(x):
    return 1.0 / (1.0 + jnp.exp(-x))


@jax.custom_vjp
def _silu(x):
    return x * _sigmoid(x)


def _silu_fwd(x):
    s = _sigmoid(x)
    return x * s, (x, s)


_silu.defvjp(_silu_fwd, lambda r, g: (g * (r[1] * (1.0 + r[0] * (1.0 - r[1]))),))


def _softplus(x):
    return jnp.where(x > 20.0, x, jnp.log(1.0 + jnp.exp(jnp.minimum(x, 20.0))))


def _rms(x, g, width=None):
    ms = jnp.sum(x * x, -1, keepdims=True) / (x.shape[-1] if width is None else width)
    return x * lax.rsqrt(ms + EPS) * g


MM_VMEM_BUDGET = 40 * 1024 * 1024


def _matmul_rows(name, a, b, mode, out_dtype, res, scatter):
    m, k = a.shape
    n = b.shape[1] if mode == "nn" else b.shape[0]
    dims = {"nn": ((1,), (0,)), "nt": ((1,), (1,))}[mode]
    out_bytes = jnp.dtype(out_dtype).itemsize
    n_in, nx = 2 + (res is not None), len(scatter)

    def vmem(tm):
        blocks = 2 * tm * k * a.dtype.itemsize + 2 * k * n * b.dtype.itemsize + 2 * tm * n * out_bytes + tm * n * 4
        return blocks + (2 * tm * n * res.dtype.itemsize if res is not None else 0)

    tm = next(c for c in (2176, 1088, 512, 256, 128, 64) if m % c == 0 and vmem(c) <= MM_VMEM_BUDGET)
    steps = m // tm

    def body(*refs):
        a_ref, b_ref, o_ref = refs[0], refs[1], refs[n_in + nx]
        i = pl.program_id(0)
        finish = _ride(scatter, True, refs[n_in:n_in + nx], refs[n_in + nx + 1:n_in + 2 * nx + 1], refs[n_in + 2 * nx + 1:], i == 0, i == steps - 1)
        out = _dot(a_ref[...], b_ref[...], dims)
        if res is not None:
            out = out + refs[2][...].astype(F32)
        o_ref[...] = out.astype(o_ref.dtype)
        finish()

    o_spec = pl.BlockSpec((tm, n), lambda i: (i, 0))
    in_specs = [pl.BlockSpec((tm, k), lambda i: (i, 0)), pl.BlockSpec(b.shape, lambda i: (0, 0))] + ([o_spec] if res is not None else [])
    args = (a, b) + ((res,) if res is not None else ())
    out = pl.pallas_call(
        body, grid=(steps,), in_specs=in_specs + [_HBM] * nx, out_specs=[o_spec] + [_HBM] * nx,
        out_shape=[jax.ShapeDtypeStruct((m, n), out_dtype)] + Exchange.out_shape(scatter, True), scratch_shapes=Exchange.scratch(nx) if nx else [],
        compiler_params=_cparams(("arbitrary",) if nx else ("parallel",)), name=name)(*args, *scatter)
    return out if nx else out[0]


def matmul(name, a, b, mode, out_dtype=F32, res=None, scatter=()):
    if mode != "tn":
        return _matmul_rows(name, a, b, mode, out_dtype, res, scatter)
    (k, m), (k2, n) = a.shape, b.shape
    assert k == k2 and res is None, (name, a.shape, b.shape, mode)
    tm = _pick(m, (1024, 512, 384, 256, 128))
    tn = _pick(n, (1024, 512, 384, 256, 128))
    tk = _pick(k, (512, 256, 128))
    nk = k // tk
    dims = ((0,), (0,))

    def body(*refs):
        if res is None:
            a_ref, b_ref, o_ref, acc_ref = refs
        else:
            a_ref, b_ref, r_ref, o_ref, acc_ref = refs
        kk = pl.program_id(2)

        @pl.when(kk == 0)
        def _():
            acc_ref[...] = jnp.zeros_like(acc_ref)

        acc_ref[...] += _dot(a_ref[...], b_ref[...], dims)

        @pl.when(kk == nk - 1)
        def _():
            out = acc_ref[...]
            if res is not None:
                out = out + r_ref[...].astype(F32)
            o_ref[...] = out.astype(o_ref.dtype)

    a_spec = pl.BlockSpec((tk, tm), lambda i, j, kk: (kk, i)) if mode == "tn" else pl.BlockSpec((tm, tk), lambda i, j, kk: (i, kk))
    b_spec = pl.BlockSpec((tn, tk), lambda i, j, kk: (j, kk)) if mode == "nt" else pl.BlockSpec((tk, tn), lambda i, j, kk: (kk, j))
    o_spec = pl.BlockSpec((tm, tn), lambda i, j, kk: (i, j))
    in_specs = [a_spec, b_spec] + ([o_spec] if res is not None else [])
    args = (a, b) + ((res,) if res is not None else ())
    return pl.pallas_call(
        body, grid=(m // tm, n // tn, nk), in_specs=in_specs, out_specs=o_spec,
        out_shape=jax.ShapeDtypeStruct((m, n), out_dtype), scratch_shapes=[pltpu.VMEM((tm, tn), F32)],
        compiler_params=_cparams(("parallel", "parallel", "arbitrary")), name=name)(*args)


@dataclasses.dataclass
class Arg:
    arr: jax.Array
    kind: str = "row"
    bc: int = 0
    base: int = 0
    ph: bool = False
    diff: bool = False
    gdt: object = F32


def _arg_spec(a, tr, nh, ntab, base=None):
    bc = a.bc or a.arr.shape[1]
    base = a.base if base is None else base
    width = bc * nh if a.ph else bc
    col = base // nh if a.ph else base
    assert not a.ph or base % nh == 0
    if a.kind == "row":
        return pl.BlockSpec((tr, width), lambda i: (i, col))
    if a.kind == "tab":
        return pl.BlockSpec((tr, width), lambda i: (i % ntab, col))
    return pl.BlockSpec((a.arr.shape[0], width), lambda i: (0, col))


def _head_view(ref, a, h):
    bc = a.bc or a.arr.shape[1]
    v = ref[:, h * bc:(h + 1) * bc] if a.ph else ref[...]
    return v.astype(F32) if jnp.issubdtype(v.dtype, jnp.floating) else v


def row_call(name, fn, args, outs, tr, nh=1, ntab=1):
    t = args[0].arr.shape[0]
    n_in = len(args)
    out_args = [Arg(None, "row", bc, 0, ph) for (_, _, bc, ph) in outs]

    def body(*refs):
        for h in range(nh):
            res = fn(*[_head_view(r, a, h) for r, a in zip(refs[:n_in], args, strict=True)])
            for r, a, v in zip(refs[n_in:], out_args, res, strict=True):
                if a.ph:
                    r[:, h * a.bc:(h + 1) * a.bc] = v.astype(r.dtype)
                elif h == nh - 1:
                    r[...] = v.astype(r.dtype)

    return pl.pallas_call(
        body, grid=(t // tr,), in_specs=[_arg_spec(a, tr, nh, ntab) for a in args], out_specs=[_arg_spec(a, tr, nh, ntab) for a in out_args],
        out_shape=[jax.ShapeDtypeStruct((t, cols), dt) for (cols, dt, _, _) in outs],
        compiler_params=_cparams(("arbitrary",)), name=name)(*[a.arr for a in args])


def row_vjp_call(name, fn, args, cts, tr, nh=1, ntab=1):
    t = args[0].arr.shape[0]
    n_in, n_ct = len(args), len(cts)
    diff_idx = [k for k, a in enumerate(args) if a.diff]

    def body(*refs):
        out_refs = refs[n_in + n_ct:]
        shared = [None] * len(diff_idx)
        for k, r in zip(diff_idx, out_refs, strict=True):
            if args[k].kind == "par":
                @pl.when(pl.program_id(0) == 0)
                def _(r=r):
                    r[...] = jnp.zeros_like(r)

        for h in range(nh):
            vals = [_head_view(r, a, h) for r, a in zip(refs[:n_in], args, strict=True)]
            ct_vals = tuple(_head_view(r, a, h) for r, a in zip(refs[n_in:n_in + n_ct], cts, strict=True))

            def f(*dv, vals=vals):
                full = list(vals)
                for k, v in zip(diff_idx, dv, strict=True):
                    full[k] = v
                return tuple(fn(*full))

            _, vjp = jax.vjp(f, *[vals[k] for k in diff_idx])
            for j, (k, r, g) in enumerate(zip(diff_idx, out_refs, vjp(ct_vals), strict=True)):
                a = args[k]
                bc = a.bc or a.arr.shape[1]
                if not a.ph:
                    shared[j] = g if shared[j] is None else shared[j] + g
                elif a.kind == "row":
                    r[:, h * bc:(h + 1) * bc] = g.astype(r.dtype)
                else:
                    r[:, h * bc:(h + 1) * bc] += g
        for j, (k, r) in enumerate(zip(diff_idx, out_refs, strict=True)):
            if not args[k].ph:
                if args[k].kind == "row":
                    r[...] = shared[j].astype(r.dtype)
                else:
                    r[...] += shared[j]

    out_specs, out_shape = [], []
    for k in diff_idx:
        a = args[k]
        bc = a.bc or a.arr.shape[1]
        out_specs.append(_arg_spec(a, tr, nh, ntab, base=0))
        out_shape.append(jax.ShapeDtypeStruct((t if a.kind == "row" else a.arr.shape[0], bc * (nh if a.ph else 1)), a.gdt if a.kind == "row" else F32))
    in_specs = [_arg_spec(a, tr, nh, ntab) for a in list(args) + list(cts)]
    return pl.pallas_call(
        body, grid=(t // tr,), in_specs=in_specs, out_specs=out_specs, out_shape=out_shape,
        compiler_params=_cparams(("arbitrary",)), name=name)(*[a.arr for a in list(args) + list(cts)])


def _conv_taps(x, w):
    rows = lax.broadcasted_iota(jnp.int32, x.shape, 0)
    y = x * w[CONV_K - 1:CONV_K, :]
    for s in range(1, CONV_K):
        y = y + jnp.where(rows >= s, pltpu.roll(x, s, 0), 0.0) * w[CONV_K - 1 - s:CONV_K - s, :]
    return y


CONV_HEADS = 4
CONV_BLOCKS_PER_THIRD = N_HEADS // CONV_HEADS


def _conv_post(y, block):
    a = _silu(y)
    normed = block < 2 * CONV_BLOCKS_PER_THIRD
    scale = jnp.where(block < CONV_BLOCKS_PER_THIRD, HEAD ** -0.5, 1.0)
    return a * jnp.where(normed, lax.rsqrt(jnp.sum(a * a, -1, keepdims=True) + EPS) * scale, 1.0)


def conv_fwd(z, w, lp):
    t, width = z.shape
    cols = CONV_HEADS * HEAD

    def body(z_ref, w_ref, o_ref, y_ref):
        block = pl.program_id(1)
        for h in range(CONV_HEADS):
            cs = slice(h * HEAD, (h + 1) * HEAD)
            y = _conv_taps(z_ref[:, cs], w_ref[:, cs])
            y_ref[:, cs] = y
            o_ref[:, cs] = _conv_post(y, block)

    blk = pl.BlockSpec((lp, cols), lambda b, j: (b, j))
    out = jax.ShapeDtypeStruct((t, width), F32)
    return pl.pallas_call(
        body, grid=(t // lp, width // cols), in_specs=[blk, pl.BlockSpec((CONV_K, cols), lambda b, j: (0, j))],
        out_specs=[blk, blk], out_shape=[out, out], compiler_params=_cparams(("arbitrary", "arbitrary")), name="a_conv_fwd")(z, w)


def conv_bwd(z, y, w, dout, lp):
    t, width = z.shape
    cols = CONV_HEADS * HEAD

    def body(z_ref, y_ref, w_ref, g_ref, dz_ref, dw_ref):
        block = pl.program_id(0)

        @pl.when(pl.program_id(1) == 0)
        def _():
            dw_ref[...] = jnp.zeros_like(dw_ref)

        for h in range(CONV_HEADS):
            cs = slice(h * HEAD, (h + 1) * HEAD)
            x, wv = z_ref[:, cs], w_ref[:, cs]
            _, vjp = jax.vjp(lambda y_: _conv_post(y_, block), y_ref[:, cs])
            (dy,) = vjp(g_ref[:, cs])
            rows = lax.broadcasted_iota(jnp.int32, x.shape, 0)
            dx = dy * wv[CONV_K - 1:CONV_K, :]
            dw_ref[CONV_K - 1:CONV_K, cs] += jnp.sum(dy * x, axis=0, keepdims=True)
            for s in range(1, CONV_K):
                dy_up = jnp.where(rows < lp - s, pltpu.roll(dy, lp - s, 0), 0.0)
                dx = dx + dy_up * wv[CONV_K - 1 - s:CONV_K - s, :]
                dw_ref[CONV_K - 1 - s:CONV_K - s, cs] += jnp.sum(dy_up * x, axis=0, keepdims=True)
            dz_ref[:, cs] = dx.astype(dz_ref.dtype)

    blk = pl.BlockSpec((lp, cols), lambda j, b: (b, j))
    w_blk = pl.BlockSpec((CONV_K, cols), lambda j, b: (0, j))
    return pl.pallas_call(
        body, grid=(width // cols, t // lp), in_specs=[blk, blk, w_blk, blk], out_specs=[blk, w_blk],
        out_shape=[jax.ShapeDtypeStruct((t, width), _MXU_DTYPE), jax.ShapeDtypeStruct((CONV_K, width), F32)],
        compiler_params=_cparams(("arbitrary", "arbitrary")), name="a_conv_bwd")(z, y, w, dout)


def _delta_chunk(q, k, v, ba, alog, dtb, state, t_stored):
    n_g, c = q.shape[0], q.shape[1]
    lane = lax.broadcasted_iota(jnp.int32, (1, HEAD), 1)

    def pick(xs, offset):
        cols = [jnp.sum(xs[i // N_HEADS if len(xs) > 1 else 0] * (lane == offset + i % N_HEADS).astype(F32), axis=1, keepdims=True)[None]
                for i in range(n_g)]
        return jnp.concatenate(cols, 0)

    b_raw, a_raw = pick(ba, 0), pick(ba, N_HEADS)
    a_log, dt_bias = pick((alog,), 0), pick((dtb,), 0)
    beta = _sigmoid(b_raw)
    g = -jnp.exp(a_log) * _softplus(a_raw + dt_bias)
    ri = lax.broadcasted_iota(jnp.int32, (c, c), 0)
    ci = lax.broadcasted_iota(jnp.int32, (c, c), 1)
    tril = ci <= ri
    lower = jnp.broadcast_to(tril.astype(F32), (n_g, c, c))
    gc_col = _dot_01(lower, g * jnp.ones((1, 1, HEAD), F32))[:, :, :1]
    gc_row = _dot_01(jnp.ones((n_g, 8, c), F32), g * (ri <= ci).astype(F32)[None])[:, 0:1, :]
    gc_last = jnp.sum(g, axis=1, keepdims=True)
    decay = jnp.exp(jnp.where(tril, gc_col - gc_row, NEG))
    e_gc = jnp.exp(gc_col)
    kb = k * beta
    a_mat = jnp.where(ci < ri, mm_nt(kb, k) * decay, 0.0)
    t_inv = _inv_unit_lower(a_mat) if t_stored is None else _inv_lookup(a_mat, t_stored)
    u_base = mm_nn(t_inv, v * beta)
    w_dec = mm_nn(t_inv, kb * e_gc)
    attn = jnp.where(tril, mm_nt(q, k) * decay, 0.0)
    u = u_base - mm_nn(w_dec, state)
    o = mm_nn(q * e_gc, state) + mm_nn(attn, u)
    new_state = state * jnp.exp(gc_last) + mm_tn(k * jnp.exp(gc_last - gc_col), u)
    return o, new_state, t_inv


DELTA_CHUNKS_FWD = 2
DELTA_CHUNKS_BWD = 2
DELTA_SEQS = 2
DELTA_BATCH = DELTA_SEQS * N_HEADS


def _heads_of(ref, rs, first_col):
    return jnp.stack([ref[i // N_HEADS, rs, first_col + (i % N_HEADS) * HEAD:first_col + (i % N_HEADS + 1) * HEAD] for i in range(DELTA_BATCH)])


def _qkv_heads(ref, rs, part):
    return _heads_of(ref, rs, part * N_HEADS * HEAD)


def _by_sequence(a, lp):
    return a.reshape(a.shape[0] // lp, lp, a.shape[1])


def _ride(bufs, scatter, refs_in, refs_out, sems, first, last):
    if not bufs:
        return lambda: None

    @pl.when(first)
    def _():
        Exchange(refs_in, refs_out, *sems, scatter).start()

    def finish():
        @pl.when(last)
        def _():
            Exchange(refs_in, refs_out, *sems, scatter).wait()

    return finish


def delta_fwd(qkv, ba, ba_block, alog, dtb, lp, gather=()):
    t = qkv.shape[0]
    nb, nc = t // lp, lp // CHUNK
    cps = DELTA_CHUNKS_FWD
    ng, rows = nc // cps, cps * CHUNK
    nx = len(gather)
    nbg = nb // DELTA_SEQS
    assert nc % cps == 0 and nb % DELTA_SEQS == 0

    def body(*refs):
        qkv_ref, ba_ref, al_ref, dt_ref = refs[:4]
        o_ref, s_ref, t_ref = refs[4 + nx:7 + nx]
        state_ref = refs[7 + 2 * nx]
        b, n = pl.program_id(0), pl.program_id(1)
        finish = _ride(gather, False, refs[4:4 + nx], refs[7 + nx:7 + 2 * nx], refs[8 + 2 * nx:], (b == 0) & (n == 0), (b == nbg - 1) & (n == ng - 1))

        @pl.when(n == 0)
        def _():
            state_ref[...] = jnp.zeros_like(state_ref)

        al, dtv = al_ref[...], dt_ref[...]
        for c in range(cps):
            rs = slice(c * CHUNK, (c + 1) * CHUNK)
            state = state_ref[...]
            o, new_state, t_inv = _delta_chunk(_qkv_heads(qkv_ref, rs, 0), _qkv_heads(qkv_ref, rs, 1), _qkv_heads(qkv_ref, rs, 2),
                                               tuple(ba_ref[i, rs, :] for i in range(DELTA_SEQS)), al, dtv, state, None)
            for i in range(DELTA_BATCH):
                seq, g = divmod(i, N_HEADS)
                o_ref[seq, rs, g * HEAD:(g + 1) * HEAD] = o[i]
                s_ref[seq, g, c] = state[i]
                t_ref[seq, g, c] = t_inv[i]
            state_ref[...] = new_state
        finish()

    rows_of = lambda width: pl.BlockSpec((DELTA_SEQS, rows, width), lambda b, n: (b, n, 0))
    par_spec = pl.BlockSpec((1, HEAD), lambda b, n: (0, 0))
    out = pl.pallas_call(
        body, grid=(nbg, ng),
        in_specs=[rows_of(3 * N_HEADS * HEAD), pl.BlockSpec((DELTA_SEQS, rows, HEAD), lambda b, n: (b, n, ba_block)), par_spec, par_spec] + [_HBM] * nx,
        out_specs=[rows_of(N_HEADS * HEAD), pl.BlockSpec((DELTA_SEQS, N_HEADS, cps, HEAD, HEAD), lambda b, n: (b, 0, n, 0, 0)),
                   pl.BlockSpec((DELTA_SEQS, N_HEADS, cps, CHUNK, CHUNK), lambda b, n: (b, 0, n, 0, 0))] + [_HBM] * nx,
        out_shape=[jax.ShapeDtypeStruct((nb, lp, N_HEADS * HEAD), F32), jax.ShapeDtypeStruct((nb, N_HEADS, nc, HEAD, HEAD), F32),
                   jax.ShapeDtypeStruct((nb, N_HEADS, nc, CHUNK, CHUNK), F32)] + Exchange.out_shape(gather, False),
        scratch_shapes=[pltpu.VMEM((DELTA_BATCH, HEAD, HEAD), F32)] + (Exchange.scratch(nx) if nx else []),
        compiler_params=_cparams(("arbitrary", "arbitrary")), name="delta_fwd")(_by_sequence(qkv, lp), _by_sequence(ba, lp), alog, dtb, *gather)
    return [out[0].reshape(t, N_HEADS * HEAD)] + list(out[1:])


def delta_bwd(qkv, ba, ba_block, alog, dtb, states, t_invs, do, lp, scatter=()):
    t = qkv.shape[0]
    nb, nc = t // lp, lp // CHUNK
    cps = DELTA_CHUNKS_BWD
    ng, rows = nc // cps, cps * CHUNK
    nx = len(scatter)
    nbg = nb // DELTA_SEQS

    def body(*refs):
        qkv_ref, ba_ref, al_ref, dt_ref, s_ref, t_ref, do_ref = refs[:7]
        dqkv_ref, dba_ref, dal_ref, ddt_ref = refs[7 + nx:11 + nx]
        dstate_ref = refs[11 + 2 * nx]
        b, step = pl.program_id(0), pl.program_id(1)
        finish = _ride(scatter, True, refs[7:7 + nx], refs[11 + nx:11 + 2 * nx], refs[12 + 2 * nx:], (b == 0) & (step == 0),
                       (b == nbg - 1) & (step == ng - 1))

        @pl.when(step == 0)
        def _():
            dstate_ref[...] = jnp.zeros_like(dstate_ref)

        @pl.when((b == 0) & (step == 0))
        def _():
            dal_ref[...] = jnp.zeros_like(dal_ref)
            ddt_ref[...] = jnp.zeros_like(ddt_ref)

        al, dtv = al_ref[...], dt_ref[...]
        d_al = jnp.zeros((1, HEAD), F32)
        d_dt = jnp.zeros((1, HEAD), F32)
        for c in reversed(range(cps)):
            rs = slice(c * CHUNK, (c + 1) * CHUNK)
            t_n = jnp.stack([t_ref[i // N_HEADS, i % N_HEADS, c] for i in range(DELTA_BATCH)])
            s_n = jnp.stack([s_ref[i // N_HEADS, i % N_HEADS, c] for i in range(DELTA_BATCH)])

            def f(q_, k_, v_, ba_, al_, dt_, s_, t_n=t_n):
                return _delta_chunk(q_, k_, v_, ba_, al_, dt_, s_, t_n)[:2]

            _, vjp = jax.vjp(f, _qkv_heads(qkv_ref, rs, 0), _qkv_heads(qkv_ref, rs, 1), _qkv_heads(qkv_ref, rs, 2), tuple(ba_ref[i, rs, :] for i in range(DELTA_SEQS)), al, dtv, s_n)
            grads = vjp((_heads_of(do_ref, rs, 0), dstate_ref[...]))
            for part in range(3):
                for i in range(DELTA_BATCH):
                    col = (part * N_HEADS + i % N_HEADS) * HEAD
                    dqkv_ref[i // N_HEADS, rs, col:col + HEAD] = grads[part][i]
            for i in range(DELTA_SEQS):
                dba_ref[i, rs, :] = grads[3][i]
            d_al, d_dt = d_al + grads[4], d_dt + grads[5]
            dstate_ref[...] = grads[6]
        dal_ref[...] += d_al
        ddt_ref[...] += d_dt
        finish()

    rows_of = lambda width: pl.BlockSpec((DELTA_SEQS, rows, width), lambda b, n: (b, ng - 1 - n, 0))
    par_spec = pl.BlockSpec((1, HEAD), lambda b, n: (0, 0))
    out = pl.pallas_call(
        body, grid=(nbg, ng),
        in_specs=[rows_of(3 * N_HEADS * HEAD), pl.BlockSpec((DELTA_SEQS, rows, HEAD), lambda b, n: (b, ng - 1 - n, ba_block)), par_spec, par_spec,
                  pl.BlockSpec((DELTA_SEQS, N_HEADS, cps, HEAD, HEAD), lambda b, n: (b, 0, ng - 1 - n, 0, 0)),
                  pl.BlockSpec((DELTA_SEQS, N_HEADS, cps, CHUNK, CHUNK), lambda b, n: (b, 0, ng - 1 - n, 0, 0)), rows_of(N_HEADS * HEAD)] + [_HBM] * nx,
        out_specs=[rows_of(3 * N_HEADS * HEAD), rows_of(HEAD), par_spec, par_spec] + [_HBM] * nx,
        out_shape=[jax.ShapeDtypeStruct((nb, lp, 3 * N_HEADS * HEAD), F32), jax.ShapeDtypeStruct((nb, lp, HEAD), F32),
                   jax.ShapeDtypeStruct((1, HEAD), F32), jax.ShapeDtypeStruct((1, HEAD), F32)] + Exchange.out_shape(scatter, True),
        scratch_shapes=[pltpu.VMEM((DELTA_BATCH, HEAD, HEAD), F32)] + (Exchange.scratch(nx) if nx else []),
        compiler_params=_cparams(("arbitrary", "arbitrary")), name="delta_bwd")(
            _by_sequence(qkv, lp), _by_sequence(ba, lp), alog, dtb, states, t_invs, _by_sequence(do, lp), *scatter)
    return [out[0].reshape(t, 3 * N_HEADS * HEAD), out[1].reshape(t, HEAD)] + list(out[2:])


ATT_Q_TILE = 256
ATT_K_TILE = 512
ATT_SCALE = QK_DIM ** -0.5


def _tiles(end, size):
    return [(s, min(s + size, end)) for s in range(0, end, size)]


def _att_visible(q0, q1, k0, k1, keys_first):
    if k1 <= q0 + CHUNK and k0 >= PAD_ROWS:
        return None
    shape = (k1 - k0, q1 - q0) if keys_first else (q1 - q0, k1 - k0)
    qpos = q0 + lax.broadcasted_iota(jnp.int32, shape, 1 if keys_first else 0)
    kpos = k0 + lax.broadcasted_iota(jnp.int32, shape, 0 if keys_first else 1)
    shift = CHUNK.bit_length() - 1
    return (jnp.right_shift(kpos, shift) <= jnp.right_shift(qpos, shift)) & (kpos >= PAD_ROWS)


def _att_seq_specs(lp):
    return pl.BlockSpec((lp, QK_PAD), lambda b, h: (b, h)), pl.BlockSpec((lp, HEAD), lambda b, h: (b, h))


def flash_fwd(q, k, v, lp):
    t = q.shape[0]
    qk_seq, o_seq = _att_seq_specs(lp)

    def body(q_ref, k_ref, v_ref, o_ref, lse_ref):
        q_tiles = _tiles(lp, ATT_Q_TILE)

        def score_steps(q0, q1, out):
            def step(k0, k1):
                s = mm_nt(q_ref[q0:q1, :], k_ref[k0:k1, :])
                vis = _att_visible(q0, q1, k0, k1, False)
                s = s if vis is None else jnp.where(vis, s, NEG)
                out["scores"].append(s)
                row_max = jnp.max(s, -1, keepdims=True)
                out["m"] = row_max if out["m"] is None else jnp.maximum(out["m"], row_max)
            return [functools.partial(step, k0, k1) for k0, k1 in _tiles(q1, ATT_K_TILE)]

        cur = {"scores": [], "m": None}
        for step in score_steps(*q_tiles[0], cur):
            step()
        for i, (q0, q1) in enumerate(q_tiles):
            nxt = {"scores": [], "m": None}
            ahead = score_steps(*q_tiles[i + 1], nxt) if i + 1 < len(q_tiles) else []
            l = jnp.zeros((q1 - q0, 1), F32)
            acc = jnp.zeros((q1 - q0, HEAD), F32)
            for s, (k0, k1) in zip(cur["scores"], _tiles(q1, ATT_K_TILE), strict=True):
                if ahead:
                    ahead.pop(0)()
                p = jnp.exp(s - cur["m"])
                l = l + jnp.sum(p, -1, keepdims=True)
                acc = acc + mm_nn(p, v_ref[k0:k1, :])
            for step in ahead:
                step()
            o_ref[q0:q1, :] = acc / l
            lse_ref[q0:q1, :] = jnp.broadcast_to(cur["m"] + jnp.log(l), (q1 - q0, HEAD))
            cur = nxt

    big = jax.ShapeDtypeStruct((t, N_HEADS * HEAD), F32)
    return pl.pallas_call(
        body, grid=(t // lp, N_HEADS), in_specs=[qk_seq, qk_seq, o_seq], out_specs=[o_seq, o_seq], out_shape=[big, big],
        compiler_params=_cparams(("arbitrary", "arbitrary")), name="flash_fwd")(q, k, v)


def flash_bwd(q, k, v, o, lse, do, lp):
    t = q.shape[0]
    qk_seq, o_seq = _att_seq_specs(lp)

    def body(q_ref, k_ref, v_ref, o_ref, lse_ref, do_ref, dq_ref, dk_ref, dv_ref):
        dk_ref[...] = jnp.zeros_like(dk_ref)
        dv_ref[...] = jnp.zeros_like(dv_ref)
        for q0, q1 in _tiles(lp, ATT_Q_TILE):
            qb, dob = q_ref[q0:q1, :], do_ref[q0:q1, :]
            lse_row = jnp.transpose(lse_ref[q0:q1, :])[0:1, :]
            dsum_row = jnp.sum(jnp.transpose(dob * o_ref[q0:q1, :]), axis=0, keepdims=True)
            dq = jnp.zeros((q1 - q0, QK_PAD), F32)
            for k0, k1 in _tiles(q1, ATT_K_TILE):
                kb, vb = k_ref[k0:k1, :], v_ref[k0:k1, :]
                s = mm_nt(kb, qb)
                vis = _att_visible(q0, q1, k0, k1, True)
                s = s if vis is None else jnp.where(vis, s, NEG)
                p = jnp.exp(s - lse_row)
                ds = p * (mm_nt(vb, dob) - dsum_row)
                dv_ref[k0:k1, :] += mm_nn(p, dob)
                dk_ref[k0:k1, :] += mm_nn(ds, qb)
                dq = dq + mm_tn(ds, kb)
            dq_ref[q0:q1, :] = dq

    return pl.pallas_call(
        body, grid=(t // lp, N_HEADS), in_specs=[qk_seq, qk_seq, o_seq, o_seq, o_seq, o_seq], out_specs=[qk_seq, qk_seq, o_seq],
        out_shape=[jax.ShapeDtypeStruct((t, N_HEADS * QK_PAD), F32), jax.ShapeDtypeStruct((t, N_HEADS * QK_PAD), F32),
                   jax.ShapeDtypeStruct((t, N_HEADS * HEAD), F32)],
        compiler_params=_cparams(("arbitrary", "arbitrary")), name="flash_bwd")(q, k, v, o, lse, do)


def loss_head(h2, target, lp):
    nb, seq, d = target.shape
    tr = 128
    nblk = lp // tr
    lead_blocks = LEAD // tr

    def body(h_ref, t_ref, loss_ref, dh_ref, acc_ref):
        b, i = pl.program_id(0), pl.program_id(1)

        @pl.when((b == 0) & (i == 0))
        def _():
            acc_ref[...] = jnp.zeros_like(acc_ref)

        @pl.when(i < lead_blocks)
        def _():
            dh_ref[...] = jnp.zeros_like(dh_ref)

        @pl.when(i >= lead_blocks)
        def _():
            err = h_ref[...] - t_ref[...]
            dh_ref[...] = err * (1.0 / d)
            acc_ref[...] += jnp.sum(err * err, axis=0, keepdims=True)

        @pl.when((b == nb - 1) & (i == nblk - 1))
        def _():
            loss_ref[...] = jnp.sum(acc_ref[...], axis=1, keepdims=True) * (0.5 / d)

    return pl.pallas_call(
        body, grid=(nb, nblk),
        in_specs=[pl.BlockSpec((None, tr, d), lambda b, i: (b, i, 0)),
                  pl.BlockSpec((None, tr, d), lambda b, i: (b, jnp.maximum(i - lead_blocks, 0), 0))],
        out_specs=[pl.BlockSpec((1, 1), lambda b, i: (0, 0)), pl.BlockSpec((None, tr, d), lambda b, i: (b, i, 0))],
        out_shape=[jax.ShapeDtypeStruct((1, 1), F32), jax.ShapeDtypeStruct((nb, lp, d), F32)],
        scratch_shapes=[pltpu.VMEM((1, d), F32)], compiler_params=_cparams(("arbitrary", "arbitrary")), name="loss_head")(h2, target)


def meta_grad(dh0):
    nb, _, d = dh0.shape

    def body(g_ref, o_ref):
        @pl.when(pl.program_id(0) == 0)
        def _():
            o_ref[...] = jnp.zeros_like(o_ref)

        o_ref[...] += g_ref[PAD_ROWS:LEAD, :]

    return pl.pallas_call(
        body, grid=(nb,), in_specs=[pl.BlockSpec((None, LEAD, d), lambda b: (b, 0, 0))],
        out_specs=pl.BlockSpec((N_META, d), lambda b: (0, 0)), out_shape=jax.ShapeDtypeStruct((N_META, d), F32),
        compiler_params=_cparams(("arbitrary",)), name="meta_grad")(dh0)


_HBM = pl.BlockSpec(memory_space=pltpu.HBM)


def _mesh_pos():
    x, y, c = lax.axis_index("x"), lax.axis_index("y"), lax.axis_index("c")
    return x, y, c


def _peer(x, y, c, k):
    px = 1 - x if k & 4 else x
    py = 1 - y if k & 2 else y
    pc = 1 - c if k & 1 else c
    return (px, py, pc), 4 * px + 2 * py + pc


class Exchange:
    def __init__(self, x_refs, out_refs, send_sems, recv_sems, local_sems, scatter):
        self.x_refs, self.out_refs, self.scatter = x_refs, out_refs, scatter
        self.send_sems, self.recv_sems, self.local_sems = send_sems, recv_sems, local_sems
        self.pos = _mesh_pos()
        x, y, c = self.pos
        self.me = 4 * x + 2 * y + c

    @staticmethod
    def scratch(n):
        return [pltpu.SemaphoreType.DMA((n, N_DEV - 1)), pltpu.SemaphoreType.DMA((n, N_DEV - 1)), pltpu.SemaphoreType.DMA((n,))]

    @staticmethod
    def out_shape(bufs, scatter):
        return [jax.ShapeDtypeStruct(b.shape if scatter else (N_DEV,) + b.shape, b.dtype) for b in bufs]

    def _local(self, i):
        return pltpu.make_async_copy(self.x_refs[i].at[self.me] if self.scatter else self.x_refs[i], self.out_refs[i].at[self.me], self.local_sems.at[i])

    def _copy(self, i, k, landing):
        peer, peer_id = _peer(*self.pos, k)
        src = self.x_refs[i].at[peer_id] if self.scatter else self.x_refs[i]
        return pltpu.make_async_remote_copy(src_ref=src, dst_ref=self.out_refs[i].at[peer_id if landing else self.me],
                                            send_sem=self.send_sems.at[i, k - 1], recv_sem=self.recv_sems.at[i, k - 1],
                                            device_id=peer, device_id_type=pl.DeviceIdType.MESH)

    def start(self):
        for i in range(len(self.x_refs)):
            self._local(i).start()
        for k in range(1, N_DEV):
            for i in range(len(self.x_refs)):
                self._copy(i, k, False).start()

    def wait(self):
        for k in range(1, N_DEV):
            for i in range(len(self.x_refs)):
                self._copy(i, k, True).wait_recv()
        for k in range(1, N_DEV):
            for i in range(len(self.x_refs)):
                self._copy(i, k, False).wait_send()
        for i in range(len(self.x_refs)):
            self._local(i).wait()


def _exchange(name, bufs, scatter):
    n = len(bufs)

    def body(*refs):
        ex = Exchange(refs[:n], refs[n:2 * n], *refs[2 * n:], scatter)
        ex.start()
        ex.wait()

    return pl.pallas_call(body, in_specs=[_HBM] * n, out_specs=[_HBM] * n, out_shape=Exchange.out_shape(bufs, scatter),
                          scratch_shapes=Exchange.scratch(n), name=name)(*bufs)


def _f_rms(x, g):
    return (_rms(x, g),)


def _f_rms2(x, g1, g2):
    r = x * lax.rsqrt(jnp.sum(x * x, -1, keepdims=True) / x.shape[-1] + EPS)
    return r * g1, r * g2


def _f_out_gate(o, gate, gain):
    return (_rms(o, gain) * _silu(gate),)


def _f_gate(o, gate):
    return (o * _silu(gate),)


@jax.custom_vjp
def _swap_rope_halves(x):
    half = ROPE // 2
    lane = lax.broadcasted_iota(jnp.int32, x.shape, 1)
    return jnp.where(lane < half, pltpu.roll(x, HEAD - half, 1), jnp.where(lane < ROPE, pltpu.roll(x, half, 1), 0.0))


_swap_rope_halves.defvjp(lambda x: (_swap_rope_halves(x), None), lambda _, g: (_swap_rope_halves(g),))


def _f_qk_final(scale, nope, rope_in, g_nope, g_rope, cos, sin):
    ms = (jnp.sum(nope * nope, -1, keepdims=True) + jnp.sum(rope_in * rope_in, -1, keepdims=True)) / QK_DIM
    r = lax.rsqrt(ms + EPS)
    a = nope * r * g_nope
    b = rope_in * r * g_rope
    out = jnp.concatenate([a, b * cos + _swap_rope_halves(b) * sin], axis=1)
    return (out if scale == 1.0 else out * scale,)


def _rope_tables(lp):
    half = ROPE // 2
    pos = jnp.maximum(jnp.arange(lp) - PAD_ROWS, 0)
    inv = ROPE_THETA ** (-jnp.arange(half, dtype=F32) / half)
    ang = pos.astype(F32)[:, None] * inv[None, :]
    zeros = jnp.zeros((lp, HEAD - ROPE), F32)
    cos = jnp.concatenate([jnp.cos(ang), jnp.cos(ang), zeros], 1)
    sin = jnp.concatenate([-jnp.sin(ang), jnp.sin(ang), zeros], 1)
    return cos, sin


def _pad_lanes(w, width=HEAD):
    return jnp.pad(w, ((0, 0), (0, width - w.shape[1])))


def _pad_rows(w, rows=HEAD):
    return jnp.pad(w, ((0, rows - w.shape[0]), (0, 0)))


def _split_heads_qk_t(w_t):
    k = w_t.shape[1]
    w3 = w_t.reshape(N_HEADS, QK_DIM, k)
    nope = w3[:, :HEAD].reshape(N_HEADS * HEAD, k)
    rope = jnp.pad(w3[:, HEAD:], ((0, 0), (0, HEAD - ROPE), (0, 0))).reshape(N_HEADS * HEAD, k)
    return jnp.concatenate([nope, rope], 0)


def _merge_heads_qk_t(g_t):
    k = g_t.shape[1]
    kw = N_HEADS * HEAD
    nope, rope = g_t[:kw].reshape(N_HEADS, HEAD, k), g_t[kw:].reshape(N_HEADS, HEAD, k)[:, :ROPE]
    return jnp.concatenate([nope, rope], 1).reshape(N_HEADS * QK_DIM, k)


def local_step(x, target, w, deferred=None):
    nb, seq, d = x.shape
    lp = seq + LEAD
    t = nb * lp
    tr = _pick(lp, (544, 128))
    ntab = lp // tr
    mxu = _MXU_DTYPE
    kw = N_HEADS * HEAD

    a_w_in_t = w["a_w_in"].astype(mxu)
    w_qkv_t, w_gba_t = a_w_in_t[:3 * kw], _pad_rows(a_w_in_t[3 * kw:], kw + HEAD)
    a_conv = w["a_conv"].T
    alog, dtb, o_gain = _pad_lanes(w["a_log"]), _pad_lanes(w["a_dt_bias"]), w["a_o_gain"]
    a_norm, kv_norm, b_norm = w["a_norm"], w["kv_norm"][None, :], w["b_norm"]
    lat_norm, qlat_norm = w["kv_latent_norm"][None, :], w["b_q_latent_norm"]
    kg_nope, kg_rope = w["k_gain"][None, :HEAD], _pad_lanes(w["k_gain"][None, HEAD:])
    qg_nope, qg_rope = w["b_q_gain"][:, :HEAD], _pad_lanes(w["b_q_gain"][:, HEAD:])
    cos, sin = _rope_tables(lp)

    meta = jnp.broadcast_to(w["meta_tokens"].T[None], (nb, N_META, d))
    h0 = jnp.concatenate([jnp.zeros((nb, PAD_ROWS, d), F32), meta, x], 1).reshape(t, d)
    (hn,) = row_call("a_norm_fwd", _f_rms, [Arg(h0), Arg(a_norm, "par")], [(d, mxu, d, False)], tr)
    z_qkv = matmul("a_in_qkv", hn, w_qkv_t, "nt")
    z_gba = matmul("a_in_gate_ba", hn, w_gba_t, "nt")
    ba_block = kw // HEAD
    qkv_a, y_conv = conv_fwd(z_qkv, a_conv, lp)
    o_a, states, t_invs, *gathered = delta_fwd(qkv_a, z_gba, ba_block, alog, dtb, lp, gather=deferred.gather_bufs if deferred else ())
    if deferred:
        w = {**w, **deferred.finish(gathered)}
    a_w_out = w["a_w_out"].astype(mxu)
    w_down = _pad_lanes(w["kv_w_down"], KV_RANK + HEAD).astype(mxu)
    w_ukv_t = jnp.concatenate([w["kv_w_uk"], w["kv_w_uv"]], 0).astype(mxu)
    b_w_in_t = w["b_w_in"].astype(mxu)
    w_cq_t, w_gb_t = b_w_in_t[:Q_RANK], b_w_in_t[Q_RANK:]
    w_q_t = _split_heads_qk_t(w["b_w_uq"]).astype(mxu)
    b_w_out = w["b_w_out"].astype(mxu)
    og_args = [Arg(o_a, bc=HEAD, ph=True, diff=True), Arg(z_gba, bc=HEAD, ph=True, diff=True, gdt=mxu), Arg(o_gain, "par", diff=True)]
    (og_a,) = row_call("a_out_gate_fwd", _f_out_gate, og_args, [(kw, mxu, HEAD, True)], tr, nh=N_HEADS)
    h1 = matmul("a_out", og_a, a_w_out, "nn", res=h0)

    hk, hb = row_call("b_norms_fwd", _f_rms2, [Arg(h1), Arg(kv_norm, "par"), Arg(b_norm, "par")], [(d, mxu, d, False), (d, mxu, d, False)], tr)
    c_down = matmul("kv_down", hk, w_down, "nn")
    c_kv_arg = Arg(c_down, bc=KV_RANK, diff=True, gdt=mxu)
    k_pe_arg = Arg(c_down, bc=HEAD, base=KV_RANK // HEAD, diff=True)
    c_q_raw = matmul("b_in_q", hb, w_cq_t, "nt")
    gate_b = matmul("b_in_gate", hb, w_gb_t, "nt")
    (c_kv,) = row_call("kv_latent_fwd", _f_rms, [c_kv_arg, Arg(lat_norm, "par")], [(KV_RANK, mxu, KV_RANK, False)], tr)
    (c_q,) = row_call("q_latent_fwd", _f_rms, [Arg(c_q_raw), Arg(qlat_norm, "par")], [(Q_RANK, mxu, Q_RANK, False)], tr)
    k_nope = matmul("k_up", c_kv, w_ukv_t[:kw], "nt")
    v_b = matmul("v_up", c_kv, w_ukv_t[kw:], "nt", out_dtype=mxu)
    q_up = matmul("q_up", c_q, w_q_t, "nt")
    tabs = [Arg(cos, "tab"), Arg(sin, "tab")]
    k_args = [Arg(k_nope, bc=HEAD, ph=True, diff=True, gdt=mxu), k_pe_arg, Arg(kg_nope, "par", diff=True), Arg(kg_rope, "par", diff=True)] + tabs
    q_args = [Arg(q_up, bc=HEAD, ph=True, diff=True, gdt=mxu), Arg(q_up, bc=HEAD, base=N_HEADS, ph=True, diff=True, gdt=mxu),
              Arg(qg_nope, "par", diff=True), Arg(qg_rope, "par", diff=True)] + tabs
    f_k_final, f_q_final = functools.partial(_f_qk_final, 1.0), functools.partial(_f_qk_final, ATT_SCALE)
    (k_fin,) = row_call("k_final_fwd", f_k_final, k_args, [(N_HEADS * QK_PAD, mxu, QK_PAD, True)], tr, nh=N_HEADS, ntab=ntab)
    (q_fin,) = row_call("q_final_fwd", f_q_final, q_args, [(N_HEADS * QK_PAD, mxu, QK_PAD, True)], tr, nh=N_HEADS, ntab=ntab)
    o_b, lse = flash_fwd(q_fin, k_fin, v_b, lp)
    gb_args = [Arg(o_b, diff=True), Arg(gate_b, diff=True, gdt=mxu)]
    (og_b,) = row_call("b_gate_fwd", _f_gate, gb_args, [(kw, mxu, kw, False)], tr)
    h2 = matmul("b_out", og_b, b_w_out, "nn", res=h1)

    loss, dh2 = loss_head(h2.reshape(nb, lp, d), target, lp)
    dh2 = dh2.reshape(t, d)
    grads = {}

    d_og_b = matmul("b_out_dx", dh2, b_w_out, "nt", out_dtype=mxu)
    grads["b_w_out"] = matmul("b_out_dw", og_b, dh2, "tn")
    d_o_b, d_gate_b = row_vjp_call("b_gate_bwd", _f_gate, gb_args, [Arg(d_og_b)], tr)
    dq_fin, dk_fin, dv_b = flash_bwd(q_fin, k_fin, v_b, o_b, lse, d_o_b, lp)
    dq_nope, dq_rope, d_qg_nope, d_qg_rope = row_vjp_call(
        "q_final_bwd", f_q_final, q_args, [Arg(dq_fin, bc=QK_PAD, ph=True)], tr, nh=N_HEADS, ntab=ntab)
    dk_nope, dk_pe, d_kg_nope, d_kg_rope = row_vjp_call(
        "k_final_bwd", f_k_final, k_args, [Arg(dk_fin, bc=QK_PAD, ph=True)], tr, nh=N_HEADS, ntab=ntab)
    grads["b_q_gain"] = jnp.concatenate([d_qg_nope, d_qg_rope[:, :ROPE]], 1)
    grads["k_gain"] = jnp.concatenate([d_kg_nope, d_kg_rope[:, :ROPE]], 1)[0]
    d_c_q = matmul("q_nope_dx", dq_nope, w_q_t[:kw], "nn")
    d_c_q = matmul("q_rope_dx", dq_rope, w_q_t[kw:], "nn", res=d_c_q)
    grads["b_w_uq"] = _merge_heads_qk_t(jnp.concatenate([matmul("q_nope_dw", dq_nope, c_q, "tn"), matmul("q_rope_dw", dq_rope, c_q, "tn")], 0))
    d_c_kv = matmul("k_up_dx", dk_nope, w_ukv_t[:kw], "nn")
    d_c_kv = matmul("v_up_dx", dv_b, w_ukv_t[kw:], "nn", res=d_c_kv)
    grads["kv_w_uk"], grads["kv_w_uv"] = matmul("k_up_dw", dk_nope, c_kv, "tn"), matmul("v_up_dw", dv_b, c_kv, "tn")
    d_c_q_raw, grads["b_q_latent_norm"] = row_vjp_call(
        "q_latent_bwd", _f_rms, [Arg(c_q_raw, diff=True, gdt=mxu), Arg(qlat_norm, "par", diff=True)], [Arg(d_c_q)], tr)
    d_c_kv_raw, d_lat = row_vjp_call(
        "kv_latent_bwd", _f_rms, [c_kv_arg, Arg(lat_norm, "par", diff=True)], [Arg(d_c_kv)], tr)
    grads["kv_latent_norm"] = d_lat[0]
    d_hb = matmul("b_in_q_dx", d_c_q_raw, w_cq_t, "nn")
    d_hb = matmul("b_in_gate_dx", d_gate_b, w_gb_t, "nn", res=d_hb, out_dtype=mxu)
    grads["b_w_in"] = jnp.concatenate([matmul("b_in_q_dw", d_c_q_raw, hb, "tn"), matmul("b_in_gate_dw", d_gate_b, hb, "tn")], 0)
    d_c_down = jnp.concatenate([d_c_kv_raw, dk_pe.astype(mxu)], 1)
    d_hk = matmul("kv_down_dx", d_c_down, w_down, "nt", out_dtype=mxu)
    grads["kv_w_down"] = matmul("kv_down_dw", hk, d_c_down, "tn")[:, :KV_RANK + ROPE]
    dh1, d_kv_norm, grads["b_norm"] = row_vjp_call(
        "b_norms_bwd", lambda x_, g1, g2: _f_rms2(x_, g1, g2) + (x_,),
        [Arg(h1, diff=True), Arg(kv_norm, "par", diff=True), Arg(b_norm, "par", diff=True)], [Arg(d_hk), Arg(d_hb), Arg(dh2)], tr)
    grads["kv_norm"] = d_kv_norm[0]

    d_og_a = matmul("a_out_dx", dh1, a_w_out, "nt", out_dtype=mxu)
    grads["a_w_out"] = matmul("a_out_dw", og_a, dh1, "tn")
    d_o_a, d_gate_a, grads["a_o_gain"] = row_vjp_call(
        "a_out_gate_bwd", _f_out_gate, og_args, [Arg(d_og_a, bc=HEAD, ph=True)], tr, nh=N_HEADS)
    dqkv_a, d_ba, d_alog, d_dtb, *received = delta_bwd(qkv_a, z_gba, ba_block, alog, dtb, states, t_invs, d_o_a, lp,
                                                        scatter=deferred.scatter_bufs(grads) if deferred else ())
    grads["a_log"], grads["a_dt_bias"] = d_alog[:, :N_HEADS], d_dtb[:, :N_HEADS]
    dz_qkv, d_conv = conv_bwd(z_qkv, y_conv, a_conv, dqkv_a, lp)
    grads["a_conv"] = d_conv.T
    dz_gba = jnp.concatenate([d_gate_a, d_ba.astype(mxu)], 1)
    grads["a_w_in"] = jnp.concatenate([matmul("a_in_qkv_dw", dz_qkv, hn, "tn"), matmul("a_in_gate_ba_dw", dz_gba, hn, "tn")[:kw + 2 * N_HEADS]], 0)
    ride = deferred.last_scatter_bufs(grads) if deferred else ()
    d_hn = matmul("a_in_qkv_dx", dz_qkv, w_qkv_t, "nn", scatter=ride)
    if ride:
        d_hn, *received_last = d_hn
        received = list(received) + received_last
    d_hn = matmul("a_in_gate_ba_dx", dz_gba, w_gba_t, "nn", res=d_hn, out_dtype=mxu)
    dh0, grads["a_norm"] = row_vjp_call("a_norm_bwd", lambda x_, g_: _f_rms(x_, g_) + (x_,),
                                        [Arg(h0, diff=True), Arg(a_norm, "par", diff=True)], [Arg(d_hn), Arg(dh1)], tr)
    dh0 = dh0.reshape(nb, lp, d)
    grads["meta_tokens"] = meta_grad(dh0).T
    return loss, dh0[:, LEAD:], grads, received


_SHARDED = (
    ("meta_tokens", True, False), ("a_norm", True, False), ("a_w_in", True, True), ("a_conv", True, False), ("a_w_out", False, True),
    ("kv_w_down", False, True), ("kv_w_uk", True, True), ("kv_w_uv", True, True), ("b_w_in", True, True), ("b_w_uq", True, True),
    ("b_w_out", False, True))
_REPLICATED = ("a_log", "a_dt_bias", "a_o_gain", "kv_norm", "kv_latent_norm", "k_gain", "b_norm", "b_q_latent_norm", "b_q_gain")
_ALL_WEIGHTS = ("meta_tokens", "a_norm", "a_w_in", "a_conv", "a_log", "a_dt_bias", "a_o_gain", "a_w_out", "kv_norm", "kv_w_down",
                "kv_latent_norm", "kv_w_uk", "kv_w_uv", "k_gain", "b_norm", "b_w_in", "b_q_latent_norm", "b_w_uq", "b_q_gain", "b_w_out")


def _round_up(n, m):
    return (n + m - 1) // m * m


def _pack_rows(pieces, row_multiple):
    padded = []
    for p in pieces:
        n = p.shape[-1]
        padded.append(jnp.pad(p, [(0, 0)] * (p.ndim - 1) + [(0, _round_up(n, PACK_COLS) - n)]))
    flat = jnp.concatenate(padded, -1)
    rows = _round_up(flat.shape[-1] // PACK_COLS, row_multiple)
    flat = jnp.pad(flat, [(0, 0)] * (flat.ndim - 1) + [(0, rows * PACK_COLS - flat.shape[-1])])
    return flat.reshape(flat.shape[:-1] + (rows, PACK_COLS))


def _unpack_rows(buf, sizes):
    flat = buf.reshape(buf.shape[:-2] + (-1,))
    out, off = [], 0
    for n in sizes:
        out.append(flat[..., off:off + n])
        off += _round_up(n, PACK_COLS)
    return out


def _shard_2d(a):
    return a.reshape(a.shape[-2:]) if a.ndim > 2 else a


def _kl_shard(a, by_cols):
    return _shard_2d(a).T if by_cols else _shard_2d(a)


_GROUPS_FIRST = (("a_w_in",),)
_GROUPS_LATER = (("a_w_out", "b_w_in", "b_w_out"), ("b_w_uq",), ("kv_w_down",), ("kv_w_uk", "kv_w_uv"))
_SMALL_SHARDED = ("meta_tokens", "a_norm", "a_conv")
_BY_COLS = {name: by_cols for name, by_cols, _ in _SHARDED}
ROW_ALIGN = 16


def _stack_rows(pieces):
    padded, starts, row = [], [], 0
    for p in pieces:
        r = p.shape[-2]
        padded.append(jnp.pad(p, [(0, 0)] * (p.ndim - 2) + [(0, _round_up(r, ROW_ALIGN) - r), (0, 0)]))
        starts.append(row)
        row += _round_up(r, ROW_ALIGN)
    return jnp.concatenate(padded, -2), starts


def _stack_group(arrays_by_name, names):
    arrays = [arrays_by_name[n].astype(BF16) for n in names]
    buf, starts = _stack_rows(arrays)
    return buf, [(n, s, a.shape[-2]) for n, s, a in zip(names, starts, arrays, strict=True)]


def _stack_groups(arrays_by_name, groups):
    stacked = [_stack_group(arrays_by_name, names) for names in groups]
    return [b for b, _ in stacked], [entries for _, entries in stacked]


def _full_from_gathered(gathered, layout):
    full = {}
    for got, entries in zip(gathered, layout, strict=True):
        for name, start, rows in entries:
            full[name] = got[:, start:start + rows].reshape(N_DEV * rows, got.shape[-1])
    return full


def gather_first_weights(local):
    shards = {n: _kl_shard(local[n], _BY_COLS[n]) for names in _GROUPS_FIRST for n in names}
    bufs, layout = _stack_groups(shards, _GROUPS_FIRST)
    small = [_kl_shard(local[n], _BY_COLS[n]) for n in _SMALL_SHARDED]
    bufs.append(_pack_rows([s.reshape(-1) for s in small], 8))
    gathered = _exchange("all_gather", bufs, scatter=False)
    full = _full_from_gathered(gathered[:-1], layout)
    for name, part, sh in zip(_SMALL_SHARDED, _unpack_rows(gathered[-1], [s.size for s in small]), small, strict=True):
        full[name] = part.reshape(N_DEV * sh.shape[0], sh.shape[1])
    full["a_norm"] = full["a_norm"].reshape(1, -1)
    return full


class LaterExchanges:
    def __init__(self, local):
        shards = {n: _kl_shard(local[n], _BY_COLS[n]) for names in _GROUPS_LATER for n in names}
        self.gather_bufs, self.layout = _stack_groups(shards, _GROUPS_LATER)

    def finish(self, gathered):
        return _full_from_gathered(gathered, self.layout)

    def scatter_bufs(self, grads):
        return _stack_groups(_owner_slices(grads, _GROUPS_LATER), _GROUPS_LATER)[0]

    def last_scatter_bufs(self, grads):
        bufs, self.last_layout = _stack_groups(_owner_slices(grads, _GROUPS_FIRST), _GROUPS_FIRST)
        return bufs


def _owner_slices(grads, groups):
    return {n: grads[n].reshape(N_DEV, -1, grads[n].shape[-1]) for names in groups for n in names}


def reduce_contributions(name, recv):
    _, r, c = recv.shape
    tr = _pick(r, (256, 128, 64, 32, 16, 8))

    def body(g_ref, o_ref):
        g = g_ref[0].astype(F32)
        for dev in range(1, N_DEV):
            g = g + g_ref[dev].astype(F32)
        o_ref[...] = g

    return pl.pallas_call(
        body, grid=(r // tr,), in_specs=[pl.BlockSpec((N_DEV, tr, c), lambda i: (0, i, 0))], out_specs=pl.BlockSpec((tr, c), lambda i: (i, 0)),
        out_shape=jax.ShapeDtypeStruct((r, c), F32), compiler_params=_cparams(("arbitrary",)), name=name)(recv)


def adamw_all(gs, ws, ms, vs):
    n = len(gs)

    def body(*refs):
        for i in range(n):
            g_ref, w_ref, m_ref, v_ref = (refs[j * n + i] for j in range(4))
            d_ref, mo_ref, vo_ref = (refs[(4 + j) * n + i] for j in range(3))
            g = g_ref[...]
            m_new = ADAM_B1 * m_ref[...] + (1.0 - ADAM_B1) * g
            v_new = ADAM_B2 * v_ref[...] + (1.0 - ADAM_B2) * (g * g)
            m_hat = m_new / (1.0 - ADAM_B1 ** ADAM_STEP)
            v_hat = v_new / (1.0 - ADAM_B2 ** ADAM_STEP)
            d_ref[...] = -ADAM_LR * (m_hat / (jnp.sqrt(v_hat) + ADAM_EPS) + ADAM_WD * w_ref[...])
            mo_ref[...] = m_new
            vo_ref[...] = v_new

    out = [jax.ShapeDtypeStruct(g.shape, F32) for g in gs] * 3
    res = pl.pallas_call(body, out_shape=out, compiler_params=pltpu.CompilerParams(vmem_limit_bytes=VMEM_LIMIT), name="adamw_all")(*gs, *ws, *ms, *vs)
    return res[:n], res[n:2 * n], res[2 * n:]


def kernel(x, meta_tokens, a_norm, a_w_in, a_conv, a_log, a_dt_bias, a_o_gain, a_w_out, kv_norm, kv_w_down, kv_latent_norm, kv_w_uk, kv_w_uv, k_gain, b_norm, b_w_in, b_q_latent_norm, b_w_uq, b_q_gain, b_w_out, loss_target, m_meta_tokens, m_a_norm, m_a_w_in, m_a_conv, m_a_log, m_a_dt_bias, m_a_o_gain, m_a_w_out, m_kv_norm, m_kv_w_down, m_kv_latent_norm, m_kv_w_uk, m_kv_w_uv, m_k_gain, m_b_norm, m_b_w_in, m_b_q_latent_norm, m_b_w_uq, m_b_q_gain, m_b_w_out, v_meta_tokens, v_a_norm, v_a_w_in, v_a_conv, v_a_log, v_a_dt_bias, v_a_o_gain, v_a_w_out, v_kv_norm, v_kv_w_down, v_kv_latent_norm, v_kv_w_uk, v_kv_w_uv, v_k_gain, v_b_norm, v_b_w_in, v_b_q_latent_norm, v_b_w_uq, v_b_q_gain, v_b_w_out):
    given = dict(locals())
    local_w = {n: given[n] for n in _ALL_WEIGHTS}
    full = gather_first_weights(local_w)
    for n in _REPLICATED:
        full[n] = local_w[n]
    later = LaterExchanges(local_w)

    loss_part, grad_x, grads, received_riding = local_step(x, loss_target, full, later)

    exact = [grads[n].reshape(N_DEV, -1) for n in _SMALL_SHARDED]
    exact += [jnp.broadcast_to(grads[n].reshape(1, -1), (N_DEV, grads[n].size)) for n in _REPLICATED]
    exact.append(jnp.broadcast_to(loss_part, (N_DEV, 1)))
    received = list(received_riding) + list(_exchange("all_to_all", [_pack_rows(exact, 8)], scatter=True))
    layout = later.layout + later.last_layout
    summed = [reduce_contributions(f"reduce_{i}", r) for i, r in enumerate(received)]

    grad_kl = {}
    for got, entries in zip(summed, layout):
        for n, start, rows in entries:
            grad_kl[n] = got[start:start + rows]
    parts = _unpack_rows(summed[-1], [p.shape[1] for p in exact])
    for n, part in zip(_SMALL_SHARDED + _REPLICATED, parts, strict=False):
        grad_kl[n] = part
    loss = parts[-1][0]

    def natural_2d(n, a):
        shape = _shard_2d(local_w[n]).shape if local_w[n].ndim > 1 else (1, local_w[n].size)
        return a.reshape(shape[::-1]).T if _BY_COLS.get(n, False) else a.reshape(shape)

    as_2d = lambda n, a: a.reshape(natural_2d(n, grad_kl[n]).shape)
    gs = [natural_2d(n, grad_kl[n]) for n in _ALL_WEIGHTS]
    deltas, new_m, new_v = adamw_all(gs, [as_2d(n, local_w[n]) for n in _ALL_WEIGHTS], [as_2d(n, given["m_" + n]) for n in _ALL_WEIGHTS],
                                     [as_2d(n, given["v_" + n]) for n in _ALL_WEIGHTS])
    results = [a.reshape(local_w[n].shape) for group in (gs, deltas, new_m, new_v) for n, a in zip(_ALL_WEIGHTS, group, strict=True)]
    return (loss, grad_x, *results)
```

```python
import dataclasses
import functools
import math

import jax
import jax.numpy as jnp
from jax import lax
from jax.experimental import pallas as pl
from jax.experimental.pallas import tpu as pltpu

F32 = jnp.float32
BF16 = jnp.bfloat16
_MXU_DTYPE = jnp.bfloat16

N_DEV = 8
D_MODEL = 1024
N_HEADS = 8
HEAD = 128
CHUNK = 64
N_META = 16
PAD_ROWS = 2 * CHUNK - N_META
LEAD = PAD_ROWS + N_META
ROPE = 64
QK_DIM = HEAD + ROPE
QK_PAD = 2 * HEAD
KV_RANK = 256
Q_RANK = 384
CONV_K = 4
EPS = 1e-6
NEG = -1e30
ROPE_THETA = 10000.0
ADAM_LR, ADAM_B1, ADAM_B2, ADAM_EPS, ADAM_WD, ADAM_STEP = 0.001, 0.9, 0.999, 1e-08, 0.01, 10
PACK_COLS = 512
VMEM_LIMIT = 56 * 1024 * 1024


def _pick(n, options):
    for o in options:
        if n % o == 0:
            return o
    raise ValueError(f"no tile for {n} among {options}")


def _cparams(sem):
    return pltpu.CompilerParams(dimension_semantics=sem, vmem_limit_bytes=VMEM_LIMIT)


def _dims(a, dims):
    if a.ndim == 2:
        return (dims, ((), ()))
    (ca,), (cb,) = dims
    return (((ca + 1,), (cb + 1,)), ((0,), (0,)))


def _dot(a, b, dims):
    return lax.dot_general(a.astype(_MXU_DTYPE), b.astype(_MXU_DTYPE), _dims(a, dims), preferred_element_type=F32)


@jax.custom_vjp
def mm_nn(a, b):
    return _dot(a, b, ((1,), (0,)))


@jax.custom_vjp
def mm_nt(a, b):
    return _dot(a, b, ((1,), (1,)))


@jax.custom_vjp
def mm_tn(a, b):
    return _dot(a, b, ((0,), (0,)))


mm_nn.defvjp(lambda a, b: (mm_nn(a, b), (a, b)), lambda r, g: (mm_nt(g, r[1]), mm_tn(r[0], g)))
mm_nt.defvjp(lambda a, b: (mm_nt(a, b), (a, b)), lambda r, g: (mm_nn(g, r[1]), mm_tn(g, r[0])))
mm_tn.defvjp(lambda a, b: (mm_tn(a, b), (a, b)), lambda r, g: (mm_nt(r[1], g), mm_nn(r[0], g)))


def _split_terms(x, n):
    terms, rest = [], x
    for _ in range(n):
        t = rest.astype(_MXU_DTYPE)
        terms.append(t)
        rest = rest - t.astype(F32)
    return terms


def _dot_01_raw(m, x, dims):
    m = m.astype(_MXU_DTYPE)
    return sum(lax.dot_general(m, t, _dims(m, dims), preferred_element_type=F32) for t in _split_terms(x, 3))


@jax.custom_vjp
def _dot_01(m, x):
    return _dot_01_raw(m, x, ((1,), (0,)))


_dot_01.defvjp(lambda m, x: (_dot_01(m, x), m), lambda m, g: (jnp.zeros_like(m), _dot_01_raw(m, g, ((0,), (0,)))))


def _inv_unit_lower(a):
    n = a.shape[-1]
    eye = (lax.broadcasted_iota(jnp.int32, (n, n), 0) == lax.broadcasted_iota(jnp.int32, (n, n), 1)).astype(F32)
    d = lambda u, w: lax.dot_general(u, w, _dims(u, ((1,), (0,))), preferred_element_type=F32)
    t = eye - a
    p = a.astype(_MXU_DTYPE)
    p = d(p, p)
    squarings = int(math.log2(n)) - 1
    for s in range(squarings):
        ph = p.astype(_MXU_DTYPE)
        t_hi, t_lo = _split_terms(t, 2)
        t = t + (d(t_hi, ph) + d(t_lo, ph))
        if s + 1 < squarings:
            p = d(ph, ph)
    return t


@jax.custom_vjp
def _inv_lookup(a, t):
    return t


def _inv_lookup_bwd(t, g):
    return -mm_tn(t, mm_nt(g, t)), jnp.zeros_like(t)


_inv_lookup.defvjp(lambda a, t: (t, t), _inv_lookup_bwd)


def _sigmoid(x):
    return 1.0 / (1.0 + jnp.exp(-x))


@jax.custom_vjp
def _silu(x):
    return x * _sigmoid(x)


def _silu_fwd(x):
    s = _sigmoid(x)
    return x * s, (x, s)


_silu.defvjp(_silu_fwd, lambda r, g: (g * (r[1] * (1.0 + r[0] * (1.0 - r[1]))),))


def _softplus(x):
    return jnp.where(x > 20.0, x, jnp.log(1.0 + jnp.exp(jnp.minimum(x, 20.0))))


def _rms(x, g, width=None):
    ms = jnp.sum(x * x, -1, keepdims=True) / (x.shape[-1] if width is None else width)
    return x * lax.rsqrt(ms + EPS) * g


MM_VMEM_BUDGET = 40 * 1024 * 1024


def _matmul_rows(name, a, b, mode, out_dtype, res, scatter):
    m, k = a.shape
    n = b.shape[1] if mode == "nn" else b.shape[0]
    dims = {"nn": ((1,), (0,)), "nt": ((1,), (1,))}[mode]
    out_bytes = jnp.dtype(out_dtype).itemsize
    n_in, nx = 2 + (res is not None), len(scatter)

    def vmem(tm):
        blocks = 2 * tm * k * a.dtype.itemsize + 2 * k * n * b.dtype.itemsize + 2 * tm * n * out_bytes + tm * n * 4
        return blocks + (2 * tm * n * res.dtype.itemsize if res is not None else 0)

    tm = next(c for c in (2176, 1088, 512, 256, 128, 64) if m % c == 0 and vmem(c) <= MM_VMEM_BUDGET)
    steps = m // tm

    def body(*refs):
        a_ref, b_ref, o_ref = refs[0], refs[1], refs[n_in + nx]
        i = pl.program_id(0)
        finish = _ride(scatter, True, refs[n_in:n_in + nx], refs[n_in + nx + 1:n_in + 2 * nx + 1], refs[n_in + 2 * nx + 1:], i == 0, i == steps - 1)
        out = _dot(a_ref[...], b_ref[...], dims)
        if res is not None:
            out = out + refs[2][...].astype(F32)
        o_ref[...] = out.astype(o_ref.dtype)
        finish()

    o_spec = pl.BlockSpec((tm, n), lambda i: (i, 0))
    in_specs = [pl.BlockSpec((tm, k), lambda i: (i, 0)), pl.BlockSpec(b.shape, lambda i: (0, 0))] + ([o_spec] if res is not None else [])
    args = (a, b) + ((res,) if res is not None else ())
    out = pl.pallas_call(
        body, grid=(steps,), in_specs=in_specs + [_HBM] * nx, out_specs=[o_spec] + [_HBM] * nx,
        out_shape=[jax.ShapeDtypeStruct((m, n), out_dtype)] + Exchange.out_shape(scatter, True), scratch_shapes=Exchange.scratch(nx) if nx else [],
        compiler_params=_cparams(("arbitrary",) if nx else ("parallel",)), name=name)(*args, *scatter)
    return out if nx else out[0]


def matmul(name, a, b, mode, out_dtype=F32, res=None, scatter=()):
    if mode != "tn":
        return _matmul_rows(name, a, b, mode, out_dtype, res, scatter)
    (k, m), (k2, n) = a.shape, b.shape
    assert k == k2 and res is None, (name, a.shape, b.shape, mode)
    tm = _pick(m, (1024, 512, 384, 256, 128))
    tn = _pick(n, (1024, 512, 384, 256, 128))
    tk = _pick(k, (512, 256, 128))
    nk = k // tk
    dims = ((0,), (0,))

    def body(*refs):
        if res is None:
            a_ref, b_ref, o_ref, acc_ref = refs
        else:
            a_ref, b_ref, r_ref, o_ref, acc_ref = refs
        kk = pl.program_id(2)

        @pl.when(kk == 0)
        def _():
            acc_ref[...] = jnp.zeros_like(acc_ref)

        acc_ref[...] += _dot(a_ref[...], b_ref[...], dims)

        @pl.when(kk == nk - 1)
        def _():
            out = acc_ref[...]
            if res is not None:
                out = out + r_ref[...].astype(F32)
            o_ref[...] = out.astype(o_ref.dtype)

    a_spec = pl.BlockSpec((tk, tm), lambda i, j, kk: (kk, i)) if mode == "tn" else pl.BlockSpec((tm, tk), lambda i, j, kk: (i, kk))
    b_spec = pl.BlockSpec((tn, tk), lambda i, j, kk: (j, kk)) if mode == "nt" else pl.BlockSpec((tk, tn), lambda i, j, kk: (kk, j))
    o_spec = pl.BlockSpec((tm, tn), lambda i, j, kk: (i, j))
    in_specs = [a_spec, b_spec] + ([o_spec] if res is not None else [])
    args = (a, b) + ((res,) if res is not None else ())
    return pl.pallas_call(
        body, grid=(m // tm, n // tn, nk), in_specs=in_specs, out_specs=o_spec,
        out_shape=jax.ShapeDtypeStruct((m, n), out_dtype), scratch_shapes=[pltpu.VMEM((tm, tn), F32)],
        compiler_params=_cparams(("parallel", "parallel", "arbitrary")), name=name)(*args)


@dataclasses.dataclass
class Arg:
    arr: jax.Array
    kind: str = "row"
    bc: int = 0
    base: int = 0
    ph: bool = False
    diff: bool = False
    gdt: object = F32


def _arg_spec(a, tr, nh, ntab, base=None):
    bc = a.bc or a.arr.shape[1]
    base = a.base if base is None else base
    width = bc * nh if a.ph else bc
    col = base // nh if a.ph else base
    assert not a.ph or base % nh == 0
    if a.kind == "row":
        return pl.BlockSpec((tr, width), lambda i: (i, col))
    if a.kind == "tab":
        return pl.BlockSpec((tr, width), lambda i: (i % ntab, col))
    return pl.BlockSpec((a.arr.shape[0], width), lambda i: (0, col))


def _head_view(ref, a, h):
    bc = a.bc or a.arr.shape[1]
    v = ref[:, h * bc:(h + 1) * bc] if a.ph else ref[...]
    return v.astype(F32) if jnp.issubdtype(v.dtype, jnp.floating) else v


def row_call(name, fn, args, outs, tr, nh=1, ntab=1):
    t = args[0].arr.shape[0]
    n_in = len(args)
    out_args = [Arg(None, "row", bc, 0, ph) for (_, _, bc, ph) in outs]

    def body(*refs):
        for h in range(nh):
            res = fn(*[_head_view(r, a, h) for r, a in zip(refs[:n_in], args, strict=True)])
            for r, a, v in zip(refs[n_in:], out_args, res, strict=True):
                if a.ph:
                    r[:, h * a.bc:(h + 1) * a.bc] = v.astype(r.dtype)
                elif h == nh - 1:
                    r[...] = v.astype(r.dtype)

    return pl.pallas_call(
        body, grid=(t // tr,), in_specs=[_arg_spec(a, tr, nh, ntab) for a in args], out_specs=[_arg_spec(a, tr, nh, ntab) for a in out_args],
        out_shape=[jax.ShapeDtypeStruct((t, cols), dt) for (cols, dt, _, _) in outs],
        compiler_params=_cparams(("arbitrary",)), name=name)(*[a.arr for a in args])


def row_vjp_call(name, fn, args, cts, tr, nh=1, ntab=1):
    t = args[0].arr.shape[0]
    n_in, n_ct = len(args), len(cts)
    diff_idx = [k for k, a in enumerate(args) if a.diff]

    def body(*refs):
        out_refs = refs[n_in + n_ct:]
        shared = [None] * len(diff_idx)
        for k, r in zip(diff_idx, out_refs, strict=True):
            if args[k].kind == "par":
                @pl.when(pl.program_id(0) == 0)
                def _(r=r):
                    r[...] = jnp.zeros_like(r)

        for h in range(nh):
            vals = [_head_view(r, a, h) for r, a in zip(refs[:n_in], args, strict=True)]
            ct_vals = tuple(_head_view(r, a, h) for r, a in zip(refs[n_in:n_in + n_ct], cts, strict=True))

            def f(*dv, vals=vals):
                full = list(vals)
                for k, v in zip(diff_idx, dv, strict=True):
                    full[k] = v
                return tuple(fn(*full))

            _, vjp = jax.vjp(f, *[vals[k] for k in diff_idx])
            for j, (k, r, g) in enumerate(zip(diff_idx, out_refs, vjp(ct_vals), strict=True)):
                a = args[k]
                bc = a.bc or a.arr.shape[1]
                if not a.ph:
                    shared[j] = g if shared[j] is None else shared[j] + g
                elif a.kind == "row":
                    r[:, h * bc:(h + 1) * bc] = g.astype(r.dtype)
                else:
                    r[:, h * bc:(h + 1) * bc] += g
        for j, (k, r) in enumerate(zip(diff_idx, out_refs, strict=True)):
            if not args[k].ph:
                if args[k].kind == "row":
                    r[...] = shared[j].astype(r.dtype)
                else:
                    r[...] += shared[j]

    out_specs, out_shape = [], []
    for k in diff_idx:
        a = args[k]
        bc = a.bc or a.arr.shape[1]
        out_specs.append(_arg_spec(a, tr, nh, ntab, base=0))
        out_shape.append(jax.ShapeDtypeStruct((t if a.kind == "row" else a.arr.shape[0], bc * (nh if a.ph else 1)), a.gdt if a.kind == "row" else F32))
    in_specs = [_arg_spec(a, tr, nh, ntab) for a in list(args) + list(cts)]
    return pl.pallas_call(
        body, grid=(t // tr,), in_specs=in_specs, out_specs=out_specs, out_shape=out_shape,
        compiler_params=_cparams(("arbitrary",)), name=name)(*[a.arr for a in list(args) + list(cts)])


def _conv_taps(x, w):
    rows = lax.broadcasted_iota(jnp.int32, x.shape, 0)
    y = x * w[CONV_K - 1:CONV_K, :]
    for s in range(1, CONV_K):
        y = y + jnp.where(rows >= s, pltpu.roll(x, s, 0), 0.0) * w[CONV_K - 1 - s:CONV_K - s, :]
    return y


CONV_HEADS = 4
CONV_BLOCKS_PER_THIRD = N_HEADS // CONV_HEADS


def _conv_post(y, block):
    a = _silu(y)
    normed = block < 2 * CONV_BLOCKS_PER_THIRD
    scale = jnp.where(block < CONV_BLOCKS_PER_THIRD, HEAD ** -0.5, 1.0)
    return a * jnp.where(normed, lax.rsqrt(jnp.sum(a * a, -1, keepdims=True) + EPS) * scale, 1.0)


def conv_fwd(z, w, lp):
    t, width = z.shape
    cols = CONV_HEADS * HEAD

    def body(z_ref, w_ref, o_ref, y_ref):
        block = pl.program_id(1)
        for h in range(CONV_HEADS):
            cs = slice(h * HEAD, (h + 1) * HEAD)
            y = _conv_taps(z_ref[:, cs], w_ref[:, cs])
            y_ref[:, cs] = y
            o_ref[:, cs] = _conv_post(y, block)

    blk = pl.BlockSpec((lp, cols), lambda b, j: (b, j))
    out = jax.ShapeDtypeStruct((t, width), F32)
    return pl.pallas_call(
        body, grid=(t // lp, width // cols), in_specs=[blk, pl.BlockSpec((CONV_K, cols), lambda b, j: (0, j))],
        out_specs=[blk, blk], out_shape=[out, out], compiler_params=_cparams(("arbitrary", "arbitrary")), name="a_conv_fwd")(z, w)


def conv_bwd(z, y, w, dout, lp):
    t, width = z.shape
    cols = CONV_HEADS * HEAD

    def body(z_ref, y_ref, w_ref, g_ref, dz_ref, dw_ref):
        block = pl.program_id(0)

        @pl.when(pl.program_id(1) == 0)
        def _():
            dw_ref[...] = jnp.zeros_like(dw_ref)

        for h in range(CONV_HEADS):
            cs = slice(h * HEAD, (h + 1) * HEAD)
            x, wv = z_ref[:, cs], w_ref[:, cs]
            _, vjp = jax.vjp(lambda y_: _conv_post(y_, block), y_ref[:, cs])
            (dy,) = vjp(g_ref[:, cs])
            rows = lax.broadcasted_iota(jnp.int32, x.shape, 0)
            dx = dy * wv[CONV_K - 1:CONV_K, :]
            dw_ref[CONV_K - 1:CONV_K, cs] += jnp.sum(dy * x, axis=0, keepdims=True)
            for s in range(1, CONV_K):
                dy_up = jnp.where(rows < lp - s, pltpu.roll(dy, lp - s, 0), 0.0)
                dx = dx + dy_up * wv[CONV_K - 1 - s:CONV_K - s, :]
                dw_ref[CONV_K - 1 - s:CONV_K - s, cs] += jnp.sum(dy_up * x, axis=0, keepdims=True)
            dz_ref[:, cs] = dx.astype(dz_ref.dtype)

    blk = pl.BlockSpec((lp, cols), lambda j, b: (b, j))
    w_blk = pl.BlockSpec((CONV_K, cols), lambda j, b: (0, j))
    return pl.pallas_call(
        body, grid=(width // cols, t // lp), in_specs=[blk, blk, w_blk, blk], out_specs=[blk, w_blk],
        out_shape=[jax.ShapeDtypeStruct((t, width), _MXU_DTYPE), jax.ShapeDtypeStruct((CONV_K, width), F32)],
        compiler_params=_cparams(("arbitrary", "arbitrary")), name="a_conv_bwd")(z, y, w, dout)


def _delta_chunk(q, k, v, ba, alog, dtb, state, t_stored):
    n_g, c = q.shape[0], q.shape[1]
    lane = lax.broadcasted_iota(jnp.int32, (1, HEAD), 1)

    def pick(xs, offset):
        cols = [jnp.sum(xs[i // N_HEADS if len(xs) > 1 else 0] * (lane == offset + i % N_HEADS).astype(F32), axis=1, keepdims=True)[None]
                for i in range(n_g)]
        return jnp.concatenate(cols, 0)

    b_raw, a_raw = pick(ba, 0), pick(ba, N_HEADS)
    a_log, dt_bias = pick((alog,), 0), pick((dtb,), 0)
    beta = _sigmoid(b_raw)
    g = -jnp.exp(a_log) * _softplus(a_raw + dt_bias)
    ri = lax.broadcasted_iota(jnp.int32, (c, c), 0)
    ci = lax.broadcasted_iota(jnp.int32, (c, c), 1)
    tril = ci <= ri
    lower = jnp.broadcast_to(tril.astype(F32), (n_g, c, c))
    gc_col = _dot_01(lower, g * jnp.ones((1, 1, HEAD), F32))[:, :, :1]
    gc_row = _dot_01(jnp.ones((n_g, 8, c), F32), g * (ri <= ci).astype(F32)[None])[:, 0:1, :]
    gc_last = jnp.sum(g, axis=1, keepdims=True)
    decay = jnp.exp(jnp.where(tril, gc_col - gc_row, NEG))
    e_gc = jnp.exp(gc_col)
    kb = k * beta
    a_mat = jnp.where(ci < ri, mm_nt(kb, k) * decay, 0.0)
    t_inv = _inv_unit_lower(a_mat) if t_stored is None else _inv_lookup(a_mat, t_stored)
    u_base = mm_nn(t_inv, v * beta)
    w_dec = mm_nn(t_inv, kb * e_gc)
    attn = jnp.where(tril, mm_nt(q, k) * decay, 0.0)
    u = u_base - mm_nn(w_dec, state)
    o = mm_nn(q * e_gc, state) + mm_nn(attn, u)
    new_state = state * jnp.exp(gc_last) + mm_tn(k * jnp.exp(gc_last - gc_col), u)
    return o, new_state, t_inv


DELTA_STEP_FWD = (4, 2)
DELTA_STEP_BWD = (2, 2)


def _heads_of(ref, rs, first_col):
    return jnp.stack([ref[i // N_HEADS, rs, first_col + (i % N_HEADS) * HEAD:first_col + (i % N_HEADS + 1) * HEAD]
                      for i in range(ref.shape[0] * N_HEADS)])


def _qkv_heads(ref, rs, part):
    return _heads_of(ref, rs, part * N_HEADS * HEAD)


def _by_sequence(a, lp):
    return a.reshape(a.shape[0] // lp, lp, a.shape[1])


def _ride(bufs, scatter, refs_in, refs_out, sems, first, last):
    if not bufs:
        return lambda: None

    @pl.when(first)
    def _():
        Exchange(refs_in, refs_out, *sems, scatter).start()

    def finish():
        @pl.when(last)
        def _():
            Exchange(refs_in, refs_out, *sems, scatter).wait()

    return finish


def delta_fwd(qkv, ba, ba_block, alog, dtb, lp, gather=()):
    t = qkv.shape[0]
    nb, nc = t // lp, lp // CHUNK
    seqs, cps = DELTA_STEP_FWD
    ng, rows = nc // cps, cps * CHUNK
    nx = len(gather)
    nbg = nb // seqs
    assert nc % cps == 0 and nb % seqs == 0

    def body(*refs):
        qkv_ref, ba_ref, al_ref, dt_ref = refs[:4]
        o_ref, s_ref, t_ref = refs[4 + nx:7 + nx]
        state_ref = refs[7 + 2 * nx]
        b, n = pl.program_id(0), pl.program_id(1)
        finish = _ride(gather, False, refs[4:4 + nx], refs[7 + nx:7 + 2 * nx], refs[8 + 2 * nx:], (b == 0) & (n == 0), (b == nbg - 1) & (n == ng - 1))

        @pl.when(n == 0)
        def _():
            state_ref[...] = jnp.zeros_like(state_ref)

        al, dtv = al_ref[...], dt_ref[...]
        for c in range(cps):
            rs = slice(c * CHUNK, (c + 1) * CHUNK)
            state = state_ref[...]
            o, new_state, t_inv = _delta_chunk(_qkv_heads(qkv_ref, rs, 0), _qkv_heads(qkv_ref, rs, 1), _qkv_heads(qkv_ref, rs, 2),
                                               tuple(ba_ref[i, rs, :] for i in range(seqs)), al, dtv, state, None)
            for i in range((seqs * N_HEADS)):
                seq, g = divmod(i, N_HEADS)
                o_ref[seq, rs, g * HEAD:(g + 1) * HEAD] = o[i]
                s_ref[seq, g, c] = state[i]
                t_ref[seq, g, c] = t_inv[i]
            state_ref[...] = new_state
        finish()

    rows_of = lambda width: pl.BlockSpec((seqs, rows, width), lambda b, n: (b, n, 0))
    par_spec = pl.BlockSpec((1, HEAD), lambda b, n: (0, 0))
    out = pl.pallas_call(
        body, grid=(nbg, ng),
        in_specs=[rows_of(3 * N_HEADS * HEAD), pl.BlockSpec((seqs, rows, HEAD), lambda b, n: (b, n, ba_block)), par_spec, par_spec] + [_HBM] * nx,
        out_specs=[rows_of(N_HEADS * HEAD), pl.BlockSpec((seqs, N_HEADS, cps, HEAD, HEAD), lambda b, n: (b, 0, n, 0, 0)),
                   pl.BlockSpec((seqs, N_HEADS, cps, CHUNK, CHUNK), lambda b, n: (b, 0, n, 0, 0))] + [_HBM] * nx,
        out_shape=[jax.ShapeDtypeStruct((nb, lp, N_HEADS * HEAD), F32), jax.ShapeDtypeStruct((nb, N_HEADS, nc, HEAD, HEAD), F32),
                   jax.ShapeDtypeStruct((nb, N_HEADS, nc, CHUNK, CHUNK), F32)] + Exchange.out_shape(gather, False),
        scratch_shapes=[pltpu.VMEM(((seqs * N_HEADS), HEAD, HEAD), F32)] + (Exchange.scratch(nx) if nx else []),
        compiler_params=_cparams(("arbitrary", "arbitrary")), name="delta_fwd")(_by_sequence(qkv, lp), _by_sequence(ba, lp), alog, dtb, *gather)
    return [out[0].reshape(t, N_HEADS * HEAD)] + list(out[1:])


def delta_bwd(qkv, ba, ba_block, alog, dtb, states, t_invs, do, lp, scatter=()):
    t = qkv.shape[0]
    nb, nc = t // lp, lp // CHUNK
    seqs, cps = DELTA_STEP_BWD
    ng, rows = nc // cps, cps * CHUNK
    nx = len(scatter)
    nbg = nb // seqs

    def body(*refs):
        qkv_ref, ba_ref, al_ref, dt_ref, s_ref, t_ref, do_ref = refs[:7]
        dqkv_ref, dba_ref, dal_ref, ddt_ref = refs[7 + nx:11 + nx]
        dstate_ref = refs[11 + 2 * nx]
        b, step = pl.program_id(0), pl.program_id(1)
        finish = _ride(scatter, True, refs[7:7 + nx], refs[11 + nx:11 + 2 * nx], refs[12 + 2 * nx:], (b == 0) & (step == 0),
                       (b == nbg - 1) & (step == ng - 1))

        @pl.when(step == 0)
        def _():
            dstate_ref[...] = jnp.zeros_like(dstate_ref)

        @pl.when((b == 0) & (step == 0))
        def _():
            dal_ref[...] = jnp.zeros_like(dal_ref)
            ddt_ref[...] = jnp.zeros_like(ddt_ref)

        al, dtv = al_ref[...], dt_ref[...]
        d_al = jnp.zeros((1, HEAD), F32)
        d_dt = jnp.zeros((1, HEAD), F32)
        for c in reversed(range(cps)):
            rs = slice(c * CHUNK, (c + 1) * CHUNK)
            t_n = jnp.stack([t_ref[i // N_HEADS, i % N_HEADS, c] for i in range((seqs * N_HEADS))])
            s_n = jnp.stack([s_ref[i // N_HEADS, i % N_HEADS, c] for i in range((seqs * N_HEADS))])

            def f(q_, k_, v_, ba_, al_, dt_, s_, t_n=t_n):
                return _delta_chunk(q_, k_, v_, ba_, al_, dt_, s_, t_n)[:2]

            _, vjp = jax.vjp(f, _qkv_heads(qkv_ref, rs, 0), _qkv_heads(qkv_ref, rs, 1), _qkv_heads(qkv_ref, rs, 2), tuple(ba_ref[i, rs, :] for i in range(seqs)), al, dtv, s_n)
            grads = vjp((_heads_of(do_ref, rs, 0), dstate_ref[...]))
            for part in range(3):
                for i in range((seqs * N_HEADS)):
                    col = (part * N_HEADS + i % N_HEADS) * HEAD
                    dqkv_ref[i // N_HEADS, rs, col:col + HEAD] = grads[part][i]
            for i in range(seqs):
                dba_ref[i, rs, :] = grads[3][i]
            d_al, d_dt = d_al + grads[4], d_dt + grads[5]
            dstate_ref[...] = grads[6]
        dal_ref[...] += d_al
        ddt_ref[...] += d_dt
        finish()

    rows_of = lambda width: pl.BlockSpec((seqs, rows, width), lambda b, n: (b, ng - 1 - n, 0))
    par_spec = pl.BlockSpec((1, HEAD), lambda b, n: (0, 0))
    out = pl.pallas_call(
        body, grid=(nbg, ng),
        in_specs=[rows_of(3 * N_HEADS * HEAD), pl.BlockSpec((seqs, rows, HEAD), lambda b, n: (b, ng - 1 - n, ba_block)), par_spec, par_spec,
                  pl.BlockSpec((seqs, N_HEADS, cps, HEAD, HEAD), lambda b, n: (b, 0, ng - 1 - n, 0, 0)),
                  pl.BlockSpec((seqs, N_HEADS, cps, CHUNK, CHUNK), lambda b, n: (b, 0, ng - 1 - n, 0, 0)), rows_of(N_HEADS * HEAD)] + [_HBM] * nx,
        out_specs=[rows_of(3 * N_HEADS * HEAD), rows_of(HEAD), par_spec, par_spec] + [_HBM] * nx,
        out_shape=[jax.ShapeDtypeStruct((nb, lp, 3 * N_HEADS * HEAD), F32), jax.ShapeDtypeStruct((nb, lp, HEAD), F32),
                   jax.ShapeDtypeStruct((1, HEAD), F32), jax.ShapeDtypeStruct((1, HEAD), F32)] + Exchange.out_shape(scatter, True),
        scratch_shapes=[pltpu.VMEM(((seqs * N_HEADS), HEAD, HEAD), F32)] + (Exchange.scratch(nx) if nx else []),
        compiler_params=_cparams(("arbitrary", "arbitrary")), name="delta_bwd")(
            _by_sequence(qkv, lp), _by_sequence(ba, lp), alog, dtb, states, t_invs, _by_sequence(do, lp), *scatter)
    return [out[0].reshape(t, 3 * N_HEADS * HEAD), out[1].reshape(t, HEAD)] + list(out[2:])


ATT_Q_TILE = 256
ATT_K_TILE = 512
ATT_SCALE = QK_DIM ** -0.5


def _tiles(end, size):
    return [(s, min(s + size, end)) for s in range(0, end, size)]


def _att_visible(q0, q1, k0, k1, keys_first):
    if k1 <= q0 + CHUNK and k0 >= PAD_ROWS:
        return None
    shape = (k1 - k0, q1 - q0) if keys_first else (q1 - q0, k1 - k0)
    qpos = q0 + lax.broadcasted_iota(jnp.int32, shape, 1 if keys_first else 0)
    kpos = k0 + lax.broadcasted_iota(jnp.int32, shape, 0 if keys_first else 1)
    shift = CHUNK.bit_length() - 1
    return (jnp.right_shift(kpos, shift) <= jnp.right_shift(qpos, shift)) & (kpos >= PAD_ROWS)


def _att_seq_specs(lp):
    return pl.BlockSpec((lp, QK_PAD), lambda b, h: (b, h)), pl.BlockSpec((lp, HEAD), lambda b, h: (b, h))


def flash_fwd(q, k, v, lp):
    t = q.shape[0]
    qk_seq, o_seq = _att_seq_specs(lp)

    def body(q_ref, k_ref, v_ref, o_ref, lse_ref):
        q_tiles = _tiles(lp, ATT_Q_TILE)

        def score_steps(q0, q1, out):
            def step(k0, k1):
                s = mm_nt(q_ref[q0:q1, :], k_ref[k0:k1, :])
                vis = _att_visible(q0, q1, k0, k1, False)
                s = s if vis is None else jnp.where(vis, s, NEG)
                out["scores"].append(s)
                row_max = jnp.max(s, -1, keepdims=True)
                out["m"] = row_max if out["m"] is None else jnp.maximum(out["m"], row_max)
            return [functools.partial(step, k0, k1) for k0, k1 in _tiles(q1, ATT_K_TILE)]

        cur = {"scores": [], "m": None}
        for step in score_steps(*q_tiles[0], cur):
            step()
        for i, (q0, q1) in enumerate(q_tiles):
            nxt = {"scores": [], "m": None}
            ahead = score_steps(*q_tiles[i + 1], nxt) if i + 1 < len(q_tiles) else []
            l = jnp.zeros((q1 - q0, 1), F32)
            acc = jnp.zeros((q1 - q0, HEAD), F32)
            for s, (k0, k1) in zip(cur["scores"], _tiles(q1, ATT_K_TILE), strict=True):
                if ahead:
                    ahead.pop(0)()
                p = jnp.exp(s - cur["m"])
                l = l + jnp.sum(p, -1, keepdims=True)
                acc = acc + mm_nn(p, v_ref[k0:k1, :])
            for step in ahead:
                step()
            o_ref[q0:q1, :] = acc / l
            lse_ref[q0:q1, :] = jnp.broadcast_to(cur["m"] + jnp.log(l), (q1 - q0, HEAD))
            cur = nxt

    big = jax.ShapeDtypeStruct((t, N_HEADS * HEAD), F32)
    return pl.pallas_call(
        body, grid=(t // lp, N_HEADS), in_specs=[qk_seq, qk_seq, o_seq], out_specs=[o_seq, o_seq], out_shape=[big, big],
        compiler_params=_cparams(("arbitrary", "arbitrary")), name="flash_fwd")(q, k, v)


def flash_bwd(q, k, v, o, lse, do, lp):
    t = q.shape[0]
    qk_seq, o_seq = _att_seq_specs(lp)

    def body(q_ref, k_ref, v_ref, o_ref, lse_ref, do_ref, dq_ref, dk_out_ref, dv_out_ref, dk_ref, dv_ref):
        dk_ref[...] = jnp.zeros_like(dk_ref)
        dv_ref[...] = jnp.zeros_like(dv_ref)
        for q0, q1 in _tiles(lp, ATT_Q_TILE):
            qb, dob = q_ref[q0:q1, :], do_ref[q0:q1, :]
            lse_row = jnp.transpose(lse_ref[q0:q1, :])[0:1, :]
            dsum_row = jnp.sum(jnp.transpose(dob * o_ref[q0:q1, :]), axis=0, keepdims=True)
            dq = jnp.zeros((q1 - q0, QK_PAD), F32)
            for k0, k1 in _tiles(q1, ATT_K_TILE):
                kb, vb = k_ref[k0:k1, :], v_ref[k0:k1, :]
                s = mm_nt(kb, qb)
                vis = _att_visible(q0, q1, k0, k1, True)
                s = s if vis is None else jnp.where(vis, s, NEG)
                p = jnp.exp(s - lse_row)
                ds = p * (mm_nt(vb, dob) - dsum_row)
                dv_ref[k0:k1, :] += mm_nn(p, dob)
                dk_ref[k0:k1, :] += mm_nn(ds, qb)
                dq = dq + mm_tn(ds, kb)
            dq_ref[q0:q1, :] = dq.astype(dq_ref.dtype)
        dk_out_ref[...] = dk_ref[...].astype(dk_out_ref.dtype)
        dv_out_ref[...] = dv_ref[...].astype(dv_out_ref.dtype)

    narrow = _MXU_DTYPE
    return pl.pallas_call(
        body, grid=(t // lp, N_HEADS), in_specs=[qk_seq, qk_seq, o_seq, o_seq, o_seq, o_seq], out_specs=[qk_seq, qk_seq, o_seq],
        out_shape=[jax.ShapeDtypeStruct((t, N_HEADS * QK_PAD), narrow), jax.ShapeDtypeStruct((t, N_HEADS * QK_PAD), narrow),
                   jax.ShapeDtypeStruct((t, N_HEADS * HEAD), narrow)],
        scratch_shapes=[pltpu.VMEM((lp, QK_PAD), F32), pltpu.VMEM((lp, HEAD), F32)],
        compiler_params=_cparams(("arbitrary", "arbitrary")), name="flash_bwd")(q, k, v, o, lse, do)


def loss_head(h2, target, lp):
    nb, seq, d = target.shape
    cols = _pick(d, (512, 128))
    ncol = d // cols

    def body(h_ref, t_ref, loss_ref, dh_ref, acc_ref):
        b, j = pl.program_id(0), pl.program_id(1)

        @pl.when((b == 0) & (j == 0))
        def _():
            acc_ref[...] = jnp.zeros_like(acc_ref)

        err = h_ref[LEAD:, :] - t_ref[...]
        dh_ref[:LEAD, :] = jnp.zeros((LEAD, cols), F32)
        dh_ref[LEAD:, :] = err * (1.0 / d)
        acc_ref[...] += jnp.sum(err * err, axis=0, keepdims=True)

        @pl.when((b == nb - 1) & (j == ncol - 1))
        def _():
            loss_ref[...] = jnp.sum(acc_ref[...], axis=1, keepdims=True) * (0.5 / d)

    return pl.pallas_call(
        body, grid=(nb, ncol),
        in_specs=[pl.BlockSpec((None, lp, cols), lambda b, j: (b, 0, j)), pl.BlockSpec((None, seq, cols), lambda b, j: (b, 0, j))],
        out_specs=[pl.BlockSpec((1, 1), lambda b, j: (0, 0)), pl.BlockSpec((None, lp, cols), lambda b, j: (b, 0, j))],
        out_shape=[jax.ShapeDtypeStruct((1, 1), F32), jax.ShapeDtypeStruct((nb, lp, d), F32)],
        scratch_shapes=[pltpu.VMEM((1, cols), F32)], compiler_params=_cparams(("arbitrary", "arbitrary")), name="loss_head")(h2, target)


def meta_grad(dh0):
    nb, _, d = dh0.shape

    def body(g_ref, o_ref):
        @pl.when(pl.program_id(0) == 0)
        def _():
            o_ref[...] = jnp.zeros_like(o_ref)

        o_ref[...] += g_ref[PAD_ROWS:LEAD, :]

    return pl.pallas_call(
        body, grid=(nb,), in_specs=[pl.BlockSpec((None, LEAD, d), lambda b: (b, 0, 0))],
        out_specs=pl.BlockSpec((N_META, d), lambda b: (0, 0)), out_shape=jax.ShapeDtypeStruct((N_META, d), F32),
        compiler_params=_cparams(("arbitrary",)), name="meta_grad")(dh0)


_HBM = pl.BlockSpec(memory_space=pltpu.HBM)


def _mesh_pos():
    x, y, c = lax.axis_index("x"), lax.axis_index("y"), lax.axis_index("c")
    return x, y, c


def _peer(x, y, c, k):
    px = 1 - x if k & 4 else x
    py = 1 - y if k & 2 else y
    pc = 1 - c if k & 1 else c
    return (px, py, pc), 4 * px + 2 * py + pc


class Exchange:
    def __init__(self, x_refs, out_refs, send_sems, recv_sems, local_sems, scatter):
        self.x_refs, self.out_refs, self.scatter = x_refs, out_refs, scatter
        self.send_sems, self.recv_sems, self.local_sems = send_sems, recv_sems, local_sems
        self.pos = _mesh_pos()
        x, y, c = self.pos
        self.me = 4 * x + 2 * y + c

    @staticmethod
    def scratch(n):
        return [pltpu.SemaphoreType.DMA((n, N_DEV - 1)), pltpu.SemaphoreType.DMA((n, N_DEV - 1)), pltpu.SemaphoreType.DMA((n,))]

    @staticmethod
    def out_shape(bufs, scatter):
        return [jax.ShapeDtypeStruct(b.shape if scatter else (N_DEV,) + b.shape, b.dtype) for b in bufs]

    def _local(self, i):
        return pltpu.make_async_copy(self.x_refs[i].at[self.me] if self.scatter else self.x_refs[i], self.out_refs[i].at[self.me], self.local_sems.at[i])

    def _copy(self, i, k, landing):
        peer, peer_id = _peer(*self.pos, k)
        src = self.x_refs[i].at[peer_id] if self.scatter else self.x_refs[i]
        return pltpu.make_async_remote_copy(src_ref=src, dst_ref=self.out_refs[i].at[peer_id if landing else self.me],
                                            send_sem=self.send_sems.at[i, k - 1], recv_sem=self.recv_sems.at[i, k - 1],
                                            device_id=peer, device_id_type=pl.DeviceIdType.MESH)

    def start(self):
        for i in range(len(self.x_refs)):
            self._local(i).start()
        for k in range(1, N_DEV):
            for i in range(len(self.x_refs)):
                self._copy(i, k, False).start()

    def wait(self):
        for k in range(1, N_DEV):
            for i in range(len(self.x_refs)):
                self._copy(i, k, True).wait_recv()
        for k in range(1, N_DEV):
            for i in range(len(self.x_refs)):
                self._copy(i, k, False).wait_send()
        for i in range(len(self.x_refs)):
            self._local(i).wait()


def _exchange(name, bufs, scatter):
    n = len(bufs)

    def body(*refs):
        ex = Exchange(refs[:n], refs[n:2 * n], *refs[2 * n:], scatter)
        ex.start()
        ex.wait()

    return pl.pallas_call(body, in_specs=[_HBM] * n, out_specs=[_HBM] * n, out_shape=Exchange.out_shape(bufs, scatter),
                          scratch_shapes=Exchange.scratch(n), name=name)(*bufs)


def _f_rms(x, g):
    return (_rms(x, g),)


def _f_rms2(x, g1, g2):
    r = x * lax.rsqrt(jnp.sum(x * x, -1, keepdims=True) / x.shape[-1] + EPS)
    return r * g1, r * g2


def _f_out_gate(o, gate, gain):
    return (_rms(o, gain) * _silu(gate),)


def _f_gate(o, gate):
    return (o * _silu(gate),)


@jax.custom_vjp
def _swap_rope_halves(x):
    half = ROPE // 2
    lane = lax.broadcasted_iota(jnp.int32, x.shape, 1)
    return jnp.where(lane < half, pltpu.roll(x, HEAD - half, 1), jnp.where(lane < ROPE, pltpu.roll(x, half, 1), 0.0))


_swap_rope_halves.defvjp(lambda x: (_swap_rope_halves(x), None), lambda _, g: (_swap_rope_halves(g),))


def _f_qk_final(scale, nope, rope_in, g_nope, g_rope, cos, sin):
    ms = (jnp.sum(nope * nope, -1, keepdims=True) + jnp.sum(rope_in * rope_in, -1, keepdims=True)) / QK_DIM
    r = lax.rsqrt(ms + EPS)
    a = nope * r * g_nope
    b = rope_in * r * g_rope
    out = jnp.concatenate([a, b * cos + _swap_rope_halves(b) * sin], axis=1)
    return (out if scale == 1.0 else out * scale,)


def _rope_tables(lp):
    half = ROPE // 2
    pos = jnp.maximum(jnp.arange(lp) - PAD_ROWS, 0)
    inv = ROPE_THETA ** (-jnp.arange(half, dtype=F32) / half)
    ang = pos.astype(F32)[:, None] * inv[None, :]
    zeros = jnp.zeros((lp, HEAD - ROPE), F32)
    cos = jnp.concatenate([jnp.cos(ang), jnp.cos(ang), zeros], 1)
    sin = jnp.concatenate([-jnp.sin(ang), jnp.sin(ang), zeros], 1)
    return cos, sin


def _pad_lanes(w, width=HEAD):
    return jnp.pad(w, ((0, 0), (0, width - w.shape[1])))


def _pad_rows(w, rows=HEAD):
    return jnp.pad(w, ((0, rows - w.shape[0]), (0, 0)))


def _split_heads_qk_t(w_t):
    k = w_t.shape[1]
    w3 = w_t.reshape(N_HEADS, QK_DIM, k)
    nope = w3[:, :HEAD].reshape(N_HEADS * HEAD, k)
    rope = jnp.pad(w3[:, HEAD:], ((0, 0), (0, HEAD - ROPE), (0, 0))).reshape(N_HEADS * HEAD, k)
    return jnp.concatenate([nope, rope], 0)


def _merge_heads_qk_t(g_t):
    k = g_t.shape[1]
    kw = N_HEADS * HEAD
    nope, rope = g_t[:kw].reshape(N_HEADS, HEAD, k), g_t[kw:].reshape(N_HEADS, HEAD, k)[:, :ROPE]
    return jnp.concatenate([nope, rope], 1).reshape(N_HEADS * QK_DIM, k)


def local_step(x, target, w, deferred=None):
    nb, seq, d = x.shape
    lp = seq + LEAD
    t = nb * lp
    tr = _pick(lp, (544, 128))
    ntab = lp // tr
    mxu = _MXU_DTYPE
    kw = N_HEADS * HEAD

    a_w_in_t = w["a_w_in"].astype(mxu)
    w_qkv_t, w_gba_t = a_w_in_t[:3 * kw], _pad_rows(a_w_in_t[3 * kw:], kw + HEAD)
    a_conv = w["a_conv"].T
    alog, dtb, o_gain = _pad_lanes(w["a_log"]), _pad_lanes(w["a_dt_bias"]), w["a_o_gain"]
    a_norm, kv_norm, b_norm = w["a_norm"], w["kv_norm"][None, :], w["b_norm"]
    lat_norm, qlat_norm = w["kv_latent_norm"][None, :], w["b_q_latent_norm"]
    kg_nope, kg_rope = w["k_gain"][None, :HEAD], _pad_lanes(w["k_gain"][None, HEAD:])
    qg_nope, qg_rope = w["b_q_gain"][:, :HEAD], _pad_lanes(w["b_q_gain"][:, HEAD:])
    cos, sin = _rope_tables(lp)

    meta = jnp.broadcast_to(w["meta_tokens"].T[None], (nb, N_META, d))
    h0 = jnp.concatenate([jnp.zeros((nb, PAD_ROWS, d), F32), meta, x], 1).reshape(t, d)
    (hn,) = row_call("a_norm_fwd", _f_rms, [Arg(h0), Arg(a_norm, "par")], [(d, mxu, d, False)], tr)
    z_qkv = matmul("a_in_qkv", hn, w_qkv_t, "nt")
    z_gba = matmul("a_in_gate_ba", hn, w_gba_t, "nt")
    ba_block = kw // HEAD
    qkv_a, y_conv = conv_fwd(z_qkv, a_conv, lp)
    o_a, states, t_invs, *gathered = delta_fwd(qkv_a, z_gba, ba_block, alog, dtb, lp, gather=deferred.gather_bufs if deferred else ())
    if deferred:
        w = {**w, **deferred.finish(gathered)}
    a_w_out = w["a_w_out"].astype(mxu)
    w_down = _pad_lanes(w["kv_w_down"], KV_RANK + HEAD).astype(mxu)
    w_ukv_t = jnp.concatenate([w["kv_w_uk"], w["kv_w_uv"]], 0).astype(mxu)
    b_w_in_t = w["b_w_in"].astype(mxu)
    w_cq_t, w_gb_t = b_w_in_t[:Q_RANK], b_w_in_t[Q_RANK:]
    w_q_t = _split_heads_qk_t(w["b_w_uq"]).astype(mxu)
    b_w_out = w["b_w_out"].astype(mxu)
    og_args = [Arg(o_a, bc=HEAD, ph=True, diff=True), Arg(z_gba, bc=HEAD, ph=True, diff=True, gdt=mxu), Arg(o_gain, "par", diff=True)]
    (og_a,) = row_call("a_out_gate_fwd", _f_out_gate, og_args, [(kw, mxu, HEAD, True)], tr, nh=N_HEADS)
    h1 = matmul("a_out", og_a, a_w_out, "nn", res=h0)

    hk, hb = row_call("b_norms_fwd", _f_rms2, [Arg(h1), Arg(kv_norm, "par"), Arg(b_norm, "par")], [(d, mxu, d, False), (d, mxu, d, False)], tr)
    c_down = matmul("kv_down", hk, w_down, "nn")
    c_kv_arg = Arg(c_down, bc=KV_RANK, diff=True, gdt=mxu)
    k_pe_arg = Arg(c_down, bc=HEAD, base=KV_RANK // HEAD, diff=True)
    c_q_raw = matmul("b_in_q", hb, w_cq_t, "nt")
    gate_b = matmul("b_in_gate", hb, w_gb_t, "nt")
    (c_kv,) = row_call("kv_latent_fwd", _f_rms, [c_kv_arg, Arg(lat_norm, "par")], [(KV_RANK, mxu, KV_RANK, False)], tr)
    (c_q,) = row_call("q_latent_fwd", _f_rms, [Arg(c_q_raw), Arg(qlat_norm, "par")], [(Q_RANK, mxu, Q_RANK, False)], tr)
    k_nope = matmul("k_up", c_kv, w_ukv_t[:kw], "nt")
    v_b = matmul("v_up", c_kv, w_ukv_t[kw:], "nt", out_dtype=mxu)
    q_up = matmul("q_up", c_q, w_q_t, "nt")
    tabs = [Arg(cos, "tab"), Arg(sin, "tab")]
    k_args = [Arg(k_nope, bc=HEAD, ph=True, diff=True, gdt=mxu), k_pe_arg, Arg(kg_nope, "par", diff=True), Arg(kg_rope, "par", diff=True)] + tabs
    q_args = [Arg(q_up, bc=HEAD, ph=True, diff=True, gdt=mxu), Arg(q_up, bc=HEAD, base=N_HEADS, ph=True, diff=True, gdt=mxu),
              Arg(qg_nope, "par", diff=True), Arg(qg_rope, "par", diff=True)] + tabs
    f_k_final, f_q_final = functools.partial(_f_qk_final, 1.0), functools.partial(_f_qk_final, ATT_SCALE)
    (k_fin,) = row_call("k_final_fwd", f_k_final, k_args, [(N_HEADS * QK_PAD, mxu, QK_PAD, True)], tr, nh=N_HEADS, ntab=ntab)
    (q_fin,) = row_call("q_final_fwd", f_q_final, q_args, [(N_HEADS * QK_PAD, mxu, QK_PAD, True)], tr, nh=N_HEADS, ntab=ntab)
    o_b, lse = flash_fwd(q_fin, k_fin, v_b, lp)
    gb_args = [Arg(o_b, diff=True), Arg(gate_b, diff=True, gdt=mxu)]
    (og_b,) = row_call("b_gate_fwd", _f_gate, gb_args, [(kw, mxu, kw, False)], tr)
    h2 = matmul("b_out", og_b, b_w_out, "nn", res=h1)

    loss, dh2 = loss_head(h2.reshape(nb, lp, d), target, lp)
    dh2 = dh2.reshape(t, d)
    grads = {}

    d_og_b = matmul("b_out_dx", dh2, b_w_out, "nt", out_dtype=mxu)
    grads["b_w_out"] = matmul("b_out_dw", og_b, dh2, "tn")
    d_o_b, d_gate_b = row_vjp_call("b_gate_bwd", _f_gate, gb_args, [Arg(d_og_b)], tr)
    dq_fin, dk_fin, dv_b = flash_bwd(q_fin, k_fin, v_b, o_b, lse, d_o_b, lp)
    dq_nope, dq_rope, d_qg_nope, d_qg_rope = row_vjp_call(
        "q_final_bwd", f_q_final, q_args, [Arg(dq_fin, bc=QK_PAD, ph=True)], tr, nh=N_HEADS, ntab=ntab)
    dk_nope, dk_pe, d_kg_nope, d_kg_rope = row_vjp_call(
        "k_final_bwd", f_k_final, k_args, [Arg(dk_fin, bc=QK_PAD, ph=True)], tr, nh=N_HEADS, ntab=ntab)
    grads["b_q_gain"] = jnp.concatenate([d_qg_nope, d_qg_rope[:, :ROPE]], 1)
    grads["k_gain"] = jnp.concatenate([d_kg_nope, d_kg_rope[:, :ROPE]], 1)[0]
    d_c_q = matmul("q_nope_dx", dq_nope, w_q_t[:kw], "nn")
    d_c_q = matmul("q_rope_dx", dq_rope, w_q_t[kw:], "nn", res=d_c_q)
    grads["b_w_uq"] = _merge_heads_qk_t(jnp.concatenate([matmul("q_nope_dw", dq_nope, c_q, "tn"), matmul("q_rope_dw", dq_rope, c_q, "tn")], 0))
    d_c_kv = matmul("k_up_dx", dk_nope, w_ukv_t[:kw], "nn")
    d_c_kv = matmul("v_up_dx", dv_b, w_ukv_t[kw:], "nn", res=d_c_kv)
    grads["kv_w_uk"], grads["kv_w_uv"] = matmul("k_up_dw", dk_nope, c_kv, "tn"), matmul("v_up_dw", dv_b, c_kv, "tn")
    d_c_q_raw, grads["b_q_latent_norm"] = row_vjp_call(
        "q_latent_bwd", _f_rms, [Arg(c_q_raw, diff=True, gdt=mxu), Arg(qlat_norm, "par", diff=True)], [Arg(d_c_q)], tr)
    d_c_kv_raw, d_lat = row_vjp_call(
        "kv_latent_bwd", _f_rms, [c_kv_arg, Arg(lat_norm, "par", diff=True)], [Arg(d_c_kv)], tr)
    grads["kv_latent_norm"] = d_lat[0]
    d_hb = matmul("b_in_q_dx", d_c_q_raw, w_cq_t, "nn")
    d_hb = matmul("b_in_gate_dx", d_gate_b, w_gb_t, "nn", res=d_hb, out_dtype=mxu)
    grads["b_w_in"] = jnp.concatenate([matmul("b_in_q_dw", d_c_q_raw, hb, "tn"), matmul("b_in_gate_dw", d_gate_b, hb, "tn")], 0)
    d_c_down = jnp.concatenate([d_c_kv_raw, dk_pe.astype(mxu)], 1)
    d_hk = matmul("kv_down_dx", d_c_down, w_down, "nt", out_dtype=mxu)
    grads["kv_w_down"] = matmul("kv_down_dw", hk, d_c_down, "tn")[:, :KV_RANK + ROPE]
    dh1, d_kv_norm, grads["b_norm"] = row_vjp_call(
        "b_norms_bwd", lambda x_, g1, g2: _f_rms2(x_, g1, g2) + (x_,),
        [Arg(h1, diff=True), Arg(kv_norm, "par", diff=True), Arg(b_norm, "par", diff=True)], [Arg(d_hk), Arg(d_hb), Arg(dh2)], tr)
    grads["kv_norm"] = d_kv_norm[0]

    d_og_a = matmul("a_out_dx", dh1, a_w_out, "nt", out_dtype=mxu)
    grads["a_w_out"] = matmul("a_out_dw", og_a, dh1, "tn")
    d_o_a, d_gate_a, grads["a_o_gain"] = row_vjp_call(
        "a_out_gate_bwd", _f_out_gate, og_args, [Arg(d_og_a, bc=HEAD, ph=True)], tr, nh=N_HEADS)
    dqkv_a, d_ba, d_alog, d_dtb, *received = delta_bwd(qkv_a, z_gba, ba_block, alog, dtb, states, t_invs, d_o_a, lp,
                                                        scatter=deferred.scatter_bufs(grads) if deferred else ())
    grads["a_log"], grads["a_dt_bias"] = d_alog[:, :N_HEADS], d_dtb[:, :N_HEADS]
    dz_qkv, d_conv = conv_bwd(z_qkv, y_conv, a_conv, dqkv_a, lp)
    grads["a_conv"] = d_conv.T
    dz_gba = jnp.concatenate([d_gate_a, d_ba.astype(mxu)], 1)
    grads["a_w_in"] = jnp.concatenate([matmul("a_in_qkv_dw", dz_qkv, hn, "tn"), matmul("a_in_gate_ba_dw", dz_gba, hn, "tn")[:kw + 2 * N_HEADS]], 0)
    ride = deferred.last_scatter_bufs(grads) if deferred else ()
    d_hn = matmul("a_in_qkv_dx", dz_qkv, w_qkv_t, "nn", scatter=ride)
    if ride:
        d_hn, *received_last = d_hn
        received = list(received) + received_last
    d_hn = matmul("a_in_gate_ba_dx", dz_gba, w_gba_t, "nn", res=d_hn, out_dtype=mxu)
    dh0, grads["a_norm"] = row_vjp_call("a_norm_bwd", lambda x_, g_: _f_rms(x_, g_) + (x_,),
                                        [Arg(h0, diff=True), Arg(a_norm, "par", diff=True)], [Arg(d_hn), Arg(dh1)], tr)
    dh0 = dh0.reshape(nb, lp, d)
    grads["meta_tokens"] = meta_grad(dh0).T
    return loss, dh0[:, LEAD:], grads, received


_SHARDED = (
    ("meta_tokens", True, False), ("a_norm", True, False), ("a_w_in", True, True), ("a_conv", True, False), ("a_w_out", False, True),
    ("kv_w_down", False, True), ("kv_w_uk", True, True), ("kv_w_uv", True, True), ("b_w_in", True, True), ("b_w_uq", True, True),
    ("b_w_out", False, True))
_REPLICATED = ("a_log", "a_dt_bias", "a_o_gain", "kv_norm", "kv_latent_norm", "k_gain", "b_norm", "b_q_latent_norm", "b_q_gain")
_ALL_WEIGHTS = ("meta_tokens", "a_norm", "a_w_in", "a_conv", "a_log", "a_dt_bias", "a_o_gain", "a_w_out", "kv_norm", "kv_w_down",
                "kv_latent_norm", "kv_w_uk", "kv_w_uv", "k_gain", "b_norm", "b_w_in", "b_q_latent_norm", "b_w_uq", "b_q_gain", "b_w_out")


def _round_up(n, m):
    return (n + m - 1) // m * m


def _pack_rows(pieces, row_multiple):
    padded = []
    for p in pieces:
        n = p.shape[-1]
        padded.append(jnp.pad(p, [(0, 0)] * (p.ndim - 1) + [(0, _round_up(n, PACK_COLS) - n)]))
    flat = jnp.concatenate(padded, -1)
    rows = _round_up(flat.shape[-1] // PACK_COLS, row_multiple)
    flat = jnp.pad(flat, [(0, 0)] * (flat.ndim - 1) + [(0, rows * PACK_COLS - flat.shape[-1])])
    return flat.reshape(flat.shape[:-1] + (rows, PACK_COLS))


def _unpack_rows(buf, sizes):
    flat = buf.reshape(buf.shape[:-2] + (-1,))
    out, off = [], 0
    for n in sizes:
        out.append(flat[..., off:off + n])
        off += _round_up(n, PACK_COLS)
    return out


def _shard_2d(a):
    return a.reshape(a.shape[-2:]) if a.ndim > 2 else a


def _kl_shard(a, by_cols):
    return _shard_2d(a).T if by_cols else _shard_2d(a)


_GROUPS_FIRST = (("a_w_in",),)
_GROUPS_LATER = (("a_w_out", "b_w_in", "b_w_out"), ("b_w_uq",), ("kv_w_down",), ("kv_w_uk", "kv_w_uv"))
_SMALL_SHARDED = ("meta_tokens", "a_norm", "a_conv")
_BY_COLS = {name: by_cols for name, by_cols, _ in _SHARDED}
ROW_ALIGN = 16


def _stack_rows(pieces):
    padded, starts, row = [], [], 0
    for p in pieces:
        r = p.shape[-2]
        padded.append(jnp.pad(p, [(0, 0)] * (p.ndim - 2) + [(0, _round_up(r, ROW_ALIGN) - r), (0, 0)]))
        starts.append(row)
        row += _round_up(r, ROW_ALIGN)
    return jnp.concatenate(padded, -2), starts


def _stack_group(arrays_by_name, names):
    arrays = [arrays_by_name[n].astype(BF16) for n in names]
    buf, starts = _stack_rows(arrays)
    return buf, [(n, s, a.shape[-2]) for n, s, a in zip(names, starts, arrays, strict=True)]


def _stack_groups(arrays_by_name, groups):
    stacked = [_stack_group(arrays_by_name, names) for names in groups]
    return [b for b, _ in stacked], [entries for _, entries in stacked]


def _full_from_gathered(gathered, layout):
    full = {}
    for got, entries in zip(gathered, layout, strict=True):
        for name, start, rows in entries:
            full[name] = got[:, start:start + rows].reshape(N_DEV * rows, got.shape[-1])
    return full


def gather_first_weights(local):
    shards = {n: _kl_shard(local[n], _BY_COLS[n]) for names in _GROUPS_FIRST for n in names}
    bufs, layout = _stack_groups(shards, _GROUPS_FIRST)
    small = [_kl_shard(local[n], _BY_COLS[n]) for n in _SMALL_SHARDED]
    bufs.append(_pack_rows([s.reshape(-1) for s in small], 8))
    gathered = _exchange("all_gather", bufs, scatter=False)
    full = _full_from_gathered(gathered[:-1], layout)
    for name, part, sh in zip(_SMALL_SHARDED, _unpack_rows(gathered[-1], [s.size for s in small]), small, strict=True):
        full[name] = part.reshape(N_DEV * sh.shape[0], sh.shape[1])
    full["a_norm"] = full["a_norm"].reshape(1, -1)
    return full


class LaterExchanges:
    def __init__(self, local):
        shards = {n: _kl_shard(local[n], _BY_COLS[n]) for names in _GROUPS_LATER for n in names}
        self.gather_bufs, self.layout = _stack_groups(shards, _GROUPS_LATER)

    def finish(self, gathered):
        return _full_from_gathered(gathered, self.layout)

    def scatter_bufs(self, grads):
        return _stack_groups(_owner_slices(grads, _GROUPS_LATER), _GROUPS_LATER)[0]

    def last_scatter_bufs(self, grads):
        bufs, self.last_layout = _stack_groups(_owner_slices(grads, _GROUPS_FIRST), _GROUPS_FIRST)
        return bufs


def _owner_slices(grads, groups):
    return {n: grads[n].reshape(N_DEV, -1, grads[n].shape[-1]) for names in groups for n in names}


def reduce_contributions(name, recv):
    _, r, c = recv.shape
    tr = _pick(r, (256, 128, 64, 32, 16, 8))

    def body(g_ref, o_ref):
        g = g_ref[0].astype(F32)
        for dev in range(1, N_DEV):
            g = g + g_ref[dev].astype(F32)
        o_ref[...] = g

    return pl.pallas_call(
        body, grid=(r // tr,), in_specs=[pl.BlockSpec((N_DEV, tr, c), lambda i: (0, i, 0))], out_specs=pl.BlockSpec((tr, c), lambda i: (i, 0)),
        out_shape=jax.ShapeDtypeStruct((r, c), F32), compiler_params=_cparams(("arbitrary",)), name=name)(recv)


def adamw_all(gs, ws, ms, vs):
    n = len(gs)

    def body(*refs):
        for i in range(n):
            g_ref, w_ref, m_ref, v_ref = (refs[j * n + i] for j in range(4))
            d_ref, mo_ref, vo_ref = (refs[(4 + j) * n + i] for j in range(3))
            g = g_ref[...]
            m_new = ADAM_B1 * m_ref[...] + (1.0 - ADAM_B1) * g
            v_new = ADAM_B2 * v_ref[...] + (1.0 - ADAM_B2) * (g * g)
            m_hat = m_new / (1.0 - ADAM_B1 ** ADAM_STEP)
            v_hat = v_new / (1.0 - ADAM_B2 ** ADAM_STEP)
            d_ref[...] = -ADAM_LR * (m_hat / (jnp.sqrt(v_hat) + ADAM_EPS) + ADAM_WD * w_ref[...])
            mo_ref[...] = m_new
            vo_ref[...] = v_new

    out = [jax.ShapeDtypeStruct(g.shape, F32) for g in gs] * 3
    res = pl.pallas_call(body, out_shape=out, compiler_params=pltpu.CompilerParams(vmem_limit_bytes=VMEM_LIMIT), name="adamw_all")(*gs, *ws, *ms, *vs)
    return res[:n], res[n:2 * n], res[2 * n:]


def kernel(x, meta_tokens, a_norm, a_w_in, a_conv, a_log, a_dt_bias, a_o_gain, a_w_out, kv_norm, kv_w_down, kv_latent_norm, kv_w_uk, kv_w_uv, k_gain, b_norm, b_w_in, b_q_latent_norm, b_w_uq, b_q_gain, b_w_out, loss_target, m_meta_tokens, m_a_norm, m_a_w_in, m_a_conv, m_a_log, m_a_dt_bias, m_a_o_gain, m_a_w_out, m_kv_norm, m_kv_w_down, m_kv_latent_norm, m_kv_w_uk, m_kv_w_uv, m_k_gain, m_b_norm, m_b_w_in, m_b_q_latent_norm, m_b_w_uq, m_b_q_gain, m_b_w_out, v_meta_tokens, v_a_norm, v_a_w_in, v_a_conv, v_a_log, v_a_dt_bias, v_a_o_gain, v_a_w_out, v_kv_norm, v_kv_w_down, v_kv_latent_norm, v_kv_w_uk, v_kv_w_uv, v_k_gain, v_b_norm, v_b_w_in, v_b_q_latent_norm, v_b_w_uq, v_b_q_gain, v_b_w_out):
    given = dict(locals())
    local_w = {n: given[n] for n in _ALL_WEIGHTS}
    full = gather_first_weights(local_w)
    for n in _REPLICATED:
        full[n] = local_w[n]
    later = LaterExchanges(local_w)

    loss_part, grad_x, grads, received_riding = local_step(x, loss_target, full, later)

    exact = [grads[n].reshape(N_DEV, -1) for n in _SMALL_SHARDED]
    exact += [jnp.broadcast_to(grads[n].reshape(1, -1), (N_DEV, grads[n].size)) for n in _REPLICATED]
    exact.append(jnp.broadcast_to(loss_part, (N_DEV, 1)))
    received = list(received_riding) + list(_exchange("all_to_all", [_pack_rows(exact, 8)], scatter=True))
    layout = later.layout + later.last_layout
    summed = [reduce_contributions(f"reduce_{i}", r) for i, r in enumerate(received)]

    grad_kl = {}
    for got, entries in zip(summed, layout):
        for n, start, rows in entries:
            grad_kl[n] = got[start:start + rows]
    parts = _unpack_rows(summed[-1], [p.shape[1] for p in exact])
    for n, part in zip(_SMALL_SHARDED + _REPLICATED, parts, strict=False):
        grad_kl[n] = part
    loss = parts[-1][0]

    def natural_2d(n, a):
        shape = _shard_2d(local_w[n]).shape if local_w[n].ndim > 1 else (1, local_w[n].size)
        return a.reshape(shape[::-1]).T if _BY_COLS.get(n, False) else a.reshape(shape)

    as_2d = lambda n, a: a.reshape(natural_2d(n, grad_kl[n]).shape)
    gs = [natural_2d(n, grad_kl[n]) for n in _ALL_WEIGHTS]
    deltas, new_m, new_v = adamw_all(gs, [as_2d(n, local_w[n]) for n in _ALL_WEIGHTS], [as_2d(n, given["m_" + n]) for n in _ALL_WEIGHTS],
                                     [as_2d(n, given["v_" + n]) for n in _ALL_WEIGHTS])
    results = [a.reshape(local_w[n].shape) for group in (gs, deltas, new_m, new_v) for n, a in zip(_ALL_WEIGHTS, group, strict=True)]
    return (loss, grad_x, *results)
```

```python
import dataclasses
import functools
import math

import jax
import jax.numpy as jnp
from jax import lax
from jax.experimental import pallas as pl
from jax.experimental.pallas import tpu as pltpu

F32 = jnp.float32
BF16 = jnp.bfloat16
_MXU_DTYPE = jnp.bfloat16

N_DEV = 8
D_MODEL = 1024
N_HEADS = 8
HEAD = 128
CHUNK = 64
N_META = 16
PAD_ROWS = 2 * CHUNK - N_META
LEAD = PAD_ROWS + N_META
ROPE = 64
QK_DIM = HEAD + ROPE
QK_PAD = 2 * HEAD
KV_RANK = 256
Q_RANK = 384
CONV_K = 4
EPS = 1e-6
NEG = -1e30
ROPE_THETA = 10000.0
ADAM_LR, ADAM_B1, ADAM_B2, ADAM_EPS, ADAM_WD, ADAM_STEP = 0.001, 0.9, 0.999, 1e-08, 0.01, 10
PACK_COLS = 512
VMEM_LIMIT = 56 * 1024 * 1024


def _pick(n, options):
    for o in options:
        if n % o == 0:
            return o
    raise ValueError(f"no tile for {n} among {options}")


def _cparams(sem):
    return pltpu.CompilerParams(dimension_semantics=sem, vmem_limit_bytes=VMEM_LIMIT)


def _dims(a, dims):
    if a.ndim == 2:
        return (dims, ((), ()))
    (ca,), (cb,) = dims
    return (((ca + 1,), (cb + 1,)), ((0,), (0,)))


def _dot(a, b, dims):
    return lax.dot_general(a.astype(_MXU_DTYPE), b.astype(_MXU_DTYPE), _dims(a, dims), preferred_element_type=F32)


@jax.custom_vjp
def mm_nn(a, b):
    return _dot(a, b, ((1,), (0,)))


@jax.custom_vjp
def mm_nt(a, b):
    return _dot(a, b, ((1,), (1,)))


@jax.custom_vjp
def mm_tn(a, b):
    return _dot(a, b, ((0,), (0,)))


mm_nn.defvjp(lambda a, b: (mm_nn(a, b), (a, b)), lambda r, g: (mm_nt(g, r[1]), mm_tn(r[0], g)))
mm_nt.defvjp(lambda a, b: (mm_nt(a, b), (a, b)), lambda r, g: (mm_nn(g, r[1]), mm_tn(g, r[0])))
mm_tn.defvjp(lambda a, b: (mm_tn(a, b), (a, b)), lambda r, g: (mm_nt(r[1], g), mm_nn(r[0], g)))


def _split_terms(x, n):
    terms, rest = [], x
    for _ in range(n):
        t = rest.astype(_MXU_DTYPE)
        terms.append(t)
        rest = rest - t.astype(F32)
    return terms


def _dot_01_raw(m, x, dims):
    m = m.astype(_MXU_DTYPE)
    return sum(lax.dot_general(m, t, _dims(m, dims), preferred_element_type=F32) for t in _split_terms(x, 3))


@jax.custom_vjp
def _dot_01(m, x):
    return _dot_01_raw(m, x, ((1,), (0,)))


_dot_01.defvjp(lambda m, x: (_dot_01(m, x), m), lambda m, g: (jnp.zeros_like(m), _dot_01_raw(m, g, ((0,), (0,)))))


def _inv_unit_lower(a):
    n = a.shape[-1]
    eye = (lax.broadcasted_iota(jnp.int32, (n, n), 0) == lax.broadcasted_iota(jnp.int32, (n, n), 1)).astype(F32)
    d = lambda u, w: lax.dot_general(u, w, _dims(u, ((1,), (0,))), preferred_element_type=F32)
    t = eye - a
    p = a.astype(_MXU_DTYPE)
    p = d(p, p)
    squarings = int(math.log2(n)) - 1
    for s in range(squarings):
        ph = p.astype(_MXU_DTYPE)
        t_hi, t_lo = _split_terms(t, 2)
        t = t + (d(t_hi, ph) + d(t_lo, ph))
        if s + 1 < squarings:
            p = d(ph, ph)
    return t


@jax.custom_vjp
def _inv_lookup(a, t):
    return t


def _inv_lookup_bwd(t, g):
    return -mm_tn(t, mm_nt(g, t)), jnp.zeros_like(t)


_inv_lookup.defvjp(lambda a, t: (t, t), _inv_lookup_bwd)


def _sigmoid(x):
    return 1.0 / (1.0 + jnp.exp(-x))


@jax.custom_vjp
def _silu(x):
    return x * _sigmoid(x)


def _silu_fwd(x):
    s = _sigmoid(x)
    return x * s, (x, s)


_silu.defvjp(_silu_fwd, lambda r, g: (g * (r[1] * (1.0 + r[0] * (1.0 - r[1]))),))


def _softplus(x):
    return jnp.where(x > 20.0, x, jnp.log(1.0 + jnp.exp(jnp.minimum(x, 20.0))))


def _rms(x, g, width=None):
    ms = jnp.sum(x * x, -1, keepdims=True) / (x.shape[-1] if width is None else width)
    return x * lax.rsqrt(ms + EPS) * g


MM_VMEM_BUDGET = 40 * 1024 * 1024


def _matmul_rows(name, a, b, mode, out_dtype, res, scatter):
    m, k = a.shape
    n = b.shape[1] if mode == "nn" else b.shape[0]
    dims = {"nn": ((1,), (0,)), "nt": ((1,), (1,))}[mode]
    out_bytes = jnp.dtype(out_dtype).itemsize
    n_in, nx = 2 + (res is not None), len(scatter)

    def vmem(tm):
        blocks = 2 * tm * k * a.dtype.itemsize + 2 * k * n * b.dtype.itemsize + 2 * tm * n * out_bytes + tm * n * 4
        return blocks + (2 * tm * n * res.dtype.itemsize if res is not None else 0)

    tm = next(c for c in (2176, 1088, 512, 256, 128, 64) if m % c == 0 and vmem(c) <= MM_VMEM_BUDGET)
    steps = m // tm

    def body(*refs):
        a_ref, b_ref, o_ref = refs[0], refs[1], refs[n_in + nx]
        i = pl.program_id(0)
        finish = _ride(scatter, True, refs[n_in:n_in + nx], refs[n_in + nx + 1:n_in + 2 * nx + 1], refs[n_in + 2 * nx + 1:], i == 0, i == steps - 1)
        out = _dot(a_ref[...], b_ref[...], dims)
        if res is not None:
            out = out + refs[2][...].astype(F32)
        o_ref[...] = out.astype(o_ref.dtype)
        finish()

    o_spec = pl.BlockSpec((tm, n), lambda i: (i, 0))
    in_specs = [pl.BlockSpec((tm, k), lambda i: (i, 0)), pl.BlockSpec(b.shape, lambda i: (0, 0))] + ([o_spec] if res is not None else [])
    args = (a, b) + ((res,) if res is not None else ())
    out = pl.pallas_call(
        body, grid=(steps,), in_specs=in_specs + [_HBM] * nx, out_specs=[o_spec] + [_HBM] * nx,
        out_shape=[jax.ShapeDtypeStruct((m, n), out_dtype)] + Exchange.out_shape(scatter, True), scratch_shapes=Exchange.scratch(nx) if nx else [],
        compiler_params=_cparams(("arbitrary",) if nx else ("parallel",)), name=name)(*args, *scatter)
    return out if nx else out[0]


def matmul(name, a, b, mode, out_dtype=F32, res=None, scatter=()):
    if mode != "tn":
        return _matmul_rows(name, a, b, mode, out_dtype, res, scatter)
    (k, m), (k2, n) = a.shape, b.shape
    assert k == k2 and res is None, (name, a.shape, b.shape, mode)
    tm = _pick(m, (1024, 512, 384, 256, 128))
    tn = _pick(n, (1024, 512, 384, 256, 128))
    tk = _pick(k, (512, 256, 128))
    nk = k // tk
    dims = ((0,), (0,))

    def body(*refs):
        if res is None:
            a_ref, b_ref, o_ref, acc_ref = refs
        else:
            a_ref, b_ref, r_ref, o_ref, acc_ref = refs
        kk = pl.program_id(2)

        @pl.when(kk == 0)
        def _():
            acc_ref[...] = jnp.zeros_like(acc_ref)

        acc_ref[...] += _dot(a_ref[...], b_ref[...], dims)

        @pl.when(kk == nk - 1)
        def _():
            out = acc_ref[...]
            if res is not None:
                out = out + r_ref[...].astype(F32)
            o_ref[...] = out.astype(o_ref.dtype)

    a_spec = pl.BlockSpec((tk, tm), lambda i, j, kk: (kk, i)) if mode == "tn" else pl.BlockSpec((tm, tk), lambda i, j, kk: (i, kk))
    b_spec = pl.BlockSpec((tn, tk), lambda i, j, kk: (j, kk)) if mode == "nt" else pl.BlockSpec((tk, tn), lambda i, j, kk: (kk, j))
    o_spec = pl.BlockSpec((tm, tn), lambda i, j, kk: (i, j))
    in_specs = [a_spec, b_spec] + ([o_spec] if res is not None else [])
    args = (a, b) + ((res,) if res is not None else ())
    return pl.pallas_call(
        body, grid=(m // tm, n // tn, nk), in_specs=in_specs, out_specs=o_spec,
        out_shape=jax.ShapeDtypeStruct((m, n), out_dtype), scratch_shapes=[pltpu.VMEM((tm, tn), F32)],
        compiler_params=_cparams(("parallel", "parallel", "arbitrary")), name=name)(*args)


@dataclasses.dataclass
class Arg:
    arr: jax.Array
    kind: str = "row"
    bc: int = 0
    base: int = 0
    ph: bool = False
    diff: bool = False
    gdt: object = F32


def _arg_spec(a, tr, nh, ntab, base=None):
    bc = a.bc or a.arr.shape[1]
    base = a.base if base is None else base
    width = bc * nh if a.ph else bc
    col = base // nh if a.ph else base
    assert not a.ph or base % nh == 0
    if a.kind == "row":
        return pl.BlockSpec((tr, width), lambda i: (i, col))
    if a.kind == "tab":
        return pl.BlockSpec((tr, width), lambda i: (i % ntab, col))
    return pl.BlockSpec((a.arr.shape[0], width), lambda i: (0, col))


def _head_view(ref, a, h, rs):
    bc = a.bc or a.arr.shape[1]
    rows = slice(None) if a.kind == "par" else rs
    v = ref[rows, h * bc:(h + 1) * bc] if a.ph else ref[rows, :]
    return v.astype(F32) if jnp.issubdtype(v.dtype, jnp.floating) else v


def row_call(name, fn, args, outs, tr, nh=1, ntab=1):
    t = args[0].arr.shape[0]
    n_in = len(args)
    out_args = [Arg(None, "row", bc, 0, ph) for (_, _, bc, ph) in outs]
    assert all(a.ph or nh == 1 for a in out_args)
    rs = slice(None)

    def body(*refs):
        for h in range(nh):
            res = fn(*[_head_view(r, a, h, rs) for r, a in zip(refs[:n_in], args, strict=True)])
            for r, a, v in zip(refs[n_in:], out_args, res, strict=True):
                r[rs, h * a.bc:(h + 1) * a.bc] = v.astype(r.dtype)

    return pl.pallas_call(
        body, grid=(t // tr,), in_specs=[_arg_spec(a, tr, nh, ntab) for a in args], out_specs=[_arg_spec(a, tr, nh, ntab) for a in out_args],
        out_shape=[jax.ShapeDtypeStruct((t, cols), dt) for (cols, dt, _, _) in outs],
        compiler_params=_cparams(("arbitrary",)), name=name)(*[a.arr for a in args])


def row_vjp_call(name, fn, args, cts, tr, nh=1, ntab=1):
    t = args[0].arr.shape[0]
    n_in, n_ct = len(args), len(cts)
    diff_idx = [k for k, a in enumerate(args) if a.diff]
    def body(*refs):
        out_refs = refs[n_in + n_ct:]
        par_sum = {}
        for k, r in zip(diff_idx, out_refs, strict=True):
            if args[k].kind == "par":
                @pl.when(pl.program_id(0) == 0)
                def _(r=r):
                    r[...] = jnp.zeros_like(r)

        for rs in (slice(None),):
            row_sum = {}
            for h in range(nh):
                vals = [_head_view(r, a, h, rs) for r, a in zip(refs[:n_in], args, strict=True)]
                ct_vals = tuple(_head_view(r, a, h, rs) for r, a in zip(refs[n_in:n_in + n_ct], cts, strict=True))

                def f(*dv, vals=vals):
                    full = list(vals)
                    for k, v in zip(diff_idx, dv, strict=True):
                        full[k] = v
                    return tuple(fn(*full))

                _, vjp = jax.vjp(f, *[vals[k] for k in diff_idx])
                for j, (k, r, g) in enumerate(zip(diff_idx, out_refs, vjp(ct_vals), strict=True)):
                    a = args[k]
                    bc = a.bc or a.arr.shape[1]
                    if a.kind == "row" and a.ph:
                        r[rs, h * bc:(h + 1) * bc] = g.astype(r.dtype)
                    elif a.kind == "row":
                        row_sum[j] = g if j not in row_sum else row_sum[j] + g
                    else:
                        key = (j, h if a.ph else 0)
                        par_sum[key] = g if key not in par_sum else par_sum[key] + g
            for j, g in row_sum.items():
                out_refs[j][rs, :] = g.astype(out_refs[j].dtype)
        for (j, h), g in par_sum.items():
            bc = g.shape[1]
            out_refs[j][:, h * bc:(h + 1) * bc] += g

    out_specs, out_shape = [], []
    for k in diff_idx:
        a = args[k]
        bc = a.bc or a.arr.shape[1]
        out_specs.append(_arg_spec(a, tr, nh, ntab, base=0))
        out_shape.append(jax.ShapeDtypeStruct((t if a.kind == "row" else a.arr.shape[0], bc * (nh if a.ph else 1)), a.gdt if a.kind == "row" else F32))
    in_specs = [_arg_spec(a, tr, nh, ntab) for a in list(args) + list(cts)]
    return pl.pallas_call(
        body, grid=(t // tr,), in_specs=in_specs, out_specs=out_specs, out_shape=out_shape,
        compiler_params=_cparams(("arbitrary",)), name=name)(*[a.arr for a in list(args) + list(cts)])


def _conv_taps(x, w):
    rows = lax.broadcasted_iota(jnp.int32, x.shape, 0)
    y = x * w[CONV_K - 1:CONV_K, :]
    for s in range(1, CONV_K):
        y = y + jnp.where(rows >= s, pltpu.roll(x, s, 0), 0.0) * w[CONV_K - 1 - s:CONV_K - s, :]
    return y


CONV_HEADS = 4
CONV_BLOCKS_PER_THIRD = N_HEADS // CONV_HEADS


def _conv_post(y, block):
    a = _silu(y)
    normed = block < 2 * CONV_BLOCKS_PER_THIRD
    scale = jnp.where(block < CONV_BLOCKS_PER_THIRD, HEAD ** -0.5, 1.0)
    return a * jnp.where(normed, lax.rsqrt(jnp.sum(a * a, -1, keepdims=True) + EPS) * scale, 1.0)


def conv_fwd(z, w, lp):
    t, width = z.shape
    cols = CONV_HEADS * HEAD

    def body(z_ref, w_ref, o_ref, y_ref):
        block = pl.program_id(1)
        for h in range(CONV_HEADS):
            cs = slice(h * HEAD, (h + 1) * HEAD)
            y = _conv_taps(z_ref[:, cs], w_ref[:, cs])
            y_ref[:, cs] = y
            o_ref[:, cs] = _conv_post(y, block)

    blk = pl.BlockSpec((lp, cols), lambda b, j: (b, j))
    out = jax.ShapeDtypeStruct((t, width), F32)
    return pl.pallas_call(
        body, grid=(t // lp, width // cols), in_specs=[blk, pl.BlockSpec((CONV_K, cols), lambda b, j: (0, j))],
        out_specs=[blk, blk], out_shape=[out, out], compiler_params=_cparams(("arbitrary", "arbitrary")), name="a_conv_fwd")(z, w)


def conv_bwd(z, y, w, dout, lp):
    t, width = z.shape
    cols = CONV_HEADS * HEAD

    def body(z_ref, y_ref, w_ref, g_ref, dz_ref, dw_ref):
        block = pl.program_id(0)

        @pl.when(pl.program_id(1) == 0)
        def _():
            dw_ref[...] = jnp.zeros_like(dw_ref)

        for h in range(CONV_HEADS):
            cs = slice(h * HEAD, (h + 1) * HEAD)
            x, wv = z_ref[:, cs], w_ref[:, cs]
            _, vjp = jax.vjp(lambda y_: _conv_post(y_, block), y_ref[:, cs])
            (dy,) = vjp(g_ref[:, cs])
            rows = lax.broadcasted_iota(jnp.int32, x.shape, 0)
            dx = dy * wv[CONV_K - 1:CONV_K, :]
            dw_ref[CONV_K - 1:CONV_K, cs] += jnp.sum(dy * x, axis=0, keepdims=True)
            for s in range(1, CONV_K):
                dy_up = jnp.where(rows < lp - s, pltpu.roll(dy, lp - s, 0), 0.0)
                dx = dx + dy_up * wv[CONV_K - 1 - s:CONV_K - s, :]
                dw_ref[CONV_K - 1 - s:CONV_K - s, cs] += jnp.sum(dy_up * x, axis=0, keepdims=True)
            dz_ref[:, cs] = dx.astype(dz_ref.dtype)

    blk = pl.BlockSpec((lp, cols), lambda j, b: (b, j))
    w_blk = pl.BlockSpec((CONV_K, cols), lambda j, b: (0, j))
    return pl.pallas_call(
        body, grid=(width // cols, t // lp), in_specs=[blk, blk, w_blk, blk], out_specs=[blk, w_blk],
        out_shape=[jax.ShapeDtypeStruct((t, width), _MXU_DTYPE), jax.ShapeDtypeStruct((CONV_K, width), F32)],
        compiler_params=_cparams(("arbitrary", "arbitrary")), name="a_conv_bwd")(z, y, w, dout)


def _delta_chunk(q, k, v, ba, alog, dtb, state, t_stored):
    n_g, c = q.shape[0], q.shape[1]
    lane = lax.broadcasted_iota(jnp.int32, (1, HEAD), 1)

    def pick(xs, offset):
        cols = [jnp.sum(xs[i // N_HEADS if len(xs) > 1 else 0] * (lane == offset + i % N_HEADS).astype(F32), axis=1, keepdims=True)[None]
                for i in range(n_g)]
        return jnp.concatenate(cols, 0)

    b_raw, a_raw = pick(ba, 0), pick(ba, N_HEADS)
    a_log, dt_bias = pick((alog,), 0), pick((dtb,), 0)
    beta = _sigmoid(b_raw)
    g = -jnp.exp(a_log) * _softplus(a_raw + dt_bias)
    ri = lax.broadcasted_iota(jnp.int32, (c, c), 0)
    ci = lax.broadcasted_iota(jnp.int32, (c, c), 1)
    tril = ci <= ri
    lower = jnp.broadcast_to(tril.astype(F32), (n_g, c, c))
    gc_col = _dot_01(lower, g * jnp.ones((1, 1, HEAD), F32))[:, :, :1]
    gc_row = _dot_01(jnp.ones((n_g, 8, c), F32), g * (ri <= ci).astype(F32)[None])[:, 0:1, :]
    gc_last = jnp.sum(g, axis=1, keepdims=True)
    decay = jnp.exp(jnp.where(tril, gc_col - gc_row, NEG))
    e_gc = jnp.exp(gc_col)
    kb = k * beta
    a_mat = jnp.where(ci < ri, mm_nt(kb, k) * decay, 0.0)
    t_inv = _inv_unit_lower(a_mat) if t_stored is None else _inv_lookup(a_mat, t_stored)
    u_base = mm_nn(t_inv, v * beta)
    w_dec = mm_nn(t_inv, kb * e_gc)
    attn = jnp.where(tril, mm_nt(q, k) * decay, 0.0)
    u = u_base - mm_nn(w_dec, state)
    o = mm_nn(q * e_gc, state) + mm_nn(attn, u)
    new_state = state * jnp.exp(gc_last) + mm_tn(k * jnp.exp(gc_last - gc_col), u)
    return o, new_state, t_inv


DELTA_STEP_FWD = (4, 2)
DELTA_STEP_BWD = (2, 2)


def _heads_of(ref, rs, first_col):
    return jnp.stack([ref[i // N_HEADS, rs, first_col + (i % N_HEADS) * HEAD:first_col + (i % N_HEADS + 1) * HEAD]
                      for i in range(ref.shape[0] * N_HEADS)])


def _qkv_heads(ref, rs, part):
    return _heads_of(ref, rs, part * N_HEADS * HEAD)


def _by_sequence(a, lp):
    return a.reshape(a.shape[0] // lp, lp, a.shape[1])


def _ride(bufs, scatter, refs_in, refs_out, sems, first, last):
    if not bufs:
        return lambda: None

    @pl.when(first)
    def _():
        Exchange(refs_in, refs_out, *sems, scatter).start()

    def finish():
        @pl.when(last)
        def _():
            Exchange(refs_in, refs_out, *sems, scatter).wait()

    return finish


def delta_fwd(qkv, ba, ba_block, alog, dtb, lp, gather=()):
    t = qkv.shape[0]
    nb, nc = t // lp, lp // CHUNK
    seqs, cps = DELTA_STEP_FWD
    ng, rows = nc // cps, cps * CHUNK
    nx = len(gather)
    nbg = nb // seqs
    assert nc % cps == 0 and nb % seqs == 0

    def body(*refs):
        qkv_ref, ba_ref, al_ref, dt_ref = refs[:4]
        o_ref, s_ref, t_ref = refs[4 + nx:7 + nx]
        state_ref = refs[7 + 2 * nx]
        b, n = pl.program_id(0), pl.program_id(1)
        finish = _ride(gather, False, refs[4:4 + nx], refs[7 + nx:7 + 2 * nx], refs[8 + 2 * nx:], (b == 0) & (n == 0), (b == nbg - 1) & (n == ng - 1))

        @pl.when(n == 0)
        def _():
            state_ref[...] = jnp.zeros_like(state_ref)

        al, dtv = al_ref[...], dt_ref[...]
        for c in range(cps):
            rs = slice(c * CHUNK, (c + 1) * CHUNK)
            state = state_ref[...]
            o, new_state, t_inv = _delta_chunk(_qkv_heads(qkv_ref, rs, 0), _qkv_heads(qkv_ref, rs, 1), _qkv_heads(qkv_ref, rs, 2),
                                               tuple(ba_ref[i, rs, :] for i in range(seqs)), al, dtv, state, None)
            for i in range((seqs * N_HEADS)):
                seq, g = divmod(i, N_HEADS)
                o_ref[seq, rs, g * HEAD:(g + 1) * HEAD] = o[i]
                s_ref[seq, g, c] = state[i]
                t_ref[seq, g, c] = t_inv[i]
            state_ref[...] = new_state
        finish()

    rows_of = lambda width: pl.BlockSpec((seqs, rows, width), lambda b, n: (b, n, 0))
    par_spec = pl.BlockSpec((1, HEAD), lambda b, n: (0, 0))
    out = pl.pallas_call(
        body, grid=(nbg, ng),
        in_specs=[rows_of(3 * N_HEADS * HEAD), pl.BlockSpec((seqs, rows, HEAD), lambda b, n: (b, n, ba_block)), par_spec, par_spec] + [_HBM] * nx,
        out_specs=[rows_of(N_HEADS * HEAD), pl.BlockSpec((seqs, N_HEADS, cps, HEAD, HEAD), lambda b, n: (b, 0, n, 0, 0)),
                   pl.BlockSpec((seqs, N_HEADS, cps, CHUNK, CHUNK), lambda b, n: (b, 0, n, 0, 0))] + [_HBM] * nx,
        out_shape=[jax.ShapeDtypeStruct((nb, lp, N_HEADS * HEAD), F32), jax.ShapeDtypeStruct((nb, N_HEADS, nc, HEAD, HEAD), F32),
                   jax.ShapeDtypeStruct((nb, N_HEADS, nc, CHUNK, CHUNK), F32)] + Exchange.out_shape(gather, False),
        scratch_shapes=[pltpu.VMEM(((seqs * N_HEADS), HEAD, HEAD), F32)] + (Exchange.scratch(nx) if nx else []),
        compiler_params=_cparams(("arbitrary", "arbitrary")), name="delta_fwd")(_by_sequence(qkv, lp), _by_sequence(ba, lp), alog, dtb, *gather)
    return [out[0].reshape(t, N_HEADS * HEAD)] + list(out[1:])


def delta_bwd(qkv, ba, ba_block, alog, dtb, states, t_invs, do, lp, scatter=()):
    t = qkv.shape[0]
    nb, nc = t // lp, lp // CHUNK
    seqs, cps = DELTA_STEP_BWD
    ng, rows = nc // cps, cps * CHUNK
    nx = len(scatter)
    nbg = nb // seqs

    def body(*refs):
        qkv_ref, ba_ref, al_ref, dt_ref, s_ref, t_ref, do_ref = refs[:7]
        dqkv_ref, dba_ref, dal_ref, ddt_ref = refs[7 + nx:11 + nx]
        dstate_ref = refs[11 + 2 * nx]
        b, step = pl.program_id(0), pl.program_id(1)
        finish = _ride(scatter, True, refs[7:7 + nx], refs[11 + nx:11 + 2 * nx], refs[12 + 2 * nx:], (b == 0) & (step == 0),
                       (b == nbg - 1) & (step == ng - 1))

        @pl.when(step == 0)
        def _():
            dstate_ref[...] = jnp.zeros_like(dstate_ref)

        @pl.when((b == 0) & (step == 0))
        def _():
            dal_ref[...] = jnp.zeros_like(dal_ref)
            ddt_ref[...] = jnp.zeros_like(ddt_ref)

        al, dtv = al_ref[...], dt_ref[...]
        d_al = jnp.zeros((1, HEAD), F32)
        d_dt = jnp.zeros((1, HEAD), F32)
        for c in reversed(range(cps)):
            rs = slice(c * CHUNK, (c + 1) * CHUNK)
            t_n = jnp.stack([t_ref[i // N_HEADS, i % N_HEADS, c] for i in range((seqs * N_HEADS))])
            s_n = jnp.stack([s_ref[i // N_HEADS, i % N_HEADS, c] for i in range((seqs * N_HEADS))])

            def f(q_, k_, v_, ba_, al_, dt_, s_, t_n=t_n):
                return _delta_chunk(q_, k_, v_, ba_, al_, dt_, s_, t_n)[:2]

            _, vjp = jax.vjp(f, _qkv_heads(qkv_ref, rs, 0), _qkv_heads(qkv_ref, rs, 1), _qkv_heads(qkv_ref, rs, 2), tuple(ba_ref[i, rs, :] for i in range(seqs)), al, dtv, s_n)
            grads = vjp((_heads_of(do_ref, rs, 0), dstate_ref[...]))
            for part in range(3):
                for i in range((seqs * N_HEADS)):
                    col = (part * N_HEADS + i % N_HEADS) * HEAD
                    dqkv_ref[i // N_HEADS, rs, col:col + HEAD] = grads[part][i]
            for i in range(seqs):
                dba_ref[i, rs, :] = grads[3][i]
            d_al, d_dt = d_al + grads[4], d_dt + grads[5]
            dstate_ref[...] = grads[6]
        dal_ref[...] += d_al
        ddt_ref[...] += d_dt
        finish()

    rows_of = lambda width: pl.BlockSpec((seqs, rows, width), lambda b, n: (b, ng - 1 - n, 0))
    par_spec = pl.BlockSpec((1, HEAD), lambda b, n: (0, 0))
    out = pl.pallas_call(
        body, grid=(nbg, ng),
        in_specs=[rows_of(3 * N_HEADS * HEAD), pl.BlockSpec((seqs, rows, HEAD), lambda b, n: (b, ng - 1 - n, ba_block)), par_spec, par_spec,
                  pl.BlockSpec((seqs, N_HEADS, cps, HEAD, HEAD), lambda b, n: (b, 0, ng - 1 - n, 0, 0)),
                  pl.BlockSpec((seqs, N_HEADS, cps, CHUNK, CHUNK), lambda b, n: (b, 0, ng - 1 - n, 0, 0)), rows_of(N_HEADS * HEAD)] + [_HBM] * nx,
        out_specs=[rows_of(3 * N_HEADS * HEAD), rows_of(HEAD), par_spec, par_spec] + [_HBM] * nx,
        out_shape=[jax.ShapeDtypeStruct((nb, lp, 3 * N_HEADS * HEAD), F32), jax.ShapeDtypeStruct((nb, lp, HEAD), F32),
                   jax.ShapeDtypeStruct((1, HEAD), F32), jax.ShapeDtypeStruct((1, HEAD), F32)] + Exchange.out_shape(scatter, True),
        scratch_shapes=[pltpu.VMEM(((seqs * N_HEADS), HEAD, HEAD), F32)] + (Exchange.scratch(nx) if nx else []),
        compiler_params=_cparams(("arbitrary", "arbitrary")), name="delta_bwd")(
            _by_sequence(qkv, lp), _by_sequence(ba, lp), alog, dtb, states, t_invs, _by_sequence(do, lp), *scatter)
    return [out[0].reshape(t, 3 * N_HEADS * HEAD), out[1].reshape(t, HEAD)] + list(out[2:])


ATT_Q_TILE = 256
ATT_K_TILE = 512
ATT_SCALE = QK_DIM ** -0.5


def _tiles(end, size):
    return [(s, min(s + size, end)) for s in range(0, end, size)]


def _att_visible(q0, q1, k0, k1, keys_first):
    if k1 <= q0 + CHUNK and k0 >= PAD_ROWS:
        return None
    shape = (k1 - k0, q1 - q0) if keys_first else (q1 - q0, k1 - k0)
    qpos = q0 + lax.broadcasted_iota(jnp.int32, shape, 1 if keys_first else 0)
    kpos = k0 + lax.broadcasted_iota(jnp.int32, shape, 0 if keys_first else 1)
    shift = CHUNK.bit_length() - 1
    return (jnp.right_shift(kpos, shift) <= jnp.right_shift(qpos, shift)) & (kpos >= PAD_ROWS)


def _att_seq_specs(lp):
    return pl.BlockSpec((lp, QK_PAD), lambda b, h: (b, h)), pl.BlockSpec((lp, HEAD), lambda b, h: (b, h))


def flash_fwd(q, k, v, lp):
    t = q.shape[0]
    qk_seq, o_seq = _att_seq_specs(lp)

    def body(q_ref, k_ref, v_ref, o_ref, lse_ref):
        q_tiles = _tiles(lp, ATT_Q_TILE)

        def score_steps(q0, q1, out):
            def step(k0, k1):
                s = mm_nt(q_ref[q0:q1, :], k_ref[k0:k1, :])
                vis = _att_visible(q0, q1, k0, k1, False)
                s = s if vis is None else jnp.where(vis, s, NEG)
                out["scores"].append(s)
                row_max = jnp.max(s, -1, keepdims=True)
                out["m"] = row_max if out["m"] is None else jnp.maximum(out["m"], row_max)
            return [functools.partial(step, k0, k1) for k0, k1 in _tiles(q1, ATT_K_TILE)]

        cur = {"scores": [], "m": None}
        for step in score_steps(*q_tiles[0], cur):
            step()
        for i, (q0, q1) in enumerate(q_tiles):
            nxt = {"scores": [], "m": None}
            ahead = score_steps(*q_tiles[i + 1], nxt) if i + 1 < len(q_tiles) else []
            l = jnp.zeros((q1 - q0, 1), F32)
            acc = jnp.zeros((q1 - q0, HEAD), F32)
            for s, (k0, k1) in zip(cur["scores"], _tiles(q1, ATT_K_TILE), strict=True):
                if ahead:
                    ahead.pop(0)()
                p = jnp.exp(s - cur["m"])
                l = l + jnp.sum(p, -1, keepdims=True)
                acc = acc + mm_nn(p, v_ref[k0:k1, :])
            for step in ahead:
                step()
            o_ref[q0:q1, :] = acc / l
            lse_ref[q0:q1, :] = jnp.broadcast_to(cur["m"] + jnp.log(l), (q1 - q0, HEAD))
            cur = nxt

    big = jax.ShapeDtypeStruct((t, N_HEADS * HEAD), F32)
    return pl.pallas_call(
        body, grid=(t // lp, N_HEADS), in_specs=[qk_seq, qk_seq, o_seq], out_specs=[o_seq, o_seq], out_shape=[big, big],
        compiler_params=_cparams(("arbitrary", "arbitrary")), name="flash_fwd")(q, k, v)


def flash_bwd(q, k, v, o, lse, do, lp):
    t = q.shape[0]
    qk_seq, o_seq = _att_seq_specs(lp)

    def body(q_ref, k_ref, v_ref, o_ref, lse_ref, do_ref, dq_ref, dk_out_ref, dv_out_ref, dk_ref, dv_ref):
        dk_ref[...] = jnp.zeros_like(dk_ref)
        dv_ref[...] = jnp.zeros_like(dv_ref)
        for q0, q1 in _tiles(lp, ATT_Q_TILE):
            qb, dob = q_ref[q0:q1, :], do_ref[q0:q1, :]
            lse_row = jnp.transpose(lse_ref[q0:q1, :])[0:1, :]
            dsum_row = jnp.sum(jnp.transpose(dob * o_ref[q0:q1, :]), axis=0, keepdims=True)
            dq = jnp.zeros((q1 - q0, QK_PAD), F32)
            for k0, k1 in _tiles(q1, ATT_K_TILE):
                kb, vb = k_ref[k0:k1, :], v_ref[k0:k1, :]
                s = mm_nt(kb, qb)
                vis = _att_visible(q0, q1, k0, k1, True)
                s = s if vis is None else jnp.where(vis, s, NEG)
                p = jnp.exp(s - lse_row)
                ds = p * (mm_nt(vb, dob) - dsum_row)
                dv_ref[k0:k1, :] += mm_nn(p, dob)
                dk_ref[k0:k1, :] += mm_nn(ds, qb)
                dq = dq + mm_tn(ds, kb)
            dq_ref[q0:q1, :] = dq.astype(dq_ref.dtype)
        dk_out_ref[...] = dk_ref[...].astype(dk_out_ref.dtype)
        dv_out_ref[...] = dv_ref[...].astype(dv_out_ref.dtype)

    narrow = _MXU_DTYPE
    return pl.pallas_call(
        body, grid=(t // lp, N_HEADS), in_specs=[qk_seq, qk_seq, o_seq, o_seq, o_seq, o_seq], out_specs=[qk_seq, qk_seq, o_seq],
        out_shape=[jax.ShapeDtypeStruct((t, N_HEADS * QK_PAD), narrow), jax.ShapeDtypeStruct((t, N_HEADS * QK_PAD), narrow),
                   jax.ShapeDtypeStruct((t, N_HEADS * HEAD), narrow)],
        scratch_shapes=[pltpu.VMEM((lp, QK_PAD), F32), pltpu.VMEM((lp, HEAD), F32)],
        compiler_params=_cparams(("arbitrary", "arbitrary")), name="flash_bwd")(q, k, v, o, lse, do)


def loss_head(h2, target, lp):
    nb, seq, d = target.shape
    cols = _pick(d, (512, 128))
    ncol = d // cols

    def body(h_ref, t_ref, loss_ref, dh_ref, acc_ref):
        b, j = pl.program_id(0), pl.program_id(1)

        @pl.when((b == 0) & (j == 0))
        def _():
            acc_ref[...] = jnp.zeros_like(acc_ref)

        err = h_ref[LEAD:, :] - t_ref[...]
        dh_ref[:LEAD, :] = jnp.zeros((LEAD, cols), F32)
        dh_ref[LEAD:, :] = err * (1.0 / d)
        acc_ref[...] += jnp.sum(err * err, axis=0, keepdims=True)

        @pl.when((b == nb - 1) & (j == ncol - 1))
        def _():
            loss_ref[...] = jnp.sum(acc_ref[...], axis=1, keepdims=True) * (0.5 / d)

    return pl.pallas_call(
        body, grid=(nb, ncol),
        in_specs=[pl.BlockSpec((None, lp, cols), lambda b, j: (b, 0, j)), pl.BlockSpec((None, seq, cols), lambda b, j: (b, 0, j))],
        out_specs=[pl.BlockSpec((1, 1), lambda b, j: (0, 0)), pl.BlockSpec((None, lp, cols), lambda b, j: (b, 0, j))],
        out_shape=[jax.ShapeDtypeStruct((1, 1), F32), jax.ShapeDtypeStruct((nb, lp, d), F32)],
        scratch_shapes=[pltpu.VMEM((1, cols), F32)], compiler_params=_cparams(("arbitrary", "arbitrary")), name="loss_head")(h2, target)


def meta_grad(dh0):
    nb, _, d = dh0.shape

    def body(g_ref, o_ref):
        @pl.when(pl.program_id(0) == 0)
        def _():
            o_ref[...] = jnp.zeros_like(o_ref)

        o_ref[...] += g_ref[PAD_ROWS:LEAD, :]

    return pl.pallas_call(
        body, grid=(nb,), in_specs=[pl.BlockSpec((None, LEAD, d), lambda b: (b, 0, 0))],
        out_specs=pl.BlockSpec((N_META, d), lambda b: (0, 0)), out_shape=jax.ShapeDtypeStruct((N_META, d), F32),
        compiler_params=_cparams(("arbitrary",)), name="meta_grad")(dh0)


_HBM = pl.BlockSpec(memory_space=pltpu.HBM)


def _mesh_pos():
    x, y, c = lax.axis_index("x"), lax.axis_index("y"), lax.axis_index("c")
    return x, y, c


def _peer(x, y, c, k):
    px = 1 - x if k & 4 else x
    py = 1 - y if k & 2 else y
    pc = 1 - c if k & 1 else c
    return (px, py, pc), 4 * px + 2 * py + pc


class Exchange:
    def __init__(self, x_refs, out_refs, send_sems, recv_sems, local_sems, scatter):
        self.x_refs, self.out_refs, self.scatter = x_refs, out_refs, scatter
        self.send_sems, self.recv_sems, self.local_sems = send_sems, recv_sems, local_sems
        self.pos = _mesh_pos()
        x, y, c = self.pos
        self.me = 4 * x + 2 * y + c

    @staticmethod
    def scratch(n):
        return [pltpu.SemaphoreType.DMA((n, N_DEV - 1)), pltpu.SemaphoreType.DMA((n, N_DEV - 1)), pltpu.SemaphoreType.DMA((n,))]

    @staticmethod
    def out_shape(bufs, scatter):
        return [jax.ShapeDtypeStruct(b.shape if scatter else (N_DEV,) + b.shape, b.dtype) for b in bufs]

    def _local(self, i):
        return pltpu.make_async_copy(self.x_refs[i].at[self.me] if self.scatter else self.x_refs[i], self.out_refs[i].at[self.me], self.local_sems.at[i])

    def _copy(self, i, k, landing):
        peer, peer_id = _peer(*self.pos, k)
        src = self.x_refs[i].at[peer_id] if self.scatter else self.x_refs[i]
        return pltpu.make_async_remote_copy(src_ref=src, dst_ref=self.out_refs[i].at[peer_id if landing else self.me],
                                            send_sem=self.send_sems.at[i, k - 1], recv_sem=self.recv_sems.at[i, k - 1],
                                            device_id=peer, device_id_type=pl.DeviceIdType.MESH)

    def start(self):
        for i in range(len(self.x_refs)):
            self._local(i).start()
        for k in range(1, N_DEV):
            for i in range(len(self.x_refs)):
                self._copy(i, k, False).start()

    def wait(self):
        for k in range(1, N_DEV):
            for i in range(len(self.x_refs)):
                self._copy(i, k, True).wait_recv()
        for k in range(1, N_DEV):
            for i in range(len(self.x_refs)):
                self._copy(i, k, False).wait_send()
        for i in range(len(self.x_refs)):
            self._local(i).wait()


def _exchange(name, bufs, scatter):
    n = len(bufs)

    def body(*refs):
        ex = Exchange(refs[:n], refs[n:2 * n], *refs[2 * n:], scatter)
        ex.start()
        ex.wait()

    return pl.pallas_call(body, in_specs=[_HBM] * n, out_specs=[_HBM] * n, out_shape=Exchange.out_shape(bufs, scatter),
                          scratch_shapes=Exchange.scratch(n), name=name)(*bufs)


def _f_rms(x, g):
    return (_rms(x, g),)


def _f_rms2(x, g1, g2):
    r = x * lax.rsqrt(jnp.sum(x * x, -1, keepdims=True) / x.shape[-1] + EPS)
    return r * g1, r * g2


def _f_out_gate(o, gate, gain):
    return (_rms(o, gain) * _silu(gate),)


def _f_gate(o, gate):
    return (o * _silu(gate),)


def _swap_rope_halves(x):
    return pltpu.roll(x, ROPE // 2, 1) + pltpu.roll(x, HEAD - ROPE // 2, 1)


def _qk_final_inv_rms(nope, rope_in):
    ms = (jnp.sum(nope * nope, -1, keepdims=True) + jnp.sum(rope_in * rope_in, -1, keepdims=True)) / QK_DIM
    return lax.rsqrt(ms + EPS)


@functools.partial(jax.custom_vjp, nondiff_argnums=(0,))
def _qk_final(scale, nope, rope_in, g_nope, g_rope, cos, sin):
    r = _qk_final_inv_rms(nope, rope_in)
    b = rope_in * (r * g_rope)
    out = jnp.concatenate([nope * (r * g_nope), b * cos + _swap_rope_halves(b) * sin], axis=1)
    return out if scale == 1.0 else out * scale


def _qk_final_fwd(scale, nope, rope_in, g_nope, g_rope, cos, sin):
    return _qk_final(scale, nope, rope_in, g_nope, g_rope, cos, sin), (nope, rope_in, g_nope, g_rope, cos, sin)


def _qk_final_bwd(scale, res, g):
    nope, rope_in, g_nope, g_rope, cos, sin = res
    r = _qk_final_inv_rms(nope, rope_in)
    ga, gb = g[:, :HEAD], g[:, HEAD:]
    if scale != 1.0:
        ga, gb = ga * scale, gb * scale
    db = gb * cos + _swap_rope_halves(gb * sin)
    t_a, t_b = ga * nope, db * rope_in
    d_r = jnp.sum(t_a * g_nope + t_b * g_rope, -1, keepdims=True)
    c = d_r * (r * r * r) * (-1.0 / QK_DIM)
    d_nope = ga * (r * g_nope) + nope * c
    d_rope = db * (r * g_rope) + rope_in * c
    d_g_nope = jnp.sum(t_a * r, 0, keepdims=True)
    d_g_rope = jnp.sum(t_b * r, 0, keepdims=True)
    return d_nope, d_rope, d_g_nope, d_g_rope, jnp.zeros_like(cos), jnp.zeros_like(sin)


_qk_final.defvjp(_qk_final_fwd, _qk_final_bwd)


def _f_qk_final(scale, nope, rope_in, g_nope, g_rope, cos, sin):
    return (_qk_final(scale, nope, rope_in, g_nope, g_rope, cos, sin),)


def _rope_tables(lp):
    half = ROPE // 2
    pos = jnp.maximum(jnp.arange(lp) - PAD_ROWS, 0)
    inv = ROPE_THETA ** (-jnp.arange(half, dtype=F32) / half)
    ang = pos.astype(F32)[:, None] * inv[None, :]
    zeros = jnp.zeros((lp, HEAD - ROPE), F32)
    cos = jnp.concatenate([jnp.cos(ang), jnp.cos(ang), zeros], 1)
    sin = jnp.concatenate([-jnp.sin(ang), jnp.sin(ang), zeros], 1)
    return cos, sin


def _pad_lanes(w, width=HEAD):
    return jnp.pad(w, ((0, 0), (0, width - w.shape[1])))


def _pad_rows(w, rows=HEAD):
    return jnp.pad(w, ((0, rows - w.shape[0]), (0, 0)))


def _split_heads_qk_t(w_t):
    k = w_t.shape[1]
    w3 = w_t.reshape(N_HEADS, QK_DIM, k)
    nope = w3[:, :HEAD].reshape(N_HEADS * HEAD, k)
    rope = jnp.pad(w3[:, HEAD:], ((0, 0), (0, HEAD - ROPE), (0, 0))).reshape(N_HEADS * HEAD, k)
    return jnp.concatenate([nope, rope], 0)


def _merge_heads_qk_t(g_t):
    k = g_t.shape[1]
    kw = N_HEADS * HEAD
    nope, rope = g_t[:kw].reshape(N_HEADS, HEAD, k), g_t[kw:].reshape(N_HEADS, HEAD, k)[:, :ROPE]
    return jnp.concatenate([nope, rope], 1).reshape(N_HEADS * QK_DIM, k)


def local_step(x, target, w, deferred=None):
    nb, seq, d = x.shape
    lp = seq + LEAD
    t = nb * lp
    tr = _pick(lp, (544, 128))
    ntab = lp // tr
    mxu = _MXU_DTYPE
    kw = N_HEADS * HEAD

    a_w_in_t = w["a_w_in"].astype(mxu)
    w_qkv_t, w_gba_t = a_w_in_t[:3 * kw], _pad_rows(a_w_in_t[3 * kw:], kw + HEAD)
    a_conv = w["a_conv"].T
    alog, dtb, o_gain = _pad_lanes(w["a_log"]), _pad_lanes(w["a_dt_bias"]), w["a_o_gain"]
    a_norm, kv_norm, b_norm = w["a_norm"], w["kv_norm"][None, :], w["b_norm"]
    lat_norm, qlat_norm = w["kv_latent_norm"][None, :], w["b_q_latent_norm"]
    kg_nope, kg_rope = w["k_gain"][None, :HEAD], _pad_lanes(w["k_gain"][None, HEAD:])
    qg_nope, qg_rope = w["b_q_gain"][:, :HEAD], _pad_lanes(w["b_q_gain"][:, HEAD:])
    cos, sin = _rope_tables(lp)

    meta = jnp.broadcast_to(w["meta_tokens"].T[None], (nb, N_META, d))
    h0 = jnp.concatenate([jnp.zeros((nb, PAD_ROWS, d), F32), meta, x], 1).reshape(t, d)
    (hn,) = row_call("a_norm_fwd", _f_rms, [Arg(h0), Arg(a_norm, "par")], [(d, mxu, d, False)], tr)
    z_qkv = matmul("a_in_qkv", hn, w_qkv_t, "nt")
    z_gba = matmul("a_in_gate_ba", hn, w_gba_t, "nt")
    ba_block = kw // HEAD
    qkv_a, y_conv = conv_fwd(z_qkv, a_conv, lp)
    o_a, states, t_invs, *gathered = delta_fwd(qkv_a, z_gba, ba_block, alog, dtb, lp, gather=deferred.gather_bufs if deferred else ())
    if deferred:
        w = {**w, **deferred.finish(gathered)}
    a_w_out = w["a_w_out"].astype(mxu)
    w_down = _pad_lanes(w["kv_w_down"], KV_RANK + HEAD).astype(mxu)
    w_ukv_t = jnp.concatenate([w["kv_w_uk"], w["kv_w_uv"]], 0).astype(mxu)
    b_w_in_t = w["b_w_in"].astype(mxu)
    w_cq_t, w_gb_t = b_w_in_t[:Q_RANK], b_w_in_t[Q_RANK:]
    w_q_t = _split_heads_qk_t(w["b_w_uq"]).astype(mxu)
    b_w_out = w["b_w_out"].astype(mxu)
    og_args = [Arg(o_a, bc=HEAD, ph=True, diff=True), Arg(z_gba, bc=HEAD, ph=True, diff=True, gdt=mxu), Arg(o_gain, "par", diff=True)]
    (og_a,) = row_call("a_out_gate_fwd", _f_out_gate, og_args, [(kw, mxu, HEAD, True)], tr, nh=N_HEADS)
    h1 = matmul("a_out", og_a, a_w_out, "nn", res=h0)

    hk, hb = row_call("b_norms_fwd", _f_rms2, [Arg(h1), Arg(kv_norm, "par"), Arg(b_norm, "par")], [(d, mxu, d, False), (d, mxu, d, False)], tr)
    c_down = matmul("kv_down", hk, w_down, "nn")
    c_kv_arg = Arg(c_down, bc=KV_RANK, diff=True, gdt=mxu)
    k_pe_arg = Arg(c_down, bc=HEAD, base=KV_RANK // HEAD, diff=True)
    c_q_raw = matmul("b_in_q", hb, w_cq_t, "nt")
    gate_b = matmul("b_in_gate", hb, w_gb_t, "nt")
    (c_kv,) = row_call("kv_latent_fwd", _f_rms, [c_kv_arg, Arg(lat_norm, "par")], [(KV_RANK, mxu, KV_RANK, False)], tr)
    (c_q,) = row_call("q_latent_fwd", _f_rms, [Arg(c_q_raw), Arg(qlat_norm, "par")], [(Q_RANK, mxu, Q_RANK, False)], tr)
    k_nope = matmul("k_up", c_kv, w_ukv_t[:kw], "nt")
    v_b = matmul("v_up", c_kv, w_ukv_t[kw:], "nt", out_dtype=mxu)
    q_up = matmul("q_up", c_q, w_q_t, "nt")
    tabs = [Arg(cos, "tab"), Arg(sin, "tab")]
    k_args = [Arg(k_nope, bc=HEAD, ph=True, diff=True, gdt=mxu), k_pe_arg, Arg(kg_nope, "par", diff=True), Arg(kg_rope, "par", diff=True)] + tabs
    q_args = [Arg(q_up, bc=HEAD, ph=True, diff=True, gdt=mxu), Arg(q_up, bc=HEAD, base=N_HEADS, ph=True, diff=True, gdt=mxu),
              Arg(qg_nope, "par", diff=True), Arg(qg_rope, "par", diff=True)] + tabs
    f_k_final, f_q_final = functools.partial(_f_qk_final, 1.0), functools.partial(_f_qk_final, ATT_SCALE)
    (k_fin,) = row_call("k_final_fwd", f_k_final, k_args, [(N_HEADS * QK_PAD, mxu, QK_PAD, True)], tr, nh=N_HEADS, ntab=ntab)
    (q_fin,) = row_call("q_final_fwd", f_q_final, q_args, [(N_HEADS * QK_PAD, mxu, QK_PAD, True)], tr, nh=N_HEADS, ntab=ntab)
    o_b, lse = flash_fwd(q_fin, k_fin, v_b, lp)
    gb_args = [Arg(o_b, diff=True), Arg(gate_b, diff=True, gdt=mxu)]
    (og_b,) = row_call("b_gate_fwd", _f_gate, gb_args, [(kw, mxu, kw, False)], tr)
    h2 = matmul("b_out", og_b, b_w_out, "nn", res=h1)

    loss, dh2 = loss_head(h2.reshape(nb, lp, d), target, lp)
    dh2 = dh2.reshape(t, d)
    grads = {}

    d_og_b = matmul("b_out_dx", dh2, b_w_out, "nt", out_dtype=mxu)
    grads["b_w_out"] = matmul("b_out_dw", og_b, dh2, "tn")
    d_o_b, d_gate_b = row_vjp_call("b_gate_bwd", _f_gate, gb_args, [Arg(d_og_b)], tr)
    dq_fin, dk_fin, dv_b = flash_bwd(q_fin, k_fin, v_b, o_b, lse, d_o_b, lp)
    dq_nope, dq_rope, d_qg_nope, d_qg_rope = row_vjp_call(
        "q_final_bwd", f_q_final, q_args, [Arg(dq_fin, bc=QK_PAD, ph=True)], tr, nh=N_HEADS, ntab=ntab)
    dk_nope, dk_pe, d_kg_nope, d_kg_rope = row_vjp_call(
        "k_final_bwd", f_k_final, k_args, [Arg(dk_fin, bc=QK_PAD, ph=True)], tr, nh=N_HEADS, ntab=ntab)
    grads["b_q_gain"] = jnp.concatenate([d_qg_nope, d_qg_rope[:, :ROPE]], 1)
    grads["k_gain"] = jnp.concatenate([d_kg_nope, d_kg_rope[:, :ROPE]], 1)[0]
    d_c_q = matmul("q_nope_dx", dq_nope, w_q_t[:kw], "nn")
    d_c_q = matmul("q_rope_dx", dq_rope, w_q_t[kw:], "nn", res=d_c_q)
    grads["b_w_uq"] = _merge_heads_qk_t(jnp.concatenate([matmul("q_nope_dw", dq_nope, c_q, "tn"), matmul("q_rope_dw", dq_rope, c_q, "tn")], 0))
    d_c_kv = matmul("k_up_dx", dk_nope, w_ukv_t[:kw], "nn")
    d_c_kv = matmul("v_up_dx", dv_b, w_ukv_t[kw:], "nn", res=d_c_kv)
    grads["kv_w_uk"], grads["kv_w_uv"] = matmul("k_up_dw", dk_nope, c_kv, "tn"), matmul("v_up_dw", dv_b, c_kv, "tn")
    d_c_q_raw, grads["b_q_latent_norm"] = row_vjp_call(
        "q_latent_bwd", _f_rms, [Arg(c_q_raw, diff=True, gdt=mxu), Arg(qlat_norm, "par", diff=True)], [Arg(d_c_q)], tr)
    d_c_kv_raw, d_lat = row_vjp_call(
        "kv_latent_bwd", _f_rms, [c_kv_arg, Arg(lat_norm, "par", diff=True)], [Arg(d_c_kv)], tr)
    grads["kv_latent_norm"] = d_lat[0]
    d_hb = matmul("b_in_q_dx", d_c_q_raw, w_cq_t, "nn")
    d_hb = matmul("b_in_gate_dx", d_gate_b, w_gb_t, "nn", res=d_hb, out_dtype=mxu)
    grads["b_w_in"] = jnp.concatenate([matmul("b_in_q_dw", d_c_q_raw, hb, "tn"), matmul("b_in_gate_dw", d_gate_b, hb, "tn")], 0)
    d_c_down = jnp.concatenate([d_c_kv_raw, dk_pe.astype(mxu)], 1)
    d_hk = matmul("kv_down_dx", d_c_down, w_down, "nt", out_dtype=mxu)
    grads["kv_w_down"] = matmul("kv_down_dw", hk, d_c_down, "tn")[:, :KV_RANK + ROPE]
    dh1, d_kv_norm, grads["b_norm"] = row_vjp_call(
        "b_norms_bwd", lambda x_, g1, g2: _f_rms2(x_, g1, g2) + (x_,),
        [Arg(h1, diff=True), Arg(kv_norm, "par", diff=True), Arg(b_norm, "par", diff=True)], [Arg(d_hk), Arg(d_hb), Arg(dh2)], tr)
    grads["kv_norm"] = d_kv_norm[0]

    d_og_a = matmul("a_out_dx", dh1, a_w_out, "nt", out_dtype=mxu)
    grads["a_w_out"] = matmul("a_out_dw", og_a, dh1, "tn")
    d_o_a, d_gate_a, grads["a_o_gain"] = row_vjp_call(
        "a_out_gate_bwd", _f_out_gate, og_args, [Arg(d_og_a, bc=HEAD, ph=True)], tr, nh=N_HEADS)
    dqkv_a, d_ba, d_alog, d_dtb, *received = delta_bwd(qkv_a, z_gba, ba_block, alog, dtb, states, t_invs, d_o_a, lp,
                                                        scatter=deferred.scatter_bufs(grads) if deferred else ())
    grads["a_log"], grads["a_dt_bias"] = d_alog[:, :N_HEADS], d_dtb[:, :N_HEADS]
    dz_qkv, d_conv = conv_bwd(z_qkv, y_conv, a_conv, dqkv_a, lp)
    grads["a_conv"] = d_conv.T
    dz_gba = jnp.concatenate([d_gate_a, d_ba.astype(mxu)], 1)
    grads["a_w_in"] = jnp.concatenate([matmul("a_in_qkv_dw", dz_qkv, hn, "tn"), matmul("a_in_gate_ba_dw", dz_gba, hn, "tn")[:kw + 2 * N_HEADS]], 0)
    ride = deferred.last_scatter_bufs(grads) if deferred else ()
    d_hn = matmul("a_in_qkv_dx", dz_qkv, w_qkv_t, "nn", scatter=ride)
    if ride:
        d_hn, *received_last = d_hn
        received = list(received) + received_last
    d_hn = matmul("a_in_gate_ba_dx", dz_gba, w_gba_t, "nn", res=d_hn, out_dtype=mxu)
    dh0, grads["a_norm"] = row_vjp_call("a_norm_bwd", lambda x_, g_: _f_rms(x_, g_) + (x_,),
                                        [Arg(h0, diff=True), Arg(a_norm, "par", diff=True)], [Arg(d_hn), Arg(dh1)], tr)
    dh0 = dh0.reshape(nb, lp, d)
    grads["meta_tokens"] = meta_grad(dh0).T
    return loss, dh0[:, LEAD:], grads, received


_SHARDED = (
    ("meta_tokens", True, False), ("a_norm", True, False), ("a_w_in", True, True), ("a_conv", True, False), ("a_w_out", False, True),
    ("kv_w_down", False, True), ("kv_w_uk", True, True), ("kv_w_uv", True, True), ("b_w_in", True, True), ("b_w_uq", True, True),
    ("b_w_out", False, True))
_REPLICATED = ("a_log", "a_dt_bias", "a_o_gain", "kv_norm", "kv_latent_norm", "k_gain", "b_norm", "b_q_latent_norm", "b_q_gain")
_ALL_WEIGHTS = ("meta_tokens", "a_norm", "a_w_in", "a_conv", "a_log", "a_dt_bias", "a_o_gain", "a_w_out", "kv_norm", "kv_w_down",
                "kv_latent_norm", "kv_w_uk", "kv_w_uv", "k_gain", "b_norm", "b_w_in", "b_q_latent_norm", "b_w_uq", "b_q_gain", "b_w_out")


def _round_up(n, m):
    return (n + m - 1) // m * m


def _pack_rows(pieces, row_multiple):
    padded = []
    for p in pieces:
        n = p.shape[-1]
        padded.append(jnp.pad(p, [(0, 0)] * (p.ndim - 1) + [(0, _round_up(n, PACK_COLS) - n)]))
    flat = jnp.concatenate(padded, -1)
    rows = _round_up(flat.shape[-1] // PACK_COLS, row_multiple)
    flat = jnp.pad(flat, [(0, 0)] * (flat.ndim - 1) + [(0, rows * PACK_COLS - flat.shape[-1])])
    return flat.reshape(flat.shape[:-1] + (rows, PACK_COLS))


def _unpack_rows(buf, sizes):
    flat = buf.reshape(buf.shape[:-2] + (-1,))
    out, off = [], 0
    for n in sizes:
        out.append(flat[..., off:off + n])
        off += _round_up(n, PACK_COLS)
    return out


def _shard_2d(a):
    return a.reshape(a.shape[-2:]) if a.ndim > 2 else a


def _kl_shard(a, by_cols):
    return _shard_2d(a).T if by_cols else _shard_2d(a)


_GROUPS_FIRST = (("a_w_in",),)
_GROUPS_LATER = (("a_w_out", "b_w_in", "b_w_out"), ("b_w_uq",), ("kv_w_down",), ("kv_w_uk", "kv_w_uv"))
_SMALL_SHARDED = ("meta_tokens", "a_norm", "a_conv")
_BY_COLS = {name: by_cols for name, by_cols, _ in _SHARDED}
ROW_ALIGN = 16


def _stack_rows(pieces):
    padded, starts, row = [], [], 0
    for p in pieces:
        r = p.shape[-2]
        padded.append(jnp.pad(p, [(0, 0)] * (p.ndim - 2) + [(0, _round_up(r, ROW_ALIGN) - r), (0, 0)]))
        starts.append(row)
        row += _round_up(r, ROW_ALIGN)
    return jnp.concatenate(padded, -2), starts


def _stack_group(arrays_by_name, names):
    arrays = [arrays_by_name[n].astype(BF16) for n in names]
    buf, starts = _stack_rows(arrays)
    return buf, [(n, s, a.shape[-2]) for n, s, a in zip(names, starts, arrays, strict=True)]


def _stack_groups(arrays_by_name, groups):
    stacked = [_stack_group(arrays_by_name, names) for names in groups]
    return [b for b, _ in stacked], [entries for _, entries in stacked]


def _full_from_gathered(gathered, layout):
    full = {}
    for got, entries in zip(gathered, layout, strict=True):
        for name, start, rows in entries:
            full[name] = got[:, start:start + rows].reshape(N_DEV * rows, got.shape[-1])
    return full


def gather_first_weights(local):
    shards = {n: _kl_shard(local[n], _BY_COLS[n]) for names in _GROUPS_FIRST for n in names}
    bufs, layout = _stack_groups(shards, _GROUPS_FIRST)
    small = [_kl_shard(local[n], _BY_COLS[n]) for n in _SMALL_SHARDED]
    bufs.append(_pack_rows([s.reshape(-1) for s in small], 8))
    gathered = _exchange("all_gather", bufs, scatter=False)
    full = _full_from_gathered(gathered[:-1], layout)
    for name, part, sh in zip(_SMALL_SHARDED, _unpack_rows(gathered[-1], [s.size for s in small]), small, strict=True):
        full[name] = part.reshape(N_DEV * sh.shape[0], sh.shape[1])
    full["a_norm"] = full["a_norm"].reshape(1, -1)
    return full


class LaterExchanges:
    def __init__(self, local):
        shards = {n: _kl_shard(local[n], _BY_COLS[n]) for names in _GROUPS_LATER for n in names}
        self.gather_bufs, self.layout = _stack_groups(shards, _GROUPS_LATER)

    def finish(self, gathered):
        return _full_from_gathered(gathered, self.layout)

    def scatter_bufs(self, grads):
        return _stack_groups(_owner_slices(grads, _GROUPS_LATER), _GROUPS_LATER)[0]

    def last_scatter_bufs(self, grads):
        bufs, self.last_layout = _stack_groups(_owner_slices(grads, _GROUPS_FIRST), _GROUPS_FIRST)
        return bufs


def _owner_slices(grads, groups):
    return {n: grads[n].reshape(N_DEV, -1, grads[n].shape[-1]) for names in groups for n in names}


def reduce_contributions(name, recv):
    _, r, c = recv.shape
    tr = _pick(r, (256, 128, 64, 32, 16, 8))

    def body(g_ref, o_ref):
        g = g_ref[0].astype(F32)
        for dev in range(1, N_DEV):
            g = g + g_ref[dev].astype(F32)
        o_ref[...] = g

    return pl.pallas_call(
        body, grid=(r // tr,), in_specs=[pl.BlockSpec((N_DEV, tr, c), lambda i: (0, i, 0))], out_specs=pl.BlockSpec((tr, c), lambda i: (i, 0)),
        out_shape=jax.ShapeDtypeStruct((r, c), F32), compiler_params=_cparams(("arbitrary",)), name=name)(recv)


def adamw_all(gs, ws, ms, vs):
    n = len(gs)

    def body(*refs):
        for i in range(n):
            g_ref, w_ref, m_ref, v_ref = (refs[j * n + i] for j in range(4))
            d_ref, mo_ref, vo_ref = (refs[(4 + j) * n + i] for j in range(3))
            g = g_ref[...]
            m_new = ADAM_B1 * m_ref[...] + (1.0 - ADAM_B1) * g
            v_new = ADAM_B2 * v_ref[...] + (1.0 - ADAM_B2) * (g * g)
            m_hat = m_new / (1.0 - ADAM_B1 ** ADAM_STEP)
            v_hat = v_new / (1.0 - ADAM_B2 ** ADAM_STEP)
            d_ref[...] = -ADAM_LR * (m_hat / (jnp.sqrt(v_hat) + ADAM_EPS) + ADAM_WD * w_ref[...])
            mo_ref[...] = m_new
            vo_ref[...] = v_new

    out = [jax.ShapeDtypeStruct(g.shape, F32) for g in gs] * 3
    res = pl.pallas_call(body, out_shape=out, compiler_params=pltpu.CompilerParams(vmem_limit_bytes=VMEM_LIMIT), name="adamw_all")(*gs, *ws, *ms, *vs)
    return res[:n], res[n:2 * n], res[2 * n:]


def kernel(x, meta_tokens, a_norm, a_w_in, a_conv, a_log, a_dt_bias, a_o_gain, a_w_out, kv_norm, kv_w_down, kv_latent_norm, kv_w_uk, kv_w_uv, k_gain, b_norm, b_w_in, b_q_latent_norm, b_w_uq, b_q_gain, b_w_out, loss_target, m_meta_tokens, m_a_norm, m_a_w_in, m_a_conv, m_a_log, m_a_dt_bias, m_a_o_gain, m_a_w_out, m_kv_norm, m_kv_w_down, m_kv_latent_norm, m_kv_w_uk, m_kv_w_uv, m_k_gain, m_b_norm, m_b_w_in, m_b_q_latent_norm, m_b_w_uq, m_b_q_gain, m_b_w_out, v_meta_tokens, v_a_norm, v_a_w_in, v_a_conv, v_a_log, v_a_dt_bias, v_a_o_gain, v_a_w_out, v_kv_norm, v_kv_w_down, v_kv_latent_norm, v_kv_w_uk, v_kv_w_uv, v_k_gain, v_b_norm, v_b_w_in, v_b_q_latent_norm, v_b_w_uq, v_b_q_gain, v_b_w_out):
    given = dict(locals())
    local_w = {n: given[n] for n in _ALL_WEIGHTS}
    full = gather_first_weights(local_w)
    for n in _REPLICATED:
        full[n] = local_w[n]
    later = LaterExchanges(local_w)

    loss_part, grad_x, grads, received_riding = local_step(x, loss_target, full, later)

    exact = [grads[n].reshape(N_DEV, -1) for n in _SMALL_SHARDED]
    exact += [jnp.broadcast_to(grads[n].reshape(1, -1), (N_DEV, grads[n].size)) for n in _REPLICATED]
    exact.append(jnp.broadcast_to(loss_part, (N_DEV, 1)))
    received = list(received_riding) + list(_exchange("all_to_all", [_pack_rows(exact, 8)], scatter=True))
    layout = later.layout + later.last_layout
    summed = [reduce_contributions(f"reduce_{i}", r) for i, r in enumerate(received)]

    grad_kl = {}
    for got, entries in zip(summed, layout):
        for n, start, rows in entries:
            grad_kl[n] = got[start:start + rows]
    parts = _unpack_rows(summed[-1], [p.shape[1] for p in exact])
    for n, part in zip(_SMALL_SHARDED + _REPLICATED, parts, strict=False):
        grad_kl[n] = part
    loss = parts[-1][0]

    def natural_2d(n, a):
        shape = _shard_2d(local_w[n]).shape if local_w[n].ndim > 1 else (1, local_w[n].size)
        return a.reshape(shape[::-1]).T if _BY_COLS.get(n, False) else a.reshape(shape)

    as_2d = lambda n, a: a.reshape(natural_2d(n, grad_kl[n]).shape)
    gs = [natural_2d(n, grad_kl[n]) for n in _ALL_WEIGHTS]
    deltas, new_m, new_v = adamw_all(gs, [as_2d(n, local_w[n]) for n in _ALL_WEIGHTS], [as_2d(n, given["m_" + n]) for n in _ALL_WEIGHTS],
                                     [as_2d(n, given["v_" + n]) for n in _ALL_WEIGHTS])
    results = [a.reshape(local_w[n].shape) for group in (gs, deltas, new_m, new_v) for n, a in zip(_ALL_WEIGHTS, group, strict=True)]
    return (loss, grad_x, *results)
```

```python
import dataclasses
import functools
import math

import jax
import jax.numpy as jnp
from jax import lax
from jax.experimental import pallas as pl
from jax.experimental.pallas import tpu as pltpu

F32 = jnp.float32
BF16 = jnp.bfloat16
_MXU_DTYPE = jnp.bfloat16

N_DEV = 8
D_MODEL = 1024
N_HEADS = 8
HEAD = 128
CHUNK = 64
N_META = 16
PAD_ROWS = 2 * CHUNK - N_META
LEAD = PAD_ROWS + N_META
ROPE = 64
QK_DIM = HEAD + ROPE
QK_PAD = 2 * HEAD
KV_RANK = 256
Q_RANK = 384
CONV_K = 4
EPS = 1e-6
NEG = -1e30
ROPE_THETA = 10000.0
ADAM_LR, ADAM_B1, ADAM_B2, ADAM_EPS, ADAM_WD, ADAM_STEP = 0.001, 0.9, 0.999, 1e-08, 0.01, 10
PACK_COLS = 512
VMEM_LIMIT = 56 * 1024 * 1024


def _pick(n, options):
    for o in options:
        if n % o == 0:
            return o
    raise ValueError(f"no tile for {n} among {options}")


def _cparams(sem):
    return pltpu.CompilerParams(dimension_semantics=sem, vmem_limit_bytes=VMEM_LIMIT)


def _dims(a, dims):
    if a.ndim == 2:
        return (dims, ((), ()))
    (ca,), (cb,) = dims
    return (((ca + 1,), (cb + 1,)), ((0,), (0,)))


def _dot(a, b, dims):
    return lax.dot_general(a.astype(_MXU_DTYPE), b.astype(_MXU_DTYPE), _dims(a, dims), preferred_element_type=F32)


@jax.custom_vjp
def mm_nn(a, b):
    return _dot(a, b, ((1,), (0,)))


@jax.custom_vjp
def mm_nt(a, b):
    return _dot(a, b, ((1,), (1,)))


@jax.custom_vjp
def mm_tn(a, b):
    return _dot(a, b, ((0,), (0,)))


mm_nn.defvjp(lambda a, b: (mm_nn(a, b), (a, b)), lambda r, g: (mm_nt(g, r[1]), mm_tn(r[0], g)))
mm_nt.defvjp(lambda a, b: (mm_nt(a, b), (a, b)), lambda r, g: (mm_nn(g, r[1]), mm_tn(g, r[0])))
mm_tn.defvjp(lambda a, b: (mm_tn(a, b), (a, b)), lambda r, g: (mm_nt(r[1], g), mm_nn(r[0], g)))


def _split_terms(x, n):
    terms, rest = [], x
    for _ in range(n):
        t = rest.astype(_MXU_DTYPE)
        terms.append(t)
        rest = rest - t.astype(F32)
    return terms


def _dot_01_raw(m, x, dims):
    m = m.astype(_MXU_DTYPE)
    return sum(lax.dot_general(m, t, _dims(m, dims), preferred_element_type=F32) for t in _split_terms(x, 3))


@jax.custom_vjp
def _dot_01(m, x):
    return _dot_01_raw(m, x, ((1,), (0,)))


_dot_01.defvjp(lambda m, x: (_dot_01(m, x), m), lambda m, g: (jnp.zeros_like(m), _dot_01_raw(m, g, ((0,), (0,)))))


def _inv_unit_lower(a):
    n = a.shape[-1]
    eye = (lax.broadcasted_iota(jnp.int32, (n, n), 0) == lax.broadcasted_iota(jnp.int32, (n, n), 1)).astype(F32)
    d = lambda u, w: lax.dot_general(u, w, _dims(u, ((1,), (0,))), preferred_element_type=F32)
    t = eye - a
    p = a.astype(_MXU_DTYPE)
    p = d(p, p)
    squarings = int(math.log2(n)) - 1
    for s in range(squarings):
        ph = p.astype(_MXU_DTYPE)
        t_hi, t_lo = _split_terms(t, 2)
        t = t + (d(t_hi, ph) + d(t_lo, ph))
        if s + 1 < squarings:
            p = d(ph, ph)
    return t


@jax.custom_vjp
def _inv_lookup(a, t):
    return t


def _inv_lookup_bwd(t, g):
    return -mm_tn(t, mm_nt(g, t)), jnp.zeros_like(t)


_inv_lookup.defvjp(lambda a, t: (t, t), _inv_lookup_bwd)


def _sigmoid(x):
    return 1.0 / (1.0 + jnp.exp(-x))


@jax.custom_vjp
def _silu(x):
    return x * _sigmoid(x)


def _silu_fwd(x):
    s = _sigmoid(x)
    return x * s, (x, s)


_silu.defvjp(_silu_fwd, lambda r, g: (g * (r[1] * (1.0 + r[0] * (1.0 - r[1]))),))


def _softplus(x):
    return jnp.where(x > 20.0, x, jnp.log(1.0 + jnp.exp(jnp.minimum(x, 20.0))))


def _rms(x, g, width=None):
    ms = jnp.sum(x * x, -1, keepdims=True) / (x.shape[-1] if width is None else width)
    return x * lax.rsqrt(ms + EPS) * g


MM_VMEM_BUDGET = 40 * 1024 * 1024


def _matmul_rows(name, a, b, mode, out_dtype, res, scatter):
    m, k = a.shape
    n = b.shape[1] if mode == "nn" else b.shape[0]
    dims = {"nn": ((1,), (0,)), "nt": ((1,), (1,))}[mode]
    out_bytes = jnp.dtype(out_dtype).itemsize
    n_in, nx = 2 + (res is not None), len(scatter)

    def vmem(tm):
        blocks = 2 * tm * k * a.dtype.itemsize + 2 * k * n * b.dtype.itemsize + 2 * tm * n * out_bytes + tm * n * 4
        return blocks + (2 * tm * n * res.dtype.itemsize if res is not None else 0)

    tm = next(c for c in (2176, 1088, 512, 256, 128, 64) if m % c == 0 and vmem(c) <= MM_VMEM_BUDGET)
    steps = m // tm

    def body(*refs):
        a_ref, b_ref, o_ref = refs[0], refs[1], refs[n_in + nx]
        i = pl.program_id(0)
        finish = _ride(scatter, True, refs[n_in:n_in + nx], refs[n_in + nx + 1:n_in + 2 * nx + 1], refs[n_in + 2 * nx + 1:], i == 0, i == steps - 1)
        out = _dot(a_ref[...], b_ref[...], dims)
        if res is not None:
            out = out + refs[2][...].astype(F32)
        o_ref[...] = out.astype(o_ref.dtype)
        finish()

    o_spec = pl.BlockSpec((tm, n), lambda i: (i, 0))
    in_specs = [pl.BlockSpec((tm, k), lambda i: (i, 0)), pl.BlockSpec(b.shape, lambda i: (0, 0))] + ([o_spec] if res is not None else [])
    args = (a, b) + ((res,) if res is not None else ())
    out = pl.pallas_call(
        body, grid=(steps,), in_specs=in_specs + [_HBM] * nx, out_specs=[o_spec] + [_HBM] * nx,
        out_shape=[jax.ShapeDtypeStruct((m, n), out_dtype)] + Exchange.out_shape(scatter, True), scratch_shapes=Exchange.scratch(nx) if nx else [],
        compiler_params=_cparams(("arbitrary",) if nx else ("parallel",)), name=name)(*args, *scatter)
    return out if nx else out[0]


def matmul(name, a, b, mode, out_dtype=F32, res=None, scatter=()):
    if mode != "tn":
        return _matmul_rows(name, a, b, mode, out_dtype, res, scatter)
    (k, m), (k2, n) = a.shape, b.shape
    assert k == k2 and res is None, (name, a.shape, b.shape, mode)
    tm = _pick(m, (m if m <= 1536 else 1024, 1024, 512, 384, 256, 128))
    tn = _pick(n, (1024, 512, 384, 256, 128))
    tk = _pick(k, (512, 256, 128))
    nk = k // tk
    dims = ((0,), (0,))

    def body(*refs):
        if res is None:
            a_ref, b_ref, o_ref, acc_ref = refs
        else:
            a_ref, b_ref, r_ref, o_ref, acc_ref = refs
        kk = pl.program_id(2)

        @pl.when(kk == 0)
        def _():
            acc_ref[...] = jnp.zeros_like(acc_ref)

        acc_ref[...] += _dot(a_ref[...], b_ref[...], dims)

        @pl.when(kk == nk - 1)
        def _():
            out = acc_ref[...]
            if res is not None:
                out = out + r_ref[...].astype(F32)
            o_ref[...] = out.astype(o_ref.dtype)

    a_spec = pl.BlockSpec((tk, tm), lambda i, j, kk: (kk, i)) if mode == "tn" else pl.BlockSpec((tm, tk), lambda i, j, kk: (i, kk))
    b_spec = pl.BlockSpec((tn, tk), lambda i, j, kk: (j, kk)) if mode == "nt" else pl.BlockSpec((tk, tn), lambda i, j, kk: (kk, j))
    o_spec = pl.BlockSpec((tm, tn), lambda i, j, kk: (i, j))
    in_specs = [a_spec, b_spec] + ([o_spec] if res is not None else [])
    args = (a, b) + ((res,) if res is not None else ())
    return pl.pallas_call(
        body, grid=(m // tm, n // tn, nk), in_specs=in_specs, out_specs=o_spec,
        out_shape=jax.ShapeDtypeStruct((m, n), out_dtype), scratch_shapes=[pltpu.VMEM((tm, tn), F32)],
        compiler_params=_cparams(("parallel", "parallel", "arbitrary")), name=name)(*args)


@dataclasses.dataclass
class Arg:
    arr: jax.Array
    kind: str = "row"
    bc: int = 0
    base: int = 0
    ph: bool = False
    diff: bool = False
    gdt: object = F32


def _arg_spec(a, tr, nh, ntab, base=None):
    bc = a.bc or a.arr.shape[1]
    base = a.base if base is None else base
    width = bc * nh if a.ph else bc
    col = base // nh if a.ph else base
    assert not a.ph or base % nh == 0
    if a.kind == "row":
        return pl.BlockSpec((tr, width), lambda i: (i, col))
    if a.kind == "tab":
        return pl.BlockSpec((tr, width), lambda i: (i % ntab, col))
    return pl.BlockSpec((a.arr.shape[0], width), lambda i: (0, col))


def _head_view(ref, a, h, rs):
    bc = a.bc or a.arr.shape[1]
    rows = slice(None) if a.kind == "par" else rs
    v = ref[rows, h * bc:(h + 1) * bc] if a.ph else ref[rows, :]
    return v.astype(F32) if jnp.issubdtype(v.dtype, jnp.floating) else v


def row_call(name, fn, args, outs, tr, nh=1, ntab=1):
    t = args[0].arr.shape[0]
    n_in = len(args)
    out_args = [Arg(None, "row", bc, 0, ph) for (_, _, bc, ph) in outs]
    assert all(a.ph or nh == 1 for a in out_args)
    rs = slice(None)

    def body(*refs):
        for h in range(nh):
            res = fn(*[_head_view(r, a, h, rs) for r, a in zip(refs[:n_in], args, strict=True)])
            for r, a, v in zip(refs[n_in:], out_args, res, strict=True):
                r[rs, h * a.bc:(h + 1) * a.bc] = v.astype(r.dtype)

    return pl.pallas_call(
        body, grid=(t // tr,), in_specs=[_arg_spec(a, tr, nh, ntab) for a in args], out_specs=[_arg_spec(a, tr, nh, ntab) for a in out_args],
        out_shape=[jax.ShapeDtypeStruct((t, cols), dt) for (cols, dt, _, _) in outs],
        compiler_params=_cparams(("arbitrary",)), name=name)(*[a.arr for a in args])


def row_vjp_call(name, fn, args, cts, tr, nh=1, ntab=1):
    t = args[0].arr.shape[0]
    n_in, n_ct = len(args), len(cts)
    diff_idx = [k for k, a in enumerate(args) if a.diff]
    def body(*refs):
        out_refs = refs[n_in + n_ct:]
        par_sum = {}
        for k, r in zip(diff_idx, out_refs, strict=True):
            if args[k].kind == "par":
                @pl.when(pl.program_id(0) == 0)
                def _(r=r):
                    r[...] = jnp.zeros_like(r)

        for rs in (slice(None),):
            row_sum = {}
            for h in range(nh):
                vals = [_head_view(r, a, h, rs) for r, a in zip(refs[:n_in], args, strict=True)]
                ct_vals = tuple(_head_view(r, a, h, rs) for r, a in zip(refs[n_in:n_in + n_ct], cts, strict=True))

                def f(*dv, vals=vals):
                    full = list(vals)
                    for k, v in zip(diff_idx, dv, strict=True):
                        full[k] = v
                    return tuple(fn(*full))

                _, vjp = jax.vjp(f, *[vals[k] for k in diff_idx])
                for j, (k, r, g) in enumerate(zip(diff_idx, out_refs, vjp(ct_vals), strict=True)):
                    a = args[k]
                    bc = a.bc or a.arr.shape[1]
                    if a.kind == "row" and a.ph:
                        r[rs, h * bc:(h + 1) * bc] = g.astype(r.dtype)
                    elif a.kind == "row":
                        row_sum[j] = g if j not in row_sum else row_sum[j] + g
                    else:
                        key = (j, h if a.ph else 0)
                        par_sum[key] = g if key not in par_sum else par_sum[key] + g
            for j, g in row_sum.items():
                out_refs[j][rs, :] = g.astype(out_refs[j].dtype)
        for (j, h), g in par_sum.items():
            bc = g.shape[1]
            out_refs[j][:, h * bc:(h + 1) * bc] += g

    out_specs, out_shape = [], []
    for k in diff_idx:
        a = args[k]
        bc = a.bc or a.arr.shape[1]
        out_specs.append(_arg_spec(a, tr, nh, ntab, base=0))
        out_shape.append(jax.ShapeDtypeStruct((t if a.kind == "row" else a.arr.shape[0], bc * (nh if a.ph else 1)), a.gdt if a.kind == "row" else F32))
    in_specs = [_arg_spec(a, tr, nh, ntab) for a in list(args) + list(cts)]
    return pl.pallas_call(
        body, grid=(t // tr,), in_specs=in_specs, out_specs=out_specs, out_shape=out_shape,
        compiler_params=_cparams(("arbitrary",)), name=name)(*[a.arr for a in list(args) + list(cts)])


def _conv_taps(x, w):
    rows = lax.broadcasted_iota(jnp.int32, x.shape, 0)
    y = x * w[CONV_K - 1:CONV_K, :]
    for s in range(1, CONV_K):
        y = y + jnp.where(rows >= s, pltpu.roll(x, s, 0), 0.0) * w[CONV_K - 1 - s:CONV_K - s, :]
    return y


CONV_HEADS = 4
CONV_BLOCKS_PER_THIRD = N_HEADS // CONV_HEADS


def _conv_post(y, block):
    a = _silu(y)
    normed = block < 2 * CONV_BLOCKS_PER_THIRD
    scale = jnp.where(block < CONV_BLOCKS_PER_THIRD, HEAD ** -0.5, 1.0)
    return a * jnp.where(normed, lax.rsqrt(jnp.sum(a * a, -1, keepdims=True) + EPS) * scale, 1.0)


def conv_fwd(z, w, lp):
    t, width = z.shape
    cols = CONV_HEADS * HEAD

    def body(z_ref, w_ref, o_ref, y_ref):
        block = pl.program_id(1)
        for h in range(CONV_HEADS):
            cs = slice(h * HEAD, (h + 1) * HEAD)
            y = _conv_taps(z_ref[:, cs], w_ref[:, cs])
            y_ref[:, cs] = y
            o_ref[:, cs] = _conv_post(y, block)

    blk = pl.BlockSpec((lp, cols), lambda b, j: (b, j))
    out = jax.ShapeDtypeStruct((t, width), F32)
    return pl.pallas_call(
        body, grid=(t // lp, width // cols), in_specs=[blk, pl.BlockSpec((CONV_K, cols), lambda b, j: (0, j))],
        out_specs=[blk, blk], out_shape=[out, out], compiler_params=_cparams(("arbitrary", "arbitrary")), name="a_conv_fwd")(z, w)


def conv_bwd(z, y, w, dout, lp):
    t, width = z.shape
    cols = CONV_HEADS * HEAD

    def body(z_ref, y_ref, w_ref, g_ref, dz_ref, dw_ref):
        block = pl.program_id(0)

        @pl.when(pl.program_id(1) == 0)
        def _():
            dw_ref[...] = jnp.zeros_like(dw_ref)

        for h in range(CONV_HEADS):
            cs = slice(h * HEAD, (h + 1) * HEAD)
            x, wv = z_ref[:, cs], w_ref[:, cs]
            _, vjp = jax.vjp(lambda y_: _conv_post(y_, block), y_ref[:, cs])
            (dy,) = vjp(g_ref[:, cs])
            rows = lax.broadcasted_iota(jnp.int32, x.shape, 0)
            dx = dy * wv[CONV_K - 1:CONV_K, :]
            dw_ref[CONV_K - 1:CONV_K, cs] += jnp.sum(dy * x, axis=0, keepdims=True)
            for s in range(1, CONV_K):
                dy_up = jnp.where(rows < lp - s, pltpu.roll(dy, lp - s, 0), 0.0)
                dx = dx + dy_up * wv[CONV_K - 1 - s:CONV_K - s, :]
                dw_ref[CONV_K - 1 - s:CONV_K - s, cs] += jnp.sum(dy_up * x, axis=0, keepdims=True)
            dz_ref[:, cs] = dx.astype(dz_ref.dtype)

    blk = pl.BlockSpec((lp, cols), lambda j, b: (b, j))
    w_blk = pl.BlockSpec((CONV_K, cols), lambda j, b: (0, j))
    return pl.pallas_call(
        body, grid=(width // cols, t // lp), in_specs=[blk, blk, w_blk, blk], out_specs=[blk, w_blk],
        out_shape=[jax.ShapeDtypeStruct((t, width), _MXU_DTYPE), jax.ShapeDtypeStruct((CONV_K, width), F32)],
        compiler_params=_cparams(("arbitrary", "arbitrary")), name="a_conv_bwd")(z, y, w, dout)


def _delta_chunk(q, k, v, ba, alog, dtb, state, t_stored):
    n_g, c = q.shape[0], q.shape[1]
    lane = lax.broadcasted_iota(jnp.int32, (1, HEAD), 1)

    def pick(xs, offset):
        cols = [jnp.sum(xs[i // N_HEADS if len(xs) > 1 else 0] * (lane == offset + i % N_HEADS).astype(F32), axis=1, keepdims=True)[None]
                for i in range(n_g)]
        return jnp.concatenate(cols, 0)

    b_raw, a_raw = pick(ba, 0), pick(ba, N_HEADS)
    a_log, dt_bias = pick((alog,), 0), pick((dtb,), 0)
    beta = _sigmoid(b_raw)
    g = -jnp.exp(a_log) * _softplus(a_raw + dt_bias)
    ri = lax.broadcasted_iota(jnp.int32, (c, c), 0)
    ci = lax.broadcasted_iota(jnp.int32, (c, c), 1)
    tril = ci <= ri
    lower = jnp.broadcast_to(tril.astype(F32), (n_g, c, c))
    gc_col = _dot_01(lower, g * jnp.ones((1, 1, HEAD), F32))[:, :, :1]
    gc_row = _dot_01(jnp.ones((n_g, 8, c), F32), g * (ri <= ci).astype(F32)[None])[:, 0:1, :]
    gc_last = jnp.sum(g, axis=1, keepdims=True)
    decay = jnp.exp(jnp.where(tril, gc_col - gc_row, NEG))
    e_gc = jnp.exp(gc_col)
    kb = k * beta
    a_mat = jnp.where(ci < ri, mm_nt(kb, k) * decay, 0.0)
    t_inv = _inv_unit_lower(a_mat) if t_stored is None else _inv_lookup(a_mat, t_stored)
    u_base = mm_nn(t_inv, v * beta)
    w_dec = mm_nn(t_inv, kb * e_gc)
    attn = jnp.where(tril, mm_nt(q, k) * decay, 0.0)
    u = u_base - mm_nn(w_dec, state)
    o = mm_nn(q * e_gc, state) + mm_nn(attn, u)
    new_state = state * jnp.exp(gc_last) + mm_tn(k * jnp.exp(gc_last - gc_col), u)
    return o, new_state, t_inv


DELTA_STEP_FWD = (4, 2)
DELTA_STEP_BWD = (2, 2)


def _heads_of(ref, rs, first_col):
    return jnp.stack([ref[i // N_HEADS, rs, first_col + (i % N_HEADS) * HEAD:first_col + (i % N_HEADS + 1) * HEAD]
                      for i in range(ref.shape[0] * N_HEADS)])


def _qkv_heads(ref, rs, part):
    return _heads_of(ref, rs, part * N_HEADS * HEAD)


def _by_sequence(a, lp):
    return a.reshape(a.shape[0] // lp, lp, a.shape[1])


def _ride(bufs, scatter, refs_in, refs_out, sems, first, last):
    if not bufs:
        return lambda: None

    @pl.when(first)
    def _():
        Exchange(refs_in, refs_out, *sems, scatter).start()

    def finish():
        @pl.when(last)
        def _():
            Exchange(refs_in, refs_out, *sems, scatter).wait()

    return finish


def delta_fwd(qkv, ba, ba_block, alog, dtb, lp, gather=()):
    t = qkv.shape[0]
    nb, nc = t // lp, lp // CHUNK
    seqs, cps = DELTA_STEP_FWD
    ng, rows = nc // cps, cps * CHUNK
    nx = len(gather)
    nbg = nb // seqs
    assert nc % cps == 0 and nb % seqs == 0

    def body(*refs):
        qkv_ref, ba_ref, al_ref, dt_ref = refs[:4]
        o_ref, s_ref, t_ref = refs[4 + nx:7 + nx]
        state_ref = refs[7 + 2 * nx]
        b, n = pl.program_id(0), pl.program_id(1)
        finish = _ride(gather, False, refs[4:4 + nx], refs[7 + nx:7 + 2 * nx], refs[8 + 2 * nx:], (b == 0) & (n == 0), (b == nbg - 1) & (n == ng - 1))

        @pl.when(n == 0)
        def _():
            state_ref[...] = jnp.zeros_like(state_ref)

        al, dtv = al_ref[...], dt_ref[...]
        for c in range(cps):
            rs = slice(c * CHUNK, (c + 1) * CHUNK)
            state = state_ref[...]
            o, new_state, t_inv = _delta_chunk(_qkv_heads(qkv_ref, rs, 0), _qkv_heads(qkv_ref, rs, 1), _qkv_heads(qkv_ref, rs, 2),
                                               tuple(ba_ref[i, rs, :] for i in range(seqs)), al, dtv, state, None)
            for i in range((seqs * N_HEADS)):
                seq, g = divmod(i, N_HEADS)
                o_ref[seq, rs, g * HEAD:(g + 1) * HEAD] = o[i]
                s_ref[seq, g, c] = state[i]
                t_ref[seq, g, c] = t_inv[i]
            state_ref[...] = new_state
        finish()

    rows_of = lambda width: pl.BlockSpec((seqs, rows, width), lambda b, n: (b, n, 0))
    par_spec = pl.BlockSpec((1, HEAD), lambda b, n: (0, 0))
    out = pl.pallas_call(
        body, grid=(nbg, ng),
        in_specs=[rows_of(3 * N_HEADS * HEAD), pl.BlockSpec((seqs, rows, HEAD), lambda b, n: (b, n, ba_block)), par_spec, par_spec] + [_HBM] * nx,
        out_specs=[rows_of(N_HEADS * HEAD), pl.BlockSpec((seqs, N_HEADS, cps, HEAD, HEAD), lambda b, n: (b, 0, n, 0, 0)),
                   pl.BlockSpec((seqs, N_HEADS, cps, CHUNK, CHUNK), lambda b, n: (b, 0, n, 0, 0))] + [_HBM] * nx,
        out_shape=[jax.ShapeDtypeStruct((nb, lp, N_HEADS * HEAD), F32), jax.ShapeDtypeStruct((nb, N_HEADS, nc, HEAD, HEAD), F32),
                   jax.ShapeDtypeStruct((nb, N_HEADS, nc, CHUNK, CHUNK), F32)] + Exchange.out_shape(gather, False),
        scratch_shapes=[pltpu.VMEM(((seqs * N_HEADS), HEAD, HEAD), F32)] + (Exchange.scratch(nx) if nx else []),
        compiler_params=_cparams(("arbitrary", "arbitrary")), name="delta_fwd")(_by_sequence(qkv, lp), _by_sequence(ba, lp), alog, dtb, *gather)
    return [out[0].reshape(t, N_HEADS * HEAD)] + list(out[1:])


def delta_bwd(qkv, ba, ba_block, alog, dtb, states, t_invs, do, lp, scatter=()):
    t = qkv.shape[0]
    nb, nc = t // lp, lp // CHUNK
    seqs, cps = DELTA_STEP_BWD
    ng, rows = nc // cps, cps * CHUNK
    nx = len(scatter)
    nbg = nb // seqs

    def body(*refs):
        qkv_ref, ba_ref, al_ref, dt_ref, s_ref, t_ref, do_ref = refs[:7]
        dqkv_ref, dba_ref, dal_ref, ddt_ref = refs[7 + nx:11 + nx]
        dstate_ref = refs[11 + 2 * nx]
        b, step = pl.program_id(0), pl.program_id(1)
        finish = _ride(scatter, True, refs[7:7 + nx], refs[11 + nx:11 + 2 * nx], refs[12 + 2 * nx:], (b == 0) & (step == 0),
                       (b == nbg - 1) & (step == ng - 1))

        @pl.when(step == 0)
        def _():
            dstate_ref[...] = jnp.zeros_like(dstate_ref)

        @pl.when((b == 0) & (step == 0))
        def _():
            dal_ref[...] = jnp.zeros_like(dal_ref)
            ddt_ref[...] = jnp.zeros_like(ddt_ref)

        al, dtv = al_ref[...], dt_ref[...]
        d_al = jnp.zeros((1, HEAD), F32)
        d_dt = jnp.zeros((1, HEAD), F32)
        for c in reversed(range(cps)):
            rs = slice(c * CHUNK, (c + 1) * CHUNK)
            t_n = jnp.stack([t_ref[i // N_HEADS, i % N_HEADS, c] for i in range((seqs * N_HEADS))])
            s_n = jnp.stack([s_ref[i // N_HEADS, i % N_HEADS, c] for i in range((seqs * N_HEADS))])

            def f(q_, k_, v_, ba_, al_, dt_, s_, t_n=t_n):
                return _delta_chunk(q_, k_, v_, ba_, al_, dt_, s_, t_n)[:2]

            _, vjp = jax.vjp(f, _qkv_heads(qkv_ref, rs, 0), _qkv_heads(qkv_ref, rs, 1), _qkv_heads(qkv_ref, rs, 2), tuple(ba_ref[i, rs, :] for i in range(seqs)), al, dtv, s_n)
            grads = vjp((_heads_of(do_ref, rs, 0), dstate_ref[...]))
            for part in range(3):
                for i in range((seqs * N_HEADS)):
                    col = (part * N_HEADS + i % N_HEADS) * HEAD
                    dqkv_ref[i // N_HEADS, rs, col:col + HEAD] = grads[part][i]
            for i in range(seqs):
                dba_ref[i, rs, :] = grads[3][i]
            d_al, d_dt = d_al + grads[4], d_dt + grads[5]
            dstate_ref[...] = grads[6]
        dal_ref[...] += d_al
        ddt_ref[...] += d_dt
        finish()

    rows_of = lambda width: pl.BlockSpec((seqs, rows, width), lambda b, n: (b, ng - 1 - n, 0))
    par_spec = pl.BlockSpec((1, HEAD), lambda b, n: (0, 0))
    out = pl.pallas_call(
        body, grid=(nbg, ng),
        in_specs=[rows_of(3 * N_HEADS * HEAD), pl.BlockSpec((seqs, rows, HEAD), lambda b, n: (b, ng - 1 - n, ba_block)), par_spec, par_spec,
                  pl.BlockSpec((seqs, N_HEADS, cps, HEAD, HEAD), lambda b, n: (b, 0, ng - 1 - n, 0, 0)),
                  pl.BlockSpec((seqs, N_HEADS, cps, CHUNK, CHUNK), lambda b, n: (b, 0, ng - 1 - n, 0, 0)), rows_of(N_HEADS * HEAD)] + [_HBM] * nx,
        out_specs=[rows_of(3 * N_HEADS * HEAD), rows_of(HEAD), par_spec, par_spec] + [_HBM] * nx,
        out_shape=[jax.ShapeDtypeStruct((nb, lp, 3 * N_HEADS * HEAD), F32), jax.ShapeDtypeStruct((nb, lp, HEAD), F32),
                   jax.ShapeDtypeStruct((1, HEAD), F32), jax.ShapeDtypeStruct((1, HEAD), F32)] + Exchange.out_shape(scatter, True),
        scratch_shapes=[pltpu.VMEM(((seqs * N_HEADS), HEAD, HEAD), F32)] + (Exchange.scratch(nx) if nx else []),
        compiler_params=_cparams(("arbitrary", "arbitrary")), name="delta_bwd")(
            _by_sequence(qkv, lp), _by_sequence(ba, lp), alog, dtb, states, t_invs, _by_sequence(do, lp), *scatter)
    return [out[0].reshape(t, 3 * N_HEADS * HEAD), out[1].reshape(t, HEAD)] + list(out[2:])


ATT_Q_TILE = 256
ATT_K_TILE = 512
ATT_SCALE = QK_DIM ** -0.5


def _tiles(end, size):
    return [(s, min(s + size, end)) for s in range(0, end, size)]


def _att_visible(q0, q1, k0, k1, keys_first):
    if k1 <= q0 + CHUNK and k0 >= PAD_ROWS:
        return None
    shape = (k1 - k0, q1 - q0) if keys_first else (q1 - q0, k1 - k0)
    qpos = q0 + lax.broadcasted_iota(jnp.int32, shape, 1 if keys_first else 0)
    kpos = k0 + lax.broadcasted_iota(jnp.int32, shape, 0 if keys_first else 1)
    shift = CHUNK.bit_length() - 1
    return (jnp.right_shift(kpos, shift) <= jnp.right_shift(qpos, shift)) & (kpos >= PAD_ROWS)


def _att_seq_specs(lp):
    return pl.BlockSpec((lp, QK_PAD), lambda b, h: (b, h)), pl.BlockSpec((lp, HEAD), lambda b, h: (b, h))


def flash_fwd(q, k, v, lp):
    t = q.shape[0]
    qk_seq, o_seq = _att_seq_specs(lp)

    def body(q_ref, k_ref, v_ref, o_ref, lse_ref):
        q_tiles = _tiles(lp, ATT_Q_TILE)

        def score_steps(q0, q1, out):
            def step(k0, k1):
                s = mm_nt(q_ref[q0:q1, :], k_ref[k0:k1, :])
                vis = _att_visible(q0, q1, k0, k1, False)
                s = s if vis is None else jnp.where(vis, s, NEG)
                out["scores"].append(s)
                row_max = jnp.max(s, -1, keepdims=True)
                out["m"] = row_max if out["m"] is None else jnp.maximum(out["m"], row_max)
            return [functools.partial(step, k0, k1) for k0, k1 in _tiles(q1, ATT_K_TILE)]

        cur = {"scores": [], "m": None}
        for step in score_steps(*q_tiles[0], cur):
            step()
        for i, (q0, q1) in enumerate(q_tiles):
            nxt = {"scores": [], "m": None}
            ahead = score_steps(*q_tiles[i + 1], nxt) if i + 1 < len(q_tiles) else []
            l = jnp.zeros((q1 - q0, 1), F32)
            acc = jnp.zeros((q1 - q0, HEAD), F32)
            for s, (k0, k1) in zip(cur["scores"], _tiles(q1, ATT_K_TILE), strict=True):
                if ahead:
                    ahead.pop(0)()
                p = jnp.exp(s - cur["m"])
                l = l + jnp.sum(p, -1, keepdims=True)
                acc = acc + mm_nn(p, v_ref[k0:k1, :])
            for step in ahead:
                step()
            o_ref[q0:q1, :] = acc / l
            lse_ref[q0:q1, :] = jnp.broadcast_to(cur["m"] + jnp.log(l), (q1 - q0, HEAD))
            cur = nxt

    big = jax.ShapeDtypeStruct((t, N_HEADS * HEAD), F32)
    return pl.pallas_call(
        body, grid=(t // lp, N_HEADS), in_specs=[qk_seq, qk_seq, o_seq], out_specs=[o_seq, o_seq], out_shape=[big, big],
        compiler_params=_cparams(("arbitrary", "arbitrary")), name="flash_fwd")(q, k, v)


def flash_bwd(q, k, v, o, lse, do, lp):
    t = q.shape[0]
    qk_seq, o_seq = _att_seq_specs(lp)

    def body(q_ref, k_ref, v_ref, o_ref, lse_ref, do_ref, dq_ref, dk_out_ref, dv_out_ref, dk_ref, dv_ref):
        dk_ref[...] = jnp.zeros_like(dk_ref)
        dv_ref[...] = jnp.zeros_like(dv_ref)
        for q0, q1 in _tiles(lp, ATT_Q_TILE):
            qb, dob = q_ref[q0:q1, :], do_ref[q0:q1, :]
            lse_row = jnp.transpose(lse_ref[q0:q1, :])[0:1, :]
            dsum_row = jnp.sum(jnp.transpose(dob * o_ref[q0:q1, :]), axis=0, keepdims=True)
            dq = jnp.zeros((q1 - q0, QK_PAD), F32)
            for k0, k1 in _tiles(q1, ATT_K_TILE):
                kb, vb = k_ref[k0:k1, :], v_ref[k0:k1, :]
                s = mm_nt(kb, qb)
                vis = _att_visible(q0, q1, k0, k1, True)
                s = s if vis is None else jnp.where(vis, s, NEG)
                p = jnp.exp(s - lse_row)
                ds = p * (mm_nt(vb, dob) - dsum_row)
                dv_ref[k0:k1, :] += mm_nn(p, dob)
                dk_ref[k0:k1, :] += mm_nn(ds, qb)
                dq = dq + mm_tn(ds, kb)
            dq_ref[q0:q1, :] = dq.astype(dq_ref.dtype)
        dk_out_ref[...] = dk_ref[...].astype(dk_out_ref.dtype)
        dv_out_ref[...] = dv_ref[...].astype(dv_out_ref.dtype)

    narrow = _MXU_DTYPE
    return pl.pallas_call(
        body, grid=(t // lp, N_HEADS), in_specs=[qk_seq, qk_seq, o_seq, o_seq, o_seq, o_seq], out_specs=[qk_seq, qk_seq, o_seq],
        out_shape=[jax.ShapeDtypeStruct((t, N_HEADS * QK_PAD), narrow), jax.ShapeDtypeStruct((t, N_HEADS * QK_PAD), narrow),
                   jax.ShapeDtypeStruct((t, N_HEADS * HEAD), narrow)],
        scratch_shapes=[pltpu.VMEM((lp, QK_PAD), F32), pltpu.VMEM((lp, HEAD), F32)],
        compiler_params=_cparams(("arbitrary", "arbitrary")), name="flash_bwd")(q, k, v, o, lse, do)


def loss_head(h2, target, lp):
    nb, seq, d = target.shape
    cols = _pick(d, (512, 128))
    ncol = d // cols

    def body(h_ref, t_ref, loss_ref, dh_ref, acc_ref):
        b, j = pl.program_id(0), pl.program_id(1)

        @pl.when((b == 0) & (j == 0))
        def _():
            acc_ref[...] = jnp.zeros_like(acc_ref)

        err = h_ref[LEAD:, :] - t_ref[...]
        dh_ref[:LEAD, :] = jnp.zeros((LEAD, cols), F32)
        dh_ref[LEAD:, :] = err * (1.0 / d)
        acc_ref[...] += jnp.sum(err * err, axis=0, keepdims=True)

        @pl.when((b == nb - 1) & (j == ncol - 1))
        def _():
            loss_ref[...] = jnp.sum(acc_ref[...], axis=1, keepdims=True) * (0.5 / d)

    return pl.pallas_call(
        body, grid=(nb, ncol),
        in_specs=[pl.BlockSpec((None, lp, cols), lambda b, j: (b, 0, j)), pl.BlockSpec((None, seq, cols), lambda b, j: (b, 0, j))],
        out_specs=[pl.BlockSpec((1, 1), lambda b, j: (0, 0)), pl.BlockSpec((None, lp, cols), lambda b, j: (b, 0, j))],
        out_shape=[jax.ShapeDtypeStruct((1, 1), F32), jax.ShapeDtypeStruct((nb, lp, d), F32)],
        scratch_shapes=[pltpu.VMEM((1, cols), F32)], compiler_params=_cparams(("arbitrary", "arbitrary")), name="loss_head")(h2, target)


def embed_norm(x, meta, gain, lp, gather=()):
    nb, seq, d = x.shape
    nblk, nx = lp // LEAD, len(gather)

    def body(*refs):
        x_ref, meta_ref, g_ref = refs[:3]
        h_ref, hn_ref = refs[3 + nx:5 + nx]
        b, i = pl.program_id(0), pl.program_id(1)
        finish = _ride(gather, False, refs[3:3 + nx], refs[5 + nx:5 + 2 * nx], refs[5 + 2 * nx:], (b == 0) & (i == 0), (b == nb - 1) & (i == nblk - 1))

        @pl.when(i == 0)
        def _():
            h_ref[:PAD_ROWS, :] = jnp.zeros((PAD_ROWS, d), F32)
            h_ref[PAD_ROWS:, :] = meta_ref[...]

        @pl.when(i > 0)
        def _():
            h_ref[...] = x_ref[...]

        hn_ref[...] = _rms(h_ref[...], g_ref[...]).astype(hn_ref.dtype)
        finish()

    rows = pl.BlockSpec((LEAD, d), lambda b, i: (b * nblk + i, 0))
    out = pl.pallas_call(
        body, grid=(nb, nblk),
        in_specs=[pl.BlockSpec((None, LEAD, d), lambda b, i: (b, jnp.maximum(i - 1, 0), 0)), pl.BlockSpec((N_META, d), lambda b, i: (0, 0)),
                  pl.BlockSpec((1, d), lambda b, i: (0, 0))] + [_HBM] * nx,
        out_specs=[rows, rows] + [_HBM] * nx,
        out_shape=[jax.ShapeDtypeStruct((nb * lp, d), F32), jax.ShapeDtypeStruct((nb * lp, d), _MXU_DTYPE)] + Exchange.out_shape(gather, False),
        scratch_shapes=Exchange.scratch(nx) if nx else [],
        compiler_params=_cparams(("arbitrary", "arbitrary")), name="embed_norm")(x, meta, gain, *gather)
    return list(out)


def meta_grad(dh0):
    nb, _, d = dh0.shape

    def body(g_ref, o_ref):
        @pl.when(pl.program_id(0) == 0)
        def _():
            o_ref[...] = jnp.zeros_like(o_ref)

        o_ref[...] += g_ref[PAD_ROWS:LEAD, :]

    return pl.pallas_call(
        body, grid=(nb,), in_specs=[pl.BlockSpec((None, LEAD, d), lambda b: (b, 0, 0))],
        out_specs=pl.BlockSpec((N_META, d), lambda b: (0, 0)), out_shape=jax.ShapeDtypeStruct((N_META, d), F32),
        compiler_params=_cparams(("arbitrary",)), name="meta_grad")(dh0)


_HBM = pl.BlockSpec(memory_space=pltpu.HBM)


def _mesh_pos():
    x, y, c = lax.axis_index("x"), lax.axis_index("y"), lax.axis_index("c")
    return x, y, c


def _peer(x, y, c, k):
    px = 1 - x if k & 4 else x
    py = 1 - y if k & 2 else y
    pc = 1 - c if k & 1 else c
    return (px, py, pc), 4 * px + 2 * py + pc


class Exchange:
    def __init__(self, x_refs, out_refs, send_sems, recv_sems, local_sems, scatter):
        self.x_refs, self.out_refs, self.scatter = x_refs, out_refs, scatter
        self.send_sems, self.recv_sems, self.local_sems = send_sems, recv_sems, local_sems
        self.pos = _mesh_pos()
        x, y, c = self.pos
        self.me = 4 * x + 2 * y + c

    @staticmethod
    def scratch(n):
        return [pltpu.SemaphoreType.DMA((n, N_DEV - 1)), pltpu.SemaphoreType.DMA((n, N_DEV - 1)), pltpu.SemaphoreType.DMA((n,))]

    @staticmethod
    def out_shape(bufs, scatter):
        return [jax.ShapeDtypeStruct(b.shape if scatter else (N_DEV,) + b.shape, b.dtype) for b in bufs]

    def _local(self, i):
        return pltpu.make_async_copy(self.x_refs[i].at[self.me] if self.scatter else self.x_refs[i], self.out_refs[i].at[self.me], self.local_sems.at[i])

    def _copy(self, i, k, landing):
        peer, peer_id = _peer(*self.pos, k)
        src = self.x_refs[i].at[peer_id] if self.scatter else self.x_refs[i]
        return pltpu.make_async_remote_copy(src_ref=src, dst_ref=self.out_refs[i].at[peer_id if landing else self.me],
                                            send_sem=self.send_sems.at[i, k - 1], recv_sem=self.recv_sems.at[i, k - 1],
                                            device_id=peer, device_id_type=pl.DeviceIdType.MESH)

    def start(self):
        for i in range(len(self.x_refs)):
            self._local(i).start()
        for k in range(1, N_DEV):
            for i in range(len(self.x_refs)):
                self._copy(i, k, False).start()

    def wait(self):
        for k in range(1, N_DEV):
            for i in range(len(self.x_refs)):
                self._copy(i, k, True).wait_recv()
        for k in range(1, N_DEV):
            for i in range(len(self.x_refs)):
                self._copy(i, k, False).wait_send()
        for i in range(len(self.x_refs)):
            self._local(i).wait()


def _exchange(name, bufs, scatter):
    n = len(bufs)

    def body(*refs):
        ex = Exchange(refs[:n], refs[n:2 * n], *refs[2 * n:], scatter)
        ex.start()
        ex.wait()

    return pl.pallas_call(body, in_specs=[_HBM] * n, out_specs=[_HBM] * n, out_shape=Exchange.out_shape(bufs, scatter),
                          scratch_shapes=Exchange.scratch(n), name=name)(*bufs)


def _f_rms(x, g):
    return (_rms(x, g),)


def _f_rms2(x, g1, g2):
    r = x * lax.rsqrt(jnp.sum(x * x, -1, keepdims=True) / x.shape[-1] + EPS)
    return r * g1, r * g2


def _f_out_gate(o, gate, gain):
    return (_rms(o, gain) * _silu(gate),)


def _f_gate(o, gate):
    return (o * _silu(gate),)


def _swap_rope_halves(x):
    return pltpu.roll(x, ROPE // 2, 1) + pltpu.roll(x, HEAD - ROPE // 2, 1)


def _qk_final_inv_rms(nope, rope_in):
    ms = (jnp.sum(nope * nope, -1, keepdims=True) + jnp.sum(rope_in * rope_in, -1, keepdims=True)) / QK_DIM
    return lax.rsqrt(ms + EPS)


@functools.partial(jax.custom_vjp, nondiff_argnums=(0,))
def _qk_final(scale, nope, rope_in, g_nope, g_rope, cos, sin):
    r = _qk_final_inv_rms(nope, rope_in)
    b = rope_in * (r * g_rope)
    out = jnp.concatenate([nope * (r * g_nope), b * cos + _swap_rope_halves(b) * sin], axis=1)
    return out if scale == 1.0 else out * scale


def _qk_final_fwd(scale, nope, rope_in, g_nope, g_rope, cos, sin):
    return _qk_final(scale, nope, rope_in, g_nope, g_rope, cos, sin), (nope, rope_in, g_nope, g_rope, cos, sin)


def _qk_final_bwd(scale, res, g):
    nope, rope_in, g_nope, g_rope, cos, sin = res
    r = _qk_final_inv_rms(nope, rope_in)
    ga, gb = g[:, :HEAD], g[:, HEAD:]
    if scale != 1.0:
        ga, gb = ga * scale, gb * scale
    db = gb * cos + _swap_rope_halves(gb * sin)
    t_a, t_b = ga * nope, db * rope_in
    d_r = jnp.sum(t_a * g_nope + t_b * g_rope, -1, keepdims=True)
    c = d_r * (r * r * r) * (-1.0 / QK_DIM)
    d_nope = ga * (r * g_nope) + nope * c
    d_rope = db * (r * g_rope) + rope_in * c
    d_g_nope = jnp.sum(t_a * r, 0, keepdims=True)
    d_g_rope = jnp.sum(t_b * r, 0, keepdims=True)
    return d_nope, d_rope, d_g_nope, d_g_rope, jnp.zeros_like(cos), jnp.zeros_like(sin)


_qk_final.defvjp(_qk_final_fwd, _qk_final_bwd)


def _f_qk_final(scale, nope, rope_in, g_nope, g_rope, cos, sin):
    return (_qk_final(scale, nope, rope_in, g_nope, g_rope, cos, sin),)


def _rope_tables(lp):
    half = ROPE // 2
    pos = jnp.maximum(jnp.arange(lp) - PAD_ROWS, 0)
    inv = ROPE_THETA ** (-jnp.arange(half, dtype=F32) / half)
    ang = pos.astype(F32)[:, None] * inv[None, :]
    zeros = jnp.zeros((lp, HEAD - ROPE), F32)
    cos = jnp.concatenate([jnp.cos(ang), jnp.cos(ang), zeros], 1)
    sin = jnp.concatenate([-jnp.sin(ang), jnp.sin(ang), zeros], 1)
    return cos, sin


def _pad_lanes(w, width=HEAD):
    return jnp.pad(w, ((0, 0), (0, width - w.shape[1])))


def _pad_rows(w, rows=HEAD):
    return jnp.pad(w, ((0, rows - w.shape[0]), (0, 0)))


def _split_heads_qk_t(w_t):
    k = w_t.shape[1]
    w3 = w_t.reshape(N_HEADS, QK_DIM, k)
    nope = w3[:, :HEAD].reshape(N_HEADS * HEAD, k)
    rope = jnp.pad(w3[:, HEAD:], ((0, 0), (0, HEAD - ROPE), (0, 0))).reshape(N_HEADS * HEAD, k)
    return jnp.concatenate([nope, rope], 0)


def _merge_heads_qk_t(g_t):
    k = g_t.shape[1]
    kw = N_HEADS * HEAD
    nope, rope = g_t[:kw].reshape(N_HEADS, HEAD, k), g_t[kw:].reshape(N_HEADS, HEAD, k)[:, :ROPE]
    return jnp.concatenate([nope, rope], 1).reshape(N_HEADS * QK_DIM, k)


def local_step(x, target, w, deferred=None):
    nb, seq, d = x.shape
    lp = seq + LEAD
    t = nb * lp
    tr = _pick(lp, (544, 128))
    ntab = lp // tr
    mxu = _MXU_DTYPE
    kw = N_HEADS * HEAD

    a_conv = w["a_conv"].T
    alog, dtb, o_gain = _pad_lanes(w["a_log"]), _pad_lanes(w["a_dt_bias"]), w["a_o_gain"]
    a_norm, kv_norm, b_norm = w["a_norm"], w["kv_norm"][None, :], w["b_norm"]
    lat_norm, qlat_norm = w["kv_latent_norm"][None, :], w["b_q_latent_norm"]
    kg_nope, kg_rope = w["k_gain"][None, :HEAD], _pad_lanes(w["k_gain"][None, HEAD:])
    qg_nope, qg_rope = w["b_q_gain"][:, :HEAD], _pad_lanes(w["b_q_gain"][:, HEAD:])
    cos, sin = _rope_tables(lp)

    h0, hn, *gathered = embed_norm(x, w["meta_tokens"].T, a_norm, lp, gather=deferred.first_gather_bufs if deferred else ())
    if deferred:
        w = {**w, **deferred.finish_first(gathered)}
    a_w_in_t = w["a_w_in"].astype(mxu)
    w_qkv_t, w_gba_t = a_w_in_t[:3 * kw], _pad_rows(a_w_in_t[3 * kw:], kw + HEAD)
    z_qkv = matmul("a_in_qkv", hn, w_qkv_t, "nt")
    z_gba = matmul("a_in_gate_ba", hn, w_gba_t, "nt")
    ba_block = kw // HEAD
    qkv_a, y_conv = conv_fwd(z_qkv, a_conv, lp)
    o_a, states, t_invs, *gathered = delta_fwd(qkv_a, z_gba, ba_block, alog, dtb, lp, gather=deferred.gather_bufs if deferred else ())
    if deferred:
        w = {**w, **deferred.finish(gathered)}
    a_w_out = w["a_w_out"].astype(mxu)
    w_down = _pad_lanes(w["kv_w_down"], KV_RANK + HEAD).astype(mxu)
    w_ukv_t = jnp.concatenate([w["kv_w_uk"], w["kv_w_uv"]], 0).astype(mxu)
    b_w_in_t = w["b_w_in"].astype(mxu)
    w_cq_t, w_gb_t = b_w_in_t[:Q_RANK], b_w_in_t[Q_RANK:]
    w_q_t = _split_heads_qk_t(w["b_w_uq"]).astype(mxu)
    b_w_out = w["b_w_out"].astype(mxu)
    og_args = [Arg(o_a, bc=HEAD, ph=True, diff=True), Arg(z_gba, bc=HEAD, ph=True, diff=True, gdt=mxu), Arg(o_gain, "par", diff=True)]
    (og_a,) = row_call("a_out_gate_fwd", _f_out_gate, og_args, [(kw, mxu, HEAD, True)], tr, nh=N_HEADS)
    h1 = matmul("a_out", og_a, a_w_out, "nn", res=h0)

    hk, hb = row_call("b_norms_fwd", _f_rms2, [Arg(h1), Arg(kv_norm, "par"), Arg(b_norm, "par")], [(d, mxu, d, False), (d, mxu, d, False)], tr)
    c_down = matmul("kv_down", hk, w_down, "nn")
    c_kv_arg = Arg(c_down, bc=KV_RANK, diff=True, gdt=mxu)
    k_pe_arg = Arg(c_down, bc=HEAD, base=KV_RANK // HEAD, diff=True)
    c_q_raw = matmul("b_in_q", hb, w_cq_t, "nt")
    gate_b = matmul("b_in_gate", hb, w_gb_t, "nt")
    (c_kv,) = row_call("kv_latent_fwd", _f_rms, [c_kv_arg, Arg(lat_norm, "par")], [(KV_RANK, mxu, KV_RANK, False)], tr)
    (c_q,) = row_call("q_latent_fwd", _f_rms, [Arg(c_q_raw), Arg(qlat_norm, "par")], [(Q_RANK, mxu, Q_RANK, False)], tr)
    k_nope = matmul("k_up", c_kv, w_ukv_t[:kw], "nt")
    v_b = matmul("v_up", c_kv, w_ukv_t[kw:], "nt", out_dtype=mxu)
    q_up = matmul("q_up", c_q, w_q_t, "nt")
    tabs = [Arg(cos, "tab"), Arg(sin, "tab")]
    k_args = [Arg(k_nope, bc=HEAD, ph=True, diff=True, gdt=mxu), k_pe_arg, Arg(kg_nope, "par", diff=True), Arg(kg_rope, "par", diff=True)] + tabs
    q_args = [Arg(q_up, bc=HEAD, ph=True, diff=True, gdt=mxu), Arg(q_up, bc=HEAD, base=N_HEADS, ph=True, diff=True, gdt=mxu),
              Arg(qg_nope, "par", diff=True), Arg(qg_rope, "par", diff=True)] + tabs
    f_k_final, f_q_final = functools.partial(_f_qk_final, 1.0), functools.partial(_f_qk_final, ATT_SCALE)
    (k_fin,) = row_call("k_final_fwd", f_k_final, k_args, [(N_HEADS * QK_PAD, mxu, QK_PAD, True)], tr, nh=N_HEADS, ntab=ntab)
    (q_fin,) = row_call("q_final_fwd", f_q_final, q_args, [(N_HEADS * QK_PAD, mxu, QK_PAD, True)], tr, nh=N_HEADS, ntab=ntab)
    o_b, lse = flash_fwd(q_fin, k_fin, v_b, lp)
    gb_args = [Arg(o_b, diff=True), Arg(gate_b, diff=True, gdt=mxu)]
    (og_b,) = row_call("b_gate_fwd", _f_gate, gb_args, [(kw, mxu, kw, False)], tr)
    h2 = matmul("b_out", og_b, b_w_out, "nn", res=h1)

    loss, dh2 = loss_head(h2.reshape(nb, lp, d), target, lp)
    dh2 = dh2.reshape(t, d)
    grads = {}

    d_og_b = matmul("b_out_dx", dh2, b_w_out, "nt", out_dtype=mxu)
    grads["b_w_out"] = matmul("b_out_dw", og_b, dh2, "tn")
    d_o_b, d_gate_b = row_vjp_call("b_gate_bwd", _f_gate, gb_args, [Arg(d_og_b)], tr)
    dq_fin, dk_fin, dv_b = flash_bwd(q_fin, k_fin, v_b, o_b, lse, d_o_b, lp)
    dq_nope, dq_rope, d_qg_nope, d_qg_rope = row_vjp_call(
        "q_final_bwd", f_q_final, q_args, [Arg(dq_fin, bc=QK_PAD, ph=True)], tr, nh=N_HEADS, ntab=ntab)
    dk_nope, dk_pe, d_kg_nope, d_kg_rope = row_vjp_call(
        "k_final_bwd", f_k_final, k_args, [Arg(dk_fin, bc=QK_PAD, ph=True)], tr, nh=N_HEADS, ntab=ntab)
    grads["b_q_gain"] = jnp.concatenate([d_qg_nope, d_qg_rope[:, :ROPE]], 1)
    grads["k_gain"] = jnp.concatenate([d_kg_nope, d_kg_rope[:, :ROPE]], 1)[0]
    d_c_q = matmul("q_nope_dx", dq_nope, w_q_t[:kw], "nn")
    d_c_q = matmul("q_rope_dx", dq_rope, w_q_t[kw:], "nn", res=d_c_q)
    grads["b_w_uq"] = _merge_heads_qk_t(jnp.concatenate([matmul("q_nope_dw", dq_nope, c_q, "tn"), matmul("q_rope_dw", dq_rope, c_q, "tn")], 0))
    d_c_kv = matmul("k_up_dx", dk_nope, w_ukv_t[:kw], "nn")
    d_c_kv = matmul("v_up_dx", dv_b, w_ukv_t[kw:], "nn", res=d_c_kv)
    grads["kv_w_uk"], grads["kv_w_uv"] = matmul("k_up_dw", dk_nope, c_kv, "tn"), matmul("v_up_dw", dv_b, c_kv, "tn")
    d_c_q_raw, grads["b_q_latent_norm"] = row_vjp_call(
        "q_latent_bwd", _f_rms, [Arg(c_q_raw, diff=True, gdt=mxu), Arg(qlat_norm, "par", diff=True)], [Arg(d_c_q)], tr)
    d_c_kv_raw, d_lat = row_vjp_call(
        "kv_latent_bwd", _f_rms, [c_kv_arg, Arg(lat_norm, "par", diff=True)], [Arg(d_c_kv)], tr)
    grads["kv_latent_norm"] = d_lat[0]
    d_hb = matmul("b_in_q_dx", d_c_q_raw, w_cq_t, "nn")
    d_hb = matmul("b_in_gate_dx", d_gate_b, w_gb_t, "nn", res=d_hb, out_dtype=mxu)
    grads["b_w_in"] = jnp.concatenate([matmul("b_in_q_dw", d_c_q_raw, hb, "tn"), matmul("b_in_gate_dw", d_gate_b, hb, "tn")], 0)
    d_c_down = jnp.concatenate([d_c_kv_raw, dk_pe.astype(mxu)], 1)
    d_hk = matmul("kv_down_dx", d_c_down, w_down, "nt", out_dtype=mxu)
    grads["kv_w_down"] = matmul("kv_down_dw", hk, d_c_down, "tn")[:, :KV_RANK + ROPE]
    dh1, d_kv_norm, grads["b_norm"] = row_vjp_call(
        "b_norms_bwd", lambda x_, g1, g2: _f_rms2(x_, g1, g2) + (x_,),
        [Arg(h1, diff=True), Arg(kv_norm, "par", diff=True), Arg(b_norm, "par", diff=True)], [Arg(d_hk), Arg(d_hb), Arg(dh2)], tr)
    grads["kv_norm"] = d_kv_norm[0]

    d_og_a = matmul("a_out_dx", dh1, a_w_out, "nt", out_dtype=mxu)
    grads["a_w_out"] = matmul("a_out_dw", og_a, dh1, "tn")
    d_o_a, d_gate_a, grads["a_o_gain"] = row_vjp_call(
        "a_out_gate_bwd", _f_out_gate, og_args, [Arg(d_og_a, bc=HEAD, ph=True)], tr, nh=N_HEADS)
    dqkv_a, d_ba, d_alog, d_dtb, *received = delta_bwd(qkv_a, z_gba, ba_block, alog, dtb, states, t_invs, d_o_a, lp,
                                                        scatter=deferred.scatter_bufs(grads) if deferred else ())
    grads["a_log"], grads["a_dt_bias"] = d_alog[:, :N_HEADS], d_dtb[:, :N_HEADS]
    dz_qkv, d_conv = conv_bwd(z_qkv, y_conv, a_conv, dqkv_a, lp)
    grads["a_conv"] = d_conv.T
    dz_gba = jnp.concatenate([d_gate_a, d_ba.astype(mxu)], 1)
    grads["a_w_in"] = jnp.concatenate([matmul("a_in_qkv_dw", dz_qkv, hn, "tn"), matmul("a_in_gate_ba_dw", dz_gba, hn, "tn")[:kw + 2 * N_HEADS]], 0)
    ride = deferred.last_scatter_bufs(grads) if deferred else ()
    d_hn = matmul("a_in_qkv_dx", dz_qkv, w_qkv_t, "nn", scatter=ride)
    if ride:
        d_hn, *received_last = d_hn
        received = list(received) + received_last
    d_hn = matmul("a_in_gate_ba_dx", dz_gba, w_gba_t, "nn", res=d_hn, out_dtype=mxu)
    dh0, grads["a_norm"] = row_vjp_call("a_norm_bwd", lambda x_, g_: _f_rms(x_, g_) + (x_,),
                                        [Arg(h0, diff=True), Arg(a_norm, "par", diff=True)], [Arg(d_hn), Arg(dh1)], tr)
    dh0 = dh0.reshape(nb, lp, d)
    grads["meta_tokens"] = meta_grad(dh0).T
    return loss, dh0[:, LEAD:], grads, received


_SHARDED = (
    ("meta_tokens", True, False), ("a_norm", True, False), ("a_w_in", True, True), ("a_conv", True, False), ("a_w_out", False, True),
    ("kv_w_down", False, True), ("kv_w_uk", True, True), ("kv_w_uv", True, True), ("b_w_in", True, True), ("b_w_uq", True, True),
    ("b_w_out", False, True))
_REPLICATED = ("a_log", "a_dt_bias", "a_o_gain", "kv_norm", "kv_latent_norm", "k_gain", "b_norm", "b_q_latent_norm", "b_q_gain")
_ALL_WEIGHTS = ("meta_tokens", "a_norm", "a_w_in", "a_conv", "a_log", "a_dt_bias", "a_o_gain", "a_w_out", "kv_norm", "kv_w_down",
                "kv_latent_norm", "kv_w_uk", "kv_w_uv", "k_gain", "b_norm", "b_w_in", "b_q_latent_norm", "b_w_uq", "b_q_gain", "b_w_out")


def _round_up(n, m):
    return (n + m - 1) // m * m


def _pack_rows(pieces, row_multiple):
    padded = []
    for p in pieces:
        n = p.shape[-1]
        padded.append(jnp.pad(p, [(0, 0)] * (p.ndim - 1) + [(0, _round_up(n, PACK_COLS) - n)]))
    flat = jnp.concatenate(padded, -1)
    rows = _round_up(flat.shape[-1] // PACK_COLS, row_multiple)
    flat = jnp.pad(flat, [(0, 0)] * (flat.ndim - 1) + [(0, rows * PACK_COLS - flat.shape[-1])])
    return flat.reshape(flat.shape[:-1] + (rows, PACK_COLS))


def _unpack_rows(buf, sizes):
    flat = buf.reshape(buf.shape[:-2] + (-1,))
    out, off = [], 0
    for n in sizes:
        out.append(flat[..., off:off + n])
        off += _round_up(n, PACK_COLS)
    return out


def _shard_2d(a):
    return a.reshape(a.shape[-2:]) if a.ndim > 2 else a


def _kl_shard(a, by_cols):
    return _shard_2d(a).T if by_cols else _shard_2d(a)


_GROUPS_FIRST = (("a_w_in",),)
_GROUPS_LATER = (("a_w_out", "b_w_in", "b_w_out"), ("b_w_uq",), ("kv_w_down",), ("kv_w_uk", "kv_w_uv"))
_SMALL_SHARDED = ("meta_tokens", "a_norm", "a_conv")
_BY_COLS = {name: by_cols for name, by_cols, _ in _SHARDED}
ROW_ALIGN = 16


def _stack_rows(pieces):
    padded, starts, row = [], [], 0
    for p in pieces:
        r = p.shape[-2]
        padded.append(jnp.pad(p, [(0, 0)] * (p.ndim - 2) + [(0, _round_up(r, ROW_ALIGN) - r), (0, 0)]))
        starts.append(row)
        row += _round_up(r, ROW_ALIGN)
    return jnp.concatenate(padded, -2), starts


def _stack_group(arrays_by_name, names):
    arrays = [arrays_by_name[n].astype(BF16) for n in names]
    buf, starts = _stack_rows(arrays)
    return buf, [(n, s, a.shape[-2]) for n, s, a in zip(names, starts, arrays, strict=True)]


def _stack_groups(arrays_by_name, groups):
    stacked = [_stack_group(arrays_by_name, names) for names in groups]
    return [b for b, _ in stacked], [entries for _, entries in stacked]


def _full_from_gathered(gathered, layout):
    full = {}
    for got, entries in zip(gathered, layout, strict=True):
        for name, start, rows in entries:
            full[name] = got[:, start:start + rows].reshape(N_DEV * rows, got.shape[-1])
    return full


def gather_small_weights(local):
    small = [_kl_shard(local[n], _BY_COLS[n]) for n in _SMALL_SHARDED]
    (gathered,) = _exchange("all_gather", [_pack_rows([s.reshape(-1) for s in small], 8)], scatter=False)
    full = {}
    for name, part, sh in zip(_SMALL_SHARDED, _unpack_rows(gathered, [s.size for s in small]), small, strict=True):
        full[name] = part.reshape(N_DEV * sh.shape[0], sh.shape[1])
    full["a_norm"] = full["a_norm"].reshape(1, -1)
    return full


class LaterExchanges:
    def __init__(self, local):
        shards = {n: _kl_shard(local[n], _BY_COLS[n]) for names in _GROUPS_FIRST + _GROUPS_LATER for n in names}
        self.first_gather_bufs, self.first_layout = _stack_groups(shards, _GROUPS_FIRST)
        self.gather_bufs, self.layout = _stack_groups(shards, _GROUPS_LATER)

    def finish_first(self, gathered):
        return _full_from_gathered(gathered, self.first_layout)

    def finish(self, gathered):
        return _full_from_gathered(gathered, self.layout)

    def scatter_bufs(self, grads):
        return _stack_groups(_owner_slices(grads, _GROUPS_LATER), _GROUPS_LATER)[0]

    def last_scatter_bufs(self, grads):
        bufs, self.last_layout = _stack_groups(_owner_slices(grads, _GROUPS_FIRST), _GROUPS_FIRST)
        return bufs


def _owner_slices(grads, groups):
    return {n: grads[n].reshape(N_DEV, -1, grads[n].shape[-1]) for names in groups for n in names}


def reduce_contributions(name, recv):
    _, r, c = recv.shape
    tr = _pick(r, (256, 128, 64, 32, 16, 8))

    def body(g_ref, o_ref):
        g = g_ref[0].astype(F32)
        for dev in range(1, N_DEV):
            g = g + g_ref[dev].astype(F32)
        o_ref[...] = g

    return pl.pallas_call(
        body, grid=(r // tr,), in_specs=[pl.BlockSpec((N_DEV, tr, c), lambda i: (0, i, 0))], out_specs=pl.BlockSpec((tr, c), lambda i: (i, 0)),
        out_shape=jax.ShapeDtypeStruct((r, c), F32), compiler_params=_cparams(("arbitrary",)), name=name)(recv)


def adamw_all(gs, ws, ms, vs):
    n = len(gs)

    def body(*refs):
        for i in range(n):
            g_ref, w_ref, m_ref, v_ref = (refs[j * n + i] for j in range(4))
            d_ref, mo_ref, vo_ref = (refs[(4 + j) * n + i] for j in range(3))
            g = g_ref[...]
            m_new = ADAM_B1 * m_ref[...] + (1.0 - ADAM_B1) * g
            v_new = ADAM_B2 * v_ref[...] + (1.0 - ADAM_B2) * (g * g)
            m_hat = m_new / (1.0 - ADAM_B1 ** ADAM_STEP)
            v_hat = v_new / (1.0 - ADAM_B2 ** ADAM_STEP)
            d_ref[...] = -ADAM_LR * (m_hat / (jnp.sqrt(v_hat) + ADAM_EPS) + ADAM_WD * w_ref[...])
            mo_ref[...] = m_new
            vo_ref[...] = v_new

    out = [jax.ShapeDtypeStruct(g.shape, F32) for g in gs] * 3
    res = pl.pallas_call(body, out_shape=out, compiler_params=pltpu.CompilerParams(vmem_limit_bytes=VMEM_LIMIT), name="adamw_all")(*gs, *ws, *ms, *vs)
    return res[:n], res[n:2 * n], res[2 * n:]


def kernel(x, meta_tokens, a_norm, a_w_in, a_conv, a_log, a_dt_bias, a_o_gain, a_w_out, kv_norm, kv_w_down, kv_latent_norm, kv_w_uk, kv_w_uv, k_gain, b_norm, b_w_in, b_q_latent_norm, b_w_uq, b_q_gain, b_w_out, loss_target, m_meta_tokens, m_a_norm, m_a_w_in, m_a_conv, m_a_log, m_a_dt_bias, m_a_o_gain, m_a_w_out, m_kv_norm, m_kv_w_down, m_kv_latent_norm, m_kv_w_uk, m_kv_w_uv, m_k_gain, m_b_norm, m_b_w_in, m_b_q_latent_norm, m_b_w_uq, m_b_q_gain, m_b_w_out, v_meta_tokens, v_a_norm, v_a_w_in, v_a_conv, v_a_log, v_a_dt_bias, v_a_o_gain, v_a_w_out, v_kv_norm, v_kv_w_down, v_kv_latent_norm, v_kv_w_uk, v_kv_w_uv, v_k_gain, v_b_norm, v_b_w_in, v_b_q_latent_norm, v_b_w_uq, v_b_q_gain, v_b_w_out):
    given = dict(locals())
    local_w = {n: given[n] for n in _ALL_WEIGHTS}
    full = gather_small_weights(local_w)
    for n in _REPLICATED:
        full[n] = local_w[n]
    later = LaterExchanges(local_w)

    loss_part, grad_x, grads, received_riding = local_step(x, loss_target, full, later)

    exact = [grads[n].reshape(N_DEV, -1) for n in _SMALL_SHARDED]
    exact += [jnp.broadcast_to(grads[n].reshape(1, -1), (N_DEV, grads[n].size)) for n in _REPLICATED]
    exact.append(jnp.broadcast_to(loss_part, (N_DEV, 1)))
    received = list(received_riding) + list(_exchange("all_to_all", [_pack_rows(exact, 8)], scatter=True))
    layout = later.layout + later.last_layout
    summed = [reduce_contributions(f"reduce_{i}", r) for i, r in enumerate(received)]

    grad_kl = {}
    for got, entries in zip(summed, layout):
        for n, start, rows in entries:
            grad_kl[n] = got[start:start + rows]
    parts = _unpack_rows(summed[-1], [p.shape[1] for p in exact])
    for n, part in zip(_SMALL_SHARDED + _REPLICATED, parts, strict=False):
        grad_kl[n] = part
    loss = parts[-1][0]

    def natural_2d(n, a):
        shape = _shard_2d(local_w[n]).shape if local_w[n].ndim > 1 else (1, local_w[n].size)
        return a.reshape(shape[::-1]).T if _BY_COLS.get(n, False) else a.reshape(shape)

    as_2d = lambda n, a: a.reshape(natural_2d(n, grad_kl[n]).shape)
    gs = [natural_2d(n, grad_kl[n]) for n in _ALL_WEIGHTS]
    deltas, new_m, new_v = adamw_all(gs, [as_2d(n, local_w[n]) for n in _ALL_WEIGHTS], [as_2d(n, given["m_" + n]) for n in _ALL_WEIGHTS],
                                     [as_2d(n, given["v_" + n]) for n in _ALL_WEIGHTS])
    results = [a.reshape(local_w[n].shape) for group in (gs, deltas, new_m, new_v) for n, a in zip(_ALL_WEIGHTS, group, strict=True)]
    return (loss, grad_x, *results)
```

```python
import dataclasses
import functools
import math

import jax
import jax.numpy as jnp
from jax import lax
from jax.experimental import pallas as pl
from jax.experimental.pallas import tpu as pltpu

F32 = jnp.float32
BF16 = jnp.bfloat16
_MXU_DTYPE = jnp.bfloat16

N_DEV = 8
D_MODEL = 1024
N_HEADS = 8
HEAD = 128
CHUNK = 64
N_META = 16
PAD_ROWS = 2 * CHUNK - N_META
LEAD = PAD_ROWS + N_META
ROPE = 64
QK_DIM = HEAD + ROPE
QK_PAD = 2 * HEAD
KV_RANK = 256
Q_RANK = 384
CONV_K = 4
EPS = 1e-6
NEG = -1e30
ROPE_THETA = 10000.0
ADAM_LR, ADAM_B1, ADAM_B2, ADAM_EPS, ADAM_WD, ADAM_STEP = 0.001, 0.9, 0.999, 1e-08, 0.01, 10
PACK_COLS = 512
VMEM_LIMIT = 56 * 1024 * 1024


def _pick(n, options):
    for o in options:
        if n % o == 0:
            return o
    raise ValueError(f"no tile for {n} among {options}")


def _cparams(sem):
    return pltpu.CompilerParams(dimension_semantics=sem, vmem_limit_bytes=VMEM_LIMIT)


def _dims(a, dims):
    if a.ndim == 2:
        return (dims, ((), ()))
    (ca,), (cb,) = dims
    return (((ca + 1,), (cb + 1,)), ((0,), (0,)))


def _dot(a, b, dims):
    return lax.dot_general(a.astype(_MXU_DTYPE), b.astype(_MXU_DTYPE), _dims(a, dims), preferred_element_type=F32)


@jax.custom_vjp
def mm_nn(a, b):
    return _dot(a, b, ((1,), (0,)))


@jax.custom_vjp
def mm_nt(a, b):
    return _dot(a, b, ((1,), (1,)))


@jax.custom_vjp
def mm_tn(a, b):
    return _dot(a, b, ((0,), (0,)))


mm_nn.defvjp(lambda a, b: (mm_nn(a, b), (a, b)), lambda r, g: (mm_nt(g, r[1]), mm_tn(r[0], g)))
mm_nt.defvjp(lambda a, b: (mm_nt(a, b), (a, b)), lambda r, g: (mm_nn(g, r[1]), mm_tn(g, r[0])))
mm_tn.defvjp(lambda a, b: (mm_tn(a, b), (a, b)), lambda r, g: (mm_nt(r[1], g), mm_nn(r[0], g)))


def _split_terms(x, n):
    terms, rest = [], x
    for _ in range(n):
        t = rest.astype(_MXU_DTYPE)
        terms.append(t)
        rest = rest - t.astype(F32)
    return terms


def _dot_01_raw(m, x, dims):
    m = m.astype(_MXU_DTYPE)
    return sum(lax.dot_general(m, t, _dims(m, dims), preferred_element_type=F32) for t in _split_terms(x, 3))


@jax.custom_vjp
def _dot_01(m, x):
    return _dot_01_raw(m, x, ((1,), (0,)))


_dot_01.defvjp(lambda m, x: (_dot_01(m, x), m), lambda m, g: (jnp.zeros_like(m), _dot_01_raw(m, g, ((0,), (0,)))))


def _inv_unit_lower(a):
    n = a.shape[-1]
    eye = (lax.broadcasted_iota(jnp.int32, (n, n), 0) == lax.broadcasted_iota(jnp.int32, (n, n), 1)).astype(F32)
    d = lambda u, w: lax.dot_general(u, w, _dims(u, ((1,), (0,))), preferred_element_type=F32)
    t = eye - a
    p = a.astype(_MXU_DTYPE)
    p = d(p, p)
    squarings = int(math.log2(n)) - 1
    for s in range(squarings):
        ph = p.astype(_MXU_DTYPE)
        t_hi, t_lo = _split_terms(t, 2)
        t = t + (d(t_hi, ph) + d(t_lo, ph))
        if s + 1 < squarings:
            p = d(ph, ph)
    return t


@jax.custom_vjp
def _inv_lookup(a, t):
    return t


def _inv_lookup_bwd(t, g):
    return -mm_tn(t, mm_nt(g, t)), jnp.zeros_like(t)


_inv_lookup.defvjp(lambda a, t: (t, t), _inv_lookup_bwd)


def _sigmoid(x):
    return 1.0 / (1.0 + jnp.exp(-x))


@jax.custom_vjp
def _silu(x):
    return x * _sigmoid(x)


def _silu_fwd(x):
    s = _sigmoid(x)
    return x * s, (x, s)


_silu.defvjp(_silu_fwd, lambda r, g: (g * (r[1] * (1.0 + r[0] * (1.0 - r[1]))),))


def _softplus(x):
    return jnp.where(x > 20.0, x, jnp.log(1.0 + jnp.exp(jnp.minimum(x, 20.0))))


def _rms(x, g, width=None):
    ms = jnp.sum(x * x, -1, keepdims=True) / (x.shape[-1] if width is None else width)
    return x * lax.rsqrt(ms + EPS) * g


MM_VMEM_BUDGET = 40 * 1024 * 1024


def _matmul_rows(name, a, b, mode, out_dtype, res, scatter):
    m, k = a.shape
    n = b.shape[1] if mode == "nn" else b.shape[0]
    dims = {"nn": ((1,), (0,)), "nt": ((1,), (1,))}[mode]
    out_bytes = jnp.dtype(out_dtype).itemsize
    n_in, nx = 2 + (res is not None), len(scatter)

    def vmem(tm):
        blocks = 2 * tm * k * a.dtype.itemsize + 2 * k * n * b.dtype.itemsize + 2 * tm * n * out_bytes + tm * n * 4
        return blocks + (2 * tm * n * res.dtype.itemsize if res is not None else 0)

    tm = next(c for c in (2176, 1088, 512, 256, 128, 64) if m % c == 0 and vmem(c) <= MM_VMEM_BUDGET)
    steps = m // tm

    def body(*refs):
        a_ref, b_ref, o_ref = refs[0], refs[1], refs[n_in + nx]
        i = pl.program_id(0)
        finish = _ride(scatter, True, refs[n_in:n_in + nx], refs[n_in + nx + 1:n_in + 2 * nx + 1], refs[n_in + 2 * nx + 1:], i == 0, i == steps - 1)
        out = _dot(a_ref[...], b_ref[...], dims)
        if res is not None:
            out = out + refs[2][...].astype(F32)
        o_ref[...] = out.astype(o_ref.dtype)
        finish()

    o_spec = pl.BlockSpec((tm, n), lambda i: (i, 0))
    in_specs = [pl.BlockSpec((tm, k), lambda i: (i, 0)), pl.BlockSpec(b.shape, lambda i: (0, 0))] + ([o_spec] if res is not None else [])
    args = (a, b) + ((res,) if res is not None else ())
    out = pl.pallas_call(
        body, grid=(steps,), in_specs=in_specs + [_HBM] * nx, out_specs=[o_spec] + [_HBM] * nx,
        out_shape=[jax.ShapeDtypeStruct((m, n), out_dtype)] + Exchange.out_shape(scatter, True), scratch_shapes=Exchange.scratch(nx) if nx else [],
        compiler_params=_cparams(("arbitrary",) if nx else ("parallel",)), name=name)(*args, *scatter)
    return out if nx else out[0]


def matmul(name, a, b, mode, out_dtype=F32, res=None, scatter=()):
    if mode != "tn":
        return _matmul_rows(name, a, b, mode, out_dtype, res, scatter)
    (k, m), (k2, n) = a.shape, b.shape
    assert k == k2 and res is None, (name, a.shape, b.shape, mode)
    tm = _pick(m, (m if m <= 1536 else 1024, 1024, 512, 384, 256, 128))
    tn = _pick(n, (1024, 512, 384, 256, 128))
    tk = _pick(k, (512, 256, 128))
    nk = k // tk
    dims = ((0,), (0,))

    def body(*refs):
        if res is None:
            a_ref, b_ref, o_ref, acc_ref = refs
        else:
            a_ref, b_ref, r_ref, o_ref, acc_ref = refs
        kk = pl.program_id(2)

        @pl.when(kk == 0)
        def _():
            acc_ref[...] = jnp.zeros_like(acc_ref)

        acc_ref[...] += _dot(a_ref[...], b_ref[...], dims)

        @pl.when(kk == nk - 1)
        def _():
            out = acc_ref[...]
            if res is not None:
                out = out + r_ref[...].astype(F32)
            o_ref[...] = out.astype(o_ref.dtype)

    a_spec = pl.BlockSpec((tk, tm), lambda i, j, kk: (kk, i)) if mode == "tn" else pl.BlockSpec((tm, tk), lambda i, j, kk: (i, kk))
    b_spec = pl.BlockSpec((tn, tk), lambda i, j, kk: (j, kk)) if mode == "nt" else pl.BlockSpec((tk, tn), lambda i, j, kk: (kk, j))
    o_spec = pl.BlockSpec((tm, tn), lambda i, j, kk: (i, j))
    in_specs = [a_spec, b_spec] + ([o_spec] if res is not None else [])
    args = (a, b) + ((res,) if res is not None else ())
    return pl.pallas_call(
        body, grid=(m // tm, n // tn, nk), in_specs=in_specs, out_specs=o_spec,
        out_shape=jax.ShapeDtypeStruct((m, n), out_dtype), scratch_shapes=[pltpu.VMEM((tm, tn), F32)],
        compiler_params=_cparams(("parallel", "parallel", "arbitrary")), name=name)(*args)


@dataclasses.dataclass
class Arg:
    arr: jax.Array
    kind: str = "row"
    bc: int = 0
    base: int = 0
    ph: bool = False
    diff: bool = False
    gdt: object = F32


def _arg_spec(a, tr, nh, ntab, base=None):
    bc = a.bc or a.arr.shape[1]
    base = a.base if base is None else base
    width = bc * nh if a.ph else bc
    col = base // nh if a.ph else base
    assert not a.ph or base % nh == 0
    if a.kind == "row":
        return pl.BlockSpec((tr, width), lambda i: (i, col))
    if a.kind == "tab":
        return pl.BlockSpec((tr, width), lambda i: (i % ntab, col))
    return pl.BlockSpec((a.arr.shape[0], width), lambda i: (0, col))


def _head_view(ref, a, h, rs):
    bc = a.bc or a.arr.shape[1]
    rows = slice(None) if a.kind == "par" else rs
    v = ref[rows, h * bc:(h + 1) * bc] if a.ph else ref[rows, :]
    return v.astype(F32) if jnp.issubdtype(v.dtype, jnp.floating) else v


def row_call(name, fn, args, outs, tr, nh=1, ntab=1):
    t = args[0].arr.shape[0]
    n_in = len(args)
    out_args = [Arg(None, "row", bc, 0, ph) for (_, _, bc, ph) in outs]
    assert all(a.ph or nh == 1 for a in out_args)
    rs = slice(None)

    def body(*refs):
        for h in range(nh):
            res = fn(*[_head_view(r, a, h, rs) for r, a in zip(refs[:n_in], args, strict=True)])
            for r, a, v in zip(refs[n_in:], out_args, res, strict=True):
                r[rs, h * a.bc:(h + 1) * a.bc] = v.astype(r.dtype)

    return pl.pallas_call(
        body, grid=(t // tr,), in_specs=[_arg_spec(a, tr, nh, ntab) for a in args], out_specs=[_arg_spec(a, tr, nh, ntab) for a in out_args],
        out_shape=[jax.ShapeDtypeStruct((t, cols), dt) for (cols, dt, _, _) in outs],
        compiler_params=_cparams(("arbitrary",)), name=name)(*[a.arr for a in args])


def row_vjp_call(name, fn, args, cts, tr, nh=1, ntab=1):
    t = args[0].arr.shape[0]
    n_in, n_ct = len(args), len(cts)
    diff_idx = [k for k, a in enumerate(args) if a.diff]
    def body(*refs):
        out_refs = refs[n_in + n_ct:]
        par_sum = {}
        for k, r in zip(diff_idx, out_refs, strict=True):
            if args[k].kind == "par":
                @pl.when(pl.program_id(0) == 0)
                def _(r=r):
                    r[...] = jnp.zeros_like(r)

        for rs in (slice(None),):
            row_sum = {}
            for h in range(nh):
                vals = [_head_view(r, a, h, rs) for r, a in zip(refs[:n_in], args, strict=True)]
                ct_vals = tuple(_head_view(r, a, h, rs) for r, a in zip(refs[n_in:n_in + n_ct], cts, strict=True))

                def f(*dv, vals=vals):
                    full = list(vals)
                    for k, v in zip(diff_idx, dv, strict=True):
                        full[k] = v
                    return tuple(fn(*full))

                _, vjp = jax.vjp(f, *[vals[k] for k in diff_idx])
                for j, (k, r, g) in enumerate(zip(diff_idx, out_refs, vjp(ct_vals), strict=True)):
                    a = args[k]
                    bc = a.bc or a.arr.shape[1]
                    if a.kind == "row" and a.ph:
                        r[rs, h * bc:(h + 1) * bc] = g.astype(r.dtype)
                    elif a.kind == "row":
                        row_sum[j] = g if j not in row_sum else row_sum[j] + g
                    else:
                        key = (j, h if a.ph else 0)
                        par_sum[key] = g if key not in par_sum else par_sum[key] + g
            for j, g in row_sum.items():
                out_refs[j][rs, :] = g.astype(out_refs[j].dtype)
        for (j, h), g in par_sum.items():
            bc = g.shape[1]
            out_refs[j][:, h * bc:(h + 1) * bc] += g

    out_specs, out_shape = [], []
    for k in diff_idx:
        a = args[k]
        bc = a.bc or a.arr.shape[1]
        out_specs.append(_arg_spec(a, tr, nh, ntab, base=0))
        out_shape.append(jax.ShapeDtypeStruct((t if a.kind == "row" else a.arr.shape[0], bc * (nh if a.ph else 1)), a.gdt if a.kind == "row" else F32))
    in_specs = [_arg_spec(a, tr, nh, ntab) for a in list(args) + list(cts)]
    return pl.pallas_call(
        body, grid=(t // tr,), in_specs=in_specs, out_specs=out_specs, out_shape=out_shape,
        compiler_params=_cparams(("arbitrary",)), name=name)(*[a.arr for a in list(args) + list(cts)])


def _conv_taps(x, w):
    rows = lax.broadcasted_iota(jnp.int32, x.shape, 0)
    y = x * w[CONV_K - 1:CONV_K, :]
    for s in range(1, CONV_K):
        y = y + jnp.where(rows >= s, pltpu.roll(x, s, 0), 0.0) * w[CONV_K - 1 - s:CONV_K - s, :]
    return y


CONV_HEADS = 4
CONV_BLOCKS_PER_THIRD = N_HEADS // CONV_HEADS


def _conv_post(y, block):
    a = _silu(y)
    normed = block < 2 * CONV_BLOCKS_PER_THIRD
    scale = jnp.where(block < CONV_BLOCKS_PER_THIRD, HEAD ** -0.5, 1.0)
    return a * jnp.where(normed, lax.rsqrt(jnp.sum(a * a, -1, keepdims=True) + EPS) * scale, 1.0)


def conv_fwd(z, w, lp):
    t, width = z.shape
    cols = CONV_HEADS * HEAD

    def body(z_ref, w_ref, o_ref, y_ref):
        block = pl.program_id(1)
        for h in range(CONV_HEADS):
            cs = slice(h * HEAD, (h + 1) * HEAD)
            y = _conv_taps(z_ref[:, cs], w_ref[:, cs])
            y_ref[:, cs] = y
            o_ref[:, cs] = _conv_post(y, block)

    blk = pl.BlockSpec((lp, cols), lambda b, j: (b, j))
    out = jax.ShapeDtypeStruct((t, width), F32)
    return pl.pallas_call(
        body, grid=(t // lp, width // cols), in_specs=[blk, pl.BlockSpec((CONV_K, cols), lambda b, j: (0, j))],
        out_specs=[blk, blk], out_shape=[out, out], compiler_params=_cparams(("arbitrary", "arbitrary")), name="a_conv_fwd")(z, w)


def conv_bwd(z, y, w, dout, lp):
    t, width = z.shape
    cols = CONV_HEADS * HEAD

    def body(z_ref, y_ref, w_ref, g_ref, dz_ref, dw_ref):
        block = pl.program_id(0)

        @pl.when(pl.program_id(1) == 0)
        def _():
            dw_ref[...] = jnp.zeros_like(dw_ref)

        for h in range(CONV_HEADS):
            cs = slice(h * HEAD, (h + 1) * HEAD)
            x, wv = z_ref[:, cs], w_ref[:, cs]
            _, vjp = jax.vjp(lambda y_: _conv_post(y_, block), y_ref[:, cs])
            (dy,) = vjp(g_ref[:, cs])
            rows = lax.broadcasted_iota(jnp.int32, x.shape, 0)
            dx = dy * wv[CONV_K - 1:CONV_K, :]
            dw_ref[CONV_K - 1:CONV_K, cs] += jnp.sum(dy * x, axis=0, keepdims=True)
            for s in range(1, CONV_K):
                dy_up = jnp.where(rows < lp - s, pltpu.roll(dy, lp - s, 0), 0.0)
                dx = dx + dy_up * wv[CONV_K - 1 - s:CONV_K - s, :]
                dw_ref[CONV_K - 1 - s:CONV_K - s, cs] += jnp.sum(dy_up * x, axis=0, keepdims=True)
            dz_ref[:, cs] = dx.astype(dz_ref.dtype)

    blk = pl.BlockSpec((lp, cols), lambda j, b: (b, j))
    w_blk = pl.BlockSpec((CONV_K, cols), lambda j, b: (0, j))
    return pl.pallas_call(
        body, grid=(width // cols, t // lp), in_specs=[blk, blk, w_blk, blk], out_specs=[blk, w_blk],
        out_shape=[jax.ShapeDtypeStruct((t, width), _MXU_DTYPE), jax.ShapeDtypeStruct((CONV_K, width), F32)],
        compiler_params=_cparams(("arbitrary", "arbitrary")), name="a_conv_bwd")(z, y, w, dout)


def _delta_chunk(q, k, v, ba, alog, dtb, state, t_stored):
    n_g, c = q.shape[0], q.shape[1]
    lane = lax.broadcasted_iota(jnp.int32, (1, HEAD), 1)

    def pick(xs, offset):
        cols = [jnp.sum(xs[i // N_HEADS if len(xs) > 1 else 0] * (lane == offset + i % N_HEADS).astype(F32), axis=1, keepdims=True)[None]
                for i in range(n_g)]
        return jnp.concatenate(cols, 0)

    b_raw, a_raw = pick(ba, 0), pick(ba, N_HEADS)
    a_log, dt_bias = pick((alog,), 0), pick((dtb,), 0)
    beta = _sigmoid(b_raw)
    g = -jnp.exp(a_log) * _softplus(a_raw + dt_bias)
    ri = lax.broadcasted_iota(jnp.int32, (c, c), 0)
    ci = lax.broadcasted_iota(jnp.int32, (c, c), 1)
    tril = ci <= ri
    lower = jnp.broadcast_to(tril.astype(F32), (n_g, c, c))
    gc_col = _dot_01(lower, g * jnp.ones((1, 1, HEAD), F32))[:, :, :1]
    gc_row = _dot_01(jnp.ones((n_g, 8, c), F32), g * (ri <= ci).astype(F32)[None])[:, 0:1, :]
    gc_last = jnp.sum(g, axis=1, keepdims=True)
    decay = jnp.exp(jnp.where(tril, gc_col - gc_row, NEG))
    e_gc = jnp.exp(gc_col)
    kb = k * beta
    a_mat = jnp.where(ci < ri, mm_nt(kb, k) * decay, 0.0)
    t_inv = _inv_unit_lower(a_mat) if t_stored is None else _inv_lookup(a_mat, t_stored)
    u_base = mm_nn(t_inv, v * beta)
    w_dec = mm_nn(t_inv, kb * e_gc)
    attn = jnp.where(tril, mm_nt(q, k) * decay, 0.0)
    u = u_base - mm_nn(w_dec, state)
    o = mm_nn(q * e_gc, state) + mm_nn(attn, u)
    new_state = state * jnp.exp(gc_last) + mm_tn(k * jnp.exp(gc_last - gc_col), u)
    return o, new_state, t_inv


DELTA_STEP_FWD = (4, 2)
DELTA_STEP_BWD = (2, 2)


def _heads_of(ref, rs, first_col):
    return jnp.stack([ref[i // N_HEADS, rs, first_col + (i % N_HEADS) * HEAD:first_col + (i % N_HEADS + 1) * HEAD]
                      for i in range(ref.shape[0] * N_HEADS)])


def _qkv_heads(ref, rs, part):
    return _heads_of(ref, rs, part * N_HEADS * HEAD)


def _by_sequence(a, lp):
    return a.reshape(a.shape[0] // lp, lp, a.shape[1])


def _ride(bufs, scatter, refs_in, refs_out, sems, first, last, two_level=False):
    if not bufs:
        return lambda: None
    make = lambda: (TwoLevelGather if two_level else Exchange)(refs_in, refs_out, *sems, scatter)

    @pl.when(first)
    def _():
        make().start()

    def finish():
        @pl.when(last)
        def _():
            make().wait()

    return finish


def delta_fwd(qkv, ba, ba_block, alog, dtb, lp, gather=()):
    t = qkv.shape[0]
    nb, nc = t // lp, lp // CHUNK
    seqs, cps = DELTA_STEP_FWD
    ng, rows = nc // cps, cps * CHUNK
    nx = len(gather)
    nbg = nb // seqs
    assert nc % cps == 0 and nb % seqs == 0

    def body(*refs):
        qkv_ref, ba_ref, al_ref, dt_ref = refs[:4]
        o_ref, s_ref, t_ref = refs[4 + nx:7 + nx]
        state_ref = refs[7 + 2 * nx]
        b, n = pl.program_id(0), pl.program_id(1)
        finish = _ride(gather, False, refs[4:4 + nx], refs[7 + nx:7 + 2 * nx], refs[8 + 2 * nx:], (b == 0) & (n == 0), (b == nbg - 1) & (n == ng - 1))

        @pl.when(n == 0)
        def _():
            state_ref[...] = jnp.zeros_like(state_ref)

        al, dtv = al_ref[...], dt_ref[...]
        for c in range(cps):
            rs = slice(c * CHUNK, (c + 1) * CHUNK)
            state = state_ref[...]
            o, new_state, t_inv = _delta_chunk(_qkv_heads(qkv_ref, rs, 0), _qkv_heads(qkv_ref, rs, 1), _qkv_heads(qkv_ref, rs, 2),
                                               tuple(ba_ref[i, rs, :] for i in range(seqs)), al, dtv, state, None)
            for i in range((seqs * N_HEADS)):
                seq, g = divmod(i, N_HEADS)
                o_ref[seq, rs, g * HEAD:(g + 1) * HEAD] = o[i]
                s_ref[seq, g, c] = state[i]
                t_ref[seq, g, c] = t_inv[i]
            state_ref[...] = new_state
        finish()

    rows_of = lambda width: pl.BlockSpec((seqs, rows, width), lambda b, n: (b, n, 0))
    par_spec = pl.BlockSpec((1, HEAD), lambda b, n: (0, 0))
    out = pl.pallas_call(
        body, grid=(nbg, ng),
        in_specs=[rows_of(3 * N_HEADS * HEAD), pl.BlockSpec((seqs, rows, HEAD), lambda b, n: (b, n, ba_block)), par_spec, par_spec] + [_HBM] * nx,
        out_specs=[rows_of(N_HEADS * HEAD), pl.BlockSpec((seqs, N_HEADS, cps, HEAD, HEAD), lambda b, n: (b, 0, n, 0, 0)),
                   pl.BlockSpec((seqs, N_HEADS, cps, CHUNK, CHUNK), lambda b, n: (b, 0, n, 0, 0))] + [_HBM] * nx,
        out_shape=[jax.ShapeDtypeStruct((nb, lp, N_HEADS * HEAD), F32), jax.ShapeDtypeStruct((nb, N_HEADS, nc, HEAD, HEAD), F32),
                   jax.ShapeDtypeStruct((nb, N_HEADS, nc, CHUNK, CHUNK), F32)] + Exchange.out_shape(gather, False),
        scratch_shapes=[pltpu.VMEM(((seqs * N_HEADS), HEAD, HEAD), F32)] + (Exchange.scratch(nx) if nx else []),
        compiler_params=_cparams(("arbitrary", "arbitrary")), name="delta_fwd")(_by_sequence(qkv, lp), _by_sequence(ba, lp), alog, dtb, *gather)
    return [out[0].reshape(t, N_HEADS * HEAD)] + list(out[1:])


def delta_bwd(qkv, ba, ba_block, alog, dtb, states, t_invs, do, lp, scatter=()):
    t = qkv.shape[0]
    nb, nc = t // lp, lp // CHUNK
    seqs, cps = DELTA_STEP_BWD
    ng, rows = nc // cps, cps * CHUNK
    nx = len(scatter)
    nbg = nb // seqs

    def body(*refs):
        qkv_ref, ba_ref, al_ref, dt_ref, s_ref, t_ref, do_ref = refs[:7]
        dqkv_ref, dba_ref, dal_ref, ddt_ref = refs[7 + nx:11 + nx]
        dstate_ref = refs[11 + 2 * nx]
        b, step = pl.program_id(0), pl.program_id(1)
        finish = _ride(scatter, True, refs[7:7 + nx], refs[11 + nx:11 + 2 * nx], refs[12 + 2 * nx:], (b == 0) & (step == 0),
                       (b == nbg - 1) & (step == ng - 1))

        @pl.when(step == 0)
        def _():
            dstate_ref[...] = jnp.zeros_like(dstate_ref)

        @pl.when((b == 0) & (step == 0))
        def _():
            dal_ref[...] = jnp.zeros_like(dal_ref)
            ddt_ref[...] = jnp.zeros_like(ddt_ref)

        al, dtv = al_ref[...], dt_ref[...]
        d_al = jnp.zeros((1, HEAD), F32)
        d_dt = jnp.zeros((1, HEAD), F32)
        for c in reversed(range(cps)):
            rs = slice(c * CHUNK, (c + 1) * CHUNK)
            t_n = jnp.stack([t_ref[i // N_HEADS, i % N_HEADS, c] for i in range((seqs * N_HEADS))])
            s_n = jnp.stack([s_ref[i // N_HEADS, i % N_HEADS, c] for i in range((seqs * N_HEADS))])

            def f(q_, k_, v_, ba_, al_, dt_, s_, t_n=t_n):
                return _delta_chunk(q_, k_, v_, ba_, al_, dt_, s_, t_n)[:2]

            _, vjp = jax.vjp(f, _qkv_heads(qkv_ref, rs, 0), _qkv_heads(qkv_ref, rs, 1), _qkv_heads(qkv_ref, rs, 2), tuple(ba_ref[i, rs, :] for i in range(seqs)), al, dtv, s_n)
            grads = vjp((_heads_of(do_ref, rs, 0), dstate_ref[...]))
            for part in range(3):
                for i in range((seqs * N_HEADS)):
                    col = (part * N_HEADS + i % N_HEADS) * HEAD
                    dqkv_ref[i // N_HEADS, rs, col:col + HEAD] = grads[part][i]
            for i in range(seqs):
                dba_ref[i, rs, :] = grads[3][i]
            d_al, d_dt = d_al + grads[4], d_dt + grads[5]
            dstate_ref[...] = grads[6]
        dal_ref[...] += d_al
        ddt_ref[...] += d_dt
        finish()

    rows_of = lambda width: pl.BlockSpec((seqs, rows, width), lambda b, n: (b, ng - 1 - n, 0))
    par_spec = pl.BlockSpec((1, HEAD), lambda b, n: (0, 0))
    out = pl.pallas_call(
        body, grid=(nbg, ng),
        in_specs=[rows_of(3 * N_HEADS * HEAD), pl.BlockSpec((seqs, rows, HEAD), lambda b, n: (b, ng - 1 - n, ba_block)), par_spec, par_spec,
                  pl.BlockSpec((seqs, N_HEADS, cps, HEAD, HEAD), lambda b, n: (b, 0, ng - 1 - n, 0, 0)),
                  pl.BlockSpec((seqs, N_HEADS, cps, CHUNK, CHUNK), lambda b, n: (b, 0, ng - 1 - n, 0, 0)), rows_of(N_HEADS * HEAD)] + [_HBM] * nx,
        out_specs=[rows_of(3 * N_HEADS * HEAD), rows_of(HEAD), par_spec, par_spec] + [_HBM] * nx,
        out_shape=[jax.ShapeDtypeStruct((nb, lp, 3 * N_HEADS * HEAD), F32), jax.ShapeDtypeStruct((nb, lp, HEAD), F32),
                   jax.ShapeDtypeStruct((1, HEAD), F32), jax.ShapeDtypeStruct((1, HEAD), F32)] + Exchange.out_shape(scatter, True),
        scratch_shapes=[pltpu.VMEM(((seqs * N_HEADS), HEAD, HEAD), F32)] + (Exchange.scratch(nx) if nx else []),
        compiler_params=_cparams(("arbitrary", "arbitrary")), name="delta_bwd")(
            _by_sequence(qkv, lp), _by_sequence(ba, lp), alog, dtb, states, t_invs, _by_sequence(do, lp), *scatter)
    return [out[0].reshape(t, 3 * N_HEADS * HEAD), out[1].reshape(t, HEAD)] + list(out[2:])


ATT_Q_TILE = 256
ATT_K_TILE = 512
ATT_SCALE = QK_DIM ** -0.5


def _tiles(end, size):
    return [(s, min(s + size, end)) for s in range(0, end, size)]


def _att_visible(q0, q1, k0, k1, keys_first):
    if k1 <= q0 + CHUNK and k0 >= PAD_ROWS:
        return None
    shape = (k1 - k0, q1 - q0) if keys_first else (q1 - q0, k1 - k0)
    qpos = q0 + lax.broadcasted_iota(jnp.int32, shape, 1 if keys_first else 0)
    kpos = k0 + lax.broadcasted_iota(jnp.int32, shape, 0 if keys_first else 1)
    shift = CHUNK.bit_length() - 1
    return (jnp.right_shift(kpos, shift) <= jnp.right_shift(qpos, shift)) & (kpos >= PAD_ROWS)


def _att_seq_specs(lp):
    return pl.BlockSpec((lp, QK_PAD), lambda b, h: (b, h)), pl.BlockSpec((lp, HEAD), lambda b, h: (b, h))


def flash_fwd(q, k, v, lp):
    t = q.shape[0]
    qk_seq, o_seq = _att_seq_specs(lp)

    def body(q_ref, k_ref, v_ref, o_ref, lse_ref):
        q_tiles = _tiles(lp, ATT_Q_TILE)

        def score_steps(q0, q1, out):
            def step(k0, k1):
                s = mm_nt(q_ref[q0:q1, :], k_ref[k0:k1, :])
                vis = _att_visible(q0, q1, k0, k1, False)
                s = s if vis is None else jnp.where(vis, s, NEG)
                out["scores"].append(s)
                row_max = jnp.max(s, -1, keepdims=True)
                out["m"] = row_max if out["m"] is None else jnp.maximum(out["m"], row_max)
            return [functools.partial(step, k0, k1) for k0, k1 in _tiles(q1, ATT_K_TILE)]

        cur = {"scores": [], "m": None}
        for step in score_steps(*q_tiles[0], cur):
            step()
        for i, (q0, q1) in enumerate(q_tiles):
            nxt = {"scores": [], "m": None}
            ahead = score_steps(*q_tiles[i + 1], nxt) if i + 1 < len(q_tiles) else []
            l = jnp.zeros((q1 - q0, 1), F32)
            acc = jnp.zeros((q1 - q0, HEAD), F32)
            for s, (k0, k1) in zip(cur["scores"], _tiles(q1, ATT_K_TILE), strict=True):
                if ahead:
                    ahead.pop(0)()
                p = jnp.exp(s - cur["m"])
                l = l + jnp.sum(p, -1, keepdims=True)
                acc = acc + mm_nn(p, v_ref[k0:k1, :])
            for step in ahead:
                step()
            o_ref[q0:q1, :] = acc / l
            lse_ref[q0:q1, :] = jnp.broadcast_to(cur["m"] + jnp.log(l), (q1 - q0, HEAD))
            cur = nxt

    big = jax.ShapeDtypeStruct((t, N_HEADS * HEAD), F32)
    return pl.pallas_call(
        body, grid=(t // lp, N_HEADS), in_specs=[qk_seq, qk_seq, o_seq], out_specs=[o_seq, o_seq], out_shape=[big, big],
        compiler_params=_cparams(("arbitrary", "arbitrary")), name="flash_fwd")(q, k, v)


def flash_bwd(q, k, v, o, lse, do, lp):
    t = q.shape[0]
    qk_seq, o_seq = _att_seq_specs(lp)

    def body(q_ref, k_ref, v_ref, o_ref, lse_ref, do_ref, dq_ref, dk_out_ref, dv_out_ref, dk_ref, dv_ref):
        dk_ref[...] = jnp.zeros_like(dk_ref)
        dv_ref[...] = jnp.zeros_like(dv_ref)
        for q0, q1 in _tiles(lp, ATT_Q_TILE):
            qb, dob = q_ref[q0:q1, :], do_ref[q0:q1, :]
            lse_row = jnp.transpose(lse_ref[q0:q1, :])[0:1, :]
            dsum_row = jnp.sum(jnp.transpose(dob * o_ref[q0:q1, :]), axis=0, keepdims=True)
            dq = jnp.zeros((q1 - q0, QK_PAD), F32)
            for k0, k1 in _tiles(q1, ATT_K_TILE):
                kb, vb = k_ref[k0:k1, :], v_ref[k0:k1, :]
                s = mm_nt(kb, qb)
                vis = _att_visible(q0, q1, k0, k1, True)
                s = s if vis is None else jnp.where(vis, s, NEG)
                p = jnp.exp(s - lse_row)
                ds = p * (mm_nt(vb, dob) - dsum_row)
                dv_ref[k0:k1, :] += mm_nn(p, dob)
                dk_ref[k0:k1, :] += mm_nn(ds, qb)
                dq = dq + mm_tn(ds, kb)
            dq_ref[q0:q1, :] = dq.astype(dq_ref.dtype)
        dk_out_ref[...] = dk_ref[...].astype(dk_out_ref.dtype)
        dv_out_ref[...] = dv_ref[...].astype(dv_out_ref.dtype)

    narrow = _MXU_DTYPE
    return pl.pallas_call(
        body, grid=(t // lp, N_HEADS), in_specs=[qk_seq, qk_seq, o_seq, o_seq, o_seq, o_seq], out_specs=[qk_seq, qk_seq, o_seq],
        out_shape=[jax.ShapeDtypeStruct((t, N_HEADS * QK_PAD), narrow), jax.ShapeDtypeStruct((t, N_HEADS * QK_PAD), narrow),
                   jax.ShapeDtypeStruct((t, N_HEADS * HEAD), narrow)],
        scratch_shapes=[pltpu.VMEM((lp, QK_PAD), F32), pltpu.VMEM((lp, HEAD), F32)],
        compiler_params=_cparams(("arbitrary", "arbitrary")), name="flash_bwd")(q, k, v, o, lse, do)


def loss_head(h2, target, lp):
    nb, seq, d = target.shape
    cols = _pick(d, (512, 128))
    ncol = d // cols

    def body(h_ref, t_ref, loss_ref, dh_ref, acc_ref):
        b, j = pl.program_id(0), pl.program_id(1)

        @pl.when((b == 0) & (j == 0))
        def _():
            acc_ref[...] = jnp.zeros_like(acc_ref)

        err = h_ref[LEAD:, :] - t_ref[...]
        dh_ref[:LEAD, :] = jnp.zeros((LEAD, cols), F32)
        dh_ref[LEAD:, :] = err * (1.0 / d)
        acc_ref[...] += jnp.sum(err * err, axis=0, keepdims=True)

        @pl.when((b == nb - 1) & (j == ncol - 1))
        def _():
            loss_ref[...] = jnp.sum(acc_ref[...], axis=1, keepdims=True) * (0.5 / d)

    return pl.pallas_call(
        body, grid=(nb, ncol),
        in_specs=[pl.BlockSpec((None, lp, cols), lambda b, j: (b, 0, j)), pl.BlockSpec((None, seq, cols), lambda b, j: (b, 0, j))],
        out_specs=[pl.BlockSpec((1, 1), lambda b, j: (0, 0)), pl.BlockSpec((None, lp, cols), lambda b, j: (b, 0, j))],
        out_shape=[jax.ShapeDtypeStruct((1, 1), F32), jax.ShapeDtypeStruct((nb, lp, d), F32)],
        scratch_shapes=[pltpu.VMEM((1, cols), F32)], compiler_params=_cparams(("arbitrary", "arbitrary")), name="loss_head")(h2, target)


def embed_norm(x, meta, gain, lp, gather=()):
    nb, seq, d = x.shape
    nblk, nx = lp // LEAD, len(gather)

    def body(*refs):
        x_ref, meta_ref, g_ref = refs[:3]
        h_ref, hn_ref = refs[3 + nx:5 + nx]
        b, i = pl.program_id(0), pl.program_id(1)
        finish = _ride(gather, False, refs[3:3 + nx], refs[5 + nx:5 + 2 * nx], refs[5 + 2 * nx:], (b == 0) & (i == 0), (b == nb - 1) & (i == nblk - 1),
                       two_level=True)

        @pl.when(i == 0)
        def _():
            h_ref[:PAD_ROWS, :] = jnp.zeros((PAD_ROWS, d), F32)
            h_ref[PAD_ROWS:, :] = meta_ref[...]

        @pl.when(i > 0)
        def _():
            h_ref[...] = x_ref[...]

        hn_ref[...] = _rms(h_ref[...], g_ref[...]).astype(hn_ref.dtype)
        finish()

    rows = pl.BlockSpec((LEAD, d), lambda b, i: (b * nblk + i, 0))
    out = pl.pallas_call(
        body, grid=(nb, nblk),
        in_specs=[pl.BlockSpec((None, LEAD, d), lambda b, i: (b, jnp.maximum(i - 1, 0), 0)), pl.BlockSpec((N_META, d), lambda b, i: (0, 0)),
                  pl.BlockSpec((1, d), lambda b, i: (0, 0))] + [_HBM] * nx,
        out_specs=[rows, rows] + [_HBM] * nx,
        out_shape=[jax.ShapeDtypeStruct((nb * lp, d), F32), jax.ShapeDtypeStruct((nb * lp, d), _MXU_DTYPE)] + Exchange.out_shape(gather, False),
        scratch_shapes=Exchange.scratch(nx) if nx else [],
        compiler_params=_cparams(("arbitrary", "arbitrary")), name="embed_norm")(x, meta, gain, *gather)
    return list(out)


def meta_grad(dh0):
    nb, _, d = dh0.shape

    def body(g_ref, o_ref):
        @pl.when(pl.program_id(0) == 0)
        def _():
            o_ref[...] = jnp.zeros_like(o_ref)

        o_ref[...] += g_ref[PAD_ROWS:LEAD, :]

    return pl.pallas_call(
        body, grid=(nb,), in_specs=[pl.BlockSpec((None, LEAD, d), lambda b: (b, 0, 0))],
        out_specs=pl.BlockSpec((N_META, d), lambda b: (0, 0)), out_shape=jax.ShapeDtypeStruct((N_META, d), F32),
        compiler_params=_cparams(("arbitrary",)), name="meta_grad")(dh0)


_HBM = pl.BlockSpec(memory_space=pltpu.HBM)


def _mesh_pos():
    x, y, c = lax.axis_index("x"), lax.axis_index("y"), lax.axis_index("c")
    return x, y, c


def _peer(x, y, c, k):
    px = 1 - x if k & 4 else x
    py = 1 - y if k & 2 else y
    pc = 1 - c if k & 1 else c
    return (px, py, pc), 4 * px + 2 * py + pc


class Exchange:
    def __init__(self, x_refs, out_refs, send_sems, recv_sems, local_sems, scatter):
        self.x_refs, self.out_refs, self.scatter = x_refs, out_refs, scatter
        self.send_sems, self.recv_sems, self.local_sems = send_sems, recv_sems, local_sems
        self.pos = _mesh_pos()
        x, y, c = self.pos
        self.me = 4 * x + 2 * y + c

    @staticmethod
    def scratch(n):
        return [pltpu.SemaphoreType.DMA((n, N_DEV - 1)), pltpu.SemaphoreType.DMA((n, N_DEV - 1)), pltpu.SemaphoreType.DMA((n,))]

    @staticmethod
    def out_shape(bufs, scatter):
        return [jax.ShapeDtypeStruct(b.shape if scatter else (N_DEV,) + b.shape, b.dtype) for b in bufs]

    def _local(self, i):
        return pltpu.make_async_copy(self.x_refs[i].at[self.me] if self.scatter else self.x_refs[i], self.out_refs[i].at[self.me], self.local_sems.at[i])

    def _copy(self, i, k, landing):
        peer, peer_id = _peer(*self.pos, k)
        src = self.x_refs[i].at[peer_id] if self.scatter else self.x_refs[i]
        return pltpu.make_async_remote_copy(src_ref=src, dst_ref=self.out_refs[i].at[peer_id if landing else self.me],
                                            send_sem=self.send_sems.at[i, k - 1], recv_sem=self.recv_sems.at[i, k - 1],
                                            device_id=peer, device_id_type=pl.DeviceIdType.MESH)

    def start(self):
        for i in range(len(self.x_refs)):
            self._local(i).start()
        for k in range(1, N_DEV):
            for i in range(len(self.x_refs)):
                self._copy(i, k, False).start()

    def wait(self):
        for k in range(1, N_DEV):
            for i in range(len(self.x_refs)):
                self._copy(i, k, True).wait_recv()
        for k in range(1, N_DEV):
            for i in range(len(self.x_refs)):
                self._copy(i, k, False).wait_send()
        for i in range(len(self.x_refs)):
            self._local(i).wait()


class TwoLevelGather(Exchange):
    DIRECT = (1, 4, 2, 6)
    FROM_CHIPS = (4, 2, 6)

    def _forward(self, i, k):
        _, origin = _peer(*self.pos, k)
        sibling, _ = _peer(*self.pos, 1)
        block = self.out_refs[i].at[origin]
        return pltpu.make_async_remote_copy(src_ref=block, dst_ref=block, send_sem=self.send_sems.at[i, (k ^ 1) - 1],
                                            recv_sem=self.recv_sems.at[i, (k ^ 1) - 1], device_id=sibling, device_id_type=pl.DeviceIdType.MESH)

    def start(self):
        assert not self.scatter
        for i in range(len(self.x_refs)):
            self._local(i).start()
        for k in self.DIRECT:
            for i in range(len(self.x_refs)):
                self._copy(i, k, False).start()

    def wait(self):
        n = range(len(self.x_refs))
        for k in self.FROM_CHIPS:
            for i in n:
                self._copy(i, k, True).wait_recv()
                self._forward(i, k).start()
        for k in (1, 5, 3, 7):
            for i in n:
                self._copy(i, k, True).wait_recv()
        for k in self.DIRECT:
            for i in n:
                self._copy(i, k, False).wait_send()
        for k in self.FROM_CHIPS:
            for i in n:
                self._forward(i, k).wait_send()
        for i in n:
            self._local(i).wait()


def _exchange(name, bufs, scatter):
    n = len(bufs)

    def body(*refs):
        ex = Exchange(refs[:n], refs[n:2 * n], *refs[2 * n:], scatter)
        ex.start()
        ex.wait()

    return pl.pallas_call(body, in_specs=[_HBM] * n, out_specs=[_HBM] * n, out_shape=Exchange.out_shape(bufs, scatter),
                          scratch_shapes=Exchange.scratch(n), name=name)(*bufs)


def _f_rms(x, g):
    return (_rms(x, g),)


def _f_rms2(x, g1, g2):
    r = x * lax.rsqrt(jnp.sum(x * x, -1, keepdims=True) / x.shape[-1] + EPS)
    return r * g1, r * g2


def _f_out_gate(o, gate, gain):
    return (_rms(o, gain) * _silu(gate),)


def _f_gate(o, gate):
    return (o * _silu(gate),)


def _swap_rope_halves(x):
    return pltpu.roll(x, ROPE // 2, 1) + pltpu.roll(x, HEAD - ROPE // 2, 1)


def _qk_final_inv_rms(nope, rope_in):
    ms = (jnp.sum(nope * nope, -1, keepdims=True) + jnp.sum(rope_in * rope_in, -1, keepdims=True)) / QK_DIM
    return lax.rsqrt(ms + EPS)


@functools.partial(jax.custom_vjp, nondiff_argnums=(0,))
def _qk_final(scale, nope, rope_in, g_nope, g_rope, cos, sin):
    r = _qk_final_inv_rms(nope, rope_in)
    b = rope_in * (r * g_rope)
    out = jnp.concatenate([nope * (r * g_nope), b * cos + _swap_rope_halves(b) * sin], axis=1)
    return out if scale == 1.0 else out * scale


def _qk_final_fwd(scale, nope, rope_in, g_nope, g_rope, cos, sin):
    return _qk_final(scale, nope, rope_in, g_nope, g_rope, cos, sin), (nope, rope_in, g_nope, g_rope, cos, sin)


def _qk_final_bwd(scale, res, g):
    nope, rope_in, g_nope, g_rope, cos, sin = res
    r = _qk_final_inv_rms(nope, rope_in)
    ga, gb = g[:, :HEAD], g[:, HEAD:]
    if scale != 1.0:
        ga, gb = ga * scale, gb * scale
    db = gb * cos + _swap_rope_halves(gb * sin)
    t_a, t_b = ga * nope, db * rope_in
    d_r = jnp.sum(t_a * g_nope + t_b * g_rope, -1, keepdims=True)
    c = d_r * (r * r * r) * (-1.0 / QK_DIM)
    d_nope = ga * (r * g_nope) + nope * c
    d_rope = db * (r * g_rope) + rope_in * c
    d_g_nope = jnp.sum(t_a * r, 0, keepdims=True)
    d_g_rope = jnp.sum(t_b * r, 0, keepdims=True)
    return d_nope, d_rope, d_g_nope, d_g_rope, jnp.zeros_like(cos), jnp.zeros_like(sin)


_qk_final.defvjp(_qk_final_fwd, _qk_final_bwd)


def _f_qk_final(scale, nope, rope_in, g_nope, g_rope, cos, sin):
    return (_qk_final(scale, nope, rope_in, g_nope, g_rope, cos, sin),)


def _rope_tables(lp):
    half = ROPE // 2
    pos = jnp.maximum(jnp.arange(lp) - PAD_ROWS, 0)
    inv = ROPE_THETA ** (-jnp.arange(half, dtype=F32) / half)
    ang = pos.astype(F32)[:, None] * inv[None, :]
    zeros = jnp.zeros((lp, HEAD - ROPE), F32)
    cos = jnp.concatenate([jnp.cos(ang), jnp.cos(ang), zeros], 1)
    sin = jnp.concatenate([-jnp.sin(ang), jnp.sin(ang), zeros], 1)
    return cos, sin


def _pad_lanes(w, width=HEAD):
    return jnp.pad(w, ((0, 0), (0, width - w.shape[1])))


def _pad_rows(w, rows=HEAD):
    return jnp.pad(w, ((0, rows - w.shape[0]), (0, 0)))


def _split_heads_qk_t(w_t):
    k = w_t.shape[1]
    w3 = w_t.reshape(N_HEADS, QK_DIM, k)
    nope = w3[:, :HEAD].reshape(N_HEADS * HEAD, k)
    rope = jnp.pad(w3[:, HEAD:], ((0, 0), (0, HEAD - ROPE), (0, 0))).reshape(N_HEADS * HEAD, k)
    return jnp.concatenate([nope, rope], 0)


def _merge_heads_qk_t(g_t):
    k = g_t.shape[1]
    kw = N_HEADS * HEAD
    nope, rope = g_t[:kw].reshape(N_HEADS, HEAD, k), g_t[kw:].reshape(N_HEADS, HEAD, k)[:, :ROPE]
    return jnp.concatenate([nope, rope], 1).reshape(N_HEADS * QK_DIM, k)


def local_step(x, target, w, deferred=None):
    nb, seq, d = x.shape
    lp = seq + LEAD
    t = nb * lp
    tr = _pick(lp, (544, 128))
    ntab = lp // tr
    mxu = _MXU_DTYPE
    kw = N_HEADS * HEAD

    a_conv = w["a_conv"].T
    alog, dtb, o_gain = _pad_lanes(w["a_log"]), _pad_lanes(w["a_dt_bias"]), w["a_o_gain"]
    a_norm, kv_norm, b_norm = w["a_norm"], w["kv_norm"][None, :], w["b_norm"]
    lat_norm, qlat_norm = w["kv_latent_norm"][None, :], w["b_q_latent_norm"]
    kg_nope, kg_rope = w["k_gain"][None, :HEAD], _pad_lanes(w["k_gain"][None, HEAD:])
    qg_nope, qg_rope = w["b_q_gain"][:, :HEAD], _pad_lanes(w["b_q_gain"][:, HEAD:])
    cos, sin = _rope_tables(lp)

    h0, hn, *gathered = embed_norm(x, w["meta_tokens"].T, a_norm, lp, gather=deferred.first_gather_bufs if deferred else ())
    if deferred:
        w = {**w, **deferred.finish_first(gathered)}
    a_w_in_t = w["a_w_in"].astype(mxu)
    w_qkv_t, w_gba_t = a_w_in_t[:3 * kw], _pad_rows(a_w_in_t[3 * kw:], kw + HEAD)
    z_qkv = matmul("a_in_qkv", hn, w_qkv_t, "nt")
    z_gba = matmul("a_in_gate_ba", hn, w_gba_t, "nt")
    ba_block = kw // HEAD
    qkv_a, y_conv = conv_fwd(z_qkv, a_conv, lp)
    o_a, states, t_invs, *gathered = delta_fwd(qkv_a, z_gba, ba_block, alog, dtb, lp, gather=deferred.gather_bufs if deferred else ())
    if deferred:
        w = {**w, **deferred.finish(gathered)}
    a_w_out = w["a_w_out"].astype(mxu)
    w_down = _pad_lanes(w["kv_w_down"], KV_RANK + HEAD).astype(mxu)
    w_ukv_t = jnp.concatenate([w["kv_w_uk"], w["kv_w_uv"]], 0).astype(mxu)
    b_w_in_t = w["b_w_in"].astype(mxu)
    w_cq_t, w_gb_t = b_w_in_t[:Q_RANK], b_w_in_t[Q_RANK:]
    w_q_t = _split_heads_qk_t(w["b_w_uq"]).astype(mxu)
    b_w_out = w["b_w_out"].astype(mxu)
    og_args = [Arg(o_a, bc=HEAD, ph=True, diff=True), Arg(z_gba, bc=HEAD, ph=True, diff=True, gdt=mxu), Arg(o_gain, "par", diff=True)]
    (og_a,) = row_call("a_out_gate_fwd", _f_out_gate, og_args, [(kw, mxu, HEAD, True)], tr, nh=N_HEADS)
    h1 = matmul("a_out", og_a, a_w_out, "nn", res=h0)

    hk, hb = row_call("b_norms_fwd", _f_rms2, [Arg(h1), Arg(kv_norm, "par"), Arg(b_norm, "par")], [(d, mxu, d, False), (d, mxu, d, False)], tr)
    c_down = matmul("kv_down", hk, w_down, "nn")
    c_kv_arg = Arg(c_down, bc=KV_RANK, diff=True, gdt=mxu)
    k_pe_arg = Arg(c_down, bc=HEAD, base=KV_RANK // HEAD, diff=True)
    c_q_raw = matmul("b_in_q", hb, w_cq_t, "nt")
    gate_b = matmul("b_in_gate", hb, w_gb_t, "nt")
    (c_kv,) = row_call("kv_latent_fwd", _f_rms, [c_kv_arg, Arg(lat_norm, "par")], [(KV_RANK, mxu, KV_RANK, False)], tr)
    (c_q,) = row_call("q_latent_fwd", _f_rms, [Arg(c_q_raw), Arg(qlat_norm, "par")], [(Q_RANK, mxu, Q_RANK, False)], tr)
    k_nope = matmul("k_up", c_kv, w_ukv_t[:kw], "nt")
    v_b = matmul("v_up", c_kv, w_ukv_t[kw:], "nt", out_dtype=mxu)
    q_up = matmul("q_up", c_q, w_q_t, "nt")
    tabs = [Arg(cos, "tab"), Arg(sin, "tab")]
    k_args = [Arg(k_nope, bc=HEAD, ph=True, diff=True, gdt=mxu), k_pe_arg, Arg(kg_nope, "par", diff=True), Arg(kg_rope, "par", diff=True)] + tabs
    q_args = [Arg(q_up, bc=HEAD, ph=True, diff=True, gdt=mxu), Arg(q_up, bc=HEAD, base=N_HEADS, ph=True, diff=True, gdt=mxu),
              Arg(qg_nope, "par", diff=True), Arg(qg_rope, "par", diff=True)] + tabs
    f_k_final, f_q_final = functools.partial(_f_qk_final, 1.0), functools.partial(_f_qk_final, ATT_SCALE)
    (k_fin,) = row_call("k_final_fwd", f_k_final, k_args, [(N_HEADS * QK_PAD, mxu, QK_PAD, True)], tr, nh=N_HEADS, ntab=ntab)
    (q_fin,) = row_call("q_final_fwd", f_q_final, q_args, [(N_HEADS * QK_PAD, mxu, QK_PAD, True)], tr, nh=N_HEADS, ntab=ntab)
    o_b, lse = flash_fwd(q_fin, k_fin, v_b, lp)
    gb_args = [Arg(o_b, diff=True), Arg(gate_b, diff=True, gdt=mxu)]
    (og_b,) = row_call("b_gate_fwd", _f_gate, gb_args, [(kw, mxu, kw, False)], tr)
    h2 = matmul("b_out", og_b, b_w_out, "nn", res=h1)

    loss, dh2 = loss_head(h2.reshape(nb, lp, d), target, lp)
    dh2 = dh2.reshape(t, d)
    grads = {}

    d_og_b = matmul("b_out_dx", dh2, b_w_out, "nt", out_dtype=mxu)
    grads["b_w_out"] = matmul("b_out_dw", og_b, dh2, "tn")
    d_o_b, d_gate_b = row_vjp_call("b_gate_bwd", _f_gate, gb_args, [Arg(d_og_b)], tr)
    dq_fin, dk_fin, dv_b = flash_bwd(q_fin, k_fin, v_b, o_b, lse, d_o_b, lp)
    dq_nope, dq_rope, d_qg_nope, d_qg_rope = row_vjp_call(
        "q_final_bwd", f_q_final, q_args, [Arg(dq_fin, bc=QK_PAD, ph=True)], tr, nh=N_HEADS, ntab=ntab)
    dk_nope, dk_pe, d_kg_nope, d_kg_rope = row_vjp_call(
        "k_final_bwd", f_k_final, k_args, [Arg(dk_fin, bc=QK_PAD, ph=True)], tr, nh=N_HEADS, ntab=ntab)
    grads["b_q_gain"] = jnp.concatenate([d_qg_nope, d_qg_rope[:, :ROPE]], 1)
    grads["k_gain"] = jnp.concatenate([d_kg_nope, d_kg_rope[:, :ROPE]], 1)[0]
    d_c_q = matmul("q_nope_dx", dq_nope, w_q_t[:kw], "nn")
    d_c_q = matmul("q_rope_dx", dq_rope, w_q_t[kw:], "nn", res=d_c_q)
    grads["b_w_uq"] = _merge_heads_qk_t(jnp.concatenate([matmul("q_nope_dw", dq_nope, c_q, "tn"), matmul("q_rope_dw", dq_rope, c_q, "tn")], 0))
    d_c_kv = matmul("k_up_dx", dk_nope, w_ukv_t[:kw], "nn")
    d_c_kv = matmul("v_up_dx", dv_b, w_ukv_t[kw:], "nn", res=d_c_kv)
    grads["kv_w_uk"], grads["kv_w_uv"] = matmul("k_up_dw", dk_nope, c_kv, "tn"), matmul("v_up_dw", dv_b, c_kv, "tn")
    d_c_q_raw, grads["b_q_latent_norm"] = row_vjp_call(
        "q_latent_bwd", _f_rms, [Arg(c_q_raw, diff=True, gdt=mxu), Arg(qlat_norm, "par", diff=True)], [Arg(d_c_q)], tr)
    d_c_kv_raw, d_lat = row_vjp_call(
        "kv_latent_bwd", _f_rms, [c_kv_arg, Arg(lat_norm, "par", diff=True)], [Arg(d_c_kv)], tr)
    grads["kv_latent_norm"] = d_lat[0]
    d_hb = matmul("b_in_q_dx", d_c_q_raw, w_cq_t, "nn")
    d_hb = matmul("b_in_gate_dx", d_gate_b, w_gb_t, "nn", res=d_hb, out_dtype=mxu)
    grads["b_w_in"] = jnp.concatenate([matmul("b_in_q_dw", d_c_q_raw, hb, "tn"), matmul("b_in_gate_dw", d_gate_b, hb, "tn")], 0)
    d_c_down = jnp.concatenate([d_c_kv_raw, dk_pe.astype(mxu)], 1)
    d_hk = matmul("kv_down_dx", d_c_down, w_down, "nt", out_dtype=mxu)
    grads["kv_w_down"] = matmul("kv_down_dw", hk, d_c_down, "tn")[:, :KV_RANK + ROPE]
    dh1, d_kv_norm, grads["b_norm"] = row_vjp_call(
        "b_norms_bwd", lambda x_, g1, g2: _f_rms2(x_, g1, g2) + (x_,),
        [Arg(h1, diff=True), Arg(kv_norm, "par", diff=True), Arg(b_norm, "par", diff=True)], [Arg(d_hk), Arg(d_hb), Arg(dh2)], tr)
    grads["kv_norm"] = d_kv_norm[0]

    d_og_a = matmul("a_out_dx", dh1, a_w_out, "nt", out_dtype=mxu)
    grads["a_w_out"] = matmul("a_out_dw", og_a, dh1, "tn")
    d_o_a, d_gate_a, grads["a_o_gain"] = row_vjp_call(
        "a_out_gate_bwd", _f_out_gate, og_args, [Arg(d_og_a, bc=HEAD, ph=True)], tr, nh=N_HEADS)
    dqkv_a, d_ba, d_alog, d_dtb, *received = delta_bwd(qkv_a, z_gba, ba_block, alog, dtb, states, t_invs, d_o_a, lp,
                                                        scatter=deferred.scatter_bufs(grads) if deferred else ())
    grads["a_log"], grads["a_dt_bias"] = d_alog[:, :N_HEADS], d_dtb[:, :N_HEADS]
    dz_qkv, d_conv = conv_bwd(z_qkv, y_conv, a_conv, dqkv_a, lp)
    grads["a_conv"] = d_conv.T
    dz_gba = jnp.concatenate([d_gate_a, d_ba.astype(mxu)], 1)
    grads["a_w_in"] = jnp.concatenate([matmul("a_in_qkv_dw", dz_qkv, hn, "tn"), matmul("a_in_gate_ba_dw", dz_gba, hn, "tn")[:kw + 2 * N_HEADS]], 0)
    ride = deferred.last_scatter_bufs(grads) if deferred else ()
    d_hn = matmul("a_in_qkv_dx", dz_qkv, w_qkv_t, "nn", scatter=ride)
    if ride:
        d_hn, *received_last = d_hn
        received = list(received) + received_last
    d_hn = matmul("a_in_gate_ba_dx", dz_gba, w_gba_t, "nn", res=d_hn, out_dtype=mxu)
    dh0, grads["a_norm"] = row_vjp_call("a_norm_bwd", lambda x_, g_: _f_rms(x_, g_) + (x_,),
                                        [Arg(h0, diff=True), Arg(a_norm, "par", diff=True)], [Arg(d_hn), Arg(dh1)], tr)
    dh0 = dh0.reshape(nb, lp, d)
    grads["meta_tokens"] = meta_grad(dh0).T
    return loss, dh0[:, LEAD:], grads, received


_SHARDED = (
    ("meta_tokens", True, False), ("a_norm", True, False), ("a_w_in", True, True), ("a_conv", True, False), ("a_w_out", False, True),
    ("kv_w_down", False, True), ("kv_w_uk", True, True), ("kv_w_uv", True, True), ("b_w_in", True, True), ("b_w_uq", True, True),
    ("b_w_out", False, True))
_REPLICATED = ("a_log", "a_dt_bias", "a_o_gain", "kv_norm", "kv_latent_norm", "k_gain", "b_norm", "b_q_latent_norm", "b_q_gain")
_ALL_WEIGHTS = ("meta_tokens", "a_norm", "a_w_in", "a_conv", "a_log", "a_dt_bias", "a_o_gain", "a_w_out", "kv_norm", "kv_w_down",
                "kv_latent_norm", "kv_w_uk", "kv_w_uv", "k_gain", "b_norm", "b_w_in", "b_q_latent_norm", "b_w_uq", "b_q_gain", "b_w_out")


def _round_up(n, m):
    return (n + m - 1) // m * m


def _pack_rows(pieces, row_multiple):
    padded = []
    for p in pieces:
        n = p.shape[-1]
        padded.append(jnp.pad(p, [(0, 0)] * (p.ndim - 1) + [(0, _round_up(n, PACK_COLS) - n)]))
    flat = jnp.concatenate(padded, -1)
    rows = _round_up(flat.shape[-1] // PACK_COLS, row_multiple)
    flat = jnp.pad(flat, [(0, 0)] * (flat.ndim - 1) + [(0, rows * PACK_COLS - flat.shape[-1])])
    return flat.reshape(flat.shape[:-1] + (rows, PACK_COLS))


def _unpack_rows(buf, sizes):
    flat = buf.reshape(buf.shape[:-2] + (-1,))
    out, off = [], 0
    for n in sizes:
        out.append(flat[..., off:off + n])
        off += _round_up(n, PACK_COLS)
    return out


def _shard_2d(a):
    return a.reshape(a.shape[-2:]) if a.ndim > 2 else a


def _kl_shard(a, by_cols):
    return _shard_2d(a).T if by_cols else _shard_2d(a)


_GROUPS_FIRST = (("a_w_in",),)
_GROUPS_LATER = (("a_w_out", "b_w_in", "b_w_out"), ("b_w_uq",), ("kv_w_down",), ("kv_w_uk", "kv_w_uv"))
_SMALL_SHARDED = ("meta_tokens", "a_norm", "a_conv")
_BY_COLS = {name: by_cols for name, by_cols, _ in _SHARDED}
ROW_ALIGN = 16


def _stack_rows(pieces):
    padded, starts, row = [], [], 0
    for p in pieces:
        r = p.shape[-2]
        padded.append(jnp.pad(p, [(0, 0)] * (p.ndim - 2) + [(0, _round_up(r, ROW_ALIGN) - r), (0, 0)]))
        starts.append(row)
        row += _round_up(r, ROW_ALIGN)
    return jnp.concatenate(padded, -2), starts


def _stack_group(arrays_by_name, names):
    arrays = [arrays_by_name[n].astype(BF16) for n in names]
    buf, starts = _stack_rows(arrays)
    return buf, [(n, s, a.shape[-2]) for n, s, a in zip(names, starts, arrays, strict=True)]


def _stack_groups(arrays_by_name, groups):
    stacked = [_stack_group(arrays_by_name, names) for names in groups]
    return [b for b, _ in stacked], [entries for _, entries in stacked]


def _full_from_gathered(gathered, layout):
    full = {}
    for got, entries in zip(gathered, layout, strict=True):
        for name, start, rows in entries:
            full[name] = got[:, start:start + rows].reshape(N_DEV * rows, got.shape[-1])
    return full


def gather_small_weights(local):
    small = [_kl_shard(local[n], _BY_COLS[n]) for n in _SMALL_SHARDED]
    (gathered,) = _exchange("all_gather", [_pack_rows([s.reshape(-1) for s in small], 8)], scatter=False)
    full = {}
    for name, part, sh in zip(_SMALL_SHARDED, _unpack_rows(gathered, [s.size for s in small]), small, strict=True):
        full[name] = part.reshape(N_DEV * sh.shape[0], sh.shape[1])
    full["a_norm"] = full["a_norm"].reshape(1, -1)
    return full


class LaterExchanges:
    def __init__(self, local):
        shards = {n: _kl_shard(local[n], _BY_COLS[n]) for names in _GROUPS_FIRST + _GROUPS_LATER for n in names}
        self.first_gather_bufs, self.first_layout = _stack_groups(shards, _GROUPS_FIRST)
        self.gather_bufs, self.layout = _stack_groups(shards, _GROUPS_LATER)

    def finish_first(self, gathered):
        return _full_from_gathered(gathered, self.first_layout)

    def finish(self, gathered):
        return _full_from_gathered(gathered, self.layout)

    def scatter_bufs(self, grads):
        return _stack_groups(_owner_slices(grads, _GROUPS_LATER), _GROUPS_LATER)[0]

    def last_scatter_bufs(self, grads):
        bufs, self.last_layout = _stack_groups(_owner_slices(grads, _GROUPS_FIRST), _GROUPS_FIRST)
        return bufs


def _owner_slices(grads, groups):
    return {n: grads[n].reshape(N_DEV, -1, grads[n].shape[-1]) for names in groups for n in names}


def reduce_contributions(name, recv):
    _, r, c = recv.shape
    tr = _pick(r, (256, 128, 64, 32, 16, 8))

    def body(g_ref, o_ref):
        g = g_ref[0].astype(F32)
        for dev in range(1, N_DEV):
            g = g + g_ref[dev].astype(F32)
        o_ref[...] = g

    return pl.pallas_call(
        body, grid=(r // tr,), in_specs=[pl.BlockSpec((N_DEV, tr, c), lambda i: (0, i, 0))], out_specs=pl.BlockSpec((tr, c), lambda i: (i, 0)),
        out_shape=jax.ShapeDtypeStruct((r, c), F32), compiler_params=_cparams(("arbitrary",)), name=name)(recv)


def adamw_all(gs, ws, ms, vs):
    n = len(gs)

    def body(*refs):
        for i in range(n):
            g_ref, w_ref, m_ref, v_ref = (refs[j * n + i] for j in range(4))
            d_ref, mo_ref, vo_ref = (refs[(4 + j) * n + i] for j in range(3))
            g = g_ref[...]
            m_new = ADAM_B1 * m_ref[...] + (1.0 - ADAM_B1) * g
            v_new = ADAM_B2 * v_ref[...] + (1.0 - ADAM_B2) * (g * g)
            m_hat = m_new / (1.0 - ADAM_B1 ** ADAM_STEP)
            v_hat = v_new / (1.0 - ADAM_B2 ** ADAM_STEP)
            d_ref[...] = -ADAM_LR * (m_hat / (jnp.sqrt(v_hat) + ADAM_EPS) + ADAM_WD * w_ref[...])
            mo_ref[...] = m_new
            vo_ref[...] = v_new

    out = [jax.ShapeDtypeStruct(g.shape, F32) for g in gs] * 3
    res = pl.pallas_call(body, out_shape=out, compiler_params=pltpu.CompilerParams(vmem_limit_bytes=VMEM_LIMIT), name="adamw_all")(*gs, *ws, *ms, *vs)
    return res[:n], res[n:2 * n], res[2 * n:]


def kernel(x, meta_tokens, a_norm, a_w_in, a_conv, a_log, a_dt_bias, a_o_gain, a_w_out, kv_norm, kv_w_down, kv_latent_norm, kv_w_uk, kv_w_uv, k_gain, b_norm, b_w_in, b_q_latent_norm, b_w_uq, b_q_gain, b_w_out, loss_target, m_meta_tokens, m_a_norm, m_a_w_in, m_a_conv, m_a_log, m_a_dt_bias, m_a_o_gain, m_a_w_out, m_kv_norm, m_kv_w_down, m_kv_latent_norm, m_kv_w_uk, m_kv_w_uv, m_k_gain, m_b_norm, m_b_w_in, m_b_q_latent_norm, m_b_w_uq, m_b_q_gain, m_b_w_out, v_meta_tokens, v_a_norm, v_a_w_in, v_a_conv, v_a_log, v_a_dt_bias, v_a_o_gain, v_a_w_out, v_kv_norm, v_kv_w_down, v_kv_latent_norm, v_kv_w_uk, v_kv_w_uv, v_k_gain, v_b_norm, v_b_w_in, v_b_q_latent_norm, v_b_w_uq, v_b_q_gain, v_b_w_out):
    given = dict(locals())
    local_w = {n: given[n] for n in _ALL_WEIGHTS}
    full = gather_small_weights(local_w)
    for n in _REPLICATED:
        full[n] = local_w[n]
    later = LaterExchanges(local_w)

    loss_part, grad_x, grads, received_riding = local_step(x, loss_target, full, later)

    exact = [grads[n].reshape(N_DEV, -1) for n in _SMALL_SHARDED]
    exact += [jnp.broadcast_to(grads[n].reshape(1, -1), (N_DEV, grads[n].size)) for n in _REPLICATED]
    exact.append(jnp.broadcast_to(loss_part, (N_DEV, 1)))
    received = list(received_riding) + list(_exchange("all_to_all", [_pack_rows(exact, 8)], scatter=True))
    layout = later.layout + later.last_layout
    summed = [reduce_contributions(f"reduce_{i}", r) for i, r in enumerate(received)]

    grad_kl = {}
    for got, entries in zip(summed, layout):
        for n, start, rows in entries:
            grad_kl[n] = got[start:start + rows]
    parts = _unpack_rows(summed[-1], [p.shape[1] for p in exact])
    for n, part in zip(_SMALL_SHARDED + _REPLICATED, parts, strict=False):
        grad_kl[n] = part
    loss = parts[-1][0]

    def natural_2d(n, a):
        shape = _shard_2d(local_w[n]).shape if local_w[n].ndim > 1 else (1, local_w[n].size)
        return a.reshape(shape[::-1]).T if _BY_COLS.get(n, False) else a.reshape(shape)

    as_2d = lambda n, a: a.reshape(natural_2d(n, grad_kl[n]).shape)
    gs = [natural_2d(n, grad_kl[n]) for n in _ALL_WEIGHTS]
    deltas, new_m, new_v = adamw_all(gs, [as_2d(n, local_w[n]) for n in _ALL_WEIGHTS], [as_2d(n, given["m_" + n]) for n in _ALL_WEIGHTS],
                                     [as_2d(n, given["v_" + n]) for n in _ALL_WEIGHTS])
    results = [a.reshape(local_w[n].shape) for group in (gs, deltas, new_m, new_v) for n, a in zip(_ALL_WEIGHTS, group, strict=True)]
    return (loss, grad_x, *results)
```

```python
import dataclasses
import functools
import math

import jax
import jax.numpy as jnp
from jax import lax
from jax.experimental import pallas as pl
from jax.experimental.pallas import tpu as pltpu

F32 = jnp.float32
BF16 = jnp.bfloat16
_MXU_DTYPE = jnp.bfloat16

N_DEV = 8
D_MODEL = 1024
N_HEADS = 8
HEAD = 128
CHUNK = 64
N_META = 16
PAD_ROWS = 2 * CHUNK - N_META
LEAD = PAD_ROWS + N_META
ROPE = 64
QK_DIM = HEAD + ROPE
QK_PAD = 2 * HEAD
KV_RANK = 256
Q_RANK = 384
CONV_K = 4
EPS = 1e-6
NEG = -1e30
ROPE_THETA = 10000.0
ADAM_LR, ADAM_B1, ADAM_B2, ADAM_EPS, ADAM_WD, ADAM_STEP = 0.001, 0.9, 0.999, 1e-08, 0.01, 10
PACK_COLS = 512
VMEM_LIMIT = 56 * 1024 * 1024


def _pick(n, options):
    for o in options:
        if n % o == 0:
            return o
    raise ValueError(f"no tile for {n} among {options}")


def _cparams(sem):
    return pltpu.CompilerParams(dimension_semantics=sem, vmem_limit_bytes=VMEM_LIMIT)


def _dims(a, dims):
    if a.ndim == 2:
        return (dims, ((), ()))
    (ca,), (cb,) = dims
    return (((ca + 1,), (cb + 1,)), ((0,), (0,)))


def _dot(a, b, dims):
    return lax.dot_general(a.astype(_MXU_DTYPE), b.astype(_MXU_DTYPE), _dims(a, dims), preferred_element_type=F32)


@jax.custom_vjp
def mm_nn(a, b):
    return _dot(a, b, ((1,), (0,)))


@jax.custom_vjp
def mm_nt(a, b):
    return _dot(a, b, ((1,), (1,)))


@jax.custom_vjp
def mm_tn(a, b):
    return _dot(a, b, ((0,), (0,)))


mm_nn.defvjp(lambda a, b: (mm_nn(a, b), (a, b)), lambda r, g: (mm_nt(g, r[1]), mm_tn(r[0], g)))
mm_nt.defvjp(lambda a, b: (mm_nt(a, b), (a, b)), lambda r, g: (mm_nn(g, r[1]), mm_tn(g, r[0])))
mm_tn.defvjp(lambda a, b: (mm_tn(a, b), (a, b)), lambda r, g: (mm_nt(r[1], g), mm_nn(r[0], g)))


def _split_terms(x, n):
    terms, rest = [], x
    for _ in range(n):
        t = rest.astype(_MXU_DTYPE)
        terms.append(t)
        rest = rest - t.astype(F32)
    return terms


def _dot_01_raw(m, x, dims):
    m = m.astype(_MXU_DTYPE)
    return sum(lax.dot_general(m, t, _dims(m, dims), preferred_element_type=F32) for t in _split_terms(x, 3))


@jax.custom_vjp
def _dot_01(m, x):
    return _dot_01_raw(m, x, ((1,), (0,)))


_dot_01.defvjp(lambda m, x: (_dot_01(m, x), m), lambda m, g: (jnp.zeros_like(m), _dot_01_raw(m, g, ((0,), (0,)))))


def _inv_unit_lower(a):
    n = a.shape[-1]
    eye = (lax.broadcasted_iota(jnp.int32, (n, n), 0) == lax.broadcasted_iota(jnp.int32, (n, n), 1)).astype(F32)
    d = lambda u, w: lax.dot_general(u, w, _dims(u, ((1,), (0,))), preferred_element_type=F32)
    t = eye - a
    p = a.astype(_MXU_DTYPE)
    p = d(p, p)
    squarings = int(math.log2(n)) - 1
    for s in range(squarings):
        ph = p.astype(_MXU_DTYPE)
        t_hi, t_lo = _split_terms(t, 2)
        t = t + (d(t_hi, ph) + d(t_lo, ph))
        if s + 1 < squarings:
            p = d(ph, ph)
    return t


@jax.custom_vjp
def _inv_lookup(a, t):
    return t


def _inv_lookup_bwd(t, g):
    return -mm_tn(t, mm_nt(g, t)), jnp.zeros_like(t)


_inv_lookup.defvjp(lambda a, t: (t, t), _inv_lookup_bwd)


def _sigmoid(x):
    return 1.0 / (1.0 + jnp.exp(-x))


@jax.custom_vjp
def _silu(x):
    return x * _sigmoid(x)


def _silu_fwd(x):
    s = _sigmoid(x)
    return x * s, (x, s)


_silu.defvjp(_silu_fwd, lambda r, g: (g * (r[1] * (1.0 + r[0] * (1.0 - r[1]))),))


def _softplus(x):
    return jnp.where(x > 20.0, x, jnp.log(1.0 + jnp.exp(jnp.minimum(x, 20.0))))


def _rms(x, g, width=None):
    ms = jnp.sum(x * x, -1, keepdims=True) / (x.shape[-1] if width is None else width)
    return x * lax.rsqrt(ms + EPS) * g


MM_VMEM_BUDGET = 40 * 1024 * 1024


def _matmul_rows(name, a, b, mode, out_dtype, res, scatter):
    m, k = a.shape
    n = b.shape[1] if mode == "nn" else b.shape[0]
    dims = {"nn": ((1,), (0,)), "nt": ((1,), (1,))}[mode]
    out_bytes = jnp.dtype(out_dtype).itemsize
    n_in, nx = 2 + (res is not None), len(scatter)

    def vmem(tm):
        blocks = 2 * tm * k * a.dtype.itemsize + 2 * k * n * b.dtype.itemsize + 2 * tm * n * out_bytes + tm * n * 4
        return blocks + (2 * tm * n * res.dtype.itemsize if res is not None else 0)

    tm = next(c for c in (2176, 1088, 512, 256, 128, 64) if m % c == 0 and vmem(c) <= MM_VMEM_BUDGET)
    steps = m // tm

    def body(*refs):
        a_ref, b_ref, o_ref = refs[0], refs[1], refs[n_in + nx]
        i = pl.program_id(0)
        finish = _ride(scatter, True, refs[n_in:n_in + nx], refs[n_in + nx + 1:n_in + 2 * nx + 1], refs[n_in + 2 * nx + 1:], i == 0, i == steps - 1)
        out = _dot(a_ref[...], b_ref[...], dims)
        if res is not None:
            out = out + refs[2][...].astype(F32)
        o_ref[...] = out.astype(o_ref.dtype)
        finish()

    o_spec = pl.BlockSpec((tm, n), lambda i: (i, 0))
    in_specs = [pl.BlockSpec((tm, k), lambda i: (i, 0)), pl.BlockSpec(b.shape, lambda i: (0, 0))] + ([o_spec] if res is not None else [])
    args = (a, b) + ((res,) if res is not None else ())
    out = pl.pallas_call(
        body, grid=(steps,), in_specs=in_specs + [_HBM] * nx, out_specs=[o_spec] + [_HBM] * nx,
        out_shape=[jax.ShapeDtypeStruct((m, n), out_dtype)] + Exchange.out_shape(scatter, True), scratch_shapes=Exchange.scratch(nx) if nx else [],
        compiler_params=_cparams(("arbitrary",) if nx else ("parallel",)), name=name)(*args, *scatter)
    return out if nx else out[0]


def matmul(name, a, b, mode, out_dtype=F32, res=None, scatter=()):
    if mode != "tn":
        return _matmul_rows(name, a, b, mode, out_dtype, res, scatter)
    (k, m), (k2, n) = a.shape, b.shape
    assert k == k2 and res is None, (name, a.shape, b.shape, mode)
    tm = _pick(m, (m if m <= 1536 else 1024, 1024, 512, 384, 256, 128))
    tn = _pick(n, (1024, 512, 384, 256, 128))
    tk = _pick(k, (512, 256, 128))
    nk = k // tk
    dims = ((0,), (0,))

    def body(*refs):
        if res is None:
            a_ref, b_ref, o_ref, acc_ref = refs
        else:
            a_ref, b_ref, r_ref, o_ref, acc_ref = refs
        kk = pl.program_id(2)

        @pl.when(kk == 0)
        def _():
            acc_ref[...] = jnp.zeros_like(acc_ref)

        acc_ref[...] += _dot(a_ref[...], b_ref[...], dims)

        @pl.when(kk == nk - 1)
        def _():
            out = acc_ref[...]
            if res is not None:
                out = out + r_ref[...].astype(F32)
            o_ref[...] = out.astype(o_ref.dtype)

    a_spec = pl.BlockSpec((tk, tm), lambda i, j, kk: (kk, i)) if mode == "tn" else pl.BlockSpec((tm, tk), lambda i, j, kk: (i, kk))
    b_spec = pl.BlockSpec((tn, tk), lambda i, j, kk: (j, kk)) if mode == "nt" else pl.BlockSpec((tk, tn), lambda i, j, kk: (kk, j))
    o_spec = pl.BlockSpec((tm, tn), lambda i, j, kk: (i, j))
    in_specs = [a_spec, b_spec] + ([o_spec] if res is not None else [])
    args = (a, b) + ((res,) if res is not None else ())
    return pl.pallas_call(
        body, grid=(m // tm, n // tn, nk), in_specs=in_specs, out_specs=o_spec,
        out_shape=jax.ShapeDtypeStruct((m, n), out_dtype), scratch_shapes=[pltpu.VMEM((tm, tn), F32)],
        compiler_params=_cparams(("parallel", "parallel", "arbitrary")), name=name)(*args)


@dataclasses.dataclass
class Arg:
    arr: jax.Array
    kind: str = "row"
    bc: int = 0
    base: int = 0
    ph: bool = False
    diff: bool = False
    gdt: object = F32


def _arg_spec(a, tr, nh, ntab, base=None):
    bc = a.bc or a.arr.shape[1]
    base = a.base if base is None else base
    width = bc * nh if a.ph else bc
    col = base // nh if a.ph else base
    assert not a.ph or base % nh == 0
    if a.kind == "row":
        return pl.BlockSpec((tr, width), lambda i: (i, col))
    if a.kind == "tab":
        return pl.BlockSpec((tr, width), lambda i: (i % ntab, col))
    return pl.BlockSpec((a.arr.shape[0], width), lambda i: (0, col))


def _head_view(ref, a, h, rs):
    bc = a.bc or a.arr.shape[1]
    rows = slice(None) if a.kind == "par" else rs
    v = ref[rows, h * bc:(h + 1) * bc] if a.ph else ref[rows, :]
    return v.astype(F32) if jnp.issubdtype(v.dtype, jnp.floating) else v


def row_call(name, fn, args, outs, tr, nh=1, ntab=1):
    t = args[0].arr.shape[0]
    n_in = len(args)
    out_args = [Arg(None, "row", bc, 0, ph) for (_, _, bc, ph) in outs]
    assert all(a.ph or nh == 1 for a in out_args)
    rs = slice(None)

    def body(*refs):
        for h in range(nh):
            res = fn(*[_head_view(r, a, h, rs) for r, a in zip(refs[:n_in], args, strict=True)])
            for r, a, v in zip(refs[n_in:], out_args, res, strict=True):
                r[rs, h * a.bc:(h + 1) * a.bc] = v.astype(r.dtype)

    return pl.pallas_call(
        body, grid=(t // tr,), in_specs=[_arg_spec(a, tr, nh, ntab) for a in args], out_specs=[_arg_spec(a, tr, nh, ntab) for a in out_args],
        out_shape=[jax.ShapeDtypeStruct((t, cols), dt) for (cols, dt, _, _) in outs],
        compiler_params=_cparams(("arbitrary",)), name=name)(*[a.arr for a in args])


def row_vjp_call(name, fn, args, cts, tr, nh=1, ntab=1):
    t = args[0].arr.shape[0]
    n_in, n_ct = len(args), len(cts)
    diff_idx = [k for k, a in enumerate(args) if a.diff]
    def body(*refs):
        out_refs = refs[n_in + n_ct:]
        par_sum = {}
        for k, r in zip(diff_idx, out_refs, strict=True):
            if args[k].kind == "par":
                @pl.when(pl.program_id(0) == 0)
                def _(r=r):
                    r[...] = jnp.zeros_like(r)

        for rs in (slice(None),):
            row_sum = {}
            for h in range(nh):
                vals = [_head_view(r, a, h, rs) for r, a in zip(refs[:n_in], args, strict=True)]
                ct_vals = tuple(_head_view(r, a, h, rs) for r, a in zip(refs[n_in:n_in + n_ct], cts, strict=True))

                def f(*dv, vals=vals):
                    full = list(vals)
                    for k, v in zip(diff_idx, dv, strict=True):
                        full[k] = v
                    return tuple(fn(*full))

                _, vjp = jax.vjp(f, *[vals[k] for k in diff_idx])
                for j, (k, r, g) in enumerate(zip(diff_idx, out_refs, vjp(ct_vals), strict=True)):
                    a = args[k]
                    bc = a.bc or a.arr.shape[1]
                    if a.kind == "row" and a.ph:
                        r[rs, h * bc:(h + 1) * bc] = g.astype(r.dtype)
                    elif a.kind == "row":
                        row_sum[j] = g if j not in row_sum else row_sum[j] + g
                    else:
                        key = (j, h if a.ph else 0)
                        par_sum[key] = g if key not in par_sum else par_sum[key] + g
            for j, g in row_sum.items():
                out_refs[j][rs, :] = g.astype(out_refs[j].dtype)
        for (j, h), g in par_sum.items():
            bc = g.shape[1]
            out_refs[j][:, h * bc:(h + 1) * bc] += g

    out_specs, out_shape = [], []
    for k in diff_idx:
        a = args[k]
        bc = a.bc or a.arr.shape[1]
        out_specs.append(_arg_spec(a, tr, nh, ntab, base=0))
        out_shape.append(jax.ShapeDtypeStruct((t if a.kind == "row" else a.arr.shape[0], bc * (nh if a.ph else 1)), a.gdt if a.kind == "row" else F32))
    in_specs = [_arg_spec(a, tr, nh, ntab) for a in list(args) + list(cts)]
    return pl.pallas_call(
        body, grid=(t // tr,), in_specs=in_specs, out_specs=out_specs, out_shape=out_shape,
        compiler_params=_cparams(("arbitrary",)), name=name)(*[a.arr for a in list(args) + list(cts)])


def _conv_taps(x, w):
    rows = lax.broadcasted_iota(jnp.int32, x.shape, 0)
    y = x * w[CONV_K - 1:CONV_K, :]
    for s in range(1, CONV_K):
        y = y + jnp.where(rows >= s, pltpu.roll(x, s, 0), 0.0) * w[CONV_K - 1 - s:CONV_K - s, :]
    return y


CONV_HEADS = 4
CONV_BLOCKS_PER_THIRD = N_HEADS // CONV_HEADS


def _conv_post(y, block):
    a = _silu(y)
    normed = block < 2 * CONV_BLOCKS_PER_THIRD
    scale = jnp.where(block < CONV_BLOCKS_PER_THIRD, HEAD ** -0.5, 1.0)
    return a * jnp.where(normed, lax.rsqrt(jnp.sum(a * a, -1, keepdims=True) + EPS) * scale, 1.0)


def conv_fwd(z, w, lp):
    t, width = z.shape
    cols = CONV_HEADS * HEAD

    def body(z_ref, w_ref, o_ref, y_ref):
        block = pl.program_id(1)
        for h in range(CONV_HEADS):
            cs = slice(h * HEAD, (h + 1) * HEAD)
            y = _conv_taps(z_ref[:, cs], w_ref[:, cs])
            y_ref[:, cs] = y
            o_ref[:, cs] = _conv_post(y, block)

    blk = pl.BlockSpec((lp, cols), lambda b, j: (b, j))
    out = jax.ShapeDtypeStruct((t, width), F32)
    return pl.pallas_call(
        body, grid=(t // lp, width // cols), in_specs=[blk, pl.BlockSpec((CONV_K, cols), lambda b, j: (0, j))],
        out_specs=[blk, blk], out_shape=[out, out], compiler_params=_cparams(("arbitrary", "arbitrary")), name="a_conv_fwd")(z, w)


def conv_bwd(z, y, w, dout, lp):
    t, width = z.shape
    cols = CONV_HEADS * HEAD

    def body(z_ref, y_ref, w_ref, g_ref, dz_ref, dw_ref):
        block = pl.program_id(0)

        @pl.when(pl.program_id(1) == 0)
        def _():
            dw_ref[...] = jnp.zeros_like(dw_ref)

        for h in range(CONV_HEADS):
            cs = slice(h * HEAD, (h + 1) * HEAD)
            x, wv = z_ref[:, cs], w_ref[:, cs]
            _, vjp = jax.vjp(lambda y_: _conv_post(y_, block), y_ref[:, cs])
            (dy,) = vjp(g_ref[:, cs])
            rows = lax.broadcasted_iota(jnp.int32, x.shape, 0)
            dx = dy * wv[CONV_K - 1:CONV_K, :]
            dw_ref[CONV_K - 1:CONV_K, cs] += jnp.sum(dy * x, axis=0, keepdims=True)
            for s in range(1, CONV_K):
                dy_up = jnp.where(rows < lp - s, pltpu.roll(dy, lp - s, 0), 0.0)
                dx = dx + dy_up * wv[CONV_K - 1 - s:CONV_K - s, :]
                dw_ref[CONV_K - 1 - s:CONV_K - s, cs] += jnp.sum(dy_up * x, axis=0, keepdims=True)
            dz_ref[:, cs] = dx.astype(dz_ref.dtype)

    blk = pl.BlockSpec((lp, cols), lambda j, b: (b, j))
    w_blk = pl.BlockSpec((CONV_K, cols), lambda j, b: (0, j))
    return pl.pallas_call(
        body, grid=(width // cols, t // lp), in_specs=[blk, blk, w_blk, blk], out_specs=[blk, w_blk],
        out_shape=[jax.ShapeDtypeStruct((t, width), _MXU_DTYPE), jax.ShapeDtypeStruct((CONV_K, width), F32)],
        compiler_params=_cparams(("arbitrary", "arbitrary")), name="a_conv_bwd")(z, y, w, dout)


def _delta_chunk(q, k, v, ba, alog, dtb, state, t_stored):
    n_g, c = q.shape[0], q.shape[1]
    lane = lax.broadcasted_iota(jnp.int32, (1, HEAD), 1)

    def pick(xs, offset):
        cols = [jnp.sum(xs[i // N_HEADS if len(xs) > 1 else 0] * (lane == offset + i % N_HEADS).astype(F32), axis=1, keepdims=True)[None]
                for i in range(n_g)]
        return jnp.concatenate(cols, 0)

    b_raw, a_raw = pick(ba, 0), pick(ba, N_HEADS)
    a_log, dt_bias = pick((alog,), 0), pick((dtb,), 0)
    beta = _sigmoid(b_raw)
    g = -jnp.exp(a_log) * _softplus(a_raw + dt_bias)
    ri = lax.broadcasted_iota(jnp.int32, (c, c), 0)
    ci = lax.broadcasted_iota(jnp.int32, (c, c), 1)
    tril = ci <= ri
    lower = jnp.broadcast_to(tril.astype(F32), (n_g, c, c))
    gc_col = _dot_01(lower, g * jnp.ones((1, 1, HEAD), F32))[:, :, :1]
    gc_row = _dot_01(jnp.ones((n_g, 8, c), F32), g * (ri <= ci).astype(F32)[None])[:, 0:1, :]
    gc_last = jnp.sum(g, axis=1, keepdims=True)
    decay = jnp.exp(jnp.where(tril, gc_col - gc_row, NEG))
    e_gc = jnp.exp(gc_col)
    kb = k * beta
    a_mat = jnp.where(ci < ri, mm_nt(kb, k) * decay, 0.0)
    t_inv = _inv_unit_lower(a_mat) if t_stored is None else _inv_lookup(a_mat, t_stored)
    u_base = mm_nn(t_inv, v * beta)
    w_dec = mm_nn(t_inv, kb * e_gc)
    attn = jnp.where(tril, mm_nt(q, k) * decay, 0.0)
    u = u_base - mm_nn(w_dec, state)
    o = mm_nn(q * e_gc, state) + mm_nn(attn, u)
    new_state = state * jnp.exp(gc_last) + mm_tn(k * jnp.exp(gc_last - gc_col), u)
    return o, new_state, t_inv


DELTA_STEP_FWD = (4, 2)
DELTA_STEP_BWD = (2, 2)


def _heads_of(ref, rs, first_col):
    return jnp.stack([ref[i // N_HEADS, rs, first_col + (i % N_HEADS) * HEAD:first_col + (i % N_HEADS + 1) * HEAD]
                      for i in range(ref.shape[0] * N_HEADS)])


def _qkv_heads(ref, rs, part):
    return _heads_of(ref, rs, part * N_HEADS * HEAD)


def _by_sequence(a, lp):
    return a.reshape(a.shape[0] // lp, lp, a.shape[1])


def _ride(bufs, scatter, refs_in, refs_out, sems, first, last, two_level=False):
    if not bufs:
        return lambda: None
    make = lambda: (TwoLevelGather if two_level else Exchange)(refs_in, refs_out, *sems, scatter)

    @pl.when(first)
    def _():
        make().start()

    def finish():
        @pl.when(last)
        def _():
            make().wait()

    return finish


def delta_fwd(qkv, ba, ba_block, alog, dtb, lp, gather=()):
    t = qkv.shape[0]
    nb, nc = t // lp, lp // CHUNK
    seqs, cps = DELTA_STEP_FWD
    ng, rows = nc // cps, cps * CHUNK
    nx = len(gather)
    nbg = nb // seqs
    assert nc % cps == 0 and nb % seqs == 0

    def body(*refs):
        qkv_ref, ba_ref, al_ref, dt_ref = refs[:4]
        o_ref, s_ref, t_ref = refs[4 + nx:7 + nx]
        state_ref = refs[7 + 2 * nx]
        b, n = pl.program_id(0), pl.program_id(1)
        finish = _ride(gather, False, refs[4:4 + nx], refs[7 + nx:7 + 2 * nx], refs[8 + 2 * nx:], (b == 0) & (n == 0), (b == nbg - 1) & (n == ng - 1))

        @pl.when(n == 0)
        def _():
            state_ref[...] = jnp.zeros_like(state_ref)

        al, dtv = al_ref[...], dt_ref[...]
        for c in range(cps):
            rs = slice(c * CHUNK, (c + 1) * CHUNK)
            state = state_ref[...]
            o, new_state, t_inv = _delta_chunk(_qkv_heads(qkv_ref, rs, 0), _qkv_heads(qkv_ref, rs, 1), _qkv_heads(qkv_ref, rs, 2),
                                               tuple(ba_ref[i, rs, :] for i in range(seqs)), al, dtv, state, None)
            for i in range((seqs * N_HEADS)):
                seq, g = divmod(i, N_HEADS)
                o_ref[seq, rs, g * HEAD:(g + 1) * HEAD] = o[i]
                s_ref[seq, g, c] = state[i]
                t_ref[seq, g, c] = t_inv[i]
            state_ref[...] = new_state
        finish()

    rows_of = lambda width: pl.BlockSpec((seqs, rows, width), lambda b, n: (b, n, 0))
    par_spec = pl.BlockSpec((1, HEAD), lambda b, n: (0, 0))
    out = pl.pallas_call(
        body, grid=(nbg, ng),
        in_specs=[rows_of(3 * N_HEADS * HEAD), pl.BlockSpec((seqs, rows, HEAD), lambda b, n: (b, n, ba_block)), par_spec, par_spec] + [_HBM] * nx,
        out_specs=[rows_of(N_HEADS * HEAD), pl.BlockSpec((seqs, N_HEADS, cps, HEAD, HEAD), lambda b, n: (b, 0, n, 0, 0)),
                   pl.BlockSpec((seqs, N_HEADS, cps, CHUNK, CHUNK), lambda b, n: (b, 0, n, 0, 0))] + [_HBM] * nx,
        out_shape=[jax.ShapeDtypeStruct((nb, lp, N_HEADS * HEAD), F32), jax.ShapeDtypeStruct((nb, N_HEADS, nc, HEAD, HEAD), F32),
                   jax.ShapeDtypeStruct((nb, N_HEADS, nc, CHUNK, CHUNK), F32)] + Exchange.out_shape(gather, False),
        scratch_shapes=[pltpu.VMEM(((seqs * N_HEADS), HEAD, HEAD), F32)] + (Exchange.scratch(nx) if nx else []),
        compiler_params=_cparams(("arbitrary", "arbitrary")), name="delta_fwd")(_by_sequence(qkv, lp), _by_sequence(ba, lp), alog, dtb, *gather)
    return [out[0].reshape(t, N_HEADS * HEAD)] + list(out[1:])


def delta_bwd(qkv, ba, ba_block, alog, dtb, states, t_invs, do, lp, scatter=()):
    t = qkv.shape[0]
    nb, nc = t // lp, lp // CHUNK
    seqs, cps = DELTA_STEP_BWD
    ng, rows = nc // cps, cps * CHUNK
    nx = len(scatter)
    nbg = nb // seqs

    def body(*refs):
        qkv_ref, ba_ref, al_ref, dt_ref, s_ref, t_ref, do_ref = refs[:7]
        dqkv_ref, dba_ref, dal_ref, ddt_ref = refs[7 + nx:11 + nx]
        dstate_ref = refs[11 + 2 * nx]
        b, step = pl.program_id(0), pl.program_id(1)
        finish = _ride(scatter, True, refs[7:7 + nx], refs[11 + nx:11 + 2 * nx], refs[12 + 2 * nx:], (b == 0) & (step == 0),
                       (b == nbg - 1) & (step == ng - 1))

        @pl.when(step == 0)
        def _():
            dstate_ref[...] = jnp.zeros_like(dstate_ref)

        @pl.when((b == 0) & (step == 0))
        def _():
            dal_ref[...] = jnp.zeros_like(dal_ref)
            ddt_ref[...] = jnp.zeros_like(ddt_ref)

        al, dtv = al_ref[...], dt_ref[...]
        d_al = jnp.zeros((1, HEAD), F32)
        d_dt = jnp.zeros((1, HEAD), F32)
        for c in reversed(range(cps)):
            rs = slice(c * CHUNK, (c + 1) * CHUNK)
            t_n = jnp.stack([t_ref[i // N_HEADS, i % N_HEADS, c] for i in range((seqs * N_HEADS))])
            s_n = jnp.stack([s_ref[i // N_HEADS, i % N_HEADS, c] for i in range((seqs * N_HEADS))])

            def f(q_, k_, v_, ba_, al_, dt_, s_, t_n=t_n):
                return _delta_chunk(q_, k_, v_, ba_, al_, dt_, s_, t_n)[:2]

            _, vjp = jax.vjp(f, _qkv_heads(qkv_ref, rs, 0), _qkv_heads(qkv_ref, rs, 1), _qkv_heads(qkv_ref, rs, 2), tuple(ba_ref[i, rs, :] for i in range(seqs)), al, dtv, s_n)
            grads = vjp((_heads_of(do_ref, rs, 0), dstate_ref[...]))
            for part in range(3):
                for i in range((seqs * N_HEADS)):
                    col = (part * N_HEADS + i % N_HEADS) * HEAD
                    dqkv_ref[i // N_HEADS, rs, col:col + HEAD] = grads[part][i]
            for i in range(seqs):
                dba_ref[i, rs, :] = grads[3][i]
            d_al, d_dt = d_al + grads[4], d_dt + grads[5]
            dstate_ref[...] = grads[6]
        dal_ref[...] += d_al
        ddt_ref[...] += d_dt
        finish()

    rows_of = lambda width: pl.BlockSpec((seqs, rows, width), lambda b, n: (b, ng - 1 - n, 0))
    par_spec = pl.BlockSpec((1, HEAD), lambda b, n: (0, 0))
    out = pl.pallas_call(
        body, grid=(nbg, ng),
        in_specs=[rows_of(3 * N_HEADS * HEAD), pl.BlockSpec((seqs, rows, HEAD), lambda b, n: (b, ng - 1 - n, ba_block)), par_spec, par_spec,
                  pl.BlockSpec((seqs, N_HEADS, cps, HEAD, HEAD), lambda b, n: (b, 0, ng - 1 - n, 0, 0)),
                  pl.BlockSpec((seqs, N_HEADS, cps, CHUNK, CHUNK), lambda b, n: (b, 0, ng - 1 - n, 0, 0)), rows_of(N_HEADS * HEAD)] + [_HBM] * nx,
        out_specs=[rows_of(3 * N_HEADS * HEAD), rows_of(HEAD), par_spec, par_spec] + [_HBM] * nx,
        out_shape=[jax.ShapeDtypeStruct((nb, lp, 3 * N_HEADS * HEAD), F32), jax.ShapeDtypeStruct((nb, lp, HEAD), F32),
                   jax.ShapeDtypeStruct((1, HEAD), F32), jax.ShapeDtypeStruct((1, HEAD), F32)] + Exchange.out_shape(scatter, True),
        scratch_shapes=[pltpu.VMEM(((seqs * N_HEADS), HEAD, HEAD), F32)] + (Exchange.scratch(nx) if nx else []),
        compiler_params=_cparams(("arbitrary", "arbitrary")), name="delta_bwd")(
            _by_sequence(qkv, lp), _by_sequence(ba, lp), alog, dtb, states, t_invs, _by_sequence(do, lp), *scatter)
    return [out[0].reshape(t, 3 * N_HEADS * HEAD), out[1].reshape(t, HEAD)] + list(out[2:])


ATT_Q_TILE = 256
ATT_K_TILE = 512
ATT_SCALE = QK_DIM ** -0.5


def _tiles(end, size):
    return [(s, min(s + size, end)) for s in range(0, end, size)]


def _att_visible(q0, q1, k0, k1, keys_first):
    if k1 <= q0 + CHUNK and k0 >= PAD_ROWS:
        return None
    shape = (k1 - k0, q1 - q0) if keys_first else (q1 - q0, k1 - k0)
    qpos = q0 + lax.broadcasted_iota(jnp.int32, shape, 1 if keys_first else 0)
    kpos = k0 + lax.broadcasted_iota(jnp.int32, shape, 0 if keys_first else 1)
    shift = CHUNK.bit_length() - 1
    return (jnp.right_shift(kpos, shift) <= jnp.right_shift(qpos, shift)) & (kpos >= PAD_ROWS)


def _att_seq_specs(lp):
    return pl.BlockSpec((lp, QK_PAD), lambda b, h: (b, h)), pl.BlockSpec((lp, HEAD), lambda b, h: (b, h))


def flash_fwd(q, k, v, lp):
    t = q.shape[0]
    qk_seq, o_seq = _att_seq_specs(lp)

    def body(q_ref, k_ref, v_ref, o_ref, lse_ref):
        q_tiles = _tiles(lp, ATT_Q_TILE)

        def score_steps(q0, q1, out):
            def step(k0, k1):
                s = mm_nt(q_ref[q0:q1, :], k_ref[k0:k1, :])
                vis = _att_visible(q0, q1, k0, k1, False)
                s = s if vis is None else jnp.where(vis, s, NEG)
                out["scores"].append(s)
                row_max = jnp.max(s, -1, keepdims=True)
                out["m"] = row_max if out["m"] is None else jnp.maximum(out["m"], row_max)
            return [functools.partial(step, k0, k1) for k0, k1 in _tiles(q1, ATT_K_TILE)]

        cur = {"scores": [], "m": None}
        for step in score_steps(*q_tiles[0], cur):
            step()
        for i, (q0, q1) in enumerate(q_tiles):
            nxt = {"scores": [], "m": None}
            ahead = score_steps(*q_tiles[i + 1], nxt) if i + 1 < len(q_tiles) else []
            l = jnp.zeros((q1 - q0, 1), F32)
            acc = jnp.zeros((q1 - q0, HEAD), F32)
            for s, (k0, k1) in zip(cur["scores"], _tiles(q1, ATT_K_TILE), strict=True):
                if ahead:
                    ahead.pop(0)()
                p = jnp.exp(s - cur["m"])
                l = l + jnp.sum(p, -1, keepdims=True)
                acc = acc + mm_nn(p, v_ref[k0:k1, :])
            for step in ahead:
                step()
            o_ref[q0:q1, :] = acc / l
            lse_ref[q0:q1, :] = jnp.broadcast_to(cur["m"] + jnp.log(l), (q1 - q0, HEAD))
            cur = nxt

    big = jax.ShapeDtypeStruct((t, N_HEADS * HEAD), F32)
    return pl.pallas_call(
        body, grid=(t // lp, N_HEADS), in_specs=[qk_seq, qk_seq, o_seq], out_specs=[o_seq, o_seq], out_shape=[big, big],
        compiler_params=_cparams(("arbitrary", "arbitrary")), name="flash_fwd")(q, k, v)


def flash_bwd(q, k, v, o, lse, do, lp):
    t = q.shape[0]
    qk_seq, o_seq = _att_seq_specs(lp)

    def body(q_ref, k_ref, v_ref, o_ref, lse_ref, do_ref, dq_ref, dk_out_ref, dv_out_ref, dk_ref, dv_ref):
        dk_ref[...] = jnp.zeros_like(dk_ref)
        dv_ref[...] = jnp.zeros_like(dv_ref)
        for q0, q1 in _tiles(lp, ATT_Q_TILE):
            qb, dob = q_ref[q0:q1, :], do_ref[q0:q1, :]
            lse_row = jnp.transpose(lse_ref[q0:q1, :])[0:1, :]
            dsum_row = jnp.sum(jnp.transpose(dob * o_ref[q0:q1, :]), axis=0, keepdims=True)
            dq = jnp.zeros((q1 - q0, QK_PAD), F32)
            for k0, k1 in _tiles(q1, ATT_K_TILE):
                kb, vb = k_ref[k0:k1, :], v_ref[k0:k1, :]
                s = mm_nt(kb, qb)
                vis = _att_visible(q0, q1, k0, k1, True)
                s = s if vis is None else jnp.where(vis, s, NEG)
                p = jnp.exp(s - lse_row)
                ds = p * (mm_nt(vb, dob) - dsum_row)
                dv_ref[k0:k1, :] += mm_nn(p, dob)
                dk_ref[k0:k1, :] += mm_nn(ds, qb)
                dq = dq + mm_tn(ds, kb)
            dq_ref[q0:q1, :] = dq.astype(dq_ref.dtype)
        dk_out_ref[...] = dk_ref[...].astype(dk_out_ref.dtype)
        dv_out_ref[...] = dv_ref[...].astype(dv_out_ref.dtype)

    narrow = _MXU_DTYPE
    return pl.pallas_call(
        body, grid=(t // lp, N_HEADS), in_specs=[qk_seq, qk_seq, o_seq, o_seq, o_seq, o_seq], out_specs=[qk_seq, qk_seq, o_seq],
        out_shape=[jax.ShapeDtypeStruct((t, N_HEADS * QK_PAD), narrow), jax.ShapeDtypeStruct((t, N_HEADS * QK_PAD), narrow),
                   jax.ShapeDtypeStruct((t, N_HEADS * HEAD), narrow)],
        scratch_shapes=[pltpu.VMEM((lp, QK_PAD), F32), pltpu.VMEM((lp, HEAD), F32)],
        compiler_params=_cparams(("arbitrary", "arbitrary")), name="flash_bwd")(q, k, v, o, lse, do)


def loss_head(h2, target, lp):
    nb, seq, d = target.shape
    cols = _pick(d, (512, 128))
    ncol = d // cols

    def body(h_ref, t_ref, loss_ref, dh_ref, acc_ref):
        b, j = pl.program_id(0), pl.program_id(1)

        @pl.when((b == 0) & (j == 0))
        def _():
            acc_ref[...] = jnp.zeros_like(acc_ref)

        err = h_ref[LEAD:, :] - t_ref[...]
        dh_ref[:LEAD, :] = jnp.zeros((LEAD, cols), F32)
        dh_ref[LEAD:, :] = err * (1.0 / d)
        acc_ref[...] += jnp.sum(err * err, axis=0, keepdims=True)

        @pl.when((b == nb - 1) & (j == ncol - 1))
        def _():
            loss_ref[...] = jnp.sum(acc_ref[...], axis=1, keepdims=True) * (0.5 / d)

    return pl.pallas_call(
        body, grid=(nb, ncol),
        in_specs=[pl.BlockSpec((None, lp, cols), lambda b, j: (b, 0, j)), pl.BlockSpec((None, seq, cols), lambda b, j: (b, 0, j))],
        out_specs=[pl.BlockSpec((1, 1), lambda b, j: (0, 0)), pl.BlockSpec((None, lp, cols), lambda b, j: (b, 0, j))],
        out_shape=[jax.ShapeDtypeStruct((1, 1), F32), jax.ShapeDtypeStruct((nb, lp, d), F32)],
        scratch_shapes=[pltpu.VMEM((1, cols), F32)], compiler_params=_cparams(("arbitrary", "arbitrary")), name="loss_head")(h2, target)


def gated_out(name, o, gate, gain, w, res):
    t, kw = o.shape
    d = w.shape[1]
    tm = _pick(t, (512, 256, 128))

    def body(*refs):
        o_ref, gate_ref = refs[:2]
        w_ref, r_ref, h_ref, g_ref = refs[-4:]
        if gain is None:
            g_ref[...] = _f_gate(o_ref[...], gate_ref[...])[0].astype(g_ref.dtype)
        else:
            for h in range(N_HEADS):
                cs = slice(h * HEAD, (h + 1) * HEAD)
                g_ref[:, cs] = _f_out_gate(o_ref[:, cs], gate_ref[:, cs], refs[2][...])[0].astype(g_ref.dtype)
        h_ref[...] = r_ref[...] + _dot(g_ref[...], w_ref[...], ((1,), (0,)))

    rows = lambda width: pl.BlockSpec((tm, width), lambda i: (i, 0))
    whole = lambda a: pl.BlockSpec(a.shape, lambda i: (0, 0))
    params = [] if gain is None else [gain]
    return pl.pallas_call(
        body, grid=(t // tm,), in_specs=[rows(kw), rows(kw)] + [whole(p) for p in params] + [whole(w), rows(d)], out_specs=[rows(d), rows(kw)],
        out_shape=[jax.ShapeDtypeStruct((t, d), F32), jax.ShapeDtypeStruct((t, kw), _MXU_DTYPE)],
        compiler_params=_cparams(("parallel",)), name=name)(o, gate, *params, w, res)


def embed_norm(x, meta, gain, lp, gather=()):
    nb, seq, d = x.shape
    nblk, nx = lp // LEAD, len(gather)

    def body(*refs):
        x_ref, meta_ref, g_ref = refs[:3]
        h_ref, hn_ref = refs[3 + nx:5 + nx]
        b, i = pl.program_id(0), pl.program_id(1)
        finish = _ride(gather, False, refs[3:3 + nx], refs[5 + nx:5 + 2 * nx], refs[5 + 2 * nx:], (b == 0) & (i == 0), (b == nb - 1) & (i == nblk - 1),
                       two_level=True)

        @pl.when(i == 0)
        def _():
            h_ref[:PAD_ROWS, :] = jnp.zeros((PAD_ROWS, d), F32)
            h_ref[PAD_ROWS:, :] = meta_ref[...]

        @pl.when(i > 0)
        def _():
            h_ref[...] = x_ref[...]

        hn_ref[...] = _rms(h_ref[...], g_ref[...]).astype(hn_ref.dtype)
        finish()

    rows = pl.BlockSpec((LEAD, d), lambda b, i: (b * nblk + i, 0))
    out = pl.pallas_call(
        body, grid=(nb, nblk),
        in_specs=[pl.BlockSpec((None, LEAD, d), lambda b, i: (b, jnp.maximum(i - 1, 0), 0)), pl.BlockSpec((N_META, d), lambda b, i: (0, 0)),
                  pl.BlockSpec((1, d), lambda b, i: (0, 0))] + [_HBM] * nx,
        out_specs=[rows, rows] + [_HBM] * nx,
        out_shape=[jax.ShapeDtypeStruct((nb * lp, d), F32), jax.ShapeDtypeStruct((nb * lp, d), _MXU_DTYPE)] + Exchange.out_shape(gather, False),
        scratch_shapes=Exchange.scratch(nx) if nx else [],
        compiler_params=_cparams(("arbitrary", "arbitrary")), name="embed_norm")(x, meta, gain, *gather)
    return list(out)


def meta_grad(dh0):
    nb, _, d = dh0.shape

    def body(g_ref, o_ref):
        @pl.when(pl.program_id(0) == 0)
        def _():
            o_ref[...] = jnp.zeros_like(o_ref)

        o_ref[...] += g_ref[PAD_ROWS:LEAD, :]

    return pl.pallas_call(
        body, grid=(nb,), in_specs=[pl.BlockSpec((None, LEAD, d), lambda b: (b, 0, 0))],
        out_specs=pl.BlockSpec((N_META, d), lambda b: (0, 0)), out_shape=jax.ShapeDtypeStruct((N_META, d), F32),
        compiler_params=_cparams(("arbitrary",)), name="meta_grad")(dh0)


_HBM = pl.BlockSpec(memory_space=pltpu.HBM)


def _mesh_pos():
    x, y, c = lax.axis_index("x"), lax.axis_index("y"), lax.axis_index("c")
    return x, y, c


def _peer(x, y, c, k):
    px = 1 - x if k & 4 else x
    py = 1 - y if k & 2 else y
    pc = 1 - c if k & 1 else c
    return (px, py, pc), 4 * px + 2 * py + pc


class Exchange:
    def __init__(self, x_refs, out_refs, send_sems, recv_sems, local_sems, scatter):
        self.x_refs, self.out_refs, self.scatter = x_refs, out_refs, scatter
        self.send_sems, self.recv_sems, self.local_sems = send_sems, recv_sems, local_sems
        self.pos = _mesh_pos()
        x, y, c = self.pos
        self.me = 4 * x + 2 * y + c

    @staticmethod
    def scratch(n):
        return [pltpu.SemaphoreType.DMA((n, N_DEV - 1)), pltpu.SemaphoreType.DMA((n, N_DEV - 1)), pltpu.SemaphoreType.DMA((n,))]

    @staticmethod
    def out_shape(bufs, scatter):
        return [jax.ShapeDtypeStruct(b.shape if scatter else (N_DEV,) + b.shape, b.dtype) for b in bufs]

    def _local(self, i):
        return pltpu.make_async_copy(self.x_refs[i].at[self.me] if self.scatter else self.x_refs[i], self.out_refs[i].at[self.me], self.local_sems.at[i])

    def _copy(self, i, k, landing):
        peer, peer_id = _peer(*self.pos, k)
        src = self.x_refs[i].at[peer_id] if self.scatter else self.x_refs[i]
        return pltpu.make_async_remote_copy(src_ref=src, dst_ref=self.out_refs[i].at[peer_id if landing else self.me],
                                            send_sem=self.send_sems.at[i, k - 1], recv_sem=self.recv_sems.at[i, k - 1],
                                            device_id=peer, device_id_type=pl.DeviceIdType.MESH)

    def start(self):
        for i in range(len(self.x_refs)):
            self._local(i).start()
        for k in range(1, N_DEV):
            for i in range(len(self.x_refs)):
                self._copy(i, k, False).start()

    def wait(self):
        for k in range(1, N_DEV):
            for i in range(len(self.x_refs)):
                self._copy(i, k, True).wait_recv()
        for k in range(1, N_DEV):
            for i in range(len(self.x_refs)):
                self._copy(i, k, False).wait_send()
        for i in range(len(self.x_refs)):
            self._local(i).wait()


class TwoLevelGather(Exchange):
    DIRECT = (1, 4, 2, 6)
    FROM_CHIPS = (4, 2, 6)

    def _forward(self, i, k):
        _, origin = _peer(*self.pos, k)
        sibling, _ = _peer(*self.pos, 1)
        block = self.out_refs[i].at[origin]
        return pltpu.make_async_remote_copy(src_ref=block, dst_ref=block, send_sem=self.send_sems.at[i, (k ^ 1) - 1],
                                            recv_sem=self.recv_sems.at[i, (k ^ 1) - 1], device_id=sibling, device_id_type=pl.DeviceIdType.MESH)

    def start(self):
        assert not self.scatter
        for i in range(len(self.x_refs)):
            self._local(i).start()
        for k in self.DIRECT:
            for i in range(len(self.x_refs)):
                self._copy(i, k, False).start()

    def wait(self):
        n = range(len(self.x_refs))
        for k in self.FROM_CHIPS:
            for i in n:
                self._copy(i, k, True).wait_recv()
                self._forward(i, k).start()
        for k in (1, 5, 3, 7):
            for i in n:
                self._copy(i, k, True).wait_recv()
        for k in self.DIRECT:
            for i in n:
                self._copy(i, k, False).wait_send()
        for k in self.FROM_CHIPS:
            for i in n:
                self._forward(i, k).wait_send()
        for i in n:
            self._local(i).wait()


def _exchange(name, bufs, scatter):
    n = len(bufs)

    def body(*refs):
        ex = Exchange(refs[:n], refs[n:2 * n], *refs[2 * n:], scatter)
        ex.start()
        ex.wait()

    return pl.pallas_call(body, in_specs=[_HBM] * n, out_specs=[_HBM] * n, out_shape=Exchange.out_shape(bufs, scatter),
                          scratch_shapes=Exchange.scratch(n), name=name)(*bufs)


def _f_rms(x, g):
    return (_rms(x, g),)


def _f_rms2(x, g1, g2):
    r = x * lax.rsqrt(jnp.sum(x * x, -1, keepdims=True) / x.shape[-1] + EPS)
    return r * g1, r * g2


def _f_out_gate(o, gate, gain):
    return (_rms(o, gain) * _silu(gate),)


def _f_gate(o, gate):
    return (o * _silu(gate),)


def _swap_rope_halves(x):
    return pltpu.roll(x, ROPE // 2, 1) + pltpu.roll(x, HEAD - ROPE // 2, 1)


def _qk_final_inv_rms(nope, rope_in):
    ms = (jnp.sum(nope * nope, -1, keepdims=True) + jnp.sum(rope_in * rope_in, -1, keepdims=True)) / QK_DIM
    return lax.rsqrt(ms + EPS)


@functools.partial(jax.custom_vjp, nondiff_argnums=(0,))
def _qk_final(scale, nope, rope_in, g_nope, g_rope, cos, sin):
    r = _qk_final_inv_rms(nope, rope_in)
    b = rope_in * (r * g_rope)
    out = jnp.concatenate([nope * (r * g_nope), b * cos + _swap_rope_halves(b) * sin], axis=1)
    return out if scale == 1.0 else out * scale


def _qk_final_fwd(scale, nope, rope_in, g_nope, g_rope, cos, sin):
    return _qk_final(scale, nope, rope_in, g_nope, g_rope, cos, sin), (nope, rope_in, g_nope, g_rope, cos, sin)


def _qk_final_bwd(scale, res, g):
    nope, rope_in, g_nope, g_rope, cos, sin = res
    r = _qk_final_inv_rms(nope, rope_in)
    ga, gb = g[:, :HEAD], g[:, HEAD:]
    if scale != 1.0:
        ga, gb = ga * scale, gb * scale
    db = gb * cos + _swap_rope_halves(gb * sin)
    t_a, t_b = ga * nope, db * rope_in
    d_r = jnp.sum(t_a * g_nope + t_b * g_rope, -1, keepdims=True)
    c = d_r * (r * r * r) * (-1.0 / QK_DIM)
    d_nope = ga * (r * g_nope) + nope * c
    d_rope = db * (r * g_rope) + rope_in * c
    d_g_nope = jnp.sum(t_a * r, 0, keepdims=True)
    d_g_rope = jnp.sum(t_b * r, 0, keepdims=True)
    return d_nope, d_rope, d_g_nope, d_g_rope, jnp.zeros_like(cos), jnp.zeros_like(sin)


_qk_final.defvjp(_qk_final_fwd, _qk_final_bwd)


def _f_qk_final(scale, nope, rope_in, g_nope, g_rope, cos, sin):
    return (_qk_final(scale, nope, rope_in, g_nope, g_rope, cos, sin),)


def _rope_tables(lp):
    half = ROPE // 2
    pos = jnp.maximum(jnp.arange(lp) - PAD_ROWS, 0)
    inv = ROPE_THETA ** (-jnp.arange(half, dtype=F32) / half)
    ang = pos.astype(F32)[:, None] * inv[None, :]
    zeros = jnp.zeros((lp, HEAD - ROPE), F32)
    cos = jnp.concatenate([jnp.cos(ang), jnp.cos(ang), zeros], 1)
    sin = jnp.concatenate([-jnp.sin(ang), jnp.sin(ang), zeros], 1)
    return cos, sin


def _pad_lanes(w, width=HEAD):
    return jnp.pad(w, ((0, 0), (0, width - w.shape[1])))


def _pad_rows(w, rows=HEAD):
    return jnp.pad(w, ((0, rows - w.shape[0]), (0, 0)))


def _split_heads_qk_t(w_t):
    k = w_t.shape[1]
    w3 = w_t.reshape(N_HEADS, QK_DIM, k)
    nope = w3[:, :HEAD].reshape(N_HEADS * HEAD, k)
    rope = jnp.pad(w3[:, HEAD:], ((0, 0), (0, HEAD - ROPE), (0, 0))).reshape(N_HEADS * HEAD, k)
    return jnp.concatenate([nope, rope], 0)


def _merge_heads_qk_t(g_t):
    k = g_t.shape[1]
    kw = N_HEADS * HEAD
    nope, rope = g_t[:kw].reshape(N_HEADS, HEAD, k), g_t[kw:].reshape(N_HEADS, HEAD, k)[:, :ROPE]
    return jnp.concatenate([nope, rope], 1).reshape(N_HEADS * QK_DIM, k)


def local_step(x, target, w, deferred=None):
    nb, seq, d = x.shape
    lp = seq + LEAD
    t = nb * lp
    tr = _pick(lp, (544, 128))
    ntab = lp // tr
    mxu = _MXU_DTYPE
    kw = N_HEADS * HEAD

    a_conv = w["a_conv"].T
    alog, dtb, o_gain = _pad_lanes(w["a_log"]), _pad_lanes(w["a_dt_bias"]), w["a_o_gain"]
    a_norm, kv_norm, b_norm = w["a_norm"], w["kv_norm"][None, :], w["b_norm"]
    lat_norm, qlat_norm = w["kv_latent_norm"][None, :], w["b_q_latent_norm"]
    kg_nope, kg_rope = w["k_gain"][None, :HEAD], _pad_lanes(w["k_gain"][None, HEAD:])
    qg_nope, qg_rope = w["b_q_gain"][:, :HEAD], _pad_lanes(w["b_q_gain"][:, HEAD:])
    cos, sin = _rope_tables(lp)

    h0, hn, *gathered = embed_norm(x, w["meta_tokens"].T, a_norm, lp, gather=deferred.first_gather_bufs if deferred else ())
    if deferred:
        w = {**w, **deferred.finish_first(gathered)}
    a_w_in_t = w["a_w_in"].astype(mxu)
    w_qkv_t, w_gba_t = a_w_in_t[:3 * kw], _pad_rows(a_w_in_t[3 * kw:], kw + HEAD)
    z_qkv = matmul("a_in_qkv", hn, w_qkv_t, "nt")
    z_gba = matmul("a_in_gate_ba", hn, w_gba_t, "nt")
    ba_block = kw // HEAD
    qkv_a, y_conv = conv_fwd(z_qkv, a_conv, lp)
    o_a, states, t_invs, *gathered = delta_fwd(qkv_a, z_gba, ba_block, alog, dtb, lp, gather=deferred.gather_bufs if deferred else ())
    if deferred:
        w = {**w, **deferred.finish(gathered)}
    a_w_out = w["a_w_out"].astype(mxu)
    w_down = _pad_lanes(w["kv_w_down"], KV_RANK + HEAD).astype(mxu)
    w_ukv_t = jnp.concatenate([w["kv_w_uk"], w["kv_w_uv"]], 0).astype(mxu)
    b_w_in_t = w["b_w_in"].astype(mxu)
    w_cq_t, w_gb_t = b_w_in_t[:Q_RANK], b_w_in_t[Q_RANK:]
    w_q_t = _split_heads_qk_t(w["b_w_uq"]).astype(mxu)
    b_w_out = w["b_w_out"].astype(mxu)
    og_args = [Arg(o_a, bc=HEAD, ph=True, diff=True), Arg(z_gba, bc=HEAD, ph=True, diff=True, gdt=mxu), Arg(o_gain, "par", diff=True)]
    h1, og_a = gated_out("a_out", o_a, z_gba, o_gain, a_w_out, h0)

    hk, hb = row_call("b_norms_fwd", _f_rms2, [Arg(h1), Arg(kv_norm, "par"), Arg(b_norm, "par")], [(d, mxu, d, False), (d, mxu, d, False)], tr)
    c_down = matmul("kv_down", hk, w_down, "nn")
    c_kv_arg = Arg(c_down, bc=KV_RANK, diff=True, gdt=mxu)
    k_pe_arg = Arg(c_down, bc=HEAD, base=KV_RANK // HEAD, diff=True)
    c_q_raw = matmul("b_in_q", hb, w_cq_t, "nt")
    gate_b = matmul("b_in_gate", hb, w_gb_t, "nt")
    (c_kv,) = row_call("kv_latent_fwd", _f_rms, [c_kv_arg, Arg(lat_norm, "par")], [(KV_RANK, mxu, KV_RANK, False)], tr)
    (c_q,) = row_call("q_latent_fwd", _f_rms, [Arg(c_q_raw), Arg(qlat_norm, "par")], [(Q_RANK, mxu, Q_RANK, False)], tr)
    k_nope = matmul("k_up", c_kv, w_ukv_t[:kw], "nt")
    v_b = matmul("v_up", c_kv, w_ukv_t[kw:], "nt", out_dtype=mxu)
    q_up = matmul("q_up", c_q, w_q_t, "nt")
    tabs = [Arg(cos, "tab"), Arg(sin, "tab")]
    k_args = [Arg(k_nope, bc=HEAD, ph=True, diff=True, gdt=mxu), k_pe_arg, Arg(kg_nope, "par", diff=True), Arg(kg_rope, "par", diff=True)] + tabs
    q_args = [Arg(q_up, bc=HEAD, ph=True, diff=True, gdt=mxu), Arg(q_up, bc=HEAD, base=N_HEADS, ph=True, diff=True, gdt=mxu),
              Arg(qg_nope, "par", diff=True), Arg(qg_rope, "par", diff=True)] + tabs
    f_k_final, f_q_final = functools.partial(_f_qk_final, 1.0), functools.partial(_f_qk_final, ATT_SCALE)
    (k_fin,) = row_call("k_final_fwd", f_k_final, k_args, [(N_HEADS * QK_PAD, mxu, QK_PAD, True)], tr, nh=N_HEADS, ntab=ntab)
    (q_fin,) = row_call("q_final_fwd", f_q_final, q_args, [(N_HEADS * QK_PAD, mxu, QK_PAD, True)], tr, nh=N_HEADS, ntab=ntab)
    o_b, lse = flash_fwd(q_fin, k_fin, v_b, lp)
    gb_args = [Arg(o_b, diff=True), Arg(gate_b, diff=True, gdt=mxu)]
    h2, og_b = gated_out("b_out", o_b, gate_b, None, b_w_out, h1)

    loss, dh2 = loss_head(h2.reshape(nb, lp, d), target, lp)
    dh2 = dh2.reshape(t, d)
    grads = {}

    d_og_b = matmul("b_out_dx", dh2, b_w_out, "nt", out_dtype=mxu)
    grads["b_w_out"] = matmul("b_out_dw", og_b, dh2, "tn")
    d_o_b, d_gate_b = row_vjp_call("b_gate_bwd", _f_gate, gb_args, [Arg(d_og_b)], tr)
    dq_fin, dk_fin, dv_b = flash_bwd(q_fin, k_fin, v_b, o_b, lse, d_o_b, lp)
    dq_nope, dq_rope, d_qg_nope, d_qg_rope = row_vjp_call(
        "q_final_bwd", f_q_final, q_args, [Arg(dq_fin, bc=QK_PAD, ph=True)], tr, nh=N_HEADS, ntab=ntab)
    dk_nope, dk_pe, d_kg_nope, d_kg_rope = row_vjp_call(
        "k_final_bwd", f_k_final, k_args, [Arg(dk_fin, bc=QK_PAD, ph=True)], tr, nh=N_HEADS, ntab=ntab)
    grads["b_q_gain"] = jnp.concatenate([d_qg_nope, d_qg_rope[:, :ROPE]], 1)
    grads["k_gain"] = jnp.concatenate([d_kg_nope, d_kg_rope[:, :ROPE]], 1)[0]
    d_c_q = matmul("q_nope_dx", dq_nope, w_q_t[:kw], "nn")
    d_c_q = matmul("q_rope_dx", dq_rope, w_q_t[kw:], "nn", res=d_c_q)
    grads["b_w_uq"] = _merge_heads_qk_t(jnp.concatenate([matmul("q_nope_dw", dq_nope, c_q, "tn"), matmul("q_rope_dw", dq_rope, c_q, "tn")], 0))
    d_c_kv = matmul("k_up_dx", dk_nope, w_ukv_t[:kw], "nn")
    d_c_kv = matmul("v_up_dx", dv_b, w_ukv_t[kw:], "nn", res=d_c_kv)
    grads["kv_w_uk"], grads["kv_w_uv"] = matmul("k_up_dw", dk_nope, c_kv, "tn"), matmul("v_up_dw", dv_b, c_kv, "tn")
    d_c_q_raw, grads["b_q_latent_norm"] = row_vjp_call(
        "q_latent_bwd", _f_rms, [Arg(c_q_raw, diff=True, gdt=mxu), Arg(qlat_norm, "par", diff=True)], [Arg(d_c_q)], tr)
    d_c_kv_raw, d_lat = row_vjp_call(
        "kv_latent_bwd", _f_rms, [c_kv_arg, Arg(lat_norm, "par", diff=True)], [Arg(d_c_kv)], tr)
    grads["kv_latent_norm"] = d_lat[0]
    d_hb = matmul("b_in_q_dx", d_c_q_raw, w_cq_t, "nn")
    d_hb = matmul("b_in_gate_dx", d_gate_b, w_gb_t, "nn", res=d_hb, out_dtype=mxu)
    grads["b_w_in"] = jnp.concatenate([matmul("b_in_q_dw", d_c_q_raw, hb, "tn"), matmul("b_in_gate_dw", d_gate_b, hb, "tn")], 0)
    d_c_down = jnp.concatenate([d_c_kv_raw, dk_pe.astype(mxu)], 1)
    d_hk = matmul("kv_down_dx", d_c_down, w_down, "nt", out_dtype=mxu)
    grads["kv_w_down"] = matmul("kv_down_dw", hk, d_c_down, "tn")[:, :KV_RANK + ROPE]
    dh1, d_kv_norm, grads["b_norm"] = row_vjp_call(
        "b_norms_bwd", lambda x_, g1, g2: _f_rms2(x_, g1, g2) + (x_,),
        [Arg(h1, diff=True), Arg(kv_norm, "par", diff=True), Arg(b_norm, "par", diff=True)], [Arg(d_hk), Arg(d_hb), Arg(dh2)], tr)
    grads["kv_norm"] = d_kv_norm[0]

    d_og_a = matmul("a_out_dx", dh1, a_w_out, "nt", out_dtype=mxu)
    grads["a_w_out"] = matmul("a_out_dw", og_a, dh1, "tn")
    d_o_a, d_gate_a, grads["a_o_gain"] = row_vjp_call(
        "a_out_gate_bwd", _f_out_gate, og_args, [Arg(d_og_a, bc=HEAD, ph=True)], tr, nh=N_HEADS)
    dqkv_a, d_ba, d_alog, d_dtb, *received = delta_bwd(qkv_a, z_gba, ba_block, alog, dtb, states, t_invs, d_o_a, lp,
                                                        scatter=deferred.scatter_bufs(grads) if deferred else ())
    grads["a_log"], grads["a_dt_bias"] = d_alog[:, :N_HEADS], d_dtb[:, :N_HEADS]
    dz_qkv, d_conv = conv_bwd(z_qkv, y_conv, a_conv, dqkv_a, lp)
    grads["a_conv"] = d_conv.T
    dz_gba = jnp.concatenate([d_gate_a, d_ba.astype(mxu)], 1)
    grads["a_w_in"] = jnp.concatenate([matmul("a_in_qkv_dw", dz_qkv, hn, "tn"), matmul("a_in_gate_ba_dw", dz_gba, hn, "tn")[:kw + 2 * N_HEADS]], 0)
    ride = deferred.last_scatter_bufs(grads) if deferred else ()
    d_hn = matmul("a_in_qkv_dx", dz_qkv, w_qkv_t, "nn", scatter=ride)
    if ride:
        d_hn, *received_last = d_hn
        received = list(received) + received_last
    d_hn = matmul("a_in_gate_ba_dx", dz_gba, w_gba_t, "nn", res=d_hn, out_dtype=mxu)
    dh0, grads["a_norm"] = row_vjp_call("a_norm_bwd", lambda x_, g_: _f_rms(x_, g_) + (x_,),
                                        [Arg(h0, diff=True), Arg(a_norm, "par", diff=True)], [Arg(d_hn), Arg(dh1)], tr)
    dh0 = dh0.reshape(nb, lp, d)
    grads["meta_tokens"] = meta_grad(dh0).T
    return loss, dh0[:, LEAD:], grads, received


_SHARDED = (
    ("meta_tokens", True, False), ("a_norm", True, False), ("a_w_in", True, True), ("a_conv", True, False), ("a_w_out", False, True),
    ("kv_w_down", False, True), ("kv_w_uk", True, True), ("kv_w_uv", True, True), ("b_w_in", True, True), ("b_w_uq", True, True),
    ("b_w_out", False, True))
_REPLICATED = ("a_log", "a_dt_bias", "a_o_gain", "kv_norm", "kv_latent_norm", "k_gain", "b_norm", "b_q_latent_norm", "b_q_gain")
_ALL_WEIGHTS = ("meta_tokens", "a_norm", "a_w_in", "a_conv", "a_log", "a_dt_bias", "a_o_gain", "a_w_out", "kv_norm", "kv_w_down",
                "kv_latent_norm", "kv_w_uk", "kv_w_uv", "k_gain", "b_norm", "b_w_in", "b_q_latent_norm", "b_w_uq", "b_q_gain", "b_w_out")


def _round_up(n, m):
    return (n + m - 1) // m * m


def _pack_rows(pieces, row_multiple):
    padded = []
    for p in pieces:
        n = p.shape[-1]
        padded.append(jnp.pad(p, [(0, 0)] * (p.ndim - 1) + [(0, _round_up(n, PACK_COLS) - n)]))
    flat = jnp.concatenate(padded, -1)
    rows = _round_up(flat.shape[-1] // PACK_COLS, row_multiple)
    flat = jnp.pad(flat, [(0, 0)] * (flat.ndim - 1) + [(0, rows * PACK_COLS - flat.shape[-1])])
    return flat.reshape(flat.shape[:-1] + (rows, PACK_COLS))


def _unpack_rows(buf, sizes):
    flat = buf.reshape(buf.shape[:-2] + (-1,))
    out, off = [], 0
    for n in sizes:
        out.append(flat[..., off:off + n])
        off += _round_up(n, PACK_COLS)
    return out


def _shard_2d(a):
    return a.reshape(a.shape[-2:]) if a.ndim > 2 else a


def _kl_shard(a, by_cols):
    return _shard_2d(a).T if by_cols else _shard_2d(a)


_GROUPS_FIRST = (("a_w_in",),)
_GROUPS_LATER = (("a_w_out", "b_w_in", "b_w_out"), ("b_w_uq",), ("kv_w_down",), ("kv_w_uk", "kv_w_uv"))
_SMALL_SHARDED = ("meta_tokens", "a_norm", "a_conv")
_BY_COLS = {name: by_cols for name, by_cols, _ in _SHARDED}
ROW_ALIGN = 16


def _stack_rows(pieces):
    padded, starts, row = [], [], 0
    for p in pieces:
        r = p.shape[-2]
        padded.append(jnp.pad(p, [(0, 0)] * (p.ndim - 2) + [(0, _round_up(r, ROW_ALIGN) - r), (0, 0)]))
        starts.append(row)
        row += _round_up(r, ROW_ALIGN)
    return jnp.concatenate(padded, -2), starts


def _stack_group(arrays_by_name, names):
    arrays = [arrays_by_name[n].astype(BF16) for n in names]
    buf, starts = _stack_rows(arrays)
    return buf, [(n, s, a.shape[-2]) for n, s, a in zip(names, starts, arrays, strict=True)]


def _stack_groups(arrays_by_name, groups):
    stacked = [_stack_group(arrays_by_name, names) for names in groups]
    return [b for b, _ in stacked], [entries for _, entries in stacked]


def _full_from_gathered(gathered, layout):
    full = {}
    for got, entries in zip(gathered, layout, strict=True):
        for name, start, rows in entries:
            full[name] = got[:, start:start + rows].reshape(N_DEV * rows, got.shape[-1])
    return full


def gather_small_weights(local):
    small = [_kl_shard(local[n], _BY_COLS[n]) for n in _SMALL_SHARDED]
    (gathered,) = _exchange("all_gather", [_pack_rows([s.reshape(-1) for s in small], 8)], scatter=False)
    full = {}
    for name, part, sh in zip(_SMALL_SHARDED, _unpack_rows(gathered, [s.size for s in small]), small, strict=True):
        full[name] = part.reshape(N_DEV * sh.shape[0], sh.shape[1])
    full["a_norm"] = full["a_norm"].reshape(1, -1)
    return full


class LaterExchanges:
    def __init__(self, local):
        shards = {n: _kl_shard(local[n], _BY_COLS[n]) for names in _GROUPS_FIRST + _GROUPS_LATER for n in names}
        self.first_gather_bufs, self.first_layout = _stack_groups(shards, _GROUPS_FIRST)
        self.gather_bufs, self.layout = _stack_groups(shards, _GROUPS_LATER)

    def finish_first(self, gathered):
        return _full_from_gathered(gathered, self.first_layout)

    def finish(self, gathered):
        return _full_from_gathered(gathered, self.layout)

    def scatter_bufs(self, grads):
        return _stack_groups(_owner_slices(grads, _GROUPS_LATER), _GROUPS_LATER)[0]

    def last_scatter_bufs(self, grads):
        bufs, self.last_layout = _stack_groups(_owner_slices(grads, _GROUPS_FIRST), _GROUPS_FIRST)
        return bufs


def _owner_slices(grads, groups):
    return {n: grads[n].reshape(N_DEV, -1, grads[n].shape[-1]) for names in groups for n in names}


def reduce_contributions(name, recv):
    _, r, c = recv.shape
    tr = max(d for d in range(8, 513, 8) if r % d == 0 and (d % ROW_ALIGN == 0 or recv.dtype == F32))

    def body(g_ref, o_ref):
        g = g_ref[0].astype(F32)
        for dev in range(1, N_DEV):
            g = g + g_ref[dev].astype(F32)
        o_ref[...] = g

    return pl.pallas_call(
        body, grid=(r // tr,), in_specs=[pl.BlockSpec((N_DEV, tr, c), lambda i: (0, i, 0))], out_specs=pl.BlockSpec((tr, c), lambda i: (i, 0)),
        out_shape=jax.ShapeDtypeStruct((r, c), F32), compiler_params=_cparams(("arbitrary",)), name=name)(recv)


def adamw_all(gs, ws, ms, vs):
    n = len(gs)

    def body(*refs):
        for i in range(n):
            g_ref, w_ref, m_ref, v_ref = (refs[j * n + i] for j in range(4))
            d_ref, mo_ref, vo_ref = (refs[(4 + j) * n + i] for j in range(3))
            g = g_ref[...]
            m_new = ADAM_B1 * m_ref[...] + (1.0 - ADAM_B1) * g
            v_new = ADAM_B2 * v_ref[...] + (1.0 - ADAM_B2) * (g * g)
            m_hat = m_new / (1.0 - ADAM_B1 ** ADAM_STEP)
            v_hat = v_new / (1.0 - ADAM_B2 ** ADAM_STEP)
            d_ref[...] = -ADAM_LR * (m_hat / (jnp.sqrt(v_hat) + ADAM_EPS) + ADAM_WD * w_ref[...])
            mo_ref[...] = m_new
            vo_ref[...] = v_new

    out = [jax.ShapeDtypeStruct(g.shape, F32) for g in gs] * 3
    res = pl.pallas_call(body, out_shape=out, compiler_params=pltpu.CompilerParams(vmem_limit_bytes=VMEM_LIMIT), name="adamw_all")(*gs, *ws, *ms, *vs)
    return res[:n], res[n:2 * n], res[2 * n:]


def kernel(x, meta_tokens, a_norm, a_w_in, a_conv, a_log, a_dt_bias, a_o_gain, a_w_out, kv_norm, kv_w_down, kv_latent_norm, kv_w_uk, kv_w_uv, k_gain, b_norm, b_w_in, b_q_latent_norm, b_w_uq, b_q_gain, b_w_out, loss_target, m_meta_tokens, m_a_norm, m_a_w_in, m_a_conv, m_a_log, m_a_dt_bias, m_a_o_gain, m_a_w_out, m_kv_norm, m_kv_w_down, m_kv_latent_norm, m_kv_w_uk, m_kv_w_uv, m_k_gain, m_b_norm, m_b_w_in, m_b_q_latent_norm, m_b_w_uq, m_b_q_gain, m_b_w_out, v_meta_tokens, v_a_norm, v_a_w_in, v_a_conv, v_a_log, v_a_dt_bias, v_a_o_gain, v_a_w_out, v_kv_norm, v_kv_w_down, v_kv_latent_norm, v_kv_w_uk, v_kv_w_uv, v_k_gain, v_b_norm, v_b_w_in, v_b_q_latent_norm, v_b_w_uq, v_b_q_gain, v_b_w_out):
    given = dict(locals())
    local_w = {n: given[n] for n in _ALL_WEIGHTS}
    full = gather_small_weights(local_w)
    for n in _REPLICATED:
        full[n] = local_w[n]
    later = LaterExchanges(local_w)

    loss_part, grad_x, grads, received_riding = local_step(x, loss_target, full, later)

    exact = [grads[n].reshape(N_DEV, -1) for n in _SMALL_SHARDED]
    exact += [jnp.broadcast_to(grads[n].reshape(1, -1), (N_DEV, grads[n].size)) for n in _REPLICATED]
    exact.append(jnp.broadcast_to(loss_part, (N_DEV, 1)))
    received = list(received_riding) + list(_exchange("all_to_all", [_pack_rows(exact, 8)], scatter=True))
    layout = later.layout + later.last_layout
    summed = [reduce_contributions(f"reduce_{i}", r) for i, r in enumerate(received)]

    grad_kl = {}
    for got, entries in zip(summed, layout):
        for n, start, rows in entries:
            grad_kl[n] = got[start:start + rows]
    parts = _unpack_rows(summed[-1], [p.shape[1] for p in exact])
    for n, part in zip(_SMALL_SHARDED + _REPLICATED, parts, strict=False):
        grad_kl[n] = part
    loss = parts[-1][0]

    def natural_2d(n, a):
        shape = _shard_2d(local_w[n]).shape if local_w[n].ndim > 1 else (1, local_w[n].size)
        return a.reshape(shape[::-1]).T if _BY_COLS.get(n, False) else a.reshape(shape)

    as_2d = lambda n, a: a.reshape(natural_2d(n, grad_kl[n]).shape)
    gs = [natural_2d(n, grad_kl[n]) for n in _ALL_WEIGHTS]
    deltas, new_m, new_v = adamw_all(gs, [as_2d(n, local_w[n]) for n in _ALL_WEIGHTS], [as_2d(n, given["m_" + n]) for n in _ALL_WEIGHTS],
                                     [as_2d(n, given["v_" + n]) for n in _ALL_WEIGHTS])
    results = [a.reshape(local_w[n].shape) for group in (gs, deltas, new_m, new_v) for n, a in zip(_ALL_WEIGHTS, group, strict=True)]
    return (loss, grad_x, *results)
```

```python
import dataclasses
import functools
import math

import jax
import jax.numpy as jnp
from jax import lax
from jax.experimental import pallas as pl
from jax.experimental.pallas import tpu as pltpu

F32 = jnp.float32
BF16 = jnp.bfloat16
_MXU_DTYPE = jnp.bfloat16

N_DEV = 8
D_MODEL = 1024
N_HEADS = 8
HEAD = 128
CHUNK = 64
N_META = 16
PAD_ROWS = 2 * CHUNK - N_META
LEAD = PAD_ROWS + N_META
ROPE = 64
QK_DIM = HEAD + ROPE
QK_PAD = 2 * HEAD
KV_RANK = 256
Q_RANK = 384
CONV_K = 4
EPS = 1e-6
NEG = -1e30
ROPE_THETA = 10000.0
ADAM_LR, ADAM_B1, ADAM_B2, ADAM_EPS, ADAM_WD, ADAM_STEP = 0.001, 0.9, 0.999, 1e-08, 0.01, 10
PACK_COLS = 512
VMEM_LIMIT = 56 * 1024 * 1024


def _pick(n, options):
    for o in options:
        if n % o == 0:
            return o
    raise ValueError(f"no tile for {n} among {options}")


def _cparams(sem):
    return pltpu.CompilerParams(dimension_semantics=sem, vmem_limit_bytes=VMEM_LIMIT)


def _dims(a, dims):
    if a.ndim == 2:
        return (dims, ((), ()))
    (ca,), (cb,) = dims
    return (((ca + 1,), (cb + 1,)), ((0,), (0,)))


def _dot(a, b, dims):
    return lax.dot_general(a.astype(_MXU_DTYPE), b.astype(_MXU_DTYPE), _dims(a, dims), preferred_element_type=F32)


@jax.custom_vjp
def mm_nn(a, b):
    return _dot(a, b, ((1,), (0,)))


@jax.custom_vjp
def mm_nt(a, b):
    return _dot(a, b, ((1,), (1,)))


@jax.custom_vjp
def mm_tn(a, b):
    return _dot(a, b, ((0,), (0,)))


mm_nn.defvjp(lambda a, b: (mm_nn(a, b), (a, b)), lambda r, g: (mm_nt(g, r[1]), mm_tn(r[0], g)))
mm_nt.defvjp(lambda a, b: (mm_nt(a, b), (a, b)), lambda r, g: (mm_nn(g, r[1]), mm_tn(g, r[0])))
mm_tn.defvjp(lambda a, b: (mm_tn(a, b), (a, b)), lambda r, g: (mm_nt(r[1], g), mm_nn(r[0], g)))


def _split_terms(x, n):
    terms, rest = [], x
    for _ in range(n):
        t = rest.astype(_MXU_DTYPE)
        terms.append(t)
        rest = rest - t.astype(F32)
    return terms


def _dot_01_raw(m, x, dims):
    m = m.astype(_MXU_DTYPE)
    return sum(lax.dot_general(m, t, _dims(m, dims), preferred_element_type=F32) for t in _split_terms(x, 3))


@jax.custom_vjp
def _dot_01(m, x):
    return _dot_01_raw(m, x, ((1,), (0,)))


_dot_01.defvjp(lambda m, x: (_dot_01(m, x), m), lambda m, g: (jnp.zeros_like(m), _dot_01_raw(m, g, ((0,), (0,)))))


def _inv_unit_lower(a):
    n = a.shape[-1]
    eye = (lax.broadcasted_iota(jnp.int32, (n, n), 0) == lax.broadcasted_iota(jnp.int32, (n, n), 1)).astype(F32)
    d = lambda u, w: lax.dot_general(u, w, _dims(u, ((1,), (0,))), preferred_element_type=F32)
    t = eye - a
    p = a.astype(_MXU_DTYPE)
    p = d(p, p)
    squarings = int(math.log2(n)) - 1
    for s in range(squarings):
        ph = p.astype(_MXU_DTYPE)
        t_hi, t_lo = _split_terms(t, 2)
        t = t + (d(t_hi, ph) + d(t_lo, ph))
        if s + 1 < squarings:
            p = d(ph, ph)
    return t


@jax.custom_vjp
def _inv_lookup(a, t):
    return t


def _inv_lookup_bwd(t, g):
    return -mm_tn(t, mm_nt(g, t)), jnp.zeros_like(t)


_inv_lookup.defvjp(lambda a, t: (t, t), _inv_lookup_bwd)


def _sigmoid(x):
    return 1.0 / (1.0 + jnp.exp(-x))


@jax.custom_vjp
def _silu(x):
    return x * _sigmoid(x)


def _silu_fwd(x):
    s = _sigmoid(x)
    return x * s, (x, s)


_silu.defvjp(_silu_fwd, lambda r, g: (g * (r[1] * (1.0 + r[0] * (1.0 - r[1]))),))


def _softplus(x):
    return jnp.where(x > 20.0, x, jnp.log(1.0 + jnp.exp(jnp.minimum(x, 20.0))))


def _rms(x, g, width=None):
    ms = jnp.sum(x * x, -1, keepdims=True) / (x.shape[-1] if width is None else width)
    return x * lax.rsqrt(ms + EPS) * g


MM_VMEM_BUDGET = 40 * 1024 * 1024


def _matmul_rows(name, a, b, mode, out_dtype, res, scatter):
    m, k = a.shape
    n = b.shape[1] if mode == "nn" else b.shape[0]
    dims = {"nn": ((1,), (0,)), "nt": ((1,), (1,))}[mode]
    out_bytes = jnp.dtype(out_dtype).itemsize
    n_in, nx = 2 + (res is not None), len(scatter)

    def vmem(tm):
        blocks = 2 * tm * k * a.dtype.itemsize + 2 * k * n * b.dtype.itemsize + 2 * tm * n * out_bytes + tm * n * 4
        return blocks + (2 * tm * n * res.dtype.itemsize if res is not None else 0)

    tm = next(c for c in (2176, 1088, 512, 256, 128, 64) if m % c == 0 and vmem(c) <= MM_VMEM_BUDGET)
    steps = m // tm

    def body(*refs):
        a_ref, b_ref, o_ref = refs[0], refs[1], refs[n_in + nx]
        i = pl.program_id(0)
        finish = _ride(scatter, True, refs[n_in:n_in + nx], refs[n_in + nx + 1:n_in + 2 * nx + 1], refs[n_in + 2 * nx + 1:], i == 0, i == steps - 1)
        out = _dot(a_ref[...], b_ref[...], dims)
        if res is not None:
            out = out + refs[2][...].astype(F32)
        o_ref[...] = out.astype(o_ref.dtype)
        finish()

    o_spec = pl.BlockSpec((tm, n), lambda i: (i, 0))
    in_specs = [pl.BlockSpec((tm, k), lambda i: (i, 0)), pl.BlockSpec(b.shape, lambda i: (0, 0))] + ([o_spec] if res is not None else [])
    args = (a, b) + ((res,) if res is not None else ())
    out = pl.pallas_call(
        body, grid=(steps,), in_specs=in_specs + [_HBM] * nx, out_specs=[o_spec] + [_HBM] * nx,
        out_shape=[jax.ShapeDtypeStruct((m, n), out_dtype)] + Exchange.out_shape(scatter, True), scratch_shapes=Exchange.scratch(nx) if nx else [],
        compiler_params=_cparams(("arbitrary",) if nx else ("parallel",)), name=name)(*args, *scatter)
    return out if nx else out[0]


def matmul(name, a, b, mode, out_dtype=None, res=None, scatter=()):
    if mode != "tn":
        return _matmul_rows(name, a, b, mode, out_dtype or F32, res, scatter)
    out_dtype = out_dtype or _MXU_DTYPE
    (k, m), (k2, n) = a.shape, b.shape
    assert k == k2 and res is None, (name, a.shape, b.shape, mode)
    tm = _pick(m, (m if m <= 1536 else 1024, 1024, 512, 384, 256, 128))
    tn = _pick(n, (1024, 512, 384, 256, 128))
    tk = _pick(k, (512, 256, 128))
    nk = k // tk
    dims = ((0,), (0,))

    def body(*refs):
        if res is None:
            a_ref, b_ref, o_ref, acc_ref = refs
        else:
            a_ref, b_ref, r_ref, o_ref, acc_ref = refs
        kk = pl.program_id(2)

        @pl.when(kk == 0)
        def _():
            acc_ref[...] = jnp.zeros_like(acc_ref)

        acc_ref[...] += _dot(a_ref[...], b_ref[...], dims)

        @pl.when(kk == nk - 1)
        def _():
            out = acc_ref[...]
            if res is not None:
                out = out + r_ref[...].astype(F32)
            o_ref[...] = out.astype(o_ref.dtype)

    a_spec = pl.BlockSpec((tk, tm), lambda i, j, kk: (kk, i)) if mode == "tn" else pl.BlockSpec((tm, tk), lambda i, j, kk: (i, kk))
    b_spec = pl.BlockSpec((tn, tk), lambda i, j, kk: (j, kk)) if mode == "nt" else pl.BlockSpec((tk, tn), lambda i, j, kk: (kk, j))
    o_spec = pl.BlockSpec((tm, tn), lambda i, j, kk: (i, j))
    in_specs = [a_spec, b_spec] + ([o_spec] if res is not None else [])
    args = (a, b) + ((res,) if res is not None else ())
    return pl.pallas_call(
        body, grid=(m // tm, n // tn, nk), in_specs=in_specs, out_specs=o_spec,
        out_shape=jax.ShapeDtypeStruct((m, n), out_dtype), scratch_shapes=[pltpu.VMEM((tm, tn), F32)],
        compiler_params=_cparams(("parallel", "parallel", "arbitrary")), name=name)(*args)


@dataclasses.dataclass
class Arg:
    arr: jax.Array
    kind: str = "row"
    bc: int = 0
    base: int = 0
    ph: bool = False
    diff: bool = False
    gdt: object = F32


def _arg_spec(a, tr, nh, ntab, base=None):
    bc = a.bc or a.arr.shape[1]
    base = a.base if base is None else base
    width = bc * nh if a.ph else bc
    col = base // nh if a.ph else base
    assert not a.ph or base % nh == 0
    if a.kind == "row":
        return pl.BlockSpec((tr, width), lambda i: (i, col))
    if a.kind == "tab":
        return pl.BlockSpec((tr, width), lambda i: (i % ntab, col))
    return pl.BlockSpec((a.arr.shape[0], width), lambda i: (0, col))


def _head_view(ref, a, h, rs):
    bc = a.bc or a.arr.shape[1]
    rows = slice(None) if a.kind == "par" else rs
    v = ref[rows, h * bc:(h + 1) * bc] if a.ph else ref[rows, :]
    return v.astype(F32) if jnp.issubdtype(v.dtype, jnp.floating) else v


def row_call(name, fn, args, outs, tr, nh=1, ntab=1):
    t = args[0].arr.shape[0]
    n_in = len(args)
    out_args = [Arg(None, "row", bc, 0, ph) for (_, _, bc, ph) in outs]
    assert all(a.ph or nh == 1 for a in out_args)
    rs = slice(None)

    def body(*refs):
        for h in range(nh):
            res = fn(*[_head_view(r, a, h, rs) for r, a in zip(refs[:n_in], args, strict=True)])
            for r, a, v in zip(refs[n_in:], out_args, res, strict=True):
                r[rs, h * a.bc:(h + 1) * a.bc] = v.astype(r.dtype)

    return pl.pallas_call(
        body, grid=(t // tr,), in_specs=[_arg_spec(a, tr, nh, ntab) for a in args], out_specs=[_arg_spec(a, tr, nh, ntab) for a in out_args],
        out_shape=[jax.ShapeDtypeStruct((t, cols), dt) for (cols, dt, _, _) in outs],
        compiler_params=_cparams(("arbitrary",)), name=name)(*[a.arr for a in args])


def row_vjp_call(name, fn, args, cts, tr, nh=1, ntab=1):
    t = args[0].arr.shape[0]
    n_in, n_ct = len(args), len(cts)
    diff_idx = [k for k, a in enumerate(args) if a.diff]
    def body(*refs):
        out_refs = refs[n_in + n_ct:]
        par_sum = {}
        for k, r in zip(diff_idx, out_refs, strict=True):
            if args[k].kind == "par":
                @pl.when(pl.program_id(0) == 0)
                def _(r=r):
                    r[...] = jnp.zeros_like(r)

        for rs in (slice(None),):
            row_sum = {}
            for h in range(nh):
                vals = [_head_view(r, a, h, rs) for r, a in zip(refs[:n_in], args, strict=True)]
                ct_vals = tuple(_head_view(r, a, h, rs) for r, a in zip(refs[n_in:n_in + n_ct], cts, strict=True))

                def f(*dv, vals=vals):
                    full = list(vals)
                    for k, v in zip(diff_idx, dv, strict=True):
                        full[k] = v
                    return tuple(fn(*full))

                _, vjp = jax.vjp(f, *[vals[k] for k in diff_idx])
                for j, (k, r, g) in enumerate(zip(diff_idx, out_refs, vjp(ct_vals), strict=True)):
                    a = args[k]
                    bc = a.bc or a.arr.shape[1]
                    if a.kind == "row" and a.ph:
                        r[rs, h * bc:(h + 1) * bc] = g.astype(r.dtype)
                    elif a.kind == "row":
                        row_sum[j] = g if j not in row_sum else row_sum[j] + g
                    else:
                        key = (j, h if a.ph else 0)
                        par_sum[key] = g if key not in par_sum else par_sum[key] + g
            for j, g in row_sum.items():
                out_refs[j][rs, :] = g.astype(out_refs[j].dtype)
        for (j, h), g in par_sum.items():
            bc = g.shape[1]
            out_refs[j][:, h * bc:(h + 1) * bc] += g

    out_specs, out_shape = [], []
    for k in diff_idx:
        a = args[k]
        bc = a.bc or a.arr.shape[1]
        out_specs.append(_arg_spec(a, tr, nh, ntab, base=0))
        out_shape.append(jax.ShapeDtypeStruct((t if a.kind == "row" else a.arr.shape[0], bc * (nh if a.ph else 1)), a.gdt if a.kind == "row" else F32))
    in_specs = [_arg_spec(a, tr, nh, ntab) for a in list(args) + list(cts)]
    return pl.pallas_call(
        body, grid=(t // tr,), in_specs=in_specs, out_specs=out_specs, out_shape=out_shape,
        compiler_params=_cparams(("arbitrary",)), name=name)(*[a.arr for a in list(args) + list(cts)])


def _conv_taps(x, w):
    rows = lax.broadcasted_iota(jnp.int32, x.shape, 0)
    y = x * w[CONV_K - 1:CONV_K, :]
    for s in range(1, CONV_K):
        y = y + jnp.where(rows >= s, pltpu.roll(x, s, 0), 0.0) * w[CONV_K - 1 - s:CONV_K - s, :]
    return y


CONV_HEADS = 4
CONV_BLOCKS_PER_THIRD = N_HEADS // CONV_HEADS


def _conv_post(y, block):
    a = _silu(y)
    normed = block < 2 * CONV_BLOCKS_PER_THIRD
    scale = jnp.where(block < CONV_BLOCKS_PER_THIRD, HEAD ** -0.5, 1.0)
    return a * jnp.where(normed, lax.rsqrt(jnp.sum(a * a, -1, keepdims=True) + EPS) * scale, 1.0)


def conv_fwd(z, w, lp):
    t, width = z.shape
    cols = CONV_HEADS * HEAD

    def body(z_ref, w_ref, o_ref, y_ref):
        block = pl.program_id(1)
        for h in range(CONV_HEADS):
            cs = slice(h * HEAD, (h + 1) * HEAD)
            y = _conv_taps(z_ref[:, cs], w_ref[:, cs])
            y_ref[:, cs] = y
            o_ref[:, cs] = _conv_post(y, block)

    blk = pl.BlockSpec((lp, cols), lambda b, j: (b, j))
    out = jax.ShapeDtypeStruct((t, width), F32)
    return pl.pallas_call(
        body, grid=(t // lp, width // cols), in_specs=[blk, pl.BlockSpec((CONV_K, cols), lambda b, j: (0, j))],
        out_specs=[blk, blk], out_shape=[out, out], compiler_params=_cparams(("arbitrary", "arbitrary")), name="a_conv_fwd")(z, w)


def conv_bwd(z, y, w, dout, lp):
    t, width = z.shape
    cols = CONV_HEADS * HEAD

    def body(z_ref, y_ref, w_ref, g_ref, dz_ref, dw_ref):
        block = pl.program_id(0)

        @pl.when(pl.program_id(1) == 0)
        def _():
            dw_ref[...] = jnp.zeros_like(dw_ref)

        for h in range(CONV_HEADS):
            cs = slice(h * HEAD, (h + 1) * HEAD)
            x, wv = z_ref[:, cs], w_ref[:, cs]
            _, vjp = jax.vjp(lambda y_: _conv_post(y_, block), y_ref[:, cs])
            (dy,) = vjp(g_ref[:, cs])
            rows = lax.broadcasted_iota(jnp.int32, x.shape, 0)
            dx = dy * wv[CONV_K - 1:CONV_K, :]
            dw_ref[CONV_K - 1:CONV_K, cs] += jnp.sum(dy * x, axis=0, keepdims=True)
            for s in range(1, CONV_K):
                dy_up = jnp.where(rows < lp - s, pltpu.roll(dy, lp - s, 0), 0.0)
                dx = dx + dy_up * wv[CONV_K - 1 - s:CONV_K - s, :]
                dw_ref[CONV_K - 1 - s:CONV_K - s, cs] += jnp.sum(dy_up * x, axis=0, keepdims=True)
            dz_ref[:, cs] = dx.astype(dz_ref.dtype)

    blk = pl.BlockSpec((lp, cols), lambda j, b: (b, j))
    w_blk = pl.BlockSpec((CONV_K, cols), lambda j, b: (0, j))
    return pl.pallas_call(
        body, grid=(width // cols, t // lp), in_specs=[blk, blk, w_blk, blk], out_specs=[blk, w_blk],
        out_shape=[jax.ShapeDtypeStruct((t, width), _MXU_DTYPE), jax.ShapeDtypeStruct((CONV_K, width), F32)],
        compiler_params=_cparams(("arbitrary", "arbitrary")), name="a_conv_bwd")(z, y, w, dout)


def _delta_chunk(q, k, v, ba, alog, dtb, state, t_stored):
    n_g, c = q.shape[0], q.shape[1]
    lane = lax.broadcasted_iota(jnp.int32, (1, HEAD), 1)

    def pick(xs, offset):
        cols = [jnp.sum(xs[i // N_HEADS if len(xs) > 1 else 0] * (lane == offset + i % N_HEADS).astype(F32), axis=1, keepdims=True)[None]
                for i in range(n_g)]
        return jnp.concatenate(cols, 0)

    b_raw, a_raw = pick(ba, 0), pick(ba, N_HEADS)
    a_log, dt_bias = pick((alog,), 0), pick((dtb,), 0)
    beta = _sigmoid(b_raw)
    g = -jnp.exp(a_log) * _softplus(a_raw + dt_bias)
    ri = lax.broadcasted_iota(jnp.int32, (c, c), 0)
    ci = lax.broadcasted_iota(jnp.int32, (c, c), 1)
    tril = ci <= ri
    lower = jnp.broadcast_to(tril.astype(F32), (n_g, c, c))
    gc_col = _dot_01(lower, g * jnp.ones((1, 1, HEAD), F32))[:, :, :1]
    gc_row = _dot_01(jnp.ones((n_g, 8, c), F32), g * (ri <= ci).astype(F32)[None])[:, 0:1, :]
    gc_last = jnp.sum(g, axis=1, keepdims=True)
    decay = jnp.exp(jnp.where(tril, gc_col - gc_row, NEG))
    e_gc = jnp.exp(gc_col)
    kb = k * beta
    a_mat = jnp.where(ci < ri, mm_nt(kb, k) * decay, 0.0)
    t_inv = _inv_unit_lower(a_mat) if t_stored is None else _inv_lookup(a_mat, t_stored)
    u_base = mm_nn(t_inv, v * beta)
    w_dec = mm_nn(t_inv, kb * e_gc)
    attn = jnp.where(tril, mm_nt(q, k) * decay, 0.0)
    u = u_base - mm_nn(w_dec, state)
    o = mm_nn(q * e_gc, state) + mm_nn(attn, u)
    new_state = state * jnp.exp(gc_last) + mm_tn(k * jnp.exp(gc_last - gc_col), u)
    return o, new_state, t_inv


DELTA_STEP_FWD = (4, 2)
DELTA_STEP_BWD = (2, 2)


def _heads_of(ref, rs, first_col):
    return jnp.stack([ref[i // N_HEADS, rs, first_col + (i % N_HEADS) * HEAD:first_col + (i % N_HEADS + 1) * HEAD]
                      for i in range(ref.shape[0] * N_HEADS)])


def _qkv_heads(ref, rs, part):
    return _heads_of(ref, rs, part * N_HEADS * HEAD)


def _by_sequence(a, lp):
    return a.reshape(a.shape[0] // lp, lp, a.shape[1])


def _ride(bufs, scatter, refs_in, refs_out, sems, first, last, two_level=False):
    if not bufs:
        return lambda: None
    make = lambda: (TwoLevelGather if two_level else Exchange)(refs_in, refs_out, *sems, scatter)

    @pl.when(first)
    def _():
        make().start()

    def finish():
        @pl.when(last)
        def _():
            make().wait()

    return finish


def delta_fwd(qkv, ba, ba_block, alog, dtb, lp, gather=()):
    t = qkv.shape[0]
    nb, nc = t // lp, lp // CHUNK
    seqs, cps = DELTA_STEP_FWD
    ng, rows = nc // cps, cps * CHUNK
    nx = len(gather)
    nbg = nb // seqs
    assert nc % cps == 0 and nb % seqs == 0

    def body(*refs):
        qkv_ref, ba_ref, al_ref, dt_ref = refs[:4]
        o_ref, s_ref, t_ref = refs[4 + nx:7 + nx]
        state_ref = refs[7 + 2 * nx]
        b, n = pl.program_id(0), pl.program_id(1)
        finish = _ride(gather, False, refs[4:4 + nx], refs[7 + nx:7 + 2 * nx], refs[8 + 2 * nx:], (b == 0) & (n == 0), (b == nbg - 1) & (n == ng - 1))

        @pl.when(n == 0)
        def _():
            state_ref[...] = jnp.zeros_like(state_ref)

        al, dtv = al_ref[...], dt_ref[...]
        for c in range(cps):
            rs = slice(c * CHUNK, (c + 1) * CHUNK)
            state = state_ref[...]
            o, new_state, t_inv = _delta_chunk(_qkv_heads(qkv_ref, rs, 0), _qkv_heads(qkv_ref, rs, 1), _qkv_heads(qkv_ref, rs, 2),
                                               tuple(ba_ref[i, rs, :] for i in range(seqs)), al, dtv, state, None)
            for i in range((seqs * N_HEADS)):
                seq, g = divmod(i, N_HEADS)
                o_ref[seq, rs, g * HEAD:(g + 1) * HEAD] = o[i]
                s_ref[seq, g, c] = state[i]
                t_ref[seq, g, c] = t_inv[i]
            state_ref[...] = new_state
        finish()

    rows_of = lambda width: pl.BlockSpec((seqs, rows, width), lambda b, n: (b, n, 0))
    par_spec = pl.BlockSpec((1, HEAD), lambda b, n: (0, 0))
    out = pl.pallas_call(
        body, grid=(nbg, ng),
        in_specs=[rows_of(3 * N_HEADS * HEAD), pl.BlockSpec((seqs, rows, HEAD), lambda b, n: (b, n, ba_block)), par_spec, par_spec] + [_HBM] * nx,
        out_specs=[rows_of(N_HEADS * HEAD), pl.BlockSpec((seqs, N_HEADS, cps, HEAD, HEAD), lambda b, n: (b, 0, n, 0, 0)),
                   pl.BlockSpec((seqs, N_HEADS, cps, CHUNK, CHUNK), lambda b, n: (b, 0, n, 0, 0))] + [_HBM] * nx,
        out_shape=[jax.ShapeDtypeStruct((nb, lp, N_HEADS * HEAD), F32), jax.ShapeDtypeStruct((nb, N_HEADS, nc, HEAD, HEAD), F32),
                   jax.ShapeDtypeStruct((nb, N_HEADS, nc, CHUNK, CHUNK), F32)] + Exchange.out_shape(gather, False),
        scratch_shapes=[pltpu.VMEM(((seqs * N_HEADS), HEAD, HEAD), F32)] + (Exchange.scratch(nx) if nx else []),
        compiler_params=_cparams(("arbitrary", "arbitrary")), name="delta_fwd")(_by_sequence(qkv, lp), _by_sequence(ba, lp), alog, dtb, *gather)
    return [out[0].reshape(t, N_HEADS * HEAD)] + list(out[1:])


def delta_bwd(qkv, ba, ba_block, alog, dtb, states, t_invs, do, lp, scatter=()):
    t = qkv.shape[0]
    nb, nc = t // lp, lp // CHUNK
    seqs, cps = DELTA_STEP_BWD
    ng, rows = nc // cps, cps * CHUNK
    nx = len(scatter)
    nbg = nb // seqs

    def body(*refs):
        qkv_ref, ba_ref, al_ref, dt_ref, s_ref, t_ref, do_ref = refs[:7]
        dqkv_ref, dba_ref, dal_ref, ddt_ref = refs[7 + nx:11 + nx]
        dstate_ref = refs[11 + 2 * nx]
        b, step = pl.program_id(0), pl.program_id(1)
        finish = _ride(scatter, True, refs[7:7 + nx], refs[11 + nx:11 + 2 * nx], refs[12 + 2 * nx:], (b == 0) & (step == 0),
                       (b == nbg - 1) & (step == ng - 1))

        @pl.when(step == 0)
        def _():
            dstate_ref[...] = jnp.zeros_like(dstate_ref)

        @pl.when((b == 0) & (step == 0))
        def _():
            dal_ref[...] = jnp.zeros_like(dal_ref)
            ddt_ref[...] = jnp.zeros_like(ddt_ref)

        al, dtv = al_ref[...], dt_ref[...]
        d_al = jnp.zeros((1, HEAD), F32)
        d_dt = jnp.zeros((1, HEAD), F32)
        for c in reversed(range(cps)):
            rs = slice(c * CHUNK, (c + 1) * CHUNK)
            t_n = jnp.stack([t_ref[i // N_HEADS, i % N_HEADS, c] for i in range((seqs * N_HEADS))])
            s_n = jnp.stack([s_ref[i // N_HEADS, i % N_HEADS, c] for i in range((seqs * N_HEADS))])

            def f(q_, k_, v_, ba_, al_, dt_, s_, t_n=t_n):
                return _delta_chunk(q_, k_, v_, ba_, al_, dt_, s_, t_n)[:2]

            _, vjp = jax.vjp(f, _qkv_heads(qkv_ref, rs, 0), _qkv_heads(qkv_ref, rs, 1), _qkv_heads(qkv_ref, rs, 2), tuple(ba_ref[i, rs, :] for i in range(seqs)), al, dtv, s_n)
            grads = vjp((_heads_of(do_ref, rs, 0), dstate_ref[...]))
            for part in range(3):
                for i in range((seqs * N_HEADS)):
                    col = (part * N_HEADS + i % N_HEADS) * HEAD
                    dqkv_ref[i // N_HEADS, rs, col:col + HEAD] = grads[part][i]
            for i in range(seqs):
                dba_ref[i, rs, :] = grads[3][i]
            d_al, d_dt = d_al + grads[4], d_dt + grads[5]
            dstate_ref[...] = grads[6]
        dal_ref[...] += d_al
        ddt_ref[...] += d_dt
        finish()

    rows_of = lambda width: pl.BlockSpec((seqs, rows, width), lambda b, n: (b, ng - 1 - n, 0))
    par_spec = pl.BlockSpec((1, HEAD), lambda b, n: (0, 0))
    out = pl.pallas_call(
        body, grid=(nbg, ng),
        in_specs=[rows_of(3 * N_HEADS * HEAD), pl.BlockSpec((seqs, rows, HEAD), lambda b, n: (b, ng - 1 - n, ba_block)), par_spec, par_spec,
                  pl.BlockSpec((seqs, N_HEADS, cps, HEAD, HEAD), lambda b, n: (b, 0, ng - 1 - n, 0, 0)),
                  pl.BlockSpec((seqs, N_HEADS, cps, CHUNK, CHUNK), lambda b, n: (b, 0, ng - 1 - n, 0, 0)), rows_of(N_HEADS * HEAD)] + [_HBM] * nx,
        out_specs=[rows_of(3 * N_HEADS * HEAD), rows_of(HEAD), par_spec, par_spec] + [_HBM] * nx,
        out_shape=[jax.ShapeDtypeStruct((nb, lp, 3 * N_HEADS * HEAD), F32), jax.ShapeDtypeStruct((nb, lp, HEAD), F32),
                   jax.ShapeDtypeStruct((1, HEAD), F32), jax.ShapeDtypeStruct((1, HEAD), F32)] + Exchange.out_shape(scatter, True),
        scratch_shapes=[pltpu.VMEM(((seqs * N_HEADS), HEAD, HEAD), F32)] + (Exchange.scratch(nx) if nx else []),
        compiler_params=_cparams(("arbitrary", "arbitrary")), name="delta_bwd")(
            _by_sequence(qkv, lp), _by_sequence(ba, lp), alog, dtb, states, t_invs, _by_sequence(do, lp), *scatter)
    return [out[0].reshape(t, 3 * N_HEADS * HEAD), out[1].reshape(t, HEAD)] + list(out[2:])


ATT_Q_TILE = 256
ATT_K_TILE = 512
ATT_SCALE = QK_DIM ** -0.5


def _tiles(end, size):
    return [(s, min(s + size, end)) for s in range(0, end, size)]


def _att_visible(q0, q1, k0, k1, keys_first):
    if k1 <= q0 + CHUNK and k0 >= PAD_ROWS:
        return None
    shape = (k1 - k0, q1 - q0) if keys_first else (q1 - q0, k1 - k0)
    qpos = q0 + lax.broadcasted_iota(jnp.int32, shape, 1 if keys_first else 0)
    kpos = k0 + lax.broadcasted_iota(jnp.int32, shape, 0 if keys_first else 1)
    shift = CHUNK.bit_length() - 1
    return (jnp.right_shift(kpos, shift) <= jnp.right_shift(qpos, shift)) & (kpos >= PAD_ROWS)


def _att_seq_specs(lp):
    return pl.BlockSpec((lp, QK_PAD), lambda b, h: (b, h)), pl.BlockSpec((lp, HEAD), lambda b, h: (b, h))


def flash_fwd(q, k, v, lp):
    t = q.shape[0]
    qk_seq, o_seq = _att_seq_specs(lp)

    def body(q_ref, k_ref, v_ref, o_ref, lse_ref):
        q_tiles = _tiles(lp, ATT_Q_TILE)

        def score_steps(q0, q1, out):
            def step(k0, k1):
                s = mm_nt(q_ref[q0:q1, :], k_ref[k0:k1, :])
                vis = _att_visible(q0, q1, k0, k1, False)
                s = s if vis is None else jnp.where(vis, s, NEG)
                out["scores"].append(s)
                row_max = jnp.max(s, -1, keepdims=True)
                out["m"] = row_max if out["m"] is None else jnp.maximum(out["m"], row_max)
            return [functools.partial(step, k0, k1) for k0, k1 in _tiles(q1, ATT_K_TILE)]

        cur = {"scores": [], "m": None}
        for step in score_steps(*q_tiles[0], cur):
            step()
        for i, (q0, q1) in enumerate(q_tiles):
            nxt = {"scores": [], "m": None}
            ahead = score_steps(*q_tiles[i + 1], nxt) if i + 1 < len(q_tiles) else []
            l = jnp.zeros((q1 - q0, 1), F32)
            acc = jnp.zeros((q1 - q0, HEAD), F32)
            for s, (k0, k1) in zip(cur["scores"], _tiles(q1, ATT_K_TILE), strict=True):
                if ahead:
                    ahead.pop(0)()
                p = jnp.exp(s - cur["m"])
                l = l + jnp.sum(p, -1, keepdims=True)
                acc = acc + mm_nn(p, v_ref[k0:k1, :])
            for step in ahead:
                step()
            o_ref[q0:q1, :] = acc / l
            lse_ref[q0:q1, :] = jnp.broadcast_to(cur["m"] + jnp.log(l), (q1 - q0, HEAD))
            cur = nxt

    big = jax.ShapeDtypeStruct((t, N_HEADS * HEAD), F32)
    return pl.pallas_call(
        body, grid=(t // lp, N_HEADS), in_specs=[qk_seq, qk_seq, o_seq], out_specs=[o_seq, o_seq], out_shape=[big, big],
        compiler_params=_cparams(("arbitrary", "arbitrary")), name="flash_fwd")(q, k, v)


def flash_bwd(q, k, v, o, lse, do, lp):
    t = q.shape[0]
    qk_seq, o_seq = _att_seq_specs(lp)

    def body(q_ref, k_ref, v_ref, o_ref, lse_ref, do_ref, dq_ref, dk_out_ref, dv_out_ref, dk_ref, dv_ref):
        dk_ref[...] = jnp.zeros_like(dk_ref)
        dv_ref[...] = jnp.zeros_like(dv_ref)
        for q0, q1 in _tiles(lp, ATT_Q_TILE):
            qb, dob = q_ref[q0:q1, :], do_ref[q0:q1, :]
            lse_row = jnp.transpose(lse_ref[q0:q1, :])[0:1, :]
            dsum_row = jnp.sum(jnp.transpose(dob * o_ref[q0:q1, :]), axis=0, keepdims=True)
            dq = jnp.zeros((q1 - q0, QK_PAD), F32)
            for k0, k1 in _tiles(q1, ATT_K_TILE):
                kb, vb = k_ref[k0:k1, :], v_ref[k0:k1, :]
                s = mm_nt(kb, qb)
                vis = _att_visible(q0, q1, k0, k1, True)
                s = s if vis is None else jnp.where(vis, s, NEG)
                p = jnp.exp(s - lse_row)
                ds = p * (mm_nt(vb, dob) - dsum_row)
                dv_ref[k0:k1, :] += mm_nn(p, dob)
                dk_ref[k0:k1, :] += mm_nn(ds, qb)
                dq = dq + mm_tn(ds, kb)
            dq_ref[q0:q1, :] = dq.astype(dq_ref.dtype)
        dk_out_ref[...] = dk_ref[...].astype(dk_out_ref.dtype)
        dv_out_ref[...] = dv_ref[...].astype(dv_out_ref.dtype)

    narrow = _MXU_DTYPE
    return pl.pallas_call(
        body, grid=(t // lp, N_HEADS), in_specs=[qk_seq, qk_seq, o_seq, o_seq, o_seq, o_seq], out_specs=[qk_seq, qk_seq, o_seq],
        out_shape=[jax.ShapeDtypeStruct((t, N_HEADS * QK_PAD), narrow), jax.ShapeDtypeStruct((t, N_HEADS * QK_PAD), narrow),
                   jax.ShapeDtypeStruct((t, N_HEADS * HEAD), narrow)],
        scratch_shapes=[pltpu.VMEM((lp, QK_PAD), F32), pltpu.VMEM((lp, HEAD), F32)],
        compiler_params=_cparams(("arbitrary", "arbitrary")), name="flash_bwd")(q, k, v, o, lse, do)


def loss_head(h2, target, lp):
    nb, seq, d = target.shape
    cols = _pick(d, (512, 128))
    ncol = d // cols

    def body(h_ref, t_ref, loss_ref, dh_ref, acc_ref):
        b, j = pl.program_id(0), pl.program_id(1)

        @pl.when((b == 0) & (j == 0))
        def _():
            acc_ref[...] = jnp.zeros_like(acc_ref)

        err = h_ref[LEAD:, :] - t_ref[...]
        dh_ref[:LEAD, :] = jnp.zeros((LEAD, cols), F32)
        dh_ref[LEAD:, :] = err * (1.0 / d)
        acc_ref[...] += jnp.sum(err * err, axis=0, keepdims=True)

        @pl.when((b == nb - 1) & (j == ncol - 1))
        def _():
            loss_ref[...] = jnp.sum(acc_ref[...], axis=1, keepdims=True) * (0.5 / d)

    return pl.pallas_call(
        body, grid=(nb, ncol),
        in_specs=[pl.BlockSpec((None, lp, cols), lambda b, j: (b, 0, j)), pl.BlockSpec((None, seq, cols), lambda b, j: (b, 0, j))],
        out_specs=[pl.BlockSpec((1, 1), lambda b, j: (0, 0)), pl.BlockSpec((None, lp, cols), lambda b, j: (b, 0, j))],
        out_shape=[jax.ShapeDtypeStruct((1, 1), F32), jax.ShapeDtypeStruct((nb, lp, d), F32)],
        scratch_shapes=[pltpu.VMEM((1, cols), F32)], compiler_params=_cparams(("arbitrary", "arbitrary")), name="loss_head")(h2, target)


def gated_out(name, o, gate, gain, w, res):
    t, kw = o.shape
    d = w.shape[1]
    tm = _pick(t, (512, 256, 128))

    def body(*refs):
        o_ref, gate_ref = refs[:2]
        w_ref, r_ref, h_ref, g_ref = refs[-4:]
        if gain is None:
            g_ref[...] = _f_gate(o_ref[...], gate_ref[...])[0].astype(g_ref.dtype)
        else:
            for h in range(N_HEADS):
                cs = slice(h * HEAD, (h + 1) * HEAD)
                g_ref[:, cs] = _f_out_gate(o_ref[:, cs], gate_ref[:, cs], refs[2][...])[0].astype(g_ref.dtype)
        h_ref[...] = r_ref[...] + _dot(g_ref[...], w_ref[...], ((1,), (0,)))

    rows = lambda width: pl.BlockSpec((tm, width), lambda i: (i, 0))
    whole = lambda a: pl.BlockSpec(a.shape, lambda i: (0, 0))
    params = [] if gain is None else [gain]
    return pl.pallas_call(
        body, grid=(t // tm,), in_specs=[rows(kw), rows(kw)] + [whole(p) for p in params] + [whole(w), rows(d)], out_specs=[rows(d), rows(kw)],
        out_shape=[jax.ShapeDtypeStruct((t, d), F32), jax.ShapeDtypeStruct((t, kw), _MXU_DTYPE)],
        compiler_params=_cparams(("parallel",)), name=name)(o, gate, *params, w, res)


def embed_norm(x, meta, gain, lp, gather=()):
    nb, seq, d = x.shape
    nblk, nx = lp // LEAD, len(gather)

    def body(*refs):
        x_ref, meta_ref, g_ref = refs[:3]
        h_ref, hn_ref = refs[3 + nx:5 + nx]
        b, i = pl.program_id(0), pl.program_id(1)
        finish = _ride(gather, False, refs[3:3 + nx], refs[5 + nx:5 + 2 * nx], refs[5 + 2 * nx:], (b == 0) & (i == 0), (b == nb - 1) & (i == nblk - 1),
                       two_level=True)

        @pl.when(i == 0)
        def _():
            h_ref[:PAD_ROWS, :] = jnp.zeros((PAD_ROWS, d), F32)
            h_ref[PAD_ROWS:, :] = meta_ref[...]

        @pl.when(i > 0)
        def _():
            h_ref[...] = x_ref[...]

        hn_ref[...] = _rms(h_ref[...], g_ref[...]).astype(hn_ref.dtype)
        finish()

    rows = pl.BlockSpec((LEAD, d), lambda b, i: (b * nblk + i, 0))
    out = pl.pallas_call(
        body, grid=(nb, nblk),
        in_specs=[pl.BlockSpec((None, LEAD, d), lambda b, i: (b, jnp.maximum(i - 1, 0), 0)), pl.BlockSpec((N_META, d), lambda b, i: (0, 0)),
                  pl.BlockSpec((1, d), lambda b, i: (0, 0))] + [_HBM] * nx,
        out_specs=[rows, rows] + [_HBM] * nx,
        out_shape=[jax.ShapeDtypeStruct((nb * lp, d), F32), jax.ShapeDtypeStruct((nb * lp, d), _MXU_DTYPE)] + Exchange.out_shape(gather, False),
        scratch_shapes=Exchange.scratch(nx) if nx else [],
        compiler_params=_cparams(("arbitrary", "arbitrary")), name="embed_norm")(x, meta, gain, *gather)
    return list(out)


def meta_grad(dh0):
    nb, _, d = dh0.shape

    def body(g_ref, o_ref):
        @pl.when(pl.program_id(0) == 0)
        def _():
            o_ref[...] = jnp.zeros_like(o_ref)

        o_ref[...] += g_ref[PAD_ROWS:LEAD, :]

    return pl.pallas_call(
        body, grid=(nb,), in_specs=[pl.BlockSpec((None, LEAD, d), lambda b: (b, 0, 0))],
        out_specs=pl.BlockSpec((N_META, d), lambda b: (0, 0)), out_shape=jax.ShapeDtypeStruct((N_META, d), F32),
        compiler_params=_cparams(("arbitrary",)), name="meta_grad")(dh0)


_HBM = pl.BlockSpec(memory_space=pltpu.HBM)


def _mesh_pos():
    x, y, c = lax.axis_index("x"), lax.axis_index("y"), lax.axis_index("c")
    return x, y, c


def _peer(x, y, c, k):
    px = 1 - x if k & 4 else x
    py = 1 - y if k & 2 else y
    pc = 1 - c if k & 1 else c
    return (px, py, pc), 4 * px + 2 * py + pc


class Exchange:
    def __init__(self, x_refs, out_refs, send_sems, recv_sems, local_sems, scatter):
        self.x_refs, self.out_refs, self.scatter = x_refs, out_refs, scatter
        self.send_sems, self.recv_sems, self.local_sems = send_sems, recv_sems, local_sems
        self.pos = _mesh_pos()
        x, y, c = self.pos
        self.me = 4 * x + 2 * y + c

    @staticmethod
    def scratch(n):
        return [pltpu.SemaphoreType.DMA((n, N_DEV - 1)), pltpu.SemaphoreType.DMA((n, N_DEV - 1)), pltpu.SemaphoreType.DMA((n,))]

    @staticmethod
    def out_shape(bufs, scatter):
        return [jax.ShapeDtypeStruct(b.shape if scatter else (N_DEV,) + b.shape, b.dtype) for b in bufs]

    def _local(self, i):
        return pltpu.make_async_copy(self.x_refs[i].at[self.me] if self.scatter else self.x_refs[i], self.out_refs[i].at[self.me], self.local_sems.at[i])

    def _copy(self, i, k, landing):
        peer, peer_id = _peer(*self.pos, k)
        src = self.x_refs[i].at[peer_id] if self.scatter else self.x_refs[i]
        return pltpu.make_async_remote_copy(src_ref=src, dst_ref=self.out_refs[i].at[peer_id if landing else self.me],
                                            send_sem=self.send_sems.at[i, k - 1], recv_sem=self.recv_sems.at[i, k - 1],
                                            device_id=peer, device_id_type=pl.DeviceIdType.MESH)

    def start(self):
        for i in range(len(self.x_refs)):
            self._local(i).start()
        for k in range(1, N_DEV):
            for i in range(len(self.x_refs)):
                self._copy(i, k, False).start()

    def wait(self):
        for k in range(1, N_DEV):
            for i in range(len(self.x_refs)):
                self._copy(i, k, True).wait_recv()
        for k in range(1, N_DEV):
            for i in range(len(self.x_refs)):
                self._copy(i, k, False).wait_send()
        for i in range(len(self.x_refs)):
            self._local(i).wait()


class TwoLevelGather(Exchange):
    DIRECT = (1, 4, 2, 6)
    FROM_CHIPS = (4, 2, 6)

    def _forward(self, i, k):
        _, origin = _peer(*self.pos, k)
        sibling, _ = _peer(*self.pos, 1)
        block = self.out_refs[i].at[origin]
        return pltpu.make_async_remote_copy(src_ref=block, dst_ref=block, send_sem=self.send_sems.at[i, (k ^ 1) - 1],
                                            recv_sem=self.recv_sems.at[i, (k ^ 1) - 1], device_id=sibling, device_id_type=pl.DeviceIdType.MESH)

    def start(self):
        assert not self.scatter
        for i in range(len(self.x_refs)):
            self._local(i).start()
        for k in self.DIRECT:
            for i in range(len(self.x_refs)):
                self._copy(i, k, False).start()

    def wait(self):
        n = range(len(self.x_refs))
        for k in self.FROM_CHIPS:
            for i in n:
                self._copy(i, k, True).wait_recv()
                self._forward(i, k).start()
        for k in (1, 5, 3, 7):
            for i in n:
                self._copy(i, k, True).wait_recv()
        for k in self.DIRECT:
            for i in n:
                self._copy(i, k, False).wait_send()
        for k in self.FROM_CHIPS:
            for i in n:
                self._forward(i, k).wait_send()
        for i in n:
            self._local(i).wait()


def _exchange(name, bufs, scatter):
    n = len(bufs)

    def body(*refs):
        ex = Exchange(refs[:n], refs[n:2 * n], *refs[2 * n:], scatter)
        ex.start()
        ex.wait()

    return pl.pallas_call(body, in_specs=[_HBM] * n, out_specs=[_HBM] * n, out_shape=Exchange.out_shape(bufs, scatter),
                          scratch_shapes=Exchange.scratch(n), name=name)(*bufs)


def _f_rms(x, g):
    return (_rms(x, g),)


def _f_rms2(x, g1, g2):
    r = x * lax.rsqrt(jnp.sum(x * x, -1, keepdims=True) / x.shape[-1] + EPS)
    return r * g1, r * g2


@jax.custom_vjp
def _out_gate(o, gate, gain):
    return _rms(o, gain) * _silu(gate)


def _out_gate_bwd(res, g):
    o, gate, gain = res
    r = lax.rsqrt(jnp.sum(o * o, -1, keepdims=True) / o.shape[-1] + EPS)
    n = o * r
    s = _sigmoid(gate)
    g_norm = g * (gate * s)
    d_gate = g * (n * gain) * (s * (1.0 + gate * (1.0 - s)))
    gn = g_norm * gain
    d_o = r * (gn - n * (jnp.sum(gn * n, -1, keepdims=True) / o.shape[-1]))
    return d_o, d_gate, jnp.sum(g_norm * n, 0, keepdims=True)


_out_gate.defvjp(lambda o, gate, gain: (_out_gate(o, gate, gain), (o, gate, gain)), _out_gate_bwd)


def _f_out_gate(o, gate, gain):
    return (_out_gate(o, gate, gain),)


def _f_gate(o, gate):
    return (o * _silu(gate),)


def _swap_rope_halves(x):
    return pltpu.roll(x, ROPE // 2, 1) + pltpu.roll(x, HEAD - ROPE // 2, 1)


def _qk_final_inv_rms(nope, rope_in):
    ms = (jnp.sum(nope * nope, -1, keepdims=True) + jnp.sum(rope_in * rope_in, -1, keepdims=True)) / QK_DIM
    return lax.rsqrt(ms + EPS)


@functools.partial(jax.custom_vjp, nondiff_argnums=(0,))
def _qk_final(scale, nope, rope_in, g_nope, g_rope, cos, sin):
    r = _qk_final_inv_rms(nope, rope_in)
    b = rope_in * (r * g_rope)
    out = jnp.concatenate([nope * (r * g_nope), b * cos + _swap_rope_halves(b) * sin], axis=1)
    return out if scale == 1.0 else out * scale


def _qk_final_fwd(scale, nope, rope_in, g_nope, g_rope, cos, sin):
    return _qk_final(scale, nope, rope_in, g_nope, g_rope, cos, sin), (nope, rope_in, g_nope, g_rope, cos, sin)


def _qk_final_bwd(scale, res, g):
    nope, rope_in, g_nope, g_rope, cos, sin = res
    r = _qk_final_inv_rms(nope, rope_in)
    ga, gb = g[:, :HEAD], g[:, HEAD:]
    if scale != 1.0:
        ga, gb = ga * scale, gb * scale
    db = gb * cos + _swap_rope_halves(gb * sin)
    t_a, t_b = ga * nope, db * rope_in
    d_r = jnp.sum(t_a * g_nope + t_b * g_rope, -1, keepdims=True)
    c = d_r * (r * r * r) * (-1.0 / QK_DIM)
    d_nope = ga * (r * g_nope) + nope * c
    d_rope = db * (r * g_rope) + rope_in * c
    d_g_nope = jnp.sum(t_a * r, 0, keepdims=True)
    d_g_rope = jnp.sum(t_b * r, 0, keepdims=True)
    return d_nope, d_rope, d_g_nope, d_g_rope, jnp.zeros_like(cos), jnp.zeros_like(sin)


_qk_final.defvjp(_qk_final_fwd, _qk_final_bwd)


def _f_qk_final(scale, nope, rope_in, g_nope, g_rope, cos, sin):
    return (_qk_final(scale, nope, rope_in, g_nope, g_rope, cos, sin),)


def _rope_tables(lp):
    half = ROPE // 2
    pos = jnp.maximum(jnp.arange(lp) - PAD_ROWS, 0)
    inv = ROPE_THETA ** (-jnp.arange(half, dtype=F32) / half)
    ang = pos.astype(F32)[:, None] * inv[None, :]
    zeros = jnp.zeros((lp, HEAD - ROPE), F32)
    cos = jnp.concatenate([jnp.cos(ang), jnp.cos(ang), zeros], 1)
    sin = jnp.concatenate([-jnp.sin(ang), jnp.sin(ang), zeros], 1)
    return cos, sin


def _pad_lanes(w, width=HEAD):
    return jnp.pad(w, ((0, 0), (0, width - w.shape[1])))


def _pad_rows(w, rows=HEAD):
    return jnp.pad(w, ((0, rows - w.shape[0]), (0, 0)))


def _split_heads_qk_t(w_t):
    k = w_t.shape[1]
    w3 = w_t.reshape(N_HEADS, QK_DIM, k)
    nope = w3[:, :HEAD].reshape(N_HEADS * HEAD, k)
    rope = jnp.pad(w3[:, HEAD:], ((0, 0), (0, HEAD - ROPE), (0, 0))).reshape(N_HEADS * HEAD, k)
    return jnp.concatenate([nope, rope], 0)


def _merge_heads_qk_t(g_t):
    k = g_t.shape[1]
    kw = N_HEADS * HEAD
    nope, rope = g_t[:kw].reshape(N_HEADS, HEAD, k), g_t[kw:].reshape(N_HEADS, HEAD, k)[:, :ROPE]
    return jnp.concatenate([nope, rope], 1).reshape(N_HEADS * QK_DIM, k)


def local_step(x, target, w, deferred=None):
    nb, seq, d = x.shape
    lp = seq + LEAD
    t = nb * lp
    tr = _pick(lp, (544, 128))
    ntab = lp // tr
    mxu = _MXU_DTYPE
    kw = N_HEADS * HEAD

    a_conv = w["a_conv"].T
    alog, dtb, o_gain = _pad_lanes(w["a_log"]), _pad_lanes(w["a_dt_bias"]), w["a_o_gain"]
    a_norm, kv_norm, b_norm = w["a_norm"], w["kv_norm"][None, :], w["b_norm"]
    lat_norm, qlat_norm = w["kv_latent_norm"][None, :], w["b_q_latent_norm"]
    kg_nope, kg_rope = w["k_gain"][None, :HEAD], _pad_lanes(w["k_gain"][None, HEAD:])
    qg_nope, qg_rope = w["b_q_gain"][:, :HEAD], _pad_lanes(w["b_q_gain"][:, HEAD:])
    cos, sin = _rope_tables(lp)

    h0, hn, *gathered = embed_norm(x, w["meta_tokens"].T, a_norm, lp, gather=deferred.first_gather_bufs if deferred else ())
    if deferred:
        w = {**w, **deferred.finish_first(gathered)}
    a_w_in_t = w["a_w_in"].astype(mxu)
    w_qkv_t, w_gba_t = a_w_in_t[:3 * kw], _pad_rows(a_w_in_t[3 * kw:], kw + HEAD)
    z_qkv = matmul("a_in_qkv", hn, w_qkv_t, "nt")
    z_gba = matmul("a_in_gate_ba", hn, w_gba_t, "nt")
    ba_block = kw // HEAD
    qkv_a, y_conv = conv_fwd(z_qkv, a_conv, lp)
    o_a, states, t_invs, *gathered = delta_fwd(qkv_a, z_gba, ba_block, alog, dtb, lp, gather=deferred.gather_bufs if deferred else ())
    if deferred:
        w = {**w, **deferred.finish(gathered)}
    a_w_out = w["a_w_out"].astype(mxu)
    w_down = _pad_lanes(w["kv_w_down"], KV_RANK + HEAD).astype(mxu)
    w_ukv_t = jnp.concatenate([w["kv_w_uk"], w["kv_w_uv"]], 0).astype(mxu)
    b_w_in_t = w["b_w_in"].astype(mxu)
    w_cq_t, w_gb_t = b_w_in_t[:Q_RANK], b_w_in_t[Q_RANK:]
    w_q_t = _split_heads_qk_t(w["b_w_uq"]).astype(mxu)
    b_w_out = w["b_w_out"].astype(mxu)
    og_args = [Arg(o_a, bc=HEAD, ph=True, diff=True), Arg(z_gba, bc=HEAD, ph=True, diff=True, gdt=mxu), Arg(o_gain, "par", diff=True)]
    h1, og_a = gated_out("a_out", o_a, z_gba, o_gain, a_w_out, h0)

    hk, hb = row_call("b_norms_fwd", _f_rms2, [Arg(h1), Arg(kv_norm, "par"), Arg(b_norm, "par")], [(d, mxu, d, False), (d, mxu, d, False)], tr)
    c_down = matmul("kv_down", hk, w_down, "nn")
    c_kv_arg = Arg(c_down, bc=KV_RANK, diff=True, gdt=mxu)
    k_pe_arg = Arg(c_down, bc=HEAD, base=KV_RANK // HEAD, diff=True)
    c_q_raw = matmul("b_in_q", hb, w_cq_t, "nt")
    gate_b = matmul("b_in_gate", hb, w_gb_t, "nt")
    (c_kv,) = row_call("kv_latent_fwd", _f_rms, [c_kv_arg, Arg(lat_norm, "par")], [(KV_RANK, mxu, KV_RANK, False)], tr)
    (c_q,) = row_call("q_latent_fwd", _f_rms, [Arg(c_q_raw), Arg(qlat_norm, "par")], [(Q_RANK, mxu, Q_RANK, False)], tr)
    k_nope = matmul("k_up", c_kv, w_ukv_t[:kw], "nt")
    v_b = matmul("v_up", c_kv, w_ukv_t[kw:], "nt", out_dtype=mxu)
    q_up = matmul("q_up", c_q, w_q_t, "nt")
    tabs = [Arg(cos, "tab"), Arg(sin, "tab")]
    k_args = [Arg(k_nope, bc=HEAD, ph=True, diff=True, gdt=mxu), k_pe_arg, Arg(kg_nope, "par", diff=True), Arg(kg_rope, "par", diff=True)] + tabs
    q_args = [Arg(q_up, bc=HEAD, ph=True, diff=True, gdt=mxu), Arg(q_up, bc=HEAD, base=N_HEADS, ph=True, diff=True, gdt=mxu),
              Arg(qg_nope, "par", diff=True), Arg(qg_rope, "par", diff=True)] + tabs
    f_k_final, f_q_final = functools.partial(_f_qk_final, 1.0), functools.partial(_f_qk_final, ATT_SCALE)
    (k_fin,) = row_call("k_final_fwd", f_k_final, k_args, [(N_HEADS * QK_PAD, mxu, QK_PAD, True)], tr, nh=N_HEADS, ntab=ntab)
    (q_fin,) = row_call("q_final_fwd", f_q_final, q_args, [(N_HEADS * QK_PAD, mxu, QK_PAD, True)], tr, nh=N_HEADS, ntab=ntab)
    o_b, lse = flash_fwd(q_fin, k_fin, v_b, lp)
    gb_args = [Arg(o_b, diff=True), Arg(gate_b, diff=True, gdt=mxu)]
    h2, og_b = gated_out("b_out", o_b, gate_b, None, b_w_out, h1)

    loss, dh2 = loss_head(h2.reshape(nb, lp, d), target, lp)
    dh2 = dh2.reshape(t, d)
    grads = {}

    d_og_b = matmul("b_out_dx", dh2, b_w_out, "nt", out_dtype=mxu)
    grads["b_w_out"] = matmul("b_out_dw", og_b, dh2, "tn")
    d_o_b, d_gate_b = row_vjp_call("b_gate_bwd", _f_gate, gb_args, [Arg(d_og_b)], tr)
    dq_fin, dk_fin, dv_b = flash_bwd(q_fin, k_fin, v_b, o_b, lse, d_o_b, lp)
    dq_nope, dq_rope, d_qg_nope, d_qg_rope = row_vjp_call(
        "q_final_bwd", f_q_final, q_args, [Arg(dq_fin, bc=QK_PAD, ph=True)], tr, nh=N_HEADS, ntab=ntab)
    dk_nope, dk_pe, d_kg_nope, d_kg_rope = row_vjp_call(
        "k_final_bwd", f_k_final, k_args, [Arg(dk_fin, bc=QK_PAD, ph=True)], tr, nh=N_HEADS, ntab=ntab)
    grads["b_q_gain"] = jnp.concatenate([d_qg_nope, d_qg_rope[:, :ROPE]], 1)
    grads["k_gain"] = jnp.concatenate([d_kg_nope, d_kg_rope[:, :ROPE]], 1)[0]
    d_c_q = matmul("q_nope_dx", dq_nope, w_q_t[:kw], "nn")
    d_c_q = matmul("q_rope_dx", dq_rope, w_q_t[kw:], "nn", res=d_c_q)
    grads["b_w_uq"] = _merge_heads_qk_t(jnp.concatenate([matmul("q_nope_dw", dq_nope, c_q, "tn"), matmul("q_rope_dw", dq_rope, c_q, "tn")], 0))
    d_c_kv = matmul("k_up_dx", dk_nope, w_ukv_t[:kw], "nn")
    d_c_kv = matmul("v_up_dx", dv_b, w_ukv_t[kw:], "nn", res=d_c_kv)
    grads["kv_w_uk"], grads["kv_w_uv"] = matmul("k_up_dw", dk_nope, c_kv, "tn"), matmul("v_up_dw", dv_b, c_kv, "tn")
    d_c_q_raw, grads["b_q_latent_norm"] = row_vjp_call(
        "q_latent_bwd", _f_rms, [Arg(c_q_raw, diff=True, gdt=mxu), Arg(qlat_norm, "par", diff=True)], [Arg(d_c_q)], tr)
    d_c_kv_raw, d_lat = row_vjp_call(
        "kv_latent_bwd", _f_rms, [c_kv_arg, Arg(lat_norm, "par", diff=True)], [Arg(d_c_kv)], tr)
    grads["kv_latent_norm"] = d_lat[0]
    d_hb = matmul("b_in_q_dx", d_c_q_raw, w_cq_t, "nn")
    d_hb = matmul("b_in_gate_dx", d_gate_b, w_gb_t, "nn", res=d_hb, out_dtype=mxu)
    grads["b_w_in"] = jnp.concatenate([matmul("b_in_q_dw", d_c_q_raw, hb, "tn"), matmul("b_in_gate_dw", d_gate_b, hb, "tn")], 0)
    d_c_down = jnp.concatenate([d_c_kv_raw, dk_pe.astype(mxu)], 1)
    d_hk = matmul("kv_down_dx", d_c_down, w_down, "nt", out_dtype=mxu)
    grads["kv_w_down"] = matmul("kv_down_dw", hk, d_c_down, "tn")[:, :KV_RANK + ROPE]
    dh1, d_kv_norm, grads["b_norm"] = row_vjp_call(
        "b_norms_bwd", lambda x_, g1, g2: _f_rms2(x_, g1, g2) + (x_,),
        [Arg(h1, diff=True), Arg(kv_norm, "par", diff=True), Arg(b_norm, "par", diff=True)], [Arg(d_hk), Arg(d_hb), Arg(dh2)], tr)
    grads["kv_norm"] = d_kv_norm[0]

    d_og_a = matmul("a_out_dx", dh1, a_w_out, "nt", out_dtype=mxu)
    grads["a_w_out"] = matmul("a_out_dw", og_a, dh1, "tn")
    d_o_a, d_gate_a, grads["a_o_gain"] = row_vjp_call(
        "a_out_gate_bwd", _f_out_gate, og_args, [Arg(d_og_a, bc=HEAD, ph=True)], tr, nh=N_HEADS)
    dqkv_a, d_ba, d_alog, d_dtb, *received = delta_bwd(qkv_a, z_gba, ba_block, alog, dtb, states, t_invs, d_o_a, lp,
                                                        scatter=deferred.scatter_bufs(grads) if deferred else ())
    grads["a_log"], grads["a_dt_bias"] = d_alog[:, :N_HEADS], d_dtb[:, :N_HEADS]
    dz_qkv, d_conv = conv_bwd(z_qkv, y_conv, a_conv, dqkv_a, lp)
    grads["a_conv"] = d_conv.T
    dz_gba = jnp.concatenate([d_gate_a, d_ba.astype(mxu)], 1)
    grads["a_w_in"] = jnp.concatenate([matmul("a_in_qkv_dw", dz_qkv, hn, "tn"), matmul("a_in_gate_ba_dw", dz_gba, hn, "tn")[:kw + 2 * N_HEADS]], 0)
    ride = deferred.last_scatter_bufs(grads) if deferred else ()
    d_hn = matmul("a_in_qkv_dx", dz_qkv, w_qkv_t, "nn", scatter=ride)
    if ride:
        d_hn, *received_last = d_hn
        received = list(received) + received_last
    d_hn = matmul("a_in_gate_ba_dx", dz_gba, w_gba_t, "nn", res=d_hn, out_dtype=mxu)
    dh0, grads["a_norm"] = row_vjp_call("a_norm_bwd", lambda x_, g_: _f_rms(x_, g_) + (x_,),
                                        [Arg(h0, diff=True), Arg(a_norm, "par", diff=True)], [Arg(d_hn), Arg(dh1)], tr)
    dh0 = dh0.reshape(nb, lp, d)
    grads["meta_tokens"] = meta_grad(dh0).T
    return loss, dh0[:, LEAD:], grads, received


_SHARDED = (
    ("meta_tokens", True, False), ("a_norm", True, False), ("a_w_in", True, True), ("a_conv", True, False), ("a_w_out", False, True),
    ("kv_w_down", False, True), ("kv_w_uk", True, True), ("kv_w_uv", True, True), ("b_w_in", True, True), ("b_w_uq", True, True),
    ("b_w_out", False, True))
_REPLICATED = ("a_log", "a_dt_bias", "a_o_gain", "kv_norm", "kv_latent_norm", "k_gain", "b_norm", "b_q_latent_norm", "b_q_gain")
_ALL_WEIGHTS = ("meta_tokens", "a_norm", "a_w_in", "a_conv", "a_log", "a_dt_bias", "a_o_gain", "a_w_out", "kv_norm", "kv_w_down",
                "kv_latent_norm", "kv_w_uk", "kv_w_uv", "k_gain", "b_norm", "b_w_in", "b_q_latent_norm", "b_w_uq", "b_q_gain", "b_w_out")


def _round_up(n, m):
    return (n + m - 1) // m * m


def _pack_rows(pieces, row_multiple):
    padded = []
    for p in pieces:
        n = p.shape[-1]
        padded.append(jnp.pad(p, [(0, 0)] * (p.ndim - 1) + [(0, _round_up(n, PACK_COLS) - n)]))
    flat = jnp.concatenate(padded, -1)
    rows = _round_up(flat.shape[-1] // PACK_COLS, row_multiple)
    flat = jnp.pad(flat, [(0, 0)] * (flat.ndim - 1) + [(0, rows * PACK_COLS - flat.shape[-1])])
    return flat.reshape(flat.shape[:-1] + (rows, PACK_COLS))


def _unpack_rows(buf, sizes):
    flat = buf.reshape(buf.shape[:-2] + (-1,))
    out, off = [], 0
    for n in sizes:
        out.append(flat[..., off:off + n])
        off += _round_up(n, PACK_COLS)
    return out


def _shard_2d(a):
    return a.reshape(a.shape[-2:]) if a.ndim > 2 else a


def _kl_shard(a, by_cols):
    return _shard_2d(a).T if by_cols else _shard_2d(a)


_GROUPS_FIRST = (("a_w_in",),)
_GROUPS_LATER = (("a_w_out", "b_w_in", "b_w_out"), ("b_w_uq",), ("kv_w_down",), ("kv_w_uk", "kv_w_uv"))
_SMALL_SHARDED = ("meta_tokens", "a_norm", "a_conv")
_BY_COLS = {name: by_cols for name, by_cols, _ in _SHARDED}
ROW_ALIGN = 16


def _stack_rows(pieces):
    padded, starts, row = [], [], 0
    for p in pieces:
        r = p.shape[-2]
        padded.append(jnp.pad(p, [(0, 0)] * (p.ndim - 2) + [(0, _round_up(r, ROW_ALIGN) - r), (0, 0)]))
        starts.append(row)
        row += _round_up(r, ROW_ALIGN)
    return jnp.concatenate(padded, -2), starts


def _stack_group(arrays_by_name, names):
    arrays = [arrays_by_name[n].astype(BF16) for n in names]
    buf, starts = _stack_rows(arrays)
    return buf, [(n, s, a.shape[-2]) for n, s, a in zip(names, starts, arrays, strict=True)]


def _stack_groups(arrays_by_name, groups):
    stacked = [_stack_group(arrays_by_name, names) for names in groups]
    return [b for b, _ in stacked], [entries for _, entries in stacked]


def _full_from_gathered(gathered, layout):
    full = {}
    for got, entries in zip(gathered, layout, strict=True):
        for name, start, rows in entries:
            full[name] = got[:, start:start + rows].reshape(N_DEV * rows, got.shape[-1])
    return full


def gather_small_weights(local):
    small = [_kl_shard(local[n], _BY_COLS[n]) for n in _SMALL_SHARDED]
    (gathered,) = _exchange("all_gather", [_pack_rows([s.reshape(-1) for s in small], 8)], scatter=False)
    full = {}
    for name, part, sh in zip(_SMALL_SHARDED, _unpack_rows(gathered, [s.size for s in small]), small, strict=True):
        full[name] = part.reshape(N_DEV * sh.shape[0], sh.shape[1])
    full["a_norm"] = full["a_norm"].reshape(1, -1)
    return full


class LaterExchanges:
    def __init__(self, local):
        shards = {n: _kl_shard(local[n], _BY_COLS[n]) for names in _GROUPS_FIRST + _GROUPS_LATER for n in names}
        self.first_gather_bufs, self.first_layout = _stack_groups(shards, _GROUPS_FIRST)
        self.gather_bufs, self.layout = _stack_groups(shards, _GROUPS_LATER)

    def finish_first(self, gathered):
        return _full_from_gathered(gathered, self.first_layout)

    def finish(self, gathered):
        return _full_from_gathered(gathered, self.layout)

    def scatter_bufs(self, grads):
        return _stack_groups(_owner_slices(grads, _GROUPS_LATER), _GROUPS_LATER)[0]

    def last_scatter_bufs(self, grads):
        bufs, self.last_layout = _stack_groups(_owner_slices(grads, _GROUPS_FIRST), _GROUPS_FIRST)
        return bufs


def _owner_slices(grads, groups):
    return {n: grads[n].reshape(N_DEV, -1, grads[n].shape[-1]) for names in groups for n in names}


def reduce_contributions(name, recv):
    _, r, c = recv.shape
    tr = max(d for d in range(8, 513, 8) if r % d == 0 and (d % ROW_ALIGN == 0 or recv.dtype == F32))

    def body(g_ref, o_ref):
        g = g_ref[0].astype(F32)
        for dev in range(1, N_DEV):
            g = g + g_ref[dev].astype(F32)
        o_ref[...] = g

    return pl.pallas_call(
        body, grid=(r // tr,), in_specs=[pl.BlockSpec((N_DEV, tr, c), lambda i: (0, i, 0))], out_specs=pl.BlockSpec((tr, c), lambda i: (i, 0)),
        out_shape=jax.ShapeDtypeStruct((r, c), F32), compiler_params=_cparams(("arbitrary",)), name=name)(recv)


def adamw_all(gs, ws, ms, vs):
    n = len(gs)

    def body(*refs):
        for i in range(n):
            g_ref, w_ref, m_ref, v_ref = (refs[j * n + i] for j in range(4))
            d_ref, mo_ref, vo_ref = (refs[(4 + j) * n + i] for j in range(3))
            g = g_ref[...]
            m_new = ADAM_B1 * m_ref[...] + (1.0 - ADAM_B1) * g
            v_new = ADAM_B2 * v_ref[...] + (1.0 - ADAM_B2) * (g * g)
            m_hat = m_new / (1.0 - ADAM_B1 ** ADAM_STEP)
            v_hat = v_new / (1.0 - ADAM_B2 ** ADAM_STEP)
            d_ref[...] = -ADAM_LR * (m_hat / (jnp.sqrt(v_hat) + ADAM_EPS) + ADAM_WD * w_ref[...])
            mo_ref[...] = m_new
            vo_ref[...] = v_new

    out = [jax.ShapeDtypeStruct(g.shape, F32) for g in gs] * 3
    res = pl.pallas_call(body, out_shape=out, compiler_params=pltpu.CompilerParams(vmem_limit_bytes=VMEM_LIMIT), name="adamw_all")(*gs, *ws, *ms, *vs)
    return res[:n], res[n:2 * n], res[2 * n:]


def kernel(x, meta_tokens, a_norm, a_w_in, a_conv, a_log, a_dt_bias, a_o_gain, a_w_out, kv_norm, kv_w_down, kv_latent_norm, kv_w_uk, kv_w_uv, k_gain, b_norm, b_w_in, b_q_latent_norm, b_w_uq, b_q_gain, b_w_out, loss_target, m_meta_tokens, m_a_norm, m_a_w_in, m_a_conv, m_a_log, m_a_dt_bias, m_a_o_gain, m_a_w_out, m_kv_norm, m_kv_w_down, m_kv_latent_norm, m_kv_w_uk, m_kv_w_uv, m_k_gain, m_b_norm, m_b_w_in, m_b_q_latent_norm, m_b_w_uq, m_b_q_gain, m_b_w_out, v_meta_tokens, v_a_norm, v_a_w_in, v_a_conv, v_a_log, v_a_dt_bias, v_a_o_gain, v_a_w_out, v_kv_norm, v_kv_w_down, v_kv_latent_norm, v_kv_w_uk, v_kv_w_uv, v_k_gain, v_b_norm, v_b_w_in, v_b_q_latent_norm, v_b_w_uq, v_b_q_gain, v_b_w_out):
    given = dict(locals())
    local_w = {n: given[n] for n in _ALL_WEIGHTS}
    full = gather_small_weights(local_w)
    for n in _REPLICATED:
        full[n] = local_w[n]
    later = LaterExchanges(local_w)

    loss_part, grad_x, grads, received_riding = local_step(x, loss_target, full, later)

    exact = [grads[n].reshape(N_DEV, -1) for n in _SMALL_SHARDED]
    exact += [jnp.broadcast_to(grads[n].reshape(1, -1), (N_DEV, grads[n].size)) for n in _REPLICATED]
    exact.append(jnp.broadcast_to(loss_part, (N_DEV, 1)))
    received = list(received_riding) + list(_exchange("all_to_all", [_pack_rows(exact, 8)], scatter=True))
    layout = later.layout + later.last_layout
    summed = [reduce_contributions(f"reduce_{i}", r) for i, r in enumerate(received)]

    grad_kl = {}
    for got, entries in zip(summed, layout):
        for n, start, rows in entries:
            grad_kl[n] = got[start:start + rows]
    parts = _unpack_rows(summed[-1], [p.shape[1] for p in exact])
    for n, part in zip(_SMALL_SHARDED + _REPLICATED, parts, strict=False):
        grad_kl[n] = part
    loss = parts[-1][0]

    def natural_2d(n, a):
        shape = _shard_2d(local_w[n]).shape if local_w[n].ndim > 1 else (1, local_w[n].size)
        return a.reshape(shape[::-1]).T if _BY_COLS.get(n, False) else a.reshape(shape)

    as_2d = lambda n, a: a.reshape(natural_2d(n, grad_kl[n]).shape)
    gs = [natural_2d(n, grad_kl[n]) for n in _ALL_WEIGHTS]
    deltas, new_m, new_v = adamw_all(gs, [as_2d(n, local_w[n]) for n in _ALL_WEIGHTS], [as_2d(n, given["m_" + n]) for n in _ALL_WEIGHTS],
                                     [as_2d(n, given["v_" + n]) for n in _ALL_WEIGHTS])
    results = [a.reshape(local_w[n].shape) for group in (gs, deltas, new_m, new_v) for n, a in zip(_ALL_WEIGHTS, group, strict=True)]
    return (loss, grad_x, *results)
```

```python
import dataclasses
import functools
import math

import jax
import jax.numpy as jnp
from jax import lax
from jax.experimental import pallas as pl
from jax.experimental.pallas import tpu as pltpu

F32 = jnp.float32
BF16 = jnp.bfloat16
_MXU_DTYPE = jnp.bfloat16

N_DEV = 8
D_MODEL = 1024
N_HEADS = 8
HEAD = 128
CHUNK = 64
N_META = 16
PAD_ROWS = 2 * CHUNK - N_META
LEAD = PAD_ROWS + N_META
ROPE = 64
QK_DIM = HEAD + ROPE
QK_PAD = 2 * HEAD
KV_RANK = 256
Q_RANK = 384
CONV_K = 4
EPS = 1e-6
NEG = -1e30
ROPE_THETA = 10000.0
ADAM_LR, ADAM_B1, ADAM_B2, ADAM_EPS, ADAM_WD, ADAM_STEP = 0.001, 0.9, 0.999, 1e-08, 0.01, 10
PACK_COLS = 512
VMEM_LIMIT = 56 * 1024 * 1024


def _pick(n, options):
    for o in options:
        if n % o == 0:
            return o
    raise ValueError(f"no tile for {n} among {options}")


def _cparams(sem):
    return pltpu.CompilerParams(dimension_semantics=sem, vmem_limit_bytes=VMEM_LIMIT)


def _dims(a, dims):
    if a.ndim == 2:
        return (dims, ((), ()))
    (ca,), (cb,) = dims
    return (((ca + 1,), (cb + 1,)), ((0,), (0,)))


def _dot(a, b, dims):
    return lax.dot_general(a.astype(_MXU_DTYPE), b.astype(_MXU_DTYPE), _dims(a, dims), preferred_element_type=F32)


@jax.custom_vjp
def mm_nn(a, b):
    return _dot(a, b, ((1,), (0,)))


@jax.custom_vjp
def mm_nt(a, b):
    return _dot(a, b, ((1,), (1,)))


@jax.custom_vjp
def mm_tn(a, b):
    return _dot(a, b, ((0,), (0,)))


mm_nn.defvjp(lambda a, b: (mm_nn(a, b), (a, b)), lambda r, g: (mm_nt(g, r[1]), mm_tn(r[0], g)))
mm_nt.defvjp(lambda a, b: (mm_nt(a, b), (a, b)), lambda r, g: (mm_nn(g, r[1]), mm_tn(g, r[0])))
mm_tn.defvjp(lambda a, b: (mm_tn(a, b), (a, b)), lambda r, g: (mm_nt(r[1], g), mm_nn(r[0], g)))


def _split_terms(x, n):
    terms, rest = [], x
    for _ in range(n):
        t = rest.astype(_MXU_DTYPE)
        terms.append(t)
        rest = rest - t.astype(F32)
    return terms


def _dot_01_raw(m, x, dims):
    m = m.astype(_MXU_DTYPE)
    return sum(lax.dot_general(m, t, _dims(m, dims), preferred_element_type=F32) for t in _split_terms(x, 3))


@jax.custom_vjp
def _dot_01(m, x):
    return _dot_01_raw(m, x, ((1,), (0,)))


_dot_01.defvjp(lambda m, x: (_dot_01(m, x), m), lambda m, g: (jnp.zeros_like(m), _dot_01_raw(m, g, ((0,), (0,)))))


def _inv_unit_lower(a):
    n = a.shape[-1]
    eye = (lax.broadcasted_iota(jnp.int32, (n, n), 0) == lax.broadcasted_iota(jnp.int32, (n, n), 1)).astype(F32)
    d = lambda u, w: lax.dot_general(u, w, _dims(u, ((1,), (0,))), preferred_element_type=F32)
    t = eye - a
    p = a.astype(_MXU_DTYPE)
    p = d(p, p)
    squarings = int(math.log2(n)) - 1
    for s in range(squarings):
        ph = p.astype(_MXU_DTYPE)
        t_hi, t_lo = _split_terms(t, 2)
        t = t + (d(t_hi, ph) + d(t_lo, ph))
        if s + 1 < squarings:
            p = d(ph, ph)
    return t


@jax.custom_vjp
def _inv_lookup(a, t):
    return t


def _inv_lookup_bwd(t, g):
    return -mm_tn(t, mm_nt(g, t)), jnp.zeros_like(t)


_inv_lookup.defvjp(lambda a, t: (t, t), _inv_lookup_bwd)


def _sigmoid(x):
    return 1.0 / (1.0 + jnp.exp(-x))


@jax.custom_vjp
def _silu(x):
    return x * _sigmoid(x)


def _silu_fwd(x):
    s = _sigmoid(x)
    return x * s, (x, s)


_silu.defvjp(_silu_fwd, lambda r, g: (g * (r[1] * (1.0 + r[0] * (1.0 - r[1]))),))


def _softplus(x):
    return jnp.where(x > 20.0, x, jnp.log(1.0 + jnp.exp(jnp.minimum(x, 20.0))))


def _rms(x, g, width=None):
    ms = jnp.sum(x * x, -1, keepdims=True) / (x.shape[-1] if width is None else width)
    return x * lax.rsqrt(ms + EPS) * g


MM_VMEM_BUDGET = 40 * 1024 * 1024


def _matmul_rows(name, a, b, mode, out_dtype, res, scatter):
    m, k = a.shape
    n = b.shape[1] if mode == "nn" else b.shape[0]
    dims = {"nn": ((1,), (0,)), "nt": ((1,), (1,))}[mode]
    out_bytes = jnp.dtype(out_dtype).itemsize
    n_in, nx = 2 + (res is not None), len(scatter)

    def vmem(tm):
        blocks = 2 * tm * k * a.dtype.itemsize + 2 * k * n * b.dtype.itemsize + 2 * tm * n * out_bytes + tm * n * 4
        return blocks + (2 * tm * n * res.dtype.itemsize if res is not None else 0)

    tm = next(c for c in (2176, 1088, 512, 256, 128, 64) if m % c == 0 and vmem(c) <= MM_VMEM_BUDGET)
    steps = m // tm

    def body(*refs):
        a_ref, b_ref, o_ref = refs[0], refs[1], refs[n_in + nx]
        i = pl.program_id(0)
        finish = _ride(scatter, True, refs[n_in:n_in + nx], refs[n_in + nx + 1:n_in + 2 * nx + 1], refs[n_in + 2 * nx + 1:], i == 0, i == steps - 1)
        out = _dot(a_ref[...], b_ref[...], dims)
        if res is not None:
            out = out + refs[2][...].astype(F32)
        o_ref[...] = out.astype(o_ref.dtype)
        finish()

    o_spec = pl.BlockSpec((tm, n), lambda i: (i, 0))
    in_specs = [pl.BlockSpec((tm, k), lambda i: (i, 0)), pl.BlockSpec(b.shape, lambda i: (0, 0))] + ([o_spec] if res is not None else [])
    args = (a, b) + ((res,) if res is not None else ())
    out = pl.pallas_call(
        body, grid=(steps,), in_specs=in_specs + [_HBM] * nx, out_specs=[o_spec] + [_HBM] * nx,
        out_shape=[jax.ShapeDtypeStruct((m, n), out_dtype)] + Exchange.out_shape(scatter, True), scratch_shapes=Exchange.scratch(nx) if nx else [],
        compiler_params=_cparams(("arbitrary",) if nx else ("parallel",)), name=name)(*args, *scatter)
    return out if nx else out[0]


def matmul(name, a, b, mode, out_dtype=None, res=None, scatter=()):
    if mode != "tn":
        return _matmul_rows(name, a, b, mode, out_dtype or F32, res, scatter)
    out_dtype = out_dtype or _MXU_DTYPE
    (k, m), (k2, n) = a.shape, b.shape
    assert k == k2 and res is None, (name, a.shape, b.shape, mode)
    tm = _pick(m, (m if m <= 1536 else 1024, 1024, 512, 384, 256, 128))
    tn = _pick(n, (1024, 512, 384, 256, 128))
    tk = _pick(k, (512, 256, 128))
    nk = k // tk
    dims = ((0,), (0,))

    def body(*refs):
        if res is None:
            a_ref, b_ref, o_ref, acc_ref = refs
        else:
            a_ref, b_ref, r_ref, o_ref, acc_ref = refs
        kk = pl.program_id(2)

        @pl.when(kk == 0)
        def _():
            acc_ref[...] = jnp.zeros_like(acc_ref)

        acc_ref[...] += _dot(a_ref[...], b_ref[...], dims)

        @pl.when(kk == nk - 1)
        def _():
            out = acc_ref[...]
            if res is not None:
                out = out + r_ref[...].astype(F32)
            o_ref[...] = out.astype(o_ref.dtype)

    a_spec = pl.BlockSpec((tk, tm), lambda i, j, kk: (kk, i)) if mode == "tn" else pl.BlockSpec((tm, tk), lambda i, j, kk: (i, kk))
    b_spec = pl.BlockSpec((tn, tk), lambda i, j, kk: (j, kk)) if mode == "nt" else pl.BlockSpec((tk, tn), lambda i, j, kk: (kk, j))
    o_spec = pl.BlockSpec((tm, tn), lambda i, j, kk: (i, j))
    in_specs = [a_spec, b_spec] + ([o_spec] if res is not None else [])
    args = (a, b) + ((res,) if res is not None else ())
    return pl.pallas_call(
        body, grid=(m // tm, n // tn, nk), in_specs=in_specs, out_specs=o_spec,
        out_shape=jax.ShapeDtypeStruct((m, n), out_dtype), scratch_shapes=[pltpu.VMEM((tm, tn), F32)],
        compiler_params=_cparams(("parallel", "parallel", "arbitrary")), name=name)(*args)


@dataclasses.dataclass
class Arg:
    arr: jax.Array
    kind: str = "row"
    bc: int = 0
    base: int = 0
    ph: bool = False
    diff: bool = False
    gdt: object = F32


def _arg_spec(a, tr, nh, ntab, base=None):
    bc = a.bc or a.arr.shape[1]
    base = a.base if base is None else base
    width = bc * nh if a.ph else bc
    col = base // nh if a.ph else base
    assert not a.ph or base % nh == 0
    if a.kind == "row":
        return pl.BlockSpec((tr, width), lambda i: (i, col))
    if a.kind == "tab":
        return pl.BlockSpec((tr, width), lambda i: (i % ntab, col))
    return pl.BlockSpec((a.arr.shape[0], width), lambda i: (0, col))


def _head_view(ref, a, h, rs):
    bc = a.bc or a.arr.shape[1]
    rows = slice(None) if a.kind == "par" else rs
    v = ref[rows, h * bc:(h + 1) * bc] if a.ph else ref[rows, :]
    return v.astype(F32) if jnp.issubdtype(v.dtype, jnp.floating) else v


def row_call(name, fn, args, outs, tr, nh=1, ntab=1):
    t = args[0].arr.shape[0]
    n_in = len(args)
    out_args = [Arg(None, "row", bc, 0, ph) for (_, _, bc, ph) in outs]
    assert all(a.ph or nh == 1 for a in out_args)
    rs = slice(None)

    def body(*refs):
        for h in range(nh):
            res = fn(*[_head_view(r, a, h, rs) for r, a in zip(refs[:n_in], args, strict=True)])
            for r, a, v in zip(refs[n_in:], out_args, res, strict=True):
                r[rs, h * a.bc:(h + 1) * a.bc] = v.astype(r.dtype)

    return pl.pallas_call(
        body, grid=(t // tr,), in_specs=[_arg_spec(a, tr, nh, ntab) for a in args], out_specs=[_arg_spec(a, tr, nh, ntab) for a in out_args],
        out_shape=[jax.ShapeDtypeStruct((t, cols), dt) for (cols, dt, _, _) in outs],
        compiler_params=_cparams(("arbitrary",)), name=name)(*[a.arr for a in args])


def row_vjp_call(name, fn, args, cts, tr, nh=1, ntab=1):
    t = args[0].arr.shape[0]
    n_in, n_ct = len(args), len(cts)
    diff_idx = [k for k, a in enumerate(args) if a.diff]
    def body(*refs):
        out_refs = refs[n_in + n_ct:]
        par_sum = {}
        for k, r in zip(diff_idx, out_refs, strict=True):
            if args[k].kind == "par":
                @pl.when(pl.program_id(0) == 0)
                def _(r=r):
                    r[...] = jnp.zeros_like(r)

        for rs in (slice(None),):
            row_sum = {}
            for h in range(nh):
                vals = [_head_view(r, a, h, rs) for r, a in zip(refs[:n_in], args, strict=True)]
                ct_vals = tuple(_head_view(r, a, h, rs) for r, a in zip(refs[n_in:n_in + n_ct], cts, strict=True))

                def f(*dv, vals=vals):
                    full = list(vals)
                    for k, v in zip(diff_idx, dv, strict=True):
                        full[k] = v
                    return tuple(fn(*full))

                _, vjp = jax.vjp(f, *[vals[k] for k in diff_idx])
                for j, (k, r, g) in enumerate(zip(diff_idx, out_refs, vjp(ct_vals), strict=True)):
                    a = args[k]
                    bc = a.bc or a.arr.shape[1]
                    if a.kind == "row" and a.ph:
                        r[rs, h * bc:(h + 1) * bc] = g.astype(r.dtype)
                    elif a.kind == "row":
                        row_sum[j] = g if j not in row_sum else row_sum[j] + g
                    else:
                        key = (j, h if a.ph else 0)
                        par_sum[key] = g if key not in par_sum else par_sum[key] + g
            for j, g in row_sum.items():
                out_refs[j][rs, :] = g.astype(out_refs[j].dtype)
        for (j, h), g in par_sum.items():
            bc = g.shape[1]
            out_refs[j][:, h * bc:(h + 1) * bc] += g

    out_specs, out_shape = [], []
    for k in diff_idx:
        a = args[k]
        bc = a.bc or a.arr.shape[1]
        out_specs.append(_arg_spec(a, tr, nh, ntab, base=0))
        out_shape.append(jax.ShapeDtypeStruct((t if a.kind == "row" else a.arr.shape[0], bc * (nh if a.ph else 1)), a.gdt if a.kind == "row" else F32))
    in_specs = [_arg_spec(a, tr, nh, ntab) for a in list(args) + list(cts)]
    return pl.pallas_call(
        body, grid=(t // tr,), in_specs=in_specs, out_specs=out_specs, out_shape=out_shape,
        compiler_params=_cparams(("arbitrary",)), name=name)(*[a.arr for a in list(args) + list(cts)])


def _conv_taps(x, w):
    rows = lax.broadcasted_iota(jnp.int32, x.shape, 0)
    y = x * w[CONV_K - 1:CONV_K, :]
    for s in range(1, CONV_K):
        y = y + jnp.where(rows >= s, pltpu.roll(x, s, 0), 0.0) * w[CONV_K - 1 - s:CONV_K - s, :]
    return y


CONV_HEADS = 4
CONV_BLOCKS_PER_THIRD = N_HEADS // CONV_HEADS


def _conv_post(y, block):
    a = _silu(y)
    normed = block < 2 * CONV_BLOCKS_PER_THIRD
    scale = jnp.where(block < CONV_BLOCKS_PER_THIRD, HEAD ** -0.5, 1.0)
    return a * jnp.where(normed, lax.rsqrt(jnp.sum(a * a, -1, keepdims=True) + EPS) * scale, 1.0)


def conv_fwd(z, w, lp):
    t, width = z.shape
    cols = CONV_HEADS * HEAD

    def body(z_ref, w_ref, o_ref, y_ref):
        block = pl.program_id(1)
        for h in range(CONV_HEADS):
            cs = slice(h * HEAD, (h + 1) * HEAD)
            y = _conv_taps(z_ref[:, cs], w_ref[:, cs])
            y_ref[:, cs] = y
            o_ref[:, cs] = _conv_post(y, block)

    blk = pl.BlockSpec((lp, cols), lambda b, j: (b, j))
    out = jax.ShapeDtypeStruct((t, width), F32)
    return pl.pallas_call(
        body, grid=(t // lp, width // cols), in_specs=[blk, pl.BlockSpec((CONV_K, cols), lambda b, j: (0, j))],
        out_specs=[blk, blk], out_shape=[out, out], compiler_params=_cparams(("arbitrary", "arbitrary")), name="a_conv_fwd")(z, w)


def conv_bwd(z, y, w, dout, lp):
    t, width = z.shape
    cols = CONV_HEADS * HEAD

    def body(z_ref, y_ref, w_ref, g_ref, dz_ref, dw_ref):
        block = pl.program_id(0)

        @pl.when(pl.program_id(1) == 0)
        def _():
            dw_ref[...] = jnp.zeros_like(dw_ref)

        for h in range(CONV_HEADS):
            cs = slice(h * HEAD, (h + 1) * HEAD)
            x, wv = z_ref[:, cs], w_ref[:, cs]
            _, vjp = jax.vjp(lambda y_: _conv_post(y_, block), y_ref[:, cs])
            (dy,) = vjp(g_ref[:, cs])
            rows = lax.broadcasted_iota(jnp.int32, x.shape, 0)
            dx = dy * wv[CONV_K - 1:CONV_K, :]
            dw_ref[CONV_K - 1:CONV_K, cs] += jnp.sum(dy * x, axis=0, keepdims=True)
            for s in range(1, CONV_K):
                dy_up = jnp.where(rows < lp - s, pltpu.roll(dy, lp - s, 0), 0.0)
                dx = dx + dy_up * wv[CONV_K - 1 - s:CONV_K - s, :]
                dw_ref[CONV_K - 1 - s:CONV_K - s, cs] += jnp.sum(dy_up * x, axis=0, keepdims=True)
            dz_ref[:, cs] = dx.astype(dz_ref.dtype)

    blk = pl.BlockSpec((lp, cols), lambda j, b: (b, j))
    w_blk = pl.BlockSpec((CONV_K, cols), lambda j, b: (0, j))
    return pl.pallas_call(
        body, grid=(width // cols, t // lp), in_specs=[blk, blk, w_blk, blk], out_specs=[blk, w_blk],
        out_shape=[jax.ShapeDtypeStruct((t, width), _MXU_DTYPE), jax.ShapeDtypeStruct((CONV_K, width), F32)],
        compiler_params=_cparams(("arbitrary", "arbitrary")), name="a_conv_bwd")(z, y, w, dout)


def _delta_chunk(q, k, v, ba, alog, dtb, state, t_stored):
    n_g, c = q.shape[0], q.shape[1]
    lane = lax.broadcasted_iota(jnp.int32, (1, HEAD), 1)

    def pick(xs, offset):
        cols = [jnp.sum(xs[i // N_HEADS if len(xs) > 1 else 0] * (lane == offset + i % N_HEADS).astype(F32), axis=1, keepdims=True)[None]
                for i in range(n_g)]
        return jnp.concatenate(cols, 0)

    b_raw, a_raw = pick(ba, 0), pick(ba, N_HEADS)
    a_log, dt_bias = pick((alog,), 0), pick((dtb,), 0)
    beta = _sigmoid(b_raw)
    g = -jnp.exp(a_log) * _softplus(a_raw + dt_bias)
    ri = lax.broadcasted_iota(jnp.int32, (c, c), 0)
    ci = lax.broadcasted_iota(jnp.int32, (c, c), 1)
    tril = ci <= ri
    lower = jnp.broadcast_to(tril.astype(F32), (n_g, c, c))
    gc_col = _dot_01(lower, g * jnp.ones((1, 1, HEAD), F32))[:, :, :1]
    gc_row = _dot_01(jnp.ones((n_g, 8, c), F32), g * (ri <= ci).astype(F32)[None])[:, 0:1, :]
    gc_last = jnp.sum(g, axis=1, keepdims=True)
    decay = jnp.exp(jnp.where(tril, gc_col - gc_row, NEG))
    e_gc = jnp.exp(gc_col)
    kb = k * beta
    a_mat = jnp.where(ci < ri, mm_nt(kb, k) * decay, 0.0)
    t_inv = _inv_unit_lower(a_mat) if t_stored is None else _inv_lookup(a_mat, t_stored)
    u_base = mm_nn(t_inv, v * beta)
    w_dec = mm_nn(t_inv, kb * e_gc)
    attn = jnp.where(tril, mm_nt(q, k) * decay, 0.0)
    u = u_base - mm_nn(w_dec, state)
    o = mm_nn(q * e_gc, state) + mm_nn(attn, u)
    new_state = state * jnp.exp(gc_last) + mm_tn(k * jnp.exp(gc_last - gc_col), u)
    return o, new_state, t_inv


DELTA_STEP_FWD = (4, 2)
DELTA_STEP_BWD = (2, 2)


def _heads_of(ref, rs, first_col):
    return jnp.stack([ref[i // N_HEADS, rs, first_col + (i % N_HEADS) * HEAD:first_col + (i % N_HEADS + 1) * HEAD]
                      for i in range(ref.shape[0] * N_HEADS)])


def _qkv_heads(ref, rs, part):
    return _heads_of(ref, rs, part * N_HEADS * HEAD)


def _by_sequence(a, lp):
    return a.reshape(a.shape[0] // lp, lp, a.shape[1])


def _ride(bufs, scatter, refs_in, refs_out, sems, first, last, two_level=False):
    if not bufs:
        return lambda: None
    make = lambda: (TwoLevelGather if two_level else Exchange)(refs_in, refs_out, *sems, scatter)

    @pl.when(first)
    def _():
        make().start()

    def finish():
        @pl.when(last)
        def _():
            make().wait()

    return finish


def delta_fwd(qkv, ba, ba_block, alog, dtb, lp, gather=()):
    t = qkv.shape[0]
    nb, nc = t // lp, lp // CHUNK
    seqs, cps = DELTA_STEP_FWD
    ng, rows = nc // cps, cps * CHUNK
    nx = len(gather)
    nbg = nb // seqs
    assert nc % cps == 0 and nb % seqs == 0

    def body(*refs):
        qkv_ref, ba_ref, al_ref, dt_ref = refs[:4]
        o_ref, s_ref, t_ref = refs[4 + nx:7 + nx]
        state_ref = refs[7 + 2 * nx]
        b, n = pl.program_id(0), pl.program_id(1)
        finish = _ride(gather, False, refs[4:4 + nx], refs[7 + nx:7 + 2 * nx], refs[8 + 2 * nx:], (b == 0) & (n == 0), (b == nbg - 1) & (n == ng - 1))

        @pl.when(n == 0)
        def _():
            state_ref[...] = jnp.zeros_like(state_ref)

        al, dtv = al_ref[...], dt_ref[...]
        for c in range(cps):
            rs = slice(c * CHUNK, (c + 1) * CHUNK)
            state = state_ref[...]
            o, new_state, t_inv = _delta_chunk(_qkv_heads(qkv_ref, rs, 0), _qkv_heads(qkv_ref, rs, 1), _qkv_heads(qkv_ref, rs, 2),
                                               tuple(ba_ref[i, rs, :] for i in range(seqs)), al, dtv, state, None)
            for i in range((seqs * N_HEADS)):
                seq, g = divmod(i, N_HEADS)
                o_ref[seq, rs, g * HEAD:(g + 1) * HEAD] = o[i]
                s_ref[seq, g, c] = state[i]
                t_ref[seq, g, c] = t_inv[i]
            state_ref[...] = new_state
        finish()

    rows_of = lambda width: pl.BlockSpec((seqs, rows, width), lambda b, n: (b, n, 0))
    par_spec = pl.BlockSpec((1, HEAD), lambda b, n: (0, 0))
    out = pl.pallas_call(
        body, grid=(nbg, ng),
        in_specs=[rows_of(3 * N_HEADS * HEAD), pl.BlockSpec((seqs, rows, HEAD), lambda b, n: (b, n, ba_block)), par_spec, par_spec] + [_HBM] * nx,
        out_specs=[rows_of(N_HEADS * HEAD), pl.BlockSpec((seqs, N_HEADS, cps, HEAD, HEAD), lambda b, n: (b, 0, n, 0, 0)),
                   pl.BlockSpec((seqs, N_HEADS, cps, CHUNK, CHUNK), lambda b, n: (b, 0, n, 0, 0))] + [_HBM] * nx,
        out_shape=[jax.ShapeDtypeStruct((nb, lp, N_HEADS * HEAD), F32), jax.ShapeDtypeStruct((nb, N_HEADS, nc, HEAD, HEAD), F32),
                   jax.ShapeDtypeStruct((nb, N_HEADS, nc, CHUNK, CHUNK), F32)] + Exchange.out_shape(gather, False),
        scratch_shapes=[pltpu.VMEM(((seqs * N_HEADS), HEAD, HEAD), F32)] + (Exchange.scratch(nx) if nx else []),
        compiler_params=_cparams(("arbitrary", "arbitrary")), name="delta_fwd")(_by_sequence(qkv, lp), _by_sequence(ba, lp), alog, dtb, *gather)
    return [out[0].reshape(t, N_HEADS * HEAD)] + list(out[1:])


def delta_bwd(qkv, ba, ba_block, alog, dtb, states, t_invs, do, lp, scatter=()):
    t = qkv.shape[0]
    nb, nc = t // lp, lp // CHUNK
    seqs, cps = DELTA_STEP_BWD
    ng, rows = nc // cps, cps * CHUNK
    nx = len(scatter)
    nbg = nb // seqs

    def body(*refs):
        qkv_ref, ba_ref, al_ref, dt_ref, s_ref, t_ref, do_ref = refs[:7]
        dqkv_ref, dba_ref, dal_ref, ddt_ref = refs[7 + nx:11 + nx]
        dstate_ref = refs[11 + 2 * nx]
        b, step = pl.program_id(0), pl.program_id(1)
        finish = _ride(scatter, True, refs[7:7 + nx], refs[11 + nx:11 + 2 * nx], refs[12 + 2 * nx:], (b == 0) & (step == 0),
                       (b == nbg - 1) & (step == ng - 1))

        @pl.when(step == 0)
        def _():
            dstate_ref[...] = jnp.zeros_like(dstate_ref)

        @pl.when((b == 0) & (step == 0))
        def _():
            dal_ref[...] = jnp.zeros_like(dal_ref)
            ddt_ref[...] = jnp.zeros_like(ddt_ref)

        al, dtv = al_ref[...], dt_ref[...]
        d_al = jnp.zeros((1, HEAD), F32)
        d_dt = jnp.zeros((1, HEAD), F32)
        for c in reversed(range(cps)):
            rs = slice(c * CHUNK, (c + 1) * CHUNK)
            t_n = jnp.stack([t_ref[i // N_HEADS, i % N_HEADS, c] for i in range((seqs * N_HEADS))])
            s_n = jnp.stack([s_ref[i // N_HEADS, i % N_HEADS, c] for i in range((seqs * N_HEADS))])

            def f(q_, k_, v_, ba_, al_, dt_, s_, t_n=t_n):
                return _delta_chunk(q_, k_, v_, ba_, al_, dt_, s_, t_n)[:2]

            _, vjp = jax.vjp(f, _qkv_heads(qkv_ref, rs, 0), _qkv_heads(qkv_ref, rs, 1), _qkv_heads(qkv_ref, rs, 2), tuple(ba_ref[i, rs, :] for i in range(seqs)), al, dtv, s_n)
            grads = vjp((_heads_of(do_ref, rs, 0), dstate_ref[...]))
            for part in range(3):
                for i in range((seqs * N_HEADS)):
                    col = (part * N_HEADS + i % N_HEADS) * HEAD
                    dqkv_ref[i // N_HEADS, rs, col:col + HEAD] = grads[part][i]
            for i in range(seqs):
                dba_ref[i, rs, :] = grads[3][i]
            d_al, d_dt = d_al + grads[4], d_dt + grads[5]
            dstate_ref[...] = grads[6]
        dal_ref[...] += d_al
        ddt_ref[...] += d_dt
        finish()

    rows_of = lambda width: pl.BlockSpec((seqs, rows, width), lambda b, n: (b, ng - 1 - n, 0))
    par_spec = pl.BlockSpec((1, HEAD), lambda b, n: (0, 0))
    out = pl.pallas_call(
        body, grid=(nbg, ng),
        in_specs=[rows_of(3 * N_HEADS * HEAD), pl.BlockSpec((seqs, rows, HEAD), lambda b, n: (b, ng - 1 - n, ba_block)), par_spec, par_spec,
                  pl.BlockSpec((seqs, N_HEADS, cps, HEAD, HEAD), lambda b, n: (b, 0, ng - 1 - n, 0, 0)),
                  pl.BlockSpec((seqs, N_HEADS, cps, CHUNK, CHUNK), lambda b, n: (b, 0, ng - 1 - n, 0, 0)), rows_of(N_HEADS * HEAD)] + [_HBM] * nx,
        out_specs=[rows_of(3 * N_HEADS * HEAD), rows_of(HEAD), par_spec, par_spec] + [_HBM] * nx,
        out_shape=[jax.ShapeDtypeStruct((nb, lp, 3 * N_HEADS * HEAD), F32), jax.ShapeDtypeStruct((nb, lp, HEAD), F32),
                   jax.ShapeDtypeStruct((1, HEAD), F32), jax.ShapeDtypeStruct((1, HEAD), F32)] + Exchange.out_shape(scatter, True),
        scratch_shapes=[pltpu.VMEM(((seqs * N_HEADS), HEAD, HEAD), F32)] + (Exchange.scratch(nx) if nx else []),
        compiler_params=_cparams(("arbitrary", "arbitrary")), name="delta_bwd")(
            _by_sequence(qkv, lp), _by_sequence(ba, lp), alog, dtb, states, t_invs, _by_sequence(do, lp), *scatter)
    return [out[0].reshape(t, 3 * N_HEADS * HEAD), out[1].reshape(t, HEAD)] + list(out[2:])


ATT_Q_TILE = 256
ATT_K_TILE = 512
ATT_SCALE = QK_DIM ** -0.5


def _tiles(end, size):
    return [(s, min(s + size, end)) for s in range(0, end, size)]


def _att_visible(q0, q1, k0, k1, keys_first):
    if k1 <= q0 + CHUNK and k0 >= PAD_ROWS:
        return None
    shape = (k1 - k0, q1 - q0) if keys_first else (q1 - q0, k1 - k0)
    qpos = q0 + lax.broadcasted_iota(jnp.int32, shape, 1 if keys_first else 0)
    kpos = k0 + lax.broadcasted_iota(jnp.int32, shape, 0 if keys_first else 1)
    shift = CHUNK.bit_length() - 1
    return (jnp.right_shift(kpos, shift) <= jnp.right_shift(qpos, shift)) & (kpos >= PAD_ROWS)


def _att_seq_specs(lp):
    return pl.BlockSpec((lp, QK_PAD), lambda b, h: (b, h)), pl.BlockSpec((lp, HEAD), lambda b, h: (b, h))


def flash_fwd(q, k, v, lp):
    t = q.shape[0]
    qk_seq, o_seq = _att_seq_specs(lp)

    def body(q_ref, k_ref, v_ref, o_ref, lse_ref):
        q_tiles = _tiles(lp, ATT_Q_TILE)

        def score_steps(q0, q1, out):
            def step(k0, k1):
                s = mm_nt(q_ref[q0:q1, :], k_ref[k0:k1, :])
                vis = _att_visible(q0, q1, k0, k1, False)
                s = s if vis is None else jnp.where(vis, s, NEG)
                out["scores"].append(s)
                row_max = jnp.max(s, -1, keepdims=True)
                out["m"] = row_max if out["m"] is None else jnp.maximum(out["m"], row_max)
            return [functools.partial(step, k0, k1) for k0, k1 in _tiles(q1, ATT_K_TILE)]

        cur = {"scores": [], "m": None}
        for step in score_steps(*q_tiles[0], cur):
            step()
        for i, (q0, q1) in enumerate(q_tiles):
            nxt = {"scores": [], "m": None}
            ahead = score_steps(*q_tiles[i + 1], nxt) if i + 1 < len(q_tiles) else []
            l = jnp.zeros((q1 - q0, 1), F32)
            acc = jnp.zeros((q1 - q0, HEAD), F32)
            for s, (k0, k1) in zip(cur["scores"], _tiles(q1, ATT_K_TILE), strict=True):
                if ahead:
                    ahead.pop(0)()
                p = jnp.exp(s - cur["m"])
                l = l + jnp.sum(p, -1, keepdims=True)
                acc = acc + mm_nn(p, v_ref[k0:k1, :])
            for step in ahead:
                step()
            o_ref[q0:q1, :] = acc / l
            lse_ref[q0:q1, :] = jnp.broadcast_to(cur["m"] + jnp.log(l), (q1 - q0, HEAD))
            cur = nxt

    big = jax.ShapeDtypeStruct((t, N_HEADS * HEAD), F32)
    return pl.pallas_call(
        body, grid=(t // lp, N_HEADS), in_specs=[qk_seq, qk_seq, o_seq], out_specs=[o_seq, o_seq], out_shape=[big, big],
        compiler_params=_cparams(("arbitrary", "arbitrary")), name="flash_fwd")(q, k, v)


def flash_bwd(q, k, v, o, lse, do, lp):
    t = q.shape[0]
    qk_seq, o_seq = _att_seq_specs(lp)

    def body(q_ref, k_ref, v_ref, o_ref, lse_ref, do_ref, dq_ref, dk_out_ref, dv_out_ref, dk_ref, dv_ref):
        dk_ref[...] = jnp.zeros_like(dk_ref)
        dv_ref[...] = jnp.zeros_like(dv_ref)
        for q0, q1 in _tiles(lp, ATT_Q_TILE):
            qb, dob = q_ref[q0:q1, :], do_ref[q0:q1, :]
            lse_row = jnp.transpose(lse_ref[q0:q1, :])[0:1, :]
            dsum_row = jnp.sum(jnp.transpose(dob * o_ref[q0:q1, :]), axis=0, keepdims=True)
            dq = jnp.zeros((q1 - q0, QK_PAD), F32)
            for k0, k1 in _tiles(q1, ATT_K_TILE):
                kb, vb = k_ref[k0:k1, :], v_ref[k0:k1, :]
                s = mm_nt(kb, qb)
                vis = _att_visible(q0, q1, k0, k1, True)
                s = s if vis is None else jnp.where(vis, s, NEG)
                p = jnp.exp(s - lse_row)
                ds = p * (mm_nt(vb, dob) - dsum_row)
                dv_ref[k0:k1, :] += mm_nn(p, dob)
                dk_ref[k0:k1, :] += mm_nn(ds, qb)
                dq = dq + mm_tn(ds, kb)
            dq_ref[q0:q1, :] = dq.astype(dq_ref.dtype)
        dk_out_ref[...] = dk_ref[...].astype(dk_out_ref.dtype)
        dv_out_ref[...] = dv_ref[...].astype(dv_out_ref.dtype)

    narrow = _MXU_DTYPE
    return pl.pallas_call(
        body, grid=(t // lp, N_HEADS), in_specs=[qk_seq, qk_seq, o_seq, o_seq, o_seq, o_seq], out_specs=[qk_seq, qk_seq, o_seq],
        out_shape=[jax.ShapeDtypeStruct((t, N_HEADS * QK_PAD), narrow), jax.ShapeDtypeStruct((t, N_HEADS * QK_PAD), narrow),
                   jax.ShapeDtypeStruct((t, N_HEADS * HEAD), narrow)],
        scratch_shapes=[pltpu.VMEM((lp, QK_PAD), F32), pltpu.VMEM((lp, HEAD), F32)],
        compiler_params=_cparams(("arbitrary", "arbitrary")), name="flash_bwd")(q, k, v, o, lse, do)


def loss_head(h2, target, lp):
    nb, seq, d = target.shape
    cols = _pick(d, (512, 128))
    ncol = d // cols

    def body(h_ref, t_ref, loss_ref, dh_ref, acc_ref):
        b, j = pl.program_id(0), pl.program_id(1)

        @pl.when((b == 0) & (j == 0))
        def _():
            acc_ref[...] = jnp.zeros_like(acc_ref)

        err = h_ref[LEAD:, :] - t_ref[...]
        dh_ref[:LEAD, :] = jnp.zeros((LEAD, cols), F32)
        dh_ref[LEAD:, :] = err * (1.0 / d)
        acc_ref[...] += jnp.sum(err * err, axis=0, keepdims=True)

        @pl.when((b == nb - 1) & (j == ncol - 1))
        def _():
            loss_ref[...] = jnp.sum(acc_ref[...], axis=1, keepdims=True) * (0.5 / d)

    return pl.pallas_call(
        body, grid=(nb, ncol),
        in_specs=[pl.BlockSpec((None, lp, cols), lambda b, j: (b, 0, j)), pl.BlockSpec((None, seq, cols), lambda b, j: (b, 0, j))],
        out_specs=[pl.BlockSpec((1, 1), lambda b, j: (0, 0)), pl.BlockSpec((None, lp, cols), lambda b, j: (b, 0, j))],
        out_shape=[jax.ShapeDtypeStruct((1, 1), F32), jax.ShapeDtypeStruct((nb, lp, d), F32)],
        scratch_shapes=[pltpu.VMEM((1, cols), F32)], compiler_params=_cparams(("arbitrary", "arbitrary")), name="loss_head")(h2, target)


def gated_out(name, o, gate, gain, w, res):
    t, kw = o.shape
    d = w.shape[1]
    tm = _pick(t, (512, 256, 128))

    def body(*refs):
        o_ref, gate_ref = refs[:2]
        w_ref, r_ref, h_ref, g_ref = refs[-4:]
        if gain is None:
            g_ref[...] = _f_gate(o_ref[...], gate_ref[...])[0].astype(g_ref.dtype)
        else:
            for h in range(N_HEADS):
                cs = slice(h * HEAD, (h + 1) * HEAD)
                g_ref[:, cs] = _f_out_gate(o_ref[:, cs], gate_ref[:, cs], refs[2][...])[0].astype(g_ref.dtype)
        h_ref[...] = r_ref[...] + _dot(g_ref[...], w_ref[...], ((1,), (0,)))

    rows = lambda width: pl.BlockSpec((tm, width), lambda i: (i, 0))
    whole = lambda a: pl.BlockSpec(a.shape, lambda i: (0, 0))
    params = [] if gain is None else [gain]
    return pl.pallas_call(
        body, grid=(t // tm,), in_specs=[rows(kw), rows(kw)] + [whole(p) for p in params] + [whole(w), rows(d)], out_specs=[rows(d), rows(kw)],
        out_shape=[jax.ShapeDtypeStruct((t, d), F32), jax.ShapeDtypeStruct((t, kw), _MXU_DTYPE)],
        compiler_params=_cparams(("parallel",)), name=name)(o, gate, *params, w, res)


def embed_norm(x, meta, gain, lp, gather=()):
    nb, seq, d = x.shape
    nblk, nx = lp // LEAD, len(gather)

    def body(*refs):
        x_ref, meta_ref, g_ref = refs[:3]
        h_ref, hn_ref = refs[3 + nx:5 + nx]
        b, i = pl.program_id(0), pl.program_id(1)
        finish = _ride(gather, False, refs[3:3 + nx], refs[5 + nx:5 + 2 * nx], refs[5 + 2 * nx:], (b == 0) & (i == 0), (b == nb - 1) & (i == nblk - 1),
                       two_level=True)

        @pl.when(i == 0)
        def _():
            h_ref[:PAD_ROWS, :] = jnp.zeros((PAD_ROWS, d), F32)
            h_ref[PAD_ROWS:, :] = meta_ref[...]

        @pl.when(i > 0)
        def _():
            h_ref[...] = x_ref[...]

        hn_ref[...] = _rms(h_ref[...], g_ref[...]).astype(hn_ref.dtype)
        finish()

    rows = pl.BlockSpec((LEAD, d), lambda b, i: (b * nblk + i, 0))
    out = pl.pallas_call(
        body, grid=(nb, nblk),
        in_specs=[pl.BlockSpec((None, LEAD, d), lambda b, i: (b, jnp.maximum(i - 1, 0), 0)), pl.BlockSpec((N_META, d), lambda b, i: (0, 0)),
                  pl.BlockSpec((1, d), lambda b, i: (0, 0))] + [_HBM] * nx,
        out_specs=[rows, rows] + [_HBM] * nx,
        out_shape=[jax.ShapeDtypeStruct((nb * lp, d), F32), jax.ShapeDtypeStruct((nb * lp, d), _MXU_DTYPE)] + Exchange.out_shape(gather, False),
        scratch_shapes=Exchange.scratch(nx) if nx else [],
        compiler_params=_cparams(("arbitrary", "arbitrary")), name="embed_norm")(x, meta, gain, *gather)
    return list(out)


def meta_grad(dh0):
    nb, _, d = dh0.shape

    def body(g_ref, o_ref):
        @pl.when(pl.program_id(0) == 0)
        def _():
            o_ref[...] = jnp.zeros_like(o_ref)

        o_ref[...] += g_ref[PAD_ROWS:LEAD, :]

    return pl.pallas_call(
        body, grid=(nb,), in_specs=[pl.BlockSpec((None, LEAD, d), lambda b: (b, 0, 0))],
        out_specs=pl.BlockSpec((N_META, d), lambda b: (0, 0)), out_shape=jax.ShapeDtypeStruct((N_META, d), F32),
        compiler_params=_cparams(("arbitrary",)), name="meta_grad")(dh0)


_HBM = pl.BlockSpec(memory_space=pltpu.HBM)


def _mesh_pos():
    x, y, c = lax.axis_index("x"), lax.axis_index("y"), lax.axis_index("c")
    return x, y, c


def _peer(x, y, c, k):
    px = 1 - x if k & 4 else x
    py = 1 - y if k & 2 else y
    pc = 1 - c if k & 1 else c
    return (px, py, pc), 4 * px + 2 * py + pc


class Exchange:
    def __init__(self, x_refs, out_refs, send_sems, recv_sems, local_sems, scatter):
        self.x_refs, self.out_refs, self.scatter = x_refs, out_refs, scatter
        self.send_sems, self.recv_sems, self.local_sems = send_sems, recv_sems, local_sems
        self.pos = _mesh_pos()
        x, y, c = self.pos
        self.me = 4 * x + 2 * y + c

    @staticmethod
    def scratch(n):
        return [pltpu.SemaphoreType.DMA((n, N_DEV - 1)), pltpu.SemaphoreType.DMA((n, N_DEV - 1)), pltpu.SemaphoreType.DMA((n,))]

    @staticmethod
    def out_shape(bufs, scatter):
        return [jax.ShapeDtypeStruct(b.shape if scatter else (N_DEV,) + b.shape, b.dtype) for b in bufs]

    def _local(self, i):
        return pltpu.make_async_copy(self.x_refs[i].at[self.me] if self.scatter else self.x_refs[i], self.out_refs[i].at[self.me], self.local_sems.at[i])

    def _copy(self, i, k, landing):
        peer, peer_id = _peer(*self.pos, k)
        src = self.x_refs[i].at[peer_id] if self.scatter else self.x_refs[i]
        return pltpu.make_async_remote_copy(src_ref=src, dst_ref=self.out_refs[i].at[peer_id if landing else self.me],
                                            send_sem=self.send_sems.at[i, k - 1], recv_sem=self.recv_sems.at[i, k - 1],
                                            device_id=peer, device_id_type=pl.DeviceIdType.MESH)

    def start(self):
        for i in range(len(self.x_refs)):
            self._local(i).start()
        for k in range(1, N_DEV):
            for i in range(len(self.x_refs)):
                self._copy(i, k, False).start()

    def wait(self):
        for k in range(1, N_DEV):
            for i in range(len(self.x_refs)):
                self._copy(i, k, True).wait_recv()
        for k in range(1, N_DEV):
            for i in range(len(self.x_refs)):
                self._copy(i, k, False).wait_send()
        for i in range(len(self.x_refs)):
            self._local(i).wait()


class TwoLevelGather(Exchange):
    DIRECT = (1, 4, 2, 6)
    FROM_CHIPS = (4, 2, 6)

    def _forward(self, i, k):
        _, origin = _peer(*self.pos, k)
        sibling, _ = _peer(*self.pos, 1)
        block = self.out_refs[i].at[origin]
        return pltpu.make_async_remote_copy(src_ref=block, dst_ref=block, send_sem=self.send_sems.at[i, (k ^ 1) - 1],
                                            recv_sem=self.recv_sems.at[i, (k ^ 1) - 1], device_id=sibling, device_id_type=pl.DeviceIdType.MESH)

    def start(self):
        assert not self.scatter
        for i in range(len(self.x_refs)):
            self._local(i).start()
        for k in self.DIRECT:
            for i in range(len(self.x_refs)):
                self._copy(i, k, False).start()

    def wait(self):
        n = range(len(self.x_refs))
        for k in self.FROM_CHIPS:
            for i in n:
                self._copy(i, k, True).wait_recv()
                self._forward(i, k).start()
        for k in (1, 5, 3, 7):
            for i in n:
                self._copy(i, k, True).wait_recv()
        for k in self.DIRECT:
            for i in n:
                self._copy(i, k, False).wait_send()
        for k in self.FROM_CHIPS:
            for i in n:
                self._forward(i, k).wait_send()
        for i in n:
            self._local(i).wait()


def _exchange(name, bufs, scatter):
    n = len(bufs)

    def body(*refs):
        ex = Exchange(refs[:n], refs[n:2 * n], *refs[2 * n:], scatter)
        ex.start()
        ex.wait()

    return pl.pallas_call(body, in_specs=[_HBM] * n, out_specs=[_HBM] * n, out_shape=Exchange.out_shape(bufs, scatter),
                          scratch_shapes=Exchange.scratch(n), name=name)(*bufs)


def _f_rms(x, g):
    return (_rms(x, g),)


def _f_rms2(x, g1, g2):
    r = x * lax.rsqrt(jnp.sum(x * x, -1, keepdims=True) / x.shape[-1] + EPS)
    return r * g1, r * g2


@jax.custom_vjp
def _out_gate(o, gate, gain):
    return _rms(o, gain) * _silu(gate)


def _out_gate_bwd(res, g):
    o, gate, gain = res
    r = lax.rsqrt(jnp.sum(o * o, -1, keepdims=True) / o.shape[-1] + EPS)
    n = o * r
    s = _sigmoid(gate)
    g_norm = g * (gate * s)
    d_gate = g * (n * gain) * (s * (1.0 + gate * (1.0 - s)))
    gn = g_norm * gain
    d_o = r * (gn - n * (jnp.sum(gn * n, -1, keepdims=True) / o.shape[-1]))
    return d_o, d_gate, jnp.sum(g_norm * n, 0, keepdims=True)


_out_gate.defvjp(lambda o, gate, gain: (_out_gate(o, gate, gain), (o, gate, gain)), _out_gate_bwd)


def _f_out_gate(o, gate, gain):
    return (_out_gate(o, gate, gain),)


def _f_gate(o, gate):
    return (o * _silu(gate),)


def _swap_rope_halves(x):
    return pltpu.roll(x, ROPE // 2, 1) + pltpu.roll(x, HEAD - ROPE // 2, 1)


def _qk_final_inv_rms(nope, rope_in):
    ms = (jnp.sum(nope * nope, -1, keepdims=True) + jnp.sum(rope_in * rope_in, -1, keepdims=True)) / QK_DIM
    return lax.rsqrt(ms + EPS)


@functools.partial(jax.custom_vjp, nondiff_argnums=(0,))
def _qk_final(scale, nope, rope_in, g_nope, g_rope, cos, sin):
    r = _qk_final_inv_rms(nope, rope_in)
    b = rope_in * (r * g_rope)
    out = jnp.concatenate([nope * (r * g_nope), b * cos + _swap_rope_halves(b) * sin], axis=1)
    return out if scale == 1.0 else out * scale


def _qk_final_fwd(scale, nope, rope_in, g_nope, g_rope, cos, sin):
    return _qk_final(scale, nope, rope_in, g_nope, g_rope, cos, sin), (nope, rope_in, g_nope, g_rope, cos, sin)


def _qk_final_bwd(scale, res, g):
    nope, rope_in, g_nope, g_rope, cos, sin = res
    r = _qk_final_inv_rms(nope, rope_in)
    ga, gb = g[:, :HEAD], g[:, HEAD:]
    if scale != 1.0:
        ga, gb = ga * scale, gb * scale
    db = gb * cos + _swap_rope_halves(gb * sin)
    t_a, t_b = ga * nope, db * rope_in
    d_r = jnp.sum(t_a * g_nope + t_b * g_rope, -1, keepdims=True)
    c = d_r * (r * r * r) * (-1.0 / QK_DIM)
    d_nope = ga * (r * g_nope) + nope * c
    d_rope = db * (r * g_rope) + rope_in * c
    d_g_nope = jnp.sum(t_a * r, 0, keepdims=True)
    d_g_rope = jnp.sum(t_b * r, 0, keepdims=True)
    return d_nope, d_rope, d_g_nope, d_g_rope, jnp.zeros_like(cos), jnp.zeros_like(sin)


_qk_final.defvjp(_qk_final_fwd, _qk_final_bwd)


def _f_qk_final(scale, nope, rope_in, g_nope, g_rope, cos, sin):
    return (_qk_final(scale, nope, rope_in, g_nope, g_rope, cos, sin),)


def _rope_tables(lp):
    half = ROPE // 2
    pos = jnp.maximum(jnp.arange(lp) - PAD_ROWS, 0)
    inv = ROPE_THETA ** (-jnp.arange(half, dtype=F32) / half)
    ang = pos.astype(F32)[:, None] * inv[None, :]
    zeros = jnp.zeros((lp, HEAD - ROPE), F32)
    cos = jnp.concatenate([jnp.cos(ang), jnp.cos(ang), zeros], 1)
    sin = jnp.concatenate([-jnp.sin(ang), jnp.sin(ang), zeros], 1)
    return cos, sin


def _pad_lanes(w, width=HEAD):
    return jnp.pad(w, ((0, 0), (0, width - w.shape[1])))


def _pad_rows(w, rows=HEAD):
    return jnp.pad(w, ((0, rows - w.shape[0]), (0, 0)))


def _split_heads_qk_t(w_t):
    k = w_t.shape[1]
    return jnp.pad(w_t.reshape(N_HEADS, QK_DIM, k), ((0, 0), (0, QK_PAD - QK_DIM), (0, 0))).reshape(N_HEADS * QK_PAD, k)


def _merge_heads_qk_t(g_t):
    k = g_t.shape[1]
    return g_t.reshape(N_HEADS, QK_PAD, k)[:, :QK_DIM].reshape(N_HEADS * QK_DIM, k)


@functools.partial(jax.custom_vjp, nondiff_argnums=(0,))
def _q_final(scale, qh, g_nope, g_rope, cos, sin):
    return _qk_final(scale, qh[:, :HEAD], qh[:, HEAD:], g_nope, g_rope, cos, sin)


def _q_final_bwd(scale, res, g):
    qh, g_nope, g_rope, cos, sin = res
    grads = _qk_final_bwd(scale, (qh[:, :HEAD], qh[:, HEAD:], g_nope, g_rope, cos, sin), g)
    return (jnp.concatenate(grads[:2], axis=1),) + tuple(grads[2:])


_q_final.defvjp(lambda scale, qh, *rest: (_q_final(scale, qh, *rest), (qh,) + rest), _q_final_bwd)


def _f_q_final(scale, qh, g_nope, g_rope, cos, sin):
    return (_q_final(scale, qh, g_nope, g_rope, cos, sin),)


def local_step(x, target, w, deferred=None):
    nb, seq, d = x.shape
    lp = seq + LEAD
    t = nb * lp
    tr = _pick(lp, (544, 128))
    ntab = lp // tr
    mxu = _MXU_DTYPE
    kw = N_HEADS * HEAD

    a_conv = w["a_conv"].T
    alog, dtb, o_gain = _pad_lanes(w["a_log"]), _pad_lanes(w["a_dt_bias"]), w["a_o_gain"]
    a_norm, kv_norm, b_norm = w["a_norm"], w["kv_norm"][None, :], w["b_norm"]
    lat_norm, qlat_norm = w["kv_latent_norm"][None, :], w["b_q_latent_norm"]
    kg_nope, kg_rope = w["k_gain"][None, :HEAD], _pad_lanes(w["k_gain"][None, HEAD:])
    qg_nope, qg_rope = w["b_q_gain"][:, :HEAD], _pad_lanes(w["b_q_gain"][:, HEAD:])
    cos, sin = _rope_tables(lp)

    h0, hn, *gathered = embed_norm(x, w["meta_tokens"].T, a_norm, lp, gather=deferred.first_gather_bufs if deferred else ())
    if deferred:
        w = {**w, **deferred.finish_first(gathered)}
    a_w_in_t = w["a_w_in"].astype(mxu)
    w_qkv_t, w_gba_t = a_w_in_t[:3 * kw], _pad_rows(a_w_in_t[3 * kw:], kw + HEAD)
    z_qkv = matmul("a_in_qkv", hn, w_qkv_t, "nt")
    z_gba = matmul("a_in_gate_ba", hn, w_gba_t, "nt")
    ba_block = kw // HEAD
    qkv_a, y_conv = conv_fwd(z_qkv, a_conv, lp)
    o_a, states, t_invs, *gathered = delta_fwd(qkv_a, z_gba, ba_block, alog, dtb, lp, gather=deferred.gather_bufs if deferred else ())
    if deferred:
        w = {**w, **deferred.finish(gathered)}
    a_w_out = w["a_w_out"].astype(mxu)
    w_down = _pad_lanes(w["kv_w_down"], KV_RANK + HEAD).astype(mxu)
    w_ukv_t = jnp.concatenate([w["kv_w_uk"], w["kv_w_uv"]], 0).astype(mxu)
    b_w_in_t = w["b_w_in"].astype(mxu)
    w_cq_t, w_gb_t = b_w_in_t[:Q_RANK], b_w_in_t[Q_RANK:]
    w_q_t = _split_heads_qk_t(w["b_w_uq"]).astype(mxu)
    b_w_out = w["b_w_out"].astype(mxu)
    og_args = [Arg(o_a, bc=HEAD, ph=True, diff=True), Arg(z_gba, bc=HEAD, ph=True, diff=True, gdt=mxu), Arg(o_gain, "par", diff=True)]
    h1, og_a = gated_out("a_out", o_a, z_gba, o_gain, a_w_out, h0)

    hk, hb = row_call("b_norms_fwd", _f_rms2, [Arg(h1), Arg(kv_norm, "par"), Arg(b_norm, "par")], [(d, mxu, d, False), (d, mxu, d, False)], tr)
    c_down = matmul("kv_down", hk, w_down, "nn")
    c_kv_arg = Arg(c_down, bc=KV_RANK, diff=True, gdt=mxu)
    k_pe_arg = Arg(c_down, bc=HEAD, base=KV_RANK // HEAD, diff=True)
    c_q_raw = matmul("b_in_q", hb, w_cq_t, "nt")
    gate_b = matmul("b_in_gate", hb, w_gb_t, "nt")
    (c_kv,) = row_call("kv_latent_fwd", _f_rms, [c_kv_arg, Arg(lat_norm, "par")], [(KV_RANK, mxu, KV_RANK, False)], tr)
    (c_q,) = row_call("q_latent_fwd", _f_rms, [Arg(c_q_raw), Arg(qlat_norm, "par")], [(Q_RANK, mxu, Q_RANK, False)], tr)
    k_nope = matmul("k_up", c_kv, w_ukv_t[:kw], "nt")
    v_b = matmul("v_up", c_kv, w_ukv_t[kw:], "nt", out_dtype=mxu)
    q_up = matmul("q_up", c_q, w_q_t, "nt")
    tabs = [Arg(cos, "tab"), Arg(sin, "tab")]
    k_args = [Arg(k_nope, bc=HEAD, ph=True, diff=True, gdt=mxu), k_pe_arg, Arg(kg_nope, "par", diff=True), Arg(kg_rope, "par", diff=True)] + tabs
    q_args = [Arg(q_up, bc=QK_PAD, ph=True, diff=True, gdt=mxu), Arg(qg_nope, "par", diff=True), Arg(qg_rope, "par", diff=True)] + tabs
    f_k_final, f_q_final = functools.partial(_f_qk_final, 1.0), functools.partial(_f_q_final, ATT_SCALE)
    (k_fin,) = row_call("k_final_fwd", f_k_final, k_args, [(N_HEADS * QK_PAD, mxu, QK_PAD, True)], tr, nh=N_HEADS, ntab=ntab)
    (q_fin,) = row_call("q_final_fwd", f_q_final, q_args, [(N_HEADS * QK_PAD, mxu, QK_PAD, True)], tr, nh=N_HEADS, ntab=ntab)
    o_b, lse = flash_fwd(q_fin, k_fin, v_b, lp)
    gb_args = [Arg(o_b, diff=True), Arg(gate_b, diff=True, gdt=mxu)]
    h2, og_b = gated_out("b_out", o_b, gate_b, None, b_w_out, h1)

    loss, dh2 = loss_head(h2.reshape(nb, lp, d), target, lp)
    dh2 = dh2.reshape(t, d)
    grads = {}

    d_og_b = matmul("b_out_dx", dh2, b_w_out, "nt", out_dtype=mxu)
    grads["b_w_out"] = matmul("b_out_dw", og_b, dh2, "tn")
    d_o_b, d_gate_b = row_vjp_call("b_gate_bwd", _f_gate, gb_args, [Arg(d_og_b)], tr)
    dq_fin, dk_fin, dv_b = flash_bwd(q_fin, k_fin, v_b, o_b, lse, d_o_b, lp)
    dq_up, d_qg_nope, d_qg_rope = row_vjp_call(
        "q_final_bwd", f_q_final, q_args, [Arg(dq_fin, bc=QK_PAD, ph=True)], tr, nh=N_HEADS, ntab=ntab)
    dk_nope, dk_pe, d_kg_nope, d_kg_rope = row_vjp_call(
        "k_final_bwd", f_k_final, k_args, [Arg(dk_fin, bc=QK_PAD, ph=True)], tr, nh=N_HEADS, ntab=ntab)
    grads["b_q_gain"] = jnp.concatenate([d_qg_nope, d_qg_rope[:, :ROPE]], 1)
    grads["k_gain"] = jnp.concatenate([d_kg_nope, d_kg_rope[:, :ROPE]], 1)[0]
    d_c_q = matmul("q_up_dx", dq_up, w_q_t, "nn")
    grads["b_w_uq"] = _merge_heads_qk_t(matmul("q_up_dw", dq_up, c_q, "tn"))
    d_c_kv = matmul("k_up_dx", dk_nope, w_ukv_t[:kw], "nn")
    d_c_kv = matmul("v_up_dx", dv_b, w_ukv_t[kw:], "nn", res=d_c_kv)
    grads["kv_w_uk"], grads["kv_w_uv"] = matmul("k_up_dw", dk_nope, c_kv, "tn"), matmul("v_up_dw", dv_b, c_kv, "tn")
    d_c_q_raw, grads["b_q_latent_norm"] = row_vjp_call(
        "q_latent_bwd", _f_rms, [Arg(c_q_raw, diff=True, gdt=mxu), Arg(qlat_norm, "par", diff=True)], [Arg(d_c_q)], tr)
    d_c_kv_raw, d_lat = row_vjp_call(
        "kv_latent_bwd", _f_rms, [c_kv_arg, Arg(lat_norm, "par", diff=True)], [Arg(d_c_kv)], tr)
    grads["kv_latent_norm"] = d_lat[0]
    d_hb = matmul("b_in_q_dx", d_c_q_raw, w_cq_t, "nn")
    d_hb = matmul("b_in_gate_dx", d_gate_b, w_gb_t, "nn", res=d_hb, out_dtype=mxu)
    grads["b_w_in"] = jnp.concatenate([matmul("b_in_q_dw", d_c_q_raw, hb, "tn"), matmul("b_in_gate_dw", d_gate_b, hb, "tn")], 0)
    d_c_down = jnp.concatenate([d_c_kv_raw, dk_pe.astype(mxu)], 1)
    d_hk = matmul("kv_down_dx", d_c_down, w_down, "nt", out_dtype=mxu)
    grads["kv_w_down"] = matmul("kv_down_dw", hk, d_c_down, "tn")[:, :KV_RANK + ROPE]
    dh1, d_kv_norm, grads["b_norm"] = row_vjp_call(
        "b_norms_bwd", lambda x_, g1, g2: _f_rms2(x_, g1, g2) + (x_,),
        [Arg(h1, diff=True), Arg(kv_norm, "par", diff=True), Arg(b_norm, "par", diff=True)], [Arg(d_hk), Arg(d_hb), Arg(dh2)], tr)
    grads["kv_norm"] = d_kv_norm[0]

    d_og_a = matmul("a_out_dx", dh1, a_w_out, "nt", out_dtype=mxu)
    grads["a_w_out"] = matmul("a_out_dw", og_a, dh1, "tn")
    d_o_a, d_gate_a, grads["a_o_gain"] = row_vjp_call(
        "a_out_gate_bwd", _f_out_gate, og_args, [Arg(d_og_a, bc=HEAD, ph=True)], tr, nh=N_HEADS)
    dqkv_a, d_ba, d_alog, d_dtb, *received = delta_bwd(qkv_a, z_gba, ba_block, alog, dtb, states, t_invs, d_o_a, lp,
                                                        scatter=deferred.scatter_bufs(grads) if deferred else ())
    grads["a_log"], grads["a_dt_bias"] = d_alog[:, :N_HEADS], d_dtb[:, :N_HEADS]
    dz_qkv, d_conv = conv_bwd(z_qkv, y_conv, a_conv, dqkv_a, lp)
    grads["a_conv"] = d_conv.T
    dz_gba = jnp.concatenate([d_gate_a, d_ba.astype(mxu)], 1)
    grads["a_w_in"] = jnp.concatenate([matmul("a_in_qkv_dw", dz_qkv, hn, "tn"), matmul("a_in_gate_ba_dw", dz_gba, hn, "tn")[:kw + 2 * N_HEADS]], 0)
    ride = deferred.last_scatter_bufs(grads) if deferred else ()
    d_hn = matmul("a_in_qkv_dx", dz_qkv, w_qkv_t, "nn", scatter=ride)
    if ride:
        d_hn, *received_last = d_hn
        received = list(received) + received_last
    d_hn = matmul("a_in_gate_ba_dx", dz_gba, w_gba_t, "nn", res=d_hn, out_dtype=mxu)
    dh0, grads["a_norm"] = row_vjp_call("a_norm_bwd", lambda x_, g_: _f_rms(x_, g_) + (x_,),
                                        [Arg(h0, diff=True), Arg(a_norm, "par", diff=True)], [Arg(d_hn), Arg(dh1)], tr)
    dh0 = dh0.reshape(nb, lp, d)
    grads["meta_tokens"] = meta_grad(dh0).T
    return loss, dh0[:, LEAD:], grads, received


_SHARDED = (
    ("meta_tokens", True, False), ("a_norm", True, False), ("a_w_in", True, True), ("a_conv", True, False), ("a_w_out", False, True),
    ("kv_w_down", False, True), ("kv_w_uk", True, True), ("kv_w_uv", True, True), ("b_w_in", True, True), ("b_w_uq", True, True),
    ("b_w_out", False, True))
_REPLICATED = ("a_log", "a_dt_bias", "a_o_gain", "kv_norm", "kv_latent_norm", "k_gain", "b_norm", "b_q_latent_norm", "b_q_gain")
_ALL_WEIGHTS = ("meta_tokens", "a_norm", "a_w_in", "a_conv", "a_log", "a_dt_bias", "a_o_gain", "a_w_out", "kv_norm", "kv_w_down",
                "kv_latent_norm", "kv_w_uk", "kv_w_uv", "k_gain", "b_norm", "b_w_in", "b_q_latent_norm", "b_w_uq", "b_q_gain", "b_w_out")


def _round_up(n, m):
    return (n + m - 1) // m * m


def _pack_rows(pieces, row_multiple):
    padded = []
    for p in pieces:
        n = p.shape[-1]
        padded.append(jnp.pad(p, [(0, 0)] * (p.ndim - 1) + [(0, _round_up(n, PACK_COLS) - n)]))
    flat = jnp.concatenate(padded, -1)
    rows = _round_up(flat.shape[-1] // PACK_COLS, row_multiple)
    flat = jnp.pad(flat, [(0, 0)] * (flat.ndim - 1) + [(0, rows * PACK_COLS - flat.shape[-1])])
    return flat.reshape(flat.shape[:-1] + (rows, PACK_COLS))


def _unpack_rows(buf, sizes):
    flat = buf.reshape(buf.shape[:-2] + (-1,))
    out, off = [], 0
    for n in sizes:
        out.append(flat[..., off:off + n])
        off += _round_up(n, PACK_COLS)
    return out


def _shard_2d(a):
    return a.reshape(a.shape[-2:]) if a.ndim > 2 else a


def _kl_shard(a, by_cols):
    return _shard_2d(a).T if by_cols else _shard_2d(a)


_GROUPS_FIRST = (("a_w_in",),)
_GROUPS_LATER = (("a_w_out", "b_w_in", "b_w_out"), ("b_w_uq",), ("kv_w_down",), ("kv_w_uk", "kv_w_uv"))
_SMALL_SHARDED = ("meta_tokens", "a_norm", "a_conv")
_BY_COLS = {name: by_cols for name, by_cols, _ in _SHARDED}
ROW_ALIGN = 16


def _stack_rows(pieces):
    padded, starts, row = [], [], 0
    for p in pieces:
        r = p.shape[-2]
        padded.append(jnp.pad(p, [(0, 0)] * (p.ndim - 2) + [(0, _round_up(r, ROW_ALIGN) - r), (0, 0)]))
        starts.append(row)
        row += _round_up(r, ROW_ALIGN)
    return jnp.concatenate(padded, -2), starts


def _stack_group(arrays_by_name, names):
    arrays = [arrays_by_name[n].astype(BF16) for n in names]
    buf, starts = _stack_rows(arrays)
    return buf, [(n, s, a.shape[-2]) for n, s, a in zip(names, starts, arrays, strict=True)]


def _stack_groups(arrays_by_name, groups):
    stacked = [_stack_group(arrays_by_name, names) for names in groups]
    return [b for b, _ in stacked], [entries for _, entries in stacked]


def _full_from_gathered(gathered, layout):
    full = {}
    for got, entries in zip(gathered, layout, strict=True):
        for name, start, rows in entries:
            full[name] = got[:, start:start + rows].reshape(N_DEV * rows, got.shape[-1])
    return full


def gather_small_weights(local):
    small = [_kl_shard(local[n], _BY_COLS[n]) for n in _SMALL_SHARDED]
    (gathered,) = _exchange("all_gather", [_pack_rows([s.reshape(-1) for s in small], 8)], scatter=False)
    full = {}
    for name, part, sh in zip(_SMALL_SHARDED, _unpack_rows(gathered, [s.size for s in small]), small, strict=True):
        full[name] = part.reshape(N_DEV * sh.shape[0], sh.shape[1])
    full["a_norm"] = full["a_norm"].reshape(1, -1)
    return full


class LaterExchanges:
    def __init__(self, local):
        shards = {n: _kl_shard(local[n], _BY_COLS[n]) for names in _GROUPS_FIRST + _GROUPS_LATER for n in names}
        self.first_gather_bufs, self.first_layout = _stack_groups(shards, _GROUPS_FIRST)
        self.gather_bufs, self.layout = _stack_groups(shards, _GROUPS_LATER)

    def finish_first(self, gathered):
        return _full_from_gathered(gathered, self.first_layout)

    def finish(self, gathered):
        return _full_from_gathered(gathered, self.layout)

    def scatter_bufs(self, grads):
        return _stack_groups(_owner_slices(grads, _GROUPS_LATER), _GROUPS_LATER)[0]

    def last_scatter_bufs(self, grads):
        bufs, self.last_layout = _stack_groups(_owner_slices(grads, _GROUPS_FIRST), _GROUPS_FIRST)
        return bufs


def _owner_slices(grads, groups):
    return {n: grads[n].reshape(N_DEV, -1, grads[n].shape[-1]) for names in groups for n in names}


def reduce_contributions(name, recv):
    _, r, c = recv.shape
    tr = max(d for d in range(8, 513, 8) if r % d == 0 and (d % ROW_ALIGN == 0 or recv.dtype == F32))

    def body(g_ref, o_ref):
        g = g_ref[0].astype(F32)
        for dev in range(1, N_DEV):
            g = g + g_ref[dev].astype(F32)
        o_ref[...] = g

    return pl.pallas_call(
        body, grid=(r // tr,), in_specs=[pl.BlockSpec((N_DEV, tr, c), lambda i: (0, i, 0))], out_specs=pl.BlockSpec((tr, c), lambda i: (i, 0)),
        out_shape=jax.ShapeDtypeStruct((r, c), F32), compiler_params=_cparams(("arbitrary",)), name=name)(recv)


def adamw_all(gs, ws, ms, vs):
    n = len(gs)

    def body(*refs):
        for i in range(n):
            g_ref, w_ref, m_ref, v_ref = (refs[j * n + i] for j in range(4))
            d_ref, mo_ref, vo_ref = (refs[(4 + j) * n + i] for j in range(3))
            g = g_ref[...]
            m_new = ADAM_B1 * m_ref[...] + (1.0 - ADAM_B1) * g
            v_new = ADAM_B2 * v_ref[...] + (1.0 - ADAM_B2) * (g * g)
            m_hat = m_new / (1.0 - ADAM_B1 ** ADAM_STEP)
            v_hat = v_new / (1.0 - ADAM_B2 ** ADAM_STEP)
            d_ref[...] = -ADAM_LR * (m_hat / (jnp.sqrt(v_hat) + ADAM_EPS) + ADAM_WD * w_ref[...])
            mo_ref[...] = m_new
            vo_ref[...] = v_new

    out = [jax.ShapeDtypeStruct(g.shape, F32) for g in gs] * 3
    res = pl.pallas_call(body, out_shape=out, compiler_params=pltpu.CompilerParams(vmem_limit_bytes=VMEM_LIMIT), name="adamw_all")(*gs, *ws, *ms, *vs)
    return res[:n], res[n:2 * n], res[2 * n:]


def kernel(x, meta_tokens, a_norm, a_w_in, a_conv, a_log, a_dt_bias, a_o_gain, a_w_out, kv_norm, kv_w_down, kv_latent_norm, kv_w_uk, kv_w_uv, k_gain, b_norm, b_w_in, b_q_latent_norm, b_w_uq, b_q_gain, b_w_out, loss_target, m_meta_tokens, m_a_norm, m_a_w_in, m_a_conv, m_a_log, m_a_dt_bias, m_a_o_gain, m_a_w_out, m_kv_norm, m_kv_w_down, m_kv_latent_norm, m_kv_w_uk, m_kv_w_uv, m_k_gain, m_b_norm, m_b_w_in, m_b_q_latent_norm, m_b_w_uq, m_b_q_gain, m_b_w_out, v_meta_tokens, v_a_norm, v_a_w_in, v_a_conv, v_a_log, v_a_dt_bias, v_a_o_gain, v_a_w_out, v_kv_norm, v_kv_w_down, v_kv_latent_norm, v_kv_w_uk, v_kv_w_uv, v_k_gain, v_b_norm, v_b_w_in, v_b_q_latent_norm, v_b_w_uq, v_b_q_gain, v_b_w_out):
    given = dict(locals())
    local_w = {n: given[n] for n in _ALL_WEIGHTS}
    full = gather_small_weights(local_w)
    for n in _REPLICATED:
        full[n] = local_w[n]
    later = LaterExchanges(local_w)

    loss_part, grad_x, grads, received_riding = local_step(x, loss_target, full, later)

    exact = [grads[n].reshape(N_DEV, -1) for n in _SMALL_SHARDED]
    exact += [jnp.broadcast_to(grads[n].reshape(1, -1), (N_DEV, grads[n].size)) for n in _REPLICATED]
    exact.append(jnp.broadcast_to(loss_part, (N_DEV, 1)))
    received = list(received_riding) + list(_exchange("all_to_all", [_pack_rows(exact, 8)], scatter=True))
    layout = later.layout + later.last_layout
    summed = [reduce_contributions(f"reduce_{i}", r) for i, r in enumerate(received)]

    grad_kl = {}
    for got, entries in zip(summed, layout):
        for n, start, rows in entries:
            grad_kl[n] = got[start:start + rows]
    parts = _unpack_rows(summed[-1], [p.shape[1] for p in exact])
    for n, part in zip(_SMALL_SHARDED + _REPLICATED, parts, strict=False):
        grad_kl[n] = part
    loss = parts[-1][0]

    def natural_2d(n, a):
        shape = _shard_2d(local_w[n]).shape if local_w[n].ndim > 1 else (1, local_w[n].size)
        return a.reshape(shape[::-1]).T if _BY_COLS.get(n, False) else a.reshape(shape)

    as_2d = lambda n, a: a.reshape(natural_2d(n, grad_kl[n]).shape)
    gs = [natural_2d(n, grad_kl[n]) for n in _ALL_WEIGHTS]
    deltas, new_m, new_v = adamw_all(gs, [as_2d(n, local_w[n]) for n in _ALL_WEIGHTS], [as_2d(n, given["m_" + n]) for n in _ALL_WEIGHTS],
                                     [as_2d(n, given["v_" + n]) for n in _ALL_WEIGHTS])
    results = [a.reshape(local_w[n].shape) for group in (gs, deltas, new_m, new_v) for n, a in zip(_ALL_WEIGHTS, group, strict=True)]
    return (loss, grad_x, *results)
```

```python
import dataclasses
import functools
import math

import jax
import jax.numpy as jnp
from jax import lax
from jax.experimental import pallas as pl
from jax.experimental.pallas import tpu as pltpu

F32 = jnp.float32
BF16 = jnp.bfloat16
_MXU_DTYPE = jnp.bfloat16

N_DEV = 8
D_MODEL = 1024
N_HEADS = 8
HEAD = 128
CHUNK = 64
N_META = 16
PAD_ROWS = 2 * CHUNK - N_META
LEAD = PAD_ROWS + N_META
ROPE = 64
QK_DIM = HEAD + ROPE
QK_PAD = 2 * HEAD
KV_RANK = 256
Q_RANK = 384
CONV_K = 4
EPS = 1e-6
NEG = -1e30
ROPE_THETA = 10000.0
ADAM_LR, ADAM_B1, ADAM_B2, ADAM_EPS, ADAM_WD, ADAM_STEP = 0.001, 0.9, 0.999, 1e-08, 0.01, 10
PACK_COLS = 512
VMEM_LIMIT = 56 * 1024 * 1024


def _pick(n, options):
    for o in options:
        if n % o == 0:
            return o
    raise ValueError(f"no tile for {n} among {options}")


def _cparams(sem):
    return pltpu.CompilerParams(dimension_semantics=sem, vmem_limit_bytes=VMEM_LIMIT)


def _dims(a, dims):
    if a.ndim == 2:
        return (dims, ((), ()))
    (ca,), (cb,) = dims
    return (((ca + 1,), (cb + 1,)), ((0,), (0,)))


def _dot(a, b, dims):
    return lax.dot_general(a.astype(_MXU_DTYPE), b.astype(_MXU_DTYPE), _dims(a, dims), preferred_element_type=F32)


@jax.custom_vjp
def mm_nn(a, b):
    return _dot(a, b, ((1,), (0,)))


@jax.custom_vjp
def mm_nt(a, b):
    return _dot(a, b, ((1,), (1,)))


@jax.custom_vjp
def mm_tn(a, b):
    return _dot(a, b, ((0,), (0,)))


mm_nn.defvjp(lambda a, b: (mm_nn(a, b), (a, b)), lambda r, g: (mm_nt(g, r[1]), mm_tn(r[0], g)))
mm_nt.defvjp(lambda a, b: (mm_nt(a, b), (a, b)), lambda r, g: (mm_nn(g, r[1]), mm_tn(g, r[0])))
mm_tn.defvjp(lambda a, b: (mm_tn(a, b), (a, b)), lambda r, g: (mm_nt(r[1], g), mm_nn(r[0], g)))


def _split_terms(x, n):
    terms, rest = [], x
    for _ in range(n):
        t = rest.astype(_MXU_DTYPE)
        terms.append(t)
        rest = rest - t.astype(F32)
    return terms


def _dot_01_raw(m, x, dims):
    m = m.astype(_MXU_DTYPE)
    return sum(lax.dot_general(m, t, _dims(m, dims), preferred_element_type=F32) for t in _split_terms(x, 3))


@jax.custom_vjp
def _dot_01(m, x):
    return _dot_01_raw(m, x, ((1,), (0,)))


_dot_01.defvjp(lambda m, x: (_dot_01(m, x), m), lambda m, g: (jnp.zeros_like(m), _dot_01_raw(m, g, ((0,), (0,)))))


def _inv_unit_lower(a):
    n = a.shape[-1]
    eye = (lax.broadcasted_iota(jnp.int32, (n, n), 0) == lax.broadcasted_iota(jnp.int32, (n, n), 1)).astype(F32)
    d = lambda u, w: lax.dot_general(u, w, _dims(u, ((1,), (0,))), preferred_element_type=F32)
    t = eye - a
    p = a.astype(_MXU_DTYPE)
    p = d(p, p)
    squarings = int(math.log2(n)) - 1
    for s in range(squarings):
        ph = p.astype(_MXU_DTYPE)
        t_hi, t_lo = _split_terms(t, 2)
        t = t + (d(t_hi, ph) + d(t_lo, ph))
        if s + 1 < squarings:
            p = d(ph, ph)
    return t


@jax.custom_vjp
def _inv_lookup(a, t):
    return t


def _inv_lookup_bwd(t, g):
    return -mm_tn(t, mm_nt(g, t)), jnp.zeros_like(t)


_inv_lookup.defvjp(lambda a, t: (t, t), _inv_lookup_bwd)


def _sigmoid(x):
    return 1.0 / (1.0 + jnp.exp(-x))


@jax.custom_vjp
def _silu(x):
    return x * _sigmoid(x)


def _silu_fwd(x):
    s = _sigmoid(x)
    return x * s, (x, s)


_silu.defvjp(_silu_fwd, lambda r, g: (g * (r[1] * (1.0 + r[0] * (1.0 - r[1]))),))


def _softplus(x):
    return jnp.where(x > 20.0, x, jnp.log(1.0 + jnp.exp(jnp.minimum(x, 20.0))))


def _rms(x, g, width=None):
    ms = jnp.sum(x * x, -1, keepdims=True) / (x.shape[-1] if width is None else width)
    return x * lax.rsqrt(ms + EPS) * g


MM_VMEM_BUDGET = 40 * 1024 * 1024


def _matmul_rows(name, a, b, mode, out_dtype, res, scatter):
    m, k = a.shape
    n = b.shape[1] if mode == "nn" else b.shape[0]
    dims = {"nn": ((1,), (0,)), "nt": ((1,), (1,))}[mode]
    out_bytes = jnp.dtype(out_dtype).itemsize
    n_in, nx = 2 + (res is not None), len(scatter)

    def vmem(tm):
        blocks = 2 * tm * k * a.dtype.itemsize + 2 * k * n * b.dtype.itemsize + 2 * tm * n * out_bytes + tm * n * 4
        return blocks + (2 * tm * n * res.dtype.itemsize if res is not None else 0)

    tm = next(c for c in (2176, 1088, 512, 256, 128, 64) if m % c == 0 and vmem(c) <= MM_VMEM_BUDGET)
    steps = m // tm

    def body(*refs):
        a_ref, b_ref, o_ref = refs[0], refs[1], refs[n_in + nx]
        i = pl.program_id(0)
        finish = _ride(scatter, True, refs[n_in:n_in + nx], refs[n_in + nx + 1:n_in + 2 * nx + 1], refs[n_in + 2 * nx + 1:], i == 0, i == steps - 1)
        out = _dot(a_ref[...], b_ref[...], dims)
        if res is not None:
            out = out + refs[2][...].astype(F32)
        o_ref[...] = out.astype(o_ref.dtype)
        finish()

    o_spec = pl.BlockSpec((tm, n), lambda i: (i, 0))
    in_specs = [pl.BlockSpec((tm, k), lambda i: (i, 0)), pl.BlockSpec(b.shape, lambda i: (0, 0))] + ([o_spec] if res is not None else [])
    args = (a, b) + ((res,) if res is not None else ())
    out = pl.pallas_call(
        body, grid=(steps,), in_specs=in_specs + [_HBM] * nx, out_specs=[o_spec] + [_HBM] * nx,
        out_shape=[jax.ShapeDtypeStruct((m, n), out_dtype)] + Exchange.out_shape(scatter, True), scratch_shapes=Exchange.scratch(nx) if nx else [],
        compiler_params=_cparams(("arbitrary",) if nx else ("parallel",)), name=name)(*args, *scatter)
    return out if nx else out[0]


def matmul(name, a, b, mode, out_dtype=None, res=None, scatter=()):
    if mode != "tn":
        return _matmul_rows(name, a, b, mode, out_dtype or F32, res, scatter)
    out_dtype = out_dtype or _MXU_DTYPE
    (k, m), (k2, n) = a.shape, b.shape
    assert k == k2 and res is None, (name, a.shape, b.shape, mode)
    tm = _pick(m, (m if m <= 1536 else 1024, 1024, 512, 384, 256, 128))
    tn = _pick(n, (1024, 512, 384, 256, 128))
    tk = _pick(k, (512, 256, 128))
    nk = k // tk
    dims = ((0,), (0,))

    def body(*refs):
        if res is None:
            a_ref, b_ref, o_ref, acc_ref = refs
        else:
            a_ref, b_ref, r_ref, o_ref, acc_ref = refs
        kk = pl.program_id(2)

        @pl.when(kk == 0)
        def _():
            acc_ref[...] = jnp.zeros_like(acc_ref)

        acc_ref[...] += _dot(a_ref[...], b_ref[...], dims)

        @pl.when(kk == nk - 1)
        def _():
            out = acc_ref[...]
            if res is not None:
                out = out + r_ref[...].astype(F32)
            o_ref[...] = out.astype(o_ref.dtype)

    a_spec = pl.BlockSpec((tk, tm), lambda i, j, kk: (kk, i)) if mode == "tn" else pl.BlockSpec((tm, tk), lambda i, j, kk: (i, kk))
    b_spec = pl.BlockSpec((tn, tk), lambda i, j, kk: (j, kk)) if mode == "nt" else pl.BlockSpec((tk, tn), lambda i, j, kk: (kk, j))
    o_spec = pl.BlockSpec((tm, tn), lambda i, j, kk: (i, j))
    in_specs = [a_spec, b_spec] + ([o_spec] if res is not None else [])
    args = (a, b) + ((res,) if res is not None else ())
    return pl.pallas_call(
        body, grid=(m // tm, n // tn, nk), in_specs=in_specs, out_specs=o_spec,
        out_shape=jax.ShapeDtypeStruct((m, n), out_dtype), scratch_shapes=[pltpu.VMEM((tm, tn), F32)],
        compiler_params=_cparams(("parallel", "parallel", "arbitrary")), name=name)(*args)


@dataclasses.dataclass
class Arg:
    arr: jax.Array
    kind: str = "row"
    bc: int = 0
    base: int = 0
    ph: bool = False
    diff: bool = False
    gdt: object = F32


def _arg_spec(a, tr, nh, ntab, base=None):
    bc = a.bc or a.arr.shape[1]
    base = a.base if base is None else base
    width = bc * nh if a.ph else bc
    col = base // nh if a.ph else base
    assert not a.ph or base % nh == 0
    if a.kind == "row":
        return pl.BlockSpec((tr, width), lambda i: (i, col))
    if a.kind == "tab":
        return pl.BlockSpec((tr, width), lambda i: (i % ntab, col))
    return pl.BlockSpec((a.arr.shape[0], width), lambda i: (0, col))


def _head_view(ref, a, h, rs):
    bc = a.bc or a.arr.shape[1]
    rows = slice(None) if a.kind == "par" else rs
    v = ref[rows, h * bc:(h + 1) * bc] if a.ph else ref[rows, :]
    return v.astype(F32) if jnp.issubdtype(v.dtype, jnp.floating) else v


def row_call(name, fn, args, outs, tr, nh=1, ntab=1):
    t = args[0].arr.shape[0]
    n_in = len(args)
    out_args = [Arg(None, "row", bc, 0, ph) for (_, _, bc, ph) in outs]
    assert all(a.ph or nh == 1 for a in out_args)
    rs = slice(None)

    def body(*refs):
        for h in range(nh):
            res = fn(*[_head_view(r, a, h, rs) for r, a in zip(refs[:n_in], args, strict=True)])
            for r, a, v in zip(refs[n_in:], out_args, res, strict=True):
                r[rs, h * a.bc:(h + 1) * a.bc] = v.astype(r.dtype)

    return pl.pallas_call(
        body, grid=(t // tr,), in_specs=[_arg_spec(a, tr, nh, ntab) for a in args], out_specs=[_arg_spec(a, tr, nh, ntab) for a in out_args],
        out_shape=[jax.ShapeDtypeStruct((t, cols), dt) for (cols, dt, _, _) in outs],
        compiler_params=_cparams(("arbitrary",)), name=name)(*[a.arr for a in args])


def row_vjp_call(name, fn, args, cts, tr, nh=1, ntab=1):
    t = args[0].arr.shape[0]
    n_in, n_ct = len(args), len(cts)
    diff_idx = [k for k, a in enumerate(args) if a.diff]
    def body(*refs):
        out_refs = refs[n_in + n_ct:]
        par_sum = {}
        for k, r in zip(diff_idx, out_refs, strict=True):
            if args[k].kind == "par":
                @pl.when(pl.program_id(0) == 0)
                def _(r=r):
                    r[...] = jnp.zeros_like(r)

        for rs in (slice(None),):
            row_sum = {}
            for h in range(nh):
                vals = [_head_view(r, a, h, rs) for r, a in zip(refs[:n_in], args, strict=True)]
                ct_vals = tuple(_head_view(r, a, h, rs) for r, a in zip(refs[n_in:n_in + n_ct], cts, strict=True))

                def f(*dv, vals=vals):
                    full = list(vals)
                    for k, v in zip(diff_idx, dv, strict=True):
                        full[k] = v
                    return tuple(fn(*full))

                _, vjp = jax.vjp(f, *[vals[k] for k in diff_idx])
                for j, (k, r, g) in enumerate(zip(diff_idx, out_refs, vjp(ct_vals), strict=True)):
                    a = args[k]
                    bc = a.bc or a.arr.shape[1]
                    if a.kind == "row" and a.ph:
                        r[rs, h * bc:(h + 1) * bc] = g.astype(r.dtype)
                    elif a.kind == "row":
                        row_sum[j] = g if j not in row_sum else row_sum[j] + g
                    else:
                        key = (j, h if a.ph else 0)
                        par_sum[key] = g if key not in par_sum else par_sum[key] + g
            for j, g in row_sum.items():
                out_refs[j][rs, :] = g.astype(out_refs[j].dtype)
        for (j, h), g in par_sum.items():
            bc = g.shape[1]
            out_refs[j][:, h * bc:(h + 1) * bc] += g

    out_specs, out_shape = [], []
    for k in diff_idx:
        a = args[k]
        bc = a.bc or a.arr.shape[1]
        out_specs.append(_arg_spec(a, tr, nh, ntab, base=0))
        out_shape.append(jax.ShapeDtypeStruct((t if a.kind == "row" else a.arr.shape[0], bc * (nh if a.ph else 1)), a.gdt if a.kind == "row" else F32))
    in_specs = [_arg_spec(a, tr, nh, ntab) for a in list(args) + list(cts)]
    return pl.pallas_call(
        body, grid=(t // tr,), in_specs=in_specs, out_specs=out_specs, out_shape=out_shape,
        compiler_params=_cparams(("arbitrary",)), name=name)(*[a.arr for a in list(args) + list(cts)])


def _conv_taps(x, w):
    rows = lax.broadcasted_iota(jnp.int32, x.shape, 0)
    y = x * w[CONV_K - 1:CONV_K, :]
    for s in range(1, CONV_K):
        y = y + jnp.where(rows >= s, pltpu.roll(x, s, 0), 0.0) * w[CONV_K - 1 - s:CONV_K - s, :]
    return y


CONV_HEADS = 4
CONV_BLOCKS_PER_THIRD = N_HEADS // CONV_HEADS


def _conv_post(y, block):
    a = _silu(y)
    normed = block < 2 * CONV_BLOCKS_PER_THIRD
    scale = jnp.where(block < CONV_BLOCKS_PER_THIRD, HEAD ** -0.5, 1.0)
    return a * jnp.where(normed, lax.rsqrt(jnp.sum(a * a, -1, keepdims=True) + EPS) * scale, 1.0)


def conv_fwd(z, w, lp):
    t, width = z.shape
    cols = CONV_HEADS * HEAD

    def body(z_ref, w_ref, o_ref, y_ref):
        block = pl.program_id(1)
        for h in range(CONV_HEADS):
            cs = slice(h * HEAD, (h + 1) * HEAD)
            y = _conv_taps(z_ref[:, cs].astype(F32), w_ref[:, cs])
            y_ref[:, cs] = y.astype(y_ref.dtype)
            o_ref[:, cs] = _conv_post(y, block)

    blk = pl.BlockSpec((lp, cols), lambda b, j: (b, j))
    out = jax.ShapeDtypeStruct((t, width), F32)
    return pl.pallas_call(
        body, grid=(t // lp, width // cols), in_specs=[blk, pl.BlockSpec((CONV_K, cols), lambda b, j: (0, j))],
        out_specs=[blk, blk], out_shape=[out, jax.ShapeDtypeStruct((t, width), _MXU_DTYPE)],
        compiler_params=_cparams(("arbitrary", "arbitrary")), name="a_conv_fwd")(z, w)


def conv_bwd(z, y, w, dout, lp):
    t, width = z.shape
    cols = CONV_HEADS * HEAD

    def body(z_ref, y_ref, w_ref, g_ref, dz_ref, dw_ref):
        block = pl.program_id(0)

        @pl.when(pl.program_id(1) == 0)
        def _():
            dw_ref[...] = jnp.zeros_like(dw_ref)

        for h in range(CONV_HEADS):
            cs = slice(h * HEAD, (h + 1) * HEAD)
            x, wv = z_ref[:, cs].astype(F32), w_ref[:, cs]
            _, vjp = jax.vjp(lambda y_: _conv_post(y_, block), y_ref[:, cs].astype(F32))
            (dy,) = vjp(g_ref[:, cs])
            rows = lax.broadcasted_iota(jnp.int32, x.shape, 0)
            dx = dy * wv[CONV_K - 1:CONV_K, :]
            dw_ref[CONV_K - 1:CONV_K, cs] += jnp.sum(dy * x, axis=0, keepdims=True)
            for s in range(1, CONV_K):
                dy_up = jnp.where(rows < lp - s, pltpu.roll(dy, lp - s, 0), 0.0)
                dx = dx + dy_up * wv[CONV_K - 1 - s:CONV_K - s, :]
                dw_ref[CONV_K - 1 - s:CONV_K - s, cs] += jnp.sum(dy_up * x, axis=0, keepdims=True)
            dz_ref[:, cs] = dx.astype(dz_ref.dtype)

    blk = pl.BlockSpec((lp, cols), lambda j, b: (b, j))
    w_blk = pl.BlockSpec((CONV_K, cols), lambda j, b: (0, j))
    return pl.pallas_call(
        body, grid=(width // cols, t // lp), in_specs=[blk, blk, w_blk, blk], out_specs=[blk, w_blk],
        out_shape=[jax.ShapeDtypeStruct((t, width), _MXU_DTYPE), jax.ShapeDtypeStruct((CONV_K, width), F32)],
        compiler_params=_cparams(("arbitrary", "arbitrary")), name="a_conv_bwd")(z, y, w, dout)


def _delta_chunk(q, k, v, ba, alog, dtb, state, t_stored):
    n_g, c = q.shape[0], q.shape[1]
    lane = lax.broadcasted_iota(jnp.int32, (1, HEAD), 1)

    def pick(xs, offset):
        cols = [jnp.sum(xs[i // N_HEADS if len(xs) > 1 else 0] * (lane == offset + i % N_HEADS).astype(F32), axis=1, keepdims=True)[None]
                for i in range(n_g)]
        return jnp.concatenate(cols, 0)

    b_raw, a_raw = pick(ba, 0), pick(ba, N_HEADS)
    a_log, dt_bias = pick((alog,), 0), pick((dtb,), 0)
    beta = _sigmoid(b_raw)
    g = -jnp.exp(a_log) * _softplus(a_raw + dt_bias)
    ri = lax.broadcasted_iota(jnp.int32, (c, c), 0)
    ci = lax.broadcasted_iota(jnp.int32, (c, c), 1)
    tril = ci <= ri
    lower = jnp.broadcast_to(tril.astype(F32), (n_g, c, c))
    gc_col = _dot_01(lower, g * jnp.ones((1, 1, HEAD), F32))[:, :, :1]
    gc_row = _dot_01(jnp.ones((n_g, 8, c), F32), g * (ri <= ci).astype(F32)[None])[:, 0:1, :]
    gc_last = jnp.sum(g, axis=1, keepdims=True)
    decay = jnp.exp(jnp.where(tril, gc_col - gc_row, NEG))
    e_gc = jnp.exp(gc_col)
    kb = k * beta
    a_mat = jnp.where(ci < ri, mm_nt(kb, k) * decay, 0.0)
    t_inv = _inv_unit_lower(a_mat) if t_stored is None else _inv_lookup(a_mat, t_stored)
    u_base = mm_nn(t_inv, v * beta)
    w_dec = mm_nn(t_inv, kb * e_gc)
    attn = jnp.where(tril, mm_nt(q, k) * decay, 0.0)
    u = u_base - mm_nn(w_dec, state)
    o = mm_nn(q * e_gc, state) + mm_nn(attn, u)
    new_state = state * jnp.exp(gc_last) + mm_tn(k * jnp.exp(gc_last - gc_col), u)
    return o, new_state, t_inv


DELTA_STEP_FWD = (4, 2)
DELTA_STEP_BWD = (2, 2)


def _heads_of(ref, rs, first_col):
    return jnp.stack([ref[i // N_HEADS, rs, first_col + (i % N_HEADS) * HEAD:first_col + (i % N_HEADS + 1) * HEAD]
                      for i in range(ref.shape[0] * N_HEADS)])


def _qkv_heads(ref, rs, part):
    return _heads_of(ref, rs, part * N_HEADS * HEAD)


def _by_sequence(a, lp):
    return a.reshape(a.shape[0] // lp, lp, a.shape[1])


def _ride(bufs, scatter, refs_in, refs_out, sems, first, last, two_level=False):
    if not bufs:
        return lambda: None
    make = lambda: (TwoLevelGather if two_level else Exchange)(refs_in, refs_out, *sems, scatter)

    @pl.when(first)
    def _():
        make().start()

    def finish():
        @pl.when(last)
        def _():
            make().wait()

    return finish


def delta_fwd(qkv, ba, ba_block, alog, dtb, lp, gather=()):
    t = qkv.shape[0]
    nb, nc = t // lp, lp // CHUNK
    seqs, cps = DELTA_STEP_FWD
    ng, rows = nc // cps, cps * CHUNK
    nx = len(gather)
    nbg = nb // seqs
    assert nc % cps == 0 and nb % seqs == 0

    def body(*refs):
        qkv_ref, ba_ref, al_ref, dt_ref = refs[:4]
        o_ref, s_ref, t_ref = refs[4 + nx:7 + nx]
        state_ref = refs[7 + 2 * nx]
        b, n = pl.program_id(0), pl.program_id(1)
        finish = _ride(gather, False, refs[4:4 + nx], refs[7 + nx:7 + 2 * nx], refs[8 + 2 * nx:], (b == 0) & (n == 0), (b == nbg - 1) & (n == ng - 1))

        @pl.when(n == 0)
        def _():
            state_ref[...] = jnp.zeros_like(state_ref)

        al, dtv = al_ref[...], dt_ref[...]
        for c in range(cps):
            rs = slice(c * CHUNK, (c + 1) * CHUNK)
            state = state_ref[...]
            o, new_state, t_inv = _delta_chunk(_qkv_heads(qkv_ref, rs, 0), _qkv_heads(qkv_ref, rs, 1), _qkv_heads(qkv_ref, rs, 2),
                                               tuple(ba_ref[i, rs, :] for i in range(seqs)), al, dtv, state, None)
            for i in range((seqs * N_HEADS)):
                seq, g = divmod(i, N_HEADS)
                o_ref[seq, rs, g * HEAD:(g + 1) * HEAD] = o[i]
                s_ref[seq, g, c] = state[i]
                t_ref[seq, g, c] = t_inv[i]
            state_ref[...] = new_state
        finish()

    rows_of = lambda width: pl.BlockSpec((seqs, rows, width), lambda b, n: (b, n, 0))
    par_spec = pl.BlockSpec((1, HEAD), lambda b, n: (0, 0))
    out = pl.pallas_call(
        body, grid=(nbg, ng),
        in_specs=[rows_of(3 * N_HEADS * HEAD), pl.BlockSpec((seqs, rows, HEAD), lambda b, n: (b, n, ba_block)), par_spec, par_spec] + [_HBM] * nx,
        out_specs=[rows_of(N_HEADS * HEAD), pl.BlockSpec((seqs, N_HEADS, cps, HEAD, HEAD), lambda b, n: (b, 0, n, 0, 0)),
                   pl.BlockSpec((seqs, N_HEADS, cps, CHUNK, CHUNK), lambda b, n: (b, 0, n, 0, 0))] + [_HBM] * nx,
        out_shape=[jax.ShapeDtypeStruct((nb, lp, N_HEADS * HEAD), F32), jax.ShapeDtypeStruct((nb, N_HEADS, nc, HEAD, HEAD), F32),
                   jax.ShapeDtypeStruct((nb, N_HEADS, nc, CHUNK, CHUNK), F32)] + Exchange.out_shape(gather, False),
        scratch_shapes=[pltpu.VMEM(((seqs * N_HEADS), HEAD, HEAD), F32)] + (Exchange.scratch(nx) if nx else []),
        compiler_params=_cparams(("arbitrary", "arbitrary")), name="delta_fwd")(_by_sequence(qkv, lp), _by_sequence(ba, lp), alog, dtb, *gather)
    return [out[0].reshape(t, N_HEADS * HEAD)] + list(out[1:])


def delta_bwd(qkv, ba, ba_block, alog, dtb, states, t_invs, do, lp, scatter=()):
    t = qkv.shape[0]
    nb, nc = t // lp, lp // CHUNK
    seqs, cps = DELTA_STEP_BWD
    ng, rows = nc // cps, cps * CHUNK
    nx = len(scatter)
    nbg = nb // seqs

    def body(*refs):
        qkv_ref, ba_ref, al_ref, dt_ref, s_ref, t_ref, do_ref = refs[:7]
        dqkv_ref, dba_ref, dal_ref, ddt_ref = refs[7 + nx:11 + nx]
        dstate_ref = refs[11 + 2 * nx]
        b, step = pl.program_id(0), pl.program_id(1)
        finish = _ride(scatter, True, refs[7:7 + nx], refs[11 + nx:11 + 2 * nx], refs[12 + 2 * nx:], (b == 0) & (step == 0),
                       (b == nbg - 1) & (step == ng - 1))

        @pl.when(step == 0)
        def _():
            dstate_ref[...] = jnp.zeros_like(dstate_ref)

        @pl.when((b == 0) & (step == 0))
        def _():
            dal_ref[...] = jnp.zeros_like(dal_ref)
            ddt_ref[...] = jnp.zeros_like(ddt_ref)

        al, dtv = al_ref[...], dt_ref[...]
        d_al = jnp.zeros((1, HEAD), F32)
        d_dt = jnp.zeros((1, HEAD), F32)
        for c in reversed(range(cps)):
            rs = slice(c * CHUNK, (c + 1) * CHUNK)
            t_n = jnp.stack([t_ref[i // N_HEADS, i % N_HEADS, c] for i in range((seqs * N_HEADS))])
            s_n = jnp.stack([s_ref[i // N_HEADS, i % N_HEADS, c] for i in range((seqs * N_HEADS))])

            def f(q_, k_, v_, ba_, al_, dt_, s_, t_n=t_n):
                return _delta_chunk(q_, k_, v_, ba_, al_, dt_, s_, t_n)[:2]

            _, vjp = jax.vjp(f, _qkv_heads(qkv_ref, rs, 0), _qkv_heads(qkv_ref, rs, 1), _qkv_heads(qkv_ref, rs, 2), tuple(ba_ref[i, rs, :] for i in range(seqs)), al, dtv, s_n)
            grads = vjp((_heads_of(do_ref, rs, 0), dstate_ref[...]))
            for part in range(3):
                for i in range((seqs * N_HEADS)):
                    col = (part * N_HEADS + i % N_HEADS) * HEAD
                    dqkv_ref[i // N_HEADS, rs, col:col + HEAD] = grads[part][i]
            for i in range(seqs):
                dba_ref[i, rs, :] = grads[3][i]
            d_al, d_dt = d_al + grads[4], d_dt + grads[5]
            dstate_ref[...] = grads[6]
        dal_ref[...] += d_al
        ddt_ref[...] += d_dt
        finish()

    rows_of = lambda width: pl.BlockSpec((seqs, rows, width), lambda b, n: (b, ng - 1 - n, 0))
    par_spec = pl.BlockSpec((1, HEAD), lambda b, n: (0, 0))
    out = pl.pallas_call(
        body, grid=(nbg, ng),
        in_specs=[rows_of(3 * N_HEADS * HEAD), pl.BlockSpec((seqs, rows, HEAD), lambda b, n: (b, ng - 1 - n, ba_block)), par_spec, par_spec,
                  pl.BlockSpec((seqs, N_HEADS, cps, HEAD, HEAD), lambda b, n: (b, 0, ng - 1 - n, 0, 0)),
                  pl.BlockSpec((seqs, N_HEADS, cps, CHUNK, CHUNK), lambda b, n: (b, 0, ng - 1 - n, 0, 0)), rows_of(N_HEADS * HEAD)] + [_HBM] * nx,
        out_specs=[rows_of(3 * N_HEADS * HEAD), rows_of(HEAD), par_spec, par_spec] + [_HBM] * nx,
        out_shape=[jax.ShapeDtypeStruct((nb, lp, 3 * N_HEADS * HEAD), F32), jax.ShapeDtypeStruct((nb, lp, HEAD), F32),
                   jax.ShapeDtypeStruct((1, HEAD), F32), jax.ShapeDtypeStruct((1, HEAD), F32)] + Exchange.out_shape(scatter, True),
        scratch_shapes=[pltpu.VMEM(((seqs * N_HEADS), HEAD, HEAD), F32)] + (Exchange.scratch(nx) if nx else []),
        compiler_params=_cparams(("arbitrary", "arbitrary")), name="delta_bwd")(
            _by_sequence(qkv, lp), _by_sequence(ba, lp), alog, dtb, states, t_invs, _by_sequence(do, lp), *scatter)
    return [out[0].reshape(t, 3 * N_HEADS * HEAD), out[1].reshape(t, HEAD)] + list(out[2:])


ATT_Q_TILE = 256
ATT_K_TILE = 512
ATT_SCALE = QK_DIM ** -0.5


def _tiles(end, size):
    return [(s, min(s + size, end)) for s in range(0, end, size)]


def _att_visible(q0, q1, k0, k1, keys_first):
    if k1 <= q0 + CHUNK and k0 >= PAD_ROWS:
        return None
    shape = (k1 - k0, q1 - q0) if keys_first else (q1 - q0, k1 - k0)
    qpos = q0 + lax.broadcasted_iota(jnp.int32, shape, 1 if keys_first else 0)
    kpos = k0 + lax.broadcasted_iota(jnp.int32, shape, 0 if keys_first else 1)
    shift = CHUNK.bit_length() - 1
    return (jnp.right_shift(kpos, shift) <= jnp.right_shift(qpos, shift)) & (kpos >= PAD_ROWS)


def _att_seq_specs(lp):
    return pl.BlockSpec((lp, QK_PAD), lambda b, h: (b, h)), pl.BlockSpec((lp, HEAD), lambda b, h: (b, h))


def flash_fwd(q, k, v, lp):
    t = q.shape[0]
    qk_seq, o_seq = _att_seq_specs(lp)

    def body(q_ref, k_ref, v_ref, o_ref, lse_ref):
        q_tiles = _tiles(lp, ATT_Q_TILE)

        def score_steps(q0, q1, out):
            def step(k0, k1):
                s = mm_nt(q_ref[q0:q1, :], k_ref[k0:k1, :])
                vis = _att_visible(q0, q1, k0, k1, False)
                s = s if vis is None else jnp.where(vis, s, NEG)
                out["scores"].append(s)
                row_max = jnp.max(s, -1, keepdims=True)
                out["m"] = row_max if out["m"] is None else jnp.maximum(out["m"], row_max)
            return [functools.partial(step, k0, k1) for k0, k1 in _tiles(q1, ATT_K_TILE)]

        cur = {"scores": [], "m": None}
        for step in score_steps(*q_tiles[0], cur):
            step()
        for i, (q0, q1) in enumerate(q_tiles):
            nxt = {"scores": [], "m": None}
            ahead = score_steps(*q_tiles[i + 1], nxt) if i + 1 < len(q_tiles) else []
            l = jnp.zeros((q1 - q0, 1), F32)
            acc = jnp.zeros((q1 - q0, HEAD), F32)
            for s, (k0, k1) in zip(cur["scores"], _tiles(q1, ATT_K_TILE), strict=True):
                if ahead:
                    ahead.pop(0)()
                p = jnp.exp(s - cur["m"])
                l = l + jnp.sum(p, -1, keepdims=True)
                acc = acc + mm_nn(p, v_ref[k0:k1, :])
            for step in ahead:
                step()
            o_ref[q0:q1, :] = acc / l
            lse_ref[q0:q1, :] = jnp.broadcast_to(cur["m"] + jnp.log(l), (q1 - q0, HEAD))
            cur = nxt

    big = jax.ShapeDtypeStruct((t, N_HEADS * HEAD), F32)
    return pl.pallas_call(
        body, grid=(t // lp, N_HEADS), in_specs=[qk_seq, qk_seq, o_seq], out_specs=[o_seq, o_seq], out_shape=[big, big],
        compiler_params=_cparams(("arbitrary", "arbitrary")), name="flash_fwd")(q, k, v)


def flash_bwd(q, k, v, o, lse, do, lp):
    t = q.shape[0]
    qk_seq, o_seq = _att_seq_specs(lp)

    def body(q_ref, k_ref, v_ref, o_ref, lse_ref, do_ref, dq_ref, dk_out_ref, dv_out_ref, dk_ref, dv_ref):
        dk_ref[...] = jnp.zeros_like(dk_ref)
        dv_ref[...] = jnp.zeros_like(dv_ref)
        for q0, q1 in _tiles(lp, ATT_Q_TILE):
            qb, dob = q_ref[q0:q1, :], do_ref[q0:q1, :]
            lse_row = jnp.transpose(lse_ref[q0:q1, :])[0:1, :]
            dsum_row = jnp.sum(jnp.transpose(dob * o_ref[q0:q1, :]), axis=0, keepdims=True)
            dq = jnp.zeros((q1 - q0, QK_PAD), F32)
            for k0, k1 in _tiles(q1, ATT_K_TILE):
                kb, vb = k_ref[k0:k1, :], v_ref[k0:k1, :]
                s = mm_nt(kb, qb)
                vis = _att_visible(q0, q1, k0, k1, True)
                s = s if vis is None else jnp.where(vis, s, NEG)
                p = jnp.exp(s - lse_row)
                ds = p * (mm_nt(vb, dob) - dsum_row)
                dv_ref[k0:k1, :] += mm_nn(p, dob)
                dk_ref[k0:k1, :] += mm_nn(ds, qb)
                dq = dq + mm_tn(ds, kb)
            dq_ref[q0:q1, :] = dq.astype(dq_ref.dtype)
        dk_out_ref[...] = dk_ref[...].astype(dk_out_ref.dtype)
        dv_out_ref[...] = dv_ref[...].astype(dv_out_ref.dtype)

    narrow = _MXU_DTYPE
    return pl.pallas_call(
        body, grid=(t // lp, N_HEADS), in_specs=[qk_seq, qk_seq, o_seq, o_seq, o_seq, o_seq], out_specs=[qk_seq, qk_seq, o_seq],
        out_shape=[jax.ShapeDtypeStruct((t, N_HEADS * QK_PAD), narrow), jax.ShapeDtypeStruct((t, N_HEADS * QK_PAD), narrow),
                   jax.ShapeDtypeStruct((t, N_HEADS * HEAD), narrow)],
        scratch_shapes=[pltpu.VMEM((lp, QK_PAD), F32), pltpu.VMEM((lp, HEAD), F32)],
        compiler_params=_cparams(("arbitrary", "arbitrary")), name="flash_bwd")(q, k, v, o, lse, do)


def loss_head(h2, target, lp):
    nb, seq, d = target.shape
    cols = _pick(d, (512, 128))
    ncol = d // cols

    def body(h_ref, t_ref, loss_ref, dh_ref, acc_ref):
        b, j = pl.program_id(0), pl.program_id(1)

        @pl.when((b == 0) & (j == 0))
        def _():
            acc_ref[...] = jnp.zeros_like(acc_ref)

        err = h_ref[LEAD:, :] - t_ref[...]
        dh_ref[:LEAD, :] = jnp.zeros((LEAD, cols), F32)
        dh_ref[LEAD:, :] = err * (1.0 / d)
        acc_ref[...] += jnp.sum(err * err, axis=0, keepdims=True)

        @pl.when((b == nb - 1) & (j == ncol - 1))
        def _():
            loss_ref[...] = jnp.sum(acc_ref[...], axis=1, keepdims=True) * (0.5 / d)

    return pl.pallas_call(
        body, grid=(nb, ncol),
        in_specs=[pl.BlockSpec((None, lp, cols), lambda b, j: (b, 0, j)), pl.BlockSpec((None, seq, cols), lambda b, j: (b, 0, j))],
        out_specs=[pl.BlockSpec((1, 1), lambda b, j: (0, 0)), pl.BlockSpec((None, lp, cols), lambda b, j: (b, 0, j))],
        out_shape=[jax.ShapeDtypeStruct((1, 1), F32), jax.ShapeDtypeStruct((nb, lp, d), F32)],
        scratch_shapes=[pltpu.VMEM((1, cols), F32)], compiler_params=_cparams(("arbitrary", "arbitrary")), name="loss_head")(h2, target)


def gated_out(name, o, gate, gain, w, res):
    t, kw = o.shape
    d = w.shape[1]
    tm = _pick(t, (512, 256, 128))

    def body(*refs):
        o_ref, gate_ref = refs[:2]
        w_ref, r_ref, h_ref, g_ref = refs[-4:]
        if gain is None:
            g_ref[...] = _f_gate(o_ref[...], gate_ref[...])[0].astype(g_ref.dtype)
        else:
            for h in range(N_HEADS):
                cs = slice(h * HEAD, (h + 1) * HEAD)
                g_ref[:, cs] = _f_out_gate(o_ref[:, cs], gate_ref[:, cs], refs[2][...])[0].astype(g_ref.dtype)
        h_ref[...] = r_ref[...] + _dot(g_ref[...], w_ref[...], ((1,), (0,)))

    rows = lambda width: pl.BlockSpec((tm, width), lambda i: (i, 0))
    whole = lambda a: pl.BlockSpec(a.shape, lambda i: (0, 0))
    params = [] if gain is None else [gain]
    return pl.pallas_call(
        body, grid=(t // tm,), in_specs=[rows(kw), rows(kw)] + [whole(p) for p in params] + [whole(w), rows(d)], out_specs=[rows(d), rows(kw)],
        out_shape=[jax.ShapeDtypeStruct((t, d), F32), jax.ShapeDtypeStruct((t, kw), _MXU_DTYPE)],
        compiler_params=_cparams(("parallel",)), name=name)(o, gate, *params, w, res)


def embed_norm(x, meta, gain, lp, gather=()):
    nb, seq, d = x.shape
    nblk, nx = lp // LEAD, len(gather)

    def body(*refs):
        x_ref, meta_ref, g_ref = refs[:3]
        h_ref, hn_ref = refs[3 + nx:5 + nx]
        b, i = pl.program_id(0), pl.program_id(1)
        finish = _ride(gather, False, refs[3:3 + nx], refs[5 + nx:5 + 2 * nx], refs[5 + 2 * nx:], (b == 0) & (i == 0), (b == nb - 1) & (i == nblk - 1),
                       two_level=True)

        @pl.when(i == 0)
        def _():
            h_ref[:PAD_ROWS, :] = jnp.zeros((PAD_ROWS, d), F32)
            h_ref[PAD_ROWS:, :] = meta_ref[...]

        @pl.when(i > 0)
        def _():
            h_ref[...] = x_ref[...]

        hn_ref[...] = _rms(h_ref[...], g_ref[...]).astype(hn_ref.dtype)
        finish()

    rows = pl.BlockSpec((LEAD, d), lambda b, i: (b * nblk + i, 0))
    out = pl.pallas_call(
        body, grid=(nb, nblk),
        in_specs=[pl.BlockSpec((None, LEAD, d), lambda b, i: (b, jnp.maximum(i - 1, 0), 0)), pl.BlockSpec((N_META, d), lambda b, i: (0, 0)),
                  pl.BlockSpec((1, d), lambda b, i: (0, 0))] + [_HBM] * nx,
        out_specs=[rows, rows] + [_HBM] * nx,
        out_shape=[jax.ShapeDtypeStruct((nb * lp, d), F32), jax.ShapeDtypeStruct((nb * lp, d), _MXU_DTYPE)] + Exchange.out_shape(gather, False),
        scratch_shapes=Exchange.scratch(nx) if nx else [],
        compiler_params=_cparams(("arbitrary", "arbitrary")), name="embed_norm")(x, meta, gain, *gather)
    return list(out)


def meta_grad(dh0):
    nb, _, d = dh0.shape

    def body(g_ref, o_ref):
        @pl.when(pl.program_id(0) == 0)
        def _():
            o_ref[...] = jnp.zeros_like(o_ref)

        o_ref[...] += g_ref[PAD_ROWS:LEAD, :]

    return pl.pallas_call(
        body, grid=(nb,), in_specs=[pl.BlockSpec((None, LEAD, d), lambda b: (b, 0, 0))],
        out_specs=pl.BlockSpec((N_META, d), lambda b: (0, 0)), out_shape=jax.ShapeDtypeStruct((N_META, d), F32),
        compiler_params=_cparams(("arbitrary",)), name="meta_grad")(dh0)


_HBM = pl.BlockSpec(memory_space=pltpu.HBM)


def _mesh_pos():
    x, y, c = lax.axis_index("x"), lax.axis_index("y"), lax.axis_index("c")
    return x, y, c


def _peer(x, y, c, k):
    px = 1 - x if k & 4 else x
    py = 1 - y if k & 2 else y
    pc = 1 - c if k & 1 else c
    return (px, py, pc), 4 * px + 2 * py + pc


class Exchange:
    def __init__(self, x_refs, out_refs, send_sems, recv_sems, local_sems, scatter):
        self.x_refs, self.out_refs, self.scatter = x_refs, out_refs, scatter
        self.send_sems, self.recv_sems, self.local_sems = send_sems, recv_sems, local_sems
        self.pos = _mesh_pos()
        x, y, c = self.pos
        self.me = 4 * x + 2 * y + c

    @staticmethod
    def scratch(n):
        return [pltpu.SemaphoreType.DMA((n, N_DEV - 1)), pltpu.SemaphoreType.DMA((n, N_DEV - 1)), pltpu.SemaphoreType.DMA((n,))]

    @staticmethod
    def out_shape(bufs, scatter):
        return [jax.ShapeDtypeStruct(b.shape if scatter else (N_DEV,) + b.shape, b.dtype) for b in bufs]

    def _local(self, i):
        return pltpu.make_async_copy(self.x_refs[i].at[self.me] if self.scatter else self.x_refs[i], self.out_refs[i].at[self.me], self.local_sems.at[i])

    def _copy(self, i, k, landing):
        peer, peer_id = _peer(*self.pos, k)
        src = self.x_refs[i].at[peer_id] if self.scatter else self.x_refs[i]
        return pltpu.make_async_remote_copy(src_ref=src, dst_ref=self.out_refs[i].at[peer_id if landing else self.me],
                                            send_sem=self.send_sems.at[i, k - 1], recv_sem=self.recv_sems.at[i, k - 1],
                                            device_id=peer, device_id_type=pl.DeviceIdType.MESH)

    def start(self):
        for i in range(len(self.x_refs)):
            self._local(i).start()
        for k in range(1, N_DEV):
            for i in range(len(self.x_refs)):
                self._copy(i, k, False).start()

    def wait(self):
        for k in range(1, N_DEV):
            for i in range(len(self.x_refs)):
                self._copy(i, k, True).wait_recv()
        for k in range(1, N_DEV):
            for i in range(len(self.x_refs)):
                self._copy(i, k, False).wait_send()
        for i in range(len(self.x_refs)):
            self._local(i).wait()


class TwoLevelGather(Exchange):
    DIRECT = (1, 4, 2, 6)
    FROM_CHIPS = (4, 2, 6)

    def _forward(self, i, k):
        _, origin = _peer(*self.pos, k)
        sibling, _ = _peer(*self.pos, 1)
        block = self.out_refs[i].at[origin]
        return pltpu.make_async_remote_copy(src_ref=block, dst_ref=block, send_sem=self.send_sems.at[i, (k ^ 1) - 1],
                                            recv_sem=self.recv_sems.at[i, (k ^ 1) - 1], device_id=sibling, device_id_type=pl.DeviceIdType.MESH)

    def start(self):
        assert not self.scatter
        for i in range(len(self.x_refs)):
            self._local(i).start()
        for k in self.DIRECT:
            for i in range(len(self.x_refs)):
                self._copy(i, k, False).start()

    def wait(self):
        n = range(len(self.x_refs))
        for k in self.FROM_CHIPS:
            for i in n:
                self._copy(i, k, True).wait_recv()
                self._forward(i, k).start()
        for k in (1, 5, 3, 7):
            for i in n:
                self._copy(i, k, True).wait_recv()
        for k in self.DIRECT:
            for i in n:
                self._copy(i, k, False).wait_send()
        for k in self.FROM_CHIPS:
            for i in n:
                self._forward(i, k).wait_send()
        for i in n:
            self._local(i).wait()


def _exchange(name, bufs, scatter):
    n = len(bufs)

    def body(*refs):
        ex = Exchange(refs[:n], refs[n:2 * n], *refs[2 * n:], scatter)
        ex.start()
        ex.wait()

    return pl.pallas_call(body, in_specs=[_HBM] * n, out_specs=[_HBM] * n, out_shape=Exchange.out_shape(bufs, scatter),
                          scratch_shapes=Exchange.scratch(n), name=name)(*bufs)


def _f_rms(x, g):
    return (_rms(x, g),)


def _f_rms2(x, g1, g2):
    r = x * lax.rsqrt(jnp.sum(x * x, -1, keepdims=True) / x.shape[-1] + EPS)
    return r * g1, r * g2


@jax.custom_vjp
def _out_gate(o, gate, gain):
    return _rms(o, gain) * _silu(gate)


def _out_gate_bwd(res, g):
    o, gate, gain = res
    r = lax.rsqrt(jnp.sum(o * o, -1, keepdims=True) / o.shape[-1] + EPS)
    n = o * r
    s = _sigmoid(gate)
    g_norm = g * (gate * s)
    d_gate = g * (n * gain) * (s * (1.0 + gate * (1.0 - s)))
    gn = g_norm * gain
    d_o = r * (gn - n * (jnp.sum(gn * n, -1, keepdims=True) / o.shape[-1]))
    return d_o, d_gate, jnp.sum(g_norm * n, 0, keepdims=True)


_out_gate.defvjp(lambda o, gate, gain: (_out_gate(o, gate, gain), (o, gate, gain)), _out_gate_bwd)


def _f_out_gate(o, gate, gain):
    return (_out_gate(o, gate, gain),)


def _f_gate(o, gate):
    return (o * _silu(gate),)


def _swap_rope_halves(x):
    return pltpu.roll(x, ROPE // 2, 1) + pltpu.roll(x, HEAD - ROPE // 2, 1)


def _qk_final_inv_rms(nope, rope_in):
    ms = (jnp.sum(nope * nope, -1, keepdims=True) + jnp.sum(rope_in * rope_in, -1, keepdims=True)) / QK_DIM
    return lax.rsqrt(ms + EPS)


@functools.partial(jax.custom_vjp, nondiff_argnums=(0,))
def _qk_final(scale, nope, rope_in, g_nope, g_rope, cos, sin):
    r = _qk_final_inv_rms(nope, rope_in)
    b = rope_in * (r * g_rope)
    out = jnp.concatenate([nope * (r * g_nope), b * cos + _swap_rope_halves(b) * sin], axis=1)
    return out if scale == 1.0 else out * scale


def _qk_final_fwd(scale, nope, rope_in, g_nope, g_rope, cos, sin):
    return _qk_final(scale, nope, rope_in, g_nope, g_rope, cos, sin), (nope, rope_in, g_nope, g_rope, cos, sin)


def _qk_final_bwd(scale, res, g):
    nope, rope_in, g_nope, g_rope, cos, sin = res
    r = _qk_final_inv_rms(nope, rope_in)
    ga, gb = g[:, :HEAD], g[:, HEAD:]
    if scale != 1.0:
        ga, gb = ga * scale, gb * scale
    db = gb * cos + _swap_rope_halves(gb * sin)
    t_a, t_b = ga * nope, db * rope_in
    d_r = jnp.sum(t_a * g_nope + t_b * g_rope, -1, keepdims=True)
    c = d_r * (r * r * r) * (-1.0 / QK_DIM)
    d_nope = ga * (r * g_nope) + nope * c
    d_rope = db * (r * g_rope) + rope_in * c
    d_g_nope = jnp.sum(t_a * r, 0, keepdims=True)
    d_g_rope = jnp.sum(t_b * r, 0, keepdims=True)
    return d_nope, d_rope, d_g_nope, d_g_rope, jnp.zeros_like(cos), jnp.zeros_like(sin)


_qk_final.defvjp(_qk_final_fwd, _qk_final_bwd)


def _f_qk_final(scale, nope, rope_in, g_nope, g_rope, cos, sin):
    return (_qk_final(scale, nope, rope_in, g_nope, g_rope, cos, sin),)


def _rope_tables(lp):
    half = ROPE // 2
    pos = jnp.maximum(jnp.arange(lp) - PAD_ROWS, 0)
    inv = ROPE_THETA ** (-jnp.arange(half, dtype=F32) / half)
    ang = pos.astype(F32)[:, None] * inv[None, :]
    zeros = jnp.zeros((lp, HEAD - ROPE), F32)
    cos = jnp.concatenate([jnp.cos(ang), jnp.cos(ang), zeros], 1)
    sin = jnp.concatenate([-jnp.sin(ang), jnp.sin(ang), zeros], 1)
    return cos, sin


def _pad_lanes(w, width=HEAD):
    return jnp.pad(w, ((0, 0), (0, width - w.shape[1])))


def _pad_rows(w, rows=HEAD):
    return jnp.pad(w, ((0, rows - w.shape[0]), (0, 0)))


def _split_heads_qk_t(w_t):
    k = w_t.shape[1]
    return jnp.pad(w_t.reshape(N_HEADS, QK_DIM, k), ((0, 0), (0, QK_PAD - QK_DIM), (0, 0))).reshape(N_HEADS * QK_PAD, k)


def _merge_heads_qk_t(g_t):
    k = g_t.shape[1]
    return g_t.reshape(N_HEADS, QK_PAD, k)[:, :QK_DIM].reshape(N_HEADS * QK_DIM, k)


@functools.partial(jax.custom_vjp, nondiff_argnums=(0,))
def _q_final(scale, qh, g_nope, g_rope, cos, sin):
    return _qk_final(scale, qh[:, :HEAD], qh[:, HEAD:], g_nope, g_rope, cos, sin)


def _q_final_bwd(scale, res, g):
    qh, g_nope, g_rope, cos, sin = res
    grads = _qk_final_bwd(scale, (qh[:, :HEAD], qh[:, HEAD:], g_nope, g_rope, cos, sin), g)
    return (jnp.concatenate(grads[:2], axis=1),) + tuple(grads[2:])


_q_final.defvjp(lambda scale, qh, *rest: (_q_final(scale, qh, *rest), (qh,) + rest), _q_final_bwd)


def _f_q_final(scale, qh, g_nope, g_rope, cos, sin):
    return (_q_final(scale, qh, g_nope, g_rope, cos, sin),)


def local_step(x, target, w, deferred=None):
    nb, seq, d = x.shape
    lp = seq + LEAD
    t = nb * lp
    tr = _pick(lp, (544, 128))
    ntab = lp // tr
    mxu = _MXU_DTYPE
    kw = N_HEADS * HEAD

    a_conv = w["a_conv"].T
    alog, dtb, o_gain = _pad_lanes(w["a_log"]), _pad_lanes(w["a_dt_bias"]), w["a_o_gain"]
    a_norm, kv_norm, b_norm = w["a_norm"], w["kv_norm"][None, :], w["b_norm"]
    lat_norm, qlat_norm = w["kv_latent_norm"][None, :], w["b_q_latent_norm"]
    kg_nope, kg_rope = w["k_gain"][None, :HEAD], _pad_lanes(w["k_gain"][None, HEAD:])
    qg_nope, qg_rope = w["b_q_gain"][:, :HEAD], _pad_lanes(w["b_q_gain"][:, HEAD:])
    cos, sin = _rope_tables(lp)

    h0, hn, *gathered = embed_norm(x, w["meta_tokens"].T, a_norm, lp, gather=deferred.first_gather_bufs if deferred else ())
    if deferred:
        w = {**w, **deferred.finish_first(gathered)}
    a_w_in_t = w["a_w_in"].astype(mxu)
    w_qkv_t, w_gba_t = a_w_in_t[:3 * kw], _pad_rows(a_w_in_t[3 * kw:], kw + HEAD)
    z_qkv = matmul("a_in_qkv", hn, w_qkv_t, "nt", out_dtype=mxu)
    z_gba = matmul("a_in_gate_ba", hn, w_gba_t, "nt")
    ba_block = kw // HEAD
    qkv_a, y_conv = conv_fwd(z_qkv, a_conv, lp)
    o_a, states, t_invs, *gathered = delta_fwd(qkv_a, z_gba, ba_block, alog, dtb, lp, gather=deferred.gather_bufs if deferred else ())
    if deferred:
        w = {**w, **deferred.finish(gathered)}
    a_w_out = w["a_w_out"].astype(mxu)
    w_down = _pad_lanes(w["kv_w_down"], KV_RANK + HEAD).astype(mxu)
    w_ukv_t = jnp.concatenate([w["kv_w_uk"], w["kv_w_uv"]], 0).astype(mxu)
    b_w_in_t = w["b_w_in"].astype(mxu)
    w_cq_t, w_gb_t = b_w_in_t[:Q_RANK], b_w_in_t[Q_RANK:]
    w_q_t = _split_heads_qk_t(w["b_w_uq"]).astype(mxu)
    b_w_out = w["b_w_out"].astype(mxu)
    og_args = [Arg(o_a, bc=HEAD, ph=True, diff=True), Arg(z_gba, bc=HEAD, ph=True, diff=True, gdt=mxu), Arg(o_gain, "par", diff=True)]
    h1, og_a = gated_out("a_out", o_a, z_gba, o_gain, a_w_out, h0)

    hk, hb = row_call("b_norms_fwd", _f_rms2, [Arg(h1), Arg(kv_norm, "par"), Arg(b_norm, "par")], [(d, mxu, d, False), (d, mxu, d, False)], tr)
    c_down = matmul("kv_down", hk, w_down, "nn")
    c_kv_arg = Arg(c_down, bc=KV_RANK, diff=True, gdt=mxu)
    k_pe_arg = Arg(c_down, bc=HEAD, base=KV_RANK // HEAD, diff=True)
    c_q_raw = matmul("b_in_q", hb, w_cq_t, "nt")
    gate_b = matmul("b_in_gate", hb, w_gb_t, "nt")
    (c_kv,) = row_call("kv_latent_fwd", _f_rms, [c_kv_arg, Arg(lat_norm, "par")], [(KV_RANK, mxu, KV_RANK, False)], tr)
    (c_q,) = row_call("q_latent_fwd", _f_rms, [Arg(c_q_raw), Arg(qlat_norm, "par")], [(Q_RANK, mxu, Q_RANK, False)], tr)
    k_nope = matmul("k_up", c_kv, w_ukv_t[:kw], "nt")
    v_b = matmul("v_up", c_kv, w_ukv_t[kw:], "nt", out_dtype=mxu)
    q_up = matmul("q_up", c_q, w_q_t, "nt")
    tabs = [Arg(cos, "tab"), Arg(sin, "tab")]
    k_args = [Arg(k_nope, bc=HEAD, ph=True, diff=True, gdt=mxu), k_pe_arg, Arg(kg_nope, "par", diff=True), Arg(kg_rope, "par", diff=True)] + tabs
    q_args = [Arg(q_up, bc=QK_PAD, ph=True, diff=True, gdt=mxu), Arg(qg_nope, "par", diff=True), Arg(qg_rope, "par", diff=True)] + tabs
    f_k_final, f_q_final = functools.partial(_f_qk_final, 1.0), functools.partial(_f_q_final, ATT_SCALE)
    (k_fin,) = row_call("k_final_fwd", f_k_final, k_args, [(N_HEADS * QK_PAD, mxu, QK_PAD, True)], tr, nh=N_HEADS, ntab=ntab)
    (q_fin,) = row_call("q_final_fwd", f_q_final, q_args, [(N_HEADS * QK_PAD, mxu, QK_PAD, True)], tr, nh=N_HEADS, ntab=ntab)
    o_b, lse = flash_fwd(q_fin, k_fin, v_b, lp)
    gb_args = [Arg(o_b, diff=True), Arg(gate_b, diff=True, gdt=mxu)]
    h2, og_b = gated_out("b_out", o_b, gate_b, None, b_w_out, h1)

    loss, dh2 = loss_head(h2.reshape(nb, lp, d), target, lp)
    dh2 = dh2.reshape(t, d)
    grads = {}

    d_og_b = matmul("b_out_dx", dh2, b_w_out, "nt", out_dtype=mxu)
    grads["b_w_out"] = matmul("b_out_dw", og_b, dh2, "tn")
    d_o_b, d_gate_b = row_vjp_call("b_gate_bwd", _f_gate, gb_args, [Arg(d_og_b)], tr)
    dq_fin, dk_fin, dv_b = flash_bwd(q_fin, k_fin, v_b, o_b, lse, d_o_b, lp)
    dq_up, d_qg_nope, d_qg_rope = row_vjp_call(
        "q_final_bwd", f_q_final, q_args, [Arg(dq_fin, bc=QK_PAD, ph=True)], tr, nh=N_HEADS, ntab=ntab)
    dk_nope, dk_pe, d_kg_nope, d_kg_rope = row_vjp_call(
        "k_final_bwd", f_k_final, k_args, [Arg(dk_fin, bc=QK_PAD, ph=True)], tr, nh=N_HEADS, ntab=ntab)
    grads["b_q_gain"] = jnp.concatenate([d_qg_nope, d_qg_rope[:, :ROPE]], 1)
    grads["k_gain"] = jnp.concatenate([d_kg_nope, d_kg_rope[:, :ROPE]], 1)[0]
    d_c_q = matmul("q_up_dx", dq_up, w_q_t, "nn")
    grads["b_w_uq"] = _merge_heads_qk_t(matmul("q_up_dw", dq_up, c_q, "tn"))
    d_c_kv = matmul("k_up_dx", dk_nope, w_ukv_t[:kw], "nn")
    d_c_kv = matmul("v_up_dx", dv_b, w_ukv_t[kw:], "nn", res=d_c_kv)
    grads["kv_w_uk"], grads["kv_w_uv"] = matmul("k_up_dw", dk_nope, c_kv, "tn"), matmul("v_up_dw", dv_b, c_kv, "tn")
    d_c_q_raw, grads["b_q_latent_norm"] = row_vjp_call(
        "q_latent_bwd", _f_rms, [Arg(c_q_raw, diff=True, gdt=mxu), Arg(qlat_norm, "par", diff=True)], [Arg(d_c_q)], tr)
    d_c_kv_raw, d_lat = row_vjp_call(
        "kv_latent_bwd", _f_rms, [c_kv_arg, Arg(lat_norm, "par", diff=True)], [Arg(d_c_kv)], tr)
    grads["kv_latent_norm"] = d_lat[0]
    d_hb = matmul("b_in_q_dx", d_c_q_raw, w_cq_t, "nn")
    d_hb = matmul("b_in_gate_dx", d_gate_b, w_gb_t, "nn", res=d_hb, out_dtype=mxu)
    grads["b_w_in"] = jnp.concatenate([matmul("b_in_q_dw", d_c_q_raw, hb, "tn"), matmul("b_in_gate_dw", d_gate_b, hb, "tn")], 0)
    d_c_down = jnp.concatenate([d_c_kv_raw, dk_pe.astype(mxu)], 1)
    d_hk = matmul("kv_down_dx", d_c_down, w_down, "nt", out_dtype=mxu)
    grads["kv_w_down"] = matmul("kv_down_dw", hk, d_c_down, "tn")[:, :KV_RANK + ROPE]
    dh1, d_kv_norm, grads["b_norm"] = row_vjp_call(
        "b_norms_bwd", lambda x_, g1, g2: _f_rms2(x_, g1, g2) + (x_,),
        [Arg(h1, diff=True), Arg(kv_norm, "par", diff=True), Arg(b_norm, "par", diff=True)], [Arg(d_hk), Arg(d_hb), Arg(dh2)], tr)
    grads["kv_norm"] = d_kv_norm[0]

    d_og_a = matmul("a_out_dx", dh1, a_w_out, "nt", out_dtype=mxu)
    grads["a_w_out"] = matmul("a_out_dw", og_a, dh1, "tn")
    d_o_a, d_gate_a, grads["a_o_gain"] = row_vjp_call(
        "a_out_gate_bwd", _f_out_gate, og_args, [Arg(d_og_a, bc=HEAD, ph=True)], tr, nh=N_HEADS)
    dqkv_a, d_ba, d_alog, d_dtb, *received = delta_bwd(qkv_a, z_gba, ba_block, alog, dtb, states, t_invs, d_o_a, lp,
                                                        scatter=deferred.scatter_bufs(grads) if deferred else ())
    grads["a_log"], grads["a_dt_bias"] = d_alog[:, :N_HEADS], d_dtb[:, :N_HEADS]
    dz_qkv, d_conv = conv_bwd(z_qkv, y_conv, a_conv, dqkv_a, lp)
    grads["a_conv"] = d_conv.T
    dz_gba = jnp.concatenate([d_gate_a, d_ba.astype(mxu)], 1)
    grads["a_w_in"] = jnp.concatenate([matmul("a_in_qkv_dw", dz_qkv, hn, "tn"), matmul("a_in_gate_ba_dw", dz_gba, hn, "tn")[:kw + 2 * N_HEADS]], 0)
    ride = deferred.last_scatter_bufs(grads) if deferred else ()
    d_hn = matmul("a_in_qkv_dx", dz_qkv, w_qkv_t, "nn", scatter=ride)
    if ride:
        d_hn, *received_last = d_hn
        received = list(received) + received_last
    d_hn = matmul("a_in_gate_ba_dx", dz_gba, w_gba_t, "nn", res=d_hn, out_dtype=mxu)
    dh0, grads["a_norm"] = row_vjp_call("a_norm_bwd", lambda x_, g_: _f_rms(x_, g_) + (x_,),
                                        [Arg(h0, diff=True), Arg(a_norm, "par", diff=True)], [Arg(d_hn), Arg(dh1)], tr)
    dh0 = dh0.reshape(nb, lp, d)
    grads["meta_tokens"] = meta_grad(dh0).T
    return loss, dh0[:, LEAD:], grads, received


_SHARDED = (
    ("meta_tokens", True, False), ("a_norm", True, False), ("a_w_in", True, True), ("a_conv", True, False), ("a_w_out", False, True),
    ("kv_w_down", False, True), ("kv_w_uk", True, True), ("kv_w_uv", True, True), ("b_w_in", True, True), ("b_w_uq", True, True),
    ("b_w_out", False, True))
_REPLICATED = ("a_log", "a_dt_bias", "a_o_gain", "kv_norm", "kv_latent_norm", "k_gain", "b_norm", "b_q_latent_norm", "b_q_gain")
_ALL_WEIGHTS = ("meta_tokens", "a_norm", "a_w_in", "a_conv", "a_log", "a_dt_bias", "a_o_gain", "a_w_out", "kv_norm", "kv_w_down",
                "kv_latent_norm", "kv_w_uk", "kv_w_uv", "k_gain", "b_norm", "b_w_in", "b_q_latent_norm", "b_w_uq", "b_q_gain", "b_w_out")


def _round_up(n, m):
    return (n + m - 1) // m * m


def _pack_rows(pieces, row_multiple):
    padded = []
    for p in pieces:
        n = p.shape[-1]
        padded.append(jnp.pad(p, [(0, 0)] * (p.ndim - 1) + [(0, _round_up(n, PACK_COLS) - n)]))
    flat = jnp.concatenate(padded, -1)
    rows = _round_up(flat.shape[-1] // PACK_COLS, row_multiple)
    flat = jnp.pad(flat, [(0, 0)] * (flat.ndim - 1) + [(0, rows * PACK_COLS - flat.shape[-1])])
    return flat.reshape(flat.shape[:-1] + (rows, PACK_COLS))


def _unpack_rows(buf, sizes):
    flat = buf.reshape(buf.shape[:-2] + (-1,))
    out, off = [], 0
    for n in sizes:
        out.append(flat[..., off:off + n])
        off += _round_up(n, PACK_COLS)
    return out


def _shard_2d(a):
    return a.reshape(a.shape[-2:]) if a.ndim > 2 else a


def _kl_shard(a, by_cols):
    return _shard_2d(a).T if by_cols else _shard_2d(a)


_GROUPS_FIRST = (("a_w_in",),)
_GROUPS_LATER = (("a_w_out", "b_w_in", "b_w_out"), ("b_w_uq",), ("kv_w_down",), ("kv_w_uk", "kv_w_uv"))
_SMALL_SHARDED = ("meta_tokens", "a_norm", "a_conv")
_BY_COLS = {name: by_cols for name, by_cols, _ in _SHARDED}
ROW_ALIGN = 16


def _stack_rows(pieces):
    padded, starts, row = [], [], 0
    for p in pieces:
        r = p.shape[-2]
        padded.append(jnp.pad(p, [(0, 0)] * (p.ndim - 2) + [(0, _round_up(r, ROW_ALIGN) - r), (0, 0)]))
        starts.append(row)
        row += _round_up(r, ROW_ALIGN)
    return jnp.concatenate(padded, -2), starts


def _stack_group(arrays_by_name, names):
    arrays = [arrays_by_name[n].astype(BF16) for n in names]
    buf, starts = _stack_rows(arrays)
    return buf, [(n, s, a.shape[-2]) for n, s, a in zip(names, starts, arrays, strict=True)]


def _stack_groups(arrays_by_name, groups):
    stacked = [_stack_group(arrays_by_name, names) for names in groups]
    return [b for b, _ in stacked], [entries for _, entries in stacked]


def _full_from_gathered(gathered, layout):
    full = {}
    for got, entries in zip(gathered, layout, strict=True):
        for name, start, rows in entries:
            full[name] = got[:, start:start + rows].reshape(N_DEV * rows, got.shape[-1])
    return full


def gather_small_weights(local):
    small = [_kl_shard(local[n], _BY_COLS[n]) for n in _SMALL_SHARDED]
    (gathered,) = _exchange("all_gather", [_pack_rows([s.reshape(-1) for s in small], 8)], scatter=False)
    full = {}
    for name, part, sh in zip(_SMALL_SHARDED, _unpack_rows(gathered, [s.size for s in small]), small, strict=True):
        full[name] = part.reshape(N_DEV * sh.shape[0], sh.shape[1])
    full["a_norm"] = full["a_norm"].reshape(1, -1)
    return full


class LaterExchanges:
    def __init__(self, local):
        shards = {n: _kl_shard(local[n], _BY_COLS[n]) for names in _GROUPS_FIRST + _GROUPS_LATER for n in names}
        self.first_gather_bufs, self.first_layout = _stack_groups(shards, _GROUPS_FIRST)
        self.gather_bufs, self.layout = _stack_groups(shards, _GROUPS_LATER)

    def finish_first(self, gathered):
        return _full_from_gathered(gathered, self.first_layout)

    def finish(self, gathered):
        return _full_from_gathered(gathered, self.layout)

    def scatter_bufs(self, grads):
        return _stack_groups(_owner_slices(grads, _GROUPS_LATER), _GROUPS_LATER)[0]

    def last_scatter_bufs(self, grads):
        bufs, self.last_layout = _stack_groups(_owner_slices(grads, _GROUPS_FIRST), _GROUPS_FIRST)
        return bufs


def _owner_slices(grads, groups):
    return {n: grads[n].reshape(N_DEV, -1, grads[n].shape[-1]) for names in groups for n in names}


def reduce_contributions(name, recv):
    _, r, c = recv.shape
    tr = max(d for d in range(8, 513, 8) if r % d == 0 and (d % ROW_ALIGN == 0 or recv.dtype == F32))

    def body(g_ref, o_ref):
        g = g_ref[0].astype(F32)
        for dev in range(1, N_DEV):
            g = g + g_ref[dev].astype(F32)
        o_ref[...] = g

    return pl.pallas_call(
        body, grid=(r // tr,), in_specs=[pl.BlockSpec((N_DEV, tr, c), lambda i: (0, i, 0))], out_specs=pl.BlockSpec((tr, c), lambda i: (i, 0)),
        out_shape=jax.ShapeDtypeStruct((r, c), F32), compiler_params=_cparams(("arbitrary",)), name=name)(recv)


def adamw_all(gs, ws, ms, vs):
    n = len(gs)

    def body(*refs):
        for i in range(n):
            g_ref, w_ref, m_ref, v_ref = (refs[j * n + i] for j in range(4))
            d_ref, mo_ref, vo_ref = (refs[(4 + j) * n + i] for j in range(3))
            g = g_ref[...]
            m_new = ADAM_B1 * m_ref[...] + (1.0 - ADAM_B1) * g
            v_new = ADAM_B2 * v_ref[...] + (1.0 - ADAM_B2) * (g * g)
            m_hat = m_new / (1.0 - ADAM_B1 ** ADAM_STEP)
            v_hat = v_new / (1.0 - ADAM_B2 ** ADAM_STEP)
            d_ref[...] = -ADAM_LR * (m_hat / (jnp.sqrt(v_hat) + ADAM_EPS) + ADAM_WD * w_ref[...])
            mo_ref[...] = m_new
            vo_ref[...] = v_new

    out = [jax.ShapeDtypeStruct(g.shape, F32) for g in gs] * 3
    res = pl.pallas_call(body, out_shape=out, compiler_params=pltpu.CompilerParams(vmem_limit_bytes=VMEM_LIMIT), name="adamw_all")(*gs, *ws, *ms, *vs)
    return res[:n], res[n:2 * n], res[2 * n:]


def kernel(x, meta_tokens, a_norm, a_w_in, a_conv, a_log, a_dt_bias, a_o_gain, a_w_out, kv_norm, kv_w_down, kv_latent_norm, kv_w_uk, kv_w_uv, k_gain, b_norm, b_w_in, b_q_latent_norm, b_w_uq, b_q_gain, b_w_out, loss_target, m_meta_tokens, m_a_norm, m_a_w_in, m_a_conv, m_a_log, m_a_dt_bias, m_a_o_gain, m_a_w_out, m_kv_norm, m_kv_w_down, m_kv_latent_norm, m_kv_w_uk, m_kv_w_uv, m_k_gain, m_b_norm, m_b_w_in, m_b_q_latent_norm, m_b_w_uq, m_b_q_gain, m_b_w_out, v_meta_tokens, v_a_norm, v_a_w_in, v_a_conv, v_a_log, v_a_dt_bias, v_a_o_gain, v_a_w_out, v_kv_norm, v_kv_w_down, v_kv_latent_norm, v_kv_w_uk, v_kv_w_uv, v_k_gain, v_b_norm, v_b_w_in, v_b_q_latent_norm, v_b_w_uq, v_b_q_gain, v_b_w_out):
    given = dict(locals())
    local_w = {n: given[n] for n in _ALL_WEIGHTS}
    full = gather_small_weights(local_w)
    for n in _REPLICATED:
        full[n] = local_w[n]
    later = LaterExchanges(local_w)

    loss_part, grad_x, grads, received_riding = local_step(x, loss_target, full, later)

    exact = [grads[n].reshape(N_DEV, -1) for n in _SMALL_SHARDED]
    exact += [jnp.broadcast_to(grads[n].reshape(1, -1), (N_DEV, grads[n].size)) for n in _REPLICATED]
    exact.append(jnp.broadcast_to(loss_part, (N_DEV, 1)))
    received = list(received_riding) + list(_exchange("all_to_all", [_pack_rows(exact, 8)], scatter=True))
    layout = later.layout + later.last_layout
    summed = [reduce_contributions(f"reduce_{i}", r) for i, r in enumerate(received)]

    grad_kl = {}
    for got, entries in zip(summed, layout):
        for n, start, rows in entries:
            grad_kl[n] = got[start:start + rows]
    parts = _unpack_rows(summed[-1], [p.shape[1] for p in exact])
    for n, part in zip(_SMALL_SHARDED + _REPLICATED, parts, strict=False):
        grad_kl[n] = part
    loss = parts[-1][0]

    def natural_2d(n, a):
        shape = _shard_2d(local_w[n]).shape if local_w[n].ndim > 1 else (1, local_w[n].size)
        return a.reshape(shape[::-1]).T if _BY_COLS.get(n, False) else a.reshape(shape)

    as_2d = lambda n, a: a.reshape(natural_2d(n, grad_kl[n]).shape)
    gs = [natural_2d(n, grad_kl[n]) for n in _ALL_WEIGHTS]
    deltas, new_m, new_v = adamw_all(gs, [as_2d(n, local_w[n]) for n in _ALL_WEIGHTS], [as_2d(n, given["m_" + n]) for n in _ALL_WEIGHTS],
                                     [as_2d(n, given["v_" + n]) for n in _ALL_WEIGHTS])
    results = [a.reshape(local_w[n].shape) for group in (gs, deltas, new_m, new_v) for n, a in zip(_ALL_WEIGHTS, group, strict=True)]
    return (loss, grad_x, *results)
```

```python
import dataclasses
import functools
import math

import jax
import jax.numpy as jnp
from jax import lax
from jax.experimental import pallas as pl
from jax.experimental.pallas import tpu as pltpu

F32 = jnp.float32
BF16 = jnp.bfloat16
_MXU_DTYPE = jnp.bfloat16

N_DEV = 8
D_MODEL = 1024
N_HEADS = 8
HEAD = 128
CHUNK = 64
N_META = 16
PAD_ROWS = 2 * CHUNK - N_META
LEAD = PAD_ROWS + N_META
ROPE = 64
QK_DIM = HEAD + ROPE
QK_PAD = 2 * HEAD
KV_RANK = 256
Q_RANK = 384
CONV_K = 4
EPS = 1e-6
NEG = -1e30
ROPE_THETA = 10000.0
ADAM_LR, ADAM_B1, ADAM_B2, ADAM_EPS, ADAM_WD, ADAM_STEP = 0.001, 0.9, 0.999, 1e-08, 0.01, 10
PACK_COLS = 512
VMEM_LIMIT = 56 * 1024 * 1024


def _pick(n, options):
    for o in options:
        if n % o == 0:
            return o
    raise ValueError(f"no tile for {n} among {options}")


def _cparams(sem):
    return pltpu.CompilerParams(dimension_semantics=sem, vmem_limit_bytes=VMEM_LIMIT)


def _dims(a, dims):
    if a.ndim == 2:
        return (dims, ((), ()))
    (ca,), (cb,) = dims
    return (((ca + 1,), (cb + 1,)), ((0,), (0,)))


def _dot(a, b, dims):
    return lax.dot_general(a.astype(_MXU_DTYPE), b.astype(_MXU_DTYPE), _dims(a, dims), preferred_element_type=F32)


@jax.custom_vjp
def mm_nn(a, b):
    return _dot(a, b, ((1,), (0,)))


@jax.custom_vjp
def mm_nt(a, b):
    return _dot(a, b, ((1,), (1,)))


@jax.custom_vjp
def mm_tn(a, b):
    return _dot(a, b, ((0,), (0,)))


mm_nn.defvjp(lambda a, b: (mm_nn(a, b), (a, b)), lambda r, g: (mm_nt(g, r[1]), mm_tn(r[0], g)))
mm_nt.defvjp(lambda a, b: (mm_nt(a, b), (a, b)), lambda r, g: (mm_nn(g, r[1]), mm_tn(g, r[0])))
mm_tn.defvjp(lambda a, b: (mm_tn(a, b), (a, b)), lambda r, g: (mm_nt(r[1], g), mm_nn(r[0], g)))


def _split_terms(x, n):
    terms, rest = [], x
    for _ in range(n):
        t = rest.astype(_MXU_DTYPE)
        terms.append(t)
        rest = rest - t.astype(F32)
    return terms


def _dot_01_raw(m, x, dims):
    m = m.astype(_MXU_DTYPE)
    return sum(lax.dot_general(m, t, _dims(m, dims), preferred_element_type=F32) for t in _split_terms(x, 3))


@jax.custom_vjp
def _dot_01(m, x):
    return _dot_01_raw(m, x, ((1,), (0,)))


_dot_01.defvjp(lambda m, x: (_dot_01(m, x), m), lambda m, g: (jnp.zeros_like(m), _dot_01_raw(m, g, ((0,), (0,)))))


def _inv_unit_lower(a):
    n = a.shape[-1]
    eye = (lax.broadcasted_iota(jnp.int32, (n, n), 0) == lax.broadcasted_iota(jnp.int32, (n, n), 1)).astype(F32)
    d = lambda u, w: lax.dot_general(u, w, _dims(u, ((1,), (0,))), preferred_element_type=F32)
    t = eye - a
    p = a.astype(_MXU_DTYPE)
    p = d(p, p)
    squarings = int(math.log2(n)) - 1
    for s in range(squarings):
        ph = p.astype(_MXU_DTYPE)
        t_hi, t_lo = _split_terms(t, 2)
        t = t + (d(t_hi, ph) + d(t_lo, ph))
        if s + 1 < squarings:
            p = d(ph, ph)
    return t


@jax.custom_vjp
def _inv_lookup(a, t):
    return t


def _inv_lookup_bwd(t, g):
    return -mm_tn(t, mm_nt(g, t)), jnp.zeros_like(t)


_inv_lookup.defvjp(lambda a, t: (t, t), _inv_lookup_bwd)


def _sigmoid(x):
    return 1.0 / (1.0 + jnp.exp(-x))


@jax.custom_vjp
def _silu(x):
    return x * _sigmoid(x)


def _silu_fwd(x):
    s = _sigmoid(x)
    return x * s, (x, s)


_silu.defvjp(_silu_fwd, lambda r, g: (g * (r[1] * (1.0 + r[0] * (1.0 - r[1]))),))


def _softplus(x):
    return jnp.where(x > 20.0, x, jnp.log(1.0 + jnp.exp(jnp.minimum(x, 20.0))))


def _rms(x, g, width=None):
    ms = jnp.sum(x * x, -1, keepdims=True) / (x.shape[-1] if width is None else width)
    return x * lax.rsqrt(ms + EPS) * g


MM_VMEM_BUDGET = 40 * 1024 * 1024


def _matmul_rows(name, a, b, mode, out_dtype, res, scatter):
    m, k = a.shape
    n = b.shape[1] if mode == "nn" else b.shape[0]
    dims = {"nn": ((1,), (0,)), "nt": ((1,), (1,))}[mode]
    out_bytes = jnp.dtype(out_dtype).itemsize
    n_in, nx = 2 + (res is not None), len(scatter)

    def vmem(tm):
        blocks = 2 * tm * k * a.dtype.itemsize + 2 * k * n * b.dtype.itemsize + 2 * tm * n * out_bytes + tm * n * 4
        return blocks + (2 * tm * n * res.dtype.itemsize if res is not None else 0)

    tm = next(c for c in (2176, 1088, 512, 256, 128, 64) if m % c == 0 and vmem(c) <= MM_VMEM_BUDGET)
    steps = m // tm

    def body(*refs):
        a_ref, b_ref, o_ref = refs[0], refs[1], refs[n_in + nx]
        i = pl.program_id(0)
        finish = _ride(scatter, True, refs[n_in:n_in + nx], refs[n_in + nx + 1:n_in + 2 * nx + 1], refs[n_in + 2 * nx + 1:], i == 0, i == steps - 1)
        out = _dot(a_ref[...], b_ref[...], dims)
        if res is not None:
            out = out + refs[2][...].astype(F32)
        o_ref[...] = out.astype(o_ref.dtype)
        finish()

    o_spec = pl.BlockSpec((tm, n), lambda i: (i, 0))
    in_specs = [pl.BlockSpec((tm, k), lambda i: (i, 0)), pl.BlockSpec(b.shape, lambda i: (0, 0))] + ([o_spec] if res is not None else [])
    args = (a, b) + ((res,) if res is not None else ())
    out = pl.pallas_call(
        body, grid=(steps,), in_specs=in_specs + [_HBM] * nx, out_specs=[o_spec] + [_HBM] * nx,
        out_shape=[jax.ShapeDtypeStruct((m, n), out_dtype)] + Exchange.out_shape(scatter, True), scratch_shapes=Exchange.scratch(nx) if nx else [],
        compiler_params=_cparams(("arbitrary",) if nx else ("parallel",)), name=name)(*args, *scatter)
    return out if nx else out[0]


def matmul(name, a, b, mode, out_dtype=None, res=None, scatter=()):
    if mode != "tn":
        return _matmul_rows(name, a, b, mode, out_dtype or F32, res, scatter)
    out_dtype = out_dtype or _MXU_DTYPE
    (k, m), (k2, n) = a.shape, b.shape
    assert k == k2 and res is None, (name, a.shape, b.shape, mode)
    tm = _pick(m, (m if m <= 1536 else 1024, 1024, 512, 384, 256, 128))
    tn = _pick(n, (1024, 512, 384, 256, 128))
    tk = _pick(k, (512, 256, 128))
    nk = k // tk
    dims = ((0,), (0,))

    def body(*refs):
        if res is None:
            a_ref, b_ref, o_ref, acc_ref = refs
        else:
            a_ref, b_ref, r_ref, o_ref, acc_ref = refs
        kk = pl.program_id(2)

        @pl.when(kk == 0)
        def _():
            acc_ref[...] = jnp.zeros_like(acc_ref)

        acc_ref[...] += _dot(a_ref[...], b_ref[...], dims)

        @pl.when(kk == nk - 1)
        def _():
            out = acc_ref[...]
            if res is not None:
                out = out + r_ref[...].astype(F32)
            o_ref[...] = out.astype(o_ref.dtype)

    a_spec = pl.BlockSpec((tk, tm), lambda i, j, kk: (kk, i)) if mode == "tn" else pl.BlockSpec((tm, tk), lambda i, j, kk: (i, kk))
    b_spec = pl.BlockSpec((tn, tk), lambda i, j, kk: (j, kk)) if mode == "nt" else pl.BlockSpec((tk, tn), lambda i, j, kk: (kk, j))
    o_spec = pl.BlockSpec((tm, tn), lambda i, j, kk: (i, j))
    in_specs = [a_spec, b_spec] + ([o_spec] if res is not None else [])
    args = (a, b) + ((res,) if res is not None else ())
    return pl.pallas_call(
        body, grid=(m // tm, n // tn, nk), in_specs=in_specs, out_specs=o_spec,
        out_shape=jax.ShapeDtypeStruct((m, n), out_dtype), scratch_shapes=[pltpu.VMEM((tm, tn), F32)],
        compiler_params=_cparams(("parallel", "parallel", "arbitrary")), name=name)(*args)


@dataclasses.dataclass
class Arg:
    arr: jax.Array
    kind: str = "row"
    bc: int = 0
    base: int = 0
    ph: bool = False
    diff: bool = False
    gdt: object = F32


def _arg_spec(a, tr, nh, ntab, base=None):
    bc = a.bc or a.arr.shape[1]
    base = a.base if base is None else base
    width = bc * nh if a.ph else bc
    col = base // nh if a.ph else base
    assert not a.ph or base % nh == 0
    if a.kind == "row":
        return pl.BlockSpec((tr, width), lambda i: (i, col))
    if a.kind == "tab":
        return pl.BlockSpec((tr, width), lambda i: (i % ntab, col))
    return pl.BlockSpec((a.arr.shape[0], width), lambda i: (0, col))


def _head_view(ref, a, h, rs):
    bc = a.bc or a.arr.shape[1]
    rows = slice(None) if a.kind == "par" else rs
    v = ref[rows, h * bc:(h + 1) * bc] if a.ph else ref[rows, :]
    return v.astype(F32) if jnp.issubdtype(v.dtype, jnp.floating) else v


def row_call(name, fn, args, outs, tr, nh=1, ntab=1):
    t = args[0].arr.shape[0]
    n_in = len(args)
    out_args = [Arg(None, "row", bc, 0, ph) for (_, _, bc, ph) in outs]
    assert all(a.ph or nh == 1 for a in out_args)
    rs = slice(None)

    def body(*refs):
        for h in range(nh):
            res = fn(*[_head_view(r, a, h, rs) for r, a in zip(refs[:n_in], args, strict=True)])
            for r, a, v in zip(refs[n_in:], out_args, res, strict=True):
                r[rs, h * a.bc:(h + 1) * a.bc] = v.astype(r.dtype)

    return pl.pallas_call(
        body, grid=(t // tr,), in_specs=[_arg_spec(a, tr, nh, ntab) for a in args], out_specs=[_arg_spec(a, tr, nh, ntab) for a in out_args],
        out_shape=[jax.ShapeDtypeStruct((t, cols), dt) for (cols, dt, _, _) in outs],
        compiler_params=_cparams(("arbitrary",)), name=name)(*[a.arr for a in args])


def row_vjp_call(name, fn, args, cts, tr, nh=1, ntab=1):
    t = args[0].arr.shape[0]
    n_in, n_ct = len(args), len(cts)
    diff_idx = [k for k, a in enumerate(args) if a.diff]
    def body(*refs):
        out_refs = refs[n_in + n_ct:]
        par_sum = {}
        for k, r in zip(diff_idx, out_refs, strict=True):
            if args[k].kind == "par":
                @pl.when(pl.program_id(0) == 0)
                def _(r=r):
                    r[...] = jnp.zeros_like(r)

        for rs in (slice(None),):
            row_sum = {}
            for h in range(nh):
                vals = [_head_view(r, a, h, rs) for r, a in zip(refs[:n_in], args, strict=True)]
                ct_vals = tuple(_head_view(r, a, h, rs) for r, a in zip(refs[n_in:n_in + n_ct], cts, strict=True))

                def f(*dv, vals=vals):
                    full = list(vals)
                    for k, v in zip(diff_idx, dv, strict=True):
                        full[k] = v
                    return tuple(fn(*full))

                _, vjp = jax.vjp(f, *[vals[k] for k in diff_idx])
                for j, (k, r, g) in enumerate(zip(diff_idx, out_refs, vjp(ct_vals), strict=True)):
                    a = args[k]
                    bc = a.bc or a.arr.shape[1]
                    if a.kind == "row" and a.ph:
                        r[rs, h * bc:(h + 1) * bc] = g.astype(r.dtype)
                    elif a.kind == "row":
                        row_sum[j] = g if j not in row_sum else row_sum[j] + g
                    else:
                        key = (j, h if a.ph else 0)
                        par_sum[key] = g if key not in par_sum else par_sum[key] + g
            for j, g in row_sum.items():
                out_refs[j][rs, :] = g.astype(out_refs[j].dtype)
        for (j, h), g in par_sum.items():
            bc = g.shape[1]
            out_refs[j][:, h * bc:(h + 1) * bc] += g

    out_specs, out_shape = [], []
    for k in diff_idx:
        a = args[k]
        bc = a.bc or a.arr.shape[1]
        out_specs.append(_arg_spec(a, tr, nh, ntab, base=0))
        out_shape.append(jax.ShapeDtypeStruct((t if a.kind == "row" else a.arr.shape[0], bc * (nh if a.ph else 1)), a.gdt if a.kind == "row" else F32))
    in_specs = [_arg_spec(a, tr, nh, ntab) for a in list(args) + list(cts)]
    return pl.pallas_call(
        body, grid=(t // tr,), in_specs=in_specs, out_specs=out_specs, out_shape=out_shape,
        compiler_params=_cparams(("arbitrary",)), name=name)(*[a.arr for a in list(args) + list(cts)])


def _conv_taps(x, w):
    rows = lax.broadcasted_iota(jnp.int32, x.shape, 0)
    y = x * w[CONV_K - 1:CONV_K, :]
    for s in range(1, CONV_K):
        y = y + jnp.where(rows >= s, pltpu.roll(x, s, 0), 0.0) * w[CONV_K - 1 - s:CONV_K - s, :]
    return y


CONV_HEADS = 4
CONV_BLOCKS_PER_THIRD = N_HEADS // CONV_HEADS


def _conv_post(y, block):
    a = _silu(y)
    normed = block < 2 * CONV_BLOCKS_PER_THIRD
    scale = jnp.where(block < CONV_BLOCKS_PER_THIRD, HEAD ** -0.5, 1.0)
    return a * jnp.where(normed, lax.rsqrt(jnp.sum(a * a, -1, keepdims=True) + EPS) * scale, 1.0)


def conv_fwd(z, w, lp):
    t, width = z.shape
    cols = CONV_HEADS * HEAD

    def body(z_ref, w_ref, o_ref, y_ref):
        block = pl.program_id(1)
        for h in range(CONV_HEADS):
            cs = slice(h * HEAD, (h + 1) * HEAD)
            y = _conv_taps(z_ref[:, cs].astype(F32), w_ref[:, cs])
            y_ref[:, cs] = y.astype(y_ref.dtype)
            o_ref[:, cs] = _conv_post(y, block)

    blk = pl.BlockSpec((lp, cols), lambda b, j: (b, j))
    out = jax.ShapeDtypeStruct((t, width), F32)
    return pl.pallas_call(
        body, grid=(t // lp, width // cols), in_specs=[blk, pl.BlockSpec((CONV_K, cols), lambda b, j: (0, j))],
        out_specs=[blk, blk], out_shape=[out, jax.ShapeDtypeStruct((t, width), _MXU_DTYPE)],
        compiler_params=_cparams(("arbitrary", "arbitrary")), name="a_conv_fwd")(z, w)


def conv_bwd(z, y, w, dout, lp):
    t, width = z.shape
    cols = CONV_HEADS * HEAD

    def body(z_ref, y_ref, w_ref, g_ref, dz_ref, dw_ref):
        block = pl.program_id(0)

        @pl.when(pl.program_id(1) == 0)
        def _():
            dw_ref[...] = jnp.zeros_like(dw_ref)

        for h in range(CONV_HEADS):
            cs = slice(h * HEAD, (h + 1) * HEAD)
            x, wv = z_ref[:, cs].astype(F32), w_ref[:, cs]
            _, vjp = jax.vjp(lambda y_: _conv_post(y_, block), y_ref[:, cs].astype(F32))
            (dy,) = vjp(g_ref[:, cs])
            rows = lax.broadcasted_iota(jnp.int32, x.shape, 0)
            dx = dy * wv[CONV_K - 1:CONV_K, :]
            dw_ref[CONV_K - 1:CONV_K, cs] += jnp.sum(dy * x, axis=0, keepdims=True)
            for s in range(1, CONV_K):
                dy_up = jnp.where(rows < lp - s, pltpu.roll(dy, lp - s, 0), 0.0)
                dx = dx + dy_up * wv[CONV_K - 1 - s:CONV_K - s, :]
                dw_ref[CONV_K - 1 - s:CONV_K - s, cs] += jnp.sum(dy_up * x, axis=0, keepdims=True)
            dz_ref[:, cs] = dx.astype(dz_ref.dtype)

    blk = pl.BlockSpec((lp, cols), lambda j, b: (b, j))
    w_blk = pl.BlockSpec((CONV_K, cols), lambda j, b: (0, j))
    return pl.pallas_call(
        body, grid=(width // cols, t // lp), in_specs=[blk, blk, w_blk, blk], out_specs=[blk, w_blk],
        out_shape=[jax.ShapeDtypeStruct((t, width), _MXU_DTYPE), jax.ShapeDtypeStruct((CONV_K, width), F32)],
        compiler_params=_cparams(("arbitrary", "arbitrary")), name="a_conv_bwd")(z, y, w, dout)


def _delta_chunk(q, k, v, ba, alog, dtb, state, t_stored):
    n_g, c = q.shape[0], q.shape[1]
    lane = lax.broadcasted_iota(jnp.int32, (1, HEAD), 1)

    def pick(xs, offset):
        cols = [jnp.sum(xs[i // N_HEADS if len(xs) > 1 else 0] * (lane == offset + i % N_HEADS).astype(F32), axis=1, keepdims=True)[None]
                for i in range(n_g)]
        return jnp.concatenate(cols, 0)

    b_raw, a_raw = pick(ba, 0), pick(ba, N_HEADS)
    a_log, dt_bias = pick((alog,), 0), pick((dtb,), 0)
    beta = _sigmoid(b_raw)
    g = -jnp.exp(a_log) * _softplus(a_raw + dt_bias)
    ri = lax.broadcasted_iota(jnp.int32, (c, c), 0)
    ci = lax.broadcasted_iota(jnp.int32, (c, c), 1)
    tril = ci <= ri
    lower = jnp.broadcast_to(tril.astype(F32), (n_g, c, c))
    gc_col = _dot_01(lower, g * jnp.ones((1, 1, HEAD), F32))[:, :, :1]
    gc_row = _dot_01(jnp.ones((n_g, 8, c), F32), g * (ri <= ci).astype(F32)[None])[:, 0:1, :]
    gc_last = jnp.sum(g, axis=1, keepdims=True)
    decay = jnp.exp(jnp.where(tril, gc_col - gc_row, NEG))
    e_gc = jnp.exp(gc_col)
    kb = k * beta
    a_mat = jnp.where(ci < ri, mm_nt(kb, k) * decay, 0.0)
    t_inv = _inv_unit_lower(a_mat) if t_stored is None else _inv_lookup(a_mat, t_stored)
    u_base = mm_nn(t_inv, v * beta)
    w_dec = mm_nn(t_inv, kb * e_gc)
    attn = jnp.where(tril, mm_nt(q, k) * decay, 0.0)
    u = u_base - mm_nn(w_dec, state)
    o = mm_nn(q * e_gc, state) + mm_nn(attn, u)
    new_state = state * jnp.exp(gc_last) + mm_tn(k * jnp.exp(gc_last - gc_col), u)
    return o, new_state, t_inv


DELTA_STEP_FWD = (4, 2)
DELTA_STEP_BWD = (2, 2)


def _heads_of(ref, rs, first_col):
    return jnp.stack([ref[i // N_HEADS, rs, first_col + (i % N_HEADS) * HEAD:first_col + (i % N_HEADS + 1) * HEAD]
                      for i in range(ref.shape[0] * N_HEADS)])


def _qkv_heads(ref, rs, part):
    return _heads_of(ref, rs, part * N_HEADS * HEAD)


def _by_sequence(a, lp):
    return a.reshape(a.shape[0] // lp, lp, a.shape[1])


def _ride(bufs, scatter, refs_in, refs_out, sems, first, last, two_level=False):
    if not bufs:
        return lambda: None
    make = lambda: (TwoLevelGather if two_level else Exchange)(refs_in, refs_out, *sems, scatter)

    @pl.when(first)
    def _():
        make().start()

    def finish():
        @pl.when(last)
        def _():
            make().wait()

    return finish


def delta_fwd(qkv, ba, ba_block, alog, dtb, lp, gather=()):
    t = qkv.shape[0]
    nb, nc = t // lp, lp // CHUNK
    seqs, cps = DELTA_STEP_FWD
    ng, rows = nc // cps, cps * CHUNK
    nx = len(gather)
    nbg = nb // seqs
    assert nc % cps == 0 and nb % seqs == 0

    def body(*refs):
        qkv_ref, ba_ref, al_ref, dt_ref = refs[:4]
        o_ref, s_ref, t_ref = refs[4 + nx:7 + nx]
        state_ref = refs[7 + 2 * nx]
        b, n = pl.program_id(0), pl.program_id(1)
        finish = _ride(gather, False, refs[4:4 + nx], refs[7 + nx:7 + 2 * nx], refs[8 + 2 * nx:], (b == 0) & (n == 0), (b == nbg - 1) & (n == ng - 1))

        @pl.when(n == 0)
        def _():
            state_ref[...] = jnp.zeros_like(state_ref)

        al, dtv = al_ref[...], dt_ref[...]
        for c in range(cps):
            rs = slice(c * CHUNK, (c + 1) * CHUNK)
            state = state_ref[...]
            o, new_state, t_inv = _delta_chunk(_qkv_heads(qkv_ref, rs, 0), _qkv_heads(qkv_ref, rs, 1), _qkv_heads(qkv_ref, rs, 2),
                                               tuple(ba_ref[i, rs, :] for i in range(seqs)), al, dtv, state, None)
            for i in range((seqs * N_HEADS)):
                seq, g = divmod(i, N_HEADS)
                o_ref[seq, rs, g * HEAD:(g + 1) * HEAD] = o[i]
                s_ref[seq, g, c] = state[i]
                t_ref[seq, g, c] = t_inv[i]
            state_ref[...] = new_state
        finish()

    rows_of = lambda width: pl.BlockSpec((seqs, rows, width), lambda b, n: (b, n, 0))
    par_spec = pl.BlockSpec((1, HEAD), lambda b, n: (0, 0))
    out = pl.pallas_call(
        body, grid=(nbg, ng),
        in_specs=[rows_of(3 * N_HEADS * HEAD), pl.BlockSpec((seqs, rows, HEAD), lambda b, n: (b, n, ba_block)), par_spec, par_spec] + [_HBM] * nx,
        out_specs=[rows_of(N_HEADS * HEAD), pl.BlockSpec((seqs, N_HEADS, cps, HEAD, HEAD), lambda b, n: (b, 0, n, 0, 0)),
                   pl.BlockSpec((seqs, N_HEADS, cps, CHUNK, CHUNK), lambda b, n: (b, 0, n, 0, 0))] + [_HBM] * nx,
        out_shape=[jax.ShapeDtypeStruct((nb, lp, N_HEADS * HEAD), F32), jax.ShapeDtypeStruct((nb, N_HEADS, nc, HEAD, HEAD), F32),
                   jax.ShapeDtypeStruct((nb, N_HEADS, nc, CHUNK, CHUNK), F32)] + Exchange.out_shape(gather, False),
        scratch_shapes=[pltpu.VMEM(((seqs * N_HEADS), HEAD, HEAD), F32)] + (Exchange.scratch(nx) if nx else []),
        compiler_params=_cparams(("arbitrary", "arbitrary")), name="delta_fwd")(_by_sequence(qkv, lp), _by_sequence(ba, lp), alog, dtb, *gather)
    return [out[0].reshape(t, N_HEADS * HEAD)] + list(out[1:])


def delta_bwd(qkv, ba, ba_block, alog, dtb, states, t_invs, do, lp, scatter=()):
    t = qkv.shape[0]
    nb, nc = t // lp, lp // CHUNK
    seqs, cps = DELTA_STEP_BWD
    ng, rows = nc // cps, cps * CHUNK
    nx = len(scatter)
    nbg = nb // seqs

    def body(*refs):
        qkv_ref, ba_ref, al_ref, dt_ref, s_ref, t_ref, do_ref = refs[:7]
        dqkv_ref, dba_ref, dal_ref, ddt_ref = refs[7 + nx:11 + nx]
        dstate_ref = refs[11 + 2 * nx]
        b, step = pl.program_id(0), pl.program_id(1)
        finish = _ride(scatter, True, refs[7:7 + nx], refs[11 + nx:11 + 2 * nx], refs[12 + 2 * nx:], (b == 0) & (step == 0),
                       (b == nbg - 1) & (step == ng - 1))

        @pl.when(step == 0)
        def _():
            dstate_ref[...] = jnp.zeros_like(dstate_ref)

        @pl.when((b == 0) & (step == 0))
        def _():
            dal_ref[...] = jnp.zeros_like(dal_ref)
            ddt_ref[...] = jnp.zeros_like(ddt_ref)

        al, dtv = al_ref[...], dt_ref[...]
        d_al = jnp.zeros((1, HEAD), F32)
        d_dt = jnp.zeros((1, HEAD), F32)
        for c in reversed(range(cps)):
            rs = slice(c * CHUNK, (c + 1) * CHUNK)
            t_n = jnp.stack([t_ref[i // N_HEADS, i % N_HEADS, c] for i in range((seqs * N_HEADS))])
            s_n = jnp.stack([s_ref[i // N_HEADS, i % N_HEADS, c] for i in range((seqs * N_HEADS))])

            def f(q_, k_, v_, ba_, al_, dt_, s_, t_n=t_n):
                return _delta_chunk(q_, k_, v_, ba_, al_, dt_, s_, t_n)[:2]

            _, vjp = jax.vjp(f, _qkv_heads(qkv_ref, rs, 0), _qkv_heads(qkv_ref, rs, 1), _qkv_heads(qkv_ref, rs, 2), tuple(ba_ref[i, rs, :] for i in range(seqs)), al, dtv, s_n)
            grads = vjp((_heads_of(do_ref, rs, 0), dstate_ref[...]))
            for part in range(3):
                for i in range((seqs * N_HEADS)):
                    col = (part * N_HEADS + i % N_HEADS) * HEAD
                    dqkv_ref[i // N_HEADS, rs, col:col + HEAD] = grads[part][i]
            for i in range(seqs):
                dba_ref[i, rs, :] = grads[3][i]
            d_al, d_dt = d_al + grads[4], d_dt + grads[5]
            dstate_ref[...] = grads[6]
        dal_ref[...] += d_al
        ddt_ref[...] += d_dt
        finish()

    rows_of = lambda width: pl.BlockSpec((seqs, rows, width), lambda b, n: (b, ng - 1 - n, 0))
    par_spec = pl.BlockSpec((1, HEAD), lambda b, n: (0, 0))
    out = pl.pallas_call(
        body, grid=(nbg, ng),
        in_specs=[rows_of(3 * N_HEADS * HEAD), pl.BlockSpec((seqs, rows, HEAD), lambda b, n: (b, ng - 1 - n, ba_block)), par_spec, par_spec,
                  pl.BlockSpec((seqs, N_HEADS, cps, HEAD, HEAD), lambda b, n: (b, 0, ng - 1 - n, 0, 0)),
                  pl.BlockSpec((seqs, N_HEADS, cps, CHUNK, CHUNK), lambda b, n: (b, 0, ng - 1 - n, 0, 0)), rows_of(N_HEADS * HEAD)] + [_HBM] * nx,
        out_specs=[rows_of(3 * N_HEADS * HEAD), rows_of(HEAD), par_spec, par_spec] + [_HBM] * nx,
        out_shape=[jax.ShapeDtypeStruct((nb, lp, 3 * N_HEADS * HEAD), F32), jax.ShapeDtypeStruct((nb, lp, HEAD), F32),
                   jax.ShapeDtypeStruct((1, HEAD), F32), jax.ShapeDtypeStruct((1, HEAD), F32)] + Exchange.out_shape(scatter, True),
        scratch_shapes=[pltpu.VMEM(((seqs * N_HEADS), HEAD, HEAD), F32)] + (Exchange.scratch(nx) if nx else []),
        compiler_params=_cparams(("arbitrary", "arbitrary")), name="delta_bwd")(
            _by_sequence(qkv, lp), _by_sequence(ba, lp), alog, dtb, states, t_invs, _by_sequence(do, lp), *scatter)
    return [out[0].reshape(t, 3 * N_HEADS * HEAD), out[1].reshape(t, HEAD)] + list(out[2:])


ATT_Q_TILE = 256
ATT_K_TILE = 512
ATT_SCALE = QK_DIM ** -0.5


def _tiles(end, size):
    return [(s, min(s + size, end)) for s in range(0, end, size)]


def _att_visible(q0, q1, k0, k1, keys_first):
    if k1 <= q0 + CHUNK and k0 >= PAD_ROWS:
        return None
    shape = (k1 - k0, q1 - q0) if keys_first else (q1 - q0, k1 - k0)
    qpos = q0 + lax.broadcasted_iota(jnp.int32, shape, 1 if keys_first else 0)
    kpos = k0 + lax.broadcasted_iota(jnp.int32, shape, 0 if keys_first else 1)
    shift = CHUNK.bit_length() - 1
    return (jnp.right_shift(kpos, shift) <= jnp.right_shift(qpos, shift)) & (kpos >= PAD_ROWS)


def _att_seq_specs(lp):
    return pl.BlockSpec((lp, QK_PAD), lambda b, h: (b, h)), pl.BlockSpec((lp, HEAD), lambda b, h: (b, h))


def flash_fwd(q, k, v, lp):
    t = q.shape[0]
    qk_seq, o_seq = _att_seq_specs(lp)

    def body(q_ref, k_ref, v_ref, o_ref, lse_ref):
        q_tiles = _tiles(lp, ATT_Q_TILE)

        def score_steps(q0, q1, out):
            def step(k0, k1):
                s = mm_nt(q_ref[q0:q1, :], k_ref[k0:k1, :])
                vis = _att_visible(q0, q1, k0, k1, False)
                s = s if vis is None else jnp.where(vis, s, NEG)
                out["scores"].append(s)
                row_max = jnp.max(s, -1, keepdims=True)
                out["m"] = row_max if out["m"] is None else jnp.maximum(out["m"], row_max)
            return [functools.partial(step, k0, k1) for k0, k1 in _tiles(q1, ATT_K_TILE)]

        cur = {"scores": [], "m": None}
        for step in score_steps(*q_tiles[0], cur):
            step()
        for i, (q0, q1) in enumerate(q_tiles):
            nxt = {"scores": [], "m": None}
            ahead = score_steps(*q_tiles[i + 1], nxt) if i + 1 < len(q_tiles) else []
            l = jnp.zeros((q1 - q0, 1), F32)
            acc = jnp.zeros((q1 - q0, HEAD), F32)
            for s, (k0, k1) in zip(cur["scores"], _tiles(q1, ATT_K_TILE), strict=True):
                if ahead:
                    ahead.pop(0)()
                p = jnp.exp(s - cur["m"])
                l = l + jnp.sum(p, -1, keepdims=True)
                acc = acc + mm_nn(p, v_ref[k0:k1, :])
            for step in ahead:
                step()
            o_ref[q0:q1, :] = acc / l
            lse_ref[q0:q1, :] = jnp.broadcast_to(cur["m"] + jnp.log(l), (q1 - q0, HEAD))
            cur = nxt

    big = jax.ShapeDtypeStruct((t, N_HEADS * HEAD), F32)
    return pl.pallas_call(
        body, grid=(t // lp, N_HEADS), in_specs=[qk_seq, qk_seq, o_seq], out_specs=[o_seq, o_seq], out_shape=[big, big],
        compiler_params=_cparams(("arbitrary", "arbitrary")), name="flash_fwd")(q, k, v)


def flash_bwd(q, k, v, o, lse, do, lp):
    t = q.shape[0]
    qk_seq, o_seq = _att_seq_specs(lp)

    def body(q_ref, k_ref, v_ref, o_ref, lse_ref, do_ref, dq_ref, dk_out_ref, dv_out_ref, dk_ref, dv_ref):
        dk_ref[...] = jnp.zeros_like(dk_ref)
        dv_ref[...] = jnp.zeros_like(dv_ref)
        for q0, q1 in _tiles(lp, ATT_Q_TILE):
            qb, dob = q_ref[q0:q1, :], do_ref[q0:q1, :]
            lse_row = jnp.transpose(lse_ref[q0:q1, :])[0:1, :]
            dsum_row = jnp.sum(jnp.transpose(dob * o_ref[q0:q1, :]), axis=0, keepdims=True)
            dq = jnp.zeros((q1 - q0, QK_PAD), F32)
            for k0, k1 in _tiles(q1, ATT_K_TILE):
                kb, vb = k_ref[k0:k1, :], v_ref[k0:k1, :]
                s = mm_nt(kb, qb)
                vis = _att_visible(q0, q1, k0, k1, True)
                s = s if vis is None else jnp.where(vis, s, NEG)
                p = jnp.exp(s - lse_row)
                ds = p * (mm_nt(vb, dob) - dsum_row)
                dv_ref[k0:k1, :] += mm_nn(p, dob)
                dk_ref[k0:k1, :] += mm_nn(ds, qb)
                dq = dq + mm_tn(ds, kb)
            dq_ref[q0:q1, :] = dq.astype(dq_ref.dtype)
        dk_out_ref[...] = dk_ref[...].astype(dk_out_ref.dtype)
        dv_out_ref[...] = dv_ref[...].astype(dv_out_ref.dtype)

    narrow = _MXU_DTYPE
    return pl.pallas_call(
        body, grid=(t // lp, N_HEADS), in_specs=[qk_seq, qk_seq, o_seq, o_seq, o_seq, o_seq], out_specs=[qk_seq, qk_seq, o_seq],
        out_shape=[jax.ShapeDtypeStruct((t, N_HEADS * QK_PAD), narrow), jax.ShapeDtypeStruct((t, N_HEADS * QK_PAD), narrow),
                   jax.ShapeDtypeStruct((t, N_HEADS * HEAD), narrow)],
        scratch_shapes=[pltpu.VMEM((lp, QK_PAD), F32), pltpu.VMEM((lp, HEAD), F32)],
        compiler_params=_cparams(("arbitrary", "arbitrary")), name="flash_bwd")(q, k, v, o, lse, do)


def loss_head(h2, target, lp):
    nb, seq, d = target.shape
    cols = _pick(d, (512, 128))
    ncol = d // cols

    def body(h_ref, t_ref, loss_ref, dh_ref, acc_ref):
        b, j = pl.program_id(0), pl.program_id(1)

        @pl.when((b == 0) & (j == 0))
        def _():
            acc_ref[...] = jnp.zeros_like(acc_ref)

        err = h_ref[LEAD:, :] - t_ref[...]
        dh_ref[:LEAD, :] = jnp.zeros((LEAD, cols), F32)
        dh_ref[LEAD:, :] = err * (1.0 / d)
        acc_ref[...] += jnp.sum(err * err, axis=0, keepdims=True)

        @pl.when((b == nb - 1) & (j == ncol - 1))
        def _():
            loss_ref[...] = jnp.sum(acc_ref[...], axis=1, keepdims=True) * (0.5 / d)

    return pl.pallas_call(
        body, grid=(nb, ncol),
        in_specs=[pl.BlockSpec((None, lp, cols), lambda b, j: (b, 0, j)), pl.BlockSpec((None, seq, cols), lambda b, j: (b, 0, j))],
        out_specs=[pl.BlockSpec((1, 1), lambda b, j: (0, 0)), pl.BlockSpec((None, lp, cols), lambda b, j: (b, 0, j))],
        out_shape=[jax.ShapeDtypeStruct((1, 1), F32), jax.ShapeDtypeStruct((nb, lp, d), F32)],
        scratch_shapes=[pltpu.VMEM((1, cols), F32)], compiler_params=_cparams(("arbitrary", "arbitrary")), name="loss_head")(h2, target)


def gated_out(name, o, gate, gain, w, res):
    t, kw = o.shape
    d = w.shape[1]
    tm = _pick(t, (512, 256, 128))

    def body(*refs):
        o_ref, gate_ref = refs[:2]
        w_ref, r_ref, h_ref, g_ref = refs[-4:]
        if gain is None:
            g_ref[...] = _f_gate(o_ref[...], gate_ref[...])[0].astype(g_ref.dtype)
        else:
            for h in range(N_HEADS):
                cs = slice(h * HEAD, (h + 1) * HEAD)
                g_ref[:, cs] = _f_out_gate(o_ref[:, cs], gate_ref[:, cs], refs[2][...])[0].astype(g_ref.dtype)
        h_ref[...] = r_ref[...] + _dot(g_ref[...], w_ref[...], ((1,), (0,)))

    rows = lambda width: pl.BlockSpec((tm, width), lambda i: (i, 0))
    whole = lambda a: pl.BlockSpec(a.shape, lambda i: (0, 0))
    params = [] if gain is None else [gain]
    return pl.pallas_call(
        body, grid=(t // tm,), in_specs=[rows(kw), rows(kw)] + [whole(p) for p in params] + [whole(w), rows(d)], out_specs=[rows(d), rows(kw)],
        out_shape=[jax.ShapeDtypeStruct((t, d), F32), jax.ShapeDtypeStruct((t, kw), _MXU_DTYPE)],
        compiler_params=_cparams(("parallel",)), name=name)(o, gate, *params, w, res)


def embed_norm(x, meta, gain, lp, gather=()):
    nb, seq, d = x.shape
    nblk, nx = lp // LEAD, len(gather)

    def body(*refs):
        x_ref, meta_ref, g_ref = refs[:3]
        h_ref, hn_ref = refs[3 + nx:5 + nx]
        b, i = pl.program_id(0), pl.program_id(1)
        finish = _ride(gather, False, refs[3:3 + nx], refs[5 + nx:5 + 2 * nx], refs[5 + 2 * nx:], (b == 0) & (i == 0), (b == nb - 1) & (i == nblk - 1),
                       two_level=True)

        @pl.when(i == 0)
        def _():
            h_ref[:PAD_ROWS, :] = jnp.zeros((PAD_ROWS, d), F32)
            h_ref[PAD_ROWS:, :] = meta_ref[...]

        @pl.when(i > 0)
        def _():
            h_ref[...] = x_ref[...]

        hn_ref[...] = _rms(h_ref[...], g_ref[...]).astype(hn_ref.dtype)
        finish()

    rows = pl.BlockSpec((LEAD, d), lambda b, i: (b * nblk + i, 0))
    out = pl.pallas_call(
        body, grid=(nb, nblk),
        in_specs=[pl.BlockSpec((None, LEAD, d), lambda b, i: (b, jnp.maximum(i - 1, 0), 0)), pl.BlockSpec((N_META, d), lambda b, i: (0, 0)),
                  pl.BlockSpec((1, d), lambda b, i: (0, 0))] + [_HBM] * nx,
        out_specs=[rows, rows] + [_HBM] * nx,
        out_shape=[jax.ShapeDtypeStruct((nb * lp, d), F32), jax.ShapeDtypeStruct((nb * lp, d), _MXU_DTYPE)] + Exchange.out_shape(gather, False),
        scratch_shapes=Exchange.scratch(nx) if nx else [],
        compiler_params=_cparams(("arbitrary", "arbitrary")), name="embed_norm")(x, meta, gain, *gather)
    return list(out)


def meta_grad(dh0):
    nb, _, d = dh0.shape

    def body(g_ref, o_ref):
        @pl.when(pl.program_id(0) == 0)
        def _():
            o_ref[...] = jnp.zeros_like(o_ref)

        o_ref[...] += g_ref[PAD_ROWS:LEAD, :]

    return pl.pallas_call(
        body, grid=(nb,), in_specs=[pl.BlockSpec((None, LEAD, d), lambda b: (b, 0, 0))],
        out_specs=pl.BlockSpec((N_META, d), lambda b: (0, 0)), out_shape=jax.ShapeDtypeStruct((N_META, d), F32),
        compiler_params=_cparams(("arbitrary",)), name="meta_grad")(dh0)


_HBM = pl.BlockSpec(memory_space=pltpu.HBM)


def _mesh_pos():
    x, y, c = lax.axis_index("x"), lax.axis_index("y"), lax.axis_index("c")
    return x, y, c


def _peer(x, y, c, k):
    px = 1 - x if k & 4 else x
    py = 1 - y if k & 2 else y
    pc = 1 - c if k & 1 else c
    return (px, py, pc), 4 * px + 2 * py + pc


class Exchange:
    def __init__(self, x_refs, out_refs, send_sems, recv_sems, local_sems, scatter):
        self.x_refs, self.out_refs, self.scatter = x_refs, out_refs, scatter
        self.send_sems, self.recv_sems, self.local_sems = send_sems, recv_sems, local_sems
        self.pos = _mesh_pos()
        x, y, c = self.pos
        self.me = 4 * x + 2 * y + c

    @staticmethod
    def scratch(n):
        return [pltpu.SemaphoreType.DMA((n, N_DEV - 1)), pltpu.SemaphoreType.DMA((n, N_DEV - 1)), pltpu.SemaphoreType.DMA((n,))]

    @staticmethod
    def out_shape(bufs, scatter):
        return [jax.ShapeDtypeStruct(b.shape if scatter else (N_DEV,) + b.shape, b.dtype) for b in bufs]

    def _local(self, i):
        return pltpu.make_async_copy(self.x_refs[i].at[self.me] if self.scatter else self.x_refs[i], self.out_refs[i].at[self.me], self.local_sems.at[i])

    def _copy(self, i, k, landing):
        peer, peer_id = _peer(*self.pos, k)
        src = self.x_refs[i].at[peer_id] if self.scatter else self.x_refs[i]
        return pltpu.make_async_remote_copy(src_ref=src, dst_ref=self.out_refs[i].at[peer_id if landing else self.me],
                                            send_sem=self.send_sems.at[i, k - 1], recv_sem=self.recv_sems.at[i, k - 1],
                                            device_id=peer, device_id_type=pl.DeviceIdType.MESH)

    def start(self):
        for i in range(len(self.x_refs)):
            self._local(i).start()
        for k in range(1, N_DEV):
            for i in range(len(self.x_refs)):
                self._copy(i, k, False).start()

    def wait(self):
        for k in range(1, N_DEV):
            for i in range(len(self.x_refs)):
                self._copy(i, k, True).wait_recv()
        for k in range(1, N_DEV):
            for i in range(len(self.x_refs)):
                self._copy(i, k, False).wait_send()
        for i in range(len(self.x_refs)):
            self._local(i).wait()


class TwoLevelGather(Exchange):
    DIRECT = (1, 4, 2, 6)
    FROM_CHIPS = (4, 2, 6)

    def _forward(self, i, k):
        _, origin = _peer(*self.pos, k)
        sibling, _ = _peer(*self.pos, 1)
        block = self.out_refs[i].at[origin]
        return pltpu.make_async_remote_copy(src_ref=block, dst_ref=block, send_sem=self.send_sems.at[i, (k ^ 1) - 1],
                                            recv_sem=self.recv_sems.at[i, (k ^ 1) - 1], device_id=sibling, device_id_type=pl.DeviceIdType.MESH)

    def start(self):
        assert not self.scatter
        for i in range(len(self.x_refs)):
            self._local(i).start()
        for k in self.DIRECT:
            for i in range(len(self.x_refs)):
                self._copy(i, k, False).start()

    def wait(self):
        n = range(len(self.x_refs))
        for k in self.FROM_CHIPS:
            for i in n:
                self._copy(i, k, True).wait_recv()
                self._forward(i, k).start()
        for k in (1, 5, 3, 7):
            for i in n:
                self._copy(i, k, True).wait_recv()
        for k in self.DIRECT:
            for i in n:
                self._copy(i, k, False).wait_send()
        for k in self.FROM_CHIPS:
            for i in n:
                self._forward(i, k).wait_send()
        for i in n:
            self._local(i).wait()


def _exchange(name, bufs, scatter):
    n = len(bufs)

    def body(*refs):
        ex = Exchange(refs[:n], refs[n:2 * n], *refs[2 * n:], scatter)
        ex.start()
        ex.wait()

    return pl.pallas_call(body, in_specs=[_HBM] * n, out_specs=[_HBM] * n, out_shape=Exchange.out_shape(bufs, scatter),
                          scratch_shapes=Exchange.scratch(n), name=name)(*bufs)


def _f_rms(x, g):
    return (_rms(x, g),)


def _f_rms2(x, g1, g2):
    r = x * lax.rsqrt(jnp.sum(x * x, -1, keepdims=True) / x.shape[-1] + EPS)
    return r * g1, r * g2


@jax.custom_vjp
def _out_gate(o, gate, gain):
    return _rms(o, gain) * _silu(gate)


def _out_gate_bwd(res, g):
    o, gate, gain = res
    r = lax.rsqrt(jnp.sum(o * o, -1, keepdims=True) / o.shape[-1] + EPS)
    n = o * r
    s = _sigmoid(gate)
    g_norm = g * (gate * s)
    d_gate = g * (n * gain) * (s * (1.0 + gate * (1.0 - s)))
    gn = g_norm * gain
    d_o = r * (gn - n * (jnp.sum(gn * n, -1, keepdims=True) / o.shape[-1]))
    return d_o, d_gate, jnp.sum(g_norm * n, 0, keepdims=True)


_out_gate.defvjp(lambda o, gate, gain: (_out_gate(o, gate, gain), (o, gate, gain)), _out_gate_bwd)


def _f_out_gate(o, gate, gain):
    return (_out_gate(o, gate, gain),)


def _f_gate(o, gate):
    return (o * _silu(gate),)


def _swap_rope_halves(x):
    return pltpu.roll(x, ROPE // 2, 1) + pltpu.roll(x, HEAD - ROPE // 2, 1)


def _qk_final_inv_rms(nope, rope_in):
    ms = (jnp.sum(nope * nope, -1, keepdims=True) + jnp.sum(rope_in * rope_in, -1, keepdims=True)) / QK_DIM
    return lax.rsqrt(ms + EPS)


@functools.partial(jax.custom_vjp, nondiff_argnums=(0,))
def _qk_final(scale, nope, rope_in, g_nope, g_rope, cos, sin):
    r = _qk_final_inv_rms(nope, rope_in)
    b = rope_in * (r * g_rope)
    out = jnp.concatenate([nope * (r * g_nope), b * cos + _swap_rope_halves(b) * sin], axis=1)
    return out if scale == 1.0 else out * scale


def _qk_final_fwd(scale, nope, rope_in, g_nope, g_rope, cos, sin):
    return _qk_final(scale, nope, rope_in, g_nope, g_rope, cos, sin), (nope, rope_in, g_nope, g_rope, cos, sin)


def _qk_final_bwd(scale, res, g):
    nope, rope_in, g_nope, g_rope, cos, sin = res
    r = _qk_final_inv_rms(nope, rope_in)
    ga, gb = g[:, :HEAD], g[:, HEAD:]
    if scale != 1.0:
        ga, gb = ga * scale, gb * scale
    db = gb * cos + _swap_rope_halves(gb * sin)
    t_a, t_b = ga * nope, db * rope_in
    d_r = jnp.sum(t_a * g_nope + t_b * g_rope, -1, keepdims=True)
    c = d_r * (r * r * r) * (-1.0 / QK_DIM)
    d_nope = ga * (r * g_nope) + nope * c
    d_rope = db * (r * g_rope) + rope_in * c
    d_g_nope = jnp.sum(t_a * r, 0, keepdims=True)
    d_g_rope = jnp.sum(t_b * r, 0, keepdims=True)
    return d_nope, d_rope, d_g_nope, d_g_rope, jnp.zeros_like(cos), jnp.zeros_like(sin)


_qk_final.defvjp(_qk_final_fwd, _qk_final_bwd)


def _f_qk_final(scale, nope, rope_in, g_nope, g_rope, cos, sin):
    return (_qk_final(scale, nope, rope_in, g_nope, g_rope, cos, sin),)


def _rope_tables(lp):
    half = ROPE // 2
    pos = jnp.maximum(jnp.arange(lp) - PAD_ROWS, 0)
    inv = ROPE_THETA ** (-jnp.arange(half, dtype=F32) / half)
    ang = pos.astype(F32)[:, None] * inv[None, :]
    zeros = jnp.zeros((lp, HEAD - ROPE), F32)
    cos = jnp.concatenate([jnp.cos(ang), jnp.cos(ang), zeros], 1)
    sin = jnp.concatenate([-jnp.sin(ang), jnp.sin(ang), zeros], 1)
    return cos, sin


def _pad_lanes(w, width=HEAD):
    return jnp.pad(w, ((0, 0), (0, width - w.shape[1])))


def _pad_rows(w, rows=HEAD):
    return jnp.pad(w, ((0, rows - w.shape[0]), (0, 0)))


def _split_heads_qk_t(w_t):
    k = w_t.shape[1]
    return jnp.pad(w_t.reshape(N_HEADS, QK_DIM, k), ((0, 0), (0, QK_PAD - QK_DIM), (0, 0))).reshape(N_HEADS * QK_PAD, k)


def _merge_heads_qk_t(g_t):
    k = g_t.shape[1]
    return g_t.reshape(N_HEADS, QK_PAD, k)[:, :QK_DIM].reshape(N_HEADS * QK_DIM, k)


@functools.partial(jax.custom_vjp, nondiff_argnums=(0,))
def _q_final(scale, qh, g_nope, g_rope, cos, sin):
    return _qk_final(scale, qh[:, :HEAD], qh[:, HEAD:], g_nope, g_rope, cos, sin)


def _q_final_bwd(scale, res, g):
    qh, g_nope, g_rope, cos, sin = res
    grads = _qk_final_bwd(scale, (qh[:, :HEAD], qh[:, HEAD:], g_nope, g_rope, cos, sin), g)
    return (jnp.concatenate(grads[:2], axis=1),) + tuple(grads[2:])


_q_final.defvjp(lambda scale, qh, *rest: (_q_final(scale, qh, *rest), (qh,) + rest), _q_final_bwd)


def _f_q_final(scale, qh, g_nope, g_rope, cos, sin):
    return (_q_final(scale, qh, g_nope, g_rope, cos, sin),)


def local_step(x, target, w, deferred=None):
    nb, seq, d = x.shape
    lp = seq + LEAD
    t = nb * lp
    tr = _pick(lp, (544, 128))
    ntab = lp // tr
    mxu = _MXU_DTYPE
    kw = N_HEADS * HEAD

    a_conv = w["a_conv"].T
    alog, dtb, o_gain = _pad_lanes(w["a_log"]), _pad_lanes(w["a_dt_bias"]), w["a_o_gain"]
    a_norm, kv_norm, b_norm = w["a_norm"], w["kv_norm"][None, :], w["b_norm"]
    lat_norm, qlat_norm = w["kv_latent_norm"][None, :], w["b_q_latent_norm"]
    kg_nope, kg_rope = w["k_gain"][None, :HEAD], _pad_lanes(w["k_gain"][None, HEAD:])
    qg_nope, qg_rope = w["b_q_gain"][:, :HEAD], _pad_lanes(w["b_q_gain"][:, HEAD:])
    cos, sin = _rope_tables(lp)

    h0, hn, *gathered = embed_norm(x, w["meta_tokens"].T, a_norm, lp, gather=deferred.first_gather_bufs if deferred else ())
    if deferred:
        w = {**w, **deferred.finish_first(gathered)}
    a_w_in_t = w["a_w_in"].astype(mxu)
    w_qkv_t, w_gba_t = a_w_in_t[:3 * kw], _pad_rows(a_w_in_t[3 * kw:], kw + HEAD)
    z_qkv = matmul("a_in_qkv", hn, w_qkv_t, "nt", out_dtype=mxu)
    z_gba = matmul("a_in_gate_ba", hn, w_gba_t, "nt")
    ba_block = kw // HEAD
    qkv_a, y_conv = conv_fwd(z_qkv, a_conv, lp)
    o_a, states, t_invs, *gathered = delta_fwd(qkv_a, z_gba, ba_block, alog, dtb, lp, gather=deferred.gather_bufs if deferred else ())
    if deferred:
        w = {**w, **deferred.finish(gathered)}
    a_w_out = w["a_w_out"].astype(mxu)
    w_down = _pad_lanes(w["kv_w_down"], KV_RANK + HEAD).astype(mxu)
    w_ukv_t = jnp.concatenate([w["kv_w_uk"], w["kv_w_uv"]], 0).astype(mxu)
    b_w_in_t = w["b_w_in"].astype(mxu)
    w_cq_t, w_gb_t = b_w_in_t[:Q_RANK], b_w_in_t[Q_RANK:]
    w_q_t = _split_heads_qk_t(w["b_w_uq"]).astype(mxu)
    b_w_out = w["b_w_out"].astype(mxu)
    og_args = [Arg(o_a, bc=HEAD, ph=True, diff=True), Arg(z_gba, bc=HEAD, ph=True, diff=True, gdt=mxu), Arg(o_gain, "par", diff=True)]
    h1, og_a = gated_out("a_out", o_a, z_gba, o_gain, a_w_out, h0)

    hk, hb = row_call("b_norms_fwd", _f_rms2, [Arg(h1), Arg(kv_norm, "par"), Arg(b_norm, "par")], [(d, mxu, d, False), (d, mxu, d, False)], tr)
    c_down = matmul("kv_down", hk, w_down, "nn")
    c_kv_arg = Arg(c_down, bc=KV_RANK, diff=True, gdt=mxu)
    k_pe_arg = Arg(c_down, bc=HEAD, base=KV_RANK // HEAD, diff=True)
    c_q_raw = matmul("b_in_q", hb, w_cq_t, "nt")
    gate_b = matmul("b_in_gate", hb, w_gb_t, "nt")
    (c_kv,) = row_call("kv_latent_fwd", _f_rms, [c_kv_arg, Arg(lat_norm, "par")], [(KV_RANK, mxu, KV_RANK, False)], tr)
    (c_q,) = row_call("q_latent_fwd", _f_rms, [Arg(c_q_raw), Arg(qlat_norm, "par")], [(Q_RANK, mxu, Q_RANK, False)], tr)
    k_nope = matmul("k_up", c_kv, w_ukv_t[:kw], "nt")
    v_b = matmul("v_up", c_kv, w_ukv_t[kw:], "nt", out_dtype=mxu)
    q_up = matmul("q_up", c_q, w_q_t, "nt")
    tabs = [Arg(cos, "tab"), Arg(sin, "tab")]
    k_args = [Arg(k_nope, bc=HEAD, ph=True, diff=True, gdt=mxu), k_pe_arg, Arg(kg_nope, "par", diff=True), Arg(kg_rope, "par", diff=True)] + tabs
    q_args = [Arg(q_up, bc=QK_PAD, ph=True, diff=True, gdt=mxu), Arg(qg_nope, "par", diff=True), Arg(qg_rope, "par", diff=True)] + tabs
    f_k_final, f_q_final = functools.partial(_f_qk_final, 1.0), functools.partial(_f_q_final, ATT_SCALE)
    (k_fin,) = row_call("k_final_fwd", f_k_final, k_args, [(N_HEADS * QK_PAD, mxu, QK_PAD, True)], tr, nh=N_HEADS, ntab=ntab)
    (q_fin,) = row_call("q_final_fwd", f_q_final, q_args, [(N_HEADS * QK_PAD, mxu, QK_PAD, True)], tr, nh=N_HEADS, ntab=ntab)
    o_b, lse = flash_fwd(q_fin, k_fin, v_b, lp)
    gb_args = [Arg(o_b, diff=True), Arg(gate_b, diff=True, gdt=mxu)]
    h2, og_b = gated_out("b_out", o_b, gate_b, None, b_w_out, h1)

    loss, dh2 = loss_head(h2.reshape(nb, lp, d), target, lp)
    dh2 = dh2.reshape(t, d)
    grads = {}

    d_og_b = matmul("b_out_dx", dh2, b_w_out, "nt", out_dtype=mxu)
    grads["b_w_out"] = matmul("b_out_dw", og_b, dh2, "tn")
    d_o_b, d_gate_b = row_vjp_call("b_gate_bwd", _f_gate, gb_args, [Arg(d_og_b)], tr)
    dq_fin, dk_fin, dv_b = flash_bwd(q_fin, k_fin, v_b, o_b, lse, d_o_b, lp)
    dq_up, d_qg_nope, d_qg_rope = row_vjp_call(
        "q_final_bwd", f_q_final, q_args, [Arg(dq_fin, bc=QK_PAD, ph=True)], tr, nh=N_HEADS, ntab=ntab)
    dk_nope, dk_pe, d_kg_nope, d_kg_rope = row_vjp_call(
        "k_final_bwd", f_k_final, k_args, [Arg(dk_fin, bc=QK_PAD, ph=True)], tr, nh=N_HEADS, ntab=ntab)
    grads["b_q_gain"] = jnp.concatenate([d_qg_nope, d_qg_rope[:, :ROPE]], 1)
    grads["k_gain"] = jnp.concatenate([d_kg_nope, d_kg_rope[:, :ROPE]], 1)[0]
    d_c_q = matmul("q_up_dx", dq_up, w_q_t, "nn")
    grads["b_w_uq"] = _merge_heads_qk_t(matmul("q_up_dw", dq_up, c_q, "tn"))
    d_c_kv = matmul("k_up_dx", dk_nope, w_ukv_t[:kw], "nn")
    d_c_kv = matmul("v_up_dx", dv_b, w_ukv_t[kw:], "nn", res=d_c_kv)
    grads["kv_w_uk"], grads["kv_w_uv"] = matmul("k_up_dw", dk_nope, c_kv, "tn"), matmul("v_up_dw", dv_b, c_kv, "tn")
    d_c_q_raw, grads["b_q_latent_norm"] = row_vjp_call(
        "q_latent_bwd", _f_rms, [Arg(c_q_raw, diff=True, gdt=mxu), Arg(qlat_norm, "par", diff=True)], [Arg(d_c_q)], tr)
    d_c_kv_raw, d_lat = row_vjp_call(
        "kv_latent_bwd", _f_rms, [c_kv_arg, Arg(lat_norm, "par", diff=True)], [Arg(d_c_kv)], tr)
    grads["kv_latent_norm"] = d_lat[0]
    d_hb = matmul("b_in_q_dx", d_c_q_raw, w_cq_t, "nn")
    d_hb = matmul("b_in_gate_dx", d_gate_b, w_gb_t, "nn", res=d_hb, out_dtype=mxu)
    grads["b_w_in"] = jnp.concatenate([matmul("b_in_q_dw", d_c_q_raw, hb, "tn"), matmul("b_in_gate_dw", d_gate_b, hb, "tn")], 0)
    d_c_down = jnp.concatenate([d_c_kv_raw, dk_pe.astype(mxu)], 1)
    d_hk = matmul("kv_down_dx", d_c_down, w_down, "nt", out_dtype=mxu)
    grads["kv_w_down"] = matmul("kv_down_dw", hk, d_c_down, "tn")[:, :KV_RANK + ROPE]
    dh1, d_kv_norm, grads["b_norm"] = row_vjp_call(
        "b_norms_bwd", lambda x_, g1, g2: _f_rms2(x_, g1, g2) + (x_,),
        [Arg(h1, diff=True), Arg(kv_norm, "par", diff=True), Arg(b_norm, "par", diff=True)], [Arg(d_hk), Arg(d_hb), Arg(dh2)], tr)
    grads["kv_norm"] = d_kv_norm[0]

    d_og_a = matmul("a_out_dx", dh1, a_w_out, "nt", out_dtype=mxu)
    grads["a_w_out"] = matmul("a_out_dw", og_a, dh1, "tn")
    d_o_a, d_gate_a, grads["a_o_gain"] = row_vjp_call(
        "a_out_gate_bwd", _f_out_gate, og_args, [Arg(d_og_a, bc=HEAD, ph=True)], tr, nh=N_HEADS)
    dqkv_a, d_ba, d_alog, d_dtb, *received = delta_bwd(qkv_a, z_gba, ba_block, alog, dtb, states, t_invs, d_o_a, lp,
                                                        scatter=deferred.scatter_bufs(grads) if deferred else ())
    grads["a_log"], grads["a_dt_bias"] = d_alog[:, :N_HEADS], d_dtb[:, :N_HEADS]
    dz_qkv, d_conv = conv_bwd(z_qkv, y_conv, a_conv, dqkv_a, lp)
    grads["a_conv"] = d_conv.T
    dz_gba = jnp.concatenate([d_gate_a, d_ba.astype(mxu)], 1)
    grads["a_w_in"] = jnp.concatenate([matmul("a_in_qkv_dw", dz_qkv, hn, "tn"), matmul("a_in_gate_ba_dw", dz_gba, hn, "tn")[:kw + 2 * N_HEADS]], 0)
    ride = deferred.last_scatter_bufs(grads) if deferred else ((), ())
    d_hn = matmul("a_in_qkv_dx", dz_qkv, w_qkv_t, "nn", scatter=ride[0])
    if ride[0]:
        d_hn, *received_half = d_hn
        received = list(received) + received_half
    d_hn = matmul("a_in_gate_ba_dx", dz_gba, w_gba_t, "nn", res=d_hn, out_dtype=mxu, scatter=ride[1])
    if ride[1]:
        d_hn, *received_half = d_hn
        received = list(received) + received_half
    dh0, grads["a_norm"] = row_vjp_call("a_norm_bwd", lambda x_, g_: _f_rms(x_, g_) + (x_,),
                                        [Arg(h0, diff=True), Arg(a_norm, "par", diff=True)], [Arg(d_hn), Arg(dh1)], tr)
    dh0 = dh0.reshape(nb, lp, d)
    grads["meta_tokens"] = meta_grad(dh0).T
    return loss, dh0[:, LEAD:], grads, received


_SHARDED = (
    ("meta_tokens", True, False), ("a_norm", True, False), ("a_w_in", True, True), ("a_conv", True, False), ("a_w_out", False, True),
    ("kv_w_down", False, True), ("kv_w_uk", True, True), ("kv_w_uv", True, True), ("b_w_in", True, True), ("b_w_uq", True, True),
    ("b_w_out", False, True))
_REPLICATED = ("a_log", "a_dt_bias", "a_o_gain", "kv_norm", "kv_latent_norm", "k_gain", "b_norm", "b_q_latent_norm", "b_q_gain")
_ALL_WEIGHTS = ("meta_tokens", "a_norm", "a_w_in", "a_conv", "a_log", "a_dt_bias", "a_o_gain", "a_w_out", "kv_norm", "kv_w_down",
                "kv_latent_norm", "kv_w_uk", "kv_w_uv", "k_gain", "b_norm", "b_w_in", "b_q_latent_norm", "b_w_uq", "b_q_gain", "b_w_out")


def _round_up(n, m):
    return (n + m - 1) // m * m


def _pack_rows(pieces, row_multiple):
    padded = []
    for p in pieces:
        n = p.shape[-1]
        padded.append(jnp.pad(p, [(0, 0)] * (p.ndim - 1) + [(0, _round_up(n, PACK_COLS) - n)]))
    flat = jnp.concatenate(padded, -1)
    rows = _round_up(flat.shape[-1] // PACK_COLS, row_multiple)
    flat = jnp.pad(flat, [(0, 0)] * (flat.ndim - 1) + [(0, rows * PACK_COLS - flat.shape[-1])])
    return flat.reshape(flat.shape[:-1] + (rows, PACK_COLS))


def _unpack_rows(buf, sizes):
    flat = buf.reshape(buf.shape[:-2] + (-1,))
    out, off = [], 0
    for n in sizes:
        out.append(flat[..., off:off + n])
        off += _round_up(n, PACK_COLS)
    return out


def _shard_2d(a):
    return a.reshape(a.shape[-2:]) if a.ndim > 2 else a


def _kl_shard(a, by_cols):
    return _shard_2d(a).T if by_cols else _shard_2d(a)


_GROUPS_FIRST = (("a_w_in",),)
_GROUPS_LATER = (("a_w_out", "b_w_in", "b_w_out"), ("b_w_uq",), ("kv_w_down",), ("kv_w_uk", "kv_w_uv"))
_SMALL_SHARDED = ("meta_tokens", "a_norm", "a_conv")
_BY_COLS = {name: by_cols for name, by_cols, _ in _SHARDED}
ROW_ALIGN = 16


def _stack_rows(pieces):
    padded, starts, row = [], [], 0
    for p in pieces:
        r = p.shape[-2]
        padded.append(jnp.pad(p, [(0, 0)] * (p.ndim - 2) + [(0, _round_up(r, ROW_ALIGN) - r), (0, 0)]))
        starts.append(row)
        row += _round_up(r, ROW_ALIGN)
    return jnp.concatenate(padded, -2), starts


def _stack_group(arrays_by_name, names):
    arrays = [arrays_by_name[n].astype(BF16) for n in names]
    buf, starts = _stack_rows(arrays)
    return buf, [(n, s, a.shape[-2]) for n, s, a in zip(names, starts, arrays, strict=True)]


def _stack_groups(arrays_by_name, groups):
    stacked = [_stack_group(arrays_by_name, names) for names in groups]
    return [b for b, _ in stacked], [entries for _, entries in stacked]


def _full_from_gathered(gathered, layout):
    full = {}
    for got, entries in zip(gathered, layout, strict=True):
        for name, start, rows in entries:
            full[name] = got[:, start:start + rows].reshape(N_DEV * rows, got.shape[-1])
    return full


def gather_small_weights(local):
    small = [_kl_shard(local[n], _BY_COLS[n]) for n in _SMALL_SHARDED]
    (gathered,) = _exchange("all_gather", [_pack_rows([s.reshape(-1) for s in small], 8)], scatter=False)
    full = {}
    for name, part, sh in zip(_SMALL_SHARDED, _unpack_rows(gathered, [s.size for s in small]), small, strict=True):
        full[name] = part.reshape(N_DEV * sh.shape[0], sh.shape[1])
    full["a_norm"] = full["a_norm"].reshape(1, -1)
    return full


class LaterExchanges:
    def __init__(self, local):
        shards = {n: _kl_shard(local[n], _BY_COLS[n]) for names in _GROUPS_FIRST + _GROUPS_LATER for n in names}
        self.first_gather_bufs, self.first_layout = _stack_groups(shards, _GROUPS_FIRST)
        self.gather_bufs, self.layout = _stack_groups(shards, _GROUPS_LATER)

    def finish_first(self, gathered):
        return _full_from_gathered(gathered, self.first_layout)

    def finish(self, gathered):
        return _full_from_gathered(gathered, self.layout)

    def scatter_bufs(self, grads):
        return _stack_groups(_owner_slices(grads, _GROUPS_LATER), _GROUPS_LATER)[0]

    def last_scatter_bufs(self, grads):
        (buf,), self.last_layout = _stack_groups(_owner_slices(grads, _GROUPS_FIRST), _GROUPS_FIRST)
        half = buf.shape[-1] // 2
        return [buf[..., :half]], [buf[..., half:]]


def _owner_slices(grads, groups):
    return {n: grads[n].reshape(N_DEV, -1, grads[n].shape[-1]) for names in groups for n in names}


def reduce_contributions(name, recv):
    _, r, c = recv.shape
    tr = max(d for d in range(8, 513, 8) if r % d == 0 and (d % ROW_ALIGN == 0 or recv.dtype == F32))

    def body(g_ref, o_ref):
        g = g_ref[0].astype(F32)
        for dev in range(1, N_DEV):
            g = g + g_ref[dev].astype(F32)
        o_ref[...] = g

    return pl.pallas_call(
        body, grid=(r // tr,), in_specs=[pl.BlockSpec((N_DEV, tr, c), lambda i: (0, i, 0))], out_specs=pl.BlockSpec((tr, c), lambda i: (i, 0)),
        out_shape=jax.ShapeDtypeStruct((r, c), F32), compiler_params=_cparams(("arbitrary",)), name=name)(recv)


def adamw_all(gs, ws, ms, vs):
    n = len(gs)

    def body(*refs):
        for i in range(n):
            g_ref, w_ref, m_ref, v_ref = (refs[j * n + i] for j in range(4))
            d_ref, mo_ref, vo_ref = (refs[(4 + j) * n + i] for j in range(3))
            g = g_ref[...]
            m_new = ADAM_B1 * m_ref[...] + (1.0 - ADAM_B1) * g
            v_new = ADAM_B2 * v_ref[...] + (1.0 - ADAM_B2) * (g * g)
            m_hat = m_new / (1.0 - ADAM_B1 ** ADAM_STEP)
            v_hat = v_new / (1.0 - ADAM_B2 ** ADAM_STEP)
            d_ref[...] = -ADAM_LR * (m_hat / (jnp.sqrt(v_hat) + ADAM_EPS) + ADAM_WD * w_ref[...])
            mo_ref[...] = m_new
            vo_ref[...] = v_new

    out = [jax.ShapeDtypeStruct(g.shape, F32) for g in gs] * 3
    res = pl.pallas_call(body, out_shape=out, compiler_params=pltpu.CompilerParams(vmem_limit_bytes=VMEM_LIMIT), name="adamw_all")(*gs, *ws, *ms, *vs)
    return res[:n], res[n:2 * n], res[2 * n:]


def kernel(x, meta_tokens, a_norm, a_w_in, a_conv, a_log, a_dt_bias, a_o_gain, a_w_out, kv_norm, kv_w_down, kv_latent_norm, kv_w_uk, kv_w_uv, k_gain, b_norm, b_w_in, b_q_latent_norm, b_w_uq, b_q_gain, b_w_out, loss_target, m_meta_tokens, m_a_norm, m_a_w_in, m_a_conv, m_a_log, m_a_dt_bias, m_a_o_gain, m_a_w_out, m_kv_norm, m_kv_w_down, m_kv_latent_norm, m_kv_w_uk, m_kv_w_uv, m_k_gain, m_b_norm, m_b_w_in, m_b_q_latent_norm, m_b_w_uq, m_b_q_gain, m_b_w_out, v_meta_tokens, v_a_norm, v_a_w_in, v_a_conv, v_a_log, v_a_dt_bias, v_a_o_gain, v_a_w_out, v_kv_norm, v_kv_w_down, v_kv_latent_norm, v_kv_w_uk, v_kv_w_uv, v_k_gain, v_b_norm, v_b_w_in, v_b_q_latent_norm, v_b_w_uq, v_b_q_gain, v_b_w_out):
    given = dict(locals())
    local_w = {n: given[n] for n in _ALL_WEIGHTS}
    full = gather_small_weights(local_w)
    for n in _REPLICATED:
        full[n] = local_w[n]
    later = LaterExchanges(local_w)

    loss_part, grad_x, grads, received_riding = local_step(x, loss_target, full, later)

    exact = [grads[n].reshape(N_DEV, -1) for n in _SMALL_SHARDED]
    exact += [jnp.broadcast_to(grads[n].reshape(1, -1), (N_DEV, grads[n].size)) for n in _REPLICATED]
    exact.append(jnp.broadcast_to(loss_part, (N_DEV, 1)))
    received = list(received_riding) + list(_exchange("all_to_all", [_pack_rows(exact, 8)], scatter=True))
    layout = later.layout + later.last_layout
    summed = [reduce_contributions(f"reduce_{i}", r) for i, r in enumerate(received)]
    n_later = len(later.layout)
    summed = summed[:n_later] + [jnp.concatenate(summed[n_later:n_later + 2], 1)] + summed[n_later + 2:]

    grad_kl = {}
    for got, entries in zip(summed, layout):
        for n, start, rows in entries:
            grad_kl[n] = got[start:start + rows]
    parts = _unpack_rows(summed[-1], [p.shape[1] for p in exact])
    for n, part in zip(_SMALL_SHARDED + _REPLICATED, parts, strict=False):
        grad_kl[n] = part
    loss = parts[-1][0]

    def natural_2d(n, a):
        shape = _shard_2d(local_w[n]).shape if local_w[n].ndim > 1 else (1, local_w[n].size)
        return a.reshape(shape[::-1]).T if _BY_COLS.get(n, False) else a.reshape(shape)

    as_2d = lambda n, a: a.reshape(natural_2d(n, grad_kl[n]).shape)
    gs = [natural_2d(n, grad_kl[n]) for n in _ALL_WEIGHTS]
    deltas, new_m, new_v = adamw_all(gs, [as_2d(n, local_w[n]) for n in _ALL_WEIGHTS], [as_2d(n, given["m_" + n]) for n in _ALL_WEIGHTS],
                                     [as_2d(n, given["v_" + n]) for n in _ALL_WEIGHTS])
    results = [a.reshape(local_w[n].shape) for group in (gs, deltas, new_m, new_v) for n, a in zip(_ALL_WEIGHTS, group, strict=True)]
    return (loss, grad_x, *results)
```

```python
import dataclasses
import functools
import math

import jax
import jax.numpy as jnp
from jax import lax
from jax.experimental import pallas as pl
from jax.experimental.pallas import tpu as pltpu

F32 = jnp.float32
BF16 = jnp.bfloat16
_MXU_DTYPE = jnp.bfloat16

N_DEV = 8
D_MODEL = 1024
N_HEADS = 8
HEAD = 128
CHUNK = 64
N_META = 16
PAD_ROWS = 2 * CHUNK - N_META
LEAD = PAD_ROWS + N_META
ROPE = 64
QK_DIM = HEAD + ROPE
QK_PAD = 2 * HEAD
KV_RANK = 256
Q_RANK = 384
CONV_K = 4
EPS = 1e-6
NEG = -1e30
ROPE_THETA = 10000.0
ADAM_LR, ADAM_B1, ADAM_B2, ADAM_EPS, ADAM_WD, ADAM_STEP = 0.001, 0.9, 0.999, 1e-08, 0.01, 10
PACK_COLS = 512
VMEM_LIMIT = 56 * 1024 * 1024


def _pick(n, options):
    for o in options:
        if n % o == 0:
            return o
    raise ValueError(f"no tile for {n} among {options}")


def _cparams(sem):
    return pltpu.CompilerParams(dimension_semantics=sem, vmem_limit_bytes=VMEM_LIMIT)


def _dims(a, dims):
    if a.ndim == 2:
        return (dims, ((), ()))
    (ca,), (cb,) = dims
    return (((ca + 1,), (cb + 1,)), ((0,), (0,)))


def _dot(a, b, dims):
    return lax.dot_general(a.astype(_MXU_DTYPE), b.astype(_MXU_DTYPE), _dims(a, dims), preferred_element_type=F32)


@jax.custom_vjp
def mm_nn(a, b):
    return _dot(a, b, ((1,), (0,)))


@jax.custom_vjp
def mm_nt(a, b):
    return _dot(a, b, ((1,), (1,)))


@jax.custom_vjp
def mm_tn(a, b):
    return _dot(a, b, ((0,), (0,)))


mm_nn.defvjp(lambda a, b: (mm_nn(a, b), (a, b)), lambda r, g: (mm_nt(g, r[1]), mm_tn(r[0], g)))
mm_nt.defvjp(lambda a, b: (mm_nt(a, b), (a, b)), lambda r, g: (mm_nn(g, r[1]), mm_tn(g, r[0])))
mm_tn.defvjp(lambda a, b: (mm_tn(a, b), (a, b)), lambda r, g: (mm_nt(r[1], g), mm_nn(r[0], g)))


def _split_terms(x, n):
    terms, rest = [], x
    for _ in range(n):
        t = rest.astype(_MXU_DTYPE)
        terms.append(t)
        rest = rest - t.astype(F32)
    return terms


def _dot_01_raw(m, x, dims):
    m = m.astype(_MXU_DTYPE)
    return sum(lax.dot_general(m, t, _dims(m, dims), preferred_element_type=F32) for t in _split_terms(x, 3))


@jax.custom_vjp
def _dot_01(m, x):
    return _dot_01_raw(m, x, ((1,), (0,)))


_dot_01.defvjp(lambda m, x: (_dot_01(m, x), m), lambda m, g: (jnp.zeros_like(m), _dot_01_raw(m, g, ((0,), (0,)))))


def _inv_unit_lower(a):
    n = a.shape[-1]
    eye = (lax.broadcasted_iota(jnp.int32, (n, n), 0) == lax.broadcasted_iota(jnp.int32, (n, n), 1)).astype(F32)
    d = lambda u, w: lax.dot_general(u, w, _dims(u, ((1,), (0,))), preferred_element_type=F32)
    t = eye - a
    p = a.astype(_MXU_DTYPE)
    p = d(p, p)
    squarings = int(math.log2(n)) - 1
    for s in range(squarings):
        ph = p.astype(_MXU_DTYPE)
        t_hi, t_lo = _split_terms(t, 2)
        t = t + (d(t_hi, ph) + d(t_lo, ph))
        if s + 1 < squarings:
            p = d(ph, ph)
    return t


@jax.custom_vjp
def _inv_lookup(a, t):
    return t


def _inv_lookup_bwd(t, g):
    return -mm_tn(t, mm_nt(g, t)), jnp.zeros_like(t)


_inv_lookup.defvjp(lambda a, t: (t, t), _inv_lookup_bwd)


def _sigmoid(x):
    return 1.0 / (1.0 + jnp.exp(-x))


@jax.custom_vjp
def _silu(x):
    return x * _sigmoid(x)


def _silu_fwd(x):
    s = _sigmoid(x)
    return x * s, (x, s)


_silu.defvjp(_silu_fwd, lambda r, g: (g * (r[1] * (1.0 + r[0] * (1.0 - r[1]))),))


def _softplus(x):
    return jnp.where(x > 20.0, x, jnp.log(1.0 + jnp.exp(jnp.minimum(x, 20.0))))


def _rms(x, g, width=None):
    ms = jnp.sum(x * x, -1, keepdims=True) / (x.shape[-1] if width is None else width)
    return x * lax.rsqrt(ms + EPS) * g


MM_VMEM_BUDGET = 40 * 1024 * 1024


def _matmul_rows(name, a, b, mode, out_dtype, res, scatter):
    m, k = a.shape
    n = b.shape[1] if mode == "nn" else b.shape[0]
    dims = {"nn": ((1,), (0,)), "nt": ((1,), (1,))}[mode]
    out_bytes = jnp.dtype(out_dtype).itemsize
    n_in, nx = 2 + (res is not None), len(scatter)

    def vmem(tm):
        blocks = 2 * tm * k * a.dtype.itemsize + 2 * k * n * b.dtype.itemsize + 2 * tm * n * out_bytes + tm * n * 4
        return blocks + (2 * tm * n * res.dtype.itemsize if res is not None else 0)

    tm = next(c for c in (2176, 1088, 512, 256, 128, 64) if m % c == 0 and vmem(c) <= MM_VMEM_BUDGET)
    steps = m // tm

    def body(*refs):
        a_ref, b_ref, o_ref = refs[0], refs[1], refs[n_in + nx]
        i = pl.program_id(0)
        finish = _ride(scatter, True, refs[n_in:n_in + nx], refs[n_in + nx + 1:n_in + 2 * nx + 1], refs[n_in + 2 * nx + 1:], i == 0, i == steps - 1)
        out = _dot(a_ref[...], b_ref[...], dims)
        if res is not None:
            out = out + refs[2][...].astype(F32)
        o_ref[...] = out.astype(o_ref.dtype)
        finish()

    o_spec = pl.BlockSpec((tm, n), lambda i: (i, 0))
    in_specs = [pl.BlockSpec((tm, k), lambda i: (i, 0)), pl.BlockSpec(b.shape, lambda i: (0, 0))] + ([o_spec] if res is not None else [])
    args = (a, b) + ((res,) if res is not None else ())
    out = pl.pallas_call(
        body, grid=(steps,), in_specs=in_specs + [_HBM] * nx, out_specs=[o_spec] + [_HBM] * nx,
        out_shape=[jax.ShapeDtypeStruct((m, n), out_dtype)] + Exchange.out_shape(scatter, True), scratch_shapes=Exchange.scratch(nx) if nx else [],
        compiler_params=_cparams(("arbitrary",) if nx else ("parallel",)), name=name)(*args, *scatter)
    return out if nx else out[0]


def matmul(name, a, b, mode, out_dtype=None, res=None, scatter=()):
    if mode != "tn":
        return _matmul_rows(name, a, b, mode, out_dtype or F32, res, scatter)
    out_dtype = out_dtype or _MXU_DTYPE
    (k, m), (k2, n) = a.shape, b.shape
    assert k == k2 and res is None, (name, a.shape, b.shape, mode)
    tm = _pick(m, (m if m <= 1536 else 1024, 1024, 512, 384, 256, 128))
    tn = _pick(n, (1024, 512, 384, 256, 128))
    tk = _pick(k, (512, 256, 128))
    nk = k // tk
    dims = ((0,), (0,))

    def body(*refs):
        if res is None:
            a_ref, b_ref, o_ref, acc_ref = refs
        else:
            a_ref, b_ref, r_ref, o_ref, acc_ref = refs
        kk = pl.program_id(2)

        @pl.when(kk == 0)
        def _():
            acc_ref[...] = jnp.zeros_like(acc_ref)

        acc_ref[...] += _dot(a_ref[...], b_ref[...], dims)

        @pl.when(kk == nk - 1)
        def _():
            out = acc_ref[...]
            if res is not None:
                out = out + r_ref[...].astype(F32)
            o_ref[...] = out.astype(o_ref.dtype)

    a_spec = pl.BlockSpec((tk, tm), lambda i, j, kk: (kk, i)) if mode == "tn" else pl.BlockSpec((tm, tk), lambda i, j, kk: (i, kk))
    b_spec = pl.BlockSpec((tn, tk), lambda i, j, kk: (j, kk)) if mode == "nt" else pl.BlockSpec((tk, tn), lambda i, j, kk: (kk, j))
    o_spec = pl.BlockSpec((tm, tn), lambda i, j, kk: (i, j))
    in_specs = [a_spec, b_spec] + ([o_spec] if res is not None else [])
    args = (a, b) + ((res,) if res is not None else ())
    return pl.pallas_call(
        body, grid=(m // tm, n // tn, nk), in_specs=in_specs, out_specs=o_spec,
        out_shape=jax.ShapeDtypeStruct((m, n), out_dtype), scratch_shapes=[pltpu.VMEM((tm, tn), F32)],
        compiler_params=_cparams(("parallel", "parallel", "arbitrary")), name=name)(*args)


@dataclasses.dataclass
class Arg:
    arr: jax.Array
    kind: str = "row"
    bc: int = 0
    base: int = 0
    ph: bool = False
    diff: bool = False
    gdt: object = F32


def _arg_spec(a, tr, nh, ntab, base=None):
    bc = a.bc or a.arr.shape[1]
    base = a.base if base is None else base
    width = bc * nh if a.ph else bc
    col = base // nh if a.ph else base
    assert not a.ph or base % nh == 0
    if a.kind == "row":
        return pl.BlockSpec((tr, width), lambda i: (i, col))
    if a.kind == "tab":
        return pl.BlockSpec((tr, width), lambda i: (i % ntab, col))
    return pl.BlockSpec((a.arr.shape[0], width), lambda i: (0, col))


def _head_view(ref, a, h, rs):
    bc = a.bc or a.arr.shape[1]
    rows = slice(None) if a.kind == "par" else rs
    v = ref[rows, h * bc:(h + 1) * bc] if a.ph else ref[rows, :]
    return v.astype(F32) if jnp.issubdtype(v.dtype, jnp.floating) else v


def row_call(name, fn, args, outs, tr, nh=1, ntab=1):
    t = args[0].arr.shape[0]
    n_in = len(args)
    out_args = [Arg(None, "row", bc, 0, ph) for (_, _, bc, ph) in outs]
    assert all(a.ph or nh == 1 for a in out_args)
    rs = slice(None)

    def body(*refs):
        for h in range(nh):
            res = fn(*[_head_view(r, a, h, rs) for r, a in zip(refs[:n_in], args, strict=True)])
            for r, a, v in zip(refs[n_in:], out_args, res, strict=True):
                r[rs, h * a.bc:(h + 1) * a.bc] = v.astype(r.dtype)

    return pl.pallas_call(
        body, grid=(t // tr,), in_specs=[_arg_spec(a, tr, nh, ntab) for a in args], out_specs=[_arg_spec(a, tr, nh, ntab) for a in out_args],
        out_shape=[jax.ShapeDtypeStruct((t, cols), dt) for (cols, dt, _, _) in outs],
        compiler_params=_cparams(("arbitrary",)), name=name)(*[a.arr for a in args])


def row_vjp_call(name, fn, args, cts, tr, nh=1, ntab=1):
    t = args[0].arr.shape[0]
    n_in, n_ct = len(args), len(cts)
    diff_idx = [k for k, a in enumerate(args) if a.diff]
    def body(*refs):
        out_refs = refs[n_in + n_ct:]
        par_sum = {}
        for k, r in zip(diff_idx, out_refs, strict=True):
            if args[k].kind == "par":
                @pl.when(pl.program_id(0) == 0)
                def _(r=r):
                    r[...] = jnp.zeros_like(r)

        for rs in (slice(None),):
            row_sum = {}
            for h in range(nh):
                vals = [_head_view(r, a, h, rs) for r, a in zip(refs[:n_in], args, strict=True)]
                ct_vals = tuple(_head_view(r, a, h, rs) for r, a in zip(refs[n_in:n_in + n_ct], cts, strict=True))

                def f(*dv, vals=vals):
                    full = list(vals)
                    for k, v in zip(diff_idx, dv, strict=True):
                        full[k] = v
                    return tuple(fn(*full))

                _, vjp = jax.vjp(f, *[vals[k] for k in diff_idx])
                for j, (k, r, g) in enumerate(zip(diff_idx, out_refs, vjp(ct_vals), strict=True)):
                    a = args[k]
                    bc = a.bc or a.arr.shape[1]
                    if a.kind == "row" and a.ph:
                        r[rs, h * bc:(h + 1) * bc] = g.astype(r.dtype)
                    elif a.kind == "row":
                        row_sum[j] = g if j not in row_sum else row_sum[j] + g
                    else:
                        key = (j, h if a.ph else 0)
                        par_sum[key] = g if key not in par_sum else par_sum[key] + g
            for j, g in row_sum.items():
                out_refs[j][rs, :] = g.astype(out_refs[j].dtype)
        for (j, h), g in par_sum.items():
            bc = g.shape[1]
            out_refs[j][:, h * bc:(h + 1) * bc] += g

    out_specs, out_shape = [], []
    for k in diff_idx:
        a = args[k]
        bc = a.bc or a.arr.shape[1]
        out_specs.append(_arg_spec(a, tr, nh, ntab, base=0))
        out_shape.append(jax.ShapeDtypeStruct((t if a.kind == "row" else a.arr.shape[0], bc * (nh if a.ph else 1)), a.gdt if a.kind == "row" else F32))
    in_specs = [_arg_spec(a, tr, nh, ntab) for a in list(args) + list(cts)]
    return pl.pallas_call(
        body, grid=(t // tr,), in_specs=in_specs, out_specs=out_specs, out_shape=out_shape,
        compiler_params=_cparams(("arbitrary",)), name=name)(*[a.arr for a in list(args) + list(cts)])


def _conv_taps(x, w):
    rows = lax.broadcasted_iota(jnp.int32, x.shape, 0)
    y = x * w[CONV_K - 1:CONV_K, :]
    for s in range(1, CONV_K):
        y = y + jnp.where(rows >= s, pltpu.roll(x, s, 0), 0.0) * w[CONV_K - 1 - s:CONV_K - s, :]
    return y


CONV_HEADS = 4
CONV_BLOCKS_PER_THIRD = N_HEADS // CONV_HEADS


def _conv_post(y, block):
    a = _silu(y)
    normed = block < 2 * CONV_BLOCKS_PER_THIRD
    scale = jnp.where(block < CONV_BLOCKS_PER_THIRD, HEAD ** -0.5, 1.0)
    return a * jnp.where(normed, lax.rsqrt(jnp.sum(a * a, -1, keepdims=True) + EPS) * scale, 1.0)


def conv_fwd(z, w, lp):
    t, width = z.shape
    cols = CONV_HEADS * HEAD

    def body(z_ref, w_ref, o_ref, y_ref):
        block = pl.program_id(1)
        for h in range(CONV_HEADS):
            cs = slice(h * HEAD, (h + 1) * HEAD)
            y = _conv_taps(z_ref[:, cs].astype(F32), w_ref[:, cs])
            y_ref[:, cs] = y.astype(y_ref.dtype)
            o_ref[:, cs] = _conv_post(y, block)

    blk = pl.BlockSpec((lp, cols), lambda b, j: (b, j))
    out = jax.ShapeDtypeStruct((t, width), F32)
    return pl.pallas_call(
        body, grid=(t // lp, width // cols), in_specs=[blk, pl.BlockSpec((CONV_K, cols), lambda b, j: (0, j))],
        out_specs=[blk, blk], out_shape=[out, jax.ShapeDtypeStruct((t, width), _MXU_DTYPE)],
        compiler_params=_cparams(("arbitrary", "arbitrary")), name="a_conv_fwd")(z, w)


def conv_bwd(z, y, w, dout, lp):
    t, width = z.shape
    cols = CONV_HEADS * HEAD

    def body(z_ref, y_ref, w_ref, g_ref, dz_ref, dw_ref):
        block = pl.program_id(0)

        @pl.when(pl.program_id(1) == 0)
        def _():
            dw_ref[...] = jnp.zeros_like(dw_ref)

        for h in range(CONV_HEADS):
            cs = slice(h * HEAD, (h + 1) * HEAD)
            x, wv = z_ref[:, cs].astype(F32), w_ref[:, cs]
            _, vjp = jax.vjp(lambda y_: _conv_post(y_, block), y_ref[:, cs].astype(F32))
            (dy,) = vjp(g_ref[:, cs])
            rows = lax.broadcasted_iota(jnp.int32, x.shape, 0)
            dx = dy * wv[CONV_K - 1:CONV_K, :]
            dw_ref[CONV_K - 1:CONV_K, cs] += jnp.sum(dy * x, axis=0, keepdims=True)
            for s in range(1, CONV_K):
                dy_up = jnp.where(rows < lp - s, pltpu.roll(dy, lp - s, 0), 0.0)
                dx = dx + dy_up * wv[CONV_K - 1 - s:CONV_K - s, :]
                dw_ref[CONV_K - 1 - s:CONV_K - s, cs] += jnp.sum(dy_up * x, axis=0, keepdims=True)
            dz_ref[:, cs] = dx.astype(dz_ref.dtype)

    blk = pl.BlockSpec((lp, cols), lambda j, b: (b, j))
    w_blk = pl.BlockSpec((CONV_K, cols), lambda j, b: (0, j))
    return pl.pallas_call(
        body, grid=(width // cols, t // lp), in_specs=[blk, blk, w_blk, blk], out_specs=[blk, w_blk],
        out_shape=[jax.ShapeDtypeStruct((t, width), _MXU_DTYPE), jax.ShapeDtypeStruct((CONV_K, width), F32)],
        compiler_params=_cparams(("arbitrary", "arbitrary")), name="a_conv_bwd")(z, y, w, dout)


def _delta_chunk(q, k, v, ba, alog, dtb, state, t_stored):
    n_g, c = q.shape[0], q.shape[1]
    lane = lax.broadcasted_iota(jnp.int32, (1, HEAD), 1)

    def pick(xs, offset):
        cols = [jnp.sum(xs[i // N_HEADS if len(xs) > 1 else 0] * (lane == offset + i % N_HEADS).astype(F32), axis=1, keepdims=True)[None]
                for i in range(n_g)]
        return jnp.concatenate(cols, 0)

    b_raw, a_raw = pick(ba, 0), pick(ba, N_HEADS)
    a_log, dt_bias = pick((alog,), 0), pick((dtb,), 0)
    beta = _sigmoid(b_raw)
    g = -jnp.exp(a_log) * _softplus(a_raw + dt_bias)
    ri = lax.broadcasted_iota(jnp.int32, (c, c), 0)
    ci = lax.broadcasted_iota(jnp.int32, (c, c), 1)
    tril = ci <= ri
    lower = jnp.broadcast_to(tril.astype(F32), (n_g, c, c))
    gc_col = _dot_01(lower, g * jnp.ones((1, 1, HEAD), F32))[:, :, :1]
    gc_row = _dot_01(jnp.ones((n_g, 8, c), F32), g * (ri <= ci).astype(F32)[None])[:, 0:1, :]
    gc_last = jnp.sum(g, axis=1, keepdims=True)
    decay = jnp.exp(jnp.where(tril, gc_col - gc_row, NEG))
    e_gc = jnp.exp(gc_col)
    kb = k * beta
    a_mat = jnp.where(ci < ri, mm_nt(kb, k) * decay, 0.0)
    t_inv = _inv_unit_lower(a_mat) if t_stored is None else _inv_lookup(a_mat, t_stored)
    u_base = mm_nn(t_inv, v * beta)
    w_dec = mm_nn(t_inv, kb * e_gc)
    attn = jnp.where(tril, mm_nt(q, k) * decay, 0.0)
    u = u_base - mm_nn(w_dec, state)
    o = mm_nn(q * e_gc, state) + mm_nn(attn, u)
    new_state = state * jnp.exp(gc_last) + mm_tn(k * jnp.exp(gc_last - gc_col), u)
    return o, new_state, t_inv


DELTA_STEP_FWD = (4, 2)
DELTA_STEP_BWD = (2, 2)


def _heads_of(ref, rs, first_col):
    return jnp.stack([ref[i // N_HEADS, rs, first_col + (i % N_HEADS) * HEAD:first_col + (i % N_HEADS + 1) * HEAD]
                      for i in range(ref.shape[0] * N_HEADS)])


def _qkv_heads(ref, rs, part):
    return _heads_of(ref, rs, part * N_HEADS * HEAD)


def _by_sequence(a, lp):
    return a.reshape(a.shape[0] // lp, lp, a.shape[1])


def _ride(bufs, scatter, refs_in, refs_out, sems, first, last, two_level=False):
    if not bufs:
        return lambda: None
    make = lambda: (TwoLevelGather if two_level else Exchange)(refs_in, refs_out, *sems, scatter)

    @pl.when(first)
    def _():
        make().start()

    def finish():
        @pl.when(last)
        def _():
            make().wait()

    return finish


def delta_fwd(qkv, ba, ba_block, alog, dtb, lp, gather=()):
    t = qkv.shape[0]
    nb, nc = t // lp, lp // CHUNK
    seqs, cps = DELTA_STEP_FWD
    ng, rows = nc // cps, cps * CHUNK
    nx = len(gather)
    nbg = nb // seqs
    assert nc % cps == 0 and nb % seqs == 0

    def body(*refs):
        qkv_ref, ba_ref, al_ref, dt_ref = refs[:4]
        o_ref, s_ref, t_ref = refs[4 + nx:7 + nx]
        state_ref = refs[7 + 2 * nx]
        b, n = pl.program_id(0), pl.program_id(1)
        finish = _ride(gather, False, refs[4:4 + nx], refs[7 + nx:7 + 2 * nx], refs[8 + 2 * nx:], (b == 0) & (n == 0), (b == nbg - 1) & (n == ng - 1))

        @pl.when(n == 0)
        def _():
            state_ref[...] = jnp.zeros_like(state_ref)

        al, dtv = al_ref[...], dt_ref[...]
        for c in range(cps):
            rs = slice(c * CHUNK, (c + 1) * CHUNK)
            state = state_ref[...]
            o, new_state, t_inv = _delta_chunk(_qkv_heads(qkv_ref, rs, 0), _qkv_heads(qkv_ref, rs, 1), _qkv_heads(qkv_ref, rs, 2),
                                               tuple(ba_ref[i, rs, :] for i in range(seqs)), al, dtv, state, None)
            for i in range((seqs * N_HEADS)):
                seq, g = divmod(i, N_HEADS)
                o_ref[seq, rs, g * HEAD:(g + 1) * HEAD] = o[i]
                s_ref[seq, g, c] = state[i]
                t_ref[seq, g, c] = t_inv[i]
            state_ref[...] = new_state
        finish()

    rows_of = lambda width: pl.BlockSpec((seqs, rows, width), lambda b, n: (b, n, 0))
    par_spec = pl.BlockSpec((1, HEAD), lambda b, n: (0, 0))
    out = pl.pallas_call(
        body, grid=(nbg, ng),
        in_specs=[rows_of(3 * N_HEADS * HEAD), pl.BlockSpec((seqs, rows, HEAD), lambda b, n: (b, n, ba_block)), par_spec, par_spec] + [_HBM] * nx,
        out_specs=[rows_of(N_HEADS * HEAD), pl.BlockSpec((seqs, N_HEADS, cps, HEAD, HEAD), lambda b, n: (b, 0, n, 0, 0)),
                   pl.BlockSpec((seqs, N_HEADS, cps, CHUNK, CHUNK), lambda b, n: (b, 0, n, 0, 0))] + [_HBM] * nx,
        out_shape=[jax.ShapeDtypeStruct((nb, lp, N_HEADS * HEAD), F32), jax.ShapeDtypeStruct((nb, N_HEADS, nc, HEAD, HEAD), F32),
                   jax.ShapeDtypeStruct((nb, N_HEADS, nc, CHUNK, CHUNK), F32)] + Exchange.out_shape(gather, False),
        scratch_shapes=[pltpu.VMEM(((seqs * N_HEADS), HEAD, HEAD), F32)] + (Exchange.scratch(nx) if nx else []),
        compiler_params=_cparams(("arbitrary", "arbitrary")), name="delta_fwd")(_by_sequence(qkv, lp), _by_sequence(ba, lp), alog, dtb, *gather)
    return [out[0].reshape(t, N_HEADS * HEAD)] + list(out[1:])


def delta_bwd(qkv, ba, ba_block, alog, dtb, states, t_invs, do, lp, scatter=()):
    t = qkv.shape[0]
    nb, nc = t // lp, lp // CHUNK
    seqs, cps = DELTA_STEP_BWD
    ng, rows = nc // cps, cps * CHUNK
    nx = len(scatter)
    nbg = nb // seqs

    def body(*refs):
        qkv_ref, ba_ref, al_ref, dt_ref, s_ref, t_ref, do_ref = refs[:7]
        dqkv_ref, dba_ref, dal_ref, ddt_ref = refs[7 + nx:11 + nx]
        dstate_ref = refs[11 + 2 * nx]
        b, step = pl.program_id(0), pl.program_id(1)
        finish = _ride(scatter, True, refs[7:7 + nx], refs[11 + nx:11 + 2 * nx], refs[12 + 2 * nx:], (b == 0) & (step == 0),
                       (b == nbg - 1) & (step == ng - 1))

        @pl.when(step == 0)
        def _():
            dstate_ref[...] = jnp.zeros_like(dstate_ref)

        @pl.when((b == 0) & (step == 0))
        def _():
            dal_ref[...] = jnp.zeros_like(dal_ref)
            ddt_ref[...] = jnp.zeros_like(ddt_ref)

        al, dtv = al_ref[...], dt_ref[...]
        d_al = jnp.zeros((1, HEAD), F32)
        d_dt = jnp.zeros((1, HEAD), F32)
        for c in reversed(range(cps)):
            rs = slice(c * CHUNK, (c + 1) * CHUNK)
            t_n = jnp.stack([t_ref[i // N_HEADS, i % N_HEADS, c] for i in range((seqs * N_HEADS))])
            s_n = jnp.stack([s_ref[i // N_HEADS, i % N_HEADS, c] for i in range((seqs * N_HEADS))])

            def f(q_, k_, v_, ba_, al_, dt_, s_, t_n=t_n):
                return _delta_chunk(q_, k_, v_, ba_, al_, dt_, s_, t_n)[:2]

            _, vjp = jax.vjp(f, _qkv_heads(qkv_ref, rs, 0), _qkv_heads(qkv_ref, rs, 1), _qkv_heads(qkv_ref, rs, 2), tuple(ba_ref[i, rs, :] for i in range(seqs)), al, dtv, s_n)
            grads = vjp((_heads_of(do_ref, rs, 0), dstate_ref[...]))
            for part in range(3):
                for i in range((seqs * N_HEADS)):
                    col = (part * N_HEADS + i % N_HEADS) * HEAD
                    dqkv_ref[i // N_HEADS, rs, col:col + HEAD] = grads[part][i]
            for i in range(seqs):
                dba_ref[i, rs, :] = grads[3][i]
            d_al, d_dt = d_al + grads[4], d_dt + grads[5]
            dstate_ref[...] = grads[6]
        dal_ref[...] += d_al
        ddt_ref[...] += d_dt
        finish()

    rows_of = lambda width: pl.BlockSpec((seqs, rows, width), lambda b, n: (b, ng - 1 - n, 0))
    par_spec = pl.BlockSpec((1, HEAD), lambda b, n: (0, 0))
    out = pl.pallas_call(
        body, grid=(nbg, ng),
        in_specs=[rows_of(3 * N_HEADS * HEAD), pl.BlockSpec((seqs, rows, HEAD), lambda b, n: (b, ng - 1 - n, ba_block)), par_spec, par_spec,
                  pl.BlockSpec((seqs, N_HEADS, cps, HEAD, HEAD), lambda b, n: (b, 0, ng - 1 - n, 0, 0)),
                  pl.BlockSpec((seqs, N_HEADS, cps, CHUNK, CHUNK), lambda b, n: (b, 0, ng - 1 - n, 0, 0)), rows_of(N_HEADS * HEAD)] + [_HBM] * nx,
        out_specs=[rows_of(3 * N_HEADS * HEAD), rows_of(HEAD), par_spec, par_spec] + [_HBM] * nx,
        out_shape=[jax.ShapeDtypeStruct((nb, lp, 3 * N_HEADS * HEAD), F32), jax.ShapeDtypeStruct((nb, lp, HEAD), F32),
                   jax.ShapeDtypeStruct((1, HEAD), F32), jax.ShapeDtypeStruct((1, HEAD), F32)] + Exchange.out_shape(scatter, True),
        scratch_shapes=[pltpu.VMEM(((seqs * N_HEADS), HEAD, HEAD), F32)] + (Exchange.scratch(nx) if nx else []),
        compiler_params=_cparams(("arbitrary", "arbitrary")), name="delta_bwd")(
            _by_sequence(qkv, lp), _by_sequence(ba, lp), alog, dtb, states, t_invs, _by_sequence(do, lp), *scatter)
    return [out[0].reshape(t, 3 * N_HEADS * HEAD), out[1].reshape(t, HEAD)] + list(out[2:])


ATT_Q_TILE = 256
ATT_K_TILE = 512
ATT_SCALE = QK_DIM ** -0.5


def _tiles(end, size):
    return [(s, min(s + size, end)) for s in range(0, end, size)]


def _att_visible(q0, q1, k0, k1, keys_first):
    if k1 <= q0 + CHUNK and k0 >= PAD_ROWS:
        return None
    shape = (k1 - k0, q1 - q0) if keys_first else (q1 - q0, k1 - k0)
    qpos = q0 + lax.broadcasted_iota(jnp.int32, shape, 1 if keys_first else 0)
    kpos = k0 + lax.broadcasted_iota(jnp.int32, shape, 0 if keys_first else 1)
    shift = CHUNK.bit_length() - 1
    return (jnp.right_shift(kpos, shift) <= jnp.right_shift(qpos, shift)) & (kpos >= PAD_ROWS)


def _att_seq_specs(lp):
    return pl.BlockSpec((lp, QK_PAD), lambda b, h: (b, h)), pl.BlockSpec((lp, HEAD), lambda b, h: (b, h))


def flash_fwd(q, k, v, lp):
    t = q.shape[0]
    qk_seq, o_seq = _att_seq_specs(lp)

    def body(q_ref, k_ref, v_ref, o_ref, lse_ref):
        q_tiles = _tiles(lp, ATT_Q_TILE)

        def score_steps(q0, q1, out):
            def step(k0, k1):
                s = mm_nt(q_ref[q0:q1, :], k_ref[k0:k1, :])
                vis = _att_visible(q0, q1, k0, k1, False)
                s = s if vis is None else jnp.where(vis, s, NEG)
                out["scores"].append(s)
                row_max = jnp.max(s, -1, keepdims=True)
                out["m"] = row_max if out["m"] is None else jnp.maximum(out["m"], row_max)
            return [functools.partial(step, k0, k1) for k0, k1 in _tiles(q1, ATT_K_TILE)]

        cur = {"scores": [], "m": None}
        for step in score_steps(*q_tiles[0], cur):
            step()
        for i, (q0, q1) in enumerate(q_tiles):
            nxt = {"scores": [], "m": None}
            ahead = score_steps(*q_tiles[i + 1], nxt) if i + 1 < len(q_tiles) else []
            l = jnp.zeros((q1 - q0, 1), F32)
            acc = jnp.zeros((q1 - q0, HEAD), F32)
            for s, (k0, k1) in zip(cur["scores"], _tiles(q1, ATT_K_TILE), strict=True):
                if ahead:
                    ahead.pop(0)()
                p = jnp.exp(s - cur["m"])
                l = l + jnp.sum(p, -1, keepdims=True)
                acc = acc + mm_nn(p, v_ref[k0:k1, :])
            for step in ahead:
                step()
            o_ref[q0:q1, :] = acc / l
            lse_ref[q0:q1, :] = jnp.broadcast_to(cur["m"] + jnp.log(l), (q1 - q0, HEAD))
            cur = nxt

    big = jax.ShapeDtypeStruct((t, N_HEADS * HEAD), F32)
    return pl.pallas_call(
        body, grid=(t // lp, N_HEADS), in_specs=[qk_seq, qk_seq, o_seq], out_specs=[o_seq, o_seq], out_shape=[big, big],
        compiler_params=_cparams(("arbitrary", "arbitrary")), name="flash_fwd")(q, k, v)


def flash_bwd(q, k, v, o, lse, do, lp):
    t = q.shape[0]
    qk_seq, o_seq = _att_seq_specs(lp)

    def body(q_ref, k_ref, v_ref, o_ref, lse_ref, do_ref, dq_ref, dk_out_ref, dv_out_ref, dk_ref, dv_ref):
        dk_ref[...] = jnp.zeros_like(dk_ref)
        dv_ref[...] = jnp.zeros_like(dv_ref)
        for q0, q1 in _tiles(lp, ATT_Q_TILE):
            qb, dob = q_ref[q0:q1, :], do_ref[q0:q1, :]
            lse_row = jnp.transpose(lse_ref[q0:q1, :])[0:1, :]
            dsum_row = jnp.sum(jnp.transpose(dob * o_ref[q0:q1, :]), axis=0, keepdims=True)
            dq = jnp.zeros((q1 - q0, QK_PAD), F32)
            for k0, k1 in _tiles(q1, ATT_K_TILE):
                kb, vb = k_ref[k0:k1, :], v_ref[k0:k1, :]
                s = mm_nt(kb, qb)
                vis = _att_visible(q0, q1, k0, k1, True)
                s = s if vis is None else jnp.where(vis, s, NEG)
                p = jnp.exp(s - lse_row)
                ds = p * (mm_nt(vb, dob) - dsum_row)
                dv_ref[k0:k1, :] += mm_nn(p, dob)
                dk_ref[k0:k1, :] += mm_nn(ds, qb)
                dq = dq + mm_tn(ds, kb)
            dq_ref[q0:q1, :] = dq.astype(dq_ref.dtype)
        dk_out_ref[...] = dk_ref[...].astype(dk_out_ref.dtype)
        dv_out_ref[...] = dv_ref[...].astype(dv_out_ref.dtype)

    narrow = _MXU_DTYPE
    return pl.pallas_call(
        body, grid=(t // lp, N_HEADS), in_specs=[qk_seq, qk_seq, o_seq, o_seq, o_seq, o_seq], out_specs=[qk_seq, qk_seq, o_seq],
        out_shape=[jax.ShapeDtypeStruct((t, N_HEADS * QK_PAD), narrow), jax.ShapeDtypeStruct((t, N_HEADS * QK_PAD), narrow),
                   jax.ShapeDtypeStruct((t, N_HEADS * HEAD), narrow)],
        scratch_shapes=[pltpu.VMEM((lp, QK_PAD), F32), pltpu.VMEM((lp, HEAD), F32)],
        compiler_params=_cparams(("arbitrary", "arbitrary")), name="flash_bwd")(q, k, v, o, lse, do)


def loss_head(h2, target, lp):
    nb, seq, d = target.shape
    cols = _pick(d, (512, 128))
    ncol = d // cols

    def body(h_ref, t_ref, loss_ref, dh_ref, acc_ref):
        b, j = pl.program_id(0), pl.program_id(1)

        @pl.when((b == 0) & (j == 0))
        def _():
            acc_ref[...] = jnp.zeros_like(acc_ref)

        err = h_ref[LEAD:, :] - t_ref[...]
        dh_ref[:LEAD, :] = jnp.zeros((LEAD, cols), F32)
        dh_ref[LEAD:, :] = err * (1.0 / d)
        acc_ref[...] += jnp.sum(err * err, axis=0, keepdims=True)

        @pl.when((b == nb - 1) & (j == ncol - 1))
        def _():
            loss_ref[...] = jnp.sum(acc_ref[...], axis=1, keepdims=True) * (0.5 / d)

    return pl.pallas_call(
        body, grid=(nb, ncol),
        in_specs=[pl.BlockSpec((None, lp, cols), lambda b, j: (b, 0, j)), pl.BlockSpec((None, seq, cols), lambda b, j: (b, 0, j))],
        out_specs=[pl.BlockSpec((1, 1), lambda b, j: (0, 0)), pl.BlockSpec((None, lp, cols), lambda b, j: (b, 0, j))],
        out_shape=[jax.ShapeDtypeStruct((1, 1), F32), jax.ShapeDtypeStruct((nb, lp, d), F32)],
        scratch_shapes=[pltpu.VMEM((1, cols), F32)], compiler_params=_cparams(("arbitrary", "arbitrary")), name="loss_head")(h2, target)


def gated_out(name, o, gate, gain, w, res):
    t, kw = o.shape
    d = w.shape[1]
    tm = _pick(t, (512, 256, 128))

    def body(*refs):
        o_ref, gate_ref = refs[:2]
        w_ref, r_ref, h_ref, g_ref = refs[-4:]
        if gain is None:
            g_ref[...] = _f_gate(o_ref[...], gate_ref[...])[0].astype(g_ref.dtype)
        else:
            for h in range(N_HEADS):
                cs = slice(h * HEAD, (h + 1) * HEAD)
                g_ref[:, cs] = _f_out_gate(o_ref[:, cs], gate_ref[:, cs], refs[2][...])[0].astype(g_ref.dtype)
        h_ref[...] = r_ref[...] + _dot(g_ref[...], w_ref[...], ((1,), (0,)))

    rows = lambda width: pl.BlockSpec((tm, width), lambda i: (i, 0))
    whole = lambda a: pl.BlockSpec(a.shape, lambda i: (0, 0))
    params = [] if gain is None else [gain]
    return pl.pallas_call(
        body, grid=(t // tm,), in_specs=[rows(kw), rows(kw)] + [whole(p) for p in params] + [whole(w), rows(d)], out_specs=[rows(d), rows(kw)],
        out_shape=[jax.ShapeDtypeStruct((t, d), F32), jax.ShapeDtypeStruct((t, kw), _MXU_DTYPE)],
        compiler_params=_cparams(("parallel",)), name=name)(o, gate, *params, w, res)


def embed_norm(x, meta, gain, lp, gather=()):
    nb, seq, d = x.shape
    nblk, nx = lp // LEAD, len(gather)

    def body(*refs):
        x_ref, meta_ref, g_ref = refs[:3]
        h_ref, hn_ref = refs[3 + nx:5 + nx]
        b, i = pl.program_id(0), pl.program_id(1)
        finish = _ride(gather, False, refs[3:3 + nx], refs[5 + nx:5 + 2 * nx], refs[5 + 2 * nx:], (b == 0) & (i == 0), (b == nb - 1) & (i == nblk - 1),
                       two_level=True)

        @pl.when(i == 0)
        def _():
            h_ref[:PAD_ROWS, :] = jnp.zeros((PAD_ROWS, d), F32)
            h_ref[PAD_ROWS:, :] = meta_ref[...]

        @pl.when(i > 0)
        def _():
            h_ref[...] = x_ref[...]

        hn_ref[...] = _rms(h_ref[...], g_ref[...]).astype(hn_ref.dtype)
        finish()

    rows = pl.BlockSpec((LEAD, d), lambda b, i: (b * nblk + i, 0))
    out = pl.pallas_call(
        body, grid=(nb, nblk),
        in_specs=[pl.BlockSpec((None, LEAD, d), lambda b, i: (b, jnp.maximum(i - 1, 0), 0)), pl.BlockSpec((N_META, d), lambda b, i: (0, 0)),
                  pl.BlockSpec((1, d), lambda b, i: (0, 0))] + [_HBM] * nx,
        out_specs=[rows, rows] + [_HBM] * nx,
        out_shape=[jax.ShapeDtypeStruct((nb * lp, d), F32), jax.ShapeDtypeStruct((nb * lp, d), _MXU_DTYPE)] + Exchange.out_shape(gather, False),
        scratch_shapes=Exchange.scratch(nx) if nx else [],
        compiler_params=_cparams(("arbitrary", "arbitrary")), name="embed_norm")(x, meta, gain, *gather)
    return list(out)


def meta_grad(dh0):
    nb, _, d = dh0.shape

    def body(g_ref, o_ref):
        @pl.when(pl.program_id(0) == 0)
        def _():
            o_ref[...] = jnp.zeros_like(o_ref)

        o_ref[...] += g_ref[PAD_ROWS:LEAD, :]

    return pl.pallas_call(
        body, grid=(nb,), in_specs=[pl.BlockSpec((None, LEAD, d), lambda b: (b, 0, 0))],
        out_specs=pl.BlockSpec((N_META, d), lambda b: (0, 0)), out_shape=jax.ShapeDtypeStruct((N_META, d), F32),
        compiler_params=_cparams(("arbitrary",)), name="meta_grad")(dh0)


_HBM = pl.BlockSpec(memory_space=pltpu.HBM)


def _mesh_pos():
    x, y, c = lax.axis_index("x"), lax.axis_index("y"), lax.axis_index("c")
    return x, y, c


def _peer(x, y, c, k):
    px = 1 - x if k & 4 else x
    py = 1 - y if k & 2 else y
    pc = 1 - c if k & 1 else c
    return (px, py, pc), 4 * px + 2 * py + pc


class Exchange:
    def __init__(self, x_refs, out_refs, send_sems, recv_sems, local_sems, scatter):
        self.x_refs, self.out_refs, self.scatter = x_refs, out_refs, scatter
        self.send_sems, self.recv_sems, self.local_sems = send_sems, recv_sems, local_sems
        self.pos = _mesh_pos()
        x, y, c = self.pos
        self.me = 4 * x + 2 * y + c

    @staticmethod
    def scratch(n):
        return [pltpu.SemaphoreType.DMA((n, N_DEV - 1)), pltpu.SemaphoreType.DMA((n, N_DEV - 1)), pltpu.SemaphoreType.DMA((n,))]

    @staticmethod
    def out_shape(bufs, scatter):
        return [jax.ShapeDtypeStruct(b.shape if scatter else (N_DEV,) + b.shape, b.dtype) for b in bufs]

    def _local(self, i):
        return pltpu.make_async_copy(self.x_refs[i].at[self.me] if self.scatter else self.x_refs[i], self.out_refs[i].at[self.me], self.local_sems.at[i])

    def _copy(self, i, k, landing):
        peer, peer_id = _peer(*self.pos, k)
        src = self.x_refs[i].at[peer_id] if self.scatter else self.x_refs[i]
        return pltpu.make_async_remote_copy(src_ref=src, dst_ref=self.out_refs[i].at[peer_id if landing else self.me],
                                            send_sem=self.send_sems.at[i, k - 1], recv_sem=self.recv_sems.at[i, k - 1],
                                            device_id=peer, device_id_type=pl.DeviceIdType.MESH)

    def start(self):
        for i in range(len(self.x_refs)):
            self._local(i).start()
        for k in range(1, N_DEV):
            for i in range(len(self.x_refs)):
                self._copy(i, k, False).start()

    def wait(self):
        for k in range(1, N_DEV):
            for i in range(len(self.x_refs)):
                self._copy(i, k, True).wait_recv()
        for k in range(1, N_DEV):
            for i in range(len(self.x_refs)):
                self._copy(i, k, False).wait_send()
        for i in range(len(self.x_refs)):
            self._local(i).wait()


class TwoLevelGather(Exchange):
    DIRECT = (1, 4, 2, 6)
    FROM_CHIPS = (4, 2, 6)

    def _forward(self, i, k):
        _, origin = _peer(*self.pos, k)
        sibling, _ = _peer(*self.pos, 1)
        block = self.out_refs[i].at[origin]
        return pltpu.make_async_remote_copy(src_ref=block, dst_ref=block, send_sem=self.send_sems.at[i, (k ^ 1) - 1],
                                            recv_sem=self.recv_sems.at[i, (k ^ 1) - 1], device_id=sibling, device_id_type=pl.DeviceIdType.MESH)

    def start(self):
        assert not self.scatter
        for i in range(len(self.x_refs)):
            self._local(i).start()
        for k in self.DIRECT:
            for i in range(len(self.x_refs)):
                self._copy(i, k, False).start()

    def wait(self):
        n = range(len(self.x_refs))
        for k in self.FROM_CHIPS:
            for i in n:
                self._copy(i, k, True).wait_recv()
                self._forward(i, k).start()
        for k in (1, 5, 3, 7):
            for i in n:
                self._copy(i, k, True).wait_recv()
        for k in self.DIRECT:
            for i in n:
                self._copy(i, k, False).wait_send()
        for k in self.FROM_CHIPS:
            for i in n:
                self._forward(i, k).wait_send()
        for i in n:
            self._local(i).wait()


def _exchange(name, bufs, scatter):
    n = len(bufs)

    def body(*refs):
        ex = Exchange(refs[:n], refs[n:2 * n], *refs[2 * n:], scatter)
        ex.start()
        ex.wait()

    return pl.pallas_call(body, in_specs=[_HBM] * n, out_specs=[_HBM] * n, out_shape=Exchange.out_shape(bufs, scatter),
                          scratch_shapes=Exchange.scratch(n), name=name)(*bufs)


def _f_rms(x, g):
    return (_rms(x, g),)


def _f_rms2(x, g1, g2):
    r = x * lax.rsqrt(jnp.sum(x * x, -1, keepdims=True) / x.shape[-1] + EPS)
    return r * g1, r * g2


@jax.custom_vjp
def _out_gate(o, gate, gain):
    return _rms(o, gain) * _silu(gate)


def _out_gate_bwd(res, g):
    o, gate, gain = res
    r = lax.rsqrt(jnp.sum(o * o, -1, keepdims=True) / o.shape[-1] + EPS)
    n = o * r
    s = _sigmoid(gate)
    g_norm = g * (gate * s)
    d_gate = g * (n * gain) * (s * (1.0 + gate * (1.0 - s)))
    gn = g_norm * gain
    d_o = r * (gn - n * (jnp.sum(gn * n, -1, keepdims=True) / o.shape[-1]))
    return d_o, d_gate, jnp.sum(g_norm * n, 0, keepdims=True)


_out_gate.defvjp(lambda o, gate, gain: (_out_gate(o, gate, gain), (o, gate, gain)), _out_gate_bwd)


def _f_out_gate(o, gate, gain):
    return (_out_gate(o, gate, gain),)


def _f_gate(o, gate):
    return (o * _silu(gate),)


def _swap_rope_halves(x):
    return pltpu.roll(x, ROPE // 2, 1) + pltpu.roll(x, HEAD - ROPE // 2, 1)


def _qk_final_inv_rms(nope, rope_in):
    ms = (jnp.sum(nope * nope, -1, keepdims=True) + jnp.sum(rope_in * rope_in, -1, keepdims=True)) / QK_DIM
    return lax.rsqrt(ms + EPS)


@functools.partial(jax.custom_vjp, nondiff_argnums=(0,))
def _qk_final(scale, nope, rope_in, g_nope, g_rope, cos, sin):
    r = _qk_final_inv_rms(nope, rope_in)
    b = rope_in * (r * g_rope)
    out = jnp.concatenate([nope * (r * g_nope), b * cos + _swap_rope_halves(b) * sin], axis=1)
    return out if scale == 1.0 else out * scale


def _qk_final_fwd(scale, nope, rope_in, g_nope, g_rope, cos, sin):
    return _qk_final(scale, nope, rope_in, g_nope, g_rope, cos, sin), (nope, rope_in, g_nope, g_rope, cos, sin)


def _qk_final_bwd(scale, res, g):
    nope, rope_in, g_nope, g_rope, cos, sin = res
    r = _qk_final_inv_rms(nope, rope_in)
    ga, gb = g[:, :HEAD], g[:, HEAD:]
    if scale != 1.0:
        ga, gb = ga * scale, gb * scale
    db = gb * cos + _swap_rope_halves(gb * sin)
    t_a, t_b = ga * nope, db * rope_in
    d_r = jnp.sum(t_a * g_nope + t_b * g_rope, -1, keepdims=True)
    c = d_r * (r * r * r) * (-1.0 / QK_DIM)
    d_nope = ga * (r * g_nope) + nope * c
    d_rope = db * (r * g_rope) + rope_in * c
    d_g_nope = jnp.sum(t_a * r, 0, keepdims=True)
    d_g_rope = jnp.sum(t_b * r, 0, keepdims=True)
    return d_nope, d_rope, d_g_nope, d_g_rope, jnp.zeros_like(cos), jnp.zeros_like(sin)


_qk_final.defvjp(_qk_final_fwd, _qk_final_bwd)


def _f_qk_final(scale, nope, rope_in, g_nope, g_rope, cos, sin):
    return (_qk_final(scale, nope, rope_in, g_nope, g_rope, cos, sin),)


def _rope_tables(lp):
    half = ROPE // 2
    pos = jnp.maximum(jnp.arange(lp) - PAD_ROWS, 0)
    inv = ROPE_THETA ** (-jnp.arange(half, dtype=F32) / half)
    ang = pos.astype(F32)[:, None] * inv[None, :]
    zeros = jnp.zeros((lp, HEAD - ROPE), F32)
    cos = jnp.concatenate([jnp.cos(ang), jnp.cos(ang), zeros], 1)
    sin = jnp.concatenate([-jnp.sin(ang), jnp.sin(ang), zeros], 1)
    return cos, sin


def _pad_lanes(w, width=HEAD):
    return jnp.pad(w, ((0, 0), (0, width - w.shape[1])))


def _pad_rows(w, rows=HEAD):
    return jnp.pad(w, ((0, rows - w.shape[0]), (0, 0)))


def _split_heads_qk_t(w_t):
    k = w_t.shape[1]
    return jnp.pad(w_t.reshape(N_HEADS, QK_DIM, k), ((0, 0), (0, QK_PAD - QK_DIM), (0, 0))).reshape(N_HEADS * QK_PAD, k)


def _merge_heads_qk_t(g_t):
    k = g_t.shape[1]
    return g_t.reshape(N_HEADS, QK_PAD, k)[:, :QK_DIM].reshape(N_HEADS * QK_DIM, k)


@functools.partial(jax.custom_vjp, nondiff_argnums=(0,))
def _q_final(scale, qh, g_nope, g_rope, cos, sin):
    return _qk_final(scale, qh[:, :HEAD], qh[:, HEAD:], g_nope, g_rope, cos, sin)


def _q_final_bwd(scale, res, g):
    qh, g_nope, g_rope, cos, sin = res
    grads = _qk_final_bwd(scale, (qh[:, :HEAD], qh[:, HEAD:], g_nope, g_rope, cos, sin), g)
    return (jnp.concatenate(grads[:2], axis=1),) + tuple(grads[2:])


_q_final.defvjp(lambda scale, qh, *rest: (_q_final(scale, qh, *rest), (qh,) + rest), _q_final_bwd)


def _f_q_final(scale, qh, g_nope, g_rope, cos, sin):
    return (_q_final(scale, qh, g_nope, g_rope, cos, sin),)


def local_step(x, target, w, deferred=None):
    nb, seq, d = x.shape
    lp = seq + LEAD
    t = nb * lp
    tr = _pick(lp, (544, 128))
    ntab = lp // tr
    mxu = _MXU_DTYPE
    kw = N_HEADS * HEAD

    a_conv = w["a_conv"].T
    alog, dtb, o_gain = _pad_lanes(w["a_log"]), _pad_lanes(w["a_dt_bias"]), w["a_o_gain"]
    a_norm, kv_norm, b_norm = w["a_norm"], w["kv_norm"][None, :], w["b_norm"]
    lat_norm, qlat_norm = w["kv_latent_norm"][None, :], w["b_q_latent_norm"]
    kg_nope, kg_rope = w["k_gain"][None, :HEAD], _pad_lanes(w["k_gain"][None, HEAD:])
    qg_nope, qg_rope = w["b_q_gain"][:, :HEAD], _pad_lanes(w["b_q_gain"][:, HEAD:])
    cos, sin = _rope_tables(lp)

    h0, hn, *gathered = embed_norm(x, w["meta_tokens"].T, a_norm, lp, gather=deferred.first_gather_bufs if deferred else ())
    if deferred:
        w = {**w, **deferred.finish_first(gathered)}
    a_w_in_t = w["a_w_in"].astype(mxu)
    w_qkv_t, w_gba_t = a_w_in_t[:3 * kw], _pad_rows(a_w_in_t[3 * kw:], kw + HEAD)
    z_qkv = matmul("a_in_qkv", hn, w_qkv_t, "nt", out_dtype=mxu)
    z_gba = matmul("a_in_gate_ba", hn, w_gba_t, "nt")
    ba_block = kw // HEAD
    qkv_a, y_conv = conv_fwd(z_qkv, a_conv, lp)
    o_a, states, t_invs, *gathered = delta_fwd(qkv_a, z_gba, ba_block, alog, dtb, lp, gather=deferred.gather_bufs if deferred else ())
    if deferred:
        w = {**w, **deferred.finish(gathered)}
    a_w_out = w["a_w_out"].astype(mxu)
    w_down = _pad_lanes(w["kv_w_down"], KV_RANK + HEAD).astype(mxu)
    w_ukv_t = jnp.concatenate([w["kv_w_uk"], w["kv_w_uv"]], 0).astype(mxu)
    b_w_in_t = w["b_w_in"].astype(mxu)
    w_cq_t, w_gb_t = b_w_in_t[:Q_RANK], b_w_in_t[Q_RANK:]
    w_q_t = _split_heads_qk_t(w["b_w_uq"]).astype(mxu)
    b_w_out = w["b_w_out"].astype(mxu)
    og_args = [Arg(o_a, bc=HEAD, ph=True, diff=True), Arg(z_gba, bc=HEAD, ph=True, diff=True, gdt=mxu), Arg(o_gain, "par", diff=True)]
    h1, og_a = gated_out("a_out", o_a, z_gba, o_gain, a_w_out, h0)

    hk, hb = row_call("b_norms_fwd", _f_rms2, [Arg(h1), Arg(kv_norm, "par"), Arg(b_norm, "par")], [(d, mxu, d, False), (d, mxu, d, False)], tr)
    c_down = matmul("kv_down", hk, w_down, "nn")
    c_kv_arg = Arg(c_down, bc=KV_RANK, diff=True, gdt=mxu)
    k_pe_arg = Arg(c_down, bc=HEAD, base=KV_RANK // HEAD, diff=True)
    c_q_raw = matmul("b_in_q", hb, w_cq_t, "nt")
    gate_b = matmul("b_in_gate", hb, w_gb_t, "nt")
    (c_kv,) = row_call("kv_latent_fwd", _f_rms, [c_kv_arg, Arg(lat_norm, "par")], [(KV_RANK, mxu, KV_RANK, False)], tr)
    (c_q,) = row_call("q_latent_fwd", _f_rms, [Arg(c_q_raw), Arg(qlat_norm, "par")], [(Q_RANK, mxu, Q_RANK, False)], tr)
    k_nope = matmul("k_up", c_kv, w_ukv_t[:kw], "nt")
    v_b = matmul("v_up", c_kv, w_ukv_t[kw:], "nt", out_dtype=mxu)
    q_up = matmul("q_up", c_q, w_q_t, "nt")
    tabs = [Arg(cos, "tab"), Arg(sin, "tab")]
    k_args = [Arg(k_nope, bc=HEAD, ph=True, diff=True, gdt=mxu), k_pe_arg, Arg(kg_nope, "par", diff=True), Arg(kg_rope, "par", diff=True)] + tabs
    q_args = [Arg(q_up, bc=QK_PAD, ph=True, diff=True, gdt=mxu), Arg(qg_nope, "par", diff=True), Arg(qg_rope, "par", diff=True)] + tabs
    f_k_final, f_q_final = functools.partial(_f_qk_final, 1.0), functools.partial(_f_q_final, ATT_SCALE)
    (k_fin,) = row_call("k_final_fwd", f_k_final, k_args, [(N_HEADS * QK_PAD, mxu, QK_PAD, True)], tr, nh=N_HEADS, ntab=ntab)
    (q_fin,) = row_call("q_final_fwd", f_q_final, q_args, [(N_HEADS * QK_PAD, mxu, QK_PAD, True)], tr, nh=N_HEADS, ntab=ntab)
    o_b, lse = flash_fwd(q_fin, k_fin, v_b, lp)
    gb_args = [Arg(o_b, diff=True), Arg(gate_b, diff=True, gdt=mxu)]
    h2, og_b = gated_out("b_out", o_b, gate_b, None, b_w_out, h1)

    loss, dh2 = loss_head(h2.reshape(nb, lp, d), target, lp)
    dh2 = dh2.reshape(t, d)
    grads = {}

    d_og_b = matmul("b_out_dx", dh2, b_w_out, "nt", out_dtype=mxu)
    grads["b_w_out"] = matmul("b_out_dw", og_b, dh2, "tn")
    d_o_b, d_gate_b = row_vjp_call("b_gate_bwd", _f_gate, gb_args, [Arg(d_og_b)], tr)
    dq_fin, dk_fin, dv_b = flash_bwd(q_fin, k_fin, v_b, o_b, lse, d_o_b, lp)
    dq_up, d_qg_nope, d_qg_rope = row_vjp_call(
        "q_final_bwd", f_q_final, q_args, [Arg(dq_fin, bc=QK_PAD, ph=True)], tr, nh=N_HEADS, ntab=ntab)
    dk_nope, dk_pe, d_kg_nope, d_kg_rope = row_vjp_call(
        "k_final_bwd", f_k_final, k_args, [Arg(dk_fin, bc=QK_PAD, ph=True)], tr, nh=N_HEADS, ntab=ntab)
    grads["b_q_gain"] = jnp.concatenate([d_qg_nope, d_qg_rope[:, :ROPE]], 1)
    grads["k_gain"] = jnp.concatenate([d_kg_nope, d_kg_rope[:, :ROPE]], 1)[0]
    d_c_q = matmul("q_up_dx", dq_up, w_q_t, "nn")
    grads["b_w_uq"] = _merge_heads_qk_t(matmul("q_up_dw", dq_up, c_q, "tn"))
    d_c_kv = matmul("k_up_dx", dk_nope, w_ukv_t[:kw], "nn")
    d_c_kv = matmul("v_up_dx", dv_b, w_ukv_t[kw:], "nn", res=d_c_kv)
    grads["kv_w_uk"], grads["kv_w_uv"] = matmul("k_up_dw", dk_nope, c_kv, "tn"), matmul("v_up_dw", dv_b, c_kv, "tn")
    d_c_q_raw, grads["b_q_latent_norm"] = row_vjp_call(
        "q_latent_bwd", _f_rms, [Arg(c_q_raw, diff=True, gdt=mxu), Arg(qlat_norm, "par", diff=True)], [Arg(d_c_q)], tr)
    d_c_kv_raw, d_lat = row_vjp_call(
        "kv_latent_bwd", _f_rms, [c_kv_arg, Arg(lat_norm, "par", diff=True)], [Arg(d_c_kv)], tr)
    grads["kv_latent_norm"] = d_lat[0]
    d_hb = matmul("b_in_q_dx", d_c_q_raw, w_cq_t, "nn")
    d_hb = matmul("b_in_gate_dx", d_gate_b, w_gb_t, "nn", res=d_hb, out_dtype=mxu)
    grads["b_w_in"] = jnp.concatenate([matmul("b_in_q_dw", d_c_q_raw, hb, "tn"), matmul("b_in_gate_dw", d_gate_b, hb, "tn")], 0)
    d_c_down = jnp.concatenate([d_c_kv_raw, dk_pe.astype(mxu)], 1)
    d_hk = matmul("kv_down_dx", d_c_down, w_down, "nt", out_dtype=mxu)
    grads["kv_w_down"] = matmul("kv_down_dw", hk, d_c_down, "tn")[:, :KV_RANK + ROPE]
    dh1, d_kv_norm, grads["b_norm"] = row_vjp_call(
        "b_norms_bwd", lambda x_, g1, g2: _f_rms2(x_, g1, g2) + (x_,),
        [Arg(h1, diff=True), Arg(kv_norm, "par", diff=True), Arg(b_norm, "par", diff=True)], [Arg(d_hk), Arg(d_hb), Arg(dh2)], tr)
    grads["kv_norm"] = d_kv_norm[0]

    d_og_a = matmul("a_out_dx", dh1, a_w_out, "nt", out_dtype=mxu)
    grads["a_w_out"] = matmul("a_out_dw", og_a, dh1, "tn")
    d_o_a, d_gate_a, grads["a_o_gain"] = row_vjp_call(
        "a_out_gate_bwd", _f_out_gate, og_args, [Arg(d_og_a, bc=HEAD, ph=True)], tr, nh=N_HEADS)
    dqkv_a, d_ba, d_alog, d_dtb, *received = delta_bwd(qkv_a, z_gba, ba_block, alog, dtb, states, t_invs, d_o_a, lp,
                                                        scatter=deferred.scatter_bufs(grads) if deferred else ())
    grads["a_log"], grads["a_dt_bias"] = d_alog[:, :N_HEADS], d_dtb[:, :N_HEADS]
    dz_qkv, d_conv = conv_bwd(z_qkv, y_conv, a_conv, dqkv_a, lp)
    grads["a_conv"] = d_conv.T
    dz_gba = jnp.concatenate([d_gate_a, d_ba.astype(mxu)], 1)
    grads["a_w_in"] = jnp.concatenate([matmul("a_in_qkv_dw", dz_qkv, hn, "tn"), matmul("a_in_gate_ba_dw", dz_gba, hn, "tn")[:kw + 2 * N_HEADS]], 0)
    ride = deferred.last_scatter_bufs(grads) if deferred else ((), ())
    d_hn = matmul("a_in_qkv_dx", dz_qkv, w_qkv_t, "nn", scatter=ride[0])
    if ride[0]:
        d_hn, *received_half = d_hn
        received = list(received) + received_half
    d_hn = matmul("a_in_gate_ba_dx", dz_gba, w_gba_t, "nn", res=d_hn, out_dtype=mxu, scatter=ride[1])
    if ride[1]:
        d_hn, *received_half = d_hn
        received = list(received) + received_half
    dh0, grads["a_norm"] = row_vjp_call("a_norm_bwd", lambda x_, g_: _f_rms(x_, g_) + (x_,),
                                        [Arg(h0, diff=True), Arg(a_norm, "par", diff=True)], [Arg(d_hn), Arg(dh1)], tr)
    dh0 = dh0.reshape(nb, lp, d)
    grads["meta_tokens"] = meta_grad(dh0).T
    return loss, dh0[:, LEAD:], grads, received


_SHARDED = (
    ("meta_tokens", True, False), ("a_norm", True, False), ("a_w_in", True, True), ("a_conv", True, False), ("a_w_out", False, True),
    ("kv_w_down", False, True), ("kv_w_uk", True, True), ("kv_w_uv", True, True), ("b_w_in", True, True), ("b_w_uq", True, True),
    ("b_w_out", False, True))
_REPLICATED = ("a_log", "a_dt_bias", "a_o_gain", "kv_norm", "kv_latent_norm", "k_gain", "b_norm", "b_q_latent_norm", "b_q_gain")
_ALL_WEIGHTS = ("meta_tokens", "a_norm", "a_w_in", "a_conv", "a_log", "a_dt_bias", "a_o_gain", "a_w_out", "kv_norm", "kv_w_down",
                "kv_latent_norm", "kv_w_uk", "kv_w_uv", "k_gain", "b_norm", "b_w_in", "b_q_latent_norm", "b_w_uq", "b_q_gain", "b_w_out")


def _round_up(n, m):
    return (n + m - 1) // m * m


def _pack_rows(pieces, row_multiple):
    padded = []
    for p in pieces:
        n = p.shape[-1]
        padded.append(jnp.pad(p, [(0, 0)] * (p.ndim - 1) + [(0, _round_up(n, PACK_COLS) - n)]))
    flat = jnp.concatenate(padded, -1)
    rows = _round_up(flat.shape[-1] // PACK_COLS, row_multiple)
    flat = jnp.pad(flat, [(0, 0)] * (flat.ndim - 1) + [(0, rows * PACK_COLS - flat.shape[-1])])
    return flat.reshape(flat.shape[:-1] + (rows, PACK_COLS))


def _unpack_rows(buf, sizes):
    flat = buf.reshape(buf.shape[:-2] + (-1,))
    out, off = [], 0
    for n in sizes:
        out.append(flat[..., off:off + n])
        off += _round_up(n, PACK_COLS)
    return out


def _shard_2d(a):
    return a.reshape(a.shape[-2:]) if a.ndim > 2 else a


def _kl_shard(a, by_cols):
    return _shard_2d(a).T if by_cols else _shard_2d(a)


_GROUPS_FIRST = (("a_w_in",),)
_GROUPS_LATER = (("a_w_out", "b_w_in", "b_w_out"), ("b_w_uq",), ("kv_w_down",), ("kv_w_uk", "kv_w_uv"))
_SMALL_SHARDED = ("meta_tokens", "a_norm", "a_conv")
_BY_COLS = {name: by_cols for name, by_cols, _ in _SHARDED}
ROW_ALIGN = 16


def _stack_rows(pieces):
    padded, starts, row = [], [], 0
    for p in pieces:
        r = p.shape[-2]
        padded.append(jnp.pad(p, [(0, 0)] * (p.ndim - 2) + [(0, _round_up(r, ROW_ALIGN) - r), (0, 0)]))
        starts.append(row)
        row += _round_up(r, ROW_ALIGN)
    return jnp.concatenate(padded, -2), starts


def _stack_group(arrays_by_name, names):
    arrays = [arrays_by_name[n].astype(BF16) for n in names]
    buf, starts = _stack_rows(arrays)
    return buf, [(n, s, a.shape[-2]) for n, s, a in zip(names, starts, arrays, strict=True)]


def _stack_groups(arrays_by_name, groups):
    stacked = [_stack_group(arrays_by_name, names) for names in groups]
    return [b for b, _ in stacked], [entries for _, entries in stacked]


def _full_from_gathered(gathered, layout):
    full = {}
    for got, entries in zip(gathered, layout, strict=True):
        for name, start, rows in entries:
            full[name] = got[:, start:start + rows].reshape(N_DEV * rows, got.shape[-1])
    return full


def gather_small_weights(local):
    small = [_kl_shard(local[n], _BY_COLS[n]) for n in _SMALL_SHARDED]
    (gathered,) = _exchange("all_gather", [_pack_rows([s.reshape(-1) for s in small], 8)], scatter=False)
    full = {}
    for name, part, sh in zip(_SMALL_SHARDED, _unpack_rows(gathered, [s.size for s in small]), small, strict=True):
        full[name] = part.reshape(N_DEV * sh.shape[0], sh.shape[1])
    full["a_norm"] = full["a_norm"].reshape(1, -1)
    return full


class LaterExchanges:
    def __init__(self, local):
        shards = {n: _kl_shard(local[n], _BY_COLS[n]) for names in _GROUPS_FIRST + _GROUPS_LATER for n in names}
        self.first_gather_bufs, self.first_layout = _stack_groups(shards, _GROUPS_FIRST)
        self.gather_bufs, self.layout = _stack_groups(shards, _GROUPS_LATER)

    def finish_first(self, gathered):
        return _full_from_gathered(gathered, self.first_layout)

    def finish(self, gathered):
        return _full_from_gathered(gathered, self.layout)

    def scatter_bufs(self, grads):
        return _stack_groups(_owner_slices(grads, _GROUPS_LATER), _GROUPS_LATER)[0]

    def last_scatter_bufs(self, grads):
        (buf,), self.last_layout = _stack_groups(_owner_slices(grads, _GROUPS_FIRST), _GROUPS_FIRST)
        first = buf.shape[-1] * 5 // 8 // HEAD * HEAD
        return [buf[..., :first]], [buf[..., first:]]


def _owner_slices(grads, groups):
    return {n: grads[n].reshape(N_DEV, -1, grads[n].shape[-1]) for names in groups for n in names}


def reduce_contributions(name, recv):
    _, r, c = recv.shape
    tr = max(d for d in range(8, 513, 8) if r % d == 0 and (d % ROW_ALIGN == 0 or recv.dtype == F32))

    def body(g_ref, o_ref):
        g = g_ref[0].astype(F32)
        for dev in range(1, N_DEV):
            g = g + g_ref[dev].astype(F32)
        o_ref[...] = g

    return pl.pallas_call(
        body, grid=(r // tr,), in_specs=[pl.BlockSpec((N_DEV, tr, c), lambda i: (0, i, 0))], out_specs=pl.BlockSpec((tr, c), lambda i: (i, 0)),
        out_shape=jax.ShapeDtypeStruct((r, c), F32), compiler_params=_cparams(("arbitrary",)), name=name)(recv)


def adamw_all(gs, ws, ms, vs):
    n = len(gs)

    def body(*refs):
        for i in range(n):
            g_ref, w_ref, m_ref, v_ref = (refs[j * n + i] for j in range(4))
            d_ref, mo_ref, vo_ref = (refs[(4 + j) * n + i] for j in range(3))
            g = g_ref[...]
            m_new = ADAM_B1 * m_ref[...] + (1.0 - ADAM_B1) * g
            v_new = ADAM_B2 * v_ref[...] + (1.0 - ADAM_B2) * (g * g)
            m_hat = m_new / (1.0 - ADAM_B1 ** ADAM_STEP)
            v_hat = v_new / (1.0 - ADAM_B2 ** ADAM_STEP)
            d_ref[...] = -ADAM_LR * (m_hat / (jnp.sqrt(v_hat) + ADAM_EPS) + ADAM_WD * w_ref[...])
            mo_ref[...] = m_new
            vo_ref[...] = v_new

    out = [jax.ShapeDtypeStruct(g.shape, F32) for g in gs] * 3
    res = pl.pallas_call(body, out_shape=out, compiler_params=pltpu.CompilerParams(vmem_limit_bytes=VMEM_LIMIT), name="adamw_all")(*gs, *ws, *ms, *vs)
    return res[:n], res[n:2 * n], res[2 * n:]


def kernel(x, meta_tokens, a_norm, a_w_in, a_conv, a_log, a_dt_bias, a_o_gain, a_w_out, kv_norm, kv_w_down, kv_latent_norm, kv_w_uk, kv_w_uv, k_gain, b_norm, b_w_in, b_q_latent_norm, b_w_uq, b_q_gain, b_w_out, loss_target, m_meta_tokens, m_a_norm, m_a_w_in, m_a_conv, m_a_log, m_a_dt_bias, m_a_o_gain, m_a_w_out, m_kv_norm, m_kv_w_down, m_kv_latent_norm, m_kv_w_uk, m_kv_w_uv, m_k_gain, m_b_norm, m_b_w_in, m_b_q_latent_norm, m_b_w_uq, m_b_q_gain, m_b_w_out, v_meta_tokens, v_a_norm, v_a_w_in, v_a_conv, v_a_log, v_a_dt_bias, v_a_o_gain, v_a_w_out, v_kv_norm, v_kv_w_down, v_kv_latent_norm, v_kv_w_uk, v_kv_w_uv, v_k_gain, v_b_norm, v_b_w_in, v_b_q_latent_norm, v_b_w_uq, v_b_q_gain, v_b_w_out):
    given = dict(locals())
    local_w = {n: given[n] for n in _ALL_WEIGHTS}
    full = gather_small_weights(local_w)
    for n in _REPLICATED:
        full[n] = local_w[n]
    later = LaterExchanges(local_w)

    loss_part, grad_x, grads, received_riding = local_step(x, loss_target, full, later)

    exact = [grads[n].reshape(N_DEV, -1) for n in _SMALL_SHARDED]
    exact += [jnp.broadcast_to(grads[n].reshape(1, -1), (N_DEV, grads[n].size)) for n in _REPLICATED]
    exact.append(jnp.broadcast_to(loss_part, (N_DEV, 1)))
    received = list(received_riding) + list(_exchange("all_to_all", [_pack_rows(exact, 8)], scatter=True))
    layout = later.layout + later.last_layout
    summed = [reduce_contributions(f"reduce_{i}", r) for i, r in enumerate(received)]
    n_later = len(later.layout)
    summed = summed[:n_later] + [jnp.concatenate(summed[n_later:n_later + 2], 1)] + summed[n_later + 2:]

    grad_kl = {}
    for got, entries in zip(summed, layout):
        for n, start, rows in entries:
            grad_kl[n] = got[start:start + rows]
    parts = _unpack_rows(summed[-1], [p.shape[1] for p in exact])
    for n, part in zip(_SMALL_SHARDED + _REPLICATED, parts, strict=False):
        grad_kl[n] = part
    loss = parts[-1][0]

    def natural_2d(n, a):
        shape = _shard_2d(local_w[n]).shape if local_w[n].ndim > 1 else (1, local_w[n].size)
        return a.reshape(shape[::-1]).T if _BY_COLS.get(n, False) else a.reshape(shape)

    as_2d = lambda n, a: a.reshape(natural_2d(n, grad_kl[n]).shape)
    gs = [natural_2d(n, grad_kl[n]) for n in _ALL_WEIGHTS]
    deltas, new_m, new_v = adamw_all(gs, [as_2d(n, local_w[n]) for n in _ALL_WEIGHTS], [as_2d(n, given["m_" + n]) for n in _ALL_WEIGHTS],
                                     [as_2d(n, given["v_" + n]) for n in _ALL_WEIGHTS])
    results = [a.reshape(local_w[n].shape) for group in (gs, deltas, new_m, new_v) for n, a in zip(_ALL_WEIGHTS, group, strict=True)]
    return (loss, grad_x, *results)
```

```python
import dataclasses
import functools
import math

import jax
import jax.numpy as jnp
from jax import lax
from jax.experimental import pallas as pl
from jax.experimental.pallas import tpu as pltpu

F32 = jnp.float32
BF16 = jnp.bfloat16
_MXU_DTYPE = jnp.bfloat16

N_DEV = 8
D_MODEL = 1024
N_HEADS = 8
HEAD = 128
CHUNK = 64
N_META = 16
PAD_ROWS = 2 * CHUNK - N_META
LEAD = PAD_ROWS + N_META
ROPE = 64
QK_DIM = HEAD + ROPE
QK_PAD = 2 * HEAD
KV_RANK = 256
Q_RANK = 384
CONV_K = 4
EPS = 1e-6
NEG = -1e30
ROPE_THETA = 10000.0
ADAM_LR, ADAM_B1, ADAM_B2, ADAM_EPS, ADAM_WD, ADAM_STEP = 0.001, 0.9, 0.999, 1e-08, 0.01, 10
PACK_COLS = 512
VMEM_LIMIT = 56 * 1024 * 1024


def _pick(n, options):
    for o in options:
        if n % o == 0:
            return o
    raise ValueError(f"no tile for {n} among {options}")


def _cparams(sem):
    return pltpu.CompilerParams(dimension_semantics=sem, vmem_limit_bytes=VMEM_LIMIT)


def _dims(a, dims):
    if a.ndim == 2:
        return (dims, ((), ()))
    (ca,), (cb,) = dims
    return (((ca + 1,), (cb + 1,)), ((0,), (0,)))


def _dot(a, b, dims):
    return lax.dot_general(a.astype(_MXU_DTYPE), b.astype(_MXU_DTYPE), _dims(a, dims), preferred_element_type=F32)


@jax.custom_vjp
def mm_nn(a, b):
    return _dot(a, b, ((1,), (0,)))


@jax.custom_vjp
def mm_nt(a, b):
    return _dot(a, b, ((1,), (1,)))


@jax.custom_vjp
def mm_tn(a, b):
    return _dot(a, b, ((0,), (0,)))


mm_nn.defvjp(lambda a, b: (mm_nn(a, b), (a, b)), lambda r, g: (mm_nt(g, r[1]), mm_tn(r[0], g)))
mm_nt.defvjp(lambda a, b: (mm_nt(a, b), (a, b)), lambda r, g: (mm_nn(g, r[1]), mm_tn(g, r[0])))
mm_tn.defvjp(lambda a, b: (mm_tn(a, b), (a, b)), lambda r, g: (mm_nt(r[1], g), mm_nn(r[0], g)))


def _split_terms(x, n):
    terms, rest = [], x
    for _ in range(n):
        t = rest.astype(_MXU_DTYPE)
        terms.append(t)
        rest = rest - t.astype(F32)
    return terms


def _dot_01_raw(m, x, dims):
    m = m.astype(_MXU_DTYPE)
    return sum(lax.dot_general(m, t, _dims(m, dims), preferred_element_type=F32) for t in _split_terms(x, 3))


@jax.custom_vjp
def _dot_01(m, x):
    return _dot_01_raw(m, x, ((1,), (0,)))


_dot_01.defvjp(lambda m, x: (_dot_01(m, x), m), lambda m, g: (jnp.zeros_like(m), _dot_01_raw(m, g, ((0,), (0,)))))


def _inv_unit_lower(a):
    n = a.shape[-1]
    eye = (lax.broadcasted_iota(jnp.int32, (n, n), 0) == lax.broadcasted_iota(jnp.int32, (n, n), 1)).astype(F32)
    d = lambda u, w: lax.dot_general(u, w, _dims(u, ((1,), (0,))), preferred_element_type=F32)
    t = eye - a
    p = a.astype(_MXU_DTYPE)
    p = d(p, p)
    squarings = int(math.log2(n)) - 1
    for s in range(squarings):
        ph = p.astype(_MXU_DTYPE)
        t_hi, t_lo = _split_terms(t, 2)
        t = t + (d(t_hi, ph) + d(t_lo, ph))
        if s + 1 < squarings:
            p = d(ph, ph)
    return t


@jax.custom_vjp
def _inv_lookup(a, t):
    return t


def _inv_lookup_bwd(t, g):
    return -mm_tn(t, mm_nt(g, t)), jnp.zeros_like(t)


_inv_lookup.defvjp(lambda a, t: (t, t), _inv_lookup_bwd)


def _sigmoid(x):
    return 1.0 / (1.0 + jnp.exp(-x))


@jax.custom_vjp
def _silu(x):
    return x * _sigmoid(x)


def _silu_fwd(x):
    s = _sigmoid(x)
    return x * s, (x, s)


_silu.defvjp(_silu_fwd, lambda r, g: (g * (r[1] * (1.0 + r[0] * (1.0 - r[1]))),))


def _softplus(x):
    return jnp.where(x > 20.0, x, jnp.log(1.0 + jnp.exp(jnp.minimum(x, 20.0))))


def _rms(x, g, width=None):
    ms = jnp.sum(x * x, -1, keepdims=True) / (x.shape[-1] if width is None else width)
    return x * lax.rsqrt(ms + EPS) * g


MM_VMEM_BUDGET = 40 * 1024 * 1024


def _matmul_rows(name, a, b, mode, out_dtype, res, scatter):
    m, k = a.shape
    n = b.shape[1] if mode == "nn" else b.shape[0]
    dims = {"nn": ((1,), (0,)), "nt": ((1,), (1,))}[mode]
    out_bytes = jnp.dtype(out_dtype).itemsize
    n_in, nx = 2 + (res is not None), len(scatter)

    def vmem(tm):
        blocks = 2 * tm * k * a.dtype.itemsize + 2 * k * n * b.dtype.itemsize + 2 * tm * n * out_bytes + tm * n * 4
        return blocks + (2 * tm * n * res.dtype.itemsize if res is not None else 0)

    tm = next(c for c in (2176, 1088, 512, 256, 128, 64) if m % c == 0 and vmem(c) <= MM_VMEM_BUDGET)
    steps = m // tm

    def body(*refs):
        a_ref, b_ref, o_ref = refs[0], refs[1], refs[n_in + nx]
        i = pl.program_id(0)
        finish = _ride(scatter, True, refs[n_in:n_in + nx], refs[n_in + nx + 1:n_in + 2 * nx + 1], refs[n_in + 2 * nx + 1:], i == 0, i == steps - 1)
        out = _dot(a_ref[...], b_ref[...], dims)
        if res is not None:
            out = out + refs[2][...].astype(F32)
        o_ref[...] = out.astype(o_ref.dtype)
        finish()

    o_spec = pl.BlockSpec((tm, n), lambda i: (i, 0))
    in_specs = [pl.BlockSpec((tm, k), lambda i: (i, 0)), pl.BlockSpec(b.shape, lambda i: (0, 0))] + ([o_spec] if res is not None else [])
    args = (a, b) + ((res,) if res is not None else ())
    out = pl.pallas_call(
        body, grid=(steps,), in_specs=in_specs + [_HBM] * nx, out_specs=[o_spec] + [_HBM] * nx,
        out_shape=[jax.ShapeDtypeStruct((m, n), out_dtype)] + Exchange.out_shape(scatter, True), scratch_shapes=Exchange.scratch(nx) if nx else [],
        compiler_params=_cparams(("arbitrary",) if nx else ("parallel",)), name=name)(*args, *scatter)
    return out if nx else out[0]


def matmul(name, a, b, mode, out_dtype=None, res=None, scatter=()):
    if mode != "tn":
        return _matmul_rows(name, a, b, mode, out_dtype or F32, res, scatter)
    out_dtype = out_dtype or _MXU_DTYPE
    (k, m), (k2, n) = a.shape, b.shape
    assert k == k2 and res is None, (name, a.shape, b.shape, mode)
    tm = _pick(m, (m if m <= 1536 else 1024, 1024, 512, 384, 256, 128))
    tn = _pick(n, (1024, 512, 384, 256, 128))
    tk = _pick(k, (512, 256, 128))
    nk = k // tk
    dims = ((0,), (0,))

    def body(*refs):
        if res is None:
            a_ref, b_ref, o_ref, acc_ref = refs
        else:
            a_ref, b_ref, r_ref, o_ref, acc_ref = refs
        kk = pl.program_id(2)

        @pl.when(kk == 0)
        def _():
            acc_ref[...] = jnp.zeros_like(acc_ref)

        acc_ref[...] += _dot(a_ref[...], b_ref[...], dims)

        @pl.when(kk == nk - 1)
        def _():
            out = acc_ref[...]
            if res is not None:
                out = out + r_ref[...].astype(F32)
            o_ref[...] = out.astype(o_ref.dtype)

    a_spec = pl.BlockSpec((tk, tm), lambda i, j, kk: (kk, i)) if mode == "tn" else pl.BlockSpec((tm, tk), lambda i, j, kk: (i, kk))
    b_spec = pl.BlockSpec((tn, tk), lambda i, j, kk: (j, kk)) if mode == "nt" else pl.BlockSpec((tk, tn), lambda i, j, kk: (kk, j))
    o_spec = pl.BlockSpec((tm, tn), lambda i, j, kk: (i, j))
    in_specs = [a_spec, b_spec] + ([o_spec] if res is not None else [])
    args = (a, b) + ((res,) if res is not None else ())
    return pl.pallas_call(
        body, grid=(m // tm, n // tn, nk), in_specs=in_specs, out_specs=o_spec,
        out_shape=jax.ShapeDtypeStruct((m, n), out_dtype), scratch_shapes=[pltpu.VMEM((tm, tn), F32)],
        compiler_params=_cparams(("parallel", "parallel", "arbitrary")), name=name)(*args)


@dataclasses.dataclass
class Arg:
    arr: jax.Array
    kind: str = "row"
    bc: int = 0
    base: int = 0
    ph: bool = False
    diff: bool = False
    gdt: object = F32


def _arg_spec(a, tr, nh, ntab, base=None):
    bc = a.bc or a.arr.shape[1]
    base = a.base if base is None else base
    width = bc * nh if a.ph else bc
    col = base // nh if a.ph else base
    assert not a.ph or base % nh == 0
    if a.kind == "row":
        return pl.BlockSpec((tr, width), lambda i: (i, col))
    if a.kind == "tab":
        return pl.BlockSpec((tr, width), lambda i: (i % ntab, col))
    return pl.BlockSpec((a.arr.shape[0], width), lambda i: (0, col))


def _head_view(ref, a, h, rs):
    bc = a.bc or a.arr.shape[1]
    rows = slice(None) if a.kind == "par" else rs
    v = ref[rows, h * bc:(h + 1) * bc] if a.ph else ref[rows, :]
    return v.astype(F32) if jnp.issubdtype(v.dtype, jnp.floating) else v


def row_call(name, fn, args, outs, tr, nh=1, ntab=1):
    t = args[0].arr.shape[0]
    n_in = len(args)
    out_args = [Arg(None, "row", bc, 0, ph) for (_, _, bc, ph) in outs]
    assert all(a.ph or nh == 1 for a in out_args)
    rs = slice(None)

    def body(*refs):
        for h in range(nh):
            res = fn(*[_head_view(r, a, h, rs) for r, a in zip(refs[:n_in], args, strict=True)])
            for r, a, v in zip(refs[n_in:], out_args, res, strict=True):
                r[rs, h * a.bc:(h + 1) * a.bc] = v.astype(r.dtype)

    return pl.pallas_call(
        body, grid=(t // tr,), in_specs=[_arg_spec(a, tr, nh, ntab) for a in args], out_specs=[_arg_spec(a, tr, nh, ntab) for a in out_args],
        out_shape=[jax.ShapeDtypeStruct((t, cols), dt) for (cols, dt, _, _) in outs],
        compiler_params=_cparams(("arbitrary",)), name=name)(*[a.arr for a in args])


def row_vjp_call(name, fn, args, cts, tr, nh=1, ntab=1):
    t = args[0].arr.shape[0]
    n_in, n_ct = len(args), len(cts)
    diff_idx = [k for k, a in enumerate(args) if a.diff]
    def body(*refs):
        out_refs = refs[n_in + n_ct:]
        par_sum = {}
        for k, r in zip(diff_idx, out_refs, strict=True):
            if args[k].kind == "par":
                @pl.when(pl.program_id(0) == 0)
                def _(r=r):
                    r[...] = jnp.zeros_like(r)

        for rs in (slice(None),):
            row_sum = {}
            for h in range(nh):
                vals = [_head_view(r, a, h, rs) for r, a in zip(refs[:n_in], args, strict=True)]
                ct_vals = tuple(_head_view(r, a, h, rs) for r, a in zip(refs[n_in:n_in + n_ct], cts, strict=True))

                def f(*dv, vals=vals):
                    full = list(vals)
                    for k, v in zip(diff_idx, dv, strict=True):
                        full[k] = v
                    return tuple(fn(*full))

                _, vjp = jax.vjp(f, *[vals[k] for k in diff_idx])
                for j, (k, r, g) in enumerate(zip(diff_idx, out_refs, vjp(ct_vals), strict=True)):
                    a = args[k]
                    bc = a.bc or a.arr.shape[1]
                    if a.kind == "row" and a.ph:
                        r[rs, h * bc:(h + 1) * bc] = g.astype(r.dtype)
                    elif a.kind == "row":
                        row_sum[j] = g if j not in row_sum else row_sum[j] + g
                    else:
                        key = (j, h if a.ph else 0)
                        par_sum[key] = g if key not in par_sum else par_sum[key] + g
            for j, g in row_sum.items():
                out_refs[j][rs, :] = g.astype(out_refs[j].dtype)
        for (j, h), g in par_sum.items():
            bc = g.shape[1]
            out_refs[j][:, h * bc:(h + 1) * bc] += g

    out_specs, out_shape = [], []
    for k in diff_idx:
        a = args[k]
        bc = a.bc or a.arr.shape[1]
        out_specs.append(_arg_spec(a, tr, nh, ntab, base=0))
        out_shape.append(jax.ShapeDtypeStruct((t if a.kind == "row" else a.arr.shape[0], bc * (nh if a.ph else 1)), a.gdt if a.kind == "row" else F32))
    in_specs = [_arg_spec(a, tr, nh, ntab) for a in list(args) + list(cts)]
    return pl.pallas_call(
        body, grid=(t // tr,), in_specs=in_specs, out_specs=out_specs, out_shape=out_shape,
        compiler_params=_cparams(("arbitrary",)), name=name)(*[a.arr for a in list(args) + list(cts)])


def _conv_taps(x, w):
    rows = lax.broadcasted_iota(jnp.int32, x.shape, 0)
    y = x * w[CONV_K - 1:CONV_K, :]
    for s in range(1, CONV_K):
        y = y + jnp.where(rows >= s, pltpu.roll(x, s, 0), 0.0) * w[CONV_K - 1 - s:CONV_K - s, :]
    return y


CONV_HEADS = 4
CONV_BLOCKS_PER_THIRD = N_HEADS // CONV_HEADS


def _conv_post(y, block):
    a = _silu(y)
    normed = block < 2 * CONV_BLOCKS_PER_THIRD
    scale = jnp.where(block < CONV_BLOCKS_PER_THIRD, HEAD ** -0.5, 1.0)
    return a * jnp.where(normed, lax.rsqrt(jnp.sum(a * a, -1, keepdims=True) + EPS) * scale, 1.0)


def conv_fwd(z, w, lp):
    t, width = z.shape
    cols = CONV_HEADS * HEAD

    def body(z_ref, w_ref, o_ref, y_ref):
        block = pl.program_id(1)
        for h in range(CONV_HEADS):
            cs = slice(h * HEAD, (h + 1) * HEAD)
            y = _conv_taps(z_ref[:, cs].astype(F32), w_ref[:, cs])
            y_ref[:, cs] = y.astype(y_ref.dtype)
            o_ref[:, cs] = _conv_post(y, block)

    blk = pl.BlockSpec((lp, cols), lambda b, j: (b, j))
    out = jax.ShapeDtypeStruct((t, width), F32)
    return pl.pallas_call(
        body, grid=(t // lp, width // cols), in_specs=[blk, pl.BlockSpec((CONV_K, cols), lambda b, j: (0, j))],
        out_specs=[blk, blk], out_shape=[out, jax.ShapeDtypeStruct((t, width), _MXU_DTYPE)],
        compiler_params=_cparams(("arbitrary", "arbitrary")), name="a_conv_fwd")(z, w)


def conv_bwd(z, y, w, dout, lp):
    t, width = z.shape
    cols = CONV_HEADS * HEAD

    def body(z_ref, y_ref, w_ref, g_ref, dz_ref, dw_ref):
        block = pl.program_id(0)

        @pl.when(pl.program_id(1) == 0)
        def _():
            dw_ref[...] = jnp.zeros_like(dw_ref)

        for h in range(CONV_HEADS):
            cs = slice(h * HEAD, (h + 1) * HEAD)
            x, wv = z_ref[:, cs].astype(F32), w_ref[:, cs]
            _, vjp = jax.vjp(lambda y_: _conv_post(y_, block), y_ref[:, cs].astype(F32))
            (dy,) = vjp(g_ref[:, cs])
            rows = lax.broadcasted_iota(jnp.int32, x.shape, 0)
            dx = dy * wv[CONV_K - 1:CONV_K, :]
            dw_ref[CONV_K - 1:CONV_K, cs] += jnp.sum(dy * x, axis=0, keepdims=True)
            for s in range(1, CONV_K):
                dy_up = jnp.where(rows < lp - s, pltpu.roll(dy, lp - s, 0), 0.0)
                dx = dx + dy_up * wv[CONV_K - 1 - s:CONV_K - s, :]
                dw_ref[CONV_K - 1 - s:CONV_K - s, cs] += jnp.sum(dy_up * x, axis=0, keepdims=True)
            dz_ref[:, cs] = dx.astype(dz_ref.dtype)

    blk = pl.BlockSpec((lp, cols), lambda j, b: (b, j))
    w_blk = pl.BlockSpec((CONV_K, cols), lambda j, b: (0, j))
    return pl.pallas_call(
        body, grid=(width // cols, t // lp), in_specs=[blk, blk, w_blk, blk], out_specs=[blk, w_blk],
        out_shape=[jax.ShapeDtypeStruct((t, width), _MXU_DTYPE), jax.ShapeDtypeStruct((CONV_K, width), F32)],
        compiler_params=_cparams(("arbitrary", "arbitrary")), name="a_conv_bwd")(z, y, w, dout)


def _delta_chunk(q, k, v, ba, alog, dtb, state, t_stored):
    n_g, c = q.shape[0], q.shape[1]
    lane = lax.broadcasted_iota(jnp.int32, (1, HEAD), 1)

    def pick(xs, offset):
        cols = [jnp.sum(xs[i // N_HEADS if len(xs) > 1 else 0] * (lane == offset + i % N_HEADS).astype(F32), axis=1, keepdims=True)[None]
                for i in range(n_g)]
        return jnp.concatenate(cols, 0)

    b_raw, a_raw = pick(ba, 0), pick(ba, N_HEADS)
    a_log, dt_bias = pick((alog,), 0), pick((dtb,), 0)
    beta = _sigmoid(b_raw)
    g = -jnp.exp(a_log) * _softplus(a_raw + dt_bias)
    ri = lax.broadcasted_iota(jnp.int32, (c, c), 0)
    ci = lax.broadcasted_iota(jnp.int32, (c, c), 1)
    tril = ci <= ri
    lower = jnp.broadcast_to(tril.astype(F32), (n_g, c, c))
    gc_col = _dot_01(lower, g * jnp.ones((1, 1, HEAD), F32))[:, :, :1]
    gc_row = _dot_01(jnp.ones((n_g, 8, c), F32), g * (ri <= ci).astype(F32)[None])[:, 0:1, :]
    gc_last = jnp.sum(g, axis=1, keepdims=True)
    decay = jnp.exp(jnp.where(tril, gc_col - gc_row, NEG))
    e_gc = jnp.exp(gc_col)
    kb = k * beta
    a_mat = jnp.where(ci < ri, mm_nt(kb, k) * decay, 0.0)
    t_inv = _inv_unit_lower(a_mat) if t_stored is None else _inv_lookup(a_mat, t_stored)
    u_base = mm_nn(t_inv, v * beta)
    w_dec = mm_nn(t_inv, kb * e_gc)
    attn = jnp.where(tril, mm_nt(q, k) * decay, 0.0)
    u = u_base - mm_nn(w_dec, state)
    o = mm_nn(q * e_gc, state) + mm_nn(attn, u)
    new_state = state * jnp.exp(gc_last) + mm_tn(k * jnp.exp(gc_last - gc_col), u)
    return o, new_state, t_inv


DELTA_STEP_FWD = (4, 2)
DELTA_STEP_BWD = (2, 2)


def _heads_of(ref, rs, first_col):
    return jnp.stack([ref[i // N_HEADS, rs, first_col + (i % N_HEADS) * HEAD:first_col + (i % N_HEADS + 1) * HEAD]
                      for i in range(ref.shape[0] * N_HEADS)])


def _qkv_heads(ref, rs, part):
    return _heads_of(ref, rs, part * N_HEADS * HEAD)


def _by_sequence(a, lp):
    return a.reshape(a.shape[0] // lp, lp, a.shape[1])


def _ride(bufs, scatter, refs_in, refs_out, sems, first, last, two_level=False):
    if not bufs:
        return lambda: None
    make = lambda: (TwoLevelGather if two_level else Exchange)(refs_in, refs_out, *sems, scatter)

    @pl.when(first)
    def _():
        make().start()

    def finish():
        @pl.when(last)
        def _():
            make().wait()

    return finish


def delta_fwd(qkv, ba, ba_block, alog, dtb, lp, gather=()):
    t = qkv.shape[0]
    nb, nc = t // lp, lp // CHUNK
    seqs, cps = DELTA_STEP_FWD
    ng, rows = nc // cps, cps * CHUNK
    nx = len(gather)
    nbg = nb // seqs
    assert nc % cps == 0 and nb % seqs == 0

    def body(*refs):
        qkv_ref, ba_ref, al_ref, dt_ref = refs[:4]
        o_ref, s_ref, t_ref = refs[4 + nx:7 + nx]
        state_ref = refs[7 + 2 * nx]
        b, n = pl.program_id(0), pl.program_id(1)
        finish = _ride(gather, False, refs[4:4 + nx], refs[7 + nx:7 + 2 * nx], refs[8 + 2 * nx:], (b == 0) & (n == 0), (b == nbg - 1) & (n == ng - 1))

        @pl.when(n == 0)
        def _():
            state_ref[...] = jnp.zeros_like(state_ref)

        al, dtv = al_ref[...], dt_ref[...]
        for c in range(cps):
            rs = slice(c * CHUNK, (c + 1) * CHUNK)
            state = state_ref[...]
            o, new_state, t_inv = _delta_chunk(_qkv_heads(qkv_ref, rs, 0), _qkv_heads(qkv_ref, rs, 1), _qkv_heads(qkv_ref, rs, 2),
                                               tuple(ba_ref[i, rs, :] for i in range(seqs)), al, dtv, state, None)
            for i in range((seqs * N_HEADS)):
                seq, g = divmod(i, N_HEADS)
                o_ref[seq, rs, g * HEAD:(g + 1) * HEAD] = o[i]
                s_ref[seq, g, c] = state[i]
                t_ref[seq, g, c] = t_inv[i]
            state_ref[...] = new_state
        finish()

    rows_of = lambda width: pl.BlockSpec((seqs, rows, width), lambda b, n: (b, n, 0))
    par_spec = pl.BlockSpec((1, HEAD), lambda b, n: (0, 0))
    out = pl.pallas_call(
        body, grid=(nbg, ng),
        in_specs=[rows_of(3 * N_HEADS * HEAD), pl.BlockSpec((seqs, rows, HEAD), lambda b, n: (b, n, ba_block)), par_spec, par_spec] + [_HBM] * nx,
        out_specs=[rows_of(N_HEADS * HEAD), pl.BlockSpec((seqs, N_HEADS, cps, HEAD, HEAD), lambda b, n: (b, 0, n, 0, 0)),
                   pl.BlockSpec((seqs, N_HEADS, cps, CHUNK, CHUNK), lambda b, n: (b, 0, n, 0, 0))] + [_HBM] * nx,
        out_shape=[jax.ShapeDtypeStruct((nb, lp, N_HEADS * HEAD), F32), jax.ShapeDtypeStruct((nb, N_HEADS, nc, HEAD, HEAD), F32),
                   jax.ShapeDtypeStruct((nb, N_HEADS, nc, CHUNK, CHUNK), F32)] + Exchange.out_shape(gather, False),
        scratch_shapes=[pltpu.VMEM(((seqs * N_HEADS), HEAD, HEAD), F32)] + (Exchange.scratch(nx) if nx else []),
        compiler_params=_cparams(("arbitrary", "arbitrary")), name="delta_fwd")(_by_sequence(qkv, lp), _by_sequence(ba, lp), alog, dtb, *gather)
    return [out[0].reshape(t, N_HEADS * HEAD)] + list(out[1:])


def delta_bwd(qkv, ba, ba_block, alog, dtb, states, t_invs, do, lp, scatter=()):
    t = qkv.shape[0]
    nb, nc = t // lp, lp // CHUNK
    seqs, cps = DELTA_STEP_BWD
    ng, rows = nc // cps, cps * CHUNK
    nx = len(scatter)
    nbg = nb // seqs

    def body(*refs):
        qkv_ref, ba_ref, al_ref, dt_ref, s_ref, t_ref, do_ref = refs[:7]
        dqkv_ref, dba_ref, dal_ref, ddt_ref = refs[7 + nx:11 + nx]
        dstate_ref = refs[11 + 2 * nx]
        b, step = pl.program_id(0), pl.program_id(1)
        finish = _ride(scatter, True, refs[7:7 + nx], refs[11 + nx:11 + 2 * nx], refs[12 + 2 * nx:], (b == 0) & (step == 0),
                       (b == nbg - 1) & (step == ng - 1))

        @pl.when(step == 0)
        def _():
            dstate_ref[...] = jnp.zeros_like(dstate_ref)

        @pl.when((b == 0) & (step == 0))
        def _():
            dal_ref[...] = jnp.zeros_like(dal_ref)
            ddt_ref[...] = jnp.zeros_like(ddt_ref)

        al, dtv = al_ref[...], dt_ref[...]
        d_al = jnp.zeros((1, HEAD), F32)
        d_dt = jnp.zeros((1, HEAD), F32)
        for c in reversed(range(cps)):
            rs = slice(c * CHUNK, (c + 1) * CHUNK)
            t_n = jnp.stack([t_ref[i // N_HEADS, i % N_HEADS, c] for i in range((seqs * N_HEADS))])
            s_n = jnp.stack([s_ref[i // N_HEADS, i % N_HEADS, c] for i in range((seqs * N_HEADS))])

            def f(q_, k_, v_, ba_, al_, dt_, s_, t_n=t_n):
                return _delta_chunk(q_, k_, v_, ba_, al_, dt_, s_, t_n)[:2]

            _, vjp = jax.vjp(f, _qkv_heads(qkv_ref, rs, 0), _qkv_heads(qkv_ref, rs, 1), _qkv_heads(qkv_ref, rs, 2), tuple(ba_ref[i, rs, :] for i in range(seqs)), al, dtv, s_n)
            grads = vjp((_heads_of(do_ref, rs, 0), dstate_ref[...]))
            for part in range(3):
                for i in range((seqs * N_HEADS)):
                    col = (part * N_HEADS + i % N_HEADS) * HEAD
                    dqkv_ref[i // N_HEADS, rs, col:col + HEAD] = grads[part][i]
            for i in range(seqs):
                dba_ref[i, rs, :] = grads[3][i]
            d_al, d_dt = d_al + grads[4], d_dt + grads[5]
            dstate_ref[...] = grads[6]
        dal_ref[...] += d_al
        ddt_ref[...] += d_dt
        finish()

    rows_of = lambda width: pl.BlockSpec((seqs, rows, width), lambda b, n: (b, ng - 1 - n, 0))
    par_spec = pl.BlockSpec((1, HEAD), lambda b, n: (0, 0))
    out = pl.pallas_call(
        body, grid=(nbg, ng),
        in_specs=[rows_of(3 * N_HEADS * HEAD), pl.BlockSpec((seqs, rows, HEAD), lambda b, n: (b, ng - 1 - n, ba_block)), par_spec, par_spec,
                  pl.BlockSpec((seqs, N_HEADS, cps, HEAD, HEAD), lambda b, n: (b, 0, ng - 1 - n, 0, 0)),
                  pl.BlockSpec((seqs, N_HEADS, cps, CHUNK, CHUNK), lambda b, n: (b, 0, ng - 1 - n, 0, 0)), rows_of(N_HEADS * HEAD)] + [_HBM] * nx,
        out_specs=[rows_of(3 * N_HEADS * HEAD), rows_of(HEAD), par_spec, par_spec] + [_HBM] * nx,
        out_shape=[jax.ShapeDtypeStruct((nb, lp, 3 * N_HEADS * HEAD), F32), jax.ShapeDtypeStruct((nb, lp, HEAD), F32),
                   jax.ShapeDtypeStruct((1, HEAD), F32), jax.ShapeDtypeStruct((1, HEAD), F32)] + Exchange.out_shape(scatter, True),
        scratch_shapes=[pltpu.VMEM(((seqs * N_HEADS), HEAD, HEAD), F32)] + (Exchange.scratch(nx) if nx else []),
        compiler_params=_cparams(("arbitrary", "arbitrary")), name="delta_bwd")(
            _by_sequence(qkv, lp), _by_sequence(ba, lp), alog, dtb, states, t_invs, _by_sequence(do, lp), *scatter)
    return [out[0].reshape(t, 3 * N_HEADS * HEAD), out[1].reshape(t, HEAD)] + list(out[2:])


ATT_Q_TILE = 256
ATT_K_TILE = 512
ATT_SCALE = QK_DIM ** -0.5


def _tiles(end, size):
    return [(s, min(s + size, end)) for s in range(0, end, size)]


def _att_visible(q0, q1, k0, k1, keys_first):
    if k1 <= q0 + CHUNK and k0 >= PAD_ROWS:
        return None
    shape = (k1 - k0, q1 - q0) if keys_first else (q1 - q0, k1 - k0)
    qpos = q0 + lax.broadcasted_iota(jnp.int32, shape, 1 if keys_first else 0)
    kpos = k0 + lax.broadcasted_iota(jnp.int32, shape, 0 if keys_first else 1)
    shift = CHUNK.bit_length() - 1
    return (jnp.right_shift(kpos, shift) <= jnp.right_shift(qpos, shift)) & (kpos >= PAD_ROWS)


def _att_seq_specs(lp):
    return pl.BlockSpec((lp, QK_PAD), lambda b, h: (b, h)), pl.BlockSpec((lp, HEAD), lambda b, h: (b, h))


def flash_fwd(q, k, v, lp):
    t = q.shape[0]
    qk_seq, o_seq = _att_seq_specs(lp)

    def body(q_ref, k_ref, v_ref, o_ref, lse_ref):
        q_tiles = _tiles(lp, ATT_Q_TILE)

        def score_steps(q0, q1, out):
            def step(k0, k1):
                s = mm_nt(q_ref[q0:q1, :], k_ref[k0:k1, :])
                vis = _att_visible(q0, q1, k0, k1, False)
                s = s if vis is None else jnp.where(vis, s, NEG)
                out["scores"].append(s)
                row_max = jnp.max(s, -1, keepdims=True)
                out["m"] = row_max if out["m"] is None else jnp.maximum(out["m"], row_max)
            return [functools.partial(step, k0, k1) for k0, k1 in _tiles(q1, ATT_K_TILE)]

        cur = {"scores": [], "m": None}
        for step in score_steps(*q_tiles[0], cur):
            step()
        for i, (q0, q1) in enumerate(q_tiles):
            nxt = {"scores": [], "m": None}
            ahead = score_steps(*q_tiles[i + 1], nxt) if i + 1 < len(q_tiles) else []
            l = jnp.zeros((q1 - q0, 1), F32)
            acc = jnp.zeros((q1 - q0, HEAD), F32)
            for s, (k0, k1) in zip(cur["scores"], _tiles(q1, ATT_K_TILE), strict=True):
                if ahead:
                    ahead.pop(0)()
                p = jnp.exp2(s - cur["m"])
                l = l + jnp.sum(p, -1, keepdims=True)
                acc = acc + mm_nn(p, v_ref[k0:k1, :])
            for step in ahead:
                step()
            o_ref[q0:q1, :] = acc / l
            lse_ref[q0:q1, :] = jnp.broadcast_to(cur["m"] + jnp.log2(l), (q1 - q0, HEAD))
            cur = nxt

    big = jax.ShapeDtypeStruct((t, N_HEADS * HEAD), F32)
    return pl.pallas_call(
        body, grid=(t // lp, N_HEADS), in_specs=[qk_seq, qk_seq, o_seq], out_specs=[o_seq, o_seq], out_shape=[big, big],
        compiler_params=_cparams(("arbitrary", "arbitrary")), name="flash_fwd")(q, k, v)


def flash_bwd(q, k, v, o, lse, do, lp):
    t = q.shape[0]
    qk_seq, o_seq = _att_seq_specs(lp)

    def body(q_ref, k_ref, v_ref, o_ref, lse_ref, do_ref, dq_ref, dk_out_ref, dv_out_ref, dk_ref, dv_ref):
        dk_ref[...] = jnp.zeros_like(dk_ref)
        dv_ref[...] = jnp.zeros_like(dv_ref)
        for q0, q1 in _tiles(lp, ATT_Q_TILE):
            qb, dob = q_ref[q0:q1, :], do_ref[q0:q1, :]
            lse_row = jnp.transpose(lse_ref[q0:q1, :])[0:1, :]
            dob_ln2 = dob * math.log(2.0)
            dsum_row = jnp.sum(jnp.transpose(dob_ln2 * o_ref[q0:q1, :]), axis=0, keepdims=True)
            dq = jnp.zeros((q1 - q0, QK_PAD), F32)
            for k0, k1 in _tiles(q1, ATT_K_TILE):
                kb, vb = k_ref[k0:k1, :], v_ref[k0:k1, :]
                s = mm_nt(kb, qb)
                vis = _att_visible(q0, q1, k0, k1, True)
                s = s if vis is None else jnp.where(vis, s, NEG)
                p = jnp.exp2(s - lse_row)
                ds = p * (mm_nt(vb, dob_ln2) - dsum_row)
                dv_ref[k0:k1, :] += mm_nn(p, dob)
                dk_ref[k0:k1, :] += mm_nn(ds, qb)
                dq = dq + mm_tn(ds, kb)
            dq_ref[q0:q1, :] = dq.astype(dq_ref.dtype)
        dk_out_ref[...] = dk_ref[...].astype(dk_out_ref.dtype)
        dv_out_ref[...] = dv_ref[...].astype(dv_out_ref.dtype)

    narrow = _MXU_DTYPE
    return pl.pallas_call(
        body, grid=(t // lp, N_HEADS), in_specs=[qk_seq, qk_seq, o_seq, o_seq, o_seq, o_seq], out_specs=[qk_seq, qk_seq, o_seq],
        out_shape=[jax.ShapeDtypeStruct((t, N_HEADS * QK_PAD), narrow), jax.ShapeDtypeStruct((t, N_HEADS * QK_PAD), narrow),
                   jax.ShapeDtypeStruct((t, N_HEADS * HEAD), narrow)],
        scratch_shapes=[pltpu.VMEM((lp, QK_PAD), F32), pltpu.VMEM((lp, HEAD), F32)],
        compiler_params=_cparams(("arbitrary", "arbitrary")), name="flash_bwd")(q, k, v, o, lse, do)


def loss_head(h2, target, lp):
    nb, seq, d = target.shape
    cols = _pick(d, (512, 128))
    ncol = d // cols

    def body(h_ref, t_ref, loss_ref, dh_ref, acc_ref):
        b, j = pl.program_id(0), pl.program_id(1)

        @pl.when((b == 0) & (j == 0))
        def _():
            acc_ref[...] = jnp.zeros_like(acc_ref)

        err = h_ref[LEAD:, :] - t_ref[...]
        dh_ref[:LEAD, :] = jnp.zeros((LEAD, cols), F32)
        dh_ref[LEAD:, :] = err * (1.0 / d)
        acc_ref[...] += jnp.sum(err * err, axis=0, keepdims=True)

        @pl.when((b == nb - 1) & (j == ncol - 1))
        def _():
            loss_ref[...] = jnp.sum(acc_ref[...], axis=1, keepdims=True) * (0.5 / d)

    return pl.pallas_call(
        body, grid=(nb, ncol),
        in_specs=[pl.BlockSpec((None, lp, cols), lambda b, j: (b, 0, j)), pl.BlockSpec((None, seq, cols), lambda b, j: (b, 0, j))],
        out_specs=[pl.BlockSpec((1, 1), lambda b, j: (0, 0)), pl.BlockSpec((None, lp, cols), lambda b, j: (b, 0, j))],
        out_shape=[jax.ShapeDtypeStruct((1, 1), F32), jax.ShapeDtypeStruct((nb, lp, d), F32)],
        scratch_shapes=[pltpu.VMEM((1, cols), F32)], compiler_params=_cparams(("arbitrary", "arbitrary")), name="loss_head")(h2, target)


def gated_out(name, o, gate, gain, w, res):
    t, kw = o.shape
    d = w.shape[1]
    tm = _pick(t, (512, 256, 128))

    def body(*refs):
        o_ref, gate_ref = refs[:2]
        w_ref, r_ref, h_ref, g_ref = refs[-4:]
        if gain is None:
            g_ref[...] = _f_gate(o_ref[...], gate_ref[...])[0].astype(g_ref.dtype)
        else:
            for h in range(N_HEADS):
                cs = slice(h * HEAD, (h + 1) * HEAD)
                g_ref[:, cs] = _f_out_gate(o_ref[:, cs], gate_ref[:, cs], refs[2][...])[0].astype(g_ref.dtype)
        h_ref[...] = r_ref[...] + _dot(g_ref[...], w_ref[...], ((1,), (0,)))

    rows = lambda width: pl.BlockSpec((tm, width), lambda i: (i, 0))
    whole = lambda a: pl.BlockSpec(a.shape, lambda i: (0, 0))
    params = [] if gain is None else [gain]
    return pl.pallas_call(
        body, grid=(t // tm,), in_specs=[rows(kw), rows(kw)] + [whole(p) for p in params] + [whole(w), rows(d)], out_specs=[rows(d), rows(kw)],
        out_shape=[jax.ShapeDtypeStruct((t, d), F32), jax.ShapeDtypeStruct((t, kw), _MXU_DTYPE)],
        compiler_params=_cparams(("parallel",)), name=name)(o, gate, *params, w, res)


def embed_norm(x, meta, gain, lp, gather=()):
    nb, seq, d = x.shape
    nblk, nx = lp // LEAD, len(gather)

    def body(*refs):
        x_ref, meta_ref, g_ref = refs[:3]
        h_ref, hn_ref = refs[3 + nx:5 + nx]
        b, i = pl.program_id(0), pl.program_id(1)
        finish = _ride(gather, False, refs[3:3 + nx], refs[5 + nx:5 + 2 * nx], refs[5 + 2 * nx:], (b == 0) & (i == 0), (b == nb - 1) & (i == nblk - 1),
                       two_level=True)

        @pl.when(i == 0)
        def _():
            h_ref[:PAD_ROWS, :] = jnp.zeros((PAD_ROWS, d), F32)
            h_ref[PAD_ROWS:, :] = meta_ref[...]

        @pl.when(i > 0)
        def _():
            h_ref[...] = x_ref[...]

        hn_ref[...] = _rms(h_ref[...], g_ref[...]).astype(hn_ref.dtype)
        finish()

    rows = pl.BlockSpec((LEAD, d), lambda b, i: (b * nblk + i, 0))
    out = pl.pallas_call(
        body, grid=(nb, nblk),
        in_specs=[pl.BlockSpec((None, LEAD, d), lambda b, i: (b, jnp.maximum(i - 1, 0), 0)), pl.BlockSpec((N_META, d), lambda b, i: (0, 0)),
                  pl.BlockSpec((1, d), lambda b, i: (0, 0))] + [_HBM] * nx,
        out_specs=[rows, rows] + [_HBM] * nx,
        out_shape=[jax.ShapeDtypeStruct((nb * lp, d), F32), jax.ShapeDtypeStruct((nb * lp, d), _MXU_DTYPE)] + Exchange.out_shape(gather, False),
        scratch_shapes=Exchange.scratch(nx) if nx else [],
        compiler_params=_cparams(("arbitrary", "arbitrary")), name="embed_norm")(x, meta, gain, *gather)
    return list(out)


def meta_grad(dh0):
    nb, _, d = dh0.shape

    def body(g_ref, o_ref):
        @pl.when(pl.program_id(0) == 0)
        def _():
            o_ref[...] = jnp.zeros_like(o_ref)

        o_ref[...] += g_ref[PAD_ROWS:LEAD, :]

    return pl.pallas_call(
        body, grid=(nb,), in_specs=[pl.BlockSpec((None, LEAD, d), lambda b: (b, 0, 0))],
        out_specs=pl.BlockSpec((N_META, d), lambda b: (0, 0)), out_shape=jax.ShapeDtypeStruct((N_META, d), F32),
        compiler_params=_cparams(("arbitrary",)), name="meta_grad")(dh0)


_HBM = pl.BlockSpec(memory_space=pltpu.HBM)


def _mesh_pos():
    x, y, c = lax.axis_index("x"), lax.axis_index("y"), lax.axis_index("c")
    return x, y, c


def _peer(x, y, c, k):
    px = 1 - x if k & 4 else x
    py = 1 - y if k & 2 else y
    pc = 1 - c if k & 1 else c
    return (px, py, pc), 4 * px + 2 * py + pc


class Exchange:
    def __init__(self, x_refs, out_refs, send_sems, recv_sems, local_sems, scatter):
        self.x_refs, self.out_refs, self.scatter = x_refs, out_refs, scatter
        self.send_sems, self.recv_sems, self.local_sems = send_sems, recv_sems, local_sems
        self.pos = _mesh_pos()
        x, y, c = self.pos
        self.me = 4 * x + 2 * y + c

    @staticmethod
    def scratch(n):
        return [pltpu.SemaphoreType.DMA((n, N_DEV - 1)), pltpu.SemaphoreType.DMA((n, N_DEV - 1)), pltpu.SemaphoreType.DMA((n,))]

    @staticmethod
    def out_shape(bufs, scatter):
        return [jax.ShapeDtypeStruct(b.shape if scatter else (N_DEV,) + b.shape, b.dtype) for b in bufs]

    def _local(self, i):
        return pltpu.make_async_copy(self.x_refs[i].at[self.me] if self.scatter else self.x_refs[i], self.out_refs[i].at[self.me], self.local_sems.at[i])

    def _copy(self, i, k, landing):
        peer, peer_id = _peer(*self.pos, k)
        src = self.x_refs[i].at[peer_id] if self.scatter else self.x_refs[i]
        return pltpu.make_async_remote_copy(src_ref=src, dst_ref=self.out_refs[i].at[peer_id if landing else self.me],
                                            send_sem=self.send_sems.at[i, k - 1], recv_sem=self.recv_sems.at[i, k - 1],
                                            device_id=peer, device_id_type=pl.DeviceIdType.MESH)

    def start(self):
        for i in range(len(self.x_refs)):
            self._local(i).start()
        for k in range(1, N_DEV):
            for i in range(len(self.x_refs)):
                self._copy(i, k, False).start()

    def wait(self):
        for k in range(1, N_DEV):
            for i in range(len(self.x_refs)):
                self._copy(i, k, True).wait_recv()
        for k in range(1, N_DEV):
            for i in range(len(self.x_refs)):
                self._copy(i, k, False).wait_send()
        for i in range(len(self.x_refs)):
            self._local(i).wait()


class TwoLevelGather(Exchange):
    DIRECT = (1, 4, 2, 6)
    FROM_CHIPS = (4, 2, 6)

    def _forward(self, i, k):
        _, origin = _peer(*self.pos, k)
        sibling, _ = _peer(*self.pos, 1)
        block = self.out_refs[i].at[origin]
        return pltpu.make_async_remote_copy(src_ref=block, dst_ref=block, send_sem=self.send_sems.at[i, (k ^ 1) - 1],
                                            recv_sem=self.recv_sems.at[i, (k ^ 1) - 1], device_id=sibling, device_id_type=pl.DeviceIdType.MESH)

    def start(self):
        assert not self.scatter
        for i in range(len(self.x_refs)):
            self._local(i).start()
        for k in self.DIRECT:
            for i in range(len(self.x_refs)):
                self._copy(i, k, False).start()

    def wait(self):
        n = range(len(self.x_refs))
        for k in self.FROM_CHIPS:
            for i in n:
                self._copy(i, k, True).wait_recv()
                self._forward(i, k).start()
        for k in (1, 5, 3, 7):
            for i in n:
                self._copy(i, k, True).wait_recv()
        for k in self.DIRECT:
            for i in n:
                self._copy(i, k, False).wait_send()
        for k in self.FROM_CHIPS:
            for i in n:
                self._forward(i, k).wait_send()
        for i in n:
            self._local(i).wait()


def _exchange(name, bufs, scatter):
    n = len(bufs)

    def body(*refs):
        ex = Exchange(refs[:n], refs[n:2 * n], *refs[2 * n:], scatter)
        ex.start()
        ex.wait()

    return pl.pallas_call(body, in_specs=[_HBM] * n, out_specs=[_HBM] * n, out_shape=Exchange.out_shape(bufs, scatter),
                          scratch_shapes=Exchange.scratch(n), name=name)(*bufs)


def _f_rms(x, g):
    return (_rms(x, g),)


def _f_rms2(x, g1, g2):
    r = x * lax.rsqrt(jnp.sum(x * x, -1, keepdims=True) / x.shape[-1] + EPS)
    return r * g1, r * g2


@jax.custom_vjp
def _out_gate(o, gate, gain):
    return _rms(o, gain) * _silu(gate)


def _out_gate_bwd(res, g):
    o, gate, gain = res
    r = lax.rsqrt(jnp.sum(o * o, -1, keepdims=True) / o.shape[-1] + EPS)
    n = o * r
    s = _sigmoid(gate)
    g_norm = g * (gate * s)
    d_gate = g * (n * gain) * (s * (1.0 + gate * (1.0 - s)))
    gn = g_norm * gain
    d_o = r * (gn - n * (jnp.sum(gn * n, -1, keepdims=True) / o.shape[-1]))
    return d_o, d_gate, jnp.sum(g_norm * n, 0, keepdims=True)


_out_gate.defvjp(lambda o, gate, gain: (_out_gate(o, gate, gain), (o, gate, gain)), _out_gate_bwd)


def _f_out_gate(o, gate, gain):
    return (_out_gate(o, gate, gain),)


def _f_gate(o, gate):
    return (o * _silu(gate),)


def _swap_rope_halves(x):
    return pltpu.roll(x, ROPE // 2, 1) + pltpu.roll(x, HEAD - ROPE // 2, 1)


def _qk_final_inv_rms(nope, rope_in):
    ms = (jnp.sum(nope * nope, -1, keepdims=True) + jnp.sum(rope_in * rope_in, -1, keepdims=True)) / QK_DIM
    return lax.rsqrt(ms + EPS)


@functools.partial(jax.custom_vjp, nondiff_argnums=(0,))
def _qk_final(scale, nope, rope_in, g_nope, g_rope, cos, sin):
    r = _qk_final_inv_rms(nope, rope_in)
    b = rope_in * (r * g_rope)
    out = jnp.concatenate([nope * (r * g_nope), b * cos + _swap_rope_halves(b) * sin], axis=1)
    return out if scale == 1.0 else out * scale


def _qk_final_fwd(scale, nope, rope_in, g_nope, g_rope, cos, sin):
    return _qk_final(scale, nope, rope_in, g_nope, g_rope, cos, sin), (nope, rope_in, g_nope, g_rope, cos, sin)


def _qk_final_bwd(scale, res, g):
    nope, rope_in, g_nope, g_rope, cos, sin = res
    r = _qk_final_inv_rms(nope, rope_in)
    ga, gb = g[:, :HEAD], g[:, HEAD:]
    if scale != 1.0:
        ga, gb = ga * scale, gb * scale
    db = gb * cos + _swap_rope_halves(gb * sin)
    t_a, t_b = ga * nope, db * rope_in
    d_r = jnp.sum(t_a * g_nope + t_b * g_rope, -1, keepdims=True)
    c = d_r * (r * r * r) * (-1.0 / QK_DIM)
    d_nope = ga * (r * g_nope) + nope * c
    d_rope = db * (r * g_rope) + rope_in * c
    d_g_nope = jnp.sum(t_a * r, 0, keepdims=True)
    d_g_rope = jnp.sum(t_b * r, 0, keepdims=True)
    return d_nope, d_rope, d_g_nope, d_g_rope, jnp.zeros_like(cos), jnp.zeros_like(sin)


_qk_final.defvjp(_qk_final_fwd, _qk_final_bwd)


def _f_qk_final(scale, nope, rope_in, g_nope, g_rope, cos, sin):
    return (_qk_final(scale, nope, rope_in, g_nope, g_rope, cos, sin),)


def _rope_tables(lp):
    half = ROPE // 2
    pos = jnp.maximum(jnp.arange(lp) - PAD_ROWS, 0)
    inv = ROPE_THETA ** (-jnp.arange(half, dtype=F32) / half)
    ang = pos.astype(F32)[:, None] * inv[None, :]
    zeros = jnp.zeros((lp, HEAD - ROPE), F32)
    cos = jnp.concatenate([jnp.cos(ang), jnp.cos(ang), zeros], 1)
    sin = jnp.concatenate([-jnp.sin(ang), jnp.sin(ang), zeros], 1)
    return cos, sin


def _pad_lanes(w, width=HEAD):
    return jnp.pad(w, ((0, 0), (0, width - w.shape[1])))


def _pad_rows(w, rows=HEAD):
    return jnp.pad(w, ((0, rows - w.shape[0]), (0, 0)))


def _split_heads_qk_t(w_t):
    k = w_t.shape[1]
    return jnp.pad(w_t.reshape(N_HEADS, QK_DIM, k), ((0, 0), (0, QK_PAD - QK_DIM), (0, 0))).reshape(N_HEADS * QK_PAD, k)


def _merge_heads_qk_t(g_t):
    k = g_t.shape[1]
    return g_t.reshape(N_HEADS, QK_PAD, k)[:, :QK_DIM].reshape(N_HEADS * QK_DIM, k)


@functools.partial(jax.custom_vjp, nondiff_argnums=(0,))
def _q_final(scale, qh, g_nope, g_rope, cos, sin):
    return _qk_final(scale, qh[:, :HEAD], qh[:, HEAD:], g_nope, g_rope, cos, sin)


def _q_final_bwd(scale, res, g):
    qh, g_nope, g_rope, cos, sin = res
    grads = _qk_final_bwd(scale, (qh[:, :HEAD], qh[:, HEAD:], g_nope, g_rope, cos, sin), g)
    return (jnp.concatenate(grads[:2], axis=1),) + tuple(grads[2:])


_q_final.defvjp(lambda scale, qh, *rest: (_q_final(scale, qh, *rest), (qh,) + rest), _q_final_bwd)


def _f_q_final(scale, qh, g_nope, g_rope, cos, sin):
    return (_q_final(scale, qh, g_nope, g_rope, cos, sin),)


def local_step(x, target, w, deferred=None):
    nb, seq, d = x.shape
    lp = seq + LEAD
    t = nb * lp
    tr = _pick(lp, (544, 128))
    ntab = lp // tr
    mxu = _MXU_DTYPE
    kw = N_HEADS * HEAD

    a_conv = w["a_conv"].T
    alog, dtb, o_gain = _pad_lanes(w["a_log"]), _pad_lanes(w["a_dt_bias"]), w["a_o_gain"]
    a_norm, kv_norm, b_norm = w["a_norm"], w["kv_norm"][None, :], w["b_norm"]
    lat_norm, qlat_norm = w["kv_latent_norm"][None, :], w["b_q_latent_norm"]
    kg_nope, kg_rope = w["k_gain"][None, :HEAD], _pad_lanes(w["k_gain"][None, HEAD:])
    qg_nope, qg_rope = w["b_q_gain"][:, :HEAD], _pad_lanes(w["b_q_gain"][:, HEAD:])
    cos, sin = _rope_tables(lp)

    h0, hn, *gathered = embed_norm(x, w["meta_tokens"].T, a_norm, lp, gather=deferred.first_gather_bufs if deferred else ())
    if deferred:
        w = {**w, **deferred.finish_first(gathered)}
    a_w_in_t = w["a_w_in"].astype(mxu)
    w_qkv_t, w_gba_t = a_w_in_t[:3 * kw], _pad_rows(a_w_in_t[3 * kw:], kw + HEAD)
    z_qkv = matmul("a_in_qkv", hn, w_qkv_t, "nt", out_dtype=mxu)
    z_gba = matmul("a_in_gate_ba", hn, w_gba_t, "nt")
    ba_block = kw // HEAD
    qkv_a, y_conv = conv_fwd(z_qkv, a_conv, lp)
    o_a, states, t_invs, *gathered = delta_fwd(qkv_a, z_gba, ba_block, alog, dtb, lp, gather=deferred.gather_bufs if deferred else ())
    if deferred:
        w = {**w, **deferred.finish(gathered)}
    a_w_out = w["a_w_out"].astype(mxu)
    w_down = _pad_lanes(w["kv_w_down"], KV_RANK + HEAD).astype(mxu)
    w_ukv_t = jnp.concatenate([w["kv_w_uk"], w["kv_w_uv"]], 0).astype(mxu)
    b_w_in_t = w["b_w_in"].astype(mxu)
    w_cq_t, w_gb_t = b_w_in_t[:Q_RANK], b_w_in_t[Q_RANK:]
    w_q_t = _split_heads_qk_t(w["b_w_uq"]).astype(mxu)
    b_w_out = w["b_w_out"].astype(mxu)
    og_args = [Arg(o_a, bc=HEAD, ph=True, diff=True), Arg(z_gba, bc=HEAD, ph=True, diff=True, gdt=mxu), Arg(o_gain, "par", diff=True)]
    h1, og_a = gated_out("a_out", o_a, z_gba, o_gain, a_w_out, h0)

    hk, hb = row_call("b_norms_fwd", _f_rms2, [Arg(h1), Arg(kv_norm, "par"), Arg(b_norm, "par")], [(d, mxu, d, False), (d, mxu, d, False)], tr)
    c_down = matmul("kv_down", hk, w_down, "nn")
    c_kv_arg = Arg(c_down, bc=KV_RANK, diff=True, gdt=mxu)
    k_pe_arg = Arg(c_down, bc=HEAD, base=KV_RANK // HEAD, diff=True)
    c_q_raw = matmul("b_in_q", hb, w_cq_t, "nt")
    gate_b = matmul("b_in_gate", hb, w_gb_t, "nt")
    (c_kv,) = row_call("kv_latent_fwd", _f_rms, [c_kv_arg, Arg(lat_norm, "par")], [(KV_RANK, mxu, KV_RANK, False)], tr)
    (c_q,) = row_call("q_latent_fwd", _f_rms, [Arg(c_q_raw), Arg(qlat_norm, "par")], [(Q_RANK, mxu, Q_RANK, False)], tr)
    k_nope = matmul("k_up", c_kv, w_ukv_t[:kw], "nt")
    v_b = matmul("v_up", c_kv, w_ukv_t[kw:], "nt", out_dtype=mxu)
    q_up = matmul("q_up", c_q, w_q_t, "nt")
    tabs = [Arg(cos, "tab"), Arg(sin, "tab")]
    k_args = [Arg(k_nope, bc=HEAD, ph=True, diff=True, gdt=mxu), k_pe_arg, Arg(kg_nope, "par", diff=True), Arg(kg_rope, "par", diff=True)] + tabs
    q_args = [Arg(q_up, bc=QK_PAD, ph=True, diff=True, gdt=mxu), Arg(qg_nope, "par", diff=True), Arg(qg_rope, "par", diff=True)] + tabs
    f_k_final, f_q_final = functools.partial(_f_qk_final, 1.0), functools.partial(_f_q_final, ATT_SCALE * math.log2(math.e))
    (k_fin,) = row_call("k_final_fwd", f_k_final, k_args, [(N_HEADS * QK_PAD, mxu, QK_PAD, True)], tr, nh=N_HEADS, ntab=ntab)
    (q_fin,) = row_call("q_final_fwd", f_q_final, q_args, [(N_HEADS * QK_PAD, mxu, QK_PAD, True)], tr, nh=N_HEADS, ntab=ntab)
    o_b, lse = flash_fwd(q_fin, k_fin, v_b, lp)
    gb_args = [Arg(o_b, diff=True), Arg(gate_b, diff=True, gdt=mxu)]
    h2, og_b = gated_out("b_out", o_b, gate_b, None, b_w_out, h1)

    loss, dh2 = loss_head(h2.reshape(nb, lp, d), target, lp)
    dh2 = dh2.reshape(t, d)
    grads = {}

    d_og_b = matmul("b_out_dx", dh2, b_w_out, "nt", out_dtype=mxu)
    grads["b_w_out"] = matmul("b_out_dw", og_b, dh2, "tn")
    d_o_b, d_gate_b = row_vjp_call("b_gate_bwd", _f_gate, gb_args, [Arg(d_og_b)], tr)
    dq_fin, dk_fin, dv_b = flash_bwd(q_fin, k_fin, v_b, o_b, lse, d_o_b, lp)
    dq_up, d_qg_nope, d_qg_rope = row_vjp_call(
        "q_final_bwd", f_q_final, q_args, [Arg(dq_fin, bc=QK_PAD, ph=True)], tr, nh=N_HEADS, ntab=ntab)
    dk_nope, dk_pe, d_kg_nope, d_kg_rope = row_vjp_call(
        "k_final_bwd", f_k_final, k_args, [Arg(dk_fin, bc=QK_PAD, ph=True)], tr, nh=N_HEADS, ntab=ntab)
    grads["b_q_gain"] = jnp.concatenate([d_qg_nope, d_qg_rope[:, :ROPE]], 1)
    grads["k_gain"] = jnp.concatenate([d_kg_nope, d_kg_rope[:, :ROPE]], 1)[0]
    d_c_q = matmul("q_up_dx", dq_up, w_q_t, "nn")
    grads["b_w_uq"] = _merge_heads_qk_t(matmul("q_up_dw", dq_up, c_q, "tn"))
    d_c_kv = matmul("k_up_dx", dk_nope, w_ukv_t[:kw], "nn")
    d_c_kv = matmul("v_up_dx", dv_b, w_ukv_t[kw:], "nn", res=d_c_kv)
    grads["kv_w_uk"], grads["kv_w_uv"] = matmul("k_up_dw", dk_nope, c_kv, "tn"), matmul("v_up_dw", dv_b, c_kv, "tn")
    d_c_q_raw, grads["b_q_latent_norm"] = row_vjp_call(
        "q_latent_bwd", _f_rms, [Arg(c_q_raw, diff=True, gdt=mxu), Arg(qlat_norm, "par", diff=True)], [Arg(d_c_q)], tr)
    d_c_kv_raw, d_lat = row_vjp_call(
        "kv_latent_bwd", _f_rms, [c_kv_arg, Arg(lat_norm, "par", diff=True)], [Arg(d_c_kv)], tr)
    grads["kv_latent_norm"] = d_lat[0]
    d_hb = matmul("b_in_q_dx", d_c_q_raw, w_cq_t, "nn")
    d_hb = matmul("b_in_gate_dx", d_gate_b, w_gb_t, "nn", res=d_hb, out_dtype=mxu)
    grads["b_w_in"] = jnp.concatenate([matmul("b_in_q_dw", d_c_q_raw, hb, "tn"), matmul("b_in_gate_dw", d_gate_b, hb, "tn")], 0)
    d_c_down = jnp.concatenate([d_c_kv_raw, dk_pe.astype(mxu)], 1)
    d_hk = matmul("kv_down_dx", d_c_down, w_down, "nt", out_dtype=mxu)
    grads["kv_w_down"] = matmul("kv_down_dw", hk, d_c_down, "tn")[:, :KV_RANK + ROPE]
    dh1, d_kv_norm, grads["b_norm"] = row_vjp_call(
        "b_norms_bwd", lambda x_, g1, g2: _f_rms2(x_, g1, g2) + (x_,),
        [Arg(h1, diff=True), Arg(kv_norm, "par", diff=True), Arg(b_norm, "par", diff=True)], [Arg(d_hk), Arg(d_hb), Arg(dh2)], tr)
    grads["kv_norm"] = d_kv_norm[0]

    d_og_a = matmul("a_out_dx", dh1, a_w_out, "nt", out_dtype=mxu)
    grads["a_w_out"] = matmul("a_out_dw", og_a, dh1, "tn")
    d_o_a, d_gate_a, grads["a_o_gain"] = row_vjp_call(
        "a_out_gate_bwd", _f_out_gate, og_args, [Arg(d_og_a, bc=HEAD, ph=True)], tr, nh=N_HEADS)
    dqkv_a, d_ba, d_alog, d_dtb, *received = delta_bwd(qkv_a, z_gba, ba_block, alog, dtb, states, t_invs, d_o_a, lp,
                                                        scatter=deferred.scatter_bufs(grads) if deferred else ())
    grads["a_log"], grads["a_dt_bias"] = d_alog[:, :N_HEADS], d_dtb[:, :N_HEADS]
    dz_qkv, d_conv = conv_bwd(z_qkv, y_conv, a_conv, dqkv_a, lp)
    grads["a_conv"] = d_conv.T
    dz_gba = jnp.concatenate([d_gate_a, d_ba.astype(mxu)], 1)
    grads["a_w_in"] = jnp.concatenate([matmul("a_in_qkv_dw", dz_qkv, hn, "tn"), matmul("a_in_gate_ba_dw", dz_gba, hn, "tn")[:kw + 2 * N_HEADS]], 0)
    ride = deferred.last_scatter_bufs(grads) if deferred else ((), ())
    d_hn = matmul("a_in_qkv_dx", dz_qkv, w_qkv_t, "nn", scatter=ride[0])
    if ride[0]:
        d_hn, *received_half = d_hn
        received = list(received) + received_half
    d_hn = matmul("a_in_gate_ba_dx", dz_gba, w_gba_t, "nn", res=d_hn, out_dtype=mxu, scatter=ride[1])
    if ride[1]:
        d_hn, *received_half = d_hn
        received = list(received) + received_half
    dh0, grads["a_norm"] = row_vjp_call("a_norm_bwd", lambda x_, g_: _f_rms(x_, g_) + (x_,),
                                        [Arg(h0, diff=True), Arg(a_norm, "par", diff=True)], [Arg(d_hn), Arg(dh1)], tr)
    dh0 = dh0.reshape(nb, lp, d)
    grads["meta_tokens"] = meta_grad(dh0).T
    return loss, dh0[:, LEAD:], grads, received


_SHARDED = (
    ("meta_tokens", True, False), ("a_norm", True, False), ("a_w_in", True, True), ("a_conv", True, False), ("a_w_out", False, True),
    ("kv_w_down", False, True), ("kv_w_uk", True, True), ("kv_w_uv", True, True), ("b_w_in", True, True), ("b_w_uq", True, True),
    ("b_w_out", False, True))
_REPLICATED = ("a_log", "a_dt_bias", "a_o_gain", "kv_norm", "kv_latent_norm", "k_gain", "b_norm", "b_q_latent_norm", "b_q_gain")
_ALL_WEIGHTS = ("meta_tokens", "a_norm", "a_w_in", "a_conv", "a_log", "a_dt_bias", "a_o_gain", "a_w_out", "kv_norm", "kv_w_down",
                "kv_latent_norm", "kv_w_uk", "kv_w_uv", "k_gain", "b_norm", "b_w_in", "b_q_latent_norm", "b_w_uq", "b_q_gain", "b_w_out")


def _round_up(n, m):
    return (n + m - 1) // m * m


def _pack_rows(pieces, row_multiple):
    padded = []
    for p in pieces:
        n = p.shape[-1]
        padded.append(jnp.pad(p, [(0, 0)] * (p.ndim - 1) + [(0, _round_up(n, PACK_COLS) - n)]))
    flat = jnp.concatenate(padded, -1)
    rows = _round_up(flat.shape[-1] // PACK_COLS, row_multiple)
    flat = jnp.pad(flat, [(0, 0)] * (flat.ndim - 1) + [(0, rows * PACK_COLS - flat.shape[-1])])
    return flat.reshape(flat.shape[:-1] + (rows, PACK_COLS))


def _unpack_rows(buf, sizes):
    flat = buf.reshape(buf.shape[:-2] + (-1,))
    out, off = [], 0
    for n in sizes:
        out.append(flat[..., off:off + n])
        off += _round_up(n, PACK_COLS)
    return out


def _shard_2d(a):
    return a.reshape(a.shape[-2:]) if a.ndim > 2 else a


def _kl_shard(a, by_cols):
    return _shard_2d(a).T if by_cols else _shard_2d(a)


_GROUPS_FIRST = (("a_w_in",),)
_GROUPS_LATER = (("a_w_out", "b_w_in", "b_w_out"), ("b_w_uq",), ("kv_w_down",), ("kv_w_uk", "kv_w_uv"))
_SMALL_SHARDED = ("meta_tokens", "a_norm", "a_conv")
_BY_COLS = {name: by_cols for name, by_cols, _ in _SHARDED}
ROW_ALIGN = 16


def _stack_rows(pieces):
    padded, starts, row = [], [], 0
    for p in pieces:
        r = p.shape[-2]
        padded.append(jnp.pad(p, [(0, 0)] * (p.ndim - 2) + [(0, _round_up(r, ROW_ALIGN) - r), (0, 0)]))
        starts.append(row)
        row += _round_up(r, ROW_ALIGN)
    return jnp.concatenate(padded, -2), starts


def _stack_group(arrays_by_name, names):
    arrays = [arrays_by_name[n].astype(BF16) for n in names]
    buf, starts = _stack_rows(arrays)
    return buf, [(n, s, a.shape[-2]) for n, s, a in zip(names, starts, arrays, strict=True)]


def _stack_groups(arrays_by_name, groups):
    stacked = [_stack_group(arrays_by_name, names) for names in groups]
    return [b for b, _ in stacked], [entries for _, entries in stacked]


def _full_from_gathered(gathered, layout):
    full = {}
    for got, entries in zip(gathered, layout, strict=True):
        for name, start, rows in entries:
            full[name] = got[:, start:start + rows].reshape(N_DEV * rows, got.shape[-1])
    return full


def gather_small_weights(local):
    small = [_kl_shard(local[n], _BY_COLS[n]) for n in _SMALL_SHARDED]
    (gathered,) = _exchange("all_gather", [_pack_rows([s.reshape(-1) for s in small], 8)], scatter=False)
    full = {}
    for name, part, sh in zip(_SMALL_SHARDED, _unpack_rows(gathered, [s.size for s in small]), small, strict=True):
        full[name] = part.reshape(N_DEV * sh.shape[0], sh.shape[1])
    full["a_norm"] = full["a_norm"].reshape(1, -1)
    return full


class LaterExchanges:
    def __init__(self, local):
        shards = {n: _kl_shard(local[n], _BY_COLS[n]) for names in _GROUPS_FIRST + _GROUPS_LATER for n in names}
        self.first_gather_bufs, self.first_layout = _stack_groups(shards, _GROUPS_FIRST)
        self.gather_bufs, self.layout = _stack_groups(shards, _GROUPS_LATER)

    def finish_first(self, gathered):
        return _full_from_gathered(gathered, self.first_layout)

    def finish(self, gathered):
        return _full_from_gathered(gathered, self.layout)

    def scatter_bufs(self, grads):
        return _stack_groups(_owner_slices(grads, _GROUPS_LATER), _GROUPS_LATER)[0]

    def last_scatter_bufs(self, grads):
        (buf,), self.last_layout = _stack_groups(_owner_slices(grads, _GROUPS_FIRST), _GROUPS_FIRST)
        first = buf.shape[-1] * 5 // 8 // HEAD * HEAD
        return [buf[..., :first]], [buf[..., first:]]


def _owner_slices(grads, groups):
    return {n: grads[n].reshape(N_DEV, -1, grads[n].shape[-1]) for names in groups for n in names}


def reduce_contributions(name, recv):
    _, r, c = recv.shape
    tr = max(d for d in range(8, 513, 8) if r % d == 0 and (d % ROW_ALIGN == 0 or recv.dtype == F32))

    def body(g_ref, o_ref):
        g = g_ref[0].astype(F32)
        for dev in range(1, N_DEV):
            g = g + g_ref[dev].astype(F32)
        o_ref[...] = g

    return pl.pallas_call(
        body, grid=(r // tr,), in_specs=[pl.BlockSpec((N_DEV, tr, c), lambda i: (0, i, 0))], out_specs=pl.BlockSpec((tr, c), lambda i: (i, 0)),
        out_shape=jax.ShapeDtypeStruct((r, c), F32), compiler_params=_cparams(("arbitrary",)), name=name)(recv)


def adamw_all(gs, ws, ms, vs):
    n = len(gs)

    def body(*refs):
        for i in range(n):
            g_ref, w_ref, m_ref, v_ref = (refs[j * n + i] for j in range(4))
            d_ref, mo_ref, vo_ref = (refs[(4 + j) * n + i] for j in range(3))
            g = g_ref[...]
            m_new = ADAM_B1 * m_ref[...] + (1.0 - ADAM_B1) * g
            v_new = ADAM_B2 * v_ref[...] + (1.0 - ADAM_B2) * (g * g)
            m_hat = m_new / (1.0 - ADAM_B1 ** ADAM_STEP)
            v_hat = v_new / (1.0 - ADAM_B2 ** ADAM_STEP)
            d_ref[...] = -ADAM_LR * (m_hat / (jnp.sqrt(v_hat) + ADAM_EPS) + ADAM_WD * w_ref[...])
            mo_ref[...] = m_new
            vo_ref[...] = v_new

    out = [jax.ShapeDtypeStruct(g.shape, F32) for g in gs] * 3
    res = pl.pallas_call(body, out_shape=out, compiler_params=pltpu.CompilerParams(vmem_limit_bytes=VMEM_LIMIT), name="adamw_all")(*gs, *ws, *ms, *vs)
    return res[:n], res[n:2 * n], res[2 * n:]


def kernel(x, meta_tokens, a_norm, a_w_in, a_conv, a_log, a_dt_bias, a_o_gain, a_w_out, kv_norm, kv_w_down, kv_latent_norm, kv_w_uk, kv_w_uv, k_gain, b_norm, b_w_in, b_q_latent_norm, b_w_uq, b_q_gain, b_w_out, loss_target, m_meta_tokens, m_a_norm, m_a_w_in, m_a_conv, m_a_log, m_a_dt_bias, m_a_o_gain, m_a_w_out, m_kv_norm, m_kv_w_down, m_kv_latent_norm, m_kv_w_uk, m_kv_w_uv, m_k_gain, m_b_norm, m_b_w_in, m_b_q_latent_norm, m_b_w_uq, m_b_q_gain, m_b_w_out, v_meta_tokens, v_a_norm, v_a_w_in, v_a_conv, v_a_log, v_a_dt_bias, v_a_o_gain, v_a_w_out, v_kv_norm, v_kv_w_down, v_kv_latent_norm, v_kv_w_uk, v_kv_w_uv, v_k_gain, v_b_norm, v_b_w_in, v_b_q_latent_norm, v_b_w_uq, v_b_q_gain, v_b_w_out):
    given = dict(locals())
    local_w = {n: given[n] for n in _ALL_WEIGHTS}
    full = gather_small_weights(local_w)
    for n in _REPLICATED:
        full[n] = local_w[n]
    later = LaterExchanges(local_w)

    loss_part, grad_x, grads, received_riding = local_step(x, loss_target, full, later)

    exact = [grads[n].reshape(N_DEV, -1) for n in _SMALL_SHARDED]
    exact += [jnp.broadcast_to(grads[n].reshape(1, -1), (N_DEV, grads[n].size)) for n in _REPLICATED]
    exact.append(jnp.broadcast_to(loss_part, (N_DEV, 1)))
    received = list(received_riding) + list(_exchange("all_to_all", [_pack_rows(exact, 8)], scatter=True))
    layout = later.layout + later.last_layout
    summed = [reduce_contributions(f"reduce_{i}", r) for i, r in enumerate(received)]
    n_later = len(later.layout)
    summed = summed[:n_later] + [jnp.concatenate(summed[n_later:n_later + 2], 1)] + summed[n_later + 2:]

    grad_kl = {}
    for got, entries in zip(summed, layout):
        for n, start, rows in entries:
            grad_kl[n] = got[start:start + rows]
    parts = _unpack_rows(summed[-1], [p.shape[1] for p in exact])
    for n, part in zip(_SMALL_SHARDED + _REPLICATED, parts, strict=False):
        grad_kl[n] = part
    loss = parts[-1][0]

    def natural_2d(n, a):
        shape = _shard_2d(local_w[n]).shape if local_w[n].ndim > 1 else (1, local_w[n].size)
        return a.reshape(shape[::-1]).T if _BY_COLS.get(n, False) else a.reshape(shape)

    as_2d = lambda n, a: a.reshape(natural_2d(n, grad_kl[n]).shape)
    gs = [natural_2d(n, grad_kl[n]) for n in _ALL_WEIGHTS]
    deltas, new_m, new_v = adamw_all(gs, [as_2d(n, local_w[n]) for n in _ALL_WEIGHTS], [as_2d(n, given["m_" + n]) for n in _ALL_WEIGHTS],
                                     [as_2d(n, given["v_" + n]) for n in _ALL_WEIGHTS])
    results = [a.reshape(local_w[n].shape) for group in (gs, deltas, new_m, new_v) for n, a in zip(_ALL_WEIGHTS, group, strict=True)]
    return (loss, grad_x, *results)
```

```python
import dataclasses
import functools
import math

import jax
import jax.numpy as jnp
from jax import lax
from jax.experimental import pallas as pl
from jax.experimental.pallas import tpu as pltpu

F32 = jnp.float32
BF16 = jnp.bfloat16
_MXU_DTYPE = jnp.bfloat16

N_DEV = 8
D_MODEL = 1024
N_HEADS = 8
HEAD = 128
CHUNK = 64
N_META = 16
PAD_ROWS = 2 * CHUNK - N_META
LEAD = PAD_ROWS + N_META
ROPE = 64
QK_DIM = HEAD + ROPE
QK_PAD = 2 * HEAD
KV_RANK = 256
Q_RANK = 384
CONV_K = 4
EPS = 1e-6
NEG = -1e30
ROPE_THETA = 10000.0
ADAM_LR, ADAM_B1, ADAM_B2, ADAM_EPS, ADAM_WD, ADAM_STEP = 0.001, 0.9, 0.999, 1e-08, 0.01, 10
PACK_COLS = 512
VMEM_LIMIT = 56 * 1024 * 1024


def _pick(n, options):
    for o in options:
        if n % o == 0:
            return o
    raise ValueError(f"no tile for {n} among {options}")


def _cparams(sem):
    return pltpu.CompilerParams(dimension_semantics=sem, vmem_limit_bytes=VMEM_LIMIT)


def _dims(a, dims):
    if a.ndim == 2:
        return (dims, ((), ()))
    (ca,), (cb,) = dims
    return (((ca + 1,), (cb + 1,)), ((0,), (0,)))


def _dot(a, b, dims):
    return lax.dot_general(a.astype(_MXU_DTYPE), b.astype(_MXU_DTYPE), _dims(a, dims), preferred_element_type=F32)


@jax.custom_vjp
def mm_nn(a, b):
    return _dot(a, b, ((1,), (0,)))


@jax.custom_vjp
def mm_nt(a, b):
    return _dot(a, b, ((1,), (1,)))


@jax.custom_vjp
def mm_tn(a, b):
    return _dot(a, b, ((0,), (0,)))


mm_nn.defvjp(lambda a, b: (mm_nn(a, b), (a, b)), lambda r, g: (mm_nt(g, r[1]), mm_tn(r[0], g)))
mm_nt.defvjp(lambda a, b: (mm_nt(a, b), (a, b)), lambda r, g: (mm_nn(g, r[1]), mm_tn(g, r[0])))
mm_tn.defvjp(lambda a, b: (mm_tn(a, b), (a, b)), lambda r, g: (mm_nt(r[1], g), mm_nn(r[0], g)))


def _split_terms(x, n):
    terms, rest = [], x
    for _ in range(n):
        t = rest.astype(_MXU_DTYPE)
        terms.append(t)
        rest = rest - t.astype(F32)
    return terms


def _dot_01_raw(m, x, dims):
    m = m.astype(_MXU_DTYPE)
    return sum(lax.dot_general(m, t, _dims(m, dims), preferred_element_type=F32) for t in _split_terms(x, 3))


@jax.custom_vjp
def _dot_01(m, x):
    return _dot_01_raw(m, x, ((1,), (0,)))


_dot_01.defvjp(lambda m, x: (_dot_01(m, x), m), lambda m, g: (jnp.zeros_like(m), _dot_01_raw(m, g, ((0,), (0,)))))


def _inv_unit_lower(a):
    n = a.shape[-1]
    eye = (lax.broadcasted_iota(jnp.int32, (n, n), 0) == lax.broadcasted_iota(jnp.int32, (n, n), 1)).astype(F32)
    d = lambda u, w: lax.dot_general(u, w, _dims(u, ((1,), (0,))), preferred_element_type=F32)
    t = eye - a
    p = a.astype(_MXU_DTYPE)
    p = d(p, p)
    squarings = int(math.log2(n)) - 1
    for s in range(squarings):
        ph = p.astype(_MXU_DTYPE)
        t_hi, t_lo = _split_terms(t, 2)
        t = t + (d(t_hi, ph) + d(t_lo, ph))
        if s + 1 < squarings:
            p = d(ph, ph)
    return t


@jax.custom_vjp
def _inv_lookup(a, t):
    return t


def _inv_lookup_bwd(t, g):
    return -mm_tn(t, mm_nt(g, t)), jnp.zeros_like(t)


_inv_lookup.defvjp(lambda a, t: (t, t), _inv_lookup_bwd)


def _sigmoid(x):
    return 1.0 / (1.0 + jnp.exp(-x))


@jax.custom_vjp
def _silu(x):
    return x * _sigmoid(x)


def _silu_fwd(x):
    s = _sigmoid(x)
    return x * s, (x, s)


_silu.defvjp(_silu_fwd, lambda r, g: (g * (r[1] * (1.0 + r[0] * (1.0 - r[1]))),))


def _softplus(x):
    return jnp.where(x > 20.0, x, jnp.log(1.0 + jnp.exp(jnp.minimum(x, 20.0))))


def _rms(x, g, width=None):
    ms = jnp.sum(x * x, -1, keepdims=True) / (x.shape[-1] if width is None else width)
    return x * lax.rsqrt(ms + EPS) * g


MM_VMEM_BUDGET = 40 * 1024 * 1024


def _matmul_rows(name, a, b, mode, out_dtype, res, scatter):
    m, k = a.shape
    n = b.shape[1] if mode == "nn" else b.shape[0]
    dims = {"nn": ((1,), (0,)), "nt": ((1,), (1,))}[mode]
    out_bytes = jnp.dtype(out_dtype).itemsize
    n_in, nx = 2 + (res is not None), len(scatter)

    def vmem(tm):
        blocks = 2 * tm * k * a.dtype.itemsize + 2 * k * n * b.dtype.itemsize + 2 * tm * n * out_bytes + tm * n * 4
        return blocks + (2 * tm * n * res.dtype.itemsize if res is not None else 0)

    tm = next(c for c in (2176, 1088, 512, 256, 128, 64) if m % c == 0 and vmem(c) <= MM_VMEM_BUDGET)
    steps = m // tm

    def body(*refs):
        a_ref, b_ref, o_ref = refs[0], refs[1], refs[n_in + nx]
        i = pl.program_id(0)
        finish = _ride(scatter, True, refs[n_in:n_in + nx], refs[n_in + nx + 1:n_in + 2 * nx + 1], refs[n_in + 2 * nx + 1:], i == 0, i == steps - 1)
        out = _dot(a_ref[...], b_ref[...], dims)
        if res is not None:
            out = out + refs[2][...].astype(F32)
        o_ref[...] = out.astype(o_ref.dtype)
        finish()

    o_spec = pl.BlockSpec((tm, n), lambda i: (i, 0))
    in_specs = [pl.BlockSpec((tm, k), lambda i: (i, 0)), pl.BlockSpec(b.shape, lambda i: (0, 0))] + ([o_spec] if res is not None else [])
    args = (a, b) + ((res,) if res is not None else ())
    out = pl.pallas_call(
        body, grid=(steps,), in_specs=in_specs + [_HBM] * nx, out_specs=[o_spec] + [_HBM] * nx,
        out_shape=[jax.ShapeDtypeStruct((m, n), out_dtype)] + Exchange.out_shape(scatter, True), scratch_shapes=Exchange.scratch(nx) if nx else [],
        compiler_params=_cparams(("arbitrary",) if nx else ("parallel",)), name=name)(*args, *scatter)
    return out if nx else out[0]


def matmul(name, a, b, mode, out_dtype=None, res=None, scatter=()):
    if mode != "tn":
        return _matmul_rows(name, a, b, mode, out_dtype or F32, res, scatter)
    out_dtype = out_dtype or _MXU_DTYPE
    (k, m), (k2, n) = a.shape, b.shape
    assert k == k2 and res is None, (name, a.shape, b.shape, mode)
    tm = _pick(m, (m if m <= 1536 else 1024, 1024, 512, 384, 256, 128))
    tn = _pick(n, (1024, 512, 384, 256, 128))
    tk = _pick(k, (512, 256, 128))
    nk = k // tk
    dims = ((0,), (0,))

    def body(*refs):
        if res is None:
            a_ref, b_ref, o_ref, acc_ref = refs
        else:
            a_ref, b_ref, r_ref, o_ref, acc_ref = refs
        kk = pl.program_id(2)

        @pl.when(kk == 0)
        def _():
            acc_ref[...] = jnp.zeros_like(acc_ref)

        acc_ref[...] += _dot(a_ref[...], b_ref[...], dims)

        @pl.when(kk == nk - 1)
        def _():
            out = acc_ref[...]
            if res is not None:
                out = out + r_ref[...].astype(F32)
            o_ref[...] = out.astype(o_ref.dtype)

    a_spec = pl.BlockSpec((tk, tm), lambda i, j, kk: (kk, i)) if mode == "tn" else pl.BlockSpec((tm, tk), lambda i, j, kk: (i, kk))
    b_spec = pl.BlockSpec((tn, tk), lambda i, j, kk: (j, kk)) if mode == "nt" else pl.BlockSpec((tk, tn), lambda i, j, kk: (kk, j))
    o_spec = pl.BlockSpec((tm, tn), lambda i, j, kk: (i, j))
    in_specs = [a_spec, b_spec] + ([o_spec] if res is not None else [])
    args = (a, b) + ((res,) if res is not None else ())
    return pl.pallas_call(
        body, grid=(m // tm, n // tn, nk), in_specs=in_specs, out_specs=o_spec,
        out_shape=jax.ShapeDtypeStruct((m, n), out_dtype), scratch_shapes=[pltpu.VMEM((tm, tn), F32)],
        compiler_params=_cparams(("parallel", "parallel", "arbitrary")), name=name)(*args)


@dataclasses.dataclass
class Arg:
    arr: jax.Array
    kind: str = "row"
    bc: int = 0
    base: int = 0
    ph: bool = False
    diff: bool = False
    gdt: object = F32


def _arg_spec(a, tr, nh, ntab, base=None):
    bc = a.bc or a.arr.shape[1]
    base = a.base if base is None else base
    width = bc * nh if a.ph else bc
    col = base // nh if a.ph else base
    assert not a.ph or base % nh == 0
    if a.kind == "row":
        return pl.BlockSpec((tr, width), lambda i: (i, col))
    if a.kind == "tab":
        return pl.BlockSpec((tr, width), lambda i: (i % ntab, col))
    return pl.BlockSpec((a.arr.shape[0], width), lambda i: (0, col))


def _head_view(ref, a, h, rs):
    bc = a.bc or a.arr.shape[1]
    rows = slice(None) if a.kind == "par" else rs
    v = ref[rows, h * bc:(h + 1) * bc] if a.ph else ref[rows, :]
    return v.astype(F32) if jnp.issubdtype(v.dtype, jnp.floating) else v


def row_call(name, fn, args, outs, tr, nh=1, ntab=1):
    t = args[0].arr.shape[0]
    n_in = len(args)
    out_args = [Arg(None, "row", bc, 0, ph) for (_, _, bc, ph) in outs]
    assert all(a.ph or nh == 1 for a in out_args)
    rs = slice(None)

    def body(*refs):
        for h in range(nh):
            res = fn(*[_head_view(r, a, h, rs) for r, a in zip(refs[:n_in], args, strict=True)])
            for r, a, v in zip(refs[n_in:], out_args, res, strict=True):
                r[rs, h * a.bc:(h + 1) * a.bc] = v.astype(r.dtype)

    return pl.pallas_call(
        body, grid=(t // tr,), in_specs=[_arg_spec(a, tr, nh, ntab) for a in args], out_specs=[_arg_spec(a, tr, nh, ntab) for a in out_args],
        out_shape=[jax.ShapeDtypeStruct((t, cols), dt) for (cols, dt, _, _) in outs],
        compiler_params=_cparams(("arbitrary",)), name=name)(*[a.arr for a in args])


def row_vjp_call(name, fn, args, cts, tr, nh=1, ntab=1):
    t = args[0].arr.shape[0]
    n_in, n_ct = len(args), len(cts)
    diff_idx = [k for k, a in enumerate(args) if a.diff]
    def body(*refs):
        out_refs = refs[n_in + n_ct:]
        par_sum = {}
        for k, r in zip(diff_idx, out_refs, strict=True):
            if args[k].kind == "par":
                @pl.when(pl.program_id(0) == 0)
                def _(r=r):
                    r[...] = jnp.zeros_like(r)

        for rs in (slice(None),):
            row_sum = {}
            for h in range(nh):
                vals = [_head_view(r, a, h, rs) for r, a in zip(refs[:n_in], args, strict=True)]
                ct_vals = tuple(_head_view(r, a, h, rs) for r, a in zip(refs[n_in:n_in + n_ct], cts, strict=True))

                def f(*dv, vals=vals):
                    full = list(vals)
                    for k, v in zip(diff_idx, dv, strict=True):
                        full[k] = v
                    return tuple(fn(*full))

                _, vjp = jax.vjp(f, *[vals[k] for k in diff_idx])
                for j, (k, r, g) in enumerate(zip(diff_idx, out_refs, vjp(ct_vals), strict=True)):
                    a = args[k]
                    bc = a.bc or a.arr.shape[1]
                    if a.kind == "row" and a.ph:
                        r[rs, h * bc:(h + 1) * bc] = g.astype(r.dtype)
                    elif a.kind == "row":
                        row_sum[j] = g if j not in row_sum else row_sum[j] + g
                    else:
                        key = (j, h if a.ph else 0)
                        par_sum[key] = g if key not in par_sum else par_sum[key] + g
            for j, g in row_sum.items():
                out_refs[j][rs, :] = g.astype(out_refs[j].dtype)
        for (j, h), g in par_sum.items():
            bc = g.shape[1]
            out_refs[j][:, h * bc:(h + 1) * bc] += g

    out_specs, out_shape = [], []
    for k in diff_idx:
        a = args[k]
        bc = a.bc or a.arr.shape[1]
        out_specs.append(_arg_spec(a, tr, nh, ntab, base=0))
        out_shape.append(jax.ShapeDtypeStruct((t if a.kind == "row" else a.arr.shape[0], bc * (nh if a.ph else 1)), a.gdt if a.kind == "row" else F32))
    in_specs = [_arg_spec(a, tr, nh, ntab) for a in list(args) + list(cts)]
    return pl.pallas_call(
        body, grid=(t // tr,), in_specs=in_specs, out_specs=out_specs, out_shape=out_shape,
        compiler_params=_cparams(("arbitrary",)), name=name)(*[a.arr for a in list(args) + list(cts)])


def _conv_taps(x, w):
    rows = lax.broadcasted_iota(jnp.int32, x.shape, 0)
    y = x * w[CONV_K - 1:CONV_K, :]
    for s in range(1, CONV_K):
        y = y + jnp.where(rows >= s, pltpu.roll(x, s, 0), 0.0) * w[CONV_K - 1 - s:CONV_K - s, :]
    return y


CONV_HEADS = 4
CONV_BLOCKS_PER_THIRD = N_HEADS // CONV_HEADS


def _conv_post(y, block):
    a = _silu(y)
    normed = block < 2 * CONV_BLOCKS_PER_THIRD
    scale = jnp.where(block < CONV_BLOCKS_PER_THIRD, HEAD ** -0.5, 1.0)
    return a * jnp.where(normed, lax.rsqrt(jnp.sum(a * a, -1, keepdims=True) + EPS) * scale, 1.0)


def conv_fwd(z, w, lp):
    t, width = z.shape
    cols = CONV_HEADS * HEAD

    def body(z_ref, w_ref, o_ref, y_ref):
        block = pl.program_id(1)
        for h in range(CONV_HEADS):
            cs = slice(h * HEAD, (h + 1) * HEAD)
            y = _conv_taps(z_ref[:, cs].astype(F32), w_ref[:, cs])
            y_ref[:, cs] = y.astype(y_ref.dtype)
            o_ref[:, cs] = _conv_post(y, block)

    blk = pl.BlockSpec((lp, cols), lambda b, j: (b, j))
    out = jax.ShapeDtypeStruct((t, width), F32)
    return pl.pallas_call(
        body, grid=(t // lp, width // cols), in_specs=[blk, pl.BlockSpec((CONV_K, cols), lambda b, j: (0, j))],
        out_specs=[blk, blk], out_shape=[out, jax.ShapeDtypeStruct((t, width), _MXU_DTYPE)],
        compiler_params=_cparams(("arbitrary", "arbitrary")), name="a_conv_fwd")(z, w)


def conv_bwd(z, y, w, dout, lp):
    t, width = z.shape
    cols = CONV_HEADS * HEAD

    def body(z_ref, y_ref, w_ref, g_ref, dz_ref, dw_ref):
        block = pl.program_id(0)

        @pl.when(pl.program_id(1) == 0)
        def _():
            dw_ref[...] = jnp.zeros_like(dw_ref)

        for h in range(CONV_HEADS):
            cs = slice(h * HEAD, (h + 1) * HEAD)
            x, wv = z_ref[:, cs].astype(F32), w_ref[:, cs]
            _, vjp = jax.vjp(lambda y_: _conv_post(y_, block), y_ref[:, cs].astype(F32))
            (dy,) = vjp(g_ref[:, cs])
            rows = lax.broadcasted_iota(jnp.int32, x.shape, 0)
            dx = dy * wv[CONV_K - 1:CONV_K, :]
            dw_ref[CONV_K - 1:CONV_K, cs] += jnp.sum(dy * x, axis=0, keepdims=True)
            for s in range(1, CONV_K):
                dy_up = jnp.where(rows < lp - s, pltpu.roll(dy, lp - s, 0), 0.0)
                dx = dx + dy_up * wv[CONV_K - 1 - s:CONV_K - s, :]
                dw_ref[CONV_K - 1 - s:CONV_K - s, cs] += jnp.sum(dy_up * x, axis=0, keepdims=True)
            dz_ref[:, cs] = dx.astype(dz_ref.dtype)

    blk = pl.BlockSpec((lp, cols), lambda j, b: (b, j))
    w_blk = pl.BlockSpec((CONV_K, cols), lambda j, b: (0, j))
    return pl.pallas_call(
        body, grid=(width // cols, t // lp), in_specs=[blk, blk, w_blk, blk], out_specs=[blk, w_blk],
        out_shape=[jax.ShapeDtypeStruct((t, width), _MXU_DTYPE), jax.ShapeDtypeStruct((CONV_K, width), F32)],
        compiler_params=_cparams(("arbitrary", "arbitrary")), name="a_conv_bwd")(z, y, w, dout)


def _delta_chunk(q, k, v, ba, alog, dtb, state, t_stored):
    n_g, c = q.shape[0], q.shape[1]
    lane = lax.broadcasted_iota(jnp.int32, (1, HEAD), 1)

    def pick(xs, offset):
        cols = [jnp.sum(xs[i // N_HEADS if len(xs) > 1 else 0] * (lane == offset + i % N_HEADS).astype(F32), axis=1, keepdims=True)[None]
                for i in range(n_g)]
        return jnp.concatenate(cols, 0)

    b_raw, a_raw = pick(ba, 0), pick(ba, N_HEADS)
    a_log, dt_bias = pick((alog,), 0), pick((dtb,), 0)
    beta = _sigmoid(b_raw)
    g = -jnp.exp(a_log) * _softplus(a_raw + dt_bias)
    ri = lax.broadcasted_iota(jnp.int32, (c, c), 0)
    ci = lax.broadcasted_iota(jnp.int32, (c, c), 1)
    tril = ci <= ri
    lower = jnp.broadcast_to(tril.astype(F32), (n_g, c, c))
    gc_col = _dot_01(lower, g * jnp.ones((1, 1, HEAD), F32))[:, :, :1]
    gc_row = _dot_01(jnp.ones((n_g, 8, c), F32), g * (ri <= ci).astype(F32)[None])[:, 0:1, :]
    gc_last = jnp.sum(g, axis=1, keepdims=True)
    decay = jnp.exp(jnp.where(tril, gc_col - gc_row, NEG))
    e_gc = jnp.exp(gc_col)
    kb = k * beta
    a_mat = jnp.where(ci < ri, mm_nt(kb, k) * decay, 0.0)
    t_inv = _inv_unit_lower(a_mat) if t_stored is None else _inv_lookup(a_mat, t_stored)
    u_base = mm_nn(t_inv, v * beta)
    w_dec = mm_nn(t_inv, kb * e_gc)
    attn = jnp.where(tril, mm_nt(q, k) * decay, 0.0)
    u = u_base - mm_nn(w_dec, state)
    o = mm_nn(q * e_gc, state) + mm_nn(attn, u)
    new_state = state * jnp.exp(gc_last) + mm_tn(k * jnp.exp(gc_last - gc_col), u)
    return o, new_state, t_inv


DELTA_STEP_FWD = (4, 2)
DELTA_STEP_BWD = (2, 2)


def _heads_of(ref, rs, first_col):
    return jnp.stack([ref[i // N_HEADS, rs, first_col + (i % N_HEADS) * HEAD:first_col + (i % N_HEADS + 1) * HEAD]
                      for i in range(ref.shape[0] * N_HEADS)])


def _qkv_heads(ref, rs, part):
    return _heads_of(ref, rs, part * N_HEADS * HEAD)


def _by_sequence(a, lp):
    return a.reshape(a.shape[0] // lp, lp, a.shape[1])


def _ride(bufs, scatter, refs_in, refs_out, sems, first, last, two_level=False):
    if not bufs:
        return lambda: None
    make = lambda: (TwoLevelGather if two_level else Exchange)(refs_in, refs_out, *sems, scatter)

    @pl.when(first)
    def _():
        make().start()

    def finish():
        @pl.when(last)
        def _():
            make().wait()

    return finish


def delta_fwd(qkv, ba, ba_block, alog, dtb, lp, gather=()):
    t = qkv.shape[0]
    nb, nc = t // lp, lp // CHUNK
    seqs, cps = DELTA_STEP_FWD
    ng, rows = nc // cps, cps * CHUNK
    nx = len(gather)
    nbg = nb // seqs
    assert nc % cps == 0 and nb % seqs == 0

    def body(*refs):
        qkv_ref, ba_ref, al_ref, dt_ref = refs[:4]
        o_ref, s_ref, t_ref = refs[4 + nx:7 + nx]
        state_ref = refs[7 + 2 * nx]
        b, n = pl.program_id(0), pl.program_id(1)
        finish = _ride(gather, False, refs[4:4 + nx], refs[7 + nx:7 + 2 * nx], refs[8 + 2 * nx:], (b == 0) & (n == 0), (b == nbg - 1) & (n == ng - 1))

        @pl.when(n == 0)
        def _():
            state_ref[...] = jnp.zeros_like(state_ref)

        al, dtv = al_ref[...], dt_ref[...]
        for c in range(cps):
            rs = slice(c * CHUNK, (c + 1) * CHUNK)
            state = state_ref[...]
            o, new_state, t_inv = _delta_chunk(_qkv_heads(qkv_ref, rs, 0), _qkv_heads(qkv_ref, rs, 1), _qkv_heads(qkv_ref, rs, 2),
                                               tuple(ba_ref[i, rs, :] for i in range(seqs)), al, dtv, state, None)
            for i in range((seqs * N_HEADS)):
                seq, g = divmod(i, N_HEADS)
                o_ref[seq, rs, g * HEAD:(g + 1) * HEAD] = o[i]
                s_ref[seq, g, c] = state[i]
                t_ref[seq, g, c] = t_inv[i]
            state_ref[...] = new_state
        finish()

    rows_of = lambda width: pl.BlockSpec((seqs, rows, width), lambda b, n: (b, n, 0))
    par_spec = pl.BlockSpec((1, HEAD), lambda b, n: (0, 0))
    out = pl.pallas_call(
        body, grid=(nbg, ng),
        in_specs=[rows_of(3 * N_HEADS * HEAD), pl.BlockSpec((seqs, rows, HEAD), lambda b, n: (b, n, ba_block)), par_spec, par_spec] + [_HBM] * nx,
        out_specs=[rows_of(N_HEADS * HEAD), pl.BlockSpec((seqs, N_HEADS, cps, HEAD, HEAD), lambda b, n: (b, 0, n, 0, 0)),
                   pl.BlockSpec((seqs, N_HEADS, cps, CHUNK, CHUNK), lambda b, n: (b, 0, n, 0, 0))] + [_HBM] * nx,
        out_shape=[jax.ShapeDtypeStruct((nb, lp, N_HEADS * HEAD), F32), jax.ShapeDtypeStruct((nb, N_HEADS, nc, HEAD, HEAD), F32),
                   jax.ShapeDtypeStruct((nb, N_HEADS, nc, CHUNK, CHUNK), F32)] + Exchange.out_shape(gather, False),
        scratch_shapes=[pltpu.VMEM(((seqs * N_HEADS), HEAD, HEAD), F32)] + (Exchange.scratch(nx) if nx else []),
        compiler_params=_cparams(("arbitrary", "arbitrary")), name="delta_fwd")(_by_sequence(qkv, lp), _by_sequence(ba, lp), alog, dtb, *gather)
    return [out[0].reshape(t, N_HEADS * HEAD)] + list(out[1:])


def delta_bwd(qkv, ba, ba_block, alog, dtb, states, t_invs, do, lp, scatter=()):
    t = qkv.shape[0]
    nb, nc = t // lp, lp // CHUNK
    seqs, cps = DELTA_STEP_BWD
    ng, rows = nc // cps, cps * CHUNK
    nx = len(scatter)
    nbg = nb // seqs

    def body(*refs):
        qkv_ref, ba_ref, al_ref, dt_ref, s_ref, t_ref, do_ref = refs[:7]
        dqkv_ref, dba_ref, dal_ref, ddt_ref = refs[7 + nx:11 + nx]
        dstate_ref = refs[11 + 2 * nx]
        b, step = pl.program_id(0), pl.program_id(1)
        finish = _ride(scatter, True, refs[7:7 + nx], refs[11 + nx:11 + 2 * nx], refs[12 + 2 * nx:], (b == 0) & (step == 0),
                       (b == nbg - 1) & (step == ng - 1))

        @pl.when(step == 0)
        def _():
            dstate_ref[...] = jnp.zeros_like(dstate_ref)

        @pl.when((b == 0) & (step == 0))
        def _():
            dal_ref[...] = jnp.zeros_like(dal_ref)
            ddt_ref[...] = jnp.zeros_like(ddt_ref)

        al, dtv = al_ref[...], dt_ref[...]
        d_al = jnp.zeros((1, HEAD), F32)
        d_dt = jnp.zeros((1, HEAD), F32)
        for c in reversed(range(cps)):
            rs = slice(c * CHUNK, (c + 1) * CHUNK)
            t_n = jnp.stack([t_ref[i // N_HEADS, i % N_HEADS, c] for i in range((seqs * N_HEADS))])
            s_n = jnp.stack([s_ref[i // N_HEADS, i % N_HEADS, c] for i in range((seqs * N_HEADS))])

            def f(q_, k_, v_, ba_, al_, dt_, s_, t_n=t_n):
                return _delta_chunk(q_, k_, v_, ba_, al_, dt_, s_, t_n)[:2]

            _, vjp = jax.vjp(f, _qkv_heads(qkv_ref, rs, 0), _qkv_heads(qkv_ref, rs, 1), _qkv_heads(qkv_ref, rs, 2), tuple(ba_ref[i, rs, :] for i in range(seqs)), al, dtv, s_n)
            grads = vjp((_heads_of(do_ref, rs, 0), dstate_ref[...]))
            for part in range(3):
                for i in range((seqs * N_HEADS)):
                    col = (part * N_HEADS + i % N_HEADS) * HEAD
                    dqkv_ref[i // N_HEADS, rs, col:col + HEAD] = grads[part][i]
            for i in range(seqs):
                dba_ref[i, rs, :] = grads[3][i]
            d_al, d_dt = d_al + grads[4], d_dt + grads[5]
            dstate_ref[...] = grads[6]
        dal_ref[...] += d_al
        ddt_ref[...] += d_dt
        finish()

    rows_of = lambda width: pl.BlockSpec((seqs, rows, width), lambda b, n: (b, ng - 1 - n, 0))
    par_spec = pl.BlockSpec((1, HEAD), lambda b, n: (0, 0))
    out = pl.pallas_call(
        body, grid=(nbg, ng),
        in_specs=[rows_of(3 * N_HEADS * HEAD), pl.BlockSpec((seqs, rows, HEAD), lambda b, n: (b, ng - 1 - n, ba_block)), par_spec, par_spec,
                  pl.BlockSpec((seqs, N_HEADS, cps, HEAD, HEAD), lambda b, n: (b, 0, ng - 1 - n, 0, 0)),
                  pl.BlockSpec((seqs, N_HEADS, cps, CHUNK, CHUNK), lambda b, n: (b, 0, ng - 1 - n, 0, 0)), rows_of(N_HEADS * HEAD)] + [_HBM] * nx,
        out_specs=[rows_of(3 * N_HEADS * HEAD), rows_of(HEAD), par_spec, par_spec] + [_HBM] * nx,
        out_shape=[jax.ShapeDtypeStruct((nb, lp, 3 * N_HEADS * HEAD), F32), jax.ShapeDtypeStruct((nb, lp, HEAD), F32),
                   jax.ShapeDtypeStruct((1, HEAD), F32), jax.ShapeDtypeStruct((1, HEAD), F32)] + Exchange.out_shape(scatter, True),
        scratch_shapes=[pltpu.VMEM(((seqs * N_HEADS), HEAD, HEAD), F32)] + (Exchange.scratch(nx) if nx else []),
        compiler_params=_cparams(("arbitrary", "arbitrary")), name="delta_bwd")(
            _by_sequence(qkv, lp), _by_sequence(ba, lp), alog, dtb, states, t_invs, _by_sequence(do, lp), *scatter)
    return [out[0].reshape(t, 3 * N_HEADS * HEAD), out[1].reshape(t, HEAD)] + list(out[2:])


ATT_Q_TILE = 256
ATT_K_TILE = 512
ATT_K_TILE_BWD = 1024
ATT_SCALE = QK_DIM ** -0.5


def _tiles(end, size):
    return [(s, min(s + size, end)) for s in range(0, end, size)]


def _att_visible(q0, q1, k0, k1, keys_first):
    if k1 <= q0 + CHUNK and k0 >= PAD_ROWS:
        return None
    shape = (k1 - k0, q1 - q0) if keys_first else (q1 - q0, k1 - k0)
    qpos = q0 + lax.broadcasted_iota(jnp.int32, shape, 1 if keys_first else 0)
    kpos = k0 + lax.broadcasted_iota(jnp.int32, shape, 0 if keys_first else 1)
    shift = CHUNK.bit_length() - 1
    return (jnp.right_shift(kpos, shift) <= jnp.right_shift(qpos, shift)) & (kpos >= PAD_ROWS)


def _att_seq_specs(lp):
    return pl.BlockSpec((lp, QK_PAD), lambda b, h: (b, h)), pl.BlockSpec((lp, HEAD), lambda b, h: (b, h))


def flash_fwd(q, k, v, lp):
    t = q.shape[0]
    qk_seq, o_seq = _att_seq_specs(lp)

    def body(q_ref, k_ref, v_ref, o_ref, lse_ref):
        q_tiles = _tiles(lp, ATT_Q_TILE)

        def score_steps(q0, q1, out):
            def step(k0, k1):
                s = mm_nt(q_ref[q0:q1, :], k_ref[k0:k1, :])
                vis = _att_visible(q0, q1, k0, k1, False)
                s = s if vis is None else jnp.where(vis, s, NEG)
                out["scores"].append(s)
                row_max = jnp.max(s, -1, keepdims=True)
                out["m"] = row_max if out["m"] is None else jnp.maximum(out["m"], row_max)
            return [functools.partial(step, k0, k1) for k0, k1 in _tiles(q1, ATT_K_TILE)]

        cur = {"scores": [], "m": None}
        for step in score_steps(*q_tiles[0], cur):
            step()
        for i, (q0, q1) in enumerate(q_tiles):
            nxt = {"scores": [], "m": None}
            ahead = score_steps(*q_tiles[i + 1], nxt) if i + 1 < len(q_tiles) else []
            l = jnp.zeros((q1 - q0, 1), F32)
            acc = jnp.zeros((q1 - q0, HEAD), F32)
            for s, (k0, k1) in zip(cur["scores"], _tiles(q1, ATT_K_TILE), strict=True):
                if ahead:
                    ahead.pop(0)()
                p = jnp.exp2(s - cur["m"])
                l = l + jnp.sum(p, -1, keepdims=True)
                acc = acc + mm_nn(p, v_ref[k0:k1, :])
            for step in ahead:
                step()
            o_ref[q0:q1, :] = acc / l
            lse_ref[q0:q1, :] = jnp.broadcast_to(cur["m"] + jnp.log2(l), (q1 - q0, HEAD))
            cur = nxt

    big = jax.ShapeDtypeStruct((t, N_HEADS * HEAD), F32)
    return pl.pallas_call(
        body, grid=(t // lp, N_HEADS), in_specs=[qk_seq, qk_seq, o_seq], out_specs=[o_seq, o_seq], out_shape=[big, big],
        compiler_params=_cparams(("arbitrary", "arbitrary")), name="flash_fwd")(q, k, v)


def flash_bwd(q, k, v, o, lse, do, lp):
    t = q.shape[0]
    qk_seq, o_seq = _att_seq_specs(lp)

    def body(q_ref, k_ref, v_ref, o_ref, lse_ref, do_ref, dq_ref, dk_out_ref, dv_out_ref, dk_ref, dv_ref):
        dk_ref[...] = jnp.zeros_like(dk_ref)
        dv_ref[...] = jnp.zeros_like(dv_ref)
        for q0, q1 in _tiles(lp, ATT_Q_TILE):
            qb, dob = q_ref[q0:q1, :], do_ref[q0:q1, :]
            lse_row = jnp.transpose(lse_ref[q0:q1, :])[0:1, :]
            dob_ln2 = dob * math.log(2.0)
            dsum_row = jnp.sum(jnp.transpose(dob_ln2 * o_ref[q0:q1, :]), axis=0, keepdims=True)
            dq = jnp.zeros((q1 - q0, QK_PAD), F32)
            for k0, k1 in _tiles(q1, ATT_K_TILE_BWD):
                kb, vb = k_ref[k0:k1, :], v_ref[k0:k1, :]
                s = mm_nt(kb, qb)
                vis = _att_visible(q0, q1, k0, k1, True)
                s = s if vis is None else jnp.where(vis, s, NEG)
                p = jnp.exp2(s - lse_row)
                ds = p * (mm_nt(vb, dob_ln2) - dsum_row)
                dv_ref[k0:k1, :] += mm_nn(p, dob)
                dk_ref[k0:k1, :] += mm_nn(ds, qb)
                dq = dq + mm_tn(ds, kb)
            dq_ref[q0:q1, :] = dq.astype(dq_ref.dtype)
        dk_out_ref[...] = dk_ref[...].astype(dk_out_ref.dtype)
        dv_out_ref[...] = dv_ref[...].astype(dv_out_ref.dtype)

    narrow = _MXU_DTYPE
    return pl.pallas_call(
        body, grid=(t // lp, N_HEADS), in_specs=[qk_seq, qk_seq, o_seq, o_seq, o_seq, o_seq], out_specs=[qk_seq, qk_seq, o_seq],
        out_shape=[jax.ShapeDtypeStruct((t, N_HEADS * QK_PAD), narrow), jax.ShapeDtypeStruct((t, N_HEADS * QK_PAD), narrow),
                   jax.ShapeDtypeStruct((t, N_HEADS * HEAD), narrow)],
        scratch_shapes=[pltpu.VMEM((lp, QK_PAD), F32), pltpu.VMEM((lp, HEAD), F32)],
        compiler_params=_cparams(("arbitrary", "arbitrary")), name="flash_bwd")(q, k, v, o, lse, do)


def loss_head(h2, target, lp):
    nb, seq, d = target.shape
    cols = _pick(d, (512, 128))
    ncol = d // cols

    def body(h_ref, t_ref, loss_ref, dh_ref, acc_ref):
        b, j = pl.program_id(0), pl.program_id(1)

        @pl.when((b == 0) & (j == 0))
        def _():
            acc_ref[...] = jnp.zeros_like(acc_ref)

        err = h_ref[LEAD:, :] - t_ref[...]
        dh_ref[:LEAD, :] = jnp.zeros((LEAD, cols), F32)
        dh_ref[LEAD:, :] = err * (1.0 / d)
        acc_ref[...] += jnp.sum(err * err, axis=0, keepdims=True)

        @pl.when((b == nb - 1) & (j == ncol - 1))
        def _():
            loss_ref[...] = jnp.sum(acc_ref[...], axis=1, keepdims=True) * (0.5 / d)

    return pl.pallas_call(
        body, grid=(nb, ncol),
        in_specs=[pl.BlockSpec((None, lp, cols), lambda b, j: (b, 0, j)), pl.BlockSpec((None, seq, cols), lambda b, j: (b, 0, j))],
        out_specs=[pl.BlockSpec((1, 1), lambda b, j: (0, 0)), pl.BlockSpec((None, lp, cols), lambda b, j: (b, 0, j))],
        out_shape=[jax.ShapeDtypeStruct((1, 1), F32), jax.ShapeDtypeStruct((nb, lp, d), F32)],
        scratch_shapes=[pltpu.VMEM((1, cols), F32)], compiler_params=_cparams(("arbitrary", "arbitrary")), name="loss_head")(h2, target)


def gated_out(name, o, gate, gain, w, res):
    t, kw = o.shape
    d = w.shape[1]
    tm = _pick(t, (512, 256, 128))

    def body(*refs):
        o_ref, gate_ref = refs[:2]
        w_ref, r_ref, h_ref, g_ref = refs[-4:]
        if gain is None:
            g_ref[...] = _f_gate(o_ref[...], gate_ref[...])[0].astype(g_ref.dtype)
        else:
            for h in range(N_HEADS):
                cs = slice(h * HEAD, (h + 1) * HEAD)
                g_ref[:, cs] = _f_out_gate(o_ref[:, cs], gate_ref[:, cs], refs[2][...])[0].astype(g_ref.dtype)
        h_ref[...] = r_ref[...] + _dot(g_ref[...], w_ref[...], ((1,), (0,)))

    rows = lambda width: pl.BlockSpec((tm, width), lambda i: (i, 0))
    whole = lambda a: pl.BlockSpec(a.shape, lambda i: (0, 0))
    params = [] if gain is None else [gain]
    return pl.pallas_call(
        body, grid=(t // tm,), in_specs=[rows(kw), rows(kw)] + [whole(p) for p in params] + [whole(w), rows(d)], out_specs=[rows(d), rows(kw)],
        out_shape=[jax.ShapeDtypeStruct((t, d), F32), jax.ShapeDtypeStruct((t, kw), _MXU_DTYPE)],
        compiler_params=_cparams(("parallel",)), name=name)(o, gate, *params, w, res)


def embed_norm(x, meta, gain, lp, gather=()):
    nb, seq, d = x.shape
    nblk, nx = lp // LEAD, len(gather)

    def body(*refs):
        x_ref, meta_ref, g_ref = refs[:3]
        h_ref, hn_ref = refs[3 + nx:5 + nx]
        b, i = pl.program_id(0), pl.program_id(1)
        finish = _ride(gather, False, refs[3:3 + nx], refs[5 + nx:5 + 2 * nx], refs[5 + 2 * nx:], (b == 0) & (i == 0), (b == nb - 1) & (i == nblk - 1),
                       two_level=True)

        @pl.when(i == 0)
        def _():
            h_ref[:PAD_ROWS, :] = jnp.zeros((PAD_ROWS, d), F32)
            h_ref[PAD_ROWS:, :] = meta_ref[...]

        @pl.when(i > 0)
        def _():
            h_ref[...] = x_ref[...]

        hn_ref[...] = _rms(h_ref[...], g_ref[...]).astype(hn_ref.dtype)
        finish()

    rows = pl.BlockSpec((LEAD, d), lambda b, i: (b * nblk + i, 0))
    out = pl.pallas_call(
        body, grid=(nb, nblk),
        in_specs=[pl.BlockSpec((None, LEAD, d), lambda b, i: (b, jnp.maximum(i - 1, 0), 0)), pl.BlockSpec((N_META, d), lambda b, i: (0, 0)),
                  pl.BlockSpec((1, d), lambda b, i: (0, 0))] + [_HBM] * nx,
        out_specs=[rows, rows] + [_HBM] * nx,
        out_shape=[jax.ShapeDtypeStruct((nb * lp, d), F32), jax.ShapeDtypeStruct((nb * lp, d), _MXU_DTYPE)] + Exchange.out_shape(gather, False),
        scratch_shapes=Exchange.scratch(nx) if nx else [],
        compiler_params=_cparams(("arbitrary", "arbitrary")), name="embed_norm")(x, meta, gain, *gather)
    return list(out)


def meta_grad(dh0):
    nb, _, d = dh0.shape

    def body(g_ref, o_ref):
        @pl.when(pl.program_id(0) == 0)
        def _():
            o_ref[...] = jnp.zeros_like(o_ref)

        o_ref[...] += g_ref[PAD_ROWS:LEAD, :]

    return pl.pallas_call(
        body, grid=(nb,), in_specs=[pl.BlockSpec((None, LEAD, d), lambda b: (b, 0, 0))],
        out_specs=pl.BlockSpec((N_META, d), lambda b: (0, 0)), out_shape=jax.ShapeDtypeStruct((N_META, d), F32),
        compiler_params=_cparams(("arbitrary",)), name="meta_grad")(dh0)


_HBM = pl.BlockSpec(memory_space=pltpu.HBM)


def _mesh_pos():
    x, y, c = lax.axis_index("x"), lax.axis_index("y"), lax.axis_index("c")
    return x, y, c


def _peer(x, y, c, k):
    px = 1 - x if k & 4 else x
    py = 1 - y if k & 2 else y
    pc = 1 - c if k & 1 else c
    return (px, py, pc), 4 * px + 2 * py + pc


class Exchange:
    def __init__(self, x_refs, out_refs, send_sems, recv_sems, local_sems, scatter):
        self.x_refs, self.out_refs, self.scatter = x_refs, out_refs, scatter
        self.send_sems, self.recv_sems, self.local_sems = send_sems, recv_sems, local_sems
        self.pos = _mesh_pos()
        x, y, c = self.pos
        self.me = 4 * x + 2 * y + c

    @staticmethod
    def scratch(n):
        return [pltpu.SemaphoreType.DMA((n, N_DEV - 1)), pltpu.SemaphoreType.DMA((n, N_DEV - 1)), pltpu.SemaphoreType.DMA((n,))]

    @staticmethod
    def out_shape(bufs, scatter):
        return [jax.ShapeDtypeStruct(b.shape if scatter else (N_DEV,) + b.shape, b.dtype) for b in bufs]

    def _local(self, i):
        return pltpu.make_async_copy(self.x_refs[i].at[self.me] if self.scatter else self.x_refs[i], self.out_refs[i].at[self.me], self.local_sems.at[i])

    def _copy(self, i, k, landing):
        peer, peer_id = _peer(*self.pos, k)
        src = self.x_refs[i].at[peer_id] if self.scatter else self.x_refs[i]
        return pltpu.make_async_remote_copy(src_ref=src, dst_ref=self.out_refs[i].at[peer_id if landing else self.me],
                                            send_sem=self.send_sems.at[i, k - 1], recv_sem=self.recv_sems.at[i, k - 1],
                                            device_id=peer, device_id_type=pl.DeviceIdType.MESH)

    def start(self):
        for i in range(len(self.x_refs)):
            self._local(i).start()
        for k in range(1, N_DEV):
            for i in range(len(self.x_refs)):
                self._copy(i, k, False).start()

    def wait(self):
        for k in range(1, N_DEV):
            for i in range(len(self.x_refs)):
                self._copy(i, k, True).wait_recv()
        for k in range(1, N_DEV):
            for i in range(len(self.x_refs)):
                self._copy(i, k, False).wait_send()
        for i in range(len(self.x_refs)):
            self._local(i).wait()


class TwoLevelGather(Exchange):
    DIRECT = (1, 4, 2, 6)
    FROM_CHIPS = (4, 2, 6)

    def _forward(self, i, k):
        _, origin = _peer(*self.pos, k)
        sibling, _ = _peer(*self.pos, 1)
        block = self.out_refs[i].at[origin]
        return pltpu.make_async_remote_copy(src_ref=block, dst_ref=block, send_sem=self.send_sems.at[i, (k ^ 1) - 1],
                                            recv_sem=self.recv_sems.at[i, (k ^ 1) - 1], device_id=sibling, device_id_type=pl.DeviceIdType.MESH)

    def start(self):
        assert not self.scatter
        for i in range(len(self.x_refs)):
            self._local(i).start()
        for k in self.DIRECT:
            for i in range(len(self.x_refs)):
                self._copy(i, k, False).start()

    def wait(self):
        n = range(len(self.x_refs))
        for k in self.FROM_CHIPS:
            for i in n:
                self._copy(i, k, True).wait_recv()
                self._forward(i, k).start()
        for k in (1, 5, 3, 7):
            for i in n:
                self._copy(i, k, True).wait_recv()
        for k in self.DIRECT:
            for i in n:
                self._copy(i, k, False).wait_send()
        for k in self.FROM_CHIPS:
            for i in n:
                self._forward(i, k).wait_send()
        for i in n:
            self._local(i).wait()


def _exchange(name, bufs, scatter):
    n = len(bufs)

    def body(*refs):
        ex = Exchange(refs[:n], refs[n:2 * n], *refs[2 * n:], scatter)
        ex.start()
        ex.wait()

    return pl.pallas_call(body, in_specs=[_HBM] * n, out_specs=[_HBM] * n, out_shape=Exchange.out_shape(bufs, scatter),
                          scratch_shapes=Exchange.scratch(n), name=name)(*bufs)


def _f_rms(x, g):
    return (_rms(x, g),)


def _f_rms2(x, g1, g2):
    r = x * lax.rsqrt(jnp.sum(x * x, -1, keepdims=True) / x.shape[-1] + EPS)
    return r * g1, r * g2


@jax.custom_vjp
def _out_gate(o, gate, gain):
    return _rms(o, gain) * _silu(gate)


def _out_gate_bwd(res, g):
    o, gate, gain = res
    r = lax.rsqrt(jnp.sum(o * o, -1, keepdims=True) / o.shape[-1] + EPS)
    n = o * r
    s = _sigmoid(gate)
    g_norm = g * (gate * s)
    d_gate = g * (n * gain) * (s * (1.0 + gate * (1.0 - s)))
    gn = g_norm * gain
    d_o = r * (gn - n * (jnp.sum(gn * n, -1, keepdims=True) / o.shape[-1]))
    return d_o, d_gate, jnp.sum(g_norm * n, 0, keepdims=True)


_out_gate.defvjp(lambda o, gate, gain: (_out_gate(o, gate, gain), (o, gate, gain)), _out_gate_bwd)


def _f_out_gate(o, gate, gain):
    return (_out_gate(o, gate, gain),)


def _f_gate(o, gate):
    return (o * _silu(gate),)


def _swap_rope_halves(x):
    return pltpu.roll(x, ROPE // 2, 1) + pltpu.roll(x, HEAD - ROPE // 2, 1)


def _qk_final_inv_rms(nope, rope_in):
    ms = (jnp.sum(nope * nope, -1, keepdims=True) + jnp.sum(rope_in * rope_in, -1, keepdims=True)) / QK_DIM
    return lax.rsqrt(ms + EPS)


@functools.partial(jax.custom_vjp, nondiff_argnums=(0,))
def _qk_final(scale, nope, rope_in, g_nope, g_rope, cos, sin):
    r = _qk_final_inv_rms(nope, rope_in)
    b = rope_in * (r * g_rope)
    out = jnp.concatenate([nope * (r * g_nope), b * cos + _swap_rope_halves(b) * sin], axis=1)
    return out if scale == 1.0 else out * scale


def _qk_final_fwd(scale, nope, rope_in, g_nope, g_rope, cos, sin):
    return _qk_final(scale, nope, rope_in, g_nope, g_rope, cos, sin), (nope, rope_in, g_nope, g_rope, cos, sin)


def _qk_final_bwd(scale, res, g):
    nope, rope_in, g_nope, g_rope, cos, sin = res
    r = _qk_final_inv_rms(nope, rope_in)
    ga, gb = g[:, :HEAD], g[:, HEAD:]
    if scale != 1.0:
        ga, gb = ga * scale, gb * scale
    db = gb * cos + _swap_rope_halves(gb * sin)
    t_a, t_b = ga * nope, db * rope_in
    d_r = jnp.sum(t_a * g_nope + t_b * g_rope, -1, keepdims=True)
    c = d_r * (r * r * r) * (-1.0 / QK_DIM)
    d_nope = ga * (r * g_nope) + nope * c
    d_rope = db * (r * g_rope) + rope_in * c
    d_g_nope = jnp.sum(t_a * r, 0, keepdims=True)
    d_g_rope = jnp.sum(t_b * r, 0, keepdims=True)
    return d_nope, d_rope, d_g_nope, d_g_rope, jnp.zeros_like(cos), jnp.zeros_like(sin)


_qk_final.defvjp(_qk_final_fwd, _qk_final_bwd)


def _f_qk_final(scale, nope, rope_in, g_nope, g_rope, cos, sin):
    return (_qk_final(scale, nope, rope_in, g_nope, g_rope, cos, sin),)


def _rope_tables(lp):
    half = ROPE // 2
    pos = jnp.maximum(jnp.arange(lp) - PAD_ROWS, 0)
    inv = ROPE_THETA ** (-jnp.arange(half, dtype=F32) / half)
    ang = pos.astype(F32)[:, None] * inv[None, :]
    zeros = jnp.zeros((lp, HEAD - ROPE), F32)
    cos = jnp.concatenate([jnp.cos(ang), jnp.cos(ang), zeros], 1)
    sin = jnp.concatenate([-jnp.sin(ang), jnp.sin(ang), zeros], 1)
    return cos, sin


def _pad_lanes(w, width=HEAD):
    return jnp.pad(w, ((0, 0), (0, width - w.shape[1])))


def _pad_rows(w, rows=HEAD):
    return jnp.pad(w, ((0, rows - w.shape[0]), (0, 0)))


def _split_heads_qk_t(w_t):
    k = w_t.shape[1]
    return jnp.pad(w_t.reshape(N_HEADS, QK_DIM, k), ((0, 0), (0, QK_PAD - QK_DIM), (0, 0))).reshape(N_HEADS * QK_PAD, k)


def _merge_heads_qk_t(g_t):
    k = g_t.shape[1]
    return g_t.reshape(N_HEADS, QK_PAD, k)[:, :QK_DIM].reshape(N_HEADS * QK_DIM, k)


@functools.partial(jax.custom_vjp, nondiff_argnums=(0,))
def _q_final(scale, qh, g_nope, g_rope, cos, sin):
    return _qk_final(scale, qh[:, :HEAD], qh[:, HEAD:], g_nope, g_rope, cos, sin)


def _q_final_bwd(scale, res, g):
    qh, g_nope, g_rope, cos, sin = res
    grads = _qk_final_bwd(scale, (qh[:, :HEAD], qh[:, HEAD:], g_nope, g_rope, cos, sin), g)
    return (jnp.concatenate(grads[:2], axis=1),) + tuple(grads[2:])


_q_final.defvjp(lambda scale, qh, *rest: (_q_final(scale, qh, *rest), (qh,) + rest), _q_final_bwd)


def _f_q_final(scale, qh, g_nope, g_rope, cos, sin):
    return (_q_final(scale, qh, g_nope, g_rope, cos, sin),)


def local_step(x, target, w, deferred=None):
    nb, seq, d = x.shape
    lp = seq + LEAD
    t = nb * lp
    tr = _pick(lp, (544, 128))
    ntab = lp // tr
    mxu = _MXU_DTYPE
    kw = N_HEADS * HEAD

    a_conv = w["a_conv"].T
    alog, dtb, o_gain = _pad_lanes(w["a_log"]), _pad_lanes(w["a_dt_bias"]), w["a_o_gain"]
    a_norm, kv_norm, b_norm = w["a_norm"], w["kv_norm"][None, :], w["b_norm"]
    lat_norm, qlat_norm = w["kv_latent_norm"][None, :], w["b_q_latent_norm"]
    kg_nope, kg_rope = w["k_gain"][None, :HEAD], _pad_lanes(w["k_gain"][None, HEAD:])
    qg_nope, qg_rope = w["b_q_gain"][:, :HEAD], _pad_lanes(w["b_q_gain"][:, HEAD:])
    cos, sin = _rope_tables(lp)

    h0, hn, *gathered = embed_norm(x, w["meta_tokens"].T, a_norm, lp, gather=deferred.first_gather_bufs if deferred else ())
    if deferred:
        w = {**w, **deferred.finish_first(gathered)}
    a_w_in_t = w["a_w_in"].astype(mxu)
    w_qkv_t, w_gba_t = a_w_in_t[:3 * kw], _pad_rows(a_w_in_t[3 * kw:], kw + HEAD)
    z_qkv = matmul("a_in_qkv", hn, w_qkv_t, "nt", out_dtype=mxu)
    z_gba = matmul("a_in_gate_ba", hn, w_gba_t, "nt")
    ba_block = kw // HEAD
    qkv_a, y_conv = conv_fwd(z_qkv, a_conv, lp)
    o_a, states, t_invs, *gathered = delta_fwd(qkv_a, z_gba, ba_block, alog, dtb, lp, gather=deferred.gather_bufs if deferred else ())
    if deferred:
        w = {**w, **deferred.finish(gathered)}
    a_w_out = w["a_w_out"].astype(mxu)
    w_down = _pad_lanes(w["kv_w_down"], KV_RANK + HEAD).astype(mxu)
    w_ukv_t = jnp.concatenate([w["kv_w_uk"], w["kv_w_uv"]], 0).astype(mxu)
    b_w_in_t = w["b_w_in"].astype(mxu)
    w_cq_t, w_gb_t = b_w_in_t[:Q_RANK], b_w_in_t[Q_RANK:]
    w_q_t = _split_heads_qk_t(w["b_w_uq"]).astype(mxu)
    b_w_out = w["b_w_out"].astype(mxu)
    og_args = [Arg(o_a, bc=HEAD, ph=True, diff=True), Arg(z_gba, bc=HEAD, ph=True, diff=True, gdt=mxu), Arg(o_gain, "par", diff=True)]
    h1, og_a = gated_out("a_out", o_a, z_gba, o_gain, a_w_out, h0)

    hk, hb = row_call("b_norms_fwd", _f_rms2, [Arg(h1), Arg(kv_norm, "par"), Arg(b_norm, "par")], [(d, mxu, d, False), (d, mxu, d, False)], tr)
    c_down = matmul("kv_down", hk, w_down, "nn")
    c_kv_arg = Arg(c_down, bc=KV_RANK, diff=True, gdt=mxu)
    k_pe_arg = Arg(c_down, bc=HEAD, base=KV_RANK // HEAD, diff=True)
    c_q_raw = matmul("b_in_q", hb, w_cq_t, "nt")
    gate_b = matmul("b_in_gate", hb, w_gb_t, "nt")
    (c_kv,) = row_call("kv_latent_fwd", _f_rms, [c_kv_arg, Arg(lat_norm, "par")], [(KV_RANK, mxu, KV_RANK, False)], tr)
    (c_q,) = row_call("q_latent_fwd", _f_rms, [Arg(c_q_raw), Arg(qlat_norm, "par")], [(Q_RANK, mxu, Q_RANK, False)], tr)
    k_nope = matmul("k_up", c_kv, w_ukv_t[:kw], "nt")
    v_b = matmul("v_up", c_kv, w_ukv_t[kw:], "nt", out_dtype=mxu)
    q_up = matmul("q_up", c_q, w_q_t, "nt")
    tabs = [Arg(cos, "tab"), Arg(sin, "tab")]
    k_args = [Arg(k_nope, bc=HEAD, ph=True, diff=True, gdt=mxu), k_pe_arg, Arg(kg_nope, "par", diff=True), Arg(kg_rope, "par", diff=True)] + tabs
    q_args = [Arg(q_up, bc=QK_PAD, ph=True, diff=True, gdt=mxu), Arg(qg_nope, "par", diff=True), Arg(qg_rope, "par", diff=True)] + tabs
    f_k_final, f_q_final = functools.partial(_f_qk_final, 1.0), functools.partial(_f_q_final, ATT_SCALE * math.log2(math.e))
    (k_fin,) = row_call("k_final_fwd", f_k_final, k_args, [(N_HEADS * QK_PAD, mxu, QK_PAD, True)], tr, nh=N_HEADS, ntab=ntab)
    (q_fin,) = row_call("q_final_fwd", f_q_final, q_args, [(N_HEADS * QK_PAD, mxu, QK_PAD, True)], tr, nh=N_HEADS, ntab=ntab)
    o_b, lse = flash_fwd(q_fin, k_fin, v_b, lp)
    gb_args = [Arg(o_b, diff=True), Arg(gate_b, diff=True, gdt=mxu)]
    h2, og_b = gated_out("b_out", o_b, gate_b, None, b_w_out, h1)

    loss, dh2 = loss_head(h2.reshape(nb, lp, d), target, lp)
    dh2 = dh2.reshape(t, d)
    grads = {}

    d_og_b = matmul("b_out_dx", dh2, b_w_out, "nt", out_dtype=mxu)
    grads["b_w_out"] = matmul("b_out_dw", og_b, dh2, "tn")
    d_o_b, d_gate_b = row_vjp_call("b_gate_bwd", _f_gate, gb_args, [Arg(d_og_b)], tr)
    dq_fin, dk_fin, dv_b = flash_bwd(q_fin, k_fin, v_b, o_b, lse, d_o_b, lp)
    dq_up, d_qg_nope, d_qg_rope = row_vjp_call(
        "q_final_bwd", f_q_final, q_args, [Arg(dq_fin, bc=QK_PAD, ph=True)], tr, nh=N_HEADS, ntab=ntab)
    dk_nope, dk_pe, d_kg_nope, d_kg_rope = row_vjp_call(
        "k_final_bwd", f_k_final, k_args, [Arg(dk_fin, bc=QK_PAD, ph=True)], tr, nh=N_HEADS, ntab=ntab)
    grads["b_q_gain"] = jnp.concatenate([d_qg_nope, d_qg_rope[:, :ROPE]], 1)
    grads["k_gain"] = jnp.concatenate([d_kg_nope, d_kg_rope[:, :ROPE]], 1)[0]
    d_c_q = matmul("q_up_dx", dq_up, w_q_t, "nn")
    grads["b_w_uq"] = _merge_heads_qk_t(matmul("q_up_dw", dq_up, c_q, "tn"))
    d_c_kv = matmul("k_up_dx", dk_nope, w_ukv_t[:kw], "nn")
    d_c_kv = matmul("v_up_dx", dv_b, w_ukv_t[kw:], "nn", res=d_c_kv)
    grads["kv_w_uk"], grads["kv_w_uv"] = matmul("k_up_dw", dk_nope, c_kv, "tn"), matmul("v_up_dw", dv_b, c_kv, "tn")
    d_c_q_raw, grads["b_q_latent_norm"] = row_vjp_call(
        "q_latent_bwd", _f_rms, [Arg(c_q_raw, diff=True, gdt=mxu), Arg(qlat_norm, "par", diff=True)], [Arg(d_c_q)], tr)
    d_c_kv_raw, d_lat = row_vjp_call(
        "kv_latent_bwd", _f_rms, [c_kv_arg, Arg(lat_norm, "par", diff=True)], [Arg(d_c_kv)], tr)
    grads["kv_latent_norm"] = d_lat[0]
    d_hb = matmul("b_in_q_dx", d_c_q_raw, w_cq_t, "nn")
    d_hb = matmul("b_in_gate_dx", d_gate_b, w_gb_t, "nn", res=d_hb, out_dtype=mxu)
    grads["b_w_in"] = jnp.concatenate([matmul("b_in_q_dw", d_c_q_raw, hb, "tn"), matmul("b_in_gate_dw", d_gate_b, hb, "tn")], 0)
    d_c_down = jnp.concatenate([d_c_kv_raw, dk_pe.astype(mxu)], 1)
    d_hk = matmul("kv_down_dx", d_c_down, w_down, "nt", out_dtype=mxu)
    grads["kv_w_down"] = matmul("kv_down_dw", hk, d_c_down, "tn")[:, :KV_RANK + ROPE]
    dh1, d_kv_norm, grads["b_norm"] = row_vjp_call(
        "b_norms_bwd", lambda x_, g1, g2: _f_rms2(x_, g1, g2) + (x_,),
        [Arg(h1, diff=True), Arg(kv_norm, "par", diff=True), Arg(b_norm, "par", diff=True)], [Arg(d_hk), Arg(d_hb), Arg(dh2)], tr)
    grads["kv_norm"] = d_kv_norm[0]

    d_og_a = matmul("a_out_dx", dh1, a_w_out, "nt", out_dtype=mxu)
    grads["a_w_out"] = matmul("a_out_dw", og_a, dh1, "tn")
    d_o_a, d_gate_a, grads["a_o_gain"] = row_vjp_call(
        "a_out_gate_bwd", _f_out_gate, og_args, [Arg(d_og_a, bc=HEAD, ph=True)], tr, nh=N_HEADS)
    dqkv_a, d_ba, d_alog, d_dtb, *received = delta_bwd(qkv_a, z_gba, ba_block, alog, dtb, states, t_invs, d_o_a, lp,
                                                        scatter=deferred.scatter_bufs(grads) if deferred else ())
    grads["a_log"], grads["a_dt_bias"] = d_alog[:, :N_HEADS], d_dtb[:, :N_HEADS]
    dz_qkv, d_conv = conv_bwd(z_qkv, y_conv, a_conv, dqkv_a, lp)
    grads["a_conv"] = d_conv.T
    dz_gba = jnp.concatenate([d_gate_a, d_ba.astype(mxu)], 1)
    grads["a_w_in"] = jnp.concatenate([matmul("a_in_qkv_dw", dz_qkv, hn, "tn"), matmul("a_in_gate_ba_dw", dz_gba, hn, "tn")[:kw + 2 * N_HEADS]], 0)
    ride = deferred.last_scatter_bufs(grads) if deferred else ((), ())
    d_hn = matmul("a_in_qkv_dx", dz_qkv, w_qkv_t, "nn", scatter=ride[0])
    if ride[0]:
        d_hn, *received_half = d_hn
        received = list(received) + received_half
    d_hn = matmul("a_in_gate_ba_dx", dz_gba, w_gba_t, "nn", res=d_hn, out_dtype=mxu, scatter=ride[1])
    if ride[1]:
        d_hn, *received_half = d_hn
        received = list(received) + received_half
    dh0, grads["a_norm"] = row_vjp_call("a_norm_bwd", lambda x_, g_: _f_rms(x_, g_) + (x_,),
                                        [Arg(h0, diff=True), Arg(a_norm, "par", diff=True)], [Arg(d_hn), Arg(dh1)], tr)
    dh0 = dh0.reshape(nb, lp, d)
    grads["meta_tokens"] = meta_grad(dh0).T
    return loss, dh0[:, LEAD:], grads, received


_SHARDED = (
    ("meta_tokens", True, False), ("a_norm", True, False), ("a_w_in", True, True), ("a_conv", True, False), ("a_w_out", False, True),
    ("kv_w_down", False, True), ("kv_w_uk", True, True), ("kv_w_uv", True, True), ("b_w_in", True, True), ("b_w_uq", True, True),
    ("b_w_out", False, True))
_REPLICATED = ("a_log", "a_dt_bias", "a_o_gain", "kv_norm", "kv_latent_norm", "k_gain", "b_norm", "b_q_latent_norm", "b_q_gain")
_ALL_WEIGHTS = ("meta_tokens", "a_norm", "a_w_in", "a_conv", "a_log", "a_dt_bias", "a_o_gain", "a_w_out", "kv_norm", "kv_w_down",
                "kv_latent_norm", "kv_w_uk", "kv_w_uv", "k_gain", "b_norm", "b_w_in", "b_q_latent_norm", "b_w_uq", "b_q_gain", "b_w_out")


def _round_up(n, m):
    return (n + m - 1) // m * m


def _pack_rows(pieces, row_multiple):
    padded = []
    for p in pieces:
        n = p.shape[-1]
        padded.append(jnp.pad(p, [(0, 0)] * (p.ndim - 1) + [(0, _round_up(n, PACK_COLS) - n)]))
    flat = jnp.concatenate(padded, -1)
    rows = _round_up(flat.shape[-1] // PACK_COLS, row_multiple)
    flat = jnp.pad(flat, [(0, 0)] * (flat.ndim - 1) + [(0, rows * PACK_COLS - flat.shape[-1])])
    return flat.reshape(flat.shape[:-1] + (rows, PACK_COLS))


def _unpack_rows(buf, sizes):
    flat = buf.reshape(buf.shape[:-2] + (-1,))
    out, off = [], 0
    for n in sizes:
        out.append(flat[..., off:off + n])
        off += _round_up(n, PACK_COLS)
    return out


def _shard_2d(a):
    return a.reshape(a.shape[-2:]) if a.ndim > 2 else a


def _kl_shard(a, by_cols):
    return _shard_2d(a).T if by_cols else _shard_2d(a)


_GROUPS_FIRST = (("a_w_in",),)
_GROUPS_LATER = (("a_w_out", "b_w_in", "b_w_out"), ("b_w_uq",), ("kv_w_down",), ("kv_w_uk", "kv_w_uv"))
_SMALL_SHARDED = ("meta_tokens", "a_norm", "a_conv")
_BY_COLS = {name: by_cols for name, by_cols, _ in _SHARDED}
ROW_ALIGN = 16


def _stack_rows(pieces):
    padded, starts, row = [], [], 0
    for p in pieces:
        r = p.shape[-2]
        padded.append(jnp.pad(p, [(0, 0)] * (p.ndim - 2) + [(0, _round_up(r, ROW_ALIGN) - r), (0, 0)]))
        starts.append(row)
        row += _round_up(r, ROW_ALIGN)
    return jnp.concatenate(padded, -2), starts


def _stack_group(arrays_by_name, names):
    arrays = [arrays_by_name[n].astype(BF16) for n in names]
    buf, starts = _stack_rows(arrays)
    return buf, [(n, s, a.shape[-2]) for n, s, a in zip(names, starts, arrays, strict=True)]


def _stack_groups(arrays_by_name, groups):
    stacked = [_stack_group(arrays_by_name, names) for names in groups]
    return [b for b, _ in stacked], [entries for _, entries in stacked]


def _full_from_gathered(gathered, layout):
    full = {}
    for got, entries in zip(gathered, layout, strict=True):
        for name, start, rows in entries:
            full[name] = got[:, start:start + rows].reshape(N_DEV * rows, got.shape[-1])
    return full


def gather_small_weights(local):
    small = [_kl_shard(local[n], _BY_COLS[n]) for n in _SMALL_SHARDED]
    (gathered,) = _exchange("all_gather", [_pack_rows([s.reshape(-1) for s in small], 8)], scatter=False)
    full = {}
    for name, part, sh in zip(_SMALL_SHARDED, _unpack_rows(gathered, [s.size for s in small]), small, strict=True):
        full[name] = part.reshape(N_DEV * sh.shape[0], sh.shape[1])
    full["a_norm"] = full["a_norm"].reshape(1, -1)
    return full


class LaterExchanges:
    def __init__(self, local):
        shards = {n: _kl_shard(local[n], _BY_COLS[n]) for names in _GROUPS_FIRST + _GROUPS_LATER for n in names}
        self.first_gather_bufs, self.first_layout = _stack_groups(shards, _GROUPS_FIRST)
        self.gather_bufs, self.layout = _stack_groups(shards, _GROUPS_LATER)

    def finish_first(self, gathered):
        return _full_from_gathered(gathered, self.first_layout)

    def finish(self, gathered):
        return _full_from_gathered(gathered, self.layout)

    def scatter_bufs(self, grads):
        return _stack_groups(_owner_slices(grads, _GROUPS_LATER), _GROUPS_LATER)[0]

    def last_scatter_bufs(self, grads):
        (buf,), self.last_layout = _stack_groups(_owner_slices(grads, _GROUPS_FIRST), _GROUPS_FIRST)
        first = buf.shape[-1] * 5 // 8 // HEAD * HEAD
        return [buf[..., :first]], [buf[..., first:]]


def _owner_slices(grads, groups):
    return {n: grads[n].reshape(N_DEV, -1, grads[n].shape[-1]) for names in groups for n in names}


def reduce_contributions(name, recv):
    _, r, c = recv.shape
    tr = max(d for d in range(8, 513, 8) if r % d == 0 and (d % ROW_ALIGN == 0 or recv.dtype == F32))

    def body(g_ref, o_ref):
        g = g_ref[0].astype(F32)
        for dev in range(1, N_DEV):
            g = g + g_ref[dev].astype(F32)
        o_ref[...] = g

    return pl.pallas_call(
        body, grid=(r // tr,), in_specs=[pl.BlockSpec((N_DEV, tr, c), lambda i: (0, i, 0))], out_specs=pl.BlockSpec((tr, c), lambda i: (i, 0)),
        out_shape=jax.ShapeDtypeStruct((r, c), F32), compiler_params=_cparams(("arbitrary",)), name=name)(recv)


def adamw_all(gs, ws, ms, vs):
    n = len(gs)

    def body(*refs):
        for i in range(n):
            g_ref, w_ref, m_ref, v_ref = (refs[j * n + i] for j in range(4))
            d_ref, mo_ref, vo_ref = (refs[(4 + j) * n + i] for j in range(3))
            g = g_ref[...]
            m_new = ADAM_B1 * m_ref[...] + (1.0 - ADAM_B1) * g
            v_new = ADAM_B2 * v_ref[...] + (1.0 - ADAM_B2) * (g * g)
            m_hat = m_new / (1.0 - ADAM_B1 ** ADAM_STEP)
            v_hat = v_new / (1.0 - ADAM_B2 ** ADAM_STEP)
            d_ref[...] = -ADAM_LR * (m_hat / (jnp.sqrt(v_hat) + ADAM_EPS) + ADAM_WD * w_ref[...])
            mo_ref[...] = m_new
            vo_ref[...] = v_new

    out = [jax.ShapeDtypeStruct(g.shape, F32) for g in gs] * 3
    res = pl.pallas_call(body, out_shape=out, compiler_params=pltpu.CompilerParams(vmem_limit_bytes=VMEM_LIMIT), name="adamw_all")(*gs, *ws, *ms, *vs)
    return res[:n], res[n:2 * n], res[2 * n:]


def kernel(x, meta_tokens, a_norm, a_w_in, a_conv, a_log, a_dt_bias, a_o_gain, a_w_out, kv_norm, kv_w_down, kv_latent_norm, kv_w_uk, kv_w_uv, k_gain, b_norm, b_w_in, b_q_latent_norm, b_w_uq, b_q_gain, b_w_out, loss_target, m_meta_tokens, m_a_norm, m_a_w_in, m_a_conv, m_a_log, m_a_dt_bias, m_a_o_gain, m_a_w_out, m_kv_norm, m_kv_w_down, m_kv_latent_norm, m_kv_w_uk, m_kv_w_uv, m_k_gain, m_b_norm, m_b_w_in, m_b_q_latent_norm, m_b_w_uq, m_b_q_gain, m_b_w_out, v_meta_tokens, v_a_norm, v_a_w_in, v_a_conv, v_a_log, v_a_dt_bias, v_a_o_gain, v_a_w_out, v_kv_norm, v_kv_w_down, v_kv_latent_norm, v_kv_w_uk, v_kv_w_uv, v_k_gain, v_b_norm, v_b_w_in, v_b_q_latent_norm, v_b_w_uq, v_b_q_gain, v_b_w_out):
    given = dict(locals())
    local_w = {n: given[n] for n in _ALL_WEIGHTS}
    full = gather_small_weights(local_w)
    for n in _REPLICATED:
        full[n] = local_w[n]
    later = LaterExchanges(local_w)

    loss_part, grad_x, grads, received_riding = local_step(x, loss_target, full, later)

    exact = [grads[n].reshape(N_DEV, -1) for n in _SMALL_SHARDED]
    exact += [jnp.broadcast_to(grads[n].reshape(1, -1), (N_DEV, grads[n].size)) for n in _REPLICATED]
    exact.append(jnp.broadcast_to(loss_part, (N_DEV, 1)))
    received = list(received_riding) + list(_exchange("all_to_all", [_pack_rows(exact, 8)], scatter=True))
    layout = later.layout + later.last_layout
    summed = [reduce_contributions(f"reduce_{i}", r) for i, r in enumerate(received)]
    n_later = len(later.layout)
    summed = summed[:n_later] + [jnp.concatenate(summed[n_later:n_later + 2], 1)] + summed[n_later + 2:]

    grad_kl = {}
    for got, entries in zip(summed, layout):
        for n, start, rows in entries:
            grad_kl[n] = got[start:start + rows]
    parts = _unpack_rows(summed[-1], [p.shape[1] for p in exact])
    for n, part in zip(_SMALL_SHARDED + _REPLICATED, parts, strict=False):
        grad_kl[n] = part
    loss = parts[-1][0]

    def natural_2d(n, a):
        shape = _shard_2d(local_w[n]).shape if local_w[n].ndim > 1 else (1, local_w[n].size)
        return a.reshape(shape[::-1]).T if _BY_COLS.get(n, False) else a.reshape(shape)

    as_2d = lambda n, a: a.reshape(natural_2d(n, grad_kl[n]).shape)
    gs = [natural_2d(n, grad_kl[n]) for n in _ALL_WEIGHTS]
    deltas, new_m, new_v = adamw_all(gs, [as_2d(n, local_w[n]) for n in _ALL_WEIGHTS], [as_2d(n, given["m_" + n]) for n in _ALL_WEIGHTS],
                                     [as_2d(n, given["v_" + n]) for n in _ALL_WEIGHTS])
    results = [a.reshape(local_w[n].shape) for group in (gs, deltas, new_m, new_v) for n, a in zip(_ALL_WEIGHTS, group, strict=True)]
    return (loss, grad_x, *results)
```

```python
import dataclasses
import functools
import math

import jax
import jax.numpy as jnp
from jax import lax
from jax.experimental import pallas as pl
from jax.experimental.pallas import tpu as pltpu

F32 = jnp.float32
BF16 = jnp.bfloat16
_MXU_DTYPE = jnp.bfloat16

N_DEV = 8
D_MODEL = 1024
N_HEADS = 8
HEAD = 128
CHUNK = 64
N_META = 16
PAD_ROWS = 2 * CHUNK - N_META
LEAD = PAD_ROWS + N_META
ROPE = 64
QK_DIM = HEAD + ROPE
QK_PAD = 2 * HEAD
KV_RANK = 256
Q_RANK = 384
CONV_K = 4
EPS = 1e-6
NEG = -1e30
ROPE_THETA = 10000.0
ADAM_LR, ADAM_B1, ADAM_B2, ADAM_EPS, ADAM_WD, ADAM_STEP = 0.001, 0.9, 0.999, 1e-08, 0.01, 10
PACK_COLS = 512
VMEM_LIMIT = 56 * 1024 * 1024


def _pick(n, options):
    for o in options:
        if n % o == 0:
            return o
    raise ValueError(f"no tile for {n} among {options}")


def _cparams(sem):
    return pltpu.CompilerParams(dimension_semantics=sem, vmem_limit_bytes=VMEM_LIMIT)


def _dims(a, dims):
    if a.ndim == 2:
        return (dims, ((), ()))
    (ca,), (cb,) = dims
    return (((ca + 1,), (cb + 1,)), ((0,), (0,)))


def _dot(a, b, dims):
    return lax.dot_general(a.astype(_MXU_DTYPE), b.astype(_MXU_DTYPE), _dims(a, dims), preferred_element_type=F32)


@jax.custom_vjp
def mm_nn(a, b):
    return _dot(a, b, ((1,), (0,)))


@jax.custom_vjp
def mm_nt(a, b):
    return _dot(a, b, ((1,), (1,)))


@jax.custom_vjp
def mm_tn(a, b):
    return _dot(a, b, ((0,), (0,)))


mm_nn.defvjp(lambda a, b: (mm_nn(a, b), (a, b)), lambda r, g: (mm_nt(g, r[1]), mm_tn(r[0], g)))
mm_nt.defvjp(lambda a, b: (mm_nt(a, b), (a, b)), lambda r, g: (mm_nn(g, r[1]), mm_tn(g, r[0])))
mm_tn.defvjp(lambda a, b: (mm_tn(a, b), (a, b)), lambda r, g: (mm_nt(r[1], g), mm_nn(r[0], g)))


def _split_terms(x, n):
    terms, rest = [], x
    for _ in range(n):
        t = rest.astype(_MXU_DTYPE)
        terms.append(t)
        rest = rest - t.astype(F32)
    return terms


def _dot_01_raw(m, x, dims):
    m = m.astype(_MXU_DTYPE)
    return sum(lax.dot_general(m, t, _dims(m, dims), preferred_element_type=F32) for t in _split_terms(x, 3))


@jax.custom_vjp
def _dot_01(m, x):
    return _dot_01_raw(m, x, ((1,), (0,)))


_dot_01.defvjp(lambda m, x: (_dot_01(m, x), m), lambda m, g: (jnp.zeros_like(m), _dot_01_raw(m, g, ((0,), (0,)))))


def _inv_unit_lower(a):
    n = a.shape[-1]
    eye = (lax.broadcasted_iota(jnp.int32, (n, n), 0) == lax.broadcasted_iota(jnp.int32, (n, n), 1)).astype(F32)
    d = lambda u, w: lax.dot_general(u, w, _dims(u, ((1,), (0,))), preferred_element_type=F32)
    t = eye - a
    p = a.astype(_MXU_DTYPE)
    p = d(p, p)
    squarings = int(math.log2(n)) - 1
    for s in range(squarings):
        ph = p.astype(_MXU_DTYPE)
        t_hi, t_lo = _split_terms(t, 2)
        t = t + (d(t_hi, ph) + d(t_lo, ph))
        if s + 1 < squarings:
            p = d(ph, ph)
    return t


@jax.custom_vjp
def _inv_lookup(a, t):
    return t


def _inv_lookup_bwd(t, g):
    return -mm_tn(t, mm_nt(g, t)), jnp.zeros_like(t)


_inv_lookup.defvjp(lambda a, t: (t, t), _inv_lookup_bwd)


def _sigmoid(x):
    return 1.0 / (1.0 + jnp.exp(-x))


@jax.custom_vjp
def _silu(x):
    return x * _sigmoid(x)


def _silu_fwd(x):
    s = _sigmoid(x)
    return x * s, (x, s)


_silu.defvjp(_silu_fwd, lambda r, g: (g * (r[1] * (1.0 + r[0] * (1.0 - r[1]))),))


def _softplus(x):
    return jnp.where(x > 20.0, x, jnp.log(1.0 + jnp.exp(jnp.minimum(x, 20.0))))


def _rms(x, g, width=None):
    ms = jnp.sum(x * x, -1, keepdims=True) / (x.shape[-1] if width is None else width)
    return x * lax.rsqrt(ms + EPS) * g


MM_VMEM_BUDGET = 40 * 1024 * 1024


def _matmul_rows(name, a, b, mode, out_dtype, res, scatter):
    m, k = a.shape
    n = b.shape[1] if mode == "nn" else b.shape[0]
    dims = {"nn": ((1,), (0,)), "nt": ((1,), (1,))}[mode]
    out_bytes = jnp.dtype(out_dtype).itemsize
    n_in, nx = 2 + (res is not None), len(scatter)

    def vmem(tm):
        blocks = 2 * tm * k * a.dtype.itemsize + 2 * k * n * b.dtype.itemsize + 2 * tm * n * out_bytes + tm * n * 4
        return blocks + (2 * tm * n * res.dtype.itemsize if res is not None else 0)

    tm = next(c for c in (2176, 1088, 512, 256, 128, 64) if m % c == 0 and vmem(c) <= MM_VMEM_BUDGET)
    steps = m // tm

    def body(*refs):
        a_ref, b_ref, o_ref = refs[0], refs[1], refs[n_in + nx]
        i = pl.program_id(0)
        finish = _ride(scatter, True, refs[n_in:n_in + nx], refs[n_in + nx + 1:n_in + 2 * nx + 1], refs[n_in + 2 * nx + 1:], i == 0, i == steps - 1)
        out = _dot(a_ref[...], b_ref[...], dims)
        if res is not None:
            out = out + refs[2][...].astype(F32)
        o_ref[...] = out.astype(o_ref.dtype)
        finish()

    o_spec = pl.BlockSpec((tm, n), lambda i: (i, 0))
    in_specs = [pl.BlockSpec((tm, k), lambda i: (i, 0)), pl.BlockSpec(b.shape, lambda i: (0, 0))] + ([o_spec] if res is not None else [])
    args = (a, b) + ((res,) if res is not None else ())
    out = pl.pallas_call(
        body, grid=(steps,), in_specs=in_specs + [_HBM] * nx, out_specs=[o_spec] + [_HBM] * nx,
        out_shape=[jax.ShapeDtypeStruct((m, n), out_dtype)] + Exchange.out_shape(scatter, True), scratch_shapes=Exchange.scratch(nx) if nx else [],
        compiler_params=_cparams(("arbitrary",) if nx else ("parallel",)), name=name)(*args, *scatter)
    return out if nx else out[0]


def matmul(name, a, b, mode, out_dtype=None, res=None, scatter=()):
    if mode != "tn":
        return _matmul_rows(name, a, b, mode, out_dtype or F32, res, scatter)
    out_dtype = out_dtype or _MXU_DTYPE
    (k, m), (k2, n) = a.shape, b.shape
    assert k == k2 and res is None, (name, a.shape, b.shape, mode)
    tm = _pick(m, (m if m <= 1536 else 1024, 1024, 512, 384, 256, 128))
    tn = _pick(n, (1024, 512, 384, 256, 128))
    tk = _pick(k, (512, 256, 128))
    nk = k // tk
    dims = ((0,), (0,))

    def body(*refs):
        if res is None:
            a_ref, b_ref, o_ref, acc_ref = refs
        else:
            a_ref, b_ref, r_ref, o_ref, acc_ref = refs
        kk = pl.program_id(2)

        @pl.when(kk == 0)
        def _():
            acc_ref[...] = jnp.zeros_like(acc_ref)

        acc_ref[...] += _dot(a_ref[...], b_ref[...], dims)

        @pl.when(kk == nk - 1)
        def _():
            out = acc_ref[...]
            if res is not None:
                out = out + r_ref[...].astype(F32)
            o_ref[...] = out.astype(o_ref.dtype)

    a_spec = pl.BlockSpec((tk, tm), lambda i, j, kk: (kk, i)) if mode == "tn" else pl.BlockSpec((tm, tk), lambda i, j, kk: (i, kk))
    b_spec = pl.BlockSpec((tn, tk), lambda i, j, kk: (j, kk)) if mode == "nt" else pl.BlockSpec((tk, tn), lambda i, j, kk: (kk, j))
    o_spec = pl.BlockSpec((tm, tn), lambda i, j, kk: (i, j))
    in_specs = [a_spec, b_spec] + ([o_spec] if res is not None else [])
    args = (a, b) + ((res,) if res is not None else ())
    return pl.pallas_call(
        body, grid=(m // tm, n // tn, nk), in_specs=in_specs, out_specs=o_spec,
        out_shape=jax.ShapeDtypeStruct((m, n), out_dtype), scratch_shapes=[pltpu.VMEM((tm, tn), F32)],
        compiler_params=_cparams(("parallel", "parallel", "arbitrary")), name=name)(*args)


@dataclasses.dataclass
class Arg:
    arr: jax.Array
    kind: str = "row"
    bc: int = 0
    base: int = 0
    ph: bool = False
    diff: bool = False
    gdt: object = F32


def _arg_spec(a, tr, nh, ntab, base=None):
    bc = a.bc or a.arr.shape[1]
    base = a.base if base is None else base
    width = bc * nh if a.ph else bc
    col = base // nh if a.ph else base
    assert not a.ph or base % nh == 0
    if a.kind == "row":
        return pl.BlockSpec((tr, width), lambda i: (i, col))
    if a.kind == "tab":
        return pl.BlockSpec((tr, width), lambda i: (i % ntab, col))
    return pl.BlockSpec((a.arr.shape[0], width), lambda i: (0, col))


def _head_view(ref, a, h, rs):
    bc = a.bc or a.arr.shape[1]
    rows = slice(None) if a.kind == "par" else rs
    v = ref[rows, h * bc:(h + 1) * bc] if a.ph else ref[rows, :]
    return v.astype(F32) if jnp.issubdtype(v.dtype, jnp.floating) else v


def row_call(name, fn, args, outs, tr, nh=1, ntab=1):
    t = args[0].arr.shape[0]
    n_in = len(args)
    out_args = [Arg(None, "row", bc, 0, ph) for (_, _, bc, ph) in outs]
    assert all(a.ph or nh == 1 for a in out_args)
    rs = slice(None)

    def body(*refs):
        for h in range(nh):
            res = fn(*[_head_view(r, a, h, rs) for r, a in zip(refs[:n_in], args, strict=True)])
            for r, a, v in zip(refs[n_in:], out_args, res, strict=True):
                r[rs, h * a.bc:(h + 1) * a.bc] = v.astype(r.dtype)

    return pl.pallas_call(
        body, grid=(t // tr,), in_specs=[_arg_spec(a, tr, nh, ntab) for a in args], out_specs=[_arg_spec(a, tr, nh, ntab) for a in out_args],
        out_shape=[jax.ShapeDtypeStruct((t, cols), dt) for (cols, dt, _, _) in outs],
        compiler_params=_cparams(("arbitrary",)), name=name)(*[a.arr for a in args])


def row_vjp_call(name, fn, args, cts, tr, nh=1, ntab=1):
    t = args[0].arr.shape[0]
    n_in, n_ct = len(args), len(cts)
    diff_idx = [k for k, a in enumerate(args) if a.diff]
    def body(*refs):
        out_refs = refs[n_in + n_ct:]
        par_sum = {}
        for k, r in zip(diff_idx, out_refs, strict=True):
            if args[k].kind == "par":
                @pl.when(pl.program_id(0) == 0)
                def _(r=r):
                    r[...] = jnp.zeros_like(r)

        for rs in (slice(None),):
            row_sum = {}
            for h in range(nh):
                vals = [_head_view(r, a, h, rs) for r, a in zip(refs[:n_in], args, strict=True)]
                ct_vals = tuple(_head_view(r, a, h, rs) for r, a in zip(refs[n_in:n_in + n_ct], cts, strict=True))

                def f(*dv, vals=vals):
                    full = list(vals)
                    for k, v in zip(diff_idx, dv, strict=True):
                        full[k] = v
                    return tuple(fn(*full))

                _, vjp = jax.vjp(f, *[vals[k] for k in diff_idx])
                for j, (k, r, g) in enumerate(zip(diff_idx, out_refs, vjp(ct_vals), strict=True)):
                    a = args[k]
                    bc = a.bc or a.arr.shape[1]
                    if a.kind == "row" and a.ph:
                        r[rs, h * bc:(h + 1) * bc] = g.astype(r.dtype)
                    elif a.kind == "row":
                        row_sum[j] = g if j not in row_sum else row_sum[j] + g
                    else:
                        key = (j, h if a.ph else 0)
                        par_sum[key] = g if key not in par_sum else par_sum[key] + g
            for j, g in row_sum.items():
                out_refs[j][rs, :] = g.astype(out_refs[j].dtype)
        for (j, h), g in par_sum.items():
            bc = g.shape[1]
            out_refs[j][:, h * bc:(h + 1) * bc] += g

    out_specs, out_shape = [], []
    for k in diff_idx:
        a = args[k]
        bc = a.bc or a.arr.shape[1]
        out_specs.append(_arg_spec(a, tr, nh, ntab, base=0))
        out_shape.append(jax.ShapeDtypeStruct((t if a.kind == "row" else a.arr.shape[0], bc * (nh if a.ph else 1)), a.gdt if a.kind == "row" else F32))
    in_specs = [_arg_spec(a, tr, nh, ntab) for a in list(args) + list(cts)]
    return pl.pallas_call(
        body, grid=(t // tr,), in_specs=in_specs, out_specs=out_specs, out_shape=out_shape,
        compiler_params=_cparams(("arbitrary",)), name=name)(*[a.arr for a in list(args) + list(cts)])


CONV_FIX_ROWS = 16


def _conv_taps(x, w):
    y = x * w[CONV_K - 1:CONV_K, :]
    for s in range(1, CONV_K):
        y = y + pltpu.roll(x, s, 0) * w[CONV_K - 1 - s:CONV_K - s, :]
    return y


CONV_HEADS = 4
CONV_BLOCKS_PER_THIRD = N_HEADS // CONV_HEADS


def _conv_post(y, block):
    a = _silu(y)
    normed = block < 2 * CONV_BLOCKS_PER_THIRD
    scale = jnp.where(block < CONV_BLOCKS_PER_THIRD, HEAD ** -0.5, 1.0)
    return a * jnp.where(normed, lax.rsqrt(jnp.sum(a * a, -1, keepdims=True) + EPS) * scale, 1.0)


def conv_fwd(z, w, lp):
    t, width = z.shape
    cols = CONV_HEADS * HEAD

    def body(z_ref, w_ref, o_ref, y_ref):
        block = pl.program_id(1)
        for h in range(CONV_HEADS):
            cs = slice(h * HEAD, (h + 1) * HEAD)
            y = _conv_taps(z_ref[:, cs].astype(F32), w_ref[:, cs])
            y_ref[:, cs] = y.astype(y_ref.dtype)
            o_ref[:, cs] = _conv_post(y, block)
            y_ref[:CONV_FIX_ROWS, cs] = jnp.zeros((CONV_FIX_ROWS, HEAD), y_ref.dtype)
            o_ref[:CONV_FIX_ROWS, cs] = jnp.zeros((CONV_FIX_ROWS, HEAD), F32)

    blk = pl.BlockSpec((lp, cols), lambda b, j: (b, j))
    out = jax.ShapeDtypeStruct((t, width), F32)
    return pl.pallas_call(
        body, grid=(t // lp, width // cols), in_specs=[blk, pl.BlockSpec((CONV_K, cols), lambda b, j: (0, j))],
        out_specs=[blk, blk], out_shape=[out, jax.ShapeDtypeStruct((t, width), _MXU_DTYPE)],
        compiler_params=_cparams(("arbitrary", "arbitrary")), name="a_conv_fwd")(z, w)


def conv_bwd(z, y, w, dout, lp):
    t, width = z.shape
    cols = CONV_HEADS * HEAD

    def body(z_ref, y_ref, w_ref, g_ref, dz_ref, dw_ref):
        block = pl.program_id(0)

        @pl.when(pl.program_id(1) == 0)
        def _():
            dw_ref[...] = jnp.zeros_like(dw_ref)

        for h in range(CONV_HEADS):
            cs = slice(h * HEAD, (h + 1) * HEAD)
            x, wv = z_ref[:, cs].astype(F32), w_ref[:, cs]
            _, vjp = jax.vjp(lambda y_: _conv_post(y_, block), y_ref[:, cs].astype(F32))
            (dy,) = vjp(g_ref[:, cs])
            dy = jnp.where(lax.broadcasted_iota(jnp.int32, x.shape, 0) >= CONV_FIX_ROWS, dy, 0.0)
            dx = dy * wv[CONV_K - 1:CONV_K, :]
            dw_ref[CONV_K - 1:CONV_K, cs] += jnp.sum(dy * x, axis=0, keepdims=True)
            for s in range(1, CONV_K):
                dy_up = pltpu.roll(dy, lp - s, 0)
                dx = dx + dy_up * wv[CONV_K - 1 - s:CONV_K - s, :]
                dw_ref[CONV_K - 1 - s:CONV_K - s, cs] += jnp.sum(dy_up * x, axis=0, keepdims=True)
            dz_ref[:, cs] = dx.astype(dz_ref.dtype)

    blk = pl.BlockSpec((lp, cols), lambda j, b: (b, j))
    w_blk = pl.BlockSpec((CONV_K, cols), lambda j, b: (0, j))
    return pl.pallas_call(
        body, grid=(width // cols, t // lp), in_specs=[blk, blk, w_blk, blk], out_specs=[blk, w_blk],
        out_shape=[jax.ShapeDtypeStruct((t, width), _MXU_DTYPE), jax.ShapeDtypeStruct((CONV_K, width), F32)],
        compiler_params=_cparams(("arbitrary", "arbitrary")), name="a_conv_bwd")(z, y, w, dout)


def _delta_chunk(q, k, v, ba, alog, dtb, state, t_stored):
    n_g, c = q.shape[0], q.shape[1]
    lane = lax.broadcasted_iota(jnp.int32, (1, HEAD), 1)

    def pick(xs, offset):
        cols = [jnp.sum(xs[i // N_HEADS if len(xs) > 1 else 0] * (lane == offset + i % N_HEADS).astype(F32), axis=1, keepdims=True)[None]
                for i in range(n_g)]
        return jnp.concatenate(cols, 0)

    b_raw, a_raw = pick(ba, 0), pick(ba, N_HEADS)
    a_log, dt_bias = pick((alog,), 0), pick((dtb,), 0)
    beta = _sigmoid(b_raw)
    g = -jnp.exp(a_log) * _softplus(a_raw + dt_bias)
    ri = lax.broadcasted_iota(jnp.int32, (c, c), 0)
    ci = lax.broadcasted_iota(jnp.int32, (c, c), 1)
    tril = ci <= ri
    lower = jnp.broadcast_to(tril.astype(F32), (n_g, c, c))
    gc_col = _dot_01(lower, g * jnp.ones((1, 1, HEAD), F32))[:, :, :1]
    gc_row = _dot_01(jnp.ones((n_g, 8, c), F32), g * (ri <= ci).astype(F32)[None])[:, 0:1, :]
    gc_last = jnp.sum(g, axis=1, keepdims=True)
    decay = jnp.exp(jnp.where(tril, gc_col - gc_row, NEG))
    e_gc = jnp.exp(gc_col)
    kb = k * beta
    a_mat = jnp.where(ci < ri, mm_nt(kb, k) * decay, 0.0)
    t_inv = _inv_unit_lower(a_mat) if t_stored is None else _inv_lookup(a_mat, t_stored)
    u_base = mm_nn(t_inv, v * beta)
    w_dec = mm_nn(t_inv, kb * e_gc)
    attn = jnp.where(tril, mm_nt(q, k) * decay, 0.0)
    u = u_base - mm_nn(w_dec, state)
    o = mm_nn(q * e_gc, state) + mm_nn(attn, u)
    new_state = state * jnp.exp(gc_last) + mm_tn(k * jnp.exp(gc_last - gc_col), u)
    return o, new_state, t_inv


DELTA_STEP_FWD = (4, 2)
DELTA_STEP_BWD = (2, 2)


def _heads_of(ref, rs, first_col):
    return jnp.stack([ref[i // N_HEADS, rs, first_col + (i % N_HEADS) * HEAD:first_col + (i % N_HEADS + 1) * HEAD]
                      for i in range(ref.shape[0] * N_HEADS)])


def _qkv_heads(ref, rs, part):
    return _heads_of(ref, rs, part * N_HEADS * HEAD)


def _by_sequence(a, lp):
    return a.reshape(a.shape[0] // lp, lp, a.shape[1])


def _ride(bufs, scatter, refs_in, refs_out, sems, first, last, two_level=False):
    if not bufs:
        return lambda: None
    make = lambda: (TwoLevelGather if two_level else Exchange)(refs_in, refs_out, *sems, scatter)

    @pl.when(first)
    def _():
        make().start()

    def finish():
        @pl.when(last)
        def _():
            make().wait()

    return finish


def delta_fwd(qkv, ba, ba_block, alog, dtb, lp, gather=()):
    t = qkv.shape[0]
    nb, nc = t // lp, lp // CHUNK
    seqs, cps = DELTA_STEP_FWD
    ng, rows = nc // cps, cps * CHUNK
    nx = len(gather)
    nbg = nb // seqs
    assert nc % cps == 0 and nb % seqs == 0

    def body(*refs):
        qkv_ref, ba_ref, al_ref, dt_ref = refs[:4]
        o_ref, s_ref, t_ref = refs[4 + nx:7 + nx]
        state_ref = refs[7 + 2 * nx]
        b, n = pl.program_id(0), pl.program_id(1)
        finish = _ride(gather, False, refs[4:4 + nx], refs[7 + nx:7 + 2 * nx], refs[8 + 2 * nx:], (b == 0) & (n == 0), (b == nbg - 1) & (n == ng - 1))

        @pl.when(n == 0)
        def _():
            state_ref[...] = jnp.zeros_like(state_ref)

        al, dtv = al_ref[...], dt_ref[...]
        for c in range(cps):
            rs = slice(c * CHUNK, (c + 1) * CHUNK)
            state = state_ref[...]
            o, new_state, t_inv = _delta_chunk(_qkv_heads(qkv_ref, rs, 0), _qkv_heads(qkv_ref, rs, 1), _qkv_heads(qkv_ref, rs, 2),
                                               tuple(ba_ref[i, rs, :] for i in range(seqs)), al, dtv, state, None)
            for i in range((seqs * N_HEADS)):
                seq, g = divmod(i, N_HEADS)
                o_ref[seq, rs, g * HEAD:(g + 1) * HEAD] = o[i]
                s_ref[seq, g, c] = state[i]
                t_ref[seq, g, c] = t_inv[i]
            state_ref[...] = new_state
        finish()

    rows_of = lambda width: pl.BlockSpec((seqs, rows, width), lambda b, n: (b, n, 0))
    par_spec = pl.BlockSpec((1, HEAD), lambda b, n: (0, 0))
    out = pl.pallas_call(
        body, grid=(nbg, ng),
        in_specs=[rows_of(3 * N_HEADS * HEAD), pl.BlockSpec((seqs, rows, HEAD), lambda b, n: (b, n, ba_block)), par_spec, par_spec] + [_HBM] * nx,
        out_specs=[rows_of(N_HEADS * HEAD), pl.BlockSpec((seqs, N_HEADS, cps, HEAD, HEAD), lambda b, n: (b, 0, n, 0, 0)),
                   pl.BlockSpec((seqs, N_HEADS, cps, CHUNK, CHUNK), lambda b, n: (b, 0, n, 0, 0))] + [_HBM] * nx,
        out_shape=[jax.ShapeDtypeStruct((nb, lp, N_HEADS * HEAD), F32), jax.ShapeDtypeStruct((nb, N_HEADS, nc, HEAD, HEAD), F32),
                   jax.ShapeDtypeStruct((nb, N_HEADS, nc, CHUNK, CHUNK), F32)] + Exchange.out_shape(gather, False),
        scratch_shapes=[pltpu.VMEM(((seqs * N_HEADS), HEAD, HEAD), F32)] + (Exchange.scratch(nx) if nx else []),
        compiler_params=_cparams(("arbitrary", "arbitrary")), name="delta_fwd")(_by_sequence(qkv, lp), _by_sequence(ba, lp), alog, dtb, *gather)
    return [out[0].reshape(t, N_HEADS * HEAD)] + list(out[1:])


def delta_bwd(qkv, ba, ba_block, alog, dtb, states, t_invs, do, lp, scatter=()):
    t = qkv.shape[0]
    nb, nc = t // lp, lp // CHUNK
    seqs, cps = DELTA_STEP_BWD
    ng, rows = nc // cps, cps * CHUNK
    nx = len(scatter)
    nbg = nb // seqs

    def body(*refs):
        qkv_ref, ba_ref, al_ref, dt_ref, s_ref, t_ref, do_ref = refs[:7]
        dqkv_ref, dba_ref, dal_ref, ddt_ref = refs[7 + nx:11 + nx]
        dstate_ref = refs[11 + 2 * nx]
        b, step = pl.program_id(0), pl.program_id(1)
        finish = _ride(scatter, True, refs[7:7 + nx], refs[11 + nx:11 + 2 * nx], refs[12 + 2 * nx:], (b == 0) & (step == 0),
                       (b == nbg - 1) & (step == ng - 1))

        @pl.when(step == 0)
        def _():
            dstate_ref[...] = jnp.zeros_like(dstate_ref)

        @pl.when((b == 0) & (step == 0))
        def _():
            dal_ref[...] = jnp.zeros_like(dal_ref)
            ddt_ref[...] = jnp.zeros_like(ddt_ref)

        al, dtv = al_ref[...], dt_ref[...]
        d_al = jnp.zeros((1, HEAD), F32)
        d_dt = jnp.zeros((1, HEAD), F32)
        for c in reversed(range(cps)):
            rs = slice(c * CHUNK, (c + 1) * CHUNK)
            t_n = jnp.stack([t_ref[i // N_HEADS, i % N_HEADS, c] for i in range((seqs * N_HEADS))])
            s_n = jnp.stack([s_ref[i // N_HEADS, i % N_HEADS, c] for i in range((seqs * N_HEADS))])

            def f(q_, k_, v_, ba_, al_, dt_, s_, t_n=t_n):
                return _delta_chunk(q_, k_, v_, ba_, al_, dt_, s_, t_n)[:2]

            _, vjp = jax.vjp(f, _qkv_heads(qkv_ref, rs, 0), _qkv_heads(qkv_ref, rs, 1), _qkv_heads(qkv_ref, rs, 2), tuple(ba_ref[i, rs, :] for i in range(seqs)), al, dtv, s_n)
            grads = vjp((_heads_of(do_ref, rs, 0), dstate_ref[...]))
            for part in range(3):
                for i in range((seqs * N_HEADS)):
                    col = (part * N_HEADS + i % N_HEADS) * HEAD
                    dqkv_ref[i // N_HEADS, rs, col:col + HEAD] = grads[part][i]
            for i in range(seqs):
                dba_ref[i, rs, :] = grads[3][i]
            d_al, d_dt = d_al + grads[4], d_dt + grads[5]
            dstate_ref[...] = grads[6]
        dal_ref[...] += d_al
        ddt_ref[...] += d_dt
        finish()

    rows_of = lambda width: pl.BlockSpec((seqs, rows, width), lambda b, n: (b, ng - 1 - n, 0))
    par_spec = pl.BlockSpec((1, HEAD), lambda b, n: (0, 0))
    out = pl.pallas_call(
        body, grid=(nbg, ng),
        in_specs=[rows_of(3 * N_HEADS * HEAD), pl.BlockSpec((seqs, rows, HEAD), lambda b, n: (b, ng - 1 - n, ba_block)), par_spec, par_spec,
                  pl.BlockSpec((seqs, N_HEADS, cps, HEAD, HEAD), lambda b, n: (b, 0, ng - 1 - n, 0, 0)),
                  pl.BlockSpec((seqs, N_HEADS, cps, CHUNK, CHUNK), lambda b, n: (b, 0, ng - 1 - n, 0, 0)), rows_of(N_HEADS * HEAD)] + [_HBM] * nx,
        out_specs=[rows_of(3 * N_HEADS * HEAD), rows_of(HEAD), par_spec, par_spec] + [_HBM] * nx,
        out_shape=[jax.ShapeDtypeStruct((nb, lp, 3 * N_HEADS * HEAD), F32), jax.ShapeDtypeStruct((nb, lp, HEAD), F32),
                   jax.ShapeDtypeStruct((1, HEAD), F32), jax.ShapeDtypeStruct((1, HEAD), F32)] + Exchange.out_shape(scatter, True),
        scratch_shapes=[pltpu.VMEM(((seqs * N_HEADS), HEAD, HEAD), F32)] + (Exchange.scratch(nx) if nx else []),
        compiler_params=_cparams(("arbitrary", "arbitrary")), name="delta_bwd")(
            _by_sequence(qkv, lp), _by_sequence(ba, lp), alog, dtb, states, t_invs, _by_sequence(do, lp), *scatter)
    return [out[0].reshape(t, 3 * N_HEADS * HEAD), out[1].reshape(t, HEAD)] + list(out[2:])


ATT_Q_TILE = 256
ATT_K_TILE = 512
ATT_K_TILE_BWD = 1024
ATT_SCALE = QK_DIM ** -0.5


def _tiles(end, size):
    return [(s, min(s + size, end)) for s in range(0, end, size)]


def _att_visible(q0, q1, k0, k1, keys_first):
    if k1 <= q0 + CHUNK and k0 >= PAD_ROWS:
        return None
    shape = (k1 - k0, q1 - q0) if keys_first else (q1 - q0, k1 - k0)
    qpos = q0 + lax.broadcasted_iota(jnp.int32, shape, 1 if keys_first else 0)
    kpos = k0 + lax.broadcasted_iota(jnp.int32, shape, 0 if keys_first else 1)
    shift = CHUNK.bit_length() - 1
    return (jnp.right_shift(kpos, shift) <= jnp.right_shift(qpos, shift)) & (kpos >= PAD_ROWS)


def _att_seq_specs(lp):
    return pl.BlockSpec((lp, QK_PAD), lambda b, h: (b, h)), pl.BlockSpec((lp, HEAD), lambda b, h: (b, h))


def flash_fwd(q, k, v, lp):
    t = q.shape[0]
    qk_seq, o_seq = _att_seq_specs(lp)

    def body(q_ref, k_ref, v_ref, o_ref, lse_ref):
        q_tiles = _tiles(lp, ATT_Q_TILE)

        def score_steps(q0, q1, out):
            def step(k0, k1):
                s = mm_nt(q_ref[q0:q1, :], k_ref[k0:k1, :])
                vis = _att_visible(q0, q1, k0, k1, False)
                s = s if vis is None else jnp.where(vis, s, NEG)
                out["scores"].append(s)
                row_max = jnp.max(s, -1, keepdims=True)
                out["m"] = row_max if out["m"] is None else jnp.maximum(out["m"], row_max)
            return [functools.partial(step, k0, k1) for k0, k1 in _tiles(q1, ATT_K_TILE)]

        cur = {"scores": [], "m": None}
        for step in score_steps(*q_tiles[0], cur):
            step()
        for i, (q0, q1) in enumerate(q_tiles):
            nxt = {"scores": [], "m": None}
            ahead = score_steps(*q_tiles[i + 1], nxt) if i + 1 < len(q_tiles) else []
            l = jnp.zeros((q1 - q0, 1), F32)
            acc = jnp.zeros((q1 - q0, HEAD), F32)
            for s, (k0, k1) in zip(cur["scores"], _tiles(q1, ATT_K_TILE), strict=True):
                if ahead:
                    ahead.pop(0)()
                p = jnp.exp2(s - cur["m"])
                l = l + jnp.sum(p, -1, keepdims=True)
                acc = acc + mm_nn(p, v_ref[k0:k1, :])
            for step in ahead:
                step()
            o_ref[q0:q1, :] = acc / l
            lse_ref[q0:q1, :] = jnp.broadcast_to(cur["m"] + jnp.log2(l), (q1 - q0, HEAD))
            cur = nxt

    big = jax.ShapeDtypeStruct((t, N_HEADS * HEAD), F32)
    return pl.pallas_call(
        body, grid=(t // lp, N_HEADS), in_specs=[qk_seq, qk_seq, o_seq], out_specs=[o_seq, o_seq], out_shape=[big, big],
        compiler_params=_cparams(("arbitrary", "arbitrary")), name="flash_fwd")(q, k, v)


def flash_bwd(q, k, v, o, lse, do, lp):
    t = q.shape[0]
    qk_seq, o_seq = _att_seq_specs(lp)

    def body(q_ref, k_ref, v_ref, o_ref, lse_ref, do_ref, dq_ref, dk_out_ref, dv_out_ref, dk_ref, dv_ref):
        dk_ref[...] = jnp.zeros_like(dk_ref)
        dv_ref[...] = jnp.zeros_like(dv_ref)
        for q0, q1 in _tiles(lp, ATT_Q_TILE):
            qb, dob = q_ref[q0:q1, :], do_ref[q0:q1, :]
            lse_row = jnp.transpose(lse_ref[q0:q1, :])[0:1, :]
            dob_ln2 = dob * math.log(2.0)
            dsum_row = jnp.sum(jnp.transpose(dob_ln2 * o_ref[q0:q1, :]), axis=0, keepdims=True)
            dq = jnp.zeros((q1 - q0, QK_PAD), F32)
            for k0, k1 in _tiles(q1, ATT_K_TILE_BWD):
                kb, vb = k_ref[k0:k1, :], v_ref[k0:k1, :]
                s = mm_nt(kb, qb)
                vis = _att_visible(q0, q1, k0, k1, True)
                s = s if vis is None else jnp.where(vis, s, NEG)
                p = jnp.exp2(s - lse_row)
                ds = p * (mm_nt(vb, dob_ln2) - dsum_row)
                dv_ref[k0:k1, :] += mm_nn(p, dob)
                dk_ref[k0:k1, :] += mm_nn(ds, qb)
                dq = dq + mm_tn(ds, kb)
            dq_ref[q0:q1, :] = dq.astype(dq_ref.dtype)
        dk_out_ref[...] = dk_ref[...].astype(dk_out_ref.dtype)
        dv_out_ref[...] = dv_ref[...].astype(dv_out_ref.dtype)

    narrow = _MXU_DTYPE
    return pl.pallas_call(
        body, grid=(t // lp, N_HEADS), in_specs=[qk_seq, qk_seq, o_seq, o_seq, o_seq, o_seq], out_specs=[qk_seq, qk_seq, o_seq],
        out_shape=[jax.ShapeDtypeStruct((t, N_HEADS * QK_PAD), narrow), jax.ShapeDtypeStruct((t, N_HEADS * QK_PAD), narrow),
                   jax.ShapeDtypeStruct((t, N_HEADS * HEAD), narrow)],
        scratch_shapes=[pltpu.VMEM((lp, QK_PAD), F32), pltpu.VMEM((lp, HEAD), F32)],
        compiler_params=_cparams(("arbitrary", "arbitrary")), name="flash_bwd")(q, k, v, o, lse, do)


def loss_head(h2, target, lp):
    nb, seq, d = target.shape
    cols = _pick(d, (512, 128))
    ncol = d // cols

    def body(h_ref, t_ref, loss_ref, dh_ref, acc_ref):
        b, j = pl.program_id(0), pl.program_id(1)

        @pl.when((b == 0) & (j == 0))
        def _():
            acc_ref[...] = jnp.zeros_like(acc_ref)

        err = h_ref[LEAD:, :] - t_ref[...]
        dh_ref[:LEAD, :] = jnp.zeros((LEAD, cols), F32)
        dh_ref[LEAD:, :] = err * (1.0 / d)
        acc_ref[...] += jnp.sum(err * err, axis=0, keepdims=True)

        @pl.when((b == nb - 1) & (j == ncol - 1))
        def _():
            loss_ref[...] = jnp.sum(acc_ref[...], axis=1, keepdims=True) * (0.5 / d)

    return pl.pallas_call(
        body, grid=(nb, ncol),
        in_specs=[pl.BlockSpec((None, lp, cols), lambda b, j: (b, 0, j)), pl.BlockSpec((None, seq, cols), lambda b, j: (b, 0, j))],
        out_specs=[pl.BlockSpec((1, 1), lambda b, j: (0, 0)), pl.BlockSpec((None, lp, cols), lambda b, j: (b, 0, j))],
        out_shape=[jax.ShapeDtypeStruct((1, 1), F32), jax.ShapeDtypeStruct((nb, lp, d), F32)],
        scratch_shapes=[pltpu.VMEM((1, cols), F32)], compiler_params=_cparams(("arbitrary", "arbitrary")), name="loss_head")(h2, target)


def gated_out(name, o, gate, gain, w, res):
    t, kw = o.shape
    d = w.shape[1]
    tm = _pick(t, (512, 256, 128))

    def body(*refs):
        o_ref, gate_ref = refs[:2]
        w_ref, r_ref, h_ref, g_ref = refs[-4:]
        if gain is None:
            g_ref[...] = _f_gate(o_ref[...], gate_ref[...])[0].astype(g_ref.dtype)
        else:
            for h in range(N_HEADS):
                cs = slice(h * HEAD, (h + 1) * HEAD)
                g_ref[:, cs] = _f_out_gate(o_ref[:, cs], gate_ref[:, cs], refs[2][...])[0].astype(g_ref.dtype)
        h_ref[...] = r_ref[...] + _dot(g_ref[...], w_ref[...], ((1,), (0,)))

    rows = lambda width: pl.BlockSpec((tm, width), lambda i: (i, 0))
    whole = lambda a: pl.BlockSpec(a.shape, lambda i: (0, 0))
    params = [] if gain is None else [gain]
    return pl.pallas_call(
        body, grid=(t // tm,), in_specs=[rows(kw), rows(kw)] + [whole(p) for p in params] + [whole(w), rows(d)], out_specs=[rows(d), rows(kw)],
        out_shape=[jax.ShapeDtypeStruct((t, d), F32), jax.ShapeDtypeStruct((t, kw), _MXU_DTYPE)],
        compiler_params=_cparams(("parallel",)), name=name)(o, gate, *params, w, res)


def embed_norm(x, meta, gain, lp, gather=()):
    nb, seq, d = x.shape
    nblk, nx = lp // LEAD, len(gather)

    def body(*refs):
        x_ref, meta_ref, g_ref = refs[:3]
        h_ref, hn_ref = refs[3 + nx:5 + nx]
        b, i = pl.program_id(0), pl.program_id(1)
        finish = _ride(gather, False, refs[3:3 + nx], refs[5 + nx:5 + 2 * nx], refs[5 + 2 * nx:], (b == 0) & (i == 0), (b == nb - 1) & (i == nblk - 1),
                       two_level=True)

        @pl.when(i == 0)
        def _():
            h_ref[:PAD_ROWS, :] = jnp.zeros((PAD_ROWS, d), F32)
            h_ref[PAD_ROWS:, :] = meta_ref[...]

        @pl.when(i > 0)
        def _():
            h_ref[...] = x_ref[...]

        hn_ref[...] = _rms(h_ref[...], g_ref[...]).astype(hn_ref.dtype)
        finish()

    rows = pl.BlockSpec((LEAD, d), lambda b, i: (b * nblk + i, 0))
    out = pl.pallas_call(
        body, grid=(nb, nblk),
        in_specs=[pl.BlockSpec((None, LEAD, d), lambda b, i: (b, jnp.maximum(i - 1, 0), 0)), pl.BlockSpec((N_META, d), lambda b, i: (0, 0)),
                  pl.BlockSpec((1, d), lambda b, i: (0, 0))] + [_HBM] * nx,
        out_specs=[rows, rows] + [_HBM] * nx,
        out_shape=[jax.ShapeDtypeStruct((nb * lp, d), F32), jax.ShapeDtypeStruct((nb * lp, d), _MXU_DTYPE)] + Exchange.out_shape(gather, False),
        scratch_shapes=Exchange.scratch(nx) if nx else [],
        compiler_params=_cparams(("arbitrary", "arbitrary")), name="embed_norm")(x, meta, gain, *gather)
    return list(out)


def meta_grad(dh0):
    nb, _, d = dh0.shape

    def body(g_ref, o_ref):
        @pl.when(pl.program_id(0) == 0)
        def _():
            o_ref[...] = jnp.zeros_like(o_ref)

        o_ref[...] += g_ref[PAD_ROWS:LEAD, :]

    return pl.pallas_call(
        body, grid=(nb,), in_specs=[pl.BlockSpec((None, LEAD, d), lambda b: (b, 0, 0))],
        out_specs=pl.BlockSpec((N_META, d), lambda b: (0, 0)), out_shape=jax.ShapeDtypeStruct((N_META, d), F32),
        compiler_params=_cparams(("arbitrary",)), name="meta_grad")(dh0)


_HBM = pl.BlockSpec(memory_space=pltpu.HBM)


def _mesh_pos():
    x, y, c = lax.axis_index("x"), lax.axis_index("y"), lax.axis_index("c")
    return x, y, c


def _peer(x, y, c, k):
    px = 1 - x if k & 4 else x
    py = 1 - y if k & 2 else y
    pc = 1 - c if k & 1 else c
    return (px, py, pc), 4 * px + 2 * py + pc


class Exchange:
    def __init__(self, x_refs, out_refs, send_sems, recv_sems, local_sems, scatter):
        self.x_refs, self.out_refs, self.scatter = x_refs, out_refs, scatter
        self.send_sems, self.recv_sems, self.local_sems = send_sems, recv_sems, local_sems
        self.pos = _mesh_pos()
        x, y, c = self.pos
        self.me = 4 * x + 2 * y + c

    @staticmethod
    def scratch(n):
        return [pltpu.SemaphoreType.DMA((n, N_DEV - 1)), pltpu.SemaphoreType.DMA((n, N_DEV - 1)), pltpu.SemaphoreType.DMA((n,))]

    @staticmethod
    def out_shape(bufs, scatter):
        return [jax.ShapeDtypeStruct(b.shape if scatter else (N_DEV,) + b.shape, b.dtype) for b in bufs]

    def _local(self, i):
        return pltpu.make_async_copy(self.x_refs[i].at[self.me] if self.scatter else self.x_refs[i], self.out_refs[i].at[self.me], self.local_sems.at[i])

    def _copy(self, i, k, landing):
        peer, peer_id = _peer(*self.pos, k)
        src = self.x_refs[i].at[peer_id] if self.scatter else self.x_refs[i]
        return pltpu.make_async_remote_copy(src_ref=src, dst_ref=self.out_refs[i].at[peer_id if landing else self.me],
                                            send_sem=self.send_sems.at[i, k - 1], recv_sem=self.recv_sems.at[i, k - 1],
                                            device_id=peer, device_id_type=pl.DeviceIdType.MESH)

    def start(self):
        for i in range(len(self.x_refs)):
            self._local(i).start()
        for k in range(1, N_DEV):
            for i in range(len(self.x_refs)):
                self._copy(i, k, False).start()

    def wait(self):
        for k in range(1, N_DEV):
            for i in range(len(self.x_refs)):
                self._copy(i, k, True).wait_recv()
        for k in range(1, N_DEV):
            for i in range(len(self.x_refs)):
                self._copy(i, k, False).wait_send()
        for i in range(len(self.x_refs)):
            self._local(i).wait()


class TwoLevelGather(Exchange):
    DIRECT = (1, 4, 2, 6)
    FROM_CHIPS = (4, 2, 6)

    def _forward(self, i, k):
        _, origin = _peer(*self.pos, k)
        sibling, _ = _peer(*self.pos, 1)
        block = self.out_refs[i].at[origin]
        return pltpu.make_async_remote_copy(src_ref=block, dst_ref=block, send_sem=self.send_sems.at[i, (k ^ 1) - 1],
                                            recv_sem=self.recv_sems.at[i, (k ^ 1) - 1], device_id=sibling, device_id_type=pl.DeviceIdType.MESH)

    def start(self):
        assert not self.scatter
        for i in range(len(self.x_refs)):
            self._local(i).start()
        for k in self.DIRECT:
            for i in range(len(self.x_refs)):
                self._copy(i, k, False).start()

    def wait(self):
        n = range(len(self.x_refs))
        for k in self.FROM_CHIPS:
            for i in n:
                self._copy(i, k, True).wait_recv()
                self._forward(i, k).start()
        for k in (1, 5, 3, 7):
            for i in n:
                self._copy(i, k, True).wait_recv()
        for k in self.DIRECT:
            for i in n:
                self._copy(i, k, False).wait_send()
        for k in self.FROM_CHIPS:
            for i in n:
                self._forward(i, k).wait_send()
        for i in n:
            self._local(i).wait()


def _exchange(name, bufs, scatter):
    n = len(bufs)

    def body(*refs):
        ex = Exchange(refs[:n], refs[n:2 * n], *refs[2 * n:], scatter)
        ex.start()
        ex.wait()

    return pl.pallas_call(body, in_specs=[_HBM] * n, out_specs=[_HBM] * n, out_shape=Exchange.out_shape(bufs, scatter),
                          scratch_shapes=Exchange.scratch(n), name=name)(*bufs)


def _f_rms(x, g):
    return (_rms(x, g),)


def _f_rms2(x, g1, g2):
    r = x * lax.rsqrt(jnp.sum(x * x, -1, keepdims=True) / x.shape[-1] + EPS)
    return r * g1, r * g2


@jax.custom_vjp
def _out_gate(o, gate, gain):
    return _rms(o, gain) * _silu(gate)


def _out_gate_bwd(res, g):
    o, gate, gain = res
    r = lax.rsqrt(jnp.sum(o * o, -1, keepdims=True) / o.shape[-1] + EPS)
    n = o * r
    s = _sigmoid(gate)
    g_norm = g * (gate * s)
    d_gate = g * (n * gain) * (s * (1.0 + gate * (1.0 - s)))
    gn = g_norm * gain
    d_o = r * (gn - n * (jnp.sum(gn * n, -1, keepdims=True) / o.shape[-1]))
    return d_o, d_gate, jnp.sum(g_norm * n, 0, keepdims=True)


_out_gate.defvjp(lambda o, gate, gain: (_out_gate(o, gate, gain), (o, gate, gain)), _out_gate_bwd)


def _f_out_gate(o, gate, gain):
    return (_out_gate(o, gate, gain),)


def _f_gate(o, gate):
    return (o * _silu(gate),)


def _swap_rope_halves(x):
    return pltpu.roll(x, ROPE // 2, 1) + pltpu.roll(x, HEAD - ROPE // 2, 1)


def _qk_final_inv_rms(nope, rope_in):
    ms = (jnp.sum(nope * nope, -1, keepdims=True) + jnp.sum(rope_in * rope_in, -1, keepdims=True)) / QK_DIM
    return lax.rsqrt(ms + EPS)


@functools.partial(jax.custom_vjp, nondiff_argnums=(0,))
def _qk_final(scale, nope, rope_in, g_nope, g_rope, cos, sin):
    r = _qk_final_inv_rms(nope, rope_in)
    b = rope_in * (r * g_rope)
    out = jnp.concatenate([nope * (r * g_nope), b * cos + _swap_rope_halves(b) * sin], axis=1)
    return out if scale == 1.0 else out * scale


def _qk_final_fwd(scale, nope, rope_in, g_nope, g_rope, cos, sin):
    return _qk_final(scale, nope, rope_in, g_nope, g_rope, cos, sin), (nope, rope_in, g_nope, g_rope, cos, sin)


def _qk_final_bwd(scale, res, g):
    nope, rope_in, g_nope, g_rope, cos, sin = res
    r = _qk_final_inv_rms(nope, rope_in)
    ga, gb = g[:, :HEAD], g[:, HEAD:]
    if scale != 1.0:
        ga, gb = ga * scale, gb * scale
    db = gb * cos + _swap_rope_halves(gb * sin)
    t_a, t_b = ga * nope, db * rope_in
    d_r = jnp.sum(t_a * g_nope + t_b * g_rope, -1, keepdims=True)
    c = d_r * (r * r * r) * (-1.0 / QK_DIM)
    d_nope = ga * (r * g_nope) + nope * c
    d_rope = db * (r * g_rope) + rope_in * c
    d_g_nope = jnp.sum(t_a * r, 0, keepdims=True)
    d_g_rope = jnp.sum(t_b * r, 0, keepdims=True)
    return d_nope, d_rope, d_g_nope, d_g_rope, jnp.zeros_like(cos), jnp.zeros_like(sin)


_qk_final.defvjp(_qk_final_fwd, _qk_final_bwd)


def _f_qk_final(scale, nope, rope_in, g_nope, g_rope, cos, sin):
    return (_qk_final(scale, nope, rope_in, g_nope, g_rope, cos, sin),)


def _rope_tables(lp):
    half = ROPE // 2
    pos = jnp.maximum(jnp.arange(lp) - PAD_ROWS, 0)
    inv = ROPE_THETA ** (-jnp.arange(half, dtype=F32) / half)
    ang = pos.astype(F32)[:, None] * inv[None, :]
    zeros = jnp.zeros((lp, HEAD - ROPE), F32)
    cos = jnp.concatenate([jnp.cos(ang), jnp.cos(ang), zeros], 1)
    sin = jnp.concatenate([-jnp.sin(ang), jnp.sin(ang), zeros], 1)
    return cos, sin


def _pad_lanes(w, width=HEAD):
    return jnp.pad(w, ((0, 0), (0, width - w.shape[1])))


def _pad_rows(w, rows=HEAD):
    return jnp.pad(w, ((0, rows - w.shape[0]), (0, 0)))


def _split_heads_qk_t(w_t):
    k = w_t.shape[1]
    return jnp.pad(w_t.reshape(N_HEADS, QK_DIM, k), ((0, 0), (0, QK_PAD - QK_DIM), (0, 0))).reshape(N_HEADS * QK_PAD, k)


def _merge_heads_qk_t(g_t):
    k = g_t.shape[1]
    return g_t.reshape(N_HEADS, QK_PAD, k)[:, :QK_DIM].reshape(N_HEADS * QK_DIM, k)


@functools.partial(jax.custom_vjp, nondiff_argnums=(0,))
def _q_final(scale, qh, g_nope, g_rope, cos, sin):
    return _qk_final(scale, qh[:, :HEAD], qh[:, HEAD:], g_nope, g_rope, cos, sin)


def _q_final_bwd(scale, res, g):
    qh, g_nope, g_rope, cos, sin = res
    grads = _qk_final_bwd(scale, (qh[:, :HEAD], qh[:, HEAD:], g_nope, g_rope, cos, sin), g)
    return (jnp.concatenate(grads[:2], axis=1),) + tuple(grads[2:])


_q_final.defvjp(lambda scale, qh, *rest: (_q_final(scale, qh, *rest), (qh,) + rest), _q_final_bwd)


def _f_q_final(scale, qh, g_nope, g_rope, cos, sin):
    return (_q_final(scale, qh, g_nope, g_rope, cos, sin),)


def local_step(x, target, w, deferred=None):
    nb, seq, d = x.shape
    lp = seq + LEAD
    t = nb * lp
    tr = _pick(lp, (544, 128))
    ntab = lp // tr
    mxu = _MXU_DTYPE
    kw = N_HEADS * HEAD

    a_conv = w["a_conv"].T
    alog, dtb, o_gain = _pad_lanes(w["a_log"]), _pad_lanes(w["a_dt_bias"]), w["a_o_gain"]
    a_norm, kv_norm, b_norm = w["a_norm"], w["kv_norm"][None, :], w["b_norm"]
    lat_norm, qlat_norm = w["kv_latent_norm"][None, :], w["b_q_latent_norm"]
    kg_nope, kg_rope = w["k_gain"][None, :HEAD], _pad_lanes(w["k_gain"][None, HEAD:])
    qg_nope, qg_rope = w["b_q_gain"][:, :HEAD], _pad_lanes(w["b_q_gain"][:, HEAD:])
    cos, sin = _rope_tables(lp)

    h0, hn, *gathered = embed_norm(x, w["meta_tokens"].T, a_norm, lp, gather=deferred.first_gather_bufs if deferred else ())
    if deferred:
        w = {**w, **deferred.finish_first(gathered)}
    a_w_in_t = w["a_w_in"].astype(mxu)
    w_qkv_t, w_gba_t = a_w_in_t[:3 * kw], _pad_rows(a_w_in_t[3 * kw:], kw + HEAD)
    z_qkv = matmul("a_in_qkv", hn, w_qkv_t, "nt", out_dtype=mxu)
    z_gba = matmul("a_in_gate_ba", hn, w_gba_t, "nt")
    ba_block = kw // HEAD
    qkv_a, y_conv = conv_fwd(z_qkv, a_conv, lp)
    o_a, states, t_invs, *gathered = delta_fwd(qkv_a, z_gba, ba_block, alog, dtb, lp, gather=deferred.gather_bufs if deferred else ())
    if deferred:
        w = {**w, **deferred.finish(gathered)}
    a_w_out = w["a_w_out"].astype(mxu)
    w_down = _pad_lanes(w["kv_w_down"], KV_RANK + HEAD).astype(mxu)
    w_ukv_t = jnp.concatenate([w["kv_w_uk"], w["kv_w_uv"]], 0).astype(mxu)
    b_w_in_t = w["b_w_in"].astype(mxu)
    w_cq_t, w_gb_t = b_w_in_t[:Q_RANK], b_w_in_t[Q_RANK:]
    w_q_t = _split_heads_qk_t(w["b_w_uq"]).astype(mxu)
    b_w_out = w["b_w_out"].astype(mxu)
    og_args = [Arg(o_a, bc=HEAD, ph=True, diff=True), Arg(z_gba, bc=HEAD, ph=True, diff=True, gdt=mxu), Arg(o_gain, "par", diff=True)]
    h1, og_a = gated_out("a_out", o_a, z_gba, o_gain, a_w_out, h0)

    hk, hb = row_call("b_norms_fwd", _f_rms2, [Arg(h1), Arg(kv_norm, "par"), Arg(b_norm, "par")], [(d, mxu, d, False), (d, mxu, d, False)], tr)
    c_down = matmul("kv_down", hk, w_down, "nn")
    c_kv_arg = Arg(c_down, bc=KV_RANK, diff=True, gdt=mxu)
    k_pe_arg = Arg(c_down, bc=HEAD, base=KV_RANK // HEAD, diff=True)
    c_q_raw = matmul("b_in_q", hb, w_cq_t, "nt")
    gate_b = matmul("b_in_gate", hb, w_gb_t, "nt")
    (c_kv,) = row_call("kv_latent_fwd", _f_rms, [c_kv_arg, Arg(lat_norm, "par")], [(KV_RANK, mxu, KV_RANK, False)], tr)
    (c_q,) = row_call("q_latent_fwd", _f_rms, [Arg(c_q_raw), Arg(qlat_norm, "par")], [(Q_RANK, mxu, Q_RANK, False)], tr)
    k_nope = matmul("k_up", c_kv, w_ukv_t[:kw], "nt")
    v_b = matmul("v_up", c_kv, w_ukv_t[kw:], "nt", out_dtype=mxu)
    q_up = matmul("q_up", c_q, w_q_t, "nt")
    tabs = [Arg(cos, "tab"), Arg(sin, "tab")]
    k_args = [Arg(k_nope, bc=HEAD, ph=True, diff=True, gdt=mxu), k_pe_arg, Arg(kg_nope, "par", diff=True), Arg(kg_rope, "par", diff=True)] + tabs
    q_args = [Arg(q_up, bc=QK_PAD, ph=True, diff=True, gdt=mxu), Arg(qg_nope, "par", diff=True), Arg(qg_rope, "par", diff=True)] + tabs
    f_k_final, f_q_final = functools.partial(_f_qk_final, 1.0), functools.partial(_f_q_final, ATT_SCALE * math.log2(math.e))
    (k_fin,) = row_call("k_final_fwd", f_k_final, k_args, [(N_HEADS * QK_PAD, mxu, QK_PAD, True)], tr, nh=N_HEADS, ntab=ntab)
    (q_fin,) = row_call("q_final_fwd", f_q_final, q_args, [(N_HEADS * QK_PAD, mxu, QK_PAD, True)], tr, nh=N_HEADS, ntab=ntab)
    o_b, lse = flash_fwd(q_fin, k_fin, v_b, lp)
    gb_args = [Arg(o_b, diff=True), Arg(gate_b, diff=True, gdt=mxu)]
    h2, og_b = gated_out("b_out", o_b, gate_b, None, b_w_out, h1)

    loss, dh2 = loss_head(h2.reshape(nb, lp, d), target, lp)
    dh2 = dh2.reshape(t, d)
    grads = {}

    d_og_b = matmul("b_out_dx", dh2, b_w_out, "nt", out_dtype=mxu)
    grads["b_w_out"] = matmul("b_out_dw", og_b, dh2, "tn")
    d_o_b, d_gate_b = row_vjp_call("b_gate_bwd", _f_gate, gb_args, [Arg(d_og_b)], tr)
    dq_fin, dk_fin, dv_b = flash_bwd(q_fin, k_fin, v_b, o_b, lse, d_o_b, lp)
    dq_up, d_qg_nope, d_qg_rope = row_vjp_call(
        "q_final_bwd", f_q_final, q_args, [Arg(dq_fin, bc=QK_PAD, ph=True)], tr, nh=N_HEADS, ntab=ntab)
    dk_nope, dk_pe, d_kg_nope, d_kg_rope = row_vjp_call(
        "k_final_bwd", f_k_final, k_args, [Arg(dk_fin, bc=QK_PAD, ph=True)], tr, nh=N_HEADS, ntab=ntab)
    grads["b_q_gain"] = jnp.concatenate([d_qg_nope, d_qg_rope[:, :ROPE]], 1)
    grads["k_gain"] = jnp.concatenate([d_kg_nope, d_kg_rope[:, :ROPE]], 1)[0]
    d_c_q = matmul("q_up_dx", dq_up, w_q_t, "nn")
    grads["b_w_uq"] = _merge_heads_qk_t(matmul("q_up_dw", dq_up, c_q, "tn"))
    d_c_kv = matmul("k_up_dx", dk_nope, w_ukv_t[:kw], "nn")
    d_c_kv = matmul("v_up_dx", dv_b, w_ukv_t[kw:], "nn", res=d_c_kv)
    grads["kv_w_uk"], grads["kv_w_uv"] = matmul("k_up_dw", dk_nope, c_kv, "tn"), matmul("v_up_dw", dv_b, c_kv, "tn")
    d_c_q_raw, grads["b_q_latent_norm"] = row_vjp_call(
        "q_latent_bwd", _f_rms, [Arg(c_q_raw, diff=True, gdt=mxu), Arg(qlat_norm, "par", diff=True)], [Arg(d_c_q)], tr)
    d_c_kv_raw, d_lat = row_vjp_call(
        "kv_latent_bwd", _f_rms, [c_kv_arg, Arg(lat_norm, "par", diff=True)], [Arg(d_c_kv)], tr)
    grads["kv_latent_norm"] = d_lat[0]
    d_hb = matmul("b_in_q_dx", d_c_q_raw, w_cq_t, "nn")
    d_hb = matmul("b_in_gate_dx", d_gate_b, w_gb_t, "nn", res=d_hb, out_dtype=mxu)
    grads["b_w_in"] = jnp.concatenate([matmul("b_in_q_dw", d_c_q_raw, hb, "tn"), matmul("b_in_gate_dw", d_gate_b, hb, "tn")], 0)
    d_c_down = jnp.concatenate([d_c_kv_raw, dk_pe.astype(mxu)], 1)
    d_hk = matmul("kv_down_dx", d_c_down, w_down, "nt", out_dtype=mxu)
    grads["kv_w_down"] = matmul("kv_down_dw", hk, d_c_down, "tn")[:, :KV_RANK + ROPE]
    dh1, d_kv_norm, grads["b_norm"] = row_vjp_call(
        "b_norms_bwd", lambda x_, g1, g2: _f_rms2(x_, g1, g2) + (x_,),
        [Arg(h1, diff=True), Arg(kv_norm, "par", diff=True), Arg(b_norm, "par", diff=True)], [Arg(d_hk), Arg(d_hb), Arg(dh2)], tr)
    grads["kv_norm"] = d_kv_norm[0]

    d_og_a = matmul("a_out_dx", dh1, a_w_out, "nt", out_dtype=mxu)
    grads["a_w_out"] = matmul("a_out_dw", og_a, dh1, "tn")
    d_o_a, d_gate_a, grads["a_o_gain"] = row_vjp_call(
        "a_out_gate_bwd", _f_out_gate, og_args, [Arg(d_og_a, bc=HEAD, ph=True)], tr, nh=N_HEADS)
    dqkv_a, d_ba, d_alog, d_dtb, *received = delta_bwd(qkv_a, z_gba, ba_block, alog, dtb, states, t_invs, d_o_a, lp,
                                                        scatter=deferred.scatter_bufs(grads) if deferred else ())
    grads["a_log"], grads["a_dt_bias"] = d_alog[:, :N_HEADS], d_dtb[:, :N_HEADS]
    dz_qkv, d_conv = conv_bwd(z_qkv, y_conv, a_conv, dqkv_a, lp)
    grads["a_conv"] = d_conv.T
    dz_gba = jnp.concatenate([d_gate_a, d_ba.astype(mxu)], 1)
    grads["a_w_in"] = jnp.concatenate([matmul("a_in_qkv_dw", dz_qkv, hn, "tn"), matmul("a_in_gate_ba_dw", dz_gba, hn, "tn")[:kw + 2 * N_HEADS]], 0)
    ride = deferred.last_scatter_bufs(grads) if deferred else ((), ())
    d_hn = matmul("a_in_qkv_dx", dz_qkv, w_qkv_t, "nn", scatter=ride[0])
    if ride[0]:
        d_hn, *received_half = d_hn
        received = list(received) + received_half
    d_hn = matmul("a_in_gate_ba_dx", dz_gba, w_gba_t, "nn", res=d_hn, out_dtype=mxu, scatter=ride[1])
    if ride[1]:
        d_hn, *received_half = d_hn
        received = list(received) + received_half
    dh0, grads["a_norm"] = row_vjp_call("a_norm_bwd", lambda x_, g_: _f_rms(x_, g_) + (x_,),
                                        [Arg(h0, diff=True), Arg(a_norm, "par", diff=True)], [Arg(d_hn), Arg(dh1)], tr)
    dh0 = dh0.reshape(nb, lp, d)
    grads["meta_tokens"] = meta_grad(dh0).T
    return loss, dh0[:, LEAD:], grads, received


_SHARDED = (
    ("meta_tokens", True, False), ("a_norm", True, False), ("a_w_in", True, True), ("a_conv", True, False), ("a_w_out", False, True),
    ("kv_w_down", False, True), ("kv_w_uk", True, True), ("kv_w_uv", True, True), ("b_w_in", True, True), ("b_w_uq", True, True),
    ("b_w_out", False, True))
_REPLICATED = ("a_log", "a_dt_bias", "a_o_gain", "kv_norm", "kv_latent_norm", "k_gain", "b_norm", "b_q_latent_norm", "b_q_gain")
_ALL_WEIGHTS = ("meta_tokens", "a_norm", "a_w_in", "a_conv", "a_log", "a_dt_bias", "a_o_gain", "a_w_out", "kv_norm", "kv_w_down",
                "kv_latent_norm", "kv_w_uk", "kv_w_uv", "k_gain", "b_norm", "b_w_in", "b_q_latent_norm", "b_w_uq", "b_q_gain", "b_w_out")


def _round_up(n, m):
    return (n + m - 1) // m * m


def _pack_rows(pieces, row_multiple):
    padded = []
    for p in pieces:
        n = p.shape[-1]
        padded.append(jnp.pad(p, [(0, 0)] * (p.ndim - 1) + [(0, _round_up(n, PACK_COLS) - n)]))
    flat = jnp.concatenate(padded, -1)
    rows = _round_up(flat.shape[-1] // PACK_COLS, row_multiple)
    flat = jnp.pad(flat, [(0, 0)] * (flat.ndim - 1) + [(0, rows * PACK_COLS - flat.shape[-1])])
    return flat.reshape(flat.shape[:-1] + (rows, PACK_COLS))


def _unpack_rows(buf, sizes):
    flat = buf.reshape(buf.shape[:-2] + (-1,))
    out, off = [], 0
    for n in sizes:
        out.append(flat[..., off:off + n])
        off += _round_up(n, PACK_COLS)
    return out


def _shard_2d(a):
    return a.reshape(a.shape[-2:]) if a.ndim > 2 else a


def _kl_shard(a, by_cols):
    return _shard_2d(a).T if by_cols else _shard_2d(a)


_GROUPS_FIRST = (("a_w_in",),)
_GROUPS_LATER = (("a_w_out", "b_w_in", "b_w_out"), ("b_w_uq",), ("kv_w_down",), ("kv_w_uk", "kv_w_uv"))
_SMALL_SHARDED = ("meta_tokens", "a_norm", "a_conv")
_BY_COLS = {name: by_cols for name, by_cols, _ in _SHARDED}
ROW_ALIGN = 16


def _stack_rows(pieces):
    padded, starts, row = [], [], 0
    for p in pieces:
        r = p.shape[-2]
        padded.append(jnp.pad(p, [(0, 0)] * (p.ndim - 2) + [(0, _round_up(r, ROW_ALIGN) - r), (0, 0)]))
        starts.append(row)
        row += _round_up(r, ROW_ALIGN)
    return jnp.concatenate(padded, -2), starts


def _stack_group(arrays_by_name, names):
    arrays = [arrays_by_name[n].astype(BF16) for n in names]
    buf, starts = _stack_rows(arrays)
    return buf, [(n, s, a.shape[-2]) for n, s, a in zip(names, starts, arrays, strict=True)]


def _stack_groups(arrays_by_name, groups):
    stacked = [_stack_group(arrays_by_name, names) for names in groups]
    return [b for b, _ in stacked], [entries for _, entries in stacked]


def _full_from_gathered(gathered, layout):
    full = {}
    for got, entries in zip(gathered, layout, strict=True):
        for name, start, rows in entries:
            full[name] = got[:, start:start + rows].reshape(N_DEV * rows, got.shape[-1])
    return full


def gather_small_weights(local):
    small = [_kl_shard(local[n], _BY_COLS[n]) for n in _SMALL_SHARDED]
    (gathered,) = _exchange("all_gather", [_pack_rows([s.reshape(-1) for s in small], 8)], scatter=False)
    full = {}
    for name, part, sh in zip(_SMALL_SHARDED, _unpack_rows(gathered, [s.size for s in small]), small, strict=True):
        full[name] = part.reshape(N_DEV * sh.shape[0], sh.shape[1])
    full["a_norm"] = full["a_norm"].reshape(1, -1)
    return full


class LaterExchanges:
    def __init__(self, local):
        shards = {n: _kl_shard(local[n], _BY_COLS[n]) for names in _GROUPS_FIRST + _GROUPS_LATER for n in names}
        self.first_gather_bufs, self.first_layout = _stack_groups(shards, _GROUPS_FIRST)
        self.gather_bufs, self.layout = _stack_groups(shards, _GROUPS_LATER)

    def finish_first(self, gathered):
        return _full_from_gathered(gathered, self.first_layout)

    def finish(self, gathered):
        return _full_from_gathered(gathered, self.layout)

    def scatter_bufs(self, grads):
        return _stack_groups(_owner_slices(grads, _GROUPS_LATER), _GROUPS_LATER)[0]

    def last_scatter_bufs(self, grads):
        (buf,), self.last_layout = _stack_groups(_owner_slices(grads, _GROUPS_FIRST), _GROUPS_FIRST)
        first = buf.shape[-1] * 5 // 8 // HEAD * HEAD
        return [buf[..., :first]], [buf[..., first:]]


def _owner_slices(grads, groups):
    return {n: grads[n].reshape(N_DEV, -1, grads[n].shape[-1]) for names in groups for n in names}


def reduce_contributions(name, recv):
    _, r, c = recv.shape
    tr = max(d for d in range(8, 513, 8) if r % d == 0 and (d % ROW_ALIGN == 0 or recv.dtype == F32))

    def body(g_ref, o_ref):
        g = g_ref[0].astype(F32)
        for dev in range(1, N_DEV):
            g = g + g_ref[dev].astype(F32)
        o_ref[...] = g

    return pl.pallas_call(
        body, grid=(r // tr,), in_specs=[pl.BlockSpec((N_DEV, tr, c), lambda i: (0, i, 0))], out_specs=pl.BlockSpec((tr, c), lambda i: (i, 0)),
        out_shape=jax.ShapeDtypeStruct((r, c), F32), compiler_params=_cparams(("arbitrary",)), name=name)(recv)


def adamw_all(gs, ws, ms, vs):
    n = len(gs)

    def body(*refs):
        for i in range(n):
            g_ref, w_ref, m_ref, v_ref = (refs[j * n + i] for j in range(4))
            d_ref, mo_ref, vo_ref = (refs[(4 + j) * n + i] for j in range(3))
            g = g_ref[...]
            m_new = ADAM_B1 * m_ref[...] + (1.0 - ADAM_B1) * g
            v_new = ADAM_B2 * v_ref[...] + (1.0 - ADAM_B2) * (g * g)
            m_hat = m_new / (1.0 - ADAM_B1 ** ADAM_STEP)
            v_hat = v_new / (1.0 - ADAM_B2 ** ADAM_STEP)
            d_ref[...] = -ADAM_LR * (m_hat / (jnp.sqrt(v_hat) + ADAM_EPS) + ADAM_WD * w_ref[...])
            mo_ref[...] = m_new
            vo_ref[...] = v_new

    out = [jax.ShapeDtypeStruct(g.shape, F32) for g in gs] * 3
    res = pl.pallas_call(body, out_shape=out, compiler_params=pltpu.CompilerParams(vmem_limit_bytes=VMEM_LIMIT), name="adamw_all")(*gs, *ws, *ms, *vs)
    return res[:n], res[n:2 * n], res[2 * n:]


def kernel(x, meta_tokens, a_norm, a_w_in, a_conv, a_log, a_dt_bias, a_o_gain, a_w_out, kv_norm, kv_w_down, kv_latent_norm, kv_w_uk, kv_w_uv, k_gain, b_norm, b_w_in, b_q_latent_norm, b_w_uq, b_q_gain, b_w_out, loss_target, m_meta_tokens, m_a_norm, m_a_w_in, m_a_conv, m_a_log, m_a_dt_bias, m_a_o_gain, m_a_w_out, m_kv_norm, m_kv_w_down, m_kv_latent_norm, m_kv_w_uk, m_kv_w_uv, m_k_gain, m_b_norm, m_b_w_in, m_b_q_latent_norm, m_b_w_uq, m_b_q_gain, m_b_w_out, v_meta_tokens, v_a_norm, v_a_w_in, v_a_conv, v_a_log, v_a_dt_bias, v_a_o_gain, v_a_w_out, v_kv_norm, v_kv_w_down, v_kv_latent_norm, v_kv_w_uk, v_kv_w_uv, v_k_gain, v_b_norm, v_b_w_in, v_b_q_latent_norm, v_b_w_uq, v_b_q_gain, v_b_w_out):
    given = dict(locals())
    local_w = {n: given[n] for n in _ALL_WEIGHTS}
    full = gather_small_weights(local_w)
    for n in _REPLICATED:
        full[n] = local_w[n]
    later = LaterExchanges(local_w)

    loss_part, grad_x, grads, received_riding = local_step(x, loss_target, full, later)

    exact = [grads[n].reshape(N_DEV, -1) for n in _SMALL_SHARDED]
    exact += [jnp.broadcast_to(grads[n].reshape(1, -1), (N_DEV, grads[n].size)) for n in _REPLICATED]
    exact.append(jnp.broadcast_to(loss_part, (N_DEV, 1)))
    received = list(received_riding) + list(_exchange("all_to_all", [_pack_rows(exact, 8)], scatter=True))
    layout = later.layout + later.last_layout
    summed = [reduce_contributions(f"reduce_{i}", r) for i, r in enumerate(received)]
    n_later = len(later.layout)
    summed = summed[:n_later] + [jnp.concatenate(summed[n_later:n_later + 2], 1)] + summed[n_later + 2:]

    grad_kl = {}
    for got, entries in zip(summed, layout):
        for n, start, rows in entries:
            grad_kl[n] = got[start:start + rows]
    parts = _unpack_rows(summed[-1], [p.shape[1] for p in exact])
    for n, part in zip(_SMALL_SHARDED + _REPLICATED, parts, strict=False):
        grad_kl[n] = part
    loss = parts[-1][0]

    def natural_2d(n, a):
        shape = _shard_2d(local_w[n]).shape if local_w[n].ndim > 1 else (1, local_w[n].size)
        return a.reshape(shape[::-1]).T if _BY_COLS.get(n, False) else a.reshape(shape)

    as_2d = lambda n, a: a.reshape(natural_2d(n, grad_kl[n]).shape)
    gs = [natural_2d(n, grad_kl[n]) for n in _ALL_WEIGHTS]
    deltas, new_m, new_v = adamw_all(gs, [as_2d(n, local_w[n]) for n in _ALL_WEIGHTS], [as_2d(n, given["m_" + n]) for n in _ALL_WEIGHTS],
                                     [as_2d(n, given["v_" + n]) for n in _ALL_WEIGHTS])
    results = [a.reshape(local_w[n].shape) for group in (gs, deltas, new_m, new_v) for n, a in zip(_ALL_WEIGHTS, group, strict=True)]
    return (loss, grad_x, *results)
```

```python
import dataclasses
import functools
import math

import jax
import jax.numpy as jnp
from jax import lax
from jax.experimental import pallas as pl
from jax.experimental.pallas import tpu as pltpu

F32 = jnp.float32
BF16 = jnp.bfloat16
_MXU_DTYPE = jnp.bfloat16

N_DEV = 8
D_MODEL = 1024
N_HEADS = 8
HEAD = 128
CHUNK = 64
N_META = 16
PAD_ROWS = 2 * CHUNK - N_META
LEAD = PAD_ROWS + N_META
ROPE = 64
QK_DIM = HEAD + ROPE
QK_PAD = 2 * HEAD
KV_RANK = 256
Q_RANK = 384
CONV_K = 4
EPS = 1e-6
NEG = -1e30
ROPE_THETA = 10000.0
ADAM_LR, ADAM_B1, ADAM_B2, ADAM_EPS, ADAM_WD, ADAM_STEP = 0.001, 0.9, 0.999, 1e-08, 0.01, 10
PACK_COLS = 512
VMEM_LIMIT = 56 * 1024 * 1024


def _pick(n, options):
    for o in options:
        if n % o == 0:
            return o
    raise ValueError(f"no tile for {n} among {options}")


def _cparams(sem):
    return pltpu.CompilerParams(dimension_semantics=sem, vmem_limit_bytes=VMEM_LIMIT)


def _dims(a, dims):
    if a.ndim == 2:
        return (dims, ((), ()))
    (ca,), (cb,) = dims
    return (((ca + 1,), (cb + 1,)), ((0,), (0,)))


def _dot(a, b, dims):
    return lax.dot_general(a.astype(_MXU_DTYPE), b.astype(_MXU_DTYPE), _dims(a, dims), preferred_element_type=F32)


@jax.custom_vjp
def mm_nn(a, b):
    return _dot(a, b, ((1,), (0,)))


@jax.custom_vjp
def mm_nt(a, b):
    return _dot(a, b, ((1,), (1,)))


@jax.custom_vjp
def mm_tn(a, b):
    return _dot(a, b, ((0,), (0,)))


mm_nn.defvjp(lambda a, b: (mm_nn(a, b), (a, b)), lambda r, g: (mm_nt(g, r[1]), mm_tn(r[0], g)))
mm_nt.defvjp(lambda a, b: (mm_nt(a, b), (a, b)), lambda r, g: (mm_nn(g, r[1]), mm_tn(g, r[0])))
mm_tn.defvjp(lambda a, b: (mm_tn(a, b), (a, b)), lambda r, g: (mm_nt(r[1], g), mm_nn(r[0], g)))


def _split_terms(x, n):
    terms, rest = [], x
    for _ in range(n):
        t = rest.astype(_MXU_DTYPE)
        terms.append(t)
        rest = rest - t.astype(F32)
    return terms


def _dot_01_raw(m, x, dims):
    m = m.astype(_MXU_DTYPE)
    return sum(lax.dot_general(m, t, _dims(m, dims), preferred_element_type=F32) for t in _split_terms(x, 3))


@jax.custom_vjp
def _dot_01(m, x):
    return _dot_01_raw(m, x, ((1,), (0,)))


_dot_01.defvjp(lambda m, x: (_dot_01(m, x), m), lambda m, g: (jnp.zeros_like(m), _dot_01_raw(m, g, ((0,), (0,)))))


def _inv_unit_lower(a):
    n = a.shape[-1]
    eye = (lax.broadcasted_iota(jnp.int32, (n, n), 0) == lax.broadcasted_iota(jnp.int32, (n, n), 1)).astype(F32)
    d = lambda u, w: lax.dot_general(u, w, _dims(u, ((1,), (0,))), preferred_element_type=F32)
    t = eye - a
    p = a.astype(_MXU_DTYPE)
    p = d(p, p)
    squarings = int(math.log2(n)) - 1
    for s in range(squarings):
        ph = p.astype(_MXU_DTYPE)
        t_hi, t_lo = _split_terms(t, 2)
        t = t + (d(t_hi, ph) + d(t_lo, ph))
        if s + 1 < squarings:
            p = d(ph, ph)
    return t


@jax.custom_vjp
def _inv_lookup(a, t):
    return t


def _inv_lookup_bwd(t, g):
    return -mm_tn(t, mm_nt(g, t)), jnp.zeros_like(t)


_inv_lookup.defvjp(lambda a, t: (t, t), _inv_lookup_bwd)


def _sigmoid(x):
    return 1.0 / (1.0 + jnp.exp(-x))


@jax.custom_vjp
def _silu(x):
    return x * _sigmoid(x)


def _silu_fwd(x):
    s = _sigmoid(x)
    return x * s, (x, s)


_silu.defvjp(_silu_fwd, lambda r, g: (g * (r[1] * (1.0 + r[0] * (1.0 - r[1]))),))


def _softplus(x):
    return jnp.where(x > 20.0, x, jnp.log(1.0 + jnp.exp(jnp.minimum(x, 20.0))))


def _rms(x, g, width=None):
    ms = jnp.sum(x * x, -1, keepdims=True) / (x.shape[-1] if width is None else width)
    return x * lax.rsqrt(ms + EPS) * g


MM_VMEM_BUDGET = 40 * 1024 * 1024


def _matmul_rows(name, a, b, mode, out_dtype, res, scatter):
    m, k = a.shape
    n = b.shape[1] if mode == "nn" else b.shape[0]
    dims = {"nn": ((1,), (0,)), "nt": ((1,), (1,))}[mode]
    out_bytes = jnp.dtype(out_dtype).itemsize
    n_in, nx = 2 + (res is not None), len(scatter)

    def vmem(tm):
        blocks = 2 * tm * k * a.dtype.itemsize + k * n * b.dtype.itemsize + 2 * tm * n * out_bytes + tm * n * 4
        return blocks + (2 * tm * n * res.dtype.itemsize if res is not None else 0)

    tm = next(c for c in (2176, 1088, 512, 256, 128, 64) if m % c == 0 and vmem(c) <= MM_VMEM_BUDGET)
    steps = m // tm

    def body(*refs):
        a_ref, b_ref, o_ref = refs[0], refs[1], refs[n_in + nx]
        i = pl.program_id(0)
        finish = _ride(scatter, True, refs[n_in:n_in + nx], refs[n_in + nx + 1:n_in + 2 * nx + 1], refs[n_in + 2 * nx + 1:], i == 0, i == steps - 1)
        out = _dot(a_ref[...], b_ref[...], dims)
        if res is not None:
            out = out + refs[2][...].astype(F32)
        o_ref[...] = out.astype(o_ref.dtype)
        finish()

    o_spec = pl.BlockSpec((tm, n), lambda i: (i, 0))
    weight = pl.BlockSpec(b.shape, lambda i: (0, 0), pipeline_mode=pl.Buffered(1))
    in_specs = [pl.BlockSpec((tm, k), lambda i: (i, 0)), weight] + ([o_spec] if res is not None else [])
    args = (a, b) + ((res,) if res is not None else ())
    out = pl.pallas_call(
        body, grid=(steps,), in_specs=in_specs + [_HBM] * nx, out_specs=[o_spec] + [_HBM] * nx,
        out_shape=[jax.ShapeDtypeStruct((m, n), out_dtype)] + Exchange.out_shape(scatter, True), scratch_shapes=Exchange.scratch(nx) if nx else [],
        compiler_params=_cparams(("arbitrary",) if nx else ("parallel",)), name=name)(*args, *scatter)
    return out if nx else out[0]


def matmul(name, a, b, mode, out_dtype=None, res=None, scatter=()):
    if mode != "tn":
        return _matmul_rows(name, a, b, mode, out_dtype or F32, res, scatter)
    out_dtype = out_dtype or _MXU_DTYPE
    (k, m), (k2, n) = a.shape, b.shape
    assert k == k2 and res is None, (name, a.shape, b.shape, mode)
    tm = _pick(m, (m if m <= 1536 else 1024, 1024, 512, 384, 256, 128))
    tn = _pick(n, (1024, 512, 384, 256, 128))
    tk = _pick(k, (512, 256, 128))
    nk = k // tk
    dims = ((0,), (0,))

    def body(*refs):
        if res is None:
            a_ref, b_ref, o_ref, acc_ref = refs
        else:
            a_ref, b_ref, r_ref, o_ref, acc_ref = refs
        kk = pl.program_id(2)

        @pl.when(kk == 0)
        def _():
            acc_ref[...] = jnp.zeros_like(acc_ref)

        acc_ref[...] += _dot(a_ref[...], b_ref[...], dims)

        @pl.when(kk == nk - 1)
        def _():
            out = acc_ref[...]
            if res is not None:
                out = out + r_ref[...].astype(F32)
            o_ref[...] = out.astype(o_ref.dtype)

    a_spec = pl.BlockSpec((tk, tm), lambda i, j, kk: (kk, i)) if mode == "tn" else pl.BlockSpec((tm, tk), lambda i, j, kk: (i, kk))
    b_spec = pl.BlockSpec((tn, tk), lambda i, j, kk: (j, kk)) if mode == "nt" else pl.BlockSpec((tk, tn), lambda i, j, kk: (kk, j))
    o_spec = pl.BlockSpec((tm, tn), lambda i, j, kk: (i, j))
    in_specs = [a_spec, b_spec] + ([o_spec] if res is not None else [])
    args = (a, b) + ((res,) if res is not None else ())
    return pl.pallas_call(
        body, grid=(m // tm, n // tn, nk), in_specs=in_specs, out_specs=o_spec,
        out_shape=jax.ShapeDtypeStruct((m, n), out_dtype), scratch_shapes=[pltpu.VMEM((tm, tn), F32)],
        compiler_params=_cparams(("parallel", "parallel", "arbitrary")), name=name)(*args)


@dataclasses.dataclass
class Arg:
    arr: jax.Array
    kind: str = "row"
    bc: int = 0
    base: int = 0
    ph: bool = False
    diff: bool = False
    gdt: object = F32


def _arg_spec(a, tr, nh, ntab, base=None):
    bc = a.bc or a.arr.shape[1]
    base = a.base if base is None else base
    width = bc * nh if a.ph else bc
    col = base // nh if a.ph else base
    assert not a.ph or base % nh == 0
    if a.kind == "row":
        return pl.BlockSpec((tr, width), lambda i: (i, col))
    if a.kind == "tab":
        return pl.BlockSpec((tr, width), lambda i: (i % ntab, col))
    return pl.BlockSpec((a.arr.shape[0], width), lambda i: (0, col))


def _head_view(ref, a, h, rs):
    bc = a.bc or a.arr.shape[1]
    rows = slice(None) if a.kind == "par" else rs
    v = ref[rows, h * bc:(h + 1) * bc] if a.ph else ref[rows, :]
    return v.astype(F32) if jnp.issubdtype(v.dtype, jnp.floating) else v


def row_call(name, fn, args, outs, tr, nh=1, ntab=1):
    t = args[0].arr.shape[0]
    n_in = len(args)
    out_args = [Arg(None, "row", bc, 0, ph) for (_, _, bc, ph) in outs]
    assert all(a.ph or nh == 1 for a in out_args)
    rs = slice(None)

    def body(*refs):
        for h in range(nh):
            res = fn(*[_head_view(r, a, h, rs) for r, a in zip(refs[:n_in], args, strict=True)])
            for r, a, v in zip(refs[n_in:], out_args, res, strict=True):
                r[rs, h * a.bc:(h + 1) * a.bc] = v.astype(r.dtype)

    return pl.pallas_call(
        body, grid=(t // tr,), in_specs=[_arg_spec(a, tr, nh, ntab) for a in args], out_specs=[_arg_spec(a, tr, nh, ntab) for a in out_args],
        out_shape=[jax.ShapeDtypeStruct((t, cols), dt) for (cols, dt, _, _) in outs],
        compiler_params=_cparams(("arbitrary",)), name=name)(*[a.arr for a in args])


def row_vjp_call(name, fn, args, cts, tr, nh=1, ntab=1):
    t = args[0].arr.shape[0]
    n_in, n_ct = len(args), len(cts)
    diff_idx = [k for k, a in enumerate(args) if a.diff]
    def body(*refs):
        out_refs = refs[n_in + n_ct:]
        par_sum = {}
        for k, r in zip(diff_idx, out_refs, strict=True):
            if args[k].kind == "par":
                @pl.when(pl.program_id(0) == 0)
                def _(r=r):
                    r[...] = jnp.zeros_like(r)

        for rs in (slice(None),):
            row_sum = {}
            for h in range(nh):
                vals = [_head_view(r, a, h, rs) for r, a in zip(refs[:n_in], args, strict=True)]
                ct_vals = tuple(_head_view(r, a, h, rs) for r, a in zip(refs[n_in:n_in + n_ct], cts, strict=True))

                def f(*dv, vals=vals):
                    full = list(vals)
                    for k, v in zip(diff_idx, dv, strict=True):
                        full[k] = v
                    return tuple(fn(*full))

                _, vjp = jax.vjp(f, *[vals[k] for k in diff_idx])
                for j, (k, r, g) in enumerate(zip(diff_idx, out_refs, vjp(ct_vals), strict=True)):
                    a = args[k]
                    bc = a.bc or a.arr.shape[1]
                    if a.kind == "row" and a.ph:
                        r[rs, h * bc:(h + 1) * bc] = g.astype(r.dtype)
                    elif a.kind == "row":
                        row_sum[j] = g if j not in row_sum else row_sum[j] + g
                    else:
                        key = (j, h if a.ph else 0)
                        par_sum[key] = g if key not in par_sum else par_sum[key] + g
            for j, g in row_sum.items():
                out_refs[j][rs, :] = g.astype(out_refs[j].dtype)
        for (j, h), g in par_sum.items():
            bc = g.shape[1]
            out_refs[j][:, h * bc:(h + 1) * bc] += g

    out_specs, out_shape = [], []
    for k in diff_idx:
        a = args[k]
        bc = a.bc or a.arr.shape[1]
        out_specs.append(_arg_spec(a, tr, nh, ntab, base=0))
        out_shape.append(jax.ShapeDtypeStruct((t if a.kind == "row" else a.arr.shape[0], bc * (nh if a.ph else 1)), a.gdt if a.kind == "row" else F32))
    in_specs = [_arg_spec(a, tr, nh, ntab) for a in list(args) + list(cts)]
    return pl.pallas_call(
        body, grid=(t // tr,), in_specs=in_specs, out_specs=out_specs, out_shape=out_shape,
        compiler_params=_cparams(("arbitrary",)), name=name)(*[a.arr for a in list(args) + list(cts)])


def _conv_taps(x, w):
    rows = lax.broadcasted_iota(jnp.int32, x.shape, 0)
    y = x * w[CONV_K - 1:CONV_K, :]
    for s in range(1, CONV_K):
        y = y + jnp.where(rows >= s, pltpu.roll(x, s, 0), 0.0) * w[CONV_K - 1 - s:CONV_K - s, :]
    return y


CONV_HEADS = 4
CONV_BLOCKS_PER_THIRD = N_HEADS // CONV_HEADS


def _conv_post(y, block):
    a = _silu(y)
    normed = block < 2 * CONV_BLOCKS_PER_THIRD
    scale = jnp.where(block < CONV_BLOCKS_PER_THIRD, HEAD ** -0.5, 1.0)
    return a * jnp.where(normed, lax.rsqrt(jnp.sum(a * a, -1, keepdims=True) + EPS) * scale, 1.0)


def conv_fwd(z, w, lp):
    t, width = z.shape
    cols = CONV_HEADS * HEAD

    def body(z_ref, w_ref, o_ref, y_ref):
        block = pl.program_id(1)
        for h in range(CONV_HEADS):
            cs = slice(h * HEAD, (h + 1) * HEAD)
            y = _conv_taps(z_ref[:, cs].astype(F32), w_ref[:, cs])
            y_ref[:, cs] = y.astype(y_ref.dtype)
            o_ref[:, cs] = _conv_post(y, block)

    blk = pl.BlockSpec((lp, cols), lambda b, j: (b, j))
    out = jax.ShapeDtypeStruct((t, width), F32)
    return pl.pallas_call(
        body, grid=(t // lp, width // cols), in_specs=[blk, pl.BlockSpec((CONV_K, cols), lambda b, j: (0, j))],
        out_specs=[blk, blk], out_shape=[out, jax.ShapeDtypeStruct((t, width), _MXU_DTYPE)],
        compiler_params=_cparams(("arbitrary", "arbitrary")), name="a_conv_fwd")(z, w)


def conv_bwd(z, y, w, dout, lp):
    t, width = z.shape
    cols = CONV_HEADS * HEAD

    def body(z_ref, y_ref, w_ref, g_ref, dz_ref, dw_ref):
        block = pl.program_id(0)

        @pl.when(pl.program_id(1) == 0)
        def _():
            dw_ref[...] = jnp.zeros_like(dw_ref)

        for h in range(CONV_HEADS):
            cs = slice(h * HEAD, (h + 1) * HEAD)
            x, wv = z_ref[:, cs].astype(F32), w_ref[:, cs]
            _, vjp = jax.vjp(lambda y_: _conv_post(y_, block), y_ref[:, cs].astype(F32))
            (dy,) = vjp(g_ref[:, cs])
            rows = lax.broadcasted_iota(jnp.int32, x.shape, 0)
            dx = dy * wv[CONV_K - 1:CONV_K, :]
            dw_ref[CONV_K - 1:CONV_K, cs] += jnp.sum(dy * x, axis=0, keepdims=True)
            for s in range(1, CONV_K):
                dy_up = jnp.where(rows < lp - s, pltpu.roll(dy, lp - s, 0), 0.0)
                dx = dx + dy_up * wv[CONV_K - 1 - s:CONV_K - s, :]
                dw_ref[CONV_K - 1 - s:CONV_K - s, cs] += jnp.sum(dy_up * x, axis=0, keepdims=True)
            dz_ref[:, cs] = dx.astype(dz_ref.dtype)

    blk = pl.BlockSpec((lp, cols), lambda j, b: (b, j))
    w_blk = pl.BlockSpec((CONV_K, cols), lambda j, b: (0, j))
    return pl.pallas_call(
        body, grid=(width // cols, t // lp), in_specs=[blk, blk, w_blk, blk], out_specs=[blk, w_blk],
        out_shape=[jax.ShapeDtypeStruct((t, width), _MXU_DTYPE), jax.ShapeDtypeStruct((CONV_K, width), F32)],
        compiler_params=_cparams(("arbitrary", "arbitrary")), name="a_conv_bwd")(z, y, w, dout)


def _delta_chunk(q, k, v, ba, alog, dtb, state, t_stored):
    n_g, c = q.shape[0], q.shape[1]
    lane = lax.broadcasted_iota(jnp.int32, (1, HEAD), 1)

    def pick(xs, offset):
        cols = [jnp.sum(xs[i // N_HEADS if len(xs) > 1 else 0] * (lane == offset + i % N_HEADS).astype(F32), axis=1, keepdims=True)[None]
                for i in range(n_g)]
        return jnp.concatenate(cols, 0)

    b_raw, a_raw = pick(ba, 0), pick(ba, N_HEADS)
    a_log, dt_bias = pick((alog,), 0), pick((dtb,), 0)
    beta = _sigmoid(b_raw)
    g = -jnp.exp(a_log) * _softplus(a_raw + dt_bias)
    ri = lax.broadcasted_iota(jnp.int32, (c, c), 0)
    ci = lax.broadcasted_iota(jnp.int32, (c, c), 1)
    tril = ci <= ri
    lower = jnp.broadcast_to(tril.astype(F32), (n_g, c, c))
    gc_col = _dot_01(lower, g * jnp.ones((1, 1, HEAD), F32))[:, :, :1]
    gc_row = _dot_01(jnp.ones((n_g, 8, c), F32), g * (ri <= ci).astype(F32)[None])[:, 0:1, :]
    gc_last = jnp.sum(g, axis=1, keepdims=True)
    decay = jnp.exp(jnp.where(tril, gc_col - gc_row, NEG))
    e_gc = jnp.exp(gc_col)
    kb = k * beta
    a_mat = jnp.where(ci < ri, mm_nt(kb, k) * decay, 0.0)
    t_inv = _inv_unit_lower(a_mat) if t_stored is None else _inv_lookup(a_mat, t_stored)
    u_base = mm_nn(t_inv, v * beta)
    w_dec = mm_nn(t_inv, kb * e_gc)
    attn = jnp.where(tril, mm_nt(q, k) * decay, 0.0)
    u = u_base - mm_nn(w_dec, state)
    o = mm_nn(q * e_gc, state) + mm_nn(attn, u)
    new_state = state * jnp.exp(gc_last) + mm_tn(k * jnp.exp(gc_last - gc_col), u)
    return o, new_state, t_inv


DELTA_STEP_FWD = (4, 2)
DELTA_STEP_BWD = (2, 2)


def _heads_of(ref, rs, first_col):
    return jnp.stack([ref[i // N_HEADS, rs, first_col + (i % N_HEADS) * HEAD:first_col + (i % N_HEADS + 1) * HEAD]
                      for i in range(ref.shape[0] * N_HEADS)])


def _qkv_heads(ref, rs, part):
    return _heads_of(ref, rs, part * N_HEADS * HEAD)


def _by_sequence(a, lp):
    return a.reshape(a.shape[0] // lp, lp, a.shape[1])


def _ride(bufs, scatter, refs_in, refs_out, sems, first, last, two_level=False):
    if not bufs:
        return lambda: None
    make = lambda: (TwoLevelGather if two_level else Exchange)(refs_in, refs_out, *sems, scatter)

    @pl.when(first)
    def _():
        make().start()

    def finish():
        @pl.when(last)
        def _():
            make().wait()

    return finish


def delta_fwd(qkv, ba, ba_block, alog, dtb, lp, gather=()):
    t = qkv.shape[0]
    nb, nc = t // lp, lp // CHUNK
    seqs, cps = DELTA_STEP_FWD
    ng, rows = nc // cps, cps * CHUNK
    nx = len(gather)
    nbg = nb // seqs
    assert nc % cps == 0 and nb % seqs == 0

    def body(*refs):
        qkv_ref, ba_ref, al_ref, dt_ref = refs[:4]
        o_ref, s_ref, t_ref = refs[4 + nx:7 + nx]
        state_ref = refs[7 + 2 * nx]
        b, n = pl.program_id(0), pl.program_id(1)
        finish = _ride(gather, False, refs[4:4 + nx], refs[7 + nx:7 + 2 * nx], refs[8 + 2 * nx:], (b == 0) & (n == 0), (b == nbg - 1) & (n == ng - 1))

        @pl.when(n == 0)
        def _():
            state_ref[...] = jnp.zeros_like(state_ref)

        al, dtv = al_ref[...], dt_ref[...]
        for c in range(cps):
            rs = slice(c * CHUNK, (c + 1) * CHUNK)
            state = state_ref[...]
            o, new_state, t_inv = _delta_chunk(_qkv_heads(qkv_ref, rs, 0), _qkv_heads(qkv_ref, rs, 1), _qkv_heads(qkv_ref, rs, 2),
                                               tuple(ba_ref[i, rs, :] for i in range(seqs)), al, dtv, state, None)
            for i in range((seqs * N_HEADS)):
                seq, g = divmod(i, N_HEADS)
                o_ref[seq, rs, g * HEAD:(g + 1) * HEAD] = o[i]
                s_ref[seq, g, c] = state[i]
                t_ref[seq, g, c] = t_inv[i]
            state_ref[...] = new_state
        finish()

    rows_of = lambda width: pl.BlockSpec((seqs, rows, width), lambda b, n: (b, n, 0))
    par_spec = pl.BlockSpec((1, HEAD), lambda b, n: (0, 0))
    out = pl.pallas_call(
        body, grid=(nbg, ng),
        in_specs=[rows_of(3 * N_HEADS * HEAD), pl.BlockSpec((seqs, rows, HEAD), lambda b, n: (b, n, ba_block)), par_spec, par_spec] + [_HBM] * nx,
        out_specs=[rows_of(N_HEADS * HEAD), pl.BlockSpec((seqs, N_HEADS, cps, HEAD, HEAD), lambda b, n: (b, 0, n, 0, 0)),
                   pl.BlockSpec((seqs, N_HEADS, cps, CHUNK, CHUNK), lambda b, n: (b, 0, n, 0, 0))] + [_HBM] * nx,
        out_shape=[jax.ShapeDtypeStruct((nb, lp, N_HEADS * HEAD), F32), jax.ShapeDtypeStruct((nb, N_HEADS, nc, HEAD, HEAD), F32),
                   jax.ShapeDtypeStruct((nb, N_HEADS, nc, CHUNK, CHUNK), F32)] + Exchange.out_shape(gather, False),
        scratch_shapes=[pltpu.VMEM(((seqs * N_HEADS), HEAD, HEAD), F32)] + (Exchange.scratch(nx) if nx else []),
        compiler_params=_cparams(("arbitrary", "arbitrary")), name="delta_fwd")(_by_sequence(qkv, lp), _by_sequence(ba, lp), alog, dtb, *gather)
    return [out[0].reshape(t, N_HEADS * HEAD)] + list(out[1:])


def delta_bwd(qkv, ba, ba_block, alog, dtb, states, t_invs, do, lp, scatter=()):
    t = qkv.shape[0]
    nb, nc = t // lp, lp // CHUNK
    seqs, cps = DELTA_STEP_BWD
    ng, rows = nc // cps, cps * CHUNK
    nx = len(scatter)
    nbg = nb // seqs

    def body(*refs):
        qkv_ref, ba_ref, al_ref, dt_ref, s_ref, t_ref, do_ref = refs[:7]
        dqkv_ref, dba_ref, dal_ref, ddt_ref = refs[7 + nx:11 + nx]
        dstate_ref = refs[11 + 2 * nx]
        b, step = pl.program_id(0), pl.program_id(1)
        finish = _ride(scatter, True, refs[7:7 + nx], refs[11 + nx:11 + 2 * nx], refs[12 + 2 * nx:], (b == 0) & (step == 0),
                       (b == nbg - 1) & (step == ng - 1))

        @pl.when(step == 0)
        def _():
            dstate_ref[...] = jnp.zeros_like(dstate_ref)

        @pl.when((b == 0) & (step == 0))
        def _():
            dal_ref[...] = jnp.zeros_like(dal_ref)
            ddt_ref[...] = jnp.zeros_like(ddt_ref)

        al, dtv = al_ref[...], dt_ref[...]
        d_al = jnp.zeros((1, HEAD), F32)
        d_dt = jnp.zeros((1, HEAD), F32)
        for c in reversed(range(cps)):
            rs = slice(c * CHUNK, (c + 1) * CHUNK)
            t_n = jnp.stack([t_ref[i // N_HEADS, i % N_HEADS, c] for i in range((seqs * N_HEADS))])
            s_n = jnp.stack([s_ref[i // N_HEADS, i % N_HEADS, c] for i in range((seqs * N_HEADS))])

            def f(q_, k_, v_, ba_, al_, dt_, s_, t_n=t_n):
                return _delta_chunk(q_, k_, v_, ba_, al_, dt_, s_, t_n)[:2]

            _, vjp = jax.vjp(f, _qkv_heads(qkv_ref, rs, 0), _qkv_heads(qkv_ref, rs, 1), _qkv_heads(qkv_ref, rs, 2), tuple(ba_ref[i, rs, :] for i in range(seqs)), al, dtv, s_n)
            grads = vjp((_heads_of(do_ref, rs, 0), dstate_ref[...]))
            for part in range(3):
                for i in range((seqs * N_HEADS)):
                    col = (part * N_HEADS + i % N_HEADS) * HEAD
                    dqkv_ref[i // N_HEADS, rs, col:col + HEAD] = grads[part][i]
            for i in range(seqs):
                dba_ref[i, rs, :] = grads[3][i]
            d_al, d_dt = d_al + grads[4], d_dt + grads[5]
            dstate_ref[...] = grads[6]
        dal_ref[...] += d_al
        ddt_ref[...] += d_dt
        finish()

    rows_of = lambda width: pl.BlockSpec((seqs, rows, width), lambda b, n: (b, ng - 1 - n, 0))
    par_spec = pl.BlockSpec((1, HEAD), lambda b, n: (0, 0))
    out = pl.pallas_call(
        body, grid=(nbg, ng),
        in_specs=[rows_of(3 * N_HEADS * HEAD), pl.BlockSpec((seqs, rows, HEAD), lambda b, n: (b, ng - 1 - n, ba_block)), par_spec, par_spec,
                  pl.BlockSpec((seqs, N_HEADS, cps, HEAD, HEAD), lambda b, n: (b, 0, ng - 1 - n, 0, 0)),
                  pl.BlockSpec((seqs, N_HEADS, cps, CHUNK, CHUNK), lambda b, n: (b, 0, ng - 1 - n, 0, 0)), rows_of(N_HEADS * HEAD)] + [_HBM] * nx,
        out_specs=[rows_of(3 * N_HEADS * HEAD), rows_of(HEAD), par_spec, par_spec] + [_HBM] * nx,
        out_shape=[jax.ShapeDtypeStruct((nb, lp, 3 * N_HEADS * HEAD), F32), jax.ShapeDtypeStruct((nb, lp, HEAD), F32),
                   jax.ShapeDtypeStruct((1, HEAD), F32), jax.ShapeDtypeStruct((1, HEAD), F32)] + Exchange.out_shape(scatter, True),
        scratch_shapes=[pltpu.VMEM(((seqs * N_HEADS), HEAD, HEAD), F32)] + (Exchange.scratch(nx) if nx else []),
        compiler_params=_cparams(("arbitrary", "arbitrary")), name="delta_bwd")(
            _by_sequence(qkv, lp), _by_sequence(ba, lp), alog, dtb, states, t_invs, _by_sequence(do, lp), *scatter)
    return [out[0].reshape(t, 3 * N_HEADS * HEAD), out[1].reshape(t, HEAD)] + list(out[2:])


ATT_Q_TILE = 256
ATT_K_TILE = 512
ATT_K_TILE_BWD = 1024
ATT_SCALE = QK_DIM ** -0.5


def _tiles(end, size):
    return [(s, min(s + size, end)) for s in range(0, end, size)]


def _att_visible(q0, q1, k0, k1, keys_first):
    if k1 <= q0 + CHUNK and k0 >= PAD_ROWS:
        return None
    shape = (k1 - k0, q1 - q0) if keys_first else (q1 - q0, k1 - k0)
    qpos = q0 + lax.broadcasted_iota(jnp.int32, shape, 1 if keys_first else 0)
    kpos = k0 + lax.broadcasted_iota(jnp.int32, shape, 0 if keys_first else 1)
    shift = CHUNK.bit_length() - 1
    return (jnp.right_shift(kpos, shift) <= jnp.right_shift(qpos, shift)) & (kpos >= PAD_ROWS)


def _att_seq_specs(lp):
    return pl.BlockSpec((lp, QK_PAD), lambda b, h: (b, h)), pl.BlockSpec((lp, HEAD), lambda b, h: (b, h))


def flash_fwd(q, k, v, lp):
    t = q.shape[0]
    qk_seq, o_seq = _att_seq_specs(lp)

    def body(q_ref, k_ref, v_ref, o_ref, lse_ref):
        q_tiles = _tiles(lp, ATT_Q_TILE)

        def score_steps(q0, q1, out):
            def step(k0, k1):
                s = mm_nt(q_ref[q0:q1, :], k_ref[k0:k1, :])
                vis = _att_visible(q0, q1, k0, k1, False)
                s = s if vis is None else jnp.where(vis, s, NEG)
                out["scores"].append(s)
                row_max = jnp.max(s, -1, keepdims=True)
                out["m"] = row_max if out["m"] is None else jnp.maximum(out["m"], row_max)
            return [functools.partial(step, k0, k1) for k0, k1 in _tiles(q1, ATT_K_TILE)]

        cur = {"scores": [], "m": None}
        for step in score_steps(*q_tiles[0], cur):
            step()
        for i, (q0, q1) in enumerate(q_tiles):
            nxt = {"scores": [], "m": None}
            ahead = score_steps(*q_tiles[i + 1], nxt) if i + 1 < len(q_tiles) else []
            l = jnp.zeros((q1 - q0, 1), F32)
            acc = jnp.zeros((q1 - q0, HEAD), F32)
            for s, (k0, k1) in zip(cur["scores"], _tiles(q1, ATT_K_TILE), strict=True):
                if ahead:
                    ahead.pop(0)()
                p = jnp.exp2(s - cur["m"])
                l = l + jnp.sum(p, -1, keepdims=True)
                acc = acc + mm_nn(p, v_ref[k0:k1, :])
            for step in ahead:
                step()
            o_ref[q0:q1, :] = acc / l
            lse_ref[q0:q1, :] = jnp.broadcast_to(cur["m"] + jnp.log2(l), (q1 - q0, HEAD))
            cur = nxt

    big = jax.ShapeDtypeStruct((t, N_HEADS * HEAD), F32)
    return pl.pallas_call(
        body, grid=(t // lp, N_HEADS), in_specs=[qk_seq, qk_seq, o_seq], out_specs=[o_seq, o_seq], out_shape=[big, big],
        compiler_params=_cparams(("arbitrary", "arbitrary")), name="flash_fwd")(q, k, v)


def flash_bwd(q, k, v, o, lse, do, lp):
    t = q.shape[0]
    qk_seq, o_seq = _att_seq_specs(lp)

    def body(q_ref, k_ref, v_ref, o_ref, lse_ref, do_ref, dq_ref, dk_out_ref, dv_out_ref, dk_ref, dv_ref):
        dk_ref[...] = jnp.zeros_like(dk_ref)
        dv_ref[...] = jnp.zeros_like(dv_ref)
        for q0, q1 in _tiles(lp, ATT_Q_TILE):
            qb, dob = q_ref[q0:q1, :], do_ref[q0:q1, :]
            lse_row = jnp.transpose(lse_ref[q0:q1, :])[0:1, :]
            dob_ln2 = dob * math.log(2.0)
            dsum_row = jnp.sum(jnp.transpose(dob_ln2 * o_ref[q0:q1, :]), axis=0, keepdims=True)
            dq = jnp.zeros((q1 - q0, QK_PAD), F32)
            for k0, k1 in _tiles(q1, ATT_K_TILE_BWD):
                kb, vb = k_ref[k0:k1, :], v_ref[k0:k1, :]
                s = mm_nt(kb, qb)
                vis = _att_visible(q0, q1, k0, k1, True)
                s = s if vis is None else jnp.where(vis, s, NEG)
                p = jnp.exp2(s - lse_row)
                ds = p * (mm_nt(vb, dob_ln2) - dsum_row)
                dv_ref[k0:k1, :] += mm_nn(p, dob)
                dk_ref[k0:k1, :] += mm_nn(ds, qb)
                dq = dq + mm_tn(ds, kb)
            dq_ref[q0:q1, :] = dq.astype(dq_ref.dtype)
        dk_out_ref[...] = dk_ref[...].astype(dk_out_ref.dtype)
        dv_out_ref[...] = dv_ref[...].astype(dv_out_ref.dtype)

    narrow = _MXU_DTYPE
    return pl.pallas_call(
        body, grid=(t // lp, N_HEADS), in_specs=[qk_seq, qk_seq, o_seq, o_seq, o_seq, o_seq], out_specs=[qk_seq, qk_seq, o_seq],
        out_shape=[jax.ShapeDtypeStruct((t, N_HEADS * QK_PAD), narrow), jax.ShapeDtypeStruct((t, N_HEADS * QK_PAD), narrow),
                   jax.ShapeDtypeStruct((t, N_HEADS * HEAD), narrow)],
        scratch_shapes=[pltpu.VMEM((lp, QK_PAD), F32), pltpu.VMEM((lp, HEAD), F32)],
        compiler_params=_cparams(("arbitrary", "arbitrary")), name="flash_bwd")(q, k, v, o, lse, do)


def loss_head(h2, target, lp):
    nb, seq, d = target.shape
    cols = _pick(d, (512, 128))
    ncol = d // cols

    def body(h_ref, t_ref, loss_ref, dh_ref, acc_ref):
        b, j = pl.program_id(0), pl.program_id(1)

        @pl.when((b == 0) & (j == 0))
        def _():
            acc_ref[...] = jnp.zeros_like(acc_ref)

        err = h_ref[LEAD:, :] - t_ref[...]
        dh_ref[:LEAD, :] = jnp.zeros((LEAD, cols), F32)
        dh_ref[LEAD:, :] = err * (1.0 / d)
        acc_ref[...] += jnp.sum(err * err, axis=0, keepdims=True)

        @pl.when((b == nb - 1) & (j == ncol - 1))
        def _():
            loss_ref[...] = jnp.sum(acc_ref[...], axis=1, keepdims=True) * (0.5 / d)

    return pl.pallas_call(
        body, grid=(nb, ncol),
        in_specs=[pl.BlockSpec((None, lp, cols), lambda b, j: (b, 0, j)), pl.BlockSpec((None, seq, cols), lambda b, j: (b, 0, j))],
        out_specs=[pl.BlockSpec((1, 1), lambda b, j: (0, 0)), pl.BlockSpec((None, lp, cols), lambda b, j: (b, 0, j))],
        out_shape=[jax.ShapeDtypeStruct((1, 1), F32), jax.ShapeDtypeStruct((nb, lp, d), F32)],
        scratch_shapes=[pltpu.VMEM((1, cols), F32)], compiler_params=_cparams(("arbitrary", "arbitrary")), name="loss_head")(h2, target)


def gated_out(name, o, gate, gain, w, res):
    t, kw = o.shape
    d = w.shape[1]
    tm = _pick(t, (512, 256, 128))

    def body(*refs):
        o_ref, gate_ref = refs[:2]
        w_ref, r_ref, h_ref, g_ref = refs[-4:]
        if gain is None:
            g_ref[...] = _f_gate(o_ref[...], gate_ref[...])[0].astype(g_ref.dtype)
        else:
            for h in range(N_HEADS):
                cs = slice(h * HEAD, (h + 1) * HEAD)
                g_ref[:, cs] = _f_out_gate(o_ref[:, cs], gate_ref[:, cs], refs[2][...])[0].astype(g_ref.dtype)
        h_ref[...] = r_ref[...] + _dot(g_ref[...], w_ref[...], ((1,), (0,)))

    rows = lambda width: pl.BlockSpec((tm, width), lambda i: (i, 0))
    whole = lambda a: pl.BlockSpec(a.shape, lambda i: (0, 0))
    params = [] if gain is None else [gain]
    return pl.pallas_call(
        body, grid=(t // tm,), in_specs=[rows(kw), rows(kw)] + [whole(p) for p in params] + [whole(w), rows(d)], out_specs=[rows(d), rows(kw)],
        out_shape=[jax.ShapeDtypeStruct((t, d), F32), jax.ShapeDtypeStruct((t, kw), _MXU_DTYPE)],
        compiler_params=_cparams(("parallel",)), name=name)(o, gate, *params, w, res)


def embed_norm(x, meta, gain, lp, gather=()):
    nb, seq, d = x.shape
    nblk, nx = lp // LEAD, len(gather)

    def body(*refs):
        x_ref, meta_ref, g_ref = refs[:3]
        h_ref, hn_ref = refs[3 + nx:5 + nx]
        b, i = pl.program_id(0), pl.program_id(1)
        finish = _ride(gather, False, refs[3:3 + nx], refs[5 + nx:5 + 2 * nx], refs[5 + 2 * nx:], (b == 0) & (i == 0), (b == nb - 1) & (i == nblk - 1),
                       two_level=True)

        @pl.when(i == 0)
        def _():
            h_ref[:PAD_ROWS, :] = jnp.zeros((PAD_ROWS, d), F32)
            h_ref[PAD_ROWS:, :] = meta_ref[...]

        @pl.when(i > 0)
        def _():
            h_ref[...] = x_ref[...]

        hn_ref[...] = _rms(h_ref[...], g_ref[...]).astype(hn_ref.dtype)
        finish()

    rows = pl.BlockSpec((LEAD, d), lambda b, i: (b * nblk + i, 0))
    out = pl.pallas_call(
        body, grid=(nb, nblk),
        in_specs=[pl.BlockSpec((None, LEAD, d), lambda b, i: (b, jnp.maximum(i - 1, 0), 0)), pl.BlockSpec((N_META, d), lambda b, i: (0, 0)),
                  pl.BlockSpec((1, d), lambda b, i: (0, 0))] + [_HBM] * nx,
        out_specs=[rows, rows] + [_HBM] * nx,
        out_shape=[jax.ShapeDtypeStruct((nb * lp, d), F32), jax.ShapeDtypeStruct((nb * lp, d), _MXU_DTYPE)] + Exchange.out_shape(gather, False),
        scratch_shapes=Exchange.scratch(nx) if nx else [],
        compiler_params=_cparams(("arbitrary", "arbitrary")), name="embed_norm")(x, meta, gain, *gather)
    return list(out)


def meta_grad(dh0):
    nb, _, d = dh0.shape

    def body(g_ref, o_ref):
        @pl.when(pl.program_id(0) == 0)
        def _():
            o_ref[...] = jnp.zeros_like(o_ref)

        o_ref[...] += g_ref[PAD_ROWS:LEAD, :]

    return pl.pallas_call(
        body, grid=(nb,), in_specs=[pl.BlockSpec((None, LEAD, d), lambda b: (b, 0, 0))],
        out_specs=pl.BlockSpec((N_META, d), lambda b: (0, 0)), out_shape=jax.ShapeDtypeStruct((N_META, d), F32),
        compiler_params=_cparams(("arbitrary",)), name="meta_grad")(dh0)


_HBM = pl.BlockSpec(memory_space=pltpu.HBM)


def _mesh_pos():
    x, y, c = lax.axis_index("x"), lax.axis_index("y"), lax.axis_index("c")
    return x, y, c


def _peer(x, y, c, k):
    px = 1 - x if k & 4 else x
    py = 1 - y if k & 2 else y
    pc = 1 - c if k & 1 else c
    return (px, py, pc), 4 * px + 2 * py + pc


class Exchange:
    def __init__(self, x_refs, out_refs, send_sems, recv_sems, local_sems, scatter):
        self.x_refs, self.out_refs, self.scatter = x_refs, out_refs, scatter
        self.send_sems, self.recv_sems, self.local_sems = send_sems, recv_sems, local_sems
        self.pos = _mesh_pos()
        x, y, c = self.pos
        self.me = 4 * x + 2 * y + c

    @staticmethod
    def scratch(n):
        return [pltpu.SemaphoreType.DMA((n, N_DEV - 1)), pltpu.SemaphoreType.DMA((n, N_DEV - 1)), pltpu.SemaphoreType.DMA((n,))]

    @staticmethod
    def out_shape(bufs, scatter):
        return [jax.ShapeDtypeStruct(b.shape if scatter else (N_DEV,) + b.shape, b.dtype) for b in bufs]

    def _local(self, i):
        return pltpu.make_async_copy(self.x_refs[i].at[self.me] if self.scatter else self.x_refs[i], self.out_refs[i].at[self.me], self.local_sems.at[i])

    def _copy(self, i, k, landing):
        peer, peer_id = _peer(*self.pos, k)
        src = self.x_refs[i].at[peer_id] if self.scatter else self.x_refs[i]
        return pltpu.make_async_remote_copy(src_ref=src, dst_ref=self.out_refs[i].at[peer_id if landing else self.me],
                                            send_sem=self.send_sems.at[i, k - 1], recv_sem=self.recv_sems.at[i, k - 1],
                                            device_id=peer, device_id_type=pl.DeviceIdType.MESH)

    def start(self):
        for i in range(len(self.x_refs)):
            self._local(i).start()
        for k in range(1, N_DEV):
            for i in range(len(self.x_refs)):
                self._copy(i, k, False).start()

    def wait(self):
        for k in range(1, N_DEV):
            for i in range(len(self.x_refs)):
                self._copy(i, k, True).wait_recv()
        for k in range(1, N_DEV):
            for i in range(len(self.x_refs)):
                self._copy(i, k, False).wait_send()
        for i in range(len(self.x_refs)):
            self._local(i).wait()


class TwoLevelGather(Exchange):
    DIRECT = (1, 4, 2, 6)
    FROM_CHIPS = (4, 2, 6)

    def _forward(self, i, k):
        _, origin = _peer(*self.pos, k)
        sibling, _ = _peer(*self.pos, 1)
        block = self.out_refs[i].at[origin]
        return pltpu.make_async_remote_copy(src_ref=block, dst_ref=block, send_sem=self.send_sems.at[i, (k ^ 1) - 1],
                                            recv_sem=self.recv_sems.at[i, (k ^ 1) - 1], device_id=sibling, device_id_type=pl.DeviceIdType.MESH)

    def start(self):
        assert not self.scatter
        for i in range(len(self.x_refs)):
            self._local(i).start()
        for k in self.DIRECT:
            for i in range(len(self.x_refs)):
                self._copy(i, k, False).start()

    def wait(self):
        n = range(len(self.x_refs))
        for k in self.FROM_CHIPS:
            for i in n:
                self._copy(i, k, True).wait_recv()
                self._forward(i, k).start()
        for k in (1, 5, 3, 7):
            for i in n:
                self._copy(i, k, True).wait_recv()
        for k in self.DIRECT:
            for i in n:
                self._copy(i, k, False).wait_send()
        for k in self.FROM_CHIPS:
            for i in n:
                self._forward(i, k).wait_send()
        for i in n:
            self._local(i).wait()


def _exchange(name, bufs, scatter):
    n = len(bufs)

    def body(*refs):
        ex = Exchange(refs[:n], refs[n:2 * n], *refs[2 * n:], scatter)
        ex.start()
        ex.wait()

    return pl.pallas_call(body, in_specs=[_HBM] * n, out_specs=[_HBM] * n, out_shape=Exchange.out_shape(bufs, scatter),
                          scratch_shapes=Exchange.scratch(n), name=name)(*bufs)


def _f_rms(x, g):
    return (_rms(x, g),)


def _f_rms2(x, g1, g2):
    r = x * lax.rsqrt(jnp.sum(x * x, -1, keepdims=True) / x.shape[-1] + EPS)
    return r * g1, r * g2


@jax.custom_vjp
def _out_gate(o, gate, gain):
    return _rms(o, gain) * _silu(gate)


def _out_gate_bwd(res, g):
    o, gate, gain = res
    r = lax.rsqrt(jnp.sum(o * o, -1, keepdims=True) / o.shape[-1] + EPS)
    n = o * r
    s = _sigmoid(gate)
    g_norm = g * (gate * s)
    d_gate = g * (n * gain) * (s * (1.0 + gate * (1.0 - s)))
    gn = g_norm * gain
    d_o = r * (gn - n * (jnp.sum(gn * n, -1, keepdims=True) / o.shape[-1]))
    return d_o, d_gate, jnp.sum(g_norm * n, 0, keepdims=True)


_out_gate.defvjp(lambda o, gate, gain: (_out_gate(o, gate, gain), (o, gate, gain)), _out_gate_bwd)


def _f_out_gate(o, gate, gain):
    return (_out_gate(o, gate, gain),)


def _f_gate(o, gate):
    return (o * _silu(gate),)


def _swap_rope_halves(x):
    return pltpu.roll(x, ROPE // 2, 1) + pltpu.roll(x, HEAD - ROPE // 2, 1)


def _qk_final_inv_rms(nope, rope_in):
    ms = (jnp.sum(nope * nope, -1, keepdims=True) + jnp.sum(rope_in * rope_in, -1, keepdims=True)) / QK_DIM
    return lax.rsqrt(ms + EPS)


@functools.partial(jax.custom_vjp, nondiff_argnums=(0,))
def _qk_final(scale, nope, rope_in, g_nope, g_rope, cos, sin):
    r = _qk_final_inv_rms(nope, rope_in)
    b = rope_in * (r * g_rope)
    out = jnp.concatenate([nope * (r * g_nope), b * cos + _swap_rope_halves(b) * sin], axis=1)
    return out if scale == 1.0 else out * scale


def _qk_final_fwd(scale, nope, rope_in, g_nope, g_rope, cos, sin):
    return _qk_final(scale, nope, rope_in, g_nope, g_rope, cos, sin), (nope, rope_in, g_nope, g_rope, cos, sin)


def _qk_final_bwd(scale, res, g):
    nope, rope_in, g_nope, g_rope, cos, sin = res
    r = _qk_final_inv_rms(nope, rope_in)
    ga, gb = g[:, :HEAD], g[:, HEAD:]
    if scale != 1.0:
        ga, gb = ga * scale, gb * scale
    db = gb * cos + _swap_rope_halves(gb * sin)
    t_a, t_b = ga * nope, db * rope_in
    d_r = jnp.sum(t_a * g_nope + t_b * g_rope, -1, keepdims=True)
    c = d_r * (r * r * r) * (-1.0 / QK_DIM)
    d_nope = ga * (r * g_nope) + nope * c
    d_rope = db * (r * g_rope) + rope_in * c
    d_g_nope = jnp.sum(t_a * r, 0, keepdims=True)
    d_g_rope = jnp.sum(t_b * r, 0, keepdims=True)
    return d_nope, d_rope, d_g_nope, d_g_rope, jnp.zeros_like(cos), jnp.zeros_like(sin)


_qk_final.defvjp(_qk_final_fwd, _qk_final_bwd)


def _f_qk_final(scale, nope, rope_in, g_nope, g_rope, cos, sin):
    return (_qk_final(scale, nope, rope_in, g_nope, g_rope, cos, sin),)


def _rope_tables(lp):
    half = ROPE // 2
    pos = jnp.maximum(jnp.arange(lp) - PAD_ROWS, 0)
    inv = ROPE_THETA ** (-jnp.arange(half, dtype=F32) / half)
    ang = pos.astype(F32)[:, None] * inv[None, :]
    zeros = jnp.zeros((lp, HEAD - ROPE), F32)
    cos = jnp.concatenate([jnp.cos(ang), jnp.cos(ang), zeros], 1)
    sin = jnp.concatenate([-jnp.sin(ang), jnp.sin(ang), zeros], 1)
    return cos, sin


def _pad_lanes(w, width=HEAD):
    return jnp.pad(w, ((0, 0), (0, width - w.shape[1])))


def _pad_rows(w, rows=HEAD):
    return jnp.pad(w, ((0, rows - w.shape[0]), (0, 0)))


def _split_heads_qk_t(w_t):
    k = w_t.shape[1]
    return jnp.pad(w_t.reshape(N_HEADS, QK_DIM, k), ((0, 0), (0, QK_PAD - QK_DIM), (0, 0))).reshape(N_HEADS * QK_PAD, k)


def _merge_heads_qk_t(g_t):
    k = g_t.shape[1]
    return g_t.reshape(N_HEADS, QK_PAD, k)[:, :QK_DIM].reshape(N_HEADS * QK_DIM, k)


@functools.partial(jax.custom_vjp, nondiff_argnums=(0,))
def _q_final(scale, qh, g_nope, g_rope, cos, sin):
    return _qk_final(scale, qh[:, :HEAD], qh[:, HEAD:], g_nope, g_rope, cos, sin)


def _q_final_bwd(scale, res, g):
    qh, g_nope, g_rope, cos, sin = res
    grads = _qk_final_bwd(scale, (qh[:, :HEAD], qh[:, HEAD:], g_nope, g_rope, cos, sin), g)
    return (jnp.concatenate(grads[:2], axis=1),) + tuple(grads[2:])


_q_final.defvjp(lambda scale, qh, *rest: (_q_final(scale, qh, *rest), (qh,) + rest), _q_final_bwd)


def _f_q_final(scale, qh, g_nope, g_rope, cos, sin):
    return (_q_final(scale, qh, g_nope, g_rope, cos, sin),)


def local_step(x, target, w, deferred=None):
    nb, seq, d = x.shape
    lp = seq + LEAD
    t = nb * lp
    tr = _pick(lp, (544, 128))
    ntab = lp // tr
    mxu = _MXU_DTYPE
    kw = N_HEADS * HEAD

    a_conv = w["a_conv"].T
    alog, dtb, o_gain = _pad_lanes(w["a_log"]), _pad_lanes(w["a_dt_bias"]), w["a_o_gain"]
    a_norm, kv_norm, b_norm = w["a_norm"], w["kv_norm"][None, :], w["b_norm"]
    lat_norm, qlat_norm = w["kv_latent_norm"][None, :], w["b_q_latent_norm"]
    kg_nope, kg_rope = w["k_gain"][None, :HEAD], _pad_lanes(w["k_gain"][None, HEAD:])
    qg_nope, qg_rope = w["b_q_gain"][:, :HEAD], _pad_lanes(w["b_q_gain"][:, HEAD:])
    cos, sin = _rope_tables(lp)

    h0, hn, *gathered = embed_norm(x, w["meta_tokens"].T, a_norm, lp, gather=deferred.first_gather_bufs if deferred else ())
    if deferred:
        w = {**w, **deferred.finish_first(gathered)}
    a_w_in_t = w["a_w_in"].astype(mxu)
    w_qkv_t, w_gba_t = a_w_in_t[:3 * kw], _pad_rows(a_w_in_t[3 * kw:], kw + HEAD)
    z_qkv = matmul("a_in_qkv", hn, w_qkv_t, "nt", out_dtype=mxu)
    z_gba = matmul("a_in_gate_ba", hn, w_gba_t, "nt")
    ba_block = kw // HEAD
    qkv_a, y_conv = conv_fwd(z_qkv, a_conv, lp)
    o_a, states, t_invs, *gathered = delta_fwd(qkv_a, z_gba, ba_block, alog, dtb, lp, gather=deferred.gather_bufs if deferred else ())
    if deferred:
        w = {**w, **deferred.finish(gathered)}
    a_w_out = w["a_w_out"].astype(mxu)
    w_down = _pad_lanes(w["kv_w_down"], KV_RANK + HEAD).astype(mxu)
    w_ukv_t = jnp.concatenate([w["kv_w_uk"], w["kv_w_uv"]], 0).astype(mxu)
    b_w_in_t = w["b_w_in"].astype(mxu)
    w_cq_t, w_gb_t = b_w_in_t[:Q_RANK], b_w_in_t[Q_RANK:]
    w_q_t = _split_heads_qk_t(w["b_w_uq"]).astype(mxu)
    b_w_out = w["b_w_out"].astype(mxu)
    og_args = [Arg(o_a, bc=HEAD, ph=True, diff=True), Arg(z_gba, bc=HEAD, ph=True, diff=True, gdt=mxu), Arg(o_gain, "par", diff=True)]
    h1, og_a = gated_out("a_out", o_a, z_gba, o_gain, a_w_out, h0)

    hk, hb = row_call("b_norms_fwd", _f_rms2, [Arg(h1), Arg(kv_norm, "par"), Arg(b_norm, "par")], [(d, mxu, d, False), (d, mxu, d, False)], tr)
    c_down = matmul("kv_down", hk, w_down, "nn")
    c_kv_arg = Arg(c_down, bc=KV_RANK, diff=True, gdt=mxu)
    k_pe_arg = Arg(c_down, bc=HEAD, base=KV_RANK // HEAD, diff=True)
    c_q_raw = matmul("b_in_q", hb, w_cq_t, "nt")
    gate_b = matmul("b_in_gate", hb, w_gb_t, "nt")
    (c_kv,) = row_call("kv_latent_fwd", _f_rms, [c_kv_arg, Arg(lat_norm, "par")], [(KV_RANK, mxu, KV_RANK, False)], tr)
    (c_q,) = row_call("q_latent_fwd", _f_rms, [Arg(c_q_raw), Arg(qlat_norm, "par")], [(Q_RANK, mxu, Q_RANK, False)], tr)
    k_nope = matmul("k_up", c_kv, w_ukv_t[:kw], "nt")
    v_b = matmul("v_up", c_kv, w_ukv_t[kw:], "nt", out_dtype=mxu)
    q_up = matmul("q_up", c_q, w_q_t, "nt")
    tabs = [Arg(cos, "tab"), Arg(sin, "tab")]
    k_args = [Arg(k_nope, bc=HEAD, ph=True, diff=True, gdt=mxu), k_pe_arg, Arg(kg_nope, "par", diff=True), Arg(kg_rope, "par", diff=True)] + tabs
    q_args = [Arg(q_up, bc=QK_PAD, ph=True, diff=True, gdt=mxu), Arg(qg_nope, "par", diff=True), Arg(qg_rope, "par", diff=True)] + tabs
    f_k_final, f_q_final = functools.partial(_f_qk_final, 1.0), functools.partial(_f_q_final, ATT_SCALE * math.log2(math.e))
    (k_fin,) = row_call("k_final_fwd", f_k_final, k_args, [(N_HEADS * QK_PAD, mxu, QK_PAD, True)], tr, nh=N_HEADS, ntab=ntab)
    (q_fin,) = row_call("q_final_fwd", f_q_final, q_args, [(N_HEADS * QK_PAD, mxu, QK_PAD, True)], tr, nh=N_HEADS, ntab=ntab)
    o_b, lse = flash_fwd(q_fin, k_fin, v_b, lp)
    gb_args = [Arg(o_b, diff=True), Arg(gate_b, diff=True, gdt=mxu)]
    h2, og_b = gated_out("b_out", o_b, gate_b, None, b_w_out, h1)

    loss, dh2 = loss_head(h2.reshape(nb, lp, d), target, lp)
    dh2 = dh2.reshape(t, d)
    grads = {}

    d_og_b = matmul("b_out_dx", dh2, b_w_out, "nt", out_dtype=mxu)
    grads["b_w_out"] = matmul("b_out_dw", og_b, dh2, "tn")
    d_o_b, d_gate_b = row_vjp_call("b_gate_bwd", _f_gate, gb_args, [Arg(d_og_b)], tr)
    dq_fin, dk_fin, dv_b = flash_bwd(q_fin, k_fin, v_b, o_b, lse, d_o_b, lp)
    dq_up, d_qg_nope, d_qg_rope = row_vjp_call(
        "q_final_bwd", f_q_final, q_args, [Arg(dq_fin, bc=QK_PAD, ph=True)], tr, nh=N_HEADS, ntab=ntab)
    dk_nope, dk_pe, d_kg_nope, d_kg_rope = row_vjp_call(
        "k_final_bwd", f_k_final, k_args, [Arg(dk_fin, bc=QK_PAD, ph=True)], tr, nh=N_HEADS, ntab=ntab)
    grads["b_q_gain"] = jnp.concatenate([d_qg_nope, d_qg_rope[:, :ROPE]], 1)
    grads["k_gain"] = jnp.concatenate([d_kg_nope, d_kg_rope[:, :ROPE]], 1)[0]
    d_c_q = matmul("q_up_dx", dq_up, w_q_t, "nn")
    grads["b_w_uq"] = _merge_heads_qk_t(matmul("q_up_dw", dq_up, c_q, "tn"))
    d_c_kv = matmul("k_up_dx", dk_nope, w_ukv_t[:kw], "nn")
    d_c_kv = matmul("v_up_dx", dv_b, w_ukv_t[kw:], "nn", res=d_c_kv)
    grads["kv_w_uk"], grads["kv_w_uv"] = matmul("k_up_dw", dk_nope, c_kv, "tn"), matmul("v_up_dw", dv_b, c_kv, "tn")
    d_c_q_raw, grads["b_q_latent_norm"] = row_vjp_call(
        "q_latent_bwd", _f_rms, [Arg(c_q_raw, diff=True, gdt=mxu), Arg(qlat_norm, "par", diff=True)], [Arg(d_c_q)], tr)
    d_c_kv_raw, d_lat = row_vjp_call(
        "kv_latent_bwd", _f_rms, [c_kv_arg, Arg(lat_norm, "par", diff=True)], [Arg(d_c_kv)], tr)
    grads["kv_latent_norm"] = d_lat[0]
    d_hb = matmul("b_in_q_dx", d_c_q_raw, w_cq_t, "nn")
    d_hb = matmul("b_in_gate_dx", d_gate_b, w_gb_t, "nn", res=d_hb, out_dtype=mxu)
    grads["b_w_in"] = jnp.concatenate([matmul("b_in_q_dw", d_c_q_raw, hb, "tn"), matmul("b_in_gate_dw", d_gate_b, hb, "tn")], 0)
    d_c_down = jnp.concatenate([d_c_kv_raw, dk_pe.astype(mxu)], 1)
    d_hk = matmul("kv_down_dx", d_c_down, w_down, "nt", out_dtype=mxu)
    grads["kv_w_down"] = matmul("kv_down_dw", hk, d_c_down, "tn")[:, :KV_RANK + ROPE]
    dh1, d_kv_norm, grads["b_norm"] = row_vjp_call(
        "b_norms_bwd", lambda x_, g1, g2: _f_rms2(x_, g1, g2) + (x_,),
        [Arg(h1, diff=True), Arg(kv_norm, "par", diff=True), Arg(b_norm, "par", diff=True)], [Arg(d_hk), Arg(d_hb), Arg(dh2)], tr)
    grads["kv_norm"] = d_kv_norm[0]

    d_og_a = matmul("a_out_dx", dh1, a_w_out, "nt", out_dtype=mxu)
    grads["a_w_out"] = matmul("a_out_dw", og_a, dh1, "tn")
    d_o_a, d_gate_a, grads["a_o_gain"] = row_vjp_call(
        "a_out_gate_bwd", _f_out_gate, og_args, [Arg(d_og_a, bc=HEAD, ph=True)], tr, nh=N_HEADS)
    dqkv_a, d_ba, d_alog, d_dtb, *received = delta_bwd(qkv_a, z_gba, ba_block, alog, dtb, states, t_invs, d_o_a, lp,
                                                        scatter=deferred.scatter_bufs(grads) if deferred else ())
    grads["a_log"], grads["a_dt_bias"] = d_alog[:, :N_HEADS], d_dtb[:, :N_HEADS]
    dz_qkv, d_conv = conv_bwd(z_qkv, y_conv, a_conv, dqkv_a, lp)
    grads["a_conv"] = d_conv.T
    dz_gba = jnp.concatenate([d_gate_a, d_ba.astype(mxu)], 1)
    grads["a_w_in"] = jnp.concatenate([matmul("a_in_qkv_dw", dz_qkv, hn, "tn"), matmul("a_in_gate_ba_dw", dz_gba, hn, "tn")[:kw + 2 * N_HEADS]], 0)
    ride = deferred.last_scatter_bufs(grads) if deferred else ((), ())
    d_hn = matmul("a_in_qkv_dx", dz_qkv, w_qkv_t, "nn", scatter=ride[0])
    if ride[0]:
        d_hn, *received_half = d_hn
        received = list(received) + received_half
    d_hn = matmul("a_in_gate_ba_dx", dz_gba, w_gba_t, "nn", res=d_hn, out_dtype=mxu, scatter=ride[1])
    if ride[1]:
        d_hn, *received_half = d_hn
        received = list(received) + received_half
    dh0, grads["a_norm"] = row_vjp_call("a_norm_bwd", lambda x_, g_: _f_rms(x_, g_) + (x_,),
                                        [Arg(h0, diff=True), Arg(a_norm, "par", diff=True)], [Arg(d_hn), Arg(dh1)], tr)
    dh0 = dh0.reshape(nb, lp, d)
    grads["meta_tokens"] = meta_grad(dh0).T
    return loss, dh0[:, LEAD:], grads, received


_SHARDED = (
    ("meta_tokens", True, False), ("a_norm", True, False), ("a_w_in", True, True), ("a_conv", True, False), ("a_w_out", False, True),
    ("kv_w_down", False, True), ("kv_w_uk", True, True), ("kv_w_uv", True, True), ("b_w_in", True, True), ("b_w_uq", True, True),
    ("b_w_out", False, True))
_REPLICATED = ("a_log", "a_dt_bias", "a_o_gain", "kv_norm", "kv_latent_norm", "k_gain", "b_norm", "b_q_latent_norm", "b_q_gain")
_ALL_WEIGHTS = ("meta_tokens", "a_norm", "a_w_in", "a_conv", "a_log", "a_dt_bias", "a_o_gain", "a_w_out", "kv_norm", "kv_w_down",
                "kv_latent_norm", "kv_w_uk", "kv_w_uv", "k_gain", "b_norm", "b_w_in", "b_q_latent_norm", "b_w_uq", "b_q_gain", "b_w_out")


def _round_up(n, m):
    return (n + m - 1) // m * m


def _pack_rows(pieces, row_multiple):
    padded = []
    for p in pieces:
        n = p.shape[-1]
        padded.append(jnp.pad(p, [(0, 0)] * (p.ndim - 1) + [(0, _round_up(n, PACK_COLS) - n)]))
    flat = jnp.concatenate(padded, -1)
    rows = _round_up(flat.shape[-1] // PACK_COLS, row_multiple)
    flat = jnp.pad(flat, [(0, 0)] * (flat.ndim - 1) + [(0, rows * PACK_COLS - flat.shape[-1])])
    return flat.reshape(flat.shape[:-1] + (rows, PACK_COLS))


def _unpack_rows(buf, sizes):
    flat = buf.reshape(buf.shape[:-2] + (-1,))
    out, off = [], 0
    for n in sizes:
        out.append(flat[..., off:off + n])
        off += _round_up(n, PACK_COLS)
    return out


def _shard_2d(a):
    return a.reshape(a.shape[-2:]) if a.ndim > 2 else a


def _kl_shard(a, by_cols):
    return _shard_2d(a).T if by_cols else _shard_2d(a)


_GROUPS_FIRST = (("a_w_in",),)
_GROUPS_LATER = (("a_w_out", "b_w_in", "b_w_out"), ("b_w_uq",), ("kv_w_down",), ("kv_w_uk", "kv_w_uv"))
_SMALL_SHARDED = ("meta_tokens", "a_norm", "a_conv")
_BY_COLS = {name: by_cols for name, by_cols, _ in _SHARDED}
ROW_ALIGN = 16


def _stack_rows(pieces):
    padded, starts, row = [], [], 0
    for p in pieces:
        r = p.shape[-2]
        padded.append(jnp.pad(p, [(0, 0)] * (p.ndim - 2) + [(0, _round_up(r, ROW_ALIGN) - r), (0, 0)]))
        starts.append(row)
        row += _round_up(r, ROW_ALIGN)
    return jnp.concatenate(padded, -2), starts


def _stack_group(arrays_by_name, names):
    arrays = [arrays_by_name[n].astype(BF16) for n in names]
    buf, starts = _stack_rows(arrays)
    return buf, [(n, s, a.shape[-2]) for n, s, a in zip(names, starts, arrays, strict=True)]


def _stack_groups(arrays_by_name, groups):
    stacked = [_stack_group(arrays_by_name, names) for names in groups]
    return [b for b, _ in stacked], [entries for _, entries in stacked]


def _full_from_gathered(gathered, layout):
    full = {}
    for got, entries in zip(gathered, layout, strict=True):
        for name, start, rows in entries:
            full[name] = got[:, start:start + rows].reshape(N_DEV * rows, got.shape[-1])
    return full


def gather_small_weights(local):
    small = [_kl_shard(local[n], _BY_COLS[n]) for n in _SMALL_SHARDED]
    (gathered,) = _exchange("all_gather", [_pack_rows([s.reshape(-1) for s in small], 8)], scatter=False)
    full = {}
    for name, part, sh in zip(_SMALL_SHARDED, _unpack_rows(gathered, [s.size for s in small]), small, strict=True):
        full[name] = part.reshape(N_DEV * sh.shape[0], sh.shape[1])
    full["a_norm"] = full["a_norm"].reshape(1, -1)
    return full


class LaterExchanges:
    def __init__(self, local):
        shards = {n: _kl_shard(local[n], _BY_COLS[n]) for names in _GROUPS_FIRST + _GROUPS_LATER for n in names}
        self.first_gather_bufs, self.first_layout = _stack_groups(shards, _GROUPS_FIRST)
        self.gather_bufs, self.layout = _stack_groups(shards, _GROUPS_LATER)

    def finish_first(self, gathered):
        return _full_from_gathered(gathered, self.first_layout)

    def finish(self, gathered):
        return _full_from_gathered(gathered, self.layout)

    def scatter_bufs(self, grads):
        return _stack_groups(_owner_slices(grads, _GROUPS_LATER), _GROUPS_LATER)[0]

    def last_scatter_bufs(self, grads):
        (buf,), self.last_layout = _stack_groups(_owner_slices(grads, _GROUPS_FIRST), _GROUPS_FIRST)
        first = buf.shape[-1] * 5 // 8 // HEAD * HEAD
        return [buf[..., :first]], [buf[..., first:]]


def _owner_slices(grads, groups):
    return {n: grads[n].reshape(N_DEV, -1, grads[n].shape[-1]) for names in groups for n in names}


def reduce_contributions(name, recv):
    _, r, c = recv.shape
    tr = max(d for d in range(8, 513, 8) if r % d == 0 and (d % ROW_ALIGN == 0 or recv.dtype == F32))

    def body(g_ref, o_ref):
        g = g_ref[0].astype(F32)
        for dev in range(1, N_DEV):
            g = g + g_ref[dev].astype(F32)
        o_ref[...] = g

    return pl.pallas_call(
        body, grid=(r // tr,), in_specs=[pl.BlockSpec((N_DEV, tr, c), lambda i: (0, i, 0))], out_specs=pl.BlockSpec((tr, c), lambda i: (i, 0)),
        out_shape=jax.ShapeDtypeStruct((r, c), F32), compiler_params=_cparams(("arbitrary",)), name=name)(recv)


def adamw_all(gs, ws, ms, vs):
    n = len(gs)

    def body(*refs):
        for i in range(n):
            g_ref, w_ref, m_ref, v_ref = (refs[j * n + i] for j in range(4))
            d_ref, mo_ref, vo_ref = (refs[(4 + j) * n + i] for j in range(3))
            g = g_ref[...]
            m_new = ADAM_B1 * m_ref[...] + (1.0 - ADAM_B1) * g
            v_new = ADAM_B2 * v_ref[...] + (1.0 - ADAM_B2) * (g * g)
            m_hat = m_new / (1.0 - ADAM_B1 ** ADAM_STEP)
            v_hat = v_new / (1.0 - ADAM_B2 ** ADAM_STEP)
            d_ref[...] = -ADAM_LR * (m_hat / (jnp.sqrt(v_hat) + ADAM_EPS) + ADAM_WD * w_ref[...])
            mo_ref[...] = m_new
            vo_ref[...] = v_new

    out = [jax.ShapeDtypeStruct(g.shape, F32) for g in gs] * 3
    res = pl.pallas_call(body, out_shape=out, compiler_params=pltpu.CompilerParams(vmem_limit_bytes=VMEM_LIMIT), name="adamw_all")(*gs, *ws, *ms, *vs)
    return res[:n], res[n:2 * n], res[2 * n:]


def kernel(x, meta_tokens, a_norm, a_w_in, a_conv, a_log, a_dt_bias, a_o_gain, a_w_out, kv_norm, kv_w_down, kv_latent_norm, kv_w_uk, kv_w_uv, k_gain, b_norm, b_w_in, b_q_latent_norm, b_w_uq, b_q_gain, b_w_out, loss_target, m_meta_tokens, m_a_norm, m_a_w_in, m_a_conv, m_a_log, m_a_dt_bias, m_a_o_gain, m_a_w_out, m_kv_norm, m_kv_w_down, m_kv_latent_norm, m_kv_w_uk, m_kv_w_uv, m_k_gain, m_b_norm, m_b_w_in, m_b_q_latent_norm, m_b_w_uq, m_b_q_gain, m_b_w_out, v_meta_tokens, v_a_norm, v_a_w_in, v_a_conv, v_a_log, v_a_dt_bias, v_a_o_gain, v_a_w_out, v_kv_norm, v_kv_w_down, v_kv_latent_norm, v_kv_w_uk, v_kv_w_uv, v_k_gain, v_b_norm, v_b_w_in, v_b_q_latent_norm, v_b_w_uq, v_b_q_gain, v_b_w_out):
    given = dict(locals())
    local_w = {n: given[n] for n in _ALL_WEIGHTS}
    full = gather_small_weights(local_w)
    for n in _REPLICATED:
        full[n] = local_w[n]
    later = LaterExchanges(local_w)

    loss_part, grad_x, grads, received_riding = local_step(x, loss_target, full, later)

    exact = [grads[n].reshape(N_DEV, -1) for n in _SMALL_SHARDED]
    exact += [jnp.broadcast_to(grads[n].reshape(1, -1), (N_DEV, grads[n].size)) for n in _REPLICATED]
    exact.append(jnp.broadcast_to(loss_part, (N_DEV, 1)))
    received = list(received_riding) + list(_exchange("all_to_all", [_pack_rows(exact, 8)], scatter=True))
    layout = later.layout + later.last_layout
    summed = [reduce_contributions(f"reduce_{i}", r) for i, r in enumerate(received)]
    n_later = len(later.layout)
    summed = summed[:n_later] + [jnp.concatenate(summed[n_later:n_later + 2], 1)] + summed[n_later + 2:]

    grad_kl = {}
    for got, entries in zip(summed, layout):
        for n, start, rows in entries:
            grad_kl[n] = got[start:start + rows]
    parts = _unpack_rows(summed[-1], [p.shape[1] for p in exact])
    for n, part in zip(_SMALL_SHARDED + _REPLICATED, parts, strict=False):
        grad_kl[n] = part
    loss = parts[-1][0]

    def natural_2d(n, a):
        shape = _shard_2d(local_w[n]).shape if local_w[n].ndim > 1 else (1, local_w[n].size)
        return a.reshape(shape[::-1]).T if _BY_COLS.get(n, False) else a.reshape(shape)

    as_2d = lambda n, a: a.reshape(natural_2d(n, grad_kl[n]).shape)
    gs = [natural_2d(n, grad_kl[n]) for n in _ALL_WEIGHTS]
    deltas, new_m, new_v = adamw_all(gs, [as_2d(n, local_w[n]) for n in _ALL_WEIGHTS], [as_2d(n, given["m_" + n]) for n in _ALL_WEIGHTS],
                                     [as_2d(n, given["v_" + n]) for n in _ALL_WEIGHTS])
    results = [a.reshape(local_w[n].shape) for group in (gs, deltas, new_m, new_v) for n, a in zip(_ALL_WEIGHTS, group, strict=True)]
    return (loss, grad_x, *results)
```

```python
import dataclasses
import functools
import math

import jax
import jax.numpy as jnp
from jax import lax
from jax.experimental import pallas as pl
from jax.experimental.pallas import tpu as pltpu

F32 = jnp.float32
BF16 = jnp.bfloat16
_MXU_DTYPE = jnp.bfloat16

N_DEV = 8
D_MODEL = 1024
N_HEADS = 8
HEAD = 128
CHUNK = 64
N_META = 16
PAD_ROWS = 2 * CHUNK - N_META
LEAD = PAD_ROWS + N_META
ROPE = 64
QK_DIM = HEAD + ROPE
QK_PAD = 2 * HEAD
KV_RANK = 256
Q_RANK = 384
CONV_K = 4
EPS = 1e-6
NEG = -1e30
ROPE_THETA = 10000.0
ADAM_LR, ADAM_B1, ADAM_B2, ADAM_EPS, ADAM_WD, ADAM_STEP = 0.001, 0.9, 0.999, 1e-08, 0.01, 10
PACK_COLS = 512
VMEM_LIMIT = 56 * 1024 * 1024


def _pick(n, options):
    for o in options:
        if n % o == 0:
            return o
    raise ValueError(f"no tile for {n} among {options}")


def _cparams(sem):
    return pltpu.CompilerParams(dimension_semantics=sem, vmem_limit_bytes=VMEM_LIMIT)


def _dims(a, dims):
    if a.ndim == 2:
        return (dims, ((), ()))
    (ca,), (cb,) = dims
    return (((ca + 1,), (cb + 1,)), ((0,), (0,)))


def _dot(a, b, dims):
    return lax.dot_general(a.astype(_MXU_DTYPE), b.astype(_MXU_DTYPE), _dims(a, dims), preferred_element_type=F32)


@jax.custom_vjp
def mm_nn(a, b):
    return _dot(a, b, ((1,), (0,)))


@jax.custom_vjp
def mm_nt(a, b):
    return _dot(a, b, ((1,), (1,)))


@jax.custom_vjp
def mm_tn(a, b):
    return _dot(a, b, ((0,), (0,)))


mm_nn.defvjp(lambda a, b: (mm_nn(a, b), (a, b)), lambda r, g: (mm_nt(g, r[1]), mm_tn(r[0], g)))
mm_nt.defvjp(lambda a, b: (mm_nt(a, b), (a, b)), lambda r, g: (mm_nn(g, r[1]), mm_tn(g, r[0])))
mm_tn.defvjp(lambda a, b: (mm_tn(a, b), (a, b)), lambda r, g: (mm_nt(r[1], g), mm_nn(r[0], g)))


def _split_terms(x, n):
    terms, rest = [], x
    for _ in range(n):
        t = rest.astype(_MXU_DTYPE)
        terms.append(t)
        rest = rest - t.astype(F32)
    return terms


def _dot_01_raw(m, x, dims):
    m = m.astype(_MXU_DTYPE)
    return sum(lax.dot_general(m, t, _dims(m, dims), preferred_element_type=F32) for t in _split_terms(x, 3))


@jax.custom_vjp
def _dot_01(m, x):
    return _dot_01_raw(m, x, ((1,), (0,)))


_dot_01.defvjp(lambda m, x: (_dot_01(m, x), m), lambda m, g: (jnp.zeros_like(m), _dot_01_raw(m, g, ((0,), (0,)))))


def _inv_unit_lower(a):
    n = a.shape[-1]
    eye = (lax.broadcasted_iota(jnp.int32, (n, n), 0) == lax.broadcasted_iota(jnp.int32, (n, n), 1)).astype(F32)
    d = lambda u, w: lax.dot_general(u, w, _dims(u, ((1,), (0,))), preferred_element_type=F32)
    t = eye - a
    p = a.astype(_MXU_DTYPE)
    p = d(p, p)
    squarings = int(math.log2(n)) - 1
    for s in range(squarings):
        ph = p.astype(_MXU_DTYPE)
        t_hi, t_lo = _split_terms(t, 2)
        t = t + (d(t_hi, ph) + d(t_lo, ph))
        if s + 1 < squarings:
            p = d(ph, ph)
    return t


@jax.custom_vjp
def _inv_lookup(a, t):
    return t


def _inv_lookup_bwd(t, g):
    return -mm_tn(t, mm_nt(g, t)), jnp.zeros_like(t)


_inv_lookup.defvjp(lambda a, t: (t, t), _inv_lookup_bwd)


def _sigmoid(x):
    return 1.0 / (1.0 + jnp.exp(-x))


@jax.custom_vjp
def _silu(x):
    return x * _sigmoid(x)


def _silu_fwd(x):
    s = _sigmoid(x)
    return x * s, (x, s)


_silu.defvjp(_silu_fwd, lambda r, g: (g * (r[1] * (1.0 + r[0] * (1.0 - r[1]))),))


def _softplus(x):
    return jnp.where(x > 20.0, x, jnp.log(1.0 + jnp.exp(jnp.minimum(x, 20.0))))


def _rms(x, g, width=None):
    ms = jnp.sum(x * x, -1, keepdims=True) / (x.shape[-1] if width is None else width)
    return x * lax.rsqrt(ms + EPS) * g


MM_VMEM_BUDGET = 40 * 1024 * 1024


def _matmul_rows(name, a, b, mode, out_dtype, res, scatter):
    m, k = a.shape
    n = b.shape[1] if mode == "nn" else b.shape[0]
    dims = {"nn": ((1,), (0,)), "nt": ((1,), (1,))}[mode]
    out_bytes = jnp.dtype(out_dtype).itemsize
    n_in, nx = 2 + (res is not None), len(scatter)

    def vmem(tm):
        blocks = 2 * tm * k * a.dtype.itemsize + 2 * k * n * b.dtype.itemsize + 2 * tm * n * out_bytes + tm * n * 4
        return blocks + (2 * tm * n * res.dtype.itemsize if res is not None else 0)

    tm = next(c for c in (2176, 1088, 512, 256, 128, 64) if m % c == 0 and vmem(c) <= MM_VMEM_BUDGET)
    steps = m // tm

    def body(*refs):
        a_ref, b_ref, o_ref = refs[0], refs[1], refs[n_in + nx]
        i = pl.program_id(0)
        finish = _ride(scatter, True, refs[n_in:n_in + nx], refs[n_in + nx + 1:n_in + 2 * nx + 1], refs[n_in + 2 * nx + 1:], i == 0, i == steps - 1)
        out = _dot(a_ref[...], b_ref[...], dims)
        if res is not None:
            out = out + refs[2][...].astype(F32)
        o_ref[...] = out.astype(o_ref.dtype)
        finish()

    o_spec = pl.BlockSpec((tm, n), lambda i: (i, 0))
    in_specs = [pl.BlockSpec((tm, k), lambda i: (i, 0)), pl.BlockSpec(b.shape, lambda i: (0, 0))] + ([o_spec] if res is not None else [])
    args = (a, b) + ((res,) if res is not None else ())
    out = pl.pallas_call(
        body, grid=(steps,), in_specs=in_specs + [_HBM] * nx, out_specs=[o_spec] + [_HBM] * nx,
        out_shape=[jax.ShapeDtypeStruct((m, n), out_dtype)] + Exchange.out_shape(scatter, True), scratch_shapes=Exchange.scratch(nx) if nx else [],
        compiler_params=_cparams(("arbitrary",) if nx else ("parallel",)), name=name)(*args, *scatter)
    return out if nx else out[0]


def matmul(name, a, b, mode, out_dtype=None, res=None, scatter=()):
    if mode != "tn":
        return _matmul_rows(name, a, b, mode, out_dtype or F32, res, scatter)
    out_dtype = out_dtype or _MXU_DTYPE
    (k, m), (k2, n) = a.shape, b.shape
    assert k == k2 and res is None, (name, a.shape, b.shape, mode)
    tm = _pick(m, (m if m <= 1536 else 1024, 1024, 512, 384, 256, 128))
    tn = _pick(n, (1024, 512, 384, 256, 128))
    tk = _pick(k, (512, 256, 128))
    nk = k // tk
    dims = ((0,), (0,))

    def body(*refs):
        if res is None:
            a_ref, b_ref, o_ref, acc_ref = refs
        else:
            a_ref, b_ref, r_ref, o_ref, acc_ref = refs
        kk = pl.program_id(2)

        @pl.when(kk == 0)
        def _():
            acc_ref[...] = jnp.zeros_like(acc_ref)

        acc_ref[...] += _dot(a_ref[...], b_ref[...], dims)

        @pl.when(kk == nk - 1)
        def _():
            out = acc_ref[...]
            if res is not None:
                out = out + r_ref[...].astype(F32)
            o_ref[...] = out.astype(o_ref.dtype)

    a_spec = pl.BlockSpec((tk, tm), lambda i, j, kk: (kk, i)) if mode == "tn" else pl.BlockSpec((tm, tk), lambda i, j, kk: (i, kk))
    b_spec = pl.BlockSpec((tn, tk), lambda i, j, kk: (j, kk)) if mode == "nt" else pl.BlockSpec((tk, tn), lambda i, j, kk: (kk, j))
    o_spec = pl.BlockSpec((tm, tn), lambda i, j, kk: (i, j))
    in_specs = [a_spec, b_spec] + ([o_spec] if res is not None else [])
    args = (a, b) + ((res,) if res is not None else ())
    return pl.pallas_call(
        body, grid=(m // tm, n // tn, nk), in_specs=in_specs, out_specs=o_spec,
        out_shape=jax.ShapeDtypeStruct((m, n), out_dtype), scratch_shapes=[pltpu.VMEM((tm, tn), F32)],
        compiler_params=_cparams(("parallel", "parallel", "arbitrary")), name=name)(*args)


@dataclasses.dataclass
class Arg:
    arr: jax.Array
    kind: str = "row"
    bc: int = 0
    base: int = 0
    ph: bool = False
    diff: bool = False
    gdt: object = F32


def _arg_spec(a, tr, nh, ntab, base=None):
    bc = a.bc or a.arr.shape[1]
    base = a.base if base is None else base
    width = bc * nh if a.ph else bc
    col = base // nh if a.ph else base
    assert not a.ph or base % nh == 0
    if a.kind == "row":
        return pl.BlockSpec((tr, width), lambda i: (i, col))
    if a.kind == "tab":
        return pl.BlockSpec((tr, width), lambda i: (i % ntab, col))
    return pl.BlockSpec((a.arr.shape[0], width), lambda i: (0, col))


def _head_view(ref, a, h, rs):
    bc = a.bc or a.arr.shape[1]
    rows = slice(None) if a.kind == "par" else rs
    v = ref[rows, h * bc:(h + 1) * bc] if a.ph else ref[rows, :]
    return v.astype(F32) if jnp.issubdtype(v.dtype, jnp.floating) else v


def row_call(name, fn, args, outs, tr, nh=1, ntab=1):
    t = args[0].arr.shape[0]
    n_in = len(args)
    out_args = [Arg(None, "row", bc, 0, ph) for (_, _, bc, ph) in outs]
    assert all(a.ph or nh == 1 for a in out_args)
    rs = slice(None)

    def body(*refs):
        for h in range(nh):
            res = fn(*[_head_view(r, a, h, rs) for r, a in zip(refs[:n_in], args, strict=True)])
            for r, a, v in zip(refs[n_in:], out_args, res, strict=True):
                r[rs, h * a.bc:(h + 1) * a.bc] = v.astype(r.dtype)

    return pl.pallas_call(
        body, grid=(t // tr,), in_specs=[_arg_spec(a, tr, nh, ntab) for a in args], out_specs=[_arg_spec(a, tr, nh, ntab) for a in out_args],
        out_shape=[jax.ShapeDtypeStruct((t, cols), dt) for (cols, dt, _, _) in outs],
        compiler_params=_cparams(("arbitrary",)), name=name)(*[a.arr for a in args])


def row_vjp_call(name, fn, args, cts, tr, nh=1, ntab=1):
    t = args[0].arr.shape[0]
    n_in, n_ct = len(args), len(cts)
    diff_idx = [k for k, a in enumerate(args) if a.diff]
    def body(*refs):
        out_refs = refs[n_in + n_ct:]
        par_sum = {}
        for k, r in zip(diff_idx, out_refs, strict=True):
            if args[k].kind == "par":
                @pl.when(pl.program_id(0) == 0)
                def _(r=r):
                    r[...] = jnp.zeros_like(r)

        for rs in (slice(None),):
            row_sum = {}
            for h in range(nh):
                vals = [_head_view(r, a, h, rs) for r, a in zip(refs[:n_in], args, strict=True)]
                ct_vals = tuple(_head_view(r, a, h, rs) for r, a in zip(refs[n_in:n_in + n_ct], cts, strict=True))

                def f(*dv, vals=vals):
                    full = list(vals)
                    for k, v in zip(diff_idx, dv, strict=True):
                        full[k] = v
                    return tuple(fn(*full))

                _, vjp = jax.vjp(f, *[vals[k] for k in diff_idx])
                for j, (k, r, g) in enumerate(zip(diff_idx, out_refs, vjp(ct_vals), strict=True)):
                    a = args[k]
                    bc = a.bc or a.arr.shape[1]
                    if a.kind == "row" and a.ph:
                        r[rs, h * bc:(h + 1) * bc] = g.astype(r.dtype)
                    elif a.kind == "row":
                        row_sum[j] = g if j not in row_sum else row_sum[j] + g
                    else:
                        key = (j, h if a.ph else 0)
                        par_sum[key] = g if key not in par_sum else par_sum[key] + g
            for j, g in row_sum.items():
                out_refs[j][rs, :] = g.astype(out_refs[j].dtype)
        for (j, h), g in par_sum.items():
            bc = g.shape[1]
            out_refs[j][:, h * bc:(h + 1) * bc] += g

    out_specs, out_shape = [], []
    for k in diff_idx:
        a = args[k]
        bc = a.bc or a.arr.shape[1]
        out_specs.append(_arg_spec(a, tr, nh, ntab, base=0))
        out_shape.append(jax.ShapeDtypeStruct((t if a.kind == "row" else a.arr.shape[0], bc * (nh if a.ph else 1)), a.gdt if a.kind == "row" else F32))
    in_specs = [_arg_spec(a, tr, nh, ntab) for a in list(args) + list(cts)]
    return pl.pallas_call(
        body, grid=(t // tr,), in_specs=in_specs, out_specs=out_specs, out_shape=out_shape,
        compiler_params=_cparams(("arbitrary",)), name=name)(*[a.arr for a in list(args) + list(cts)])


def _conv_taps(x, w):
    rows = lax.broadcasted_iota(jnp.int32, x.shape, 0)
    y = x * w[CONV_K - 1:CONV_K, :]
    for s in range(1, CONV_K):
        y = y + jnp.where(rows >= s, pltpu.roll(x, s, 0), 0.0) * w[CONV_K - 1 - s:CONV_K - s, :]
    return y


CONV_HEADS = 4
CONV_BLOCKS_PER_THIRD = N_HEADS // CONV_HEADS


def _conv_post(y, block):
    a = _silu(y)
    normed = block < 2 * CONV_BLOCKS_PER_THIRD
    scale = jnp.where(block < CONV_BLOCKS_PER_THIRD, HEAD ** -0.5, 1.0)
    return a * jnp.where(normed, lax.rsqrt(jnp.sum(a * a, -1, keepdims=True) + EPS) * scale, 1.0)


def conv_fwd(z, w, lp):
    t, width = z.shape
    cols = CONV_HEADS * HEAD

    def body(z_ref, w_ref, o_ref, y_ref):
        block = pl.program_id(1)
        for h in range(CONV_HEADS):
            cs = slice(h * HEAD, (h + 1) * HEAD)
            y = _conv_taps(z_ref[:, cs].astype(F32), w_ref[:, cs])
            y_ref[:, cs] = y.astype(y_ref.dtype)
            o_ref[:, cs] = _conv_post(y, block)

    blk = pl.BlockSpec((lp, cols), lambda b, j: (b, j))
    out = jax.ShapeDtypeStruct((t, width), F32)
    return pl.pallas_call(
        body, grid=(t // lp, width // cols), in_specs=[blk, pl.BlockSpec((CONV_K, cols), lambda b, j: (0, j))],
        out_specs=[blk, blk], out_shape=[out, jax.ShapeDtypeStruct((t, width), _MXU_DTYPE)],
        compiler_params=_cparams(("arbitrary", "arbitrary")), name="a_conv_fwd")(z, w)


def conv_bwd(z, y, w, dout, lp):
    t, width = z.shape
    cols = CONV_HEADS * HEAD

    def body(z_ref, y_ref, w_ref, g_ref, dz_ref, dw_ref):
        block = pl.program_id(0)

        @pl.when(pl.program_id(1) == 0)
        def _():
            dw_ref[...] = jnp.zeros_like(dw_ref)

        for h in range(CONV_HEADS):
            cs = slice(h * HEAD, (h + 1) * HEAD)
            x, wv = z_ref[:, cs].astype(F32), w_ref[:, cs]
            _, vjp = jax.vjp(lambda y_: _conv_post(y_, block), y_ref[:, cs].astype(F32))
            (dy,) = vjp(g_ref[:, cs])
            rows = lax.broadcasted_iota(jnp.int32, x.shape, 0)
            dx = dy * wv[CONV_K - 1:CONV_K, :]
            dw_ref[CONV_K - 1:CONV_K, cs] += jnp.sum(dy * x, axis=0, keepdims=True)
            for s in range(1, CONV_K):
                dy_up = jnp.where(rows < lp - s, pltpu.roll(dy, lp - s, 0), 0.0)
                dx = dx + dy_up * wv[CONV_K - 1 - s:CONV_K - s, :]
                dw_ref[CONV_K - 1 - s:CONV_K - s, cs] += jnp.sum(dy_up * x, axis=0, keepdims=True)
            dz_ref[:, cs] = dx.astype(dz_ref.dtype)

    blk = pl.BlockSpec((lp, cols), lambda j, b: (b, j))
    w_blk = pl.BlockSpec((CONV_K, cols), lambda j, b: (0, j))
    return pl.pallas_call(
        body, grid=(width // cols, t // lp), in_specs=[blk, blk, w_blk, blk], out_specs=[blk, w_blk],
        out_shape=[jax.ShapeDtypeStruct((t, width), _MXU_DTYPE), jax.ShapeDtypeStruct((CONV_K, width), F32)],
        compiler_params=_cparams(("arbitrary", "arbitrary")), name="a_conv_bwd")(z, y, w, dout)


def _delta_chunk(q, k, v, ba, alog, dtb, state, t_stored):
    n_g, c = q.shape[0], q.shape[1]
    lane = lax.broadcasted_iota(jnp.int32, (1, HEAD), 1)

    def pick(xs, offset):
        cols = [jnp.sum(xs[i // N_HEADS if len(xs) > 1 else 0] * (lane == offset + i % N_HEADS).astype(F32), axis=1, keepdims=True)[None]
                for i in range(n_g)]
        return jnp.concatenate(cols, 0)

    b_raw, a_raw = pick(ba, 0), pick(ba, N_HEADS)
    a_log, dt_bias = pick((alog,), 0), pick((dtb,), 0)
    beta = _sigmoid(b_raw)
    g = -jnp.exp(a_log) * _softplus(a_raw + dt_bias)
    ri = lax.broadcasted_iota(jnp.int32, (c, c), 0)
    ci = lax.broadcasted_iota(jnp.int32, (c, c), 1)
    tril = ci <= ri
    lower = jnp.broadcast_to(tril.astype(F32), (n_g, c, c))
    gc_col = _dot_01(lower, g * jnp.ones((1, 1, HEAD), F32))[:, :, :1]
    gc_row = _dot_01(jnp.ones((n_g, 8, c), F32), g * (ri <= ci).astype(F32)[None])[:, 0:1, :]
    gc_last = jnp.sum(g, axis=1, keepdims=True)
    decay = jnp.exp(jnp.where(tril, gc_col - gc_row, NEG))
    e_gc = jnp.exp(gc_col)
    kb = k * beta
    a_mat = jnp.where(ci < ri, mm_nt(kb, k) * decay, 0.0)
    t_inv = _inv_unit_lower(a_mat) if t_stored is None else _inv_lookup(a_mat, t_stored)
    u_base = mm_nn(t_inv, v * beta)
    w_dec = mm_nn(t_inv, kb * e_gc)
    attn = jnp.where(tril, mm_nt(q, k) * decay, 0.0)
    u = u_base - mm_nn(w_dec, state)
    o = mm_nn(q * e_gc, state) + mm_nn(attn, u)
    new_state = state * jnp.exp(gc_last) + mm_tn(k * jnp.exp(gc_last - gc_col), u)
    return o, new_state, t_inv


DELTA_STEP_FWD = (4, 2)
DELTA_STEP_BWD = (2, 2)


def _heads_of(ref, rs, first_col):
    return jnp.stack([ref[i // N_HEADS, rs, first_col + (i % N_HEADS) * HEAD:first_col + (i % N_HEADS + 1) * HEAD]
                      for i in range(ref.shape[0] * N_HEADS)])


def _qkv_heads(ref, rs, part):
    return _heads_of(ref, rs, part * N_HEADS * HEAD)


def _by_sequence(a, lp):
    return a.reshape(a.shape[0] // lp, lp, a.shape[1])


def _ride(bufs, scatter, refs_in, refs_out, sems, first, last, two_level=False):
    if not bufs:
        return lambda: None
    make = lambda: (TwoLevelGather if two_level else Exchange)(refs_in, refs_out, *sems, scatter)

    @pl.when(first)
    def _():
        make().start()

    def finish():
        @pl.when(last)
        def _():
            make().wait()

    return finish


def delta_fwd(qkv, ba, ba_block, alog, dtb, lp, gather=()):
    t = qkv.shape[0]
    nb, nc = t // lp, lp // CHUNK
    seqs, cps = DELTA_STEP_FWD
    ng, rows = nc // cps, cps * CHUNK
    nx = len(gather)
    nbg = nb // seqs
    assert nc % cps == 0 and nb % seqs == 0

    def body(*refs):
        qkv_ref, ba_ref, al_ref, dt_ref = refs[:4]
        o_ref, s_ref, t_ref = refs[4 + nx:7 + nx]
        state_ref = refs[7 + 2 * nx]
        b, n = pl.program_id(0), pl.program_id(1)
        finish = _ride(gather, False, refs[4:4 + nx], refs[7 + nx:7 + 2 * nx], refs[8 + 2 * nx:], (b == 0) & (n == 0), (b == nbg - 1) & (n == ng - 1))

        @pl.when(n == 0)
        def _():
            state_ref[...] = jnp.zeros_like(state_ref)

        al, dtv = al_ref[...], dt_ref[...]
        for c in range(cps):
            rs = slice(c * CHUNK, (c + 1) * CHUNK)
            state = state_ref[...]
            o, new_state, t_inv = _delta_chunk(_qkv_heads(qkv_ref, rs, 0), _qkv_heads(qkv_ref, rs, 1), _qkv_heads(qkv_ref, rs, 2),
                                               tuple(ba_ref[i, rs, :] for i in range(seqs)), al, dtv, state, None)
            for i in range((seqs * N_HEADS)):
                seq, g = divmod(i, N_HEADS)
                o_ref[seq, rs, g * HEAD:(g + 1) * HEAD] = o[i]
                s_ref[seq, g, c] = state[i]
                t_ref[seq, g, c] = t_inv[i]
            state_ref[...] = new_state
        finish()

    rows_of = lambda width: pl.BlockSpec((seqs, rows, width), lambda b, n: (b, n, 0))
    par_spec = pl.BlockSpec((1, HEAD), lambda b, n: (0, 0))
    out = pl.pallas_call(
        body, grid=(nbg, ng),
        in_specs=[rows_of(3 * N_HEADS * HEAD), pl.BlockSpec((seqs, rows, HEAD), lambda b, n: (b, n, ba_block)), par_spec, par_spec] + [_HBM] * nx,
        out_specs=[rows_of(N_HEADS * HEAD), pl.BlockSpec((seqs, N_HEADS, cps, HEAD, HEAD), lambda b, n: (b, 0, n, 0, 0)),
                   pl.BlockSpec((seqs, N_HEADS, cps, CHUNK, CHUNK), lambda b, n: (b, 0, n, 0, 0))] + [_HBM] * nx,
        out_shape=[jax.ShapeDtypeStruct((nb, lp, N_HEADS * HEAD), F32), jax.ShapeDtypeStruct((nb, N_HEADS, nc, HEAD, HEAD), F32),
                   jax.ShapeDtypeStruct((nb, N_HEADS, nc, CHUNK, CHUNK), F32)] + Exchange.out_shape(gather, False),
        scratch_shapes=[pltpu.VMEM(((seqs * N_HEADS), HEAD, HEAD), F32)] + (Exchange.scratch(nx) if nx else []),
        compiler_params=_cparams(("arbitrary", "arbitrary")), name="delta_fwd")(_by_sequence(qkv, lp), _by_sequence(ba, lp), alog, dtb, *gather)
    return [out[0].reshape(t, N_HEADS * HEAD)] + list(out[1:])


def delta_bwd(qkv, ba, ba_block, alog, dtb, states, t_invs, do, lp, scatter=()):
    t = qkv.shape[0]
    nb, nc = t // lp, lp // CHUNK
    seqs, cps = DELTA_STEP_BWD
    ng, rows = nc // cps, cps * CHUNK
    nx = len(scatter)
    nbg = nb // seqs

    def body(*refs):
        qkv_ref, ba_ref, al_ref, dt_ref, s_ref, t_ref, do_ref = refs[:7]
        dqkv_ref, dba_ref, dal_ref, ddt_ref = refs[7 + nx:11 + nx]
        dstate_ref = refs[11 + 2 * nx]
        b, step = pl.program_id(0), pl.program_id(1)
        finish = _ride(scatter, True, refs[7:7 + nx], refs[11 + nx:11 + 2 * nx], refs[12 + 2 * nx:], (b == 0) & (step == 0),
                       (b == nbg - 1) & (step == ng - 1))

        @pl.when(step == 0)
        def _():
            dstate_ref[...] = jnp.zeros_like(dstate_ref)

        @pl.when((b == 0) & (step == 0))
        def _():
            dal_ref[...] = jnp.zeros_like(dal_ref)
            ddt_ref[...] = jnp.zeros_like(ddt_ref)

        al, dtv = al_ref[...], dt_ref[...]
        d_al = jnp.zeros((1, HEAD), F32)
        d_dt = jnp.zeros((1, HEAD), F32)
        for c in reversed(range(cps)):
            rs = slice(c * CHUNK, (c + 1) * CHUNK)
            t_n = jnp.stack([t_ref[i // N_HEADS, i % N_HEADS, c] for i in range((seqs * N_HEADS))])
            s_n = jnp.stack([s_ref[i // N_HEADS, i % N_HEADS, c] for i in range((seqs * N_HEADS))])

            def f(q_, k_, v_, ba_, al_, dt_, s_, t_n=t_n):
                return _delta_chunk(q_, k_, v_, ba_, al_, dt_, s_, t_n)[:2]

            _, vjp = jax.vjp(f, _qkv_heads(qkv_ref, rs, 0), _qkv_heads(qkv_ref, rs, 1), _qkv_heads(qkv_ref, rs, 2), tuple(ba_ref[i, rs, :] for i in range(seqs)), al, dtv, s_n)
            grads = vjp((_heads_of(do_ref, rs, 0), dstate_ref[...]))
            for part in range(3):
                for i in range((seqs * N_HEADS)):
                    col = (part * N_HEADS + i % N_HEADS) * HEAD
                    dqkv_ref[i // N_HEADS, rs, col:col + HEAD] = grads[part][i]
            for i in range(seqs):
                dba_ref[i, rs, :] = grads[3][i]
            d_al, d_dt = d_al + grads[4], d_dt + grads[5]
            dstate_ref[...] = grads[6]
        dal_ref[...] += d_al
        ddt_ref[...] += d_dt
        finish()

    rows_of = lambda width: pl.BlockSpec((seqs, rows, width), lambda b, n: (b, ng - 1 - n, 0))
    par_spec = pl.BlockSpec((1, HEAD), lambda b, n: (0, 0))
    out = pl.pallas_call(
        body, grid=(nbg, ng),
        in_specs=[rows_of(3 * N_HEADS * HEAD), pl.BlockSpec((seqs, rows, HEAD), lambda b, n: (b, ng - 1 - n, ba_block)), par_spec, par_spec,
                  pl.BlockSpec((seqs, N_HEADS, cps, HEAD, HEAD), lambda b, n: (b, 0, ng - 1 - n, 0, 0)),
                  pl.BlockSpec((seqs, N_HEADS, cps, CHUNK, CHUNK), lambda b, n: (b, 0, ng - 1 - n, 0, 0)), rows_of(N_HEADS * HEAD)] + [_HBM] * nx,
        out_specs=[rows_of(3 * N_HEADS * HEAD), rows_of(HEAD), par_spec, par_spec] + [_HBM] * nx,
        out_shape=[jax.ShapeDtypeStruct((nb, lp, 3 * N_HEADS * HEAD), F32), jax.ShapeDtypeStruct((nb, lp, HEAD), F32),
                   jax.ShapeDtypeStruct((1, HEAD), F32), jax.ShapeDtypeStruct((1, HEAD), F32)] + Exchange.out_shape(scatter, True),
        scratch_shapes=[pltpu.VMEM(((seqs * N_HEADS), HEAD, HEAD), F32)] + (Exchange.scratch(nx) if nx else []),
        compiler_params=_cparams(("arbitrary", "arbitrary")), name="delta_bwd")(
            _by_sequence(qkv, lp), _by_sequence(ba, lp), alog, dtb, states, t_invs, _by_sequence(do, lp), *scatter)
    return [out[0].reshape(t, 3 * N_HEADS * HEAD), out[1].reshape(t, HEAD)] + list(out[2:])


ATT_Q_TILE = 256
ATT_K_TILE = 512
ATT_K_TILE_BWD = 1024
ATT_SCALE = QK_DIM ** -0.5


def _tiles(end, size):
    return [(s, min(s + size, end)) for s in range(0, end, size)]


def _att_visible(q0, q1, k0, k1, keys_first):
    if k1 <= q0 + CHUNK and k0 >= PAD_ROWS:
        return None
    shape = (k1 - k0, q1 - q0) if keys_first else (q1 - q0, k1 - k0)
    qpos = q0 + lax.broadcasted_iota(jnp.int32, shape, 1 if keys_first else 0)
    kpos = k0 + lax.broadcasted_iota(jnp.int32, shape, 0 if keys_first else 1)
    shift = CHUNK.bit_length() - 1
    return (jnp.right_shift(kpos, shift) <= jnp.right_shift(qpos, shift)) & (kpos >= PAD_ROWS)


def _att_seq_specs(lp):
    return pl.BlockSpec((lp, QK_PAD), lambda b, h: (b, h)), pl.BlockSpec((lp, HEAD), lambda b, h: (b, h))


def flash_fwd(q, k, v, lp):
    t = q.shape[0]
    qk_seq, o_seq = _att_seq_specs(lp)

    def body(q_ref, k_ref, v_ref, o_ref, lse_ref):
        q_tiles = _tiles(lp, ATT_Q_TILE)

        def score_steps(q0, q1, out):
            def step(k0, k1):
                s = mm_nt(q_ref[q0:q1, :], k_ref[k0:k1, :])
                vis = _att_visible(q0, q1, k0, k1, False)
                s = s if vis is None else jnp.where(vis, s, NEG)
                out["scores"].append(s)
                row_max = jnp.max(s, -1, keepdims=True)
                out["m"] = row_max if out["m"] is None else jnp.maximum(out["m"], row_max)
            return [functools.partial(step, k0, k1) for k0, k1 in _tiles(q1, ATT_K_TILE)]

        cur = {"scores": [], "m": None}
        for step in score_steps(*q_tiles[0], cur):
            step()
        for i, (q0, q1) in enumerate(q_tiles):
            nxt = {"scores": [], "m": None}
            ahead = score_steps(*q_tiles[i + 1], nxt) if i + 1 < len(q_tiles) else []
            l = jnp.zeros((q1 - q0, 1), F32)
            acc = jnp.zeros((q1 - q0, HEAD), F32)
            for s, (k0, k1) in zip(cur["scores"], _tiles(q1, ATT_K_TILE), strict=True):
                if ahead:
                    ahead.pop(0)()
                p = jnp.exp2(s - cur["m"])
                l = l + jnp.sum(p, -1, keepdims=True)
                acc = acc + mm_nn(p, v_ref[k0:k1, :])
            for step in ahead:
                step()
            o_ref[q0:q1, :] = acc / l
            lse_ref[q0:q1, :] = jnp.broadcast_to(cur["m"] + jnp.log2(l), (q1 - q0, HEAD))
            cur = nxt

    big = jax.ShapeDtypeStruct((t, N_HEADS * HEAD), F32)
    return pl.pallas_call(
        body, grid=(t // lp, N_HEADS), in_specs=[qk_seq, qk_seq, o_seq], out_specs=[o_seq, o_seq], out_shape=[big, big],
        compiler_params=_cparams(("arbitrary", "arbitrary")), name="flash_fwd")(q, k, v)


def flash_bwd(q, k, v, o, lse, do, lp):
    t = q.shape[0]
    qk_seq, o_seq = _att_seq_specs(lp)

    def body(q_ref, k_ref, v_ref, o_ref, lse_ref, do_ref, dq_ref, dk_out_ref, dv_out_ref, dk_ref, dv_ref):
        dk_ref[...] = jnp.zeros_like(dk_ref)
        dv_ref[...] = jnp.zeros_like(dv_ref)
        pairs = [(q0, q1, k0, k1) for q0, q1 in _tiles(lp, ATT_Q_TILE) for k0, k1 in _tiles(q1, ATT_K_TILE_BWD)]
        per_query_tile = {}

        def query_tile(q0, q1):
            if q0 not in per_query_tile:
                qb, dob = q_ref[q0:q1, :], do_ref[q0:q1, :]
                lse_row = jnp.transpose(lse_ref[q0:q1, :])[0:1, :]
                dob_ln2 = dob * math.log(2.0)
                dsum_row = jnp.sum(jnp.transpose(dob_ln2 * o_ref[q0:q1, :]), axis=0, keepdims=True)
                per_query_tile[q0] = (qb, dob, lse_row, dob_ln2, dsum_row)
            return per_query_tile[q0]

        def probabilities(q0, q1, k0, k1):
            qb, _, lse_row, dob_ln2, dsum_row = query_tile(q0, q1)
            s = mm_nt(k_ref[k0:k1, :], qb)
            vis = _att_visible(q0, q1, k0, k1, True)
            s = s if vis is None else jnp.where(vis, s, NEG)
            p = jnp.exp2(s - lse_row)
            return p, p * (mm_nt(v_ref[k0:k1, :], dob_ln2) - dsum_row)

        ahead = probabilities(*pairs[0])
        dq = None
        for i, (q0, q1, k0, k1) in enumerate(pairs):
            p, ds = ahead
            if i + 1 < len(pairs):
                ahead = probabilities(*pairs[i + 1])
            qb, dob = query_tile(q0, q1)[:2]
            dv_ref[k0:k1, :] += mm_nn(p, dob)
            dk_ref[k0:k1, :] += mm_nn(ds, qb)
            dq_part = mm_tn(ds, k_ref[k0:k1, :])
            dq = dq_part if k0 == 0 else dq + dq_part
            if k1 == q1:
                dq_ref[q0:q1, :] = dq.astype(dq_ref.dtype)
        dk_out_ref[...] = dk_ref[...].astype(dk_out_ref.dtype)
        dv_out_ref[...] = dv_ref[...].astype(dv_out_ref.dtype)

    narrow = _MXU_DTYPE
    return pl.pallas_call(
        body, grid=(t // lp, N_HEADS), in_specs=[qk_seq, qk_seq, o_seq, o_seq, o_seq, o_seq], out_specs=[qk_seq, qk_seq, o_seq],
        out_shape=[jax.ShapeDtypeStruct((t, N_HEADS * QK_PAD), narrow), jax.ShapeDtypeStruct((t, N_HEADS * QK_PAD), narrow),
                   jax.ShapeDtypeStruct((t, N_HEADS * HEAD), narrow)],
        scratch_shapes=[pltpu.VMEM((lp, QK_PAD), F32), pltpu.VMEM((lp, HEAD), F32)],
        compiler_params=_cparams(("arbitrary", "arbitrary")), name="flash_bwd")(q, k, v, o, lse, do)


def loss_head(h2, target, lp):
    nb, seq, d = target.shape
    cols = _pick(d, (512, 128))
    ncol = d // cols

    def body(h_ref, t_ref, loss_ref, dh_ref, acc_ref):
        b, j = pl.program_id(0), pl.program_id(1)

        @pl.when((b == 0) & (j == 0))
        def _():
            acc_ref[...] = jnp.zeros_like(acc_ref)

        err = h_ref[LEAD:, :] - t_ref[...]
        dh_ref[:LEAD, :] = jnp.zeros((LEAD, cols), F32)
        dh_ref[LEAD:, :] = err * (1.0 / d)
        acc_ref[...] += jnp.sum(err * err, axis=0, keepdims=True)

        @pl.when((b == nb - 1) & (j == ncol - 1))
        def _():
            loss_ref[...] = jnp.sum(acc_ref[...], axis=1, keepdims=True) * (0.5 / d)

    return pl.pallas_call(
        body, grid=(nb, ncol),
        in_specs=[pl.BlockSpec((None, lp, cols), lambda b, j: (b, 0, j)), pl.BlockSpec((None, seq, cols), lambda b, j: (b, 0, j))],
        out_specs=[pl.BlockSpec((1, 1), lambda b, j: (0, 0)), pl.BlockSpec((None, lp, cols), lambda b, j: (b, 0, j))],
        out_shape=[jax.ShapeDtypeStruct((1, 1), F32), jax.ShapeDtypeStruct((nb, lp, d), F32)],
        scratch_shapes=[pltpu.VMEM((1, cols), F32)], compiler_params=_cparams(("arbitrary", "arbitrary")), name="loss_head")(h2, target)


def gated_out(name, o, gate, gain, w, res):
    t, kw = o.shape
    d = w.shape[1]
    tm = _pick(t, (512, 256, 128))

    def body(*refs):
        o_ref, gate_ref = refs[:2]
        w_ref, r_ref, h_ref, g_ref = refs[-4:]
        if gain is None:
            g_ref[...] = _f_gate(o_ref[...], gate_ref[...])[0].astype(g_ref.dtype)
        else:
            for h in range(N_HEADS):
                cs = slice(h * HEAD, (h + 1) * HEAD)
                g_ref[:, cs] = _f_out_gate(o_ref[:, cs], gate_ref[:, cs], refs[2][...])[0].astype(g_ref.dtype)
        h_ref[...] = r_ref[...] + _dot(g_ref[...], w_ref[...], ((1,), (0,)))

    rows = lambda width: pl.BlockSpec((tm, width), lambda i: (i, 0))
    whole = lambda a: pl.BlockSpec(a.shape, lambda i: (0, 0))
    params = [] if gain is None else [gain]
    return pl.pallas_call(
        body, grid=(t // tm,), in_specs=[rows(kw), rows(kw)] + [whole(p) for p in params] + [whole(w), rows(d)], out_specs=[rows(d), rows(kw)],
        out_shape=[jax.ShapeDtypeStruct((t, d), F32), jax.ShapeDtypeStruct((t, kw), _MXU_DTYPE)],
        compiler_params=_cparams(("parallel",)), name=name)(o, gate, *params, w, res)


def embed_norm(x, meta, gain, lp, gather=()):
    nb, seq, d = x.shape
    nblk, nx = lp // LEAD, len(gather)

    def body(*refs):
        x_ref, meta_ref, g_ref = refs[:3]
        h_ref, hn_ref = refs[3 + nx:5 + nx]
        b, i = pl.program_id(0), pl.program_id(1)
        finish = _ride(gather, False, refs[3:3 + nx], refs[5 + nx:5 + 2 * nx], refs[5 + 2 * nx:], (b == 0) & (i == 0), (b == nb - 1) & (i == nblk - 1),
                       two_level=True)

        @pl.when(i == 0)
        def _():
            h_ref[:PAD_ROWS, :] = jnp.zeros((PAD_ROWS, d), F32)
            h_ref[PAD_ROWS:, :] = meta_ref[...]

        @pl.when(i > 0)
        def _():
            h_ref[...] = x_ref[...]

        hn_ref[...] = _rms(h_ref[...], g_ref[...]).astype(hn_ref.dtype)
        finish()

    rows = pl.BlockSpec((LEAD, d), lambda b, i: (b * nblk + i, 0))
    out = pl.pallas_call(
        body, grid=(nb, nblk),
        in_specs=[pl.BlockSpec((None, LEAD, d), lambda b, i: (b, jnp.maximum(i - 1, 0), 0)), pl.BlockSpec((N_META, d), lambda b, i: (0, 0)),
                  pl.BlockSpec((1, d), lambda b, i: (0, 0))] + [_HBM] * nx,
        out_specs=[rows, rows] + [_HBM] * nx,
        out_shape=[jax.ShapeDtypeStruct((nb * lp, d), F32), jax.ShapeDtypeStruct((nb * lp, d), _MXU_DTYPE)] + Exchange.out_shape(gather, False),
        scratch_shapes=Exchange.scratch(nx) if nx else [],
        compiler_params=_cparams(("arbitrary", "arbitrary")), name="embed_norm")(x, meta, gain, *gather)
    return list(out)


def meta_grad(dh0):
    nb, _, d = dh0.shape

    def body(g_ref, o_ref):
        @pl.when(pl.program_id(0) == 0)
        def _():
            o_ref[...] = jnp.zeros_like(o_ref)

        o_ref[...] += g_ref[PAD_ROWS:LEAD, :]

    return pl.pallas_call(
        body, grid=(nb,), in_specs=[pl.BlockSpec((None, LEAD, d), lambda b: (b, 0, 0))],
        out_specs=pl.BlockSpec((N_META, d), lambda b: (0, 0)), out_shape=jax.ShapeDtypeStruct((N_META, d), F32),
        compiler_params=_cparams(("arbitrary",)), name="meta_grad")(dh0)


_HBM = pl.BlockSpec(memory_space=pltpu.HBM)


def _mesh_pos():
    x, y, c = lax.axis_index("x"), lax.axis_index("y"), lax.axis_index("c")
    return x, y, c


def _peer(x, y, c, k):
    px = 1 - x if k & 4 else x
    py = 1 - y if k & 2 else y
    pc = 1 - c if k & 1 else c
    return (px, py, pc), 4 * px + 2 * py + pc


class Exchange:
    def __init__(self, x_refs, out_refs, send_sems, recv_sems, local_sems, scatter):
        self.x_refs, self.out_refs, self.scatter = x_refs, out_refs, scatter
        self.send_sems, self.recv_sems, self.local_sems = send_sems, recv_sems, local_sems
        self.pos = _mesh_pos()
        x, y, c = self.pos
        self.me = 4 * x + 2 * y + c

    @staticmethod
    def scratch(n):
        return [pltpu.SemaphoreType.DMA((n, N_DEV - 1)), pltpu.SemaphoreType.DMA((n, N_DEV - 1)), pltpu.SemaphoreType.DMA((n,))]

    @staticmethod
    def out_shape(bufs, scatter):
        return [jax.ShapeDtypeStruct(b.shape if scatter else (N_DEV,) + b.shape, b.dtype) for b in bufs]

    def _local(self, i):
        return pltpu.make_async_copy(self.x_refs[i].at[self.me] if self.scatter else self.x_refs[i], self.out_refs[i].at[self.me], self.local_sems.at[i])

    def _copy(self, i, k, landing):
        peer, peer_id = _peer(*self.pos, k)
        src = self.x_refs[i].at[peer_id] if self.scatter else self.x_refs[i]
        return pltpu.make_async_remote_copy(src_ref=src, dst_ref=self.out_refs[i].at[peer_id if landing else self.me],
                                            send_sem=self.send_sems.at[i, k - 1], recv_sem=self.recv_sems.at[i, k - 1],
                                            device_id=peer, device_id_type=pl.DeviceIdType.MESH)

    def start(self):
        for i in range(len(self.x_refs)):
            self._local(i).start()
        for k in range(1, N_DEV):
            for i in range(len(self.x_refs)):
                self._copy(i, k, False).start()

    def wait(self):
        for k in range(1, N_DEV):
            for i in range(len(self.x_refs)):
                self._copy(i, k, True).wait_recv()
        for k in range(1, N_DEV):
            for i in range(len(self.x_refs)):
                self._copy(i, k, False).wait_send()
        for i in range(len(self.x_refs)):
            self._local(i).wait()


class TwoLevelGather(Exchange):
    DIRECT = (1, 4, 2, 6)
    FROM_CHIPS = (4, 2, 6)

    def _forward(self, i, k):
        _, origin = _peer(*self.pos, k)
        sibling, _ = _peer(*self.pos, 1)
        block = self.out_refs[i].at[origin]
        return pltpu.make_async_remote_copy(src_ref=block, dst_ref=block, send_sem=self.send_sems.at[i, (k ^ 1) - 1],
                                            recv_sem=self.recv_sems.at[i, (k ^ 1) - 1], device_id=sibling, device_id_type=pl.DeviceIdType.MESH)

    def start(self):
        assert not self.scatter
        for i in range(len(self.x_refs)):
            self._local(i).start()
        for k in self.DIRECT:
            for i in range(len(self.x_refs)):
                self._copy(i, k, False).start()

    def wait(self):
        n = range(len(self.x_refs))
        for k in self.FROM_CHIPS:
            for i in n:
                self._copy(i, k, True).wait_recv()
                self._forward(i, k).start()
        for k in (1, 5, 3, 7):
            for i in n:
                self._copy(i, k, True).wait_recv()
        for k in self.DIRECT:
            for i in n:
                self._copy(i, k, False).wait_send()
        for k in self.FROM_CHIPS:
            for i in n:
                self._forward(i, k).wait_send()
        for i in n:
            self._local(i).wait()


def _exchange(name, bufs, scatter):
    n = len(bufs)

    def body(*refs):
        ex = Exchange(refs[:n], refs[n:2 * n], *refs[2 * n:], scatter)
        ex.start()
        ex.wait()

    return pl.pallas_call(body, in_specs=[_HBM] * n, out_specs=[_HBM] * n, out_shape=Exchange.out_shape(bufs, scatter),
                          scratch_shapes=Exchange.scratch(n), name=name)(*bufs)


def _f_rms(x, g):
    return (_rms(x, g),)


def _f_rms2(x, g1, g2):
    r = x * lax.rsqrt(jnp.sum(x * x, -1, keepdims=True) / x.shape[-1] + EPS)
    return r * g1, r * g2


@jax.custom_vjp
def _out_gate(o, gate, gain):
    return _rms(o, gain) * _silu(gate)


def _out_gate_bwd(res, g):
    o, gate, gain = res
    r = lax.rsqrt(jnp.sum(o * o, -1, keepdims=True) / o.shape[-1] + EPS)
    n = o * r
    s = _sigmoid(gate)
    g_norm = g * (gate * s)
    d_gate = g * (n * gain) * (s * (1.0 + gate * (1.0 - s)))
    gn = g_norm * gain
    d_o = r * (gn - n * (jnp.sum(gn * n, -1, keepdims=True) / o.shape[-1]))
    return d_o, d_gate, jnp.sum(g_norm * n, 0, keepdims=True)


_out_gate.defvjp(lambda o, gate, gain: (_out_gate(o, gate, gain), (o, gate, gain)), _out_gate_bwd)


def _f_out_gate(o, gate, gain):
    return (_out_gate(o, gate, gain),)


def _f_gate(o, gate):
    return (o * _silu(gate),)


def _swap_rope_halves(x):
    return pltpu.roll(x, ROPE // 2, 1) + pltpu.roll(x, HEAD - ROPE // 2, 1)


def _qk_final_inv_rms(nope, rope_in):
    ms = (jnp.sum(nope * nope, -1, keepdims=True) + jnp.sum(rope_in * rope_in, -1, keepdims=True)) / QK_DIM
    return lax.rsqrt(ms + EPS)


@functools.partial(jax.custom_vjp, nondiff_argnums=(0,))
def _qk_final(scale, nope, rope_in, g_nope, g_rope, cos, sin):
    r = _qk_final_inv_rms(nope, rope_in)
    b = rope_in * (r * g_rope)
    out = jnp.concatenate([nope * (r * g_nope), b * cos + _swap_rope_halves(b) * sin], axis=1)
    return out if scale == 1.0 else out * scale


def _qk_final_fwd(scale, nope, rope_in, g_nope, g_rope, cos, sin):
    return _qk_final(scale, nope, rope_in, g_nope, g_rope, cos, sin), (nope, rope_in, g_nope, g_rope, cos, sin)


def _qk_final_bwd(scale, res, g):
    nope, rope_in, g_nope, g_rope, cos, sin = res
    r = _qk_final_inv_rms(nope, rope_in)
    ga, gb = g[:, :HEAD], g[:, HEAD:]
    if scale != 1.0:
        ga, gb = ga * scale, gb * scale
    db = gb * cos + _swap_rope_halves(gb * sin)
    t_a, t_b = ga * nope, db * rope_in
    d_r = jnp.sum(t_a * g_nope + t_b * g_rope, -1, keepdims=True)
    c = d_r * (r * r * r) * (-1.0 / QK_DIM)
    d_nope = ga * (r * g_nope) + nope * c
    d_rope = db * (r * g_rope) + rope_in * c
    d_g_nope = jnp.sum(t_a * r, 0, keepdims=True)
    d_g_rope = jnp.sum(t_b * r, 0, keepdims=True)
    return d_nope, d_rope, d_g_nope, d_g_rope, jnp.zeros_like(cos), jnp.zeros_like(sin)


_qk_final.defvjp(_qk_final_fwd, _qk_final_bwd)


def _f_qk_final(scale, nope, rope_in, g_nope, g_rope, cos, sin):
    return (_qk_final(scale, nope, rope_in, g_nope, g_rope, cos, sin),)


def _rope_tables(lp):
    half = ROPE // 2
    pos = jnp.maximum(jnp.arange(lp) - PAD_ROWS, 0)
    inv = ROPE_THETA ** (-jnp.arange(half, dtype=F32) / half)
    ang = pos.astype(F32)[:, None] * inv[None, :]
    zeros = jnp.zeros((lp, HEAD - ROPE), F32)
    cos = jnp.concatenate([jnp.cos(ang), jnp.cos(ang), zeros], 1)
    sin = jnp.concatenate([-jnp.sin(ang), jnp.sin(ang), zeros], 1)
    return cos, sin


def _pad_lanes(w, width=HEAD):
    return jnp.pad(w, ((0, 0), (0, width - w.shape[1])))


def _pad_rows(w, rows=HEAD):
    return jnp.pad(w, ((0, rows - w.shape[0]), (0, 0)))


def _split_heads_qk_t(w_t):
    k = w_t.shape[1]
    return jnp.pad(w_t.reshape(N_HEADS, QK_DIM, k), ((0, 0), (0, QK_PAD - QK_DIM), (0, 0))).reshape(N_HEADS * QK_PAD, k)


def _merge_heads_qk_t(g_t):
    k = g_t.shape[1]
    return g_t.reshape(N_HEADS, QK_PAD, k)[:, :QK_DIM].reshape(N_HEADS * QK_DIM, k)


@functools.partial(jax.custom_vjp, nondiff_argnums=(0,))
def _q_final(scale, qh, g_nope, g_rope, cos, sin):
    return _qk_final(scale, qh[:, :HEAD], qh[:, HEAD:], g_nope, g_rope, cos, sin)


def _q_final_bwd(scale, res, g):
    qh, g_nope, g_rope, cos, sin = res
    grads = _qk_final_bwd(scale, (qh[:, :HEAD], qh[:, HEAD:], g_nope, g_rope, cos, sin), g)
    return (jnp.concatenate(grads[:2], axis=1),) + tuple(grads[2:])


_q_final.defvjp(lambda scale, qh, *rest: (_q_final(scale, qh, *rest), (qh,) + rest), _q_final_bwd)


def _f_q_final(scale, qh, g_nope, g_rope, cos, sin):
    return (_q_final(scale, qh, g_nope, g_rope, cos, sin),)


def local_step(x, target, w, deferred=None):
    nb, seq, d = x.shape
    lp = seq + LEAD
    t = nb * lp
    tr = _pick(lp, (544, 128))
    ntab = lp // tr
    mxu = _MXU_DTYPE
    kw = N_HEADS * HEAD

    a_conv = w["a_conv"].T
    alog, dtb, o_gain = _pad_lanes(w["a_log"]), _pad_lanes(w["a_dt_bias"]), w["a_o_gain"]
    a_norm, kv_norm, b_norm = w["a_norm"], w["kv_norm"][None, :], w["b_norm"]
    lat_norm, qlat_norm = w["kv_latent_norm"][None, :], w["b_q_latent_norm"]
    kg_nope, kg_rope = w["k_gain"][None, :HEAD], _pad_lanes(w["k_gain"][None, HEAD:])
    qg_nope, qg_rope = w["b_q_gain"][:, :HEAD], _pad_lanes(w["b_q_gain"][:, HEAD:])
    cos, sin = _rope_tables(lp)

    h0, hn, *gathered = embed_norm(x, w["meta_tokens"].T, a_norm, lp, gather=deferred.first_gather_bufs if deferred else ())
    if deferred:
        w = {**w, **deferred.finish_first(gathered)}
    a_w_in_t = w["a_w_in"].astype(mxu)
    w_qkv_t, w_gba_t = a_w_in_t[:3 * kw], _pad_rows(a_w_in_t[3 * kw:], kw + HEAD)
    z_qkv = matmul("a_in_qkv", hn, w_qkv_t, "nt", out_dtype=mxu)
    z_gba = matmul("a_in_gate_ba", hn, w_gba_t, "nt")
    ba_block = kw // HEAD
    qkv_a, y_conv = conv_fwd(z_qkv, a_conv, lp)
    o_a, states, t_invs, *gathered = delta_fwd(qkv_a, z_gba, ba_block, alog, dtb, lp, gather=deferred.gather_bufs if deferred else ())
    if deferred:
        w = {**w, **deferred.finish(gathered)}
    a_w_out = w["a_w_out"].astype(mxu)
    w_down = _pad_lanes(w["kv_w_down"], KV_RANK + HEAD).astype(mxu)
    w_ukv_t = jnp.concatenate([w["kv_w_uk"], w["kv_w_uv"]], 0).astype(mxu)
    b_w_in_t = w["b_w_in"].astype(mxu)
    w_cq_t, w_gb_t = b_w_in_t[:Q_RANK], b_w_in_t[Q_RANK:]
    w_q_t = _split_heads_qk_t(w["b_w_uq"]).astype(mxu)
    b_w_out = w["b_w_out"].astype(mxu)
    og_args = [Arg(o_a, bc=HEAD, ph=True, diff=True), Arg(z_gba, bc=HEAD, ph=True, diff=True, gdt=mxu), Arg(o_gain, "par", diff=True)]
    h1, og_a = gated_out("a_out", o_a, z_gba, o_gain, a_w_out, h0)

    hk, hb = row_call("b_norms_fwd", _f_rms2, [Arg(h1), Arg(kv_norm, "par"), Arg(b_norm, "par")], [(d, mxu, d, False), (d, mxu, d, False)], tr)
    c_down = matmul("kv_down", hk, w_down, "nn")
    c_kv_arg = Arg(c_down, bc=KV_RANK, diff=True, gdt=mxu)
    k_pe_arg = Arg(c_down, bc=HEAD, base=KV_RANK // HEAD, diff=True)
    c_q_raw = matmul("b_in_q", hb, w_cq_t, "nt")
    gate_b = matmul("b_in_gate", hb, w_gb_t, "nt")
    (c_kv,) = row_call("kv_latent_fwd", _f_rms, [c_kv_arg, Arg(lat_norm, "par")], [(KV_RANK, mxu, KV_RANK, False)], tr)
    (c_q,) = row_call("q_latent_fwd", _f_rms, [Arg(c_q_raw), Arg(qlat_norm, "par")], [(Q_RANK, mxu, Q_RANK, False)], tr)
    k_nope = matmul("k_up", c_kv, w_ukv_t[:kw], "nt")
    v_b = matmul("v_up", c_kv, w_ukv_t[kw:], "nt", out_dtype=mxu)
    q_up = matmul("q_up", c_q, w_q_t, "nt")
    tabs = [Arg(cos, "tab"), Arg(sin, "tab")]
    k_args = [Arg(k_nope, bc=HEAD, ph=True, diff=True, gdt=mxu), k_pe_arg, Arg(kg_nope, "par", diff=True), Arg(kg_rope, "par", diff=True)] + tabs
    q_args = [Arg(q_up, bc=QK_PAD, ph=True, diff=True, gdt=mxu), Arg(qg_nope, "par", diff=True), Arg(qg_rope, "par", diff=True)] + tabs
    f_k_final, f_q_final = functools.partial(_f_qk_final, 1.0), functools.partial(_f_q_final, ATT_SCALE * math.log2(math.e))
    (k_fin,) = row_call("k_final_fwd", f_k_final, k_args, [(N_HEADS * QK_PAD, mxu, QK_PAD, True)], tr, nh=N_HEADS, ntab=ntab)
    (q_fin,) = row_call("q_final_fwd", f_q_final, q_args, [(N_HEADS * QK_PAD, mxu, QK_PAD, True)], tr, nh=N_HEADS, ntab=ntab)
    o_b, lse = flash_fwd(q_fin, k_fin, v_b, lp)
    gb_args = [Arg(o_b, diff=True), Arg(gate_b, diff=True, gdt=mxu)]
    h2, og_b = gated_out("b_out", o_b, gate_b, None, b_w_out, h1)

    loss, dh2 = loss_head(h2.reshape(nb, lp, d), target, lp)
    dh2 = dh2.reshape(t, d)
    grads = {}

    d_og_b = matmul("b_out_dx", dh2, b_w_out, "nt", out_dtype=mxu)
    grads["b_w_out"] = matmul("b_out_dw", og_b, dh2, "tn")
    d_o_b, d_gate_b = row_vjp_call("b_gate_bwd", _f_gate, gb_args, [Arg(d_og_b)], tr)
    dq_fin, dk_fin, dv_b = flash_bwd(q_fin, k_fin, v_b, o_b, lse, d_o_b, lp)
    dq_up, d_qg_nope, d_qg_rope = row_vjp_call(
        "q_final_bwd", f_q_final, q_args, [Arg(dq_fin, bc=QK_PAD, ph=True)], tr, nh=N_HEADS, ntab=ntab)
    dk_nope, dk_pe, d_kg_nope, d_kg_rope = row_vjp_call(
        "k_final_bwd", f_k_final, k_args, [Arg(dk_fin, bc=QK_PAD, ph=True)], tr, nh=N_HEADS, ntab=ntab)
    grads["b_q_gain"] = jnp.concatenate([d_qg_nope, d_qg_rope[:, :ROPE]], 1)
    grads["k_gain"] = jnp.concatenate([d_kg_nope, d_kg_rope[:, :ROPE]], 1)[0]
    d_c_q = matmul("q_up_dx", dq_up, w_q_t, "nn")
    grads["b_w_uq"] = _merge_heads_qk_t(matmul("q_up_dw", dq_up, c_q, "tn"))
    d_c_kv = matmul("k_up_dx", dk_nope, w_ukv_t[:kw], "nn")
    d_c_kv = matmul("v_up_dx", dv_b, w_ukv_t[kw:], "nn", res=d_c_kv)
    grads["kv_w_uk"], grads["kv_w_uv"] = matmul("k_up_dw", dk_nope, c_kv, "tn"), matmul("v_up_dw", dv_b, c_kv, "tn")
    d_c_q_raw, grads["b_q_latent_norm"] = row_vjp_call(
        "q_latent_bwd", _f_rms, [Arg(c_q_raw, diff=True, gdt=mxu), Arg(qlat_norm, "par", diff=True)], [Arg(d_c_q)], tr)
    d_c_kv_raw, d_lat = row_vjp_call(
        "kv_latent_bwd", _f_rms, [c_kv_arg, Arg(lat_norm, "par", diff=True)], [Arg(d_c_kv)], tr)
    grads["kv_latent_norm"] = d_lat[0]
    d_hb = matmul("b_in_q_dx", d_c_q_raw, w_cq_t, "nn")
    d_hb = matmul("b_in_gate_dx", d_gate_b, w_gb_t, "nn", res=d_hb, out_dtype=mxu)
    grads["b_w_in"] = jnp.concatenate([matmul("b_in_q_dw", d_c_q_raw, hb, "tn"), matmul("b_in_gate_dw", d_gate_b, hb, "tn")], 0)
    d_c_down = jnp.concatenate([d_c_kv_raw, dk_pe.astype(mxu)], 1)
    d_hk = matmul("kv_down_dx", d_c_down, w_down, "nt", out_dtype=mxu)
    grads["kv_w_down"] = matmul("kv_down_dw", hk, d_c_down, "tn")[:, :KV_RANK + ROPE]
    dh1, d_kv_norm, grads["b_norm"] = row_vjp_call(
        "b_norms_bwd", lambda x_, g1, g2: _f_rms2(x_, g1, g2) + (x_,),
        [Arg(h1, diff=True), Arg(kv_norm, "par", diff=True), Arg(b_norm, "par", diff=True)], [Arg(d_hk), Arg(d_hb), Arg(dh2)], tr)
    grads["kv_norm"] = d_kv_norm[0]

    d_og_a = matmul("a_out_dx", dh1, a_w_out, "nt", out_dtype=mxu)
    grads["a_w_out"] = matmul("a_out_dw", og_a, dh1, "tn")
    d_o_a, d_gate_a, grads["a_o_gain"] = row_vjp_call(
        "a_out_gate_bwd", _f_out_gate, og_args, [Arg(d_og_a, bc=HEAD, ph=True)], tr, nh=N_HEADS)
    dqkv_a, d_ba, d_alog, d_dtb, *received = delta_bwd(qkv_a, z_gba, ba_block, alog, dtb, states, t_invs, d_o_a, lp,
                                                        scatter=deferred.scatter_bufs(grads) if deferred else ())
    grads["a_log"], grads["a_dt_bias"] = d_alog[:, :N_HEADS], d_dtb[:, :N_HEADS]
    dz_qkv, d_conv = conv_bwd(z_qkv, y_conv, a_conv, dqkv_a, lp)
    grads["a_conv"] = d_conv.T
    dz_gba = jnp.concatenate([d_gate_a, d_ba.astype(mxu)], 1)
    grads["a_w_in"] = jnp.concatenate([matmul("a_in_qkv_dw", dz_qkv, hn, "tn"), matmul("a_in_gate_ba_dw", dz_gba, hn, "tn")[:kw + 2 * N_HEADS]], 0)
    ride = deferred.last_scatter_bufs(grads) if deferred else ((), ())
    d_hn = matmul("a_in_qkv_dx", dz_qkv, w_qkv_t, "nn", scatter=ride[0])
    if ride[0]:
        d_hn, *received_half = d_hn
        received = list(received) + received_half
    d_hn = matmul("a_in_gate_ba_dx", dz_gba, w_gba_t, "nn", res=d_hn, out_dtype=mxu, scatter=ride[1])
    if ride[1]:
        d_hn, *received_half = d_hn
        received = list(received) + received_half
    dh0, grads["a_norm"] = row_vjp_call("a_norm_bwd", lambda x_, g_: _f_rms(x_, g_) + (x_,),
                                        [Arg(h0, diff=True), Arg(a_norm, "par", diff=True)], [Arg(d_hn), Arg(dh1)], tr)
    dh0 = dh0.reshape(nb, lp, d)
    grads["meta_tokens"] = meta_grad(dh0).T
    return loss, dh0[:, LEAD:], grads, received


_SHARDED = (
    ("meta_tokens", True, False), ("a_norm", True, False), ("a_w_in", True, True), ("a_conv", True, False), ("a_w_out", False, True),
    ("kv_w_down", False, True), ("kv_w_uk", True, True), ("kv_w_uv", True, True), ("b_w_in", True, True), ("b_w_uq", True, True),
    ("b_w_out", False, True))
_REPLICATED = ("a_log", "a_dt_bias", "a_o_gain", "kv_norm", "kv_latent_norm", "k_gain", "b_norm", "b_q_latent_norm", "b_q_gain")
_ALL_WEIGHTS = ("meta_tokens", "a_norm", "a_w_in", "a_conv", "a_log", "a_dt_bias", "a_o_gain", "a_w_out", "kv_norm", "kv_w_down",
                "kv_latent_norm", "kv_w_uk", "kv_w_uv", "k_gain", "b_norm", "b_w_in", "b_q_latent_norm", "b_w_uq", "b_q_gain", "b_w_out")


def _round_up(n, m):
    return (n + m - 1) // m * m


def _pack_rows(pieces, row_multiple):
    padded = []
    for p in pieces:
        n = p.shape[-1]
        padded.append(jnp.pad(p, [(0, 0)] * (p.ndim - 1) + [(0, _round_up(n, PACK_COLS) - n)]))
    flat = jnp.concatenate(padded, -1)
    rows = _round_up(flat.shape[-1] // PACK_COLS, row_multiple)
    flat = jnp.pad(flat, [(0, 0)] * (flat.ndim - 1) + [(0, rows * PACK_COLS - flat.shape[-1])])
    return flat.reshape(flat.shape[:-1] + (rows, PACK_COLS))


def _unpack_rows(buf, sizes):
    flat = buf.reshape(buf.shape[:-2] + (-1,))
    out, off = [], 0
    for n in sizes:
        out.append(flat[..., off:off + n])
        off += _round_up(n, PACK_COLS)
    return out


def _shard_2d(a):
    return a.reshape(a.shape[-2:]) if a.ndim > 2 else a


def _kl_shard(a, by_cols):
    return _shard_2d(a).T if by_cols else _shard_2d(a)


_GROUPS_FIRST = (("a_w_in",),)
_GROUPS_LATER = (("a_w_out", "b_w_in", "b_w_out"), ("b_w_uq",), ("kv_w_down",), ("kv_w_uk", "kv_w_uv"))
_SMALL_SHARDED = ("meta_tokens", "a_norm", "a_conv")
_BY_COLS = {name: by_cols for name, by_cols, _ in _SHARDED}
ROW_ALIGN = 16


def _stack_rows(pieces):
    padded, starts, row = [], [], 0
    for p in pieces:
        r = p.shape[-2]
        padded.append(jnp.pad(p, [(0, 0)] * (p.ndim - 2) + [(0, _round_up(r, ROW_ALIGN) - r), (0, 0)]))
        starts.append(row)
        row += _round_up(r, ROW_ALIGN)
    return jnp.concatenate(padded, -2), starts


def _stack_group(arrays_by_name, names):
    arrays = [arrays_by_name[n].astype(BF16) for n in names]
    buf, starts = _stack_rows(arrays)
    return buf, [(n, s, a.shape[-2]) for n, s, a in zip(names, starts, arrays, strict=True)]


def _stack_groups(arrays_by_name, groups):
    stacked = [_stack_group(arrays_by_name, names) for names in groups]
    return [b for b, _ in stacked], [entries for _, entries in stacked]


def _full_from_gathered(gathered, layout):
    full = {}
    for got, entries in zip(gathered, layout, strict=True):
        for name, start, rows in entries:
            full[name] = got[:, start:start + rows].reshape(N_DEV * rows, got.shape[-1])
    return full


def gather_small_weights(local):
    small = [_kl_shard(local[n], _BY_COLS[n]) for n in _SMALL_SHARDED]
    (gathered,) = _exchange("all_gather", [_pack_rows([s.reshape(-1) for s in small], 8)], scatter=False)
    full = {}
    for name, part, sh in zip(_SMALL_SHARDED, _unpack_rows(gathered, [s.size for s in small]), small, strict=True):
        full[name] = part.reshape(N_DEV * sh.shape[0], sh.shape[1])
    full["a_norm"] = full["a_norm"].reshape(1, -1)
    return full


class LaterExchanges:
    def __init__(self, local):
        shards = {n: _kl_shard(local[n], _BY_COLS[n]) for names in _GROUPS_FIRST + _GROUPS_LATER for n in names}
        self.first_gather_bufs, self.first_layout = _stack_groups(shards, _GROUPS_FIRST)
        self.gather_bufs, self.layout = _stack_groups(shards, _GROUPS_LATER)

    def finish_first(self, gathered):
        return _full_from_gathered(gathered, self.first_layout)

    def finish(self, gathered):
        return _full_from_gathered(gathered, self.layout)

    def scatter_bufs(self, grads):
        return _stack_groups(_owner_slices(grads, _GROUPS_LATER), _GROUPS_LATER)[0]

    def last_scatter_bufs(self, grads):
        (buf,), self.last_layout = _stack_groups(_owner_slices(grads, _GROUPS_FIRST), _GROUPS_FIRST)
        first = buf.shape[-1] * 5 // 8 // HEAD * HEAD
        return [buf[..., :first]], [buf[..., first:]]


def _owner_slices(grads, groups):
    return {n: grads[n].reshape(N_DEV, -1, grads[n].shape[-1]) for names in groups for n in names}


def reduce_contributions(name, recv):
    _, r, c = recv.shape
    tr = max(d for d in range(8, 513, 8) if r % d == 0 and (d % ROW_ALIGN == 0 or recv.dtype == F32))

    def body(g_ref, o_ref):
        g = g_ref[0].astype(F32)
        for dev in range(1, N_DEV):
            g = g + g_ref[dev].astype(F32)
        o_ref[...] = g

    return pl.pallas_call(
        body, grid=(r // tr,), in_specs=[pl.BlockSpec((N_DEV, tr, c), lambda i: (0, i, 0))], out_specs=pl.BlockSpec((tr, c), lambda i: (i, 0)),
        out_shape=jax.ShapeDtypeStruct((r, c), F32), compiler_params=_cparams(("arbitrary",)), name=name)(recv)


def adamw_all(gs, ws, ms, vs):
    n = len(gs)

    def body(*refs):
        for i in range(n):
            g_ref, w_ref, m_ref, v_ref = (refs[j * n + i] for j in range(4))
            d_ref, mo_ref, vo_ref = (refs[(4 + j) * n + i] for j in range(3))
            g = g_ref[...]
            m_new = ADAM_B1 * m_ref[...] + (1.0 - ADAM_B1) * g
            v_new = ADAM_B2 * v_ref[...] + (1.0 - ADAM_B2) * (g * g)
            m_hat = m_new / (1.0 - ADAM_B1 ** ADAM_STEP)
            v_hat = v_new / (1.0 - ADAM_B2 ** ADAM_STEP)
            d_ref[...] = -ADAM_LR * (m_hat / (jnp.sqrt(v_hat) + ADAM_EPS) + ADAM_WD * w_ref[...])
            mo_ref[...] = m_new
            vo_ref[...] = v_new

    out = [jax.ShapeDtypeStruct(g.shape, F32) for g in gs] * 3
    res = pl.pallas_call(body, out_shape=out, compiler_params=pltpu.CompilerParams(vmem_limit_bytes=VMEM_LIMIT), name="adamw_all")(*gs, *ws, *ms, *vs)
    return res[:n], res[n:2 * n], res[2 * n:]


def kernel(x, meta_tokens, a_norm, a_w_in, a_conv, a_log, a_dt_bias, a_o_gain, a_w_out, kv_norm, kv_w_down, kv_latent_norm, kv_w_uk, kv_w_uv, k_gain, b_norm, b_w_in, b_q_latent_norm, b_w_uq, b_q_gain, b_w_out, loss_target, m_meta_tokens, m_a_norm, m_a_w_in, m_a_conv, m_a_log, m_a_dt_bias, m_a_o_gain, m_a_w_out, m_kv_norm, m_kv_w_down, m_kv_latent_norm, m_kv_w_uk, m_kv_w_uv, m_k_gain, m_b_norm, m_b_w_in, m_b_q_latent_norm, m_b_w_uq, m_b_q_gain, m_b_w_out, v_meta_tokens, v_a_norm, v_a_w_in, v_a_conv, v_a_log, v_a_dt_bias, v_a_o_gain, v_a_w_out, v_kv_norm, v_kv_w_down, v_kv_latent_norm, v_kv_w_uk, v_kv_w_uv, v_k_gain, v_b_norm, v_b_w_in, v_b_q_latent_norm, v_b_w_uq, v_b_q_gain, v_b_w_out):
    given = dict(locals())
    local_w = {n: given[n] for n in _ALL_WEIGHTS}
    full = gather_small_weights(local_w)
    for n in _REPLICATED:
        full[n] = local_w[n]
    later = LaterExchanges(local_w)

    loss_part, grad_x, grads, received_riding = local_step(x, loss_target, full, later)

    exact = [grads[n].reshape(N_DEV, -1) for n in _SMALL_SHARDED]
    exact += [jnp.broadcast_to(grads[n].reshape(1, -1), (N_DEV, grads[n].size)) for n in _REPLICATED]
    exact.append(jnp.broadcast_to(loss_part, (N_DEV, 1)))
    received = list(received_riding) + list(_exchange("all_to_all", [_pack_rows(exact, 8)], scatter=True))
    layout = later.layout + later.last_layout
    summed = [reduce_contributions(f"reduce_{i}", r) for i, r in enumerate(received)]
    n_later = len(later.layout)
    summed = summed[:n_later] + [jnp.concatenate(summed[n_later:n_later + 2], 1)] + summed[n_later + 2:]

    grad_kl = {}
    for got, entries in zip(summed, layout):
        for n, start, rows in entries:
            grad_kl[n] = got[start:start + rows]
    parts = _unpack_rows(summed[-1], [p.shape[1] for p in exact])
    for n, part in zip(_SMALL_SHARDED + _REPLICATED, parts, strict=False):
        grad_kl[n] = part
    loss = parts[-1][0]

    def natural_2d(n, a):
        shape = _shard_2d(local_w[n]).shape if local_w[n].ndim > 1 else (1, local_w[n].size)
        return a.reshape(shape[::-1]).T if _BY_COLS.get(n, False) else a.reshape(shape)

    as_2d = lambda n, a: a.reshape(natural_2d(n, grad_kl[n]).shape)
    gs = [natural_2d(n, grad_kl[n]) for n in _ALL_WEIGHTS]
    deltas, new_m, new_v = adamw_all(gs, [as_2d(n, local_w[n]) for n in _ALL_WEIGHTS], [as_2d(n, given["m_" + n]) for n in _ALL_WEIGHTS],
                                     [as_2d(n, given["v_" + n]) for n in _ALL_WEIGHTS])
    results = [a.reshape(local_w[n].shape) for group in (gs, deltas, new_m, new_v) for n, a in zip(_ALL_WEIGHTS, group, strict=True)]
    return (loss, grad_x, *results)
```
